```python
import jax, jax.numpy as jnp
from jax import lax
import numpy as np

D_MODEL = 1024
BATCH = 8
SEQ = 8192
DEPTH = 1

CHUNK = 64
N_MEM = 256
ATT_HEADS = 8
HEAD_DIM = 64
D_ATT = ATT_HEADS * HEAD_DIM
D_LRU = D_MODEL - D_ATT
LRU_BLOCKS = 8
LRU_BLOCK = D_LRU // LRU_BLOCKS
CONV_W = 4
LRU_C = 8.0
LEFT_CHUNKS = 8
BAND = (LEFT_CHUNKS + 1) * CHUNK
MAX_REL = 128
X_HEADS = 4
X_HEAD_DIM = D_MODEL // X_HEADS
D_FF = -(-8 * D_MODEL // (3 * 256)) * 256
D_IN = 3 * D_ATT + 2 * D_LRU
EPS = 1e-6

kernel_name = "hymba_chunk_attn_rglru_xmem_swiglu"


def rmsnorm(x, g):
    x32 = x.astype(jnp.float32)
    y = x32 * lax.rsqrt(jnp.mean(x32 * x32, axis=-1, keepdims=True) + EPS)
    return (y * g.astype(jnp.float32)).astype(x.dtype)


def chunk_attention(q, k, v, rel_bias):
    B, S, H, Dh = q.shape
    nc = S // CHUNK
    pad = LEFT_CHUNKS * CHUNK
    kp = jnp.pad(k, ((0, 0), (pad, 0), (0, 0), (0, 0)))
    vp = jnp.pad(v, ((0, 0), (pad, 0), (0, 0), (0, 0)))
    rel = (LEFT_CHUNKS * CHUNK + np.arange(CHUNK)[:, None]) - np.arange(BAND)[None, :]
    idx = np.clip(rel, -MAX_REL, MAX_REL) + MAX_REL
    bias = rel_bias[:, idx].astype(jnp.float32)
    qc = q.reshape(B, nc, CHUNK, H, Dh).transpose(1, 0, 2, 3, 4)
    scale = HEAD_DIM ** -0.5
    key_off = jnp.arange(BAND, dtype=jnp.int32)

    def one_chunk(args):
        c, qb = args
        kb = lax.dynamic_slice_in_dim(kp, c * CHUNK, BAND, axis=1)
        vb = lax.dynamic_slice_in_dim(vp, c * CHUNK, BAND, axis=1)
        s = jnp.einsum('bqhd,bkhd->bhqk', qb, kb).astype(jnp.float32) * scale + bias
        valid = (c - LEFT_CHUNKS) * CHUNK + key_off >= 0
        s = jnp.where(valid, s, -1e30)
        p = jax.nn.softmax(s, axis=-1).astype(vb.dtype)
        return jnp.einsum('bhqk,bkhd->bqhd', p, vb)

    o = lax.map(one_chunk, (jnp.arange(nc, dtype=jnp.int32), qc))
    return o.transpose(1, 0, 2, 3, 4).reshape(B, S, H * Dh)


def causal_conv(u, w, b):
    S = u.shape[1]
    up = jnp.pad(u, ((0, 0), (CONV_W - 1, 0), (0, 0)))
    out = up[:, 0:S] * w[0]
    for j in range(1, CONV_W):
        out = out + up[:, j:j + S] * w[j]
    return out + b


def _lin_combine(left, right):
    a_l, b_l = left
    a_r, b_r = right
    return a_r * a_l, a_r * b_l + b_r


def rg_lru(u, w_rg, b_rg, w_ig, b_ig, L):
    B, S, _ = u.shape
    ub = u.reshape(B, S, LRU_BLOCKS, LRU_BLOCK)
    r = jax.nn.sigmoid(jnp.einsum('bsnc,ncd->bsnd', ub, w_rg).reshape(B, S, D_LRU) + b_rg)
    i = jax.nn.sigmoid(jnp.einsum('bsnc,ncd->bsnd', ub, w_ig).reshape(B, S, D_LRU) + b_ig)
    log_a = -LRU_C * r.astype(jnp.float32) * jax.nn.softplus(-L.astype(jnp.float32))
    a = jnp.exp(log_a)
    mult = jnp.sqrt(jnp.maximum(-jnp.expm1(2.0 * log_a), 0.0))
    b = mult * (i * u).astype(jnp.float32)
    _, h = lax.associative_scan(_lin_combine, (a, b), axis=1)
    return h.astype(u.dtype)


def _fwd_setup_inputs(seed: int = 0) -> dict:
    key = jax.random.key(seed)
    ks = jax.random.split(key, 32)
    f32 = jnp.float32

    def w(k, shape, fan_in):
        return jax.random.normal(k, shape, f32) * (fan_in ** -0.5)

    def gain(k, shape):
        return 1.0 + 0.05 * jax.random.normal(k, shape, f32)

    def small(k, shape, s=0.01):
        return s * jax.random.normal(k, shape, f32)

    a0 = jax.random.uniform(ks[11], (DEPTH, D_LRU), f32, 0.9, 0.999) ** (1.0 / LRU_C)
    lru_L = jnp.log(a0) - jnp.log1p(-a0)
    return {
        "x": jax.random.normal(ks[0], (BATCH, SEQ, D_MODEL), f32),
        "mem": jax.random.normal(ks[1], (BATCH, N_MEM, D_MODEL), f32),
        "g_mix": gain(ks[2], (DEPTH, D_MODEL)),
        "w_in": w(ks[3], (DEPTH, D_MODEL, D_IN), D_MODEL),
        "rel_bias": 0.1 * jax.random.normal(ks[4], (DEPTH, ATT_HEADS, 2 * MAX_REL + 1), f32),
        "conv_w": w(ks[5], (DEPTH, CONV_W, D_LRU), CONV_W),
        "conv_b": small(ks[6], (DEPTH, D_LRU)),
        "w_rg": w(ks[7], (DEPTH, LRU_BLOCKS, LRU_BLOCK, LRU_BLOCK), LRU_BLOCK),
        "b_rg": small(ks[8], (DEPTH, D_LRU)),
        "w_ig": w(ks[9], (DEPTH, LRU_BLOCKS, LRU_BLOCK, LRU_BLOCK), LRU_BLOCK),
        "b_ig": small(ks[10], (DEPTH, D_LRU)),
        "lru_L": lru_L,
        "g_out_attn": gain(ks[12], (DEPTH, D_ATT)),
        "g_out_lru": gain(ks[13], (DEPTH, D_LRU)),
        "w_out": w(ks[14], (DEPTH, D_ATT + D_LRU, D_MODEL), D_ATT + D_LRU),
        "g_cross": gain(ks[15], (DEPTH, D_MODEL)),
        "g_mem": gain(ks[16], (DEPTH, D_MODEL)),
        "wq_c": w(ks[17], (DEPTH, D_MODEL, D_MODEL), D_MODEL),
        "wk_c": w(ks[18], (DEPTH, D_MODEL, D_MODEL), D_MODEL),
        "wv_c": w(ks[19], (DEPTH, D_MODEL, D_MODEL), D_MODEL),
        "wo_c": w(ks[20], (DEPTH, D_MODEL, D_MODEL), D_MODEL),
        "g_ffn": gain(ks[21], (DEPTH, D_MODEL)),
        "w_gate": w(ks[22], (DEPTH, D_MODEL, D_FF), D_MODEL),
        "w_up": w(ks[23], (DEPTH, D_MODEL, D_FF), D_MODEL),
        "w_down": w(ks[24], (DEPTH, D_FF, D_MODEL), D_FF),
        "g_final": gain(ks[25], (D_MODEL,)),
    }


def _fwd_reference(x, mem, g_mix, w_in, rel_bias, conv_w, conv_b, w_rg, b_rg, w_ig, b_ig, lru_L,
              g_out_attn, g_out_lru, w_out, g_cross, g_mem, wq_c, wk_c, wv_c, wo_c,
              g_ffn, w_gate, w_up, w_down, g_final):
    B, S, _ = x.shape
    M = mem.shape[1]
    splits = [D_ATT, 2 * D_ATT, 3 * D_ATT, 3 * D_ATT + D_LRU]
    for l in range(DEPTH):
        h = rmsnorm(x, g_mix[l])
        proj = h @ w_in[l]
        q, k, v, xu, gu = jnp.split(proj, splits, axis=-1)
        att = chunk_attention(q.reshape(B, S, ATT_HEADS, HEAD_DIM),
                              k.reshape(B, S, ATT_HEADS, HEAD_DIM),
                              v.reshape(B, S, ATT_HEADS, HEAD_DIM), rel_bias[l])
        xu = causal_conv(xu, conv_w[l], conv_b[l])
        rec = rg_lru(xu, w_rg[l], b_rg[l], w_ig[l], b_ig[l], lru_L[l]) * jax.nn.gelu(gu)
        merged = jnp.concatenate([rmsnorm(att, g_out_attn[l]), rmsnorm(rec, g_out_lru[l])], axis=-1)
        x = x + merged @ w_out[l]

        hc = rmsnorm(x, g_cross[l])
        mn = rmsnorm(mem, g_mem[l])
        qx = (hc @ wq_c[l]).reshape(B, S, X_HEADS, X_HEAD_DIM)
        kx = (mn @ wk_c[l]).reshape(B, M, X_HEADS, X_HEAD_DIM)
        vx = (mn @ wv_c[l]).reshape(B, M, X_HEADS, X_HEAD_DIM)
        s = jnp.einsum('bshd,bmhd->bhsm', qx, kx).astype(jnp.float32) * (X_HEAD_DIM ** -0.5)
        p = jax.nn.softmax(s, axis=-1).astype(vx.dtype)
        ox = jnp.einsum('bhsm,bmhd->bshd', p, vx).reshape(B, S, D_MODEL)
        x = x + ox @ wo_c[l]

        hf = rmsnorm(x, g_ffn[l])
        x = x + (jax.nn.silu(hf @ w_gate[l]) * (hf @ w_up[l])) @ w_down[l]
    return rmsnorm(x, g_final)


import jax as _jax
import jax.numpy as _jnp

TWIN_FORMAT = 'train_step'
FWD_PARAMS = ['x', 'mem', 'g_mix', 'w_in', 'rel_bias', 'conv_w', 'conv_b', 'w_rg', 'b_rg', 'w_ig', 'b_ig', 'lru_L', 'g_out_attn', 'g_out_lru', 'w_out', 'g_cross', 'g_mem', 'wq_c', 'wk_c', 'wv_c', 'wo_c', 'g_ffn', 'w_gate', 'w_up', 'w_down', 'g_final']
TWIN_WEIGHTS = ['g_mix', 'w_in', 'rel_bias', 'conv_w', 'conv_b', 'w_rg', 'b_rg', 'w_ig', 'b_ig', 'lru_L', 'g_out_attn', 'g_out_lru', 'w_out', 'g_cross', 'g_mem', 'wq_c', 'wk_c', 'wv_c', 'wo_c', 'g_ffn', 'w_gate', 'w_up', 'w_down', 'g_final']
TWIN_DIFF_INPUT = 'x'
TWIN_INPUTS = ['x', 'mem', 'g_mix', 'w_in', 'rel_bias', 'conv_w', 'conv_b', 'w_rg', 'b_rg', 'w_ig', 'b_ig', 'lru_L', 'g_out_attn', 'g_out_lru', 'w_out', 'g_cross', 'g_mem', 'wq_c', 'wk_c', 'wv_c', 'wo_c', 'g_ffn', 'w_gate', 'w_up', 'w_down', 'g_final', 'loss_target', 'm_g_mix', 'm_w_in', 'm_rel_bias', 'm_conv_w', 'm_conv_b', 'm_w_rg', 'm_b_rg', 'm_w_ig', 'm_b_ig', 'm_lru_L', 'm_g_out_attn', 'm_g_out_lru', 'm_w_out', 'm_g_cross', 'm_g_mem', 'm_wq_c', 'm_wk_c', 'm_wv_c', 'm_wo_c', 'm_g_ffn', 'm_w_gate', 'm_w_up', 'm_w_down', 'm_g_final', 'v_g_mix', 'v_w_in', 'v_rel_bias', 'v_conv_w', 'v_conv_b', 'v_w_rg', 'v_b_rg', 'v_w_ig', 'v_b_ig', 'v_lru_L', 'v_g_out_attn', 'v_g_out_lru', 'v_w_out', 'v_g_cross', 'v_g_mem', 'v_wq_c', 'v_wk_c', 'v_wv_c', 'v_wo_c', 'v_g_ffn', 'v_w_gate', 'v_w_up', 'v_w_down', 'v_g_final']
TWIN_OUTPUTS = ['loss', 'grad_x', 'grad_g_mix', 'grad_w_in', 'grad_rel_bias', 'grad_conv_w', 'grad_conv_b', 'grad_w_rg', 'grad_b_rg', 'grad_w_ig', 'grad_b_ig', 'grad_lru_L', 'grad_g_out_attn', 'grad_g_out_lru', 'grad_w_out', 'grad_g_cross', 'grad_g_mem', 'grad_wq_c', 'grad_wk_c', 'grad_wv_c', 'grad_wo_c', 'grad_g_ffn', 'grad_w_gate', 'grad_w_up', 'grad_w_down', 'grad_g_final', 'delta_g_mix', 'delta_w_in', 'delta_rel_bias', 'delta_conv_w', 'delta_conv_b', 'delta_w_rg', 'delta_b_rg', 'delta_w_ig', 'delta_b_ig', 'delta_lru_L', 'delta_g_out_attn', 'delta_g_out_lru', 'delta_w_out', 'delta_g_cross', 'delta_g_mem', 'delta_wq_c', 'delta_wk_c', 'delta_wv_c', 'delta_wo_c', 'delta_g_ffn', 'delta_w_gate', 'delta_w_up', 'delta_w_down', 'delta_g_final', 'new_m_g_mix', 'new_m_w_in', 'new_m_rel_bias', 'new_m_conv_w', 'new_m_conv_b', 'new_m_w_rg', 'new_m_b_rg', 'new_m_w_ig', 'new_m_b_ig', 'new_m_lru_L', 'new_m_g_out_attn', 'new_m_g_out_lru', 'new_m_w_out', 'new_m_g_cross', 'new_m_g_mem', 'new_m_wq_c', 'new_m_wk_c', 'new_m_wv_c', 'new_m_wo_c', 'new_m_g_ffn', 'new_m_w_gate', 'new_m_w_up', 'new_m_w_down', 'new_m_g_final', 'new_v_g_mix', 'new_v_w_in', 'new_v_rel_bias', 'new_v_conv_w', 'new_v_conv_b', 'new_v_w_rg', 'new_v_b_rg', 'new_v_w_ig', 'new_v_b_ig', 'new_v_lru_L', 'new_v_g_out_attn', 'new_v_g_out_lru', 'new_v_w_out', 'new_v_g_cross', 'new_v_g_mem', 'new_v_wq_c', 'new_v_wk_c', 'new_v_wv_c', 'new_v_wo_c', 'new_v_g_ffn', 'new_v_w_gate', 'new_v_w_up', 'new_v_w_down', 'new_v_g_final']
TWIN_LEAF_KINDS = {'loss': 'loss', 'grad_x': 'grad_x', 'grad_g_mix': 'grad_w', 'grad_w_in': 'grad_w', 'grad_rel_bias': 'grad_w', 'grad_conv_w': 'grad_w', 'grad_conv_b': 'grad_w', 'grad_w_rg': 'grad_w', 'grad_b_rg': 'grad_w', 'grad_w_ig': 'grad_w', 'grad_b_ig': 'grad_w', 'grad_lru_L': 'grad_w', 'grad_g_out_attn': 'grad_w', 'grad_g_out_lru': 'grad_w', 'grad_w_out': 'grad_w', 'grad_g_cross': 'grad_w', 'grad_g_mem': 'grad_w', 'grad_wq_c': 'grad_w', 'grad_wk_c': 'grad_w', 'grad_wv_c': 'grad_w', 'grad_wo_c': 'grad_w', 'grad_g_ffn': 'grad_w', 'grad_w_gate': 'grad_w', 'grad_w_up': 'grad_w', 'grad_w_down': 'grad_w', 'grad_g_final': 'grad_w', 'delta_g_mix': 'delta_w', 'delta_w_in': 'delta_w', 'delta_rel_bias': 'delta_w', 'delta_conv_w': 'delta_w', 'delta_conv_b': 'delta_w', 'delta_w_rg': 'delta_w', 'delta_b_rg': 'delta_w', 'delta_w_ig': 'delta_w', 'delta_b_ig': 'delta_w', 'delta_lru_L': 'delta_w', 'delta_g_out_attn': 'delta_w', 'delta_g_out_lru': 'delta_w', 'delta_w_out': 'delta_w', 'delta_g_cross': 'delta_w', 'delta_g_mem': 'delta_w', 'delta_wq_c': 'delta_w', 'delta_wk_c': 'delta_w', 'delta_wv_c': 'delta_w', 'delta_wo_c': 'delta_w', 'delta_g_ffn': 'delta_w', 'delta_w_gate': 'delta_w', 'delta_w_up': 'delta_w', 'delta_w_down': 'delta_w', 'delta_g_final': 'delta_w', 'new_m_g_mix': 'new_m', 'new_m_w_in': 'new_m', 'new_m_rel_bias': 'new_m', 'new_m_conv_w': 'new_m', 'new_m_conv_b': 'new_m', 'new_m_w_rg': 'new_m', 'new_m_b_rg': 'new_m', 'new_m_w_ig': 'new_m', 'new_m_b_ig': 'new_m', 'new_m_lru_L': 'new_m', 'new_m_g_out_attn': 'new_m', 'new_m_g_out_lru': 'new_m', 'new_m_w_out': 'new_m', 'new_m_g_cross': 'new_m', 'new_m_g_mem': 'new_m', 'new_m_wq_c': 'new_m', 'new_m_wk_c': 'new_m', 'new_m_wv_c': 'new_m', 'new_m_wo_c': 'new_m', 'new_m_g_ffn': 'new_m', 'new_m_w_gate': 'new_m', 'new_m_w_up': 'new_m', 'new_m_w_down': 'new_m', 'new_m_g_final': 'new_m', 'new_v_g_mix': 'new_v', 'new_v_w_in': 'new_v', 'new_v_rel_bias': 'new_v', 'new_v_conv_w': 'new_v', 'new_v_conv_b': 'new_v', 'new_v_w_rg': 'new_v', 'new_v_b_rg': 'new_v', 'new_v_w_ig': 'new_v', 'new_v_b_ig': 'new_v', 'new_v_lru_L': 'new_v', 'new_v_g_out_attn': 'new_v', 'new_v_g_out_lru': 'new_v', 'new_v_w_out': 'new_v', 'new_v_g_cross': 'new_v', 'new_v_g_mem': 'new_v', 'new_v_wq_c': 'new_v', 'new_v_wk_c': 'new_v', 'new_v_wv_c': 'new_v', 'new_v_wo_c': 'new_v', 'new_v_g_ffn': 'new_v', 'new_v_w_gate': 'new_v', 'new_v_w_up': 'new_v', 'new_v_w_down': 'new_v', 'new_v_g_final': 'new_v'}


def _forward(args):
    return _fwd_reference(*[args[k] for k in FWD_PARAMS])


def _output_shape():
    def fwd():
        inp = _fwd_setup_inputs(0)
        return _fwd_reference(*[inp[k] for k in FWD_PARAMS])
    out = _jax.eval_shape(fwd)
    return out.shape, out.dtype

N_MICROBATCH = 1
ADAM_LR = 0.001
ADAM_B1 = 0.9
ADAM_B2 = 0.999
ADAM_EPS = 1e-08
ADAM_WD = 0.01
ADAM_STEP = 10
PER_EXAMPLE_BATCH_AXIS = {'x': 0, 'mem': 0, 'loss_target': 0}
SHARED_INPUTS = []
_WEIGHT_DTYPES = {'g_mix': _jnp.float32, 'w_in': _jnp.float32, 'rel_bias': _jnp.float32, 'conv_w': _jnp.float32, 'conv_b': _jnp.float32, 'w_rg': _jnp.float32, 'b_rg': _jnp.float32, 'w_ig': _jnp.float32, 'b_ig': _jnp.float32, 'lru_L': _jnp.float32, 'g_out_attn': _jnp.float32, 'g_out_lru': _jnp.float32, 'w_out': _jnp.float32, 'g_cross': _jnp.float32, 'g_mem': _jnp.float32, 'wq_c': _jnp.float32, 'wk_c': _jnp.float32, 'wv_c': _jnp.float32, 'wo_c': _jnp.float32, 'g_ffn': _jnp.float32, 'w_gate': _jnp.float32, 'w_up': _jnp.float32, 'w_down': _jnp.float32, 'g_final': _jnp.float32}
MOMENT_SCALE = {'g_mix': 3.315694e-01, 'w_in': 1.945403e-01, 'rel_bias': 9.224045e-02, 'conv_w': 2.045343e-01, 'conv_b': 2.698848e+00, 'w_rg': 7.983135e-02, 'b_rg': 6.061761e-02, 'w_ig': 1.486270e-01, 'b_ig': 7.362869e-02, 'lru_L': 1.265180e-01, 'g_out_attn': 2.175531e-01, 'g_out_lru': 2.168048e-01, 'w_out': 2.187052e-01, 'g_cross': 2.158118e-02, 'g_mem': 3.033855e-02, 'wq_c': 2.073602e-02, 'wk_c': 2.084142e-02, 'wv_c': 2.337564e-02, 'wo_c': 2.349102e-02, 'g_ffn': 1.549815e-01, 'w_gate': 6.206696e-02, 'w_up': 6.116270e-02, 'w_down': 1.018957e-01, 'g_final': 6.394844e+01}


def _to_microbatches(a, axis):
    t = _jnp.moveaxis(a, axis, 0)
    t = t.reshape((N_MICROBATCH, t.shape[0] // N_MICROBATCH) + t.shape[1:])
    return _jnp.moveaxis(t, 1, axis + 1)


def setup_inputs(seed: int = 0) -> dict:
    inp = _fwd_setup_inputs(seed)
    key = _jax.random.fold_in(_jax.random.key(seed), 7919)
    shape, _ = _output_shape()
    out = dict(inp)
    out["loss_target"] = _jax.random.normal(_jax.random.fold_in(key, 0), shape, _jnp.float32)
    for i, name in enumerate(TWIN_WEIGHTS):
        w = inp[name].astype(_jnp.float32)
        if MOMENT_SCALE is None:
            s = _jnp.sqrt(_jnp.mean(_jnp.square(w)) + 1e-30)
        else:
            s = MOMENT_SCALE[name]
        km, kv = _jax.random.split(_jax.random.fold_in(key, i + 1))
        out[name] = w
        out["m_" + name] = s * _jax.random.normal(km, w.shape, _jnp.float32)
        out["v_" + name] = (s * s) * _jax.random.uniform(kv, w.shape, _jnp.float32, 0.5, 1.5)
    if N_MICROBATCH > 1:
        for name, axis in PER_EXAMPLE_BATCH_AXIS.items():
            out[name] = _to_microbatches(out[name], axis)
    return {'x': out['x'], 'mem': out['mem'], 'g_mix': out['g_mix'], 'w_in': out['w_in'], 'rel_bias': out['rel_bias'], 'conv_w': out['conv_w'], 'conv_b': out['conv_b'], 'w_rg': out['w_rg'], 'b_rg': out['b_rg'], 'w_ig': out['w_ig'], 'b_ig': out['b_ig'], 'lru_L': out['lru_L'], 'g_out_attn': out['g_out_attn'], 'g_out_lru': out['g_out_lru'], 'w_out': out['w_out'], 'g_cross': out['g_cross'], 'g_mem': out['g_mem'], 'wq_c': out['wq_c'], 'wk_c': out['wk_c'], 'wv_c': out['wv_c'], 'wo_c': out['wo_c'], 'g_ffn': out['g_ffn'], 'w_gate': out['w_gate'], 'w_up': out['w_up'], 'w_down': out['w_down'], 'g_final': out['g_final'], 'loss_target': out['loss_target'], 'm_g_mix': out['m_g_mix'], 'm_w_in': out['m_w_in'], 'm_rel_bias': out['m_rel_bias'], 'm_conv_w': out['m_conv_w'], 'm_conv_b': out['m_conv_b'], 'm_w_rg': out['m_w_rg'], 'm_b_rg': out['m_b_rg'], 'm_w_ig': out['m_w_ig'], 'm_b_ig': out['m_b_ig'], 'm_lru_L': out['m_lru_L'], 'm_g_out_attn': out['m_g_out_attn'], 'm_g_out_lru': out['m_g_out_lru'], 'm_w_out': out['m_w_out'], 'm_g_cross': out['m_g_cross'], 'm_g_mem': out['m_g_mem'], 'm_wq_c': out['m_wq_c'], 'm_wk_c': out['m_wk_c'], 'm_wv_c': out['m_wv_c'], 'm_wo_c': out['m_wo_c'], 'm_g_ffn': out['m_g_ffn'], 'm_w_gate': out['m_w_gate'], 'm_w_up': out['m_w_up'], 'm_w_down': out['m_w_down'], 'm_g_final': out['m_g_final'], 'v_g_mix': out['v_g_mix'], 'v_w_in': out['v_w_in'], 'v_rel_bias': out['v_rel_bias'], 'v_conv_w': out['v_conv_w'], 'v_conv_b': out['v_conv_b'], 'v_w_rg': out['v_w_rg'], 'v_b_rg': out['v_b_rg'], 'v_w_ig': out['v_w_ig'], 'v_b_ig': out['v_b_ig'], 'v_lru_L': out['v_lru_L'], 'v_g_out_attn': out['v_g_out_attn'], 'v_g_out_lru': out['v_g_out_lru'], 'v_w_out': out['v_w_out'], 'v_g_cross': out['v_g_cross'], 'v_g_mem': out['v_g_mem'], 'v_wq_c': out['v_wq_c'], 'v_wk_c': out['v_wk_c'], 'v_wv_c': out['v_wv_c'], 'v_wo_c': out['v_wo_c'], 'v_g_ffn': out['v_g_ffn'], 'v_w_gate': out['v_w_gate'], 'v_w_up': out['v_w_up'], 'v_w_down': out['v_w_down'], 'v_g_final': out['v_g_final']}


def _loss(weights, diff, rest, loss_target):
    with _jax.named_scope("forward"):
        args = {**rest, TWIN_DIFF_INPUT: diff, **{k: w.astype(_WEIGHT_DTYPES[k]) for k, w in weights.items()}}
        y = _forward(args)
    with _jax.named_scope("loss_head"):
        err = _jnp.square(y.astype(_jnp.float32) - loss_target)
        return 0.5 * _jnp.sum(_jnp.mean(err, axis=-1)) if err.ndim else 0.5 * err


def _adamw(w, g, m, v):
    m = ADAM_B1 * m + (1.0 - ADAM_B1) * g
    v = ADAM_B2 * v + (1.0 - ADAM_B2) * _jnp.square(g)
    m_hat = m / (1.0 - ADAM_B1 ** ADAM_STEP)
    v_hat = v / (1.0 - ADAM_B2 ** ADAM_STEP)
    delta = -ADAM_LR * (m_hat / (_jnp.sqrt(v_hat) + ADAM_EPS) + ADAM_WD * w)
    return delta, m, v


def reference(x, mem, g_mix, w_in, rel_bias, conv_w, conv_b, w_rg, b_rg, w_ig, b_ig, lru_L, g_out_attn, g_out_lru, w_out, g_cross, g_mem, wq_c, wk_c, wv_c, wo_c, g_ffn, w_gate, w_up, w_down, g_final, loss_target, m_g_mix, m_w_in, m_rel_bias, m_conv_w, m_conv_b, m_w_rg, m_b_rg, m_w_ig, m_b_ig, m_lru_L, m_g_out_attn, m_g_out_lru, m_w_out, m_g_cross, m_g_mem, m_wq_c, m_wk_c, m_wv_c, m_wo_c, m_g_ffn, m_w_gate, m_w_up, m_w_down, m_g_final, v_g_mix, v_w_in, v_rel_bias, v_conv_w, v_conv_b, v_w_rg, v_b_rg, v_w_ig, v_b_ig, v_lru_L, v_g_out_attn, v_g_out_lru, v_w_out, v_g_cross, v_g_mem, v_wq_c, v_wk_c, v_wv_c, v_wo_c, v_g_ffn, v_w_gate, v_w_up, v_w_down, v_g_final):
    given = dict(x=x, mem=mem, g_mix=g_mix, w_in=w_in, rel_bias=rel_bias, conv_w=conv_w, conv_b=conv_b, w_rg=w_rg, b_rg=b_rg, w_ig=w_ig, b_ig=b_ig, lru_L=lru_L, g_out_attn=g_out_attn, g_out_lru=g_out_lru, w_out=w_out, g_cross=g_cross, g_mem=g_mem, wq_c=wq_c, wk_c=wk_c, wv_c=wv_c, wo_c=wo_c, g_ffn=g_ffn, w_gate=w_gate, w_up=w_up, w_down=w_down, g_final=g_final, loss_target=loss_target, m_g_mix=m_g_mix, m_w_in=m_w_in, m_rel_bias=m_rel_bias, m_conv_w=m_conv_w, m_conv_b=m_conv_b, m_w_rg=m_w_rg, m_b_rg=m_b_rg, m_w_ig=m_w_ig, m_b_ig=m_b_ig, m_lru_L=m_lru_L, m_g_out_attn=m_g_out_attn, m_g_out_lru=m_g_out_lru, m_w_out=m_w_out, m_g_cross=m_g_cross, m_g_mem=m_g_mem, m_wq_c=m_wq_c, m_wk_c=m_wk_c, m_wv_c=m_wv_c, m_wo_c=m_wo_c, m_g_ffn=m_g_ffn, m_w_gate=m_w_gate, m_w_up=m_w_up, m_w_down=m_w_down, m_g_final=m_g_final, v_g_mix=v_g_mix, v_w_in=v_w_in, v_rel_bias=v_rel_bias, v_conv_w=v_conv_w, v_conv_b=v_conv_b, v_w_rg=v_w_rg, v_b_rg=v_b_rg, v_w_ig=v_w_ig, v_b_ig=v_b_ig, v_lru_L=v_lru_L, v_g_out_attn=v_g_out_attn, v_g_out_lru=v_g_out_lru, v_w_out=v_w_out, v_g_cross=v_g_cross, v_g_mem=v_g_mem, v_wq_c=v_wq_c, v_wk_c=v_wk_c, v_wv_c=v_wv_c, v_wo_c=v_wo_c, v_g_ffn=v_g_ffn, v_w_gate=v_w_gate, v_w_up=v_w_up, v_w_down=v_w_down, v_g_final=v_g_final)
    weights = {n: given[n] for n in TWIN_WEIGHTS}
    shared = {n: given[n] for n in SHARED_INPUTS}
    per_example = {n: given[n] for n in ['x', 'mem']}
    grad_fn = _jax.value_and_grad(_loss, argnums=(0, 1))

    def one_microbatch(ex, loss_target):
        ex = dict(ex)
        diff = ex.pop(TWIN_DIFF_INPUT)
        return grad_fn(weights, diff, {**shared, **ex}, loss_target)

    if N_MICROBATCH == 1:
        loss, (grad_w, grad_x) = one_microbatch(per_example, given["loss_target"])
    else:
        def body(carry, xs):
            loss_sum, grad_sum = carry
            l_k, (gw_k, gx_k) = one_microbatch(xs[0], xs[1])
            with _jax.named_scope("update"):
                return (loss_sum + l_k, _jax.tree.map(_jnp.add, grad_sum, gw_k)), gx_k

        init = (_jnp.zeros((), _jnp.float32), _jax.tree.map(_jnp.zeros_like, weights))
        (loss, grad_w), grad_x = _jax.lax.scan(body, init, (per_example, given["loss_target"]))
    with _jax.named_scope("update"):
        delta_w, new_m, new_v = {}, {}, {}
        for n in TWIN_WEIGHTS:
            delta_w[n], new_m[n], new_v[n] = _adamw(weights[n], grad_w[n], given["m_" + n], given["v_" + n])
    return (loss, grad_x, *[grad_w[n] for n in TWIN_WEIGHTS], *[delta_w[n] for n in TWIN_WEIGHTS],
            *[new_m[n] for n in TWIN_WEIGHTS], *[new_v[n] for n in TWIN_WEIGHTS])
```

```python
import functools
import math

import jax
import jax.numpy as jnp
from jax import lax
from jax.experimental import pallas as pl
from jax.experimental.pallas import tpu as pltpu

F32 = jnp.float32
BF16 = jnp.bfloat16

D_MODEL = 1024
D_ATT = 512
D_LRU = 512
HEAD_DIM = 64
ATT_HEADS = 8
CHUNK = 64
LEFT_CHUNKS = 8
X_HEADS = 4
X_HEAD_DIM = 256
N_SHARD = 4
IN_SH = 640
FF_SH = 704
EPS = 1e-6
LRU_C = 8.0
QB = 256
KB = 768
ROLL_W = 1024
NEG = -1e30
ATT_SCALE = HEAD_DIM ** -0.5
X_SCALE = X_HEAD_DIM ** -0.5

ADAM_LR = 0.001
ADAM_B1 = 0.9
ADAM_B2 = 0.999
ADAM_EPS = 1e-08
ADAM_WD = 0.01
ADAM_STEP = 10

VMEM_LIMIT_V7X = 56 * 1024 * 1024
MESH_ID = pl.DeviceIdType.MESH

WEIGHTS = ['g_mix', 'w_in', 'rel_bias', 'conv_w', 'conv_b', 'w_rg', 'b_rg', 'w_ig', 'b_ig', 'lru_L',
           'g_out_attn', 'g_out_lru', 'w_out', 'g_cross', 'g_mem', 'wq_c', 'wk_c', 'wv_c', 'wo_c',
           'g_ffn', 'w_gate', 'w_up', 'w_down', 'g_final']
BIG = ['w_in', 'w_out', 'wq_c', 'wk_c', 'wv_c', 'wo_c', 'w_gate', 'w_up', 'w_down']
SMALL = [n for n in WEIGHTS if n not in BIG]
SMALL_SHAPES = {
    'g_mix': (1, 1024), 'rel_bias': (1, 8, 257), 'conv_w': (1, 4, 512), 'conv_b': (1, 512),
    'w_rg': (1, 8, 64, 64), 'b_rg': (1, 512), 'w_ig': (1, 8, 64, 64), 'b_ig': (1, 512), 'lru_L': (1, 512),
    'g_out_attn': (1, 512), 'g_out_lru': (1, 512), 'g_cross': (1, 1024), 'g_mem': (1, 1024),
    'g_ffn': (1, 1024), 'g_final': (1024,)}


def _sds(shape, dtype):
    return jax.ShapeDtypeStruct(shape, dtype)


def _cp(*sem):
    return pltpu.CompilerParams(dimension_semantics=sem or None, vmem_limit_bytes=VMEM_LIMIT_V7X)


def _rows(tm, n):
    return pl.BlockSpec((tm, n), lambda i: (i, 0))


def _full(shape):
    nd = len(shape)
    return pl.BlockSpec(shape, lambda i: (0,) * nd)


def _dot(a, b):
    return jnp.dot(a, b, preferred_element_type=F32)


def _dot_nt(a, b):
    return lax.dot_general(a, b, (((1,), (1,)), ((), ())), preferred_element_type=F32)


def _dot_tn(a, b):
    return lax.dot_general(a, b, (((0,), (0,)), ((), ())), preferred_element_type=F32)


def _rinv(x):
    return lax.rsqrt(jnp.mean(x * x, axis=-1, keepdims=True) + EPS)


def _rms_bwd(dy, x, g):
    r = _rinv(x)
    yh = x * r
    dyh = dy * g
    dx = r * (dyh - yh * jnp.mean(dyh * yh, axis=-1, keepdims=True))
    return dx, jnp.sum(dy * yh, axis=0, keepdims=True)


def _gelu(x):
    c = math.sqrt(2.0 / math.pi)
    t = jnp.tanh(c * (x + 0.044715 * x * x * x))
    return 0.5 * x * (1.0 + t)


def _gelu_and_grad(x):
    c = math.sqrt(2.0 / math.pi)
    t = jnp.tanh(c * (x + 0.044715 * x * x * x))
    g = 0.5 * x * (1.0 + t)
    dg = 0.5 * (1.0 + t) + 0.5 * x * (1.0 - t * t) * c * (1.0 + 3.0 * 0.044715 * x * x)
    return g, dg


def _neg_expm1(z):
    series = -z * (1 + z / 2 * (1 + z / 3 * (1 + z / 4 * (1 + z / 5 * (1 + z / 6 * (1 + z / 7))))))
    return jnp.where(z > -0.25, series, 1.0 - jnp.exp(z))


def _lru_gates(u, wrg, brg, wig, big, lam):
    ub = u.astype(BF16)
    r = jax.nn.sigmoid(_dot(ub, wrg) + brg)
    ig = jax.nn.sigmoid(_dot(ub, wig) + big)
    sp = jnp.maximum(-lam, 0.0) + jnp.log1p(jnp.exp(-jnp.abs(lam)))
    la = -LRU_C * r * sp
    a = jnp.exp(la)
    mult = jnp.sqrt(jnp.maximum(_neg_expm1(2.0 * la), 0.0))
    return ub, r, ig, sp, a, mult


def _scan8(a8, b8, hprev):
    row = lax.broadcasted_iota(jnp.int32, a8.shape, 0)
    aa, bb = a8, b8
    for d in (1, 2, 4):
        a_s = pltpu.roll(aa, d, 0)
        b_s = pltpu.roll(bb, d, 0)
        m = row >= d
        bb = jnp.where(m, aa * b_s + bb, bb)
        aa = jnp.where(m, aa * a_s, aa)
    return aa * hprev + bb


def _rscan8(c8, d8, lnext):
    row = lax.broadcasted_iota(jnp.int32, c8.shape, 0)
    cc, dd = c8, d8
    for d in (1, 2, 4):
        c_s = pltpu.roll(cc, 8 - d, 0)
        d_s = pltpu.roll(dd, 8 - d, 0)
        m = row < 8 - d
        dd = jnp.where(m, cc * d_s + dd, dd)
        cc = jnp.where(m, cc * c_s, cc)
    return cc * lnext + dd


def _f_inproj(x, g_mix, w_in_g, tm):
    s_len = x.shape[0]

    def body(x_ref, g_ref, w_ref, h_ref, qkv_ref, xg_ref):
        xv = x_ref[...]
        h = (xv * _rinv(xv) * g_ref[...]).astype(BF16)
        h_ref[...] = h
        qkv_ref[:, 0:640] = _dot(h, w_ref[0]).astype(BF16)
        qkv_ref[:, 640:1280] = _dot(h, w_ref[1]).astype(BF16)
        p2 = _dot(h, w_ref[2])
        qkv_ref[:, 1280:1536] = p2[:, 0:256].astype(BF16)
        xg_ref[:, 0:384] = p2[:, 256:640]
        xg_ref[:, 384:1024] = _dot(h, w_ref[3])

    return pl.pallas_call(
        body, name="f_inproj", grid=(s_len // tm,),
        in_specs=[_rows(tm, 1024), _full((1, 1024)), _full((N_SHARD, 1024, IN_SH))],
        out_specs=[_rows(tm, 1024), _rows(tm, 1536), _rows(tm, 1024)],
        out_shape=[_sds((s_len, 1024), BF16), _sds((s_len, 1536), BF16), _sds((s_len, 1024), F32)],
        compiler_params=_cp("parallel"))(x, g_mix, w_in_g)


def _bias_table(frow_ref, bias_sc):
    qa = lax.broadcasted_iota(jnp.int32, (QB, KB), 0) // CHUNK
    kb = lax.broadcasted_iota(jnp.int32, (QB, KB), 1) // CHUNK
    band = jnp.where((kb >= qa) & (kb - qa <= LEFT_CHUNKS), 0.0, NEG).astype(F32)
    for h in range(ATT_HEADS):
        row = jnp.broadcast_to(frow_ref[h:h + 1, :], (QB, ROLL_W))
        toep = pltpu.roll(row, 0, 1, stride=1, stride_axis=0)
        bias_sc[h] = toep[:, 0:KB] + band


def _att_scores(q, ks, bias, padmask):
    s = jnp.concatenate([_dot_nt(q, k) for k in ks], axis=1) * ATT_SCALE + bias + padmask
    m = jnp.max(s, axis=-1, keepdims=True)
    p = jnp.exp(s - m)
    return p, jnp.sum(p, axis=-1, keepdims=True)


def _pad_mask(blk):
    kpos = blk * QB - LEFT_CHUNKS * CHUNK + lax.broadcasted_iota(jnp.int32, (QB, KB), 1)
    return jnp.where(kpos >= 0, 0.0, NEG).astype(F32)


def _att_in_specs(clamp):
    def spec(j, col):
        return pl.BlockSpec((QB, D_ATT), lambda i: (clamp(i) + j, col))
    return [spec(2, 0), spec(0, 1), spec(1, 1), spec(2, 1), spec(0, 2), spec(1, 2), spec(2, 2)]


def _f_attn(qkv_pad, frow):
    s_len = qkv_pad.shape[0] - LEFT_CHUNKS * CHUNK
    nb = s_len // QB

    def body(q_ref, k0, k1, k2, v0, v1, v2, frow_ref, o_ref, bias_sc):
        i = pl.program_id(0)

        @pl.when(i == 0)
        def _():
            _bias_table(frow_ref, bias_sc)

        padmask = _pad_mask(i)
        for h in range(ATT_HEADS):
            sl = slice(h * HEAD_DIM, (h + 1) * HEAD_DIM)
            p, l = _att_scores(q_ref[:, sl], [k0[:, sl], k1[:, sl], k2[:, sl]], bias_sc[h], padmask)
            pb = p.astype(BF16)
            o = (_dot(pb[:, 0:QB], v0[:, sl]) + _dot(pb[:, QB:2 * QB], v1[:, sl])
                 + _dot(pb[:, 2 * QB:3 * QB], v2[:, sl]))
            o_ref[:, sl] = o / l

    return pl.pallas_call(
        body, name="f_attn", grid=(nb,),
        in_specs=_att_in_specs(lambda i: i) * 1 + [_full((ATT_HEADS, ROLL_W))],
        out_specs=_rows(QB, D_ATT),
        out_shape=_sds((s_len, D_ATT), F32),
        scratch_shapes=[pltpu.VMEM((ATT_HEADS, QB, KB), F32)],
        compiler_params=_cp("arbitrary"))(*([qkv_pad] * 7), frow)


def _f_lru(xg, conv_w, conv_b, wrg, brg, wig, big, lam, tl):
    s_len = xg.shape[0]

    def body(xg_ref, cw_ref, cb_ref, wrg_ref, brg_ref, wig_ref, big_ref, l_ref,
             rec_ref, u_ref, hs_ref, xbuf, a_sc, b_sc, hcar):
        i = pl.program_id(0)

        @pl.when(i == 0)
        def _():
            xbuf[0:8, :] = jnp.zeros((8, D_LRU), F32)
            hcar[...] = jnp.zeros((8, D_LRU), F32)

        xu0 = xg_ref[:, 0:D_LRU]
        xbuf[8:8 + tl, :] = xu0
        u = cb_ref[...] + cw_ref[0:1, :] * xbuf[pl.ds(5, tl), :]
        for j in range(1, 4):
            u = u + cw_ref[j:j + 1, :] * xbuf[pl.ds(5 + j, tl), :]
        xbuf[0:8, :] = xu0[tl - 8:tl, :]
        u_ref[...] = u
        _, _, ig, _, a, mult = _lru_gates(u, wrg_ref[...], brg_ref[...], wig_ref[...], big_ref[...], l_ref[...])
        a_sc[...] = a
        b_sc[...] = mult * (ig * u)

        def grp(g, hprev):
            off = pl.multiple_of(g * 8, 8)
            h8 = _scan8(a_sc[pl.ds(off, 8), :], b_sc[pl.ds(off, 8), :], hprev)
            hs_ref[pl.ds(off, 8), :] = h8
            return h8[7:8, :]

        hcar[0:1, :] = lax.fori_loop(0, tl // 8, grp, hcar[0:1, :])
        rec_ref[...] = hs_ref[...] * _gelu(xg_ref[:, D_LRU:2 * D_LRU])

    vec = _full((1, D_LRU))
    return pl.pallas_call(
        body, name="f_lru", grid=(s_len // tl,),
        in_specs=[_rows(tl, 1024), _full((4, D_LRU)), vec, _full((D_LRU, D_LRU)), vec,
                  _full((D_LRU, D_LRU)), vec, vec],
        out_specs=[_rows(tl, D_LRU)] * 3,
        out_shape=[_sds((s_len, D_LRU), F32)] * 3,
        scratch_shapes=[pltpu.VMEM((tl + 8, D_LRU), F32), pltpu.VMEM((tl, D_LRU), F32),
                        pltpu.VMEM((tl, D_LRU), F32), pltpu.VMEM((8, D_LRU), F32)],
        compiler_params=_cp("arbitrary"))(xg, conv_w, conv_b, wrg, brg, wig, big, lam)


def _f_mem(mem, g_mem, wk, wv):
    def body(mem_ref, g_ref, wk_ref, wv_ref, mn_ref, kx_ref, vx_ref):
        mv = mem_ref[...]
        mn = (mv * _rinv(mv) * g_ref[...]).astype(BF16)
        mn_ref[...] = mn
        kx_ref[...] = _dot(mn, wk_ref[...]).astype(BF16)
        vx_ref[...] = _dot(mn, wv_ref[...]).astype(BF16)

    m = mem.shape[0]
    return pl.pallas_call(
        body, name="f_mem", out_shape=[_sds((m, 1024), BF16)] * 3,
        compiler_params=_cp())(mem, g_mem, wk, wv)


def _xattn_probs(q, k):
    s = _dot_nt(q, k) * X_SCALE
    m = jnp.max(s, axis=-1, keepdims=True)
    p = jnp.exp(s - m)
    return p, jnp.sum(p, axis=-1, keepdims=True)


def _f_mid(x, att, rec, g_oa, g_ol, w_out, g_cross, wq, kx, vx, wo, tm):
    s_len = x.shape[0]
    m_len = kx.shape[0]

    def body(x_ref, att_ref, rec_ref, goa_ref, gol_ref, wout_ref, gc_ref, wq_ref, kx_ref, vx_ref, wo_ref,
             mg_ref, x1_ref, hc_ref, qx_ref, ox_ref, x2_ref):
        av = att_ref[...]
        rv = rec_ref[...]
        mg_ref[:, 0:D_ATT] = (av * _rinv(av) * goa_ref[...]).astype(BF16)
        mg_ref[:, D_ATT:1024] = (rv * _rinv(rv) * gol_ref[...]).astype(BF16)
        x1 = x_ref[...] + _dot(mg_ref[...], wout_ref[...])
        x1_ref[...] = x1
        hc = (x1 * _rinv(x1) * gc_ref[...]).astype(BF16)
        hc_ref[...] = hc
        qx_ref[...] = _dot(hc, wq_ref[...]).astype(BF16)
        for h in range(X_HEADS):
            sl = slice(h * X_HEAD_DIM, (h + 1) * X_HEAD_DIM)
            p, l = _xattn_probs(qx_ref[:, sl], kx_ref[:, sl])
            ox_ref[:, sl] = (_dot(p.astype(BF16), vx_ref[:, sl]) / l).astype(BF16)
        x2_ref[...] = x1 + _dot(ox_ref[...], wo_ref[...])

    sq = _full((1024, 1024))
    return pl.pallas_call(
        body, name="f_mid", grid=(s_len // tm,),
        in_specs=[_rows(tm, 1024), _rows(tm, 512), _rows(tm, 512), _full((1, 512)), _full((1, 512)), sq,
                  _full((1, 1024)), sq, _full((m_len, 1024)), _full((m_len, 1024)), sq],
        out_specs=[_rows(tm, 1024)] * 6,
        out_shape=[_sds((s_len, 1024), BF16), _sds((s_len, 1024), F32), _sds((s_len, 1024), BF16),
                   _sds((s_len, 1024), BF16), _sds((s_len, 1024), BF16), _sds((s_len, 1024), F32)],
        compiler_params=_cp("parallel"))(x, att, rec, g_oa, g_ol, w_out, g_cross, wq, kx, vx, wo)


def _load_weights_once(pairs):
    @pl.when(pl.program_id(0) == 0)
    def _():
        for hbm, vmem in pairs:
            pltpu.sync_copy(hbm, vmem)


def _any():
    return pl.BlockSpec(memory_space=pl.ANY)


def _sh_rows(tm, n):
    return pl.BlockSpec((N_SHARD, tm, n), lambda i: (0, i, 0))


def _f_ffn(x2, tgt, g_ffn, g_final, wg, wu, wd, tm):
    s_len = x2.shape[0]

    def body(x2_ref, t_ref, gf_ref, gfin_ref, wg_hbm, wu_hbm, wd_hbm,
             hf_ref, g_ref, u_ref, a_ref, dx3_ref, loss_ref, dgfin_ref, wg_ref, wu_ref, wd_ref):
        _load_weights_once([(wg_hbm, wg_ref), (wu_hbm, wu_ref), (wd_hbm, wd_ref)])

        @pl.when(pl.program_id(0) == 0)
        def _():
            loss_ref[...] = jnp.zeros_like(loss_ref)
            dgfin_ref[...] = jnp.zeros_like(dgfin_ref)

        x2v = x2_ref[...]
        hf = (x2v * _rinv(x2v) * gf_ref[...]).astype(BF16)
        hf_ref[...] = hf
        x3 = x2v
        for s in range(N_SHARD):
            gv = _dot(hf, wg_ref[s])
            uv = _dot(hf, wu_ref[s])
            av = (gv * jax.nn.sigmoid(gv) * uv).astype(BF16)
            g_ref[s] = gv.astype(BF16)
            u_ref[s] = uv.astype(BF16)
            a_ref[s] = av
            x3 = x3 + _dot(av, wd_ref[s])
        r3 = _rinv(x3)
        yh = x3 * r3
        gfin = gfin_ref[...]
        err = yh * gfin - t_ref[...]
        loss_ref[...] += jnp.full((1, 128), 0.5 / D_MODEL, F32) * jnp.sum(err * err)
        dy = err * (1.0 / D_MODEL)
        dgfin_ref[...] += jnp.sum(dy * yh, axis=0, keepdims=True)
        dyh = dy * gfin
        dx3_ref[...] = r3 * (dyh - yh * jnp.mean(dyh * yh, axis=-1, keepdims=True))

    vec = _full((1, 1024))
    return pl.pallas_call(
        body, name="f_ffn", grid=(s_len // tm,),
        in_specs=[_rows(tm, 1024), _rows(tm, 1024), vec, vec, _any(), _any(), _any()],
        out_specs=[_rows(tm, 1024), _sh_rows(tm, FF_SH), _sh_rows(tm, FF_SH), _sh_rows(tm, FF_SH),
                   _rows(tm, 1024), _full((1, 128)), vec],
        out_shape=[_sds((s_len, 1024), BF16)] + [_sds((N_SHARD, s_len, FF_SH), BF16)] * 3
                  + [_sds((s_len, 1024), F32), _sds((1, 128), F32), _sds((1, 1024), F32)],
        scratch_shapes=[pltpu.VMEM((N_SHARD, 1024, FF_SH), BF16), pltpu.VMEM((N_SHARD, 1024, FF_SH), BF16),
                        pltpu.VMEM((N_SHARD, FF_SH, 1024), BF16)],
        compiler_params=_cp("arbitrary"))(x2, tgt, g_ffn, g_final, wg, wu, wd)


def _b_ffn(dx3, x2, gact, uact, g_ffn, wg, wu, wd, tm):
    s_len = x2.shape[0]

    def body(dx3_ref, x2_ref, g_ref, u_ref, gf_ref, wg_hbm, wu_hbm, wd_hbm,
             dg_ref, du_ref, dx2_ref, dgf_ref, wg_ref, wu_ref, wd_ref):
        _load_weights_once([(wg_hbm, wg_ref), (wu_hbm, wu_ref), (wd_hbm, wd_ref)])

        @pl.when(pl.program_id(0) == 0)
        def _():
            dgf_ref[...] = jnp.zeros_like(dgf_ref)

        dx3v = dx3_ref[...]
        dx3b = dx3v.astype(BF16)
        dhf = jnp.zeros(dx3v.shape, F32)
        for s in range(N_SHARD):
            da = _dot_nt(dx3b, wd_ref[s])
            gv = g_ref[s].astype(F32)
            uv = u_ref[s].astype(F32)
            sg = jax.nn.sigmoid(gv)
            dub = (da * gv * sg).astype(BF16)
            dgb = (da * uv * (sg * (1.0 + gv * (1.0 - sg)))).astype(BF16)
            du_ref[s] = dub
            dg_ref[s] = dgb
            dhf = dhf + _dot_nt(dgb, wg_ref[s]) + _dot_nt(dub, wu_ref[s])
        dx, dgf = _rms_bwd(dhf, x2_ref[...], gf_ref[...])
        dx2_ref[...] = dx3v + dx
        dgf_ref[...] += dgf

    vec = _full((1, 1024))
    return pl.pallas_call(
        body, name="b_ffn", grid=(s_len // tm,),
        in_specs=[_rows(tm, 1024), _rows(tm, 1024), _sh_rows(tm, FF_SH), _sh_rows(tm, FF_SH), vec,
                  _any(), _any(), _any()],
        out_specs=[_sh_rows(tm, FF_SH), _sh_rows(tm, FF_SH), _rows(tm, 1024), vec],
        out_shape=[_sds((N_SHARD, s_len, FF_SH), BF16)] * 2 + [_sds((s_len, 1024), F32), _sds((1, 1024), F32)],
        scratch_shapes=[pltpu.VMEM((N_SHARD, 1024, FF_SH), BF16), pltpu.VMEM((N_SHARD, 1024, FF_SH), BF16),
                        pltpu.VMEM((N_SHARD, FF_SH, 1024), BF16)],
        compiler_params=_cp("arbitrary"))(dx3, x2, gact, uact, g_ffn, wg, wu, wd)


def _b_mid(dx2, qx, x1, att, rec, kx, vx, wo, wq, w_out, g_cross, g_oa, g_ol, tm):
    s_len = x1.shape[0]
    m_len = kx.shape[0]

    def body(dx2_ref, qx_ref, x1_ref, att_ref, rec_ref, kx_ref, vx_ref, wo_ref, wq_ref, wout_ref,
             gc_ref, goa_ref, gol_ref,
             dqx_ref, dx1_ref, datt_ref, drec_ref, dkx_ref, dvx_ref, dgc_ref, dgoa_ref, dgol_ref):
        @pl.when(pl.program_id(0) == 0)
        def _():
            for r in (dkx_ref, dvx_ref, dgc_ref, dgoa_ref, dgol_ref):
                r[...] = jnp.zeros_like(r)

        dx2v = dx2_ref[...]
        dox = _dot_nt(dx2v.astype(BF16), wo_ref[...])
        for h in range(X_HEADS):
            sl = slice(h * X_HEAD_DIM, (h + 1) * X_HEAD_DIM)
            q = qx_ref[:, sl]
            p, l = _xattn_probs(q, kx_ref[:, sl])
            pn = p / l
            dob = dox[:, sl].astype(BF16)
            dp = _dot_nt(dob, vx_ref[:, sl])
            dvx_ref[:, sl] += _dot_tn(pn.astype(BF16), dob)
            ds = pn * (dp - jnp.sum(dp * pn, axis=-1, keepdims=True))
            dsb = (ds * X_SCALE).astype(BF16)
            dqx_ref[:, sl] = _dot(dsb, kx_ref[:, sl]).astype(BF16)
            dkx_ref[:, sl] += _dot_tn(dsb, q)
        dhc = _dot_nt(dqx_ref[...], wq_ref[...])
        dx, dgc = _rms_bwd(dhc, x1_ref[...], gc_ref[...])
        dx1 = dx2v + dx
        dx1_ref[...] = dx1
        dgc_ref[...] += dgc
        dmg = _dot_nt(dx1.astype(BF16), wout_ref[...])
        da, dgoa = _rms_bwd(dmg[:, 0:D_ATT], att_ref[...], goa_ref[...])
        datt_ref[...] = da
        dgoa_ref[...] += dgoa
        dr, dgol = _rms_bwd(dmg[:, D_ATT:1024], rec_ref[...], gol_ref[...])
        drec_ref[...] = dr
        dgol_ref[...] += dgol

    sq = _full((1024, 1024))
    mk = _full((m_len, 1024))
    return pl.pallas_call(
        body, name="b_mid", grid=(s_len // tm,),
        in_specs=[_rows(tm, 1024), _rows(tm, 1024), _rows(tm, 1024), _rows(tm, 512), _rows(tm, 512), mk, mk,
                  sq, sq, sq, _full((1, 1024)), _full((1, 512)), _full((1, 512))],
        out_specs=[_rows(tm, 1024), _rows(tm, 1024), _rows(tm, 512), _rows(tm, 512), mk, mk,
                   _full((1, 1024)), _full((1, 512)), _full((1, 512))],
        out_shape=[_sds((s_len, 1024), BF16), _sds((s_len, 1024), F32), _sds((s_len, 512), F32),
                   _sds((s_len, 512), F32), _sds((m_len, 1024), F32), _sds((m_len, 1024), F32),
                   _sds((1, 1024), F32), _sds((1, 512), F32), _sds((1, 512), F32)],
        compiler_params=_cp("arbitrary"))(dx2, qx, x1, att, rec, kx, vx, wo, wq, w_out, g_cross, g_oa, g_ol)


def _b_mem(dkx, dvx, mem, mn, g_mem, wk, wv):
    def body(dkx_ref, dvx_ref, mem_ref, mn_ref, g_ref, wk_ref, wv_ref, dwk_ref, dwv_ref, dgm_ref):
        dkb = dkx_ref[...].astype(BF16)
        dvb = dvx_ref[...].astype(BF16)
        dwk_ref[...] = _dot_tn(mn_ref[...], dkb)
        dwv_ref[...] = _dot_tn(mn_ref[...], dvb)
        dmn = _dot_nt(dkb, wk_ref[...]) + _dot_nt(dvb, wv_ref[...])
        mv = mem_ref[...]
        dgm_ref[...] = jnp.sum(dmn * (mv * _rinv(mv)), axis=0, keepdims=True)

    return pl.pallas_call(
        body, name="b_mem",
        out_shape=[_sds((1024, 1024), F32), _sds((1024, 1024), F32), _sds((1, 1024), F32)],
        compiler_params=_cp())(dkx, dvx, mem, mn, g_mem, wk, wv)


def _b_lru(drec, hs, u, xg, conv_w, wrg, brg, wig, big, lam, tl):
    s_len = xg.shape[0]
    nt = s_len // tl

    def body(drec_ref, hs_ref, hsp_ref, u_ref, xg_ref, cw_ref, wrg_ref, brg_ref, wig_ref, big_ref, l_ref,
             dxg_ref, dwrg_ref, dwig_ref, dbrg_ref, dbig_ref, dlam_ref, dcw_ref, dcb_ref,
             hbuf, abuf, dubuf, c_sc, d_sc, lam_sc, lcar):
        i = pl.program_id(0)
        tt = nt - 1 - i

        @pl.when(i == 0)
        def _():
            for r in (dwrg_ref, dwig_ref, dbrg_ref, dbig_ref, dlam_ref, dcw_ref, dcb_ref):
                r[...] = jnp.zeros_like(r)
            abuf[tl:tl + 8, :] = jnp.zeros((8, D_LRU), F32)
            dubuf[tl:tl + 8, :] = jnp.zeros((8, D_LRU), F32)
            lcar[...] = jnp.zeros((8, D_LRU), F32)

        xu0 = xg_ref[:, 0:D_LRU]
        hsv = hs_ref[...]
        uv = u_ref[...]
        hbuf[8:8 + tl, :] = hsv
        hbuf[0:8, :] = jnp.where(tt > 0, hsp_ref[...], 0.0)
        hshift = hbuf[pl.ds(7, tl), :]
        wrg_v = wrg_ref[...]
        wig_v = wig_ref[...]
        lamv = l_ref[...]
        ub, r, ig, sp, a, mult = _lru_gates(uv, wrg_v, brg_ref[...], wig_v, big_ref[...], lamv)
        abuf[0:tl, :] = a
        c_sc[...] = abuf[pl.ds(1, tl), :]
        gel, dgel = _gelu_and_grad(xg_ref[:, D_LRU:2 * D_LRU])
        drv = drec_ref[...]
        d_sc[...] = drv * gel
        dxg_ref[:, D_LRU:2 * D_LRU] = (drv * hsv * dgel).astype(BF16)

        def grp(k, lnext):
            off = pl.multiple_of((tl // 8 - 1 - k) * 8, 8)
            l8 = _rscan8(c_sc[pl.ds(off, 8), :], d_sc[pl.ds(off, 8), :], lnext)
            lam_sc[pl.ds(off, 8), :] = l8
            return l8[0:1, :]

        lcar[0:1, :] = lax.fori_loop(0, tl // 8, grp, lcar[0:1, :])
        abuf[tl:tl + 8, :] = a[0:8, :]
        db = lam_sc[...]
        da = db * hshift
        dmult = db * (ig * uv)
        dig = db * mult * uv
        du = db * mult * ig
        dla = da * a - dmult * (a * a) / mult
        dlam_ref[...] += jnp.sum(dla * (-LRU_C) * r, axis=0, keepdims=True)
        dzr = dla * (-LRU_C * sp) * r * (1.0 - r)
        dzi = dig * ig * (1.0 - ig)
        dzrb = dzr.astype(BF16)
        dzib = dzi.astype(BF16)
        du = du + _dot_nt(dzrb, wrg_v) + _dot_nt(dzib, wig_v)
        dwrg_ref[...] += _dot_tn(ub, dzrb)
        dwig_ref[...] += _dot_tn(ub, dzib)
        dbrg_ref[...] += jnp.sum(dzr, axis=0, keepdims=True)
        dbig_ref[...] += jnp.sum(dzi, axis=0, keepdims=True)
        dcb_ref[...] += jnp.sum(du, axis=0, keepdims=True)
        dubuf[0:tl, :] = du
        dxu0 = jnp.zeros((tl, D_LRU), F32)
        for j in range(4):
            dsh = dubuf[pl.ds(3 - j, tl), :]
            dxu0 = dxu0 + cw_ref[j:j + 1, :] * dsh
            dcw_ref[j:j + 1, :] += jnp.sum(xu0 * dsh, axis=0, keepdims=True)
        dubuf[tl:tl + 8, :] = du[0:8, :]
        dxg_ref[:, 0:D_LRU] = dxu0.astype(BF16)

        @pl.when(i == nt - 1)
        def _():
            dlam_ref[...] = dlam_ref[...] * (-jax.nn.sigmoid(-lamv))

    def rev(n):
        return pl.BlockSpec((tl, n), lambda i: (nt - 1 - i, 0))

    prev8 = pl.BlockSpec((8, D_LRU), lambda i: (jnp.maximum((nt - 1 - i) * (tl // 8) - 1, 0), 0))
    vec = _full((1, D_LRU))
    sq = _full((D_LRU, D_LRU))
    return pl.pallas_call(
        body, name="b_lru", grid=(nt,),
        in_specs=[rev(D_LRU), rev(D_LRU), prev8, rev(D_LRU), rev(1024), _full((4, D_LRU)), sq, vec, sq, vec, vec],
        out_specs=[rev(1024), sq, sq, vec, vec, vec, _full((4, D_LRU)), vec],
        out_shape=[_sds((s_len, 1024), BF16), _sds((D_LRU, D_LRU), F32), _sds((D_LRU, D_LRU), F32),
                   _sds((1, D_LRU), F32), _sds((1, D_LRU), F32), _sds((1, D_LRU), F32),
                   _sds((4, D_LRU), F32), _sds((1, D_LRU), F32)],
        scratch_shapes=[pltpu.VMEM((tl + 8, D_LRU), F32)] * 3 + [pltpu.VMEM((tl, D_LRU), F32)] * 3
                       + [pltpu.VMEM((8, D_LRU), F32)],
        compiler_params=_cp("arbitrary"))(drec, hs, hs, u, xg, conv_w, wrg, brg, wig, big, lam)


def _b_attn(qkv_pad, datt, frow):
    s_len = datt.shape[0]
    nb = s_len // QB

    def body(q_ref, k0, k1, k2, v0, v1, v2, do_ref, frow_ref, dq_ref, dkv_ref, dfrow_ref,
             bias_sc, dt_sc, acc_sc):
        t = pl.program_id(0)

        @pl.when(t == 0)
        def _():
            _bias_table(frow_ref, bias_sc)
            dt_sc[...] = jnp.zeros_like(dt_sc)
            acc_sc[...] = jnp.zeros_like(acc_sc)

        @pl.when(t < nb)
        def _():
            padmask = _pad_mask(t)
            for h in range(ATT_HEADS):
                sl = slice(h * HEAD_DIM, (h + 1) * HEAD_DIM)
                q = q_ref[:, sl]
                ks = [k0[:, sl], k1[:, sl], k2[:, sl]]
                vs = [v0[:, sl], v1[:, sl], v2[:, sl]]
                p, l = _att_scores(q, ks, bias_sc[h], padmask)
                pn = p / l
                dob = do_ref[:, sl].astype(BF16)
                dp = jnp.concatenate([_dot_nt(dob, v) for v in vs], axis=1)
                ds = pn * (dp - jnp.sum(dp * pn, axis=-1, keepdims=True))
                dt_sc[h] += ds
                dsb = (ds * ATT_SCALE).astype(BF16)
                pnb = pn.astype(BF16)
                dq = _dot(dsb[:, 0:QB], ks[0])
                for j in (1, 2):
                    dq = dq + _dot(dsb[:, j * QB:(j + 1) * QB], ks[j])
                dq_ref[:, sl] = dq.astype(BF16)
                for j in range(3):
                    slot = (t + 1 + j) % 3
                    cs = slice(j * QB, (j + 1) * QB)
                    acc_sc[slot, :, sl] += _dot_tn(dsb[:, cs], q)
                    acc_sc[slot, :, D_ATT + h * HEAD_DIM:D_ATT + (h + 1) * HEAD_DIM] += _dot_tn(pnb[:, cs], dob)

        done = (t + 1) % 3

        @pl.when(t >= 2)
        def _():
            dkv_ref[...] = acc_sc[done].astype(BF16)

        acc_sc[done] = jnp.zeros((QB, 2 * D_ATT), F32)

        @pl.when(t == nb + 1)
        def _():
            row = lax.broadcasted_iota(jnp.int32, (8, ROLL_W), 0)
            pad = jnp.zeros((8, ROLL_W - KB), F32)
            for h in range(ATT_HEADS):
                acc8 = jnp.concatenate([dt_sc[h, 0:8, :], pad], axis=1)
                for a1 in range(1, QB // 8):
                    blk = jnp.concatenate([dt_sc[h, 8 * a1:8 * a1 + 8, :], pad], axis=1)
                    acc8 = acc8 + pltpu.roll(blk, ROLL_W - 8 * a1, 1)
                for k in range(3):
                    acc8 = jnp.where(((row >> k) & 1) == 1, pltpu.roll(acc8, ROLL_W - (1 << k), 1), acc8)
                dfrow_ref[h:h + 1, :] = jnp.sum(acc8, axis=0, keepdims=True)

    clamp = lambda t: jnp.minimum(t, nb - 1)
    return pl.pallas_call(
        body, name="b_attn", grid=(nb + 2,),
        in_specs=_att_in_specs(clamp) + [pl.BlockSpec((QB, D_ATT), lambda t: (clamp(t), 0)),
                                         _full((ATT_HEADS, ROLL_W))],
        out_specs=[pl.BlockSpec((QB, D_ATT), lambda t: (clamp(t), 0)),
                   pl.BlockSpec((QB, 2 * D_ATT), lambda t: (jnp.maximum(t - 2, 0), 0)),
                   _full((ATT_HEADS, ROLL_W))],
        out_shape=[_sds((s_len, D_ATT), BF16), _sds((s_len, 2 * D_ATT), BF16), _sds((ATT_HEADS, ROLL_W), F32)],
        scratch_shapes=[pltpu.VMEM((ATT_HEADS, QB, KB), F32), pltpu.VMEM((ATT_HEADS, QB, KB), F32),
                        pltpu.VMEM((3, QB, 2 * D_ATT), F32)],
        compiler_params=_cp("arbitrary"))(*([qkv_pad] * 7), datt, frow)


def _b_inproj(dq, dkv, dxg, h, x, dx1, g_mix, w_in_g, tm):
    s_len = x.shape[0]

    def body(dq_ref, dkv_ref, dxg_ref, h_ref, x_ref, dx1_ref, g_ref, w_ref, gx_ref, dgm_ref, dw_ref):
        @pl.when(pl.program_id(0) == 0)
        def _():
            dgm_ref[...] = jnp.zeros_like(dgm_ref)
            dw_ref[...] = jnp.zeros_like(dw_ref)

        dproj = jnp.concatenate([dq_ref[...], dkv_ref[...], dxg_ref[...]], axis=1)
        hv = h_ref[...]
        dh = jnp.zeros((tm, 1024), F32)
        for s in range(N_SHARD):
            dps = dproj[:, s * IN_SH:(s + 1) * IN_SH]
            dh = dh + _dot_nt(dps, w_ref[s])
            dw_ref[s] += _dot_tn(hv, dps)
        dx, dgm = _rms_bwd(dh, x_ref[...], g_ref[...])
        gx_ref[...] = dx1_ref[...] + dx
        dgm_ref[...] += dgm

    wspec = _full((N_SHARD, 1024, IN_SH))
    return pl.pallas_call(
        body, name="b_inproj", grid=(s_len // tm,),
        in_specs=[_rows(tm, 512), _rows(tm, 1024), _rows(tm, 1024), _rows(tm, 1024), _rows(tm, 1024),
                  _rows(tm, 1024), _full((1, 1024)), wspec],
        out_specs=[_rows(tm, 1024), _full((1, 1024)), wspec],
        out_shape=[_sds((s_len, 1024), F32), _sds((1, 1024), F32), _sds((N_SHARD, 1024, IN_SH), F32)],
        compiler_params=_cp("arbitrary"))(dq, dkv, dxg, h, x, dx1, g_mix, w_in_g)


def _mm_tn(xa, ya, name, ts):
    s_len, k = xa.shape
    n = ya.shape[1]

    def body(x_ref, y_ref, o_ref):
        @pl.when(pl.program_id(0) == 0)
        def _():
            o_ref[...] = jnp.zeros_like(o_ref)
        o_ref[...] += _dot_tn(x_ref[...].astype(BF16), y_ref[...].astype(BF16))

    return pl.pallas_call(
        body, name=name, grid=(s_len // ts,), in_specs=[_rows(ts, k), _rows(ts, n)],
        out_specs=_full((k, n)), out_shape=_sds((k, n), F32), compiler_params=_cp("arbitrary"))(xa, ya)


def _mm_tn_ysh(xa, y4, name, ts):
    s_len, k = xa.shape
    n = y4.shape[2]

    def body(x_ref, y_ref, o_ref):
        @pl.when(pl.program_id(0) == 0)
        def _():
            o_ref[...] = jnp.zeros_like(o_ref)
        xb = x_ref[...].astype(BF16)
        for s in range(N_SHARD):
            o_ref[s] += _dot_tn(xb, y_ref[s])

    return pl.pallas_call(
        body, name=name, grid=(s_len // ts,), in_specs=[_rows(ts, k), _sh_rows(ts, n)],
        out_specs=_full((N_SHARD, k, n)), out_shape=_sds((N_SHARD, k, n), F32),
        compiler_params=_cp("arbitrary"))(xa, y4)


def _mm_tn_xsh(x4, ya, name, ts):
    s_len, n = ya.shape
    k = x4.shape[2]

    def body(x_ref, y_ref, o_ref):
        @pl.when(pl.program_id(0) == 0)
        def _():
            o_ref[...] = jnp.zeros_like(o_ref)
        yb = y_ref[...].astype(BF16)
        for s in range(N_SHARD):
            o_ref[s] += _dot_tn(x_ref[s], yb)

    return pl.pallas_call(
        body, name=name, grid=(s_len // ts,), in_specs=[_sh_rows(ts, k), _rows(ts, n)],
        out_specs=_full((N_SHARD, k, n)), out_shape=_sds((N_SHARD, k, n), F32),
        compiler_params=_cp("arbitrary"))(x4, ya)


def _frow_from_rel_bias(rb):
    hi = jnp.broadcast_to(rb[:, 256:257], (ATT_HEADS, 385))
    mid = rb[:, 1:256][:, ::-1]
    lo = jnp.broadcast_to(rb[:, 0:1], (ATT_HEADS, 128))
    wrap = jnp.broadcast_to(rb[:, 256:257], (ATT_HEADS, ROLL_W - KB))
    return jnp.concatenate([hi, mid, lo, wrap], axis=1)


def _rel_bias_grad_from_dfrow(df):
    g256 = jnp.sum(df[:, 0:385], axis=1, keepdims=True) + jnp.sum(df[:, KB:ROLL_W], axis=1, keepdims=True)
    mid = df[:, 385:640][:, ::-1]
    g0 = jnp.sum(df[:, 640:KB], axis=1, keepdims=True)
    return jnp.concatenate([g0, mid, g256], axis=1)


def _block_diag(w):
    eye = jnp.eye(8, dtype=w.dtype)
    return (w[:, :, None, :] * eye[:, None, :, None]).reshape(D_LRU, D_LRU)


def _block_diag_extract(dense):
    eye = jnp.eye(8, dtype=dense.dtype)
    return jnp.sum(dense.reshape(8, 64, 8, 64) * eye[:, None, :, None], axis=2)


def _local_step(x, mem, tgt, p, gw):
    s_len = x.shape[0]
    tm = min(256, s_len)
    tl = min(512, s_len)
    w_out = gw['w_out'].reshape(1024, 1024)
    wq = gw['wq_c'].reshape(1024, 1024)
    wk = gw['wk_c'].reshape(1024, 1024)
    wv = gw['wv_c'].reshape(1024, 1024)
    wo = gw['wo_c'].reshape(1024, 1024)
    frow = _frow_from_rel_bias(p['rel_bias'])
    wrg = _block_diag(p['w_rg']).astype(BF16)
    wig = _block_diag(p['w_ig']).astype(BF16)

    h, qkv, xg = _f_inproj(x, p['g_mix'], gw['w_in'], tm)
    qkv_pad = jnp.pad(qkv, ((LEFT_CHUNKS * CHUNK, 0), (0, 0)))
    att = _f_attn(qkv_pad, frow)
    rec, u, hs = _f_lru(xg, p['conv_w'], p['conv_b'], wrg, p['b_rg'], wig, p['b_ig'], p['lru_L'], tl)
    mn, kx, vx = _f_mem(mem, p['g_mem'], wk, wv)
    mg, x1, hc, qx, ox, x2 = _f_mid(x, att, rec, p['g_out_attn'], p['g_out_lru'], w_out, p['g_cross'],
                                    wq, kx, vx, wo, tm)
    hf, gact, uact, aact, dx3, loss, dg_final = _f_ffn(x2, tgt, p['g_ffn'], p['g_final'],
                                                       gw['w_gate'], gw['w_up'], gw['w_down'], tm)

    dgact, duact, dx2, dg_ffn = _b_ffn(dx3, x2, gact, uact, p['g_ffn'], gw['w_gate'], gw['w_up'], gw['w_down'], tm)
    dqx, dx1, datt, drec, dkx, dvx, dg_cross, dg_oa, dg_ol = _b_mid(
        dx2, qx, x1, att, rec, kx, vx, wo, wq, w_out, p['g_cross'], p['g_out_attn'], p['g_out_lru'], tm)
    dwk, dwv, dg_mem = _b_mem(dkx, dvx, mem, mn, p['g_mem'], wk, wv)
    dxg, dwrg, dwig, dbrg, dbig, dlam, dcw, dcb = _b_lru(
        drec, hs, u, xg, p['conv_w'], wrg, p['b_rg'], wig, p['b_ig'], p['lru_L'], tl)
    dq, dkv, dfrow = _b_attn(qkv_pad, datt, frow)
    grad_x, dg_mix, dw_in = _b_inproj(dq, dkv, dxg, h, x, dx1, p['g_mix'], gw['w_in'], tm)

    ts = min(512, s_len)
    big = {
        'w_in': dw_in,
        'w_out': _mm_tn(mg, dx1, "dw_out", ts).reshape(N_SHARD, 256, 1024),
        'wq_c': _mm_tn(hc, dqx, "dw_q", ts).reshape(N_SHARD, 256, 1024),
        'wk_c': dwk.reshape(N_SHARD, 256, 1024),
        'wv_c': dwv.reshape(N_SHARD, 256, 1024),
        'wo_c': _mm_tn(ox, dx2, "dw_o", ts).reshape(N_SHARD, 256, 1024),
        'w_gate': _mm_tn_ysh(hf, dgact, "dw_gate", ts),
        'w_up': _mm_tn_ysh(hf, duact, "dw_up", ts),
        'w_down': _mm_tn_xsh(aact, dx3, "dw_down", ts),
    }
    small = {
        'g_mix': dg_mix, 'rel_bias': _rel_bias_grad_from_dfrow(dfrow), 'conv_w': dcw, 'conv_b': dcb,
        'w_rg': _block_diag_extract(dwrg), 'b_rg': dbrg, 'w_ig': _block_diag_extract(dwig), 'b_ig': dbig,
        'lru_L': dlam,
        'g_out_attn': dg_oa, 'g_out_lru': dg_ol, 'g_cross': dg_cross, 'g_mem': dg_mem, 'g_ffn': dg_ffn,
        'g_final': dg_final,
    }
    return jnp.sum(loss[0, 0:1]), grad_x, small, big


def _mesh_pos():
    return lax.axis_index("x"), lax.axis_index("y"), lax.axis_index("c")


def _other_chips(x, y):
    return [(1 - x, y), (x, 1 - y), (1 - x, 1 - y)]


def _cast_shards(ws):
    def body(*refs):
        n = len(refs) // 2
        for src, dst in zip(refs[:n], refs[n:]):
            dst[...] = src[...].astype(BF16)

    return pl.pallas_call(body, name="cast_shards", out_shape=[_sds(w.shape, BF16) for w in ws],
                          compiler_params=_cp())(*ws)


def _ag_weights(shards):
    n = len(shards)

    def body(*refs):
        ins, outs = refs[:n], refs[n:2 * n]
        send_sems, recv_sems, loc_sems = refs[2 * n:]
        x, y, c = _mesh_pos()
        mine = 2 * x + y
        chips = _other_chips(x, y)

        def remote(k, j, slot):
            px, py = chips[j]
            return pltpu.make_async_remote_copy(
                src_ref=ins[k], dst_ref=outs[k].at[slot], send_sem=send_sems.at[k, j], recv_sem=recv_sems.at[k, j],
                device_id=(px, py, c), device_id_type=MESH_ID)

        local = [pltpu.make_async_copy(ins[k], outs[k].at[mine], loc_sems.at[k]) for k in range(n)]
        for k in range(n):
            local[k].start()
            for j in range(3):
                remote(k, j, mine).start()
        for k in range(n):
            for j, (px, py) in enumerate(chips):
                remote(k, j, 2 * px + py).wait_recv()
        for k in range(n):
            for j in range(3):
                remote(k, j, mine).wait_send()
            local[k].wait()

    return pl.pallas_call(
        body, name="ag_weights",
        in_specs=[_any()] * n, out_specs=[_any()] * n,
        out_shape=[_sds((N_SHARD,) + w.shape, w.dtype) for w in shards],
        scratch_shapes=[pltpu.SemaphoreType.DMA((n, 3)), pltpu.SemaphoreType.DMA((n, 3)),
                        pltpu.SemaphoreType.DMA((n,))],
        compiler_params=_cp())(*shards)


def _rs_ici(grads):
    n = len(grads)

    def body(*refs):
        ins, outs = refs[:n], refs[n:2 * n]
        send_sems, recv_sems, loc_sems = refs[2 * n:]
        x, y, c = _mesh_pos()
        mine = 2 * x + y
        chips = _other_chips(x, y)

        def remote(k, j, src_slot, dst_slot):
            px, py = chips[j]
            return pltpu.make_async_remote_copy(
                src_ref=ins[k].at[src_slot], dst_ref=outs[k].at[dst_slot],
                send_sem=send_sems.at[k, j], recv_sem=recv_sems.at[k, j],
                device_id=(px, py, c), device_id_type=MESH_ID)

        local = [pltpu.make_async_copy(ins[k].at[mine], outs[k].at[mine], loc_sems.at[k]) for k in range(n)]
        for k in range(n):
            local[k].start()
            for j, (px, py) in enumerate(chips):
                remote(k, j, 2 * px + py, mine).start()
        for k in range(n):
            for j, (px, py) in enumerate(chips):
                remote(k, j, mine, 2 * px + py).wait_recv()
        for k in range(n):
            for j, (px, py) in enumerate(chips):
                remote(k, j, 2 * px + py, mine).wait_send()
            local[k].wait()

    return pl.pallas_call(
        body, name="rs_ici",
        in_specs=[_any()] * n, out_specs=[_any()] * n,
        out_shape=[_sds(g.shape, g.dtype) for g in grads],
        scratch_shapes=[pltpu.SemaphoreType.DMA((n, 3)), pltpu.SemaphoreType.DMA((n, 3)),
                        pltpu.SemaphoreType.DMA((n,))],
        compiler_params=_cp())(*grads)


def _sum4(recv, name):
    _, r, c = recv.shape
    tr = r // 4

    def body(in_ref, o_ref):
        o_ref[...] = ((in_ref[0] + in_ref[1]) + in_ref[2]) + in_ref[3]

    return pl.pallas_call(
        body, name=name, grid=(4,), in_specs=[pl.BlockSpec((N_SHARD, tr, c), lambda i: (0, i, 0))],
        out_specs=_rows(tr, c), out_shape=_sds((r, c), F32), compiler_params=_cp("parallel"))(recv)


def _swap_sibling(parts):
    n = len(parts)

    def body(*refs):
        ins, outs = refs[:n], refs[n:2 * n]
        send_sems, recv_sems = refs[2 * n:]
        x, y, c = _mesh_pos()
        copies = [pltpu.make_async_remote_copy(
            src_ref=ins[k], dst_ref=outs[k], send_sem=send_sems.at[k], recv_sem=recv_sems.at[k],
            device_id=(x, y, 1 - c), device_id_type=MESH_ID) for k in range(n)]
        for cp in copies:
            cp.start()
        for cp in copies:
            cp.wait()

    return pl.pallas_call(
        body, name="swap_sibling",
        in_specs=[_any()] * n, out_specs=[_any()] * n,
        out_shape=[_sds(p.shape, p.dtype) for p in parts],
        scratch_shapes=[pltpu.SemaphoreType.DMA((n,)), pltpu.SemaphoreType.DMA((n,))],
        compiler_params=_cp())(*parts)


def _adamw_math(w, g, m, v):
    m = ADAM_B1 * m + (1.0 - ADAM_B1) * g
    v = ADAM_B2 * v + (1.0 - ADAM_B2) * (g * g)
    m_hat = m / (1.0 - ADAM_B1 ** ADAM_STEP)
    v_hat = v / (1.0 - ADAM_B2 ** ADAM_STEP)
    delta = -ADAM_LR * (m_hat / (jnp.sqrt(v_hat) + ADAM_EPS) + ADAM_WD * w)
    return delta, m, v


def _final_adamw(pa, pb, w, m, v, name):
    r, c = w.shape
    tr = r // 4

    def body(pa_ref, pb_ref, w_ref, m_ref, v_ref, g_ref, d_ref, nm_ref, nv_ref):
        g = pa_ref[...] + pb_ref[...]
        g_ref[...] = g
        d_ref[...], nm_ref[...], nv_ref[...] = _adamw_math(w_ref[...], g, m_ref[...], v_ref[...])

    return pl.pallas_call(
        body, name=name, grid=(4,), in_specs=[_rows(tr, c)] * 5, out_specs=[_rows(tr, c)] * 4,
        out_shape=[_sds((r, c), F32)] * 4, compiler_params=_cp("parallel"))(pa, pb, w, m, v)


def _adamw_small(w, g, m, v):
    def body(w_ref, g_ref, m_ref, v_ref, d_ref, nm_ref, nv_ref):
        d_ref[...], nm_ref[...], nv_ref[...] = _adamw_math(w_ref[...], g_ref[...], m_ref[...], v_ref[...])

    return pl.pallas_call(body, name="adamw_conv_w", out_shape=[_sds(w.shape, F32)] * 3,
                          compiler_params=_cp())(w, g, m, v)


def _ar_small(gp, wp, mp, vp):
    rows = gp.shape[0]

    def body(g_ref, w_ref, m_ref, v_ref, go_ref, d_ref, nm_ref, nv_ref, buf, send_sems, recv_sems):
        x, y, c = _mesh_pos()
        me = 4 * x + 2 * y + c

        def peer(k):
            px = 1 - x if k & 4 else x
            py = 1 - y if k & 2 else y
            pc = 1 - c if k & 1 else c
            return px, py, pc

        def remote(k, slot):
            return pltpu.make_async_remote_copy(
                src_ref=g_ref, dst_ref=buf.at[slot], send_sem=send_sems.at[k - 1], recv_sem=recv_sems.at[k - 1],
                device_id=peer(k), device_id_type=MESH_ID)

        for k in range(1, 8):
            remote(k, me).start()
        buf[me] = g_ref[...]
        for k in range(1, 8):
            px, py, pc = peer(k)
            remote(k, 4 * px + 2 * py + pc).wait_recv()
        for k in range(1, 8):
            remote(k, me).wait_send()
        tot = buf[0]
        for k in range(1, 8):
            tot = tot + buf[k]
        go_ref[...] = tot
        d_ref[...], nm_ref[...], nv_ref[...] = _adamw_math(w_ref[...], tot, m_ref[...], v_ref[...])

    return pl.pallas_call(
        body, name="ar_small", out_shape=[_sds((rows, 128), F32)] * 4,
        scratch_shapes=[pltpu.VMEM((8, rows, 128), F32), pltpu.SemaphoreType.DMA((7,)),
                        pltpu.SemaphoreType.DMA((7,))],
        compiler_params=_cp())(gp, wp, mp, vp)


def _pack(parts):
    flat = jnp.concatenate([parts[n].reshape(-1) for n in SMALL])
    rows = -(-flat.shape[0] // 1024) * 8
    return jnp.pad(flat, (0, rows * 128 - flat.shape[0])).reshape(rows, 128)


def _unpack(pack):
    flat = pack.reshape(-1)
    out, off = {}, 0
    for n in SMALL:
        size = math.prod(SMALL_SHAPES[n])
        out[n] = flat[off:off + size].reshape(SMALL_SHAPES[n])
        off += size
    return out


INPUT_NAMES = (['x', 'mem'] + WEIGHTS + ['loss_target'] + ['m_' + n for n in WEIGHTS] + ['v_' + n for n in WEIGHTS])


def kernel(x, mem, g_mix, w_in, rel_bias, conv_w, conv_b, w_rg, b_rg, w_ig, b_ig, lru_L, g_out_attn, g_out_lru, w_out, g_cross, g_mem, wq_c, wk_c, wv_c, wo_c, g_ffn, w_gate, w_up, w_down, g_final, loss_target, m_g_mix, m_w_in, m_rel_bias, m_conv_w, m_conv_b, m_w_rg, m_b_rg, m_w_ig, m_b_ig, m_lru_L, m_g_out_attn, m_g_out_lru, m_w_out, m_g_cross, m_g_mem, m_wq_c, m_wk_c, m_wv_c, m_wo_c, m_g_ffn, m_w_gate, m_w_up, m_w_down, m_g_final, v_g_mix, v_w_in, v_rel_bias, v_conv_w, v_conv_b, v_w_rg, v_b_rg, v_w_ig, v_b_ig, v_lru_L, v_g_out_attn, v_g_out_lru, v_w_out, v_g_cross, v_g_mem, v_wq_c, v_wk_c, v_wv_c, v_wo_c, v_g_ffn, v_w_gate, v_w_up, v_w_down, v_g_final):
    a = dict(zip(INPUT_NAMES, (x, mem, g_mix, w_in, rel_bias, conv_w, conv_b, w_rg, b_rg, w_ig, b_ig, lru_L, g_out_attn, g_out_lru, w_out, g_cross, g_mem, wq_c, wk_c, wv_c, wo_c, g_ffn, w_gate, w_up, w_down, g_final, loss_target, m_g_mix, m_w_in, m_rel_bias, m_conv_w, m_conv_b, m_w_rg, m_b_rg, m_w_ig, m_b_ig, m_lru_L, m_g_out_attn, m_g_out_lru, m_w_out, m_g_cross, m_g_mem, m_wq_c, m_wk_c, m_wv_c, m_wo_c, m_g_ffn, m_w_gate, m_w_up, m_w_down, m_g_final, v_g_mix, v_w_in, v_rel_bias, v_conv_w, v_conv_b, v_w_rg, v_b_rg, v_w_ig, v_b_ig, v_lru_L, v_g_out_attn, v_g_out_lru, v_w_out, v_g_cross, v_g_mem, v_wq_c, v_wk_c, v_wv_c, v_wo_c, v_g_ffn, v_w_gate, v_w_up, v_w_down, v_g_final)))
    chip = 2 * lax.axis_index("x") + lax.axis_index("y")

    gathered = _ag_weights(list(_cast_shards([a[n][0] for n in BIG])) + [a['conv_w'][0]])
    gw = dict(zip(BIG, gathered[:-1]))
    conv_w_full = gathered[-1].transpose(1, 0, 2).reshape(4, D_LRU)

    p = {n: a[n] for n in SMALL}
    p['rel_bias'] = a['rel_bias'][0]
    p['w_rg'] = a['w_rg'][0]
    p['w_ig'] = a['w_ig'][0]
    p['conv_w'] = conv_w_full
    p['g_final'] = a['g_final'][None, :]
    loss_part, grad_x, small, big = _local_step(a['x'][0], a['mem'][0], a['loss_target'][0], p, gw)
    loss = lax.psum(loss_part, ("x", "y", "c"))

    recv = _rs_ici([big[n] for n in BIG])
    part = [_sum4(r, "sum4_" + n) for n, r in zip(BIG, recv)]
    sib = _swap_sibling(part)
    out = {}
    for n, pa, pb in zip(BIG, part, sib):
        out[n] = _final_adamw(pa, pb, a[n][0], a['m_' + n][0], a['v_' + n][0], "adamw_" + n)

    zeros_cw = jnp.zeros((1, 4, D_LRU), F32)
    def packed(prefix):
        d = {n: a[prefix + n] for n in SMALL}
        d['conv_w'] = zeros_cw
        return _pack(d)
    packs = _ar_small(_pack(small), packed(''), packed('m_'), packed('v_'))
    sg, sd, sm, sv = [_unpack(pk) for pk in packs]
    g_cw = lax.dynamic_slice(sg['conv_w'][0], (0, chip * 128), (4, 128))
    d_cw, m_cw, v_cw = _adamw_small(a['conv_w'][0], g_cw, a['m_conv_w'][0], a['v_conv_w'][0])
    sg['conv_w'], sd['conv_w'], sm['conv_w'], sv['conv_w'] = g_cw[None], d_cw[None], m_cw[None], v_cw[None]

    def leaf(i, n):
        if n in BIG:
            return out[n][i][None]
        return (sg, sd, sm, sv)[i][n]

    return (loss, grad_x[None], *[leaf(i, n) for i in range(4) for n in WEIGHTS])
```

```python
import math

import jax
import jax.numpy as jnp
from jax import lax
from jax.experimental import pallas as pl
from jax.experimental.pallas import tpu as pltpu

F32 = jnp.float32
BF16 = jnp.bfloat16

D_MODEL = 1024
D_ATT = 512
D_LRU = 512
HEAD_DIM = 64
ATT_HEADS = 8
CHUNK = 64
LEFT_CHUNKS = 8
X_HEADS = 4
X_HEAD_DIM = 256
N_SHARD = 4
IN_SH = 640
FF_SH = 704
EPS = 1e-6
LRU_C = 8.0
QB = 256
KB = 768
ROLL_W = 1024
NEG = -1e30
ATT_SCALE = HEAD_DIM ** -0.5
X_SCALE = X_HEAD_DIM ** -0.5

ADAM_LR = 0.001
ADAM_B1 = 0.9
ADAM_B2 = 0.999
ADAM_EPS = 1e-08
ADAM_WD = 0.01
ADAM_STEP = 10

VMEM_LIMIT_V7X = 56 * 1024 * 1024
MESH_ID = pl.DeviceIdType.MESH

WEIGHTS = ['g_mix', 'w_in', 'rel_bias', 'conv_w', 'conv_b', 'w_rg', 'b_rg', 'w_ig', 'b_ig', 'lru_L',
           'g_out_attn', 'g_out_lru', 'w_out', 'g_cross', 'g_mem', 'wq_c', 'wk_c', 'wv_c', 'wo_c',
           'g_ffn', 'w_gate', 'w_up', 'w_down', 'g_final']
BIG = ['w_in', 'w_out', 'wq_c', 'wk_c', 'wv_c', 'wo_c', 'w_gate', 'w_up', 'w_down']
SMALL = [n for n in WEIGHTS if n not in BIG]
SMALL_SHAPES = {
    'g_mix': (1, 1024), 'rel_bias': (1, 8, 257), 'conv_w': (1, 4, 512), 'conv_b': (1, 512),
    'w_rg': (1, 8, 64, 64), 'b_rg': (1, 512), 'w_ig': (1, 8, 64, 64), 'b_ig': (1, 512), 'lru_L': (1, 512),
    'g_out_attn': (1, 512), 'g_out_lru': (1, 512), 'g_cross': (1, 1024), 'g_mem': (1, 1024),
    'g_ffn': (1, 1024), 'g_final': (1024,)}


def _sds(shape, dtype):
    return jax.ShapeDtypeStruct(shape, dtype)


def _cp(*sem):
    return pltpu.CompilerParams(dimension_semantics=sem or None, vmem_limit_bytes=VMEM_LIMIT_V7X)


def _rows(tm, n):
    return pl.BlockSpec((tm, n), lambda i: (i, 0))


def _full(shape):
    nd = len(shape)
    return pl.BlockSpec(shape, lambda i: (0,) * nd)


def _dot(a, b):
    return jnp.dot(a, b, preferred_element_type=F32)


def _dot_nt(a, b):
    return lax.dot_general(a, b, (((1,), (1,)), ((), ())), preferred_element_type=F32)


def _dot_tn(a, b):
    return lax.dot_general(a, b, (((0,), (0,)), ((), ())), preferred_element_type=F32)


def _rinv(x):
    return lax.rsqrt(jnp.mean(x * x, axis=-1, keepdims=True) + EPS)


def _rms_bwd(dy, x, g):
    r = _rinv(x)
    yh = x * r
    dyh = dy * g
    dx = r * (dyh - yh * jnp.mean(dyh * yh, axis=-1, keepdims=True))
    return dx, jnp.sum(dy * yh, axis=0, keepdims=True)


def _gelu(x):
    c = math.sqrt(2.0 / math.pi)
    t = jnp.tanh(c * (x + 0.044715 * x * x * x))
    return 0.5 * x * (1.0 + t)


def _gelu_and_grad(x):
    c = math.sqrt(2.0 / math.pi)
    t = jnp.tanh(c * (x + 0.044715 * x * x * x))
    g = 0.5 * x * (1.0 + t)
    dg = 0.5 * (1.0 + t) + 0.5 * x * (1.0 - t * t) * c * (1.0 + 3.0 * 0.044715 * x * x)
    return g, dg


def _neg_expm1(z):
    series = -z * (1 + z / 2 * (1 + z / 3 * (1 + z / 4 * (1 + z / 5 * (1 + z / 6 * (1 + z / 7))))))
    return jnp.where(z > -0.25, series, 1.0 - jnp.exp(z))


def _lru_gates(u, wrg, brg, wig, big, lam):
    ub = u.astype(BF16)
    r = jax.nn.sigmoid(_dot(ub, wrg) + brg)
    ig = jax.nn.sigmoid(_dot(ub, wig) + big)
    sp = jnp.maximum(-lam, 0.0) + jnp.log1p(jnp.exp(-jnp.abs(lam)))
    la = -LRU_C * r * sp
    a = jnp.exp(la)
    mult = jnp.sqrt(jnp.maximum(_neg_expm1(2.0 * la), 0.0))
    return ub, r, ig, sp, a, mult


def _scan8(a8, b8, hprev):
    row = lax.broadcasted_iota(jnp.int32, a8.shape, 0)
    aa, bb = a8, b8
    for d in (1, 2, 4):
        a_s = pltpu.roll(aa, d, 0)
        b_s = pltpu.roll(bb, d, 0)
        m = row >= d
        bb = jnp.where(m, aa * b_s + bb, bb)
        aa = jnp.where(m, aa * a_s, aa)
    return aa * hprev + bb


def _mesh_pos():
    return lax.axis_index("x"), lax.axis_index("y"), lax.axis_index("c")


def _other_chips(x, y):
    return [(1 - x, y), (x, 1 - y), (1 - x, 1 - y)]


def _ag_copies(ins, outs, sems):
    send_sems, recv_sems, loc_sems = sems
    n = len(ins)
    x, y, c = _mesh_pos()
    mine = 2 * x + y
    chips = _other_chips(x, y)

    def remote(k, j, slot):
        px, py = chips[j]
        return pltpu.make_async_remote_copy(
            src_ref=ins[k], dst_ref=outs[k].at[slot], send_sem=send_sems.at[k, j], recv_sem=recv_sems.at[k, j],
            device_id=(px, py, c), device_id_type=MESH_ID)

    def local(k):
        return pltpu.make_async_copy(ins[k], outs[k].at[mine], loc_sems.at[k])

    def start():
        for k in range(n):
            local(k).start()
            for j in range(3):
                remote(k, j, mine).start()

    def wait():
        for k in range(n):
            for j, (px, py) in enumerate(chips):
                remote(k, j, 2 * px + py).wait_recv()
        for k in range(n):
            for j in range(3):
                remote(k, j, mine).wait_send()
            local(k).wait()

    return start, wait


def _rs_copies(ins, outs, sems):
    send_sems, recv_sems = sems
    n = len(ins)
    x, y, c = _mesh_pos()
    chips = _other_chips(x, y)

    def remote(k, j):
        px, py = chips[j]
        return pltpu.make_async_remote_copy(
            src_ref=ins[k].at[2 * px + py], dst_ref=outs[k].at[j],
            send_sem=send_sems.at[k, j], recv_sem=recv_sems.at[k, j],
            device_id=(px, py, c), device_id_type=MESH_ID)

    def start():
        for k in range(n):
            for j in range(3):
                remote(k, j).start()

    def wait():
        for k in range(n):
            for j in range(3):
                remote(k, j).wait_recv()
        for k in range(n):
            for j in range(3):
                remote(k, j).wait_send()

    return start, wait


def _comm_plan(comm):
    kind, arrs = comm
    k = len(arrs)
    if kind == "ag":
        shapes = [_sds((N_SHARD,) + w.shape, w.dtype) for w in arrs]
        sems = [pltpu.SemaphoreType.DMA((k, 3)), pltpu.SemaphoreType.DMA((k, 3)), pltpu.SemaphoreType.DMA((k,))]
        return _ag_copies, shapes, sems
    shapes = [_sds((3,) + g.shape[1:], g.dtype) for g in arrs]
    return _rs_copies, shapes, [pltpu.SemaphoreType.DMA((k, 3)), pltpu.SemaphoreType.DMA((k, 3))]


def _call(body, name, grid, in_specs, out_specs, out_shape, scratch, args, sem, comm=None):
    if comm is None:
        return pl.pallas_call(body, name=name, grid=grid, in_specs=in_specs, out_specs=out_specs,
                              out_shape=out_shape, scratch_shapes=scratch, compiler_params=_cp(sem))(*args)
    maker, c_shapes, c_sems = _comm_plan(comm)
    k = len(comm[1])
    n_in, n_out, n_scr = len(in_specs), len(out_specs), len(scratch)
    last = grid[0] - 1

    def wrapped(*refs):
        ins, cins = refs[:n_in], refs[n_in:n_in + k]
        o0 = n_in + k
        outs, couts = refs[o0:o0 + n_out], refs[o0 + n_out:o0 + n_out + k]
        s0 = o0 + n_out + k
        start, wait = maker(cins, couts, refs[s0 + n_scr:])
        pl.when(pl.program_id(0) == 0)(start)
        body(*ins, *outs, *refs[s0:s0 + n_scr])
        pl.when(pl.program_id(0) == last)(wait)

    return pl.pallas_call(
        wrapped, name=name, grid=grid, in_specs=list(in_specs) + [_any()] * k,
        out_specs=list(out_specs) + [_any()] * k, out_shape=list(out_shape) + c_shapes,
        scratch_shapes=list(scratch) + c_sems, compiler_params=_cp(sem))(*args, *comm[1])


def _comm_only(name, comm):
    maker, c_shapes, c_sems = _comm_plan(comm)
    k = len(comm[1])

    def body(*refs):
        start, wait = maker(refs[:k], refs[k:2 * k], refs[2 * k:])
        start()
        wait()

    return pl.pallas_call(body, name=name, in_specs=[_any()] * k, out_specs=[_any()] * k, out_shape=c_shapes,
                          scratch_shapes=c_sems, compiler_params=_cp())(*comm[1])


def _any():
    return pl.BlockSpec(memory_space=pl.ANY)


def _rscan8(c8, d8, lnext):
    row = lax.broadcasted_iota(jnp.int32, c8.shape, 0)
    cc, dd = c8, d8
    for d in (1, 2, 4):
        c_s = pltpu.roll(cc, 8 - d, 0)
        d_s = pltpu.roll(dd, 8 - d, 0)
        m = row < 8 - d
        dd = jnp.where(m, cc * d_s + dd, dd)
        cc = jnp.where(m, cc * c_s, cc)
    return cc * lnext + dd


def _f_inproj(x, g_mix, w_in_g, tm, comm=None):
    s_len = x.shape[0]

    def body(x_ref, g_ref, w_ref, h_ref, qkv_ref, xg_ref):
        xv = x_ref[...]
        h = (xv * _rinv(xv) * g_ref[...]).astype(BF16)
        h_ref[...] = h
        qkv_ref[:, 0:640] = _dot(h, w_ref[0]).astype(BF16)
        qkv_ref[:, 640:1280] = _dot(h, w_ref[1]).astype(BF16)
        p2 = _dot(h, w_ref[2])
        qkv_ref[:, 1280:1536] = p2[:, 0:256].astype(BF16)
        xg_ref[:, 0:384] = p2[:, 256:640]
        xg_ref[:, 384:1024] = _dot(h, w_ref[3])

    return _call(
        body, "f_inproj", (s_len // tm,),
        [_rows(tm, 1024), _full((1, 1024)), _full((N_SHARD, 1024, IN_SH))],
        [_rows(tm, 1024), _rows(tm, 1536), _rows(tm, 1024)],
        [_sds((s_len, 1024), BF16), _sds((s_len, 1536), BF16), _sds((s_len, 1024), F32)],
        [], (x, g_mix, w_in_g), "arbitrary", comm)


def _bias_table(frow_ref, bias_sc):
    qa = lax.broadcasted_iota(jnp.int32, (QB, KB), 0) // CHUNK
    kb = lax.broadcasted_iota(jnp.int32, (QB, KB), 1) // CHUNK
    band = jnp.where((kb >= qa) & (kb - qa <= LEFT_CHUNKS), 0.0, NEG).astype(F32)
    for h in range(ATT_HEADS):
        row = jnp.broadcast_to(frow_ref[h:h + 1, :], (QB, ROLL_W))
        toep = pltpu.roll(row, 0, 1, stride=1, stride_axis=0)
        bias_sc[h] = toep[:, 0:KB] + band


def _att_scores(q, ks, bias, padmask):
    s = jnp.concatenate([_dot_nt(q, k) for k in ks], axis=1) * ATT_SCALE + bias + padmask
    m = jnp.max(s, axis=-1, keepdims=True)
    p = jnp.exp(s - m)
    return p, jnp.sum(p, axis=-1, keepdims=True)


def _pad_mask(blk):
    kpos = blk * QB - LEFT_CHUNKS * CHUNK + lax.broadcasted_iota(jnp.int32, (QB, KB), 1)
    return jnp.where(kpos >= 0, 0.0, NEG).astype(F32)


def _att_in_specs(clamp):
    def spec(j, col):
        return pl.BlockSpec((QB, D_ATT), lambda i: (clamp(i) + j, col))
    return [spec(2, 0), spec(0, 1), spec(1, 1), spec(2, 1), spec(0, 2), spec(1, 2), spec(2, 2)]


def _f_attn(qkv_pad, frow, comm=None):
    s_len = qkv_pad.shape[0] - LEFT_CHUNKS * CHUNK
    nb = s_len // QB

    def body(q_ref, k0, k1, k2, v0, v1, v2, frow_ref, o_ref, bias_sc):
        i = pl.program_id(0)

        @pl.when(i == 0)
        def _():
            _bias_table(frow_ref, bias_sc)

        padmask = _pad_mask(i)
        for h in range(ATT_HEADS):
            sl = slice(h * HEAD_DIM, (h + 1) * HEAD_DIM)
            p, l = _att_scores(q_ref[:, sl], [k0[:, sl], k1[:, sl], k2[:, sl]], bias_sc[h], padmask)
            pb = p.astype(BF16)
            o = (_dot(pb[:, 0:QB], v0[:, sl]) + _dot(pb[:, QB:2 * QB], v1[:, sl])
                 + _dot(pb[:, 2 * QB:3 * QB], v2[:, sl]))
            o_ref[:, sl] = o / l

    return _call(
        body, "f_attn", (nb,),
        _att_in_specs(lambda i: i) + [_full((ATT_HEADS, ROLL_W))],
        [_rows(QB, D_ATT)], [_sds((s_len, D_ATT), F32)],
        [pltpu.VMEM((ATT_HEADS, QB, KB), F32)], (*([qkv_pad] * 7), frow), "arbitrary", comm)


def _f_lru(xg, conv_w, conv_b, wrg, brg, wig, big, lam, tl):
    s_len = xg.shape[0]

    def body(xg_ref, cw_ref, cb_ref, wrg_ref, brg_ref, wig_ref, big_ref, l_ref,
             rec_ref, u_ref, hs_ref, xbuf, a_sc, b_sc, hcar):
        i = pl.program_id(0)

        @pl.when(i == 0)
        def _():
            xbuf[0:8, :] = jnp.zeros((8, D_LRU), F32)
            hcar[...] = jnp.zeros((8, D_LRU), F32)

        xu0 = xg_ref[:, 0:D_LRU]
        xbuf[8:8 + tl, :] = xu0
        u = cb_ref[...] + cw_ref[0:1, :] * xbuf[pl.ds(5, tl), :]
        for j in range(1, 4):
            u = u + cw_ref[j:j + 1, :] * xbuf[pl.ds(5 + j, tl), :]
        xbuf[0:8, :] = xu0[tl - 8:tl, :]
        u_ref[...] = u
        _, _, ig, _, a, mult = _lru_gates(u, wrg_ref[...], brg_ref[...], wig_ref[...], big_ref[...], l_ref[...])
        a_sc[...] = a
        b_sc[...] = mult * (ig * u)

        def grp(g, hprev):
            off = pl.multiple_of(g * 8, 8)
            h8 = _scan8(a_sc[pl.ds(off, 8), :], b_sc[pl.ds(off, 8), :], hprev)
            hs_ref[pl.ds(off, 8), :] = h8
            return h8[7:8, :]

        hcar[0:1, :] = lax.fori_loop(0, tl // 8, grp, hcar[0:1, :])
        rec_ref[...] = hs_ref[...] * _gelu(xg_ref[:, D_LRU:2 * D_LRU])

    vec = _full((1, D_LRU))
    return pl.pallas_call(
        body, name="f_lru", grid=(s_len // tl,),
        in_specs=[_rows(tl, 1024), _full((4, D_LRU)), vec, _full((D_LRU, D_LRU)), vec,
                  _full((D_LRU, D_LRU)), vec, vec],
        out_specs=[_rows(tl, D_LRU)] * 3,
        out_shape=[_sds((s_len, D_LRU), F32)] * 3,
        scratch_shapes=[pltpu.VMEM((tl + 8, D_LRU), F32), pltpu.VMEM((tl, D_LRU), F32),
                        pltpu.VMEM((tl, D_LRU), F32), pltpu.VMEM((8, D_LRU), F32)],
        compiler_params=_cp("arbitrary"))(xg, conv_w, conv_b, wrg, brg, wig, big, lam)


def _f_mem(mem, g_mem, wk, wv):
    def body(mem_ref, g_ref, wk_ref, wv_ref, mn_ref, kx_ref, vx_ref):
        mv = mem_ref[...]
        mn = (mv * _rinv(mv) * g_ref[...]).astype(BF16)
        mn_ref[...] = mn
        kx_ref[...] = _dot(mn, wk_ref[...]).astype(BF16)
        vx_ref[...] = _dot(mn, wv_ref[...]).astype(BF16)

    m = mem.shape[0]
    return pl.pallas_call(
        body, name="f_mem", out_shape=[_sds((m, 1024), BF16)] * 3,
        compiler_params=_cp())(mem, g_mem, wk, wv)


def _xattn_probs(q, k):
    s = _dot_nt(q, k) * X_SCALE
    m = jnp.max(s, axis=-1, keepdims=True)
    p = jnp.exp(s - m)
    return p, jnp.sum(p, axis=-1, keepdims=True)


def _f_mid(x, att, rec, g_oa, g_ol, w_out, g_cross, wq, kx, vx, wo, tm):
    s_len = x.shape[0]
    m_len = kx.shape[0]

    def body(x_ref, att_ref, rec_ref, goa_ref, gol_ref, wout_ref, gc_ref, wq_ref, kx_ref, vx_ref, wo_ref,
             mg_ref, x1_ref, hc_ref, qx_ref, ox_ref, x2_ref):
        av = att_ref[...]
        rv = rec_ref[...]
        mg_ref[:, 0:D_ATT] = (av * _rinv(av) * goa_ref[...]).astype(BF16)
        mg_ref[:, D_ATT:1024] = (rv * _rinv(rv) * gol_ref[...]).astype(BF16)
        x1 = x_ref[...] + _dot(mg_ref[...], wout_ref[...])
        x1_ref[...] = x1
        hc = (x1 * _rinv(x1) * gc_ref[...]).astype(BF16)
        hc_ref[...] = hc
        qx_ref[...] = _dot(hc, wq_ref[...]).astype(BF16)
        for h in range(X_HEADS):
            sl = slice(h * X_HEAD_DIM, (h + 1) * X_HEAD_DIM)
            p, l = _xattn_probs(qx_ref[:, sl], kx_ref[:, sl])
            ox_ref[:, sl] = (_dot(p.astype(BF16), vx_ref[:, sl]) / l).astype(BF16)
        x2_ref[...] = x1 + _dot(ox_ref[...], wo_ref[...])

    sq = _full((1024, 1024))
    return pl.pallas_call(
        body, name="f_mid", grid=(s_len // tm,),
        in_specs=[_rows(tm, 1024), _rows(tm, 512), _rows(tm, 512), _full((1, 512)), _full((1, 512)), sq,
                  _full((1, 1024)), sq, _full((m_len, 1024)), _full((m_len, 1024)), sq],
        out_specs=[_rows(tm, 1024)] * 6,
        out_shape=[_sds((s_len, 1024), BF16), _sds((s_len, 1024), F32), _sds((s_len, 1024), BF16),
                   _sds((s_len, 1024), BF16), _sds((s_len, 1024), BF16), _sds((s_len, 1024), F32)],
        compiler_params=_cp("parallel"))(x, att, rec, g_oa, g_ol, w_out, g_cross, wq, kx, vx, wo)


def _load_weights_once(pairs):
    @pl.when(pl.program_id(0) == 0)
    def _():
        for hbm, vmem in pairs:
            pltpu.sync_copy(hbm, vmem)


def _sh_rows(tm, n):
    return pl.BlockSpec((N_SHARD, tm, n), lambda i: (0, i, 0))


def _f_ffn(x2, tgt, g_ffn, g_final, wg, wu, wd, tm):
    s_len = x2.shape[0]

    def body(x2_ref, t_ref, gf_ref, gfin_ref, wg_hbm, wu_hbm, wd_hbm,
             hf_ref, g_ref, u_ref, a_ref, dx3_ref, loss_ref, dgfin_ref, wg_ref, wu_ref, wd_ref):
        _load_weights_once([(wg_hbm, wg_ref), (wu_hbm, wu_ref), (wd_hbm, wd_ref)])

        @pl.when(pl.program_id(0) == 0)
        def _():
            loss_ref[...] = jnp.zeros_like(loss_ref)
            dgfin_ref[...] = jnp.zeros_like(dgfin_ref)

        x2v = x2_ref[...]
        hf = (x2v * _rinv(x2v) * gf_ref[...]).astype(BF16)
        hf_ref[...] = hf
        x3 = x2v
        for s in range(N_SHARD):
            gv = _dot(hf, wg_ref[s])
            uv = _dot(hf, wu_ref[s])
            av = (gv * jax.nn.sigmoid(gv) * uv).astype(BF16)
            g_ref[s] = gv.astype(BF16)
            u_ref[s] = uv.astype(BF16)
            a_ref[s] = av
            x3 = x3 + _dot(av, wd_ref[s])
        r3 = _rinv(x3)
        yh = x3 * r3
        gfin = gfin_ref[...]
        err = yh * gfin - t_ref[...]
        loss_ref[...] += jnp.full((1, 128), 0.5 / D_MODEL, F32) * jnp.sum(err * err)
        dy = err * (1.0 / D_MODEL)
        dgfin_ref[...] += jnp.sum(dy * yh, axis=0, keepdims=True)
        dyh = dy * gfin
        dx3_ref[...] = r3 * (dyh - yh * jnp.mean(dyh * yh, axis=-1, keepdims=True))

    vec = _full((1, 1024))
    return pl.pallas_call(
        body, name="f_ffn", grid=(s_len // tm,),
        in_specs=[_rows(tm, 1024), _rows(tm, 1024), vec, vec, _any(), _any(), _any()],
        out_specs=[_rows(tm, 1024), _sh_rows(tm, FF_SH), _sh_rows(tm, FF_SH), _sh_rows(tm, FF_SH),
                   _rows(tm, 1024), _full((1, 128)), vec],
        out_shape=[_sds((s_len, 1024), BF16)] + [_sds((N_SHARD, s_len, FF_SH), BF16)] * 3
                  + [_sds((s_len, 1024), F32), _sds((1, 128), F32), _sds((1, 1024), F32)],
        scratch_shapes=[pltpu.VMEM((N_SHARD, 1024, FF_SH), BF16), pltpu.VMEM((N_SHARD, 1024, FF_SH), BF16),
                        pltpu.VMEM((N_SHARD, FF_SH, 1024), BF16)],
        compiler_params=_cp("arbitrary"))(x2, tgt, g_ffn, g_final, wg, wu, wd)


def _b_ffn(dx3, x2, gact, uact, g_ffn, wg, wu, wd, tm):
    s_len = x2.shape[0]

    def body(dx3_ref, x2_ref, g_ref, u_ref, gf_ref, wg_hbm, wu_hbm, wd_hbm,
             dg_ref, du_ref, dx2_ref, dgf_ref, wg_ref, wu_ref, wd_ref):
        _load_weights_once([(wg_hbm, wg_ref), (wu_hbm, wu_ref), (wd_hbm, wd_ref)])

        @pl.when(pl.program_id(0) == 0)
        def _():
            dgf_ref[...] = jnp.zeros_like(dgf_ref)

        dx3v = dx3_ref[...]
        dx3b = dx3v.astype(BF16)
        dhf = jnp.zeros(dx3v.shape, F32)
        for s in range(N_SHARD):
            da = _dot_nt(dx3b, wd_ref[s])
            gv = g_ref[s].astype(F32)
            uv = u_ref[s].astype(F32)
            sg = jax.nn.sigmoid(gv)
            dub = (da * gv * sg).astype(BF16)
            dgb = (da * uv * (sg * (1.0 + gv * (1.0 - sg)))).astype(BF16)
            du_ref[s] = dub
            dg_ref[s] = dgb
            dhf = dhf + _dot_nt(dgb, wg_ref[s]) + _dot_nt(dub, wu_ref[s])
        dx, dgf = _rms_bwd(dhf, x2_ref[...], gf_ref[...])
        dx2_ref[...] = dx3v + dx
        dgf_ref[...] += dgf

    vec = _full((1, 1024))
    return pl.pallas_call(
        body, name="b_ffn", grid=(s_len // tm,),
        in_specs=[_rows(tm, 1024), _rows(tm, 1024), _sh_rows(tm, FF_SH), _sh_rows(tm, FF_SH), vec,
                  _any(), _any(), _any()],
        out_specs=[_sh_rows(tm, FF_SH), _sh_rows(tm, FF_SH), _rows(tm, 1024), vec],
        out_shape=[_sds((N_SHARD, s_len, FF_SH), BF16)] * 2 + [_sds((s_len, 1024), F32), _sds((1, 1024), F32)],
        scratch_shapes=[pltpu.VMEM((N_SHARD, 1024, FF_SH), BF16), pltpu.VMEM((N_SHARD, 1024, FF_SH), BF16),
                        pltpu.VMEM((N_SHARD, FF_SH, 1024), BF16)],
        compiler_params=_cp("arbitrary"))(dx3, x2, gact, uact, g_ffn, wg, wu, wd)


def _b_mid(dx2, qx, x1, att, rec, kx, vx, wo, wq, w_out, g_cross, g_oa, g_ol, tm, comm=None):
    s_len = x1.shape[0]
    m_len = kx.shape[0]

    def body(dx2_ref, qx_ref, x1_ref, att_ref, rec_ref, kx_ref, vx_ref, wo_ref, wq_ref, wout_ref,
             gc_ref, goa_ref, gol_ref,
             dqx_ref, dx1_ref, datt_ref, drec_ref, dkx_ref, dvx_ref, dgc_ref, dgoa_ref, dgol_ref):
        @pl.when(pl.program_id(0) == 0)
        def _():
            for r in (dkx_ref, dvx_ref, dgc_ref, dgoa_ref, dgol_ref):
                r[...] = jnp.zeros_like(r)

        dx2v = dx2_ref[...]
        dox = _dot_nt(dx2v.astype(BF16), wo_ref[...])
        for h in range(X_HEADS):
            sl = slice(h * X_HEAD_DIM, (h + 1) * X_HEAD_DIM)
            q = qx_ref[:, sl]
            p, l = _xattn_probs(q, kx_ref[:, sl])
            pn = p / l
            dob = dox[:, sl].astype(BF16)
            dp = _dot_nt(dob, vx_ref[:, sl])
            dvx_ref[:, sl] += _dot_tn(pn.astype(BF16), dob)
            ds = pn * (dp - jnp.sum(dp * pn, axis=-1, keepdims=True))
            dsb = (ds * X_SCALE).astype(BF16)
            dqx_ref[:, sl] = _dot(dsb, kx_ref[:, sl]).astype(BF16)
            dkx_ref[:, sl] += _dot_tn(dsb, q)
        dhc = _dot_nt(dqx_ref[...], wq_ref[...])
        dx, dgc = _rms_bwd(dhc, x1_ref[...], gc_ref[...])
        dx1 = dx2v + dx
        dx1_ref[...] = dx1
        dgc_ref[...] += dgc
        dmg = _dot_nt(dx1.astype(BF16), wout_ref[...])
        da, dgoa = _rms_bwd(dmg[:, 0:D_ATT], att_ref[...], goa_ref[...])
        datt_ref[...] = da
        dgoa_ref[...] += dgoa
        dr, dgol = _rms_bwd(dmg[:, D_ATT:1024], rec_ref[...], gol_ref[...])
        drec_ref[...] = dr
        dgol_ref[...] += dgol

    sq = _full((1024, 1024))
    mk = _full((m_len, 1024))
    return _call(
        body, "b_mid", (s_len // tm,),
        [_rows(tm, 1024), _rows(tm, 1024), _rows(tm, 1024), _rows(tm, 512), _rows(tm, 512), mk, mk,
         sq, sq, sq, _full((1, 1024)), _full((1, 512)), _full((1, 512))],
        [_rows(tm, 1024), _rows(tm, 1024), _rows(tm, 512), _rows(tm, 512), mk, mk,
         _full((1, 1024)), _full((1, 512)), _full((1, 512))],
        [_sds((s_len, 1024), BF16), _sds((s_len, 1024), F32), _sds((s_len, 512), F32),
         _sds((s_len, 512), F32), _sds((m_len, 1024), F32), _sds((m_len, 1024), F32),
         _sds((1, 1024), F32), _sds((1, 512), F32), _sds((1, 512), F32)],
        [], (dx2, qx, x1, att, rec, kx, vx, wo, wq, w_out, g_cross, g_oa, g_ol), "arbitrary", comm)


def _b_mem(dkx, dvx, mem, mn, g_mem, wk, wv):
    def body(dkx_ref, dvx_ref, mem_ref, mn_ref, g_ref, wk_ref, wv_ref, dwk_ref, dwv_ref, dgm_ref,
             dwkb_ref, dwvb_ref):
        dkb = dkx_ref[...].astype(BF16)
        dvb = dvx_ref[...].astype(BF16)
        dwk = _dot_tn(mn_ref[...], dkb)
        dwv = _dot_tn(mn_ref[...], dvb)
        dwk_ref[...] = dwk
        dwv_ref[...] = dwv
        dwkb_ref[...] = dwk.astype(BF16)
        dwvb_ref[...] = dwv.astype(BF16)
        dmn = _dot_nt(dkb, wk_ref[...]) + _dot_nt(dvb, wv_ref[...])
        mv = mem_ref[...]
        dgm_ref[...] = jnp.sum(dmn * (mv * _rinv(mv)), axis=0, keepdims=True)

    return pl.pallas_call(
        body, name="b_mem",
        out_shape=[_sds((1024, 1024), F32), _sds((1024, 1024), F32), _sds((1, 1024), F32),
                   _sds((1024, 1024), BF16), _sds((1024, 1024), BF16)],
        compiler_params=_cp())(dkx, dvx, mem, mn, g_mem, wk, wv)


def _b_lru(drec, hs, u, xg, conv_w, wrg, brg, wig, big, lam, tl, comm=None):
    s_len = xg.shape[0]
    nt = s_len // tl

    def body(drec_ref, hs_ref, hsp_ref, u_ref, xg_ref, cw_ref, wrg_ref, brg_ref, wig_ref, big_ref, l_ref,
             dxg_ref, dwrg_ref, dwig_ref, dbrg_ref, dbig_ref, dlam_ref, dcw_ref, dcb_ref,
             hbuf, abuf, dubuf, c_sc, d_sc, lam_sc, lcar):
        i = pl.program_id(0)
        tt = nt - 1 - i

        @pl.when(i == 0)
        def _():
            for r in (dwrg_ref, dwig_ref, dbrg_ref, dbig_ref, dlam_ref, dcw_ref, dcb_ref):
                r[...] = jnp.zeros_like(r)
            abuf[tl:tl + 8, :] = jnp.zeros((8, D_LRU), F32)
            dubuf[tl:tl + 8, :] = jnp.zeros((8, D_LRU), F32)
            lcar[...] = jnp.zeros((8, D_LRU), F32)

        xu0 = xg_ref[:, 0:D_LRU]
        hsv = hs_ref[...]
        uv = u_ref[...]
        hbuf[8:8 + tl, :] = hsv
        hbuf[0:8, :] = jnp.where(tt > 0, hsp_ref[...], 0.0)
        hshift = hbuf[pl.ds(7, tl), :]
        wrg_v = wrg_ref[...]
        wig_v = wig_ref[...]
        lamv = l_ref[...]
        ub, r, ig, sp, a, mult = _lru_gates(uv, wrg_v, brg_ref[...], wig_v, big_ref[...], lamv)
        abuf[0:tl, :] = a
        c_sc[...] = abuf[pl.ds(1, tl), :]
        gel, dgel = _gelu_and_grad(xg_ref[:, D_LRU:2 * D_LRU])
        drv = drec_ref[...]
        d_sc[...] = drv * gel
        dxg_ref[:, D_LRU:2 * D_LRU] = (drv * hsv * dgel).astype(BF16)

        def grp(k, lnext):
            off = pl.multiple_of((tl // 8 - 1 - k) * 8, 8)
            l8 = _rscan8(c_sc[pl.ds(off, 8), :], d_sc[pl.ds(off, 8), :], lnext)
            lam_sc[pl.ds(off, 8), :] = l8
            return l8[0:1, :]

        lcar[0:1, :] = lax.fori_loop(0, tl // 8, grp, lcar[0:1, :])
        abuf[tl:tl + 8, :] = a[0:8, :]
        db = lam_sc[...]
        da = db * hshift
        dmult = db * (ig * uv)
        dig = db * mult * uv
        du = db * mult * ig
        dla = da * a - dmult * (a * a) / mult
        dlam_ref[...] += jnp.sum(dla * (-LRU_C) * r, axis=0, keepdims=True)
        dzr = dla * (-LRU_C * sp) * r * (1.0 - r)
        dzi = dig * ig * (1.0 - ig)
        dzrb = dzr.astype(BF16)
        dzib = dzi.astype(BF16)
        du = du + _dot_nt(dzrb, wrg_v) + _dot_nt(dzib, wig_v)
        dwrg_ref[...] += _dot_tn(ub, dzrb)
        dwig_ref[...] += _dot_tn(ub, dzib)
        dbrg_ref[...] += jnp.sum(dzr, axis=0, keepdims=True)
        dbig_ref[...] += jnp.sum(dzi, axis=0, keepdims=True)
        dcb_ref[...] += jnp.sum(du, axis=0, keepdims=True)
        dubuf[0:tl, :] = du
        dxu0 = jnp.zeros((tl, D_LRU), F32)
        for j in range(4):
            dsh = dubuf[pl.ds(3 - j, tl), :]
            dxu0 = dxu0 + cw_ref[j:j + 1, :] * dsh
            dcw_ref[j:j + 1, :] += jnp.sum(xu0 * dsh, axis=0, keepdims=True)
        dubuf[tl:tl + 8, :] = du[0:8, :]
        dxg_ref[:, 0:D_LRU] = dxu0.astype(BF16)

        @pl.when(i == nt - 1)
        def _():
            dlam_ref[...] = dlam_ref[...] * (-jax.nn.sigmoid(-lamv))

    def rev(n):
        return pl.BlockSpec((tl, n), lambda i: (nt - 1 - i, 0))

    prev8 = pl.BlockSpec((8, D_LRU), lambda i: (jnp.maximum((nt - 1 - i) * (tl // 8) - 1, 0), 0))
    vec = _full((1, D_LRU))
    sq = _full((D_LRU, D_LRU))
    return _call(
        body, "b_lru", (nt,),
        [rev(D_LRU), rev(D_LRU), prev8, rev(D_LRU), rev(1024), _full((4, D_LRU)), sq, vec, sq, vec, vec],
        [rev(1024), sq, sq, vec, vec, vec, _full((4, D_LRU)), vec],
        [_sds((s_len, 1024), BF16), _sds((D_LRU, D_LRU), F32), _sds((D_LRU, D_LRU), F32),
         _sds((1, D_LRU), F32), _sds((1, D_LRU), F32), _sds((1, D_LRU), F32),
         _sds((4, D_LRU), F32), _sds((1, D_LRU), F32)],
        [pltpu.VMEM((tl + 8, D_LRU), F32)] * 3 + [pltpu.VMEM((tl, D_LRU), F32)] * 3
        + [pltpu.VMEM((8, D_LRU), F32)],
        (drec, hs, hs, u, xg, conv_w, wrg, brg, wig, big, lam), "arbitrary", comm)


def _b_attn(qkv_pad, datt, frow):
    s_len = datt.shape[0]
    nb = s_len // QB

    def body(q_ref, k0, k1, k2, v0, v1, v2, do_ref, frow_ref, dq_ref, dkv_ref, dfrow_ref,
             bias_sc, dt_sc, acc_sc):
        t = pl.program_id(0)

        @pl.when(t == 0)
        def _():
            _bias_table(frow_ref, bias_sc)
            dt_sc[...] = jnp.zeros_like(dt_sc)
            acc_sc[...] = jnp.zeros_like(acc_sc)

        @pl.when(t < nb)
        def _():
            padmask = _pad_mask(t)
            for h in range(ATT_HEADS):
                sl = slice(h * HEAD_DIM, (h + 1) * HEAD_DIM)
                q = q_ref[:, sl]
                ks = [k0[:, sl], k1[:, sl], k2[:, sl]]
                vs = [v0[:, sl], v1[:, sl], v2[:, sl]]
                p, l = _att_scores(q, ks, bias_sc[h], padmask)
                pn = p / l
                dob = do_ref[:, sl].astype(BF16)
                dp = jnp.concatenate([_dot_nt(dob, v) for v in vs], axis=1)
                ds = pn * (dp - jnp.sum(dp * pn, axis=-1, keepdims=True))
                dt_sc[h] += ds
                dsb = (ds * ATT_SCALE).astype(BF16)
                pnb = pn.astype(BF16)
                dq = _dot(dsb[:, 0:QB], ks[0])
                for j in (1, 2):
                    dq = dq + _dot(dsb[:, j * QB:(j + 1) * QB], ks[j])
                dq_ref[:, sl] = dq.astype(BF16)
                for j in range(3):
                    slot = (t + 1 + j) % 3
                    cs = slice(j * QB, (j + 1) * QB)
                    acc_sc[slot, :, sl] += _dot_tn(dsb[:, cs], q)
                    acc_sc[slot, :, D_ATT + h * HEAD_DIM:D_ATT + (h + 1) * HEAD_DIM] += _dot_tn(pnb[:, cs], dob)

        done = (t + 1) % 3

        @pl.when(t >= 2)
        def _():
            dkv_ref[...] = acc_sc[done].astype(BF16)

        acc_sc[done] = jnp.zeros((QB, 2 * D_ATT), F32)

        @pl.when(t == nb + 1)
        def _():
            row = lax.broadcasted_iota(jnp.int32, (8, ROLL_W), 0)
            pad = jnp.zeros((8, ROLL_W - KB), F32)
            for h in range(ATT_HEADS):
                acc8 = jnp.concatenate([dt_sc[h, 0:8, :], pad], axis=1)
                for a1 in range(1, QB // 8):
                    blk = jnp.concatenate([dt_sc[h, 8 * a1:8 * a1 + 8, :], pad], axis=1)
                    acc8 = acc8 + pltpu.roll(blk, ROLL_W - 8 * a1, 1)
                for k in range(3):
                    acc8 = jnp.where(((row >> k) & 1) == 1, pltpu.roll(acc8, ROLL_W - (1 << k), 1), acc8)
                dfrow_ref[h:h + 1, :] = jnp.sum(acc8, axis=0, keepdims=True)

    clamp = lambda t: jnp.minimum(t, nb - 1)
    return pl.pallas_call(
        body, name="b_attn", grid=(nb + 2,),
        in_specs=_att_in_specs(clamp) + [pl.BlockSpec((QB, D_ATT), lambda t: (clamp(t), 0)),
                                         _full((ATT_HEADS, ROLL_W))],
        out_specs=[pl.BlockSpec((QB, D_ATT), lambda t: (clamp(t), 0)),
                   pl.BlockSpec((QB, 2 * D_ATT), lambda t: (jnp.maximum(t - 2, 0), 0)),
                   _full((ATT_HEADS, ROLL_W))],
        out_shape=[_sds((s_len, D_ATT), BF16), _sds((s_len, 2 * D_ATT), BF16), _sds((ATT_HEADS, ROLL_W), F32)],
        scratch_shapes=[pltpu.VMEM((ATT_HEADS, QB, KB), F32), pltpu.VMEM((ATT_HEADS, QB, KB), F32),
                        pltpu.VMEM((3, QB, 2 * D_ATT), F32)],
        compiler_params=_cp("arbitrary"))(*([qkv_pad] * 7), datt, frow)


def _b_inproj(dq, dkv, dxg, h, x, dx1, g_mix, w_in_g, tm):
    s_len = x.shape[0]

    def body(dq_ref, dkv_ref, dxg_ref, h_ref, x_ref, dx1_ref, g_ref, w_ref, gx_ref, dgm_ref, dw_ref):
        @pl.when(pl.program_id(0) == 0)
        def _():
            dgm_ref[...] = jnp.zeros_like(dgm_ref)
            dw_ref[...] = jnp.zeros_like(dw_ref)

        dproj = jnp.concatenate([dq_ref[...], dkv_ref[...], dxg_ref[...]], axis=1)
        hv = h_ref[...]
        dh = jnp.zeros((tm, 1024), F32)
        for s in range(N_SHARD):
            dps = dproj[:, s * IN_SH:(s + 1) * IN_SH]
            dh = dh + _dot_nt(dps, w_ref[s])
            dw_ref[s] += _dot_tn(hv, dps)
        dx, dgm = _rms_bwd(dh, x_ref[...], g_ref[...])
        gx_ref[...] = dx1_ref[...] + dx
        dgm_ref[...] += dgm

    wspec = _full((N_SHARD, 1024, IN_SH))
    return pl.pallas_call(
        body, name="b_inproj", grid=(s_len // tm,),
        in_specs=[_rows(tm, 512), _rows(tm, 1024), _rows(tm, 1024), _rows(tm, 1024), _rows(tm, 1024),
                  _rows(tm, 1024), _full((1, 1024)), wspec],
        out_specs=[_rows(tm, 1024), _full((1, 1024)), wspec],
        out_shape=[_sds((s_len, 1024), F32), _sds((1, 1024), F32), _sds((N_SHARD, 1024, IN_SH), F32)],
        compiler_params=_cp("arbitrary"))(dq, dkv, dxg, h, x, dx1, g_mix, w_in_g)


def _mm_tn(xa, ya, name, ts):
    s_len, k = xa.shape
    n = ya.shape[1]

    steps = s_len // ts

    def body(x_ref, y_ref, o_ref, ob_ref):
        @pl.when(pl.program_id(0) == 0)
        def _():
            o_ref[...] = jnp.zeros_like(o_ref)
        o_ref[...] += _dot_tn(x_ref[...].astype(BF16), y_ref[...].astype(BF16))

        @pl.when(pl.program_id(0) == steps - 1)
        def _():
            ob_ref[...] = o_ref[...].astype(BF16)

    return pl.pallas_call(
        body, name=name, grid=(steps,), in_specs=[_rows(ts, k), _rows(ts, n)],
        out_specs=[_full((k, n))] * 2, out_shape=[_sds((k, n), F32), _sds((k, n), BF16)],
        compiler_params=_cp("arbitrary"))(xa, ya)


def _mm_tn_ysh(xa, y4, name, ts):
    s_len, k = xa.shape
    n = y4.shape[2]

    steps = s_len // ts

    def body(x_ref, y_ref, o_ref, ob_ref):
        @pl.when(pl.program_id(0) == 0)
        def _():
            o_ref[...] = jnp.zeros_like(o_ref)
        xb = x_ref[...].astype(BF16)
        for s in range(N_SHARD):
            o_ref[s] += _dot_tn(xb, y_ref[s])

        @pl.when(pl.program_id(0) == steps - 1)
        def _():
            ob_ref[...] = o_ref[...].astype(BF16)

    return pl.pallas_call(
        body, name=name, grid=(steps,), in_specs=[_rows(ts, k), _sh_rows(ts, n)],
        out_specs=[_full((N_SHARD, k, n))] * 2,
        out_shape=[_sds((N_SHARD, k, n), F32), _sds((N_SHARD, k, n), BF16)],
        compiler_params=_cp("arbitrary"))(xa, y4)


def _mm_tn_xsh(x4, ya, name, ts):
    s_len, n = ya.shape
    k = x4.shape[2]

    steps = s_len // ts

    def body(x_ref, y_ref, o_ref, ob_ref):
        @pl.when(pl.program_id(0) == 0)
        def _():
            o_ref[...] = jnp.zeros_like(o_ref)
        yb = y_ref[...].astype(BF16)
        for s in range(N_SHARD):
            o_ref[s] += _dot_tn(x_ref[s], yb)

        @pl.when(pl.program_id(0) == steps - 1)
        def _():
            ob_ref[...] = o_ref[...].astype(BF16)

    return pl.pallas_call(
        body, name=name, grid=(steps,), in_specs=[_sh_rows(ts, k), _rows(ts, n)],
        out_specs=[_full((N_SHARD, k, n))] * 2,
        out_shape=[_sds((N_SHARD, k, n), F32), _sds((N_SHARD, k, n), BF16)],
        compiler_params=_cp("arbitrary"))(x4, ya)


def _frow_from_rel_bias(rb):
    hi = jnp.broadcast_to(rb[:, 256:257], (ATT_HEADS, 385))
    mid = rb[:, 1:256][:, ::-1]
    lo = jnp.broadcast_to(rb[:, 0:1], (ATT_HEADS, 128))
    wrap = jnp.broadcast_to(rb[:, 256:257], (ATT_HEADS, ROLL_W - KB))
    return jnp.concatenate([hi, mid, lo, wrap], axis=1)


def _rel_bias_grad_from_dfrow(df):
    g256 = jnp.sum(df[:, 0:385], axis=1, keepdims=True) + jnp.sum(df[:, KB:ROLL_W], axis=1, keepdims=True)
    mid = df[:, 385:640][:, ::-1]
    g0 = jnp.sum(df[:, 640:KB], axis=1, keepdims=True)
    return jnp.concatenate([g0, mid, g256], axis=1)


def _block_diag(w):
    eye = jnp.eye(8, dtype=w.dtype)
    return (w[:, :, None, :] * eye[:, None, :, None]).reshape(D_LRU, D_LRU)


def _block_diag_extract(dense):
    eye = jnp.eye(8, dtype=dense.dtype)
    return jnp.sum(dense.reshape(8, 64, 8, 64) * eye[:, None, :, None], axis=2)


MID = ['w_out', 'wq_c', 'wk_c', 'wv_c', 'wo_c']
FFN = ['w_gate', 'w_up', 'w_down']


def _local_step(x, mem, tgt, p, gw, shards=None):
    s_len = x.shape[0]
    tm = min(256, s_len)
    tl = min(512, s_len)
    frow = _frow_from_rel_bias(p['rel_bias'])
    wrg = _block_diag(p['w_rg']).astype(BF16)
    wig = _block_diag(p['w_ig']).astype(BF16)
    gw = dict(gw)

    ag_mid = None if shards is None else ("ag", [shards[n] for n in MID])
    h, qkv, xg, *got = _f_inproj(x, p['g_mix'], gw['w_in'], tm, ag_mid)
    gw.update(zip(MID, got))
    qkv_pad = jnp.pad(qkv, ((LEFT_CHUNKS * CHUNK, 0), (0, 0)))
    ag_ffn = None if shards is None else ("ag", [shards[n] for n in FFN])
    att, *got = _f_attn(qkv_pad, frow, ag_ffn)
    gw.update(zip(FFN, got))
    w_out = gw['w_out'].reshape(1024, 1024)
    wq = gw['wq_c'].reshape(1024, 1024)
    wk = gw['wk_c'].reshape(1024, 1024)
    wv = gw['wv_c'].reshape(1024, 1024)
    wo = gw['wo_c'].reshape(1024, 1024)
    rec, u, hs = _f_lru(xg, p['conv_w'], p['conv_b'], wrg, p['b_rg'], wig, p['b_ig'], p['lru_L'], tl)
    mn, kx, vx = _f_mem(mem, p['g_mem'], wk, wv)
    mg, x1, hc, qx, ox, x2 = _f_mid(x, att, rec, p['g_out_attn'], p['g_out_lru'], w_out, p['g_cross'],
                                    wq, kx, vx, wo, tm)
    hf, gact, uact, aact, dx3, loss, dg_final = _f_ffn(x2, tgt, p['g_ffn'], p['g_final'],
                                                       gw['w_gate'], gw['w_up'], gw['w_down'], tm)

    ts = min(512, s_len)
    big, bigb, recv = {}, {}, {}
    dgact, duact, dx2, dg_ffn = _b_ffn(dx3, x2, gact, uact, p['g_ffn'], gw['w_gate'], gw['w_up'], gw['w_down'], tm)
    big['w_gate'], bigb['w_gate'] = _mm_tn_ysh(hf, dgact, "dw_gate", ts)
    big['w_up'], bigb['w_up'] = _mm_tn_ysh(hf, duact, "dw_up", ts)
    big['w_down'], bigb['w_down'] = _mm_tn_xsh(aact, dx3, "dw_down", ts)

    rs_ffn = None if shards is None else ("rs", [bigb[n] for n in FFN])
    dqx, dx1, datt, drec, dkx, dvx, dg_cross, dg_oa, dg_ol, *got = _b_mid(
        dx2, qx, x1, att, rec, kx, vx, wo, wq, w_out, p['g_cross'], p['g_out_attn'], p['g_out_lru'], tm, rs_ffn)
    recv.update(zip(FFN, got))
    dwk, dwv, dg_mem, dwkb, dwvb = _b_mem(dkx, dvx, mem, mn, p['g_mem'], wk, wv)
    big['wk_c'], bigb['wk_c'] = dwk, dwkb
    big['wv_c'], bigb['wv_c'] = dwv, dwvb
    big['w_out'], bigb['w_out'] = _mm_tn(mg, dx1, "dw_out", ts)
    big['wq_c'], bigb['wq_c'] = _mm_tn(hc, dqx, "dw_q", ts)
    big['wo_c'], bigb['wo_c'] = _mm_tn(ox, dx2, "dw_o", ts)
    for n in MID:
        big[n] = big[n].reshape(N_SHARD, 256, 1024)
        bigb[n] = bigb[n].reshape(N_SHARD, 256, 1024)

    rs_mid = None if shards is None else ("rs", [bigb[n] for n in MID])
    dxg, dwrg, dwig, dbrg, dbig, dlam, dcw, dcb, *got = _b_lru(
        drec, hs, u, xg, p['conv_w'], wrg, p['b_rg'], wig, p['b_ig'], p['lru_L'], tl, rs_mid)
    recv.update(zip(MID, got))
    dq, dkv, dfrow = _b_attn(qkv_pad, datt, frow)
    grad_x, dg_mix, big['w_in'] = _b_inproj(dq, dkv, dxg, h, x, dx1, p['g_mix'], gw['w_in'], tm)
    small = {
        'g_mix': dg_mix, 'rel_bias': _rel_bias_grad_from_dfrow(dfrow), 'conv_w': dcw, 'conv_b': dcb,
        'w_rg': _block_diag_extract(dwrg), 'b_rg': dbrg, 'w_ig': _block_diag_extract(dwig), 'b_ig': dbig,
        'lru_L': dlam,
        'g_out_attn': dg_oa, 'g_out_lru': dg_ol, 'g_cross': dg_cross, 'g_mem': dg_mem, 'g_ffn': dg_ffn,
        'g_final': dg_final,
    }
    return jnp.sum(loss[0, 0:1]), grad_x, small, big, recv


def _cast_shards(ws):
    def body(*refs):
        n = len(refs) // 2
        for src, dst in zip(refs[:n], refs[n:]):
            dst[...] = src[...].astype(BF16)

    return pl.pallas_call(body, name="cast_shards", out_shape=[_sds(w.shape, BF16) for w in ws],
                          compiler_params=_cp())(*ws)


def _cast_slots(g4, name):
    _, r, c = g4.shape

    def body(in_ref, o_ref):
        o_ref[...] = in_ref[...].astype(BF16)

    spec = pl.BlockSpec((1, r, c), lambda i: (i, 0, 0))
    return pl.pallas_call(body, name=name, grid=(N_SHARD,), in_specs=[spec], out_specs=spec,
                          out_shape=_sds(g4.shape, BF16), compiler_params=_cp("parallel"))(g4)


def _sum_parts(own4, recv3, chip, name):
    _, r, c = own4.shape
    tr = r // 4

    def body(chip_ref, own_ref, rc_ref, o_ref):
        o_ref[...] = ((own_ref[0] + rc_ref[0].astype(F32)) + rc_ref[1].astype(F32)) + rc_ref[2].astype(F32)

    grid_spec = pltpu.PrefetchScalarGridSpec(
        num_scalar_prefetch=1, grid=(4,),
        in_specs=[pl.BlockSpec((1, tr, c), lambda i, ch: (ch[0], i, 0)),
                  pl.BlockSpec((3, tr, c), lambda i, ch: (0, i, 0))],
        out_specs=pl.BlockSpec((tr, c), lambda i, ch: (i, 0)))
    return pl.pallas_call(body, name=name, grid_spec=grid_spec, out_shape=_sds((r, c), F32),
                          compiler_params=_cp("parallel"))(chip, own4, recv3)


def _swap_sibling(parts):
    n = len(parts)

    def body(*refs):
        ins, outs = refs[:n], refs[n:2 * n]
        send_sems, recv_sems = refs[2 * n:]
        x, y, c = _mesh_pos()
        copies = [pltpu.make_async_remote_copy(
            src_ref=ins[k], dst_ref=outs[k], send_sem=send_sems.at[k], recv_sem=recv_sems.at[k],
            device_id=(x, y, 1 - c), device_id_type=MESH_ID) for k in range(n)]
        for cp in copies:
            cp.start()
        for cp in copies:
            cp.wait()

    return pl.pallas_call(
        body, name="swap_sibling",
        in_specs=[_any()] * n, out_specs=[_any()] * n,
        out_shape=[_sds(p.shape, p.dtype) for p in parts],
        scratch_shapes=[pltpu.SemaphoreType.DMA((n,)), pltpu.SemaphoreType.DMA((n,))],
        compiler_params=_cp())(*parts)


def _adamw_math(w, g, m, v):
    m = ADAM_B1 * m + (1.0 - ADAM_B1) * g
    v = ADAM_B2 * v + (1.0 - ADAM_B2) * (g * g)
    m_hat = m / (1.0 - ADAM_B1 ** ADAM_STEP)
    v_hat = v / (1.0 - ADAM_B2 ** ADAM_STEP)
    delta = -ADAM_LR * (m_hat / (jnp.sqrt(v_hat) + ADAM_EPS) + ADAM_WD * w)
    return delta, m, v


def _final_adamw(pa, pb, w, m, v, name):
    r, c = w.shape
    tr = r // 4

    def body(pa_ref, pb_ref, w_ref, m_ref, v_ref, g_ref, d_ref, nm_ref, nv_ref):
        g = pa_ref[...] + pb_ref[...]
        g_ref[...] = g
        d_ref[...], nm_ref[...], nv_ref[...] = _adamw_math(w_ref[...], g, m_ref[...], v_ref[...])

    return pl.pallas_call(
        body, name=name, grid=(4,), in_specs=[_rows(tr, c)] * 5, out_specs=[_rows(tr, c)] * 4,
        out_shape=[_sds((r, c), F32)] * 4, compiler_params=_cp("parallel"))(pa, pb, w, m, v)


def _adamw_small(w, g, m, v):
    def body(w_ref, g_ref, m_ref, v_ref, d_ref, nm_ref, nv_ref):
        d_ref[...], nm_ref[...], nv_ref[...] = _adamw_math(w_ref[...], g_ref[...], m_ref[...], v_ref[...])

    return pl.pallas_call(body, name="adamw_conv_w", out_shape=[_sds(w.shape, F32)] * 3,
                          compiler_params=_cp())(w, g, m, v)


def _ar_small(gp, wp, mp, vp):
    rows = gp.shape[0]

    def body(g_ref, w_ref, m_ref, v_ref, go_ref, d_ref, nm_ref, nv_ref, buf, send_sems, recv_sems):
        x, y, c = _mesh_pos()
        me = 4 * x + 2 * y + c

        def peer(k):
            px = 1 - x if k & 4 else x
            py = 1 - y if k & 2 else y
            pc = 1 - c if k & 1 else c
            return px, py, pc

        def remote(k, slot):
            return pltpu.make_async_remote_copy(
                src_ref=g_ref, dst_ref=buf.at[slot], send_sem=send_sems.at[k - 1], recv_sem=recv_sems.at[k - 1],
                device_id=peer(k), device_id_type=MESH_ID)

        for k in range(1, 8):
            remote(k, me).start()
        buf[me] = g_ref[...]
        for k in range(1, 8):
            px, py, pc = peer(k)
            remote(k, 4 * px + 2 * py + pc).wait_recv()
        for k in range(1, 8):
            remote(k, me).wait_send()
        tot = buf[0]
        for k in range(1, 8):
            tot = tot + buf[k]
        go_ref[...] = tot
        d_ref[...], nm_ref[...], nv_ref[...] = _adamw_math(w_ref[...], tot, m_ref[...], v_ref[...])

    return pl.pallas_call(
        body, name="ar_small", out_shape=[_sds((rows, 128), F32)] * 4,
        scratch_shapes=[pltpu.VMEM((8, rows, 128), F32), pltpu.SemaphoreType.DMA((7,)),
                        pltpu.SemaphoreType.DMA((7,))],
        compiler_params=_cp())(gp, wp, mp, vp)


def _pack(parts):
    flat = jnp.concatenate([parts[n].reshape(-1) for n in SMALL])
    rows = -(-flat.shape[0] // 1024) * 8
    return jnp.pad(flat, (0, rows * 128 - flat.shape[0])).reshape(rows, 128)


def _unpack(pack):
    flat = pack.reshape(-1)
    out, off = {}, 0
    for n in SMALL:
        size = math.prod(SMALL_SHAPES[n])
        out[n] = flat[off:off + size].reshape(SMALL_SHAPES[n])
        off += size
    return out


INPUT_NAMES = (['x', 'mem'] + WEIGHTS + ['loss_target'] + ['m_' + n for n in WEIGHTS] + ['v_' + n for n in WEIGHTS])


def kernel(x, mem, g_mix, w_in, rel_bias, conv_w, conv_b, w_rg, b_rg, w_ig, b_ig, lru_L, g_out_attn, g_out_lru, w_out, g_cross, g_mem, wq_c, wk_c, wv_c, wo_c, g_ffn, w_gate, w_up, w_down, g_final, loss_target, m_g_mix, m_w_in, m_rel_bias, m_conv_w, m_conv_b, m_w_rg, m_b_rg, m_w_ig, m_b_ig, m_lru_L, m_g_out_attn, m_g_out_lru, m_w_out, m_g_cross, m_g_mem, m_wq_c, m_wk_c, m_wv_c, m_wo_c, m_g_ffn, m_w_gate, m_w_up, m_w_down, m_g_final, v_g_mix, v_w_in, v_rel_bias, v_conv_w, v_conv_b, v_w_rg, v_b_rg, v_w_ig, v_b_ig, v_lru_L, v_g_out_attn, v_g_out_lru, v_w_out, v_g_cross, v_g_mem, v_wq_c, v_wk_c, v_wv_c, v_wo_c, v_g_ffn, v_w_gate, v_w_up, v_w_down, v_g_final):
    a = dict(zip(INPUT_NAMES, (x, mem, g_mix, w_in, rel_bias, conv_w, conv_b, w_rg, b_rg, w_ig, b_ig, lru_L, g_out_attn, g_out_lru, w_out, g_cross, g_mem, wq_c, wk_c, wv_c, wo_c, g_ffn, w_gate, w_up, w_down, g_final, loss_target, m_g_mix, m_w_in, m_rel_bias, m_conv_w, m_conv_b, m_w_rg, m_b_rg, m_w_ig, m_b_ig, m_lru_L, m_g_out_attn, m_g_out_lru, m_w_out, m_g_cross, m_g_mem, m_wq_c, m_wk_c, m_wv_c, m_wo_c, m_g_ffn, m_w_gate, m_w_up, m_w_down, m_g_final, v_g_mix, v_w_in, v_rel_bias, v_conv_w, v_conv_b, v_w_rg, v_b_rg, v_w_ig, v_b_ig, v_lru_L, v_g_out_attn, v_g_out_lru, v_w_out, v_g_cross, v_g_mem, v_wq_c, v_wk_c, v_wv_c, v_wo_c, v_g_ffn, v_w_gate, v_w_up, v_w_down, v_g_final)))
    chip = 2 * lax.axis_index("x") + lax.axis_index("y")

    shards = dict(zip(BIG, _cast_shards([a[n][0] for n in BIG])))
    w_in_g, conv_w_g = _comm_only("ag_w_in", ("ag", [shards['w_in'], a['conv_w'][0]]))
    conv_w_full = conv_w_g.transpose(1, 0, 2).reshape(4, D_LRU)

    p = {n: a[n] for n in SMALL}
    p['rel_bias'] = a['rel_bias'][0]
    p['w_rg'] = a['w_rg'][0]
    p['w_ig'] = a['w_ig'][0]
    p['conv_w'] = conv_w_full
    p['g_final'] = a['g_final'][None, :]
    loss_part, grad_x, small, big, recv = _local_step(
        a['x'][0], a['mem'][0], a['loss_target'][0], p, {'w_in': w_in_g}, shards)
    loss = lax.psum(loss_part, ("x", "y", "c"))

    recv['w_in'], = _comm_only("rs_w_in", ("rs", [_cast_slots(big['w_in'], "cast_dw_in")]))
    chip_arr = jnp.reshape(chip, (1,)).astype(jnp.int32)
    part = [_sum_parts(big[n], recv[n], chip_arr, "sum_" + n) for n in BIG]
    sib = _swap_sibling(part)
    out = {}
    for n, pa, pb in zip(BIG, part, sib):
        out[n] = _final_adamw(pa, pb, a[n][0], a['m_' + n][0], a['v_' + n][0], "adamw_" + n)

    zeros_cw = jnp.zeros((1, 4, D_LRU), F32)
    def packed(prefix):
        d = {n: a[prefix + n] for n in SMALL}
        d['conv_w'] = zeros_cw
        return _pack(d)
    packs = _ar_small(_pack(small), packed(''), packed('m_'), packed('v_'))
    sg, sd, sm, sv = [_unpack(pk) for pk in packs]
    g_cw = lax.dynamic_slice(sg['conv_w'][0], (0, chip * 128), (4, 128))
    d_cw, m_cw, v_cw = _adamw_small(a['conv_w'][0], g_cw, a['m_conv_w'][0], a['v_conv_w'][0])
    sg['conv_w'], sd['conv_w'], sm['conv_w'], sv['conv_w'] = g_cw[None], d_cw[None], m_cw[None], v_cw[None]

    def leaf(i, n):
        if n in BIG:
            return out[n][i][None]
        return (sg, sd, sm, sv)[i][n]

    return (loss, grad_x[None], *[leaf(i, n) for i in range(4) for n in WEIGHTS])
```

```python
import math

import jax
import jax.numpy as jnp
from jax import lax
from jax.experimental import pallas as pl
from jax.experimental.pallas import tpu as pltpu

F32 = jnp.float32
BF16 = jnp.bfloat16

D_MODEL = 1024
D_ATT = 512
D_LRU = 512
HEAD_DIM = 64
ATT_HEADS = 8
CHUNK = 64
LEFT_CHUNKS = 8
X_HEADS = 4
X_HEAD_DIM = 256
N_SHARD = 4
IN_SH = 640
FF_SH = 704
EPS = 1e-6
LRU_C = 8.0
QB = 256
KB = 768
ROLL_W = 1024
NEG = -1e30
ATT_SCALE = HEAD_DIM ** -0.5
X_SCALE = X_HEAD_DIM ** -0.5

ADAM_LR = 0.001
ADAM_B1 = 0.9
ADAM_B2 = 0.999
ADAM_EPS = 1e-08
ADAM_WD = 0.01
ADAM_STEP = 10

VMEM_LIMIT_V7X = 56 * 1024 * 1024
MESH_ID = pl.DeviceIdType.MESH

WEIGHTS = ['g_mix', 'w_in', 'rel_bias', 'conv_w', 'conv_b', 'w_rg', 'b_rg', 'w_ig', 'b_ig', 'lru_L',
           'g_out_attn', 'g_out_lru', 'w_out', 'g_cross', 'g_mem', 'wq_c', 'wk_c', 'wv_c', 'wo_c',
           'g_ffn', 'w_gate', 'w_up', 'w_down', 'g_final']
BIG = ['w_in', 'w_out', 'wq_c', 'wk_c', 'wv_c', 'wo_c', 'w_gate', 'w_up', 'w_down']
SMALL = [n for n in WEIGHTS if n not in BIG]
SMALL_SHAPES = {
    'g_mix': (1, 1024), 'rel_bias': (1, 8, 257), 'conv_w': (1, 4, 512), 'conv_b': (1, 512),
    'w_rg': (1, 8, 64, 64), 'b_rg': (1, 512), 'w_ig': (1, 8, 64, 64), 'b_ig': (1, 512), 'lru_L': (1, 512),
    'g_out_attn': (1, 512), 'g_out_lru': (1, 512), 'g_cross': (1, 1024), 'g_mem': (1, 1024),
    'g_ffn': (1, 1024), 'g_final': (1024,)}


def _sds(shape, dtype):
    return jax.ShapeDtypeStruct(shape, dtype)


def _cp(*sem):
    return pltpu.CompilerParams(dimension_semantics=sem or None, vmem_limit_bytes=VMEM_LIMIT_V7X)


def _rows(tm, n):
    return pl.BlockSpec((tm, n), lambda i: (i, 0))


def _full(shape):
    nd = len(shape)
    return pl.BlockSpec(shape, lambda i: (0,) * nd)


def _dot(a, b):
    return jnp.dot(a, b, preferred_element_type=F32)


def _dot_nt(a, b):
    return lax.dot_general(a, b, (((1,), (1,)), ((), ())), preferred_element_type=F32)


def _dot_tn(a, b):
    return lax.dot_general(a, b, (((0,), (0,)), ((), ())), preferred_element_type=F32)


def _rinv(x):
    return lax.rsqrt(jnp.mean(x * x, axis=-1, keepdims=True) + EPS)


def _rms_bwd(dy, x, g):
    r = _rinv(x)
    yh = x * r
    dyh = dy * g
    dx = r * (dyh - yh * jnp.mean(dyh * yh, axis=-1, keepdims=True))
    return dx, jnp.sum(dy * yh, axis=0, keepdims=True)


def _gelu(x):
    c = math.sqrt(2.0 / math.pi)
    t = jnp.tanh(c * (x + 0.044715 * x * x * x))
    return 0.5 * x * (1.0 + t)


def _gelu_and_grad(x):
    c = math.sqrt(2.0 / math.pi)
    t = jnp.tanh(c * (x + 0.044715 * x * x * x))
    g = 0.5 * x * (1.0 + t)
    dg = 0.5 * (1.0 + t) + 0.5 * x * (1.0 - t * t) * c * (1.0 + 3.0 * 0.044715 * x * x)
    return g, dg


def _neg_expm1(z):
    series = -z * (1 + z / 2 * (1 + z / 3 * (1 + z / 4 * (1 + z / 5 * (1 + z / 6 * (1 + z / 7))))))
    return jnp.where(z > -0.25, series, 1.0 - jnp.exp(z))


def _lru_gates(u, wrg, brg, wig, big, lam):
    ub = u.astype(BF16)
    r = jax.nn.sigmoid(_dot(ub, wrg) + brg)
    ig = jax.nn.sigmoid(_dot(ub, wig) + big)
    sp = jnp.maximum(-lam, 0.0) + jnp.log1p(jnp.exp(-jnp.abs(lam)))
    la = -LRU_C * r * sp
    a = jnp.exp(la)
    mult = jnp.sqrt(jnp.maximum(_neg_expm1(2.0 * la), 0.0))
    return ub, r, ig, sp, a, mult


def _scan8(a8, b8, hprev):
    row = lax.broadcasted_iota(jnp.int32, a8.shape, 0)
    aa, bb = a8, b8
    for d in (1, 2, 4):
        a_s = pltpu.roll(aa, d, 0)
        b_s = pltpu.roll(bb, d, 0)
        m = row >= d
        bb = jnp.where(m, aa * b_s + bb, bb)
        aa = jnp.where(m, aa * a_s, aa)
    return aa * hprev + bb


def _mesh_pos():
    return lax.axis_index("x"), lax.axis_index("y"), lax.axis_index("c")


def _other_chips(x, y):
    return [(1 - x, y), (x, 1 - y), (1 - x, 1 - y)]


def _ag_copies(ins, outs, sems):
    send_sems, recv_sems, loc_sems = sems
    n = len(ins)
    x, y, c = _mesh_pos()
    mine = 2 * x + y
    chips = _other_chips(x, y)

    def remote(k, j, slot):
        px, py = chips[j]
        return pltpu.make_async_remote_copy(
            src_ref=ins[k], dst_ref=outs[k].at[slot], send_sem=send_sems.at[k, j], recv_sem=recv_sems.at[k, j],
            device_id=(px, py, c), device_id_type=MESH_ID)

    def local(k):
        return pltpu.make_async_copy(ins[k], outs[k].at[mine], loc_sems.at[k])

    def start():
        for k in range(n):
            local(k).start()
            for j in range(3):
                remote(k, j, mine).start()

    def wait():
        for k in range(n):
            for j, (px, py) in enumerate(chips):
                remote(k, j, 2 * px + py).wait_recv()
        for k in range(n):
            for j in range(3):
                remote(k, j, mine).wait_send()
            local(k).wait()

    return start, wait


def _rs_copies(ins, outs, sems):
    send_sems, recv_sems = sems
    n = len(ins)
    x, y, c = _mesh_pos()
    chips = _other_chips(x, y)

    def remote(k, j):
        px, py = chips[j]
        return pltpu.make_async_remote_copy(
            src_ref=ins[k].at[2 * px + py], dst_ref=outs[k].at[j],
            send_sem=send_sems.at[k, j], recv_sem=recv_sems.at[k, j],
            device_id=(px, py, c), device_id_type=MESH_ID)

    def start():
        for k in range(n):
            for j in range(3):
                remote(k, j).start()

    def wait():
        for k in range(n):
            for j in range(3):
                remote(k, j).wait_recv()
        for k in range(n):
            for j in range(3):
                remote(k, j).wait_send()

    return start, wait


def _comm_plan(comm):
    kind, arrs = comm
    k = len(arrs)
    if kind == "ag":
        shapes = [_sds((N_SHARD,) + w.shape, w.dtype) for w in arrs]
        sems = [pltpu.SemaphoreType.DMA((k, 3)), pltpu.SemaphoreType.DMA((k, 3)), pltpu.SemaphoreType.DMA((k,))]
        return _ag_copies, shapes, sems
    shapes = [_sds((3,) + g.shape[1:], g.dtype) for g in arrs]
    return _rs_copies, shapes, [pltpu.SemaphoreType.DMA((k, 3)), pltpu.SemaphoreType.DMA((k, 3))]


def _call(body, name, grid, in_specs, out_specs, out_shape, scratch, args, sem, comm=None):
    if comm is None:
        return pl.pallas_call(body, name=name, grid=grid, in_specs=in_specs, out_specs=out_specs,
                              out_shape=out_shape, scratch_shapes=scratch, compiler_params=_cp(sem))(*args)
    maker, c_shapes, c_sems = _comm_plan(comm)
    k = len(comm[1])
    n_in, n_out, n_scr = len(in_specs), len(out_specs), len(scratch)
    last = grid[0] - 1

    def wrapped(*refs):
        ins, cins = refs[:n_in], refs[n_in:n_in + k]
        o0 = n_in + k
        outs, couts = refs[o0:o0 + n_out], refs[o0 + n_out:o0 + n_out + k]
        s0 = o0 + n_out + k
        start, wait = maker(cins, couts, refs[s0 + n_scr:])
        pl.when(pl.program_id(0) == 0)(start)
        body(*ins, *outs, *refs[s0:s0 + n_scr])
        pl.when(pl.program_id(0) == last)(wait)

    return pl.pallas_call(
        wrapped, name=name, grid=grid, in_specs=list(in_specs) + [_any()] * k,
        out_specs=list(out_specs) + [_any()] * k, out_shape=list(out_shape) + c_shapes,
        scratch_shapes=list(scratch) + c_sems, compiler_params=_cp(sem))(*args, *comm[1])


def _comm_only(name, comm):
    maker, c_shapes, c_sems = _comm_plan(comm)
    k = len(comm[1])

    def body(*refs):
        start, wait = maker(refs[:k], refs[k:2 * k], refs[2 * k:])
        start()
        wait()

    return pl.pallas_call(body, name=name, in_specs=[_any()] * k, out_specs=[_any()] * k, out_shape=c_shapes,
                          scratch_shapes=c_sems, compiler_params=_cp())(*comm[1])


def _any():
    return pl.BlockSpec(memory_space=pl.ANY)


def _rscan8(c8, d8, lnext):
    row = lax.broadcasted_iota(jnp.int32, c8.shape, 0)
    cc, dd = c8, d8
    for d in (1, 2, 4):
        c_s = pltpu.roll(cc, 8 - d, 0)
        d_s = pltpu.roll(dd, 8 - d, 0)
        m = row < 8 - d
        dd = jnp.where(m, cc * d_s + dd, dd)
        cc = jnp.where(m, cc * c_s, cc)
    return cc * lnext + dd


def _f_inproj(x, g_mix, w_in_g, tm, comm=None):
    s_len = x.shape[0]

    def body(x_ref, g_ref, w_ref, h_ref, qkv_ref, xg_ref):
        xv = x_ref[...]
        h = (xv * _rinv(xv) * g_ref[...]).astype(BF16)
        h_ref[...] = h
        qkv_ref[:, 0:640] = _dot(h, w_ref[0]).astype(BF16)
        qkv_ref[:, 640:1280] = _dot(h, w_ref[1]).astype(BF16)
        p2 = _dot(h, w_ref[2])
        qkv_ref[:, 1280:1536] = p2[:, 0:256].astype(BF16)
        xg_ref[:, 0:384] = p2[:, 256:640]
        xg_ref[:, 384:1024] = _dot(h, w_ref[3])

    return _call(
        body, "f_inproj", (s_len // tm,),
        [_rows(tm, 1024), _full((1, 1024)), _full((N_SHARD, 1024, IN_SH))],
        [_rows(tm, 1024), _rows(tm, 1536), _rows(tm, 1024)],
        [_sds((s_len, 1024), BF16), _sds((s_len, 1536), BF16), _sds((s_len, 1024), F32)],
        [], (x, g_mix, w_in_g), "arbitrary", comm)


def _bias_table(frow_ref, bias_sc):
    qa = lax.broadcasted_iota(jnp.int32, (QB, KB), 0) // CHUNK
    kb = lax.broadcasted_iota(jnp.int32, (QB, KB), 1) // CHUNK
    band = jnp.where((kb >= qa) & (kb - qa <= LEFT_CHUNKS), 0.0, NEG).astype(F32)
    for h in range(ATT_HEADS):
        row = jnp.broadcast_to(frow_ref[h:h + 1, :], (QB, ROLL_W))
        toep = pltpu.roll(row, 0, 1, stride=1, stride_axis=0)
        bias_sc[h] = toep[:, 0:KB] + band


def _att_scores(q, ks, bias, padmask):
    s = jnp.concatenate([_dot_nt(q, k) for k in ks], axis=1) * ATT_SCALE + bias + padmask
    m = jnp.max(s, axis=-1, keepdims=True)
    p = jnp.exp(s - m)
    return p, jnp.sum(p, axis=-1, keepdims=True)


def _pad_mask(blk):
    kpos = blk * QB - LEFT_CHUNKS * CHUNK + lax.broadcasted_iota(jnp.int32, (QB, KB), 1)
    return jnp.where(kpos >= 0, 0.0, NEG).astype(F32)


def _att_in_specs(clamp):
    def spec(j, col):
        return pl.BlockSpec((QB, D_ATT), lambda i: (clamp(i) + j, col))
    return [spec(2, 0), spec(0, 1), spec(1, 1), spec(2, 1), spec(0, 2), spec(1, 2), spec(2, 2)]


def _f_attn(qkv_pad, frow, comm=None):
    s_len = qkv_pad.shape[0] - LEFT_CHUNKS * CHUNK
    nb = s_len // QB

    def body(q_ref, k0, k1, k2, v0, v1, v2, frow_ref, o_ref, bias_sc):
        i = pl.program_id(0)

        @pl.when(i == 0)
        def _():
            _bias_table(frow_ref, bias_sc)

        padmask = _pad_mask(i)
        for h in range(ATT_HEADS):
            sl = slice(h * HEAD_DIM, (h + 1) * HEAD_DIM)
            p, l = _att_scores(q_ref[:, sl], [k0[:, sl], k1[:, sl], k2[:, sl]], bias_sc[h], padmask)
            pb = p.astype(BF16)
            o = (_dot(pb[:, 0:QB], v0[:, sl]) + _dot(pb[:, QB:2 * QB], v1[:, sl])
                 + _dot(pb[:, 2 * QB:3 * QB], v2[:, sl]))
            o_ref[:, sl] = o / l

    return _call(
        body, "f_attn", (nb,),
        _att_in_specs(lambda i: i) + [_full((ATT_HEADS, ROLL_W))],
        [_rows(QB, D_ATT)], [_sds((s_len, D_ATT), F32)],
        [pltpu.VMEM((ATT_HEADS, QB, KB), F32)], (*([qkv_pad] * 7), frow), "arbitrary", comm)


def _f_lru(xg, conv_w, conv_b, wrg, brg, wig, big, lam, tl, comm=None):
    s_len = xg.shape[0]

    def body(xg_ref, cw_ref, cb_ref, wrg_ref, brg_ref, wig_ref, big_ref, l_ref,
             rec_ref, u_ref, hs_ref, xbuf, a_sc, b_sc, hcar):
        i = pl.program_id(0)

        @pl.when(i == 0)
        def _():
            xbuf[0:8, :] = jnp.zeros((8, D_LRU), F32)
            hcar[...] = jnp.zeros((8, D_LRU), F32)

        xu0 = xg_ref[:, 0:D_LRU]
        xbuf[8:8 + tl, :] = xu0
        u = cb_ref[...] + cw_ref[0:1, :] * xbuf[pl.ds(5, tl), :]
        for j in range(1, 4):
            u = u + cw_ref[j:j + 1, :] * xbuf[pl.ds(5 + j, tl), :]
        xbuf[0:8, :] = xu0[tl - 8:tl, :]
        u_ref[...] = u
        _, _, ig, _, a, mult = _lru_gates(u, wrg_ref[...], brg_ref[...], wig_ref[...], big_ref[...], l_ref[...])
        a_sc[...] = a
        b_sc[...] = mult * (ig * u)

        def grp(g, hprev):
            off = pl.multiple_of(g * 8, 8)
            h8 = _scan8(a_sc[pl.ds(off, 8), :], b_sc[pl.ds(off, 8), :], hprev)
            hs_ref[pl.ds(off, 8), :] = h8
            return h8[7:8, :]

        hcar[0:1, :] = lax.fori_loop(0, tl // 8, grp, hcar[0:1, :])
        rec_ref[...] = hs_ref[...] * _gelu(xg_ref[:, D_LRU:2 * D_LRU])

    vec = _full((1, D_LRU))
    return _call(
        body, "f_lru", (s_len // tl,),
        [_rows(tl, 1024), _full((4, D_LRU)), vec, _full((D_LRU, D_LRU)), vec, _full((D_LRU, D_LRU)), vec, vec],
        [_rows(tl, D_LRU)] * 3, [_sds((s_len, D_LRU), F32)] * 3,
        [pltpu.VMEM((tl + 8, D_LRU), F32), pltpu.VMEM((tl, D_LRU), F32),
         pltpu.VMEM((tl, D_LRU), F32), pltpu.VMEM((8, D_LRU), F32)],
        (xg, conv_w, conv_b, wrg, brg, wig, big, lam), "arbitrary", comm)


def _f_mem(mem, g_mem, wk, wv):
    def body(mem_ref, g_ref, wk_ref, wv_ref, mn_ref, kx_ref, vx_ref):
        mv = mem_ref[...]
        mn = (mv * _rinv(mv) * g_ref[...]).astype(BF16)
        mn_ref[...] = mn
        kx_ref[...] = _dot(mn, wk_ref[...]).astype(BF16)
        vx_ref[...] = _dot(mn, wv_ref[...]).astype(BF16)

    m = mem.shape[0]
    return pl.pallas_call(
        body, name="f_mem", out_shape=[_sds((m, 1024), BF16)] * 3,
        compiler_params=_cp())(mem, g_mem, wk, wv)


def _xattn_probs(q, k):
    s = _dot_nt(q, k) * X_SCALE
    m = jnp.max(s, axis=-1, keepdims=True)
    p = jnp.exp(s - m)
    return p, jnp.sum(p, axis=-1, keepdims=True)


def _f_mid(x, att, rec, g_oa, g_ol, w_out, g_cross, wq, kx, vx, wo, tm, comm=None):
    s_len = x.shape[0]
    m_len = kx.shape[0]

    def body(x_ref, att_ref, rec_ref, goa_ref, gol_ref, wout_ref, gc_ref, wq_ref, kx_ref, vx_ref, wo_ref,
             mg_ref, x1_ref, hc_ref, qx_ref, ox_ref, x2_ref):
        av = att_ref[...]
        rv = rec_ref[...]
        mg_ref[:, 0:D_ATT] = (av * _rinv(av) * goa_ref[...]).astype(BF16)
        mg_ref[:, D_ATT:1024] = (rv * _rinv(rv) * gol_ref[...]).astype(BF16)
        x1 = x_ref[...] + _dot(mg_ref[...], wout_ref[...])
        x1_ref[...] = x1
        hc = (x1 * _rinv(x1) * gc_ref[...]).astype(BF16)
        hc_ref[...] = hc
        qx_ref[...] = _dot(hc, wq_ref[...]).astype(BF16)
        for h in range(X_HEADS):
            sl = slice(h * X_HEAD_DIM, (h + 1) * X_HEAD_DIM)
            p, l = _xattn_probs(qx_ref[:, sl], kx_ref[:, sl])
            ox_ref[:, sl] = (_dot(p.astype(BF16), vx_ref[:, sl]) / l).astype(BF16)
        x2_ref[...] = x1 + _dot(ox_ref[...], wo_ref[...])

    sq = _full((1024, 1024))
    return _call(
        body, "f_mid", (s_len // tm,),
        [_rows(tm, 1024), _rows(tm, 512), _rows(tm, 512), _full((1, 512)), _full((1, 512)), sq,
         _full((1, 1024)), sq, _full((m_len, 1024)), _full((m_len, 1024)), sq],
        [_rows(tm, 1024)] * 6,
        [_sds((s_len, 1024), BF16), _sds((s_len, 1024), F32), _sds((s_len, 1024), BF16),
         _sds((s_len, 1024), BF16), _sds((s_len, 1024), BF16), _sds((s_len, 1024), F32)],
        [], (x, att, rec, g_oa, g_ol, w_out, g_cross, wq, kx, vx, wo), "arbitrary", comm)


def _load_weights_once(pairs):
    @pl.when(pl.program_id(0) == 0)
    def _():
        for hbm, vmem in pairs:
            pltpu.sync_copy(hbm, vmem)


def _sh_rows(tm, n):
    return pl.BlockSpec((N_SHARD, tm, n), lambda i: (0, i, 0))


def _f_ffn(x2, tgt, g_ffn, g_final, wg, wu, wd, tm):
    s_len = x2.shape[0]

    def body(x2_ref, t_ref, gf_ref, gfin_ref, wg_hbm, wu_hbm, wd_hbm,
             hf_ref, g_ref, u_ref, a_ref, dx3_ref, loss_ref, dgfin_ref, wg_ref, wu_ref, wd_ref):
        _load_weights_once([(wg_hbm, wg_ref), (wu_hbm, wu_ref), (wd_hbm, wd_ref)])

        @pl.when(pl.program_id(0) == 0)
        def _():
            loss_ref[...] = jnp.zeros_like(loss_ref)
            dgfin_ref[...] = jnp.zeros_like(dgfin_ref)

        x2v = x2_ref[...]
        hf = (x2v * _rinv(x2v) * gf_ref[...]).astype(BF16)
        hf_ref[...] = hf
        x3 = x2v
        for s in range(N_SHARD):
            gv = _dot(hf, wg_ref[s])
            uv = _dot(hf, wu_ref[s])
            av = (gv * jax.nn.sigmoid(gv) * uv).astype(BF16)
            g_ref[s] = gv.astype(BF16)
            u_ref[s] = uv.astype(BF16)
            a_ref[s] = av
            x3 = x3 + _dot(av, wd_ref[s])
        r3 = _rinv(x3)
        yh = x3 * r3
        gfin = gfin_ref[...]
        err = yh * gfin - t_ref[...]
        loss_ref[...] += jnp.full((1, 128), 0.5 / D_MODEL, F32) * jnp.sum(err * err)
        dy = err * (1.0 / D_MODEL)
        dgfin_ref[...] += jnp.sum(dy * yh, axis=0, keepdims=True)
        dyh = dy * gfin
        dx3_ref[...] = r3 * (dyh - yh * jnp.mean(dyh * yh, axis=-1, keepdims=True))

    vec = _full((1, 1024))
    return pl.pallas_call(
        body, name="f_ffn", grid=(s_len // tm,),
        in_specs=[_rows(tm, 1024), _rows(tm, 1024), vec, vec, _any(), _any(), _any()],
        out_specs=[_rows(tm, 1024), _sh_rows(tm, FF_SH), _sh_rows(tm, FF_SH), _sh_rows(tm, FF_SH),
                   _rows(tm, 1024), _full((1, 128)), vec],
        out_shape=[_sds((s_len, 1024), BF16)] + [_sds((N_SHARD, s_len, FF_SH), BF16)] * 3
                  + [_sds((s_len, 1024), F32), _sds((1, 128), F32), _sds((1, 1024), F32)],
        scratch_shapes=[pltpu.VMEM((N_SHARD, 1024, FF_SH), BF16), pltpu.VMEM((N_SHARD, 1024, FF_SH), BF16),
                        pltpu.VMEM((N_SHARD, FF_SH, 1024), BF16)],
        compiler_params=_cp("arbitrary"))(x2, tgt, g_ffn, g_final, wg, wu, wd)


def _b_ffn(dx3, x2, gact, uact, g_ffn, wg, wu, wd, tm):
    s_len = x2.shape[0]

    def body(dx3_ref, x2_ref, g_ref, u_ref, gf_ref, wg_hbm, wu_hbm, wd_hbm,
             dg_ref, du_ref, dx2_ref, dgf_ref, wg_ref, wu_ref, wd_ref):
        _load_weights_once([(wg_hbm, wg_ref), (wu_hbm, wu_ref), (wd_hbm, wd_ref)])

        @pl.when(pl.program_id(0) == 0)
        def _():
            dgf_ref[...] = jnp.zeros_like(dgf_ref)

        dx3v = dx3_ref[...]
        dx3b = dx3v.astype(BF16)
        dhf = jnp.zeros(dx3v.shape, F32)
        for s in range(N_SHARD):
            da = _dot_nt(dx3b, wd_ref[s])
            gv = g_ref[s].astype(F32)
            uv = u_ref[s].astype(F32)
            sg = jax.nn.sigmoid(gv)
            dub = (da * gv * sg).astype(BF16)
            dgb = (da * uv * (sg * (1.0 + gv * (1.0 - sg)))).astype(BF16)
            du_ref[s] = dub
            dg_ref[s] = dgb
            dhf = dhf + _dot_nt(dgb, wg_ref[s]) + _dot_nt(dub, wu_ref[s])
        dx, dgf = _rms_bwd(dhf, x2_ref[...], gf_ref[...])
        dx2_ref[...] = dx3v + dx
        dgf_ref[...] += dgf

    vec = _full((1, 1024))
    return pl.pallas_call(
        body, name="b_ffn", grid=(s_len // tm,),
        in_specs=[_rows(tm, 1024), _rows(tm, 1024), _sh_rows(tm, FF_SH), _sh_rows(tm, FF_SH), vec,
                  _any(), _any(), _any()],
        out_specs=[_sh_rows(tm, FF_SH), _sh_rows(tm, FF_SH), _rows(tm, 1024), vec],
        out_shape=[_sds((N_SHARD, s_len, FF_SH), BF16)] * 2 + [_sds((s_len, 1024), F32), _sds((1, 1024), F32)],
        scratch_shapes=[pltpu.VMEM((N_SHARD, 1024, FF_SH), BF16), pltpu.VMEM((N_SHARD, 1024, FF_SH), BF16),
                        pltpu.VMEM((N_SHARD, FF_SH, 1024), BF16)],
        compiler_params=_cp("arbitrary"))(dx3, x2, gact, uact, g_ffn, wg, wu, wd)


def _b_mid(dx2, qx, x1, att, rec, kx, vx, wo, wq, w_out, g_cross, g_oa, g_ol, tm, comm=None):
    s_len = x1.shape[0]
    m_len = kx.shape[0]

    def body(dx2_ref, qx_ref, x1_ref, att_ref, rec_ref, kx_ref, vx_ref, wo_ref, wq_ref, wout_ref,
             gc_ref, goa_ref, gol_ref,
             dqx_ref, dx1_ref, datt_ref, drec_ref, dkx_ref, dvx_ref, dgc_ref, dgoa_ref, dgol_ref):
        @pl.when(pl.program_id(0) == 0)
        def _():
            for r in (dkx_ref, dvx_ref, dgc_ref, dgoa_ref, dgol_ref):
                r[...] = jnp.zeros_like(r)

        dx2v = dx2_ref[...]
        dox = _dot_nt(dx2v.astype(BF16), wo_ref[...])
        for h in range(X_HEADS):
            sl = slice(h * X_HEAD_DIM, (h + 1) * X_HEAD_DIM)
            q = qx_ref[:, sl]
            p, l = _xattn_probs(q, kx_ref[:, sl])
            pn = p / l
            dob = dox[:, sl].astype(BF16)
            dp = _dot_nt(dob, vx_ref[:, sl])
            dvx_ref[:, sl] += _dot_tn(pn.astype(BF16), dob)
            ds = pn * (dp - jnp.sum(dp * pn, axis=-1, keepdims=True))
            dsb = (ds * X_SCALE).astype(BF16)
            dqx_ref[:, sl] = _dot(dsb, kx_ref[:, sl]).astype(BF16)
            dkx_ref[:, sl] += _dot_tn(dsb, q)
        dhc = _dot_nt(dqx_ref[...], wq_ref[...])
        dx, dgc = _rms_bwd(dhc, x1_ref[...], gc_ref[...])
        dx1 = dx2v + dx
        dx1_ref[...] = dx1
        dgc_ref[...] += dgc
        dmg = _dot_nt(dx1.astype(BF16), wout_ref[...])
        da, dgoa = _rms_bwd(dmg[:, 0:D_ATT], att_ref[...], goa_ref[...])
        datt_ref[...] = da
        dgoa_ref[...] += dgoa
        dr, dgol = _rms_bwd(dmg[:, D_ATT:1024], rec_ref[...], gol_ref[...])
        drec_ref[...] = dr
        dgol_ref[...] += dgol

    sq = _full((1024, 1024))
    mk = _full((m_len, 1024))
    return _call(
        body, "b_mid", (s_len // tm,),
        [_rows(tm, 1024), _rows(tm, 1024), _rows(tm, 1024), _rows(tm, 512), _rows(tm, 512), mk, mk,
         sq, sq, sq, _full((1, 1024)), _full((1, 512)), _full((1, 512))],
        [_rows(tm, 1024), _rows(tm, 1024), _rows(tm, 512), _rows(tm, 512), mk, mk,
         _full((1, 1024)), _full((1, 512)), _full((1, 512))],
        [_sds((s_len, 1024), BF16), _sds((s_len, 1024), F32), _sds((s_len, 512), F32),
         _sds((s_len, 512), F32), _sds((m_len, 1024), F32), _sds((m_len, 1024), F32),
         _sds((1, 1024), F32), _sds((1, 512), F32), _sds((1, 512), F32)],
        [], (dx2, qx, x1, att, rec, kx, vx, wo, wq, w_out, g_cross, g_oa, g_ol), "arbitrary", comm)


def _b_mem(dkx, dvx, mem, mn, g_mem, wk, wv):
    def body(dkx_ref, dvx_ref, mem_ref, mn_ref, g_ref, wk_ref, wv_ref, dwk_ref, dwv_ref, dgm_ref,
             dwkb_ref, dwvb_ref):
        dkb = dkx_ref[...].astype(BF16)
        dvb = dvx_ref[...].astype(BF16)
        dwk = _dot_tn(mn_ref[...], dkb)
        dwv = _dot_tn(mn_ref[...], dvb)
        dwk_ref[...] = dwk
        dwv_ref[...] = dwv
        dwkb_ref[...] = dwk.astype(BF16)
        dwvb_ref[...] = dwv.astype(BF16)
        dmn = _dot_nt(dkb, wk_ref[...]) + _dot_nt(dvb, wv_ref[...])
        mv = mem_ref[...]
        dgm_ref[...] = jnp.sum(dmn * (mv * _rinv(mv)), axis=0, keepdims=True)

    return pl.pallas_call(
        body, name="b_mem",
        out_shape=[_sds((1024, 1024), F32), _sds((1024, 1024), F32), _sds((1, 1024), F32),
                   _sds((1024, 1024), BF16), _sds((1024, 1024), BF16)],
        compiler_params=_cp())(dkx, dvx, mem, mn, g_mem, wk, wv)


def _b_lru(drec, hs, u, xg, conv_w, wrg, brg, wig, big, lam, tl, comm=None):
    s_len = xg.shape[0]
    nt = s_len // tl

    def body(drec_ref, hs_ref, hsp_ref, u_ref, xg_ref, cw_ref, wrg_ref, brg_ref, wig_ref, big_ref, l_ref,
             dxg_ref, dwrg_ref, dwig_ref, dbrg_ref, dbig_ref, dlam_ref, dcw_ref, dcb_ref,
             hbuf, abuf, dubuf, c_sc, d_sc, lam_sc, lcar):
        i = pl.program_id(0)
        tt = nt - 1 - i

        @pl.when(i == 0)
        def _():
            for r in (dwrg_ref, dwig_ref, dbrg_ref, dbig_ref, dlam_ref, dcw_ref, dcb_ref):
                r[...] = jnp.zeros_like(r)
            abuf[tl:tl + 8, :] = jnp.zeros((8, D_LRU), F32)
            dubuf[tl:tl + 8, :] = jnp.zeros((8, D_LRU), F32)
            lcar[...] = jnp.zeros((8, D_LRU), F32)

        xu0 = xg_ref[:, 0:D_LRU]
        hsv = hs_ref[...]
        uv = u_ref[...]
        hbuf[8:8 + tl, :] = hsv
        hbuf[0:8, :] = jnp.where(tt > 0, hsp_ref[...], 0.0)
        hshift = hbuf[pl.ds(7, tl), :]
        wrg_v = wrg_ref[...]
        wig_v = wig_ref[...]
        lamv = l_ref[...]
        ub, r, ig, sp, a, mult = _lru_gates(uv, wrg_v, brg_ref[...], wig_v, big_ref[...], lamv)
        abuf[0:tl, :] = a
        c_sc[...] = abuf[pl.ds(1, tl), :]
        gel, dgel = _gelu_and_grad(xg_ref[:, D_LRU:2 * D_LRU])
        drv = drec_ref[...]
        d_sc[...] = drv * gel
        dxg_ref[:, D_LRU:2 * D_LRU] = (drv * hsv * dgel).astype(BF16)

        def grp(k, lnext):
            off = pl.multiple_of((tl // 8 - 1 - k) * 8, 8)
            l8 = _rscan8(c_sc[pl.ds(off, 8), :], d_sc[pl.ds(off, 8), :], lnext)
            lam_sc[pl.ds(off, 8), :] = l8
            return l8[0:1, :]

        lcar[0:1, :] = lax.fori_loop(0, tl // 8, grp, lcar[0:1, :])
        abuf[tl:tl + 8, :] = a[0:8, :]
        db = lam_sc[...]
        da = db * hshift
        dmult = db * (ig * uv)
        dig = db * mult * uv
        du = db * mult * ig
        dla = da * a - dmult * (a * a) / mult
        dlam_ref[...] += jnp.sum(dla * (-LRU_C) * r, axis=0, keepdims=True)
        dzr = dla * (-LRU_C * sp) * r * (1.0 - r)
        dzi = dig * ig * (1.0 - ig)
        dzrb = dzr.astype(BF16)
        dzib = dzi.astype(BF16)
        du = du + _dot_nt(dzrb, wrg_v) + _dot_nt(dzib, wig_v)
        dwrg_ref[...] += _dot_tn(ub, dzrb)
        dwig_ref[...] += _dot_tn(ub, dzib)
        dbrg_ref[...] += jnp.sum(dzr, axis=0, keepdims=True)
        dbig_ref[...] += jnp.sum(dzi, axis=0, keepdims=True)
        dcb_ref[...] += jnp.sum(du, axis=0, keepdims=True)
        dubuf[0:tl, :] = du
        dxu0 = jnp.zeros((tl, D_LRU), F32)
        for j in range(4):
            dsh = dubuf[pl.ds(3 - j, tl), :]
            dxu0 = dxu0 + cw_ref[j:j + 1, :] * dsh
            dcw_ref[j:j + 1, :] += jnp.sum(xu0 * dsh, axis=0, keepdims=True)
        dubuf[tl:tl + 8, :] = du[0:8, :]
        dxg_ref[:, 0:D_LRU] = dxu0.astype(BF16)

        @pl.when(i == nt - 1)
        def _():
            dlam_ref[...] = dlam_ref[...] * (-jax.nn.sigmoid(-lamv))

    def rev(n):
        return pl.BlockSpec((tl, n), lambda i: (nt - 1 - i, 0))

    prev8 = pl.BlockSpec((8, D_LRU), lambda i: (jnp.maximum((nt - 1 - i) * (tl // 8) - 1, 0), 0))
    vec = _full((1, D_LRU))
    sq = _full((D_LRU, D_LRU))
    return _call(
        body, "b_lru", (nt,),
        [rev(D_LRU), rev(D_LRU), prev8, rev(D_LRU), rev(1024), _full((4, D_LRU)), sq, vec, sq, vec, vec],
        [rev(1024), sq, sq, vec, vec, vec, _full((4, D_LRU)), vec],
        [_sds((s_len, 1024), BF16), _sds((D_LRU, D_LRU), F32), _sds((D_LRU, D_LRU), F32),
         _sds((1, D_LRU), F32), _sds((1, D_LRU), F32), _sds((1, D_LRU), F32),
         _sds((4, D_LRU), F32), _sds((1, D_LRU), F32)],
        [pltpu.VMEM((tl + 8, D_LRU), F32)] * 3 + [pltpu.VMEM((tl, D_LRU), F32)] * 3
        + [pltpu.VMEM((8, D_LRU), F32)],
        (drec, hs, hs, u, xg, conv_w, wrg, brg, wig, big, lam), "arbitrary", comm)


def _b_attn(qkv_pad, datt, frow, comm=None):
    s_len = datt.shape[0]
    nb = s_len // QB

    def body(q_ref, k0, k1, k2, v0, v1, v2, do_ref, frow_ref, dq_ref, dkv_ref, dfrow_ref,
             bias_sc, dt_sc, acc_sc):
        t = pl.program_id(0)

        @pl.when(t == 0)
        def _():
            _bias_table(frow_ref, bias_sc)
            dt_sc[...] = jnp.zeros_like(dt_sc)
            acc_sc[...] = jnp.zeros_like(acc_sc)

        @pl.when(t < nb)
        def _():
            padmask = _pad_mask(t)
            for h in range(ATT_HEADS):
                sl = slice(h * HEAD_DIM, (h + 1) * HEAD_DIM)
                q = q_ref[:, sl]
                ks = [k0[:, sl], k1[:, sl], k2[:, sl]]
                vs = [v0[:, sl], v1[:, sl], v2[:, sl]]
                p, l = _att_scores(q, ks, bias_sc[h], padmask)
                pn = p / l
                dob = do_ref[:, sl].astype(BF16)
                dp = jnp.concatenate([_dot_nt(dob, v) for v in vs], axis=1)
                ds = pn * (dp - jnp.sum(dp * pn, axis=-1, keepdims=True))
                dt_sc[h] += ds
                dsb = (ds * ATT_SCALE).astype(BF16)
                pnb = pn.astype(BF16)
                dq = _dot(dsb[:, 0:QB], ks[0])
                for j in (1, 2):
                    dq = dq + _dot(dsb[:, j * QB:(j + 1) * QB], ks[j])
                dq_ref[:, sl] = dq.astype(BF16)
                for j in range(3):
                    slot = (t + 1 + j) % 3
                    cs = slice(j * QB, (j + 1) * QB)
                    acc_sc[slot, :, sl] += _dot_tn(dsb[:, cs], q)
                    acc_sc[slot, :, D_ATT + h * HEAD_DIM:D_ATT + (h + 1) * HEAD_DIM] += _dot_tn(pnb[:, cs], dob)

        done = (t + 1) % 3

        @pl.when(t >= 2)
        def _():
            dkv_ref[...] = acc_sc[done].astype(BF16)

        acc_sc[done] = jnp.zeros((QB, 2 * D_ATT), F32)

        @pl.when(t == nb + 1)
        def _():
            row = lax.broadcasted_iota(jnp.int32, (8, ROLL_W), 0)
            pad = jnp.zeros((8, ROLL_W - KB), F32)
            for h in range(ATT_HEADS):
                acc8 = jnp.concatenate([dt_sc[h, 0:8, :], pad], axis=1)
                for a1 in range(1, QB // 8):
                    blk = jnp.concatenate([dt_sc[h, 8 * a1:8 * a1 + 8, :], pad], axis=1)
                    acc8 = acc8 + pltpu.roll(blk, ROLL_W - 8 * a1, 1)
                for k in range(3):
                    acc8 = jnp.where(((row >> k) & 1) == 1, pltpu.roll(acc8, ROLL_W - (1 << k), 1), acc8)
                dfrow_ref[h:h + 1, :] = jnp.sum(acc8, axis=0, keepdims=True)

    clamp = lambda t: jnp.minimum(t, nb - 1)
    return _call(
        body, "b_attn", (nb + 2,),
        _att_in_specs(clamp) + [pl.BlockSpec((QB, D_ATT), lambda t: (clamp(t), 0)), _full((ATT_HEADS, ROLL_W))],
        [pl.BlockSpec((QB, D_ATT), lambda t: (clamp(t), 0)),
         pl.BlockSpec((QB, 2 * D_ATT), lambda t: (jnp.maximum(t - 2, 0), 0)),
         _full((ATT_HEADS, ROLL_W))],
        [_sds((s_len, D_ATT), BF16), _sds((s_len, 2 * D_ATT), BF16), _sds((ATT_HEADS, ROLL_W), F32)],
        [pltpu.VMEM((ATT_HEADS, QB, KB), F32), pltpu.VMEM((ATT_HEADS, QB, KB), F32),
         pltpu.VMEM((3, QB, 2 * D_ATT), F32)],
        (*([qkv_pad] * 7), datt, frow), "arbitrary", comm)


def _b_inproj(dq, dkv, dxg, h, x, dx1, g_mix, w_in_g, tm):
    s_len = x.shape[0]

    def body(dq_ref, dkv_ref, dxg_ref, h_ref, x_ref, dx1_ref, g_ref, w_ref, gx_ref, dgm_ref, dw_ref):
        @pl.when(pl.program_id(0) == 0)
        def _():
            dgm_ref[...] = jnp.zeros_like(dgm_ref)
            dw_ref[...] = jnp.zeros_like(dw_ref)

        dproj = jnp.concatenate([dq_ref[...], dkv_ref[...], dxg_ref[...]], axis=1)
        hv = h_ref[...]
        dh = jnp.zeros((tm, 1024), F32)
        for s in range(N_SHARD):
            dps = dproj[:, s * IN_SH:(s + 1) * IN_SH]
            dh = dh + _dot_nt(dps, w_ref[s])
            dw_ref[s] += _dot_tn(hv, dps)
        dx, dgm = _rms_bwd(dh, x_ref[...], g_ref[...])
        gx_ref[...] = dx1_ref[...] + dx
        dgm_ref[...] += dgm

    wspec = _full((N_SHARD, 1024, IN_SH))
    return pl.pallas_call(
        body, name="b_inproj", grid=(s_len // tm,),
        in_specs=[_rows(tm, 512), _rows(tm, 1024), _rows(tm, 1024), _rows(tm, 1024), _rows(tm, 1024),
                  _rows(tm, 1024), _full((1, 1024)), wspec],
        out_specs=[_rows(tm, 1024), _full((1, 1024)), wspec],
        out_shape=[_sds((s_len, 1024), F32), _sds((1, 1024), F32), _sds((N_SHARD, 1024, IN_SH), F32)],
        compiler_params=_cp("arbitrary"))(dq, dkv, dxg, h, x, dx1, g_mix, w_in_g)


def _mm_tn(xa, ya, name, ts):
    s_len, k = xa.shape
    n = ya.shape[1]

    steps = s_len // ts

    def body(x_ref, y_ref, o_ref, ob_ref):
        @pl.when(pl.program_id(0) == 0)
        def _():
            o_ref[...] = jnp.zeros_like(o_ref)
        o_ref[...] += _dot_tn(x_ref[...].astype(BF16), y_ref[...].astype(BF16))

        @pl.when(pl.program_id(0) == steps - 1)
        def _():
            ob_ref[...] = o_ref[...].astype(BF16)

    return pl.pallas_call(
        body, name=name, grid=(steps,), in_specs=[_rows(ts, k), _rows(ts, n)],
        out_specs=[_full((k, n))] * 2, out_shape=[_sds((k, n), F32), _sds((k, n), BF16)],
        compiler_params=_cp("arbitrary"))(xa, ya)


def _mm_tn_ysh(xa, y4, name, ts):
    s_len, k = xa.shape
    n = y4.shape[2]

    steps = s_len // ts

    def body(x_ref, y_ref, o_ref, ob_ref):
        @pl.when(pl.program_id(0) == 0)
        def _():
            o_ref[...] = jnp.zeros_like(o_ref)
        xb = x_ref[...].astype(BF16)
        for s in range(N_SHARD):
            o_ref[s] += _dot_tn(xb, y_ref[s])

        @pl.when(pl.program_id(0) == steps - 1)
        def _():
            ob_ref[...] = o_ref[...].astype(BF16)

    return pl.pallas_call(
        body, name=name, grid=(steps,), in_specs=[_rows(ts, k), _sh_rows(ts, n)],
        out_specs=[_full((N_SHARD, k, n))] * 2,
        out_shape=[_sds((N_SHARD, k, n), F32), _sds((N_SHARD, k, n), BF16)],
        compiler_params=_cp("arbitrary"))(xa, y4)


def _mm_tn_xsh(x4, ya, name, ts):
    s_len, n = ya.shape
    k = x4.shape[2]

    steps = s_len // ts

    def body(x_ref, y_ref, o_ref, ob_ref):
        @pl.when(pl.program_id(0) == 0)
        def _():
            o_ref[...] = jnp.zeros_like(o_ref)
        yb = y_ref[...].astype(BF16)
        for s in range(N_SHARD):
            o_ref[s] += _dot_tn(x_ref[s], yb)

        @pl.when(pl.program_id(0) == steps - 1)
        def _():
            ob_ref[...] = o_ref[...].astype(BF16)

    return pl.pallas_call(
        body, name=name, grid=(steps,), in_specs=[_sh_rows(ts, k), _rows(ts, n)],
        out_specs=[_full((N_SHARD, k, n))] * 2,
        out_shape=[_sds((N_SHARD, k, n), F32), _sds((N_SHARD, k, n), BF16)],
        compiler_params=_cp("arbitrary"))(x4, ya)


def _frow_from_rel_bias(rb):
    hi = jnp.broadcast_to(rb[:, 256:257], (ATT_HEADS, 385))
    mid = rb[:, 1:256][:, ::-1]
    lo = jnp.broadcast_to(rb[:, 0:1], (ATT_HEADS, 128))
    wrap = jnp.broadcast_to(rb[:, 256:257], (ATT_HEADS, ROLL_W - KB))
    return jnp.concatenate([hi, mid, lo, wrap], axis=1)


def _rel_bias_grad_from_dfrow(df):
    g256 = jnp.sum(df[:, 0:385], axis=1, keepdims=True) + jnp.sum(df[:, KB:ROLL_W], axis=1, keepdims=True)
    mid = df[:, 385:640][:, ::-1]
    g0 = jnp.sum(df[:, 640:KB], axis=1, keepdims=True)
    return jnp.concatenate([g0, mid, g256], axis=1)


def _block_diag(w):
    eye = jnp.eye(8, dtype=w.dtype)
    return (w[:, :, None, :] * eye[:, None, :, None]).reshape(D_LRU, D_LRU)


def _block_diag_extract(dense):
    eye = jnp.eye(8, dtype=dense.dtype)
    return jnp.sum(dense.reshape(8, 64, 8, 64) * eye[:, None, :, None], axis=2)


MID = ['w_out', 'wq_c', 'wk_c', 'wv_c', 'wo_c']
AG_IN_INPROJ = ['w_out', 'wq_c', 'wk_c']
AG_IN_ATTN = ['wv_c', 'wo_c', 'w_gate']
AG_IN_LRU = ['w_up']
AG_IN_MID = ['w_down']
RS_IN_MID = ['w_gate', 'w_up']
RS_IN_LRU = ['w_down']
RS_IN_ATTN = MID


def _local_step(x, mem, tgt, p, gw, shards=None):
    s_len = x.shape[0]
    tm = min(256, s_len)
    tl = min(512, s_len)
    frow = _frow_from_rel_bias(p['rel_bias'])
    wrg = _block_diag(p['w_rg']).astype(BF16)
    wig = _block_diag(p['w_ig']).astype(BF16)
    gw = dict(gw)

    def ag(names):
        return None if shards is None else ("ag", [shards[n] for n in names])

    def rs(names):
        return None if shards is None else ("rs", [bigb[n] for n in names])

    h, qkv, xg, *got = _f_inproj(x, p['g_mix'], gw['w_in'], tm, ag(AG_IN_INPROJ))
    gw.update(zip(AG_IN_INPROJ, got))
    qkv_pad = jnp.pad(qkv, ((LEFT_CHUNKS * CHUNK, 0), (0, 0)))
    att, *got = _f_attn(qkv_pad, frow, ag(AG_IN_ATTN))
    gw.update(zip(AG_IN_ATTN, got))
    rec, u, hs, *got = _f_lru(xg, p['conv_w'], p['conv_b'], wrg, p['b_rg'], wig, p['b_ig'], p['lru_L'], tl,
                              ag(AG_IN_LRU))
    gw.update(zip(AG_IN_LRU, got))
    w_out = gw['w_out'].reshape(1024, 1024)
    wq = gw['wq_c'].reshape(1024, 1024)
    wk = gw['wk_c'].reshape(1024, 1024)
    wv = gw['wv_c'].reshape(1024, 1024)
    wo = gw['wo_c'].reshape(1024, 1024)
    mn, kx, vx = _f_mem(mem, p['g_mem'], wk, wv)
    mg, x1, hc, qx, ox, x2, *got = _f_mid(x, att, rec, p['g_out_attn'], p['g_out_lru'], w_out, p['g_cross'],
                                          wq, kx, vx, wo, tm, ag(AG_IN_MID))
    gw.update(zip(AG_IN_MID, got))
    hf, gact, uact, aact, dx3, loss, dg_final = _f_ffn(x2, tgt, p['g_ffn'], p['g_final'],
                                                       gw['w_gate'], gw['w_up'], gw['w_down'], tm)

    ts = min(512, s_len)
    big, bigb, recv = {}, {}, {}
    dgact, duact, dx2, dg_ffn = _b_ffn(dx3, x2, gact, uact, p['g_ffn'], gw['w_gate'], gw['w_up'], gw['w_down'], tm)
    big['w_gate'], bigb['w_gate'] = _mm_tn_ysh(hf, dgact, "dw_gate", ts)
    big['w_up'], bigb['w_up'] = _mm_tn_ysh(hf, duact, "dw_up", ts)
    big['w_down'], bigb['w_down'] = _mm_tn_xsh(aact, dx3, "dw_down", ts)

    dqx, dx1, datt, drec, dkx, dvx, dg_cross, dg_oa, dg_ol, *got = _b_mid(
        dx2, qx, x1, att, rec, kx, vx, wo, wq, w_out, p['g_cross'], p['g_out_attn'], p['g_out_lru'], tm,
        rs(RS_IN_MID))
    recv.update(zip(RS_IN_MID, got))
    dwk, dwv, dg_mem, dwkb, dwvb = _b_mem(dkx, dvx, mem, mn, p['g_mem'], wk, wv)
    big['wk_c'], bigb['wk_c'] = dwk, dwkb
    big['wv_c'], bigb['wv_c'] = dwv, dwvb
    big['w_out'], bigb['w_out'] = _mm_tn(mg, dx1, "dw_out", ts)
    big['wq_c'], bigb['wq_c'] = _mm_tn(hc, dqx, "dw_q", ts)
    big['wo_c'], bigb['wo_c'] = _mm_tn(ox, dx2, "dw_o", ts)
    for n in MID:
        big[n] = big[n].reshape(N_SHARD, 256, 1024)
        bigb[n] = bigb[n].reshape(N_SHARD, 256, 1024)

    dxg, dwrg, dwig, dbrg, dbig, dlam, dcw, dcb, *got = _b_lru(
        drec, hs, u, xg, p['conv_w'], wrg, p['b_rg'], wig, p['b_ig'], p['lru_L'], tl, rs(RS_IN_LRU))
    recv.update(zip(RS_IN_LRU, got))
    dq, dkv, dfrow, *got = _b_attn(qkv_pad, datt, frow, rs(RS_IN_ATTN))
    recv.update(zip(RS_IN_ATTN, got))
    grad_x, dg_mix, big['w_in'] = _b_inproj(dq, dkv, dxg, h, x, dx1, p['g_mix'], gw['w_in'], tm)
    small = {
        'g_mix': dg_mix, 'rel_bias': _rel_bias_grad_from_dfrow(dfrow), 'conv_w': dcw, 'conv_b': dcb,
        'w_rg': _block_diag_extract(dwrg), 'b_rg': dbrg, 'w_ig': _block_diag_extract(dwig), 'b_ig': dbig,
        'lru_L': dlam,
        'g_out_attn': dg_oa, 'g_out_lru': dg_ol, 'g_cross': dg_cross, 'g_mem': dg_mem, 'g_ffn': dg_ffn,
        'g_final': dg_final,
    }
    return jnp.sum(loss[0, 0:1]), grad_x, small, big, recv


def _cast_shards(ws):
    def body(*refs):
        n = len(refs) // 2
        for src, dst in zip(refs[:n], refs[n:]):
            dst[...] = src[...].astype(BF16)

    return pl.pallas_call(body, name="cast_shards", out_shape=[_sds(w.shape, BF16) for w in ws],
                          compiler_params=_cp())(*ws)


def _cast_slots(g4, name):
    _, r, c = g4.shape

    def body(in_ref, o_ref):
        o_ref[...] = in_ref[...].astype(BF16)

    spec = pl.BlockSpec((1, r, c), lambda i: (i, 0, 0))
    return pl.pallas_call(body, name=name, grid=(N_SHARD,), in_specs=[spec], out_specs=spec,
                          out_shape=_sds(g4.shape, BF16), compiler_params=_cp("parallel"))(g4)


def _sum_parts(own4, recv3, chip, name):
    _, r, c = own4.shape
    tr = r // 4

    def body(chip_ref, own_ref, rc_ref, o_ref):
        o_ref[...] = ((own_ref[0] + rc_ref[0].astype(F32)) + rc_ref[1].astype(F32)) + rc_ref[2].astype(F32)

    grid_spec = pltpu.PrefetchScalarGridSpec(
        num_scalar_prefetch=1, grid=(4,),
        in_specs=[pl.BlockSpec((1, tr, c), lambda i, ch: (ch[0], i, 0)),
                  pl.BlockSpec((3, tr, c), lambda i, ch: (0, i, 0))],
        out_specs=pl.BlockSpec((tr, c), lambda i, ch: (i, 0)))
    return pl.pallas_call(body, name=name, grid_spec=grid_spec, out_shape=_sds((r, c), F32),
                          compiler_params=_cp("parallel"))(chip, own4, recv3)


def _swap_sibling(parts):
    n = len(parts)

    def body(*refs):
        ins, outs = refs[:n], refs[n:2 * n]
        send_sems, recv_sems = refs[2 * n:]
        x, y, c = _mesh_pos()
        copies = [pltpu.make_async_remote_copy(
            src_ref=ins[k], dst_ref=outs[k], send_sem=send_sems.at[k], recv_sem=recv_sems.at[k],
            device_id=(x, y, 1 - c), device_id_type=MESH_ID) for k in range(n)]
        for cp in copies:
            cp.start()
        for cp in copies:
            cp.wait()

    return pl.pallas_call(
        body, name="swap_sibling",
        in_specs=[_any()] * n, out_specs=[_any()] * n,
        out_shape=[_sds(p.shape, p.dtype) for p in parts],
        scratch_shapes=[pltpu.SemaphoreType.DMA((n,)), pltpu.SemaphoreType.DMA((n,))],
        compiler_params=_cp())(*parts)


def _adamw_math(w, g, m, v):
    m = ADAM_B1 * m + (1.0 - ADAM_B1) * g
    v = ADAM_B2 * v + (1.0 - ADAM_B2) * (g * g)
    m_hat = m / (1.0 - ADAM_B1 ** ADAM_STEP)
    v_hat = v / (1.0 - ADAM_B2 ** ADAM_STEP)
    delta = -ADAM_LR * (m_hat / (jnp.sqrt(v_hat) + ADAM_EPS) + ADAM_WD * w)
    return delta, m, v


def _final_adamw(pa, pb, w, m, v, name):
    r, c = w.shape
    tr = r // 4

    def body(pa_ref, pb_ref, w_ref, m_ref, v_ref, g_ref, d_ref, nm_ref, nv_ref):
        g = pa_ref[...] + pb_ref[...]
        g_ref[...] = g
        d_ref[...], nm_ref[...], nv_ref[...] = _adamw_math(w_ref[...], g, m_ref[...], v_ref[...])

    return pl.pallas_call(
        body, name=name, grid=(4,), in_specs=[_rows(tr, c)] * 5, out_specs=[_rows(tr, c)] * 4,
        out_shape=[_sds((r, c), F32)] * 4, compiler_params=_cp("parallel"))(pa, pb, w, m, v)


def _adamw_small(w, g, m, v):
    def body(w_ref, g_ref, m_ref, v_ref, d_ref, nm_ref, nv_ref):
        d_ref[...], nm_ref[...], nv_ref[...] = _adamw_math(w_ref[...], g_ref[...], m_ref[...], v_ref[...])

    return pl.pallas_call(body, name="adamw_conv_w", out_shape=[_sds(w.shape, F32)] * 3,
                          compiler_params=_cp())(w, g, m, v)


def _ar_small(gp, wp, mp, vp):
    rows = gp.shape[0]

    def body(g_ref, w_ref, m_ref, v_ref, go_ref, d_ref, nm_ref, nv_ref, buf, send_sems, recv_sems):
        x, y, c = _mesh_pos()
        me = 4 * x + 2 * y + c

        def peer(k):
            px = 1 - x if k & 4 else x
            py = 1 - y if k & 2 else y
            pc = 1 - c if k & 1 else c
            return px, py, pc

        def remote(k, slot):
            return pltpu.make_async_remote_copy(
                src_ref=g_ref, dst_ref=buf.at[slot], send_sem=send_sems.at[k - 1], recv_sem=recv_sems.at[k - 1],
                device_id=peer(k), device_id_type=MESH_ID)

        for k in range(1, 8):
            remote(k, me).start()
        buf[me] = g_ref[...]
        for k in range(1, 8):
            px, py, pc = peer(k)
            remote(k, 4 * px + 2 * py + pc).wait_recv()
        for k in range(1, 8):
            remote(k, me).wait_send()
        tot = buf[0]
        for k in range(1, 8):
            tot = tot + buf[k]
        go_ref[...] = tot
        d_ref[...], nm_ref[...], nv_ref[...] = _adamw_math(w_ref[...], tot, m_ref[...], v_ref[...])

    return pl.pallas_call(
        body, name="ar_small", out_shape=[_sds((rows, 128), F32)] * 4,
        scratch_shapes=[pltpu.VMEM((8, rows, 128), F32), pltpu.SemaphoreType.DMA((7,)),
                        pltpu.SemaphoreType.DMA((7,))],
        compiler_params=_cp())(gp, wp, mp, vp)


def _pack(parts):
    flat = jnp.concatenate([parts[n].reshape(-1) for n in SMALL])
    rows = -(-flat.shape[0] // 1024) * 8
    return jnp.pad(flat, (0, rows * 128 - flat.shape[0])).reshape(rows, 128)


def _unpack(pack):
    flat = pack.reshape(-1)
    out, off = {}, 0
    for n in SMALL:
        size = math.prod(SMALL_SHAPES[n])
        out[n] = flat[off:off + size].reshape(SMALL_SHAPES[n])
        off += size
    return out


INPUT_NAMES = (['x', 'mem'] + WEIGHTS + ['loss_target'] + ['m_' + n for n in WEIGHTS] + ['v_' + n for n in WEIGHTS])


def kernel(x, mem, g_mix, w_in, rel_bias, conv_w, conv_b, w_rg, b_rg, w_ig, b_ig, lru_L, g_out_attn, g_out_lru, w_out, g_cross, g_mem, wq_c, wk_c, wv_c, wo_c, g_ffn, w_gate, w_up, w_down, g_final, loss_target, m_g_mix, m_w_in, m_rel_bias, m_conv_w, m_conv_b, m_w_rg, m_b_rg, m_w_ig, m_b_ig, m_lru_L, m_g_out_attn, m_g_out_lru, m_w_out, m_g_cross, m_g_mem, m_wq_c, m_wk_c, m_wv_c, m_wo_c, m_g_ffn, m_w_gate, m_w_up, m_w_down, m_g_final, v_g_mix, v_w_in, v_rel_bias, v_conv_w, v_conv_b, v_w_rg, v_b_rg, v_w_ig, v_b_ig, v_lru_L, v_g_out_attn, v_g_out_lru, v_w_out, v_g_cross, v_g_mem, v_wq_c, v_wk_c, v_wv_c, v_wo_c, v_g_ffn, v_w_gate, v_w_up, v_w_down, v_g_final):
    a = dict(zip(INPUT_NAMES, (x, mem, g_mix, w_in, rel_bias, conv_w, conv_b, w_rg, b_rg, w_ig, b_ig, lru_L, g_out_attn, g_out_lru, w_out, g_cross, g_mem, wq_c, wk_c, wv_c, wo_c, g_ffn, w_gate, w_up, w_down, g_final, loss_target, m_g_mix, m_w_in, m_rel_bias, m_conv_w, m_conv_b, m_w_rg, m_b_rg, m_w_ig, m_b_ig, m_lru_L, m_g_out_attn, m_g_out_lru, m_w_out, m_g_cross, m_g_mem, m_wq_c, m_wk_c, m_wv_c, m_wo_c, m_g_ffn, m_w_gate, m_w_up, m_w_down, m_g_final, v_g_mix, v_w_in, v_rel_bias, v_conv_w, v_conv_b, v_w_rg, v_b_rg, v_w_ig, v_b_ig, v_lru_L, v_g_out_attn, v_g_out_lru, v_w_out, v_g_cross, v_g_mem, v_wq_c, v_wk_c, v_wv_c, v_wo_c, v_g_ffn, v_w_gate, v_w_up, v_w_down, v_g_final)))
    chip = 2 * lax.axis_index("x") + lax.axis_index("y")

    shards = dict(zip(BIG, _cast_shards([a[n][0] for n in BIG])))
    w_in_g, conv_w_g = _comm_only("ag_w_in", ("ag", [shards['w_in'], a['conv_w'][0]]))
    conv_w_full = conv_w_g.transpose(1, 0, 2).reshape(4, D_LRU)

    p = {n: a[n] for n in SMALL}
    p['rel_bias'] = a['rel_bias'][0]
    p['w_rg'] = a['w_rg'][0]
    p['w_ig'] = a['w_ig'][0]
    p['conv_w'] = conv_w_full
    p['g_final'] = a['g_final'][None, :]
    loss_part, grad_x, small, big, recv = _local_step(
        a['x'][0], a['mem'][0], a['loss_target'][0], p, {'w_in': w_in_g}, shards)
    loss = lax.psum(loss_part, ("x", "y", "c"))

    recv['w_in'], = _comm_only("rs_w_in", ("rs", [_cast_slots(big['w_in'], "cast_dw_in")]))
    chip_arr = jnp.reshape(chip, (1,)).astype(jnp.int32)
    part = [_sum_parts(big[n], recv[n], chip_arr, "sum_" + n) for n in BIG]
    sib = _swap_sibling(part)
    out = {}
    for n, pa, pb in zip(BIG, part, sib):
        out[n] = _final_adamw(pa, pb, a[n][0], a['m_' + n][0], a['v_' + n][0], "adamw_" + n)

    zeros_cw = jnp.zeros((1, 4, D_LRU), F32)
    def packed(prefix):
        d = {n: a[prefix + n] for n in SMALL}
        d['conv_w'] = zeros_cw
        return _pack(d)
    packs = _ar_small(_pack(small), packed(''), packed('m_'), packed('v_'))
    sg, sd, sm, sv = [_unpack(pk) for pk in packs]
    g_cw = lax.dynamic_slice(sg['conv_w'][0], (0, chip * 128), (4, 128))
    d_cw, m_cw, v_cw = _adamw_small(a['conv_w'][0], g_cw, a['m_conv_w'][0], a['v_conv_w'][0])
    sg['conv_w'], sd['conv_w'], sm['conv_w'], sv['conv_w'] = g_cw[None], d_cw[None], m_cw[None], v_cw[None]

    def leaf(i, n):
        if n in BIG:
            return out[n][i][None]
        return (sg, sd, sm, sv)[i][n]

    return (loss, grad_x[None], *[leaf(i, n) for i in range(4) for n in WEIGHTS])
```

```python
import math

import jax
import jax.numpy as jnp
from jax import lax
from jax.experimental import pallas as pl
from jax.experimental.pallas import tpu as pltpu

F32 = jnp.float32
BF16 = jnp.bfloat16

D_MODEL = 1024
D_ATT = 512
D_LRU = 512
HEAD_DIM = 64
ATT_HEADS = 8
CHUNK = 64
LEFT_CHUNKS = 8
X_HEADS = 4
X_HEAD_DIM = 256
N_SHARD = 4
IN_SH = 640
FF_SH = 704
EPS = 1e-6
LRU_C = 8.0
QB = 256
KB = 768
ROLL_W = 1024
NEG = -1e30
ATT_SCALE = HEAD_DIM ** -0.5
X_SCALE = X_HEAD_DIM ** -0.5

ADAM_LR = 0.001
ADAM_B1 = 0.9
ADAM_B2 = 0.999
ADAM_EPS = 1e-08
ADAM_WD = 0.01
ADAM_STEP = 10

VMEM_LIMIT_V7X = 56 * 1024 * 1024
MESH_ID = pl.DeviceIdType.MESH

WEIGHTS = ['g_mix', 'w_in', 'rel_bias', 'conv_w', 'conv_b', 'w_rg', 'b_rg', 'w_ig', 'b_ig', 'lru_L',
           'g_out_attn', 'g_out_lru', 'w_out', 'g_cross', 'g_mem', 'wq_c', 'wk_c', 'wv_c', 'wo_c',
           'g_ffn', 'w_gate', 'w_up', 'w_down', 'g_final']
BIG = ['w_in', 'w_out', 'wq_c', 'wk_c', 'wv_c', 'wo_c', 'w_gate', 'w_up', 'w_down']
SMALL = [n for n in WEIGHTS if n not in BIG]
SMALL_SHAPES = {
    'g_mix': (1, 1024), 'rel_bias': (1, 8, 257), 'conv_w': (1, 4, 512), 'conv_b': (1, 512),
    'w_rg': (1, 8, 64, 64), 'b_rg': (1, 512), 'w_ig': (1, 8, 64, 64), 'b_ig': (1, 512), 'lru_L': (1, 512),
    'g_out_attn': (1, 512), 'g_out_lru': (1, 512), 'g_cross': (1, 1024), 'g_mem': (1, 1024),
    'g_ffn': (1, 1024), 'g_final': (1024,)}


def _sds(shape, dtype):
    return jax.ShapeDtypeStruct(shape, dtype)


def _cp(*sem):
    return pltpu.CompilerParams(dimension_semantics=sem or None, vmem_limit_bytes=VMEM_LIMIT_V7X)


def _rows(tm, n):
    return pl.BlockSpec((tm, n), lambda i: (i, 0))


def _full(shape):
    nd = len(shape)
    return pl.BlockSpec(shape, lambda i: (0,) * nd)


def _dot(a, b):
    return jnp.dot(a, b, preferred_element_type=F32)


def _dot_nt(a, b):
    return lax.dot_general(a, b, (((1,), (1,)), ((), ())), preferred_element_type=F32)


def _dot_tn(a, b):
    return lax.dot_general(a, b, (((0,), (0,)), ((), ())), preferred_element_type=F32)


def _rinv(x):
    return lax.rsqrt(jnp.mean(x * x, axis=-1, keepdims=True) + EPS)


def _rms_bwd(dy, x, g):
    r = _rinv(x)
    yh = x * r
    dyh = dy * g
    dx = r * (dyh - yh * jnp.mean(dyh * yh, axis=-1, keepdims=True))
    return dx, jnp.sum(dy * yh, axis=0, keepdims=True)


def _gelu(x):
    c = math.sqrt(2.0 / math.pi)
    t = jnp.tanh(c * (x + 0.044715 * x * x * x))
    return 0.5 * x * (1.0 + t)


def _gelu_and_grad(x):
    c = math.sqrt(2.0 / math.pi)
    t = jnp.tanh(c * (x + 0.044715 * x * x * x))
    g = 0.5 * x * (1.0 + t)
    dg = 0.5 * (1.0 + t) + 0.5 * x * (1.0 - t * t) * c * (1.0 + 3.0 * 0.044715 * x * x)
    return g, dg


def _neg_expm1(z):
    series = -z * (1 + z / 2 * (1 + z / 3 * (1 + z / 4 * (1 + z / 5 * (1 + z / 6 * (1 + z / 7))))))
    return jnp.where(z > -0.25, series, 1.0 - jnp.exp(z))


def _lru_gates(u, wrg, brg, wig, big, lam):
    ub = u.astype(BF16)
    r = jax.nn.sigmoid(_dot(ub, wrg) + brg)
    ig = jax.nn.sigmoid(_dot(ub, wig) + big)
    sp = jnp.maximum(-lam, 0.0) + jnp.log1p(jnp.exp(-jnp.abs(lam)))
    la = -LRU_C * r * sp
    a = jnp.exp(la)
    mult = jnp.sqrt(jnp.maximum(_neg_expm1(2.0 * la), 0.0))
    return ub, r, ig, sp, a, mult


def _scan8(a8, b8, hprev):
    row = lax.broadcasted_iota(jnp.int32, a8.shape, 0)
    aa, bb = a8, b8
    for d in (1, 2, 4):
        a_s = pltpu.roll(aa, d, 0)
        b_s = pltpu.roll(bb, d, 0)
        m = row >= d
        bb = jnp.where(m, aa * b_s + bb, bb)
        aa = jnp.where(m, aa * a_s, aa)
    return aa * hprev + bb


def _mesh_pos():
    return lax.axis_index("x"), lax.axis_index("y"), lax.axis_index("c")


def _other_chips(x, y):
    return [(1 - x, y), (x, 1 - y), (1 - x, 1 - y)]


def _ag_copies(ins, outs, sems):
    send_sems, recv_sems, loc_sems = sems
    n = len(ins)
    x, y, c = _mesh_pos()
    mine = 2 * x + y
    chips = _other_chips(x, y)

    def remote(k, j, slot):
        px, py = chips[j]
        return pltpu.make_async_remote_copy(
            src_ref=ins[k], dst_ref=outs[k].at[slot], send_sem=send_sems.at[k, j], recv_sem=recv_sems.at[k, j],
            device_id=(px, py, c), device_id_type=MESH_ID)

    def local(k):
        return pltpu.make_async_copy(ins[k], outs[k].at[mine], loc_sems.at[k])

    def start():
        for k in range(n):
            local(k).start()
            for j in range(3):
                remote(k, j, mine).start()

    def wait():
        for k in range(n):
            for j, (px, py) in enumerate(chips):
                remote(k, j, 2 * px + py).wait_recv()
        for k in range(n):
            for j in range(3):
                remote(k, j, mine).wait_send()
            local(k).wait()

    return start, wait


def _rs_copies(ins, outs, sems):
    send_sems, recv_sems = sems
    n = len(ins)
    x, y, c = _mesh_pos()
    chips = _other_chips(x, y)

    def remote(k, j):
        px, py = chips[j]
        return pltpu.make_async_remote_copy(
            src_ref=ins[k].at[2 * px + py], dst_ref=outs[k].at[j],
            send_sem=send_sems.at[k, j], recv_sem=recv_sems.at[k, j],
            device_id=(px, py, c), device_id_type=MESH_ID)

    def start():
        for k in range(n):
            for j in range(3):
                remote(k, j).start()

    def wait():
        for k in range(n):
            for j in range(3):
                remote(k, j).wait_recv()
        for k in range(n):
            for j in range(3):
                remote(k, j).wait_send()

    return start, wait


def _swap_copies(ins, outs, sems):
    send_sems, recv_sems = sems
    x, y, c = _mesh_pos()
    copies = [pltpu.make_async_remote_copy(
        src_ref=ins[k], dst_ref=outs[k], send_sem=send_sems.at[k], recv_sem=recv_sems.at[k],
        device_id=(x, y, 1 - c), device_id_type=MESH_ID) for k in range(len(ins))]

    def start():
        for cp in copies:
            cp.start()

    def wait():
        for cp in copies:
            cp.wait()

    return start, wait


def _comm_plan(groups):
    plan, arrs, shapes, sems = [], [], [], []
    for kind, group in groups:
        k = len(group)
        arrs += group
        if kind == "ag":
            shapes += [_sds((N_SHARD,) + w.shape, w.dtype) for w in group]
            gsems = [pltpu.SemaphoreType.DMA((k, 3)), pltpu.SemaphoreType.DMA((k, 3)), pltpu.SemaphoreType.DMA((k,))]
            maker = _ag_copies
        elif kind == "rs":
            shapes += [_sds((3,) + g.shape[1:], g.dtype) for g in group]
            gsems = [pltpu.SemaphoreType.DMA((k, 3)), pltpu.SemaphoreType.DMA((k, 3))]
            maker = _rs_copies
        else:
            shapes += [_sds(g.shape, g.dtype) for g in group]
            gsems = [pltpu.SemaphoreType.DMA((k,)), pltpu.SemaphoreType.DMA((k,))]
            maker = _swap_copies
        plan.append((maker, k, len(gsems)))
        sems += gsems
    return plan, arrs, shapes, sems


def _comm_fns(plan, cins, couts, sems):
    fns, a, s = [], 0, 0
    for maker, k, ns in plan:
        fns.append(maker(cins[a:a + k], couts[a:a + k], sems[s:s + ns]))
        a += k
        s += ns

    def start():
        for st, _ in fns:
            st()

    def wait():
        for _, wt in fns:
            wt()

    return start, wait


def _call(body, name, grid, in_specs, out_specs, out_shape, scratch, args, sem, comm=None):
    if not comm:
        return pl.pallas_call(body, name=name, grid=grid, in_specs=in_specs, out_specs=out_specs,
                              out_shape=out_shape, scratch_shapes=scratch, compiler_params=_cp(sem))(*args)
    plan, c_arrs, c_shapes, c_sems = _comm_plan(comm)
    k = len(c_arrs)
    n_in, n_out, n_scr = len(in_specs), len(out_specs), len(scratch)
    last = grid[0] - 1

    def wrapped(*refs):
        ins, cins = refs[:n_in], refs[n_in:n_in + k]
        o0 = n_in + k
        outs, couts = refs[o0:o0 + n_out], refs[o0 + n_out:o0 + n_out + k]
        s0 = o0 + n_out + k
        start, wait = _comm_fns(plan, cins, couts, refs[s0 + n_scr:])
        pl.when(pl.program_id(0) == 0)(start)
        body(*ins, *outs, *refs[s0:s0 + n_scr])
        pl.when(pl.program_id(0) == last)(wait)

    return pl.pallas_call(
        wrapped, name=name, grid=grid, in_specs=list(in_specs) + [_any()] * k,
        out_specs=list(out_specs) + [_any()] * k, out_shape=list(out_shape) + c_shapes,
        scratch_shapes=list(scratch) + c_sems, compiler_params=_cp(sem))(*args, *c_arrs)


def _comm_only(name, comm):
    plan, c_arrs, c_shapes, c_sems = _comm_plan(comm)
    k = len(c_arrs)

    def body(*refs):
        start, wait = _comm_fns(plan, refs[:k], refs[k:2 * k], refs[2 * k:])
        start()
        wait()

    return pl.pallas_call(body, name=name, in_specs=[_any()] * k, out_specs=[_any()] * k, out_shape=c_shapes,
                          scratch_shapes=c_sems, compiler_params=_cp())(*c_arrs)


def _any():
    return pl.BlockSpec(memory_space=pl.ANY)


def _rscan8(c8, d8, lnext):
    row = lax.broadcasted_iota(jnp.int32, c8.shape, 0)
    cc, dd = c8, d8
    for d in (1, 2, 4):
        c_s = pltpu.roll(cc, 8 - d, 0)
        d_s = pltpu.roll(dd, 8 - d, 0)
        m = row < 8 - d
        dd = jnp.where(m, cc * d_s + dd, dd)
        cc = jnp.where(m, cc * c_s, cc)
    return cc * lnext + dd


def _f_inproj(x, g_mix, w_in_g, tm, comm=None):
    s_len = x.shape[0]

    def body(x_ref, g_ref, w_ref, h_ref, qkv_ref, xg_ref):
        xv = x_ref[...]
        h = (xv * _rinv(xv) * g_ref[...]).astype(BF16)
        h_ref[...] = h
        qkv_ref[:, 0:640] = _dot(h, w_ref[0]).astype(BF16)
        qkv_ref[:, 640:1280] = _dot(h, w_ref[1]).astype(BF16)
        p2 = _dot(h, w_ref[2])
        qkv_ref[:, 1280:1536] = p2[:, 0:256].astype(BF16)
        xg_ref[:, 0:384] = p2[:, 256:640]
        xg_ref[:, 384:1024] = _dot(h, w_ref[3])

    return _call(
        body, "f_inproj", (s_len // tm,),
        [_rows(tm, 1024), _full((1, 1024)), _full((N_SHARD, 1024, IN_SH))],
        [_rows(tm, 1024), _rows(tm, 1536), _rows(tm, 1024)],
        [_sds((s_len, 1024), BF16), _sds((s_len, 1536), BF16), _sds((s_len, 1024), F32)],
        [], (x, g_mix, w_in_g), "arbitrary", comm)


def _bias_table(frow_ref, bias_sc):
    qa = lax.broadcasted_iota(jnp.int32, (QB, KB), 0) // CHUNK
    kb = lax.broadcasted_iota(jnp.int32, (QB, KB), 1) // CHUNK
    band = jnp.where((kb >= qa) & (kb - qa <= LEFT_CHUNKS), 0.0, NEG).astype(F32)
    for h in range(ATT_HEADS):
        row = jnp.broadcast_to(frow_ref[h:h + 1, :], (QB, ROLL_W))
        toep = pltpu.roll(row, 0, 1, stride=1, stride_axis=0)
        bias_sc[h] = toep[:, 0:KB] + band


def _att_scores(q, ks, bias, padmask):
    s = jnp.concatenate([_dot_nt(q, k) for k in ks], axis=1) * ATT_SCALE + bias + padmask
    m = jnp.max(s, axis=-1, keepdims=True)
    p = jnp.exp(s - m)
    return p, jnp.sum(p, axis=-1, keepdims=True)


def _pad_mask(blk):
    kpos = blk * QB - LEFT_CHUNKS * CHUNK + lax.broadcasted_iota(jnp.int32, (QB, KB), 1)
    return jnp.where(kpos >= 0, 0.0, NEG).astype(F32)


def _att_in_specs(clamp):
    def spec(j, col):
        return pl.BlockSpec((QB, D_ATT), lambda i: (clamp(i) + j, col))
    return [spec(2, 0), spec(0, 1), spec(1, 1), spec(2, 1), spec(0, 2), spec(1, 2), spec(2, 2)]


def _f_attn(qkv_pad, frow, comm=None):
    s_len = qkv_pad.shape[0] - LEFT_CHUNKS * CHUNK
    nb = s_len // QB

    def body(q_ref, k0, k1, k2, v0, v1, v2, frow_ref, o_ref, bias_sc):
        i = pl.program_id(0)

        @pl.when(i == 0)
        def _():
            _bias_table(frow_ref, bias_sc)

        padmask = _pad_mask(i)
        for h in range(ATT_HEADS):
            sl = slice(h * HEAD_DIM, (h + 1) * HEAD_DIM)
            p, l = _att_scores(q_ref[:, sl], [k0[:, sl], k1[:, sl], k2[:, sl]], bias_sc[h], padmask)
            pb = p.astype(BF16)
            o = (_dot(pb[:, 0:QB], v0[:, sl]) + _dot(pb[:, QB:2 * QB], v1[:, sl])
                 + _dot(pb[:, 2 * QB:3 * QB], v2[:, sl]))
            o_ref[:, sl] = o / l

    return _call(
        body, "f_attn", (nb,),
        _att_in_specs(lambda i: i) + [_full((ATT_HEADS, ROLL_W))],
        [_rows(QB, D_ATT)], [_sds((s_len, D_ATT), F32)],
        [pltpu.VMEM((ATT_HEADS, QB, KB), F32)], (*([qkv_pad] * 7), frow), "arbitrary", comm)


def _f_lru(xg, conv_w, conv_b, wrg, brg, wig, big, lam, tl, comm=None):
    s_len = xg.shape[0]

    def body(xg_ref, cw_ref, cb_ref, wrg_ref, brg_ref, wig_ref, big_ref, l_ref,
             rec_ref, u_ref, hs_ref, xbuf, a_sc, b_sc, hcar):
        i = pl.program_id(0)

        @pl.when(i == 0)
        def _():
            xbuf[0:8, :] = jnp.zeros((8, D_LRU), F32)
            hcar[...] = jnp.zeros((8, D_LRU), F32)

        xu0 = xg_ref[:, 0:D_LRU]
        xbuf[8:8 + tl, :] = xu0
        u = cb_ref[...] + cw_ref[0:1, :] * xbuf[pl.ds(5, tl), :]
        for j in range(1, 4):
            u = u + cw_ref[j:j + 1, :] * xbuf[pl.ds(5 + j, tl), :]
        xbuf[0:8, :] = xu0[tl - 8:tl, :]
        u_ref[...] = u
        _, _, ig, _, a, mult = _lru_gates(u, wrg_ref[...], brg_ref[...], wig_ref[...], big_ref[...], l_ref[...])
        a_sc[...] = a
        b_sc[...] = mult * (ig * u)

        def grp(g, hprev):
            off = pl.multiple_of(g * 8, 8)
            h8 = _scan8(a_sc[pl.ds(off, 8), :], b_sc[pl.ds(off, 8), :], hprev)
            hs_ref[pl.ds(off, 8), :] = h8
            return h8[7:8, :]

        hcar[0:1, :] = lax.fori_loop(0, tl // 8, grp, hcar[0:1, :])
        rec_ref[...] = hs_ref[...] * _gelu(xg_ref[:, D_LRU:2 * D_LRU])

    vec = _full((1, D_LRU))
    return _call(
        body, "f_lru", (s_len // tl,),
        [_rows(tl, 1024), _full((4, D_LRU)), vec, _full((D_LRU, D_LRU)), vec, _full((D_LRU, D_LRU)), vec, vec],
        [_rows(tl, D_LRU)] * 3, [_sds((s_len, D_LRU), F32)] * 3,
        [pltpu.VMEM((tl + 8, D_LRU), F32), pltpu.VMEM((tl, D_LRU), F32),
         pltpu.VMEM((tl, D_LRU), F32), pltpu.VMEM((8, D_LRU), F32)],
        (xg, conv_w, conv_b, wrg, brg, wig, big, lam), "arbitrary", comm)


def _f_mem(mem, g_mem, wk, wv):
    def body(mem_ref, g_ref, wk_ref, wv_ref, mn_ref, kx_ref, vx_ref):
        mv = mem_ref[...]
        mn = (mv * _rinv(mv) * g_ref[...]).astype(BF16)
        mn_ref[...] = mn
        kx_ref[...] = _dot(mn, wk_ref[...]).astype(BF16)
        vx_ref[...] = _dot(mn, wv_ref[...]).astype(BF16)

    m = mem.shape[0]
    return pl.pallas_call(
        body, name="f_mem", out_shape=[_sds((m, 1024), BF16)] * 3,
        compiler_params=_cp())(mem, g_mem, wk, wv)


def _xattn_probs(q, k):
    s = _dot_nt(q, k) * X_SCALE
    m = jnp.max(s, axis=-1, keepdims=True)
    p = jnp.exp(s - m)
    return p, jnp.sum(p, axis=-1, keepdims=True)


def _f_mid(x, att, rec, g_oa, g_ol, w_out, g_cross, wq, kx, vx, wo, tm, comm=None):
    s_len = x.shape[0]
    m_len = kx.shape[0]

    def body(x_ref, att_ref, rec_ref, goa_ref, gol_ref, wout_ref, gc_ref, wq_ref, kx_ref, vx_ref, wo_ref,
             mg_ref, x1_ref, hc_ref, qx_ref, ox_ref, x2_ref):
        av = att_ref[...]
        rv = rec_ref[...]
        mg_ref[:, 0:D_ATT] = (av * _rinv(av) * goa_ref[...]).astype(BF16)
        mg_ref[:, D_ATT:1024] = (rv * _rinv(rv) * gol_ref[...]).astype(BF16)
        x1 = x_ref[...] + _dot(mg_ref[...], wout_ref[...])
        x1_ref[...] = x1
        hc = (x1 * _rinv(x1) * gc_ref[...]).astype(BF16)
        hc_ref[...] = hc
        qx_ref[...] = _dot(hc, wq_ref[...]).astype(BF16)
        for h in range(X_HEADS):
            sl = slice(h * X_HEAD_DIM, (h + 1) * X_HEAD_DIM)
            p, l = _xattn_probs(qx_ref[:, sl], kx_ref[:, sl])
            ox_ref[:, sl] = (_dot(p.astype(BF16), vx_ref[:, sl]) / l).astype(BF16)
        x2_ref[...] = x1 + _dot(ox_ref[...], wo_ref[...])

    sq = _full((1024, 1024))
    return _call(
        body, "f_mid", (s_len // tm,),
        [_rows(tm, 1024), _rows(tm, 512), _rows(tm, 512), _full((1, 512)), _full((1, 512)), sq,
         _full((1, 1024)), sq, _full((m_len, 1024)), _full((m_len, 1024)), sq],
        [_rows(tm, 1024)] * 6,
        [_sds((s_len, 1024), BF16), _sds((s_len, 1024), F32), _sds((s_len, 1024), BF16),
         _sds((s_len, 1024), BF16), _sds((s_len, 1024), BF16), _sds((s_len, 1024), F32)],
        [], (x, att, rec, g_oa, g_ol, w_out, g_cross, wq, kx, vx, wo), "arbitrary", comm)


def _load_weights_once(pairs):
    @pl.when(pl.program_id(0) == 0)
    def _():
        for hbm, vmem in pairs:
            pltpu.sync_copy(hbm, vmem)


def _sh_rows(tm, n):
    return pl.BlockSpec((N_SHARD, tm, n), lambda i: (0, i, 0))


def _f_ffn(x2, tgt, g_ffn, g_final, wg, wu, wd, tm):
    s_len = x2.shape[0]

    def body(x2_ref, t_ref, gf_ref, gfin_ref, wg_hbm, wu_hbm, wd_hbm,
             hf_ref, g_ref, u_ref, a_ref, dx3_ref, loss_ref, dgfin_ref, wg_ref, wu_ref, wd_ref):
        _load_weights_once([(wg_hbm, wg_ref), (wu_hbm, wu_ref), (wd_hbm, wd_ref)])

        @pl.when(pl.program_id(0) == 0)
        def _():
            loss_ref[...] = jnp.zeros_like(loss_ref)
            dgfin_ref[...] = jnp.zeros_like(dgfin_ref)

        x2v = x2_ref[...]
        hf = (x2v * _rinv(x2v) * gf_ref[...]).astype(BF16)
        hf_ref[...] = hf
        x3 = x2v
        for s in range(N_SHARD):
            gv = _dot(hf, wg_ref[s])
            uv = _dot(hf, wu_ref[s])
            av = (gv * jax.nn.sigmoid(gv) * uv).astype(BF16)
            g_ref[s] = gv.astype(BF16)
            u_ref[s] = uv.astype(BF16)
            a_ref[s] = av
            x3 = x3 + _dot(av, wd_ref[s])
        r3 = _rinv(x3)
        yh = x3 * r3
        gfin = gfin_ref[...]
        err = yh * gfin - t_ref[...]
        loss_ref[...] += jnp.full((1, 128), 0.5 / D_MODEL, F32) * jnp.sum(err * err)
        dy = err * (1.0 / D_MODEL)
        dgfin_ref[...] += jnp.sum(dy * yh, axis=0, keepdims=True)
        dyh = dy * gfin
        dx3_ref[...] = r3 * (dyh - yh * jnp.mean(dyh * yh, axis=-1, keepdims=True))

    vec = _full((1, 1024))
    return pl.pallas_call(
        body, name="f_ffn", grid=(s_len // tm,),
        in_specs=[_rows(tm, 1024), _rows(tm, 1024), vec, vec, _any(), _any(), _any()],
        out_specs=[_rows(tm, 1024), _sh_rows(tm, FF_SH), _sh_rows(tm, FF_SH), _sh_rows(tm, FF_SH),
                   _rows(tm, 1024), _full((1, 128)), vec],
        out_shape=[_sds((s_len, 1024), BF16)] + [_sds((N_SHARD, s_len, FF_SH), BF16)] * 3
                  + [_sds((s_len, 1024), F32), _sds((1, 128), F32), _sds((1, 1024), F32)],
        scratch_shapes=[pltpu.VMEM((N_SHARD, 1024, FF_SH), BF16), pltpu.VMEM((N_SHARD, 1024, FF_SH), BF16),
                        pltpu.VMEM((N_SHARD, FF_SH, 1024), BF16)],
        compiler_params=_cp("arbitrary"))(x2, tgt, g_ffn, g_final, wg, wu, wd)


def _b_ffn(dx3, x2, gact, uact, g_ffn, wg, wu, wd, tm):
    s_len = x2.shape[0]

    def body(dx3_ref, x2_ref, g_ref, u_ref, gf_ref, wg_hbm, wu_hbm, wd_hbm,
             dg_ref, du_ref, dx2_ref, dgf_ref, wg_ref, wu_ref, wd_ref):
        _load_weights_once([(wg_hbm, wg_ref), (wu_hbm, wu_ref), (wd_hbm, wd_ref)])

        @pl.when(pl.program_id(0) == 0)
        def _():
            dgf_ref[...] = jnp.zeros_like(dgf_ref)

        dx3v = dx3_ref[...]
        dx3b = dx3v.astype(BF16)
        dhf = jnp.zeros(dx3v.shape, F32)
        for s in range(N_SHARD):
            da = _dot_nt(dx3b, wd_ref[s])
            gv = g_ref[s].astype(F32)
            uv = u_ref[s].astype(F32)
            sg = jax.nn.sigmoid(gv)
            dub = (da * gv * sg).astype(BF16)
            dgb = (da * uv * (sg * (1.0 + gv * (1.0 - sg)))).astype(BF16)
            du_ref[s] = dub
            dg_ref[s] = dgb
            dhf = dhf + _dot_nt(dgb, wg_ref[s]) + _dot_nt(dub, wu_ref[s])
        dx, dgf = _rms_bwd(dhf, x2_ref[...], gf_ref[...])
        dx2_ref[...] = dx3v + dx
        dgf_ref[...] += dgf

    vec = _full((1, 1024))
    return pl.pallas_call(
        body, name="b_ffn", grid=(s_len // tm,),
        in_specs=[_rows(tm, 1024), _rows(tm, 1024), _sh_rows(tm, FF_SH), _sh_rows(tm, FF_SH), vec,
                  _any(), _any(), _any()],
        out_specs=[_sh_rows(tm, FF_SH), _sh_rows(tm, FF_SH), _rows(tm, 1024), vec],
        out_shape=[_sds((N_SHARD, s_len, FF_SH), BF16)] * 2 + [_sds((s_len, 1024), F32), _sds((1, 1024), F32)],
        scratch_shapes=[pltpu.VMEM((N_SHARD, 1024, FF_SH), BF16), pltpu.VMEM((N_SHARD, 1024, FF_SH), BF16),
                        pltpu.VMEM((N_SHARD, FF_SH, 1024), BF16)],
        compiler_params=_cp("arbitrary"))(dx3, x2, gact, uact, g_ffn, wg, wu, wd)


def _b_mid(dx2, qx, x1, att, rec, kx, vx, wo, wq, w_out, g_cross, g_oa, g_ol, tm, comm=None):
    s_len = x1.shape[0]
    m_len = kx.shape[0]

    def body(dx2_ref, qx_ref, x1_ref, att_ref, rec_ref, kx_ref, vx_ref, wo_ref, wq_ref, wout_ref,
             gc_ref, goa_ref, gol_ref,
             dqx_ref, dx1_ref, datt_ref, drec_ref, dkx_ref, dvx_ref, dgc_ref, dgoa_ref, dgol_ref):
        @pl.when(pl.program_id(0) == 0)
        def _():
            for r in (dkx_ref, dvx_ref, dgc_ref, dgoa_ref, dgol_ref):
                r[...] = jnp.zeros_like(r)

        dx2v = dx2_ref[...]
        dox = _dot_nt(dx2v.astype(BF16), wo_ref[...])
        for h in range(X_HEADS):
            sl = slice(h * X_HEAD_DIM, (h + 1) * X_HEAD_DIM)
            q = qx_ref[:, sl]
            p, l = _xattn_probs(q, kx_ref[:, sl])
            pn = p / l
            dob = dox[:, sl].astype(BF16)
            dp = _dot_nt(dob, vx_ref[:, sl])
            dvx_ref[:, sl] += _dot_tn(pn.astype(BF16), dob)
            ds = pn * (dp - jnp.sum(dp * pn, axis=-1, keepdims=True))
            dsb = (ds * X_SCALE).astype(BF16)
            dqx_ref[:, sl] = _dot(dsb, kx_ref[:, sl]).astype(BF16)
            dkx_ref[:, sl] += _dot_tn(dsb, q)
        dhc = _dot_nt(dqx_ref[...], wq_ref[...])
        dx, dgc = _rms_bwd(dhc, x1_ref[...], gc_ref[...])
        dx1 = dx2v + dx
        dx1_ref[...] = dx1
        dgc_ref[...] += dgc
        dmg = _dot_nt(dx1.astype(BF16), wout_ref[...])
        da, dgoa = _rms_bwd(dmg[:, 0:D_ATT], att_ref[...], goa_ref[...])
        datt_ref[...] = da
        dgoa_ref[...] += dgoa
        dr, dgol = _rms_bwd(dmg[:, D_ATT:1024], rec_ref[...], gol_ref[...])
        drec_ref[...] = dr
        dgol_ref[...] += dgol

    sq = _full((1024, 1024))
    mk = _full((m_len, 1024))
    return _call(
        body, "b_mid", (s_len // tm,),
        [_rows(tm, 1024), _rows(tm, 1024), _rows(tm, 1024), _rows(tm, 512), _rows(tm, 512), mk, mk,
         sq, sq, sq, _full((1, 1024)), _full((1, 512)), _full((1, 512))],
        [_rows(tm, 1024), _rows(tm, 1024), _rows(tm, 512), _rows(tm, 512), mk, mk,
         _full((1, 1024)), _full((1, 512)), _full((1, 512))],
        [_sds((s_len, 1024), BF16), _sds((s_len, 1024), F32), _sds((s_len, 512), F32),
         _sds((s_len, 512), F32), _sds((m_len, 1024), F32), _sds((m_len, 1024), F32),
         _sds((1, 1024), F32), _sds((1, 512), F32), _sds((1, 512), F32)],
        [], (dx2, qx, x1, att, rec, kx, vx, wo, wq, w_out, g_cross, g_oa, g_ol), "arbitrary", comm)


def _b_mem(dkx, dvx, mem, mn, g_mem, wk, wv):
    def body(dkx_ref, dvx_ref, mem_ref, mn_ref, g_ref, wk_ref, wv_ref, dwk_ref, dwv_ref, dgm_ref,
             dwkb_ref, dwvb_ref):
        dkb = dkx_ref[...].astype(BF16)
        dvb = dvx_ref[...].astype(BF16)
        dwk = _dot_tn(mn_ref[...], dkb)
        dwv = _dot_tn(mn_ref[...], dvb)
        dwk_ref[...] = dwk
        dwv_ref[...] = dwv
        dwkb_ref[...] = dwk.astype(BF16)
        dwvb_ref[...] = dwv.astype(BF16)
        dmn = _dot_nt(dkb, wk_ref[...]) + _dot_nt(dvb, wv_ref[...])
        mv = mem_ref[...]
        dgm_ref[...] = jnp.sum(dmn * (mv * _rinv(mv)), axis=0, keepdims=True)

    return pl.pallas_call(
        body, name="b_mem",
        out_shape=[_sds((1024, 1024), F32), _sds((1024, 1024), F32), _sds((1, 1024), F32),
                   _sds((1024, 1024), BF16), _sds((1024, 1024), BF16)],
        compiler_params=_cp())(dkx, dvx, mem, mn, g_mem, wk, wv)


def _b_lru(drec, hs, u, xg, conv_w, wrg, brg, wig, big, lam, tl, comm=None):
    s_len = xg.shape[0]
    nt = s_len // tl

    def body(drec_ref, hs_ref, hsp_ref, u_ref, xg_ref, cw_ref, wrg_ref, brg_ref, wig_ref, big_ref, l_ref,
             dxg_ref, dwrg_ref, dwig_ref, dbrg_ref, dbig_ref, dlam_ref, dcw_ref, dcb_ref,
             hbuf, abuf, dubuf, c_sc, d_sc, lam_sc, lcar):
        i = pl.program_id(0)
        tt = nt - 1 - i

        @pl.when(i == 0)
        def _():
            for r in (dwrg_ref, dwig_ref, dbrg_ref, dbig_ref, dlam_ref, dcw_ref, dcb_ref):
                r[...] = jnp.zeros_like(r)
            abuf[tl:tl + 8, :] = jnp.zeros((8, D_LRU), F32)
            dubuf[tl:tl + 8, :] = jnp.zeros((8, D_LRU), F32)
            lcar[...] = jnp.zeros((8, D_LRU), F32)

        xu0 = xg_ref[:, 0:D_LRU]
        hsv = hs_ref[...]
        uv = u_ref[...]
        hbuf[8:8 + tl, :] = hsv
        hbuf[0:8, :] = jnp.where(tt > 0, hsp_ref[...], 0.0)
        hshift = hbuf[pl.ds(7, tl), :]
        wrg_v = wrg_ref[...]
        wig_v = wig_ref[...]
        lamv = l_ref[...]
        ub, r, ig, sp, a, mult = _lru_gates(uv, wrg_v, brg_ref[...], wig_v, big_ref[...], lamv)
        abuf[0:tl, :] = a
        c_sc[...] = abuf[pl.ds(1, tl), :]
        gel, dgel = _gelu_and_grad(xg_ref[:, D_LRU:2 * D_LRU])
        drv = drec_ref[...]
        d_sc[...] = drv * gel
        dxg_ref[:, D_LRU:2 * D_LRU] = (drv * hsv * dgel).astype(BF16)

        def grp(k, lnext):
            off = pl.multiple_of((tl // 8 - 1 - k) * 8, 8)
            l8 = _rscan8(c_sc[pl.ds(off, 8), :], d_sc[pl.ds(off, 8), :], lnext)
            lam_sc[pl.ds(off, 8), :] = l8
            return l8[0:1, :]

        lcar[0:1, :] = lax.fori_loop(0, tl // 8, grp, lcar[0:1, :])
        abuf[tl:tl + 8, :] = a[0:8, :]
        db = lam_sc[...]
        da = db * hshift
        dmult = db * (ig * uv)
        dig = db * mult * uv
        du = db * mult * ig
        dla = da * a - dmult * (a * a) / mult
        dlam_ref[...] += jnp.sum(dla * (-LRU_C) * r, axis=0, keepdims=True)
        dzr = dla * (-LRU_C * sp) * r * (1.0 - r)
        dzi = dig * ig * (1.0 - ig)
        dzrb = dzr.astype(BF16)
        dzib = dzi.astype(BF16)
        du = du + _dot_nt(dzrb, wrg_v) + _dot_nt(dzib, wig_v)
        dwrg_ref[...] += _dot_tn(ub, dzrb)
        dwig_ref[...] += _dot_tn(ub, dzib)
        dbrg_ref[...] += jnp.sum(dzr, axis=0, keepdims=True)
        dbig_ref[...] += jnp.sum(dzi, axis=0, keepdims=True)
        dcb_ref[...] += jnp.sum(du, axis=0, keepdims=True)
        dubuf[0:tl, :] = du
        dxu0 = jnp.zeros((tl, D_LRU), F32)
        for j in range(4):
            dsh = dubuf[pl.ds(3 - j, tl), :]
            dxu0 = dxu0 + cw_ref[j:j + 1, :] * dsh
            dcw_ref[j:j + 1, :] += jnp.sum(xu0 * dsh, axis=0, keepdims=True)
        dubuf[tl:tl + 8, :] = du[0:8, :]
        dxg_ref[:, 0:D_LRU] = dxu0.astype(BF16)

        @pl.when(i == nt - 1)
        def _():
            dlam_ref[...] = dlam_ref[...] * (-jax.nn.sigmoid(-lamv))

    def rev(n):
        return pl.BlockSpec((tl, n), lambda i: (nt - 1 - i, 0))

    prev8 = pl.BlockSpec((8, D_LRU), lambda i: (jnp.maximum((nt - 1 - i) * (tl // 8) - 1, 0), 0))
    vec = _full((1, D_LRU))
    sq = _full((D_LRU, D_LRU))
    return _call(
        body, "b_lru", (nt,),
        [rev(D_LRU), rev(D_LRU), prev8, rev(D_LRU), rev(1024), _full((4, D_LRU)), sq, vec, sq, vec, vec],
        [rev(1024), sq, sq, vec, vec, vec, _full((4, D_LRU)), vec],
        [_sds((s_len, 1024), BF16), _sds((D_LRU, D_LRU), F32), _sds((D_LRU, D_LRU), F32),
         _sds((1, D_LRU), F32), _sds((1, D_LRU), F32), _sds((1, D_LRU), F32),
         _sds((4, D_LRU), F32), _sds((1, D_LRU), F32)],
        [pltpu.VMEM((tl + 8, D_LRU), F32)] * 3 + [pltpu.VMEM((tl, D_LRU), F32)] * 3
        + [pltpu.VMEM((8, D_LRU), F32)],
        (drec, hs, hs, u, xg, conv_w, wrg, brg, wig, big, lam), "arbitrary", comm)


def _b_attn(qkv_pad, datt, frow, comm=None):
    s_len = datt.shape[0]
    nb = s_len // QB

    def body(q_ref, k0, k1, k2, v0, v1, v2, do_ref, frow_ref, dq_ref, dkv_ref, dfrow_ref,
             bias_sc, dt_sc, acc_sc):
        t = pl.program_id(0)

        @pl.when(t == 0)
        def _():
            _bias_table(frow_ref, bias_sc)
            dt_sc[...] = jnp.zeros_like(dt_sc)
            acc_sc[...] = jnp.zeros_like(acc_sc)

        @pl.when(t < nb)
        def _():
            padmask = _pad_mask(t)
            for h in range(ATT_HEADS):
                sl = slice(h * HEAD_DIM, (h + 1) * HEAD_DIM)
                q = q_ref[:, sl]
                ks = [k0[:, sl], k1[:, sl], k2[:, sl]]
                vs = [v0[:, sl], v1[:, sl], v2[:, sl]]
                p, l = _att_scores(q, ks, bias_sc[h], padmask)
                pn = p / l
                dob = do_ref[:, sl].astype(BF16)
                dp = jnp.concatenate([_dot_nt(dob, v) for v in vs], axis=1)
                ds = pn * (dp - jnp.sum(dp * pn, axis=-1, keepdims=True))
                dt_sc[h] += ds
                dsb = (ds * ATT_SCALE).astype(BF16)
                pnb = pn.astype(BF16)
                dq = _dot(dsb[:, 0:QB], ks[0])
                for j in (1, 2):
                    dq = dq + _dot(dsb[:, j * QB:(j + 1) * QB], ks[j])
                dq_ref[:, sl] = dq.astype(BF16)
                for j in range(3):
                    slot = (t + 1 + j) % 3
                    cs = slice(j * QB, (j + 1) * QB)
                    acc_sc[slot, :, sl] += _dot_tn(dsb[:, cs], q)
                    acc_sc[slot, :, D_ATT + h * HEAD_DIM:D_ATT + (h + 1) * HEAD_DIM] += _dot_tn(pnb[:, cs], dob)

        done = (t + 1) % 3

        @pl.when(t >= 2)
        def _():
            dkv_ref[...] = acc_sc[done].astype(BF16)

        acc_sc[done] = jnp.zeros((QB, 2 * D_ATT), F32)

        @pl.when(t == nb + 1)
        def _():
            row = lax.broadcasted_iota(jnp.int32, (8, ROLL_W), 0)
            pad = jnp.zeros((8, ROLL_W - KB), F32)
            for h in range(ATT_HEADS):
                acc8 = jnp.concatenate([dt_sc[h, 0:8, :], pad], axis=1)
                for a1 in range(1, QB // 8):
                    blk = jnp.concatenate([dt_sc[h, 8 * a1:8 * a1 + 8, :], pad], axis=1)
                    acc8 = acc8 + pltpu.roll(blk, ROLL_W - 8 * a1, 1)
                for k in range(3):
                    acc8 = jnp.where(((row >> k) & 1) == 1, pltpu.roll(acc8, ROLL_W - (1 << k), 1), acc8)
                dfrow_ref[h:h + 1, :] = jnp.sum(acc8, axis=0, keepdims=True)

    clamp = lambda t: jnp.minimum(t, nb - 1)
    return _call(
        body, "b_attn", (nb + 2,),
        _att_in_specs(clamp) + [pl.BlockSpec((QB, D_ATT), lambda t: (clamp(t), 0)), _full((ATT_HEADS, ROLL_W))],
        [pl.BlockSpec((QB, D_ATT), lambda t: (clamp(t), 0)),
         pl.BlockSpec((QB, 2 * D_ATT), lambda t: (jnp.maximum(t - 2, 0), 0)),
         _full((ATT_HEADS, ROLL_W))],
        [_sds((s_len, D_ATT), BF16), _sds((s_len, 2 * D_ATT), BF16), _sds((ATT_HEADS, ROLL_W), F32)],
        [pltpu.VMEM((ATT_HEADS, QB, KB), F32), pltpu.VMEM((ATT_HEADS, QB, KB), F32),
         pltpu.VMEM((3, QB, 2 * D_ATT), F32)],
        (*([qkv_pad] * 7), datt, frow), "arbitrary", comm)


def _b_win(dq, dkv, dxg, h, ts):
    s_len = h.shape[0]
    steps = s_len // ts

    def body(dq_ref, dkv_ref, dxg_ref, h_ref, dw_ref, dwb_ref):
        @pl.when(pl.program_id(0) == 0)
        def _():
            dw_ref[...] = jnp.zeros_like(dw_ref)

        dproj = jnp.concatenate([dq_ref[...], dkv_ref[...], dxg_ref[...]], axis=1)
        hv = h_ref[...]
        for s in range(N_SHARD):
            dw_ref[s] += _dot_tn(hv, dproj[:, s * IN_SH:(s + 1) * IN_SH])

        @pl.when(pl.program_id(0) == steps - 1)
        def _():
            dwb_ref[...] = dw_ref[...].astype(BF16)

    wspec = _full((N_SHARD, 1024, IN_SH))
    return pl.pallas_call(
        body, name="b_win", grid=(steps,),
        in_specs=[_rows(ts, 512), _rows(ts, 1024), _rows(ts, 1024), _rows(ts, 1024)],
        out_specs=[wspec, wspec],
        out_shape=[_sds((N_SHARD, 1024, IN_SH), F32), _sds((N_SHARD, 1024, IN_SH), BF16)],
        compiler_params=_cp("arbitrary"))(dq, dkv, dxg, h)


def _b_inproj(dq, dkv, dxg, x, dx1, g_mix, w_in_g, tm, comm=None):
    s_len = x.shape[0]

    def body(dq_ref, dkv_ref, dxg_ref, x_ref, dx1_ref, g_ref, w_ref, gx_ref, dgm_ref):
        @pl.when(pl.program_id(0) == 0)
        def _():
            dgm_ref[...] = jnp.zeros_like(dgm_ref)

        dproj = jnp.concatenate([dq_ref[...], dkv_ref[...], dxg_ref[...]], axis=1)
        dh = jnp.zeros((tm, 1024), F32)
        for s in range(N_SHARD):
            dh = dh + _dot_nt(dproj[:, s * IN_SH:(s + 1) * IN_SH], w_ref[s])
        dx, dgm = _rms_bwd(dh, x_ref[...], g_ref[...])
        gx_ref[...] = dx1_ref[...] + dx
        dgm_ref[...] += dgm

    return _call(
        body, "b_inproj", (s_len // tm,),
        [_rows(tm, 512), _rows(tm, 1024), _rows(tm, 1024), _rows(tm, 1024), _rows(tm, 1024),
         _full((1, 1024)), _full((N_SHARD, 1024, IN_SH))],
        [_rows(tm, 1024), _full((1, 1024))],
        [_sds((s_len, 1024), F32), _sds((1, 1024), F32)],
        [], (dq, dkv, dxg, x, dx1, g_mix, w_in_g), "arbitrary", comm)


def _mm_tn(xa, ya, name, ts):
    s_len, k = xa.shape
    n = ya.shape[1]

    steps = s_len // ts

    def body(x_ref, y_ref, o_ref, ob_ref):
        @pl.when(pl.program_id(0) == 0)
        def _():
            o_ref[...] = jnp.zeros_like(o_ref)
        o_ref[...] += _dot_tn(x_ref[...].astype(BF16), y_ref[...].astype(BF16))

        @pl.when(pl.program_id(0) == steps - 1)
        def _():
            ob_ref[...] = o_ref[...].astype(BF16)

    return pl.pallas_call(
        body, name=name, grid=(steps,), in_specs=[_rows(ts, k), _rows(ts, n)],
        out_specs=[_full((k, n))] * 2, out_shape=[_sds((k, n), F32), _sds((k, n), BF16)],
        compiler_params=_cp("arbitrary"))(xa, ya)


def _mm_tn_ysh(xa, y4, name, ts):
    s_len, k = xa.shape
    n = y4.shape[2]

    steps = s_len // ts

    def body(x_ref, y_ref, o_ref, ob_ref):
        @pl.when(pl.program_id(0) == 0)
        def _():
            o_ref[...] = jnp.zeros_like(o_ref)
        xb = x_ref[...].astype(BF16)
        for s in range(N_SHARD):
            o_ref[s] += _dot_tn(xb, y_ref[s])

        @pl.when(pl.program_id(0) == steps - 1)
        def _():
            ob_ref[...] = o_ref[...].astype(BF16)

    return pl.pallas_call(
        body, name=name, grid=(steps,), in_specs=[_rows(ts, k), _sh_rows(ts, n)],
        out_specs=[_full((N_SHARD, k, n))] * 2,
        out_shape=[_sds((N_SHARD, k, n), F32), _sds((N_SHARD, k, n), BF16)],
        compiler_params=_cp("arbitrary"))(xa, y4)


def _mm_tn_xsh(x4, ya, name, ts):
    s_len, n = ya.shape
    k = x4.shape[2]

    steps = s_len // ts

    def body(x_ref, y_ref, o_ref, ob_ref):
        @pl.when(pl.program_id(0) == 0)
        def _():
            o_ref[...] = jnp.zeros_like(o_ref)
        yb = y_ref[...].astype(BF16)
        for s in range(N_SHARD):
            o_ref[s] += _dot_tn(x_ref[s], yb)

        @pl.when(pl.program_id(0) == steps - 1)
        def _():
            ob_ref[...] = o_ref[...].astype(BF16)

    return pl.pallas_call(
        body, name=name, grid=(steps,), in_specs=[_sh_rows(ts, k), _rows(ts, n)],
        out_specs=[_full((N_SHARD, k, n))] * 2,
        out_shape=[_sds((N_SHARD, k, n), F32), _sds((N_SHARD, k, n), BF16)],
        compiler_params=_cp("arbitrary"))(x4, ya)


def _frow_from_rel_bias(rb):
    hi = jnp.broadcast_to(rb[:, 256:257], (ATT_HEADS, 385))
    mid = rb[:, 1:256][:, ::-1]
    lo = jnp.broadcast_to(rb[:, 0:1], (ATT_HEADS, 128))
    wrap = jnp.broadcast_to(rb[:, 256:257], (ATT_HEADS, ROLL_W - KB))
    return jnp.concatenate([hi, mid, lo, wrap], axis=1)


def _rel_bias_grad_from_dfrow(df):
    g256 = jnp.sum(df[:, 0:385], axis=1, keepdims=True) + jnp.sum(df[:, KB:ROLL_W], axis=1, keepdims=True)
    mid = df[:, 385:640][:, ::-1]
    g0 = jnp.sum(df[:, 640:KB], axis=1, keepdims=True)
    return jnp.concatenate([g0, mid, g256], axis=1)


def _block_diag(w):
    eye = jnp.eye(8, dtype=w.dtype)
    return (w[:, :, None, :] * eye[:, None, :, None]).reshape(D_LRU, D_LRU)


def _block_diag_extract(dense):
    eye = jnp.eye(8, dtype=dense.dtype)
    return jnp.sum(dense.reshape(8, 64, 8, 64) * eye[:, None, :, None], axis=2)


MID = ['w_out', 'wq_c', 'wk_c', 'wv_c', 'wo_c']
AG_IN_INPROJ = ['w_out', 'wq_c', 'wk_c']
AG_IN_ATTN = ['wv_c', 'wo_c', 'w_gate']
AG_IN_LRU = ['w_up']
AG_IN_MID = ['w_down']
RS_IN_MID = ['w_gate', 'w_up']
RS_IN_LRU = ['w_down']
RS_IN_ATTN = MID


def _local_step(x, mem, tgt, p, gw, shards=None, chip=None):
    s_len = x.shape[0]
    tm = min(256, s_len)
    tl = min(512, s_len)
    frow = _frow_from_rel_bias(p['rel_bias'])
    wrg = _block_diag(p['w_rg']).astype(BF16)
    wig = _block_diag(p['w_ig']).astype(BF16)
    gw = dict(gw)

    big, bigb, recv, part, sib = {}, {}, {}, {}, {}

    def ag(names):
        return [] if shards is None else [("ag", [shards[n] for n in names])]

    def rs(names):
        return [] if shards is None else [("rs", [bigb[n] for n in names])]

    def swap(names):
        return [] if shards is None else [("swap", [part[n] for n in names])]

    def reduce_own(names):
        if shards is not None:
            for n in names:
                part[n] = _sum_parts(big[n], recv[n], chip, "sum_" + n)

    h, qkv, xg, *got = _f_inproj(x, p['g_mix'], gw['w_in'], tm, ag(AG_IN_INPROJ))
    gw.update(zip(AG_IN_INPROJ, got))
    qkv_pad = jnp.pad(qkv, ((LEFT_CHUNKS * CHUNK, 0), (0, 0)))
    att, *got = _f_attn(qkv_pad, frow, ag(AG_IN_ATTN))
    gw.update(zip(AG_IN_ATTN, got))
    rec, u, hs, *got = _f_lru(xg, p['conv_w'], p['conv_b'], wrg, p['b_rg'], wig, p['b_ig'], p['lru_L'], tl,
                              ag(AG_IN_LRU))
    gw.update(zip(AG_IN_LRU, got))
    w_out = gw['w_out'].reshape(1024, 1024)
    wq = gw['wq_c'].reshape(1024, 1024)
    wk = gw['wk_c'].reshape(1024, 1024)
    wv = gw['wv_c'].reshape(1024, 1024)
    wo = gw['wo_c'].reshape(1024, 1024)
    mn, kx, vx = _f_mem(mem, p['g_mem'], wk, wv)
    mg, x1, hc, qx, ox, x2, *got = _f_mid(x, att, rec, p['g_out_attn'], p['g_out_lru'], w_out, p['g_cross'],
                                          wq, kx, vx, wo, tm, ag(AG_IN_MID))
    gw.update(zip(AG_IN_MID, got))
    hf, gact, uact, aact, dx3, loss, dg_final = _f_ffn(x2, tgt, p['g_ffn'], p['g_final'],
                                                       gw['w_gate'], gw['w_up'], gw['w_down'], tm)

    ts = min(512, s_len)
    dgact, duact, dx2, dg_ffn = _b_ffn(dx3, x2, gact, uact, p['g_ffn'], gw['w_gate'], gw['w_up'], gw['w_down'], tm)
    big['w_gate'], bigb['w_gate'] = _mm_tn_ysh(hf, dgact, "dw_gate", ts)
    big['w_up'], bigb['w_up'] = _mm_tn_ysh(hf, duact, "dw_up", ts)
    big['w_down'], bigb['w_down'] = _mm_tn_xsh(aact, dx3, "dw_down", ts)

    dqx, dx1, datt, drec, dkx, dvx, dg_cross, dg_oa, dg_ol, *got = _b_mid(
        dx2, qx, x1, att, rec, kx, vx, wo, wq, w_out, p['g_cross'], p['g_out_attn'], p['g_out_lru'], tm,
        rs(RS_IN_MID))
    recv.update(zip(RS_IN_MID, got))
    reduce_own(RS_IN_MID)
    dwk, dwv, dg_mem, dwkb, dwvb = _b_mem(dkx, dvx, mem, mn, p['g_mem'], wk, wv)
    big['wk_c'], bigb['wk_c'] = dwk, dwkb
    big['wv_c'], bigb['wv_c'] = dwv, dwvb
    big['w_out'], bigb['w_out'] = _mm_tn(mg, dx1, "dw_out", ts)
    big['wq_c'], bigb['wq_c'] = _mm_tn(hc, dqx, "dw_q", ts)
    big['wo_c'], bigb['wo_c'] = _mm_tn(ox, dx2, "dw_o", ts)
    for n in MID:
        big[n] = big[n].reshape(N_SHARD, 256, 1024)
        bigb[n] = bigb[n].reshape(N_SHARD, 256, 1024)

    dxg, dwrg, dwig, dbrg, dbig, dlam, dcw, dcb, *got = _b_lru(
        drec, hs, u, xg, p['conv_w'], wrg, p['b_rg'], wig, p['b_ig'], p['lru_L'], tl,
        rs(RS_IN_LRU) + swap(RS_IN_MID))
    recv.update(zip(RS_IN_LRU, got))
    sib.update(zip(RS_IN_MID, got[len(RS_IN_LRU):]))
    reduce_own(RS_IN_LRU)
    dq, dkv, dfrow, *got = _b_attn(qkv_pad, datt, frow, rs(RS_IN_ATTN) + swap(RS_IN_LRU))
    recv.update(zip(RS_IN_ATTN, got))
    sib.update(zip(RS_IN_LRU, got[len(RS_IN_ATTN):]))
    reduce_own(RS_IN_ATTN)
    big['w_in'], bigb['w_in'] = _b_win(dq, dkv, dxg, h, ts)
    grad_x, dg_mix, *got = _b_inproj(dq, dkv, dxg, x, dx1, p['g_mix'], gw['w_in'], tm,
                                     rs(['w_in']) + swap(RS_IN_ATTN))
    recv.update(zip(['w_in'], got))
    sib.update(zip(RS_IN_ATTN, got[1:]))
    reduce_own(['w_in'])
    small = {
        'g_mix': dg_mix, 'rel_bias': _rel_bias_grad_from_dfrow(dfrow), 'conv_w': dcw, 'conv_b': dcb,
        'w_rg': _block_diag_extract(dwrg), 'b_rg': dbrg, 'w_ig': _block_diag_extract(dwig), 'b_ig': dbig,
        'lru_L': dlam,
        'g_out_attn': dg_oa, 'g_out_lru': dg_ol, 'g_cross': dg_cross, 'g_mem': dg_mem, 'g_ffn': dg_ffn,
        'g_final': dg_final,
    }
    return jnp.sum(loss[0, 0:1]), grad_x, small, big, part, sib


def _cast_shards(ws):
    def body(*refs):
        n = len(refs) // 2
        for src, dst in zip(refs[:n], refs[n:]):
            dst[...] = src[...].astype(BF16)

    return pl.pallas_call(body, name="cast_shards", out_shape=[_sds(w.shape, BF16) for w in ws],
                          compiler_params=_cp())(*ws)


def _sum_parts(own4, recv3, chip, name):
    _, r, c = own4.shape
    tr = r // 4

    def body(chip_ref, own_ref, rc_ref, o_ref):
        o_ref[...] = ((own_ref[0] + rc_ref[0].astype(F32)) + rc_ref[1].astype(F32)) + rc_ref[2].astype(F32)

    grid_spec = pltpu.PrefetchScalarGridSpec(
        num_scalar_prefetch=1, grid=(4,),
        in_specs=[pl.BlockSpec((1, tr, c), lambda i, ch: (ch[0], i, 0)),
                  pl.BlockSpec((3, tr, c), lambda i, ch: (0, i, 0))],
        out_specs=pl.BlockSpec((tr, c), lambda i, ch: (i, 0)))
    return pl.pallas_call(body, name=name, grid_spec=grid_spec, out_shape=_sds((r, c), F32),
                          compiler_params=_cp("parallel"))(chip, own4, recv3)


def _adamw_math(w, g, m, v):
    m = ADAM_B1 * m + (1.0 - ADAM_B1) * g
    v = ADAM_B2 * v + (1.0 - ADAM_B2) * (g * g)
    m_hat = m / (1.0 - ADAM_B1 ** ADAM_STEP)
    v_hat = v / (1.0 - ADAM_B2 ** ADAM_STEP)
    delta = -ADAM_LR * (m_hat / (jnp.sqrt(v_hat) + ADAM_EPS) + ADAM_WD * w)
    return delta, m, v


def _final_adamw(pa, pb, w, m, v, name):
    r, c = w.shape
    tr = r // 4

    def body(pa_ref, pb_ref, w_ref, m_ref, v_ref, g_ref, d_ref, nm_ref, nv_ref):
        g = pa_ref[...] + pb_ref[...]
        g_ref[...] = g
        d_ref[...], nm_ref[...], nv_ref[...] = _adamw_math(w_ref[...], g, m_ref[...], v_ref[...])

    return pl.pallas_call(
        body, name=name, grid=(4,), in_specs=[_rows(tr, c)] * 5, out_specs=[_rows(tr, c)] * 4,
        out_shape=[_sds((r, c), F32)] * 4, compiler_params=_cp("parallel"))(pa, pb, w, m, v)


def _adamw_small(w, g, m, v):
    def body(w_ref, g_ref, m_ref, v_ref, d_ref, nm_ref, nv_ref):
        d_ref[...], nm_ref[...], nv_ref[...] = _adamw_math(w_ref[...], g_ref[...], m_ref[...], v_ref[...])

    return pl.pallas_call(body, name="adamw_conv_w", out_shape=[_sds(w.shape, F32)] * 3,
                          compiler_params=_cp())(w, g, m, v)


def _ar_small(gp, wp, mp, vp):
    rows = gp.shape[0]

    def body(g_ref, w_ref, m_ref, v_ref, go_ref, d_ref, nm_ref, nv_ref, buf, send_sems, recv_sems):
        x, y, c = _mesh_pos()
        me = 4 * x + 2 * y + c

        def peer(k):
            px = 1 - x if k & 4 else x
            py = 1 - y if k & 2 else y
            pc = 1 - c if k & 1 else c
            return px, py, pc

        def remote(k, slot):
            return pltpu.make_async_remote_copy(
                src_ref=g_ref, dst_ref=buf.at[slot], send_sem=send_sems.at[k - 1], recv_sem=recv_sems.at[k - 1],
                device_id=peer(k), device_id_type=MESH_ID)

        for k in range(1, 8):
            remote(k, me).start()
        buf[me] = g_ref[...]
        for k in range(1, 8):
            px, py, pc = peer(k)
            remote(k, 4 * px + 2 * py + pc).wait_recv()
        for k in range(1, 8):
            remote(k, me).wait_send()
        tot = buf[0]
        for k in range(1, 8):
            tot = tot + buf[k]
        go_ref[...] = tot
        d_ref[...], nm_ref[...], nv_ref[...] = _adamw_math(w_ref[...], tot, m_ref[...], v_ref[...])

    return pl.pallas_call(
        body, name="ar_small", out_shape=[_sds((rows, 128), F32)] * 4,
        scratch_shapes=[pltpu.VMEM((8, rows, 128), F32), pltpu.SemaphoreType.DMA((7,)),
                        pltpu.SemaphoreType.DMA((7,))],
        compiler_params=_cp())(gp, wp, mp, vp)


def _pack(parts):
    flat = jnp.concatenate([parts[n].reshape(-1) for n in SMALL])
    rows = -(-flat.shape[0] // 1024) * 8
    return jnp.pad(flat, (0, rows * 128 - flat.shape[0])).reshape(rows, 128)


def _unpack(pack):
    flat = pack.reshape(-1)
    out, off = {}, 0
    for n in SMALL:
        size = math.prod(SMALL_SHAPES[n])
        out[n] = flat[off:off + size].reshape(SMALL_SHAPES[n])
        off += size
    return out


INPUT_NAMES = (['x', 'mem'] + WEIGHTS + ['loss_target'] + ['m_' + n for n in WEIGHTS] + ['v_' + n for n in WEIGHTS])


def kernel(x, mem, g_mix, w_in, rel_bias, conv_w, conv_b, w_rg, b_rg, w_ig, b_ig, lru_L, g_out_attn, g_out_lru, w_out, g_cross, g_mem, wq_c, wk_c, wv_c, wo_c, g_ffn, w_gate, w_up, w_down, g_final, loss_target, m_g_mix, m_w_in, m_rel_bias, m_conv_w, m_conv_b, m_w_rg, m_b_rg, m_w_ig, m_b_ig, m_lru_L, m_g_out_attn, m_g_out_lru, m_w_out, m_g_cross, m_g_mem, m_wq_c, m_wk_c, m_wv_c, m_wo_c, m_g_ffn, m_w_gate, m_w_up, m_w_down, m_g_final, v_g_mix, v_w_in, v_rel_bias, v_conv_w, v_conv_b, v_w_rg, v_b_rg, v_w_ig, v_b_ig, v_lru_L, v_g_out_attn, v_g_out_lru, v_w_out, v_g_cross, v_g_mem, v_wq_c, v_wk_c, v_wv_c, v_wo_c, v_g_ffn, v_w_gate, v_w_up, v_w_down, v_g_final):
    a = dict(zip(INPUT_NAMES, (x, mem, g_mix, w_in, rel_bias, conv_w, conv_b, w_rg, b_rg, w_ig, b_ig, lru_L, g_out_attn, g_out_lru, w_out, g_cross, g_mem, wq_c, wk_c, wv_c, wo_c, g_ffn, w_gate, w_up, w_down, g_final, loss_target, m_g_mix, m_w_in, m_rel_bias, m_conv_w, m_conv_b, m_w_rg, m_b_rg, m_w_ig, m_b_ig, m_lru_L, m_g_out_attn, m_g_out_lru, m_w_out, m_g_cross, m_g_mem, m_wq_c, m_wk_c, m_wv_c, m_wo_c, m_g_ffn, m_w_gate, m_w_up, m_w_down, m_g_final, v_g_mix, v_w_in, v_rel_bias, v_conv_w, v_conv_b, v_w_rg, v_b_rg, v_w_ig, v_b_ig, v_lru_L, v_g_out_attn, v_g_out_lru, v_w_out, v_g_cross, v_g_mem, v_wq_c, v_wk_c, v_wv_c, v_wo_c, v_g_ffn, v_w_gate, v_w_up, v_w_down, v_g_final)))
    chip = 2 * lax.axis_index("x") + lax.axis_index("y")

    shards = dict(zip(BIG, _cast_shards([a[n][0] for n in BIG])))
    w_in_g, conv_w_g = _comm_only("ag_w_in", [("ag", [shards['w_in'], a['conv_w'][0]])])
    conv_w_full = conv_w_g.transpose(1, 0, 2).reshape(4, D_LRU)

    p = {n: a[n] for n in SMALL}
    p['rel_bias'] = a['rel_bias'][0]
    p['w_rg'] = a['w_rg'][0]
    p['w_ig'] = a['w_ig'][0]
    p['conv_w'] = conv_w_full
    p['g_final'] = a['g_final'][None, :]
    chip_arr = jnp.reshape(chip, (1,)).astype(jnp.int32)
    loss_part, grad_x, small, _, part, sib = _local_step(
        a['x'][0], a['mem'][0], a['loss_target'][0], p, {'w_in': w_in_g}, shards, chip_arr)
    loss = lax.psum(loss_part, ("x", "y", "c"))

    sib['w_in'], = _comm_only("swap_w_in", [("swap", [part['w_in']])])
    out = {}
    for n in BIG:
        out[n] = _final_adamw(part[n], sib[n], a[n][0], a['m_' + n][0], a['v_' + n][0], "adamw_" + n)

    zeros_cw = jnp.zeros((1, 4, D_LRU), F32)
    def packed(prefix):
        d = {n: a[prefix + n] for n in SMALL}
        d['conv_w'] = zeros_cw
        return _pack(d)
    packs = _ar_small(_pack(small), packed(''), packed('m_'), packed('v_'))
    sg, sd, sm, sv = [_unpack(pk) for pk in packs]
    g_cw = lax.dynamic_slice(sg['conv_w'][0], (0, chip * 128), (4, 128))
    d_cw, m_cw, v_cw = _adamw_small(a['conv_w'][0], g_cw, a['m_conv_w'][0], a['v_conv_w'][0])
    sg['conv_w'], sd['conv_w'], sm['conv_w'], sv['conv_w'] = g_cw[None], d_cw[None], m_cw[None], v_cw[None]

    def leaf(i, n):
        if n in BIG:
            return out[n][i][None]
        return (sg, sd, sm, sv)[i][n]

    return (loss, grad_x[None], *[leaf(i, n) for i in range(4) for n in WEIGHTS])
```

```python
import math

import jax
import jax.numpy as jnp
from jax import lax
from jax.experimental import pallas as pl
from jax.experimental.pallas import tpu as pltpu

F32 = jnp.float32
BF16 = jnp.bfloat16

D_MODEL = 1024
D_ATT = 512
D_LRU = 512
HEAD_DIM = 64
ATT_HEADS = 8
CHUNK = 64
LEFT_CHUNKS = 8
X_HEADS = 4
X_HEAD_DIM = 256
N_SHARD = 4
IN_SH = 640
FF_SH = 704
EPS = 1e-6
LRU_C = 8.0
QB = 256
KB = 768
ROLL_W = 1024
NEG = -1e30
ATT_SCALE = HEAD_DIM ** -0.5
X_SCALE = X_HEAD_DIM ** -0.5

ADAM_LR = 0.001
ADAM_B1 = 0.9
ADAM_B2 = 0.999
ADAM_EPS = 1e-08
ADAM_WD = 0.01
ADAM_STEP = 10

VMEM_LIMIT_V7X = 56 * 1024 * 1024
MESH_ID = pl.DeviceIdType.MESH

WEIGHTS = ['g_mix', 'w_in', 'rel_bias', 'conv_w', 'conv_b', 'w_rg', 'b_rg', 'w_ig', 'b_ig', 'lru_L',
           'g_out_attn', 'g_out_lru', 'w_out', 'g_cross', 'g_mem', 'wq_c', 'wk_c', 'wv_c', 'wo_c',
           'g_ffn', 'w_gate', 'w_up', 'w_down', 'g_final']
BIG = ['w_in', 'w_out', 'wq_c', 'wk_c', 'wv_c', 'wo_c', 'w_gate', 'w_up', 'w_down']
SMALL = [n for n in WEIGHTS if n not in BIG]
SMALL_SHAPES = {
    'g_mix': (1, 1024), 'rel_bias': (1, 8, 257), 'conv_w': (1, 4, 512), 'conv_b': (1, 512),
    'w_rg': (1, 8, 64, 64), 'b_rg': (1, 512), 'w_ig': (1, 8, 64, 64), 'b_ig': (1, 512), 'lru_L': (1, 512),
    'g_out_attn': (1, 512), 'g_out_lru': (1, 512), 'g_cross': (1, 1024), 'g_mem': (1, 1024),
    'g_ffn': (1, 1024), 'g_final': (1024,)}


def _sds(shape, dtype):
    return jax.ShapeDtypeStruct(shape, dtype)


def _cp(*sem):
    return pltpu.CompilerParams(dimension_semantics=sem or None, vmem_limit_bytes=VMEM_LIMIT_V7X)


def _rows(tm, n):
    return pl.BlockSpec((tm, n), lambda i: (i, 0))


def _full(shape):
    nd = len(shape)
    return pl.BlockSpec(shape, lambda i: (0,) * nd)


def _dot(a, b):
    return jnp.dot(a, b, preferred_element_type=F32)


def _dot_nt(a, b):
    return lax.dot_general(a, b, (((1,), (1,)), ((), ())), preferred_element_type=F32)


def _dot_tn(a, b):
    return lax.dot_general(a, b, (((0,), (0,)), ((), ())), preferred_element_type=F32)


def _rinv(x):
    return lax.rsqrt(jnp.mean(x * x, axis=-1, keepdims=True) + EPS)


def _rms_bwd(dy, x, g):
    r = _rinv(x)
    yh = x * r
    dyh = dy * g
    dx = r * (dyh - yh * jnp.mean(dyh * yh, axis=-1, keepdims=True))
    return dx, jnp.sum(dy * yh, axis=0, keepdims=True)


def _gelu(x):
    c = math.sqrt(2.0 / math.pi)
    t = jnp.tanh(c * (x + 0.044715 * x * x * x))
    return 0.5 * x * (1.0 + t)


def _gelu_and_grad(x):
    c = math.sqrt(2.0 / math.pi)
    t = jnp.tanh(c * (x + 0.044715 * x * x * x))
    g = 0.5 * x * (1.0 + t)
    dg = 0.5 * (1.0 + t) + 0.5 * x * (1.0 - t * t) * c * (1.0 + 3.0 * 0.044715 * x * x)
    return g, dg


def _neg_expm1(z):
    series = -z * (1 + z / 2 * (1 + z / 3 * (1 + z / 4 * (1 + z / 5 * (1 + z / 6 * (1 + z / 7))))))
    return jnp.where(z > -0.25, series, 1.0 - jnp.exp(z))


def _lru_gates(u, wrg, brg, wig, big, lam):
    ub = u.astype(BF16)
    r = jax.nn.sigmoid(_dot(ub, wrg) + brg)
    ig = jax.nn.sigmoid(_dot(ub, wig) + big)
    sp = jnp.maximum(-lam, 0.0) + jnp.log1p(jnp.exp(-jnp.abs(lam)))
    la = -LRU_C * r * sp
    a = jnp.exp(la)
    mult = jnp.sqrt(jnp.maximum(_neg_expm1(2.0 * la), 0.0))
    return ub, r, ig, sp, a, mult


def _scan8(a8, b8, hprev):
    row = lax.broadcasted_iota(jnp.int32, a8.shape, 0)
    aa, bb = a8, b8
    for d in (1, 2, 4):
        a_s = pltpu.roll(aa, d, 0)
        b_s = pltpu.roll(bb, d, 0)
        m = row >= d
        bb = jnp.where(m, aa * b_s + bb, bb)
        aa = jnp.where(m, aa * a_s, aa)
    return aa * hprev + bb


def _mesh_pos():
    return lax.axis_index("x"), lax.axis_index("y"), lax.axis_index("c")


def _other_chips(x, y):
    return [(1 - x, y), (x, 1 - y), (1 - x, 1 - y)]


def _ag_copies(ins, outs, sems):
    send_sems, recv_sems, loc_sems = sems
    n = len(ins)
    x, y, c = _mesh_pos()
    mine = 2 * x + y
    chips = _other_chips(x, y)

    def remote(k, j, slot):
        px, py = chips[j]
        return pltpu.make_async_remote_copy(
            src_ref=ins[k], dst_ref=outs[k].at[slot], send_sem=send_sems.at[k, j], recv_sem=recv_sems.at[k, j],
            device_id=(px, py, c), device_id_type=MESH_ID)

    def local(k):
        return pltpu.make_async_copy(ins[k], outs[k].at[mine], loc_sems.at[k])

    def start():
        for k in range(n):
            local(k).start()
            for j in range(3):
                remote(k, j, mine).start()

    def wait():
        for k in range(n):
            for j, (px, py) in enumerate(chips):
                remote(k, j, 2 * px + py).wait_recv()
        for k in range(n):
            for j in range(3):
                remote(k, j, mine).wait_send()
            local(k).wait()

    return start, wait


def _rs_copies(ins, outs, sems):
    send_sems, recv_sems = sems
    n = len(ins)
    x, y, c = _mesh_pos()
    chips = _other_chips(x, y)

    def remote(k, j):
        px, py = chips[j]
        return pltpu.make_async_remote_copy(
            src_ref=ins[k].at[2 * px + py], dst_ref=outs[k].at[j],
            send_sem=send_sems.at[k, j], recv_sem=recv_sems.at[k, j],
            device_id=(px, py, c), device_id_type=MESH_ID)

    def start():
        for k in range(n):
            for j in range(3):
                remote(k, j).start()

    def wait():
        for k in range(n):
            for j in range(3):
                remote(k, j).wait_recv()
        for k in range(n):
            for j in range(3):
                remote(k, j).wait_send()

    return start, wait


def _swap_copies(ins, outs, sems):
    send_sems, recv_sems = sems
    x, y, c = _mesh_pos()
    copies = [pltpu.make_async_remote_copy(
        src_ref=ins[k], dst_ref=outs[k], send_sem=send_sems.at[k], recv_sem=recv_sems.at[k],
        device_id=(x, y, 1 - c), device_id_type=MESH_ID) for k in range(len(ins))]

    def start():
        for cp in copies:
            cp.start()

    def wait():
        for cp in copies:
            cp.wait()

    return start, wait


def _comm_plan(groups):
    plan, arrs, shapes, sems = [], [], [], []
    for kind, group in groups:
        k = len(group)
        arrs += group
        if kind == "ag":
            shapes += [_sds((N_SHARD,) + w.shape, w.dtype) for w in group]
            gsems = [pltpu.SemaphoreType.DMA((k, 3)), pltpu.SemaphoreType.DMA((k, 3)), pltpu.SemaphoreType.DMA((k,))]
            maker = _ag_copies
        elif kind == "rs":
            shapes += [_sds((3,) + g.shape[1:], g.dtype) for g in group]
            gsems = [pltpu.SemaphoreType.DMA((k, 3)), pltpu.SemaphoreType.DMA((k, 3))]
            maker = _rs_copies
        else:
            shapes += [_sds(g.shape, g.dtype) for g in group]
            gsems = [pltpu.SemaphoreType.DMA((k,)), pltpu.SemaphoreType.DMA((k,))]
            maker = _swap_copies
        plan.append((maker, k, len(gsems)))
        sems += gsems
    return plan, arrs, shapes, sems


def _comm_fns(plan, cins, couts, sems):
    fns, a, s = [], 0, 0
    for maker, k, ns in plan:
        fns.append(maker(cins[a:a + k], couts[a:a + k], sems[s:s + ns]))
        a += k
        s += ns

    def start():
        for st, _ in fns:
            st()

    def wait():
        for _, wt in fns:
            wt()

    return start, wait


def _call(body, name, grid, in_specs, out_specs, out_shape, scratch, args, sem, comm=None):
    if not comm:
        return pl.pallas_call(body, name=name, grid=grid, in_specs=in_specs, out_specs=out_specs,
                              out_shape=out_shape, scratch_shapes=scratch, compiler_params=_cp(sem))(*args)
    plan, c_arrs, c_shapes, c_sems = _comm_plan(comm)
    k = len(c_arrs)
    n_in, n_out, n_scr = len(in_specs), len(out_specs), len(scratch)
    last = grid[0] - 1

    def wrapped(*refs):
        ins, cins = refs[:n_in], refs[n_in:n_in + k]
        o0 = n_in + k
        outs, couts = refs[o0:o0 + n_out], refs[o0 + n_out:o0 + n_out + k]
        s0 = o0 + n_out + k
        start, wait = _comm_fns(plan, cins, couts, refs[s0 + n_scr:])
        pl.when(pl.program_id(0) == 0)(start)
        body(*ins, *outs, *refs[s0:s0 + n_scr])
        pl.when(pl.program_id(0) == last)(wait)

    return pl.pallas_call(
        wrapped, name=name, grid=grid, in_specs=list(in_specs) + [_any()] * k,
        out_specs=list(out_specs) + [_any()] * k, out_shape=list(out_shape) + c_shapes,
        scratch_shapes=list(scratch) + c_sems, compiler_params=_cp(sem))(*args, *c_arrs)


def _comm_only(name, comm):
    plan, c_arrs, c_shapes, c_sems = _comm_plan(comm)
    k = len(c_arrs)

    def body(*refs):
        start, wait = _comm_fns(plan, refs[:k], refs[k:2 * k], refs[2 * k:])
        start()
        wait()

    return pl.pallas_call(body, name=name, in_specs=[_any()] * k, out_specs=[_any()] * k, out_shape=c_shapes,
                          scratch_shapes=c_sems, compiler_params=_cp())(*c_arrs)


def _any():
    return pl.BlockSpec(memory_space=pl.ANY)


def _rscan8(c8, d8, lnext):
    row = lax.broadcasted_iota(jnp.int32, c8.shape, 0)
    cc, dd = c8, d8
    for d in (1, 2, 4):
        c_s = pltpu.roll(cc, 8 - d, 0)
        d_s = pltpu.roll(dd, 8 - d, 0)
        m = row < 8 - d
        dd = jnp.where(m, cc * d_s + dd, dd)
        cc = jnp.where(m, cc * c_s, cc)
    return cc * lnext + dd


def _f_inproj(x, g_mix, w_in_g, tm, comm=None):
    s_len = x.shape[0]

    def body(x_ref, g_ref, w_ref, h_ref, qkv_ref, xg_ref):
        xv = x_ref[...]
        h = (xv * _rinv(xv) * g_ref[...]).astype(BF16)
        h_ref[...] = h
        p0 = _dot(h, w_ref[0])
        qkv_ref[:, 0:D_ATT] = (p0[:, 0:D_ATT] * ATT_SCALE).astype(BF16)
        qkv_ref[:, D_ATT:640] = p0[:, D_ATT:640].astype(BF16)
        qkv_ref[:, 640:1280] = _dot(h, w_ref[1]).astype(BF16)
        p2 = _dot(h, w_ref[2])
        qkv_ref[:, 1280:1536] = p2[:, 0:256].astype(BF16)
        xg_ref[:, 0:384] = p2[:, 256:640]
        xg_ref[:, 384:1024] = _dot(h, w_ref[3])

    return _call(
        body, "f_inproj", (s_len // tm,),
        [_rows(tm, 1024), _full((1, 1024)), _full((N_SHARD, 1024, IN_SH))],
        [_rows(tm, 1024), _rows(tm, 1536), _rows(tm, 1024)],
        [_sds((s_len, 1024), BF16), _sds((s_len, 1536), BF16), _sds((s_len, 1024), F32)],
        [], (x, g_mix, w_in_g), "arbitrary", comm)


N_BIAS = 3


def _bias_table(frow_ref, bias_sc):
    qa = lax.broadcasted_iota(jnp.int32, (QB, KB), 0) // CHUNK
    kcol = lax.broadcasted_iota(jnp.int32, (QB, KB), 1)
    kb = kcol // CHUNK
    band = jnp.where((kb >= qa) & (kb - qa <= LEFT_CHUNKS), 0.0, NEG).astype(F32)
    for h in range(ATT_HEADS):
        row = jnp.broadcast_to(frow_ref[h:h + 1, :], (QB, ROLL_W))
        toep = pltpu.roll(row, 0, 1, stride=1, stride_axis=0)
        gen = toep[:, 0:KB] + band
        bias_sc[N_BIAS - 1, h] = gen
        for v in range(N_BIAS - 1):
            pad_keys = LEFT_CHUNKS * CHUNK - v * QB
            bias_sc[v, h] = gen + jnp.where(kcol < pad_keys, NEG, 0.0).astype(F32)


def _even_lanes():
    return lax.broadcasted_iota(jnp.int32, (1, 2 * HEAD_DIM), 1) < HEAD_DIM


def _att_probs(qm, kts, bias):
    s = jnp.concatenate([_dot_nt(qm, k) for k in kts], axis=1) + bias
    return jnp.exp(s - jnp.max(s, axis=-1, keepdims=True))


def _att_in_specs(clamp):
    def spec(j, col):
        return pl.BlockSpec((QB, D_ATT), lambda i: (clamp(i) + j, col))
    return [spec(2, 0), spec(0, 1), spec(1, 1), spec(2, 1), spec(0, 2), spec(1, 2), spec(2, 2)]


def _f_attn(qkv_pad, frow, comm=None):
    s_len = qkv_pad.shape[0] - LEFT_CHUNKS * CHUNK
    nb = s_len // QB

    def body(q_ref, k0, k1, k2, v0, v1, v2, frow_ref, o_ref, bias_sc):
        i = pl.program_id(0)

        @pl.when(i == 0)
        def _():
            _bias_table(frow_ref, bias_sc)

        var = jnp.minimum(i, N_BIAS - 1)
        even = _even_lanes()
        for hp in range(ATT_HEADS // 2):
            cs = slice(hp * 2 * HEAD_DIM, (hp + 1) * 2 * HEAD_DIM)
            qt = q_ref[:, cs]
            kts = [k0[:, cs], k1[:, cs], k2[:, cs]]
            vts = [v0[:, cs], v1[:, cs], v2[:, cs]]
            res = []
            for e in range(2):
                keep = even if e == 0 else jnp.logical_not(even)
                pb = _att_probs(jnp.where(keep, qt, 0), kts, bias_sc[var, 2 * hp + e]).astype(BF16)
                r = _dot(pb[:, 0:QB], jnp.where(keep, vts[0], 1))
                for j in (1, 2):
                    r = r + _dot(pb[:, j * QB:(j + 1) * QB], jnp.where(keep, vts[j], 1))
                res.append(r / pltpu.roll(r, HEAD_DIM, 1))
            o_ref[:, cs] = jnp.where(even, res[0], res[1])

    return _call(
        body, "f_attn", (nb,),
        _att_in_specs(lambda i: i) + [_full((ATT_HEADS, ROLL_W))],
        [_rows(QB, D_ATT)], [_sds((s_len, D_ATT), F32)],
        [pltpu.VMEM((N_BIAS, ATT_HEADS, QB, KB), F32)], (*([qkv_pad] * 7), frow), "arbitrary", comm)


def _f_lru(xg, conv_w, conv_b, wrg, brg, wig, big, lam, tl, comm=None):
    s_len = xg.shape[0]

    def body(xg_ref, cw_ref, cb_ref, wrg_ref, brg_ref, wig_ref, big_ref, l_ref,
             rec_ref, u_ref, hs_ref, xbuf, a_sc, b_sc, hcar):
        i = pl.program_id(0)

        @pl.when(i == 0)
        def _():
            xbuf[0:8, :] = jnp.zeros((8, D_LRU), F32)
            hcar[...] = jnp.zeros((8, D_LRU), F32)

        xu0 = xg_ref[:, 0:D_LRU]
        xbuf[8:8 + tl, :] = xu0
        u = cb_ref[...] + cw_ref[0:1, :] * xbuf[pl.ds(5, tl), :]
        for j in range(1, 4):
            u = u + cw_ref[j:j + 1, :] * xbuf[pl.ds(5 + j, tl), :]
        xbuf[0:8, :] = xu0[tl - 8:tl, :]
        u_ref[...] = u
        _, _, ig, _, a, mult = _lru_gates(u, wrg_ref[...], brg_ref[...], wig_ref[...], big_ref[...], l_ref[...])
        a_sc[...] = a
        b_sc[...] = mult * (ig * u)

        def grp(g, hprev):
            off = pl.multiple_of(g * 8, 8)
            h8 = _scan8(a_sc[pl.ds(off, 8), :], b_sc[pl.ds(off, 8), :], hprev)
            hs_ref[pl.ds(off, 8), :] = h8
            return h8[7:8, :]

        hcar[0:1, :] = lax.fori_loop(0, tl // 8, grp, hcar[0:1, :])
        rec_ref[...] = hs_ref[...] * _gelu(xg_ref[:, D_LRU:2 * D_LRU])

    vec = _full((1, D_LRU))
    return _call(
        body, "f_lru", (s_len // tl,),
        [_rows(tl, 1024), _full((4, D_LRU)), vec, _full((D_LRU, D_LRU)), vec, _full((D_LRU, D_LRU)), vec, vec],
        [_rows(tl, D_LRU)] * 3, [_sds((s_len, D_LRU), F32)] * 3,
        [pltpu.VMEM((tl + 8, D_LRU), F32), pltpu.VMEM((tl, D_LRU), F32),
         pltpu.VMEM((tl, D_LRU), F32), pltpu.VMEM((8, D_LRU), F32)],
        (xg, conv_w, conv_b, wrg, brg, wig, big, lam), "arbitrary", comm)


def _f_mem(mem, g_mem, wk, wv):
    def body(mem_ref, g_ref, wk_ref, wv_ref, mn_ref, kx_ref, vx_ref):
        mv = mem_ref[...]
        mn = (mv * _rinv(mv) * g_ref[...]).astype(BF16)
        mn_ref[...] = mn
        kx_ref[...] = _dot(mn, wk_ref[...]).astype(BF16)
        vx_ref[...] = _dot(mn, wv_ref[...]).astype(BF16)

    m = mem.shape[0]
    return pl.pallas_call(
        body, name="f_mem", out_shape=[_sds((m, 1024), BF16)] * 3,
        compiler_params=_cp())(mem, g_mem, wk, wv)


def _xattn_probs(q, k):
    s = _dot_nt(q, k) * X_SCALE
    m = jnp.max(s, axis=-1, keepdims=True)
    p = jnp.exp(s - m)
    return p, jnp.sum(p, axis=-1, keepdims=True)


def _f_mid(x, att, rec, g_oa, g_ol, w_out, g_cross, wq, kx, vx, wo, tm, comm=None):
    s_len = x.shape[0]
    m_len = kx.shape[0]

    def body(x_ref, att_ref, rec_ref, goa_ref, gol_ref, wout_ref, gc_ref, wq_ref, kx_ref, vx_ref, wo_ref,
             mg_ref, x1_ref, hc_ref, qx_ref, ox_ref, x2_ref):
        av = att_ref[...]
        rv = rec_ref[...]
        mg_ref[:, 0:D_ATT] = (av * _rinv(av) * goa_ref[...]).astype(BF16)
        mg_ref[:, D_ATT:1024] = (rv * _rinv(rv) * gol_ref[...]).astype(BF16)
        x1 = x_ref[...] + _dot(mg_ref[...], wout_ref[...])
        x1_ref[...] = x1
        hc = (x1 * _rinv(x1) * gc_ref[...]).astype(BF16)
        hc_ref[...] = hc
        qx_ref[...] = _dot(hc, wq_ref[...]).astype(BF16)
        for h in range(X_HEADS):
            sl = slice(h * X_HEAD_DIM, (h + 1) * X_HEAD_DIM)
            p, l = _xattn_probs(qx_ref[:, sl], kx_ref[:, sl])
            ox_ref[:, sl] = (_dot(p.astype(BF16), vx_ref[:, sl]) / l).astype(BF16)
        x2_ref[...] = x1 + _dot(ox_ref[...], wo_ref[...])

    sq = _full((1024, 1024))
    return _call(
        body, "f_mid", (s_len // tm,),
        [_rows(tm, 1024), _rows(tm, 512), _rows(tm, 512), _full((1, 512)), _full((1, 512)), sq,
         _full((1, 1024)), sq, _full((m_len, 1024)), _full((m_len, 1024)), sq],
        [_rows(tm, 1024)] * 6,
        [_sds((s_len, 1024), BF16), _sds((s_len, 1024), F32), _sds((s_len, 1024), BF16),
         _sds((s_len, 1024), BF16), _sds((s_len, 1024), BF16), _sds((s_len, 1024), F32)],
        [], (x, att, rec, g_oa, g_ol, w_out, g_cross, wq, kx, vx, wo), "arbitrary", comm)


def _load_weights_once(pairs):
    @pl.when(pl.program_id(0) == 0)
    def _():
        for hbm, vmem in pairs:
            pltpu.sync_copy(hbm, vmem)


def _sh_rows(tm, n):
    return pl.BlockSpec((N_SHARD, tm, n), lambda i: (0, i, 0))


def _f_ffn(x2, tgt, g_ffn, g_final, wg, wu, wd, tm):
    s_len = x2.shape[0]

    def body(x2_ref, t_ref, gf_ref, gfin_ref, wg_hbm, wu_hbm, wd_hbm,
             hf_ref, g_ref, u_ref, a_ref, dx3_ref, loss_ref, dgfin_ref, wg_ref, wu_ref, wd_ref):
        _load_weights_once([(wg_hbm, wg_ref), (wu_hbm, wu_ref), (wd_hbm, wd_ref)])

        @pl.when(pl.program_id(0) == 0)
        def _():
            loss_ref[...] = jnp.zeros_like(loss_ref)
            dgfin_ref[...] = jnp.zeros_like(dgfin_ref)

        x2v = x2_ref[...]
        hf = (x2v * _rinv(x2v) * gf_ref[...]).astype(BF16)
        hf_ref[...] = hf
        x3 = x2v
        for s in range(N_SHARD):
            gv = _dot(hf, wg_ref[s])
            uv = _dot(hf, wu_ref[s])
            av = (gv * jax.nn.sigmoid(gv) * uv).astype(BF16)
            g_ref[s] = gv.astype(BF16)
            u_ref[s] = uv.astype(BF16)
            a_ref[s] = av
            x3 = x3 + _dot(av, wd_ref[s])
        r3 = _rinv(x3)
        yh = x3 * r3
        gfin = gfin_ref[...]
        err = yh * gfin - t_ref[...]
        loss_ref[...] += jnp.full((1, 128), 0.5 / D_MODEL, F32) * jnp.sum(err * err)
        dy = err * (1.0 / D_MODEL)
        dgfin_ref[...] += jnp.sum(dy * yh, axis=0, keepdims=True)
        dyh = dy * gfin
        dx3_ref[...] = r3 * (dyh - yh * jnp.mean(dyh * yh, axis=-1, keepdims=True))

    vec = _full((1, 1024))
    return pl.pallas_call(
        body, name="f_ffn", grid=(s_len // tm,),
        in_specs=[_rows(tm, 1024), _rows(tm, 1024), vec, vec, _any(), _any(), _any()],
        out_specs=[_rows(tm, 1024), _sh_rows(tm, FF_SH), _sh_rows(tm, FF_SH), _sh_rows(tm, FF_SH),
                   _rows(tm, 1024), _full((1, 128)), vec],
        out_shape=[_sds((s_len, 1024), BF16)] + [_sds((N_SHARD, s_len, FF_SH), BF16)] * 3
                  + [_sds((s_len, 1024), F32), _sds((1, 128), F32), _sds((1, 1024), F32)],
        scratch_shapes=[pltpu.VMEM((N_SHARD, 1024, FF_SH), BF16), pltpu.VMEM((N_SHARD, 1024, FF_SH), BF16),
                        pltpu.VMEM((N_SHARD, FF_SH, 1024), BF16)],
        compiler_params=_cp("arbitrary"))(x2, tgt, g_ffn, g_final, wg, wu, wd)


def _b_ffn(dx3, x2, gact, uact, g_ffn, wg, wu, wd, tm):
    s_len = x2.shape[0]

    def body(dx3_ref, x2_ref, g_ref, u_ref, gf_ref, wg_hbm, wu_hbm, wd_hbm,
             dg_ref, du_ref, dx2_ref, dgf_ref, wg_ref, wu_ref, wd_ref):
        _load_weights_once([(wg_hbm, wg_ref), (wu_hbm, wu_ref), (wd_hbm, wd_ref)])

        @pl.when(pl.program_id(0) == 0)
        def _():
            dgf_ref[...] = jnp.zeros_like(dgf_ref)

        dx3v = dx3_ref[...]
        dx3b = dx3v.astype(BF16)
        dhf = jnp.zeros(dx3v.shape, F32)
        for s in range(N_SHARD):
            da = _dot_nt(dx3b, wd_ref[s])
            gv = g_ref[s].astype(F32)
            uv = u_ref[s].astype(F32)
            sg = jax.nn.sigmoid(gv)
            dub = (da * gv * sg).astype(BF16)
            dgb = (da * uv * (sg * (1.0 + gv * (1.0 - sg)))).astype(BF16)
            du_ref[s] = dub
            dg_ref[s] = dgb
            dhf = dhf + _dot_nt(dgb, wg_ref[s]) + _dot_nt(dub, wu_ref[s])
        dx, dgf = _rms_bwd(dhf, x2_ref[...], gf_ref[...])
        dx2_ref[...] = dx3v + dx
        dgf_ref[...] += dgf

    vec = _full((1, 1024))
    return pl.pallas_call(
        body, name="b_ffn", grid=(s_len // tm,),
        in_specs=[_rows(tm, 1024), _rows(tm, 1024), _sh_rows(tm, FF_SH), _sh_rows(tm, FF_SH), vec,
                  _any(), _any(), _any()],
        out_specs=[_sh_rows(tm, FF_SH), _sh_rows(tm, FF_SH), _rows(tm, 1024), vec],
        out_shape=[_sds((N_SHARD, s_len, FF_SH), BF16)] * 2 + [_sds((s_len, 1024), F32), _sds((1, 1024), F32)],
        scratch_shapes=[pltpu.VMEM((N_SHARD, 1024, FF_SH), BF16), pltpu.VMEM((N_SHARD, 1024, FF_SH), BF16),
                        pltpu.VMEM((N_SHARD, FF_SH, 1024), BF16)],
        compiler_params=_cp("arbitrary"))(dx3, x2, gact, uact, g_ffn, wg, wu, wd)


def _b_mid(dx2, qx, x1, att, rec, kx, vx, wo, wq, w_out, g_cross, g_oa, g_ol, tm, comm=None):
    s_len = x1.shape[0]
    m_len = kx.shape[0]

    def body(dx2_ref, qx_ref, x1_ref, att_ref, rec_ref, kx_ref, vx_ref, wo_ref, wq_ref, wout_ref,
             gc_ref, goa_ref, gol_ref,
             dqx_ref, dx1_ref, datt_ref, drec_ref, dkx_ref, dvx_ref, dgc_ref, dgoa_ref, dgol_ref):
        @pl.when(pl.program_id(0) == 0)
        def _():
            for r in (dkx_ref, dvx_ref, dgc_ref, dgoa_ref, dgol_ref):
                r[...] = jnp.zeros_like(r)

        dx2v = dx2_ref[...]
        dox = _dot_nt(dx2v.astype(BF16), wo_ref[...])
        for h in range(X_HEADS):
            sl = slice(h * X_HEAD_DIM, (h + 1) * X_HEAD_DIM)
            q = qx_ref[:, sl]
            p, l = _xattn_probs(q, kx_ref[:, sl])
            pn = p / l
            dob = dox[:, sl].astype(BF16)
            dp = _dot_nt(dob, vx_ref[:, sl])
            dvx_ref[:, sl] += _dot_tn(pn.astype(BF16), dob)
            ds = pn * (dp - jnp.sum(dp * pn, axis=-1, keepdims=True))
            dsb = (ds * X_SCALE).astype(BF16)
            dqx_ref[:, sl] = _dot(dsb, kx_ref[:, sl]).astype(BF16)
            dkx_ref[:, sl] += _dot_tn(dsb, q)
        dhc = _dot_nt(dqx_ref[...], wq_ref[...])
        dx, dgc = _rms_bwd(dhc, x1_ref[...], gc_ref[...])
        dx1 = dx2v + dx
        dx1_ref[...] = dx1
        dgc_ref[...] += dgc
        dmg = _dot_nt(dx1.astype(BF16), wout_ref[...])
        da, dgoa = _rms_bwd(dmg[:, 0:D_ATT], att_ref[...], goa_ref[...])
        datt_ref[...] = da
        dgoa_ref[...] += dgoa
        dr, dgol = _rms_bwd(dmg[:, D_ATT:1024], rec_ref[...], gol_ref[...])
        drec_ref[...] = dr
        dgol_ref[...] += dgol

    sq = _full((1024, 1024))
    mk = _full((m_len, 1024))
    return _call(
        body, "b_mid", (s_len // tm,),
        [_rows(tm, 1024), _rows(tm, 1024), _rows(tm, 1024), _rows(tm, 512), _rows(tm, 512), mk, mk,
         sq, sq, sq, _full((1, 1024)), _full((1, 512)), _full((1, 512))],
        [_rows(tm, 1024), _rows(tm, 1024), _rows(tm, 512), _rows(tm, 512), mk, mk,
         _full((1, 1024)), _full((1, 512)), _full((1, 512))],
        [_sds((s_len, 1024), BF16), _sds((s_len, 1024), F32), _sds((s_len, 512), F32),
         _sds((s_len, 512), F32), _sds((m_len, 1024), F32), _sds((m_len, 1024), F32),
         _sds((1, 1024), F32), _sds((1, 512), F32), _sds((1, 512), F32)],
        [], (dx2, qx, x1, att, rec, kx, vx, wo, wq, w_out, g_cross, g_oa, g_ol), "arbitrary", comm)


def _b_mem(dkx, dvx, mem, mn, g_mem, wk, wv):
    def body(dkx_ref, dvx_ref, mem_ref, mn_ref, g_ref, wk_ref, wv_ref, dwk_ref, dwv_ref, dgm_ref,
             dwkb_ref, dwvb_ref):
        dkb = dkx_ref[...].astype(BF16)
        dvb = dvx_ref[...].astype(BF16)
        dwk = _dot_tn(mn_ref[...], dkb)
        dwv = _dot_tn(mn_ref[...], dvb)
        dwk_ref[...] = dwk
        dwv_ref[...] = dwv
        dwkb_ref[...] = dwk.astype(BF16)
        dwvb_ref[...] = dwv.astype(BF16)
        dmn = _dot_nt(dkb, wk_ref[...]) + _dot_nt(dvb, wv_ref[...])
        mv = mem_ref[...]
        dgm_ref[...] = jnp.sum(dmn * (mv * _rinv(mv)), axis=0, keepdims=True)

    return pl.pallas_call(
        body, name="b_mem",
        out_shape=[_sds((1024, 1024), F32), _sds((1024, 1024), F32), _sds((1, 1024), F32),
                   _sds((1024, 1024), BF16), _sds((1024, 1024), BF16)],
        compiler_params=_cp())(dkx, dvx, mem, mn, g_mem, wk, wv)


def _b_lru(drec, hs, u, xg, conv_w, wrg, brg, wig, big, lam, tl, comm=None):
    s_len = xg.shape[0]
    nt = s_len // tl

    def body(drec_ref, hs_ref, hsp_ref, u_ref, xg_ref, cw_ref, wrg_ref, brg_ref, wig_ref, big_ref, l_ref,
             dxg_ref, dwrg_ref, dwig_ref, dbrg_ref, dbig_ref, dlam_ref, dcw_ref, dcb_ref,
             hbuf, abuf, dubuf, c_sc, d_sc, lam_sc, lcar):
        i = pl.program_id(0)
        tt = nt - 1 - i

        @pl.when(i == 0)
        def _():
            for r in (dwrg_ref, dwig_ref, dbrg_ref, dbig_ref, dlam_ref, dcw_ref, dcb_ref):
                r[...] = jnp.zeros_like(r)
            abuf[tl:tl + 8, :] = jnp.zeros((8, D_LRU), F32)
            dubuf[tl:tl + 8, :] = jnp.zeros((8, D_LRU), F32)
            lcar[...] = jnp.zeros((8, D_LRU), F32)

        xu0 = xg_ref[:, 0:D_LRU]
        hsv = hs_ref[...]
        uv = u_ref[...]
        hbuf[8:8 + tl, :] = hsv
        hbuf[0:8, :] = jnp.where(tt > 0, hsp_ref[...], 0.0)
        hshift = hbuf[pl.ds(7, tl), :]
        wrg_v = wrg_ref[...]
        wig_v = wig_ref[...]
        lamv = l_ref[...]
        ub, r, ig, sp, a, mult = _lru_gates(uv, wrg_v, brg_ref[...], wig_v, big_ref[...], lamv)
        abuf[0:tl, :] = a
        c_sc[...] = abuf[pl.ds(1, tl), :]
        gel, dgel = _gelu_and_grad(xg_ref[:, D_LRU:2 * D_LRU])
        drv = drec_ref[...]
        d_sc[...] = drv * gel
        dxg_ref[:, D_LRU:2 * D_LRU] = (drv * hsv * dgel).astype(BF16)

        def grp(k, lnext):
            off = pl.multiple_of((tl // 8 - 1 - k) * 8, 8)
            l8 = _rscan8(c_sc[pl.ds(off, 8), :], d_sc[pl.ds(off, 8), :], lnext)
            lam_sc[pl.ds(off, 8), :] = l8
            return l8[0:1, :]

        lcar[0:1, :] = lax.fori_loop(0, tl // 8, grp, lcar[0:1, :])
        abuf[tl:tl + 8, :] = a[0:8, :]
        db = lam_sc[...]
        da = db * hshift
        dmult = db * (ig * uv)
        dig = db * mult * uv
        du = db * mult * ig
        dla = da * a - dmult * (a * a) / mult
        dlam_ref[...] += jnp.sum(dla * (-LRU_C) * r, axis=0, keepdims=True)
        dzr = dla * (-LRU_C * sp) * r * (1.0 - r)
        dzi = dig * ig * (1.0 - ig)
        dzrb = dzr.astype(BF16)
        dzib = dzi.astype(BF16)
        du = du + _dot_nt(dzrb, wrg_v) + _dot_nt(dzib, wig_v)
        dwrg_ref[...] += _dot_tn(ub, dzrb)
        dwig_ref[...] += _dot_tn(ub, dzib)
        dbrg_ref[...] += jnp.sum(dzr, axis=0, keepdims=True)
        dbig_ref[...] += jnp.sum(dzi, axis=0, keepdims=True)
        dcb_ref[...] += jnp.sum(du, axis=0, keepdims=True)
        dubuf[0:tl, :] = du
        dxu0 = jnp.zeros((tl, D_LRU), F32)
        for j in range(4):
            dsh = dubuf[pl.ds(3 - j, tl), :]
            dxu0 = dxu0 + cw_ref[j:j + 1, :] * dsh
            dcw_ref[j:j + 1, :] += jnp.sum(xu0 * dsh, axis=0, keepdims=True)
        dubuf[tl:tl + 8, :] = du[0:8, :]
        dxg_ref[:, 0:D_LRU] = dxu0.astype(BF16)

        @pl.when(i == nt - 1)
        def _():
            dlam_ref[...] = dlam_ref[...] * (-jax.nn.sigmoid(-lamv))

    def rev(n):
        return pl.BlockSpec((tl, n), lambda i: (nt - 1 - i, 0))

    prev8 = pl.BlockSpec((8, D_LRU), lambda i: (jnp.maximum((nt - 1 - i) * (tl // 8) - 1, 0), 0))
    vec = _full((1, D_LRU))
    sq = _full((D_LRU, D_LRU))
    return _call(
        body, "b_lru", (nt,),
        [rev(D_LRU), rev(D_LRU), prev8, rev(D_LRU), rev(1024), _full((4, D_LRU)), sq, vec, sq, vec, vec],
        [rev(1024), sq, sq, vec, vec, vec, _full((4, D_LRU)), vec],
        [_sds((s_len, 1024), BF16), _sds((D_LRU, D_LRU), F32), _sds((D_LRU, D_LRU), F32),
         _sds((1, D_LRU), F32), _sds((1, D_LRU), F32), _sds((1, D_LRU), F32),
         _sds((4, D_LRU), F32), _sds((1, D_LRU), F32)],
        [pltpu.VMEM((tl + 8, D_LRU), F32)] * 3 + [pltpu.VMEM((tl, D_LRU), F32)] * 3
        + [pltpu.VMEM((8, D_LRU), F32)],
        (drec, hs, hs, u, xg, conv_w, wrg, brg, wig, big, lam), "arbitrary", comm)


def _b_attn(qkv_pad, att, datt, frow, comm=None):
    s_len = datt.shape[0]
    nb = s_len // QB
    n_pair = ATT_HEADS // 2
    pair_w = 2 * HEAD_DIM

    def body(q_ref, k0, k1, k2, v0, v1, v2, o_ref, do_ref, frow_ref, dq_ref, dkv_ref, dfrow_ref,
             bias_sc, dt_sc, acc_sc):
        t = pl.program_id(0)

        @pl.when(t == 0)
        def _():
            _bias_table(frow_ref, bias_sc)
            dt_sc[...] = jnp.zeros_like(dt_sc)
            acc_sc[...] = jnp.zeros_like(acc_sc)

        @pl.when(t < nb)
        def _():
            var = jnp.minimum(t, N_BIAS - 1)
            even = _even_lanes()
            for hp in range(n_pair):
                cs = slice(hp * pair_w, (hp + 1) * pair_w)
                qt = q_ref[:, cs]
                kts = [k0[:, cs], k1[:, cs], k2[:, cs]]
                vts = [v0[:, cs], v1[:, cs], v2[:, cs]]
                dot = do_ref[:, cs]
                dd = dot * o_ref[:, cs]
                qmt, dost, dsbs, pbs, dqs = [], [], [], [], []
                for e in range(2):
                    keep = even if e == 0 else jnp.logical_not(even)
                    qm = jnp.where(keep, qt, 0)
                    p = _att_probs(qm, kts, bias_sc[var, 2 * hp + e])
                    inv = 1.0 / jnp.sum(p, axis=-1, keepdims=True)
                    dos = jnp.where(keep, dot * inv, 0.0)
                    delta = jnp.sum(jnp.where(keep, dd, 0.0), axis=-1, keepdims=True) * inv
                    dp = jnp.concatenate([_dot_nt(dos.astype(BF16), v) for v in vts], axis=1)
                    ds = p * (dp - delta)
                    dt_sc[2 * hp + e] += ds
                    dsb = ds.astype(BF16)
                    dq = _dot(dsb[:, 0:QB], kts[0])
                    for j in (1, 2):
                        dq = dq + _dot(dsb[:, j * QB:(j + 1) * QB], kts[j])
                    dqs.append(dq)
                    dsbs.append(dsb)
                    pbs.append(p.astype(BF16))
                    qmt.append(qm.astype(F32).T.astype(BF16))
                    dost.append(dos.T.astype(BF16))
                dq_ref[:, cs] = (jnp.where(even, dqs[0], dqs[1]) * ATT_SCALE).astype(BF16)
                for j in range(3):
                    slot = (t + 1 + j) % 3
                    js = slice(j * QB, (j + 1) * QB)
                    acc_sc[slot, hp] += _dot(qmt[0], dsbs[0][:, js]) + _dot(qmt[1], dsbs[1][:, js])
                    acc_sc[slot, n_pair + hp] += _dot(dost[0], pbs[0][:, js]) + _dot(dost[1], pbs[1][:, js])

        done = (t + 1) % 3

        @pl.when(t >= 2)
        def _():
            for i in range(2 * n_pair):
                dkv_ref[:, i * pair_w:(i + 1) * pair_w] = acc_sc[done, i].T.astype(BF16)

        acc_sc[done] = jnp.zeros((2 * n_pair, pair_w, QB), F32)

        @pl.when(t == nb + 1)
        def _():
            row = lax.broadcasted_iota(jnp.int32, (8, ROLL_W), 0)
            pad = jnp.zeros((8, ROLL_W - KB), F32)
            for h in range(ATT_HEADS):
                acc8 = jnp.concatenate([dt_sc[h, 0:8, :], pad], axis=1)
                for a1 in range(1, QB // 8):
                    blk = jnp.concatenate([dt_sc[h, 8 * a1:8 * a1 + 8, :], pad], axis=1)
                    acc8 = acc8 + pltpu.roll(blk, ROLL_W - 8 * a1, 1)
                for k in range(3):
                    acc8 = jnp.where(((row >> k) & 1) == 1, pltpu.roll(acc8, ROLL_W - (1 << k), 1), acc8)
                dfrow_ref[h:h + 1, :] = jnp.sum(acc8, axis=0, keepdims=True)

    clamp = lambda t: jnp.minimum(t, nb - 1)
    qrows = pl.BlockSpec((QB, D_ATT), lambda t: (clamp(t), 0))
    return _call(
        body, "b_attn", (nb + 2,),
        _att_in_specs(clamp) + [qrows, qrows, _full((ATT_HEADS, ROLL_W))],
        [qrows, pl.BlockSpec((QB, 2 * D_ATT), lambda t: (jnp.maximum(t - 2, 0), 0)),
         _full((ATT_HEADS, ROLL_W))],
        [_sds((s_len, D_ATT), BF16), _sds((s_len, 2 * D_ATT), BF16), _sds((ATT_HEADS, ROLL_W), F32)],
        [pltpu.VMEM((N_BIAS, ATT_HEADS, QB, KB), F32), pltpu.VMEM((ATT_HEADS, QB, KB), F32),
         pltpu.VMEM((3, 2 * n_pair, pair_w, QB), F32)],
        (*([qkv_pad] * 7), att, datt, frow), "arbitrary", comm)


def _b_win(dq, dkv, dxg, h, ts):
    s_len = h.shape[0]
    steps = s_len // ts

    def body(dq_ref, dkv_ref, dxg_ref, h_ref, dw_ref, dwb_ref):
        @pl.when(pl.program_id(0) == 0)
        def _():
            dw_ref[...] = jnp.zeros_like(dw_ref)

        dproj = jnp.concatenate([dq_ref[...], dkv_ref[...], dxg_ref[...]], axis=1)
        hv = h_ref[...]
        for s in range(N_SHARD):
            dw_ref[s] += _dot_tn(hv, dproj[:, s * IN_SH:(s + 1) * IN_SH])

        @pl.when(pl.program_id(0) == steps - 1)
        def _():
            dwb_ref[...] = dw_ref[...].astype(BF16)

    wspec = _full((N_SHARD, 1024, IN_SH))
    return pl.pallas_call(
        body, name="b_win", grid=(steps,),
        in_specs=[_rows(ts, 512), _rows(ts, 1024), _rows(ts, 1024), _rows(ts, 1024)],
        out_specs=[wspec, wspec],
        out_shape=[_sds((N_SHARD, 1024, IN_SH), F32), _sds((N_SHARD, 1024, IN_SH), BF16)],
        compiler_params=_cp("arbitrary"))(dq, dkv, dxg, h)


def _b_inproj(dq, dkv, dxg, x, dx1, g_mix, w_in_g, tm, comm=None):
    s_len = x.shape[0]

    def body(dq_ref, dkv_ref, dxg_ref, x_ref, dx1_ref, g_ref, w_ref, gx_ref, dgm_ref):
        @pl.when(pl.program_id(0) == 0)
        def _():
            dgm_ref[...] = jnp.zeros_like(dgm_ref)

        dproj = jnp.concatenate([dq_ref[...], dkv_ref[...], dxg_ref[...]], axis=1)
        dh = jnp.zeros((tm, 1024), F32)
        for s in range(N_SHARD):
            dh = dh + _dot_nt(dproj[:, s * IN_SH:(s + 1) * IN_SH], w_ref[s])
        dx, dgm = _rms_bwd(dh, x_ref[...], g_ref[...])
        gx_ref[...] = dx1_ref[...] + dx
        dgm_ref[...] += dgm

    return _call(
        body, "b_inproj", (s_len // tm,),
        [_rows(tm, 512), _rows(tm, 1024), _rows(tm, 1024), _rows(tm, 1024), _rows(tm, 1024),
         _full((1, 1024)), _full((N_SHARD, 1024, IN_SH))],
        [_rows(tm, 1024), _full((1, 1024))],
        [_sds((s_len, 1024), F32), _sds((1, 1024), F32)],
        [], (dq, dkv, dxg, x, dx1, g_mix, w_in_g), "arbitrary", comm)


def _mm_tn(xa, ya, name, ts):
    s_len, k = xa.shape
    n = ya.shape[1]

    steps = s_len // ts

    def body(x_ref, y_ref, o_ref, ob_ref):
        @pl.when(pl.program_id(0) == 0)
        def _():
            o_ref[...] = jnp.zeros_like(o_ref)
        o_ref[...] += _dot_tn(x_ref[...].astype(BF16), y_ref[...].astype(BF16))

        @pl.when(pl.program_id(0) == steps - 1)
        def _():
            ob_ref[...] = o_ref[...].astype(BF16)

    return pl.pallas_call(
        body, name=name, grid=(steps,), in_specs=[_rows(ts, k), _rows(ts, n)],
        out_specs=[_full((k, n))] * 2, out_shape=[_sds((k, n), F32), _sds((k, n), BF16)],
        compiler_params=_cp("arbitrary"))(xa, ya)


def _mm_tn_ysh(xa, y4, name, ts):
    s_len, k = xa.shape
    n = y4.shape[2]

    steps = s_len // ts

    def body(x_ref, y_ref, o_ref, ob_ref):
        @pl.when(pl.program_id(0) == 0)
        def _():
            o_ref[...] = jnp.zeros_like(o_ref)
        xb = x_ref[...].astype(BF16)
        for s in range(N_SHARD):
            o_ref[s] += _dot_tn(xb, y_ref[s])

        @pl.when(pl.program_id(0) == steps - 1)
        def _():
            ob_ref[...] = o_ref[...].astype(BF16)

    return pl.pallas_call(
        body, name=name, grid=(steps,), in_specs=[_rows(ts, k), _sh_rows(ts, n)],
        out_specs=[_full((N_SHARD, k, n))] * 2,
        out_shape=[_sds((N_SHARD, k, n), F32), _sds((N_SHARD, k, n), BF16)],
        compiler_params=_cp("arbitrary"))(xa, y4)


def _mm_tn_xsh(x4, ya, name, ts):
    s_len, n = ya.shape
    k = x4.shape[2]

    steps = s_len // ts

    def body(x_ref, y_ref, o_ref, ob_ref):
        @pl.when(pl.program_id(0) == 0)
        def _():
            o_ref[...] = jnp.zeros_like(o_ref)
        yb = y_ref[...].astype(BF16)
        for s in range(N_SHARD):
            o_ref[s] += _dot_tn(x_ref[s], yb)

        @pl.when(pl.program_id(0) == steps - 1)
        def _():
            ob_ref[...] = o_ref[...].astype(BF16)

    return pl.pallas_call(
        body, name=name, grid=(steps,), in_specs=[_sh_rows(ts, k), _rows(ts, n)],
        out_specs=[_full((N_SHARD, k, n))] * 2,
        out_shape=[_sds((N_SHARD, k, n), F32), _sds((N_SHARD, k, n), BF16)],
        compiler_params=_cp("arbitrary"))(x4, ya)


def _frow_from_rel_bias(rb):
    hi = jnp.broadcast_to(rb[:, 256:257], (ATT_HEADS, 385))
    mid = rb[:, 1:256][:, ::-1]
    lo = jnp.broadcast_to(rb[:, 0:1], (ATT_HEADS, 128))
    wrap = jnp.broadcast_to(rb[:, 256:257], (ATT_HEADS, ROLL_W - KB))
    return jnp.concatenate([hi, mid, lo, wrap], axis=1)


def _rel_bias_grad_from_dfrow(df):
    g256 = jnp.sum(df[:, 0:385], axis=1, keepdims=True) + jnp.sum(df[:, KB:ROLL_W], axis=1, keepdims=True)
    mid = df[:, 385:640][:, ::-1]
    g0 = jnp.sum(df[:, 640:KB], axis=1, keepdims=True)
    return jnp.concatenate([g0, mid, g256], axis=1)


def _block_diag(w):
    eye = jnp.eye(8, dtype=w.dtype)
    return (w[:, :, None, :] * eye[:, None, :, None]).reshape(D_LRU, D_LRU)


def _block_diag_extract(dense):
    eye = jnp.eye(8, dtype=dense.dtype)
    return jnp.sum(dense.reshape(8, 64, 8, 64) * eye[:, None, :, None], axis=2)


MID = ['w_out', 'wq_c', 'wk_c', 'wv_c', 'wo_c']
AG_IN_INPROJ = ['w_out', 'wq_c', 'wk_c']
AG_IN_ATTN = ['wv_c', 'wo_c', 'w_gate']
AG_IN_LRU = ['w_up']
AG_IN_MID = ['w_down']
RS_IN_MID = ['w_gate', 'w_up']
RS_IN_LRU = ['w_down']
RS_IN_ATTN = MID


def _local_step(x, mem, tgt, p, gw, shards=None, chip=None):
    s_len = x.shape[0]
    tm = min(256, s_len)
    tl = min(512, s_len)
    frow = _frow_from_rel_bias(p['rel_bias'])
    wrg = _block_diag(p['w_rg']).astype(BF16)
    wig = _block_diag(p['w_ig']).astype(BF16)
    gw = dict(gw)

    big, bigb, recv, part, sib = {}, {}, {}, {}, {}

    def ag(names):
        return [] if shards is None else [("ag", [shards[n] for n in names])]

    def rs(names):
        return [] if shards is None else [("rs", [bigb[n] for n in names])]

    def swap(names):
        return [] if shards is None else [("swap", [part[n] for n in names])]

    def reduce_own(names):
        if shards is not None:
            for n in names:
                part[n] = _sum_parts(big[n], recv[n], chip, "sum_" + n)

    h, qkv, xg, *got = _f_inproj(x, p['g_mix'], gw['w_in'], tm, ag(AG_IN_INPROJ))
    gw.update(zip(AG_IN_INPROJ, got))
    qkv_pad = jnp.pad(qkv, ((LEFT_CHUNKS * CHUNK, 0), (0, 0)))
    att, *got = _f_attn(qkv_pad, frow, ag(AG_IN_ATTN))
    gw.update(zip(AG_IN_ATTN, got))
    rec, u, hs, *got = _f_lru(xg, p['conv_w'], p['conv_b'], wrg, p['b_rg'], wig, p['b_ig'], p['lru_L'], tl,
                              ag(AG_IN_LRU))
    gw.update(zip(AG_IN_LRU, got))
    w_out = gw['w_out'].reshape(1024, 1024)
    wq = gw['wq_c'].reshape(1024, 1024)
    wk = gw['wk_c'].reshape(1024, 1024)
    wv = gw['wv_c'].reshape(1024, 1024)
    wo = gw['wo_c'].reshape(1024, 1024)
    mn, kx, vx = _f_mem(mem, p['g_mem'], wk, wv)
    mg, x1, hc, qx, ox, x2, *got = _f_mid(x, att, rec, p['g_out_attn'], p['g_out_lru'], w_out, p['g_cross'],
                                          wq, kx, vx, wo, tm, ag(AG_IN_MID))
    gw.update(zip(AG_IN_MID, got))
    hf, gact, uact, aact, dx3, loss, dg_final = _f_ffn(x2, tgt, p['g_ffn'], p['g_final'],
                                                       gw['w_gate'], gw['w_up'], gw['w_down'], tm)

    ts = min(512, s_len)
    dgact, duact, dx2, dg_ffn = _b_ffn(dx3, x2, gact, uact, p['g_ffn'], gw['w_gate'], gw['w_up'], gw['w_down'], tm)
    big['w_gate'], bigb['w_gate'] = _mm_tn_ysh(hf, dgact, "dw_gate", ts)
    big['w_up'], bigb['w_up'] = _mm_tn_ysh(hf, duact, "dw_up", ts)
    big['w_down'], bigb['w_down'] = _mm_tn_xsh(aact, dx3, "dw_down", ts)

    dqx, dx1, datt, drec, dkx, dvx, dg_cross, dg_oa, dg_ol, *got = _b_mid(
        dx2, qx, x1, att, rec, kx, vx, wo, wq, w_out, p['g_cross'], p['g_out_attn'], p['g_out_lru'], tm,
        rs(RS_IN_MID))
    recv.update(zip(RS_IN_MID, got))
    reduce_own(RS_IN_MID)
    dwk, dwv, dg_mem, dwkb, dwvb = _b_mem(dkx, dvx, mem, mn, p['g_mem'], wk, wv)
    big['wk_c'], bigb['wk_c'] = dwk, dwkb
    big['wv_c'], bigb['wv_c'] = dwv, dwvb
    big['w_out'], bigb['w_out'] = _mm_tn(mg, dx1, "dw_out", ts)
    big['wq_c'], bigb['wq_c'] = _mm_tn(hc, dqx, "dw_q", ts)
    big['wo_c'], bigb['wo_c'] = _mm_tn(ox, dx2, "dw_o", ts)
    for n in MID:
        big[n] = big[n].reshape(N_SHARD, 256, 1024)
        bigb[n] = bigb[n].reshape(N_SHARD, 256, 1024)

    dxg, dwrg, dwig, dbrg, dbig, dlam, dcw, dcb, *got = _b_lru(
        drec, hs, u, xg, p['conv_w'], wrg, p['b_rg'], wig, p['b_ig'], p['lru_L'], tl,
        rs(RS_IN_LRU) + swap(RS_IN_MID))
    recv.update(zip(RS_IN_LRU, got))
    sib.update(zip(RS_IN_MID, got[len(RS_IN_LRU):]))
    reduce_own(RS_IN_LRU)
    dq, dkv, dfrow, *got = _b_attn(qkv_pad, att, datt, frow, rs(RS_IN_ATTN) + swap(RS_IN_LRU))
    recv.update(zip(RS_IN_ATTN, got))
    sib.update(zip(RS_IN_LRU, got[len(RS_IN_ATTN):]))
    reduce_own(RS_IN_ATTN)
    big['w_in'], bigb['w_in'] = _b_win(dq, dkv, dxg, h, ts)
    grad_x, dg_mix, *got = _b_inproj(dq, dkv, dxg, x, dx1, p['g_mix'], gw['w_in'], tm,
                                     rs(['w_in']) + swap(RS_IN_ATTN))
    recv.update(zip(['w_in'], got))
    sib.update(zip(RS_IN_ATTN, got[1:]))
    reduce_own(['w_in'])
    small = {
        'g_mix': dg_mix, 'rel_bias': _rel_bias_grad_from_dfrow(dfrow), 'conv_w': dcw, 'conv_b': dcb,
        'w_rg': _block_diag_extract(dwrg), 'b_rg': dbrg, 'w_ig': _block_diag_extract(dwig), 'b_ig': dbig,
        'lru_L': dlam,
        'g_out_attn': dg_oa, 'g_out_lru': dg_ol, 'g_cross': dg_cross, 'g_mem': dg_mem, 'g_ffn': dg_ffn,
        'g_final': dg_final,
    }
    return jnp.sum(loss[0, 0:1]), grad_x, small, big, part, sib


def _cast_shards(ws):
    def body(*refs):
        n = len(refs) // 2
        for src, dst in zip(refs[:n], refs[n:]):
            dst[...] = src[...].astype(BF16)

    return pl.pallas_call(body, name="cast_shards", out_shape=[_sds(w.shape, BF16) for w in ws],
                          compiler_params=_cp())(*ws)


def _sum_parts(own4, recv3, chip, name):
    _, r, c = own4.shape
    tr = r // 4

    def body(chip_ref, own_ref, rc_ref, o_ref):
        o_ref[...] = ((own_ref[0] + rc_ref[0].astype(F32)) + rc_ref[1].astype(F32)) + rc_ref[2].astype(F32)

    grid_spec = pltpu.PrefetchScalarGridSpec(
        num_scalar_prefetch=1, grid=(4,),
        in_specs=[pl.BlockSpec((1, tr, c), lambda i, ch: (ch[0], i, 0)),
                  pl.BlockSpec((3, tr, c), lambda i, ch: (0, i, 0))],
        out_specs=pl.BlockSpec((tr, c), lambda i, ch: (i, 0)))
    return pl.pallas_call(body, name=name, grid_spec=grid_spec, out_shape=_sds((r, c), F32),
                          compiler_params=_cp("parallel"))(chip, own4, recv3)


def _adamw_math(w, g, m, v):
    m = ADAM_B1 * m + (1.0 - ADAM_B1) * g
    v = ADAM_B2 * v + (1.0 - ADAM_B2) * (g * g)
    m_hat = m / (1.0 - ADAM_B1 ** ADAM_STEP)
    v_hat = v / (1.0 - ADAM_B2 ** ADAM_STEP)
    delta = -ADAM_LR * (m_hat / (jnp.sqrt(v_hat) + ADAM_EPS) + ADAM_WD * w)
    return delta, m, v


def _final_adamw(pa, pb, w, m, v, name):
    r, c = w.shape
    tr = r // 4

    def body(pa_ref, pb_ref, w_ref, m_ref, v_ref, g_ref, d_ref, nm_ref, nv_ref):
        g = pa_ref[...] + pb_ref[...]
        g_ref[...] = g
        d_ref[...], nm_ref[...], nv_ref[...] = _adamw_math(w_ref[...], g, m_ref[...], v_ref[...])

    return pl.pallas_call(
        body, name=name, grid=(4,), in_specs=[_rows(tr, c)] * 5, out_specs=[_rows(tr, c)] * 4,
        out_shape=[_sds((r, c), F32)] * 4, compiler_params=_cp("parallel"))(pa, pb, w, m, v)


def _adamw_small(w, g, m, v):
    def body(w_ref, g_ref, m_ref, v_ref, d_ref, nm_ref, nv_ref):
        d_ref[...], nm_ref[...], nv_ref[...] = _adamw_math(w_ref[...], g_ref[...], m_ref[...], v_ref[...])

    return pl.pallas_call(body, name="adamw_conv_w", out_shape=[_sds(w.shape, F32)] * 3,
                          compiler_params=_cp())(w, g, m, v)


def _ar_small(gp, wp, mp, vp):
    rows = gp.shape[0]

    def body(g_ref, w_ref, m_ref, v_ref, go_ref, d_ref, nm_ref, nv_ref, buf, send_sems, recv_sems):
        x, y, c = _mesh_pos()
        me = 4 * x + 2 * y + c

        def peer(k):
            px = 1 - x if k & 4 else x
            py = 1 - y if k & 2 else y
            pc = 1 - c if k & 1 else c
            return px, py, pc

        def remote(k, slot):
            return pltpu.make_async_remote_copy(
                src_ref=g_ref, dst_ref=buf.at[slot], send_sem=send_sems.at[k - 1], recv_sem=recv_sems.at[k - 1],
                device_id=peer(k), device_id_type=MESH_ID)

        for k in range(1, 8):
            remote(k, me).start()
        buf[me] = g_ref[...]
        for k in range(1, 8):
            px, py, pc = peer(k)
            remote(k, 4 * px + 2 * py + pc).wait_recv()
        for k in range(1, 8):
            remote(k, me).wait_send()
        tot = buf[0]
        for k in range(1, 8):
            tot = tot + buf[k]
        go_ref[...] = tot
        d_ref[...], nm_ref[...], nv_ref[...] = _adamw_math(w_ref[...], tot, m_ref[...], v_ref[...])

    return pl.pallas_call(
        body, name="ar_small", out_shape=[_sds((rows, 128), F32)] * 4,
        scratch_shapes=[pltpu.VMEM((8, rows, 128), F32), pltpu.SemaphoreType.DMA((7,)),
                        pltpu.SemaphoreType.DMA((7,))],
        compiler_params=_cp())(gp, wp, mp, vp)


def _pack(parts):
    flat = jnp.concatenate([parts[n].reshape(-1) for n in SMALL])
    rows = -(-flat.shape[0] // 1024) * 8
    return jnp.pad(flat, (0, rows * 128 - flat.shape[0])).reshape(rows, 128)


def _unpack(pack):
    flat = pack.reshape(-1)
    out, off = {}, 0
    for n in SMALL:
        size = math.prod(SMALL_SHAPES[n])
        out[n] = flat[off:off + size].reshape(SMALL_SHAPES[n])
        off += size
    return out


INPUT_NAMES = (['x', 'mem'] + WEIGHTS + ['loss_target'] + ['m_' + n for n in WEIGHTS] + ['v_' + n for n in WEIGHTS])


def kernel(x, mem, g_mix, w_in, rel_bias, conv_w, conv_b, w_rg, b_rg, w_ig, b_ig, lru_L, g_out_attn, g_out_lru, w_out, g_cross, g_mem, wq_c, wk_c, wv_c, wo_c, g_ffn, w_gate, w_up, w_down, g_final, loss_target, m_g_mix, m_w_in, m_rel_bias, m_conv_w, m_conv_b, m_w_rg, m_b_rg, m_w_ig, m_b_ig, m_lru_L, m_g_out_attn, m_g_out_lru, m_w_out, m_g_cross, m_g_mem, m_wq_c, m_wk_c, m_wv_c, m_wo_c, m_g_ffn, m_w_gate, m_w_up, m_w_down, m_g_final, v_g_mix, v_w_in, v_rel_bias, v_conv_w, v_conv_b, v_w_rg, v_b_rg, v_w_ig, v_b_ig, v_lru_L, v_g_out_attn, v_g_out_lru, v_w_out, v_g_cross, v_g_mem, v_wq_c, v_wk_c, v_wv_c, v_wo_c, v_g_ffn, v_w_gate, v_w_up, v_w_down, v_g_final):
    a = dict(zip(INPUT_NAMES, (x, mem, g_mix, w_in, rel_bias, conv_w, conv_b, w_rg, b_rg, w_ig, b_ig, lru_L, g_out_attn, g_out_lru, w_out, g_cross, g_mem, wq_c, wk_c, wv_c, wo_c, g_ffn, w_gate, w_up, w_down, g_final, loss_target, m_g_mix, m_w_in, m_rel_bias, m_conv_w, m_conv_b, m_w_rg, m_b_rg, m_w_ig, m_b_ig, m_lru_L, m_g_out_attn, m_g_out_lru, m_w_out, m_g_cross, m_g_mem, m_wq_c, m_wk_c, m_wv_c, m_wo_c, m_g_ffn, m_w_gate, m_w_up, m_w_down, m_g_final, v_g_mix, v_w_in, v_rel_bias, v_conv_w, v_conv_b, v_w_rg, v_b_rg, v_w_ig, v_b_ig, v_lru_L, v_g_out_attn, v_g_out_lru, v_w_out, v_g_cross, v_g_mem, v_wq_c, v_wk_c, v_wv_c, v_wo_c, v_g_ffn, v_w_gate, v_w_up, v_w_down, v_g_final)))
    chip = 2 * lax.axis_index("x") + lax.axis_index("y")

    shards = dict(zip(BIG, _cast_shards([a[n][0] for n in BIG])))
    w_in_g, conv_w_g = _comm_only("ag_w_in", [("ag", [shards['w_in'], a['conv_w'][0]])])
    conv_w_full = conv_w_g.transpose(1, 0, 2).reshape(4, D_LRU)

    p = {n: a[n] for n in SMALL}
    p['rel_bias'] = a['rel_bias'][0]
    p['w_rg'] = a['w_rg'][0]
    p['w_ig'] = a['w_ig'][0]
    p['conv_w'] = conv_w_full
    p['g_final'] = a['g_final'][None, :]
    chip_arr = jnp.reshape(chip, (1,)).astype(jnp.int32)
    loss_part, grad_x, small, _, part, sib = _local_step(
        a['x'][0], a['mem'][0], a['loss_target'][0], p, {'w_in': w_in_g}, shards, chip_arr)
    loss = lax.psum(loss_part, ("x", "y", "c"))

    sib['w_in'], = _comm_only("swap_w_in", [("swap", [part['w_in']])])
    out = {}
    for n in BIG:
        out[n] = _final_adamw(part[n], sib[n], a[n][0], a['m_' + n][0], a['v_' + n][0], "adamw_" + n)

    zeros_cw = jnp.zeros((1, 4, D_LRU), F32)
    def packed(prefix):
        d = {n: a[prefix + n] for n in SMALL}
        d['conv_w'] = zeros_cw
        return _pack(d)
    packs = _ar_small(_pack(small), packed(''), packed('m_'), packed('v_'))
    sg, sd, sm, sv = [_unpack(pk) for pk in packs]
    g_cw = lax.dynamic_slice(sg['conv_w'][0], (0, chip * 128), (4, 128))
    d_cw, m_cw, v_cw = _adamw_small(a['conv_w'][0], g_cw, a['m_conv_w'][0], a['v_conv_w'][0])
    sg['conv_w'], sd['conv_w'], sm['conv_w'], sv['conv_w'] = g_cw[None], d_cw[None], m_cw[None], v_cw[None]

    def leaf(i, n):
        if n in BIG:
            return out[n][i][None]
        return (sg, sd, sm, sv)[i][n]

    return (loss, grad_x[None], *[leaf(i, n) for i in range(4) for n in WEIGHTS])
```

```python
import math

import jax
import jax.numpy as jnp
from jax import lax
from jax.experimental import pallas as pl
from jax.experimental.pallas import tpu as pltpu

F32 = jnp.float32
BF16 = jnp.bfloat16

D_MODEL = 1024
D_ATT = 512
D_LRU = 512
HEAD_DIM = 64
ATT_HEADS = 8
CHUNK = 64
LEFT_CHUNKS = 8
X_HEADS = 4
X_HEAD_DIM = 256
N_SHARD = 4
IN_SH = 640
FF_SH = 704
EPS = 1e-6
LRU_C = 8.0
QB = 256
KB = 768
ROLL_W = 1024
NEG = -1e30
ATT_SCALE = HEAD_DIM ** -0.5
X_SCALE = X_HEAD_DIM ** -0.5

ADAM_LR = 0.001
ADAM_B1 = 0.9
ADAM_B2 = 0.999
ADAM_EPS = 1e-08
ADAM_WD = 0.01
ADAM_STEP = 10

VMEM_LIMIT_V7X = 56 * 1024 * 1024
MESH_ID = pl.DeviceIdType.MESH

WEIGHTS = ['g_mix', 'w_in', 'rel_bias', 'conv_w', 'conv_b', 'w_rg', 'b_rg', 'w_ig', 'b_ig', 'lru_L',
           'g_out_attn', 'g_out_lru', 'w_out', 'g_cross', 'g_mem', 'wq_c', 'wk_c', 'wv_c', 'wo_c',
           'g_ffn', 'w_gate', 'w_up', 'w_down', 'g_final']
BIG = ['w_in', 'w_out', 'wq_c', 'wk_c', 'wv_c', 'wo_c', 'w_gate', 'w_up', 'w_down']
SMALL = [n for n in WEIGHTS if n not in BIG]
SMALL_SHAPES = {
    'g_mix': (1, 1024), 'rel_bias': (1, 8, 257), 'conv_w': (1, 4, 512), 'conv_b': (1, 512),
    'w_rg': (1, 8, 64, 64), 'b_rg': (1, 512), 'w_ig': (1, 8, 64, 64), 'b_ig': (1, 512), 'lru_L': (1, 512),
    'g_out_attn': (1, 512), 'g_out_lru': (1, 512), 'g_cross': (1, 1024), 'g_mem': (1, 1024),
    'g_ffn': (1, 1024), 'g_final': (1024,)}


def _sds(shape, dtype):
    return jax.ShapeDtypeStruct(shape, dtype)


def _cp(*sem):
    return pltpu.CompilerParams(dimension_semantics=sem or None, vmem_limit_bytes=VMEM_LIMIT_V7X)


def _rows(tm, n):
    return pl.BlockSpec((tm, n), lambda i: (i, 0))


def _full(shape):
    nd = len(shape)
    return pl.BlockSpec(shape, lambda i: (0,) * nd)


def _dot(a, b):
    return jnp.dot(a, b, preferred_element_type=F32)


def _dot_nt(a, b):
    return lax.dot_general(a, b, (((1,), (1,)), ((), ())), preferred_element_type=F32)


def _dot_tn(a, b):
    return lax.dot_general(a, b, (((0,), (0,)), ((), ())), preferred_element_type=F32)


def _rinv(x):
    return lax.rsqrt(jnp.mean(x * x, axis=-1, keepdims=True) + EPS)


def _rms_bwd(dy, x, g):
    r = _rinv(x)
    yh = x * r
    dyh = dy * g
    dx = r * (dyh - yh * jnp.mean(dyh * yh, axis=-1, keepdims=True))
    return dx, jnp.sum(dy * yh, axis=0, keepdims=True)


def _gelu(x):
    c = math.sqrt(2.0 / math.pi)
    t = jnp.tanh(c * (x + 0.044715 * x * x * x))
    return 0.5 * x * (1.0 + t)


def _gelu_and_grad(x):
    c = math.sqrt(2.0 / math.pi)
    t = jnp.tanh(c * (x + 0.044715 * x * x * x))
    g = 0.5 * x * (1.0 + t)
    dg = 0.5 * (1.0 + t) + 0.5 * x * (1.0 - t * t) * c * (1.0 + 3.0 * 0.044715 * x * x)
    return g, dg


def _neg_expm1(z):
    series = -z * (1 + z / 2 * (1 + z / 3 * (1 + z / 4 * (1 + z / 5 * (1 + z / 6 * (1 + z / 7))))))
    return jnp.where(z > -0.25, series, 1.0 - jnp.exp(z))


def _lru_gates(u, wrg, brg, wig, big, lam):
    ub = u.astype(BF16)
    r = jax.nn.sigmoid(_dot(ub, wrg) + brg)
    ig = jax.nn.sigmoid(_dot(ub, wig) + big)
    sp = jnp.maximum(-lam, 0.0) + jnp.log1p(jnp.exp(-jnp.abs(lam)))
    la = -LRU_C * r * sp
    a = jnp.exp(la)
    mult = jnp.sqrt(jnp.maximum(_neg_expm1(2.0 * la), 0.0))
    return ub, r, ig, sp, a, mult


def _scan8(a8, b8, hprev):
    row = lax.broadcasted_iota(jnp.int32, a8.shape, 0)
    aa, bb = a8, b8
    for d in (1, 2, 4):
        a_s = pltpu.roll(aa, d, 0)
        b_s = pltpu.roll(bb, d, 0)
        m = row >= d
        bb = jnp.where(m, aa * b_s + bb, bb)
        aa = jnp.where(m, aa * a_s, aa)
    return aa * hprev + bb


def _mesh_pos():
    return lax.axis_index("x"), lax.axis_index("y"), lax.axis_index("c")


def _other_chips(x, y):
    return [(1 - x, y), (x, 1 - y), (1 - x, 1 - y)]


def _no_forward():
    pass


def _ag_full_copies(ins, outs, sems):
    send_sems, recv_sems, loc_sems = sems
    n = len(ins)
    x, y, c = _mesh_pos()
    mine = 2 * x + y
    chips = _other_chips(x, y)

    def remote(k, j, slot):
        px, py = chips[j]
        return pltpu.make_async_remote_copy(
            src_ref=ins[k], dst_ref=outs[k].at[slot], send_sem=send_sems.at[k, j], recv_sem=recv_sems.at[k, j],
            device_id=(px, py, c), device_id_type=MESH_ID)

    def local(k):
        return pltpu.make_async_copy(ins[k], outs[k].at[mine], loc_sems.at[k])

    def start():
        for k in range(n):
            local(k).start()
            for j in range(3):
                remote(k, j, mine).start()

    def wait():
        for k in range(n):
            for j, (px, py) in enumerate(chips):
                remote(k, j, 2 * px + py).wait_recv()
        for k in range(n):
            for j in range(3):
                remote(k, j, mine).wait_send()
            local(k).wait()

    return start, _no_forward, wait


def _ag_copies(ins, outs, sems):
    send_sems, recv_sems, fsend_sems, frecv_sems, loc_sems = sems
    n = len(ins)
    x, y, c = _mesh_pos()
    mine = 2 * x + y
    chips = _other_chips(x, y)

    def half(ref, hc):
        r = ref.shape[0] // 2
        return ref.at[pl.ds(pl.multiple_of(hc * r, 16), r)]

    def ici(k, j, slot):
        px, py = chips[j]
        return pltpu.make_async_remote_copy(
            src_ref=half(ins[k], c), dst_ref=half(outs[k].at[slot], c),
            send_sem=send_sems.at[k, j], recv_sem=recv_sems.at[k, j],
            device_id=(px, py, c), device_id_type=MESH_ID)

    def d2d(k, j, hc):
        px, py = chips[j]
        part = half(outs[k].at[2 * px + py], hc)
        return pltpu.make_async_remote_copy(
            src_ref=part, dst_ref=part, send_sem=fsend_sems.at[k, j], recv_sem=frecv_sems.at[k, j],
            device_id=(x, y, 1 - c), device_id_type=MESH_ID)

    def local(k):
        return pltpu.make_async_copy(ins[k], outs[k].at[mine], loc_sems.at[k])

    def start():
        for k in range(n):
            local(k).start()
            for j in range(3):
                ici(k, j, mine).start()

    def forward():
        for k in range(n):
            for j, (px, py) in enumerate(chips):
                ici(k, j, 2 * px + py).wait_recv()
                d2d(k, j, c).start()

    def wait():
        for k in range(n):
            for j in range(3):
                d2d(k, j, 1 - c).wait_recv()
        for k in range(n):
            for j in range(3):
                d2d(k, j, c).wait_send()
                ici(k, j, mine).wait_send()
            local(k).wait()

    return start, forward, wait


def _rs_copies(ins, outs, sems):
    send_sems, recv_sems = sems
    n = len(ins)
    x, y, c = _mesh_pos()
    chips = _other_chips(x, y)

    def remote(k, j):
        px, py = chips[j]
        return pltpu.make_async_remote_copy(
            src_ref=ins[k].at[2 * px + py], dst_ref=outs[k].at[j],
            send_sem=send_sems.at[k, j], recv_sem=recv_sems.at[k, j],
            device_id=(px, py, c), device_id_type=MESH_ID)

    def start():
        for k in range(n):
            for j in range(3):
                remote(k, j).start()

    def wait():
        for k in range(n):
            for j in range(3):
                remote(k, j).wait_recv()
        for k in range(n):
            for j in range(3):
                remote(k, j).wait_send()

    return start, _no_forward, wait


def _swap_copies(ins, outs, sems):
    send_sems, recv_sems = sems
    x, y, c = _mesh_pos()
    copies = [pltpu.make_async_remote_copy(
        src_ref=ins[k], dst_ref=outs[k], send_sem=send_sems.at[k], recv_sem=recv_sems.at[k],
        device_id=(x, y, 1 - c), device_id_type=MESH_ID) for k in range(len(ins))]

    def start():
        for cp in copies:
            cp.start()

    def wait():
        for cp in copies:
            cp.wait()

    return start, _no_forward, wait


def _comm_plan(groups):
    plan, arrs, shapes, sems = [], [], [], []
    for kind, group in groups:
        k = len(group)
        arrs += group
        per_peer = pltpu.SemaphoreType.DMA((k, 3))
        if kind == "ag":
            shapes += [_sds((N_SHARD,) + w.shape, w.dtype) for w in group]
            gsems = [per_peer] * 4 + [pltpu.SemaphoreType.DMA((k,))]
            maker = _ag_copies
        elif kind == "agf":
            shapes += [_sds((N_SHARD,) + w.shape, w.dtype) for w in group]
            gsems = [per_peer] * 2 + [pltpu.SemaphoreType.DMA((k,))]
            maker = _ag_full_copies
        elif kind == "rs":
            shapes += [_sds((3,) + g.shape[1:], g.dtype) for g in group]
            gsems = [pltpu.SemaphoreType.DMA((k, 3)), pltpu.SemaphoreType.DMA((k, 3))]
            maker = _rs_copies
        else:
            shapes += [_sds(g.shape, g.dtype) for g in group]
            gsems = [pltpu.SemaphoreType.DMA((k,)), pltpu.SemaphoreType.DMA((k,))]
            maker = _swap_copies
        plan.append((maker, k, len(gsems)))
        sems += gsems
    return plan, arrs, shapes, sems


def _comm_fns(plan, cins, couts, sems):
    fns, a, s = [], 0, 0
    for maker, k, ns in plan:
        fns.append(maker(cins[a:a + k], couts[a:a + k], sems[s:s + ns]))
        a += k
        s += ns

    def start():
        for st, _, _ in fns:
            st()

    def forward():
        for _, fw, _ in fns:
            fw()

    def wait():
        for _, _, wt in fns:
            wt()

    return start, forward, wait


def _call(body, name, grid, in_specs, out_specs, out_shape, scratch, args, sem, comm=None):
    if not comm:
        return pl.pallas_call(body, name=name, grid=grid, in_specs=in_specs, out_specs=out_specs,
                              out_shape=out_shape, scratch_shapes=scratch, compiler_params=_cp(sem))(*args)
    plan, c_arrs, c_shapes, c_sems = _comm_plan(comm)
    k = len(c_arrs)
    n_in, n_out, n_scr = len(in_specs), len(out_specs), len(scratch)
    last = grid[0] - 1
    fwd_step = max(1, (2 * last) // 3)

    def wrapped(*refs):
        ins, cins = refs[:n_in], refs[n_in:n_in + k]
        o0 = n_in + k
        outs, couts = refs[o0:o0 + n_out], refs[o0 + n_out:o0 + n_out + k]
        s0 = o0 + n_out + k
        start, forward, wait = _comm_fns(plan, cins, couts, refs[s0 + n_scr:])
        pl.when(pl.program_id(0) == 0)(start)
        pl.when(pl.program_id(0) == fwd_step)(forward)
        body(*ins, *outs, *refs[s0:s0 + n_scr])
        pl.when(pl.program_id(0) == last)(wait)

    return pl.pallas_call(
        wrapped, name=name, grid=grid, in_specs=list(in_specs) + [_any()] * k,
        out_specs=list(out_specs) + [_any()] * k, out_shape=list(out_shape) + c_shapes,
        scratch_shapes=list(scratch) + c_sems, compiler_params=_cp(sem))(*args, *c_arrs)


def _comm_only(name, comm):
    plan, c_arrs, c_shapes, c_sems = _comm_plan(comm)
    k = len(c_arrs)

    def body(*refs):
        start, forward, wait = _comm_fns(plan, refs[:k], refs[k:2 * k], refs[2 * k:])
        start()
        forward()
        wait()

    return pl.pallas_call(body, name=name, in_specs=[_any()] * k, out_specs=[_any()] * k, out_shape=c_shapes,
                          scratch_shapes=c_sems, compiler_params=_cp())(*c_arrs)


def _any():
    return pl.BlockSpec(memory_space=pl.ANY)


def _rscan8(c8, d8, lnext):
    row = lax.broadcasted_iota(jnp.int32, c8.shape, 0)
    cc, dd = c8, d8
    for d in (1, 2, 4):
        c_s = pltpu.roll(cc, 8 - d, 0)
        d_s = pltpu.roll(dd, 8 - d, 0)
        m = row < 8 - d
        dd = jnp.where(m, cc * d_s + dd, dd)
        cc = jnp.where(m, cc * c_s, cc)
    return cc * lnext + dd


def _f_inproj(x, g_mix, w_in_g, tm, comm=None):
    s_len = x.shape[0]

    def body(x_ref, g_ref, w_ref, h_ref, qkv_ref, xg_ref):
        xv = x_ref[...]
        h = (xv * _rinv(xv) * g_ref[...]).astype(BF16)
        h_ref[...] = h
        p0 = _dot(h, w_ref[0])
        qkv_ref[:, 0:D_ATT] = (p0[:, 0:D_ATT] * ATT_SCALE).astype(BF16)
        qkv_ref[:, D_ATT:640] = p0[:, D_ATT:640].astype(BF16)
        qkv_ref[:, 640:1280] = _dot(h, w_ref[1]).astype(BF16)
        p2 = _dot(h, w_ref[2])
        qkv_ref[:, 1280:1536] = p2[:, 0:256].astype(BF16)
        xg_ref[:, 0:384] = p2[:, 256:640]
        xg_ref[:, 384:1024] = _dot(h, w_ref[3])

    return _call(
        body, "f_inproj", (s_len // tm,),
        [_rows(tm, 1024), _full((1, 1024)), _full((N_SHARD, 1024, IN_SH))],
        [_rows(tm, 1024), _rows(tm, 1536), _rows(tm, 1024)],
        [_sds((s_len, 1024), BF16), _sds((s_len, 1536), BF16), _sds((s_len, 1024), F32)],
        [], (x, g_mix, w_in_g), "arbitrary", comm)


N_BIAS = 3


def _bias_table(frow_ref, bias_sc):
    qa = lax.broadcasted_iota(jnp.int32, (QB, KB), 0) // CHUNK
    kcol = lax.broadcasted_iota(jnp.int32, (QB, KB), 1)
    kb = kcol // CHUNK
    band = jnp.where((kb >= qa) & (kb - qa <= LEFT_CHUNKS), 0.0, NEG).astype(F32)
    for h in range(ATT_HEADS):
        row = jnp.broadcast_to(frow_ref[h:h + 1, :], (QB, ROLL_W))
        toep = pltpu.roll(row, 0, 1, stride=1, stride_axis=0)
        gen = toep[:, 0:KB] + band
        bias_sc[N_BIAS - 1, h] = gen
        for v in range(N_BIAS - 1):
            pad_keys = LEFT_CHUNKS * CHUNK - v * QB
            bias_sc[v, h] = gen + jnp.where(kcol < pad_keys, NEG, 0.0).astype(F32)


def _even_lanes():
    return lax.broadcasted_iota(jnp.int32, (1, 2 * HEAD_DIM), 1) < HEAD_DIM


def _att_probs(qm, kts, bias):
    s = jnp.concatenate([_dot_nt(qm, k) for k in kts], axis=1) + bias
    return jnp.exp(s - jnp.max(s, axis=-1, keepdims=True))


def _att_in_specs(clamp):
    def spec(j, col):
        return pl.BlockSpec((QB, D_ATT), lambda i: (clamp(i) + j, col))
    return [spec(2, 0), spec(0, 1), spec(1, 1), spec(2, 1), spec(0, 2), spec(1, 2), spec(2, 2)]


def _f_attn(qkv_pad, frow, comm=None):
    s_len = qkv_pad.shape[0] - LEFT_CHUNKS * CHUNK
    nb = s_len // QB

    def body(q_ref, k0, k1, k2, v0, v1, v2, frow_ref, o_ref, bias_sc):
        i = pl.program_id(0)

        @pl.when(i == 0)
        def _():
            _bias_table(frow_ref, bias_sc)

        var = jnp.minimum(i, N_BIAS - 1)
        even = _even_lanes()
        for hp in range(ATT_HEADS // 2):
            cs = slice(hp * 2 * HEAD_DIM, (hp + 1) * 2 * HEAD_DIM)
            qt = q_ref[:, cs]
            kts = [k0[:, cs], k1[:, cs], k2[:, cs]]
            vts = [v0[:, cs], v1[:, cs], v2[:, cs]]
            res = []
            for e in range(2):
                keep = even if e == 0 else jnp.logical_not(even)
                pb = _att_probs(jnp.where(keep, qt, 0), kts, bias_sc[var, 2 * hp + e]).astype(BF16)
                r = _dot(pb[:, 0:QB], jnp.where(keep, vts[0], 1))
                for j in (1, 2):
                    r = r + _dot(pb[:, j * QB:(j + 1) * QB], jnp.where(keep, vts[j], 1))
                res.append(r / pltpu.roll(r, HEAD_DIM, 1))
            o_ref[:, cs] = jnp.where(even, res[0], res[1])

    return _call(
        body, "f_attn", (nb,),
        _att_in_specs(lambda i: i) + [_full((ATT_HEADS, ROLL_W))],
        [_rows(QB, D_ATT)], [_sds((s_len, D_ATT), F32)],
        [pltpu.VMEM((N_BIAS, ATT_HEADS, QB, KB), F32)], (*([qkv_pad] * 7), frow), "arbitrary", comm)


def _f_lru(xg, conv_w, conv_b, wrg, brg, wig, big, lam, tl, comm=None):
    s_len = xg.shape[0]

    def body(xg_ref, cw_ref, cb_ref, wrg_ref, brg_ref, wig_ref, big_ref, l_ref,
             rec_ref, u_ref, hs_ref, xbuf, a_sc, b_sc, hcar):
        i = pl.program_id(0)

        @pl.when(i == 0)
        def _():
            xbuf[0:8, :] = jnp.zeros((8, D_LRU), F32)
            hcar[...] = jnp.zeros((8, D_LRU), F32)

        xu0 = xg_ref[:, 0:D_LRU]
        xbuf[8:8 + tl, :] = xu0
        u = cb_ref[...] + cw_ref[0:1, :] * xbuf[pl.ds(5, tl), :]
        for j in range(1, 4):
            u = u + cw_ref[j:j + 1, :] * xbuf[pl.ds(5 + j, tl), :]
        xbuf[0:8, :] = xu0[tl - 8:tl, :]
        u_ref[...] = u
        _, _, ig, _, a, mult = _lru_gates(u, wrg_ref[...], brg_ref[...], wig_ref[...], big_ref[...], l_ref[...])
        a_sc[...] = a
        b_sc[...] = mult * (ig * u)

        def grp(g, hprev):
            off = pl.multiple_of(g * 8, 8)
            h8 = _scan8(a_sc[pl.ds(off, 8), :], b_sc[pl.ds(off, 8), :], hprev)
            hs_ref[pl.ds(off, 8), :] = h8
            return h8[7:8, :]

        hcar[0:1, :] = lax.fori_loop(0, tl // 8, grp, hcar[0:1, :])
        rec_ref[...] = hs_ref[...] * _gelu(xg_ref[:, D_LRU:2 * D_LRU])

    vec = _full((1, D_LRU))
    return _call(
        body, "f_lru", (s_len // tl,),
        [_rows(tl, 1024), _full((4, D_LRU)), vec, _full((D_LRU, D_LRU)), vec, _full((D_LRU, D_LRU)), vec, vec],
        [_rows(tl, D_LRU)] * 3, [_sds((s_len, D_LRU), F32)] * 3,
        [pltpu.VMEM((tl + 8, D_LRU), F32), pltpu.VMEM((tl, D_LRU), F32),
         pltpu.VMEM((tl, D_LRU), F32), pltpu.VMEM((8, D_LRU), F32)],
        (xg, conv_w, conv_b, wrg, brg, wig, big, lam), "arbitrary", comm)


def _f_mem(mem, g_mem, wk, wv):
    def body(mem_ref, g_ref, wk_ref, wv_ref, mn_ref, kx_ref, vx_ref):
        mv = mem_ref[...]
        mn = (mv * _rinv(mv) * g_ref[...]).astype(BF16)
        mn_ref[...] = mn
        kx_ref[...] = _dot(mn, wk_ref[...]).astype(BF16)
        vx_ref[...] = _dot(mn, wv_ref[...]).astype(BF16)

    m = mem.shape[0]
    return pl.pallas_call(
        body, name="f_mem", out_shape=[_sds((m, 1024), BF16)] * 3,
        compiler_params=_cp())(mem, g_mem, wk, wv)


def _xattn_probs(q, k):
    s = _dot_nt(q, k) * X_SCALE
    m = jnp.max(s, axis=-1, keepdims=True)
    p = jnp.exp(s - m)
    return p, jnp.sum(p, axis=-1, keepdims=True)


def _f_mid(x, att, rec, g_oa, g_ol, w_out, g_cross, wq, kx, vx, wo, tm, comm=None):
    s_len = x.shape[0]
    m_len = kx.shape[0]

    def body(x_ref, att_ref, rec_ref, goa_ref, gol_ref, wout_ref, gc_ref, wq_ref, kx_ref, vx_ref, wo_ref,
             mg_ref, x1_ref, hc_ref, qx_ref, ox_ref, x2_ref):
        av = att_ref[...]
        rv = rec_ref[...]
        mg_ref[:, 0:D_ATT] = (av * _rinv(av) * goa_ref[...]).astype(BF16)
        mg_ref[:, D_ATT:1024] = (rv * _rinv(rv) * gol_ref[...]).astype(BF16)
        x1 = x_ref[...] + _dot(mg_ref[...], wout_ref[...])
        x1_ref[...] = x1
        hc = (x1 * _rinv(x1) * gc_ref[...]).astype(BF16)
        hc_ref[...] = hc
        qx_ref[...] = _dot(hc, wq_ref[...]).astype(BF16)
        for h in range(X_HEADS):
            sl = slice(h * X_HEAD_DIM, (h + 1) * X_HEAD_DIM)
            p, l = _xattn_probs(qx_ref[:, sl], kx_ref[:, sl])
            ox_ref[:, sl] = (_dot(p.astype(BF16), vx_ref[:, sl]) / l).astype(BF16)
        x2_ref[...] = x1 + _dot(ox_ref[...], wo_ref[...])

    sq = _full((1024, 1024))
    return _call(
        body, "f_mid", (s_len // tm,),
        [_rows(tm, 1024), _rows(tm, 512), _rows(tm, 512), _full((1, 512)), _full((1, 512)), sq,
         _full((1, 1024)), sq, _full((m_len, 1024)), _full((m_len, 1024)), sq],
        [_rows(tm, 1024)] * 6,
        [_sds((s_len, 1024), BF16), _sds((s_len, 1024), F32), _sds((s_len, 1024), BF16),
         _sds((s_len, 1024), BF16), _sds((s_len, 1024), BF16), _sds((s_len, 1024), F32)],
        [], (x, att, rec, g_oa, g_ol, w_out, g_cross, wq, kx, vx, wo), "arbitrary", comm)


def _load_weights_once(pairs):
    @pl.when(pl.program_id(0) == 0)
    def _():
        for hbm, vmem in pairs:
            pltpu.sync_copy(hbm, vmem)


def _sh_rows(tm, n):
    return pl.BlockSpec((N_SHARD, tm, n), lambda i: (0, i, 0))


def _f_ffn(x2, tgt, g_ffn, g_final, wg, wu, wd, tm):
    s_len = x2.shape[0]

    def body(x2_ref, t_ref, gf_ref, gfin_ref, wg_hbm, wu_hbm, wd_hbm,
             hf_ref, g_ref, u_ref, a_ref, dx3_ref, loss_ref, dgfin_ref, wg_ref, wu_ref, wd_ref):
        _load_weights_once([(wg_hbm, wg_ref), (wu_hbm, wu_ref), (wd_hbm, wd_ref)])

        @pl.when(pl.program_id(0) == 0)
        def _():
            loss_ref[...] = jnp.zeros_like(loss_ref)
            dgfin_ref[...] = jnp.zeros_like(dgfin_ref)

        x2v = x2_ref[...]
        hf = (x2v * _rinv(x2v) * gf_ref[...]).astype(BF16)
        hf_ref[...] = hf
        x3 = x2v
        for s in range(N_SHARD):
            gv = _dot(hf, wg_ref[s])
            uv = _dot(hf, wu_ref[s])
            av = (gv * jax.nn.sigmoid(gv) * uv).astype(BF16)
            g_ref[s] = gv.astype(BF16)
            u_ref[s] = uv.astype(BF16)
            a_ref[s] = av
            x3 = x3 + _dot(av, wd_ref[s])
        r3 = _rinv(x3)
        yh = x3 * r3
        gfin = gfin_ref[...]
        err = yh * gfin - t_ref[...]
        loss_ref[...] += jnp.full((1, 128), 0.5 / D_MODEL, F32) * jnp.sum(err * err)
        dy = err * (1.0 / D_MODEL)
        dgfin_ref[...] += jnp.sum(dy * yh, axis=0, keepdims=True)
        dyh = dy * gfin
        dx3_ref[...] = r3 * (dyh - yh * jnp.mean(dyh * yh, axis=-1, keepdims=True))

    vec = _full((1, 1024))
    return pl.pallas_call(
        body, name="f_ffn", grid=(s_len // tm,),
        in_specs=[_rows(tm, 1024), _rows(tm, 1024), vec, vec, _any(), _any(), _any()],
        out_specs=[_rows(tm, 1024), _sh_rows(tm, FF_SH), _sh_rows(tm, FF_SH), _sh_rows(tm, FF_SH),
                   _rows(tm, 1024), _full((1, 128)), vec],
        out_shape=[_sds((s_len, 1024), BF16)] + [_sds((N_SHARD, s_len, FF_SH), BF16)] * 3
                  + [_sds((s_len, 1024), F32), _sds((1, 128), F32), _sds((1, 1024), F32)],
        scratch_shapes=[pltpu.VMEM((N_SHARD, 1024, FF_SH), BF16), pltpu.VMEM((N_SHARD, 1024, FF_SH), BF16),
                        pltpu.VMEM((N_SHARD, FF_SH, 1024), BF16)],
        compiler_params=_cp("arbitrary"))(x2, tgt, g_ffn, g_final, wg, wu, wd)


def _b_ffn(dx3, x2, gact, uact, g_ffn, wg, wu, wd, tm):
    s_len = x2.shape[0]

    def body(dx3_ref, x2_ref, g_ref, u_ref, gf_ref, wg_hbm, wu_hbm, wd_hbm,
             dg_ref, du_ref, dx2_ref, dgf_ref, wg_ref, wu_ref, wd_ref):
        _load_weights_once([(wg_hbm, wg_ref), (wu_hbm, wu_ref), (wd_hbm, wd_ref)])

        @pl.when(pl.program_id(0) == 0)
        def _():
            dgf_ref[...] = jnp.zeros_like(dgf_ref)

        dx3v = dx3_ref[...]
        dx3b = dx3v.astype(BF16)
        dhf = jnp.zeros(dx3v.shape, F32)
        for s in range(N_SHARD):
            da = _dot_nt(dx3b, wd_ref[s])
            gv = g_ref[s].astype(F32)
            uv = u_ref[s].astype(F32)
            sg = jax.nn.sigmoid(gv)
            dub = (da * gv * sg).astype(BF16)
            dgb = (da * uv * (sg * (1.0 + gv * (1.0 - sg)))).astype(BF16)
            du_ref[s] = dub
            dg_ref[s] = dgb
            dhf = dhf + _dot_nt(dgb, wg_ref[s]) + _dot_nt(dub, wu_ref[s])
        dx, dgf = _rms_bwd(dhf, x2_ref[...], gf_ref[...])
        dx2_ref[...] = dx3v + dx
        dgf_ref[...] += dgf

    vec = _full((1, 1024))
    return pl.pallas_call(
        body, name="b_ffn", grid=(s_len // tm,),
        in_specs=[_rows(tm, 1024), _rows(tm, 1024), _sh_rows(tm, FF_SH), _sh_rows(tm, FF_SH), vec,
                  _any(), _any(), _any()],
        out_specs=[_sh_rows(tm, FF_SH), _sh_rows(tm, FF_SH), _rows(tm, 1024), vec],
        out_shape=[_sds((N_SHARD, s_len, FF_SH), BF16)] * 2 + [_sds((s_len, 1024), F32), _sds((1, 1024), F32)],
        scratch_shapes=[pltpu.VMEM((N_SHARD, 1024, FF_SH), BF16), pltpu.VMEM((N_SHARD, 1024, FF_SH), BF16),
                        pltpu.VMEM((N_SHARD, FF_SH, 1024), BF16)],
        compiler_params=_cp("arbitrary"))(dx3, x2, gact, uact, g_ffn, wg, wu, wd)


def _b_mid(dx2, qx, x1, att, rec, kx, vx, wo, wq, w_out, g_cross, g_oa, g_ol, tm, comm=None):
    s_len = x1.shape[0]
    m_len = kx.shape[0]

    def body(dx2_ref, qx_ref, x1_ref, att_ref, rec_ref, kx_ref, vx_ref, wo_ref, wq_ref, wout_ref,
             gc_ref, goa_ref, gol_ref,
             dqx_ref, dx1_ref, datt_ref, drec_ref, dkx_ref, dvx_ref, dgc_ref, dgoa_ref, dgol_ref):
        @pl.when(pl.program_id(0) == 0)
        def _():
            for r in (dkx_ref, dvx_ref, dgc_ref, dgoa_ref, dgol_ref):
                r[...] = jnp.zeros_like(r)

        dx2v = dx2_ref[...]
        dox = _dot_nt(dx2v.astype(BF16), wo_ref[...])
        for h in range(X_HEADS):
            sl = slice(h * X_HEAD_DIM, (h + 1) * X_HEAD_DIM)
            q = qx_ref[:, sl]
            p, l = _xattn_probs(q, kx_ref[:, sl])
            pn = p / l
            dob = dox[:, sl].astype(BF16)
            dp = _dot_nt(dob, vx_ref[:, sl])
            dvx_ref[:, sl] += _dot_tn(pn.astype(BF16), dob)
            ds = pn * (dp - jnp.sum(dp * pn, axis=-1, keepdims=True))
            dsb = (ds * X_SCALE).astype(BF16)
            dqx_ref[:, sl] = _dot(dsb, kx_ref[:, sl]).astype(BF16)
            dkx_ref[:, sl] += _dot_tn(dsb, q)
        dhc = _dot_nt(dqx_ref[...], wq_ref[...])
        dx, dgc = _rms_bwd(dhc, x1_ref[...], gc_ref[...])
        dx1 = dx2v + dx
        dx1_ref[...] = dx1
        dgc_ref[...] += dgc
        dmg = _dot_nt(dx1.astype(BF16), wout_ref[...])
        da, dgoa = _rms_bwd(dmg[:, 0:D_ATT], att_ref[...], goa_ref[...])
        datt_ref[...] = da
        dgoa_ref[...] += dgoa
        dr, dgol = _rms_bwd(dmg[:, D_ATT:1024], rec_ref[...], gol_ref[...])
        drec_ref[...] = dr
        dgol_ref[...] += dgol

    sq = _full((1024, 1024))
    mk = _full((m_len, 1024))
    return _call(
        body, "b_mid", (s_len // tm,),
        [_rows(tm, 1024), _rows(tm, 1024), _rows(tm, 1024), _rows(tm, 512), _rows(tm, 512), mk, mk,
         sq, sq, sq, _full((1, 1024)), _full((1, 512)), _full((1, 512))],
        [_rows(tm, 1024), _rows(tm, 1024), _rows(tm, 512), _rows(tm, 512), mk, mk,
         _full((1, 1024)), _full((1, 512)), _full((1, 512))],
        [_sds((s_len, 1024), BF16), _sds((s_len, 1024), F32), _sds((s_len, 512), F32),
         _sds((s_len, 512), F32), _sds((m_len, 1024), F32), _sds((m_len, 1024), F32),
         _sds((1, 1024), F32), _sds((1, 512), F32), _sds((1, 512), F32)],
        [], (dx2, qx, x1, att, rec, kx, vx, wo, wq, w_out, g_cross, g_oa, g_ol), "arbitrary", comm)


def _b_mem(dkx, dvx, mem, mn, g_mem, wk, wv):
    def body(dkx_ref, dvx_ref, mem_ref, mn_ref, g_ref, wk_ref, wv_ref, dwk_ref, dwv_ref, dgm_ref,
             dwkb_ref, dwvb_ref):
        dkb = dkx_ref[...].astype(BF16)
        dvb = dvx_ref[...].astype(BF16)
        dwk = _dot_tn(mn_ref[...], dkb)
        dwv = _dot_tn(mn_ref[...], dvb)
        dwk_ref[...] = dwk
        dwv_ref[...] = dwv
        dwkb_ref[...] = dwk.astype(BF16)
        dwvb_ref[...] = dwv.astype(BF16)
        dmn = _dot_nt(dkb, wk_ref[...]) + _dot_nt(dvb, wv_ref[...])
        mv = mem_ref[...]
        dgm_ref[...] = jnp.sum(dmn * (mv * _rinv(mv)), axis=0, keepdims=True)

    return pl.pallas_call(
        body, name="b_mem",
        out_shape=[_sds((1024, 1024), F32), _sds((1024, 1024), F32), _sds((1, 1024), F32),
                   _sds((1024, 1024), BF16), _sds((1024, 1024), BF16)],
        compiler_params=_cp())(dkx, dvx, mem, mn, g_mem, wk, wv)


def _b_lru(drec, hs, u, xg, conv_w, wrg, brg, wig, big, lam, tl, comm=None):
    s_len = xg.shape[0]
    nt = s_len // tl

    def body(drec_ref, hs_ref, hsp_ref, u_ref, xg_ref, cw_ref, wrg_ref, brg_ref, wig_ref, big_ref, l_ref,
             dxg_ref, dwrg_ref, dwig_ref, dbrg_ref, dbig_ref, dlam_ref, dcw_ref, dcb_ref,
             hbuf, abuf, dubuf, c_sc, d_sc, lam_sc, lcar):
        i = pl.program_id(0)
        tt = nt - 1 - i

        @pl.when(i == 0)
        def _():
            for r in (dwrg_ref, dwig_ref, dbrg_ref, dbig_ref, dlam_ref, dcw_ref, dcb_ref):
                r[...] = jnp.zeros_like(r)
            abuf[tl:tl + 8, :] = jnp.zeros((8, D_LRU), F32)
            dubuf[tl:tl + 8, :] = jnp.zeros((8, D_LRU), F32)
            lcar[...] = jnp.zeros((8, D_LRU), F32)

        xu0 = xg_ref[:, 0:D_LRU]
        hsv = hs_ref[...]
        uv = u_ref[...]
        hbuf[8:8 + tl, :] = hsv
        hbuf[0:8, :] = jnp.where(tt > 0, hsp_ref[...], 0.0)
        hshift = hbuf[pl.ds(7, tl), :]
        wrg_v = wrg_ref[...]
        wig_v = wig_ref[...]
        lamv = l_ref[...]
        ub, r, ig, sp, a, mult = _lru_gates(uv, wrg_v, brg_ref[...], wig_v, big_ref[...], lamv)
        abuf[0:tl, :] = a
        c_sc[...] = abuf[pl.ds(1, tl), :]
        gel, dgel = _gelu_and_grad(xg_ref[:, D_LRU:2 * D_LRU])
        drv = drec_ref[...]
        d_sc[...] = drv * gel
        dxg_ref[:, D_LRU:2 * D_LRU] = (drv * hsv * dgel).astype(BF16)

        def grp(k, lnext):
            off = pl.multiple_of((tl // 8 - 1 - k) * 8, 8)
            l8 = _rscan8(c_sc[pl.ds(off, 8), :], d_sc[pl.ds(off, 8), :], lnext)
            lam_sc[pl.ds(off, 8), :] = l8
            return l8[0:1, :]

        lcar[0:1, :] = lax.fori_loop(0, tl // 8, grp, lcar[0:1, :])
        abuf[tl:tl + 8, :] = a[0:8, :]
        db = lam_sc[...]
        da = db * hshift
        dmult = db * (ig * uv)
        dig = db * mult * uv
        du = db * mult * ig
        dla = da * a - dmult * (a * a) / mult
        dlam_ref[...] += jnp.sum(dla * (-LRU_C) * r, axis=0, keepdims=True)
        dzr = dla * (-LRU_C * sp) * r * (1.0 - r)
        dzi = dig * ig * (1.0 - ig)
        dzrb = dzr.astype(BF16)
        dzib = dzi.astype(BF16)
        du = du + _dot_nt(dzrb, wrg_v) + _dot_nt(dzib, wig_v)
        dwrg_ref[...] += _dot_tn(ub, dzrb)
        dwig_ref[...] += _dot_tn(ub, dzib)
        dbrg_ref[...] += jnp.sum(dzr, axis=0, keepdims=True)
        dbig_ref[...] += jnp.sum(dzi, axis=0, keepdims=True)
        dcb_ref[...] += jnp.sum(du, axis=0, keepdims=True)
        dubuf[0:tl, :] = du
        dxu0 = jnp.zeros((tl, D_LRU), F32)
        for j in range(4):
            dsh = dubuf[pl.ds(3 - j, tl), :]
            dxu0 = dxu0 + cw_ref[j:j + 1, :] * dsh
            dcw_ref[j:j + 1, :] += jnp.sum(xu0 * dsh, axis=0, keepdims=True)
        dubuf[tl:tl + 8, :] = du[0:8, :]
        dxg_ref[:, 0:D_LRU] = dxu0.astype(BF16)

        @pl.when(i == nt - 1)
        def _():
            dlam_ref[...] = dlam_ref[...] * (-jax.nn.sigmoid(-lamv))

    def rev(n):
        return pl.BlockSpec((tl, n), lambda i: (nt - 1 - i, 0))

    prev8 = pl.BlockSpec((8, D_LRU), lambda i: (jnp.maximum((nt - 1 - i) * (tl // 8) - 1, 0), 0))
    vec = _full((1, D_LRU))
    sq = _full((D_LRU, D_LRU))
    return _call(
        body, "b_lru", (nt,),
        [rev(D_LRU), rev(D_LRU), prev8, rev(D_LRU), rev(1024), _full((4, D_LRU)), sq, vec, sq, vec, vec],
        [rev(1024), sq, sq, vec, vec, vec, _full((4, D_LRU)), vec],
        [_sds((s_len, 1024), BF16), _sds((D_LRU, D_LRU), F32), _sds((D_LRU, D_LRU), F32),
         _sds((1, D_LRU), F32), _sds((1, D_LRU), F32), _sds((1, D_LRU), F32),
         _sds((4, D_LRU), F32), _sds((1, D_LRU), F32)],
        [pltpu.VMEM((tl + 8, D_LRU), F32)] * 3 + [pltpu.VMEM((tl, D_LRU), F32)] * 3
        + [pltpu.VMEM((8, D_LRU), F32)],
        (drec, hs, hs, u, xg, conv_w, wrg, brg, wig, big, lam), "arbitrary", comm)


def _b_attn(qkv_pad, att, datt, frow, comm=None):
    s_len = datt.shape[0]
    nb = s_len // QB
    n_pair = ATT_HEADS // 2
    pair_w = 2 * HEAD_DIM

    def body(q_ref, k0, k1, k2, v0, v1, v2, o_ref, do_ref, frow_ref, dq_ref, dkv_ref, dfrow_ref,
             bias_sc, dt_sc, acc_sc):
        t = pl.program_id(0)

        @pl.when(t == 0)
        def _():
            _bias_table(frow_ref, bias_sc)
            dt_sc[...] = jnp.zeros_like(dt_sc)
            acc_sc[...] = jnp.zeros_like(acc_sc)

        @pl.when(t < nb)
        def _():
            var = jnp.minimum(t, N_BIAS - 1)
            even = _even_lanes()
            for hp in range(n_pair):
                cs = slice(hp * pair_w, (hp + 1) * pair_w)
                qt = q_ref[:, cs]
                kts = [k0[:, cs], k1[:, cs], k2[:, cs]]
                vts = [v0[:, cs], v1[:, cs], v2[:, cs]]
                dot = do_ref[:, cs]
                dd = dot * o_ref[:, cs]
                qmt, dost, dsbs, pbs, dqs = [], [], [], [], []
                for e in range(2):
                    keep = even if e == 0 else jnp.logical_not(even)
                    qm = jnp.where(keep, qt, 0)
                    p = _att_probs(qm, kts, bias_sc[var, 2 * hp + e])
                    inv = 1.0 / jnp.sum(p, axis=-1, keepdims=True)
                    dos = jnp.where(keep, dot * inv, 0.0)
                    delta = jnp.sum(jnp.where(keep, dd, 0.0), axis=-1, keepdims=True) * inv
                    dp = jnp.concatenate([_dot_nt(dos.astype(BF16), v) for v in vts], axis=1)
                    ds = p * (dp - delta)
                    dt_sc[2 * hp + e] += ds
                    dsb = ds.astype(BF16)
                    dq = _dot(dsb[:, 0:QB], kts[0])
                    for j in (1, 2):
                        dq = dq + _dot(dsb[:, j * QB:(j + 1) * QB], kts[j])
                    dqs.append(dq)
                    dsbs.append(dsb)
                    pbs.append(p.astype(BF16))
                    qmt.append(qm.astype(F32).T.astype(BF16))
                    dost.append(dos.T.astype(BF16))
                dq_ref[:, cs] = (jnp.where(even, dqs[0], dqs[1]) * ATT_SCALE).astype(BF16)
                for j in range(3):
                    slot = (t + 1 + j) % 3
                    js = slice(j * QB, (j + 1) * QB)
                    acc_sc[slot, hp] += _dot(qmt[0], dsbs[0][:, js]) + _dot(qmt[1], dsbs[1][:, js])
                    acc_sc[slot, n_pair + hp] += _dot(dost[0], pbs[0][:, js]) + _dot(dost[1], pbs[1][:, js])

        done = (t + 1) % 3

        @pl.when(t >= 2)
        def _():
            for i in range(2 * n_pair):
                dkv_ref[:, i * pair_w:(i + 1) * pair_w] = acc_sc[done, i].T.astype(BF16)

        acc_sc[done] = jnp.zeros((2 * n_pair, pair_w, QB), F32)

        @pl.when(t == nb + 1)
        def _():
            row = lax.broadcasted_iota(jnp.int32, (8, ROLL_W), 0)
            pad = jnp.zeros((8, ROLL_W - KB), F32)
            for h in range(ATT_HEADS):
                acc8 = jnp.concatenate([dt_sc[h, 0:8, :], pad], axis=1)
                for a1 in range(1, QB // 8):
                    blk = jnp.concatenate([dt_sc[h, 8 * a1:8 * a1 + 8, :], pad], axis=1)
                    acc8 = acc8 + pltpu.roll(blk, ROLL_W - 8 * a1, 1)
                for k in range(3):
                    acc8 = jnp.where(((row >> k) & 1) == 1, pltpu.roll(acc8, ROLL_W - (1 << k), 1), acc8)
                dfrow_ref[h:h + 1, :] = jnp.sum(acc8, axis=0, keepdims=True)

    clamp = lambda t: jnp.minimum(t, nb - 1)
    qrows = pl.BlockSpec((QB, D_ATT), lambda t: (clamp(t), 0))
    return _call(
        body, "b_attn", (nb + 2,),
        _att_in_specs(clamp) + [qrows, qrows, _full((ATT_HEADS, ROLL_W))],
        [qrows, pl.BlockSpec((QB, 2 * D_ATT), lambda t: (jnp.maximum(t - 2, 0), 0)),
         _full((ATT_HEADS, ROLL_W))],
        [_sds((s_len, D_ATT), BF16), _sds((s_len, 2 * D_ATT), BF16), _sds((ATT_HEADS, ROLL_W), F32)],
        [pltpu.VMEM((N_BIAS, ATT_HEADS, QB, KB), F32), pltpu.VMEM((ATT_HEADS, QB, KB), F32),
         pltpu.VMEM((3, 2 * n_pair, pair_w, QB), F32)],
        (*([qkv_pad] * 7), att, datt, frow), "arbitrary", comm)


def _b_win(dq, dkv, dxg, h, ts):
    s_len = h.shape[0]
    steps = s_len // ts

    def body(dq_ref, dkv_ref, dxg_ref, h_ref, dw_ref, dwb_ref):
        @pl.when(pl.program_id(0) == 0)
        def _():
            dw_ref[...] = jnp.zeros_like(dw_ref)

        dproj = jnp.concatenate([dq_ref[...], dkv_ref[...], dxg_ref[...]], axis=1)
        hv = h_ref[...]
        for s in range(N_SHARD):
            dw_ref[s] += _dot_tn(hv, dproj[:, s * IN_SH:(s + 1) * IN_SH])

        @pl.when(pl.program_id(0) == steps - 1)
        def _():
            dwb_ref[...] = dw_ref[...].astype(BF16)

    wspec = _full((N_SHARD, 1024, IN_SH))
    return pl.pallas_call(
        body, name="b_win", grid=(steps,),
        in_specs=[_rows(ts, 512), _rows(ts, 1024), _rows(ts, 1024), _rows(ts, 1024)],
        out_specs=[wspec, wspec],
        out_shape=[_sds((N_SHARD, 1024, IN_SH), F32), _sds((N_SHARD, 1024, IN_SH), BF16)],
        compiler_params=_cp("arbitrary"))(dq, dkv, dxg, h)


def _b_inproj(dq, dkv, dxg, x, dx1, g_mix, w_in_g, tm, comm=None):
    s_len = x.shape[0]

    def body(dq_ref, dkv_ref, dxg_ref, x_ref, dx1_ref, g_ref, w_ref, gx_ref, dgm_ref):
        @pl.when(pl.program_id(0) == 0)
        def _():
            dgm_ref[...] = jnp.zeros_like(dgm_ref)

        dproj = jnp.concatenate([dq_ref[...], dkv_ref[...], dxg_ref[...]], axis=1)
        dh = jnp.zeros((tm, 1024), F32)
        for s in range(N_SHARD):
            dh = dh + _dot_nt(dproj[:, s * IN_SH:(s + 1) * IN_SH], w_ref[s])
        dx, dgm = _rms_bwd(dh, x_ref[...], g_ref[...])
        gx_ref[...] = dx1_ref[...] + dx
        dgm_ref[...] += dgm

    return _call(
        body, "b_inproj", (s_len // tm,),
        [_rows(tm, 512), _rows(tm, 1024), _rows(tm, 1024), _rows(tm, 1024), _rows(tm, 1024),
         _full((1, 1024)), _full((N_SHARD, 1024, IN_SH))],
        [_rows(tm, 1024), _full((1, 1024))],
        [_sds((s_len, 1024), F32), _sds((1, 1024), F32)],
        [], (dq, dkv, dxg, x, dx1, g_mix, w_in_g), "arbitrary", comm)


def _mm_tn(xa, ya, name, ts):
    s_len, k = xa.shape
    n = ya.shape[1]

    steps = s_len // ts

    def body(x_ref, y_ref, o_ref, ob_ref):
        @pl.when(pl.program_id(0) == 0)
        def _():
            o_ref[...] = jnp.zeros_like(o_ref)
        o_ref[...] += _dot_tn(x_ref[...].astype(BF16), y_ref[...].astype(BF16))

        @pl.when(pl.program_id(0) == steps - 1)
        def _():
            ob_ref[...] = o_ref[...].astype(BF16)

    return pl.pallas_call(
        body, name=name, grid=(steps,), in_specs=[_rows(ts, k), _rows(ts, n)],
        out_specs=[_full((k, n))] * 2, out_shape=[_sds((k, n), F32), _sds((k, n), BF16)],
        compiler_params=_cp("arbitrary"))(xa, ya)


def _mm_tn_ysh(xa, y4, name, ts):
    s_len, k = xa.shape
    n = y4.shape[2]

    steps = s_len // ts

    def body(x_ref, y_ref, o_ref, ob_ref):
        @pl.when(pl.program_id(0) == 0)
        def _():
            o_ref[...] = jnp.zeros_like(o_ref)
        xb = x_ref[...].astype(BF16)
        for s in range(N_SHARD):
            o_ref[s] += _dot_tn(xb, y_ref[s])

        @pl.when(pl.program_id(0) == steps - 1)
        def _():
            ob_ref[...] = o_ref[...].astype(BF16)

    return pl.pallas_call(
        body, name=name, grid=(steps,), in_specs=[_rows(ts, k), _sh_rows(ts, n)],
        out_specs=[_full((N_SHARD, k, n))] * 2,
        out_shape=[_sds((N_SHARD, k, n), F32), _sds((N_SHARD, k, n), BF16)],
        compiler_params=_cp("arbitrary"))(xa, y4)


def _mm_tn_xsh(x4, ya, name, ts):
    s_len, n = ya.shape
    k = x4.shape[2]

    steps = s_len // ts

    def body(x_ref, y_ref, o_ref, ob_ref):
        @pl.when(pl.program_id(0) == 0)
        def _():
            o_ref[...] = jnp.zeros_like(o_ref)
        yb = y_ref[...].astype(BF16)
        for s in range(N_SHARD):
            o_ref[s] += _dot_tn(x_ref[s], yb)

        @pl.when(pl.program_id(0) == steps - 1)
        def _():
            ob_ref[...] = o_ref[...].astype(BF16)

    return pl.pallas_call(
        body, name=name, grid=(steps,), in_specs=[_sh_rows(ts, k), _rows(ts, n)],
        out_specs=[_full((N_SHARD, k, n))] * 2,
        out_shape=[_sds((N_SHARD, k, n), F32), _sds((N_SHARD, k, n), BF16)],
        compiler_params=_cp("arbitrary"))(x4, ya)


def _frow_from_rel_bias(rb):
    hi = jnp.broadcast_to(rb[:, 256:257], (ATT_HEADS, 385))
    mid = rb[:, 1:256][:, ::-1]
    lo = jnp.broadcast_to(rb[:, 0:1], (ATT_HEADS, 128))
    wrap = jnp.broadcast_to(rb[:, 256:257], (ATT_HEADS, ROLL_W - KB))
    return jnp.concatenate([hi, mid, lo, wrap], axis=1)


def _rel_bias_grad_from_dfrow(df):
    g256 = jnp.sum(df[:, 0:385], axis=1, keepdims=True) + jnp.sum(df[:, KB:ROLL_W], axis=1, keepdims=True)
    mid = df[:, 385:640][:, ::-1]
    g0 = jnp.sum(df[:, 640:KB], axis=1, keepdims=True)
    return jnp.concatenate([g0, mid, g256], axis=1)


def _block_diag(w):
    eye = jnp.eye(8, dtype=w.dtype)
    return (w[:, :, None, :] * eye[:, None, :, None]).reshape(D_LRU, D_LRU)


def _block_diag_extract(dense):
    eye = jnp.eye(8, dtype=dense.dtype)
    return jnp.sum(dense.reshape(8, 64, 8, 64) * eye[:, None, :, None], axis=2)


MID = ['w_out', 'wq_c', 'wk_c', 'wv_c', 'wo_c']
AG_IN_INPROJ = ['w_out', 'wq_c', 'wk_c']
AG_IN_ATTN = ['wv_c', 'wo_c', 'w_gate']
AG_IN_LRU = ['w_up']
AG_IN_MID = ['w_down']
RS_IN_MID = ['w_gate', 'w_up']
RS_IN_LRU = ['w_down']
RS_IN_ATTN = MID


def _local_step(x, mem, tgt, p, gw, shards=None, chip=None):
    s_len = x.shape[0]
    tm = min(256, s_len)
    tl = min(512, s_len)
    frow = _frow_from_rel_bias(p['rel_bias'])
    wrg = _block_diag(p['w_rg']).astype(BF16)
    wig = _block_diag(p['w_ig']).astype(BF16)
    gw = dict(gw)

    big, bigb, recv, part, sib = {}, {}, {}, {}, {}

    def ag(names):
        return [] if shards is None else [("ag", [shards[n] for n in names])]

    def rs(names):
        return [] if shards is None else [("rs", [bigb[n] for n in names])]

    def swap(names):
        return [] if shards is None else [("swap", [part[n] for n in names])]

    def reduce_own(names):
        if shards is not None:
            for n in names:
                part[n] = _sum_parts(big[n], recv[n], chip, "sum_" + n)

    h, qkv, xg, *got = _f_inproj(x, p['g_mix'], gw['w_in'], tm, ag(AG_IN_INPROJ))
    gw.update(zip(AG_IN_INPROJ, got))
    qkv_pad = jnp.pad(qkv, ((LEFT_CHUNKS * CHUNK, 0), (0, 0)))
    att, *got = _f_attn(qkv_pad, frow, ag(AG_IN_ATTN))
    gw.update(zip(AG_IN_ATTN, got))
    rec, u, hs, *got = _f_lru(xg, p['conv_w'], p['conv_b'], wrg, p['b_rg'], wig, p['b_ig'], p['lru_L'], tl,
                              ag(AG_IN_LRU))
    gw.update(zip(AG_IN_LRU, got))
    w_out = gw['w_out'].reshape(1024, 1024)
    wq = gw['wq_c'].reshape(1024, 1024)
    wk = gw['wk_c'].reshape(1024, 1024)
    wv = gw['wv_c'].reshape(1024, 1024)
    wo = gw['wo_c'].reshape(1024, 1024)
    mn, kx, vx = _f_mem(mem, p['g_mem'], wk, wv)
    mg, x1, hc, qx, ox, x2, *got = _f_mid(x, att, rec, p['g_out_attn'], p['g_out_lru'], w_out, p['g_cross'],
                                          wq, kx, vx, wo, tm, ag(AG_IN_MID))
    gw.update(zip(AG_IN_MID, got))
    hf, gact, uact, aact, dx3, loss, dg_final = _f_ffn(x2, tgt, p['g_ffn'], p['g_final'],
                                                       gw['w_gate'], gw['w_up'], gw['w_down'], tm)

    ts = min(512, s_len)
    dgact, duact, dx2, dg_ffn = _b_ffn(dx3, x2, gact, uact, p['g_ffn'], gw['w_gate'], gw['w_up'], gw['w_down'], tm)
    big['w_gate'], bigb['w_gate'] = _mm_tn_ysh(hf, dgact, "dw_gate", ts)
    big['w_up'], bigb['w_up'] = _mm_tn_ysh(hf, duact, "dw_up", ts)
    big['w_down'], bigb['w_down'] = _mm_tn_xsh(aact, dx3, "dw_down", ts)

    dqx, dx1, datt, drec, dkx, dvx, dg_cross, dg_oa, dg_ol, *got = _b_mid(
        dx2, qx, x1, att, rec, kx, vx, wo, wq, w_out, p['g_cross'], p['g_out_attn'], p['g_out_lru'], tm,
        rs(RS_IN_MID))
    recv.update(zip(RS_IN_MID, got))
    reduce_own(RS_IN_MID)
    dwk, dwv, dg_mem, dwkb, dwvb = _b_mem(dkx, dvx, mem, mn, p['g_mem'], wk, wv)
    big['wk_c'], bigb['wk_c'] = dwk, dwkb
    big['wv_c'], bigb['wv_c'] = dwv, dwvb
    big['w_out'], bigb['w_out'] = _mm_tn(mg, dx1, "dw_out", ts)
    big['wq_c'], bigb['wq_c'] = _mm_tn(hc, dqx, "dw_q", ts)
    big['wo_c'], bigb['wo_c'] = _mm_tn(ox, dx2, "dw_o", ts)
    for n in MID:
        big[n] = big[n].reshape(N_SHARD, 256, 1024)
        bigb[n] = bigb[n].reshape(N_SHARD, 256, 1024)

    dxg, dwrg, dwig, dbrg, dbig, dlam, dcw, dcb, *got = _b_lru(
        drec, hs, u, xg, p['conv_w'], wrg, p['b_rg'], wig, p['b_ig'], p['lru_L'], tl,
        rs(RS_IN_LRU) + swap(RS_IN_MID))
    recv.update(zip(RS_IN_LRU, got))
    sib.update(zip(RS_IN_MID, got[len(RS_IN_LRU):]))
    reduce_own(RS_IN_LRU)
    dq, dkv, dfrow, *got = _b_attn(qkv_pad, att, datt, frow, rs(RS_IN_ATTN) + swap(RS_IN_LRU))
    recv.update(zip(RS_IN_ATTN, got))
    sib.update(zip(RS_IN_LRU, got[len(RS_IN_ATTN):]))
    reduce_own(RS_IN_ATTN)
    big['w_in'], bigb['w_in'] = _b_win(dq, dkv, dxg, h, ts)
    grad_x, dg_mix, *got = _b_inproj(dq, dkv, dxg, x, dx1, p['g_mix'], gw['w_in'], tm,
                                     rs(['w_in']) + swap(RS_IN_ATTN))
    recv.update(zip(['w_in'], got))
    sib.update(zip(RS_IN_ATTN, got[1:]))
    reduce_own(['w_in'])
    small = {
        'g_mix': dg_mix, 'rel_bias': _rel_bias_grad_from_dfrow(dfrow), 'conv_w': dcw, 'conv_b': dcb,
        'w_rg': _block_diag_extract(dwrg), 'b_rg': dbrg, 'w_ig': _block_diag_extract(dwig), 'b_ig': dbig,
        'lru_L': dlam,
        'g_out_attn': dg_oa, 'g_out_lru': dg_ol, 'g_cross': dg_cross, 'g_mem': dg_mem, 'g_ffn': dg_ffn,
        'g_final': dg_final,
    }
    return jnp.sum(loss[0, 0:1]), grad_x, small, big, part, sib


def _cast_shards(ws):
    def body(*refs):
        n = len(refs) // 2
        for src, dst in zip(refs[:n], refs[n:]):
            dst[...] = src[...].astype(BF16)

    return pl.pallas_call(body, name="cast_shards", out_shape=[_sds(w.shape, BF16) for w in ws],
                          compiler_params=_cp())(*ws)


def _sum_parts(own4, recv3, chip, name):
    _, r, c = own4.shape
    tr = r // 4

    def body(chip_ref, own_ref, rc_ref, o_ref):
        o_ref[...] = ((own_ref[0] + rc_ref[0].astype(F32)) + rc_ref[1].astype(F32)) + rc_ref[2].astype(F32)

    grid_spec = pltpu.PrefetchScalarGridSpec(
        num_scalar_prefetch=1, grid=(4,),
        in_specs=[pl.BlockSpec((1, tr, c), lambda i, ch: (ch[0], i, 0)),
                  pl.BlockSpec((3, tr, c), lambda i, ch: (0, i, 0))],
        out_specs=pl.BlockSpec((tr, c), lambda i, ch: (i, 0)))
    return pl.pallas_call(body, name=name, grid_spec=grid_spec, out_shape=_sds((r, c), F32),
                          compiler_params=_cp("parallel"))(chip, own4, recv3)


def _adamw_math(w, g, m, v):
    m = ADAM_B1 * m + (1.0 - ADAM_B1) * g
    v = ADAM_B2 * v + (1.0 - ADAM_B2) * (g * g)
    m_hat = m / (1.0 - ADAM_B1 ** ADAM_STEP)
    v_hat = v / (1.0 - ADAM_B2 ** ADAM_STEP)
    delta = -ADAM_LR * (m_hat / (jnp.sqrt(v_hat) + ADAM_EPS) + ADAM_WD * w)
    return delta, m, v


def _final_adamw(pa, pb, w, m, v, name):
    r, c = w.shape
    tr = r // 4

    def body(pa_ref, pb_ref, w_ref, m_ref, v_ref, g_ref, d_ref, nm_ref, nv_ref):
        g = pa_ref[...] + pb_ref[...]
        g_ref[...] = g
        d_ref[...], nm_ref[...], nv_ref[...] = _adamw_math(w_ref[...], g, m_ref[...], v_ref[...])

    return pl.pallas_call(
        body, name=name, grid=(4,), in_specs=[_rows(tr, c)] * 5, out_specs=[_rows(tr, c)] * 4,
        out_shape=[_sds((r, c), F32)] * 4, compiler_params=_cp("parallel"))(pa, pb, w, m, v)


def _adamw_small(w, g, m, v):
    def body(w_ref, g_ref, m_ref, v_ref, d_ref, nm_ref, nv_ref):
        d_ref[...], nm_ref[...], nv_ref[...] = _adamw_math(w_ref[...], g_ref[...], m_ref[...], v_ref[...])

    return pl.pallas_call(body, name="adamw_conv_w", out_shape=[_sds(w.shape, F32)] * 3,
                          compiler_params=_cp())(w, g, m, v)


def _ar_small(gp, wp, mp, vp):
    rows = gp.shape[0]

    def body(g_ref, w_ref, m_ref, v_ref, go_ref, d_ref, nm_ref, nv_ref, buf, send_sems, recv_sems):
        x, y, c = _mesh_pos()
        me = 4 * x + 2 * y + c

        def peer(k):
            px = 1 - x if k & 4 else x
            py = 1 - y if k & 2 else y
            pc = 1 - c if k & 1 else c
            return px, py, pc

        def remote(k, slot):
            return pltpu.make_async_remote_copy(
                src_ref=g_ref, dst_ref=buf.at[slot], send_sem=send_sems.at[k - 1], recv_sem=recv_sems.at[k - 1],
                device_id=peer(k), device_id_type=MESH_ID)

        for k in range(1, 8):
            remote(k, me).start()
        buf[me] = g_ref[...]
        for k in range(1, 8):
            px, py, pc = peer(k)
            remote(k, 4 * px + 2 * py + pc).wait_recv()
        for k in range(1, 8):
            remote(k, me).wait_send()
        tot = buf[0]
        for k in range(1, 8):
            tot = tot + buf[k]
        go_ref[...] = tot
        d_ref[...], nm_ref[...], nv_ref[...] = _adamw_math(w_ref[...], tot, m_ref[...], v_ref[...])

    return pl.pallas_call(
        body, name="ar_small", out_shape=[_sds((rows, 128), F32)] * 4,
        scratch_shapes=[pltpu.VMEM((8, rows, 128), F32), pltpu.SemaphoreType.DMA((7,)),
                        pltpu.SemaphoreType.DMA((7,))],
        compiler_params=_cp())(gp, wp, mp, vp)


def _pack(parts):
    flat = jnp.concatenate([parts[n].reshape(-1) for n in SMALL])
    rows = -(-flat.shape[0] // 1024) * 8
    return jnp.pad(flat, (0, rows * 128 - flat.shape[0])).reshape(rows, 128)


def _unpack(pack):
    flat = pack.reshape(-1)
    out, off = {}, 0
    for n in SMALL:
        size = math.prod(SMALL_SHAPES[n])
        out[n] = flat[off:off + size].reshape(SMALL_SHAPES[n])
        off += size
    return out


INPUT_NAMES = (['x', 'mem'] + WEIGHTS + ['loss_target'] + ['m_' + n for n in WEIGHTS] + ['v_' + n for n in WEIGHTS])


def kernel(x, mem, g_mix, w_in, rel_bias, conv_w, conv_b, w_rg, b_rg, w_ig, b_ig, lru_L, g_out_attn, g_out_lru, w_out, g_cross, g_mem, wq_c, wk_c, wv_c, wo_c, g_ffn, w_gate, w_up, w_down, g_final, loss_target, m_g_mix, m_w_in, m_rel_bias, m_conv_w, m_conv_b, m_w_rg, m_b_rg, m_w_ig, m_b_ig, m_lru_L, m_g_out_attn, m_g_out_lru, m_w_out, m_g_cross, m_g_mem, m_wq_c, m_wk_c, m_wv_c, m_wo_c, m_g_ffn, m_w_gate, m_w_up, m_w_down, m_g_final, v_g_mix, v_w_in, v_rel_bias, v_conv_w, v_conv_b, v_w_rg, v_b_rg, v_w_ig, v_b_ig, v_lru_L, v_g_out_attn, v_g_out_lru, v_w_out, v_g_cross, v_g_mem, v_wq_c, v_wk_c, v_wv_c, v_wo_c, v_g_ffn, v_w_gate, v_w_up, v_w_down, v_g_final):
    a = dict(zip(INPUT_NAMES, (x, mem, g_mix, w_in, rel_bias, conv_w, conv_b, w_rg, b_rg, w_ig, b_ig, lru_L, g_out_attn, g_out_lru, w_out, g_cross, g_mem, wq_c, wk_c, wv_c, wo_c, g_ffn, w_gate, w_up, w_down, g_final, loss_target, m_g_mix, m_w_in, m_rel_bias, m_conv_w, m_conv_b, m_w_rg, m_b_rg, m_w_ig, m_b_ig, m_lru_L, m_g_out_attn, m_g_out_lru, m_w_out, m_g_cross, m_g_mem, m_wq_c, m_wk_c, m_wv_c, m_wo_c, m_g_ffn, m_w_gate, m_w_up, m_w_down, m_g_final, v_g_mix, v_w_in, v_rel_bias, v_conv_w, v_conv_b, v_w_rg, v_b_rg, v_w_ig, v_b_ig, v_lru_L, v_g_out_attn, v_g_out_lru, v_w_out, v_g_cross, v_g_mem, v_wq_c, v_wk_c, v_wv_c, v_wo_c, v_g_ffn, v_w_gate, v_w_up, v_w_down, v_g_final)))
    chip = 2 * lax.axis_index("x") + lax.axis_index("y")

    shards = dict(zip(BIG, _cast_shards([a[n][0] for n in BIG])))
    w_in_g, conv_w_g = _comm_only("ag_w_in", [("ag", [shards['w_in']]), ("agf", [a['conv_w'][0]])])
    conv_w_full = conv_w_g.transpose(1, 0, 2).reshape(4, D_LRU)

    p = {n: a[n] for n in SMALL}
    p['rel_bias'] = a['rel_bias'][0]
    p['w_rg'] = a['w_rg'][0]
    p['w_ig'] = a['w_ig'][0]
    p['conv_w'] = conv_w_full
    p['g_final'] = a['g_final'][None, :]
    chip_arr = jnp.reshape(chip, (1,)).astype(jnp.int32)
    loss_part, grad_x, small, _, part, sib = _local_step(
        a['x'][0], a['mem'][0], a['loss_target'][0], p, {'w_in': w_in_g}, shards, chip_arr)
    loss = lax.psum(loss_part, ("x", "y", "c"))

    sib['w_in'], = _comm_only("swap_w_in", [("swap", [part['w_in']])])
    out = {}
    for n in BIG:
        out[n] = _final_adamw(part[n], sib[n], a[n][0], a['m_' + n][0], a['v_' + n][0], "adamw_" + n)

    zeros_cw = jnp.zeros((1, 4, D_LRU), F32)
    def packed(prefix):
        d = {n: a[prefix + n] for n in SMALL}
        d['conv_w'] = zeros_cw
        return _pack(d)
    packs = _ar_small(_pack(small), packed(''), packed('m_'), packed('v_'))
    sg, sd, sm, sv = [_unpack(pk) for pk in packs]
    g_cw = lax.dynamic_slice(sg['conv_w'][0], (0, chip * 128), (4, 128))
    d_cw, m_cw, v_cw = _adamw_small(a['conv_w'][0], g_cw, a['m_conv_w'][0], a['v_conv_w'][0])
    sg['conv_w'], sd['conv_w'], sm['conv_w'], sv['conv_w'] = g_cw[None], d_cw[None], m_cw[None], v_cw[None]

    def leaf(i, n):
        if n in BIG:
            return out[n][i][None]
        return (sg, sd, sm, sv)[i][n]

    return (loss, grad_x[None], *[leaf(i, n) for i in range(4) for n in WEIGHTS])
```

```python
import math

import jax
import jax.numpy as jnp
from jax import lax
from jax.experimental import pallas as pl
from jax.experimental.pallas import tpu as pltpu

F32 = jnp.float32
BF16 = jnp.bfloat16

D_MODEL = 1024
D_ATT = 512
D_LRU = 512
HEAD_DIM = 64
ATT_HEADS = 8
CHUNK = 64
LEFT_CHUNKS = 8
X_HEADS = 4
X_HEAD_DIM = 256
N_SHARD = 4
IN_SH = 640
FF_SH = 704
EPS = 1e-6
LRU_C = 8.0
QB = 256
KB = 768
ROLL_W = 1024
NEG = -1e30
ATT_SCALE = HEAD_DIM ** -0.5
X_SCALE = X_HEAD_DIM ** -0.5

ADAM_LR = 0.001
ADAM_B1 = 0.9
ADAM_B2 = 0.999
ADAM_EPS = 1e-08
ADAM_WD = 0.01
ADAM_STEP = 10

VMEM_LIMIT_V7X = 56 * 1024 * 1024
MESH_ID = pl.DeviceIdType.MESH

WEIGHTS = ['g_mix', 'w_in', 'rel_bias', 'conv_w', 'conv_b', 'w_rg', 'b_rg', 'w_ig', 'b_ig', 'lru_L',
           'g_out_attn', 'g_out_lru', 'w_out', 'g_cross', 'g_mem', 'wq_c', 'wk_c', 'wv_c', 'wo_c',
           'g_ffn', 'w_gate', 'w_up', 'w_down', 'g_final']
BIG = ['w_in', 'w_out', 'wq_c', 'wk_c', 'wv_c', 'wo_c', 'w_gate', 'w_up', 'w_down']
SMALL = [n for n in WEIGHTS if n not in BIG]
SMALL_SHAPES = {
    'g_mix': (1, 1024), 'rel_bias': (1, 8, 257), 'conv_w': (1, 4, 512), 'conv_b': (1, 512),
    'w_rg': (1, 8, 64, 64), 'b_rg': (1, 512), 'w_ig': (1, 8, 64, 64), 'b_ig': (1, 512), 'lru_L': (1, 512),
    'g_out_attn': (1, 512), 'g_out_lru': (1, 512), 'g_cross': (1, 1024), 'g_mem': (1, 1024),
    'g_ffn': (1, 1024), 'g_final': (1024,)}


def _sds(shape, dtype):
    return jax.ShapeDtypeStruct(shape, dtype)


def _cp(*sem):
    return pltpu.CompilerParams(dimension_semantics=sem or None, vmem_limit_bytes=VMEM_LIMIT_V7X)


def _rows(tm, n):
    return pl.BlockSpec((tm, n), lambda i: (i, 0))


def _full(shape):
    nd = len(shape)
    return pl.BlockSpec(shape, lambda i: (0,) * nd)


def _dot(a, b):
    return jnp.dot(a, b, preferred_element_type=F32)


def _dot_nt(a, b):
    return lax.dot_general(a, b, (((1,), (1,)), ((), ())), preferred_element_type=F32)


def _dot_tn(a, b):
    return lax.dot_general(a, b, (((0,), (0,)), ((), ())), preferred_element_type=F32)


def _rinv(x):
    return lax.rsqrt(jnp.mean(x * x, axis=-1, keepdims=True) + EPS)


def _rms_bwd(dy, x, g):
    r = _rinv(x)
    yh = x * r
    dyh = dy * g
    dx = r * (dyh - yh * jnp.mean(dyh * yh, axis=-1, keepdims=True))
    return dx, jnp.sum(dy * yh, axis=0, keepdims=True)


def _gelu(x):
    c = math.sqrt(2.0 / math.pi)
    t = jnp.tanh(c * (x + 0.044715 * x * x * x))
    return 0.5 * x * (1.0 + t)


def _gelu_and_grad(x):
    c = math.sqrt(2.0 / math.pi)
    t = jnp.tanh(c * (x + 0.044715 * x * x * x))
    g = 0.5 * x * (1.0 + t)
    dg = 0.5 * (1.0 + t) + 0.5 * x * (1.0 - t * t) * c * (1.0 + 3.0 * 0.044715 * x * x)
    return g, dg


def _neg_expm1(z):
    series = -z * (1 + z / 2 * (1 + z / 3 * (1 + z / 4 * (1 + z / 5 * (1 + z / 6 * (1 + z / 7))))))
    return jnp.where(z > -0.25, series, 1.0 - jnp.exp(z))


def _lru_gates(u, wrg, brg, wig, big, lam):
    ub = u.astype(BF16)
    r = jax.nn.sigmoid(_dot(ub, wrg) + brg)
    ig = jax.nn.sigmoid(_dot(ub, wig) + big)
    sp = jnp.maximum(-lam, 0.0) + jnp.log1p(jnp.exp(-jnp.abs(lam)))
    la = -LRU_C * r * sp
    a = jnp.exp(la)
    mult = jnp.sqrt(jnp.maximum(_neg_expm1(2.0 * la), 0.0))
    return ub, r, ig, sp, a, mult


def _scan8(a8, b8, hprev):
    row = lax.broadcasted_iota(jnp.int32, a8.shape, 0)
    aa, bb = a8, b8
    for d in (1, 2, 4):
        a_s = pltpu.roll(aa, d, 0)
        b_s = pltpu.roll(bb, d, 0)
        m = row >= d
        bb = jnp.where(m, aa * b_s + bb, bb)
        aa = jnp.where(m, aa * a_s, aa)
    return aa * hprev + bb


def _mesh_pos():
    return lax.axis_index("x"), lax.axis_index("y"), lax.axis_index("c")


def _other_chips(x, y):
    return [(1 - x, y), (x, 1 - y), (1 - x, 1 - y)]


def _no_forward():
    pass


def _ag_full_copies(ins, outs, sems):
    send_sems, recv_sems, loc_sems = sems
    n = len(ins)
    x, y, c = _mesh_pos()
    mine = 2 * x + y
    chips = _other_chips(x, y)

    def remote(k, j, slot):
        px, py = chips[j]
        return pltpu.make_async_remote_copy(
            src_ref=ins[k], dst_ref=outs[k].at[slot], send_sem=send_sems.at[k, j], recv_sem=recv_sems.at[k, j],
            device_id=(px, py, c), device_id_type=MESH_ID)

    def local(k):
        return pltpu.make_async_copy(ins[k], outs[k].at[mine], loc_sems.at[k])

    def start():
        for k in range(n):
            local(k).start()
            for j in range(3):
                remote(k, j, mine).start()

    def wait():
        for k in range(n):
            for j, (px, py) in enumerate(chips):
                remote(k, j, 2 * px + py).wait_recv()
        for k in range(n):
            for j in range(3):
                remote(k, j, mine).wait_send()
            local(k).wait()

    return start, _no_forward, wait


def _ag_copies(ins, outs, sems):
    send_sems, recv_sems, fsend_sems, frecv_sems, loc_sems = sems
    n = len(ins)
    x, y, c = _mesh_pos()
    mine = 2 * x + y
    chips = _other_chips(x, y)

    def half(ref, hc):
        r = ref.shape[0] // 2
        return ref.at[pl.ds(pl.multiple_of(hc * r, 16), r)]

    def ici(k, j, slot):
        px, py = chips[j]
        return pltpu.make_async_remote_copy(
            src_ref=half(ins[k], c), dst_ref=half(outs[k].at[slot], c),
            send_sem=send_sems.at[k, j], recv_sem=recv_sems.at[k, j],
            device_id=(px, py, c), device_id_type=MESH_ID)

    def d2d(k, j, hc):
        px, py = chips[j]
        part = half(outs[k].at[2 * px + py], hc)
        return pltpu.make_async_remote_copy(
            src_ref=part, dst_ref=part, send_sem=fsend_sems.at[k, j], recv_sem=frecv_sems.at[k, j],
            device_id=(x, y, 1 - c), device_id_type=MESH_ID)

    def local(k):
        return pltpu.make_async_copy(ins[k], outs[k].at[mine], loc_sems.at[k])

    def start():
        for k in range(n):
            local(k).start()
            for j in range(3):
                ici(k, j, mine).start()

    def forward():
        for k in range(n):
            for j, (px, py) in enumerate(chips):
                ici(k, j, 2 * px + py).wait_recv()
                d2d(k, j, c).start()

    def wait():
        for k in range(n):
            for j in range(3):
                d2d(k, j, 1 - c).wait_recv()
        for k in range(n):
            for j in range(3):
                d2d(k, j, c).wait_send()
                ici(k, j, mine).wait_send()
            local(k).wait()

    return start, forward, wait


def _rs_copies(ins, outs, sems):
    send_sems, recv_sems = sems
    n = len(ins)
    x, y, c = _mesh_pos()
    chips = _other_chips(x, y)

    def remote(k, j):
        px, py = chips[j]
        return pltpu.make_async_remote_copy(
            src_ref=ins[k].at[2 * px + py], dst_ref=outs[k].at[j],
            send_sem=send_sems.at[k, j], recv_sem=recv_sems.at[k, j],
            device_id=(px, py, c), device_id_type=MESH_ID)

    def start():
        for k in range(n):
            for j in range(3):
                remote(k, j).start()

    def wait():
        for k in range(n):
            for j in range(3):
                remote(k, j).wait_recv()
        for k in range(n):
            for j in range(3):
                remote(k, j).wait_send()

    return start, _no_forward, wait


def _swap_copies(ins, outs, sems):
    send_sems, recv_sems = sems
    x, y, c = _mesh_pos()
    copies = [pltpu.make_async_remote_copy(
        src_ref=ins[k], dst_ref=outs[k], send_sem=send_sems.at[k], recv_sem=recv_sems.at[k],
        device_id=(x, y, 1 - c), device_id_type=MESH_ID) for k in range(len(ins))]

    def start():
        for cp in copies:
            cp.start()

    def wait():
        for cp in copies:
            cp.wait()

    return start, _no_forward, wait


def _comm_plan(groups):
    plan, arrs, shapes, sems = [], [], [], []
    for kind, group in groups:
        k = len(group)
        arrs += group
        per_peer = pltpu.SemaphoreType.DMA((k, 3))
        if kind == "ag":
            shapes += [_sds((N_SHARD,) + w.shape, w.dtype) for w in group]
            gsems = [per_peer] * 4 + [pltpu.SemaphoreType.DMA((k,))]
            maker = _ag_copies
        elif kind == "agf":
            shapes += [_sds((N_SHARD,) + w.shape, w.dtype) for w in group]
            gsems = [per_peer] * 2 + [pltpu.SemaphoreType.DMA((k,))]
            maker = _ag_full_copies
        elif kind == "rs":
            shapes += [_sds((3,) + g.shape[1:], g.dtype) for g in group]
            gsems = [pltpu.SemaphoreType.DMA((k, 3)), pltpu.SemaphoreType.DMA((k, 3))]
            maker = _rs_copies
        else:
            shapes += [_sds(g.shape, g.dtype) for g in group]
            gsems = [pltpu.SemaphoreType.DMA((k,)), pltpu.SemaphoreType.DMA((k,))]
            maker = _swap_copies
        plan.append((maker, k, len(gsems)))
        sems += gsems
    return plan, arrs, shapes, sems


def _comm_fns(plan, cins, couts, sems):
    fns, a, s = [], 0, 0
    for maker, k, ns in plan:
        fns.append(maker(cins[a:a + k], couts[a:a + k], sems[s:s + ns]))
        a += k
        s += ns

    def start():
        for st, _, _ in fns:
            st()

    def forward():
        for _, fw, _ in fns:
            fw()

    def wait():
        for _, _, wt in fns:
            wt()

    return start, forward, wait


def _call(body, name, grid, in_specs, out_specs, out_shape, scratch, args, sem, comm=None):
    if not comm:
        return pl.pallas_call(body, name=name, grid=grid, in_specs=in_specs, out_specs=out_specs,
                              out_shape=out_shape, scratch_shapes=scratch, compiler_params=_cp(sem))(*args)
    plan, c_arrs, c_shapes, c_sems = _comm_plan(comm)
    k = len(c_arrs)
    n_in, n_out, n_scr = len(in_specs), len(out_specs), len(scratch)
    last = grid[0] - 1
    fwd_step = max(1, (2 * last) // 3)

    def wrapped(*refs):
        ins, cins = refs[:n_in], refs[n_in:n_in + k]
        o0 = n_in + k
        outs, couts = refs[o0:o0 + n_out], refs[o0 + n_out:o0 + n_out + k]
        s0 = o0 + n_out + k
        start, forward, wait = _comm_fns(plan, cins, couts, refs[s0 + n_scr:])
        pl.when(pl.program_id(0) == 0)(start)
        pl.when(pl.program_id(0) == fwd_step)(forward)
        body(*ins, *outs, *refs[s0:s0 + n_scr])
        pl.when(pl.program_id(0) == last)(wait)

    return pl.pallas_call(
        wrapped, name=name, grid=grid, in_specs=list(in_specs) + [_any()] * k,
        out_specs=list(out_specs) + [_any()] * k, out_shape=list(out_shape) + c_shapes,
        scratch_shapes=list(scratch) + c_sems, compiler_params=_cp(sem))(*args, *c_arrs)


def _comm_only(name, comm):
    plan, c_arrs, c_shapes, c_sems = _comm_plan(comm)
    k = len(c_arrs)

    def body(*refs):
        start, forward, wait = _comm_fns(plan, refs[:k], refs[k:2 * k], refs[2 * k:])
        start()
        forward()
        wait()

    return pl.pallas_call(body, name=name, in_specs=[_any()] * k, out_specs=[_any()] * k, out_shape=c_shapes,
                          scratch_shapes=c_sems, compiler_params=_cp())(*c_arrs)


def _any():
    return pl.BlockSpec(memory_space=pl.ANY)


def _rscan8(c8, d8, lnext):
    row = lax.broadcasted_iota(jnp.int32, c8.shape, 0)
    cc, dd = c8, d8
    for d in (1, 2, 4):
        c_s = pltpu.roll(cc, 8 - d, 0)
        d_s = pltpu.roll(dd, 8 - d, 0)
        m = row < 8 - d
        dd = jnp.where(m, cc * d_s + dd, dd)
        cc = jnp.where(m, cc * c_s, cc)
    return cc * lnext + dd


def _f_inproj(x, g_mix, w_in_g, tm, comm=None):
    s_len = x.shape[0]
    pad_rows = LEFT_CHUNKS * CHUNK
    npad = pad_rows // tm

    def body(x_ref, g_ref, w_ref, h_ref, qkv_ref, xg_ref):
        i = pl.program_id(0)

        @pl.when(i < npad)
        def _():
            qkv_ref[...] = jnp.zeros_like(qkv_ref)

        @pl.when(i >= npad)
        def _():
            xv = x_ref[...]
            h = (xv * _rinv(xv) * g_ref[...]).astype(BF16)
            h_ref[...] = h
            p0 = _dot(h, w_ref[0])
            qkv_ref[:, 0:D_ATT] = (p0[:, 0:D_ATT] * ATT_SCALE).astype(BF16)
            qkv_ref[:, D_ATT:640] = p0[:, D_ATT:640].astype(BF16)
            qkv_ref[:, 640:1280] = _dot(h, w_ref[1]).astype(BF16)
            p2 = _dot(h, w_ref[2])
            qkv_ref[:, 1280:1536] = p2[:, 0:256].astype(BF16)
            xg_ref[:, 0:384] = p2[:, 256:640]
            xg_ref[:, 384:1024] = _dot(h, w_ref[3])

    def tok(n):
        return pl.BlockSpec((tm, n), lambda i: (jnp.maximum(i - npad, 0), 0))

    return _call(
        body, "f_inproj", (s_len // tm + npad,),
        [tok(1024), _full((1, 1024)), _full((N_SHARD, 1024, IN_SH))],
        [tok(1024), _rows(tm, 1536), tok(1024)],
        [_sds((s_len, 1024), BF16), _sds((s_len + pad_rows, 1536), BF16), _sds((s_len, 1024), F32)],
        [], (x, g_mix, w_in_g), "arbitrary", comm)


N_BIAS = 3


def _bias_table(frow_ref, bias_sc):
    qa = lax.broadcasted_iota(jnp.int32, (QB, KB), 0) // CHUNK
    kcol = lax.broadcasted_iota(jnp.int32, (QB, KB), 1)
    kb = kcol // CHUNK
    band = jnp.where((kb >= qa) & (kb - qa <= LEFT_CHUNKS), 0.0, NEG).astype(F32)
    for h in range(ATT_HEADS):
        row = jnp.broadcast_to(frow_ref[h:h + 1, :], (QB, ROLL_W))
        toep = pltpu.roll(row, 0, 1, stride=1, stride_axis=0)
        gen = toep[:, 0:KB] + band
        bias_sc[N_BIAS - 1, h] = gen
        for v in range(N_BIAS - 1):
            pad_keys = LEFT_CHUNKS * CHUNK - v * QB
            bias_sc[v, h] = gen + jnp.where(kcol < pad_keys, NEG, 0.0).astype(F32)


def _even_lanes():
    return lax.broadcasted_iota(jnp.int32, (1, 2 * HEAD_DIM), 1) < HEAD_DIM


def _att_probs(qm, kts, bias):
    s = jnp.concatenate([_dot_nt(qm, k) for k in kts], axis=1) + bias
    return jnp.exp(s - jnp.max(s, axis=-1, keepdims=True))


def _att_in_specs(clamp):
    def spec(j, col):
        return pl.BlockSpec((QB, D_ATT), lambda i: (clamp(i) + j, col))
    return [spec(2, 0), spec(0, 1), spec(1, 1), spec(2, 1), spec(0, 2), spec(1, 2), spec(2, 2)]


def _f_attn(qkv_pad, frow, comm=None):
    s_len = qkv_pad.shape[0] - LEFT_CHUNKS * CHUNK
    nb = s_len // QB

    def body(q_ref, k0, k1, k2, v0, v1, v2, frow_ref, o_ref, bias_sc):
        i = pl.program_id(0)

        @pl.when(i == 0)
        def _():
            _bias_table(frow_ref, bias_sc)

        var = jnp.minimum(i, N_BIAS - 1)
        even = _even_lanes()
        for hp in range(ATT_HEADS // 2):
            cs = slice(hp * 2 * HEAD_DIM, (hp + 1) * 2 * HEAD_DIM)
            qt = q_ref[:, cs]
            kts = [k0[:, cs], k1[:, cs], k2[:, cs]]
            vts = [v0[:, cs], v1[:, cs], v2[:, cs]]
            res = []
            for e in range(2):
                keep = even if e == 0 else jnp.logical_not(even)
                pb = _att_probs(jnp.where(keep, qt, 0), kts, bias_sc[var, 2 * hp + e]).astype(BF16)
                r = _dot(pb[:, 0:QB], jnp.where(keep, vts[0], 1))
                for j in (1, 2):
                    r = r + _dot(pb[:, j * QB:(j + 1) * QB], jnp.where(keep, vts[j], 1))
                res.append(r / pltpu.roll(r, HEAD_DIM, 1))
            o_ref[:, cs] = jnp.where(even, res[0], res[1])

    return _call(
        body, "f_attn", (nb,),
        _att_in_specs(lambda i: i) + [_full((ATT_HEADS, ROLL_W))],
        [_rows(QB, D_ATT)], [_sds((s_len, D_ATT), F32)],
        [pltpu.VMEM((N_BIAS, ATT_HEADS, QB, KB), F32)], (*([qkv_pad] * 7), frow), "arbitrary", comm)


def _f_lru(xg, conv_w, conv_b, wrg, brg, wig, big, lam, tl, comm=None):
    s_len = xg.shape[0]

    def body(xg_ref, cw_ref, cb_ref, wrg_ref, brg_ref, wig_ref, big_ref, l_ref,
             rec_ref, u_ref, hs_ref, xbuf, a_sc, b_sc, hcar):
        i = pl.program_id(0)

        @pl.when(i == 0)
        def _():
            xbuf[0:8, :] = jnp.zeros((8, D_LRU), F32)
            hcar[...] = jnp.zeros((8, D_LRU), F32)

        xu0 = xg_ref[:, 0:D_LRU]
        xbuf[8:8 + tl, :] = xu0
        u = cb_ref[...] + cw_ref[0:1, :] * xbuf[pl.ds(5, tl), :]
        for j in range(1, 4):
            u = u + cw_ref[j:j + 1, :] * xbuf[pl.ds(5 + j, tl), :]
        xbuf[0:8, :] = xu0[tl - 8:tl, :]
        u_ref[...] = u
        _, _, ig, _, a, mult = _lru_gates(u, wrg_ref[...], brg_ref[...], wig_ref[...], big_ref[...], l_ref[...])
        a_sc[...] = a
        b_sc[...] = mult * (ig * u)

        def grp(g, hprev):
            off = pl.multiple_of(g * 8, 8)
            h8 = _scan8(a_sc[pl.ds(off, 8), :], b_sc[pl.ds(off, 8), :], hprev)
            hs_ref[pl.ds(off, 8), :] = h8
            return h8[7:8, :]

        hcar[0:1, :] = lax.fori_loop(0, tl // 8, grp, hcar[0:1, :])
        rec_ref[...] = hs_ref[...] * _gelu(xg_ref[:, D_LRU:2 * D_LRU])

    vec = _full((1, D_LRU))
    return _call(
        body, "f_lru", (s_len // tl,),
        [_rows(tl, 1024), _full((4, D_LRU)), vec, _full((D_LRU, D_LRU)), vec, _full((D_LRU, D_LRU)), vec, vec],
        [_rows(tl, D_LRU)] * 3, [_sds((s_len, D_LRU), F32)] * 3,
        [pltpu.VMEM((tl + 8, D_LRU), F32), pltpu.VMEM((tl, D_LRU), F32),
         pltpu.VMEM((tl, D_LRU), F32), pltpu.VMEM((8, D_LRU), F32)],
        (xg, conv_w, conv_b, wrg, brg, wig, big, lam), "arbitrary", comm)


def _f_mem(mem, g_mem, wk, wv):
    def body(mem_ref, g_ref, wk_ref, wv_ref, mn_ref, kx_ref, vx_ref):
        mv = mem_ref[...]
        mn = (mv * _rinv(mv) * g_ref[...]).astype(BF16)
        mn_ref[...] = mn
        kx_ref[...] = _dot(mn, wk_ref[...]).astype(BF16)
        vx_ref[...] = _dot(mn, wv_ref[...]).astype(BF16)

    m = mem.shape[0]
    return pl.pallas_call(
        body, name="f_mem", out_shape=[_sds((m, 1024), BF16)] * 3,
        compiler_params=_cp())(mem, g_mem, wk, wv)


def _xattn_probs(q, k):
    s = _dot_nt(q, k) * X_SCALE
    m = jnp.max(s, axis=-1, keepdims=True)
    p = jnp.exp(s - m)
    return p, jnp.sum(p, axis=-1, keepdims=True)


def _f_mid(x, att, rec, g_oa, g_ol, w_out, g_cross, wq, kx, vx, wo, tm, comm=None):
    s_len = x.shape[0]
    m_len = kx.shape[0]

    def body(x_ref, att_ref, rec_ref, goa_ref, gol_ref, wout_ref, gc_ref, wq_ref, kx_ref, vx_ref, wo_ref,
             mg_ref, x1_ref, hc_ref, qx_ref, ox_ref, x2_ref):
        av = att_ref[...]
        rv = rec_ref[...]
        mg_ref[:, 0:D_ATT] = (av * _rinv(av) * goa_ref[...]).astype(BF16)
        mg_ref[:, D_ATT:1024] = (rv * _rinv(rv) * gol_ref[...]).astype(BF16)
        x1 = x_ref[...] + _dot(mg_ref[...], wout_ref[...])
        x1_ref[...] = x1
        hc = (x1 * _rinv(x1) * gc_ref[...]).astype(BF16)
        hc_ref[...] = hc
        qx_ref[...] = _dot(hc, wq_ref[...]).astype(BF16)
        for h in range(X_HEADS):
            sl = slice(h * X_HEAD_DIM, (h + 1) * X_HEAD_DIM)
            p, l = _xattn_probs(qx_ref[:, sl], kx_ref[:, sl])
            ox_ref[:, sl] = (_dot(p.astype(BF16), vx_ref[:, sl]) / l).astype(BF16)
        x2_ref[...] = x1 + _dot(ox_ref[...], wo_ref[...])

    sq = _full((1024, 1024))
    return _call(
        body, "f_mid", (s_len // tm,),
        [_rows(tm, 1024), _rows(tm, 512), _rows(tm, 512), _full((1, 512)), _full((1, 512)), sq,
         _full((1, 1024)), sq, _full((m_len, 1024)), _full((m_len, 1024)), sq],
        [_rows(tm, 1024)] * 6,
        [_sds((s_len, 1024), BF16), _sds((s_len, 1024), F32), _sds((s_len, 1024), BF16),
         _sds((s_len, 1024), BF16), _sds((s_len, 1024), BF16), _sds((s_len, 1024), F32)],
        [], (x, att, rec, g_oa, g_ol, w_out, g_cross, wq, kx, vx, wo), "arbitrary", comm)


def _load_weights_once(pairs):
    @pl.when(pl.program_id(0) == 0)
    def _():
        for hbm, vmem in pairs:
            pltpu.sync_copy(hbm, vmem)


def _sh_rows(tm, n):
    return pl.BlockSpec((N_SHARD, tm, n), lambda i: (0, i, 0))


def _f_ffn(x2, tgt, g_ffn, g_final, wg, wu, wd, tm):
    s_len = x2.shape[0]

    def body(x2_ref, t_ref, gf_ref, gfin_ref, wg_hbm, wu_hbm, wd_hbm,
             hf_ref, g_ref, u_ref, a_ref, dx3_ref, loss_ref, dgfin_ref, wg_ref, wu_ref, wd_ref):
        _load_weights_once([(wg_hbm, wg_ref), (wu_hbm, wu_ref), (wd_hbm, wd_ref)])

        @pl.when(pl.program_id(0) == 0)
        def _():
            loss_ref[...] = jnp.zeros_like(loss_ref)
            dgfin_ref[...] = jnp.zeros_like(dgfin_ref)

        x2v = x2_ref[...]
        hf = (x2v * _rinv(x2v) * gf_ref[...]).astype(BF16)
        hf_ref[...] = hf
        x3 = x2v
        for s in range(N_SHARD):
            gv = _dot(hf, wg_ref[s])
            uv = _dot(hf, wu_ref[s])
            av = (gv * jax.nn.sigmoid(gv) * uv).astype(BF16)
            g_ref[s] = gv.astype(BF16)
            u_ref[s] = uv.astype(BF16)
            a_ref[s] = av
            x3 = x3 + _dot(av, wd_ref[s])
        r3 = _rinv(x3)
        yh = x3 * r3
        gfin = gfin_ref[...]
        err = yh * gfin - t_ref[...]
        loss_ref[...] += jnp.full((1, 128), 0.5 / D_MODEL, F32) * jnp.sum(err * err)
        dy = err * (1.0 / D_MODEL)
        dgfin_ref[...] += jnp.sum(dy * yh, axis=0, keepdims=True)
        dyh = dy * gfin
        dx3_ref[...] = r3 * (dyh - yh * jnp.mean(dyh * yh, axis=-1, keepdims=True))

    vec = _full((1, 1024))
    return pl.pallas_call(
        body, name="f_ffn", grid=(s_len // tm,),
        in_specs=[_rows(tm, 1024), _rows(tm, 1024), vec, vec, _any(), _any(), _any()],
        out_specs=[_rows(tm, 1024), _sh_rows(tm, FF_SH), _sh_rows(tm, FF_SH), _sh_rows(tm, FF_SH),
                   _rows(tm, 1024), _full((1, 128)), vec],
        out_shape=[_sds((s_len, 1024), BF16)] + [_sds((N_SHARD, s_len, FF_SH), BF16)] * 3
                  + [_sds((s_len, 1024), F32), _sds((1, 128), F32), _sds((1, 1024), F32)],
        scratch_shapes=[pltpu.VMEM((N_SHARD, 1024, FF_SH), BF16), pltpu.VMEM((N_SHARD, 1024, FF_SH), BF16),
                        pltpu.VMEM((N_SHARD, FF_SH, 1024), BF16)],
        compiler_params=_cp("arbitrary"))(x2, tgt, g_ffn, g_final, wg, wu, wd)


def _b_ffn(dx3, x2, gact, uact, g_ffn, wg, wu, wd, tm):
    s_len = x2.shape[0]

    def body(dx3_ref, x2_ref, g_ref, u_ref, gf_ref, wg_hbm, wu_hbm, wd_hbm,
             dg_ref, du_ref, dx2_ref, dgf_ref, wg_ref, wu_ref, wd_ref):
        _load_weights_once([(wg_hbm, wg_ref), (wu_hbm, wu_ref), (wd_hbm, wd_ref)])

        @pl.when(pl.program_id(0) == 0)
        def _():
            dgf_ref[...] = jnp.zeros_like(dgf_ref)

        dx3v = dx3_ref[...]
        dx3b = dx3v.astype(BF16)
        dhf = jnp.zeros(dx3v.shape, F32)
        for s in range(N_SHARD):
            da = _dot_nt(dx3b, wd_ref[s])
            gv = g_ref[s].astype(F32)
            uv = u_ref[s].astype(F32)
            sg = jax.nn.sigmoid(gv)
            dub = (da * gv * sg).astype(BF16)
            dgb = (da * uv * (sg * (1.0 + gv * (1.0 - sg)))).astype(BF16)
            du_ref[s] = dub
            dg_ref[s] = dgb
            dhf = dhf + _dot_nt(dgb, wg_ref[s]) + _dot_nt(dub, wu_ref[s])
        dx, dgf = _rms_bwd(dhf, x2_ref[...], gf_ref[...])
        dx2_ref[...] = dx3v + dx
        dgf_ref[...] += dgf

    vec = _full((1, 1024))
    return pl.pallas_call(
        body, name="b_ffn", grid=(s_len // tm,),
        in_specs=[_rows(tm, 1024), _rows(tm, 1024), _sh_rows(tm, FF_SH), _sh_rows(tm, FF_SH), vec,
                  _any(), _any(), _any()],
        out_specs=[_sh_rows(tm, FF_SH), _sh_rows(tm, FF_SH), _rows(tm, 1024), vec],
        out_shape=[_sds((N_SHARD, s_len, FF_SH), BF16)] * 2 + [_sds((s_len, 1024), F32), _sds((1, 1024), F32)],
        scratch_shapes=[pltpu.VMEM((N_SHARD, 1024, FF_SH), BF16), pltpu.VMEM((N_SHARD, 1024, FF_SH), BF16),
                        pltpu.VMEM((N_SHARD, FF_SH, 1024), BF16)],
        compiler_params=_cp("arbitrary"))(dx3, x2, gact, uact, g_ffn, wg, wu, wd)


def _b_mid(dx2, qx, x1, att, rec, kx, vx, wo, wq, w_out, g_cross, g_oa, g_ol, tm, comm=None):
    s_len = x1.shape[0]
    m_len = kx.shape[0]

    def body(dx2_ref, qx_ref, x1_ref, att_ref, rec_ref, kx_ref, vx_ref, wo_ref, wq_ref, wout_ref,
             gc_ref, goa_ref, gol_ref,
             dqx_ref, dx1_ref, datt_ref, drec_ref, dkx_ref, dvx_ref, dgc_ref, dgoa_ref, dgol_ref):
        @pl.when(pl.program_id(0) == 0)
        def _():
            for r in (dkx_ref, dvx_ref, dgc_ref, dgoa_ref, dgol_ref):
                r[...] = jnp.zeros_like(r)

        dx2v = dx2_ref[...]
        dox = _dot_nt(dx2v.astype(BF16), wo_ref[...])
        for h in range(X_HEADS):
            sl = slice(h * X_HEAD_DIM, (h + 1) * X_HEAD_DIM)
            q = qx_ref[:, sl]
            p, l = _xattn_probs(q, kx_ref[:, sl])
            pn = p / l
            dob = dox[:, sl].astype(BF16)
            dp = _dot_nt(dob, vx_ref[:, sl])
            dvx_ref[:, sl] += _dot_tn(pn.astype(BF16), dob)
            ds = pn * (dp - jnp.sum(dp * pn, axis=-1, keepdims=True))
            dsb = (ds * X_SCALE).astype(BF16)
            dqx_ref[:, sl] = _dot(dsb, kx_ref[:, sl]).astype(BF16)
            dkx_ref[:, sl] += _dot_tn(dsb, q)
        dhc = _dot_nt(dqx_ref[...], wq_ref[...])
        dx, dgc = _rms_bwd(dhc, x1_ref[...], gc_ref[...])
        dx1 = dx2v + dx
        dx1_ref[...] = dx1
        dgc_ref[...] += dgc
        dmg = _dot_nt(dx1.astype(BF16), wout_ref[...])
        da, dgoa = _rms_bwd(dmg[:, 0:D_ATT], att_ref[...], goa_ref[...])
        datt_ref[...] = da
        dgoa_ref[...] += dgoa
        dr, dgol = _rms_bwd(dmg[:, D_ATT:1024], rec_ref[...], gol_ref[...])
        drec_ref[...] = dr
        dgol_ref[...] += dgol

    sq = _full((1024, 1024))
    mk = _full((m_len, 1024))
    return _call(
        body, "b_mid", (s_len // tm,),
        [_rows(tm, 1024), _rows(tm, 1024), _rows(tm, 1024), _rows(tm, 512), _rows(tm, 512), mk, mk,
         sq, sq, sq, _full((1, 1024)), _full((1, 512)), _full((1, 512))],
        [_rows(tm, 1024), _rows(tm, 1024), _rows(tm, 512), _rows(tm, 512), mk, mk,
         _full((1, 1024)), _full((1, 512)), _full((1, 512))],
        [_sds((s_len, 1024), BF16), _sds((s_len, 1024), F32), _sds((s_len, 512), F32),
         _sds((s_len, 512), F32), _sds((m_len, 1024), F32), _sds((m_len, 1024), F32),
         _sds((1, 1024), F32), _sds((1, 512), F32), _sds((1, 512), F32)],
        [], (dx2, qx, x1, att, rec, kx, vx, wo, wq, w_out, g_cross, g_oa, g_ol), "arbitrary", comm)


def _b_mem(dkx, dvx, mem, mn, g_mem, wk, wv):
    def body(dkx_ref, dvx_ref, mem_ref, mn_ref, g_ref, wk_ref, wv_ref, dwk_ref, dwv_ref, dgm_ref,
             dwkb_ref, dwvb_ref):
        dkb = dkx_ref[...].astype(BF16)
        dvb = dvx_ref[...].astype(BF16)
        dwk = _dot_tn(mn_ref[...], dkb)
        dwv = _dot_tn(mn_ref[...], dvb)
        dwk_ref[...] = dwk
        dwv_ref[...] = dwv
        dwkb_ref[...] = dwk.astype(BF16)
        dwvb_ref[...] = dwv.astype(BF16)
        dmn = _dot_nt(dkb, wk_ref[...]) + _dot_nt(dvb, wv_ref[...])
        mv = mem_ref[...]
        dgm_ref[...] = jnp.sum(dmn * (mv * _rinv(mv)), axis=0, keepdims=True)

    return pl.pallas_call(
        body, name="b_mem",
        out_shape=[_sds((1024, 1024), F32), _sds((1024, 1024), F32), _sds((1, 1024), F32),
                   _sds((1024, 1024), BF16), _sds((1024, 1024), BF16)],
        compiler_params=_cp())(dkx, dvx, mem, mn, g_mem, wk, wv)


def _b_lru(drec, hs, u, xg, conv_w, wrg, brg, wig, big, lam, tl, comm=None):
    s_len = xg.shape[0]
    nt = s_len // tl

    def body(drec_ref, hs_ref, hsp_ref, u_ref, xg_ref, cw_ref, wrg_ref, brg_ref, wig_ref, big_ref, l_ref,
             dxg_ref, dwrg_ref, dwig_ref, dbrg_ref, dbig_ref, dlam_ref, dcw_ref, dcb_ref,
             hbuf, abuf, dubuf, c_sc, d_sc, lam_sc, lcar):
        i = pl.program_id(0)
        tt = nt - 1 - i

        @pl.when(i == 0)
        def _():
            for r in (dwrg_ref, dwig_ref, dbrg_ref, dbig_ref, dlam_ref, dcw_ref, dcb_ref):
                r[...] = jnp.zeros_like(r)
            abuf[tl:tl + 8, :] = jnp.zeros((8, D_LRU), F32)
            dubuf[tl:tl + 8, :] = jnp.zeros((8, D_LRU), F32)
            lcar[...] = jnp.zeros((8, D_LRU), F32)

        xu0 = xg_ref[:, 0:D_LRU]
        hsv = hs_ref[...]
        uv = u_ref[...]
        hbuf[8:8 + tl, :] = hsv
        hbuf[0:8, :] = jnp.where(tt > 0, hsp_ref[...], 0.0)
        hshift = hbuf[pl.ds(7, tl), :]
        wrg_v = wrg_ref[...]
        wig_v = wig_ref[...]
        lamv = l_ref[...]
        ub, r, ig, sp, a, mult = _lru_gates(uv, wrg_v, brg_ref[...], wig_v, big_ref[...], lamv)
        abuf[0:tl, :] = a
        c_sc[...] = abuf[pl.ds(1, tl), :]
        gel, dgel = _gelu_and_grad(xg_ref[:, D_LRU:2 * D_LRU])
        drv = drec_ref[...]
        d_sc[...] = drv * gel
        dxg_ref[:, D_LRU:2 * D_LRU] = (drv * hsv * dgel).astype(BF16)

        def grp(k, lnext):
            off = pl.multiple_of((tl // 8 - 1 - k) * 8, 8)
            l8 = _rscan8(c_sc[pl.ds(off, 8), :], d_sc[pl.ds(off, 8), :], lnext)
            lam_sc[pl.ds(off, 8), :] = l8
            return l8[0:1, :]

        lcar[0:1, :] = lax.fori_loop(0, tl // 8, grp, lcar[0:1, :])
        abuf[tl:tl + 8, :] = a[0:8, :]
        db = lam_sc[...]
        da = db * hshift
        dmult = db * (ig * uv)
        dig = db * mult * uv
        du = db * mult * ig
        dla = da * a - dmult * (a * a) / mult
        dlam_ref[...] += jnp.sum(dla * (-LRU_C) * r, axis=0, keepdims=True)
        dzr = dla * (-LRU_C * sp) * r * (1.0 - r)
        dzi = dig * ig * (1.0 - ig)
        dzrb = dzr.astype(BF16)
        dzib = dzi.astype(BF16)
        du = du + _dot_nt(dzrb, wrg_v) + _dot_nt(dzib, wig_v)
        dwrg_ref[...] += _dot_tn(ub, dzrb)
        dwig_ref[...] += _dot_tn(ub, dzib)
        dbrg_ref[...] += jnp.sum(dzr, axis=0, keepdims=True)
        dbig_ref[...] += jnp.sum(dzi, axis=0, keepdims=True)
        dcb_ref[...] += jnp.sum(du, axis=0, keepdims=True)
        dubuf[0:tl, :] = du
        dxu0 = jnp.zeros((tl, D_LRU), F32)
        for j in range(4):
            dsh = dubuf[pl.ds(3 - j, tl), :]
            dxu0 = dxu0 + cw_ref[j:j + 1, :] * dsh
            dcw_ref[j:j + 1, :] += jnp.sum(xu0 * dsh, axis=0, keepdims=True)
        dubuf[tl:tl + 8, :] = du[0:8, :]
        dxg_ref[:, 0:D_LRU] = dxu0.astype(BF16)

        @pl.when(i == nt - 1)
        def _():
            dlam_ref[...] = dlam_ref[...] * (-jax.nn.sigmoid(-lamv))

    def rev(n):
        return pl.BlockSpec((tl, n), lambda i: (nt - 1 - i, 0))

    prev8 = pl.BlockSpec((8, D_LRU), lambda i: (jnp.maximum((nt - 1 - i) * (tl // 8) - 1, 0), 0))
    vec = _full((1, D_LRU))
    sq = _full((D_LRU, D_LRU))
    return _call(
        body, "b_lru", (nt,),
        [rev(D_LRU), rev(D_LRU), prev8, rev(D_LRU), rev(1024), _full((4, D_LRU)), sq, vec, sq, vec, vec],
        [rev(1024), sq, sq, vec, vec, vec, _full((4, D_LRU)), vec],
        [_sds((s_len, 1024), BF16), _sds((D_LRU, D_LRU), F32), _sds((D_LRU, D_LRU), F32),
         _sds((1, D_LRU), F32), _sds((1, D_LRU), F32), _sds((1, D_LRU), F32),
         _sds((4, D_LRU), F32), _sds((1, D_LRU), F32)],
        [pltpu.VMEM((tl + 8, D_LRU), F32)] * 3 + [pltpu.VMEM((tl, D_LRU), F32)] * 3
        + [pltpu.VMEM((8, D_LRU), F32)],
        (drec, hs, hs, u, xg, conv_w, wrg, brg, wig, big, lam), "arbitrary", comm)


def _b_attn(qkv_pad, att, datt, frow, comm=None):
    s_len = datt.shape[0]
    nb = s_len // QB
    n_pair = ATT_HEADS // 2
    pair_w = 2 * HEAD_DIM

    def body(q_ref, k0, k1, k2, v0, v1, v2, o_ref, do_ref, frow_ref, dq_ref, dkv_ref, dfrow_ref,
             bias_sc, dt_sc, acc_sc):
        t = pl.program_id(0)

        @pl.when(t == 0)
        def _():
            _bias_table(frow_ref, bias_sc)
            dt_sc[...] = jnp.zeros_like(dt_sc)
            acc_sc[...] = jnp.zeros_like(acc_sc)

        @pl.when(t < nb)
        def _():
            var = jnp.minimum(t, N_BIAS - 1)
            even = _even_lanes()
            for hp in range(n_pair):
                cs = slice(hp * pair_w, (hp + 1) * pair_w)
                qt = q_ref[:, cs]
                kts = [k0[:, cs], k1[:, cs], k2[:, cs]]
                vts = [v0[:, cs], v1[:, cs], v2[:, cs]]
                dot = do_ref[:, cs]
                dd = dot * o_ref[:, cs]
                qmt, dost, dsbs, pbs, dqs = [], [], [], [], []
                for e in range(2):
                    keep = even if e == 0 else jnp.logical_not(even)
                    qm = jnp.where(keep, qt, 0)
                    p = _att_probs(qm, kts, bias_sc[var, 2 * hp + e])
                    inv = 1.0 / jnp.sum(p, axis=-1, keepdims=True)
                    dos = jnp.where(keep, dot * inv, 0.0)
                    delta = jnp.sum(jnp.where(keep, dd, 0.0), axis=-1, keepdims=True) * inv
                    dp = jnp.concatenate([_dot_nt(dos.astype(BF16), v) for v in vts], axis=1)
                    ds = p * (dp - delta)
                    dt_sc[2 * hp + e] += ds
                    dsb = ds.astype(BF16)
                    dq = _dot(dsb[:, 0:QB], kts[0])
                    for j in (1, 2):
                        dq = dq + _dot(dsb[:, j * QB:(j + 1) * QB], kts[j])
                    dqs.append(dq)
                    dsbs.append(dsb)
                    pbs.append(p.astype(BF16))
                    qmt.append(qm.astype(F32).T.astype(BF16))
                    dost.append(dos.T.astype(BF16))
                dq_ref[:, cs] = (jnp.where(even, dqs[0], dqs[1]) * ATT_SCALE).astype(BF16)
                for j in range(3):
                    slot = (t + 1 + j) % 3
                    js = slice(j * QB, (j + 1) * QB)
                    acc_sc[slot, hp] += _dot(qmt[0], dsbs[0][:, js]) + _dot(qmt[1], dsbs[1][:, js])
                    acc_sc[slot, n_pair + hp] += _dot(dost[0], pbs[0][:, js]) + _dot(dost[1], pbs[1][:, js])

        done = (t + 1) % 3

        @pl.when(t >= 2)
        def _():
            for i in range(2 * n_pair):
                dkv_ref[:, i * pair_w:(i + 1) * pair_w] = acc_sc[done, i].T.astype(BF16)

        acc_sc[done] = jnp.zeros((2 * n_pair, pair_w, QB), F32)

        @pl.when(t == nb + 1)
        def _():
            row = lax.broadcasted_iota(jnp.int32, (8, ROLL_W), 0)
            pad = jnp.zeros((8, ROLL_W - KB), F32)
            for h in range(ATT_HEADS):
                acc8 = jnp.concatenate([dt_sc[h, 0:8, :], pad], axis=1)
                for a1 in range(1, QB // 8):
                    blk = jnp.concatenate([dt_sc[h, 8 * a1:8 * a1 + 8, :], pad], axis=1)
                    acc8 = acc8 + pltpu.roll(blk, ROLL_W - 8 * a1, 1)
                for k in range(3):
                    acc8 = jnp.where(((row >> k) & 1) == 1, pltpu.roll(acc8, ROLL_W - (1 << k), 1), acc8)
                dfrow_ref[h:h + 1, :] = jnp.sum(acc8, axis=0, keepdims=True)

    clamp = lambda t: jnp.minimum(t, nb - 1)
    qrows = pl.BlockSpec((QB, D_ATT), lambda t: (clamp(t), 0))
    return _call(
        body, "b_attn", (nb + 2,),
        _att_in_specs(clamp) + [qrows, qrows, _full((ATT_HEADS, ROLL_W))],
        [qrows, pl.BlockSpec((QB, 2 * D_ATT), lambda t: (jnp.maximum(t - 2, 0), 0)),
         _full((ATT_HEADS, ROLL_W))],
        [_sds((s_len, D_ATT), BF16), _sds((s_len, 2 * D_ATT), BF16), _sds((ATT_HEADS, ROLL_W), F32)],
        [pltpu.VMEM((N_BIAS, ATT_HEADS, QB, KB), F32), pltpu.VMEM((ATT_HEADS, QB, KB), F32),
         pltpu.VMEM((3, 2 * n_pair, pair_w, QB), F32)],
        (*([qkv_pad] * 7), att, datt, frow), "arbitrary", comm)


def _b_win(dq, dkv, dxg, h, ts):
    s_len = h.shape[0]
    steps = s_len // ts

    def body(dq_ref, dkv_ref, dxg_ref, h_ref, dw_ref, dwb_ref):
        @pl.when(pl.program_id(0) == 0)
        def _():
            dw_ref[...] = jnp.zeros_like(dw_ref)

        dproj = jnp.concatenate([dq_ref[...], dkv_ref[...], dxg_ref[...]], axis=1)
        hv = h_ref[...]
        for s in range(N_SHARD):
            dw_ref[s] += _dot_tn(hv, dproj[:, s * IN_SH:(s + 1) * IN_SH])

        @pl.when(pl.program_id(0) == steps - 1)
        def _():
            dwb_ref[...] = dw_ref[...].astype(BF16)

    wspec = _full((N_SHARD, 1024, IN_SH))
    return pl.pallas_call(
        body, name="b_win", grid=(steps,),
        in_specs=[_rows(ts, 512), _rows(ts, 1024), _rows(ts, 1024), _rows(ts, 1024)],
        out_specs=[wspec, wspec],
        out_shape=[_sds((N_SHARD, 1024, IN_SH), F32), _sds((N_SHARD, 1024, IN_SH), BF16)],
        compiler_params=_cp("arbitrary"))(dq, dkv, dxg, h)


def _b_inproj(dq, dkv, dxg, x, dx1, g_mix, w_in_g, tm, comm=None):
    s_len = x.shape[0]

    def body(dq_ref, dkv_ref, dxg_ref, x_ref, dx1_ref, g_ref, w_ref, gx_ref, dgm_ref):
        @pl.when(pl.program_id(0) == 0)
        def _():
            dgm_ref[...] = jnp.zeros_like(dgm_ref)

        dproj = jnp.concatenate([dq_ref[...], dkv_ref[...], dxg_ref[...]], axis=1)
        dh = jnp.zeros((tm, 1024), F32)
        for s in range(N_SHARD):
            dh = dh + _dot_nt(dproj[:, s * IN_SH:(s + 1) * IN_SH], w_ref[s])
        dx, dgm = _rms_bwd(dh, x_ref[...], g_ref[...])
        gx_ref[...] = dx1_ref[...] + dx
        dgm_ref[...] += dgm

    return _call(
        body, "b_inproj", (s_len // tm,),
        [_rows(tm, 512), _rows(tm, 1024), _rows(tm, 1024), _rows(tm, 1024), _rows(tm, 1024),
         _full((1, 1024)), _full((N_SHARD, 1024, IN_SH))],
        [_rows(tm, 1024), _full((1, 1024))],
        [_sds((s_len, 1024), F32), _sds((1, 1024), F32)],
        [], (dq, dkv, dxg, x, dx1, g_mix, w_in_g), "arbitrary", comm)


def _mm_tn(xa, ya, name, ts):
    s_len, k = xa.shape
    n = ya.shape[1]

    steps = s_len // ts

    def body(x_ref, y_ref, o_ref, ob_ref):
        @pl.when(pl.program_id(0) == 0)
        def _():
            o_ref[...] = jnp.zeros_like(o_ref)
        o_ref[...] += _dot_tn(x_ref[...].astype(BF16), y_ref[...].astype(BF16))

        @pl.when(pl.program_id(0) == steps - 1)
        def _():
            ob_ref[...] = o_ref[...].astype(BF16)

    return pl.pallas_call(
        body, name=name, grid=(steps,), in_specs=[_rows(ts, k), _rows(ts, n)],
        out_specs=[_full((k, n))] * 2, out_shape=[_sds((k, n), F32), _sds((k, n), BF16)],
        compiler_params=_cp("arbitrary"))(xa, ya)


def _mm_tn_ysh(xa, y4, name, ts):
    s_len, k = xa.shape
    n = y4.shape[2]

    steps = s_len // ts

    def body(x_ref, y_ref, o_ref, ob_ref):
        @pl.when(pl.program_id(0) == 0)
        def _():
            o_ref[...] = jnp.zeros_like(o_ref)
        xb = x_ref[...].astype(BF16)
        for s in range(N_SHARD):
            o_ref[s] += _dot_tn(xb, y_ref[s])

        @pl.when(pl.program_id(0) == steps - 1)
        def _():
            ob_ref[...] = o_ref[...].astype(BF16)

    return pl.pallas_call(
        body, name=name, grid=(steps,), in_specs=[_rows(ts, k), _sh_rows(ts, n)],
        out_specs=[_full((N_SHARD, k, n))] * 2,
        out_shape=[_sds((N_SHARD, k, n), F32), _sds((N_SHARD, k, n), BF16)],
        compiler_params=_cp("arbitrary"))(xa, y4)


def _mm_tn_xsh(x4, ya, name, ts):
    s_len, n = ya.shape
    k = x4.shape[2]

    steps = s_len // ts

    def body(x_ref, y_ref, o_ref, ob_ref):
        @pl.when(pl.program_id(0) == 0)
        def _():
            o_ref[...] = jnp.zeros_like(o_ref)
        yb = y_ref[...].astype(BF16)
        for s in range(N_SHARD):
            o_ref[s] += _dot_tn(x_ref[s], yb)

        @pl.when(pl.program_id(0) == steps - 1)
        def _():
            ob_ref[...] = o_ref[...].astype(BF16)

    return pl.pallas_call(
        body, name=name, grid=(steps,), in_specs=[_sh_rows(ts, k), _rows(ts, n)],
        out_specs=[_full((N_SHARD, k, n))] * 2,
        out_shape=[_sds((N_SHARD, k, n), F32), _sds((N_SHARD, k, n), BF16)],
        compiler_params=_cp("arbitrary"))(x4, ya)


def _frow_from_rel_bias(rb):
    hi = jnp.broadcast_to(rb[:, 256:257], (ATT_HEADS, 385))
    mid = rb[:, 1:256][:, ::-1]
    lo = jnp.broadcast_to(rb[:, 0:1], (ATT_HEADS, 128))
    wrap = jnp.broadcast_to(rb[:, 256:257], (ATT_HEADS, ROLL_W - KB))
    return jnp.concatenate([hi, mid, lo, wrap], axis=1)


def _rel_bias_grad_from_dfrow(df):
    g256 = jnp.sum(df[:, 0:385], axis=1, keepdims=True) + jnp.sum(df[:, KB:ROLL_W], axis=1, keepdims=True)
    mid = df[:, 385:640][:, ::-1]
    g0 = jnp.sum(df[:, 640:KB], axis=1, keepdims=True)
    return jnp.concatenate([g0, mid, g256], axis=1)


def _block_diag(w):
    eye = jnp.eye(8, dtype=w.dtype)
    return (w[:, :, None, :] * eye[:, None, :, None]).reshape(D_LRU, D_LRU)


def _block_diag_extract(dense):
    eye = jnp.eye(8, dtype=dense.dtype)
    return jnp.sum(dense.reshape(8, 64, 8, 64) * eye[:, None, :, None], axis=2)


MID = ['w_out', 'wq_c', 'wk_c', 'wv_c', 'wo_c']
AG_IN_INPROJ = ['w_out', 'wq_c', 'wk_c']
AG_IN_ATTN = ['wv_c', 'wo_c', 'w_gate']
AG_IN_LRU = ['w_up']
AG_IN_MID = ['w_down']
RS_IN_MID = ['w_gate', 'w_up']
RS_IN_LRU = ['w_down']
RS_IN_ATTN = MID


def _local_step(x, mem, tgt, p, gw, shards=None, chip=None):
    s_len = x.shape[0]
    tm = min(256, s_len)
    tmb = min(512, s_len)
    tl = min(512, s_len)
    frow = _frow_from_rel_bias(p['rel_bias'])
    wrg = _block_diag(p['w_rg']).astype(BF16)
    wig = _block_diag(p['w_ig']).astype(BF16)
    gw = dict(gw)

    big, bigb, recv, part, sib = {}, {}, {}, {}, {}

    def ag(names):
        return [] if shards is None else [("ag", [shards[n] for n in names])]

    def rs(names):
        return [] if shards is None else [("rs", [bigb[n] for n in names])]

    def swap(names):
        return [] if shards is None else [("swap", [part[n] for n in names])]

    def reduce_own(names):
        if shards is not None:
            for n in names:
                part[n] = _sum_parts(big[n], recv[n], chip, "sum_" + n)

    h, qkv_pad, xg, *got = _f_inproj(x, p['g_mix'], gw['w_in'], tmb, ag(AG_IN_INPROJ))
    gw.update(zip(AG_IN_INPROJ, got))
    att, *got = _f_attn(qkv_pad, frow, ag(AG_IN_ATTN))
    gw.update(zip(AG_IN_ATTN, got))
    rec, u, hs, *got = _f_lru(xg, p['conv_w'], p['conv_b'], wrg, p['b_rg'], wig, p['b_ig'], p['lru_L'], tl,
                              ag(AG_IN_LRU))
    gw.update(zip(AG_IN_LRU, got))
    w_out = gw['w_out'].reshape(1024, 1024)
    wq = gw['wq_c'].reshape(1024, 1024)
    wk = gw['wk_c'].reshape(1024, 1024)
    wv = gw['wv_c'].reshape(1024, 1024)
    wo = gw['wo_c'].reshape(1024, 1024)
    mn, kx, vx = _f_mem(mem, p['g_mem'], wk, wv)
    mg, x1, hc, qx, ox, x2, *got = _f_mid(x, att, rec, p['g_out_attn'], p['g_out_lru'], w_out, p['g_cross'],
                                          wq, kx, vx, wo, tmb, ag(AG_IN_MID))
    gw.update(zip(AG_IN_MID, got))
    hf, gact, uact, aact, dx3, loss, dg_final = _f_ffn(x2, tgt, p['g_ffn'], p['g_final'],
                                                       gw['w_gate'], gw['w_up'], gw['w_down'], tm)

    ts = min(512, s_len)
    dgact, duact, dx2, dg_ffn = _b_ffn(dx3, x2, gact, uact, p['g_ffn'], gw['w_gate'], gw['w_up'], gw['w_down'], tm)
    big['w_gate'], bigb['w_gate'] = _mm_tn_ysh(hf, dgact, "dw_gate", ts)
    big['w_up'], bigb['w_up'] = _mm_tn_ysh(hf, duact, "dw_up", ts)
    big['w_down'], bigb['w_down'] = _mm_tn_xsh(aact, dx3, "dw_down", ts)

    dqx, dx1, datt, drec, dkx, dvx, dg_cross, dg_oa, dg_ol, *got = _b_mid(
        dx2, qx, x1, att, rec, kx, vx, wo, wq, w_out, p['g_cross'], p['g_out_attn'], p['g_out_lru'], tm,
        rs(RS_IN_MID))
    recv.update(zip(RS_IN_MID, got))
    reduce_own(RS_IN_MID)
    dwk, dwv, dg_mem, dwkb, dwvb = _b_mem(dkx, dvx, mem, mn, p['g_mem'], wk, wv)
    big['wk_c'], bigb['wk_c'] = dwk, dwkb
    big['wv_c'], bigb['wv_c'] = dwv, dwvb
    big['w_out'], bigb['w_out'] = _mm_tn(mg, dx1, "dw_out", ts)
    big['wq_c'], bigb['wq_c'] = _mm_tn(hc, dqx, "dw_q", ts)
    big['wo_c'], bigb['wo_c'] = _mm_tn(ox, dx2, "dw_o", ts)
    for n in MID:
        big[n] = big[n].reshape(N_SHARD, 256, 1024)
        bigb[n] = bigb[n].reshape(N_SHARD, 256, 1024)

    dxg, dwrg, dwig, dbrg, dbig, dlam, dcw, dcb, *got = _b_lru(
        drec, hs, u, xg, p['conv_w'], wrg, p['b_rg'], wig, p['b_ig'], p['lru_L'], tl,
        rs(RS_IN_LRU) + swap(RS_IN_MID))
    recv.update(zip(RS_IN_LRU, got))
    sib.update(zip(RS_IN_MID, got[len(RS_IN_LRU):]))
    reduce_own(RS_IN_LRU)
    dq, dkv, dfrow, *got = _b_attn(qkv_pad, att, datt, frow, rs(RS_IN_ATTN) + swap(RS_IN_LRU))
    recv.update(zip(RS_IN_ATTN, got))
    sib.update(zip(RS_IN_LRU, got[len(RS_IN_ATTN):]))
    reduce_own(RS_IN_ATTN)
    big['w_in'], bigb['w_in'] = _b_win(dq, dkv, dxg, h, ts)
    grad_x, dg_mix, *got = _b_inproj(dq, dkv, dxg, x, dx1, p['g_mix'], gw['w_in'], tmb,
                                     rs(['w_in']) + swap(RS_IN_ATTN))
    recv.update(zip(['w_in'], got))
    sib.update(zip(RS_IN_ATTN, got[1:]))
    reduce_own(['w_in'])
    small = {
        'g_mix': dg_mix, 'rel_bias': _rel_bias_grad_from_dfrow(dfrow), 'conv_w': dcw, 'conv_b': dcb,
        'w_rg': _block_diag_extract(dwrg), 'b_rg': dbrg, 'w_ig': _block_diag_extract(dwig), 'b_ig': dbig,
        'lru_L': dlam,
        'g_out_attn': dg_oa, 'g_out_lru': dg_ol, 'g_cross': dg_cross, 'g_mem': dg_mem, 'g_ffn': dg_ffn,
        'g_final': dg_final,
    }
    return jnp.sum(loss[0, 0:1]), grad_x, small, big, part, sib


def _cast_shards(ws):
    def body(*refs):
        n = len(refs) // 2
        for src, dst in zip(refs[:n], refs[n:]):
            dst[...] = src[...].astype(BF16)

    return pl.pallas_call(body, name="cast_shards", out_shape=[_sds(w.shape, BF16) for w in ws],
                          compiler_params=_cp())(*ws)


def _sum_parts(own4, recv3, chip, name):
    _, r, c = own4.shape
    tr = r // 4

    def body(chip_ref, own_ref, rc_ref, o_ref):
        o_ref[...] = ((own_ref[0] + rc_ref[0].astype(F32)) + rc_ref[1].astype(F32)) + rc_ref[2].astype(F32)

    grid_spec = pltpu.PrefetchScalarGridSpec(
        num_scalar_prefetch=1, grid=(4,),
        in_specs=[pl.BlockSpec((1, tr, c), lambda i, ch: (ch[0], i, 0)),
                  pl.BlockSpec((3, tr, c), lambda i, ch: (0, i, 0))],
        out_specs=pl.BlockSpec((tr, c), lambda i, ch: (i, 0)))
    return pl.pallas_call(body, name=name, grid_spec=grid_spec, out_shape=_sds((r, c), F32),
                          compiler_params=_cp("parallel"))(chip, own4, recv3)


def _adamw_math(w, g, m, v):
    m = ADAM_B1 * m + (1.0 - ADAM_B1) * g
    v = ADAM_B2 * v + (1.0 - ADAM_B2) * (g * g)
    m_hat = m / (1.0 - ADAM_B1 ** ADAM_STEP)
    v_hat = v / (1.0 - ADAM_B2 ** ADAM_STEP)
    delta = -ADAM_LR * (m_hat / (jnp.sqrt(v_hat) + ADAM_EPS) + ADAM_WD * w)
    return delta, m, v


def _final_adamw(pa, pb, w, m, v, name):
    r, c = w.shape
    tr = r // 4

    def body(pa_ref, pb_ref, w_ref, m_ref, v_ref, g_ref, d_ref, nm_ref, nv_ref):
        g = pa_ref[...] + pb_ref[...]
        g_ref[...] = g
        d_ref[...], nm_ref[...], nv_ref[...] = _adamw_math(w_ref[...], g, m_ref[...], v_ref[...])

    return pl.pallas_call(
        body, name=name, grid=(4,), in_specs=[_rows(tr, c)] * 5, out_specs=[_rows(tr, c)] * 4,
        out_shape=[_sds((r, c), F32)] * 4, compiler_params=_cp("parallel"))(pa, pb, w, m, v)


def _ar_small(g, w, m, v):
    n = len(g)
    cw = SMALL.index('conv_w')

    def body(*refs):
        g_refs, w_refs, m_refs, v_refs = (refs[i * n:(i + 1) * n] for i in range(4))
        go, do, mo, vo, bufs = (refs[(4 + i) * n:(5 + i) * n] for i in range(5))
        cw_sc, send_sems, recv_sems = refs[9 * n:]
        x, y, c = _mesh_pos()
        me = 4 * x + 2 * y + c

        def peer(k):
            px = 1 - x if k & 4 else x
            py = 1 - y if k & 2 else y
            pc = 1 - c if k & 1 else c
            return px, py, pc

        def remote(a, k, slot):
            return pltpu.make_async_remote_copy(
                src_ref=g_refs[a], dst_ref=bufs[a].at[slot], send_sem=send_sems.at[a, k - 1],
                recv_sem=recv_sems.at[a, k - 1], device_id=peer(k), device_id_type=MESH_ID)

        for a in range(n):
            for k in range(1, 8):
                remote(a, k, me).start()
        for a in range(n):
            bufs[a][me] = g_refs[a][...]
        for a in range(n):
            for k in range(1, 8):
                px, py, pc = peer(k)
                remote(a, k, 4 * px + 2 * py + pc).wait_recv()
        for a in range(n):
            for k in range(1, 8):
                remote(a, k, me).wait_send()
        for a in range(n):
            tot = bufs[a][0]
            for k in range(1, 8):
                tot = tot + bufs[a][k]
            if a == cw:
                cw_sc[...] = tot
                tot = cw_sc[:, pl.ds(pl.multiple_of((2 * x + y) * 128, 128), 128)]
            go[a][...] = tot
            do[a][...], mo[a][...], vo[a][...] = _adamw_math(w_refs[a][...], tot, m_refs[a][...], v_refs[a][...])

    out_shape = [_sds(a.shape, F32) for a in w] * 4
    return pl.pallas_call(
        body, name="ar_small", out_shape=out_shape,
        scratch_shapes=[pltpu.VMEM((8,) + a.shape, F32) for a in g]
                       + [pltpu.VMEM(g[cw].shape, F32), pltpu.SemaphoreType.DMA((n, 7)),
                          pltpu.SemaphoreType.DMA((n, 7))],
        compiler_params=_cp())(*g, *w, *m, *v)


INPUT_NAMES = (['x', 'mem'] + WEIGHTS + ['loss_target'] + ['m_' + n for n in WEIGHTS] + ['v_' + n for n in WEIGHTS])


def kernel(x, mem, g_mix, w_in, rel_bias, conv_w, conv_b, w_rg, b_rg, w_ig, b_ig, lru_L, g_out_attn, g_out_lru, w_out, g_cross, g_mem, wq_c, wk_c, wv_c, wo_c, g_ffn, w_gate, w_up, w_down, g_final, loss_target, m_g_mix, m_w_in, m_rel_bias, m_conv_w, m_conv_b, m_w_rg, m_b_rg, m_w_ig, m_b_ig, m_lru_L, m_g_out_attn, m_g_out_lru, m_w_out, m_g_cross, m_g_mem, m_wq_c, m_wk_c, m_wv_c, m_wo_c, m_g_ffn, m_w_gate, m_w_up, m_w_down, m_g_final, v_g_mix, v_w_in, v_rel_bias, v_conv_w, v_conv_b, v_w_rg, v_b_rg, v_w_ig, v_b_ig, v_lru_L, v_g_out_attn, v_g_out_lru, v_w_out, v_g_cross, v_g_mem, v_wq_c, v_wk_c, v_wv_c, v_wo_c, v_g_ffn, v_w_gate, v_w_up, v_w_down, v_g_final):
    a = dict(zip(INPUT_NAMES, (x, mem, g_mix, w_in, rel_bias, conv_w, conv_b, w_rg, b_rg, w_ig, b_ig, lru_L, g_out_attn, g_out_lru, w_out, g_cross, g_mem, wq_c, wk_c, wv_c, wo_c, g_ffn, w_gate, w_up, w_down, g_final, loss_target, m_g_mix, m_w_in, m_rel_bias, m_conv_w, m_conv_b, m_w_rg, m_b_rg, m_w_ig, m_b_ig, m_lru_L, m_g_out_attn, m_g_out_lru, m_w_out, m_g_cross, m_g_mem, m_wq_c, m_wk_c, m_wv_c, m_wo_c, m_g_ffn, m_w_gate, m_w_up, m_w_down, m_g_final, v_g_mix, v_w_in, v_rel_bias, v_conv_w, v_conv_b, v_w_rg, v_b_rg, v_w_ig, v_b_ig, v_lru_L, v_g_out_attn, v_g_out_lru, v_w_out, v_g_cross, v_g_mem, v_wq_c, v_wk_c, v_wv_c, v_wo_c, v_g_ffn, v_w_gate, v_w_up, v_w_down, v_g_final)))
    chip = 2 * lax.axis_index("x") + lax.axis_index("y")

    shards = dict(zip(BIG, _cast_shards([a[n][0] for n in BIG])))
    w_in_g, conv_w_g = _comm_only("ag_w_in", [("ag", [shards['w_in']]), ("agf", [a['conv_w'][0]])])
    conv_w_full = conv_w_g.transpose(1, 0, 2).reshape(4, D_LRU)

    p = {n: a[n] for n in SMALL}
    p['rel_bias'] = a['rel_bias'][0]
    p['w_rg'] = a['w_rg'][0]
    p['w_ig'] = a['w_ig'][0]
    p['conv_w'] = conv_w_full
    p['g_final'] = a['g_final'][None, :]
    chip_arr = jnp.reshape(chip, (1,)).astype(jnp.int32)
    loss_part, grad_x, small, _, part, sib = _local_step(
        a['x'][0], a['mem'][0], a['loss_target'][0], p, {'w_in': w_in_g}, shards, chip_arr)
    loss = lax.psum(loss_part, ("x", "y", "c"))

    sib['w_in'], = _comm_only("swap_w_in", [("swap", [part['w_in']])])
    out = {}
    for n in BIG:
        out[n] = _final_adamw(part[n], sib[n], a[n][0], a['m_' + n][0], a['v_' + n][0], "adamw_" + n)

    def natural(arr):
        return arr[0] if arr.ndim >= 3 else (arr[None, :] if arr.ndim == 1 else arr)

    small_out = _ar_small([small[n] for n in SMALL], *[[natural(a[pre + n]) for n in SMALL] for pre in ('', 'm_', 'v_')])
    ns = len(SMALL)

    def leaf(i, n):
        if n in BIG:
            return out[n][i][None]
        return small_out[i * ns + SMALL.index(n)].reshape(a[n].shape)

    return (loss, grad_x[None], *[leaf(i, n) for i in range(4) for n in WEIGHTS])
```

```python
import math

import jax
import jax.numpy as jnp
from jax import lax
from jax.experimental import pallas as pl
from jax.experimental.pallas import tpu as pltpu

F32 = jnp.float32
BF16 = jnp.bfloat16

D_MODEL = 1024
D_ATT = 512
D_LRU = 512
HEAD_DIM = 64
ATT_HEADS = 8
CHUNK = 64
LEFT_CHUNKS = 8
X_HEADS = 4
X_HEAD_DIM = 256
N_SHARD = 4
IN_SH = 640
FF_SH = 704
EPS = 1e-6
LRU_C = 8.0
LRU_BLOCKS = 8
LRU_BLOCK = 64
QB = 256
KB = 768
ROLL_W = 1024
NEG = -1e30
ATT_SCALE = HEAD_DIM ** -0.5
X_SCALE = X_HEAD_DIM ** -0.5

ADAM_LR = 0.001
ADAM_B1 = 0.9
ADAM_B2 = 0.999
ADAM_EPS = 1e-08
ADAM_WD = 0.01
ADAM_STEP = 10

VMEM_LIMIT_V7X = 56 * 1024 * 1024
MESH_ID = pl.DeviceIdType.MESH

WEIGHTS = ['g_mix', 'w_in', 'rel_bias', 'conv_w', 'conv_b', 'w_rg', 'b_rg', 'w_ig', 'b_ig', 'lru_L',
           'g_out_attn', 'g_out_lru', 'w_out', 'g_cross', 'g_mem', 'wq_c', 'wk_c', 'wv_c', 'wo_c',
           'g_ffn', 'w_gate', 'w_up', 'w_down', 'g_final']
BIG = ['w_in', 'w_out', 'wq_c', 'wk_c', 'wv_c', 'wo_c', 'w_gate', 'w_up', 'w_down']
SMALL = [n for n in WEIGHTS if n not in BIG]


def _sds(shape, dtype):
    return jax.ShapeDtypeStruct(shape, dtype)


def _cp(*sem):
    return pltpu.CompilerParams(dimension_semantics=sem or None, vmem_limit_bytes=VMEM_LIMIT_V7X)


def _rows(tm, n):
    return pl.BlockSpec((tm, n), lambda i: (i, 0))


def _full(shape):
    nd = len(shape)
    return pl.BlockSpec(shape, lambda i: (0,) * nd)


def _dot(a, b):
    return jnp.dot(a, b, preferred_element_type=F32)


def _dot_nt(a, b):
    return lax.dot_general(a, b, (((1,), (1,)), ((), ())), preferred_element_type=F32)


def _dot_tn(a, b):
    return lax.dot_general(a, b, (((0,), (0,)), ((), ())), preferred_element_type=F32)


def _rinv(x):
    return lax.rsqrt(jnp.mean(x * x, axis=-1, keepdims=True) + EPS)


def _rms_bwd(dy, x, g):
    r = _rinv(x)
    yh = x * r
    dyh = dy * g
    dx = r * (dyh - yh * jnp.mean(dyh * yh, axis=-1, keepdims=True))
    return dx, jnp.sum(dy * yh, axis=0, keepdims=True)


def _gelu(x):
    c = math.sqrt(2.0 / math.pi)
    t = jnp.tanh(c * (x + 0.044715 * x * x * x))
    return 0.5 * x * (1.0 + t)


def _gelu_and_grad(x):
    c = math.sqrt(2.0 / math.pi)
    t = jnp.tanh(c * (x + 0.044715 * x * x * x))
    g = 0.5 * x * (1.0 + t)
    dg = 0.5 * (1.0 + t) + 0.5 * x * (1.0 - t * t) * c * (1.0 + 3.0 * 0.044715 * x * x)
    return g, dg


def _neg_expm1(z):
    series = -z * (1 + z / 2 * (1 + z / 3 * (1 + z / 4)))
    return jnp.where(z > -0.03, series, 1.0 - jnp.exp(z))


def _lru_gates(u, wrg, brg, wig, big, lam):
    ub = u.astype(BF16)
    r = jax.nn.sigmoid(_dot(ub, wrg) + brg)
    ig = jax.nn.sigmoid(_dot(ub, wig) + big)
    sp = jnp.maximum(-lam, 0.0) + jnp.log1p(jnp.exp(-jnp.abs(lam)))
    la = -LRU_C * r * sp
    a = jnp.exp(la)
    mult = jnp.sqrt(jnp.maximum(_neg_expm1(2.0 * la), 0.0))
    return ub, r, ig, sp, a, mult


def _scan8(a8, b8, hprev):
    row = lax.broadcasted_iota(jnp.int32, a8.shape, 0)
    aa, bb = a8, b8
    for d in (1, 2, 4):
        a_s = pltpu.roll(aa, d, 0)
        b_s = pltpu.roll(bb, d, 0)
        m = row >= d
        bb = jnp.where(m, aa * b_s + bb, bb)
        aa = jnp.where(m, aa * a_s, aa)
    return aa * hprev + bb


def _mesh_pos():
    return lax.axis_index("x"), lax.axis_index("y"), lax.axis_index("c")


def _other_chips(x, y):
    return [(1 - x, y), (x, 1 - y), (1 - x, 1 - y)]


def _no_forward():
    pass


def _ag_full_copies(ins, outs, sems):
    send_sems, recv_sems, loc_sems = sems
    n = len(ins)
    x, y, c = _mesh_pos()
    mine = 2 * x + y
    chips = _other_chips(x, y)

    def remote(k, j, slot):
        px, py = chips[j]
        return pltpu.make_async_remote_copy(
            src_ref=ins[k], dst_ref=outs[k].at[slot], send_sem=send_sems.at[k, j], recv_sem=recv_sems.at[k, j],
            device_id=(px, py, c), device_id_type=MESH_ID)

    def local(k):
        return pltpu.make_async_copy(ins[k], outs[k].at[mine], loc_sems.at[k])

    def start():
        for k in range(n):
            local(k).start()
            for j in range(3):
                remote(k, j, mine).start()

    def wait():
        for k in range(n):
            for j, (px, py) in enumerate(chips):
                remote(k, j, 2 * px + py).wait_recv()
        for k in range(n):
            for j in range(3):
                remote(k, j, mine).wait_send()
            local(k).wait()

    return start, _no_forward, wait


def _ag_copies(ins, outs, sems):
    send_sems, recv_sems, fsend_sems, frecv_sems, loc_sems = sems
    n = len(ins)
    x, y, c = _mesh_pos()
    mine = 2 * x + y
    chips = _other_chips(x, y)

    def half(ref, hc):
        r = ref.shape[0] // 2
        return ref.at[pl.ds(pl.multiple_of(hc * r, 16), r)]

    def ici(k, j, slot):
        px, py = chips[j]
        return pltpu.make_async_remote_copy(
            src_ref=half(ins[k], c), dst_ref=half(outs[k].at[slot], c),
            send_sem=send_sems.at[k, j], recv_sem=recv_sems.at[k, j],
            device_id=(px, py, c), device_id_type=MESH_ID)

    def d2d(k, j, hc):
        px, py = chips[j]
        part = half(outs[k].at[2 * px + py], hc)
        return pltpu.make_async_remote_copy(
            src_ref=part, dst_ref=part, send_sem=fsend_sems.at[k, j], recv_sem=frecv_sems.at[k, j],
            device_id=(x, y, 1 - c), device_id_type=MESH_ID)

    def local(k):
        return pltpu.make_async_copy(ins[k], outs[k].at[mine], loc_sems.at[k])

    def start():
        for k in range(n):
            local(k).start()
            for j in range(3):
                ici(k, j, mine).start()

    def forward():
        for k in range(n):
            for j, (px, py) in enumerate(chips):
                ici(k, j, 2 * px + py).wait_recv()
                d2d(k, j, c).start()

    def wait():
        for k in range(n):
            for j in range(3):
                d2d(k, j, 1 - c).wait_recv()
        for k in range(n):
            for j in range(3):
                d2d(k, j, c).wait_send()
                ici(k, j, mine).wait_send()
            local(k).wait()

    return start, forward, wait


def _rs_copies(ins, outs, sems):
    send_sems, recv_sems = sems
    n = len(ins)
    x, y, c = _mesh_pos()
    chips = _other_chips(x, y)

    def remote(k, j):
        px, py = chips[j]
        return pltpu.make_async_remote_copy(
            src_ref=ins[k].at[2 * px + py], dst_ref=outs[k].at[j],
            send_sem=send_sems.at[k, j], recv_sem=recv_sems.at[k, j],
            device_id=(px, py, c), device_id_type=MESH_ID)

    def start():
        for k in range(n):
            for j in range(3):
                remote(k, j).start()

    def wait():
        for k in range(n):
            for j in range(3):
                remote(k, j).wait_recv()
        for k in range(n):
            for j in range(3):
                remote(k, j).wait_send()

    return start, _no_forward, wait


def _swap_copies(ins, outs, sems):
    send_sems, recv_sems = sems
    x, y, c = _mesh_pos()
    copies = [pltpu.make_async_remote_copy(
        src_ref=ins[k], dst_ref=outs[k], send_sem=send_sems.at[k], recv_sem=recv_sems.at[k],
        device_id=(x, y, 1 - c), device_id_type=MESH_ID) for k in range(len(ins))]

    def start():
        for cp in copies:
            cp.start()

    def wait():
        for cp in copies:
            cp.wait()

    return start, _no_forward, wait


def _comm_plan(groups):
    plan, arrs, shapes, sems = [], [], [], []
    for kind, group in groups:
        k = len(group)
        arrs += group
        per_peer = pltpu.SemaphoreType.DMA((k, 3))
        if kind == "ag":
            shapes += [_sds((N_SHARD,) + w.shape, w.dtype) for w in group]
            gsems = [per_peer] * 4 + [pltpu.SemaphoreType.DMA((k,))]
            maker = _ag_copies
        elif kind == "agf":
            shapes += [_sds((N_SHARD,) + w.shape, w.dtype) for w in group]
            gsems = [per_peer] * 2 + [pltpu.SemaphoreType.DMA((k,))]
            maker = _ag_full_copies
        elif kind == "rs":
            shapes += [_sds((3,) + g.shape[1:], g.dtype) for g in group]
            gsems = [pltpu.SemaphoreType.DMA((k, 3)), pltpu.SemaphoreType.DMA((k, 3))]
            maker = _rs_copies
        else:
            shapes += [_sds(g.shape, g.dtype) for g in group]
            gsems = [pltpu.SemaphoreType.DMA((k,)), pltpu.SemaphoreType.DMA((k,))]
            maker = _swap_copies
        plan.append((maker, k, len(gsems)))
        sems += gsems
    return plan, arrs, shapes, sems


def _comm_fns(plan, cins, couts, sems):
    fns, a, s = [], 0, 0
    for maker, k, ns in plan:
        fns.append(maker(cins[a:a + k], couts[a:a + k], sems[s:s + ns]))
        a += k
        s += ns

    def start():
        for st, _, _ in fns:
            st()

    def forward():
        for _, fw, _ in fns:
            fw()

    def wait():
        for _, _, wt in fns:
            wt()

    return start, forward, wait


def _call(body, name, grid, in_specs, out_specs, out_shape, scratch, args, sem, comm=None):
    if not comm:
        return pl.pallas_call(body, name=name, grid=grid, in_specs=in_specs, out_specs=out_specs,
                              out_shape=out_shape, scratch_shapes=scratch, compiler_params=_cp(sem))(*args)
    plan, c_arrs, c_shapes, c_sems = _comm_plan(comm)
    k = len(c_arrs)
    n_in, n_out, n_scr = len(in_specs), len(out_specs), len(scratch)
    last = grid[0] - 1
    fwd_step = max(1, (2 * last) // 3)

    def wrapped(*refs):
        ins, cins = refs[:n_in], refs[n_in:n_in + k]
        o0 = n_in + k
        outs, couts = refs[o0:o0 + n_out], refs[o0 + n_out:o0 + n_out + k]
        s0 = o0 + n_out + k
        start, forward, wait = _comm_fns(plan, cins, couts, refs[s0 + n_scr:])
        pl.when(pl.program_id(0) == 0)(start)
        pl.when(pl.program_id(0) == fwd_step)(forward)
        body(*ins, *outs, *refs[s0:s0 + n_scr])
        pl.when(pl.program_id(0) == last)(wait)

    return pl.pallas_call(
        wrapped, name=name, grid=grid, in_specs=list(in_specs) + [_any()] * k,
        out_specs=list(out_specs) + [_any()] * k, out_shape=list(out_shape) + c_shapes,
        scratch_shapes=list(scratch) + c_sems, compiler_params=_cp(sem))(*args, *c_arrs)


def _comm_only(name, comm):
    plan, c_arrs, c_shapes, c_sems = _comm_plan(comm)
    k = len(c_arrs)

    def body(*refs):
        start, forward, wait = _comm_fns(plan, refs[:k], refs[k:2 * k], refs[2 * k:])
        start()
        forward()
        wait()

    return pl.pallas_call(body, name=name, in_specs=[_any()] * k, out_specs=[_any()] * k, out_shape=c_shapes,
                          scratch_shapes=c_sems, compiler_params=_cp())(*c_arrs)


def _any():
    return pl.BlockSpec(memory_space=pl.ANY)


def _rscan8(c8, d8, lnext):
    row = lax.broadcasted_iota(jnp.int32, c8.shape, 0)
    cc, dd = c8, d8
    for d in (1, 2, 4):
        c_s = pltpu.roll(cc, 8 - d, 0)
        d_s = pltpu.roll(dd, 8 - d, 0)
        m = row < 8 - d
        dd = jnp.where(m, cc * d_s + dd, dd)
        cc = jnp.where(m, cc * c_s, cc)
    return cc * lnext + dd


def _f_inproj(x, g_mix, w_in_g, tm, comm=None):
    s_len = x.shape[0]
    pad_rows = LEFT_CHUNKS * CHUNK
    npad = pad_rows // tm

    def body(x_ref, g_ref, w_ref, h_ref, qkv_ref, xg_ref):
        i = pl.program_id(0)

        @pl.when(i < npad)
        def _():
            qkv_ref[...] = jnp.zeros_like(qkv_ref)

        @pl.when(i >= npad)
        def _():
            xv = x_ref[...]
            h = (xv * _rinv(xv) * g_ref[...]).astype(BF16)
            h_ref[...] = h
            p0 = _dot(h, w_ref[0])
            qkv_ref[:, 0:D_ATT] = (p0[:, 0:D_ATT] * ATT_SCALE).astype(BF16)
            qkv_ref[:, D_ATT:640] = p0[:, D_ATT:640].astype(BF16)
            qkv_ref[:, 640:1280] = _dot(h, w_ref[1]).astype(BF16)
            p2 = _dot(h, w_ref[2])
            qkv_ref[:, 1280:1536] = p2[:, 0:256].astype(BF16)
            xg_ref[:, 0:384] = p2[:, 256:640]
            xg_ref[:, 384:1024] = _dot(h, w_ref[3])

    def tok(n):
        return pl.BlockSpec((tm, n), lambda i: (jnp.maximum(i - npad, 0), 0))

    return _call(
        body, "f_inproj", (s_len // tm + npad,),
        [tok(1024), _full((1, 1024)), _full((N_SHARD, 1024, IN_SH))],
        [tok(1024), _rows(tm, 1536), tok(1024)],
        [_sds((s_len, 1024), BF16), _sds((s_len + pad_rows, 1536), BF16), _sds((s_len, 1024), F32)],
        [], (x, g_mix, w_in_g), "arbitrary", comm)


N_BIAS = 3


def _bias_table(frow_ref, bias_sc):
    qa = lax.broadcasted_iota(jnp.int32, (QB, KB), 0) // CHUNK
    kcol = lax.broadcasted_iota(jnp.int32, (QB, KB), 1)
    kb = kcol // CHUNK
    band = jnp.where((kb >= qa) & (kb - qa <= LEFT_CHUNKS), 0.0, NEG).astype(F32)
    for h in range(ATT_HEADS):
        row = jnp.broadcast_to(frow_ref[h:h + 1, :], (QB, ROLL_W))
        toep = pltpu.roll(row, 0, 1, stride=1, stride_axis=0)
        gen = toep[:, 0:KB] + band
        bias_sc[N_BIAS - 1, h] = gen
        for v in range(N_BIAS - 1):
            pad_keys = LEFT_CHUNKS * CHUNK - v * QB
            bias_sc[v, h] = gen + jnp.where(kcol < pad_keys, NEG, 0.0).astype(F32)


def _even_lanes():
    return lax.broadcasted_iota(jnp.int32, (1, 2 * HEAD_DIM), 1) < HEAD_DIM


def _att_probs(qm, kts, bias):
    s = jnp.concatenate([_dot_nt(qm, k) for k in kts], axis=1) + bias
    return jnp.exp(s - jnp.max(s, axis=-1, keepdims=True))


def _att_in_specs(clamp):
    def spec(j, col):
        return pl.BlockSpec((QB, D_ATT), lambda i: (clamp(i) + j, col))
    return [spec(2, 0), spec(0, 1), spec(1, 1), spec(2, 1), spec(0, 2), spec(1, 2), spec(2, 2)]


def _f_attn(qkv_pad, frow, comm=None):
    s_len = qkv_pad.shape[0] - LEFT_CHUNKS * CHUNK
    nb = s_len // QB

    def body(q_ref, k0, k1, k2, v0, v1, v2, frow_ref, o_ref, bias_sc):
        i = pl.program_id(0)

        @pl.when(i == 0)
        def _():
            _bias_table(frow_ref, bias_sc)

        var = jnp.minimum(i, N_BIAS - 1)
        even = _even_lanes()
        for hp in range(ATT_HEADS // 2):
            cs = slice(hp * 2 * HEAD_DIM, (hp + 1) * 2 * HEAD_DIM)
            qt = q_ref[:, cs]
            kts = [k0[:, cs], k1[:, cs], k2[:, cs]]
            vts = [v0[:, cs], v1[:, cs], v2[:, cs]]
            res = []
            for e in range(2):
                keep = even if e == 0 else jnp.logical_not(even)
                pb = _att_probs(jnp.where(keep, qt, 0), kts, bias_sc[var, 2 * hp + e]).astype(BF16)
                r = _dot(pb[:, 0:QB], jnp.where(keep, vts[0], 1))
                for j in (1, 2):
                    r = r + _dot(pb[:, j * QB:(j + 1) * QB], jnp.where(keep, vts[j], 1))
                res.append(r / pltpu.roll(r, HEAD_DIM, 1))
            o_ref[:, cs] = jnp.where(even, res[0], res[1])

    return _call(
        body, "f_attn", (nb,),
        _att_in_specs(lambda i: i) + [_full((ATT_HEADS, ROLL_W))],
        [_rows(QB, D_ATT)], [_sds((s_len, D_ATT), F32)],
        [pltpu.VMEM((N_BIAS, ATT_HEADS, QB, KB), F32)], (*([qkv_pad] * 7), frow), "arbitrary", comm)


def _f_lru(xg, conv_w, conv_b, wrg, brg, wig, big, lam, tl, comm=None):
    s_len = xg.shape[0]

    def body(xg_ref, cw_ref, cb_ref, wrg_ref, brg_ref, wig_ref, big_ref, l_ref,
             rec_ref, u_ref, hs_ref, xbuf, a_sc, b_sc, hcar):
        i = pl.program_id(0)

        @pl.when(i == 0)
        def _():
            xbuf[0:8, :] = jnp.zeros((8, D_LRU), F32)
            hcar[...] = jnp.zeros((8, D_LRU), F32)

        xu0 = xg_ref[:, 0:D_LRU]
        xbuf[8:8 + tl, :] = xu0
        u = cb_ref[...] + cw_ref[0:1, :] * xbuf[pl.ds(5, tl), :]
        for j in range(1, 4):
            u = u + cw_ref[j:j + 1, :] * xbuf[pl.ds(5 + j, tl), :]
        xbuf[0:8, :] = xu0[tl - 8:tl, :]
        u_ref[...] = u
        _, _, ig, _, a, mult = _lru_gates(u, wrg_ref[...], brg_ref[...], wig_ref[...], big_ref[...], l_ref[...])
        a_sc[...] = a
        b_sc[...] = mult * (ig * u)

        def grp(g, hprev):
            off = pl.multiple_of(g * 8, 8)
            h8 = _scan8(a_sc[pl.ds(off, 8), :], b_sc[pl.ds(off, 8), :], hprev)
            hs_ref[pl.ds(off, 8), :] = h8
            return h8[7:8, :]

        hcar[0:1, :] = lax.fori_loop(0, tl // 8, grp, hcar[0:1, :])
        rec_ref[...] = hs_ref[...] * _gelu(xg_ref[:, D_LRU:2 * D_LRU])

    vec = _full((1, D_LRU))
    return _call(
        body, "f_lru", (s_len // tl,),
        [_rows(tl, 1024), _full((4, D_LRU)), vec, _full((D_LRU, D_LRU)), vec, _full((D_LRU, D_LRU)), vec, vec],
        [_rows(tl, D_LRU)] * 3, [_sds((s_len, D_LRU), F32)] * 3,
        [pltpu.VMEM((tl + 8, D_LRU), F32), pltpu.VMEM((tl, D_LRU), F32),
         pltpu.VMEM((tl, D_LRU), F32), pltpu.VMEM((8, D_LRU), F32)],
        (xg, conv_w, conv_b, wrg, brg, wig, big, lam), "arbitrary", comm)


def _f_mem(mem, g_mem, wk, wv):
    def body(mem_ref, g_ref, wk_ref, wv_ref, mn_ref, kx_ref, vx_ref):
        mv = mem_ref[...]
        mn = (mv * _rinv(mv) * g_ref[...]).astype(BF16)
        mn_ref[...] = mn
        kx_ref[...] = _dot(mn, wk_ref[...]).astype(BF16)
        vx_ref[...] = _dot(mn, wv_ref[...]).astype(BF16)

    m = mem.shape[0]
    return pl.pallas_call(
        body, name="f_mem", out_shape=[_sds((m, 1024), BF16)] * 3,
        compiler_params=_cp())(mem, g_mem, wk, wv)


def _xattn_probs(q, k):
    s = _dot_nt(q, k) * X_SCALE
    m = jnp.max(s, axis=-1, keepdims=True)
    p = jnp.exp(s - m)
    return p, jnp.sum(p, axis=-1, keepdims=True)


def _f_mid(x, att, rec, g_oa, g_ol, w_out, g_cross, wq, kx, vx, wo, tm, comm=None):
    s_len = x.shape[0]
    m_len = kx.shape[0]

    def body(x_ref, att_ref, rec_ref, goa_ref, gol_ref, wout_ref, gc_ref, wq_ref, kx_ref, vx_ref, wo_ref,
             mg_ref, x1_ref, hc_ref, qx_ref, ox_ref, x2_ref):
        av = att_ref[...]
        rv = rec_ref[...]
        mg_ref[:, 0:D_ATT] = (av * _rinv(av) * goa_ref[...]).astype(BF16)
        mg_ref[:, D_ATT:1024] = (rv * _rinv(rv) * gol_ref[...]).astype(BF16)
        x1 = x_ref[...] + _dot(mg_ref[...], wout_ref[...])
        x1_ref[...] = x1
        hc = (x1 * _rinv(x1) * gc_ref[...]).astype(BF16)
        hc_ref[...] = hc
        qx_ref[...] = _dot(hc, wq_ref[...]).astype(BF16)
        for h in range(X_HEADS):
            sl = slice(h * X_HEAD_DIM, (h + 1) * X_HEAD_DIM)
            p, l = _xattn_probs(qx_ref[:, sl], kx_ref[:, sl])
            ox_ref[:, sl] = (_dot(p.astype(BF16), vx_ref[:, sl]) / l).astype(BF16)
        x2_ref[...] = x1 + _dot(ox_ref[...], wo_ref[...])

    sq = _full((1024, 1024))
    return _call(
        body, "f_mid", (s_len // tm,),
        [_rows(tm, 1024), _rows(tm, 512), _rows(tm, 512), _full((1, 512)), _full((1, 512)), sq,
         _full((1, 1024)), sq, _full((m_len, 1024)), _full((m_len, 1024)), sq],
        [_rows(tm, 1024)] * 6,
        [_sds((s_len, 1024), BF16), _sds((s_len, 1024), F32), _sds((s_len, 1024), BF16),
         _sds((s_len, 1024), BF16), _sds((s_len, 1024), BF16), _sds((s_len, 1024), F32)],
        [], (x, att, rec, g_oa, g_ol, w_out, g_cross, wq, kx, vx, wo), "arbitrary", comm)


def _load_weights_once(pairs):
    @pl.when(pl.program_id(0) == 0)
    def _():
        for hbm, vmem in pairs:
            pltpu.sync_copy(hbm, vmem)


def _sh_rows(tm, n):
    return pl.BlockSpec((N_SHARD, tm, n), lambda i: (0, i, 0))


def _f_ffn(x2, tgt, g_ffn, g_final, wg, wu, wd, tm):
    s_len = x2.shape[0]

    def body(x2_ref, t_ref, gf_ref, gfin_ref, wg_hbm, wu_hbm, wd_hbm,
             hf_ref, g_ref, u_ref, a_ref, dx3_ref, loss_ref, dgfin_ref, wg_ref, wu_ref, wd_ref):
        _load_weights_once([(wg_hbm, wg_ref), (wu_hbm, wu_ref), (wd_hbm, wd_ref)])

        @pl.when(pl.program_id(0) == 0)
        def _():
            loss_ref[...] = jnp.zeros_like(loss_ref)
            dgfin_ref[...] = jnp.zeros_like(dgfin_ref)

        x2v = x2_ref[...]
        hf = (x2v * _rinv(x2v) * gf_ref[...]).astype(BF16)
        hf_ref[...] = hf
        x3 = x2v
        for s in range(N_SHARD):
            gv = _dot(hf, wg_ref[s])
            uv = _dot(hf, wu_ref[s])
            av = (gv * jax.nn.sigmoid(gv) * uv).astype(BF16)
            g_ref[s] = gv.astype(BF16)
            u_ref[s] = uv.astype(BF16)
            a_ref[s] = av
            x3 = x3 + _dot(av, wd_ref[s])
        r3 = _rinv(x3)
        yh = x3 * r3
        gfin = gfin_ref[...]
        err = yh * gfin - t_ref[...]
        loss_ref[...] += jnp.full((1, 128), 0.5 / D_MODEL, F32) * jnp.sum(err * err)
        dy = err * (1.0 / D_MODEL)
        dgfin_ref[...] += jnp.sum(dy * yh, axis=0, keepdims=True)
        dyh = dy * gfin
        dx3_ref[...] = r3 * (dyh - yh * jnp.mean(dyh * yh, axis=-1, keepdims=True))

    vec = _full((1, 1024))
    return pl.pallas_call(
        body, name="f_ffn", grid=(s_len // tm,),
        in_specs=[_rows(tm, 1024), _rows(tm, 1024), vec, vec, _any(), _any(), _any()],
        out_specs=[_rows(tm, 1024), _sh_rows(tm, FF_SH), _sh_rows(tm, FF_SH), _sh_rows(tm, FF_SH),
                   _rows(tm, 1024), _full((1, 128)), vec],
        out_shape=[_sds((s_len, 1024), BF16)] + [_sds((N_SHARD, s_len, FF_SH), BF16)] * 3
                  + [_sds((s_len, 1024), F32), _sds((1, 128), F32), _sds((1, 1024), F32)],
        scratch_shapes=[pltpu.VMEM((N_SHARD, 1024, FF_SH), BF16), pltpu.VMEM((N_SHARD, 1024, FF_SH), BF16),
                        pltpu.VMEM((N_SHARD, FF_SH, 1024), BF16)],
        compiler_params=_cp("arbitrary"))(x2, tgt, g_ffn, g_final, wg, wu, wd)


def _b_ffn(dx3, x2, gact, uact, g_ffn, wg, wu, wd, tm):
    s_len = x2.shape[0]

    def body(dx3_ref, x2_ref, g_ref, u_ref, gf_ref, wg_hbm, wu_hbm, wd_hbm,
             dg_ref, du_ref, dx2_ref, dgf_ref, wg_ref, wu_ref, wd_ref):
        _load_weights_once([(wg_hbm, wg_ref), (wu_hbm, wu_ref), (wd_hbm, wd_ref)])

        @pl.when(pl.program_id(0) == 0)
        def _():
            dgf_ref[...] = jnp.zeros_like(dgf_ref)

        dx3v = dx3_ref[...]
        dx3b = dx3v.astype(BF16)
        dhf = jnp.zeros(dx3v.shape, F32)
        for s in range(N_SHARD):
            da = _dot_nt(dx3b, wd_ref[s])
            gv = g_ref[s].astype(F32)
            uv = u_ref[s].astype(F32)
            sg = jax.nn.sigmoid(gv)
            dub = (da * gv * sg).astype(BF16)
            dgb = (da * uv * (sg * (1.0 + gv * (1.0 - sg)))).astype(BF16)
            du_ref[s] = dub
            dg_ref[s] = dgb
            dhf = dhf + _dot_nt(dgb, wg_ref[s]) + _dot_nt(dub, wu_ref[s])
        dx, dgf = _rms_bwd(dhf, x2_ref[...], gf_ref[...])
        dx2_ref[...] = dx3v + dx
        dgf_ref[...] += dgf

    vec = _full((1, 1024))
    return pl.pallas_call(
        body, name="b_ffn", grid=(s_len // tm,),
        in_specs=[_rows(tm, 1024), _rows(tm, 1024), _sh_rows(tm, FF_SH), _sh_rows(tm, FF_SH), vec,
                  _any(), _any(), _any()],
        out_specs=[_sh_rows(tm, FF_SH), _sh_rows(tm, FF_SH), _rows(tm, 1024), vec],
        out_shape=[_sds((N_SHARD, s_len, FF_SH), BF16)] * 2 + [_sds((s_len, 1024), F32), _sds((1, 1024), F32)],
        scratch_shapes=[pltpu.VMEM((N_SHARD, 1024, FF_SH), BF16), pltpu.VMEM((N_SHARD, 1024, FF_SH), BF16),
                        pltpu.VMEM((N_SHARD, FF_SH, 1024), BF16)],
        compiler_params=_cp("arbitrary"))(dx3, x2, gact, uact, g_ffn, wg, wu, wd)


def _b_mid(dx2, qx, x1, att, rec, kx, vx, wo, wq, w_out, g_cross, g_oa, g_ol, tm, comm=None):
    s_len = x1.shape[0]
    m_len = kx.shape[0]

    def body(dx2_ref, qx_ref, x1_ref, att_ref, rec_ref, kx_ref, vx_ref, wo_ref, wq_ref, wout_ref,
             gc_ref, goa_ref, gol_ref,
             dqx_ref, dx1_ref, datt_ref, drec_ref, dkx_ref, dvx_ref, dgc_ref, dgoa_ref, dgol_ref):
        @pl.when(pl.program_id(0) == 0)
        def _():
            for r in (dkx_ref, dvx_ref, dgc_ref, dgoa_ref, dgol_ref):
                r[...] = jnp.zeros_like(r)

        dx2v = dx2_ref[...]
        dox = _dot_nt(dx2v.astype(BF16), wo_ref[...])
        for h in range(X_HEADS):
            sl = slice(h * X_HEAD_DIM, (h + 1) * X_HEAD_DIM)
            q = qx_ref[:, sl]
            p, l = _xattn_probs(q, kx_ref[:, sl])
            pn = p / l
            dob = dox[:, sl].astype(BF16)
            dp = _dot_nt(dob, vx_ref[:, sl])
            dvx_ref[:, sl] += _dot_tn(pn.astype(BF16), dob)
            ds = pn * (dp - jnp.sum(dp * pn, axis=-1, keepdims=True))
            dsb = (ds * X_SCALE).astype(BF16)
            dqx_ref[:, sl] = _dot(dsb, kx_ref[:, sl]).astype(BF16)
            dkx_ref[:, sl] += _dot_tn(dsb, q)
        dhc = _dot_nt(dqx_ref[...], wq_ref[...])
        dx, dgc = _rms_bwd(dhc, x1_ref[...], gc_ref[...])
        dx1 = dx2v + dx
        dx1_ref[...] = dx1
        dgc_ref[...] += dgc
        dmg = _dot_nt(dx1.astype(BF16), wout_ref[...])
        da, dgoa = _rms_bwd(dmg[:, 0:D_ATT], att_ref[...], goa_ref[...])
        datt_ref[...] = da
        dgoa_ref[...] += dgoa
        dr, dgol = _rms_bwd(dmg[:, D_ATT:1024], rec_ref[...], gol_ref[...])
        drec_ref[...] = dr
        dgol_ref[...] += dgol

    sq = _full((1024, 1024))
    mk = _full((m_len, 1024))
    return _call(
        body, "b_mid", (s_len // tm,),
        [_rows(tm, 1024), _rows(tm, 1024), _rows(tm, 1024), _rows(tm, 512), _rows(tm, 512), mk, mk,
         sq, sq, sq, _full((1, 1024)), _full((1, 512)), _full((1, 512))],
        [_rows(tm, 1024), _rows(tm, 1024), _rows(tm, 512), _rows(tm, 512), mk, mk,
         _full((1, 1024)), _full((1, 512)), _full((1, 512))],
        [_sds((s_len, 1024), BF16), _sds((s_len, 1024), F32), _sds((s_len, 512), F32),
         _sds((s_len, 512), F32), _sds((m_len, 1024), F32), _sds((m_len, 1024), F32),
         _sds((1, 1024), F32), _sds((1, 512), F32), _sds((1, 512), F32)],
        [], (dx2, qx, x1, att, rec, kx, vx, wo, wq, w_out, g_cross, g_oa, g_ol), "arbitrary", comm)


def _b_mem(dkx, dvx, mem, mn, g_mem, wk, wv):
    def body(dkx_ref, dvx_ref, mem_ref, mn_ref, g_ref, wk_ref, wv_ref, dwk_ref, dwv_ref, dgm_ref,
             dwkb_ref, dwvb_ref):
        dkb = dkx_ref[...].astype(BF16)
        dvb = dvx_ref[...].astype(BF16)
        dwk = _dot_tn(mn_ref[...], dkb)
        dwv = _dot_tn(mn_ref[...], dvb)
        dwk_ref[...] = dwk
        dwv_ref[...] = dwv
        dwkb_ref[...] = dwk.astype(BF16)
        dwvb_ref[...] = dwv.astype(BF16)
        dmn = _dot_nt(dkb, wk_ref[...]) + _dot_nt(dvb, wv_ref[...])
        mv = mem_ref[...]
        dgm_ref[...] = jnp.sum(dmn * (mv * _rinv(mv)), axis=0, keepdims=True)

    return pl.pallas_call(
        body, name="b_mem",
        out_shape=[_sds((1024, 1024), F32), _sds((1024, 1024), F32), _sds((1, 1024), F32),
                   _sds((1024, 1024), BF16), _sds((1024, 1024), BF16)],
        compiler_params=_cp())(dkx, dvx, mem, mn, g_mem, wk, wv)


def _b_lru(drec, hs, u, xg, conv_w, wrg, brg, wig, big, lam, tl, comm=None):
    s_len = xg.shape[0]
    nt = s_len // tl

    def body(drec_ref, hs_ref, hsp_ref, u_ref, xg_ref, cw_ref, wrg_ref, brg_ref, wig_ref, big_ref, l_ref,
             dxg_ref, dwrg_ref, dwig_ref, dbrg_ref, dbig_ref, dlam_ref, dcw_ref, dcb_ref,
             hbuf, abuf, dubuf, c_sc, d_sc, lam_sc, lcar, wacc_r, wacc_i):
        i = pl.program_id(0)
        tt = nt - 1 - i

        @pl.when(i == 0)
        def _():
            for r in (wacc_r, wacc_i, dbrg_ref, dbig_ref, dlam_ref, dcw_ref, dcb_ref):
                r[...] = jnp.zeros_like(r)
            abuf[tl:tl + 8, :] = jnp.zeros((8, D_LRU), F32)
            dubuf[tl:tl + 8, :] = jnp.zeros((8, D_LRU), F32)
            lcar[...] = jnp.zeros((8, D_LRU), F32)

        xu0 = xg_ref[:, 0:D_LRU]
        hsv = hs_ref[...]
        uv = u_ref[...]
        hbuf[8:8 + tl, :] = hsv
        hbuf[0:8, :] = jnp.where(tt > 0, hsp_ref[...], 0.0)
        hshift = hbuf[pl.ds(7, tl), :]
        wrg_v = wrg_ref[...]
        wig_v = wig_ref[...]
        lamv = l_ref[...]
        ub, r, ig, sp, a, mult = _lru_gates(uv, wrg_v, brg_ref[...], wig_v, big_ref[...], lamv)
        abuf[0:tl, :] = a
        c_sc[...] = abuf[pl.ds(1, tl), :]
        gel, dgel = _gelu_and_grad(xg_ref[:, D_LRU:2 * D_LRU])
        drv = drec_ref[...]
        d_sc[...] = drv * gel
        dxg_ref[:, D_LRU:2 * D_LRU] = (drv * hsv * dgel).astype(BF16)

        def grp(k, lnext):
            off = pl.multiple_of((tl // 8 - 1 - k) * 8, 8)
            l8 = _rscan8(c_sc[pl.ds(off, 8), :], d_sc[pl.ds(off, 8), :], lnext)
            lam_sc[pl.ds(off, 8), :] = l8
            return l8[0:1, :]

        lcar[0:1, :] = lax.fori_loop(0, tl // 8, grp, lcar[0:1, :])
        abuf[tl:tl + 8, :] = a[0:8, :]
        db = lam_sc[...]
        da = db * hshift
        dmult = db * (ig * uv)
        dig = db * mult * uv
        du = db * mult * ig
        dla = da * a - dmult * (a * a) / mult
        dlam_ref[...] += jnp.sum(dla * (-LRU_C) * r, axis=0, keepdims=True)
        dzr = dla * (-LRU_C * sp) * r * (1.0 - r)
        dzi = dig * ig * (1.0 - ig)
        dzrb = dzr.astype(BF16)
        dzib = dzi.astype(BF16)
        du = du + _dot_nt(dzrb, wrg_v) + _dot_nt(dzib, wig_v)
        wacc_r[...] += _dot_tn(ub, dzrb)
        wacc_i[...] += _dot_tn(ub, dzib)
        dbrg_ref[...] += jnp.sum(dzr, axis=0, keepdims=True)
        dbig_ref[...] += jnp.sum(dzi, axis=0, keepdims=True)
        dcb_ref[...] += jnp.sum(du, axis=0, keepdims=True)
        dubuf[0:tl, :] = du
        dxu0 = jnp.zeros((tl, D_LRU), F32)
        for j in range(4):
            dsh = dubuf[pl.ds(3 - j, tl), :]
            dxu0 = dxu0 + cw_ref[j:j + 1, :] * dsh
            dcw_ref[j:j + 1, :] += jnp.sum(xu0 * dsh, axis=0, keepdims=True)
        dubuf[tl:tl + 8, :] = du[0:8, :]
        dxg_ref[:, 0:D_LRU] = dxu0.astype(BF16)

        @pl.when(i == nt - 1)
        def _():
            dlam_ref[...] = dlam_ref[...] * (-jax.nn.sigmoid(-lamv))
            for n in range(LRU_BLOCKS):
                blk = slice(n * LRU_BLOCK, (n + 1) * LRU_BLOCK)
                dwrg_ref[n] = wacc_r[blk, blk]
                dwig_ref[n] = wacc_i[blk, blk]

    def rev(n):
        return pl.BlockSpec((tl, n), lambda i: (nt - 1 - i, 0))

    prev8 = pl.BlockSpec((8, D_LRU), lambda i: (jnp.maximum((nt - 1 - i) * (tl // 8) - 1, 0), 0))
    vec = _full((1, D_LRU))
    sq = _full((D_LRU, D_LRU))
    blocks_shape = (LRU_BLOCKS, LRU_BLOCK, LRU_BLOCK)
    blocks = _full(blocks_shape)
    return _call(
        body, "b_lru", (nt,),
        [rev(D_LRU), rev(D_LRU), prev8, rev(D_LRU), rev(1024), _full((4, D_LRU)), sq, vec, sq, vec, vec],
        [rev(1024), blocks, blocks, vec, vec, vec, _full((4, D_LRU)), vec],
        [_sds((s_len, 1024), BF16), _sds(blocks_shape, F32), _sds(blocks_shape, F32),
         _sds((1, D_LRU), F32), _sds((1, D_LRU), F32), _sds((1, D_LRU), F32),
         _sds((4, D_LRU), F32), _sds((1, D_LRU), F32)],
        [pltpu.VMEM((tl + 8, D_LRU), F32)] * 3 + [pltpu.VMEM((tl, D_LRU), F32)] * 3
        + [pltpu.VMEM((8, D_LRU), F32)] + [pltpu.VMEM((D_LRU, D_LRU), F32)] * 2,
        (drec, hs, hs, u, xg, conv_w, wrg, brg, wig, big, lam), "arbitrary", comm)


def _b_attn(qkv_pad, att, datt, frow, comm=None):
    s_len = datt.shape[0]
    nb = s_len // QB
    n_pair = ATT_HEADS // 2
    pair_w = 2 * HEAD_DIM

    def body(q_ref, k0, k1, k2, v0, v1, v2, o_ref, do_ref, frow_ref, dq_ref, dkv_ref, dfrow_ref,
             bias_sc, dt_sc, acc_sc):
        t = pl.program_id(0)

        @pl.when(t == 0)
        def _():
            _bias_table(frow_ref, bias_sc)
            dt_sc[...] = jnp.zeros_like(dt_sc)
            acc_sc[...] = jnp.zeros_like(acc_sc)

        @pl.when(t < nb)
        def _():
            var = jnp.minimum(t, N_BIAS - 1)
            even = _even_lanes()
            for hp in range(n_pair):
                cs = slice(hp * pair_w, (hp + 1) * pair_w)
                qt = q_ref[:, cs]
                kts = [k0[:, cs], k1[:, cs], k2[:, cs]]
                vts = [v0[:, cs], v1[:, cs], v2[:, cs]]
                dot = do_ref[:, cs]
                dd = dot * o_ref[:, cs]
                qmt, dost, dsbs, pbs, dqs = [], [], [], [], []
                for e in range(2):
                    keep = even if e == 0 else jnp.logical_not(even)
                    qm = jnp.where(keep, qt, 0)
                    p = _att_probs(qm, kts, bias_sc[var, 2 * hp + e])
                    inv = 1.0 / jnp.sum(p, axis=-1, keepdims=True)
                    dos = jnp.where(keep, dot * inv, 0.0)
                    delta = jnp.sum(jnp.where(keep, dd, 0.0), axis=-1, keepdims=True) * inv
                    dp = jnp.concatenate([_dot_nt(dos.astype(BF16), v) for v in vts], axis=1)
                    ds = p * (dp - delta)
                    dt_sc[2 * hp + e] += ds
                    dsb = ds.astype(BF16)
                    dq = _dot(dsb[:, 0:QB], kts[0])
                    for j in (1, 2):
                        dq = dq + _dot(dsb[:, j * QB:(j + 1) * QB], kts[j])
                    dqs.append(dq)
                    dsbs.append(dsb)
                    pbs.append(p.astype(BF16))
                    qmt.append(qm.astype(F32).T.astype(BF16))
                    dost.append(dos.T.astype(BF16))
                dq_ref[:, cs] = (jnp.where(even, dqs[0], dqs[1]) * ATT_SCALE).astype(BF16)
                for j in range(3):
                    slot = (t + 1 + j) % 3
                    js = slice(j * QB, (j + 1) * QB)
                    acc_sc[slot, hp] += _dot(qmt[0], dsbs[0][:, js]) + _dot(qmt[1], dsbs[1][:, js])
                    acc_sc[slot, n_pair + hp] += _dot(dost[0], pbs[0][:, js]) + _dot(dost[1], pbs[1][:, js])

        done = (t + 1) % 3

        @pl.when(t >= 2)
        def _():
            for i in range(2 * n_pair):
                dkv_ref[:, i * pair_w:(i + 1) * pair_w] = acc_sc[done, i].T.astype(BF16)

        acc_sc[done] = jnp.zeros((2 * n_pair, pair_w, QB), F32)

        @pl.when(t == nb + 1)
        def _():
            row = lax.broadcasted_iota(jnp.int32, (8, ROLL_W), 0)
            pad = jnp.zeros((8, ROLL_W - KB), F32)
            for h in range(ATT_HEADS):
                acc8 = jnp.concatenate([dt_sc[h, 0:8, :], pad], axis=1)
                for a1 in range(1, QB // 8):
                    blk = jnp.concatenate([dt_sc[h, 8 * a1:8 * a1 + 8, :], pad], axis=1)
                    acc8 = acc8 + pltpu.roll(blk, ROLL_W - 8 * a1, 1)
                for k in range(3):
                    acc8 = jnp.where(((row >> k) & 1) == 1, pltpu.roll(acc8, ROLL_W - (1 << k), 1), acc8)
                dfrow_ref[h:h + 1, :] = jnp.sum(acc8, axis=0, keepdims=True)

    clamp = lambda t: jnp.minimum(t, nb - 1)
    qrows = pl.BlockSpec((QB, D_ATT), lambda t: (clamp(t), 0))
    return _call(
        body, "b_attn", (nb + 2,),
        _att_in_specs(clamp) + [qrows, qrows, _full((ATT_HEADS, ROLL_W))],
        [qrows, pl.BlockSpec((QB, 2 * D_ATT), lambda t: (jnp.maximum(t - 2, 0), 0)),
         _full((ATT_HEADS, ROLL_W))],
        [_sds((s_len, D_ATT), BF16), _sds((s_len, 2 * D_ATT), BF16), _sds((ATT_HEADS, ROLL_W), F32)],
        [pltpu.VMEM((N_BIAS, ATT_HEADS, QB, KB), F32), pltpu.VMEM((ATT_HEADS, QB, KB), F32),
         pltpu.VMEM((3, 2 * n_pair, pair_w, QB), F32)],
        (*([qkv_pad] * 7), att, datt, frow), "arbitrary", comm)


def _b_win(dq, dkv, dxg, h, ts):
    s_len = h.shape[0]
    steps = s_len // ts

    def body(dq_ref, dkv_ref, dxg_ref, h_ref, dw_ref, dwb_ref):
        @pl.when(pl.program_id(0) == 0)
        def _():
            dw_ref[...] = jnp.zeros_like(dw_ref)

        dproj = jnp.concatenate([dq_ref[...], dkv_ref[...], dxg_ref[...]], axis=1)
        hv = h_ref[...]
        for s in range(N_SHARD):
            dw_ref[s] += _dot_tn(hv, dproj[:, s * IN_SH:(s + 1) * IN_SH])

        @pl.when(pl.program_id(0) == steps - 1)
        def _():
            dwb_ref[...] = dw_ref[...].astype(BF16)

    wspec = _full((N_SHARD, 1024, IN_SH))
    return pl.pallas_call(
        body, name="b_win", grid=(steps,),
        in_specs=[_rows(ts, 512), _rows(ts, 1024), _rows(ts, 1024), _rows(ts, 1024)],
        out_specs=[wspec, wspec],
        out_shape=[_sds((N_SHARD, 1024, IN_SH), F32), _sds((N_SHARD, 1024, IN_SH), BF16)],
        compiler_params=_cp("arbitrary"))(dq, dkv, dxg, h)


def _b_inproj(dq, dkv, dxg, x, dx1, g_mix, w_in_g, tm, comm=None):
    s_len = x.shape[0]

    def body(dq_ref, dkv_ref, dxg_ref, x_ref, dx1_ref, g_ref, w_ref, gx_ref, dgm_ref):
        @pl.when(pl.program_id(0) == 0)
        def _():
            dgm_ref[...] = jnp.zeros_like(dgm_ref)

        dproj = jnp.concatenate([dq_ref[...], dkv_ref[...], dxg_ref[...]], axis=1)
        dh = jnp.zeros((tm, 1024), F32)
        for s in range(N_SHARD):
            dh = dh + _dot_nt(dproj[:, s * IN_SH:(s + 1) * IN_SH], w_ref[s])
        dx, dgm = _rms_bwd(dh, x_ref[...], g_ref[...])
        gx_ref[...] = dx1_ref[...] + dx
        dgm_ref[...] += dgm

    return _call(
        body, "b_inproj", (s_len // tm,),
        [_rows(tm, 512), _rows(tm, 1024), _rows(tm, 1024), _rows(tm, 1024), _rows(tm, 1024),
         _full((1, 1024)), _full((N_SHARD, 1024, IN_SH))],
        [_rows(tm, 1024), _full((1, 1024))],
        [_sds((s_len, 1024), F32), _sds((1, 1024), F32)],
        [], (dq, dkv, dxg, x, dx1, g_mix, w_in_g), "arbitrary", comm)


def _mm_tn(xa, ya, name, ts):
    s_len, k = xa.shape
    n = ya.shape[1]

    steps = s_len // ts

    def body(x_ref, y_ref, o_ref, ob_ref):
        @pl.when(pl.program_id(0) == 0)
        def _():
            o_ref[...] = jnp.zeros_like(o_ref)
        o_ref[...] += _dot_tn(x_ref[...].astype(BF16), y_ref[...].astype(BF16))

        @pl.when(pl.program_id(0) == steps - 1)
        def _():
            ob_ref[...] = o_ref[...].astype(BF16)

    return pl.pallas_call(
        body, name=name, grid=(steps,), in_specs=[_rows(ts, k), _rows(ts, n)],
        out_specs=[_full((k, n))] * 2, out_shape=[_sds((k, n), F32), _sds((k, n), BF16)],
        compiler_params=_cp("arbitrary"))(xa, ya)


def _mm_tn_ysh(xa, y4, name, ts):
    s_len, k = xa.shape
    n = y4.shape[2]

    steps = s_len // ts

    def body(x_ref, y_ref, o_ref, ob_ref):
        @pl.when(pl.program_id(0) == 0)
        def _():
            o_ref[...] = jnp.zeros_like(o_ref)
        xb = x_ref[...].astype(BF16)
        for s in range(N_SHARD):
            o_ref[s] += _dot_tn(xb, y_ref[s])

        @pl.when(pl.program_id(0) == steps - 1)
        def _():
            ob_ref[...] = o_ref[...].astype(BF16)

    return pl.pallas_call(
        body, name=name, grid=(steps,), in_specs=[_rows(ts, k), _sh_rows(ts, n)],
        out_specs=[_full((N_SHARD, k, n))] * 2,
        out_shape=[_sds((N_SHARD, k, n), F32), _sds((N_SHARD, k, n), BF16)],
        compiler_params=_cp("arbitrary"))(xa, y4)


def _mm_tn_xsh(x4, ya, name, ts):
    s_len, n = ya.shape
    k = x4.shape[2]

    steps = s_len // ts

    def body(x_ref, y_ref, o_ref, ob_ref):
        @pl.when(pl.program_id(0) == 0)
        def _():
            o_ref[...] = jnp.zeros_like(o_ref)
        yb = y_ref[...].astype(BF16)
        for s in range(N_SHARD):
            o_ref[s] += _dot_tn(x_ref[s], yb)

        @pl.when(pl.program_id(0) == steps - 1)
        def _():
            ob_ref[...] = o_ref[...].astype(BF16)

    return pl.pallas_call(
        body, name=name, grid=(steps,), in_specs=[_sh_rows(ts, k), _rows(ts, n)],
        out_specs=[_full((N_SHARD, k, n))] * 2,
        out_shape=[_sds((N_SHARD, k, n), F32), _sds((N_SHARD, k, n), BF16)],
        compiler_params=_cp("arbitrary"))(x4, ya)


def _frow_from_rel_bias(rb):
    hi = jnp.broadcast_to(rb[:, 256:257], (ATT_HEADS, 385))
    mid = rb[:, 1:256][:, ::-1]
    lo = jnp.broadcast_to(rb[:, 0:1], (ATT_HEADS, 128))
    wrap = jnp.broadcast_to(rb[:, 256:257], (ATT_HEADS, ROLL_W - KB))
    return jnp.concatenate([hi, mid, lo, wrap], axis=1)


def _rel_bias_grad_from_dfrow(df):
    g256 = jnp.sum(df[:, 0:385], axis=1, keepdims=True) + jnp.sum(df[:, KB:ROLL_W], axis=1, keepdims=True)
    mid = df[:, 385:640][:, ::-1]
    g0 = jnp.sum(df[:, 640:KB], axis=1, keepdims=True)
    return jnp.concatenate([g0, mid, g256], axis=1)


def _block_diag(w):
    eye = jnp.eye(8, dtype=w.dtype)
    return (w[:, :, None, :] * eye[:, None, :, None]).reshape(D_LRU, D_LRU)


MID = ['w_out', 'wq_c', 'wk_c', 'wv_c', 'wo_c']
AG_IN_INPROJ = ['w_out', 'wq_c', 'wk_c']
AG_IN_ATTN = ['wv_c', 'wo_c', 'w_gate']
AG_IN_LRU = ['w_up']
AG_IN_MID = ['w_down']
RS_IN_MID = ['w_gate', 'w_up']
RS_IN_LRU = ['w_down']
RS_IN_ATTN = MID


def _local_step(x, mem, tgt, p, gw, shards=None, chip=None):
    s_len = x.shape[0]
    tm = min(256, s_len)
    tmb = min(512, s_len)
    tl = min(512, s_len)
    frow = _frow_from_rel_bias(p['rel_bias'])
    wrg = _block_diag(p['w_rg']).astype(BF16)
    wig = _block_diag(p['w_ig']).astype(BF16)
    gw = dict(gw)

    big, bigb, recv, part, sib = {}, {}, {}, {}, {}

    def ag(names):
        return [] if shards is None else [("ag", [shards[n] for n in names])]

    def rs(names):
        return [] if shards is None else [("rs", [bigb[n] for n in names])]

    def swap(names):
        return [] if shards is None else [("swap", [part[n] for n in names])]

    def reduce_own(names):
        if shards is not None:
            for n in names:
                part[n] = _sum_parts(big[n], recv[n], chip, "sum_" + n)

    h, qkv_pad, xg, *got = _f_inproj(x, p['g_mix'], gw['w_in'], tmb, ag(AG_IN_INPROJ))
    gw.update(zip(AG_IN_INPROJ, got))
    att, *got = _f_attn(qkv_pad, frow, ag(AG_IN_ATTN))
    gw.update(zip(AG_IN_ATTN, got))
    rec, u, hs, *got = _f_lru(xg, p['conv_w'], p['conv_b'], wrg, p['b_rg'], wig, p['b_ig'], p['lru_L'], tl,
                              ag(AG_IN_LRU))
    gw.update(zip(AG_IN_LRU, got))
    w_out = gw['w_out'].reshape(1024, 1024)
    wq = gw['wq_c'].reshape(1024, 1024)
    wk = gw['wk_c'].reshape(1024, 1024)
    wv = gw['wv_c'].reshape(1024, 1024)
    wo = gw['wo_c'].reshape(1024, 1024)
    mn, kx, vx = _f_mem(mem, p['g_mem'], wk, wv)
    mg, x1, hc, qx, ox, x2, *got = _f_mid(x, att, rec, p['g_out_attn'], p['g_out_lru'], w_out, p['g_cross'],
                                          wq, kx, vx, wo, tmb, ag(AG_IN_MID))
    gw.update(zip(AG_IN_MID, got))
    hf, gact, uact, aact, dx3, loss, dg_final = _f_ffn(x2, tgt, p['g_ffn'], p['g_final'],
                                                       gw['w_gate'], gw['w_up'], gw['w_down'], tm)

    ts = min(512, s_len)
    dgact, duact, dx2, dg_ffn = _b_ffn(dx3, x2, gact, uact, p['g_ffn'], gw['w_gate'], gw['w_up'], gw['w_down'], tm)
    big['w_gate'], bigb['w_gate'] = _mm_tn_ysh(hf, dgact, "dw_gate", ts)
    big['w_up'], bigb['w_up'] = _mm_tn_ysh(hf, duact, "dw_up", ts)
    big['w_down'], bigb['w_down'] = _mm_tn_xsh(aact, dx3, "dw_down", ts)

    dqx, dx1, datt, drec, dkx, dvx, dg_cross, dg_oa, dg_ol, *got = _b_mid(
        dx2, qx, x1, att, rec, kx, vx, wo, wq, w_out, p['g_cross'], p['g_out_attn'], p['g_out_lru'], tm,
        rs(RS_IN_MID))
    recv.update(zip(RS_IN_MID, got))
    reduce_own(RS_IN_MID)
    dwk, dwv, dg_mem, dwkb, dwvb = _b_mem(dkx, dvx, mem, mn, p['g_mem'], wk, wv)
    big['wk_c'], bigb['wk_c'] = dwk, dwkb
    big['wv_c'], bigb['wv_c'] = dwv, dwvb
    big['w_out'], bigb['w_out'] = _mm_tn(mg, dx1, "dw_out", ts)
    big['wq_c'], bigb['wq_c'] = _mm_tn(hc, dqx, "dw_q", ts)
    big['wo_c'], bigb['wo_c'] = _mm_tn(ox, dx2, "dw_o", ts)
    for n in MID:
        big[n] = big[n].reshape(N_SHARD, 256, 1024)
        bigb[n] = bigb[n].reshape(N_SHARD, 256, 1024)

    dxg, dwrg, dwig, dbrg, dbig, dlam, dcw, dcb, *got = _b_lru(
        drec, hs, u, xg, p['conv_w'], wrg, p['b_rg'], wig, p['b_ig'], p['lru_L'], tl,
        rs(RS_IN_LRU) + swap(RS_IN_MID))
    recv.update(zip(RS_IN_LRU, got))
    sib.update(zip(RS_IN_MID, got[len(RS_IN_LRU):]))
    reduce_own(RS_IN_LRU)
    dq, dkv, dfrow, *got = _b_attn(qkv_pad, att, datt, frow, rs(RS_IN_ATTN) + swap(RS_IN_LRU))
    recv.update(zip(RS_IN_ATTN, got))
    sib.update(zip(RS_IN_LRU, got[len(RS_IN_ATTN):]))
    reduce_own(RS_IN_ATTN)
    big['w_in'], bigb['w_in'] = _b_win(dq, dkv, dxg, h, ts)
    grad_x, dg_mix, *got = _b_inproj(dq, dkv, dxg, x, dx1, p['g_mix'], gw['w_in'], tmb,
                                     rs(['w_in']) + swap(RS_IN_ATTN))
    recv.update(zip(['w_in'], got))
    sib.update(zip(RS_IN_ATTN, got[1:]))
    reduce_own(['w_in'])
    small = {
        'g_mix': dg_mix, 'rel_bias': _rel_bias_grad_from_dfrow(dfrow), 'conv_w': dcw, 'conv_b': dcb,
        'w_rg': dwrg, 'b_rg': dbrg, 'w_ig': dwig, 'b_ig': dbig,
        'lru_L': dlam,
        'g_out_attn': dg_oa, 'g_out_lru': dg_ol, 'g_cross': dg_cross, 'g_mem': dg_mem, 'g_ffn': dg_ffn,
        'g_final': dg_final,
    }
    return loss, grad_x, small, big, part, sib


def _cast_shards(ws):
    def body(*refs):
        n = len(refs) // 2
        for src, dst in zip(refs[:n], refs[n:]):
            dst[...] = src[...].astype(BF16)

    return pl.pallas_call(body, name="cast_shards", out_shape=[_sds(w.shape, BF16) for w in ws],
                          compiler_params=_cp())(*ws)


def _sum_parts(own4, recv3, chip, name):
    _, r, c = own4.shape
    tr = r // 4

    def body(chip_ref, own_ref, rc_ref, o_ref):
        o_ref[...] = ((own_ref[0] + rc_ref[0].astype(F32)) + rc_ref[1].astype(F32)) + rc_ref[2].astype(F32)

    grid_spec = pltpu.PrefetchScalarGridSpec(
        num_scalar_prefetch=1, grid=(4,),
        in_specs=[pl.BlockSpec((1, tr, c), lambda i, ch: (ch[0], i, 0)),
                  pl.BlockSpec((3, tr, c), lambda i, ch: (0, i, 0))],
        out_specs=pl.BlockSpec((tr, c), lambda i, ch: (i, 0)))
    return pl.pallas_call(body, name=name, grid_spec=grid_spec, out_shape=_sds((r, c), F32),
                          compiler_params=_cp("parallel"))(chip, own4, recv3)


def _adamw_math(w, g, m, v):
    m = ADAM_B1 * m + (1.0 - ADAM_B1) * g
    v = ADAM_B2 * v + (1.0 - ADAM_B2) * (g * g)
    m_hat = m / (1.0 - ADAM_B1 ** ADAM_STEP)
    v_hat = v / (1.0 - ADAM_B2 ** ADAM_STEP)
    delta = -ADAM_LR * (m_hat / (jnp.sqrt(v_hat) + ADAM_EPS) + ADAM_WD * w)
    return delta, m, v


def _final_adamw(pa, pb, w, m, v, name):
    r, c = w.shape
    tr = r // 4

    def body(pa_ref, pb_ref, w_ref, m_ref, v_ref, g_ref, d_ref, nm_ref, nv_ref):
        g = pa_ref[...] + pb_ref[...]
        g_ref[...] = g
        d_ref[...], nm_ref[...], nv_ref[...] = _adamw_math(w_ref[...], g, m_ref[...], v_ref[...])

    return pl.pallas_call(
        body, name=name, grid=(4,), in_specs=[_rows(tr, c)] * 5, out_specs=[_rows(tr, c)] * 4,
        out_shape=[_sds((r, c), F32)] * 4, compiler_params=_cp("parallel"))(pa, pb, w, m, v)


def _ar_small(g, loss, w, m, v):
    n = len(g)
    rows = _pack_rows()

    def put(ref, name, val_ref):
        r = rows[name]
        shape = val_ref.shape
        if len(shape) == 3:
            for b in range(shape[0]):
                ref[r:r + shape[1], b * shape[2]:(b + 1) * shape[2]] = val_ref[b]
        elif shape[1] == 2 * PACK_W:
            ref[r:r + 1, :] = val_ref[:, 0:PACK_W]
            ref[r + 1:r + 2, :] = val_ref[:, PACK_W:2 * PACK_W]
        else:
            ref[r:r + shape[0], 0:shape[1]] = val_ref[...]

    def get(ref, name, shape):
        r = rows[name]
        if len(shape) == 3:
            return jnp.stack([ref[r:r + shape[1], b * shape[2]:(b + 1) * shape[2]] for b in range(shape[0])])
        if shape[1] == 2 * PACK_W:
            return jnp.concatenate([ref[r:r + 1, :], ref[r + 1:r + 2, :]], axis=1)
        return ref[r:r + shape[0], 0:shape[1]]

    def body(*refs):
        g_refs, loss_ref = refs[:n], refs[n]
        w_refs, m_refs, v_refs = (refs[n + 1 + i * n:n + 1 + (i + 1) * n] for i in range(3))
        o0 = 4 * n + 1
        go, do, mo, vo = (refs[o0 + i * n:o0 + (i + 1) * n] for i in range(4))
        loss_out, pack, buf, tot_sc, send_sems, recv_sems = refs[o0 + 4 * n:]
        x, y, c = _mesh_pos()
        me = 4 * x + 2 * y + c

        def peer(k):
            px = 1 - x if k & 4 else x
            py = 1 - y if k & 2 else y
            pc = 1 - c if k & 1 else c
            return px, py, pc

        def remote(k, slot):
            return pltpu.make_async_remote_copy(
                src_ref=pack, dst_ref=buf.at[slot], send_sem=send_sems.at[k - 1], recv_sem=recv_sems.at[k - 1],
                device_id=peer(k), device_id_type=MESH_ID)

        pack[...] = jnp.zeros_like(pack)
        for a, name in enumerate(SMALL):
            put(pack, name, g_refs[a])
        put(pack, 'loss', loss_ref)
        for k in range(1, 8):
            remote(k, me).start()
        buf[me] = pack[...]
        for k in range(1, 8):
            px, py, pc = peer(k)
            remote(k, 4 * px + 2 * py + pc).wait_recv()
        for k in range(1, 8):
            remote(k, me).wait_send()
        tot = buf[0]
        for k in range(1, 8):
            tot = tot + buf[k]
        tot_sc[...] = tot
        loss_out[...] = get(tot_sc, 'loss', loss_ref.shape)
        for a, name in enumerate(SMALL):
            if name == 'conv_w':
                r = rows[name]
                ga = tot_sc[r:r + g_refs[a].shape[0], pl.ds(pl.multiple_of((2 * x + y) * 128, 128), 128)]
            else:
                ga = get(tot_sc, name, g_refs[a].shape)
            go[a][...] = ga
            do[a][...], mo[a][...], vo[a][...] = _adamw_math(w_refs[a][...], ga, m_refs[a][...], v_refs[a][...])

    out_shape = [_sds(a.shape, F32) for a in w] * 4 + [_sds(loss.shape, F32)]
    return pl.pallas_call(
        body, name="ar_small", out_shape=out_shape,
        scratch_shapes=[pltpu.VMEM((PACK_ROWS, PACK_W), F32), pltpu.VMEM((8, PACK_ROWS, PACK_W), F32),
                        pltpu.VMEM((PACK_ROWS, PACK_W), F32), pltpu.SemaphoreType.DMA((7,)),
                        pltpu.SemaphoreType.DMA((7,))],
        compiler_params=_cp())(*g, loss, *w, *m, *v)


PACK_W = 512
PACK_ROWS = 160


def _pack_rows():
    rows, r = {}, 0
    for name in ['g_mix', 'g_cross', 'g_mem', 'g_ffn', 'g_final']:
        rows[name] = r
        r += 2
    for name in ['conv_b', 'b_rg', 'b_ig', 'lru_L', 'g_out_attn', 'g_out_lru']:
        rows[name] = r
        r += 1
    rows['conv_w'] = r
    rows['loss'] = r + 4
    rows['rel_bias'] = 24
    rows['w_rg'] = 32
    rows['w_ig'] = 32 + LRU_BLOCK
    assert r + 5 <= 24 and rows['w_ig'] + LRU_BLOCK == PACK_ROWS
    return rows


INPUT_NAMES = (['x', 'mem'] + WEIGHTS + ['loss_target'] + ['m_' + n for n in WEIGHTS] + ['v_' + n for n in WEIGHTS])


def kernel(x, mem, g_mix, w_in, rel_bias, conv_w, conv_b, w_rg, b_rg, w_ig, b_ig, lru_L, g_out_attn, g_out_lru, w_out, g_cross, g_mem, wq_c, wk_c, wv_c, wo_c, g_ffn, w_gate, w_up, w_down, g_final, loss_target, m_g_mix, m_w_in, m_rel_bias, m_conv_w, m_conv_b, m_w_rg, m_b_rg, m_w_ig, m_b_ig, m_lru_L, m_g_out_attn, m_g_out_lru, m_w_out, m_g_cross, m_g_mem, m_wq_c, m_wk_c, m_wv_c, m_wo_c, m_g_ffn, m_w_gate, m_w_up, m_w_down, m_g_final, v_g_mix, v_w_in, v_rel_bias, v_conv_w, v_conv_b, v_w_rg, v_b_rg, v_w_ig, v_b_ig, v_lru_L, v_g_out_attn, v_g_out_lru, v_w_out, v_g_cross, v_g_mem, v_wq_c, v_wk_c, v_wv_c, v_wo_c, v_g_ffn, v_w_gate, v_w_up, v_w_down, v_g_final):
    a = dict(zip(INPUT_NAMES, (x, mem, g_mix, w_in, rel_bias, conv_w, conv_b, w_rg, b_rg, w_ig, b_ig, lru_L, g_out_attn, g_out_lru, w_out, g_cross, g_mem, wq_c, wk_c, wv_c, wo_c, g_ffn, w_gate, w_up, w_down, g_final, loss_target, m_g_mix, m_w_in, m_rel_bias, m_conv_w, m_conv_b, m_w_rg, m_b_rg, m_w_ig, m_b_ig, m_lru_L, m_g_out_attn, m_g_out_lru, m_w_out, m_g_cross, m_g_mem, m_wq_c, m_wk_c, m_wv_c, m_wo_c, m_g_ffn, m_w_gate, m_w_up, m_w_down, m_g_final, v_g_mix, v_w_in, v_rel_bias, v_conv_w, v_conv_b, v_w_rg, v_b_rg, v_w_ig, v_b_ig, v_lru_L, v_g_out_attn, v_g_out_lru, v_w_out, v_g_cross, v_g_mem, v_wq_c, v_wk_c, v_wv_c, v_wo_c, v_g_ffn, v_w_gate, v_w_up, v_w_down, v_g_final)))
    chip = 2 * lax.axis_index("x") + lax.axis_index("y")

    shards = dict(zip(BIG, _cast_shards([a[n][0] for n in BIG])))
    w_in_g, conv_w_g = _comm_only("ag_w_in", [("ag", [shards['w_in']]), ("agf", [a['conv_w'][0]])])
    conv_w_full = conv_w_g.transpose(1, 0, 2).reshape(4, D_LRU)

    p = {n: a[n] for n in SMALL}
    p['rel_bias'] = a['rel_bias'][0]
    p['w_rg'] = a['w_rg'][0]
    p['w_ig'] = a['w_ig'][0]
    p['conv_w'] = conv_w_full
    p['g_final'] = a['g_final'][None, :]
    chip_arr = jnp.reshape(chip, (1,)).astype(jnp.int32)
    loss_part, grad_x, small, _, part, sib = _local_step(
        a['x'][0], a['mem'][0], a['loss_target'][0], p, {'w_in': w_in_g}, shards, chip_arr)

    sib['w_in'], = _comm_only("swap_w_in", [("swap", [part['w_in']])])
    out = {}
    for n in BIG:
        out[n] = _final_adamw(part[n], sib[n], a[n][0], a['m_' + n][0], a['v_' + n][0], "adamw_" + n)

    def natural(arr):
        return arr[0] if arr.ndim >= 3 else (arr[None, :] if arr.ndim == 1 else arr)

    small_out = _ar_small([small[n] for n in SMALL], loss_part,
                          *[[natural(a[pre + n]) for n in SMALL] for pre in ('', 'm_', 'v_')])
    ns = len(SMALL)
    loss = small_out[4 * ns][0, 0]

    def leaf(i, n):
        if n in BIG:
            return out[n][i][None]
        return small_out[i * ns + SMALL.index(n)].reshape(a[n].shape)

    return (loss, grad_x[None], *[leaf(i, n) for i in range(4) for n in WEIGHTS])
```

```python
import math

import jax
import jax.numpy as jnp
from jax import lax
from jax.experimental import pallas as pl
from jax.experimental.pallas import tpu as pltpu

F32 = jnp.float32
BF16 = jnp.bfloat16

D_MODEL = 1024
D_ATT = 512
D_LRU = 512
HEAD_DIM = 64
ATT_HEADS = 8
CHUNK = 64
LEFT_CHUNKS = 8
X_HEADS = 4
X_HEAD_DIM = 256
N_SHARD = 4
IN_SH = 640
FF_SH = 704
EPS = 1e-6
LRU_C = 8.0
LRU_BLOCKS = 8
LRU_BLOCK = 64
QB = 256
KB = 768
ROLL_W = 1024
NEG = -1e30
ATT_SCALE = HEAD_DIM ** -0.5
X_SCALE = X_HEAD_DIM ** -0.5

ADAM_LR = 0.001
ADAM_B1 = 0.9
ADAM_B2 = 0.999
ADAM_EPS = 1e-08
ADAM_WD = 0.01
ADAM_STEP = 10

VMEM_LIMIT_V7X = 56 * 1024 * 1024
MESH_ID = pl.DeviceIdType.MESH

WEIGHTS = ['g_mix', 'w_in', 'rel_bias', 'conv_w', 'conv_b', 'w_rg', 'b_rg', 'w_ig', 'b_ig', 'lru_L',
           'g_out_attn', 'g_out_lru', 'w_out', 'g_cross', 'g_mem', 'wq_c', 'wk_c', 'wv_c', 'wo_c',
           'g_ffn', 'w_gate', 'w_up', 'w_down', 'g_final']
BIG = ['w_in', 'w_out', 'wq_c', 'wk_c', 'wv_c', 'wo_c', 'w_gate', 'w_up', 'w_down']
SMALL = [n for n in WEIGHTS if n not in BIG]


def _sds(shape, dtype):
    return jax.ShapeDtypeStruct(shape, dtype)


def _cp(*sem):
    return pltpu.CompilerParams(dimension_semantics=sem or None, vmem_limit_bytes=VMEM_LIMIT_V7X)


def _rows(tm, n):
    return pl.BlockSpec((tm, n), lambda i: (i, 0))


def _full(shape):
    nd = len(shape)
    return pl.BlockSpec(shape, lambda i: (0,) * nd)


def _dot(a, b):
    return jnp.dot(a, b, preferred_element_type=F32)


def _dot_nt(a, b):
    return lax.dot_general(a, b, (((1,), (1,)), ((), ())), preferred_element_type=F32)


def _dot_tn(a, b):
    return lax.dot_general(a, b, (((0,), (0,)), ((), ())), preferred_element_type=F32)


def _rinv(x):
    return lax.rsqrt(jnp.mean(x * x, axis=-1, keepdims=True) + EPS)


def _rms_bwd(dy, x, g):
    r = _rinv(x)
    yh = x * r
    dyh = dy * g
    dx = r * (dyh - yh * jnp.mean(dyh * yh, axis=-1, keepdims=True))
    return dx, jnp.sum(dy * yh, axis=0, keepdims=True)


def _gelu(x):
    c = math.sqrt(2.0 / math.pi)
    t = jnp.tanh(c * (x + 0.044715 * x * x * x))
    return 0.5 * x * (1.0 + t)


def _gelu_and_grad(x):
    c = math.sqrt(2.0 / math.pi)
    t = jnp.tanh(c * (x + 0.044715 * x * x * x))
    g = 0.5 * x * (1.0 + t)
    dg = 0.5 * (1.0 + t) + 0.5 * x * (1.0 - t * t) * c * (1.0 + 3.0 * 0.044715 * x * x)
    return g, dg


def _neg_expm1(z):
    series = -z * (1 + z / 2 * (1 + z / 3 * (1 + z / 4)))
    return jnp.where(z > -0.03, series, 1.0 - jnp.exp(z))


def _lru_gates(u, wrg, brg, wig, big, lam):
    ub = u.astype(BF16)
    r = jax.nn.sigmoid(_dot(ub, wrg) + brg)
    ig = jax.nn.sigmoid(_dot(ub, wig) + big)
    sp = jnp.maximum(-lam, 0.0) + jnp.log1p(jnp.exp(-jnp.abs(lam)))
    la = -LRU_C * r * sp
    a = jnp.exp(la)
    mult = jnp.sqrt(jnp.maximum(_neg_expm1(2.0 * la), 0.0))
    return ub, r, ig, sp, a, mult


def _scan8(a8, b8, hprev):
    row = lax.broadcasted_iota(jnp.int32, a8.shape, 0)
    aa, bb = a8, b8
    for d in (1, 2, 4):
        a_s = pltpu.roll(aa, d, 0)
        b_s = pltpu.roll(bb, d, 0)
        m = row >= d
        bb = jnp.where(m, aa * b_s + bb, bb)
        aa = jnp.where(m, aa * a_s, aa)
    return aa * hprev + bb


def _mesh_pos():
    return lax.axis_index("x"), lax.axis_index("y"), lax.axis_index("c")


def _other_chips(x, y):
    return [(1 - x, y), (x, 1 - y), (1 - x, 1 - y)]


def _no_forward():
    pass


def _ag_full_copies(ins, outs, sems):
    send_sems, recv_sems, loc_sems = sems
    n = len(ins)
    x, y, c = _mesh_pos()
    mine = 2 * x + y
    chips = _other_chips(x, y)

    def remote(k, j, slot):
        px, py = chips[j]
        return pltpu.make_async_remote_copy(
            src_ref=ins[k], dst_ref=outs[k].at[slot], send_sem=send_sems.at[k, j], recv_sem=recv_sems.at[k, j],
            device_id=(px, py, c), device_id_type=MESH_ID)

    def local(k):
        return pltpu.make_async_copy(ins[k], outs[k].at[mine], loc_sems.at[k])

    def start():
        for k in range(n):
            local(k).start()
            for j in range(3):
                remote(k, j, mine).start()

    def wait():
        for k in range(n):
            for j, (px, py) in enumerate(chips):
                remote(k, j, 2 * px + py).wait_recv()
        for k in range(n):
            for j in range(3):
                remote(k, j, mine).wait_send()
            local(k).wait()

    return start, _no_forward, wait


def _ag_copies(ins, outs, sems):
    send_sems, recv_sems, fsend_sems, frecv_sems, loc_sems = sems
    n = len(ins)
    x, y, c = _mesh_pos()
    mine = 2 * x + y
    chips = _other_chips(x, y)

    def half(ref, hc):
        r = ref.shape[0] // 2
        return ref.at[pl.ds(pl.multiple_of(hc * r, 16), r)]

    def ici(k, j, slot):
        px, py = chips[j]
        return pltpu.make_async_remote_copy(
            src_ref=half(ins[k], c), dst_ref=half(outs[k].at[slot], c),
            send_sem=send_sems.at[k, j], recv_sem=recv_sems.at[k, j],
            device_id=(px, py, c), device_id_type=MESH_ID)

    def d2d(k, j, hc):
        px, py = chips[j]
        part = half(outs[k].at[2 * px + py], hc)
        return pltpu.make_async_remote_copy(
            src_ref=part, dst_ref=part, send_sem=fsend_sems.at[k, j], recv_sem=frecv_sems.at[k, j],
            device_id=(x, y, 1 - c), device_id_type=MESH_ID)

    def local(k):
        return pltpu.make_async_copy(ins[k], outs[k].at[mine], loc_sems.at[k])

    def start():
        for k in range(n):
            local(k).start()
            for j in range(3):
                ici(k, j, mine).start()

    def forward():
        for k in range(n):
            for j, (px, py) in enumerate(chips):
                ici(k, j, 2 * px + py).wait_recv()
                d2d(k, j, c).start()

    def wait():
        for k in range(n):
            for j in range(3):
                d2d(k, j, 1 - c).wait_recv()
        for k in range(n):
            for j in range(3):
                d2d(k, j, c).wait_send()
                ici(k, j, mine).wait_send()
            local(k).wait()

    return start, forward, wait


def _rs_copies(ins, outs, sems):
    send_sems, recv_sems = sems
    n = len(ins)
    x, y, c = _mesh_pos()
    chips = _other_chips(x, y)

    def remote(k, j):
        px, py = chips[j]
        return pltpu.make_async_remote_copy(
            src_ref=ins[k].at[2 * px + py], dst_ref=outs[k].at[j],
            send_sem=send_sems.at[k, j], recv_sem=recv_sems.at[k, j],
            device_id=(px, py, c), device_id_type=MESH_ID)

    def start():
        for k in range(n):
            for j in range(3):
                remote(k, j).start()

    def wait():
        for k in range(n):
            for j in range(3):
                remote(k, j).wait_recv()
        for k in range(n):
            for j in range(3):
                remote(k, j).wait_send()

    return start, _no_forward, wait


def _swap_copies(ins, outs, sems):
    send_sems, recv_sems = sems
    x, y, c = _mesh_pos()
    copies = [pltpu.make_async_remote_copy(
        src_ref=ins[k], dst_ref=outs[k], send_sem=send_sems.at[k], recv_sem=recv_sems.at[k],
        device_id=(x, y, 1 - c), device_id_type=MESH_ID) for k in range(len(ins))]

    def start():
        for cp in copies:
            cp.start()

    def wait():
        for cp in copies:
            cp.wait()

    return start, _no_forward, wait


def _comm_plan(groups):
    plan, arrs, shapes, sems = [], [], [], []
    for kind, group in groups:
        k = len(group)
        arrs += group
        per_peer = pltpu.SemaphoreType.DMA((k, 3))
        if kind == "ag":
            shapes += [_sds((N_SHARD,) + w.shape, w.dtype) for w in group]
            gsems = [per_peer] * 4 + [pltpu.SemaphoreType.DMA((k,))]
            maker = _ag_copies
        elif kind == "agf":
            shapes += [_sds((N_SHARD,) + w.shape, w.dtype) for w in group]
            gsems = [per_peer] * 2 + [pltpu.SemaphoreType.DMA((k,))]
            maker = _ag_full_copies
        elif kind == "rs":
            shapes += [_sds((3,) + g.shape[1:], g.dtype) for g in group]
            gsems = [pltpu.SemaphoreType.DMA((k, 3)), pltpu.SemaphoreType.DMA((k, 3))]
            maker = _rs_copies
        else:
            shapes += [_sds(g.shape, g.dtype) for g in group]
            gsems = [pltpu.SemaphoreType.DMA((k,)), pltpu.SemaphoreType.DMA((k,))]
            maker = _swap_copies
        plan.append((maker, k, len(gsems)))
        sems += gsems
    return plan, arrs, shapes, sems


def _comm_fns(plan, cins, couts, sems):
    fns, a, s = [], 0, 0
    for maker, k, ns in plan:
        fns.append(maker(cins[a:a + k], couts[a:a + k], sems[s:s + ns]))
        a += k
        s += ns

    def start():
        for st, _, _ in fns:
            st()

    def forward():
        for _, fw, _ in fns:
            fw()

    def wait():
        for _, _, wt in fns:
            wt()

    return start, forward, wait


def _call(body, name, grid, in_specs, out_specs, out_shape, scratch, args, sem, comm=None):
    if not comm:
        return pl.pallas_call(body, name=name, grid=grid, in_specs=in_specs, out_specs=out_specs,
                              out_shape=out_shape, scratch_shapes=scratch, compiler_params=_cp(sem))(*args)
    plan, c_arrs, c_shapes, c_sems = _comm_plan(comm)
    k = len(c_arrs)
    n_in, n_out, n_scr = len(in_specs), len(out_specs), len(scratch)
    last = grid[0] - 1
    fwd_step = max(1, (2 * last) // 3)

    def wrapped(*refs):
        ins, cins = refs[:n_in], refs[n_in:n_in + k]
        o0 = n_in + k
        outs, couts = refs[o0:o0 + n_out], refs[o0 + n_out:o0 + n_out + k]
        s0 = o0 + n_out + k
        start, forward, wait = _comm_fns(plan, cins, couts, refs[s0 + n_scr:])
        pl.when(pl.program_id(0) == 0)(start)
        pl.when(pl.program_id(0) == fwd_step)(forward)
        body(*ins, *outs, *refs[s0:s0 + n_scr])
        pl.when(pl.program_id(0) == last)(wait)

    return pl.pallas_call(
        wrapped, name=name, grid=grid, in_specs=list(in_specs) + [_any()] * k,
        out_specs=list(out_specs) + [_any()] * k, out_shape=list(out_shape) + c_shapes,
        scratch_shapes=list(scratch) + c_sems, compiler_params=_cp(sem))(*args, *c_arrs)


def _comm_only(name, comm):
    plan, c_arrs, c_shapes, c_sems = _comm_plan(comm)
    k = len(c_arrs)

    def body(*refs):
        start, forward, wait = _comm_fns(plan, refs[:k], refs[k:2 * k], refs[2 * k:])
        start()
        forward()
        wait()

    return pl.pallas_call(body, name=name, in_specs=[_any()] * k, out_specs=[_any()] * k, out_shape=c_shapes,
                          scratch_shapes=c_sems, compiler_params=_cp())(*c_arrs)


def _any():
    return pl.BlockSpec(memory_space=pl.ANY)


def _rscan8(c8, d8, lnext):
    row = lax.broadcasted_iota(jnp.int32, c8.shape, 0)
    cc, dd = c8, d8
    for d in (1, 2, 4):
        c_s = pltpu.roll(cc, 8 - d, 0)
        d_s = pltpu.roll(dd, 8 - d, 0)
        m = row < 8 - d
        dd = jnp.where(m, cc * d_s + dd, dd)
        cc = jnp.where(m, cc * c_s, cc)
    return cc * lnext + dd


def _f_inproj(x, g_mix, w_in_g, tm, comm=None):
    s_len = x.shape[0]
    pad_rows = LEFT_CHUNKS * CHUNK
    npad = pad_rows // tm

    def body(x_ref, g_ref, w_ref, h_ref, qkv_ref, xg_ref):
        i = pl.program_id(0)

        @pl.when(i < npad)
        def _():
            qkv_ref[...] = jnp.zeros_like(qkv_ref)

        @pl.when(i >= npad)
        def _():
            xv = x_ref[...]
            h = (xv * _rinv(xv) * g_ref[...]).astype(BF16)
            h_ref[...] = h
            p0 = _dot(h, w_ref[0])
            qkv_ref[:, 0:D_ATT] = (p0[:, 0:D_ATT] * ATT_SCALE).astype(BF16)
            qkv_ref[:, D_ATT:640] = p0[:, D_ATT:640].astype(BF16)
            qkv_ref[:, 640:1280] = _dot(h, w_ref[1]).astype(BF16)
            p2 = _dot(h, w_ref[2])
            qkv_ref[:, 1280:1536] = p2[:, 0:256].astype(BF16)
            xg_ref[:, 0:384] = p2[:, 256:640]
            xg_ref[:, 384:1024] = _dot(h, w_ref[3])

    def tok(n):
        return pl.BlockSpec((tm, n), lambda i: (jnp.maximum(i - npad, 0), 0))

    return _call(
        body, "f_inproj", (s_len // tm + npad,),
        [tok(1024), _full((1, 1024)), _full((N_SHARD, 1024, IN_SH))],
        [tok(1024), _rows(tm, 1536), tok(1024)],
        [_sds((s_len, 1024), BF16), _sds((s_len + pad_rows, 1536), BF16), _sds((s_len, 1024), F32)],
        [], (x, g_mix, w_in_g), "arbitrary", comm)


N_BIAS = 3


def _bias_table(frow_ref, bias_sc):
    qa = lax.broadcasted_iota(jnp.int32, (QB, KB), 0) // CHUNK
    kcol = lax.broadcasted_iota(jnp.int32, (QB, KB), 1)
    kb = kcol // CHUNK
    band = jnp.where((kb >= qa) & (kb - qa <= LEFT_CHUNKS), 0.0, NEG).astype(F32)
    for h in range(ATT_HEADS):
        row = jnp.broadcast_to(frow_ref[h:h + 1, :], (QB, ROLL_W))
        toep = pltpu.roll(row, 0, 1, stride=1, stride_axis=0)
        gen = toep[:, 0:KB] + band
        bias_sc[N_BIAS - 1, h] = gen
        for v in range(N_BIAS - 1):
            pad_keys = LEFT_CHUNKS * CHUNK - v * QB
            bias_sc[v, h] = gen + jnp.where(kcol < pad_keys, NEG, 0.0).astype(F32)


def _even_lanes():
    return lax.broadcasted_iota(jnp.int32, (1, 2 * HEAD_DIM), 1) < HEAD_DIM


def _att_probs(qm, kts, bias):
    s = jnp.concatenate([_dot_nt(qm, k) for k in kts], axis=1) + bias
    return jnp.exp(s - jnp.max(s, axis=-1, keepdims=True))


def _att_in_specs(clamp):
    def spec(j, col):
        return pl.BlockSpec((QB, D_ATT), lambda i: (clamp(i) + j, col))
    return [spec(2, 0), spec(0, 1), spec(1, 1), spec(2, 1), spec(0, 2), spec(1, 2), spec(2, 2)]


def _f_attn(qkv_pad, frow, comm=None):
    s_len = qkv_pad.shape[0] - LEFT_CHUNKS * CHUNK
    nb = s_len // QB

    def body(q_ref, k0, k1, k2, v0, v1, v2, frow_ref, o_ref, bias_sc):
        i = pl.program_id(0)

        @pl.when(i == 0)
        def _():
            _bias_table(frow_ref, bias_sc)

        var = jnp.minimum(i, N_BIAS - 1)
        even = _even_lanes()
        for hp in range(ATT_HEADS // 2):
            cs = slice(hp * 2 * HEAD_DIM, (hp + 1) * 2 * HEAD_DIM)
            qt = q_ref[:, cs]
            kts = [k0[:, cs], k1[:, cs], k2[:, cs]]
            vts = [v0[:, cs], v1[:, cs], v2[:, cs]]
            res = []
            for e in range(2):
                keep = even if e == 0 else jnp.logical_not(even)
                pb = _att_probs(jnp.where(keep, qt, 0), kts, bias_sc[var, 2 * hp + e]).astype(BF16)
                r = _dot(pb[:, 0:QB], jnp.where(keep, vts[0], 1))
                for j in (1, 2):
                    r = r + _dot(pb[:, j * QB:(j + 1) * QB], jnp.where(keep, vts[j], 1))
                res.append(r / pltpu.roll(r, HEAD_DIM, 1))
            o_ref[:, cs] = jnp.where(even, res[0], res[1])

    return _call(
        body, "f_attn", (nb,),
        _att_in_specs(lambda i: i) + [_full((ATT_HEADS, ROLL_W))],
        [_rows(QB, D_ATT)], [_sds((s_len, D_ATT), F32)],
        [pltpu.VMEM((N_BIAS, ATT_HEADS, QB, KB), F32)], (*([qkv_pad] * 7), frow), "arbitrary", comm)


def _f_lru(xg, conv_w, conv_b, wrg, brg, wig, big, lam, tl, comm=None):
    s_len = xg.shape[0]

    def body(xg_ref, cw_ref, cb_ref, wrg_ref, brg_ref, wig_ref, big_ref, l_ref,
             rec_ref, u_ref, hs_ref, xbuf, a_sc, b_sc, hcar):
        i = pl.program_id(0)

        @pl.when(i == 0)
        def _():
            xbuf[0:8, :] = jnp.zeros((8, D_LRU), F32)
            hcar[...] = jnp.zeros((8, D_LRU), F32)

        xu0 = xg_ref[:, 0:D_LRU]
        xbuf[8:8 + tl, :] = xu0
        u = cb_ref[...] + cw_ref[0:1, :] * xbuf[pl.ds(5, tl), :]
        for j in range(1, 4):
            u = u + cw_ref[j:j + 1, :] * xbuf[pl.ds(5 + j, tl), :]
        xbuf[0:8, :] = xu0[tl - 8:tl, :]
        u_ref[...] = u
        _, _, ig, _, a, mult = _lru_gates(u, wrg_ref[...], brg_ref[...], wig_ref[...], big_ref[...], l_ref[...])
        a_sc[...] = a
        b_sc[...] = mult * (ig * u)

        def grp(g, hprev):
            off = pl.multiple_of(g * 8, 8)
            h8 = _scan8(a_sc[pl.ds(off, 8), :], b_sc[pl.ds(off, 8), :], hprev)
            hs_ref[pl.ds(off, 8), :] = h8
            return h8[7:8, :]

        hcar[0:1, :] = lax.fori_loop(0, tl // 8, grp, hcar[0:1, :])
        rec_ref[...] = hs_ref[...] * _gelu(xg_ref[:, D_LRU:2 * D_LRU])

    vec = _full((1, D_LRU))
    return _call(
        body, "f_lru", (s_len // tl,),
        [_rows(tl, 1024), _full((4, D_LRU)), vec, _full((D_LRU, D_LRU)), vec, _full((D_LRU, D_LRU)), vec, vec],
        [_rows(tl, D_LRU)] * 3, [_sds((s_len, D_LRU), F32)] * 3,
        [pltpu.VMEM((tl + 8, D_LRU), F32), pltpu.VMEM((tl, D_LRU), F32),
         pltpu.VMEM((tl, D_LRU), F32), pltpu.VMEM((8, D_LRU), F32)],
        (xg, conv_w, conv_b, wrg, brg, wig, big, lam), "arbitrary", comm)


def _f_mem(mem, g_mem, wk, wv):
    def body(mem_ref, g_ref, wk_ref, wv_ref, mn_ref, kx_ref, vx_ref):
        mv = mem_ref[...]
        mn = (mv * _rinv(mv) * g_ref[...]).astype(BF16)
        mn_ref[...] = mn
        kx_ref[...] = _dot(mn, wk_ref[...]).astype(BF16)
        vx_ref[...] = _dot(mn, wv_ref[...]).astype(BF16)

    m = mem.shape[0]
    return pl.pallas_call(
        body, name="f_mem", out_shape=[_sds((m, 1024), BF16)] * 3,
        compiler_params=_cp())(mem, g_mem, wk, wv)


def _xattn_probs(q, k):
    s = _dot_nt(q, k) * X_SCALE
    m = jnp.max(s, axis=-1, keepdims=True)
    p = jnp.exp(s - m)
    return p, jnp.sum(p, axis=-1, keepdims=True)


def _f_mid(x, att, rec, g_oa, g_ol, w_out, g_cross, wq, kx, vx, wo, tm, comm=None):
    s_len = x.shape[0]
    m_len = kx.shape[0]

    def body(x_ref, att_ref, rec_ref, goa_ref, gol_ref, wout_ref, gc_ref, wq_ref, kx_ref, vx_ref, wo_ref,
             mg_ref, x1_ref, hc_ref, qx_ref, ox_ref, x2_ref):
        av = att_ref[...]
        rv = rec_ref[...]
        mg_ref[:, 0:D_ATT] = (av * _rinv(av) * goa_ref[...]).astype(BF16)
        mg_ref[:, D_ATT:1024] = (rv * _rinv(rv) * gol_ref[...]).astype(BF16)
        x1 = x_ref[...] + _dot(mg_ref[...], wout_ref[...])
        x1_ref[...] = x1
        hc = (x1 * _rinv(x1) * gc_ref[...]).astype(BF16)
        hc_ref[...] = hc
        qx_ref[...] = _dot(hc, wq_ref[...]).astype(BF16)
        for h in range(X_HEADS):
            sl = slice(h * X_HEAD_DIM, (h + 1) * X_HEAD_DIM)
            p, l = _xattn_probs(qx_ref[:, sl], kx_ref[:, sl])
            ox_ref[:, sl] = (_dot(p.astype(BF16), vx_ref[:, sl]) / l).astype(BF16)
        x2_ref[...] = x1 + _dot(ox_ref[...], wo_ref[...])

    sq = _full((1024, 1024))
    return _call(
        body, "f_mid", (s_len // tm,),
        [_rows(tm, 1024), _rows(tm, 512), _rows(tm, 512), _full((1, 512)), _full((1, 512)), sq,
         _full((1, 1024)), sq, _full((m_len, 1024)), _full((m_len, 1024)), sq],
        [_rows(tm, 1024)] * 6,
        [_sds((s_len, 1024), BF16), _sds((s_len, 1024), F32), _sds((s_len, 1024), BF16),
         _sds((s_len, 1024), BF16), _sds((s_len, 1024), BF16), _sds((s_len, 1024), F32)],
        [], (x, att, rec, g_oa, g_ol, w_out, g_cross, wq, kx, vx, wo), "arbitrary", comm)


def _load_weights_once(pairs):
    @pl.when(pl.program_id(0) == 0)
    def _():
        for hbm, vmem in pairs:
            pltpu.sync_copy(hbm, vmem)


def _sh_rows(tm, n):
    return pl.BlockSpec((N_SHARD, tm, n), lambda i: (0, i, 0))


def _f_ffn(x2, tgt, g_ffn, g_final, wg, wu, wd, tm):
    s_len = x2.shape[0]

    def body(x2_ref, t_ref, gf_ref, gfin_ref, wg_hbm, wu_hbm, wd_hbm,
             hf_ref, g_ref, u_ref, a_ref, dx3_ref, loss_ref, dgfin_ref, wg_ref, wu_ref, wd_ref):
        _load_weights_once([(wg_hbm, wg_ref), (wu_hbm, wu_ref), (wd_hbm, wd_ref)])

        @pl.when(pl.program_id(0) == 0)
        def _():
            loss_ref[...] = jnp.zeros_like(loss_ref)
            dgfin_ref[...] = jnp.zeros_like(dgfin_ref)

        x2v = x2_ref[...]
        hf = (x2v * _rinv(x2v) * gf_ref[...]).astype(BF16)
        hf_ref[...] = hf
        x3 = x2v
        for s in range(N_SHARD):
            gv = _dot_nt(hf, wg_ref[s])
            uv = _dot_nt(hf, wu_ref[s])
            av = (gv * jax.nn.sigmoid(gv) * uv).astype(BF16)
            g_ref[s] = gv.astype(BF16)
            u_ref[s] = uv.astype(BF16)
            a_ref[s] = av
            x3 = x3 + _dot(av, wd_ref[s])
        r3 = _rinv(x3)
        yh = x3 * r3
        gfin = gfin_ref[...]
        err = yh * gfin - t_ref[...]
        loss_ref[...] += jnp.full((1, 128), 0.5 / D_MODEL, F32) * jnp.sum(err * err)
        dy = err * (1.0 / D_MODEL)
        dgfin_ref[...] += jnp.sum(dy * yh, axis=0, keepdims=True)
        dyh = dy * gfin
        dx3_ref[...] = r3 * (dyh - yh * jnp.mean(dyh * yh, axis=-1, keepdims=True))

    vec = _full((1, 1024))
    return pl.pallas_call(
        body, name="f_ffn", grid=(s_len // tm,),
        in_specs=[_rows(tm, 1024), _rows(tm, 1024), vec, vec, _any(), _any(), _any()],
        out_specs=[_rows(tm, 1024), _sh_rows(tm, FF_SH), _sh_rows(tm, FF_SH), _sh_rows(tm, FF_SH),
                   _rows(tm, 1024), _full((1, 128)), vec],
        out_shape=[_sds((s_len, 1024), BF16)] + [_sds((N_SHARD, s_len, FF_SH), BF16)] * 3
                  + [_sds((s_len, 1024), F32), _sds((1, 128), F32), _sds((1, 1024), F32)],
        scratch_shapes=[pltpu.VMEM((N_SHARD, FF_SH, 1024), BF16)] * 3,
        compiler_params=_cp("arbitrary"))(x2, tgt, g_ffn, g_final, wg, wu, wd)


def _b_ffn(dx3, x2, gact, uact, g_ffn, wg, wu, wd, tm):
    s_len = x2.shape[0]

    def body(dx3_ref, x2_ref, g_ref, u_ref, gf_ref, wg_hbm, wu_hbm, wd_hbm,
             dg_ref, du_ref, dx2_ref, dgf_ref, wg_ref, wu_ref, wd_ref):
        _load_weights_once([(wg_hbm, wg_ref), (wu_hbm, wu_ref), (wd_hbm, wd_ref)])

        @pl.when(pl.program_id(0) == 0)
        def _():
            dgf_ref[...] = jnp.zeros_like(dgf_ref)

        dx3v = dx3_ref[...]
        dx3b = dx3v.astype(BF16)
        dhf = jnp.zeros(dx3v.shape, F32)
        for s in range(N_SHARD):
            da = _dot_nt(dx3b, wd_ref[s])
            gv = g_ref[s].astype(F32)
            uv = u_ref[s].astype(F32)
            sg = jax.nn.sigmoid(gv)
            dub = (da * gv * sg).astype(BF16)
            dgb = (da * uv * (sg * (1.0 + gv * (1.0 - sg)))).astype(BF16)
            du_ref[s] = dub
            dg_ref[s] = dgb
            dhf = dhf + _dot(dgb, wg_ref[s]) + _dot(dub, wu_ref[s])
        dx, dgf = _rms_bwd(dhf, x2_ref[...], gf_ref[...])
        dx2_ref[...] = dx3v + dx
        dgf_ref[...] += dgf

    vec = _full((1, 1024))
    return pl.pallas_call(
        body, name="b_ffn", grid=(s_len // tm,),
        in_specs=[_rows(tm, 1024), _rows(tm, 1024), _sh_rows(tm, FF_SH), _sh_rows(tm, FF_SH), vec,
                  _any(), _any(), _any()],
        out_specs=[_sh_rows(tm, FF_SH), _sh_rows(tm, FF_SH), _rows(tm, 1024), vec],
        out_shape=[_sds((N_SHARD, s_len, FF_SH), BF16)] * 2 + [_sds((s_len, 1024), F32), _sds((1, 1024), F32)],
        scratch_shapes=[pltpu.VMEM((N_SHARD, FF_SH, 1024), BF16)] * 3,
        compiler_params=_cp("arbitrary"))(dx3, x2, gact, uact, g_ffn, wg, wu, wd)


def _b_mid(dx2, qx, x1, att, rec, kx, vx, wo, wq, w_out, g_cross, g_oa, g_ol, tm, comm=None):
    s_len = x1.shape[0]
    m_len = kx.shape[0]

    def body(dx2_ref, qx_ref, x1_ref, att_ref, rec_ref, kx_ref, vx_ref, wo_ref, wq_ref, wout_ref,
             gc_ref, goa_ref, gol_ref,
             dqx_ref, dx1_ref, datt_ref, drec_ref, dkx_ref, dvx_ref, dgc_ref, dgoa_ref, dgol_ref):
        @pl.when(pl.program_id(0) == 0)
        def _():
            for r in (dkx_ref, dvx_ref, dgc_ref, dgoa_ref, dgol_ref):
                r[...] = jnp.zeros_like(r)

        dx2v = dx2_ref[...]
        dox = _dot_nt(dx2v.astype(BF16), wo_ref[...])
        for h in range(X_HEADS):
            sl = slice(h * X_HEAD_DIM, (h + 1) * X_HEAD_DIM)
            q = qx_ref[:, sl]
            p, l = _xattn_probs(q, kx_ref[:, sl])
            pn = p / l
            dob = dox[:, sl].astype(BF16)
            dp = _dot_nt(dob, vx_ref[:, sl])
            dvx_ref[:, sl] += _dot_tn(pn.astype(BF16), dob)
            ds = pn * (dp - jnp.sum(dp * pn, axis=-1, keepdims=True))
            dsb = (ds * X_SCALE).astype(BF16)
            dqx_ref[:, sl] = _dot(dsb, kx_ref[:, sl]).astype(BF16)
            dkx_ref[:, sl] += _dot_tn(dsb, q)
        dhc = _dot_nt(dqx_ref[...], wq_ref[...])
        dx, dgc = _rms_bwd(dhc, x1_ref[...], gc_ref[...])
        dx1 = dx2v + dx
        dx1_ref[...] = dx1
        dgc_ref[...] += dgc
        dmg = _dot_nt(dx1.astype(BF16), wout_ref[...])
        da, dgoa = _rms_bwd(dmg[:, 0:D_ATT], att_ref[...], goa_ref[...])
        datt_ref[...] = da
        dgoa_ref[...] += dgoa
        dr, dgol = _rms_bwd(dmg[:, D_ATT:1024], rec_ref[...], gol_ref[...])
        drec_ref[...] = dr
        dgol_ref[...] += dgol

    sq = _full((1024, 1024))
    mk = _full((m_len, 1024))
    return _call(
        body, "b_mid", (s_len // tm,),
        [_rows(tm, 1024), _rows(tm, 1024), _rows(tm, 1024), _rows(tm, 512), _rows(tm, 512), mk, mk,
         sq, sq, sq, _full((1, 1024)), _full((1, 512)), _full((1, 512))],
        [_rows(tm, 1024), _rows(tm, 1024), _rows(tm, 512), _rows(tm, 512), mk, mk,
         _full((1, 1024)), _full((1, 512)), _full((1, 512))],
        [_sds((s_len, 1024), BF16), _sds((s_len, 1024), F32), _sds((s_len, 512), F32),
         _sds((s_len, 512), F32), _sds((m_len, 1024), F32), _sds((m_len, 1024), F32),
         _sds((1, 1024), F32), _sds((1, 512), F32), _sds((1, 512), F32)],
        [], (dx2, qx, x1, att, rec, kx, vx, wo, wq, w_out, g_cross, g_oa, g_ol), "arbitrary", comm)


def _b_mem(dkx, dvx, mem, mn, g_mem, wk, wv):
    def body(dkx_ref, dvx_ref, mem_ref, mn_ref, g_ref, wk_ref, wv_ref, dwk_ref, dwv_ref, dgm_ref,
             dwkb_ref, dwvb_ref):
        dkb = dkx_ref[...].astype(BF16)
        dvb = dvx_ref[...].astype(BF16)
        dwk = _dot_tn(mn_ref[...], dkb)
        dwv = _dot_tn(mn_ref[...], dvb)
        dwk_ref[...] = dwk
        dwv_ref[...] = dwv
        dwkb_ref[...] = dwk.astype(BF16)
        dwvb_ref[...] = dwv.astype(BF16)
        dmn = _dot_nt(dkb, wk_ref[...]) + _dot_nt(dvb, wv_ref[...])
        mv = mem_ref[...]
        dgm_ref[...] = jnp.sum(dmn * (mv * _rinv(mv)), axis=0, keepdims=True)

    return pl.pallas_call(
        body, name="b_mem",
        out_shape=[_sds((1024, 1024), F32), _sds((1024, 1024), F32), _sds((1, 1024), F32),
                   _sds((1024, 1024), BF16), _sds((1024, 1024), BF16)],
        compiler_params=_cp())(dkx, dvx, mem, mn, g_mem, wk, wv)


def _b_lru(drec, hs, u, xg, conv_w, wrg, brg, wig, big, lam, tl, comm=None):
    s_len = xg.shape[0]
    nt = s_len // tl

    def body(drec_ref, hs_ref, hsp_ref, u_ref, xg_ref, cw_ref, wrg_ref, brg_ref, wig_ref, big_ref, l_ref,
             dxg_ref, dwrg_ref, dwig_ref, dbrg_ref, dbig_ref, dlam_ref, dcw_ref, dcb_ref,
             hbuf, abuf, dubuf, c_sc, d_sc, lam_sc, lcar, wacc_r, wacc_i):
        i = pl.program_id(0)
        tt = nt - 1 - i

        @pl.when(i == 0)
        def _():
            for r in (wacc_r, wacc_i, dbrg_ref, dbig_ref, dlam_ref, dcw_ref, dcb_ref):
                r[...] = jnp.zeros_like(r)
            abuf[tl:tl + 8, :] = jnp.zeros((8, D_LRU), F32)
            dubuf[tl:tl + 8, :] = jnp.zeros((8, D_LRU), F32)
            lcar[...] = jnp.zeros((8, D_LRU), F32)

        xu0 = xg_ref[:, 0:D_LRU]
        hsv = hs_ref[...]
        uv = u_ref[...]
        hbuf[8:8 + tl, :] = hsv
        hbuf[0:8, :] = jnp.where(tt > 0, hsp_ref[...], 0.0)
        hshift = hbuf[pl.ds(7, tl), :]
        wrg_v = wrg_ref[...]
        wig_v = wig_ref[...]
        lamv = l_ref[...]
        ub, r, ig, sp, a, mult = _lru_gates(uv, wrg_v, brg_ref[...], wig_v, big_ref[...], lamv)
        abuf[0:tl, :] = a
        c_sc[...] = abuf[pl.ds(1, tl), :]
        gel, dgel = _gelu_and_grad(xg_ref[:, D_LRU:2 * D_LRU])
        drv = drec_ref[...]
        d_sc[...] = drv * gel
        dxg_ref[:, D_LRU:2 * D_LRU] = (drv * hsv * dgel).astype(BF16)

        def grp(k, lnext):
            off = pl.multiple_of((tl // 8 - 1 - k) * 8, 8)
            l8 = _rscan8(c_sc[pl.ds(off, 8), :], d_sc[pl.ds(off, 8), :], lnext)
            lam_sc[pl.ds(off, 8), :] = l8
            return l8[0:1, :]

        lcar[0:1, :] = lax.fori_loop(0, tl // 8, grp, lcar[0:1, :])
        abuf[tl:tl + 8, :] = a[0:8, :]
        db = lam_sc[...]
        da = db * hshift
        dmult = db * (ig * uv)
        dig = db * mult * uv
        du = db * mult * ig
        dla = da * a - dmult * (a * a) / mult
        dlam_ref[...] += jnp.sum(dla * (-LRU_C) * r, axis=0, keepdims=True)
        dzr = dla * (-LRU_C * sp) * r * (1.0 - r)
        dzi = dig * ig * (1.0 - ig)
        dzrb = dzr.astype(BF16)
        dzib = dzi.astype(BF16)
        du = du + _dot_nt(dzrb, wrg_v) + _dot_nt(dzib, wig_v)
        wacc_r[...] += _dot_tn(ub, dzrb)
        wacc_i[...] += _dot_tn(ub, dzib)
        dbrg_ref[...] += jnp.sum(dzr, axis=0, keepdims=True)
        dbig_ref[...] += jnp.sum(dzi, axis=0, keepdims=True)
        dcb_ref[...] += jnp.sum(du, axis=0, keepdims=True)
        dubuf[0:tl, :] = du
        dxu0 = jnp.zeros((tl, D_LRU), F32)
        for j in range(4):
            dsh = dubuf[pl.ds(3 - j, tl), :]
            dxu0 = dxu0 + cw_ref[j:j + 1, :] * dsh
            dcw_ref[j:j + 1, :] += jnp.sum(xu0 * dsh, axis=0, keepdims=True)
        dubuf[tl:tl + 8, :] = du[0:8, :]
        dxg_ref[:, 0:D_LRU] = dxu0.astype(BF16)

        @pl.when(i == nt - 1)
        def _():
            dlam_ref[...] = dlam_ref[...] * (-jax.nn.sigmoid(-lamv))
            for n in range(LRU_BLOCKS):
                blk = slice(n * LRU_BLOCK, (n + 1) * LRU_BLOCK)
                dwrg_ref[n] = wacc_r[blk, blk]
                dwig_ref[n] = wacc_i[blk, blk]

    def rev(n):
        return pl.BlockSpec((tl, n), lambda i: (nt - 1 - i, 0))

    prev8 = pl.BlockSpec((8, D_LRU), lambda i: (jnp.maximum((nt - 1 - i) * (tl // 8) - 1, 0), 0))
    vec = _full((1, D_LRU))
    sq = _full((D_LRU, D_LRU))
    blocks_shape = (LRU_BLOCKS, LRU_BLOCK, LRU_BLOCK)
    blocks = _full(blocks_shape)
    return _call(
        body, "b_lru", (nt,),
        [rev(D_LRU), rev(D_LRU), prev8, rev(D_LRU), rev(1024), _full((4, D_LRU)), sq, vec, sq, vec, vec],
        [rev(1024), blocks, blocks, vec, vec, vec, _full((4, D_LRU)), vec],
        [_sds((s_len, 1024), BF16), _sds(blocks_shape, F32), _sds(blocks_shape, F32),
         _sds((1, D_LRU), F32), _sds((1, D_LRU), F32), _sds((1, D_LRU), F32),
         _sds((4, D_LRU), F32), _sds((1, D_LRU), F32)],
        [pltpu.VMEM((tl + 8, D_LRU), F32)] * 3 + [pltpu.VMEM((tl, D_LRU), F32)] * 3
        + [pltpu.VMEM((8, D_LRU), F32)] + [pltpu.VMEM((D_LRU, D_LRU), F32)] * 2,
        (drec, hs, hs, u, xg, conv_w, wrg, brg, wig, big, lam), "arbitrary", comm)


def _b_attn(qkv_pad, att, datt, frow, comm=None):
    s_len = datt.shape[0]
    nb = s_len // QB
    n_pair = ATT_HEADS // 2
    pair_w = 2 * HEAD_DIM

    def body(q_ref, k0, k1, k2, v0, v1, v2, o_ref, do_ref, frow_ref, dq_ref, dkv_ref, dfrow_ref,
             bias_sc, dt_sc, acc_sc):
        t = pl.program_id(0)

        @pl.when(t == 0)
        def _():
            _bias_table(frow_ref, bias_sc)
            dt_sc[...] = jnp.zeros_like(dt_sc)
            acc_sc[...] = jnp.zeros_like(acc_sc)

        @pl.when(t < nb)
        def _():
            var = jnp.minimum(t, N_BIAS - 1)
            even = _even_lanes()
            for hp in range(n_pair):
                cs = slice(hp * pair_w, (hp + 1) * pair_w)
                qt = q_ref[:, cs]
                kts = [k0[:, cs], k1[:, cs], k2[:, cs]]
                vts = [v0[:, cs], v1[:, cs], v2[:, cs]]
                dot = do_ref[:, cs]
                dd = dot * o_ref[:, cs]
                qmt, dost, dsbs, pbs, dqs = [], [], [], [], []
                for e in range(2):
                    keep = even if e == 0 else jnp.logical_not(even)
                    qm = jnp.where(keep, qt, 0)
                    p = _att_probs(qm, kts, bias_sc[var, 2 * hp + e])
                    inv = 1.0 / jnp.sum(p, axis=-1, keepdims=True)
                    dos = jnp.where(keep, dot * inv, 0.0)
                    delta = jnp.sum(jnp.where(keep, dd, 0.0), axis=-1, keepdims=True) * inv
                    dp = jnp.concatenate([_dot_nt(dos.astype(BF16), v) for v in vts], axis=1)
                    ds = p * (dp - delta)
                    dt_sc[2 * hp + e] += ds
                    dsb = ds.astype(BF16)
                    dq = _dot(dsb[:, 0:QB], kts[0])
                    for j in (1, 2):
                        dq = dq + _dot(dsb[:, j * QB:(j + 1) * QB], kts[j])
                    dqs.append(dq)
                    dsbs.append(dsb)
                    pbs.append(p.astype(BF16))
                    qmt.append(qm.astype(F32).T.astype(BF16))
                    dost.append(dos.T.astype(BF16))
                dq_ref[:, cs] = (jnp.where(even, dqs[0], dqs[1]) * ATT_SCALE).astype(BF16)
                for j in range(3):
                    slot = (t + 1 + j) % 3
                    js = slice(j * QB, (j + 1) * QB)
                    acc_sc[slot, hp] += _dot(qmt[0], dsbs[0][:, js]) + _dot(qmt[1], dsbs[1][:, js])
                    acc_sc[slot, n_pair + hp] += _dot(dost[0], pbs[0][:, js]) + _dot(dost[1], pbs[1][:, js])

        done = (t + 1) % 3

        @pl.when(t >= 2)
        def _():
            for i in range(2 * n_pair):
                dkv_ref[:, i * pair_w:(i + 1) * pair_w] = acc_sc[done, i].T.astype(BF16)

        acc_sc[done] = jnp.zeros((2 * n_pair, pair_w, QB), F32)

        @pl.when(t == nb + 1)
        def _():
            row = lax.broadcasted_iota(jnp.int32, (8, ROLL_W), 0)
            pad = jnp.zeros((8, ROLL_W - KB), F32)
            for h in range(ATT_HEADS):
                acc8 = jnp.concatenate([dt_sc[h, 0:8, :], pad], axis=1)
                for a1 in range(1, QB // 8):
                    blk = jnp.concatenate([dt_sc[h, 8 * a1:8 * a1 + 8, :], pad], axis=1)
                    acc8 = acc8 + pltpu.roll(blk, ROLL_W - 8 * a1, 1)
                for k in range(3):
                    acc8 = jnp.where(((row >> k) & 1) == 1, pltpu.roll(acc8, ROLL_W - (1 << k), 1), acc8)
                dfrow_ref[h:h + 1, :] = jnp.sum(acc8, axis=0, keepdims=True)

    clamp = lambda t: jnp.minimum(t, nb - 1)
    qrows = pl.BlockSpec((QB, D_ATT), lambda t: (clamp(t), 0))
    return _call(
        body, "b_attn", (nb + 2,),
        _att_in_specs(clamp) + [qrows, qrows, _full((ATT_HEADS, ROLL_W))],
        [qrows, pl.BlockSpec((QB, 2 * D_ATT), lambda t: (jnp.maximum(t - 2, 0), 0)),
         _full((ATT_HEADS, ROLL_W))],
        [_sds((s_len, D_ATT), BF16), _sds((s_len, 2 * D_ATT), BF16), _sds((ATT_HEADS, ROLL_W), F32)],
        [pltpu.VMEM((N_BIAS, ATT_HEADS, QB, KB), F32), pltpu.VMEM((ATT_HEADS, QB, KB), F32),
         pltpu.VMEM((3, 2 * n_pair, pair_w, QB), F32)],
        (*([qkv_pad] * 7), att, datt, frow), "arbitrary", comm)


def _b_win(dq, dkv, dxg, h, ts):
    s_len = h.shape[0]
    steps = s_len // ts

    def body(dq_ref, dkv_ref, dxg_ref, h_ref, dw_ref, dwb_ref):
        @pl.when(pl.program_id(0) == 0)
        def _():
            dw_ref[...] = jnp.zeros_like(dw_ref)

        dproj = jnp.concatenate([dq_ref[...], dkv_ref[...], dxg_ref[...]], axis=1)
        hv = h_ref[...]
        for s in range(N_SHARD):
            dw_ref[s] += _dot_tn(hv, dproj[:, s * IN_SH:(s + 1) * IN_SH])

        @pl.when(pl.program_id(0) == steps - 1)
        def _():
            dwb_ref[...] = dw_ref[...].astype(BF16)

    wspec = _full((N_SHARD, 1024, IN_SH))
    return pl.pallas_call(
        body, name="b_win", grid=(steps,),
        in_specs=[_rows(ts, 512), _rows(ts, 1024), _rows(ts, 1024), _rows(ts, 1024)],
        out_specs=[wspec, wspec],
        out_shape=[_sds((N_SHARD, 1024, IN_SH), F32), _sds((N_SHARD, 1024, IN_SH), BF16)],
        compiler_params=_cp("arbitrary"))(dq, dkv, dxg, h)


def _b_inproj(dq, dkv, dxg, x, dx1, g_mix, w_in_g, tm, comm=None):
    s_len = x.shape[0]

    def body(dq_ref, dkv_ref, dxg_ref, x_ref, dx1_ref, g_ref, w_ref, gx_ref, dgm_ref):
        @pl.when(pl.program_id(0) == 0)
        def _():
            dgm_ref[...] = jnp.zeros_like(dgm_ref)

        dproj = jnp.concatenate([dq_ref[...], dkv_ref[...], dxg_ref[...]], axis=1)
        dh = jnp.zeros((tm, 1024), F32)
        for s in range(N_SHARD):
            dh = dh + _dot_nt(dproj[:, s * IN_SH:(s + 1) * IN_SH], w_ref[s])
        dx, dgm = _rms_bwd(dh, x_ref[...], g_ref[...])
        gx_ref[...] = dx1_ref[...] + dx
        dgm_ref[...] += dgm

    return _call(
        body, "b_inproj", (s_len // tm,),
        [_rows(tm, 512), _rows(tm, 1024), _rows(tm, 1024), _rows(tm, 1024), _rows(tm, 1024),
         _full((1, 1024)), _full((N_SHARD, 1024, IN_SH))],
        [_rows(tm, 1024), _full((1, 1024))],
        [_sds((s_len, 1024), F32), _sds((1, 1024), F32)],
        [], (dq, dkv, dxg, x, dx1, g_mix, w_in_g), "arbitrary", comm)


def _mm_tn(xa, ya, name, ts):
    s_len, k = xa.shape
    n = ya.shape[1]

    steps = s_len // ts

    def body(x_ref, y_ref, o_ref, ob_ref):
        @pl.when(pl.program_id(0) == 0)
        def _():
            o_ref[...] = jnp.zeros_like(o_ref)
        o_ref[...] += _dot_tn(x_ref[...].astype(BF16), y_ref[...].astype(BF16))

        @pl.when(pl.program_id(0) == steps - 1)
        def _():
            ob_ref[...] = o_ref[...].astype(BF16)

    return pl.pallas_call(
        body, name=name, grid=(steps,), in_specs=[_rows(ts, k), _rows(ts, n)],
        out_specs=[_full((k, n))] * 2, out_shape=[_sds((k, n), F32), _sds((k, n), BF16)],
        compiler_params=_cp("arbitrary"))(xa, ya)


def _mm_tn_ysh(xa, y4, name, ts):
    s_len, k = xa.shape
    n = y4.shape[2]

    steps = s_len // ts

    def body(x_ref, y_ref, o_ref, ob_ref):
        @pl.when(pl.program_id(0) == 0)
        def _():
            o_ref[...] = jnp.zeros_like(o_ref)
        xb = x_ref[...].astype(BF16)
        for s in range(N_SHARD):
            o_ref[s] += _dot_tn(xb, y_ref[s])

        @pl.when(pl.program_id(0) == steps - 1)
        def _():
            ob_ref[...] = o_ref[...].astype(BF16)

    return pl.pallas_call(
        body, name=name, grid=(steps,), in_specs=[_rows(ts, k), _sh_rows(ts, n)],
        out_specs=[_full((N_SHARD, k, n))] * 2,
        out_shape=[_sds((N_SHARD, k, n), F32), _sds((N_SHARD, k, n), BF16)],
        compiler_params=_cp("arbitrary"))(xa, y4)


def _mm_tn_xsh(x4, ya, name, ts):
    s_len, n = ya.shape
    k = x4.shape[2]

    steps = s_len // ts

    def body(x_ref, y_ref, o_ref, ob_ref):
        @pl.when(pl.program_id(0) == 0)
        def _():
            o_ref[...] = jnp.zeros_like(o_ref)
        yb = y_ref[...].astype(BF16)
        for s in range(N_SHARD):
            o_ref[s] += _dot_tn(x_ref[s], yb)

        @pl.when(pl.program_id(0) == steps - 1)
        def _():
            ob_ref[...] = o_ref[...].astype(BF16)

    return pl.pallas_call(
        body, name=name, grid=(steps,), in_specs=[_sh_rows(ts, k), _rows(ts, n)],
        out_specs=[_full((N_SHARD, k, n))] * 2,
        out_shape=[_sds((N_SHARD, k, n), F32), _sds((N_SHARD, k, n), BF16)],
        compiler_params=_cp("arbitrary"))(x4, ya)


def _frow_from_rel_bias(rb):
    hi = jnp.broadcast_to(rb[:, 256:257], (ATT_HEADS, 385))
    mid = rb[:, 1:256][:, ::-1]
    lo = jnp.broadcast_to(rb[:, 0:1], (ATT_HEADS, 128))
    wrap = jnp.broadcast_to(rb[:, 256:257], (ATT_HEADS, ROLL_W - KB))
    return jnp.concatenate([hi, mid, lo, wrap], axis=1)


def _rel_bias_grad_from_dfrow(df):
    g256 = jnp.sum(df[:, 0:385], axis=1, keepdims=True) + jnp.sum(df[:, KB:ROLL_W], axis=1, keepdims=True)
    mid = df[:, 385:640][:, ::-1]
    g0 = jnp.sum(df[:, 640:KB], axis=1, keepdims=True)
    return jnp.concatenate([g0, mid, g256], axis=1)


def _block_diag(w):
    eye = jnp.eye(8, dtype=w.dtype)
    return (w[:, :, None, :] * eye[:, None, :, None]).reshape(D_LRU, D_LRU)


MID = ['w_out', 'wq_c', 'wk_c', 'wv_c', 'wo_c']
TRANSPOSED = ['w_gate', 'w_up']
AG_IN_INPROJ = ['w_out', 'wq_c', 'wk_c']
AG_IN_ATTN = ['wv_c', 'wo_c', 'w_gate']
AG_IN_LRU = ['w_up']
AG_IN_MID = ['w_down']
RS_IN_MID = ['w_gate', 'w_up']
RS_IN_LRU = ['w_down']
RS_IN_ATTN = MID


def _local_step(x, mem, tgt, p, gw, shards=None, chip=None):
    s_len = x.shape[0]
    tm = min(256, s_len)
    tmb = min(512, s_len)
    tl = min(512, s_len)
    frow = _frow_from_rel_bias(p['rel_bias'])
    wrg = _block_diag(p['w_rg']).astype(BF16)
    wig = _block_diag(p['w_ig']).astype(BF16)
    gw = dict(gw)

    big, bigb, recv, part, sib = {}, {}, {}, {}, {}

    def ag(names):
        return [] if shards is None else [("ag", [shards[n] for n in names])]

    def rs(names):
        return [] if shards is None else [("rs", [bigb[n] for n in names])]

    def swap(names):
        return [] if shards is None else [("swap", [part[n] for n in names])]

    def reduce_own(names):
        if shards is not None:
            for n in names:
                part[n] = _sum_parts(big[n], recv[n], chip, "sum_" + n)

    h, qkv_pad, xg, *got = _f_inproj(x, p['g_mix'], gw['w_in'], tmb, ag(AG_IN_INPROJ))
    gw.update(zip(AG_IN_INPROJ, got))
    att, *got = _f_attn(qkv_pad, frow, ag(AG_IN_ATTN))
    gw.update(zip(AG_IN_ATTN, got))
    rec, u, hs, *got = _f_lru(xg, p['conv_w'], p['conv_b'], wrg, p['b_rg'], wig, p['b_ig'], p['lru_L'], tl,
                              ag(AG_IN_LRU))
    gw.update(zip(AG_IN_LRU, got))
    w_out = gw['w_out'].reshape(1024, 1024)
    wq = gw['wq_c'].reshape(1024, 1024)
    wk = gw['wk_c'].reshape(1024, 1024)
    wv = gw['wv_c'].reshape(1024, 1024)
    wo = gw['wo_c'].reshape(1024, 1024)
    mn, kx, vx = _f_mem(mem, p['g_mem'], wk, wv)
    mg, x1, hc, qx, ox, x2, *got = _f_mid(x, att, rec, p['g_out_attn'], p['g_out_lru'], w_out, p['g_cross'],
                                          wq, kx, vx, wo, tmb, ag(AG_IN_MID))
    gw.update(zip(AG_IN_MID, got))
    hf, gact, uact, aact, dx3, loss, dg_final = _f_ffn(x2, tgt, p['g_ffn'], p['g_final'],
                                                       gw['w_gate'], gw['w_up'], gw['w_down'], tm)

    ts = min(512, s_len)
    dgact, duact, dx2, dg_ffn = _b_ffn(dx3, x2, gact, uact, p['g_ffn'], gw['w_gate'], gw['w_up'], gw['w_down'], tm)
    big['w_gate'], bigb['w_gate'] = _mm_tn_xsh(dgact, hf, "dw_gate", ts)
    big['w_up'], bigb['w_up'] = _mm_tn_xsh(duact, hf, "dw_up", ts)
    big['w_down'], bigb['w_down'] = _mm_tn_xsh(aact, dx3, "dw_down", ts)

    dqx, dx1, datt, drec, dkx, dvx, dg_cross, dg_oa, dg_ol, *got = _b_mid(
        dx2, qx, x1, att, rec, kx, vx, wo, wq, w_out, p['g_cross'], p['g_out_attn'], p['g_out_lru'], tm,
        rs(RS_IN_MID))
    recv.update(zip(RS_IN_MID, got))
    reduce_own(RS_IN_MID)
    dwk, dwv, dg_mem, dwkb, dwvb = _b_mem(dkx, dvx, mem, mn, p['g_mem'], wk, wv)
    big['wk_c'], bigb['wk_c'] = dwk, dwkb
    big['wv_c'], bigb['wv_c'] = dwv, dwvb
    big['w_out'], bigb['w_out'] = _mm_tn(mg, dx1, "dw_out", ts)
    big['wq_c'], bigb['wq_c'] = _mm_tn(hc, dqx, "dw_q", ts)
    big['wo_c'], bigb['wo_c'] = _mm_tn(ox, dx2, "dw_o", ts)
    for n in MID:
        big[n] = big[n].reshape(N_SHARD, 256, 1024)
        bigb[n] = bigb[n].reshape(N_SHARD, 256, 1024)

    dxg, dwrg, dwig, dbrg, dbig, dlam, dcw, dcb, *got = _b_lru(
        drec, hs, u, xg, p['conv_w'], wrg, p['b_rg'], wig, p['b_ig'], p['lru_L'], tl,
        rs(RS_IN_LRU) + swap(RS_IN_MID))
    recv.update(zip(RS_IN_LRU, got))
    sib.update(zip(RS_IN_MID, got[len(RS_IN_LRU):]))
    reduce_own(RS_IN_LRU)
    dq, dkv, dfrow, *got = _b_attn(qkv_pad, att, datt, frow, rs(RS_IN_ATTN) + swap(RS_IN_LRU))
    recv.update(zip(RS_IN_ATTN, got))
    sib.update(zip(RS_IN_LRU, got[len(RS_IN_ATTN):]))
    reduce_own(RS_IN_ATTN)
    big['w_in'], bigb['w_in'] = _b_win(dq, dkv, dxg, h, ts)
    grad_x, dg_mix, *got = _b_inproj(dq, dkv, dxg, x, dx1, p['g_mix'], gw['w_in'], tmb,
                                     rs(['w_in']) + swap(RS_IN_ATTN))
    recv.update(zip(['w_in'], got))
    sib.update(zip(RS_IN_ATTN, got[1:]))
    reduce_own(['w_in'])
    small = {
        'g_mix': dg_mix, 'rel_bias': _rel_bias_grad_from_dfrow(dfrow), 'conv_w': dcw, 'conv_b': dcb,
        'w_rg': dwrg, 'b_rg': dbrg, 'w_ig': dwig, 'b_ig': dbig,
        'lru_L': dlam,
        'g_out_attn': dg_oa, 'g_out_lru': dg_ol, 'g_cross': dg_cross, 'g_mem': dg_mem, 'g_ffn': dg_ffn,
        'g_final': dg_final,
    }
    return loss, grad_x, small, big, part, sib


def _cast_shards(ws):
    def body(*refs):
        n = len(refs) // 2
        for src, dst in zip(refs[:n], refs[n:]):
            dst[...] = src[...].astype(BF16)

    return pl.pallas_call(body, name="cast_shards", out_shape=[_sds(w.shape, BF16) for w in ws],
                          compiler_params=_cp())(*ws)


def _sum_parts(own4, recv3, chip, name):
    _, r, c = own4.shape
    tr = r // 4

    def body(chip_ref, own_ref, rc_ref, o_ref):
        o_ref[...] = ((own_ref[0] + rc_ref[0].astype(F32)) + rc_ref[1].astype(F32)) + rc_ref[2].astype(F32)

    grid_spec = pltpu.PrefetchScalarGridSpec(
        num_scalar_prefetch=1, grid=(4,),
        in_specs=[pl.BlockSpec((1, tr, c), lambda i, ch: (ch[0], i, 0)),
                  pl.BlockSpec((3, tr, c), lambda i, ch: (0, i, 0))],
        out_specs=pl.BlockSpec((tr, c), lambda i, ch: (i, 0)))
    return pl.pallas_call(body, name=name, grid_spec=grid_spec, out_shape=_sds((r, c), F32),
                          compiler_params=_cp("parallel"))(chip, own4, recv3)


def _adamw_math(w, g, m, v):
    m = ADAM_B1 * m + (1.0 - ADAM_B1) * g
    v = ADAM_B2 * v + (1.0 - ADAM_B2) * (g * g)
    m_hat = m / (1.0 - ADAM_B1 ** ADAM_STEP)
    v_hat = v / (1.0 - ADAM_B2 ** ADAM_STEP)
    delta = -ADAM_LR * (m_hat / (jnp.sqrt(v_hat) + ADAM_EPS) + ADAM_WD * w)
    return delta, m, v


def _final_adamw(pa, pb, w, m, v, name):
    r, c = w.shape
    tr = r // 4

    def body(pa_ref, pb_ref, w_ref, m_ref, v_ref, g_ref, d_ref, nm_ref, nv_ref):
        g = pa_ref[...] + pb_ref[...]
        g_ref[...] = g
        d_ref[...], nm_ref[...], nv_ref[...] = _adamw_math(w_ref[...], g, m_ref[...], v_ref[...])

    return pl.pallas_call(
        body, name=name, grid=(4,), in_specs=[_rows(tr, c)] * 5, out_specs=[_rows(tr, c)] * 4,
        out_shape=[_sds((r, c), F32)] * 4, compiler_params=_cp("parallel"))(pa, pb, w, m, v)


def _pack_put(ref, name, val_ref):
    r = _pack_rows()[name]
    shape = val_ref.shape
    if len(shape) == 3:
        for b in range(shape[0]):
            ref[r:r + shape[1], b * shape[2]:(b + 1) * shape[2]] = val_ref[b]
    elif shape[1] == 2 * PACK_W:
        ref[r:r + 1, :] = val_ref[:, 0:PACK_W]
        ref[r + 1:r + 2, :] = val_ref[:, PACK_W:2 * PACK_W]
    else:
        ref[r:r + shape[0], 0:shape[1]] = val_ref[...]


def _pack_get(ref, name, shape):
    r = _pack_rows()[name]
    if len(shape) == 3:
        return jnp.stack([ref[r:r + shape[1], b * shape[2]:(b + 1) * shape[2]] for b in range(shape[0])])
    if shape[1] == 2 * PACK_W:
        return jnp.concatenate([ref[r:r + 1, :], ref[r + 1:r + 2, :]], axis=1)
    return ref[r:r + shape[0], 0:shape[1]]


def _ar_small(g, loss):
    n = len(g)

    def body(*refs):
        g_refs, loss_ref = refs[:n], refs[n]
        tot_ref, pack, buf, send_sems, recv_sems = refs[n + 1:]
        x, y, c = _mesh_pos()
        me = 4 * x + 2 * y + c

        def peer(k):
            px = 1 - x if k & 4 else x
            py = 1 - y if k & 2 else y
            pc = 1 - c if k & 1 else c
            return px, py, pc

        def remote(k, slot):
            return pltpu.make_async_remote_copy(
                src_ref=pack, dst_ref=buf.at[slot], send_sem=send_sems.at[k - 1], recv_sem=recv_sems.at[k - 1],
                device_id=peer(k), device_id_type=MESH_ID)

        pack[...] = jnp.zeros_like(pack)
        for a, name in enumerate(SMALL):
            _pack_put(pack, name, g_refs[a])
        _pack_put(pack, 'loss', loss_ref)
        for k in range(1, 8):
            remote(k, me).start()
        buf[me] = pack[...]
        for k in range(1, 8):
            px, py, pc = peer(k)
            remote(k, 4 * px + 2 * py + pc).wait_recv()
        for k in range(1, 8):
            remote(k, me).wait_send()
        tot = buf[0]
        for k in range(1, 8):
            tot = tot + buf[k]
        tot_ref[...] = tot

    return pl.pallas_call(
        body, name="ar_small", out_shape=_sds((PACK_ROWS, PACK_W), F32),
        scratch_shapes=[pltpu.VMEM((PACK_ROWS, PACK_W), F32), pltpu.VMEM((8, PACK_ROWS, PACK_W), F32),
                        pltpu.SemaphoreType.DMA((7,)), pltpu.SemaphoreType.DMA((7,))],
        compiler_params=_cp())(*g, loss)


def _adamw_small(tot, g_shapes, loss_shape, w, m, v):
    n = len(w)

    def body(*refs):
        tot_ref = refs[0]
        w_refs, m_refs, v_refs = (refs[1 + i * n:1 + (i + 1) * n] for i in range(3))
        o0 = 3 * n + 1
        go, do, mo, vo = (refs[o0 + i * n:o0 + (i + 1) * n] for i in range(4))
        loss_out = refs[o0 + 4 * n]
        x, y, _ = _mesh_pos()
        loss_out[...] = _pack_get(tot_ref, 'loss', loss_shape)
        for a, name in enumerate(SMALL):
            if name == 'conv_w':
                r = _pack_rows()[name]
                ga = tot_ref[r:r + g_shapes[a][0], pl.ds(pl.multiple_of((2 * x + y) * 128, 128), 128)]
            else:
                ga = _pack_get(tot_ref, name, g_shapes[a])
            go[a][...] = ga
            do[a][...], mo[a][...], vo[a][...] = _adamw_math(w_refs[a][...], ga, m_refs[a][...], v_refs[a][...])

    out_shape = [_sds(a.shape, F32) for a in w] * 4 + [_sds(loss_shape, F32)]
    return pl.pallas_call(body, name="adamw_small", out_shape=out_shape, compiler_params=_cp())(tot, *w, *m, *v)


PACK_W = 512
PACK_ROWS = 160


def _pack_rows():
    rows, r = {}, 0
    for name in ['g_mix', 'g_cross', 'g_mem', 'g_ffn', 'g_final']:
        rows[name] = r
        r += 2
    for name in ['conv_b', 'b_rg', 'b_ig', 'lru_L', 'g_out_attn', 'g_out_lru']:
        rows[name] = r
        r += 1
    rows['conv_w'] = r
    rows['loss'] = r + 4
    rows['rel_bias'] = 24
    rows['w_rg'] = 32
    rows['w_ig'] = 32 + LRU_BLOCK
    assert r + 5 <= 24 and rows['w_ig'] + LRU_BLOCK == PACK_ROWS
    return rows


INPUT_NAMES = (['x', 'mem'] + WEIGHTS + ['loss_target'] + ['m_' + n for n in WEIGHTS] + ['v_' + n for n in WEIGHTS])


def kernel(x, mem, g_mix, w_in, rel_bias, conv_w, conv_b, w_rg, b_rg, w_ig, b_ig, lru_L, g_out_attn, g_out_lru, w_out, g_cross, g_mem, wq_c, wk_c, wv_c, wo_c, g_ffn, w_gate, w_up, w_down, g_final, loss_target, m_g_mix, m_w_in, m_rel_bias, m_conv_w, m_conv_b, m_w_rg, m_b_rg, m_w_ig, m_b_ig, m_lru_L, m_g_out_attn, m_g_out_lru, m_w_out, m_g_cross, m_g_mem, m_wq_c, m_wk_c, m_wv_c, m_wo_c, m_g_ffn, m_w_gate, m_w_up, m_w_down, m_g_final, v_g_mix, v_w_in, v_rel_bias, v_conv_w, v_conv_b, v_w_rg, v_b_rg, v_w_ig, v_b_ig, v_lru_L, v_g_out_attn, v_g_out_lru, v_w_out, v_g_cross, v_g_mem, v_wq_c, v_wk_c, v_wv_c, v_wo_c, v_g_ffn, v_w_gate, v_w_up, v_w_down, v_g_final):
    a = dict(zip(INPUT_NAMES, (x, mem, g_mix, w_in, rel_bias, conv_w, conv_b, w_rg, b_rg, w_ig, b_ig, lru_L, g_out_attn, g_out_lru, w_out, g_cross, g_mem, wq_c, wk_c, wv_c, wo_c, g_ffn, w_gate, w_up, w_down, g_final, loss_target, m_g_mix, m_w_in, m_rel_bias, m_conv_w, m_conv_b, m_w_rg, m_b_rg, m_w_ig, m_b_ig, m_lru_L, m_g_out_attn, m_g_out_lru, m_w_out, m_g_cross, m_g_mem, m_wq_c, m_wk_c, m_wv_c, m_wo_c, m_g_ffn, m_w_gate, m_w_up, m_w_down, m_g_final, v_g_mix, v_w_in, v_rel_bias, v_conv_w, v_conv_b, v_w_rg, v_b_rg, v_w_ig, v_b_ig, v_lru_L, v_g_out_attn, v_g_out_lru, v_w_out, v_g_cross, v_g_mem, v_wq_c, v_wk_c, v_wv_c, v_wo_c, v_g_ffn, v_w_gate, v_w_up, v_w_down, v_g_final)))
    chip = 2 * lax.axis_index("x") + lax.axis_index("y")

    def shard(name):
        arr = a[name][0]
        return jnp.swapaxes(arr, 0, 1) if name[2:] in TRANSPOSED or name in TRANSPOSED else arr

    shards = dict(zip(BIG, _cast_shards([shard(n) for n in BIG])))
    w_in_g, conv_w_g = _comm_only("ag_w_in", [("ag", [shards['w_in']]), ("agf", [a['conv_w'][0]])])
    conv_w_full = conv_w_g.transpose(1, 0, 2).reshape(4, D_LRU)

    p = {n: a[n] for n in SMALL}
    p['rel_bias'] = a['rel_bias'][0]
    p['w_rg'] = a['w_rg'][0]
    p['w_ig'] = a['w_ig'][0]
    p['conv_w'] = conv_w_full
    p['g_final'] = a['g_final'][None, :]
    chip_arr = jnp.reshape(chip, (1,)).astype(jnp.int32)
    loss_part, grad_x, small, _, part, sib = _local_step(
        a['x'][0], a['mem'][0], a['loss_target'][0], p, {'w_in': w_in_g}, shards, chip_arr)

    sib['w_in'], = _comm_only("swap_w_in", [("swap", [part['w_in']])])
    out = {}
    for n in BIG:
        res = _final_adamw(part[n], sib[n], shard(n), shard('m_' + n), shard('v_' + n), "adamw_" + n)
        out[n] = [jnp.swapaxes(r, 0, 1) for r in res] if n in TRANSPOSED else res

    def natural(arr):
        return arr[0] if arr.ndim >= 3 else (arr[None, :] if arr.ndim == 1 else arr)

    small_g = [small[n] for n in SMALL]
    small_out = _adamw_small(_ar_small(small_g, loss_part), [g.shape for g in small_g], loss_part.shape,
                             *[[natural(a[pre + n]) for n in SMALL] for pre in ('', 'm_', 'v_')])
    ns = len(SMALL)
    loss = small_out[4 * ns][0, 0]

    def leaf(i, n):
        if n in BIG:
            return out[n][i][None]
        return small_out[i * ns + SMALL.index(n)].reshape(a[n].shape)

    return (loss, grad_x[None], *[leaf(i, n) for i in range(4) for n in WEIGHTS])
```

```python
import math

import jax
import jax.numpy as jnp
from jax import lax
from jax.experimental import pallas as pl
from jax.experimental.pallas import tpu as pltpu

F32 = jnp.float32
BF16 = jnp.bfloat16

D_MODEL = 1024
D_ATT = 512
D_LRU = 512
HEAD_DIM = 64
ATT_HEADS = 8
CHUNK = 64
LEFT_CHUNKS = 8
X_HEADS = 4
X_HEAD_DIM = 256
N_SHARD = 4
IN_SH = 640
D_IN = N_SHARD * IN_SH
FF_SH = 704
D_FF = N_SHARD * FF_SH
EPS = 1e-6
LRU_C = 8.0
LRU_BLOCKS = 8
LRU_BLOCK = 64
QB = 256
KB = 768
ROLL_W = 1024
NEG = -1e30
ATT_SCALE = HEAD_DIM ** -0.5
X_SCALE = X_HEAD_DIM ** -0.5

ADAM_LR = 0.001
ADAM_B1 = 0.9
ADAM_B2 = 0.999
ADAM_EPS = 1e-08
ADAM_WD = 0.01
ADAM_STEP = 10

VMEM_LIMIT_V7X = 56 * 1024 * 1024
MESH_ID = pl.DeviceIdType.MESH

WEIGHTS = ['g_mix', 'w_in', 'rel_bias', 'conv_w', 'conv_b', 'w_rg', 'b_rg', 'w_ig', 'b_ig', 'lru_L',
           'g_out_attn', 'g_out_lru', 'w_out', 'g_cross', 'g_mem', 'wq_c', 'wk_c', 'wv_c', 'wo_c',
           'g_ffn', 'w_gate', 'w_up', 'w_down', 'g_final']
BIG = ['w_in', 'w_out', 'wq_c', 'wk_c', 'wv_c', 'wo_c', 'w_gate', 'w_up', 'w_down']
SMALL = [n for n in WEIGHTS if n not in BIG]


def _sds(shape, dtype):
    return jax.ShapeDtypeStruct(shape, dtype)


def _cp(*sem):
    return pltpu.CompilerParams(dimension_semantics=sem or None, vmem_limit_bytes=VMEM_LIMIT_V7X)


def _rows(tm, n):
    return pl.BlockSpec((tm, n), lambda i: (i, 0))


def _full(shape):
    nd = len(shape)
    return pl.BlockSpec(shape, lambda i: (0,) * nd)


def _dot(a, b):
    return jnp.dot(a, b, preferred_element_type=F32)


def _dot_nt(a, b):
    return lax.dot_general(a, b, (((1,), (1,)), ((), ())), preferred_element_type=F32)


def _dot_tn(a, b):
    return lax.dot_general(a, b, (((0,), (0,)), ((), ())), preferred_element_type=F32)


def _rinv(x):
    return lax.rsqrt(jnp.mean(x * x, axis=-1, keepdims=True) + EPS)


def _rms_bwd(dy, x, g):
    r = _rinv(x)
    yh = x * r
    dyh = dy * g
    dx = r * (dyh - yh * jnp.mean(dyh * yh, axis=-1, keepdims=True))
    return dx, jnp.sum(dy * yh, axis=0, keepdims=True)


def _gelu(x):
    c = math.sqrt(2.0 / math.pi)
    t = jnp.tanh(c * (x + 0.044715 * x * x * x))
    return 0.5 * x * (1.0 + t)


def _gelu_and_grad(x):
    c = math.sqrt(2.0 / math.pi)
    t = jnp.tanh(c * (x + 0.044715 * x * x * x))
    g = 0.5 * x * (1.0 + t)
    dg = 0.5 * (1.0 + t) + 0.5 * x * (1.0 - t * t) * c * (1.0 + 3.0 * 0.044715 * x * x)
    return g, dg


def _neg_expm1(z):
    series = -z * (1 + z / 2 * (1 + z / 3 * (1 + z / 4)))
    return jnp.where(z > -0.03, series, 1.0 - jnp.exp(z))


def _lru_gates(u, wrg, brg, wig, big, lam):
    ub = u.astype(BF16)
    r = jax.nn.sigmoid(_dot(ub, wrg) + brg)
    ig = jax.nn.sigmoid(_dot(ub, wig) + big)
    sp = jnp.maximum(-lam, 0.0) + jnp.log1p(jnp.exp(-jnp.abs(lam)))
    la = -LRU_C * r * sp
    a = jnp.exp(la)
    mult = jnp.sqrt(jnp.maximum(_neg_expm1(2.0 * la), 0.0))
    return ub, r, ig, sp, a, mult


def _scan8(a8, b8, hprev):
    row = lax.broadcasted_iota(jnp.int32, a8.shape, 0)
    aa, bb = a8, b8
    for d in (1, 2, 4):
        a_s = pltpu.roll(aa, d, 0)
        b_s = pltpu.roll(bb, d, 0)
        m = row >= d
        bb = jnp.where(m, aa * b_s + bb, bb)
        aa = jnp.where(m, aa * a_s, aa)
    return aa * hprev + bb


def _mesh_pos():
    return lax.axis_index("x"), lax.axis_index("y"), lax.axis_index("c")


def _other_chips(x, y):
    return [(1 - x, y), (x, 1 - y), (1 - x, 1 - y)]


def _no_forward():
    pass


def _ag_full_copies(ins, outs, sems):
    send_sems, recv_sems, loc_sems = sems
    n = len(ins)
    x, y, c = _mesh_pos()
    mine = 2 * x + y
    chips = _other_chips(x, y)

    def remote(k, j, slot):
        px, py = chips[j]
        return pltpu.make_async_remote_copy(
            src_ref=ins[k], dst_ref=outs[k].at[slot], send_sem=send_sems.at[k, j], recv_sem=recv_sems.at[k, j],
            device_id=(px, py, c), device_id_type=MESH_ID)

    def local(k):
        return pltpu.make_async_copy(ins[k], outs[k].at[mine], loc_sems.at[k])

    def start():
        for k in range(n):
            local(k).start()
            for j in range(3):
                remote(k, j, mine).start()

    def wait():
        for k in range(n):
            for j, (px, py) in enumerate(chips):
                remote(k, j, 2 * px + py).wait_recv()
        for k in range(n):
            for j in range(3):
                remote(k, j, mine).wait_send()
            local(k).wait()

    return start, _no_forward, wait


def _ag_copies(ins, outs, sems):
    send_sems, recv_sems, fsend_sems, frecv_sems, loc_sems = sems
    n = len(ins)
    x, y, c = _mesh_pos()
    mine = 2 * x + y
    chips = _other_chips(x, y)

    def half(ref, hc):
        r = ref.shape[0] // 2
        return ref.at[pl.ds(pl.multiple_of(hc * r, 16), r)]

    def ici(k, j, slot):
        px, py = chips[j]
        return pltpu.make_async_remote_copy(
            src_ref=half(ins[k], c), dst_ref=half(outs[k].at[slot], c),
            send_sem=send_sems.at[k, j], recv_sem=recv_sems.at[k, j],
            device_id=(px, py, c), device_id_type=MESH_ID)

    def d2d(k, j, hc):
        px, py = chips[j]
        part = half(outs[k].at[2 * px + py], hc)
        return pltpu.make_async_remote_copy(
            src_ref=part, dst_ref=part, send_sem=fsend_sems.at[k, j], recv_sem=frecv_sems.at[k, j],
            device_id=(x, y, 1 - c), device_id_type=MESH_ID)

    def local(k):
        return pltpu.make_async_copy(ins[k], outs[k].at[mine], loc_sems.at[k])

    def start():
        for k in range(n):
            local(k).start()
            for j in range(3):
                ici(k, j, mine).start()

    def forward():
        for k in range(n):
            for j, (px, py) in enumerate(chips):
                ici(k, j, 2 * px + py).wait_recv()
                d2d(k, j, c).start()

    def wait():
        for k in range(n):
            for j in range(3):
                d2d(k, j, 1 - c).wait_recv()
        for k in range(n):
            for j in range(3):
                d2d(k, j, c).wait_send()
                ici(k, j, mine).wait_send()
            local(k).wait()

    return start, forward, wait


def _rs_copies(ins, outs, sems):
    send_sems, recv_sems = sems
    n = len(ins)
    x, y, c = _mesh_pos()
    chips = _other_chips(x, y)

    def remote(k, j):
        px, py = chips[j]
        return pltpu.make_async_remote_copy(
            src_ref=ins[k].at[2 * px + py], dst_ref=outs[k].at[j],
            send_sem=send_sems.at[k, j], recv_sem=recv_sems.at[k, j],
            device_id=(px, py, c), device_id_type=MESH_ID)

    def start():
        for k in range(n):
            for j in range(3):
                remote(k, j).start()

    def wait():
        for k in range(n):
            for j in range(3):
                remote(k, j).wait_recv()
        for k in range(n):
            for j in range(3):
                remote(k, j).wait_send()

    return start, _no_forward, wait


def _swap_copies(ins, outs, sems):
    send_sems, recv_sems = sems
    x, y, c = _mesh_pos()
    copies = [pltpu.make_async_remote_copy(
        src_ref=ins[k], dst_ref=outs[k], send_sem=send_sems.at[k], recv_sem=recv_sems.at[k],
        device_id=(x, y, 1 - c), device_id_type=MESH_ID) for k in range(len(ins))]

    def start():
        for cp in copies:
            cp.start()

    def wait():
        for cp in copies:
            cp.wait()

    return start, _no_forward, wait


def _comm_plan(groups):
    plan, arrs, shapes, sems = [], [], [], []
    for kind, group in groups:
        k = len(group)
        arrs += group
        per_peer = pltpu.SemaphoreType.DMA((k, 3))
        if kind == "ag":
            shapes += [_sds((N_SHARD,) + w.shape, w.dtype) for w in group]
            gsems = [per_peer] * 4 + [pltpu.SemaphoreType.DMA((k,))]
            maker = _ag_copies
        elif kind == "agf":
            shapes += [_sds((N_SHARD,) + w.shape, w.dtype) for w in group]
            gsems = [per_peer] * 2 + [pltpu.SemaphoreType.DMA((k,))]
            maker = _ag_full_copies
        elif kind == "rs":
            shapes += [_sds((3,) + g.shape[1:], g.dtype) for g in group]
            gsems = [pltpu.SemaphoreType.DMA((k, 3)), pltpu.SemaphoreType.DMA((k, 3))]
            maker = _rs_copies
        else:
            shapes += [_sds(g.shape, g.dtype) for g in group]
            gsems = [pltpu.SemaphoreType.DMA((k,)), pltpu.SemaphoreType.DMA((k,))]
            maker = _swap_copies
        plan.append((maker, k, len(gsems)))
        sems += gsems
    return plan, arrs, shapes, sems


def _comm_fns(plan, cins, couts, sems):
    fns, a, s = [], 0, 0
    for maker, k, ns in plan:
        fns.append(maker(cins[a:a + k], couts[a:a + k], sems[s:s + ns]))
        a += k
        s += ns

    def start():
        for st, _, _ in fns:
            st()

    def forward():
        for _, fw, _ in fns:
            fw()

    def wait():
        for _, _, wt in fns:
            wt()

    return start, forward, wait


def _call(body, name, grid, in_specs, out_specs, out_shape, scratch, args, sem, comm=None):
    if not comm:
        return pl.pallas_call(body, name=name, grid=grid, in_specs=in_specs, out_specs=out_specs,
                              out_shape=out_shape, scratch_shapes=scratch, compiler_params=_cp(sem))(*args)
    plan, c_arrs, c_shapes, c_sems = _comm_plan(comm)
    k = len(c_arrs)
    n_in, n_out, n_scr = len(in_specs), len(out_specs), len(scratch)
    last = grid[0] - 1
    fwd_step = max(1, (2 * last) // 3)

    def wrapped(*refs):
        ins, cins = refs[:n_in], refs[n_in:n_in + k]
        o0 = n_in + k
        outs, couts = refs[o0:o0 + n_out], refs[o0 + n_out:o0 + n_out + k]
        s0 = o0 + n_out + k
        start, forward, wait = _comm_fns(plan, cins, couts, refs[s0 + n_scr:])
        pl.when(pl.program_id(0) == 0)(start)
        pl.when(pl.program_id(0) == fwd_step)(forward)
        body(*ins, *outs, *refs[s0:s0 + n_scr])
        pl.when(pl.program_id(0) == last)(wait)

    return pl.pallas_call(
        wrapped, name=name, grid=grid, in_specs=list(in_specs) + [_any()] * k,
        out_specs=list(out_specs) + [_any()] * k, out_shape=list(out_shape) + c_shapes,
        scratch_shapes=list(scratch) + c_sems, compiler_params=_cp(sem))(*args, *c_arrs)


def _comm_only(name, comm):
    plan, c_arrs, c_shapes, c_sems = _comm_plan(comm)
    k = len(c_arrs)

    def body(*refs):
        start, forward, wait = _comm_fns(plan, refs[:k], refs[k:2 * k], refs[2 * k:])
        start()
        forward()
        wait()

    return pl.pallas_call(body, name=name, in_specs=[_any()] * k, out_specs=[_any()] * k, out_shape=c_shapes,
                          scratch_shapes=c_sems, compiler_params=_cp())(*c_arrs)


def _any():
    return pl.BlockSpec(memory_space=pl.ANY)


def _rscan8(c8, d8, lnext):
    row = lax.broadcasted_iota(jnp.int32, c8.shape, 0)
    cc, dd = c8, d8
    for d in (1, 2, 4):
        c_s = pltpu.roll(cc, 8 - d, 0)
        d_s = pltpu.roll(dd, 8 - d, 0)
        m = row < 8 - d
        dd = jnp.where(m, cc * d_s + dd, dd)
        cc = jnp.where(m, cc * c_s, cc)
    return cc * lnext + dd


def _load_w_in_once(w_hbm, w_ref):
    @pl.when(pl.program_id(0) == 0)
    def _():
        for s in range(N_SHARD):
            pltpu.sync_copy(w_hbm.at[s], w_ref.at[:, pl.ds(s * IN_SH, IN_SH)])


def _f_inproj(x, g_mix, w_in_g, tm, comm=None):
    s_len = x.shape[0]
    pad_rows = LEFT_CHUNKS * CHUNK
    npad = pad_rows // tm

    def body(x_ref, g_ref, w_hbm, h_ref, qkv_ref, xg_ref, w_ref):
        i = pl.program_id(0)
        _load_w_in_once(w_hbm, w_ref)

        @pl.when(i < npad)
        def _():
            qkv_ref[...] = jnp.zeros_like(qkv_ref)

        @pl.when(i >= npad)
        def _():
            xv = x_ref[...]
            h = (xv * _rinv(xv) * g_ref[...]).astype(BF16)
            h_ref[...] = h
            proj = _dot(h, w_ref[...])
            qkv_ref[:, 0:D_ATT] = (proj[:, 0:D_ATT] * ATT_SCALE).astype(BF16)
            qkv_ref[:, D_ATT:3 * D_ATT] = proj[:, D_ATT:3 * D_ATT].astype(BF16)
            xg_ref[...] = proj[:, 3 * D_ATT:D_IN]

    def tok(n):
        return pl.BlockSpec((tm, n), lambda i: (jnp.maximum(i - npad, 0), 0))

    return _call(
        body, "f_inproj", (s_len // tm + npad,),
        [tok(1024), _full((1, 1024)), _any()],
        [tok(1024), _rows(tm, 1536), tok(1024)],
        [_sds((s_len, 1024), BF16), _sds((s_len + pad_rows, 1536), BF16), _sds((s_len, 1024), F32)],
        [pltpu.VMEM((1024, D_IN), BF16)], (x, g_mix, w_in_g), "arbitrary", comm)


N_BIAS = 3


def _bias_table(frow_ref, bias_sc):
    qa = lax.broadcasted_iota(jnp.int32, (QB, KB), 0) // CHUNK
    kcol = lax.broadcasted_iota(jnp.int32, (QB, KB), 1)
    kb = kcol // CHUNK
    band = jnp.where((kb >= qa) & (kb - qa <= LEFT_CHUNKS), 0.0, NEG).astype(F32)
    for h in range(ATT_HEADS):
        row = jnp.broadcast_to(frow_ref[h:h + 1, :], (QB, ROLL_W))
        toep = pltpu.roll(row, 0, 1, stride=1, stride_axis=0)
        gen = toep[:, 0:KB] + band
        bias_sc[N_BIAS - 1, h] = gen
        for v in range(N_BIAS - 1):
            pad_keys = LEFT_CHUNKS * CHUNK - v * QB
            bias_sc[v, h] = gen + jnp.where(kcol < pad_keys, NEG, 0.0).astype(F32)


def _even_lanes():
    return lax.broadcasted_iota(jnp.int32, (1, 2 * HEAD_DIM), 1) < HEAD_DIM


def _att_probs(qm, kts, bias):
    s = jnp.concatenate([_dot_nt(qm, k) for k in kts], axis=1) + bias
    return jnp.exp(s - jnp.max(s, axis=-1, keepdims=True))


def _att_in_specs(clamp):
    def spec(j, col):
        return pl.BlockSpec((QB, D_ATT), lambda i: (clamp(i) + j, col))
    return [spec(2, 0), spec(0, 1), spec(1, 1), spec(2, 1), spec(0, 2), spec(1, 2), spec(2, 2)]


def _f_attn(qkv_pad, frow, comm=None):
    s_len = qkv_pad.shape[0] - LEFT_CHUNKS * CHUNK
    nb = s_len // QB

    def body(q_ref, k0, k1, k2, v0, v1, v2, frow_ref, o_ref, bias_sc):
        i = pl.program_id(0)

        @pl.when(i == 0)
        def _():
            _bias_table(frow_ref, bias_sc)

        var = jnp.minimum(i, N_BIAS - 1)
        even = _even_lanes()
        for hp in range(ATT_HEADS // 2):
            cs = slice(hp * 2 * HEAD_DIM, (hp + 1) * 2 * HEAD_DIM)
            qt = q_ref[:, cs]
            kts = [k0[:, cs], k1[:, cs], k2[:, cs]]
            vts = [v0[:, cs], v1[:, cs], v2[:, cs]]
            res = []
            for e in range(2):
                keep = even if e == 0 else jnp.logical_not(even)
                pb = _att_probs(jnp.where(keep, qt, 0), kts, bias_sc[var, 2 * hp + e]).astype(BF16)
                r = _dot(pb[:, 0:QB], jnp.where(keep, vts[0], 1))
                for j in (1, 2):
                    r = r + _dot(pb[:, j * QB:(j + 1) * QB], jnp.where(keep, vts[j], 1))
                res.append(r / pltpu.roll(r, HEAD_DIM, 1))
            o_ref[:, cs] = jnp.where(even, res[0], res[1])

    return _call(
        body, "f_attn", (nb,),
        _att_in_specs(lambda i: i) + [_full((ATT_HEADS, ROLL_W))],
        [_rows(QB, D_ATT)], [_sds((s_len, D_ATT), F32)],
        [pltpu.VMEM((N_BIAS, ATT_HEADS, QB, KB), F32)], (*([qkv_pad] * 7), frow), "arbitrary", comm)


def _f_lru(xg, conv_w, conv_b, wrg, brg, wig, big, lam, tl, comm=None):
    s_len = xg.shape[0]

    def body(xg_ref, cw_ref, cb_ref, wrg_ref, brg_ref, wig_ref, big_ref, l_ref,
             rec_ref, u_ref, hs_ref, xbuf, a_sc, b_sc, hcar):
        i = pl.program_id(0)

        @pl.when(i == 0)
        def _():
            xbuf[0:8, :] = jnp.zeros((8, D_LRU), F32)
            hcar[...] = jnp.zeros((8, D_LRU), F32)

        xu0 = xg_ref[:, 0:D_LRU]
        xbuf[8:8 + tl, :] = xu0
        u = cb_ref[...] + cw_ref[0:1, :] * xbuf[pl.ds(5, tl), :]
        for j in range(1, 4):
            u = u + cw_ref[j:j + 1, :] * xbuf[pl.ds(5 + j, tl), :]
        xbuf[0:8, :] = xu0[tl - 8:tl, :]
        u_ref[...] = u
        _, _, ig, _, a, mult = _lru_gates(u, wrg_ref[...], brg_ref[...], wig_ref[...], big_ref[...], l_ref[...])
        a_sc[...] = a
        b_sc[...] = mult * (ig * u)

        def grp(g, hprev):
            off = pl.multiple_of(g * 8, 8)
            h8 = _scan8(a_sc[pl.ds(off, 8), :], b_sc[pl.ds(off, 8), :], hprev)
            hs_ref[pl.ds(off, 8), :] = h8
            return h8[7:8, :]

        hcar[0:1, :] = lax.fori_loop(0, tl // 8, grp, hcar[0:1, :])
        rec_ref[...] = hs_ref[...] * _gelu(xg_ref[:, D_LRU:2 * D_LRU])

    vec = _full((1, D_LRU))
    return _call(
        body, "f_lru", (s_len // tl,),
        [_rows(tl, 1024), _full((4, D_LRU)), vec, _full((D_LRU, D_LRU)), vec, _full((D_LRU, D_LRU)), vec, vec],
        [_rows(tl, D_LRU)] * 3, [_sds((s_len, D_LRU), F32)] * 3,
        [pltpu.VMEM((tl + 8, D_LRU), F32), pltpu.VMEM((tl, D_LRU), F32),
         pltpu.VMEM((tl, D_LRU), F32), pltpu.VMEM((8, D_LRU), F32)],
        (xg, conv_w, conv_b, wrg, brg, wig, big, lam), "arbitrary", comm)


def _f_mem(mem, g_mem, wk, wv):
    def body(mem_ref, g_ref, wk_ref, wv_ref, mn_ref, kx_ref, vx_ref):
        mv = mem_ref[...]
        mn = (mv * _rinv(mv) * g_ref[...]).astype(BF16)
        mn_ref[...] = mn
        kx_ref[...] = _dot(mn, wk_ref[...]).astype(BF16)
        vx_ref[...] = _dot(mn, wv_ref[...]).astype(BF16)

    m = mem.shape[0]
    return pl.pallas_call(
        body, name="f_mem", out_shape=[_sds((m, 1024), BF16)] * 3,
        compiler_params=_cp())(mem, g_mem, wk, wv)


def _xattn_probs(q, k):
    s = _dot_nt(q, k) * X_SCALE
    m = jnp.max(s, axis=-1, keepdims=True)
    p = jnp.exp(s - m)
    return p, jnp.sum(p, axis=-1, keepdims=True)


def _f_mid(x, att, rec, g_oa, g_ol, w_out, g_cross, wq, kx, vx, wo, tm, comm=None):
    s_len = x.shape[0]
    m_len = kx.shape[0]

    def body(x_ref, att_ref, rec_ref, goa_ref, gol_ref, wout_ref, gc_ref, wq_ref, kx_ref, vx_ref, wo_ref,
             mg_ref, x1_ref, hc_ref, qx_ref, ox_ref, x2_ref):
        av = att_ref[...]
        rv = rec_ref[...]
        mg_ref[:, 0:D_ATT] = (av * _rinv(av) * goa_ref[...]).astype(BF16)
        mg_ref[:, D_ATT:1024] = (rv * _rinv(rv) * gol_ref[...]).astype(BF16)
        x1 = x_ref[...] + _dot(mg_ref[...], wout_ref[...])
        x1_ref[...] = x1
        hc = (x1 * _rinv(x1) * gc_ref[...]).astype(BF16)
        hc_ref[...] = hc
        qx_ref[...] = _dot(hc, wq_ref[...]).astype(BF16)
        for h in range(X_HEADS):
            sl = slice(h * X_HEAD_DIM, (h + 1) * X_HEAD_DIM)
            p, l = _xattn_probs(qx_ref[:, sl], kx_ref[:, sl])
            ox_ref[:, sl] = (_dot(p.astype(BF16), vx_ref[:, sl]) / l).astype(BF16)
        x2_ref[...] = x1 + _dot(ox_ref[...], wo_ref[...])

    sq = _full((1024, 1024))
    return _call(
        body, "f_mid", (s_len // tm,),
        [_rows(tm, 1024), _rows(tm, 512), _rows(tm, 512), _full((1, 512)), _full((1, 512)), sq,
         _full((1, 1024)), sq, _full((m_len, 1024)), _full((m_len, 1024)), sq],
        [_rows(tm, 1024)] * 6,
        [_sds((s_len, 1024), BF16), _sds((s_len, 1024), F32), _sds((s_len, 1024), BF16),
         _sds((s_len, 1024), BF16), _sds((s_len, 1024), BF16), _sds((s_len, 1024), F32)],
        [], (x, att, rec, g_oa, g_ol, w_out, g_cross, wq, kx, vx, wo), "arbitrary", comm)


def _load_weights_once(pairs):
    @pl.when(pl.program_id(0) == 0)
    def _():
        for hbm, vmem in pairs:
            pltpu.sync_copy(hbm, vmem)


FF_CHUNKS = [(0, 1280), (1280, D_FF)]


def _f_ffn(x2, tgt, g_ffn, g_final, wg, wu, wd, tm):
    s_len = x2.shape[0]

    def body(x2_ref, t_ref, gf_ref, gfin_ref, wg_hbm, wu_hbm, wd_hbm,
             hf_ref, g_ref, u_ref, a_ref, dx3_ref, loss_ref, dgfin_ref, wg_ref, wu_ref, wd_ref):
        _load_weights_once([(wg_hbm, wg_ref), (wu_hbm, wu_ref), (wd_hbm, wd_ref)])

        @pl.when(pl.program_id(0) == 0)
        def _():
            loss_ref[...] = jnp.zeros_like(loss_ref)
            dgfin_ref[...] = jnp.zeros_like(dgfin_ref)

        x2v = x2_ref[...]
        hf = (x2v * _rinv(x2v) * gf_ref[...]).astype(BF16)
        hf_ref[...] = hf
        x3 = x2v
        for c0, c1 in FF_CHUNKS:
            gv = _dot_nt(hf, wg_ref[c0:c1, :])
            uv = _dot_nt(hf, wu_ref[c0:c1, :])
            av = (gv * jax.nn.sigmoid(gv) * uv).astype(BF16)
            g_ref[:, c0:c1] = gv.astype(BF16)
            u_ref[:, c0:c1] = uv.astype(BF16)
            a_ref[:, c0:c1] = av
            x3 = x3 + _dot(av, wd_ref[c0:c1, :])
        r3 = _rinv(x3)
        yh = x3 * r3
        gfin = gfin_ref[...]
        err = yh * gfin - t_ref[...]
        loss_ref[...] += jnp.full((1, 128), 0.5 / D_MODEL, F32) * jnp.sum(err * err)
        dy = err * (1.0 / D_MODEL)
        dgfin_ref[...] += jnp.sum(dy * yh, axis=0, keepdims=True)
        dyh = dy * gfin
        dx3_ref[...] = r3 * (dyh - yh * jnp.mean(dyh * yh, axis=-1, keepdims=True))

    vec = _full((1, 1024))
    return pl.pallas_call(
        body, name="f_ffn", grid=(s_len // tm,),
        in_specs=[_rows(tm, 1024), _rows(tm, 1024), vec, vec, _any(), _any(), _any()],
        out_specs=[_rows(tm, 1024), _rows(tm, D_FF), _rows(tm, D_FF), _rows(tm, D_FF),
                   _rows(tm, 1024), _full((1, 128)), vec],
        out_shape=[_sds((s_len, 1024), BF16)] + [_sds((s_len, D_FF), BF16)] * 3
                  + [_sds((s_len, 1024), F32), _sds((1, 128), F32), _sds((1, 1024), F32)],
        scratch_shapes=[pltpu.VMEM((D_FF, 1024), BF16)] * 3,
        compiler_params=_cp("arbitrary"))(x2, tgt, g_ffn, g_final, wg, wu, wd)


def _b_ffn(dx3, x2, gact, uact, g_ffn, wg, wu, wd, tm):
    s_len = x2.shape[0]

    def body(dx3_ref, x2_ref, g_ref, u_ref, gf_ref, wg_hbm, wu_hbm, wd_hbm,
             dg_ref, du_ref, dx2_ref, dgf_ref, wg_ref, wu_ref, wd_ref):
        _load_weights_once([(wg_hbm, wg_ref), (wu_hbm, wu_ref), (wd_hbm, wd_ref)])

        @pl.when(pl.program_id(0) == 0)
        def _():
            dgf_ref[...] = jnp.zeros_like(dgf_ref)

        dx3v = dx3_ref[...]
        dx3b = dx3v.astype(BF16)
        dhf = jnp.zeros(dx3v.shape, F32)
        for c0, c1 in FF_CHUNKS:
            da = _dot_nt(dx3b, wd_ref[c0:c1, :])
            gv = g_ref[:, c0:c1].astype(F32)
            uv = u_ref[:, c0:c1].astype(F32)
            sg = jax.nn.sigmoid(gv)
            dub = (da * gv * sg).astype(BF16)
            dgb = (da * uv * (sg * (1.0 + gv * (1.0 - sg)))).astype(BF16)
            du_ref[:, c0:c1] = dub
            dg_ref[:, c0:c1] = dgb
            dhf = dhf + _dot(dgb, wg_ref[c0:c1, :]) + _dot(dub, wu_ref[c0:c1, :])
        dx, dgf = _rms_bwd(dhf, x2_ref[...], gf_ref[...])
        dx2_ref[...] = dx3v + dx
        dgf_ref[...] += dgf

    vec = _full((1, 1024))
    return pl.pallas_call(
        body, name="b_ffn", grid=(s_len // tm,),
        in_specs=[_rows(tm, 1024), _rows(tm, 1024), _rows(tm, D_FF), _rows(tm, D_FF), vec,
                  _any(), _any(), _any()],
        out_specs=[_rows(tm, D_FF), _rows(tm, D_FF), _rows(tm, 1024), vec],
        out_shape=[_sds((s_len, D_FF), BF16)] * 2 + [_sds((s_len, 1024), F32), _sds((1, 1024), F32)],
        scratch_shapes=[pltpu.VMEM((D_FF, 1024), BF16)] * 3,
        compiler_params=_cp("arbitrary"))(dx3, x2, gact, uact, g_ffn, wg, wu, wd)


def _b_mid(dx2, qx, x1, att, rec, kx, vx, wo, wq, w_out, g_cross, g_oa, g_ol, tm, comm=None):
    s_len = x1.shape[0]
    m_len = kx.shape[0]

    def body(dx2_ref, qx_ref, x1_ref, att_ref, rec_ref, kx_ref, vx_ref, wo_ref, wq_ref, wout_ref,
             gc_ref, goa_ref, gol_ref,
             dqx_ref, dx1_ref, datt_ref, drec_ref, dkx_ref, dvx_ref, dgc_ref, dgoa_ref, dgol_ref):
        @pl.when(pl.program_id(0) == 0)
        def _():
            for r in (dkx_ref, dvx_ref, dgc_ref, dgoa_ref, dgol_ref):
                r[...] = jnp.zeros_like(r)

        dx2v = dx2_ref[...]
        dox = _dot_nt(dx2v.astype(BF16), wo_ref[...])
        for h in range(X_HEADS):
            sl = slice(h * X_HEAD_DIM, (h + 1) * X_HEAD_DIM)
            q = qx_ref[:, sl]
            p, l = _xattn_probs(q, kx_ref[:, sl])
            pn = p / l
            dob = dox[:, sl].astype(BF16)
            dp = _dot_nt(dob, vx_ref[:, sl])
            dvx_ref[:, sl] += _dot_tn(pn.astype(BF16), dob)
            ds = pn * (dp - jnp.sum(dp * pn, axis=-1, keepdims=True))
            dsb = (ds * X_SCALE).astype(BF16)
            dqx_ref[:, sl] = _dot(dsb, kx_ref[:, sl]).astype(BF16)
            dkx_ref[:, sl] += _dot_tn(dsb, q)
        dhc = _dot_nt(dqx_ref[...], wq_ref[...])
        dx, dgc = _rms_bwd(dhc, x1_ref[...], gc_ref[...])
        dx1 = dx2v + dx
        dx1_ref[...] = dx1
        dgc_ref[...] += dgc
        dmg = _dot_nt(dx1.astype(BF16), wout_ref[...])
        da, dgoa = _rms_bwd(dmg[:, 0:D_ATT], att_ref[...], goa_ref[...])
        datt_ref[...] = da
        dgoa_ref[...] += dgoa
        dr, dgol = _rms_bwd(dmg[:, D_ATT:1024], rec_ref[...], gol_ref[...])
        drec_ref[...] = dr
        dgol_ref[...] += dgol

    sq = _full((1024, 1024))
    mk = _full((m_len, 1024))
    return _call(
        body, "b_mid", (s_len // tm,),
        [_rows(tm, 1024), _rows(tm, 1024), _rows(tm, 1024), _rows(tm, 512), _rows(tm, 512), mk, mk,
         sq, sq, sq, _full((1, 1024)), _full((1, 512)), _full((1, 512))],
        [_rows(tm, 1024), _rows(tm, 1024), _rows(tm, 512), _rows(tm, 512), mk, mk,
         _full((1, 1024)), _full((1, 512)), _full((1, 512))],
        [_sds((s_len, 1024), BF16), _sds((s_len, 1024), F32), _sds((s_len, 512), F32),
         _sds((s_len, 512), F32), _sds((m_len, 1024), F32), _sds((m_len, 1024), F32),
         _sds((1, 1024), F32), _sds((1, 512), F32), _sds((1, 512), F32)],
        [], (dx2, qx, x1, att, rec, kx, vx, wo, wq, w_out, g_cross, g_oa, g_ol), "arbitrary", comm)


def _b_mem(dkx, dvx, mem, mn, g_mem, wk, wv):
    def body(dkx_ref, dvx_ref, mem_ref, mn_ref, g_ref, wk_ref, wv_ref, dwk_ref, dwv_ref, dgm_ref,
             dwkb_ref, dwvb_ref):
        dkb = dkx_ref[...].astype(BF16)
        dvb = dvx_ref[...].astype(BF16)
        dwk = _dot_tn(mn_ref[...], dkb)
        dwv = _dot_tn(mn_ref[...], dvb)
        dwk_ref[...] = dwk
        dwv_ref[...] = dwv
        dwkb_ref[...] = dwk.astype(BF16)
        dwvb_ref[...] = dwv.astype(BF16)
        dmn = _dot_nt(dkb, wk_ref[...]) + _dot_nt(dvb, wv_ref[...])
        mv = mem_ref[...]
        dgm_ref[...] = jnp.sum(dmn * (mv * _rinv(mv)), axis=0, keepdims=True)

    return pl.pallas_call(
        body, name="b_mem",
        out_shape=[_sds((1024, 1024), F32), _sds((1024, 1024), F32), _sds((1, 1024), F32),
                   _sds((1024, 1024), BF16), _sds((1024, 1024), BF16)],
        compiler_params=_cp())(dkx, dvx, mem, mn, g_mem, wk, wv)


def _b_lru(drec, hs, u, xg, conv_w, wrg, brg, wig, big, lam, tl, comm=None):
    s_len = xg.shape[0]
    nt = s_len // tl

    def body(drec_ref, hs_ref, hsp_ref, u_ref, xg_ref, cw_ref, wrg_ref, brg_ref, wig_ref, big_ref, l_ref,
             dxg_ref, dwrg_ref, dwig_ref, dbrg_ref, dbig_ref, dlam_ref, dcw_ref, dcb_ref,
             hbuf, abuf, dubuf, c_sc, d_sc, lam_sc, lcar, wacc_r, wacc_i):
        i = pl.program_id(0)
        tt = nt - 1 - i

        @pl.when(i == 0)
        def _():
            for r in (wacc_r, wacc_i, dbrg_ref, dbig_ref, dlam_ref, dcw_ref, dcb_ref):
                r[...] = jnp.zeros_like(r)
            abuf[tl:tl + 8, :] = jnp.zeros((8, D_LRU), F32)
            dubuf[tl:tl + 8, :] = jnp.zeros((8, D_LRU), F32)
            lcar[...] = jnp.zeros((8, D_LRU), F32)

        xu0 = xg_ref[:, 0:D_LRU]
        hsv = hs_ref[...]
        uv = u_ref[...]
        hbuf[8:8 + tl, :] = hsv
        hbuf[0:8, :] = jnp.where(tt > 0, hsp_ref[...], 0.0)
        hshift = hbuf[pl.ds(7, tl), :]
        wrg_v = wrg_ref[...]
        wig_v = wig_ref[...]
        lamv = l_ref[...]
        ub, r, ig, sp, a, mult = _lru_gates(uv, wrg_v, brg_ref[...], wig_v, big_ref[...], lamv)
        abuf[0:tl, :] = a
        c_sc[...] = abuf[pl.ds(1, tl), :]
        gel, dgel = _gelu_and_grad(xg_ref[:, D_LRU:2 * D_LRU])
        drv = drec_ref[...]
        d_sc[...] = drv * gel
        dxg_ref[:, D_LRU:2 * D_LRU] = (drv * hsv * dgel).astype(BF16)

        def grp(k, lnext):
            off = pl.multiple_of((tl // 8 - 1 - k) * 8, 8)
            l8 = _rscan8(c_sc[pl.ds(off, 8), :], d_sc[pl.ds(off, 8), :], lnext)
            lam_sc[pl.ds(off, 8), :] = l8
            return l8[0:1, :]

        lcar[0:1, :] = lax.fori_loop(0, tl // 8, grp, lcar[0:1, :])
        abuf[tl:tl + 8, :] = a[0:8, :]
        db = lam_sc[...]
        da = db * hshift
        dmult = db * (ig * uv)
        dig = db * mult * uv
        du = db * mult * ig
        dla = da * a - dmult * (a * a) / mult
        dlam_ref[...] += jnp.sum(dla * (-LRU_C) * r, axis=0, keepdims=True)
        dzr = dla * (-LRU_C * sp) * r * (1.0 - r)
        dzi = dig * ig * (1.0 - ig)
        dzrb = dzr.astype(BF16)
        dzib = dzi.astype(BF16)
        du = du + _dot_nt(dzrb, wrg_v) + _dot_nt(dzib, wig_v)
        wacc_r[...] += _dot_tn(ub, dzrb)
        wacc_i[...] += _dot_tn(ub, dzib)
        dbrg_ref[...] += jnp.sum(dzr, axis=0, keepdims=True)
        dbig_ref[...] += jnp.sum(dzi, axis=0, keepdims=True)
        dcb_ref[...] += jnp.sum(du, axis=0, keepdims=True)
        dubuf[0:tl, :] = du
        dxu0 = jnp.zeros((tl, D_LRU), F32)
        for j in range(4):
            dsh = dubuf[pl.ds(3 - j, tl), :]
            dxu0 = dxu0 + cw_ref[j:j + 1, :] * dsh
            dcw_ref[j:j + 1, :] += jnp.sum(xu0 * dsh, axis=0, keepdims=True)
        dubuf[tl:tl + 8, :] = du[0:8, :]
        dxg_ref[:, 0:D_LRU] = dxu0.astype(BF16)

        @pl.when(i == nt - 1)
        def _():
            dlam_ref[...] = dlam_ref[...] * (-jax.nn.sigmoid(-lamv))
            for n in range(LRU_BLOCKS):
                blk = slice(n * LRU_BLOCK, (n + 1) * LRU_BLOCK)
                dwrg_ref[n] = wacc_r[blk, blk]
                dwig_ref[n] = wacc_i[blk, blk]

    def rev(n):
        return pl.BlockSpec((tl, n), lambda i: (nt - 1 - i, 0))

    prev8 = pl.BlockSpec((8, D_LRU), lambda i: (jnp.maximum((nt - 1 - i) * (tl // 8) - 1, 0), 0))
    vec = _full((1, D_LRU))
    sq = _full((D_LRU, D_LRU))
    blocks_shape = (LRU_BLOCKS, LRU_BLOCK, LRU_BLOCK)
    blocks = _full(blocks_shape)
    return _call(
        body, "b_lru", (nt,),
        [rev(D_LRU), rev(D_LRU), prev8, rev(D_LRU), rev(1024), _full((4, D_LRU)), sq, vec, sq, vec, vec],
        [rev(1024), blocks, blocks, vec, vec, vec, _full((4, D_LRU)), vec],
        [_sds((s_len, 1024), BF16), _sds(blocks_shape, F32), _sds(blocks_shape, F32),
         _sds((1, D_LRU), F32), _sds((1, D_LRU), F32), _sds((1, D_LRU), F32),
         _sds((4, D_LRU), F32), _sds((1, D_LRU), F32)],
        [pltpu.VMEM((tl + 8, D_LRU), F32)] * 3 + [pltpu.VMEM((tl, D_LRU), F32)] * 3
        + [pltpu.VMEM((8, D_LRU), F32)] + [pltpu.VMEM((D_LRU, D_LRU), F32)] * 2,
        (drec, hs, hs, u, xg, conv_w, wrg, brg, wig, big, lam), "arbitrary", comm)


def _b_attn(qkv_pad, att, datt, frow, comm=None):
    s_len = datt.shape[0]
    nb = s_len // QB
    n_pair = ATT_HEADS // 2
    pair_w = 2 * HEAD_DIM

    def body(q_ref, k0, k1, k2, v0, v1, v2, o_ref, do_ref, frow_ref, dq_ref, dkv_ref, dfrow_ref,
             bias_sc, dt_sc, acc_sc):
        t = pl.program_id(0)

        @pl.when(t == 0)
        def _():
            _bias_table(frow_ref, bias_sc)
            dt_sc[...] = jnp.zeros_like(dt_sc)
            acc_sc[...] = jnp.zeros_like(acc_sc)

        @pl.when(t < nb)
        def _():
            var = jnp.minimum(t, N_BIAS - 1)
            even = _even_lanes()
            for hp in range(n_pair):
                cs = slice(hp * pair_w, (hp + 1) * pair_w)
                qt = q_ref[:, cs]
                kts = [k0[:, cs], k1[:, cs], k2[:, cs]]
                vts = [v0[:, cs], v1[:, cs], v2[:, cs]]
                dot = do_ref[:, cs]
                dd = dot * o_ref[:, cs]
                qmt, dost, dsbs, pbs, dqs = [], [], [], [], []
                for e in range(2):
                    keep = even if e == 0 else jnp.logical_not(even)
                    qm = jnp.where(keep, qt, 0)
                    p = _att_probs(qm, kts, bias_sc[var, 2 * hp + e])
                    inv = 1.0 / jnp.sum(p, axis=-1, keepdims=True)
                    dos = jnp.where(keep, dot * inv, 0.0)
                    delta = jnp.sum(jnp.where(keep, dd, 0.0), axis=-1, keepdims=True) * inv
                    dp = jnp.concatenate([_dot_nt(dos.astype(BF16), v) for v in vts], axis=1)
                    ds = p * (dp - delta)
                    dt_sc[2 * hp + e] += ds
                    dsb = ds.astype(BF16)
                    dq = _dot(dsb[:, 0:QB], kts[0])
                    for j in (1, 2):
                        dq = dq + _dot(dsb[:, j * QB:(j + 1) * QB], kts[j])
                    dqs.append(dq)
                    dsbs.append(dsb)
                    pbs.append(p.astype(BF16))
                    qmt.append(qm.astype(F32).T.astype(BF16))
                    dost.append(dos.T.astype(BF16))
                dq_ref[:, cs] = (jnp.where(even, dqs[0], dqs[1]) * ATT_SCALE).astype(BF16)
                for j in range(3):
                    slot = (t + 1 + j) % 3
                    js = slice(j * QB, (j + 1) * QB)
                    acc_sc[slot, hp] += _dot(qmt[0], dsbs[0][:, js]) + _dot(qmt[1], dsbs[1][:, js])
                    acc_sc[slot, n_pair + hp] += _dot(dost[0], pbs[0][:, js]) + _dot(dost[1], pbs[1][:, js])

        done = (t + 1) % 3

        @pl.when(t >= 2)
        def _():
            for i in range(2 * n_pair):
                dkv_ref[:, i * pair_w:(i + 1) * pair_w] = acc_sc[done, i].T.astype(BF16)

        acc_sc[done] = jnp.zeros((2 * n_pair, pair_w, QB), F32)

        @pl.when(t == nb + 1)
        def _():
            row = lax.broadcasted_iota(jnp.int32, (8, ROLL_W), 0)
            pad = jnp.zeros((8, ROLL_W - KB), F32)
            for h in range(ATT_HEADS):
                acc8 = jnp.concatenate([dt_sc[h, 0:8, :], pad], axis=1)
                for a1 in range(1, QB // 8):
                    blk = jnp.concatenate([dt_sc[h, 8 * a1:8 * a1 + 8, :], pad], axis=1)
                    acc8 = acc8 + pltpu.roll(blk, ROLL_W - 8 * a1, 1)
                for k in range(3):
                    acc8 = jnp.where(((row >> k) & 1) == 1, pltpu.roll(acc8, ROLL_W - (1 << k), 1), acc8)
                dfrow_ref[h:h + 1, :] = jnp.sum(acc8, axis=0, keepdims=True)

    clamp = lambda t: jnp.minimum(t, nb - 1)
    qrows = pl.BlockSpec((QB, D_ATT), lambda t: (clamp(t), 0))
    return _call(
        body, "b_attn", (nb + 2,),
        _att_in_specs(clamp) + [qrows, qrows, _full((ATT_HEADS, ROLL_W))],
        [qrows, pl.BlockSpec((QB, 2 * D_ATT), lambda t: (jnp.maximum(t - 2, 0), 0)),
         _full((ATT_HEADS, ROLL_W))],
        [_sds((s_len, D_ATT), BF16), _sds((s_len, 2 * D_ATT), BF16), _sds((ATT_HEADS, ROLL_W), F32)],
        [pltpu.VMEM((N_BIAS, ATT_HEADS, QB, KB), F32), pltpu.VMEM((ATT_HEADS, QB, KB), F32),
         pltpu.VMEM((3, 2 * n_pair, pair_w, QB), F32)],
        (*([qkv_pad] * 7), att, datt, frow), "arbitrary", comm)


def _b_win(dq, dkv, dxg, h, ts):
    s_len = h.shape[0]
    steps = s_len // ts

    def body(dq_ref, dkv_ref, dxg_ref, h_ref, dw_ref, dwb_ref):
        @pl.when(pl.program_id(0) == 0)
        def _():
            dw_ref[...] = jnp.zeros_like(dw_ref)

        dproj = jnp.concatenate([dq_ref[...], dkv_ref[...], dxg_ref[...]], axis=1)
        hv = h_ref[...]
        for s in range(N_SHARD):
            dw_ref[s] += _dot_tn(hv, dproj[:, s * IN_SH:(s + 1) * IN_SH])

        @pl.when(pl.program_id(0) == steps - 1)
        def _():
            dwb_ref[...] = dw_ref[...].astype(BF16)

    wspec = _full((N_SHARD, 1024, IN_SH))
    return pl.pallas_call(
        body, name="b_win", grid=(steps,),
        in_specs=[_rows(ts, 512), _rows(ts, 1024), _rows(ts, 1024), _rows(ts, 1024)],
        out_specs=[wspec, wspec],
        out_shape=[_sds((N_SHARD, 1024, IN_SH), F32), _sds((N_SHARD, 1024, IN_SH), BF16)],
        compiler_params=_cp("arbitrary"))(dq, dkv, dxg, h)


def _b_inproj(dq, dkv, dxg, x, dx1, g_mix, w_in_g, tm, comm=None):
    s_len = x.shape[0]

    def body(dq_ref, dkv_ref, dxg_ref, x_ref, dx1_ref, g_ref, w_hbm, gx_ref, dgm_ref, w_ref):
        _load_w_in_once(w_hbm, w_ref)

        @pl.when(pl.program_id(0) == 0)
        def _():
            dgm_ref[...] = jnp.zeros_like(dgm_ref)

        dproj = jnp.concatenate([dq_ref[...], dkv_ref[...], dxg_ref[...]], axis=1)
        dh = _dot_nt(dproj, w_ref[...])
        dx, dgm = _rms_bwd(dh, x_ref[...], g_ref[...])
        gx_ref[...] = dx1_ref[...] + dx
        dgm_ref[...] += dgm

    return _call(
        body, "b_inproj", (s_len // tm,),
        [_rows(tm, 512), _rows(tm, 1024), _rows(tm, 1024), _rows(tm, 1024), _rows(tm, 1024),
         _full((1, 1024)), _any()],
        [_rows(tm, 1024), _full((1, 1024))],
        [_sds((s_len, 1024), F32), _sds((1, 1024), F32)],
        [pltpu.VMEM((1024, D_IN), BF16)], (dq, dkv, dxg, x, dx1, g_mix, w_in_g), "arbitrary", comm)


def _mm_tn(xa, ya, name, ts):
    s_len, k = xa.shape
    n = ya.shape[1]

    steps = s_len // ts

    def body(x_ref, y_ref, o_ref, ob_ref):
        @pl.when(pl.program_id(0) == 0)
        def _():
            o_ref[...] = jnp.zeros_like(o_ref)
        o_ref[...] += _dot_tn(x_ref[...].astype(BF16), y_ref[...].astype(BF16))

        @pl.when(pl.program_id(0) == steps - 1)
        def _():
            ob_ref[...] = o_ref[...].astype(BF16)

    return pl.pallas_call(
        body, name=name, grid=(steps,), in_specs=[_rows(ts, k), _rows(ts, n)],
        out_specs=[_full((k, n))] * 2, out_shape=[_sds((k, n), F32), _sds((k, n), BF16)],
        compiler_params=_cp("arbitrary"))(xa, ya)


def _frow_from_rel_bias(rb):
    hi = jnp.broadcast_to(rb[:, 256:257], (ATT_HEADS, 385))
    mid = rb[:, 1:256][:, ::-1]
    lo = jnp.broadcast_to(rb[:, 0:1], (ATT_HEADS, 128))
    wrap = jnp.broadcast_to(rb[:, 256:257], (ATT_HEADS, ROLL_W - KB))
    return jnp.concatenate([hi, mid, lo, wrap], axis=1)


def _rel_bias_grad_from_dfrow(df):
    g256 = jnp.sum(df[:, 0:385], axis=1, keepdims=True) + jnp.sum(df[:, KB:ROLL_W], axis=1, keepdims=True)
    mid = df[:, 385:640][:, ::-1]
    g0 = jnp.sum(df[:, 640:KB], axis=1, keepdims=True)
    return jnp.concatenate([g0, mid, g256], axis=1)


def _block_diag(w):
    eye = jnp.eye(8, dtype=w.dtype)
    return (w[:, :, None, :] * eye[:, None, :, None]).reshape(D_LRU, D_LRU)


MID = ['w_out', 'wq_c', 'wk_c', 'wv_c', 'wo_c']
TRANSPOSED = ['w_gate', 'w_up']
AG_IN_INPROJ = ['w_out', 'wq_c', 'wk_c']
AG_IN_ATTN = ['wv_c', 'wo_c', 'w_gate']
AG_IN_LRU = ['w_up']
AG_IN_MID = ['w_down']
RS_IN_MID = ['w_gate', 'w_up']
RS_IN_LRU = ['w_down']
RS_IN_ATTN = MID


def _local_step(x, mem, tgt, p, gw, shards=None, chip=None):
    s_len = x.shape[0]
    tm = min(256, s_len)
    tmb = min(512, s_len)
    tl = min(512, s_len)
    frow = _frow_from_rel_bias(p['rel_bias'])
    wrg = _block_diag(p['w_rg']).astype(BF16)
    wig = _block_diag(p['w_ig']).astype(BF16)
    gw = dict(gw)

    big, bigb, recv, part, sib = {}, {}, {}, {}, {}

    def ag(names):
        return [] if shards is None else [("ag", [shards[n] for n in names])]

    def rs(names):
        return [] if shards is None else [("rs", [bigb[n] for n in names])]

    def swap(names):
        return [] if shards is None else [("swap", [part[n] for n in names])]

    def reduce_own(names):
        if shards is not None:
            for n in names:
                part[n] = _sum_parts(big[n], recv[n], chip, "sum_" + n)

    h, qkv_pad, xg, *got = _f_inproj(x, p['g_mix'], gw['w_in'], tmb, ag(AG_IN_INPROJ))
    gw.update(zip(AG_IN_INPROJ, got))
    att, *got = _f_attn(qkv_pad, frow, ag(AG_IN_ATTN))
    gw.update(zip(AG_IN_ATTN, got))
    rec, u, hs, *got = _f_lru(xg, p['conv_w'], p['conv_b'], wrg, p['b_rg'], wig, p['b_ig'], p['lru_L'], tl,
                              ag(AG_IN_LRU))
    gw.update(zip(AG_IN_LRU, got))
    w_out = gw['w_out'].reshape(1024, 1024)
    wq = gw['wq_c'].reshape(1024, 1024)
    wk = gw['wk_c'].reshape(1024, 1024)
    wv = gw['wv_c'].reshape(1024, 1024)
    wo = gw['wo_c'].reshape(1024, 1024)
    mn, kx, vx = _f_mem(mem, p['g_mem'], wk, wv)
    mg, x1, hc, qx, ox, x2, *got = _f_mid(x, att, rec, p['g_out_attn'], p['g_out_lru'], w_out, p['g_cross'],
                                          wq, kx, vx, wo, tmb, ag(AG_IN_MID))
    gw.update(zip(AG_IN_MID, got))
    ffn_w = [gw[n].reshape(D_FF, 1024) for n in ('w_gate', 'w_up', 'w_down')]
    hf, gact, uact, aact, dx3, loss, dg_final = _f_ffn(x2, tgt, p['g_ffn'], p['g_final'], *ffn_w, tm)

    ts = min(512, s_len)
    dgact, duact, dx2, dg_ffn = _b_ffn(dx3, x2, gact, uact, p['g_ffn'], *ffn_w, tm)
    big['w_gate'], bigb['w_gate'] = _mm_tn(dgact, hf, "dw_gate", ts)
    big['w_up'], bigb['w_up'] = _mm_tn(duact, hf, "dw_up", ts)
    big['w_down'], bigb['w_down'] = _mm_tn(aact, dx3, "dw_down", ts)
    for n in ('w_gate', 'w_up', 'w_down'):
        big[n] = big[n].reshape(N_SHARD, FF_SH, 1024)
        bigb[n] = bigb[n].reshape(N_SHARD, FF_SH, 1024)

    dqx, dx1, datt, drec, dkx, dvx, dg_cross, dg_oa, dg_ol, *got = _b_mid(
        dx2, qx, x1, att, rec, kx, vx, wo, wq, w_out, p['g_cross'], p['g_out_attn'], p['g_out_lru'], tm,
        rs(RS_IN_MID))
    recv.update(zip(RS_IN_MID, got))
    reduce_own(RS_IN_MID)
    dwk, dwv, dg_mem, dwkb, dwvb = _b_mem(dkx, dvx, mem, mn, p['g_mem'], wk, wv)
    big['wk_c'], bigb['wk_c'] = dwk, dwkb
    big['wv_c'], bigb['wv_c'] = dwv, dwvb
    big['w_out'], bigb['w_out'] = _mm_tn(mg, dx1, "dw_out", ts)
    big['wq_c'], bigb['wq_c'] = _mm_tn(hc, dqx, "dw_q", ts)
    big['wo_c'], bigb['wo_c'] = _mm_tn(ox, dx2, "dw_o", ts)
    for n in MID:
        big[n] = big[n].reshape(N_SHARD, 256, 1024)
        bigb[n] = bigb[n].reshape(N_SHARD, 256, 1024)

    dxg, dwrg, dwig, dbrg, dbig, dlam, dcw, dcb, *got = _b_lru(
        drec, hs, u, xg, p['conv_w'], wrg, p['b_rg'], wig, p['b_ig'], p['lru_L'], tl,
        rs(RS_IN_LRU) + swap(RS_IN_MID))
    recv.update(zip(RS_IN_LRU, got))
    sib.update(zip(RS_IN_MID, got[len(RS_IN_LRU):]))
    reduce_own(RS_IN_LRU)
    dq, dkv, dfrow, *got = _b_attn(qkv_pad, att, datt, frow, rs(RS_IN_ATTN) + swap(RS_IN_LRU))
    recv.update(zip(RS_IN_ATTN, got))
    sib.update(zip(RS_IN_LRU, got[len(RS_IN_ATTN):]))
    reduce_own(RS_IN_ATTN)
    big['w_in'], bigb['w_in'] = _b_win(dq, dkv, dxg, h, ts)
    grad_x, dg_mix, *got = _b_inproj(dq, dkv, dxg, x, dx1, p['g_mix'], gw['w_in'], tmb,
                                     rs(['w_in']) + swap(RS_IN_ATTN))
    recv.update(zip(['w_in'], got))
    sib.update(zip(RS_IN_ATTN, got[1:]))
    reduce_own(['w_in'])
    small = {
        'g_mix': dg_mix, 'rel_bias': _rel_bias_grad_from_dfrow(dfrow), 'conv_w': dcw, 'conv_b': dcb,
        'w_rg': dwrg, 'b_rg': dbrg, 'w_ig': dwig, 'b_ig': dbig,
        'lru_L': dlam,
        'g_out_attn': dg_oa, 'g_out_lru': dg_ol, 'g_cross': dg_cross, 'g_mem': dg_mem, 'g_ffn': dg_ffn,
        'g_final': dg_final,
    }
    return loss, grad_x, small, big, part, sib


def _cast_shards(ws):
    def body(*refs):
        n = len(refs) // 2
        for src, dst in zip(refs[:n], refs[n:]):
            dst[...] = src[...].astype(BF16)

    return pl.pallas_call(body, name="cast_shards", out_shape=[_sds(w.shape, BF16) for w in ws],
                          compiler_params=_cp())(*ws)


def _sum_parts(own4, recv3, chip, name):
    _, r, c = own4.shape
    tr = r // 4

    def body(chip_ref, own_ref, rc_ref, o_ref):
        o_ref[...] = ((own_ref[0] + rc_ref[0].astype(F32)) + rc_ref[1].astype(F32)) + rc_ref[2].astype(F32)

    grid_spec = pltpu.PrefetchScalarGridSpec(
        num_scalar_prefetch=1, grid=(4,),
        in_specs=[pl.BlockSpec((1, tr, c), lambda i, ch: (ch[0], i, 0)),
                  pl.BlockSpec((3, tr, c), lambda i, ch: (0, i, 0))],
        out_specs=pl.BlockSpec((tr, c), lambda i, ch: (i, 0)))
    return pl.pallas_call(body, name=name, grid_spec=grid_spec, out_shape=_sds((r, c), F32),
                          compiler_params=_cp("parallel"))(chip, own4, recv3)


def _adamw_math(w, g, m, v):
    m = ADAM_B1 * m + (1.0 - ADAM_B1) * g
    v = ADAM_B2 * v + (1.0 - ADAM_B2) * (g * g)
    m_hat = m / (1.0 - ADAM_B1 ** ADAM_STEP)
    v_hat = v / (1.0 - ADAM_B2 ** ADAM_STEP)
    delta = -ADAM_LR * (m_hat / (jnp.sqrt(v_hat) + ADAM_EPS) + ADAM_WD * w)
    return delta, m, v


def _final_adamw(pa, pb, w, m, v, name):
    r, c = w.shape
    tr = r // 4

    def body(pa_ref, pb_ref, w_ref, m_ref, v_ref, g_ref, d_ref, nm_ref, nv_ref):
        g = pa_ref[...] + pb_ref[...]
        g_ref[...] = g
        d_ref[...], nm_ref[...], nv_ref[...] = _adamw_math(w_ref[...], g, m_ref[...], v_ref[...])

    return pl.pallas_call(
        body, name=name, grid=(4,), in_specs=[_rows(tr, c)] * 5, out_specs=[_rows(tr, c)] * 4,
        out_shape=[_sds((r, c), F32)] * 4, compiler_params=_cp("parallel"))(pa, pb, w, m, v)


def _pack_put(ref, name, val_ref):
    r = _pack_rows()[name]
    shape = val_ref.shape
    if len(shape) == 3:
        for b in range(shape[0]):
            ref[r:r + shape[1], b * shape[2]:(b + 1) * shape[2]] = val_ref[b]
    elif shape[1] == 2 * PACK_W:
        ref[r:r + 1, :] = val_ref[:, 0:PACK_W]
        ref[r + 1:r + 2, :] = val_ref[:, PACK_W:2 * PACK_W]
    else:
        ref[r:r + shape[0], 0:shape[1]] = val_ref[...]


def _pack_get(ref, name, shape):
    r = _pack_rows()[name]
    if len(shape) == 3:
        return jnp.stack([ref[r:r + shape[1], b * shape[2]:(b + 1) * shape[2]] for b in range(shape[0])])
    if shape[1] == 2 * PACK_W:
        return jnp.concatenate([ref[r:r + 1, :], ref[r + 1:r + 2, :]], axis=1)
    return ref[r:r + shape[0], 0:shape[1]]


def _ar_small(g, loss):
    n = len(g)

    def body(*refs):
        g_refs, loss_ref = refs[:n], refs[n]
        tot_ref, pack, buf, send_sems, recv_sems = refs[n + 1:]
        x, y, c = _mesh_pos()
        me = 4 * x + 2 * y + c

        def peer(k):
            px = 1 - x if k & 4 else x
            py = 1 - y if k & 2 else y
            pc = 1 - c if k & 1 else c
            return px, py, pc

        def remote(k, slot):
            return pltpu.make_async_remote_copy(
                src_ref=pack, dst_ref=buf.at[slot], send_sem=send_sems.at[k - 1], recv_sem=recv_sems.at[k - 1],
                device_id=peer(k), device_id_type=MESH_ID)

        pack[...] = jnp.zeros_like(pack)
        for a, name in enumerate(SMALL):
            _pack_put(pack, name, g_refs[a])
        _pack_put(pack, 'loss', loss_ref)
        for k in range(1, 8):
            remote(k, me).start()
        buf[me] = pack[...]
        for k in range(1, 8):
            px, py, pc = peer(k)
            remote(k, 4 * px + 2 * py + pc).wait_recv()
        for k in range(1, 8):
            remote(k, me).wait_send()
        tot = buf[0]
        for k in range(1, 8):
            tot = tot + buf[k]
        tot_ref[...] = tot

    return pl.pallas_call(
        body, name="ar_small", out_shape=_sds((PACK_ROWS, PACK_W), F32),
        scratch_shapes=[pltpu.VMEM((PACK_ROWS, PACK_W), F32), pltpu.VMEM((8, PACK_ROWS, PACK_W), F32),
                        pltpu.SemaphoreType.DMA((7,)), pltpu.SemaphoreType.DMA((7,))],
        compiler_params=_cp())(*g, loss)


def _adamw_small(tot, g_shapes, loss_shape, w, m, v):
    n = len(w)

    def body(*refs):
        tot_ref = refs[0]
        w_refs, m_refs, v_refs = (refs[1 + i * n:1 + (i + 1) * n] for i in range(3))
        o0 = 3 * n + 1
        go, do, mo, vo = (refs[o0 + i * n:o0 + (i + 1) * n] for i in range(4))
        loss_out = refs[o0 + 4 * n]
        x, y, _ = _mesh_pos()
        loss_out[...] = _pack_get(tot_ref, 'loss', loss_shape)
        for a, name in enumerate(SMALL):
            if name == 'conv_w':
                r = _pack_rows()[name]
                ga = tot_ref[r:r + g_shapes[a][0], pl.ds(pl.multiple_of((2 * x + y) * 128, 128), 128)]
            else:
                ga = _pack_get(tot_ref, name, g_shapes[a])
            go[a][...] = ga
            do[a][...], mo[a][...], vo[a][...] = _adamw_math(w_refs[a][...], ga, m_refs[a][...], v_refs[a][...])

    out_shape = [_sds(a.shape, F32) for a in w] * 4 + [_sds(loss_shape, F32)]
    return pl.pallas_call(body, name="adamw_small", out_shape=out_shape, compiler_params=_cp())(tot, *w, *m, *v)


PACK_W = 512
PACK_ROWS = 160


def _pack_rows():
    rows, r = {}, 0
    for name in ['g_mix', 'g_cross', 'g_mem', 'g_ffn', 'g_final']:
        rows[name] = r
        r += 2
    for name in ['conv_b', 'b_rg', 'b_ig', 'lru_L', 'g_out_attn', 'g_out_lru']:
        rows[name] = r
        r += 1
    rows['conv_w'] = r
    rows['loss'] = r + 4
    rows['rel_bias'] = 24
    rows['w_rg'] = 32
    rows['w_ig'] = 32 + LRU_BLOCK
    assert r + 5 <= 24 and rows['w_ig'] + LRU_BLOCK == PACK_ROWS
    return rows


INPUT_NAMES = (['x', 'mem'] + WEIGHTS + ['loss_target'] + ['m_' + n for n in WEIGHTS] + ['v_' + n for n in WEIGHTS])


def kernel(x, mem, g_mix, w_in, rel_bias, conv_w, conv_b, w_rg, b_rg, w_ig, b_ig, lru_L, g_out_attn, g_out_lru, w_out, g_cross, g_mem, wq_c, wk_c, wv_c, wo_c, g_ffn, w_gate, w_up, w_down, g_final, loss_target, m_g_mix, m_w_in, m_rel_bias, m_conv_w, m_conv_b, m_w_rg, m_b_rg, m_w_ig, m_b_ig, m_lru_L, m_g_out_attn, m_g_out_lru, m_w_out, m_g_cross, m_g_mem, m_wq_c, m_wk_c, m_wv_c, m_wo_c, m_g_ffn, m_w_gate, m_w_up, m_w_down, m_g_final, v_g_mix, v_w_in, v_rel_bias, v_conv_w, v_conv_b, v_w_rg, v_b_rg, v_w_ig, v_b_ig, v_lru_L, v_g_out_attn, v_g_out_lru, v_w_out, v_g_cross, v_g_mem, v_wq_c, v_wk_c, v_wv_c, v_wo_c, v_g_ffn, v_w_gate, v_w_up, v_w_down, v_g_final):
    a = dict(zip(INPUT_NAMES, (x, mem, g_mix, w_in, rel_bias, conv_w, conv_b, w_rg, b_rg, w_ig, b_ig, lru_L, g_out_attn, g_out_lru, w_out, g_cross, g_mem, wq_c, wk_c, wv_c, wo_c, g_ffn, w_gate, w_up, w_down, g_final, loss_target, m_g_mix, m_w_in, m_rel_bias, m_conv_w, m_conv_b, m_w_rg, m_b_rg, m_w_ig, m_b_ig, m_lru_L, m_g_out_attn, m_g_out_lru, m_w_out, m_g_cross, m_g_mem, m_wq_c, m_wk_c, m_wv_c, m_wo_c, m_g_ffn, m_w_gate, m_w_up, m_w_down, m_g_final, v_g_mix, v_w_in, v_rel_bias, v_conv_w, v_conv_b, v_w_rg, v_b_rg, v_w_ig, v_b_ig, v_lru_L, v_g_out_attn, v_g_out_lru, v_w_out, v_g_cross, v_g_mem, v_wq_c, v_wk_c, v_wv_c, v_wo_c, v_g_ffn, v_w_gate, v_w_up, v_w_down, v_g_final)))
    chip = 2 * lax.axis_index("x") + lax.axis_index("y")

    def shard(name):
        arr = a[name][0]
        return jnp.swapaxes(arr, 0, 1) if name[2:] in TRANSPOSED or name in TRANSPOSED else arr

    shards = dict(zip(BIG, _cast_shards([shard(n) for n in BIG])))
    w_in_g, conv_w_g = _comm_only("ag_w_in", [("ag", [shards['w_in']]), ("agf", [a['conv_w'][0]])])
    conv_w_full = conv_w_g.transpose(1, 0, 2).reshape(4, D_LRU)

    p = {n: a[n] for n in SMALL}
    p['rel_bias'] = a['rel_bias'][0]
    p['w_rg'] = a['w_rg'][0]
    p['w_ig'] = a['w_ig'][0]
    p['conv_w'] = conv_w_full
    p['g_final'] = a['g_final'][None, :]
    chip_arr = jnp.reshape(chip, (1,)).astype(jnp.int32)
    loss_part, grad_x, small, _, part, sib = _local_step(
        a['x'][0], a['mem'][0], a['loss_target'][0], p, {'w_in': w_in_g}, shards, chip_arr)

    sib['w_in'], = _comm_only("swap_w_in", [("swap", [part['w_in']])])
    out = {}
    for n in BIG:
        res = _final_adamw(part[n], sib[n], shard(n), shard('m_' + n), shard('v_' + n), "adamw_" + n)
        out[n] = [jnp.swapaxes(r, 0, 1) for r in res] if n in TRANSPOSED else res

    def natural(arr):
        return arr[0] if arr.ndim >= 3 else (arr[None, :] if arr.ndim == 1 else arr)

    small_g = [small[n] for n in SMALL]
    small_out = _adamw_small(_ar_small(small_g, loss_part), [g.shape for g in small_g], loss_part.shape,
                             *[[natural(a[pre + n]) for n in SMALL] for pre in ('', 'm_', 'v_')])
    ns = len(SMALL)
    loss = small_out[4 * ns][0, 0]

    def leaf(i, n):
        if n in BIG:
            return out[n][i][None]
        return small_out[i * ns + SMALL.index(n)].reshape(a[n].shape)

    return (loss, grad_x[None], *[leaf(i, n) for i in range(4) for n in WEIGHTS])
```

```python
import math

import jax
import jax.numpy as jnp
from jax import lax
from jax.experimental import pallas as pl
from jax.experimental.pallas import tpu as pltpu

F32 = jnp.float32
BF16 = jnp.bfloat16

D_MODEL = 1024
D_ATT = 512
D_LRU = 512
HEAD_DIM = 64
ATT_HEADS = 8
CHUNK = 64
LEFT_CHUNKS = 8
X_HEADS = 4
X_HEAD_DIM = 256
N_SHARD = 4
IN_SH = 640
D_IN = N_SHARD * IN_SH
FF_SH = 704
D_FF = N_SHARD * FF_SH
EPS = 1e-6
LRU_C = 8.0
LRU_BLOCKS = 8
LRU_BLOCK = 64
QB = 256
KB = 768
ROLL_W = 1024
NEG = -1e30
ATT_SCALE = HEAD_DIM ** -0.5
X_SCALE = X_HEAD_DIM ** -0.5

ADAM_LR = 0.001
ADAM_B1 = 0.9
ADAM_B2 = 0.999
ADAM_EPS = 1e-08
ADAM_WD = 0.01
ADAM_STEP = 10

VMEM_LIMIT_V7X = 56 * 1024 * 1024
MESH_ID = pl.DeviceIdType.MESH

WEIGHTS = ['g_mix', 'w_in', 'rel_bias', 'conv_w', 'conv_b', 'w_rg', 'b_rg', 'w_ig', 'b_ig', 'lru_L',
           'g_out_attn', 'g_out_lru', 'w_out', 'g_cross', 'g_mem', 'wq_c', 'wk_c', 'wv_c', 'wo_c',
           'g_ffn', 'w_gate', 'w_up', 'w_down', 'g_final']
BIG = ['w_in', 'w_out', 'wq_c', 'wk_c', 'wv_c', 'wo_c', 'w_gate', 'w_up', 'w_down']
SMALL = [n for n in WEIGHTS if n not in BIG]


def _sds(shape, dtype):
    return jax.ShapeDtypeStruct(shape, dtype)


def _cp(*sem):
    return pltpu.CompilerParams(dimension_semantics=sem or None, vmem_limit_bytes=VMEM_LIMIT_V7X)


def _rows(tm, n):
    return pl.BlockSpec((tm, n), lambda i: (i, 0))


def _full(shape):
    nd = len(shape)
    return pl.BlockSpec(shape, lambda i: (0,) * nd)


def _dot(a, b):
    return jnp.dot(a, b, preferred_element_type=F32)


def _dot_nt(a, b):
    return lax.dot_general(a, b, (((1,), (1,)), ((), ())), preferred_element_type=F32)


def _dot_tn(a, b):
    return lax.dot_general(a, b, (((0,), (0,)), ((), ())), preferred_element_type=F32)


def _rinv(x):
    return lax.rsqrt(jnp.mean(x * x, axis=-1, keepdims=True) + EPS)


def _rms_bwd(dy, x, g):
    r = _rinv(x)
    yh = x * r
    dyh = dy * g
    dx = r * (dyh - yh * jnp.mean(dyh * yh, axis=-1, keepdims=True))
    return dx, jnp.sum(dy * yh, axis=0, keepdims=True)


def _gelu(x):
    c = math.sqrt(2.0 / math.pi)
    t = jnp.tanh(c * (x + 0.044715 * x * x * x))
    return 0.5 * x * (1.0 + t)


def _gelu_and_grad(x):
    c = math.sqrt(2.0 / math.pi)
    t = jnp.tanh(c * (x + 0.044715 * x * x * x))
    g = 0.5 * x * (1.0 + t)
    dg = 0.5 * (1.0 + t) + 0.5 * x * (1.0 - t * t) * c * (1.0 + 3.0 * 0.044715 * x * x)
    return g, dg


def _neg_expm1(z):
    series = -z * (1 + z / 2 * (1 + z / 3 * (1 + z / 4)))
    return jnp.where(z > -0.03, series, 1.0 - jnp.exp(z))


def _lru_gates(u, wrg, brg, wig, big, lam):
    ub = u.astype(BF16)
    r = jax.nn.sigmoid(_dot(ub, wrg) + brg)
    ig = jax.nn.sigmoid(_dot(ub, wig) + big)
    sp = jnp.maximum(-lam, 0.0) + jnp.log1p(jnp.exp(-jnp.abs(lam)))
    la = -LRU_C * r * sp
    a = jnp.exp(la)
    mult = jnp.sqrt(jnp.maximum(_neg_expm1(2.0 * la), 0.0))
    return ub, r, ig, sp, a, mult


def _scan8(a8, b8, hprev):
    row = lax.broadcasted_iota(jnp.int32, a8.shape, 0)
    aa, bb = a8, b8
    for d in (1, 2, 4):
        a_s = pltpu.roll(aa, d, 0)
        b_s = pltpu.roll(bb, d, 0)
        m = row >= d
        bb = jnp.where(m, aa * b_s + bb, bb)
        aa = jnp.where(m, aa * a_s, aa)
    return aa * hprev + bb


def _mesh_pos():
    return lax.axis_index("x"), lax.axis_index("y"), lax.axis_index("c")


def _other_chips(x, y):
    return [(1 - x, y), (x, 1 - y), (1 - x, 1 - y)]


def _no_forward():
    pass


def _ag_full_copies(ins, outs, sems):
    send_sems, recv_sems, loc_sems = sems
    n = len(ins)
    x, y, c = _mesh_pos()
    mine = 2 * x + y
    chips = _other_chips(x, y)

    def remote(k, j, slot):
        px, py = chips[j]
        return pltpu.make_async_remote_copy(
            src_ref=ins[k], dst_ref=outs[k].at[slot], send_sem=send_sems.at[k, j], recv_sem=recv_sems.at[k, j],
            device_id=(px, py, c), device_id_type=MESH_ID)

    def local(k):
        return pltpu.make_async_copy(ins[k], outs[k].at[mine], loc_sems.at[k])

    def start():
        for k in range(n):
            local(k).start()
            for j in range(3):
                remote(k, j, mine).start()

    def wait():
        for k in range(n):
            for j, (px, py) in enumerate(chips):
                remote(k, j, 2 * px + py).wait_recv()
        for k in range(n):
            for j in range(3):
                remote(k, j, mine).wait_send()
            local(k).wait()

    return start, _no_forward, wait


def _ag_copies(ins, outs, sems):
    send_sems, recv_sems, fsend_sems, frecv_sems, loc_sems = sems
    n = len(ins)
    x, y, c = _mesh_pos()
    mine = 2 * x + y
    chips = _other_chips(x, y)

    def half(ref, hc):
        r = ref.shape[0] // 2
        return ref.at[pl.ds(pl.multiple_of(hc * r, 16), r)]

    def ici(k, j, slot):
        px, py = chips[j]
        return pltpu.make_async_remote_copy(
            src_ref=half(ins[k], c), dst_ref=half(outs[k].at[slot], c),
            send_sem=send_sems.at[k, j], recv_sem=recv_sems.at[k, j],
            device_id=(px, py, c), device_id_type=MESH_ID)

    def d2d(k, j, hc):
        px, py = chips[j]
        part = half(outs[k].at[2 * px + py], hc)
        return pltpu.make_async_remote_copy(
            src_ref=part, dst_ref=part, send_sem=fsend_sems.at[k, j], recv_sem=frecv_sems.at[k, j],
            device_id=(x, y, 1 - c), device_id_type=MESH_ID)

    def local(k):
        return pltpu.make_async_copy(ins[k], outs[k].at[mine], loc_sems.at[k])

    def start():
        for k in range(n):
            local(k).start()
            for j in range(3):
                ici(k, j, mine).start()

    def forward():
        for k in range(n):
            for j, (px, py) in enumerate(chips):
                ici(k, j, 2 * px + py).wait_recv()
                d2d(k, j, c).start()

    def wait():
        for k in range(n):
            for j in range(3):
                d2d(k, j, 1 - c).wait_recv()
        for k in range(n):
            for j in range(3):
                d2d(k, j, c).wait_send()
                ici(k, j, mine).wait_send()
            local(k).wait()

    return start, forward, wait


def _rs_copies(ins, outs, sems):
    send_sems, recv_sems = sems
    n = len(ins)
    x, y, c = _mesh_pos()
    chips = _other_chips(x, y)

    def remote(k, j):
        px, py = chips[j]
        return pltpu.make_async_remote_copy(
            src_ref=ins[k].at[2 * px + py], dst_ref=outs[k].at[j],
            send_sem=send_sems.at[k, j], recv_sem=recv_sems.at[k, j],
            device_id=(px, py, c), device_id_type=MESH_ID)

    def start():
        for k in range(n):
            for j in range(3):
                remote(k, j).start()

    def wait():
        for k in range(n):
            for j in range(3):
                remote(k, j).wait_recv()
        for k in range(n):
            for j in range(3):
                remote(k, j).wait_send()

    return start, _no_forward, wait


def _swap_copies(ins, outs, sems):
    send_sems, recv_sems = sems
    x, y, c = _mesh_pos()
    copies = [pltpu.make_async_remote_copy(
        src_ref=ins[k], dst_ref=outs[k], send_sem=send_sems.at[k], recv_sem=recv_sems.at[k],
        device_id=(x, y, 1 - c), device_id_type=MESH_ID) for k in range(len(ins))]

    def start():
        for cp in copies:
            cp.start()

    def wait():
        for cp in copies:
            cp.wait()

    return start, _no_forward, wait


def _comm_plan(groups):
    plan, arrs, shapes, sems = [], [], [], []
    for kind, group in groups:
        k = len(group)
        arrs += group
        per_peer = pltpu.SemaphoreType.DMA((k, 3))
        if kind == "ag":
            shapes += [_sds((N_SHARD,) + w.shape, w.dtype) for w in group]
            gsems = [per_peer] * 4 + [pltpu.SemaphoreType.DMA((k,))]
            maker = _ag_copies
        elif kind == "agf":
            shapes += [_sds((N_SHARD,) + w.shape, w.dtype) for w in group]
            gsems = [per_peer] * 2 + [pltpu.SemaphoreType.DMA((k,))]
            maker = _ag_full_copies
        elif kind == "rs":
            shapes += [_sds((3,) + g.shape[1:], g.dtype) for g in group]
            gsems = [pltpu.SemaphoreType.DMA((k, 3)), pltpu.SemaphoreType.DMA((k, 3))]
            maker = _rs_copies
        else:
            shapes += [_sds(g.shape, g.dtype) for g in group]
            gsems = [pltpu.SemaphoreType.DMA((k,)), pltpu.SemaphoreType.DMA((k,))]
            maker = _swap_copies
        plan.append((maker, k, len(gsems)))
        sems += gsems
    return plan, arrs, shapes, sems


def _comm_fns(plan, cins, couts, sems):
    fns, a, s = [], 0, 0
    for maker, k, ns in plan:
        fns.append(maker(cins[a:a + k], couts[a:a + k], sems[s:s + ns]))
        a += k
        s += ns

    def start():
        for st, _, _ in fns:
            st()

    def forward():
        for _, fw, _ in fns:
            fw()

    def wait():
        for _, _, wt in fns:
            wt()

    return start, forward, wait


def _call(body, name, grid, in_specs, out_specs, out_shape, scratch, args, sem, comm=None):
    if not comm:
        return pl.pallas_call(body, name=name, grid=grid, in_specs=in_specs, out_specs=out_specs,
                              out_shape=out_shape, scratch_shapes=scratch, compiler_params=_cp(sem))(*args)
    plan, c_arrs, c_shapes, c_sems = _comm_plan(comm)
    k = len(c_arrs)
    n_in, n_out, n_scr = len(in_specs), len(out_specs), len(scratch)
    last = grid[0] - 1
    fwd_step = max(1, (2 * last) // 3)

    def wrapped(*refs):
        ins, cins = refs[:n_in], refs[n_in:n_in + k]
        o0 = n_in + k
        outs, couts = refs[o0:o0 + n_out], refs[o0 + n_out:o0 + n_out + k]
        s0 = o0 + n_out + k
        start, forward, wait = _comm_fns(plan, cins, couts, refs[s0 + n_scr:])
        pl.when(pl.program_id(0) == 0)(start)
        pl.when(pl.program_id(0) == fwd_step)(forward)
        body(*ins, *outs, *refs[s0:s0 + n_scr])
        pl.when(pl.program_id(0) == last)(wait)

    return pl.pallas_call(
        wrapped, name=name, grid=grid, in_specs=list(in_specs) + [_any()] * k,
        out_specs=list(out_specs) + [_any()] * k, out_shape=list(out_shape) + c_shapes,
        scratch_shapes=list(scratch) + c_sems, compiler_params=_cp(sem))(*args, *c_arrs)


def _comm_only(name, comm):
    plan, c_arrs, c_shapes, c_sems = _comm_plan(comm)
    k = len(c_arrs)

    def body(*refs):
        start, forward, wait = _comm_fns(plan, refs[:k], refs[k:2 * k], refs[2 * k:])
        start()
        forward()
        wait()

    return pl.pallas_call(body, name=name, in_specs=[_any()] * k, out_specs=[_any()] * k, out_shape=c_shapes,
                          scratch_shapes=c_sems, compiler_params=_cp())(*c_arrs)


def _any():
    return pl.BlockSpec(memory_space=pl.ANY)


def _rscan8(c8, d8, lnext):
    row = lax.broadcasted_iota(jnp.int32, c8.shape, 0)
    cc, dd = c8, d8
    for d in (1, 2, 4):
        c_s = pltpu.roll(cc, 8 - d, 0)
        d_s = pltpu.roll(dd, 8 - d, 0)
        m = row < 8 - d
        dd = jnp.where(m, cc * d_s + dd, dd)
        cc = jnp.where(m, cc * c_s, cc)
    return cc * lnext + dd


def _load_w_in_once(w_hbm, w_ref):
    @pl.when(pl.program_id(0) == 0)
    def _():
        for s in range(N_SHARD):
            pltpu.sync_copy(w_hbm.at[s], w_ref.at[:, pl.ds(s * IN_SH, IN_SH)])


def _f_inproj(x, g_mix, w_in_g, tm, comm=None):
    s_len = x.shape[0]
    pad_rows = LEFT_CHUNKS * CHUNK
    npad = pad_rows // tm

    def body(x_ref, g_ref, w_hbm, h_ref, qkv_ref, xg_ref, w_ref):
        i = pl.program_id(0)
        _load_w_in_once(w_hbm, w_ref)

        @pl.when(i < npad)
        def _():
            qkv_ref[...] = jnp.zeros_like(qkv_ref)

        @pl.when(i >= npad)
        def _():
            xv = x_ref[...]
            h = (xv * _rinv(xv) * g_ref[...]).astype(BF16)
            h_ref[...] = h
            proj = _dot(h, w_ref[...])
            qkv_ref[:, 0:D_ATT] = (proj[:, 0:D_ATT] * ATT_SCALE).astype(BF16)
            qkv_ref[:, D_ATT:3 * D_ATT] = proj[:, D_ATT:3 * D_ATT].astype(BF16)
            xg_ref[...] = proj[:, 3 * D_ATT:D_IN]

    def tok(n):
        return pl.BlockSpec((tm, n), lambda i: (jnp.maximum(i - npad, 0), 0))

    return _call(
        body, "f_inproj", (s_len // tm + npad,),
        [tok(1024), _full((1, 1024)), _any()],
        [tok(1024), _rows(tm, 1536), tok(1024)],
        [_sds((s_len, 1024), BF16), _sds((s_len + pad_rows, 1536), BF16), _sds((s_len, 1024), F32)],
        [pltpu.VMEM((1024, D_IN), BF16)], (x, g_mix, w_in_g), "arbitrary", comm)


N_BIAS = 3


def _bias_table(frow_ref, bias_sc):
    qa = lax.broadcasted_iota(jnp.int32, (QB, KB), 0) // CHUNK
    kcol = lax.broadcasted_iota(jnp.int32, (QB, KB), 1)
    kb = kcol // CHUNK
    band = jnp.where((kb >= qa) & (kb - qa <= LEFT_CHUNKS), 0.0, NEG).astype(F32)
    for h in range(ATT_HEADS):
        row = jnp.broadcast_to(frow_ref[h:h + 1, :], (QB, ROLL_W))
        toep = pltpu.roll(row, 0, 1, stride=1, stride_axis=0)
        gen = toep[:, 0:KB] + band
        bias_sc[N_BIAS - 1, h] = gen
        for v in range(N_BIAS - 1):
            pad_keys = LEFT_CHUNKS * CHUNK - v * QB
            bias_sc[v, h] = gen + jnp.where(kcol < pad_keys, NEG, 0.0).astype(F32)


def _even_lanes():
    return lax.broadcasted_iota(jnp.int32, (1, 2 * HEAD_DIM), 1) < HEAD_DIM


def _att_probs(qm, kts, bias):
    s = jnp.concatenate([_dot_nt(qm, k) for k in kts], axis=1) + bias
    return jnp.exp(s - jnp.max(s, axis=-1, keepdims=True))


def _att_in_specs(clamp):
    def spec(j, col):
        return pl.BlockSpec((QB, D_ATT), lambda i: (clamp(i) + j, col))
    return [spec(2, 0), spec(0, 1), spec(1, 1), spec(2, 1), spec(0, 2), spec(1, 2), spec(2, 2)]


def _f_attn(qkv_pad, frow, comm=None):
    s_len = qkv_pad.shape[0] - LEFT_CHUNKS * CHUNK
    nb = s_len // QB

    def body(q_ref, k0, k1, k2, v0, v1, v2, frow_ref, o_ref, bias_sc):
        i = pl.program_id(0)

        @pl.when(i == 0)
        def _():
            _bias_table(frow_ref, bias_sc)

        var = jnp.minimum(i, N_BIAS - 1)
        even = _even_lanes()
        for hp in range(ATT_HEADS // 2):
            cs = slice(hp * 2 * HEAD_DIM, (hp + 1) * 2 * HEAD_DIM)
            qt = q_ref[:, cs]
            kts = [k0[:, cs], k1[:, cs], k2[:, cs]]
            vts = [v0[:, cs], v1[:, cs], v2[:, cs]]
            res = []
            for e in range(2):
                keep = even if e == 0 else jnp.logical_not(even)
                pb = _att_probs(jnp.where(keep, qt, 0), kts, bias_sc[var, 2 * hp + e]).astype(BF16)
                r = _dot(pb[:, 0:QB], jnp.where(keep, vts[0], 1))
                for j in (1, 2):
                    r = r + _dot(pb[:, j * QB:(j + 1) * QB], jnp.where(keep, vts[j], 1))
                res.append(r / pltpu.roll(r, HEAD_DIM, 1))
            o_ref[:, cs] = jnp.where(even, res[0], res[1])

    return _call(
        body, "f_attn", (nb,),
        _att_in_specs(lambda i: i) + [_full((ATT_HEADS, ROLL_W))],
        [_rows(QB, D_ATT)], [_sds((s_len, D_ATT), F32)],
        [pltpu.VMEM((N_BIAS, ATT_HEADS, QB, KB), F32)], (*([qkv_pad] * 7), frow), "arbitrary", comm)


def _f_lru(xg, conv_w, conv_b, wrg, brg, wig, big, lam, tl, comm=None):
    s_len = xg.shape[0]

    def body(xg_ref, cw_ref, cb_ref, wrg_ref, brg_ref, wig_ref, big_ref, l_ref,
             rec_ref, u_ref, hs_ref, xbuf, a_sc, b_sc, hcar):
        i = pl.program_id(0)

        @pl.when(i == 0)
        def _():
            xbuf[0:8, :] = jnp.zeros((8, D_LRU), F32)
            hcar[...] = jnp.zeros((8, D_LRU), F32)

        xu0 = xg_ref[:, 0:D_LRU]
        xbuf[8:8 + tl, :] = xu0
        u = cb_ref[...] + cw_ref[0:1, :] * xbuf[pl.ds(5, tl), :]
        for j in range(1, 4):
            u = u + cw_ref[j:j + 1, :] * xbuf[pl.ds(5 + j, tl), :]
        xbuf[0:8, :] = xu0[tl - 8:tl, :]
        u_ref[...] = u
        _, _, ig, _, a, mult = _lru_gates(u, wrg_ref[...], brg_ref[...], wig_ref[...], big_ref[...], l_ref[...])
        a_sc[...] = a
        b_sc[...] = mult * (ig * u)

        def grp(g, hprev):
            off = pl.multiple_of(g * 8, 8)
            h8 = _scan8(a_sc[pl.ds(off, 8), :], b_sc[pl.ds(off, 8), :], hprev)
            hs_ref[pl.ds(off, 8), :] = h8
            return h8[7:8, :]

        hcar[0:1, :] = lax.fori_loop(0, tl // 8, grp, hcar[0:1, :])
        rec_ref[...] = hs_ref[...] * _gelu(xg_ref[:, D_LRU:2 * D_LRU])

    vec = _full((1, D_LRU))
    return _call(
        body, "f_lru", (s_len // tl,),
        [_rows(tl, 1024), _full((4, D_LRU)), vec, _full((D_LRU, D_LRU)), vec, _full((D_LRU, D_LRU)), vec, vec],
        [_rows(tl, D_LRU)] * 3, [_sds((s_len, D_LRU), F32)] * 3,
        [pltpu.VMEM((tl + 8, D_LRU), F32), pltpu.VMEM((tl, D_LRU), F32),
         pltpu.VMEM((tl, D_LRU), F32), pltpu.VMEM((8, D_LRU), F32)],
        (xg, conv_w, conv_b, wrg, brg, wig, big, lam), "arbitrary", comm)


def _f_mem(mem, g_mem, wk, wv):
    def body(mem_ref, g_ref, wk_ref, wv_ref, mn_ref, kx_ref, vx_ref):
        mv = mem_ref[...]
        mn = (mv * _rinv(mv) * g_ref[...]).astype(BF16)
        mn_ref[...] = mn
        kx_ref[...] = _dot(mn, wk_ref[...]).astype(BF16)
        vx_ref[...] = _dot(mn, wv_ref[...]).astype(BF16)

    m = mem.shape[0]
    return pl.pallas_call(
        body, name="f_mem", out_shape=[_sds((m, 1024), BF16)] * 3,
        compiler_params=_cp())(mem, g_mem, wk, wv)


def _xattn_probs(q, k):
    s = _dot_nt(q, k) * X_SCALE
    m = jnp.max(s, axis=-1, keepdims=True)
    p = jnp.exp(s - m)
    return p, jnp.sum(p, axis=-1, keepdims=True)


def _f_mid(x, att, rec, g_oa, g_ol, w_out, g_cross, wq, kx, vx, wo, tm, comm=None):
    s_len = x.shape[0]
    m_len = kx.shape[0]

    def body(x_ref, att_ref, rec_ref, goa_ref, gol_ref, wout_ref, gc_ref, wq_ref, kx_ref, vx_ref, wo_ref,
             mg_ref, x1_ref, hc_ref, qx_ref, ox_ref, x2_ref):
        av = att_ref[...]
        rv = rec_ref[...]
        mg_ref[:, 0:D_ATT] = (av * _rinv(av) * goa_ref[...]).astype(BF16)
        mg_ref[:, D_ATT:1024] = (rv * _rinv(rv) * gol_ref[...]).astype(BF16)
        x1 = x_ref[...] + _dot(mg_ref[...], wout_ref[...])
        x1_ref[...] = x1
        hc = (x1 * _rinv(x1) * gc_ref[...]).astype(BF16)
        hc_ref[...] = hc
        qx_ref[...] = _dot(hc, wq_ref[...]).astype(BF16)
        for h in range(X_HEADS):
            sl = slice(h * X_HEAD_DIM, (h + 1) * X_HEAD_DIM)
            p, l = _xattn_probs(qx_ref[:, sl], kx_ref[:, sl])
            ox_ref[:, sl] = (_dot(p.astype(BF16), vx_ref[:, sl]) / l).astype(BF16)
        x2_ref[...] = x1 + _dot(ox_ref[...], wo_ref[...])

    sq = _full((1024, 1024))
    return _call(
        body, "f_mid", (s_len // tm,),
        [_rows(tm, 1024), _rows(tm, 512), _rows(tm, 512), _full((1, 512)), _full((1, 512)), sq,
         _full((1, 1024)), sq, _full((m_len, 1024)), _full((m_len, 1024)), sq],
        [_rows(tm, 1024)] * 6,
        [_sds((s_len, 1024), BF16), _sds((s_len, 1024), F32), _sds((s_len, 1024), BF16),
         _sds((s_len, 1024), BF16), _sds((s_len, 1024), BF16), _sds((s_len, 1024), F32)],
        [], (x, att, rec, g_oa, g_ol, w_out, g_cross, wq, kx, vx, wo), "arbitrary", comm)


def _load_weights_once(pairs):
    @pl.when(pl.program_id(0) == 0)
    def _():
        for hbm, vmem in pairs:
            pltpu.sync_copy(hbm, vmem)


FF_CHUNKS = [(0, 1280), (1280, D_FF)]


def _f_ffn(x2, tgt, g_ffn, g_final, wg, wu, wd, tm):
    s_len = x2.shape[0]

    def body(x2_ref, t_ref, gf_ref, gfin_ref, wg_hbm, wu_hbm, wd_hbm,
             hf_ref, g_ref, u_ref, a_ref, dx3_ref, loss_ref, dgfin_ref, wg_ref, wu_ref, wd_ref):
        _load_weights_once([(wg_hbm, wg_ref), (wu_hbm, wu_ref), (wd_hbm, wd_ref)])

        @pl.when(pl.program_id(0) == 0)
        def _():
            loss_ref[...] = jnp.zeros_like(loss_ref)
            dgfin_ref[...] = jnp.zeros_like(dgfin_ref)

        x2v = x2_ref[...]
        hf = (x2v * _rinv(x2v) * gf_ref[...]).astype(BF16)
        hf_ref[...] = hf
        x3 = x2v
        for c0, c1 in FF_CHUNKS:
            gv = _dot_nt(hf, wg_ref[c0:c1, :])
            uv = _dot_nt(hf, wu_ref[c0:c1, :])
            av = (gv * jax.nn.sigmoid(gv) * uv).astype(BF16)
            g_ref[:, c0:c1] = gv.astype(BF16)
            u_ref[:, c0:c1] = uv.astype(BF16)
            a_ref[:, c0:c1] = av
            x3 = x3 + _dot(av, wd_ref[c0:c1, :])
        r3 = _rinv(x3)
        yh = x3 * r3
        gfin = gfin_ref[...]
        err = yh * gfin - t_ref[...]
        loss_ref[...] += jnp.full((1, 128), 0.5 / D_MODEL, F32) * jnp.sum(err * err)
        dy = err * (1.0 / D_MODEL)
        dgfin_ref[...] += jnp.sum(dy * yh, axis=0, keepdims=True)
        dyh = dy * gfin
        dx3_ref[...] = r3 * (dyh - yh * jnp.mean(dyh * yh, axis=-1, keepdims=True))

    vec = _full((1, 1024))
    return pl.pallas_call(
        body, name="f_ffn", grid=(s_len // tm,),
        in_specs=[_rows(tm, 1024), _rows(tm, 1024), vec, vec, _any(), _any(), _any()],
        out_specs=[_rows(tm, 1024), _rows(tm, D_FF), _rows(tm, D_FF), _rows(tm, D_FF),
                   _rows(tm, 1024), _full((1, 128)), vec],
        out_shape=[_sds((s_len, 1024), BF16)] + [_sds((s_len, D_FF), BF16)] * 3
                  + [_sds((s_len, 1024), F32), _sds((1, 128), F32), _sds((1, 1024), F32)],
        scratch_shapes=[pltpu.VMEM((D_FF, 1024), BF16)] * 3,
        compiler_params=_cp("arbitrary"))(x2, tgt, g_ffn, g_final, wg, wu, wd)


def _b_ffn(dx3, x2, gact, uact, g_ffn, wg, wu, wd, tm):
    s_len = x2.shape[0]

    def body(dx3_ref, x2_ref, g_ref, u_ref, gf_ref, wg_hbm, wu_hbm, wd_hbm,
             dg_ref, du_ref, dx2_ref, dgf_ref, wg_ref, wu_ref, wd_ref):
        _load_weights_once([(wg_hbm, wg_ref), (wu_hbm, wu_ref), (wd_hbm, wd_ref)])

        @pl.when(pl.program_id(0) == 0)
        def _():
            dgf_ref[...] = jnp.zeros_like(dgf_ref)

        dx3v = dx3_ref[...]
        dx3b = dx3v.astype(BF16)
        dhf = jnp.zeros(dx3v.shape, F32)
        for c0, c1 in FF_CHUNKS:
            da = _dot_nt(dx3b, wd_ref[c0:c1, :])
            gv = g_ref[:, c0:c1].astype(F32)
            uv = u_ref[:, c0:c1].astype(F32)
            sg = jax.nn.sigmoid(gv)
            dub = (da * gv * sg).astype(BF16)
            dgb = (da * uv * (sg * (1.0 + gv * (1.0 - sg)))).astype(BF16)
            du_ref[:, c0:c1] = dub
            dg_ref[:, c0:c1] = dgb
            dhf = dhf + _dot(dgb, wg_ref[c0:c1, :]) + _dot(dub, wu_ref[c0:c1, :])
        dx, dgf = _rms_bwd(dhf, x2_ref[...], gf_ref[...])
        dx2_ref[...] = dx3v + dx
        dgf_ref[...] += dgf

    vec = _full((1, 1024))
    return pl.pallas_call(
        body, name="b_ffn", grid=(s_len // tm,),
        in_specs=[_rows(tm, 1024), _rows(tm, 1024), _rows(tm, D_FF), _rows(tm, D_FF), vec,
                  _any(), _any(), _any()],
        out_specs=[_rows(tm, D_FF), _rows(tm, D_FF), _rows(tm, 1024), vec],
        out_shape=[_sds((s_len, D_FF), BF16)] * 2 + [_sds((s_len, 1024), F32), _sds((1, 1024), F32)],
        scratch_shapes=[pltpu.VMEM((D_FF, 1024), BF16)] * 3,
        compiler_params=_cp("arbitrary"))(dx3, x2, gact, uact, g_ffn, wg, wu, wd)


def _b_mid(dx2, qx, x1, att, rec, kx, vx, wo, wq, w_out, g_cross, g_oa, g_ol, tm, comm=None):
    s_len = x1.shape[0]
    m_len = kx.shape[0]

    def body(dx2_ref, qx_ref, x1_ref, att_ref, rec_ref, kx_ref, vx_ref, wo_ref, wq_ref, wout_ref,
             gc_ref, goa_ref, gol_ref,
             dqx_ref, dx1_ref, datt_ref, drec_ref, dkx_ref, dvx_ref, dgc_ref, dgoa_ref, dgol_ref):
        @pl.when(pl.program_id(0) == 0)
        def _():
            for r in (dkx_ref, dvx_ref, dgc_ref, dgoa_ref, dgol_ref):
                r[...] = jnp.zeros_like(r)

        dx2v = dx2_ref[...]
        dox = _dot_nt(dx2v.astype(BF16), wo_ref[...])
        for h in range(X_HEADS):
            sl = slice(h * X_HEAD_DIM, (h + 1) * X_HEAD_DIM)
            q = qx_ref[:, sl]
            p, l = _xattn_probs(q, kx_ref[:, sl])
            pn = p / l
            dob = dox[:, sl].astype(BF16)
            dp = _dot_nt(dob, vx_ref[:, sl])
            dvx_ref[:, sl] += _dot_tn(pn.astype(BF16), dob)
            ds = pn * (dp - jnp.sum(dp * pn, axis=-1, keepdims=True))
            dsb = (ds * X_SCALE).astype(BF16)
            dqx_ref[:, sl] = _dot(dsb, kx_ref[:, sl]).astype(BF16)
            dkx_ref[:, sl] += _dot_tn(dsb, q)
        dhc = _dot_nt(dqx_ref[...], wq_ref[...])
        dx, dgc = _rms_bwd(dhc, x1_ref[...], gc_ref[...])
        dx1 = dx2v + dx
        dx1_ref[...] = dx1
        dgc_ref[...] += dgc
        dmg = _dot_nt(dx1.astype(BF16), wout_ref[...])
        da, dgoa = _rms_bwd(dmg[:, 0:D_ATT], att_ref[...], goa_ref[...])
        datt_ref[...] = da
        dgoa_ref[...] += dgoa
        dr, dgol = _rms_bwd(dmg[:, D_ATT:1024], rec_ref[...], gol_ref[...])
        drec_ref[...] = dr
        dgol_ref[...] += dgol

    sq = _full((1024, 1024))
    mk = _full((m_len, 1024))
    return _call(
        body, "b_mid", (s_len // tm,),
        [_rows(tm, 1024), _rows(tm, 1024), _rows(tm, 1024), _rows(tm, 512), _rows(tm, 512), mk, mk,
         sq, sq, sq, _full((1, 1024)), _full((1, 512)), _full((1, 512))],
        [_rows(tm, 1024), _rows(tm, 1024), _rows(tm, 512), _rows(tm, 512), mk, mk,
         _full((1, 1024)), _full((1, 512)), _full((1, 512))],
        [_sds((s_len, 1024), BF16), _sds((s_len, 1024), F32), _sds((s_len, 512), F32),
         _sds((s_len, 512), F32), _sds((m_len, 1024), F32), _sds((m_len, 1024), F32),
         _sds((1, 1024), F32), _sds((1, 512), F32), _sds((1, 512), F32)],
        [], (dx2, qx, x1, att, rec, kx, vx, wo, wq, w_out, g_cross, g_oa, g_ol), "arbitrary", comm)


def _b_mem(dkx, dvx, mem, mn, g_mem, wk, wv):
    def body(dkx_ref, dvx_ref, mem_ref, mn_ref, g_ref, wk_ref, wv_ref, dwk_ref, dwv_ref, dgm_ref,
             dwkb_ref, dwvb_ref):
        dkb = dkx_ref[...].astype(BF16)
        dvb = dvx_ref[...].astype(BF16)
        dwk = _dot_tn(mn_ref[...], dkb)
        dwv = _dot_tn(mn_ref[...], dvb)
        dwk_ref[...] = dwk
        dwv_ref[...] = dwv
        dwkb_ref[...] = dwk.astype(BF16)
        dwvb_ref[...] = dwv.astype(BF16)
        dmn = _dot_nt(dkb, wk_ref[...]) + _dot_nt(dvb, wv_ref[...])
        mv = mem_ref[...]
        dgm_ref[...] = jnp.sum(dmn * (mv * _rinv(mv)), axis=0, keepdims=True)

    return pl.pallas_call(
        body, name="b_mem",
        out_shape=[_sds((1024, 1024), F32), _sds((1024, 1024), F32), _sds((1, 1024), F32),
                   _sds((1024, 1024), BF16), _sds((1024, 1024), BF16)],
        compiler_params=_cp())(dkx, dvx, mem, mn, g_mem, wk, wv)


def _b_lru(drec, hs, u, xg, conv_w, wrg, brg, wig, big, lam, tl, comm=None):
    s_len = xg.shape[0]
    nt = s_len // tl

    def body(drec_ref, hs_ref, hsp_ref, u_ref, xg_ref, cw_ref, wrg_ref, brg_ref, wig_ref, big_ref, l_ref,
             dxg_ref, dwrg_ref, dwig_ref, dbrg_ref, dbig_ref, dlam_ref, dcw_ref, dcb_ref,
             hbuf, abuf, dubuf, c_sc, d_sc, lam_sc, lcar, wacc_r, wacc_i):
        i = pl.program_id(0)
        tt = nt - 1 - i

        @pl.when(i == 0)
        def _():
            for r in (wacc_r, wacc_i, dbrg_ref, dbig_ref, dlam_ref, dcw_ref, dcb_ref):
                r[...] = jnp.zeros_like(r)
            abuf[tl:tl + 8, :] = jnp.zeros((8, D_LRU), F32)
            dubuf[tl:tl + 8, :] = jnp.zeros((8, D_LRU), F32)
            lcar[...] = jnp.zeros((8, D_LRU), F32)

        xu0 = xg_ref[:, 0:D_LRU]
        hsv = hs_ref[...]
        uv = u_ref[...]
        hbuf[8:8 + tl, :] = hsv
        hbuf[0:8, :] = jnp.where(tt > 0, hsp_ref[...], 0.0)
        hshift = hbuf[pl.ds(7, tl), :]
        wrg_v = wrg_ref[...]
        wig_v = wig_ref[...]
        lamv = l_ref[...]
        ub, r, ig, sp, a, mult = _lru_gates(uv, wrg_v, brg_ref[...], wig_v, big_ref[...], lamv)
        abuf[0:tl, :] = a
        c_sc[...] = abuf[pl.ds(1, tl), :]
        gel, dgel = _gelu_and_grad(xg_ref[:, D_LRU:2 * D_LRU])
        drv = drec_ref[...]
        d_sc[...] = drv * gel
        dxg_ref[:, D_LRU:2 * D_LRU] = (drv * hsv * dgel).astype(BF16)

        def grp(k, lnext):
            off = pl.multiple_of((tl // 8 - 1 - k) * 8, 8)
            l8 = _rscan8(c_sc[pl.ds(off, 8), :], d_sc[pl.ds(off, 8), :], lnext)
            lam_sc[pl.ds(off, 8), :] = l8
            return l8[0:1, :]

        lcar[0:1, :] = lax.fori_loop(0, tl // 8, grp, lcar[0:1, :])
        abuf[tl:tl + 8, :] = a[0:8, :]
        db = lam_sc[...]
        da = db * hshift
        dmult = db * (ig * uv)
        dig = db * mult * uv
        du = db * mult * ig
        dla = da * a - dmult * (a * a) / mult
        dlam_ref[...] += jnp.sum(dla * (-LRU_C) * r, axis=0, keepdims=True)
        dzr = dla * (-LRU_C * sp) * r * (1.0 - r)
        dzi = dig * ig * (1.0 - ig)
        dzrb = dzr.astype(BF16)
        dzib = dzi.astype(BF16)
        du = du + _dot_nt(dzrb, wrg_v) + _dot_nt(dzib, wig_v)
        wacc_r[...] += _dot_tn(ub, dzrb)
        wacc_i[...] += _dot_tn(ub, dzib)
        dbrg_ref[...] += jnp.sum(dzr, axis=0, keepdims=True)
        dbig_ref[...] += jnp.sum(dzi, axis=0, keepdims=True)
        dcb_ref[...] += jnp.sum(du, axis=0, keepdims=True)
        dubuf[0:tl, :] = du
        dxu0 = jnp.zeros((tl, D_LRU), F32)
        for j in range(4):
            dsh = dubuf[pl.ds(3 - j, tl), :]
            dxu0 = dxu0 + cw_ref[j:j + 1, :] * dsh
            dcw_ref[j:j + 1, :] += jnp.sum(xu0 * dsh, axis=0, keepdims=True)
        dubuf[tl:tl + 8, :] = du[0:8, :]
        dxg_ref[:, 0:D_LRU] = dxu0.astype(BF16)

        @pl.when(i == nt - 1)
        def _():
            dlam_ref[...] = dlam_ref[...] * (-jax.nn.sigmoid(-lamv))
            for n in range(LRU_BLOCKS):
                blk = slice(n * LRU_BLOCK, (n + 1) * LRU_BLOCK)
                dwrg_ref[n] = wacc_r[blk, blk]
                dwig_ref[n] = wacc_i[blk, blk]

    def rev(n):
        return pl.BlockSpec((tl, n), lambda i: (nt - 1 - i, 0))

    prev8 = pl.BlockSpec((8, D_LRU), lambda i: (jnp.maximum((nt - 1 - i) * (tl // 8) - 1, 0), 0))
    vec = _full((1, D_LRU))
    sq = _full((D_LRU, D_LRU))
    blocks_shape = (LRU_BLOCKS, LRU_BLOCK, LRU_BLOCK)
    blocks = _full(blocks_shape)
    return _call(
        body, "b_lru", (nt,),
        [rev(D_LRU), rev(D_LRU), prev8, rev(D_LRU), rev(1024), _full((4, D_LRU)), sq, vec, sq, vec, vec],
        [rev(1024), blocks, blocks, vec, vec, vec, _full((4, D_LRU)), vec],
        [_sds((s_len, 1024), BF16), _sds(blocks_shape, F32), _sds(blocks_shape, F32),
         _sds((1, D_LRU), F32), _sds((1, D_LRU), F32), _sds((1, D_LRU), F32),
         _sds((4, D_LRU), F32), _sds((1, D_LRU), F32)],
        [pltpu.VMEM((tl + 8, D_LRU), F32)] * 3 + [pltpu.VMEM((tl, D_LRU), F32)] * 3
        + [pltpu.VMEM((8, D_LRU), F32)] + [pltpu.VMEM((D_LRU, D_LRU), F32)] * 2,
        (drec, hs, hs, u, xg, conv_w, wrg, brg, wig, big, lam), "arbitrary", comm)


def _b_attn(qkv_pad, att, datt, frow, comm=None):
    s_len = datt.shape[0]
    nb = s_len // QB
    n_pair = ATT_HEADS // 2
    pair_w = 2 * HEAD_DIM

    def body(q_ref, k0, k1, k2, v0, v1, v2, o_ref, do_ref, frow_ref, dq_ref, dkv_ref, dfrow_ref,
             bias_sc, dt_sc, acc_sc):
        t = pl.program_id(0)

        @pl.when(t == 0)
        def _():
            _bias_table(frow_ref, bias_sc)
            dt_sc[...] = jnp.zeros_like(dt_sc)
            acc_sc[...] = jnp.zeros_like(acc_sc)

        @pl.when(t < nb)
        def _():
            var = jnp.minimum(t, N_BIAS - 1)
            even = _even_lanes()
            for hp in range(n_pair):
                cs = slice(hp * pair_w, (hp + 1) * pair_w)
                qt = q_ref[:, cs]
                kts = [k0[:, cs], k1[:, cs], k2[:, cs]]
                vts = [v0[:, cs], v1[:, cs], v2[:, cs]]
                dot = do_ref[:, cs]
                dd = dot * o_ref[:, cs]
                qmt, dost, dsbs, pbs, dqs = [], [], [], [], []
                for e in range(2):
                    keep = even if e == 0 else jnp.logical_not(even)
                    qm = jnp.where(keep, qt, 0)
                    p = _att_probs(qm, kts, bias_sc[var, 2 * hp + e])
                    inv = 1.0 / jnp.sum(p, axis=-1, keepdims=True)
                    dos = jnp.where(keep, dot * inv, 0.0)
                    delta = jnp.sum(jnp.where(keep, dd, 0.0), axis=-1, keepdims=True) * inv
                    dp = jnp.concatenate([_dot_nt(dos.astype(BF16), v) for v in vts], axis=1)
                    ds = p * (dp - delta)
                    dt_sc[2 * hp + e] += ds
                    dsb = ds.astype(BF16)
                    dq = _dot(dsb[:, 0:QB], kts[0])
                    for j in (1, 2):
                        dq = dq + _dot(dsb[:, j * QB:(j + 1) * QB], kts[j])
                    dqs.append(dq)
                    dsbs.append(dsb)
                    pbs.append(p.astype(BF16))
                    qmt.append(qm.astype(F32).T.astype(BF16))
                    dost.append(dos.T.astype(BF16))
                dq_ref[:, cs] = (jnp.where(even, dqs[0], dqs[1]) * ATT_SCALE).astype(BF16)
                for j in range(3):
                    slot = (t + 1 + j) % 3
                    js = slice(j * QB, (j + 1) * QB)
                    acc_sc[slot, hp] += _dot(qmt[0], dsbs[0][:, js]) + _dot(qmt[1], dsbs[1][:, js])
                    acc_sc[slot, n_pair + hp] += _dot(dost[0], pbs[0][:, js]) + _dot(dost[1], pbs[1][:, js])

        done = (t + 1) % 3

        @pl.when(t >= 2)
        def _():
            for i in range(2 * n_pair):
                dkv_ref[:, i * pair_w:(i + 1) * pair_w] = acc_sc[done, i].T.astype(BF16)

        acc_sc[done] = jnp.zeros((2 * n_pair, pair_w, QB), F32)

        @pl.when(t == nb + 1)
        def _():
            row = lax.broadcasted_iota(jnp.int32, (8, ROLL_W), 0)
            pad = jnp.zeros((8, ROLL_W - KB), F32)
            for h in range(ATT_HEADS):
                acc8 = jnp.concatenate([dt_sc[h, 0:8, :], pad], axis=1)
                for a1 in range(1, QB // 8):
                    blk = jnp.concatenate([dt_sc[h, 8 * a1:8 * a1 + 8, :], pad], axis=1)
                    acc8 = acc8 + pltpu.roll(blk, ROLL_W - 8 * a1, 1)
                for k in range(3):
                    acc8 = jnp.where(((row >> k) & 1) == 1, pltpu.roll(acc8, ROLL_W - (1 << k), 1), acc8)
                dfrow_ref[h:h + 1, :] = jnp.sum(acc8, axis=0, keepdims=True)

    clamp = lambda t: jnp.minimum(t, nb - 1)
    qrows = pl.BlockSpec((QB, D_ATT), lambda t: (clamp(t), 0))
    return _call(
        body, "b_attn", (nb + 2,),
        _att_in_specs(clamp) + [qrows, qrows, _full((ATT_HEADS, ROLL_W))],
        [qrows, pl.BlockSpec((QB, 2 * D_ATT), lambda t: (jnp.maximum(t - 2, 0), 0)),
         _full((ATT_HEADS, ROLL_W))],
        [_sds((s_len, D_ATT), BF16), _sds((s_len, 2 * D_ATT), BF16), _sds((ATT_HEADS, ROLL_W), F32)],
        [pltpu.VMEM((N_BIAS, ATT_HEADS, QB, KB), F32), pltpu.VMEM((ATT_HEADS, QB, KB), F32),
         pltpu.VMEM((3, 2 * n_pair, pair_w, QB), F32)],
        (*([qkv_pad] * 7), att, datt, frow), "arbitrary", comm)


def _b_win(dq, dkv, dxg, h, ts):
    s_len = h.shape[0]
    steps = s_len // ts

    def body(dq_ref, dkv_ref, dxg_ref, h_ref, dw_ref, dwb_ref):
        @pl.when(pl.program_id(0) == 0)
        def _():
            dw_ref[...] = jnp.zeros_like(dw_ref)

        dproj = jnp.concatenate([dq_ref[...], dkv_ref[...], dxg_ref[...]], axis=1)
        hv = h_ref[...]
        for s in range(N_SHARD):
            dw_ref[s] += _dot_tn(hv, dproj[:, s * IN_SH:(s + 1) * IN_SH])

        @pl.when(pl.program_id(0) == steps - 1)
        def _():
            dwb_ref[...] = dw_ref[...].astype(BF16)

    wspec = _full((N_SHARD, 1024, IN_SH))
    return pl.pallas_call(
        body, name="b_win", grid=(steps,),
        in_specs=[_rows(ts, 512), _rows(ts, 1024), _rows(ts, 1024), _rows(ts, 1024)],
        out_specs=[wspec, wspec],
        out_shape=[_sds((N_SHARD, 1024, IN_SH), F32), _sds((N_SHARD, 1024, IN_SH), BF16)],
        compiler_params=_cp("arbitrary"))(dq, dkv, dxg, h)


def _b_inproj(dq, dkv, dxg, x, dx1, g_mix, w_in_g, tm, comm=None):
    s_len = x.shape[0]

    def body(dq_ref, dkv_ref, dxg_ref, x_ref, dx1_ref, g_ref, w_hbm, gx_ref, dgm_ref, w_ref):
        _load_w_in_once(w_hbm, w_ref)

        @pl.when(pl.program_id(0) == 0)
        def _():
            dgm_ref[...] = jnp.zeros_like(dgm_ref)

        dproj = jnp.concatenate([dq_ref[...], dkv_ref[...], dxg_ref[...]], axis=1)
        dh = _dot_nt(dproj, w_ref[...])
        dx, dgm = _rms_bwd(dh, x_ref[...], g_ref[...])
        gx_ref[...] = dx1_ref[...] + dx
        dgm_ref[...] += dgm

    return _call(
        body, "b_inproj", (s_len // tm,),
        [_rows(tm, 512), _rows(tm, 1024), _rows(tm, 1024), _rows(tm, 1024), _rows(tm, 1024),
         _full((1, 1024)), _any()],
        [_rows(tm, 1024), _full((1, 1024))],
        [_sds((s_len, 1024), F32), _sds((1, 1024), F32)],
        [pltpu.VMEM((1024, D_IN), BF16)], (dq, dkv, dxg, x, dx1, g_mix, w_in_g), "arbitrary", comm)


def _mm_tn(xa, ya, name, ts):
    s_len, k = xa.shape
    n = ya.shape[1]

    steps = s_len // ts

    def body(x_ref, y_ref, o_ref, ob_ref):
        @pl.when(pl.program_id(0) == 0)
        def _():
            o_ref[...] = jnp.zeros_like(o_ref)
        o_ref[...] += _dot_tn(x_ref[...].astype(BF16), y_ref[...].astype(BF16))

        @pl.when(pl.program_id(0) == steps - 1)
        def _():
            ob_ref[...] = o_ref[...].astype(BF16)

    return pl.pallas_call(
        body, name=name, grid=(steps,), in_specs=[_rows(ts, k), _rows(ts, n)],
        out_specs=[_full((k, n))] * 2, out_shape=[_sds((k, n), F32), _sds((k, n), BF16)],
        compiler_params=_cp("arbitrary"))(xa, ya)


def _frow_from_rel_bias(rb):
    hi = jnp.broadcast_to(rb[:, 256:257], (ATT_HEADS, 385))
    mid = rb[:, 1:256][:, ::-1]
    lo = jnp.broadcast_to(rb[:, 0:1], (ATT_HEADS, 128))
    wrap = jnp.broadcast_to(rb[:, 256:257], (ATT_HEADS, ROLL_W - KB))
    return jnp.concatenate([hi, mid, lo, wrap], axis=1)


def _rel_bias_grad_from_dfrow(df):
    g256 = jnp.sum(df[:, 0:385], axis=1, keepdims=True) + jnp.sum(df[:, KB:ROLL_W], axis=1, keepdims=True)
    mid = df[:, 385:640][:, ::-1]
    g0 = jnp.sum(df[:, 640:KB], axis=1, keepdims=True)
    return jnp.concatenate([g0, mid, g256], axis=1)


def _block_diag(w):
    eye = jnp.eye(8, dtype=w.dtype)
    return (w[:, :, None, :] * eye[:, None, :, None]).reshape(D_LRU, D_LRU)


MID = ['w_out', 'wq_c', 'wk_c', 'wv_c', 'wo_c']
TRANSPOSED = ['w_gate', 'w_up']
AG_IN_INPROJ = ['w_out', 'wq_c', 'wk_c']
AG_IN_ATTN = ['wv_c', 'wo_c', 'w_gate']
AG_IN_LRU = ['w_up']
AG_IN_MID = ['w_down']
RS_IN_MID = ['w_gate', 'w_up']
RS_IN_LRU = ['w_down']
RS_IN_ATTN = MID


def _local_step(x, mem, tgt, p, gw, shards=None, chip=None):
    s_len = x.shape[0]
    tm = min(256, s_len)
    tmb = min(512, s_len)
    tl = min(512, s_len)
    frow = _frow_from_rel_bias(p['rel_bias'])
    wrg = _block_diag(p['w_rg']).astype(BF16)
    wig = _block_diag(p['w_ig']).astype(BF16)
    gw = dict(gw)

    big, bigb, recv, part, sib = {}, {}, {}, {}, {}

    def ag(names):
        return [] if shards is None else [("ag", [shards[n] for n in names])]

    def rs(names):
        return [] if shards is None else [("rs", [bigb[n] for n in names])]

    def swap(names):
        return [] if shards is None else [("swap", [part[n] for n in names])]

    def reduce_own(names):
        if shards is not None:
            for n in names:
                part[n] = _sum_parts(big[n], recv[n], chip, "sum_" + n)

    h, qkv_pad, xg, *got = _f_inproj(x, p['g_mix'], gw['w_in'], tmb, ag(AG_IN_INPROJ))
    gw.update(zip(AG_IN_INPROJ, got))
    att, *got = _f_attn(qkv_pad, frow, ag(AG_IN_ATTN))
    gw.update(zip(AG_IN_ATTN, got))
    rec, u, hs, *got = _f_lru(xg, p['conv_w'], p['conv_b'], wrg, p['b_rg'], wig, p['b_ig'], p['lru_L'], tl,
                              ag(AG_IN_LRU))
    gw.update(zip(AG_IN_LRU, got))
    w_out = gw['w_out'].reshape(1024, 1024)
    wq = gw['wq_c'].reshape(1024, 1024)
    wk = gw['wk_c'].reshape(1024, 1024)
    wv = gw['wv_c'].reshape(1024, 1024)
    wo = gw['wo_c'].reshape(1024, 1024)
    mn, kx, vx = _f_mem(mem, p['g_mem'], wk, wv)
    mg, x1, hc, qx, ox, x2, *got = _f_mid(x, att, rec, p['g_out_attn'], p['g_out_lru'], w_out, p['g_cross'],
                                          wq, kx, vx, wo, tmb, ag(AG_IN_MID))
    gw.update(zip(AG_IN_MID, got))
    ffn_w = [gw[n].reshape(D_FF, 1024) for n in ('w_gate', 'w_up', 'w_down')]
    hf, gact, uact, aact, dx3, loss, dg_final = _f_ffn(x2, tgt, p['g_ffn'], p['g_final'], *ffn_w, tmb)

    ts = min(512, s_len)
    dgact, duact, dx2, dg_ffn = _b_ffn(dx3, x2, gact, uact, p['g_ffn'], *ffn_w, tm)
    big['w_gate'], bigb['w_gate'] = _mm_tn(dgact, hf, "dw_gate", ts)
    big['w_up'], bigb['w_up'] = _mm_tn(duact, hf, "dw_up", ts)
    big['w_down'], bigb['w_down'] = _mm_tn(aact, dx3, "dw_down", ts)
    for n in ('w_gate', 'w_up', 'w_down'):
        big[n] = big[n].reshape(N_SHARD, FF_SH, 1024)
        bigb[n] = bigb[n].reshape(N_SHARD, FF_SH, 1024)

    dqx, dx1, datt, drec, dkx, dvx, dg_cross, dg_oa, dg_ol, *got = _b_mid(
        dx2, qx, x1, att, rec, kx, vx, wo, wq, w_out, p['g_cross'], p['g_out_attn'], p['g_out_lru'], tmb,
        rs(RS_IN_MID))
    recv.update(zip(RS_IN_MID, got))
    reduce_own(RS_IN_MID)
    dwk, dwv, dg_mem, dwkb, dwvb = _b_mem(dkx, dvx, mem, mn, p['g_mem'], wk, wv)
    big['wk_c'], bigb['wk_c'] = dwk, dwkb
    big['wv_c'], bigb['wv_c'] = dwv, dwvb
    big['w_out'], bigb['w_out'] = _mm_tn(mg, dx1, "dw_out", ts)
    big['wq_c'], bigb['wq_c'] = _mm_tn(hc, dqx, "dw_q", ts)
    big['wo_c'], bigb['wo_c'] = _mm_tn(ox, dx2, "dw_o", ts)
    for n in MID:
        big[n] = big[n].reshape(N_SHARD, 256, 1024)
        bigb[n] = bigb[n].reshape(N_SHARD, 256, 1024)

    dxg, dwrg, dwig, dbrg, dbig, dlam, dcw, dcb, *got = _b_lru(
        drec, hs, u, xg, p['conv_w'], wrg, p['b_rg'], wig, p['b_ig'], p['lru_L'], tl,
        rs(RS_IN_LRU) + swap(RS_IN_MID))
    recv.update(zip(RS_IN_LRU, got))
    sib.update(zip(RS_IN_MID, got[len(RS_IN_LRU):]))
    reduce_own(RS_IN_LRU)
    dq, dkv, dfrow, *got = _b_attn(qkv_pad, att, datt, frow, rs(RS_IN_ATTN) + swap(RS_IN_LRU))
    recv.update(zip(RS_IN_ATTN, got))
    sib.update(zip(RS_IN_LRU, got[len(RS_IN_ATTN):]))
    reduce_own(RS_IN_ATTN)
    big['w_in'], bigb['w_in'] = _b_win(dq, dkv, dxg, h, ts)
    grad_x, dg_mix, *got = _b_inproj(dq, dkv, dxg, x, dx1, p['g_mix'], gw['w_in'], tmb,
                                     rs(['w_in']) + swap(RS_IN_ATTN))
    recv.update(zip(['w_in'], got))
    sib.update(zip(RS_IN_ATTN, got[1:]))
    reduce_own(['w_in'])
    small = {
        'g_mix': dg_mix, 'rel_bias': _rel_bias_grad_from_dfrow(dfrow), 'conv_w': dcw, 'conv_b': dcb,
        'w_rg': dwrg, 'b_rg': dbrg, 'w_ig': dwig, 'b_ig': dbig,
        'lru_L': dlam,
        'g_out_attn': dg_oa, 'g_out_lru': dg_ol, 'g_cross': dg_cross, 'g_mem': dg_mem, 'g_ffn': dg_ffn,
        'g_final': dg_final,
    }
    return loss, grad_x, small, big, part, sib


def _cast_shards(ws):
    def body(*refs):
        n = len(refs) // 2
        for src, dst in zip(refs[:n], refs[n:]):
            dst[...] = src[...].astype(BF16)

    return pl.pallas_call(body, name="cast_shards", out_shape=[_sds(w.shape, BF16) for w in ws],
                          compiler_params=_cp())(*ws)


def _sum_parts(own4, recv3, chip, name):
    _, r, c = own4.shape
    tr = r // 4

    def body(chip_ref, own_ref, rc_ref, o_ref):
        o_ref[...] = ((own_ref[0] + rc_ref[0].astype(F32)) + rc_ref[1].astype(F32)) + rc_ref[2].astype(F32)

    grid_spec = pltpu.PrefetchScalarGridSpec(
        num_scalar_prefetch=1, grid=(4,),
        in_specs=[pl.BlockSpec((1, tr, c), lambda i, ch: (ch[0], i, 0)),
                  pl.BlockSpec((3, tr, c), lambda i, ch: (0, i, 0))],
        out_specs=pl.BlockSpec((tr, c), lambda i, ch: (i, 0)))
    return pl.pallas_call(body, name=name, grid_spec=grid_spec, out_shape=_sds((r, c), F32),
                          compiler_params=_cp("parallel"))(chip, own4, recv3)


def _adamw_math(w, g, m, v):
    m = ADAM_B1 * m + (1.0 - ADAM_B1) * g
    v = ADAM_B2 * v + (1.0 - ADAM_B2) * (g * g)
    m_hat = m / (1.0 - ADAM_B1 ** ADAM_STEP)
    v_hat = v / (1.0 - ADAM_B2 ** ADAM_STEP)
    delta = -ADAM_LR * (m_hat / (jnp.sqrt(v_hat) + ADAM_EPS) + ADAM_WD * w)
    return delta, m, v


def _final_adamw(pa, pb, w, m, v, name):
    r, c = w.shape
    tr = r // 4

    def body(pa_ref, pb_ref, w_ref, m_ref, v_ref, g_ref, d_ref, nm_ref, nv_ref):
        g = pa_ref[...] + pb_ref[...]
        g_ref[...] = g
        d_ref[...], nm_ref[...], nv_ref[...] = _adamw_math(w_ref[...], g, m_ref[...], v_ref[...])

    return pl.pallas_call(
        body, name=name, grid=(4,), in_specs=[_rows(tr, c)] * 5, out_specs=[_rows(tr, c)] * 4,
        out_shape=[_sds((r, c), F32)] * 4, compiler_params=_cp("parallel"))(pa, pb, w, m, v)


def _pack_put(ref, name, val_ref):
    r = _pack_rows()[name]
    shape = val_ref.shape
    if len(shape) == 3:
        for b in range(shape[0]):
            ref[r:r + shape[1], b * shape[2]:(b + 1) * shape[2]] = val_ref[b]
    elif shape[1] == 2 * PACK_W:
        ref[r:r + 1, :] = val_ref[:, 0:PACK_W]
        ref[r + 1:r + 2, :] = val_ref[:, PACK_W:2 * PACK_W]
    else:
        ref[r:r + shape[0], 0:shape[1]] = val_ref[...]


def _pack_get(ref, name, shape):
    r = _pack_rows()[name]
    if len(shape) == 3:
        return jnp.stack([ref[r:r + shape[1], b * shape[2]:(b + 1) * shape[2]] for b in range(shape[0])])
    if shape[1] == 2 * PACK_W:
        return jnp.concatenate([ref[r:r + 1, :], ref[r + 1:r + 2, :]], axis=1)
    return ref[r:r + shape[0], 0:shape[1]]


def _ar_small(g, loss):
    n = len(g)

    def body(*refs):
        g_refs, loss_ref = refs[:n], refs[n]
        tot_ref, pack, buf, send_sems, recv_sems = refs[n + 1:]
        x, y, c = _mesh_pos()
        me = 4 * x + 2 * y + c

        def peer(k):
            px = 1 - x if k & 4 else x
            py = 1 - y if k & 2 else y
            pc = 1 - c if k & 1 else c
            return px, py, pc

        def remote(k, slot):
            return pltpu.make_async_remote_copy(
                src_ref=pack, dst_ref=buf.at[slot], send_sem=send_sems.at[k - 1], recv_sem=recv_sems.at[k - 1],
                device_id=peer(k), device_id_type=MESH_ID)

        pack[...] = jnp.zeros_like(pack)
        for a, name in enumerate(SMALL):
            _pack_put(pack, name, g_refs[a])
        _pack_put(pack, 'loss', loss_ref)
        for k in range(1, 8):
            remote(k, me).start()
        buf[me] = pack[...]
        for k in range(1, 8):
            px, py, pc = peer(k)
            remote(k, 4 * px + 2 * py + pc).wait_recv()
        for k in range(1, 8):
            remote(k, me).wait_send()
        tot = buf[0]
        for k in range(1, 8):
            tot = tot + buf[k]
        tot_ref[...] = tot

    return pl.pallas_call(
        body, name="ar_small", out_shape=_sds((PACK_ROWS, PACK_W), F32),
        scratch_shapes=[pltpu.VMEM((PACK_ROWS, PACK_W), F32), pltpu.VMEM((8, PACK_ROWS, PACK_W), F32),
                        pltpu.SemaphoreType.DMA((7,)), pltpu.SemaphoreType.DMA((7,))],
        compiler_params=_cp())(*g, loss)


def _adamw_small(tot, g_shapes, loss_shape, w, m, v):
    n = len(w)

    def body(*refs):
        tot_ref = refs[0]
        w_refs, m_refs, v_refs = (refs[1 + i * n:1 + (i + 1) * n] for i in range(3))
        o0 = 3 * n + 1
        go, do, mo, vo = (refs[o0 + i * n:o0 + (i + 1) * n] for i in range(4))
        loss_out = refs[o0 + 4 * n]
        x, y, _ = _mesh_pos()
        loss_out[...] = _pack_get(tot_ref, 'loss', loss_shape)
        for a, name in enumerate(SMALL):
            if name == 'conv_w':
                r = _pack_rows()[name]
                ga = tot_ref[r:r + g_shapes[a][0], pl.ds(pl.multiple_of((2 * x + y) * 128, 128), 128)]
            else:
                ga = _pack_get(tot_ref, name, g_shapes[a])
            go[a][...] = ga
            do[a][...], mo[a][...], vo[a][...] = _adamw_math(w_refs[a][...], ga, m_refs[a][...], v_refs[a][...])

    out_shape = [_sds(a.shape, F32) for a in w] * 4 + [_sds(loss_shape, F32)]
    return pl.pallas_call(body, name="adamw_small", out_shape=out_shape, compiler_params=_cp())(tot, *w, *m, *v)


PACK_W = 512
PACK_ROWS = 160


def _pack_rows():
    rows, r = {}, 0
    for name in ['g_mix', 'g_cross', 'g_mem', 'g_ffn', 'g_final']:
        rows[name] = r
        r += 2
    for name in ['conv_b', 'b_rg', 'b_ig', 'lru_L', 'g_out_attn', 'g_out_lru']:
        rows[name] = r
        r += 1
    rows['conv_w'] = r
    rows['loss'] = r + 4
    rows['rel_bias'] = 24
    rows['w_rg'] = 32
    rows['w_ig'] = 32 + LRU_BLOCK
    assert r + 5 <= 24 and rows['w_ig'] + LRU_BLOCK == PACK_ROWS
    return rows


INPUT_NAMES = (['x', 'mem'] + WEIGHTS + ['loss_target'] + ['m_' + n for n in WEIGHTS] + ['v_' + n for n in WEIGHTS])


def kernel(x, mem, g_mix, w_in, rel_bias, conv_w, conv_b, w_rg, b_rg, w_ig, b_ig, lru_L, g_out_attn, g_out_lru, w_out, g_cross, g_mem, wq_c, wk_c, wv_c, wo_c, g_ffn, w_gate, w_up, w_down, g_final, loss_target, m_g_mix, m_w_in, m_rel_bias, m_conv_w, m_conv_b, m_w_rg, m_b_rg, m_w_ig, m_b_ig, m_lru_L, m_g_out_attn, m_g_out_lru, m_w_out, m_g_cross, m_g_mem, m_wq_c, m_wk_c, m_wv_c, m_wo_c, m_g_ffn, m_w_gate, m_w_up, m_w_down, m_g_final, v_g_mix, v_w_in, v_rel_bias, v_conv_w, v_conv_b, v_w_rg, v_b_rg, v_w_ig, v_b_ig, v_lru_L, v_g_out_attn, v_g_out_lru, v_w_out, v_g_cross, v_g_mem, v_wq_c, v_wk_c, v_wv_c, v_wo_c, v_g_ffn, v_w_gate, v_w_up, v_w_down, v_g_final):
    a = dict(zip(INPUT_NAMES, (x, mem, g_mix, w_in, rel_bias, conv_w, conv_b, w_rg, b_rg, w_ig, b_ig, lru_L, g_out_attn, g_out_lru, w_out, g_cross, g_mem, wq_c, wk_c, wv_c, wo_c, g_ffn, w_gate, w_up, w_down, g_final, loss_target, m_g_mix, m_w_in, m_rel_bias, m_conv_w, m_conv_b, m_w_rg, m_b_rg, m_w_ig, m_b_ig, m_lru_L, m_g_out_attn, m_g_out_lru, m_w_out, m_g_cross, m_g_mem, m_wq_c, m_wk_c, m_wv_c, m_wo_c, m_g_ffn, m_w_gate, m_w_up, m_w_down, m_g_final, v_g_mix, v_w_in, v_rel_bias, v_conv_w, v_conv_b, v_w_rg, v_b_rg, v_w_ig, v_b_ig, v_lru_L, v_g_out_attn, v_g_out_lru, v_w_out, v_g_cross, v_g_mem, v_wq_c, v_wk_c, v_wv_c, v_wo_c, v_g_ffn, v_w_gate, v_w_up, v_w_down, v_g_final)))
    chip = 2 * lax.axis_index("x") + lax.axis_index("y")

    def shard(name):
        arr = a[name][0]
        return jnp.swapaxes(arr, 0, 1) if name[2:] in TRANSPOSED or name in TRANSPOSED else arr

    shards = dict(zip(BIG, _cast_shards([shard(n) for n in BIG])))
    w_in_g, conv_w_g = _comm_only("ag_w_in", [("ag", [shards['w_in']]), ("agf", [a['conv_w'][0]])])
    conv_w_full = conv_w_g.transpose(1, 0, 2).reshape(4, D_LRU)

    p = {n: a[n] for n in SMALL}
    p['rel_bias'] = a['rel_bias'][0]
    p['w_rg'] = a['w_rg'][0]
    p['w_ig'] = a['w_ig'][0]
    p['conv_w'] = conv_w_full
    p['g_final'] = a['g_final'][None, :]
    chip_arr = jnp.reshape(chip, (1,)).astype(jnp.int32)
    loss_part, grad_x, small, _, part, sib = _local_step(
        a['x'][0], a['mem'][0], a['loss_target'][0], p, {'w_in': w_in_g}, shards, chip_arr)

    sib['w_in'], = _comm_only("swap_w_in", [("swap", [part['w_in']])])
    out = {}
    for n in BIG:
        res = _final_adamw(part[n], sib[n], shard(n), shard('m_' + n), shard('v_' + n), "adamw_" + n)
        out[n] = [jnp.swapaxes(r, 0, 1) for r in res] if n in TRANSPOSED else res

    def natural(arr):
        return arr[0] if arr.ndim >= 3 else (arr[None, :] if arr.ndim == 1 else arr)

    small_g = [small[n] for n in SMALL]
    small_out = _adamw_small(_ar_small(small_g, loss_part), [g.shape for g in small_g], loss_part.shape,
                             *[[natural(a[pre + n]) for n in SMALL] for pre in ('', 'm_', 'v_')])
    ns = len(SMALL)
    loss = small_out[4 * ns][0, 0]

    def leaf(i, n):
        if n in BIG:
            return out[n][i][None]
        return small_out[i * ns + SMALL.index(n)].reshape(a[n].shape)

    return (loss, grad_x[None], *[leaf(i, n) for i in range(4) for n in WEIGHTS])
```

```python
import math

import jax
import jax.numpy as jnp
from jax import lax
from jax.experimental import pallas as pl
from jax.experimental.pallas import tpu as pltpu

F32 = jnp.float32
BF16 = jnp.bfloat16

D_MODEL = 1024
D_ATT = 512
D_LRU = 512
HEAD_DIM = 64
ATT_HEADS = 8
CHUNK = 64
LEFT_CHUNKS = 8
X_HEADS = 4
X_HEAD_DIM = 256
N_SHARD = 4
IN_SH = 640
D_IN = N_SHARD * IN_SH
FF_SH = 704
D_FF = N_SHARD * FF_SH
EPS = 1e-6
LRU_C = 8.0
LRU_BLOCKS = 8
LRU_BLOCK = 64
QB = 256
KB = 768
ROLL_W = 1024
NEG = -1e30
ATT_SCALE = HEAD_DIM ** -0.5
X_SCALE = X_HEAD_DIM ** -0.5

ADAM_LR = 0.001
ADAM_B1 = 0.9
ADAM_B2 = 0.999
ADAM_EPS = 1e-08
ADAM_WD = 0.01
ADAM_STEP = 10

VMEM_LIMIT_V7X = 56 * 1024 * 1024
MESH_ID = pl.DeviceIdType.MESH

WEIGHTS = ['g_mix', 'w_in', 'rel_bias', 'conv_w', 'conv_b', 'w_rg', 'b_rg', 'w_ig', 'b_ig', 'lru_L',
           'g_out_attn', 'g_out_lru', 'w_out', 'g_cross', 'g_mem', 'wq_c', 'wk_c', 'wv_c', 'wo_c',
           'g_ffn', 'w_gate', 'w_up', 'w_down', 'g_final']
BIG = ['w_in', 'w_out', 'wq_c', 'wk_c', 'wv_c', 'wo_c', 'w_gate', 'w_up', 'w_down']
SMALL = [n for n in WEIGHTS if n not in BIG]


def _sds(shape, dtype):
    return jax.ShapeDtypeStruct(shape, dtype)


def _cp(*sem):
    return pltpu.CompilerParams(dimension_semantics=sem or None, vmem_limit_bytes=VMEM_LIMIT_V7X)


def _rows(tm, n):
    return pl.BlockSpec((tm, n), lambda i: (i, 0))


def _full(shape):
    nd = len(shape)
    return pl.BlockSpec(shape, lambda i: (0,) * nd)


def _dot(a, b):
    return jnp.dot(a, b, preferred_element_type=F32)


def _dot_nt(a, b):
    return lax.dot_general(a, b, (((1,), (1,)), ((), ())), preferred_element_type=F32)


def _dot_tn(a, b):
    return lax.dot_general(a, b, (((0,), (0,)), ((), ())), preferred_element_type=F32)


def _rinv(x):
    return lax.rsqrt(jnp.mean(x * x, axis=-1, keepdims=True) + EPS)


def _rms_bwd(dy, x, g):
    r = _rinv(x)
    yh = x * r
    dyh = dy * g
    dx = r * (dyh - yh * jnp.mean(dyh * yh, axis=-1, keepdims=True))
    return dx, jnp.sum(dy * yh, axis=0, keepdims=True)


def _gelu(x):
    c = math.sqrt(2.0 / math.pi)
    t = jnp.tanh(c * (x + 0.044715 * x * x * x))
    return 0.5 * x * (1.0 + t)


def _gelu_and_grad(x):
    c = math.sqrt(2.0 / math.pi)
    t = jnp.tanh(c * (x + 0.044715 * x * x * x))
    g = 0.5 * x * (1.0 + t)
    dg = 0.5 * (1.0 + t) + 0.5 * x * (1.0 - t * t) * c * (1.0 + 3.0 * 0.044715 * x * x)
    return g, dg


def _neg_expm1(z):
    series = -z * (1 + z / 2 * (1 + z / 3 * (1 + z / 4)))
    return jnp.where(z > -0.03, series, 1.0 - jnp.exp(z))


def _lru_gates(u, wrg, brg, wig, big, lam):
    ub = u.astype(BF16)
    r = jax.nn.sigmoid(_dot(ub, wrg) + brg)
    ig = jax.nn.sigmoid(_dot(ub, wig) + big)
    sp = jnp.maximum(-lam, 0.0) + jnp.log1p(jnp.exp(-jnp.abs(lam)))
    la = -LRU_C * r * sp
    a = jnp.exp(la)
    mult = jnp.sqrt(jnp.maximum(_neg_expm1(2.0 * la), 0.0))
    return ub, r, ig, sp, a, mult


def _scan8(a8, b8, hprev):
    row = lax.broadcasted_iota(jnp.int32, a8.shape, 0)
    aa, bb = a8, b8
    for d in (1, 2, 4):
        a_s = pltpu.roll(aa, d, 0)
        b_s = pltpu.roll(bb, d, 0)
        m = row >= d
        bb = jnp.where(m, aa * b_s + bb, bb)
        aa = jnp.where(m, aa * a_s, aa)
    return aa * hprev + bb


def _mesh_pos():
    return lax.axis_index("x"), lax.axis_index("y"), lax.axis_index("c")


def _other_chips(x, y):
    return [(1 - x, y), (x, 1 - y), (1 - x, 1 - y)]


def _no_forward():
    pass


def _ag_full_copies(ins, outs, sems):
    send_sems, recv_sems, loc_sems = sems
    n = len(ins)
    x, y, c = _mesh_pos()
    mine = 2 * x + y
    chips = _other_chips(x, y)

    def remote(k, j, slot):
        px, py = chips[j]
        return pltpu.make_async_remote_copy(
            src_ref=ins[k], dst_ref=outs[k].at[slot], send_sem=send_sems.at[k, j], recv_sem=recv_sems.at[k, j],
            device_id=(px, py, c), device_id_type=MESH_ID)

    def local(k):
        return pltpu.make_async_copy(ins[k], outs[k].at[mine], loc_sems.at[k])

    def start():
        for k in range(n):
            local(k).start()
            for j in range(3):
                remote(k, j, mine).start()

    def wait():
        for k in range(n):
            for j, (px, py) in enumerate(chips):
                remote(k, j, 2 * px + py).wait_recv()
        for k in range(n):
            for j in range(3):
                remote(k, j, mine).wait_send()
            local(k).wait()

    return start, _no_forward, wait


def _ag_copies(ins, outs, sems):
    send_sems, recv_sems, fsend_sems, frecv_sems, loc_sems = sems
    n = len(ins)
    x, y, c = _mesh_pos()
    mine = 2 * x + y
    chips = _other_chips(x, y)

    def half(ref, hc):
        r = ref.shape[0] // 2
        return ref.at[pl.ds(pl.multiple_of(hc * r, 16), r)]

    def ici(k, j, slot):
        px, py = chips[j]
        return pltpu.make_async_remote_copy(
            src_ref=half(ins[k], c), dst_ref=half(outs[k].at[slot], c),
            send_sem=send_sems.at[k, j], recv_sem=recv_sems.at[k, j],
            device_id=(px, py, c), device_id_type=MESH_ID)

    def d2d(k, j, hc):
        px, py = chips[j]
        part = half(outs[k].at[2 * px + py], hc)
        return pltpu.make_async_remote_copy(
            src_ref=part, dst_ref=part, send_sem=fsend_sems.at[k, j], recv_sem=frecv_sems.at[k, j],
            device_id=(x, y, 1 - c), device_id_type=MESH_ID)

    def local(k):
        return pltpu.make_async_copy(ins[k], outs[k].at[mine], loc_sems.at[k])

    def start():
        for k in range(n):
            local(k).start()
            for j in range(3):
                ici(k, j, mine).start()

    def forward():
        for k in range(n):
            for j, (px, py) in enumerate(chips):
                ici(k, j, 2 * px + py).wait_recv()
                d2d(k, j, c).start()

    def wait():
        for k in range(n):
            for j in range(3):
                d2d(k, j, 1 - c).wait_recv()
        for k in range(n):
            for j in range(3):
                d2d(k, j, c).wait_send()
                ici(k, j, mine).wait_send()
            local(k).wait()

    return start, forward, wait


def _rs_copies(ins, outs, sems):
    send_sems, recv_sems = sems
    n = len(ins)
    x, y, c = _mesh_pos()
    chips = _other_chips(x, y)

    def remote(k, j):
        px, py = chips[j]
        return pltpu.make_async_remote_copy(
            src_ref=ins[k].at[2 * px + py], dst_ref=outs[k].at[j],
            send_sem=send_sems.at[k, j], recv_sem=recv_sems.at[k, j],
            device_id=(px, py, c), device_id_type=MESH_ID)

    def start():
        for k in range(n):
            for j in range(3):
                remote(k, j).start()

    def wait():
        for k in range(n):
            for j in range(3):
                remote(k, j).wait_recv()
        for k in range(n):
            for j in range(3):
                remote(k, j).wait_send()

    return start, _no_forward, wait


def _swap_copies(ins, outs, sems):
    send_sems, recv_sems = sems
    x, y, c = _mesh_pos()
    copies = [pltpu.make_async_remote_copy(
        src_ref=ins[k], dst_ref=outs[k], send_sem=send_sems.at[k], recv_sem=recv_sems.at[k],
        device_id=(x, y, 1 - c), device_id_type=MESH_ID) for k in range(len(ins))]

    def start():
        for cp in copies:
            cp.start()

    def wait():
        for cp in copies:
            cp.wait()

    return start, _no_forward, wait


def _comm_plan(groups):
    plan, arrs, shapes, sems = [], [], [], []
    for kind, group in groups:
        k = len(group)
        arrs += group
        per_peer = pltpu.SemaphoreType.DMA((k, 3))
        if kind == "ag":
            shapes += [_sds((N_SHARD,) + w.shape, w.dtype) for w in group]
            gsems = [per_peer] * 4 + [pltpu.SemaphoreType.DMA((k,))]
            maker = _ag_copies
        elif kind == "agf":
            shapes += [_sds((N_SHARD,) + w.shape, w.dtype) for w in group]
            gsems = [per_peer] * 2 + [pltpu.SemaphoreType.DMA((k,))]
            maker = _ag_full_copies
        elif kind == "ag8":
            shapes += [_sds((8,) + g.shape, g.dtype) for g in group]
            gsems = [pltpu.SemaphoreType.DMA((k, 7))] * 2 + [pltpu.SemaphoreType.DMA((k,))]
            maker = _ag8_copies
        elif kind == "rs":
            shapes += [_sds((3,) + g.shape[1:], g.dtype) for g in group]
            gsems = [pltpu.SemaphoreType.DMA((k, 3)), pltpu.SemaphoreType.DMA((k, 3))]
            maker = _rs_copies
        else:
            shapes += [_sds(g.shape, g.dtype) for g in group]
            gsems = [pltpu.SemaphoreType.DMA((k,)), pltpu.SemaphoreType.DMA((k,))]
            maker = _swap_copies
        plan.append((maker, k, len(gsems)))
        sems += gsems
    return plan, arrs, shapes, sems


def _comm_fns(plan, cins, couts, sems):
    fns, a, s = [], 0, 0
    for maker, k, ns in plan:
        fns.append(maker(cins[a:a + k], couts[a:a + k], sems[s:s + ns]))
        a += k
        s += ns

    def start():
        for st, _, _ in fns:
            st()

    def forward():
        for _, fw, _ in fns:
            fw()

    def wait():
        for _, _, wt in fns:
            wt()

    return start, forward, wait


def _call(body, name, grid, in_specs, out_specs, out_shape, scratch, args, sem, comm=None):
    if not comm:
        return pl.pallas_call(body, name=name, grid=grid, in_specs=in_specs, out_specs=out_specs,
                              out_shape=out_shape, scratch_shapes=scratch, compiler_params=_cp(sem))(*args)
    plan, c_arrs, c_shapes, c_sems = _comm_plan(comm)
    k = len(c_arrs)
    n_in, n_out, n_scr = len(in_specs), len(out_specs), len(scratch)
    last = grid[0] - 1
    fwd_step = max(1, (2 * last) // 3)

    def wrapped(*refs):
        ins, cins = refs[:n_in], refs[n_in:n_in + k]
        o0 = n_in + k
        outs, couts = refs[o0:o0 + n_out], refs[o0 + n_out:o0 + n_out + k]
        s0 = o0 + n_out + k
        start, forward, wait = _comm_fns(plan, cins, couts, refs[s0 + n_scr:])
        pl.when(pl.program_id(0) == 0)(start)
        pl.when(pl.program_id(0) == fwd_step)(forward)
        body(*ins, *outs, *refs[s0:s0 + n_scr])
        pl.when(pl.program_id(0) == last)(wait)

    return pl.pallas_call(
        wrapped, name=name, grid=grid, in_specs=list(in_specs) + [_any()] * k,
        out_specs=list(out_specs) + [_any()] * k, out_shape=list(out_shape) + c_shapes,
        scratch_shapes=list(scratch) + c_sems, compiler_params=_cp(sem))(*args, *c_arrs)


def _comm_only(name, comm):
    plan, c_arrs, c_shapes, c_sems = _comm_plan(comm)
    k = len(c_arrs)

    def body(*refs):
        start, forward, wait = _comm_fns(plan, refs[:k], refs[k:2 * k], refs[2 * k:])
        start()
        forward()
        wait()

    return pl.pallas_call(body, name=name, in_specs=[_any()] * k, out_specs=[_any()] * k, out_shape=c_shapes,
                          scratch_shapes=c_sems, compiler_params=_cp())(*c_arrs)


def _any():
    return pl.BlockSpec(memory_space=pl.ANY)


def _rscan8(c8, d8, lnext):
    row = lax.broadcasted_iota(jnp.int32, c8.shape, 0)
    cc, dd = c8, d8
    for d in (1, 2, 4):
        c_s = pltpu.roll(cc, 8 - d, 0)
        d_s = pltpu.roll(dd, 8 - d, 0)
        m = row < 8 - d
        dd = jnp.where(m, cc * d_s + dd, dd)
        cc = jnp.where(m, cc * c_s, cc)
    return cc * lnext + dd


def _load_w_in_once(w_hbm, w_ref):
    @pl.when(pl.program_id(0) == 0)
    def _():
        for s in range(N_SHARD):
            pltpu.sync_copy(w_hbm.at[s], w_ref.at[:, pl.ds(s * IN_SH, IN_SH)])


def _f_inproj(x, g_mix, w_in_g, tm, comm=None):
    s_len = x.shape[0]
    pad_rows = LEFT_CHUNKS * CHUNK
    npad = pad_rows // tm

    def body(x_ref, g_ref, w_hbm, h_ref, qkv_ref, xg_ref, w_ref):
        i = pl.program_id(0)
        _load_w_in_once(w_hbm, w_ref)

        @pl.when(i < npad)
        def _():
            qkv_ref[...] = jnp.zeros_like(qkv_ref)

        @pl.when(i >= npad)
        def _():
            xv = x_ref[...]
            h = (xv * _rinv(xv) * g_ref[...]).astype(BF16)
            h_ref[...] = h
            proj = _dot(h, w_ref[...])
            qkv_ref[:, 0:D_ATT] = (proj[:, 0:D_ATT] * ATT_SCALE).astype(BF16)
            qkv_ref[:, D_ATT:3 * D_ATT] = proj[:, D_ATT:3 * D_ATT].astype(BF16)
            xg_ref[...] = proj[:, 3 * D_ATT:D_IN]

    def tok(n):
        return pl.BlockSpec((tm, n), lambda i: (jnp.maximum(i - npad, 0), 0))

    return _call(
        body, "f_inproj", (s_len // tm + npad,),
        [tok(1024), _full((1, 1024)), _any()],
        [tok(1024), _rows(tm, 1536), tok(1024)],
        [_sds((s_len, 1024), BF16), _sds((s_len + pad_rows, 1536), BF16), _sds((s_len, 1024), F32)],
        [pltpu.VMEM((1024, D_IN), BF16)], (x, g_mix, w_in_g), "arbitrary", comm)


N_BIAS = 3


def _bias_table(frow_ref, bias_sc):
    qa = lax.broadcasted_iota(jnp.int32, (QB, KB), 0) // CHUNK
    kcol = lax.broadcasted_iota(jnp.int32, (QB, KB), 1)
    kb = kcol // CHUNK
    band = jnp.where((kb >= qa) & (kb - qa <= LEFT_CHUNKS), 0.0, NEG).astype(F32)
    for h in range(ATT_HEADS):
        row = jnp.broadcast_to(frow_ref[h:h + 1, :], (QB, ROLL_W))
        toep = pltpu.roll(row, 0, 1, stride=1, stride_axis=0)
        gen = toep[:, 0:KB] + band
        bias_sc[N_BIAS - 1, h] = gen
        for v in range(N_BIAS - 1):
            pad_keys = LEFT_CHUNKS * CHUNK - v * QB
            bias_sc[v, h] = gen + jnp.where(kcol < pad_keys, NEG, 0.0).astype(F32)


def _even_lanes():
    return lax.broadcasted_iota(jnp.int32, (1, 2 * HEAD_DIM), 1) < HEAD_DIM


def _att_probs(qm, kts, bias):
    s = jnp.concatenate([_dot_nt(qm, k) for k in kts], axis=1) + bias
    return jnp.exp(s - jnp.max(s, axis=-1, keepdims=True))


def _att_in_specs(clamp):
    def spec(j, col):
        return pl.BlockSpec((QB, D_ATT), lambda i: (clamp(i) + j, col))
    return [spec(2, 0), spec(0, 1), spec(1, 1), spec(2, 1), spec(0, 2), spec(1, 2), spec(2, 2)]


def _f_attn(qkv_pad, frow, comm=None):
    s_len = qkv_pad.shape[0] - LEFT_CHUNKS * CHUNK
    nb = s_len // QB

    def body(q_ref, k0, k1, k2, v0, v1, v2, frow_ref, o_ref, bias_sc):
        i = pl.program_id(0)

        @pl.when(i == 0)
        def _():
            _bias_table(frow_ref, bias_sc)

        var = jnp.minimum(i, N_BIAS - 1)
        even = _even_lanes()
        for hp in range(ATT_HEADS // 2):
            cs = slice(hp * 2 * HEAD_DIM, (hp + 1) * 2 * HEAD_DIM)
            qt = q_ref[:, cs]
            kts = [k0[:, cs], k1[:, cs], k2[:, cs]]
            vts = [v0[:, cs], v1[:, cs], v2[:, cs]]
            res = []
            for e in range(2):
                keep = even if e == 0 else jnp.logical_not(even)
                pb = _att_probs(jnp.where(keep, qt, 0), kts, bias_sc[var, 2 * hp + e]).astype(BF16)
                r = _dot(pb[:, 0:QB], jnp.where(keep, vts[0], 1))
                for j in (1, 2):
                    r = r + _dot(pb[:, j * QB:(j + 1) * QB], jnp.where(keep, vts[j], 1))
                res.append(r / pltpu.roll(r, HEAD_DIM, 1))
            o_ref[:, cs] = jnp.where(even, res[0], res[1])

    return _call(
        body, "f_attn", (nb,),
        _att_in_specs(lambda i: i) + [_full((ATT_HEADS, ROLL_W))],
        [_rows(QB, D_ATT)], [_sds((s_len, D_ATT), F32)],
        [pltpu.VMEM((N_BIAS, ATT_HEADS, QB, KB), F32)], (*([qkv_pad] * 7), frow), "arbitrary", comm)


def _f_lru(xg, conv_w, conv_b, wrg, brg, wig, big, lam, tl, comm=None):
    s_len = xg.shape[0]

    def body(xg_ref, cw_ref, cb_ref, wrg_ref, brg_ref, wig_ref, big_ref, l_ref,
             rec_ref, u_ref, hs_ref, xbuf, a_sc, b_sc, hcar):
        i = pl.program_id(0)

        @pl.when(i == 0)
        def _():
            xbuf[0:8, :] = jnp.zeros((8, D_LRU), F32)
            hcar[...] = jnp.zeros((8, D_LRU), F32)

        xu0 = xg_ref[:, 0:D_LRU]
        xbuf[8:8 + tl, :] = xu0
        u = cb_ref[...] + cw_ref[0:1, :] * xbuf[pl.ds(5, tl), :]
        for j in range(1, 4):
            u = u + cw_ref[j:j + 1, :] * xbuf[pl.ds(5 + j, tl), :]
        xbuf[0:8, :] = xu0[tl - 8:tl, :]
        u_ref[...] = u
        _, _, ig, _, a, mult = _lru_gates(u, wrg_ref[...], brg_ref[...], wig_ref[...], big_ref[...], l_ref[...])
        a_sc[...] = a
        b_sc[...] = mult * (ig * u)

        def grp(g, hprev):
            off = pl.multiple_of(g * 8, 8)
            h8 = _scan8(a_sc[pl.ds(off, 8), :], b_sc[pl.ds(off, 8), :], hprev)
            hs_ref[pl.ds(off, 8), :] = h8
            return h8[7:8, :]

        hcar[0:1, :] = lax.fori_loop(0, tl // 8, grp, hcar[0:1, :])
        rec_ref[...] = hs_ref[...] * _gelu(xg_ref[:, D_LRU:2 * D_LRU])

    vec = _full((1, D_LRU))
    return _call(
        body, "f_lru", (s_len // tl,),
        [_rows(tl, 1024), _full((4, D_LRU)), vec, _full((D_LRU, D_LRU)), vec, _full((D_LRU, D_LRU)), vec, vec],
        [_rows(tl, D_LRU)] * 3, [_sds((s_len, D_LRU), F32)] * 3,
        [pltpu.VMEM((tl + 8, D_LRU), F32), pltpu.VMEM((tl, D_LRU), F32),
         pltpu.VMEM((tl, D_LRU), F32), pltpu.VMEM((8, D_LRU), F32)],
        (xg, conv_w, conv_b, wrg, brg, wig, big, lam), "arbitrary", comm)


def _f_mem(mem, g_mem, wk, wv):
    def body(mem_ref, g_ref, wk_ref, wv_ref, mn_ref, kx_ref, vx_ref):
        mv = mem_ref[...]
        mn = (mv * _rinv(mv) * g_ref[...]).astype(BF16)
        mn_ref[...] = mn
        kx_ref[...] = _dot(mn, wk_ref[...]).astype(BF16)
        vx_ref[...] = _dot(mn, wv_ref[...]).astype(BF16)

    m = mem.shape[0]
    return pl.pallas_call(
        body, name="f_mem", out_shape=[_sds((m, 1024), BF16)] * 3,
        compiler_params=_cp())(mem, g_mem, wk, wv)


def _xattn_probs(q, k):
    s = _dot_nt(q, k) * X_SCALE
    m = jnp.max(s, axis=-1, keepdims=True)
    p = jnp.exp(s - m)
    return p, jnp.sum(p, axis=-1, keepdims=True)


def _f_mid(x, att, rec, g_oa, g_ol, w_out, g_cross, wq, kx, vx, wo, tm, comm=None):
    s_len = x.shape[0]
    m_len = kx.shape[0]

    def body(x_ref, att_ref, rec_ref, goa_ref, gol_ref, wout_ref, gc_ref, wq_ref, kx_ref, vx_ref, wo_ref,
             mg_ref, x1_ref, hc_ref, qx_ref, ox_ref, x2_ref):
        av = att_ref[...]
        rv = rec_ref[...]
        mg_ref[:, 0:D_ATT] = (av * _rinv(av) * goa_ref[...]).astype(BF16)
        mg_ref[:, D_ATT:1024] = (rv * _rinv(rv) * gol_ref[...]).astype(BF16)
        x1 = x_ref[...] + _dot(mg_ref[...], wout_ref[...])
        x1_ref[...] = x1
        hc = (x1 * _rinv(x1) * gc_ref[...]).astype(BF16)
        hc_ref[...] = hc
        qx_ref[...] = _dot(hc, wq_ref[...]).astype(BF16)
        for h in range(X_HEADS):
            sl = slice(h * X_HEAD_DIM, (h + 1) * X_HEAD_DIM)
            p, l = _xattn_probs(qx_ref[:, sl], kx_ref[:, sl])
            ox_ref[:, sl] = (_dot(p.astype(BF16), vx_ref[:, sl]) / l).astype(BF16)
        x2_ref[...] = x1 + _dot(ox_ref[...], wo_ref[...])

    sq = _full((1024, 1024))
    return _call(
        body, "f_mid", (s_len // tm,),
        [_rows(tm, 1024), _rows(tm, 512), _rows(tm, 512), _full((1, 512)), _full((1, 512)), sq,
         _full((1, 1024)), sq, _full((m_len, 1024)), _full((m_len, 1024)), sq],
        [_rows(tm, 1024)] * 6,
        [_sds((s_len, 1024), BF16), _sds((s_len, 1024), F32), _sds((s_len, 1024), BF16),
         _sds((s_len, 1024), BF16), _sds((s_len, 1024), BF16), _sds((s_len, 1024), F32)],
        [], (x, att, rec, g_oa, g_ol, w_out, g_cross, wq, kx, vx, wo), "arbitrary", comm)


def _load_weights_once(pairs):
    @pl.when(pl.program_id(0) == 0)
    def _():
        for hbm, vmem in pairs:
            pltpu.sync_copy(hbm, vmem)


FF_CHUNKS = [(0, 1280), (1280, D_FF)]


def _f_ffn(x2, tgt, g_ffn, g_final, wg, wu, wd, tm):
    s_len = x2.shape[0]

    def body(x2_ref, t_ref, gf_ref, gfin_ref, wg_hbm, wu_hbm, wd_hbm,
             hf_ref, g_ref, u_ref, a_ref, dx3_ref, loss_ref, dgfin_ref, wg_ref, wu_ref, wd_ref):
        _load_weights_once([(wg_hbm, wg_ref), (wu_hbm, wu_ref), (wd_hbm, wd_ref)])

        @pl.when(pl.program_id(0) == 0)
        def _():
            loss_ref[...] = jnp.zeros_like(loss_ref)
            dgfin_ref[...] = jnp.zeros_like(dgfin_ref)

        x2v = x2_ref[...]
        hf = (x2v * _rinv(x2v) * gf_ref[...]).astype(BF16)
        hf_ref[...] = hf
        x3 = x2v
        for c0, c1 in FF_CHUNKS:
            gv = _dot_nt(hf, wg_ref[c0:c1, :])
            uv = _dot_nt(hf, wu_ref[c0:c1, :])
            av = (gv * jax.nn.sigmoid(gv) * uv).astype(BF16)
            g_ref[:, c0:c1] = gv.astype(BF16)
            u_ref[:, c0:c1] = uv.astype(BF16)
            a_ref[:, c0:c1] = av
            x3 = x3 + _dot(av, wd_ref[c0:c1, :])
        r3 = _rinv(x3)
        yh = x3 * r3
        gfin = gfin_ref[...]
        err = yh * gfin - t_ref[...]
        loss_ref[...] += jnp.full((1, 128), 0.5 / D_MODEL, F32) * jnp.sum(err * err)
        dy = err * (1.0 / D_MODEL)
        dgfin_ref[...] += jnp.sum(dy * yh, axis=0, keepdims=True)
        dyh = dy * gfin
        dx3_ref[...] = r3 * (dyh - yh * jnp.mean(dyh * yh, axis=-1, keepdims=True))

    vec = _full((1, 1024))
    return pl.pallas_call(
        body, name="f_ffn", grid=(s_len // tm,),
        in_specs=[_rows(tm, 1024), _rows(tm, 1024), vec, vec, _any(), _any(), _any()],
        out_specs=[_rows(tm, 1024), _rows(tm, D_FF), _rows(tm, D_FF), _rows(tm, D_FF),
                   _rows(tm, 1024), _full((1, 128)), vec],
        out_shape=[_sds((s_len, 1024), BF16)] + [_sds((s_len, D_FF), BF16)] * 3
                  + [_sds((s_len, 1024), F32), _sds((1, 128), F32), _sds((1, 1024), F32)],
        scratch_shapes=[pltpu.VMEM((D_FF, 1024), BF16)] * 3,
        compiler_params=_cp("arbitrary"))(x2, tgt, g_ffn, g_final, wg, wu, wd)


def _b_ffn(dx3, x2, gact, uact, g_ffn, wg, wu, wd, tm):
    s_len = x2.shape[0]

    def body(dx3_ref, x2_ref, g_ref, u_ref, gf_ref, wg_hbm, wu_hbm, wd_hbm,
             dg_ref, du_ref, dx2_ref, dgf_ref, wg_ref, wu_ref, wd_ref):
        _load_weights_once([(wg_hbm, wg_ref), (wu_hbm, wu_ref), (wd_hbm, wd_ref)])

        @pl.when(pl.program_id(0) == 0)
        def _():
            dgf_ref[...] = jnp.zeros_like(dgf_ref)

        dx3v = dx3_ref[...]
        dx3b = dx3v.astype(BF16)
        dhf = jnp.zeros(dx3v.shape, F32)
        for c0, c1 in FF_CHUNKS:
            da = _dot_nt(dx3b, wd_ref[c0:c1, :])
            gv = g_ref[:, c0:c1].astype(F32)
            uv = u_ref[:, c0:c1].astype(F32)
            sg = jax.nn.sigmoid(gv)
            dub = (da * gv * sg).astype(BF16)
            dgb = (da * uv * (sg * (1.0 + gv * (1.0 - sg)))).astype(BF16)
            du_ref[:, c0:c1] = dub
            dg_ref[:, c0:c1] = dgb
            dhf = dhf + _dot(dgb, wg_ref[c0:c1, :]) + _dot(dub, wu_ref[c0:c1, :])
        dx, dgf = _rms_bwd(dhf, x2_ref[...], gf_ref[...])
        dx2_ref[...] = dx3v + dx
        dgf_ref[...] += dgf

    vec = _full((1, 1024))
    return pl.pallas_call(
        body, name="b_ffn", grid=(s_len // tm,),
        in_specs=[_rows(tm, 1024), _rows(tm, 1024), _rows(tm, D_FF), _rows(tm, D_FF), vec,
                  _any(), _any(), _any()],
        out_specs=[_rows(tm, D_FF), _rows(tm, D_FF), _rows(tm, 1024), vec],
        out_shape=[_sds((s_len, D_FF), BF16)] * 2 + [_sds((s_len, 1024), F32), _sds((1, 1024), F32)],
        scratch_shapes=[pltpu.VMEM((D_FF, 1024), BF16)] * 3,
        compiler_params=_cp("arbitrary"))(dx3, x2, gact, uact, g_ffn, wg, wu, wd)


def _b_mid(dx2, qx, x1, att, rec, kx, vx, wo, wq, w_out, g_cross, g_oa, g_ol, tm, comm=None):
    s_len = x1.shape[0]
    m_len = kx.shape[0]

    def body(dx2_ref, qx_ref, x1_ref, att_ref, rec_ref, kx_ref, vx_ref, wo_ref, wq_ref, wout_ref,
             gc_ref, goa_ref, gol_ref,
             dqx_ref, dx1_ref, datt_ref, drec_ref, dkx_ref, dvx_ref, dgc_ref, dgoa_ref, dgol_ref):
        @pl.when(pl.program_id(0) == 0)
        def _():
            for r in (dkx_ref, dvx_ref, dgc_ref, dgoa_ref, dgol_ref):
                r[...] = jnp.zeros_like(r)

        dx2v = dx2_ref[...]
        dox = _dot_nt(dx2v.astype(BF16), wo_ref[...])
        for h in range(X_HEADS):
            sl = slice(h * X_HEAD_DIM, (h + 1) * X_HEAD_DIM)
            q = qx_ref[:, sl]
            p, l = _xattn_probs(q, kx_ref[:, sl])
            pn = p / l
            dob = dox[:, sl].astype(BF16)
            dp = _dot_nt(dob, vx_ref[:, sl])
            dvx_ref[:, sl] += _dot_tn(pn.astype(BF16), dob)
            ds = pn * (dp - jnp.sum(dp * pn, axis=-1, keepdims=True))
            dsb = (ds * X_SCALE).astype(BF16)
            dqx_ref[:, sl] = _dot(dsb, kx_ref[:, sl]).astype(BF16)
            dkx_ref[:, sl] += _dot_tn(dsb, q)
        dhc = _dot_nt(dqx_ref[...], wq_ref[...])
        dx, dgc = _rms_bwd(dhc, x1_ref[...], gc_ref[...])
        dx1 = dx2v + dx
        dx1_ref[...] = dx1
        dgc_ref[...] += dgc
        dmg = _dot_nt(dx1.astype(BF16), wout_ref[...])
        da, dgoa = _rms_bwd(dmg[:, 0:D_ATT], att_ref[...], goa_ref[...])
        datt_ref[...] = da
        dgoa_ref[...] += dgoa
        dr, dgol = _rms_bwd(dmg[:, D_ATT:1024], rec_ref[...], gol_ref[...])
        drec_ref[...] = dr
        dgol_ref[...] += dgol

    sq = _full((1024, 1024))
    mk = _full((m_len, 1024))
    return _call(
        body, "b_mid", (s_len // tm,),
        [_rows(tm, 1024), _rows(tm, 1024), _rows(tm, 1024), _rows(tm, 512), _rows(tm, 512), mk, mk,
         sq, sq, sq, _full((1, 1024)), _full((1, 512)), _full((1, 512))],
        [_rows(tm, 1024), _rows(tm, 1024), _rows(tm, 512), _rows(tm, 512), mk, mk,
         _full((1, 1024)), _full((1, 512)), _full((1, 512))],
        [_sds((s_len, 1024), BF16), _sds((s_len, 1024), F32), _sds((s_len, 512), F32),
         _sds((s_len, 512), F32), _sds((m_len, 1024), F32), _sds((m_len, 1024), F32),
         _sds((1, 1024), F32), _sds((1, 512), F32), _sds((1, 512), F32)],
        [], (dx2, qx, x1, att, rec, kx, vx, wo, wq, w_out, g_cross, g_oa, g_ol), "arbitrary", comm)


def _b_mem(dkx, dvx, mem, mn, g_mem, wk, wv):
    def body(dkx_ref, dvx_ref, mem_ref, mn_ref, g_ref, wk_ref, wv_ref, dwk_ref, dwv_ref, dgm_ref,
             dwkb_ref, dwvb_ref):
        dkb = dkx_ref[...].astype(BF16)
        dvb = dvx_ref[...].astype(BF16)
        dwk = _dot_tn(mn_ref[...], dkb)
        dwv = _dot_tn(mn_ref[...], dvb)
        dwk_ref[...] = dwk
        dwv_ref[...] = dwv
        dwkb_ref[...] = dwk.astype(BF16)
        dwvb_ref[...] = dwv.astype(BF16)
        dmn = _dot_nt(dkb, wk_ref[...]) + _dot_nt(dvb, wv_ref[...])
        mv = mem_ref[...]
        dgm_ref[...] = jnp.sum(dmn * (mv * _rinv(mv)), axis=0, keepdims=True)

    return pl.pallas_call(
        body, name="b_mem",
        out_shape=[_sds((1024, 1024), F32), _sds((1024, 1024), F32), _sds((1, 1024), F32),
                   _sds((1024, 1024), BF16), _sds((1024, 1024), BF16)],
        compiler_params=_cp())(dkx, dvx, mem, mn, g_mem, wk, wv)


def _b_lru(drec, hs, u, xg, conv_w, wrg, brg, wig, big, lam, tl, comm=None):
    s_len = xg.shape[0]
    nt = s_len // tl

    def body(drec_ref, hs_ref, hsp_ref, u_ref, xg_ref, cw_ref, wrg_ref, brg_ref, wig_ref, big_ref, l_ref,
             dxg_ref, dwrg_ref, dwig_ref, dbrg_ref, dbig_ref, dlam_ref, dcw_ref, dcb_ref,
             hbuf, abuf, dubuf, c_sc, d_sc, lam_sc, lcar, wacc_r, wacc_i):
        i = pl.program_id(0)
        tt = nt - 1 - i

        @pl.when(i == 0)
        def _():
            for r in (wacc_r, wacc_i, dbrg_ref, dbig_ref, dlam_ref, dcw_ref, dcb_ref):
                r[...] = jnp.zeros_like(r)
            abuf[tl:tl + 8, :] = jnp.zeros((8, D_LRU), F32)
            dubuf[tl:tl + 8, :] = jnp.zeros((8, D_LRU), F32)
            lcar[...] = jnp.zeros((8, D_LRU), F32)

        xu0 = xg_ref[:, 0:D_LRU]
        hsv = hs_ref[...]
        uv = u_ref[...]
        hbuf[8:8 + tl, :] = hsv
        hbuf[0:8, :] = jnp.where(tt > 0, hsp_ref[...], 0.0)
        hshift = hbuf[pl.ds(7, tl), :]
        wrg_v = wrg_ref[...]
        wig_v = wig_ref[...]
        lamv = l_ref[...]
        ub, r, ig, sp, a, mult = _lru_gates(uv, wrg_v, brg_ref[...], wig_v, big_ref[...], lamv)
        abuf[0:tl, :] = a
        c_sc[...] = abuf[pl.ds(1, tl), :]
        gel, dgel = _gelu_and_grad(xg_ref[:, D_LRU:2 * D_LRU])
        drv = drec_ref[...]
        d_sc[...] = drv * gel
        dxg_ref[:, D_LRU:2 * D_LRU] = (drv * hsv * dgel).astype(BF16)

        def grp(k, lnext):
            off = pl.multiple_of((tl // 8 - 1 - k) * 8, 8)
            l8 = _rscan8(c_sc[pl.ds(off, 8), :], d_sc[pl.ds(off, 8), :], lnext)
            lam_sc[pl.ds(off, 8), :] = l8
            return l8[0:1, :]

        lcar[0:1, :] = lax.fori_loop(0, tl // 8, grp, lcar[0:1, :])
        abuf[tl:tl + 8, :] = a[0:8, :]
        db = lam_sc[...]
        da = db * hshift
        dmult = db * (ig * uv)
        dig = db * mult * uv
        du = db * mult * ig
        dla = da * a - dmult * (a * a) / mult
        dlam_ref[...] += jnp.sum(dla * (-LRU_C) * r, axis=0, keepdims=True)
        dzr = dla * (-LRU_C * sp) * r * (1.0 - r)
        dzi = dig * ig * (1.0 - ig)
        dzrb = dzr.astype(BF16)
        dzib = dzi.astype(BF16)
        du = du + _dot_nt(dzrb, wrg_v) + _dot_nt(dzib, wig_v)
        wacc_r[...] += _dot_tn(ub, dzrb)
        wacc_i[...] += _dot_tn(ub, dzib)
        dbrg_ref[...] += jnp.sum(dzr, axis=0, keepdims=True)
        dbig_ref[...] += jnp.sum(dzi, axis=0, keepdims=True)
        dcb_ref[...] += jnp.sum(du, axis=0, keepdims=True)
        dubuf[0:tl, :] = du
        dxu0 = jnp.zeros((tl, D_LRU), F32)
        for j in range(4):
            dsh = dubuf[pl.ds(3 - j, tl), :]
            dxu0 = dxu0 + cw_ref[j:j + 1, :] * dsh
            dcw_ref[j:j + 1, :] += jnp.sum(xu0 * dsh, axis=0, keepdims=True)
        dubuf[tl:tl + 8, :] = du[0:8, :]
        dxg_ref[:, 0:D_LRU] = dxu0.astype(BF16)

        @pl.when(i == nt - 1)
        def _():
            dlam_ref[...] = dlam_ref[...] * (-jax.nn.sigmoid(-lamv))
            for n in range(LRU_BLOCKS):
                blk = slice(n * LRU_BLOCK, (n + 1) * LRU_BLOCK)
                dwrg_ref[n] = wacc_r[blk, blk]
                dwig_ref[n] = wacc_i[blk, blk]

    def rev(n):
        return pl.BlockSpec((tl, n), lambda i: (nt - 1 - i, 0))

    prev8 = pl.BlockSpec((8, D_LRU), lambda i: (jnp.maximum((nt - 1 - i) * (tl // 8) - 1, 0), 0))
    vec = _full((1, D_LRU))
    sq = _full((D_LRU, D_LRU))
    blocks_shape = (LRU_BLOCKS, LRU_BLOCK, LRU_BLOCK)
    blocks = _full(blocks_shape)
    return _call(
        body, "b_lru", (nt,),
        [rev(D_LRU), rev(D_LRU), prev8, rev(D_LRU), rev(1024), _full((4, D_LRU)), sq, vec, sq, vec, vec],
        [rev(1024), blocks, blocks, vec, vec, vec, _full((4, D_LRU)), vec],
        [_sds((s_len, 1024), BF16), _sds(blocks_shape, F32), _sds(blocks_shape, F32),
         _sds((1, D_LRU), F32), _sds((1, D_LRU), F32), _sds((1, D_LRU), F32),
         _sds((4, D_LRU), F32), _sds((1, D_LRU), F32)],
        [pltpu.VMEM((tl + 8, D_LRU), F32)] * 3 + [pltpu.VMEM((tl, D_LRU), F32)] * 3
        + [pltpu.VMEM((8, D_LRU), F32)] + [pltpu.VMEM((D_LRU, D_LRU), F32)] * 2,
        (drec, hs, hs, u, xg, conv_w, wrg, brg, wig, big, lam), "arbitrary", comm)


def _b_attn(qkv_pad, att, datt, frow, comm=None):
    s_len = datt.shape[0]
    nb = s_len // QB
    n_pair = ATT_HEADS // 2
    pair_w = 2 * HEAD_DIM

    def body(q_ref, k0, k1, k2, v0, v1, v2, o_ref, do_ref, frow_ref, dq_ref, dkv_ref, dfrow_ref,
             bias_sc, dt_sc, acc_sc):
        t = pl.program_id(0)

        @pl.when(t == 0)
        def _():
            _bias_table(frow_ref, bias_sc)
            dt_sc[...] = jnp.zeros_like(dt_sc)
            acc_sc[...] = jnp.zeros_like(acc_sc)

        @pl.when(t < nb)
        def _():
            var = jnp.minimum(t, N_BIAS - 1)
            even = _even_lanes()
            for hp in range(n_pair):
                cs = slice(hp * pair_w, (hp + 1) * pair_w)
                qt = q_ref[:, cs]
                kts = [k0[:, cs], k1[:, cs], k2[:, cs]]
                vts = [v0[:, cs], v1[:, cs], v2[:, cs]]
                dot = do_ref[:, cs]
                dd = dot * o_ref[:, cs]
                qmt, dost, dsbs, pbs, dqs = [], [], [], [], []
                for e in range(2):
                    keep = even if e == 0 else jnp.logical_not(even)
                    qm = jnp.where(keep, qt, 0)
                    p = _att_probs(qm, kts, bias_sc[var, 2 * hp + e])
                    inv = 1.0 / jnp.sum(p, axis=-1, keepdims=True)
                    dos = jnp.where(keep, dot * inv, 0.0)
                    delta = jnp.sum(jnp.where(keep, dd, 0.0), axis=-1, keepdims=True) * inv
                    dp = jnp.concatenate([_dot_nt(dos.astype(BF16), v) for v in vts], axis=1)
                    ds = p * (dp - delta)
                    dt_sc[2 * hp + e] += ds
                    dsb = ds.astype(BF16)
                    dq = _dot(dsb[:, 0:QB], kts[0])
                    for j in (1, 2):
                        dq = dq + _dot(dsb[:, j * QB:(j + 1) * QB], kts[j])
                    dqs.append(dq)
                    dsbs.append(dsb)
                    pbs.append(p.astype(BF16))
                    qmt.append(qm.astype(F32).T.astype(BF16))
                    dost.append(dos.T.astype(BF16))
                dq_ref[:, cs] = (jnp.where(even, dqs[0], dqs[1]) * ATT_SCALE).astype(BF16)
                for j in range(3):
                    slot = (t + 1 + j) % 3
                    js = slice(j * QB, (j + 1) * QB)
                    acc_sc[slot, hp] += _dot(qmt[0], dsbs[0][:, js]) + _dot(qmt[1], dsbs[1][:, js])
                    acc_sc[slot, n_pair + hp] += _dot(dost[0], pbs[0][:, js]) + _dot(dost[1], pbs[1][:, js])

        done = (t + 1) % 3

        @pl.when(t >= 2)
        def _():
            for i in range(2 * n_pair):
                dkv_ref[:, i * pair_w:(i + 1) * pair_w] = acc_sc[done, i].T.astype(BF16)

        acc_sc[done] = jnp.zeros((2 * n_pair, pair_w, QB), F32)

        @pl.when(t == nb + 1)
        def _():
            row = lax.broadcasted_iota(jnp.int32, (8, ROLL_W), 0)
            pad = jnp.zeros((8, ROLL_W - KB), F32)
            for h in range(ATT_HEADS):
                acc8 = jnp.concatenate([dt_sc[h, 0:8, :], pad], axis=1)
                for a1 in range(1, QB // 8):
                    blk = jnp.concatenate([dt_sc[h, 8 * a1:8 * a1 + 8, :], pad], axis=1)
                    acc8 = acc8 + pltpu.roll(blk, ROLL_W - 8 * a1, 1)
                for k in range(3):
                    acc8 = jnp.where(((row >> k) & 1) == 1, pltpu.roll(acc8, ROLL_W - (1 << k), 1), acc8)
                dfrow_ref[h:h + 1, :] = jnp.sum(acc8, axis=0, keepdims=True)

    clamp = lambda t: jnp.minimum(t, nb - 1)
    qrows = pl.BlockSpec((QB, D_ATT), lambda t: (clamp(t), 0))
    return _call(
        body, "b_attn", (nb + 2,),
        _att_in_specs(clamp) + [qrows, qrows, _full((ATT_HEADS, ROLL_W))],
        [qrows, pl.BlockSpec((QB, 2 * D_ATT), lambda t: (jnp.maximum(t - 2, 0), 0)),
         _full((ATT_HEADS, ROLL_W))],
        [_sds((s_len, D_ATT), BF16), _sds((s_len, 2 * D_ATT), BF16), _sds((ATT_HEADS, ROLL_W), F32)],
        [pltpu.VMEM((N_BIAS, ATT_HEADS, QB, KB), F32), pltpu.VMEM((ATT_HEADS, QB, KB), F32),
         pltpu.VMEM((3, 2 * n_pair, pair_w, QB), F32)],
        (*([qkv_pad] * 7), att, datt, frow), "arbitrary", comm)


def _b_win(dq, dkv, dxg, h, ts):
    s_len = h.shape[0]
    steps = s_len // ts

    def body(dq_ref, dkv_ref, dxg_ref, h_ref, dw_ref, dwb_ref):
        @pl.when(pl.program_id(0) == 0)
        def _():
            dw_ref[...] = jnp.zeros_like(dw_ref)

        dproj = jnp.concatenate([dq_ref[...], dkv_ref[...], dxg_ref[...]], axis=1)
        hv = h_ref[...]
        for s in range(N_SHARD):
            dw_ref[s] += _dot_tn(hv, dproj[:, s * IN_SH:(s + 1) * IN_SH])

        @pl.when(pl.program_id(0) == steps - 1)
        def _():
            dwb_ref[...] = dw_ref[...].astype(BF16)

    wspec = _full((N_SHARD, 1024, IN_SH))
    return pl.pallas_call(
        body, name="b_win", grid=(steps,),
        in_specs=[_rows(ts, 512), _rows(ts, 1024), _rows(ts, 1024), _rows(ts, 1024)],
        out_specs=[wspec, wspec],
        out_shape=[_sds((N_SHARD, 1024, IN_SH), F32), _sds((N_SHARD, 1024, IN_SH), BF16)],
        compiler_params=_cp("arbitrary"))(dq, dkv, dxg, h)


def _b_inproj(dq, dkv, dxg, x, dx1, g_mix, w_in_g, tm, comm=None):
    s_len = x.shape[0]

    def body(dq_ref, dkv_ref, dxg_ref, x_ref, dx1_ref, g_ref, w_hbm, gx_ref, dgm_ref, w_ref):
        _load_w_in_once(w_hbm, w_ref)

        @pl.when(pl.program_id(0) == 0)
        def _():
            dgm_ref[...] = jnp.zeros_like(dgm_ref)

        dproj = jnp.concatenate([dq_ref[...], dkv_ref[...], dxg_ref[...]], axis=1)
        dh = _dot_nt(dproj, w_ref[...])
        dx, dgm = _rms_bwd(dh, x_ref[...], g_ref[...])
        gx_ref[...] = dx1_ref[...] + dx
        dgm_ref[...] += dgm

    return _call(
        body, "b_inproj", (s_len // tm,),
        [_rows(tm, 512), _rows(tm, 1024), _rows(tm, 1024), _rows(tm, 1024), _rows(tm, 1024),
         _full((1, 1024)), _any()],
        [_rows(tm, 1024), _full((1, 1024))],
        [_sds((s_len, 1024), F32), _sds((1, 1024), F32)],
        [pltpu.VMEM((1024, D_IN), BF16)], (dq, dkv, dxg, x, dx1, g_mix, w_in_g), "arbitrary", comm)


def _mm_tn(xa, ya, name, ts):
    s_len, k = xa.shape
    n = ya.shape[1]

    steps = s_len // ts

    def body(x_ref, y_ref, o_ref, ob_ref):
        @pl.when(pl.program_id(0) == 0)
        def _():
            o_ref[...] = jnp.zeros_like(o_ref)
        o_ref[...] += _dot_tn(x_ref[...].astype(BF16), y_ref[...].astype(BF16))

        @pl.when(pl.program_id(0) == steps - 1)
        def _():
            ob_ref[...] = o_ref[...].astype(BF16)

    return pl.pallas_call(
        body, name=name, grid=(steps,), in_specs=[_rows(ts, k), _rows(ts, n)],
        out_specs=[_full((k, n))] * 2, out_shape=[_sds((k, n), F32), _sds((k, n), BF16)],
        compiler_params=_cp("arbitrary"))(xa, ya)


def _frow_from_rel_bias(rb):
    hi = jnp.broadcast_to(rb[:, 256:257], (ATT_HEADS, 385))
    mid = rb[:, 1:256][:, ::-1]
    lo = jnp.broadcast_to(rb[:, 0:1], (ATT_HEADS, 128))
    wrap = jnp.broadcast_to(rb[:, 256:257], (ATT_HEADS, ROLL_W - KB))
    return jnp.concatenate([hi, mid, lo, wrap], axis=1)


def _rel_bias_grad_from_dfrow(df):
    g256 = jnp.sum(df[:, 0:385], axis=1, keepdims=True) + jnp.sum(df[:, KB:ROLL_W], axis=1, keepdims=True)
    mid = df[:, 385:640][:, ::-1]
    g0 = jnp.sum(df[:, 640:KB], axis=1, keepdims=True)
    return jnp.concatenate([g0, mid, g256], axis=1)


def _block_diag(w):
    eye = jnp.eye(8, dtype=w.dtype)
    return (w[:, :, None, :] * eye[:, None, :, None]).reshape(D_LRU, D_LRU)


MID = ['w_out', 'wq_c', 'wk_c', 'wv_c', 'wo_c']
TRANSPOSED = ['w_gate', 'w_up']
AG_IN_INPROJ = ['w_out', 'wq_c', 'wk_c']
AG_IN_ATTN = ['wv_c', 'wo_c', 'w_gate']
AG_IN_LRU = ['w_up']
AG_IN_MID = ['w_down']
RS_IN_MID = ['w_gate', 'w_up']
RS_IN_LRU = ['w_down']
RS_IN_ATTN = MID


def _local_step(x, mem, tgt, p, gw, shards=None, chip=None):
    s_len = x.shape[0]
    tm = min(256, s_len)
    tmb = min(512, s_len)
    tl = min(512, s_len)
    frow = _frow_from_rel_bias(p['rel_bias'])
    wrg = _block_diag(p['w_rg']).astype(BF16)
    wig = _block_diag(p['w_ig']).astype(BF16)
    gw = dict(gw)

    big, bigb, recv, part, sib = {}, {}, {}, {}, {}

    def ag(names):
        return [] if shards is None else [("ag", [shards[n] for n in names])]

    def rs(names):
        return [] if shards is None else [("rs", [bigb[n] for n in names])]

    def swap(names):
        return [] if shards is None else [("swap", [part[n] for n in names])]

    def reduce_own(names):
        if shards is not None:
            for n in names:
                part[n] = _sum_parts(big[n], recv[n], chip, "sum_" + n)

    h, qkv_pad, xg, *got = _f_inproj(x, p['g_mix'], gw['w_in'], tmb, ag(AG_IN_INPROJ))
    gw.update(zip(AG_IN_INPROJ, got))
    att, *got = _f_attn(qkv_pad, frow, ag(AG_IN_ATTN))
    gw.update(zip(AG_IN_ATTN, got))
    rec, u, hs, *got = _f_lru(xg, p['conv_w'], p['conv_b'], wrg, p['b_rg'], wig, p['b_ig'], p['lru_L'], tl,
                              ag(AG_IN_LRU))
    gw.update(zip(AG_IN_LRU, got))
    w_out = gw['w_out'].reshape(1024, 1024)
    wq = gw['wq_c'].reshape(1024, 1024)
    wk = gw['wk_c'].reshape(1024, 1024)
    wv = gw['wv_c'].reshape(1024, 1024)
    wo = gw['wo_c'].reshape(1024, 1024)
    mn, kx, vx = _f_mem(mem, p['g_mem'], wk, wv)
    mg, x1, hc, qx, ox, x2, *got = _f_mid(x, att, rec, p['g_out_attn'], p['g_out_lru'], w_out, p['g_cross'],
                                          wq, kx, vx, wo, tmb, ag(AG_IN_MID))
    gw.update(zip(AG_IN_MID, got))
    ffn_w = [gw[n].reshape(D_FF, 1024) for n in ('w_gate', 'w_up', 'w_down')]
    hf, gact, uact, aact, dx3, loss, dg_final = _f_ffn(x2, tgt, p['g_ffn'], p['g_final'], *ffn_w, tmb)

    ts = min(512, s_len)
    dgact, duact, dx2, dg_ffn = _b_ffn(dx3, x2, gact, uact, p['g_ffn'], *ffn_w, tm)
    big['w_gate'], bigb['w_gate'] = _mm_tn(dgact, hf, "dw_gate", ts)
    big['w_up'], bigb['w_up'] = _mm_tn(duact, hf, "dw_up", ts)
    big['w_down'], bigb['w_down'] = _mm_tn(aact, dx3, "dw_down", ts)
    for n in ('w_gate', 'w_up', 'w_down'):
        big[n] = big[n].reshape(N_SHARD, FF_SH, 1024)
        bigb[n] = bigb[n].reshape(N_SHARD, FF_SH, 1024)

    dqx, dx1, datt, drec, dkx, dvx, dg_cross, dg_oa, dg_ol, *got = _b_mid(
        dx2, qx, x1, att, rec, kx, vx, wo, wq, w_out, p['g_cross'], p['g_out_attn'], p['g_out_lru'], tmb,
        rs(RS_IN_MID))
    recv.update(zip(RS_IN_MID, got))
    reduce_own(RS_IN_MID)
    dwk, dwv, dg_mem, dwkb, dwvb = _b_mem(dkx, dvx, mem, mn, p['g_mem'], wk, wv)
    big['wk_c'], bigb['wk_c'] = dwk, dwkb
    big['wv_c'], bigb['wv_c'] = dwv, dwvb
    big['w_out'], bigb['w_out'] = _mm_tn(mg, dx1, "dw_out", ts)
    big['wq_c'], bigb['wq_c'] = _mm_tn(hc, dqx, "dw_q", ts)
    big['wo_c'], bigb['wo_c'] = _mm_tn(ox, dx2, "dw_o", ts)
    for n in MID:
        big[n] = big[n].reshape(N_SHARD, 256, 1024)
        bigb[n] = bigb[n].reshape(N_SHARD, 256, 1024)

    dxg, dwrg, dwig, dbrg, dbig, dlam, dcw, dcb, *got = _b_lru(
        drec, hs, u, xg, p['conv_w'], wrg, p['b_rg'], wig, p['b_ig'], p['lru_L'], tl,
        rs(RS_IN_LRU) + swap(RS_IN_MID))
    recv.update(zip(RS_IN_LRU, got))
    sib.update(zip(RS_IN_MID, got[len(RS_IN_LRU):]))
    reduce_own(RS_IN_LRU)
    dq, dkv, dfrow, *got = _b_attn(qkv_pad, att, datt, frow, rs(RS_IN_ATTN) + swap(RS_IN_LRU))
    recv.update(zip(RS_IN_ATTN, got))
    sib.update(zip(RS_IN_LRU, got[len(RS_IN_ATTN):]))
    reduce_own(RS_IN_ATTN)
    big['w_in'], bigb['w_in'] = _b_win(dq, dkv, dxg, h, ts)
    small = {
        'rel_bias': _rel_bias_grad_from_dfrow(dfrow), 'conv_w': dcw, 'conv_b': dcb,
        'w_rg': dwrg, 'b_rg': dbrg, 'w_ig': dwig, 'b_ig': dbig,
        'lru_L': dlam,
        'g_out_attn': dg_oa, 'g_out_lru': dg_ol, 'g_cross': dg_cross, 'g_mem': dg_mem, 'g_ffn': dg_ffn,
        'g_final': dg_final,
    }
    names = [n for n in SMALL if n in small]
    gather = [] if shards is None else [("ag8", [_pack_small(names, [small[n] for n in names], loss)])]
    grad_x, small['g_mix'], *got = _b_inproj(dq, dkv, dxg, x, dx1, p['g_mix'], gw['w_in'], tmb,
                                             rs(['w_in']) + swap(RS_IN_ATTN) + gather)
    recv.update(zip(['w_in'], got))
    sib.update(zip(RS_IN_ATTN, got[1:]))
    reduce_own(['w_in'])
    packs = got[-1] if gather else None
    return loss, grad_x, small, big, part, sib, packs


def _cast_shards(ws):
    def body(*refs):
        n = len(refs) // 2
        for src, dst in zip(refs[:n], refs[n:]):
            dst[...] = src[...].astype(BF16)

    return pl.pallas_call(body, name="cast_shards", out_shape=[_sds(w.shape, BF16) for w in ws],
                          compiler_params=_cp())(*ws)


def _sum_parts(own4, recv3, chip, name):
    _, r, c = own4.shape
    tr = r // 4

    def body(chip_ref, own_ref, rc_ref, o_ref):
        o_ref[...] = ((own_ref[0] + rc_ref[0].astype(F32)) + rc_ref[1].astype(F32)) + rc_ref[2].astype(F32)

    grid_spec = pltpu.PrefetchScalarGridSpec(
        num_scalar_prefetch=1, grid=(4,),
        in_specs=[pl.BlockSpec((1, tr, c), lambda i, ch: (ch[0], i, 0)),
                  pl.BlockSpec((3, tr, c), lambda i, ch: (0, i, 0))],
        out_specs=pl.BlockSpec((tr, c), lambda i, ch: (i, 0)))
    return pl.pallas_call(body, name=name, grid_spec=grid_spec, out_shape=_sds((r, c), F32),
                          compiler_params=_cp("parallel"))(chip, own4, recv3)


def _adamw_math(w, g, m, v):
    m = ADAM_B1 * m + (1.0 - ADAM_B1) * g
    v = ADAM_B2 * v + (1.0 - ADAM_B2) * (g * g)
    m_hat = m / (1.0 - ADAM_B1 ** ADAM_STEP)
    v_hat = v / (1.0 - ADAM_B2 ** ADAM_STEP)
    delta = -ADAM_LR * (m_hat / (jnp.sqrt(v_hat) + ADAM_EPS) + ADAM_WD * w)
    return delta, m, v


def _final_adamw(pa, pb, w, m, v, name):
    r, c = w.shape
    tr = r // 4

    def body(pa_ref, pb_ref, w_ref, m_ref, v_ref, g_ref, d_ref, nm_ref, nv_ref):
        g = pa_ref[...] + pb_ref[...]
        g_ref[...] = g
        d_ref[...], nm_ref[...], nv_ref[...] = _adamw_math(w_ref[...], g, m_ref[...], v_ref[...])

    return pl.pallas_call(
        body, name=name, grid=(4,), in_specs=[_rows(tr, c)] * 5, out_specs=[_rows(tr, c)] * 4,
        out_shape=[_sds((r, c), F32)] * 4, compiler_params=_cp("parallel"))(pa, pb, w, m, v)


def _pack_put(ref, name, val_ref):
    r = _pack_rows()[name]
    shape = val_ref.shape
    if len(shape) == 3:
        for b in range(shape[0]):
            ref[r:r + shape[1], b * shape[2]:(b + 1) * shape[2]] = val_ref[b]
    elif shape[1] == 2 * PACK_W:
        ref[r:r + 1, :] = val_ref[:, 0:PACK_W]
        ref[r + 1:r + 2, :] = val_ref[:, PACK_W:2 * PACK_W]
    else:
        ref[r:r + shape[0], 0:shape[1]] = val_ref[...]


def _pack_get(ref, name, shape):
    r = _pack_rows()[name]
    if len(shape) == 3:
        return jnp.stack([ref[r:r + shape[1], b * shape[2]:(b + 1) * shape[2]] for b in range(shape[0])])
    if shape[1] == 2 * PACK_W:
        return jnp.concatenate([ref[r:r + 1, :], ref[r + 1:r + 2, :]], axis=1)
    return ref[r:r + shape[0], 0:shape[1]]


def _pack_small(names, g, loss):
    n = len(g)

    def body(*refs):
        pack = refs[n + 1]
        pack[...] = jnp.zeros_like(pack)
        for a, name in enumerate(names):
            _pack_put(pack, name, refs[a])
        _pack_put(pack, 'loss', refs[n])

    return pl.pallas_call(body, name="pack_small", out_shape=_sds((PACK_ROWS, PACK_W), F32),
                          compiler_params=_cp())(*g, loss)


def _all_peers():
    x, y, c = _mesh_pos()
    peers = []
    for k in range(1, 8):
        px = 1 - x if k & 4 else x
        py = 1 - y if k & 2 else y
        pc = 1 - c if k & 1 else c
        peers.append(((px, py, pc), 4 * px + 2 * py + pc))
    return peers, 4 * x + 2 * y + c


def _ag8_copies(ins, outs, sems):
    send_sems, recv_sems, loc_sems = sems
    n = len(ins)
    peers, me = _all_peers()

    def remote(k, j, slot):
        return pltpu.make_async_remote_copy(
            src_ref=ins[k], dst_ref=outs[k].at[slot], send_sem=send_sems.at[k, j], recv_sem=recv_sems.at[k, j],
            device_id=peers[j][0], device_id_type=MESH_ID)

    def local(k):
        return pltpu.make_async_copy(ins[k], outs[k].at[me], loc_sems.at[k])

    def start():
        for k in range(n):
            local(k).start()
            for j in range(7):
                remote(k, j, me).start()

    def wait():
        for k in range(n):
            for j in range(7):
                remote(k, j, peers[j][1]).wait_recv()
        for k in range(n):
            for j in range(7):
                remote(k, j, me).wait_send()
            local(k).wait()

    return start, _no_forward, wait


def _ar_row(g):
    def body(g_ref, tot_ref, buf, send_sems, recv_sems):
        peers, me = _all_peers()

        def remote(j, slot):
            return pltpu.make_async_remote_copy(
                src_ref=g_ref, dst_ref=buf.at[slot], send_sem=send_sems.at[j], recv_sem=recv_sems.at[j],
                device_id=peers[j][0], device_id_type=MESH_ID)

        for j in range(7):
            remote(j, me).start()
        buf[me] = g_ref[...]
        for j in range(7):
            remote(j, peers[j][1]).wait_recv()
        for j in range(7):
            remote(j, me).wait_send()
        tot = buf[0]
        for d in range(1, 8):
            tot = tot + buf[d]
        tot_ref[...] = tot

    return pl.pallas_call(
        body, name="ar_row", out_shape=_sds(g.shape, F32),
        scratch_shapes=[pltpu.VMEM((8,) + g.shape, F32), pltpu.SemaphoreType.DMA((7,)),
                        pltpu.SemaphoreType.DMA((7,))],
        compiler_params=_cp())(g)


def _adamw_small(packs, late_name, late_tot, g_shapes, loss_shape, w, m, v):
    n = len(w)

    def body(*refs):
        packs_ref, late_ref = refs[0], refs[1]
        w_refs, m_refs, v_refs = (refs[2 + i * n:2 + (i + 1) * n] for i in range(3))
        o0 = 3 * n + 2
        go, do, mo, vo = (refs[o0 + i * n:o0 + (i + 1) * n] for i in range(4))
        loss_out, tot_ref = refs[o0 + 4 * n], refs[o0 + 4 * n + 1]
        x, y, _ = _mesh_pos()
        tot = packs_ref[0]
        for d in range(1, 8):
            tot = tot + packs_ref[d]
        tot_ref[...] = tot
        _pack_put(tot_ref, late_name, late_ref)
        loss_out[...] = _pack_get(tot_ref, 'loss', loss_shape)
        for a, name in enumerate(SMALL):
            if name == 'conv_w':
                r = _pack_rows()[name]
                ga = tot_ref[r:r + g_shapes[a][0], pl.ds(pl.multiple_of((2 * x + y) * 128, 128), 128)]
            else:
                ga = _pack_get(tot_ref, name, g_shapes[a])
            go[a][...] = ga
            do[a][...], mo[a][...], vo[a][...] = _adamw_math(w_refs[a][...], ga, m_refs[a][...], v_refs[a][...])

    out_shape = [_sds(a.shape, F32) for a in w] * 4 + [_sds(loss_shape, F32)]
    return pl.pallas_call(body, name="adamw_small", out_shape=out_shape,
                          scratch_shapes=[pltpu.VMEM((PACK_ROWS, PACK_W), F32)],
                          compiler_params=_cp())(packs, late_tot, *w, *m, *v)


PACK_W = 512
PACK_ROWS = 160


def _pack_rows():
    rows, r = {}, 0
    for name in ['g_mix', 'g_cross', 'g_mem', 'g_ffn', 'g_final']:
        rows[name] = r
        r += 2
    for name in ['conv_b', 'b_rg', 'b_ig', 'lru_L', 'g_out_attn', 'g_out_lru']:
        rows[name] = r
        r += 1
    rows['conv_w'] = r
    rows['loss'] = r + 4
    rows['rel_bias'] = 24
    rows['w_rg'] = 32
    rows['w_ig'] = 32 + LRU_BLOCK
    assert r + 5 <= 24 and rows['w_ig'] + LRU_BLOCK == PACK_ROWS
    return rows


INPUT_NAMES = (['x', 'mem'] + WEIGHTS + ['loss_target'] + ['m_' + n for n in WEIGHTS] + ['v_' + n for n in WEIGHTS])


def kernel(x, mem, g_mix, w_in, rel_bias, conv_w, conv_b, w_rg, b_rg, w_ig, b_ig, lru_L, g_out_attn, g_out_lru, w_out, g_cross, g_mem, wq_c, wk_c, wv_c, wo_c, g_ffn, w_gate, w_up, w_down, g_final, loss_target, m_g_mix, m_w_in, m_rel_bias, m_conv_w, m_conv_b, m_w_rg, m_b_rg, m_w_ig, m_b_ig, m_lru_L, m_g_out_attn, m_g_out_lru, m_w_out, m_g_cross, m_g_mem, m_wq_c, m_wk_c, m_wv_c, m_wo_c, m_g_ffn, m_w_gate, m_w_up, m_w_down, m_g_final, v_g_mix, v_w_in, v_rel_bias, v_conv_w, v_conv_b, v_w_rg, v_b_rg, v_w_ig, v_b_ig, v_lru_L, v_g_out_attn, v_g_out_lru, v_w_out, v_g_cross, v_g_mem, v_wq_c, v_wk_c, v_wv_c, v_wo_c, v_g_ffn, v_w_gate, v_w_up, v_w_down, v_g_final):
    a = dict(zip(INPUT_NAMES, (x, mem, g_mix, w_in, rel_bias, conv_w, conv_b, w_rg, b_rg, w_ig, b_ig, lru_L, g_out_attn, g_out_lru, w_out, g_cross, g_mem, wq_c, wk_c, wv_c, wo_c, g_ffn, w_gate, w_up, w_down, g_final, loss_target, m_g_mix, m_w_in, m_rel_bias, m_conv_w, m_conv_b, m_w_rg, m_b_rg, m_w_ig, m_b_ig, m_lru_L, m_g_out_attn, m_g_out_lru, m_w_out, m_g_cross, m_g_mem, m_wq_c, m_wk_c, m_wv_c, m_wo_c, m_g_ffn, m_w_gate, m_w_up, m_w_down, m_g_final, v_g_mix, v_w_in, v_rel_bias, v_conv_w, v_conv_b, v_w_rg, v_b_rg, v_w_ig, v_b_ig, v_lru_L, v_g_out_attn, v_g_out_lru, v_w_out, v_g_cross, v_g_mem, v_wq_c, v_wk_c, v_wv_c, v_wo_c, v_g_ffn, v_w_gate, v_w_up, v_w_down, v_g_final)))
    chip = 2 * lax.axis_index("x") + lax.axis_index("y")

    def shard(name):
        arr = a[name][0]
        return jnp.swapaxes(arr, 0, 1) if name[2:] in TRANSPOSED or name in TRANSPOSED else arr

    shards = dict(zip(BIG, _cast_shards([shard(n) for n in BIG])))
    w_in_g, conv_w_g = _comm_only("ag_w_in", [("ag", [shards['w_in']]), ("agf", [a['conv_w'][0]])])
    conv_w_full = conv_w_g.transpose(1, 0, 2).reshape(4, D_LRU)

    p = {n: a[n] for n in SMALL}
    p['rel_bias'] = a['rel_bias'][0]
    p['w_rg'] = a['w_rg'][0]
    p['w_ig'] = a['w_ig'][0]
    p['conv_w'] = conv_w_full
    p['g_final'] = a['g_final'][None, :]
    chip_arr = jnp.reshape(chip, (1,)).astype(jnp.int32)
    loss_part, grad_x, small, _, part, sib, packs = _local_step(
        a['x'][0], a['mem'][0], a['loss_target'][0], p, {'w_in': w_in_g}, shards, chip_arr)

    sib['w_in'], = _comm_only("swap_w_in", [("swap", [part['w_in']])])
    out = {}
    for n in BIG:
        res = _final_adamw(part[n], sib[n], shard(n), shard('m_' + n), shard('v_' + n), "adamw_" + n)
        out[n] = [jnp.swapaxes(r, 0, 1) for r in res] if n in TRANSPOSED else res

    def natural(arr):
        return arr[0] if arr.ndim >= 3 else (arr[None, :] if arr.ndim == 1 else arr)

    small_out = _adamw_small(packs, 'g_mix', _ar_row(small['g_mix']), [small[n].shape for n in SMALL],
                             loss_part.shape, *[[natural(a[pre + n]) for n in SMALL] for pre in ('', 'm_', 'v_')])
    ns = len(SMALL)
    loss = small_out[4 * ns][0, 0]

    def leaf(i, n):
        if n in BIG:
            return out[n][i][None]
        return small_out[i * ns + SMALL.index(n)].reshape(a[n].shape)

    return (loss, grad_x[None], *[leaf(i, n) for i in range(4) for n in WEIGHTS])
```

```python
import math

import jax
import jax.numpy as jnp
from jax import lax
from jax.experimental import pallas as pl
from jax.experimental.pallas import tpu as pltpu

F32 = jnp.float32
BF16 = jnp.bfloat16

D_MODEL = 1024
D_ATT = 512
D_LRU = 512
HEAD_DIM = 64
ATT_HEADS = 8
CHUNK = 64
LEFT_CHUNKS = 8
X_HEADS = 4
X_HEAD_DIM = 256
N_SHARD = 4
IN_SH = 640
D_IN = N_SHARD * IN_SH
FF_SH = 704
D_FF = N_SHARD * FF_SH
EPS = 1e-6
LRU_C = 8.0
LRU_BLOCKS = 8
LRU_BLOCK = 64
QB = 256
KB = 768
ROLL_W = 1024
NEG = -1e30
ATT_SCALE = HEAD_DIM ** -0.5
X_SCALE = X_HEAD_DIM ** -0.5

ADAM_LR = 0.001
ADAM_B1 = 0.9
ADAM_B2 = 0.999
ADAM_EPS = 1e-08
ADAM_WD = 0.01
ADAM_STEP = 10

VMEM_LIMIT_V7X = 56 * 1024 * 1024
MESH_ID = pl.DeviceIdType.MESH

WEIGHTS = ['g_mix', 'w_in', 'rel_bias', 'conv_w', 'conv_b', 'w_rg', 'b_rg', 'w_ig', 'b_ig', 'lru_L',
           'g_out_attn', 'g_out_lru', 'w_out', 'g_cross', 'g_mem', 'wq_c', 'wk_c', 'wv_c', 'wo_c',
           'g_ffn', 'w_gate', 'w_up', 'w_down', 'g_final']
BIG = ['w_in', 'w_out', 'wq_c', 'wk_c', 'wv_c', 'wo_c', 'w_gate', 'w_up', 'w_down']
SMALL = [n for n in WEIGHTS if n not in BIG]


def _sds(shape, dtype):
    return jax.ShapeDtypeStruct(shape, dtype)


def _cp(*sem):
    return pltpu.CompilerParams(dimension_semantics=sem or None, vmem_limit_bytes=VMEM_LIMIT_V7X)


def _rows(tm, n):
    return pl.BlockSpec((tm, n), lambda i: (i, 0))


def _full(shape):
    nd = len(shape)
    return pl.BlockSpec(shape, lambda i: (0,) * nd)


def _dot(a, b):
    return jnp.dot(a, b, preferred_element_type=F32)


def _dot_nt(a, b):
    return lax.dot_general(a, b, (((1,), (1,)), ((), ())), preferred_element_type=F32)


def _dot_tn(a, b):
    return lax.dot_general(a, b, (((0,), (0,)), ((), ())), preferred_element_type=F32)


def _rinv(x):
    return lax.rsqrt(jnp.mean(x * x, axis=-1, keepdims=True) + EPS)


def _rms_bwd(dy, x, g):
    r = _rinv(x)
    yh = x * r
    dyh = dy * g
    dx = r * (dyh - yh * jnp.mean(dyh * yh, axis=-1, keepdims=True))
    return dx, jnp.sum(dy * yh, axis=0, keepdims=True)


def _gelu(x):
    c = math.sqrt(2.0 / math.pi)
    t = jnp.tanh(c * (x + 0.044715 * x * x * x))
    return 0.5 * x * (1.0 + t)


def _gelu_and_grad(x):
    c = math.sqrt(2.0 / math.pi)
    t = jnp.tanh(c * (x + 0.044715 * x * x * x))
    g = 0.5 * x * (1.0 + t)
    dg = 0.5 * (1.0 + t) + 0.5 * x * (1.0 - t * t) * c * (1.0 + 3.0 * 0.044715 * x * x)
    return g, dg


def _neg_expm1(z):
    series = -z * (1 + z / 2 * (1 + z / 3 * (1 + z / 4)))
    return jnp.where(z > -0.03, series, 1.0 - jnp.exp(z))


def _lru_gates(u, wrg, brg, wig, big, lam):
    ub = u.astype(BF16)
    r = jax.nn.sigmoid(_dot(ub, wrg) + brg)
    ig = jax.nn.sigmoid(_dot(ub, wig) + big)
    sp = jnp.maximum(-lam, 0.0) + jnp.log1p(jnp.exp(-jnp.abs(lam)))
    la = -LRU_C * r * sp
    a = jnp.exp(la)
    mult = jnp.sqrt(jnp.maximum(_neg_expm1(2.0 * la), 0.0))
    return ub, r, ig, sp, a, mult


def _scan8(a8, b8, hprev):
    row = lax.broadcasted_iota(jnp.int32, a8.shape, 0)
    aa, bb = a8, b8
    for d in (1, 2, 4):
        a_s = pltpu.roll(aa, d, 0)
        b_s = pltpu.roll(bb, d, 0)
        m = row >= d
        bb = jnp.where(m, aa * b_s + bb, bb)
        aa = jnp.where(m, aa * a_s, aa)
    return aa * hprev + bb


def _mesh_pos():
    return lax.axis_index("x"), lax.axis_index("y"), lax.axis_index("c")


def _other_chips(x, y):
    return [(1 - x, y), (x, 1 - y), (1 - x, 1 - y)]


def _no_forward():
    pass


def _ag_full_copies(ins, outs, sems):
    send_sems, recv_sems, loc_sems = sems
    n = len(ins)
    x, y, c = _mesh_pos()
    mine = 2 * x + y
    chips = _other_chips(x, y)

    def remote(k, j, slot):
        px, py = chips[j]
        return pltpu.make_async_remote_copy(
            src_ref=ins[k], dst_ref=outs[k].at[slot], send_sem=send_sems.at[k, j], recv_sem=recv_sems.at[k, j],
            device_id=(px, py, c), device_id_type=MESH_ID)

    def local(k):
        return pltpu.make_async_copy(ins[k], outs[k].at[mine], loc_sems.at[k])

    def start():
        for k in range(n):
            local(k).start()
            for j in range(3):
                remote(k, j, mine).start()

    def wait():
        for k in range(n):
            for j, (px, py) in enumerate(chips):
                remote(k, j, 2 * px + py).wait_recv()
        for k in range(n):
            for j in range(3):
                remote(k, j, mine).wait_send()
            local(k).wait()

    return start, _no_forward, wait


def _ag_copies(ins, outs, sems):
    send_sems, recv_sems, fsend_sems, frecv_sems, loc_sems = sems
    n = len(ins)
    x, y, c = _mesh_pos()
    mine = 2 * x + y
    chips = _other_chips(x, y)

    def half(ref, hc):
        r = ref.shape[0] // 2
        return ref.at[pl.ds(pl.multiple_of(hc * r, 16), r)]

    def ici(k, j, slot):
        px, py = chips[j]
        return pltpu.make_async_remote_copy(
            src_ref=half(ins[k], c), dst_ref=half(outs[k].at[slot], c),
            send_sem=send_sems.at[k, j], recv_sem=recv_sems.at[k, j],
            device_id=(px, py, c), device_id_type=MESH_ID)

    def d2d(k, j, hc):
        px, py = chips[j]
        part = half(outs[k].at[2 * px + py], hc)
        return pltpu.make_async_remote_copy(
            src_ref=part, dst_ref=part, send_sem=fsend_sems.at[k, j], recv_sem=frecv_sems.at[k, j],
            device_id=(x, y, 1 - c), device_id_type=MESH_ID)

    def local(k):
        return pltpu.make_async_copy(ins[k], outs[k].at[mine], loc_sems.at[k])

    def start():
        for k in range(n):
            local(k).start()
            for j in range(3):
                ici(k, j, mine).start()

    def forward():
        for k in range(n):
            for j, (px, py) in enumerate(chips):
                ici(k, j, 2 * px + py).wait_recv()
                d2d(k, j, c).start()

    def wait():
        for k in range(n):
            for j in range(3):
                d2d(k, j, 1 - c).wait_recv()
        for k in range(n):
            for j in range(3):
                d2d(k, j, c).wait_send()
                ici(k, j, mine).wait_send()
            local(k).wait()

    return start, forward, wait


def _rs_copies(ins, outs, sems):
    send_sems, recv_sems = sems
    n = len(ins)
    x, y, c = _mesh_pos()
    chips = _other_chips(x, y)

    def remote(k, j):
        px, py = chips[j]
        return pltpu.make_async_remote_copy(
            src_ref=ins[k].at[2 * px + py], dst_ref=outs[k].at[j],
            send_sem=send_sems.at[k, j], recv_sem=recv_sems.at[k, j],
            device_id=(px, py, c), device_id_type=MESH_ID)

    def start():
        for k in range(n):
            for j in range(3):
                remote(k, j).start()

    def wait():
        for k in range(n):
            for j in range(3):
                remote(k, j).wait_recv()
        for k in range(n):
            for j in range(3):
                remote(k, j).wait_send()

    return start, _no_forward, wait


def _swap_copies(ins, outs, sems):
    send_sems, recv_sems = sems
    x, y, c = _mesh_pos()
    copies = [pltpu.make_async_remote_copy(
        src_ref=ins[k], dst_ref=outs[k], send_sem=send_sems.at[k], recv_sem=recv_sems.at[k],
        device_id=(x, y, 1 - c), device_id_type=MESH_ID) for k in range(len(ins))]

    def start():
        for cp in copies:
            cp.start()

    def wait():
        for cp in copies:
            cp.wait()

    return start, _no_forward, wait


def _comm_plan(groups):
    plan, arrs, shapes, sems = [], [], [], []
    for kind, group in groups:
        k = len(group)
        arrs += group
        per_peer = pltpu.SemaphoreType.DMA((k, 3))
        if kind == "ag":
            shapes += [_sds((N_SHARD,) + w.shape, w.dtype) for w in group]
            gsems = [per_peer] * 4 + [pltpu.SemaphoreType.DMA((k,))]
            maker = _ag_copies
        elif kind == "agf":
            shapes += [_sds((N_SHARD,) + w.shape, w.dtype) for w in group]
            gsems = [per_peer] * 2 + [pltpu.SemaphoreType.DMA((k,))]
            maker = _ag_full_copies
        elif kind == "ag8":
            shapes += [_sds((8,) + g.shape, g.dtype) for g in group]
            gsems = [pltpu.SemaphoreType.DMA((k, 7))] * 2 + [pltpu.SemaphoreType.DMA((k,))]
            maker = _ag8_copies
        elif kind == "rs":
            shapes += [_sds((3,) + g.shape[1:], g.dtype) for g in group]
            gsems = [pltpu.SemaphoreType.DMA((k, 3)), pltpu.SemaphoreType.DMA((k, 3))]
            maker = _rs_copies
        else:
            shapes += [_sds(g.shape, g.dtype) for g in group]
            gsems = [pltpu.SemaphoreType.DMA((k,)), pltpu.SemaphoreType.DMA((k,))]
            maker = _swap_copies
        plan.append((maker, k, len(gsems)))
        sems += gsems
    return plan, arrs, shapes, sems


def _comm_fns(plan, cins, couts, sems):
    fns, a, s = [], 0, 0
    for maker, k, ns in plan:
        fns.append(maker(cins[a:a + k], couts[a:a + k], sems[s:s + ns]))
        a += k
        s += ns

    def start():
        for st, _, _ in fns:
            st()

    def forward():
        for _, fw, _ in fns:
            fw()

    def wait():
        for _, _, wt in fns:
            wt()

    return start, forward, wait


def _call(body, name, grid, in_specs, out_specs, out_shape, scratch, args, sem, comm=None):
    if not comm:
        return pl.pallas_call(body, name=name, grid=grid, in_specs=in_specs, out_specs=out_specs,
                              out_shape=out_shape, scratch_shapes=scratch, compiler_params=_cp(sem))(*args)
    plan, c_arrs, c_shapes, c_sems = _comm_plan(comm)
    k = len(c_arrs)
    n_in, n_out, n_scr = len(in_specs), len(out_specs), len(scratch)
    last = grid[0] - 1
    fwd_step = max(1, (2 * last) // 3)

    def wrapped(*refs):
        ins, cins = refs[:n_in], refs[n_in:n_in + k]
        o0 = n_in + k
        outs, couts = refs[o0:o0 + n_out], refs[o0 + n_out:o0 + n_out + k]
        s0 = o0 + n_out + k
        start, forward, wait = _comm_fns(plan, cins, couts, refs[s0 + n_scr:])
        pl.when(pl.program_id(0) == 0)(start)
        pl.when(pl.program_id(0) == fwd_step)(forward)
        body(*ins, *outs, *refs[s0:s0 + n_scr])
        pl.when(pl.program_id(0) == last)(wait)

    return pl.pallas_call(
        wrapped, name=name, grid=grid, in_specs=list(in_specs) + [_any()] * k,
        out_specs=list(out_specs) + [_any()] * k, out_shape=list(out_shape) + c_shapes,
        scratch_shapes=list(scratch) + c_sems, compiler_params=_cp(sem))(*args, *c_arrs)


def _comm_only(name, comm):
    plan, c_arrs, c_shapes, c_sems = _comm_plan(comm)
    k = len(c_arrs)

    def body(*refs):
        start, forward, wait = _comm_fns(plan, refs[:k], refs[k:2 * k], refs[2 * k:])
        start()
        forward()
        wait()

    return pl.pallas_call(body, name=name, in_specs=[_any()] * k, out_specs=[_any()] * k, out_shape=c_shapes,
                          scratch_shapes=c_sems, compiler_params=_cp())(*c_arrs)


def _any():
    return pl.BlockSpec(memory_space=pl.ANY)


def _rscan8(c8, d8, lnext):
    row = lax.broadcasted_iota(jnp.int32, c8.shape, 0)
    cc, dd = c8, d8
    for d in (1, 2, 4):
        c_s = pltpu.roll(cc, 8 - d, 0)
        d_s = pltpu.roll(dd, 8 - d, 0)
        m = row < 8 - d
        dd = jnp.where(m, cc * d_s + dd, dd)
        cc = jnp.where(m, cc * c_s, cc)
    return cc * lnext + dd


def _load_w_in_once(w_hbm, w_ref):
    @pl.when(pl.program_id(0) == 0)
    def _():
        for s in range(N_SHARD):
            pltpu.sync_copy(w_hbm.at[s], w_ref.at[:, pl.ds(s * IN_SH, IN_SH)])


def _f_inproj(x, g_mix, w_in_g, tm, comm=None):
    s_len = x.shape[0]
    pad_rows = LEFT_CHUNKS * CHUNK
    npad = pad_rows // tm

    def body(x_ref, g_ref, w_hbm, h_ref, qkv_ref, xg_ref, w_ref):
        i = pl.program_id(0)
        _load_w_in_once(w_hbm, w_ref)

        @pl.when(i < npad)
        def _():
            qkv_ref[...] = jnp.zeros_like(qkv_ref)

        @pl.when(i >= npad)
        def _():
            xv = x_ref[...]
            h = (xv * _rinv(xv) * g_ref[...]).astype(BF16)
            h_ref[...] = h
            proj = _dot(h, w_ref[...])
            qkv_ref[:, 0:D_ATT] = (proj[:, 0:D_ATT] * ATT_SCALE).astype(BF16)
            qkv_ref[:, D_ATT:3 * D_ATT] = proj[:, D_ATT:3 * D_ATT].astype(BF16)
            xg_ref[...] = proj[:, 3 * D_ATT:D_IN]

    def tok(n):
        return pl.BlockSpec((tm, n), lambda i: (jnp.maximum(i - npad, 0), 0))

    return _call(
        body, "f_inproj", (s_len // tm + npad,),
        [tok(1024), _full((1, 1024)), _any()],
        [tok(1024), _rows(tm, 1536), tok(1024)],
        [_sds((s_len, 1024), BF16), _sds((s_len + pad_rows, 1536), BF16), _sds((s_len, 1024), F32)],
        [pltpu.VMEM((1024, D_IN), BF16)], (x, g_mix, w_in_g), "arbitrary", comm)


N_BIAS = 3


def _bias_table(frow_ref, bias_sc):
    qa = lax.broadcasted_iota(jnp.int32, (QB, KB), 0) // CHUNK
    kcol = lax.broadcasted_iota(jnp.int32, (QB, KB), 1)
    kb = kcol // CHUNK
    band = jnp.where((kb >= qa) & (kb - qa <= LEFT_CHUNKS), 0.0, NEG).astype(F32)
    for h in range(ATT_HEADS):
        row = jnp.broadcast_to(frow_ref[h:h + 1, :], (QB, ROLL_W))
        toep = pltpu.roll(row, 0, 1, stride=1, stride_axis=0)
        gen = toep[:, 0:KB] + band
        bias_sc[N_BIAS - 1, h] = gen
        for v in range(N_BIAS - 1):
            pad_keys = LEFT_CHUNKS * CHUNK - v * QB
            bias_sc[v, h] = gen + jnp.where(kcol < pad_keys, NEG, 0.0).astype(F32)


def _even_lanes():
    return lax.broadcasted_iota(jnp.int32, (1, 2 * HEAD_DIM), 1) < HEAD_DIM


def _att_probs(qm, kts, bias):
    s = jnp.concatenate([_dot_nt(qm, k) for k in kts], axis=1) + bias
    return jnp.exp(s - jnp.max(s, axis=-1, keepdims=True))


def _att_in_specs(clamp):
    def spec(j, col):
        return pl.BlockSpec((QB, D_ATT), lambda i: (clamp(i) + j, col))
    return [spec(2, 0), spec(0, 1), spec(1, 1), spec(2, 1), spec(0, 2), spec(1, 2), spec(2, 2)]


def _f_attn(qkv_pad, frow, comm=None):
    s_len = qkv_pad.shape[0] - LEFT_CHUNKS * CHUNK
    nb = s_len // QB

    def body(q_ref, k0, k1, k2, v0, v1, v2, frow_ref, o_ref, bias_sc):
        i = pl.program_id(0)

        @pl.when(i == 0)
        def _():
            _bias_table(frow_ref, bias_sc)

        var = jnp.minimum(i, N_BIAS - 1)
        even = _even_lanes()
        for hp in range(ATT_HEADS // 2):
            cs = slice(hp * 2 * HEAD_DIM, (hp + 1) * 2 * HEAD_DIM)
            qt = q_ref[:, cs]
            kts = [k0[:, cs], k1[:, cs], k2[:, cs]]
            vts = [v0[:, cs], v1[:, cs], v2[:, cs]]
            res = []
            for e in range(2):
                keep = even if e == 0 else jnp.logical_not(even)
                pb = _att_probs(jnp.where(keep, qt, 0), kts, bias_sc[var, 2 * hp + e]).astype(BF16)
                r = _dot(pb[:, 0:QB], jnp.where(keep, vts[0], 1))
                for j in (1, 2):
                    r = r + _dot(pb[:, j * QB:(j + 1) * QB], jnp.where(keep, vts[j], 1))
                res.append(r / pltpu.roll(r, HEAD_DIM, 1))
            o_ref[:, cs] = jnp.where(even, res[0], res[1])

    return _call(
        body, "f_attn", (nb,),
        _att_in_specs(lambda i: i) + [_full((ATT_HEADS, ROLL_W))],
        [_rows(QB, D_ATT)], [_sds((s_len, D_ATT), F32)],
        [pltpu.VMEM((N_BIAS, ATT_HEADS, QB, KB), F32)], (*([qkv_pad] * 7), frow), "arbitrary", comm)


def _f_lru(xg, conv_w, conv_b, wrg, brg, wig, big, lam, tl, comm=None):
    s_len = xg.shape[0]

    def body(xg_ref, cw_ref, cb_ref, wrg_ref, brg_ref, wig_ref, big_ref, l_ref,
             rec_ref, u_ref, hs_ref, xbuf, a_sc, b_sc, hcar):
        i = pl.program_id(0)

        @pl.when(i == 0)
        def _():
            xbuf[0:8, :] = jnp.zeros((8, D_LRU), F32)
            hcar[...] = jnp.zeros((8, D_LRU), F32)

        xu0 = xg_ref[:, 0:D_LRU]
        xbuf[8:8 + tl, :] = xu0
        u = cb_ref[...] + cw_ref[0:1, :] * xbuf[pl.ds(5, tl), :]
        for j in range(1, 4):
            u = u + cw_ref[j:j + 1, :] * xbuf[pl.ds(5 + j, tl), :]
        xbuf[0:8, :] = xu0[tl - 8:tl, :]
        u_ref[...] = u
        _, _, ig, _, a, mult = _lru_gates(u, wrg_ref[...], brg_ref[...], wig_ref[...], big_ref[...], l_ref[...])
        a_sc[...] = a
        b_sc[...] = mult * (ig * u)

        def grp(g, hprev):
            off = pl.multiple_of(g * 8, 8)
            h8 = _scan8(a_sc[pl.ds(off, 8), :], b_sc[pl.ds(off, 8), :], hprev)
            hs_ref[pl.ds(off, 8), :] = h8
            return h8[7:8, :]

        hcar[0:1, :] = lax.fori_loop(0, tl // 8, grp, hcar[0:1, :])
        rec_ref[...] = hs_ref[...] * _gelu(xg_ref[:, D_LRU:2 * D_LRU])

    vec = _full((1, D_LRU))
    return _call(
        body, "f_lru", (s_len // tl,),
        [_rows(tl, 1024), _full((4, D_LRU)), vec, _full((D_LRU, D_LRU)), vec, _full((D_LRU, D_LRU)), vec, vec],
        [_rows(tl, D_LRU)] * 3, [_sds((s_len, D_LRU), F32)] * 3,
        [pltpu.VMEM((tl + 8, D_LRU), F32), pltpu.VMEM((tl, D_LRU), F32),
         pltpu.VMEM((tl, D_LRU), F32), pltpu.VMEM((8, D_LRU), F32)],
        (xg, conv_w, conv_b, wrg, brg, wig, big, lam), "arbitrary", comm)


def _f_mem(mem, g_mem, wk, wv):
    def body(mem_ref, g_ref, wk_ref, wv_ref, mn_ref, kx_ref, vx_ref):
        mv = mem_ref[...]
        mn = (mv * _rinv(mv) * g_ref[...]).astype(BF16)
        mn_ref[...] = mn
        kx_ref[...] = _dot(mn, wk_ref[...]).astype(BF16)
        vx_ref[...] = _dot(mn, wv_ref[...]).astype(BF16)

    m = mem.shape[0]
    return pl.pallas_call(
        body, name="f_mem", out_shape=[_sds((m, 1024), BF16)] * 3,
        compiler_params=_cp())(mem, g_mem, wk, wv)


def _xattn_probs(q, k):
    s = _dot_nt(q, k) * X_SCALE
    m = jnp.max(s, axis=-1, keepdims=True)
    p = jnp.exp(s - m)
    return p, jnp.sum(p, axis=-1, keepdims=True)


def _f_mid(x, att, rec, g_oa, g_ol, w_out, g_cross, wq, kx, vx, wo, tm, comm=None):
    s_len = x.shape[0]
    m_len = kx.shape[0]

    def body(x_ref, att_ref, rec_ref, goa_ref, gol_ref, wout_ref, gc_ref, wq_ref, kx_ref, vx_ref, wo_ref,
             mg_ref, x1_ref, hc_ref, qx_ref, ox_ref, x2_ref):
        av = att_ref[...]
        rv = rec_ref[...]
        mg_ref[:, 0:D_ATT] = (av * _rinv(av) * goa_ref[...]).astype(BF16)
        mg_ref[:, D_ATT:1024] = (rv * _rinv(rv) * gol_ref[...]).astype(BF16)
        x1 = x_ref[...] + _dot(mg_ref[...], wout_ref[...])
        x1_ref[...] = x1
        hc = (x1 * _rinv(x1) * gc_ref[...]).astype(BF16)
        hc_ref[...] = hc
        qx_ref[...] = _dot(hc, wq_ref[...]).astype(BF16)
        for h in range(X_HEADS):
            sl = slice(h * X_HEAD_DIM, (h + 1) * X_HEAD_DIM)
            p, l = _xattn_probs(qx_ref[:, sl], kx_ref[:, sl])
            ox_ref[:, sl] = (_dot(p.astype(BF16), vx_ref[:, sl]) / l).astype(BF16)
        x2_ref[...] = x1 + _dot(ox_ref[...], wo_ref[...])

    sq = _full((1024, 1024))
    return _call(
        body, "f_mid", (s_len // tm,),
        [_rows(tm, 1024), _rows(tm, 512), _rows(tm, 512), _full((1, 512)), _full((1, 512)), sq,
         _full((1, 1024)), sq, _full((m_len, 1024)), _full((m_len, 1024)), sq],
        [_rows(tm, 1024)] * 6,
        [_sds((s_len, 1024), BF16), _sds((s_len, 1024), F32), _sds((s_len, 1024), BF16),
         _sds((s_len, 1024), BF16), _sds((s_len, 1024), BF16), _sds((s_len, 1024), F32)],
        [], (x, att, rec, g_oa, g_ol, w_out, g_cross, wq, kx, vx, wo), "arbitrary", comm)


def _load_weights_once(pairs):
    @pl.when(pl.program_id(0) == 0)
    def _():
        for hbm, vmem in pairs:
            pltpu.sync_copy(hbm, vmem)


FF_CHUNKS = [(0, 1280), (1280, D_FF)]


def _f_ffn(x2, tgt, g_ffn, g_final, wg, wu, wd, tm):
    s_len = x2.shape[0]

    def body(x2_ref, t_ref, gf_ref, gfin_ref, wg_hbm, wu_hbm, wd_hbm,
             hf_ref, g_ref, u_ref, a_ref, dx3_ref, loss_ref, dgfin_ref, wg_ref, wu_ref, wd_ref):
        _load_weights_once([(wg_hbm, wg_ref), (wu_hbm, wu_ref), (wd_hbm, wd_ref)])

        @pl.when(pl.program_id(0) == 0)
        def _():
            loss_ref[...] = jnp.zeros_like(loss_ref)
            dgfin_ref[...] = jnp.zeros_like(dgfin_ref)

        x2v = x2_ref[...]
        hf = (x2v * _rinv(x2v) * gf_ref[...]).astype(BF16)
        hf_ref[...] = hf
        x3 = x2v
        for c0, c1 in FF_CHUNKS:
            gv = _dot_nt(hf, wg_ref[c0:c1, :])
            uv = _dot_nt(hf, wu_ref[c0:c1, :])
            av = (gv * jax.nn.sigmoid(gv) * uv).astype(BF16)
            g_ref[:, c0:c1] = gv.astype(BF16)
            u_ref[:, c0:c1] = uv.astype(BF16)
            a_ref[:, c0:c1] = av
            x3 = x3 + _dot(av, wd_ref[c0:c1, :])
        r3 = _rinv(x3)
        yh = x3 * r3
        gfin = gfin_ref[...]
        err = yh * gfin - t_ref[...]
        loss_ref[...] += jnp.full((1, 128), 0.5 / D_MODEL, F32) * jnp.sum(err * err)
        dy = err * (1.0 / D_MODEL)
        dgfin_ref[...] += jnp.sum(dy * yh, axis=0, keepdims=True)
        dyh = dy * gfin
        dx3_ref[...] = r3 * (dyh - yh * jnp.mean(dyh * yh, axis=-1, keepdims=True))

    vec = _full((1, 1024))
    return pl.pallas_call(
        body, name="f_ffn", grid=(s_len // tm,),
        in_specs=[_rows(tm, 1024), _rows(tm, 1024), vec, vec, _any(), _any(), _any()],
        out_specs=[_rows(tm, 1024), _rows(tm, D_FF), _rows(tm, D_FF), _rows(tm, D_FF),
                   _rows(tm, 1024), _full((1, 128)), vec],
        out_shape=[_sds((s_len, 1024), BF16)] + [_sds((s_len, D_FF), BF16)] * 3
                  + [_sds((s_len, 1024), F32), _sds((1, 128), F32), _sds((1, 1024), F32)],
        scratch_shapes=[pltpu.VMEM((D_FF, 1024), BF16)] * 3,
        compiler_params=_cp("arbitrary"))(x2, tgt, g_ffn, g_final, wg, wu, wd)


def _b_ffn(dx3, x2, gact, uact, g_ffn, wg, wu, wd, tm):
    s_len = x2.shape[0]

    def body(dx3_ref, x2_ref, g_ref, u_ref, gf_ref, wg_hbm, wu_hbm, wd_hbm,
             dg_ref, du_ref, dx2_ref, dgf_ref, wg_ref, wu_ref, wd_ref):
        _load_weights_once([(wg_hbm, wg_ref), (wu_hbm, wu_ref), (wd_hbm, wd_ref)])

        @pl.when(pl.program_id(0) == 0)
        def _():
            dgf_ref[...] = jnp.zeros_like(dgf_ref)

        dx3v = dx3_ref[...]
        dx3b = dx3v.astype(BF16)
        dhf = jnp.zeros(dx3v.shape, F32)
        for c0, c1 in FF_CHUNKS:
            da = _dot_nt(dx3b, wd_ref[c0:c1, :])
            gv = g_ref[:, c0:c1].astype(F32)
            uv = u_ref[:, c0:c1].astype(F32)
            sg = jax.nn.sigmoid(gv)
            dub = (da * gv * sg).astype(BF16)
            dgb = (da * uv * (sg * (1.0 + gv * (1.0 - sg)))).astype(BF16)
            du_ref[:, c0:c1] = dub
            dg_ref[:, c0:c1] = dgb
            dhf = dhf + _dot(dgb, wg_ref[c0:c1, :]) + _dot(dub, wu_ref[c0:c1, :])
        dx, dgf = _rms_bwd(dhf, x2_ref[...], gf_ref[...])
        dx2_ref[...] = dx3v + dx
        dgf_ref[...] += dgf

    vec = _full((1, 1024))
    return pl.pallas_call(
        body, name="b_ffn", grid=(s_len // tm,),
        in_specs=[_rows(tm, 1024), _rows(tm, 1024), _rows(tm, D_FF), _rows(tm, D_FF), vec,
                  _any(), _any(), _any()],
        out_specs=[_rows(tm, D_FF), _rows(tm, D_FF), _rows(tm, 1024), vec],
        out_shape=[_sds((s_len, D_FF), BF16)] * 2 + [_sds((s_len, 1024), F32), _sds((1, 1024), F32)],
        scratch_shapes=[pltpu.VMEM((D_FF, 1024), BF16)] * 3,
        compiler_params=_cp("arbitrary"))(dx3, x2, gact, uact, g_ffn, wg, wu, wd)


def _b_mid(dx2, qx, x1, att, rec, kx, vx, wo, wq, w_out, g_cross, g_oa, g_ol, tm, comm=None):
    s_len = x1.shape[0]
    m_len = kx.shape[0]

    def body(dx2_ref, qx_ref, x1_ref, att_ref, rec_ref, kx_ref, vx_ref, wo_ref, wq_ref, wout_ref,
             gc_ref, goa_ref, gol_ref,
             dqx_ref, dx1_ref, datt_ref, drec_ref, dkx_ref, dvx_ref, dgc_ref, dgoa_ref, dgol_ref):
        @pl.when(pl.program_id(0) == 0)
        def _():
            for r in (dkx_ref, dvx_ref, dgc_ref, dgoa_ref, dgol_ref):
                r[...] = jnp.zeros_like(r)

        dx2v = dx2_ref[...]
        dox = _dot_nt(dx2v.astype(BF16), wo_ref[...])
        for h in range(X_HEADS):
            sl = slice(h * X_HEAD_DIM, (h + 1) * X_HEAD_DIM)
            q = qx_ref[:, sl]
            p, l = _xattn_probs(q, kx_ref[:, sl])
            pn = p / l
            dob = dox[:, sl].astype(BF16)
            dp = _dot_nt(dob, vx_ref[:, sl])
            dvx_ref[:, sl] += _dot_tn(pn.astype(BF16), dob)
            ds = pn * (dp - jnp.sum(dp * pn, axis=-1, keepdims=True))
            dsb = (ds * X_SCALE).astype(BF16)
            dqx_ref[:, sl] = _dot(dsb, kx_ref[:, sl]).astype(BF16)
            dkx_ref[:, sl] += _dot_tn(dsb, q)
        dhc = _dot_nt(dqx_ref[...], wq_ref[...])
        dx, dgc = _rms_bwd(dhc, x1_ref[...], gc_ref[...])
        dx1 = dx2v + dx
        dx1_ref[...] = dx1
        dgc_ref[...] += dgc
        dmg = _dot_nt(dx1.astype(BF16), wout_ref[...])
        da, dgoa = _rms_bwd(dmg[:, 0:D_ATT], att_ref[...], goa_ref[...])
        datt_ref[...] = da
        dgoa_ref[...] += dgoa
        dr, dgol = _rms_bwd(dmg[:, D_ATT:1024], rec_ref[...], gol_ref[...])
        drec_ref[...] = dr
        dgol_ref[...] += dgol

    sq = _full((1024, 1024))
    mk = _full((m_len, 1024))
    return _call(
        body, "b_mid", (s_len // tm,),
        [_rows(tm, 1024), _rows(tm, 1024), _rows(tm, 1024), _rows(tm, 512), _rows(tm, 512), mk, mk,
         sq, sq, sq, _full((1, 1024)), _full((1, 512)), _full((1, 512))],
        [_rows(tm, 1024), _rows(tm, 1024), _rows(tm, 512), _rows(tm, 512), mk, mk,
         _full((1, 1024)), _full((1, 512)), _full((1, 512))],
        [_sds((s_len, 1024), BF16), _sds((s_len, 1024), F32), _sds((s_len, 512), F32),
         _sds((s_len, 512), F32), _sds((m_len, 1024), F32), _sds((m_len, 1024), F32),
         _sds((1, 1024), F32), _sds((1, 512), F32), _sds((1, 512), F32)],
        [], (dx2, qx, x1, att, rec, kx, vx, wo, wq, w_out, g_cross, g_oa, g_ol), "arbitrary", comm)


def _b_mem(dkx, dvx, mem, mn, g_mem, wk, wv):
    def body(dkx_ref, dvx_ref, mem_ref, mn_ref, g_ref, wk_ref, wv_ref, dwk_ref, dwv_ref, dgm_ref,
             dwkb_ref, dwvb_ref):
        dkb = dkx_ref[...].astype(BF16)
        dvb = dvx_ref[...].astype(BF16)
        dwk = _dot_tn(mn_ref[...], dkb)
        dwv = _dot_tn(mn_ref[...], dvb)
        dwk_ref[...] = dwk
        dwv_ref[...] = dwv
        dwkb_ref[...] = dwk.astype(BF16)
        dwvb_ref[...] = dwv.astype(BF16)
        dmn = _dot_nt(dkb, wk_ref[...]) + _dot_nt(dvb, wv_ref[...])
        mv = mem_ref[...]
        dgm_ref[...] = jnp.sum(dmn * (mv * _rinv(mv)), axis=0, keepdims=True)

    return pl.pallas_call(
        body, name="b_mem",
        out_shape=[_sds((1024, 1024), F32), _sds((1024, 1024), F32), _sds((1, 1024), F32),
                   _sds((1024, 1024), BF16), _sds((1024, 1024), BF16)],
        compiler_params=_cp())(dkx, dvx, mem, mn, g_mem, wk, wv)


def _b_lru(drec, hs, u, xg, conv_w, wrg, brg, wig, big, lam, tl, comm=None):
    s_len = xg.shape[0]
    nt = s_len // tl

    def body(drec_ref, hs_ref, hsp_ref, u_ref, xg_ref, cw_ref, wrg_ref, brg_ref, wig_ref, big_ref, l_ref,
             dxg_ref, dwrg_ref, dwig_ref, dbrg_ref, dbig_ref, dlam_ref, dcw_ref, dcb_ref,
             hbuf, abuf, dubuf, c_sc, d_sc, lam_sc, lcar, wacc_r, wacc_i):
        i = pl.program_id(0)
        tt = nt - 1 - i

        @pl.when(i == 0)
        def _():
            for r in (wacc_r, wacc_i, dbrg_ref, dbig_ref, dlam_ref, dcw_ref, dcb_ref):
                r[...] = jnp.zeros_like(r)
            abuf[tl:tl + 8, :] = jnp.zeros((8, D_LRU), F32)
            dubuf[tl:tl + 8, :] = jnp.zeros((8, D_LRU), F32)
            lcar[...] = jnp.zeros((8, D_LRU), F32)

        xu0 = xg_ref[:, 0:D_LRU]
        hsv = hs_ref[...]
        uv = u_ref[...]
        hbuf[8:8 + tl, :] = hsv
        hbuf[0:8, :] = jnp.where(tt > 0, hsp_ref[...], 0.0)
        hshift = hbuf[pl.ds(7, tl), :]
        wrg_v = wrg_ref[...]
        wig_v = wig_ref[...]
        lamv = l_ref[...]
        ub, r, ig, sp, a, mult = _lru_gates(uv, wrg_v, brg_ref[...], wig_v, big_ref[...], lamv)
        abuf[0:tl, :] = a
        c_sc[...] = abuf[pl.ds(1, tl), :]
        gel, dgel = _gelu_and_grad(xg_ref[:, D_LRU:2 * D_LRU])
        drv = drec_ref[...]
        d_sc[...] = drv * gel
        dxg_ref[:, D_LRU:2 * D_LRU] = (drv * hsv * dgel).astype(BF16)

        def grp(k, lnext):
            off = pl.multiple_of((tl // 8 - 1 - k) * 8, 8)
            l8 = _rscan8(c_sc[pl.ds(off, 8), :], d_sc[pl.ds(off, 8), :], lnext)
            lam_sc[pl.ds(off, 8), :] = l8
            return l8[0:1, :]

        lcar[0:1, :] = lax.fori_loop(0, tl // 8, grp, lcar[0:1, :])
        abuf[tl:tl + 8, :] = a[0:8, :]
        db = lam_sc[...]
        da = db * hshift
        dmult = db * (ig * uv)
        dig = db * mult * uv
        du = db * mult * ig
        dla = da * a - dmult * (a * a) / mult
        dlam_ref[...] += jnp.sum(dla * (-LRU_C) * r, axis=0, keepdims=True)
        dzr = dla * (-LRU_C * sp) * r * (1.0 - r)
        dzi = dig * ig * (1.0 - ig)
        dzrb = dzr.astype(BF16)
        dzib = dzi.astype(BF16)
        du = du + _dot_nt(dzrb, wrg_v) + _dot_nt(dzib, wig_v)
        wacc_r[...] += _dot_tn(ub, dzrb)
        wacc_i[...] += _dot_tn(ub, dzib)
        dbrg_ref[...] += jnp.sum(dzr, axis=0, keepdims=True)
        dbig_ref[...] += jnp.sum(dzi, axis=0, keepdims=True)
        dcb_ref[...] += jnp.sum(du, axis=0, keepdims=True)
        dubuf[0:tl, :] = du
        dxu0 = jnp.zeros((tl, D_LRU), F32)
        for j in range(4):
            dsh = dubuf[pl.ds(3 - j, tl), :]
            dxu0 = dxu0 + cw_ref[j:j + 1, :] * dsh
            dcw_ref[j:j + 1, :] += jnp.sum(xu0 * dsh, axis=0, keepdims=True)
        dubuf[tl:tl + 8, :] = du[0:8, :]
        dxg_ref[:, 0:D_LRU] = dxu0.astype(BF16)

        @pl.when(i == nt - 1)
        def _():
            dlam_ref[...] = dlam_ref[...] * (-jax.nn.sigmoid(-lamv))
            for n in range(LRU_BLOCKS):
                blk = slice(n * LRU_BLOCK, (n + 1) * LRU_BLOCK)
                dwrg_ref[n] = wacc_r[blk, blk]
                dwig_ref[n] = wacc_i[blk, blk]

    def rev(n):
        return pl.BlockSpec((tl, n), lambda i: (nt - 1 - i, 0))

    prev8 = pl.BlockSpec((8, D_LRU), lambda i: (jnp.maximum((nt - 1 - i) * (tl // 8) - 1, 0), 0))
    vec = _full((1, D_LRU))
    sq = _full((D_LRU, D_LRU))
    blocks_shape = (LRU_BLOCKS, LRU_BLOCK, LRU_BLOCK)
    blocks = _full(blocks_shape)
    return _call(
        body, "b_lru", (nt,),
        [rev(D_LRU), rev(D_LRU), prev8, rev(D_LRU), rev(1024), _full((4, D_LRU)), sq, vec, sq, vec, vec],
        [rev(1024), blocks, blocks, vec, vec, vec, _full((4, D_LRU)), vec],
        [_sds((s_len, 1024), BF16), _sds(blocks_shape, F32), _sds(blocks_shape, F32),
         _sds((1, D_LRU), F32), _sds((1, D_LRU), F32), _sds((1, D_LRU), F32),
         _sds((4, D_LRU), F32), _sds((1, D_LRU), F32)],
        [pltpu.VMEM((tl + 8, D_LRU), F32)] * 3 + [pltpu.VMEM((tl, D_LRU), F32)] * 3
        + [pltpu.VMEM((8, D_LRU), F32)] + [pltpu.VMEM((D_LRU, D_LRU), F32)] * 2,
        (drec, hs, hs, u, xg, conv_w, wrg, brg, wig, big, lam), "arbitrary", comm)


def _b_attn(qkv_pad, att, datt, frow, comm=None):
    s_len = datt.shape[0]
    nb = s_len // QB
    n_pair = ATT_HEADS // 2
    pair_w = 2 * HEAD_DIM

    def body(q_ref, k0, k1, k2, v0, v1, v2, o_ref, do_ref, frow_ref, dq_ref, dkv_ref, dfrow_ref,
             bias_sc, dt_sc, acc_sc):
        t = pl.program_id(0)

        @pl.when(t == 0)
        def _():
            _bias_table(frow_ref, bias_sc)
            dt_sc[...] = jnp.zeros_like(dt_sc)
            acc_sc[...] = jnp.zeros_like(acc_sc)

        @pl.when(t < nb)
        def _():
            var = jnp.minimum(t, N_BIAS - 1)
            even = _even_lanes()
            for hp in range(n_pair):
                cs = slice(hp * pair_w, (hp + 1) * pair_w)
                qt = q_ref[:, cs]
                kts = [k0[:, cs], k1[:, cs], k2[:, cs]]
                vts = [v0[:, cs], v1[:, cs], v2[:, cs]]
                dot = do_ref[:, cs]
                dd = dot * o_ref[:, cs]
                dos_pair, dsbs, pbs, dqs = None, [], [], []
                for e in range(2):
                    keep = even if e == 0 else jnp.logical_not(even)
                    qm = jnp.where(keep, qt, 0)
                    p = _att_probs(qm, kts, bias_sc[var, 2 * hp + e])
                    inv = 1.0 / jnp.sum(p, axis=-1, keepdims=True)
                    dos = jnp.where(keep, dot * inv, 0.0)
                    delta = jnp.sum(jnp.where(keep, dd, 0.0), axis=-1, keepdims=True) * inv
                    dp = jnp.concatenate([_dot_nt(dos.astype(BF16), v) for v in vts], axis=1)
                    ds = p * (dp - delta)
                    dt_sc[2 * hp + e] += ds
                    dsb = ds.astype(BF16)
                    dq = _dot(dsb[:, 0:QB], kts[0])
                    for j in (1, 2):
                        dq = dq + _dot(dsb[:, j * QB:(j + 1) * QB], kts[j])
                    dqs.append(dq)
                    dsbs.append(dsb)
                    pbs.append(p.astype(BF16))
                    dos_pair = dos if e == 0 else dos_pair + dos
                dq_ref[:, cs] = (jnp.where(even, dqs[0], dqs[1]) * ATT_SCALE).astype(BF16)
                qtt = qt.astype(F32).T.astype(BF16)
                dost = dos_pair.T.astype(BF16)
                for j in range(3):
                    slot = (t + 1 + j) % 3
                    js = slice(j * QB, (j + 1) * QB)
                    for e in range(2):
                        hr = slice(e * HEAD_DIM, (e + 1) * HEAD_DIM)
                        acc_sc[slot, hp, hr, :] += _dot(qtt[hr], dsbs[e][:, js])
                        acc_sc[slot, n_pair + hp, hr, :] += _dot(dost[hr], pbs[e][:, js])

        done = (t + 1) % 3

        @pl.when(t >= 2)
        def _():
            for i in range(2 * n_pair):
                dkv_ref[:, i * pair_w:(i + 1) * pair_w] = acc_sc[done, i].T.astype(BF16)

        acc_sc[done] = jnp.zeros((2 * n_pair, pair_w, QB), F32)

        @pl.when(t == nb + 1)
        def _():
            row = lax.broadcasted_iota(jnp.int32, (8, ROLL_W), 0)
            pad = jnp.zeros((8, ROLL_W - KB), F32)
            for h in range(ATT_HEADS):
                acc8 = jnp.concatenate([dt_sc[h, 0:8, :], pad], axis=1)
                for a1 in range(1, QB // 8):
                    blk = jnp.concatenate([dt_sc[h, 8 * a1:8 * a1 + 8, :], pad], axis=1)
                    acc8 = acc8 + pltpu.roll(blk, ROLL_W - 8 * a1, 1)
                for k in range(3):
                    acc8 = jnp.where(((row >> k) & 1) == 1, pltpu.roll(acc8, ROLL_W - (1 << k), 1), acc8)
                dfrow_ref[h:h + 1, :] = jnp.sum(acc8, axis=0, keepdims=True)

    clamp = lambda t: jnp.minimum(t, nb - 1)
    qrows = pl.BlockSpec((QB, D_ATT), lambda t: (clamp(t), 0))
    return _call(
        body, "b_attn", (nb + 2,),
        _att_in_specs(clamp) + [qrows, qrows, _full((ATT_HEADS, ROLL_W))],
        [qrows, pl.BlockSpec((QB, 2 * D_ATT), lambda t: (jnp.maximum(t - 2, 0), 0)),
         _full((ATT_HEADS, ROLL_W))],
        [_sds((s_len, D_ATT), BF16), _sds((s_len, 2 * D_ATT), BF16), _sds((ATT_HEADS, ROLL_W), F32)],
        [pltpu.VMEM((N_BIAS, ATT_HEADS, QB, KB), F32), pltpu.VMEM((ATT_HEADS, QB, KB), F32),
         pltpu.VMEM((3, 2 * n_pair, pair_w, QB), F32)],
        (*([qkv_pad] * 7), att, datt, frow), "arbitrary", comm)


def _b_win(dq, dkv, dxg, h, ts):
    s_len = h.shape[0]
    steps = s_len // ts

    def body(dq_ref, dkv_ref, dxg_ref, h_ref, dw_ref, dwb_ref):
        @pl.when(pl.program_id(0) == 0)
        def _():
            dw_ref[...] = jnp.zeros_like(dw_ref)

        dproj = jnp.concatenate([dq_ref[...], dkv_ref[...], dxg_ref[...]], axis=1)
        hv = h_ref[...]
        for s in range(N_SHARD):
            dw_ref[s] += _dot_tn(hv, dproj[:, s * IN_SH:(s + 1) * IN_SH])

        @pl.when(pl.program_id(0) == steps - 1)
        def _():
            dwb_ref[...] = dw_ref[...].astype(BF16)

    wspec = _full((N_SHARD, 1024, IN_SH))
    return pl.pallas_call(
        body, name="b_win", grid=(steps,),
        in_specs=[_rows(ts, 512), _rows(ts, 1024), _rows(ts, 1024), _rows(ts, 1024)],
        out_specs=[wspec, wspec],
        out_shape=[_sds((N_SHARD, 1024, IN_SH), F32), _sds((N_SHARD, 1024, IN_SH), BF16)],
        compiler_params=_cp("arbitrary"))(dq, dkv, dxg, h)


def _b_inproj(dq, dkv, dxg, x, dx1, g_mix, w_in_g, tm, comm=None):
    s_len = x.shape[0]

    def body(dq_ref, dkv_ref, dxg_ref, x_ref, dx1_ref, g_ref, w_hbm, gx_ref, dgm_ref, w_ref):
        _load_w_in_once(w_hbm, w_ref)

        @pl.when(pl.program_id(0) == 0)
        def _():
            dgm_ref[...] = jnp.zeros_like(dgm_ref)

        dproj = jnp.concatenate([dq_ref[...], dkv_ref[...], dxg_ref[...]], axis=1)
        dh = _dot_nt(dproj, w_ref[...])
        dx, dgm = _rms_bwd(dh, x_ref[...], g_ref[...])
        gx_ref[...] = dx1_ref[...] + dx
        dgm_ref[...] += dgm

    return _call(
        body, "b_inproj", (s_len // tm,),
        [_rows(tm, 512), _rows(tm, 1024), _rows(tm, 1024), _rows(tm, 1024), _rows(tm, 1024),
         _full((1, 1024)), _any()],
        [_rows(tm, 1024), _full((1, 1024))],
        [_sds((s_len, 1024), F32), _sds((1, 1024), F32)],
        [pltpu.VMEM((1024, D_IN), BF16)], (dq, dkv, dxg, x, dx1, g_mix, w_in_g), "arbitrary", comm)


def _mm_tn(xa, ya, name, ts):
    s_len, k = xa.shape
    n = ya.shape[1]

    steps = s_len // ts

    def body(x_ref, y_ref, o_ref, ob_ref):
        @pl.when(pl.program_id(0) == 0)
        def _():
            o_ref[...] = jnp.zeros_like(o_ref)
        o_ref[...] += _dot_tn(x_ref[...].astype(BF16), y_ref[...].astype(BF16))

        @pl.when(pl.program_id(0) == steps - 1)
        def _():
            ob_ref[...] = o_ref[...].astype(BF16)

    return pl.pallas_call(
        body, name=name, grid=(steps,), in_specs=[_rows(ts, k), _rows(ts, n)],
        out_specs=[_full((k, n))] * 2, out_shape=[_sds((k, n), F32), _sds((k, n), BF16)],
        compiler_params=_cp("arbitrary"))(xa, ya)


def _frow_from_rel_bias(rb):
    hi = jnp.broadcast_to(rb[:, 256:257], (ATT_HEADS, 385))
    mid = rb[:, 1:256][:, ::-1]
    lo = jnp.broadcast_to(rb[:, 0:1], (ATT_HEADS, 128))
    wrap = jnp.broadcast_to(rb[:, 256:257], (ATT_HEADS, ROLL_W - KB))
    return jnp.concatenate([hi, mid, lo, wrap], axis=1)


def _rel_bias_grad_from_dfrow(df):
    g256 = jnp.sum(df[:, 0:385], axis=1, keepdims=True) + jnp.sum(df[:, KB:ROLL_W], axis=1, keepdims=True)
    mid = df[:, 385:640][:, ::-1]
    g0 = jnp.sum(df[:, 640:KB], axis=1, keepdims=True)
    return jnp.concatenate([g0, mid, g256], axis=1)


def _block_diag(w):
    eye = jnp.eye(8, dtype=w.dtype)
    return (w[:, :, None, :] * eye[:, None, :, None]).reshape(D_LRU, D_LRU)


MID = ['w_out', 'wq_c', 'wk_c', 'wv_c', 'wo_c']
TRANSPOSED = ['w_gate', 'w_up']
AG_IN_INPROJ = ['w_out', 'wq_c', 'wk_c']
AG_IN_ATTN = ['wv_c', 'wo_c', 'w_gate']
AG_IN_LRU = ['w_up']
AG_IN_MID = ['w_down']
RS_IN_MID = ['w_gate', 'w_up']
RS_IN_LRU = ['w_down']
RS_IN_ATTN = MID


def _local_step(x, mem, tgt, p, gw, shards=None, chip=None):
    s_len = x.shape[0]
    tm = min(256, s_len)
    tmb = min(512, s_len)
    tl = min(512, s_len)
    frow = _frow_from_rel_bias(p['rel_bias'])
    wrg = _block_diag(p['w_rg']).astype(BF16)
    wig = _block_diag(p['w_ig']).astype(BF16)
    gw = dict(gw)

    big, bigb, recv, part, sib = {}, {}, {}, {}, {}

    def ag(names):
        return [] if shards is None else [("ag", [shards[n] for n in names])]

    def rs(names):
        return [] if shards is None else [("rs", [bigb[n] for n in names])]

    def swap(names):
        return [] if shards is None else [("swap", [part[n] for n in names])]

    def reduce_own(names):
        if shards is not None:
            for n in names:
                part[n] = _sum_parts(big[n], recv[n], chip, "sum_" + n)

    h, qkv_pad, xg, *got = _f_inproj(x, p['g_mix'], gw['w_in'], tmb, ag(AG_IN_INPROJ))
    gw.update(zip(AG_IN_INPROJ, got))
    att, *got = _f_attn(qkv_pad, frow, ag(AG_IN_ATTN))
    gw.update(zip(AG_IN_ATTN, got))
    rec, u, hs, *got = _f_lru(xg, p['conv_w'], p['conv_b'], wrg, p['b_rg'], wig, p['b_ig'], p['lru_L'], tl,
                              ag(AG_IN_LRU))
    gw.update(zip(AG_IN_LRU, got))
    w_out = gw['w_out'].reshape(1024, 1024)
    wq = gw['wq_c'].reshape(1024, 1024)
    wk = gw['wk_c'].reshape(1024, 1024)
    wv = gw['wv_c'].reshape(1024, 1024)
    wo = gw['wo_c'].reshape(1024, 1024)
    mn, kx, vx = _f_mem(mem, p['g_mem'], wk, wv)
    mg, x1, hc, qx, ox, x2, *got = _f_mid(x, att, rec, p['g_out_attn'], p['g_out_lru'], w_out, p['g_cross'],
                                          wq, kx, vx, wo, tmb, ag(AG_IN_MID))
    gw.update(zip(AG_IN_MID, got))
    ffn_w = [gw[n].reshape(D_FF, 1024) for n in ('w_gate', 'w_up', 'w_down')]
    hf, gact, uact, aact, dx3, loss, dg_final = _f_ffn(x2, tgt, p['g_ffn'], p['g_final'], *ffn_w, tmb)

    ts = min(512, s_len)
    dgact, duact, dx2, dg_ffn = _b_ffn(dx3, x2, gact, uact, p['g_ffn'], *ffn_w, tm)
    big['w_gate'], bigb['w_gate'] = _mm_tn(dgact, hf, "dw_gate", ts)
    big['w_up'], bigb['w_up'] = _mm_tn(duact, hf, "dw_up", ts)
    big['w_down'], bigb['w_down'] = _mm_tn(aact, dx3, "dw_down", ts)
    for n in ('w_gate', 'w_up', 'w_down'):
        big[n] = big[n].reshape(N_SHARD, FF_SH, 1024)
        bigb[n] = bigb[n].reshape(N_SHARD, FF_SH, 1024)

    dqx, dx1, datt, drec, dkx, dvx, dg_cross, dg_oa, dg_ol, *got = _b_mid(
        dx2, qx, x1, att, rec, kx, vx, wo, wq, w_out, p['g_cross'], p['g_out_attn'], p['g_out_lru'], tmb,
        rs(RS_IN_MID))
    recv.update(zip(RS_IN_MID, got))
    reduce_own(RS_IN_MID)
    dwk, dwv, dg_mem, dwkb, dwvb = _b_mem(dkx, dvx, mem, mn, p['g_mem'], wk, wv)
    big['wk_c'], bigb['wk_c'] = dwk, dwkb
    big['wv_c'], bigb['wv_c'] = dwv, dwvb
    big['w_out'], bigb['w_out'] = _mm_tn(mg, dx1, "dw_out", ts)
    big['wq_c'], bigb['wq_c'] = _mm_tn(hc, dqx, "dw_q", ts)
    big['wo_c'], bigb['wo_c'] = _mm_tn(ox, dx2, "dw_o", ts)
    for n in MID:
        big[n] = big[n].reshape(N_SHARD, 256, 1024)
        bigb[n] = bigb[n].reshape(N_SHARD, 256, 1024)

    dxg, dwrg, dwig, dbrg, dbig, dlam, dcw, dcb, *got = _b_lru(
        drec, hs, u, xg, p['conv_w'], wrg, p['b_rg'], wig, p['b_ig'], p['lru_L'], tl,
        rs(RS_IN_LRU) + swap(RS_IN_MID))
    recv.update(zip(RS_IN_LRU, got))
    sib.update(zip(RS_IN_MID, got[len(RS_IN_LRU):]))
    reduce_own(RS_IN_LRU)
    small = {
        'conv_w': dcw, 'conv_b': dcb, 'w_rg': dwrg, 'b_rg': dbrg, 'w_ig': dwig, 'b_ig': dbig, 'lru_L': dlam,
        'g_out_attn': dg_oa, 'g_out_lru': dg_ol, 'g_cross': dg_cross, 'g_mem': dg_mem, 'g_ffn': dg_ffn,
        'g_final': dg_final,
    }
    names = [n for n in SMALL if n in small]
    gather = [] if shards is None else [("ag8", [_pack_small(names, [small[n] for n in names], loss)])]
    dq, dkv, dfrow, *got = _b_attn(qkv_pad, att, datt, frow, rs(RS_IN_ATTN) + swap(RS_IN_LRU) + gather)
    recv.update(zip(RS_IN_ATTN, got))
    sib.update(zip(RS_IN_LRU, got[len(RS_IN_ATTN):]))
    packs = got[-1] if gather else None
    reduce_own(RS_IN_ATTN)
    small['rel_bias'] = _rel_bias_grad_from_dfrow(dfrow)
    big['w_in'], bigb['w_in'] = _b_win(dq, dkv, dxg, h, ts)
    grad_x, small['g_mix'], *got = _b_inproj(dq, dkv, dxg, x, dx1, p['g_mix'], gw['w_in'], tmb,
                                             rs(['w_in']) + swap(RS_IN_ATTN))
    recv.update(zip(['w_in'], got))
    sib.update(zip(RS_IN_ATTN, got[1:]))
    reduce_own(['w_in'])
    return loss, grad_x, small, big, part, sib, packs


def _cast_shards(ws):
    def body(*refs):
        n = len(refs) // 2
        for src, dst in zip(refs[:n], refs[n:]):
            dst[...] = src[...].astype(BF16)

    return pl.pallas_call(body, name="cast_shards", out_shape=[_sds(w.shape, BF16) for w in ws],
                          compiler_params=_cp())(*ws)


def _sum_parts(own4, recv3, chip, name):
    _, r, c = own4.shape
    tr = r // 4

    def body(chip_ref, own_ref, rc_ref, o_ref):
        o_ref[...] = ((own_ref[0] + rc_ref[0].astype(F32)) + rc_ref[1].astype(F32)) + rc_ref[2].astype(F32)

    grid_spec = pltpu.PrefetchScalarGridSpec(
        num_scalar_prefetch=1, grid=(4,),
        in_specs=[pl.BlockSpec((1, tr, c), lambda i, ch: (ch[0], i, 0)),
                  pl.BlockSpec((3, tr, c), lambda i, ch: (0, i, 0))],
        out_specs=pl.BlockSpec((tr, c), lambda i, ch: (i, 0)))
    return pl.pallas_call(body, name=name, grid_spec=grid_spec, out_shape=_sds((r, c), F32),
                          compiler_params=_cp("parallel"))(chip, own4, recv3)


def _adamw_math(w, g, m, v):
    m = ADAM_B1 * m + (1.0 - ADAM_B1) * g
    v = ADAM_B2 * v + (1.0 - ADAM_B2) * (g * g)
    m_hat = m / (1.0 - ADAM_B1 ** ADAM_STEP)
    v_hat = v / (1.0 - ADAM_B2 ** ADAM_STEP)
    delta = -ADAM_LR * (m_hat / (jnp.sqrt(v_hat) + ADAM_EPS) + ADAM_WD * w)
    return delta, m, v


def _final_adamw(pa, pb, w, m, v, name):
    r, c = w.shape
    tr = r // 4

    def body(pa_ref, pb_ref, w_ref, m_ref, v_ref, g_ref, d_ref, nm_ref, nv_ref):
        g = pa_ref[...] + pb_ref[...]
        g_ref[...] = g
        d_ref[...], nm_ref[...], nv_ref[...] = _adamw_math(w_ref[...], g, m_ref[...], v_ref[...])

    return pl.pallas_call(
        body, name=name, grid=(4,), in_specs=[_rows(tr, c)] * 5, out_specs=[_rows(tr, c)] * 4,
        out_shape=[_sds((r, c), F32)] * 4, compiler_params=_cp("parallel"))(pa, pb, w, m, v)


def _pack_put(ref, name, val_ref):
    r = _pack_rows()[name]
    shape = val_ref.shape
    if len(shape) == 3:
        for b in range(shape[0]):
            ref[r:r + shape[1], b * shape[2]:(b + 1) * shape[2]] = val_ref[b]
    elif shape[1] == 2 * PACK_W:
        ref[r:r + 1, :] = val_ref[:, 0:PACK_W]
        ref[r + 1:r + 2, :] = val_ref[:, PACK_W:2 * PACK_W]
    else:
        ref[r:r + shape[0], 0:shape[1]] = val_ref[...]


def _pack_get(ref, name, shape):
    r = _pack_rows()[name]
    if len(shape) == 3:
        return jnp.stack([ref[r:r + shape[1], b * shape[2]:(b + 1) * shape[2]] for b in range(shape[0])])
    if shape[1] == 2 * PACK_W:
        return jnp.concatenate([ref[r:r + 1, :], ref[r + 1:r + 2, :]], axis=1)
    return ref[r:r + shape[0], 0:shape[1]]


def _pack_small(names, g, loss):
    n = len(g)

    def body(*refs):
        pack = refs[n + 1]
        pack[...] = jnp.zeros_like(pack)
        for a, name in enumerate(names):
            _pack_put(pack, name, refs[a])
        _pack_put(pack, 'loss', refs[n])

    return pl.pallas_call(body, name="pack_small", out_shape=_sds((PACK_ROWS, PACK_W), F32),
                          compiler_params=_cp())(*g, loss)


def _all_peers():
    x, y, c = _mesh_pos()
    peers = []
    for k in range(1, 8):
        px = 1 - x if k & 4 else x
        py = 1 - y if k & 2 else y
        pc = 1 - c if k & 1 else c
        peers.append(((px, py, pc), 4 * px + 2 * py + pc))
    return peers, 4 * x + 2 * y + c


def _ag8_copies(ins, outs, sems):
    send_sems, recv_sems, loc_sems = sems
    n = len(ins)
    peers, me = _all_peers()

    def remote(k, j, slot):
        return pltpu.make_async_remote_copy(
            src_ref=ins[k], dst_ref=outs[k].at[slot], send_sem=send_sems.at[k, j], recv_sem=recv_sems.at[k, j],
            device_id=peers[j][0], device_id_type=MESH_ID)

    def local(k):
        return pltpu.make_async_copy(ins[k], outs[k].at[me], loc_sems.at[k])

    def start():
        for k in range(n):
            local(k).start()
            for j in range(7):
                remote(k, j, me).start()

    def wait():
        for k in range(n):
            for j in range(7):
                remote(k, j, peers[j][1]).wait_recv()
        for k in range(n):
            for j in range(7):
                remote(k, j, me).wait_send()
            local(k).wait()

    return start, _no_forward, wait


def _ar_late(names, g):
    n = len(g)

    def body(*refs):
        tot_ref, pack, buf, send_sems, recv_sems = refs[n:]
        peers, me = _all_peers()

        def remote(j, slot):
            return pltpu.make_async_remote_copy(
                src_ref=pack, dst_ref=buf.at[slot], send_sem=send_sems.at[j], recv_sem=recv_sems.at[j],
                device_id=peers[j][0], device_id_type=MESH_ID)

        pack[...] = jnp.zeros_like(pack)
        for a, name in enumerate(names):
            _pack_put(pack, name, refs[a])
        for j in range(7):
            remote(j, me).start()
        buf[me] = pack[...]
        for j in range(7):
            remote(j, peers[j][1]).wait_recv()
        for j in range(7):
            remote(j, me).wait_send()
        tot = buf[0]
        for d in range(1, 8):
            tot = tot + buf[d]
        tot_ref[...] = tot

    return pl.pallas_call(
        body, name="ar_late", out_shape=_sds((LATE_ROWS, PACK_W), F32),
        scratch_shapes=[pltpu.VMEM((LATE_ROWS, PACK_W), F32), pltpu.VMEM((8, LATE_ROWS, PACK_W), F32),
                        pltpu.SemaphoreType.DMA((7,)), pltpu.SemaphoreType.DMA((7,))],
        compiler_params=_cp())(*g)


def _adamw_small(packs, late_tot, g_shapes, loss_shape, w, m, v):
    n = len(w)

    def body(*refs):
        packs_ref, late_ref = refs[0], refs[1]
        w_refs, m_refs, v_refs = (refs[2 + i * n:2 + (i + 1) * n] for i in range(3))
        o0 = 3 * n + 2
        go, do, mo, vo = (refs[o0 + i * n:o0 + (i + 1) * n] for i in range(4))
        loss_out, tot_ref = refs[o0 + 4 * n], refs[o0 + 4 * n + 1]
        x, y, _ = _mesh_pos()
        tot = packs_ref[0]
        for d in range(1, 8):
            tot = tot + packs_ref[d]
        tot_ref[...] = tot
        tot_ref[0:LATE_ROWS, :] += late_ref[...]
        loss_out[...] = _pack_get(tot_ref, 'loss', loss_shape)
        for a, name in enumerate(SMALL):
            if name == 'conv_w':
                r = _pack_rows()[name]
                ga = tot_ref[r:r + g_shapes[a][0], pl.ds(pl.multiple_of((2 * x + y) * 128, 128), 128)]
            else:
                ga = _pack_get(tot_ref, name, g_shapes[a])
            go[a][...] = ga
            do[a][...], mo[a][...], vo[a][...] = _adamw_math(w_refs[a][...], ga, m_refs[a][...], v_refs[a][...])

    out_shape = [_sds(a.shape, F32) for a in w] * 4 + [_sds(loss_shape, F32)]
    return pl.pallas_call(body, name="adamw_small", out_shape=out_shape,
                          scratch_shapes=[pltpu.VMEM((PACK_ROWS, PACK_W), F32)],
                          compiler_params=_cp())(packs, late_tot, *w, *m, *v)


PACK_W = 512
PACK_ROWS = 160
LATE = ['g_mix', 'rel_bias']
LATE_ROWS = 32


def _pack_rows():
    rows, r = {}, 0
    for name in ['g_mix', 'g_cross', 'g_mem', 'g_ffn', 'g_final']:
        rows[name] = r
        r += 2
    for name in ['conv_b', 'b_rg', 'b_ig', 'lru_L', 'g_out_attn', 'g_out_lru']:
        rows[name] = r
        r += 1
    rows['conv_w'] = r
    rows['loss'] = r + 4
    rows['rel_bias'] = 24
    rows['w_rg'] = 32
    rows['w_ig'] = 32 + LRU_BLOCK
    assert r + 5 <= 24 and rows['w_ig'] + LRU_BLOCK == PACK_ROWS
    assert rows['g_mix'] + 2 <= LATE_ROWS and rows['rel_bias'] + 8 <= LATE_ROWS
    return rows


INPUT_NAMES = (['x', 'mem'] + WEIGHTS + ['loss_target'] + ['m_' + n for n in WEIGHTS] + ['v_' + n for n in WEIGHTS])


def kernel(x, mem, g_mix, w_in, rel_bias, conv_w, conv_b, w_rg, b_rg, w_ig, b_ig, lru_L, g_out_attn, g_out_lru, w_out, g_cross, g_mem, wq_c, wk_c, wv_c, wo_c, g_ffn, w_gate, w_up, w_down, g_final, loss_target, m_g_mix, m_w_in, m_rel_bias, m_conv_w, m_conv_b, m_w_rg, m_b_rg, m_w_ig, m_b_ig, m_lru_L, m_g_out_attn, m_g_out_lru, m_w_out, m_g_cross, m_g_mem, m_wq_c, m_wk_c, m_wv_c, m_wo_c, m_g_ffn, m_w_gate, m_w_up, m_w_down, m_g_final, v_g_mix, v_w_in, v_rel_bias, v_conv_w, v_conv_b, v_w_rg, v_b_rg, v_w_ig, v_b_ig, v_lru_L, v_g_out_attn, v_g_out_lru, v_w_out, v_g_cross, v_g_mem, v_wq_c, v_wk_c, v_wv_c, v_wo_c, v_g_ffn, v_w_gate, v_w_up, v_w_down, v_g_final):
    a = dict(zip(INPUT_NAMES, (x, mem, g_mix, w_in, rel_bias, conv_w, conv_b, w_rg, b_rg, w_ig, b_ig, lru_L, g_out_attn, g_out_lru, w_out, g_cross, g_mem, wq_c, wk_c, wv_c, wo_c, g_ffn, w_gate, w_up, w_down, g_final, loss_target, m_g_mix, m_w_in, m_rel_bias, m_conv_w, m_conv_b, m_w_rg, m_b_rg, m_w_ig, m_b_ig, m_lru_L, m_g_out_attn, m_g_out_lru, m_w_out, m_g_cross, m_g_mem, m_wq_c, m_wk_c, m_wv_c, m_wo_c, m_g_ffn, m_w_gate, m_w_up, m_w_down, m_g_final, v_g_mix, v_w_in, v_rel_bias, v_conv_w, v_conv_b, v_w_rg, v_b_rg, v_w_ig, v_b_ig, v_lru_L, v_g_out_attn, v_g_out_lru, v_w_out, v_g_cross, v_g_mem, v_wq_c, v_wk_c, v_wv_c, v_wo_c, v_g_ffn, v_w_gate, v_w_up, v_w_down, v_g_final)))
    chip = 2 * lax.axis_index("x") + lax.axis_index("y")

    def shard(name):
        arr = a[name][0]
        return jnp.swapaxes(arr, 0, 1) if name[2:] in TRANSPOSED or name in TRANSPOSED else arr

    shards = dict(zip(BIG, _cast_shards([shard(n) for n in BIG])))
    w_in_g, conv_w_g = _comm_only("ag_w_in", [("ag", [shards['w_in']]), ("agf", [a['conv_w'][0]])])
    conv_w_full = conv_w_g.transpose(1, 0, 2).reshape(4, D_LRU)

    p = {n: a[n] for n in SMALL}
    p['rel_bias'] = a['rel_bias'][0]
    p['w_rg'] = a['w_rg'][0]
    p['w_ig'] = a['w_ig'][0]
    p['conv_w'] = conv_w_full
    p['g_final'] = a['g_final'][None, :]
    chip_arr = jnp.reshape(chip, (1,)).astype(jnp.int32)
    loss_part, grad_x, small, _, part, sib, packs = _local_step(
        a['x'][0], a['mem'][0], a['loss_target'][0], p, {'w_in': w_in_g}, shards, chip_arr)

    sib['w_in'], = _comm_only("swap_w_in", [("swap", [part['w_in']])])
    out = {}
    for n in BIG:
        res = _final_adamw(part[n], sib[n], shard(n), shard('m_' + n), shard('v_' + n), "adamw_" + n)
        out[n] = [jnp.swapaxes(r, 0, 1) for r in res] if n in TRANSPOSED else res

    def natural(arr):
        return arr[0] if arr.ndim >= 3 else (arr[None, :] if arr.ndim == 1 else arr)

    small_out = _adamw_small(packs, _ar_late(LATE, [small[n] for n in LATE]), [small[n].shape for n in SMALL],
                             loss_part.shape, *[[natural(a[pre + n]) for n in SMALL] for pre in ('', 'm_', 'v_')])
    ns = len(SMALL)
    loss = small_out[4 * ns][0, 0]

    def leaf(i, n):
        if n in BIG:
            return out[n][i][None]
        return small_out[i * ns + SMALL.index(n)].reshape(a[n].shape)

    return (loss, grad_x[None], *[leaf(i, n) for i in range(4) for n in WEIGHTS])
```

```python
import math

import jax
import jax.numpy as jnp
from jax import lax
from jax.experimental import pallas as pl
from jax.experimental.pallas import tpu as pltpu

F32 = jnp.float32
BF16 = jnp.bfloat16

D_MODEL = 1024
D_ATT = 512
D_LRU = 512
HEAD_DIM = 64
ATT_HEADS = 8
CHUNK = 64
LEFT_CHUNKS = 8
X_HEADS = 4
X_HEAD_DIM = 256
N_SHARD = 4
IN_SH = 640
D_IN = N_SHARD * IN_SH
FF_SH = 704
D_FF = N_SHARD * FF_SH
EPS = 1e-6
LRU_C = 8.0
LRU_BLOCKS = 8
LRU_BLOCK = 64
QB = 256
KB = 768
ROLL_W = 1024
NEG = -1e30
ATT_SCALE = HEAD_DIM ** -0.5
X_SCALE = X_HEAD_DIM ** -0.5

ADAM_LR = 0.001
ADAM_B1 = 0.9
ADAM_B2 = 0.999
ADAM_EPS = 1e-08
ADAM_WD = 0.01
ADAM_STEP = 10

VMEM_LIMIT_V7X = 56 * 1024 * 1024
BF16_ROWS = 16


def _ew_steps(rows):
    return max(s for s in (2, 1) if rows % (s * BF16_ROWS) == 0)
MESH_ID = pl.DeviceIdType.MESH

WEIGHTS = ['g_mix', 'w_in', 'rel_bias', 'conv_w', 'conv_b', 'w_rg', 'b_rg', 'w_ig', 'b_ig', 'lru_L',
           'g_out_attn', 'g_out_lru', 'w_out', 'g_cross', 'g_mem', 'wq_c', 'wk_c', 'wv_c', 'wo_c',
           'g_ffn', 'w_gate', 'w_up', 'w_down', 'g_final']
BIG = ['w_in', 'w_out', 'wq_c', 'wk_c', 'wv_c', 'wo_c', 'w_gate', 'w_up', 'w_down']
SMALL = [n for n in WEIGHTS if n not in BIG]


def _sds(shape, dtype):
    return jax.ShapeDtypeStruct(shape, dtype)


def _cp(*sem):
    return pltpu.CompilerParams(dimension_semantics=sem or None, vmem_limit_bytes=VMEM_LIMIT_V7X)


def _rows(tm, n):
    return pl.BlockSpec((tm, n), lambda i: (i, 0))


def _full(shape):
    nd = len(shape)
    return pl.BlockSpec(shape, lambda i: (0,) * nd)


def _dot(a, b):
    return jnp.dot(a, b, preferred_element_type=F32)


def _dot_nt(a, b):
    return lax.dot_general(a, b, (((1,), (1,)), ((), ())), preferred_element_type=F32)


def _dot_tn(a, b):
    return lax.dot_general(a, b, (((0,), (0,)), ((), ())), preferred_element_type=F32)


def _rinv(x):
    return lax.rsqrt(jnp.mean(x * x, axis=-1, keepdims=True) + EPS)


def _rms_bwd(dy, x, g):
    r = _rinv(x)
    yh = x * r
    dyh = dy * g
    dx = r * (dyh - yh * jnp.mean(dyh * yh, axis=-1, keepdims=True))
    return dx, jnp.sum(dy * yh, axis=0, keepdims=True)


def _gelu(x):
    c = math.sqrt(2.0 / math.pi)
    t = jnp.tanh(c * (x + 0.044715 * x * x * x))
    return 0.5 * x * (1.0 + t)


def _gelu_and_grad(x):
    c = math.sqrt(2.0 / math.pi)
    t = jnp.tanh(c * (x + 0.044715 * x * x * x))
    g = 0.5 * x * (1.0 + t)
    dg = 0.5 * (1.0 + t) + 0.5 * x * (1.0 - t * t) * c * (1.0 + 3.0 * 0.044715 * x * x)
    return g, dg


def _neg_expm1(z):
    series = -z * (1 + z / 2 * (1 + z / 3 * (1 + z / 4)))
    return jnp.where(z > -0.03, series, 1.0 - jnp.exp(z))


def _lru_gates(u, wrg, brg, wig, big, lam):
    ub = u.astype(BF16)
    r = jax.nn.sigmoid(_dot(ub, wrg) + brg)
    ig = jax.nn.sigmoid(_dot(ub, wig) + big)
    sp = jnp.maximum(-lam, 0.0) + jnp.log1p(jnp.exp(-jnp.abs(lam)))
    la = -LRU_C * r * sp
    a = jnp.exp(la)
    mult = jnp.sqrt(jnp.maximum(_neg_expm1(2.0 * la), 0.0))
    return ub, r, ig, sp, a, mult


def _scan8(a8, b8, hprev):
    row = lax.broadcasted_iota(jnp.int32, a8.shape, 0)
    aa, bb = a8, b8
    for d in (1, 2, 4):
        a_s = pltpu.roll(aa, d, 0)
        b_s = pltpu.roll(bb, d, 0)
        m = row >= d
        bb = jnp.where(m, aa * b_s + bb, bb)
        aa = jnp.where(m, aa * a_s, aa)
    return aa * hprev + bb


def _mesh_pos():
    return lax.axis_index("x"), lax.axis_index("y"), lax.axis_index("c")


def _other_chips(x, y):
    return [(1 - x, y), (x, 1 - y), (1 - x, 1 - y)]


def _no_forward():
    pass


def _ag_full_copies(ins, outs, sems):
    send_sems, recv_sems, loc_sems = sems
    n = len(ins)
    x, y, c = _mesh_pos()
    mine = 2 * x + y
    chips = _other_chips(x, y)

    def remote(k, j, slot):
        px, py = chips[j]
        return pltpu.make_async_remote_copy(
            src_ref=ins[k], dst_ref=outs[k].at[slot], send_sem=send_sems.at[k, j], recv_sem=recv_sems.at[k, j],
            device_id=(px, py, c), device_id_type=MESH_ID)

    def local(k):
        return pltpu.make_async_copy(ins[k], outs[k].at[mine], loc_sems.at[k])

    def start():
        for k in range(n):
            local(k).start()
            for j in range(3):
                remote(k, j, mine).start()

    def wait():
        for k in range(n):
            for j, (px, py) in enumerate(chips):
                remote(k, j, 2 * px + py).wait_recv()
        for k in range(n):
            for j in range(3):
                remote(k, j, mine).wait_send()
            local(k).wait()

    return start, _no_forward, wait


def _ag_copies(ins, outs, sems):
    send_sems, recv_sems, fsend_sems, frecv_sems, loc_sems = sems
    n = len(ins)
    x, y, c = _mesh_pos()
    mine = 2 * x + y
    chips = _other_chips(x, y)

    def half(ref, hc):
        r = ref.shape[0] // 2
        return ref.at[pl.ds(pl.multiple_of(hc * r, 16), r)]

    def ici(k, j, slot):
        px, py = chips[j]
        return pltpu.make_async_remote_copy(
            src_ref=half(ins[k], c), dst_ref=half(outs[k].at[slot], c),
            send_sem=send_sems.at[k, j], recv_sem=recv_sems.at[k, j],
            device_id=(px, py, c), device_id_type=MESH_ID)

    def d2d(k, j, hc):
        px, py = chips[j]
        part = half(outs[k].at[2 * px + py], hc)
        return pltpu.make_async_remote_copy(
            src_ref=part, dst_ref=part, send_sem=fsend_sems.at[k, j], recv_sem=frecv_sems.at[k, j],
            device_id=(x, y, 1 - c), device_id_type=MESH_ID)

    def local(k):
        return pltpu.make_async_copy(ins[k], outs[k].at[mine], loc_sems.at[k])

    def start():
        for k in range(n):
            local(k).start()
            for j in range(3):
                ici(k, j, mine).start()

    def forward():
        for k in range(n):
            for j, (px, py) in enumerate(chips):
                ici(k, j, 2 * px + py).wait_recv()
                d2d(k, j, c).start()

    def wait():
        for k in range(n):
            for j in range(3):
                d2d(k, j, 1 - c).wait_recv()
        for k in range(n):
            for j in range(3):
                d2d(k, j, c).wait_send()
                ici(k, j, mine).wait_send()
            local(k).wait()

    return start, forward, wait


def _rs_copies(ins, outs, sems):
    send_sems, recv_sems = sems
    n = len(ins)
    x, y, c = _mesh_pos()
    chips = _other_chips(x, y)

    def remote(k, j):
        px, py = chips[j]
        return pltpu.make_async_remote_copy(
            src_ref=ins[k].at[2 * px + py], dst_ref=outs[k].at[j],
            send_sem=send_sems.at[k, j], recv_sem=recv_sems.at[k, j],
            device_id=(px, py, c), device_id_type=MESH_ID)

    def start():
        for k in range(n):
            for j in range(3):
                remote(k, j).start()

    def wait():
        for k in range(n):
            for j in range(3):
                remote(k, j).wait_recv()
        for k in range(n):
            for j in range(3):
                remote(k, j).wait_send()

    return start, _no_forward, wait


def _swap_copies(ins, outs, sems):
    send_sems, recv_sems = sems
    x, y, c = _mesh_pos()
    copies = [pltpu.make_async_remote_copy(
        src_ref=ins[k], dst_ref=outs[k], send_sem=send_sems.at[k], recv_sem=recv_sems.at[k],
        device_id=(x, y, 1 - c), device_id_type=MESH_ID) for k in range(len(ins))]

    def start():
        for cp in copies:
            cp.start()

    def wait():
        for cp in copies:
            cp.wait()

    return start, _no_forward, wait


def _comm_plan(groups):
    plan, arrs, shapes, sems = [], [], [], []
    for kind, group in groups:
        k = len(group)
        arrs += group
        per_peer = pltpu.SemaphoreType.DMA((k, 3))
        if kind == "ag":
            shapes += [_sds((N_SHARD,) + w.shape, w.dtype) for w in group]
            gsems = [per_peer] * 4 + [pltpu.SemaphoreType.DMA((k,))]
            maker = _ag_copies
        elif kind == "agf":
            shapes += [_sds((N_SHARD,) + w.shape, w.dtype) for w in group]
            gsems = [per_peer] * 2 + [pltpu.SemaphoreType.DMA((k,))]
            maker = _ag_full_copies
        elif kind == "ag8":
            shapes += [_sds((8,) + g.shape, g.dtype) for g in group]
            gsems = [pltpu.SemaphoreType.DMA((k, 7))] * 2 + [pltpu.SemaphoreType.DMA((k,))]
            maker = _ag8_copies
        elif kind == "rs":
            shapes += [_sds((3,) + g.shape[1:], g.dtype) for g in group]
            gsems = [pltpu.SemaphoreType.DMA((k, 3)), pltpu.SemaphoreType.DMA((k, 3))]
            maker = _rs_copies
        else:
            shapes += [_sds(g.shape, g.dtype) for g in group]
            gsems = [pltpu.SemaphoreType.DMA((k,)), pltpu.SemaphoreType.DMA((k,))]
            maker = _swap_copies
        plan.append((maker, k, len(gsems)))
        sems += gsems
    return plan, arrs, shapes, sems


def _comm_fns(plan, cins, couts, sems):
    fns, a, s = [], 0, 0
    for maker, k, ns in plan:
        fns.append(maker(cins[a:a + k], couts[a:a + k], sems[s:s + ns]))
        a += k
        s += ns

    def start():
        for st, _, _ in fns:
            st()

    def forward():
        for _, fw, _ in fns:
            fw()

    def wait():
        for _, _, wt in fns:
            wt()

    return start, forward, wait


def _call(body, name, grid, in_specs, out_specs, out_shape, scratch, args, sem, comm=None):
    if not comm:
        return pl.pallas_call(body, name=name, grid=grid, in_specs=in_specs, out_specs=out_specs,
                              out_shape=out_shape, scratch_shapes=scratch, compiler_params=_cp(sem))(*args)
    plan, c_arrs, c_shapes, c_sems = _comm_plan(comm)
    k = len(c_arrs)
    n_in, n_out, n_scr = len(in_specs), len(out_specs), len(scratch)
    last = grid[0] - 1
    fwd_step = max(1, (2 * last) // 3)

    def wrapped(*refs):
        ins, cins = refs[:n_in], refs[n_in:n_in + k]
        o0 = n_in + k
        outs, couts = refs[o0:o0 + n_out], refs[o0 + n_out:o0 + n_out + k]
        s0 = o0 + n_out + k
        start, forward, wait = _comm_fns(plan, cins, couts, refs[s0 + n_scr:])
        pl.when(pl.program_id(0) == 0)(start)
        pl.when(pl.program_id(0) == fwd_step)(forward)
        body(*ins, *outs, *refs[s0:s0 + n_scr])
        pl.when(pl.program_id(0) == last)(wait)

    return pl.pallas_call(
        wrapped, name=name, grid=grid, in_specs=list(in_specs) + [_any()] * k,
        out_specs=list(out_specs) + [_any()] * k, out_shape=list(out_shape) + c_shapes,
        scratch_shapes=list(scratch) + c_sems, compiler_params=_cp(sem))(*args, *c_arrs)


def _comm_only(name, comm):
    plan, c_arrs, c_shapes, c_sems = _comm_plan(comm)
    k = len(c_arrs)

    def body(*refs):
        start, forward, wait = _comm_fns(plan, refs[:k], refs[k:2 * k], refs[2 * k:])
        start()
        forward()
        wait()

    return pl.pallas_call(body, name=name, in_specs=[_any()] * k, out_specs=[_any()] * k, out_shape=c_shapes,
                          scratch_shapes=c_sems, compiler_params=_cp())(*c_arrs)


def _any():
    return pl.BlockSpec(memory_space=pl.ANY)


def _rscan8(c8, d8, lnext):
    row = lax.broadcasted_iota(jnp.int32, c8.shape, 0)
    cc, dd = c8, d8
    for d in (1, 2, 4):
        c_s = pltpu.roll(cc, 8 - d, 0)
        d_s = pltpu.roll(dd, 8 - d, 0)
        m = row < 8 - d
        dd = jnp.where(m, cc * d_s + dd, dd)
        cc = jnp.where(m, cc * c_s, cc)
    return cc * lnext + dd


def _load_w_in_once(w_hbm, w_ref):
    @pl.when(pl.program_id(0) == 0)
    def _():
        for s in range(N_SHARD):
            pltpu.sync_copy(w_hbm.at[s], w_ref.at[:, pl.ds(s * IN_SH, IN_SH)])


def _f_inproj(x, g_mix, w_in_g, tm, comm=None):
    s_len = x.shape[0]
    pad_rows = LEFT_CHUNKS * CHUNK
    npad = pad_rows // tm

    def body(x_ref, g_ref, w_hbm, h_ref, qkv_ref, xg_ref, w_ref):
        i = pl.program_id(0)
        _load_w_in_once(w_hbm, w_ref)

        @pl.when(i < npad)
        def _():
            qkv_ref[...] = jnp.zeros_like(qkv_ref)

        @pl.when(i >= npad)
        def _():
            xv = x_ref[...]
            h = (xv * _rinv(xv) * g_ref[...]).astype(BF16)
            h_ref[...] = h
            proj = _dot(h, w_ref[...])
            qkv_ref[:, 0:D_ATT] = (proj[:, 0:D_ATT] * ATT_SCALE).astype(BF16)
            qkv_ref[:, D_ATT:3 * D_ATT] = proj[:, D_ATT:3 * D_ATT].astype(BF16)
            xg_ref[...] = proj[:, 3 * D_ATT:D_IN]

    def tok(n):
        return pl.BlockSpec((tm, n), lambda i: (jnp.maximum(i - npad, 0), 0))

    return _call(
        body, "f_inproj", (s_len // tm + npad,),
        [tok(1024), _full((1, 1024)), _any()],
        [tok(1024), _rows(tm, 1536), tok(1024)],
        [_sds((s_len, 1024), BF16), _sds((s_len + pad_rows, 1536), BF16), _sds((s_len, 1024), F32)],
        [pltpu.VMEM((1024, D_IN), BF16)], (x, g_mix, w_in_g), "arbitrary", comm)


N_BIAS = 3


def _bias_table(frow_ref, bias_sc):
    qa = lax.broadcasted_iota(jnp.int32, (QB, KB), 0) // CHUNK
    kcol = lax.broadcasted_iota(jnp.int32, (QB, KB), 1)
    kb = kcol // CHUNK
    band = jnp.where((kb >= qa) & (kb - qa <= LEFT_CHUNKS), 0.0, NEG).astype(F32)
    for h in range(ATT_HEADS):
        row = jnp.broadcast_to(frow_ref[h:h + 1, :], (QB, ROLL_W))
        toep = pltpu.roll(row, 0, 1, stride=1, stride_axis=0)
        gen = toep[:, 0:KB] + band
        bias_sc[N_BIAS - 1, h] = gen
        for v in range(N_BIAS - 1):
            pad_keys = LEFT_CHUNKS * CHUNK - v * QB
            bias_sc[v, h] = gen + jnp.where(kcol < pad_keys, NEG, 0.0).astype(F32)


def _even_lanes():
    return lax.broadcasted_iota(jnp.int32, (1, 2 * HEAD_DIM), 1) < HEAD_DIM


def _att_probs(qm, kts, bias):
    s = jnp.concatenate([_dot_nt(qm, k) for k in kts], axis=1) + bias
    return jnp.exp(s - jnp.max(s, axis=-1, keepdims=True))


def _att_in_specs(clamp):
    def spec(j, col):
        return pl.BlockSpec((QB, D_ATT), lambda i: (clamp(i) + j, col))
    return [spec(2, 0), spec(0, 1), spec(1, 1), spec(2, 1), spec(0, 2), spec(1, 2), spec(2, 2)]


def _f_attn(qkv_pad, frow, comm=None):
    s_len = qkv_pad.shape[0] - LEFT_CHUNKS * CHUNK
    nb = s_len // QB

    def body(q_ref, k0, k1, k2, v0, v1, v2, frow_ref, o_ref, bias_sc):
        i = pl.program_id(0)

        @pl.when(i == 0)
        def _():
            _bias_table(frow_ref, bias_sc)

        var = jnp.minimum(i, N_BIAS - 1)
        even = _even_lanes()
        for hp in range(ATT_HEADS // 2):
            cs = slice(hp * 2 * HEAD_DIM, (hp + 1) * 2 * HEAD_DIM)
            qt = q_ref[:, cs]
            kts = [k0[:, cs], k1[:, cs], k2[:, cs]]
            vts = [v0[:, cs], v1[:, cs], v2[:, cs]]
            res = []
            for e in range(2):
                keep = even if e == 0 else jnp.logical_not(even)
                pb = _att_probs(jnp.where(keep, qt, 0), kts, bias_sc[var, 2 * hp + e]).astype(BF16)
                r = _dot(pb[:, 0:QB], jnp.where(keep, vts[0], 1))
                for j in (1, 2):
                    r = r + _dot(pb[:, j * QB:(j + 1) * QB], jnp.where(keep, vts[j], 1))
                res.append(r / pltpu.roll(r, HEAD_DIM, 1))
            o_ref[:, cs] = jnp.where(even, res[0], res[1])

    return _call(
        body, "f_attn", (nb,),
        _att_in_specs(lambda i: i) + [_full((ATT_HEADS, ROLL_W))],
        [_rows(QB, D_ATT)], [_sds((s_len, D_ATT), F32)],
        [pltpu.VMEM((N_BIAS, ATT_HEADS, QB, KB), F32)], (*([qkv_pad] * 7), frow), "arbitrary", comm)


def _f_lru(xg, conv_w, conv_b, wrg, brg, wig, big, lam, tl, comm=None):
    s_len = xg.shape[0]

    def body(xg_ref, cw_ref, cb_ref, wrg_ref, brg_ref, wig_ref, big_ref, l_ref,
             rec_ref, u_ref, hs_ref, xbuf, a_sc, b_sc, hcar):
        i = pl.program_id(0)

        @pl.when(i == 0)
        def _():
            xbuf[0:8, :] = jnp.zeros((8, D_LRU), F32)
            hcar[...] = jnp.zeros((8, D_LRU), F32)

        xu0 = xg_ref[:, 0:D_LRU]
        xbuf[8:8 + tl, :] = xu0
        u = cb_ref[...] + cw_ref[0:1, :] * xbuf[pl.ds(5, tl), :]
        for j in range(1, 4):
            u = u + cw_ref[j:j + 1, :] * xbuf[pl.ds(5 + j, tl), :]
        xbuf[0:8, :] = xu0[tl - 8:tl, :]
        u_ref[...] = u
        _, _, ig, _, a, mult = _lru_gates(u, wrg_ref[...], brg_ref[...], wig_ref[...], big_ref[...], l_ref[...])
        a_sc[...] = a
        b_sc[...] = mult * (ig * u)

        def grp(g, hprev):
            off = pl.multiple_of(g * 8, 8)
            h8 = _scan8(a_sc[pl.ds(off, 8), :], b_sc[pl.ds(off, 8), :], hprev)
            hs_ref[pl.ds(off, 8), :] = h8
            return h8[7:8, :]

        hcar[0:1, :] = lax.fori_loop(0, tl // 8, grp, hcar[0:1, :])
        rec_ref[...] = hs_ref[...] * _gelu(xg_ref[:, D_LRU:2 * D_LRU])

    vec = _full((1, D_LRU))
    return _call(
        body, "f_lru", (s_len // tl,),
        [_rows(tl, 1024), _full((4, D_LRU)), vec, _full((D_LRU, D_LRU)), vec, _full((D_LRU, D_LRU)), vec, vec],
        [_rows(tl, D_LRU)] * 3, [_sds((s_len, D_LRU), F32)] * 3,
        [pltpu.VMEM((tl + 8, D_LRU), F32), pltpu.VMEM((tl, D_LRU), F32),
         pltpu.VMEM((tl, D_LRU), F32), pltpu.VMEM((8, D_LRU), F32)],
        (xg, conv_w, conv_b, wrg, brg, wig, big, lam), "arbitrary", comm)


def _f_mem(mem, g_mem, wk, wv):
    def body(mem_ref, g_ref, wk_ref, wv_ref, mn_ref, kx_ref, vx_ref):
        mv = mem_ref[...]
        mn = (mv * _rinv(mv) * g_ref[...]).astype(BF16)
        mn_ref[...] = mn
        kx_ref[...] = _dot(mn, wk_ref[...]).astype(BF16)
        vx_ref[...] = _dot(mn, wv_ref[...]).astype(BF16)

    m = mem.shape[0]
    return pl.pallas_call(
        body, name="f_mem", out_shape=[_sds((m, 1024), BF16)] * 3,
        compiler_params=_cp())(mem, g_mem, wk, wv)


def _xattn_probs(q, k):
    s = _dot_nt(q, k) * X_SCALE
    m = jnp.max(s, axis=-1, keepdims=True)
    p = jnp.exp(s - m)
    return p, jnp.sum(p, axis=-1, keepdims=True)


def _f_mid(x, att, rec, g_oa, g_ol, w_out, g_cross, wq, kx, vx, wo, tm, comm=None):
    s_len = x.shape[0]
    m_len = kx.shape[0]

    def body(x_ref, att_ref, rec_ref, goa_ref, gol_ref, wout_ref, gc_ref, wq_ref, kx_ref, vx_ref, wo_ref,
             mg_ref, x1_ref, hc_ref, qx_ref, ox_ref, x2_ref):
        av = att_ref[...]
        rv = rec_ref[...]
        mg_ref[:, 0:D_ATT] = (av * _rinv(av) * goa_ref[...]).astype(BF16)
        mg_ref[:, D_ATT:1024] = (rv * _rinv(rv) * gol_ref[...]).astype(BF16)
        x1 = x_ref[...] + _dot(mg_ref[...], wout_ref[...])
        x1_ref[...] = x1
        hc = (x1 * _rinv(x1) * gc_ref[...]).astype(BF16)
        hc_ref[...] = hc
        qx_ref[...] = _dot(hc, wq_ref[...]).astype(BF16)
        for h in range(X_HEADS):
            sl = slice(h * X_HEAD_DIM, (h + 1) * X_HEAD_DIM)
            p, l = _xattn_probs(qx_ref[:, sl], kx_ref[:, sl])
            ox_ref[:, sl] = (_dot(p.astype(BF16), vx_ref[:, sl]) / l).astype(BF16)
        x2_ref[...] = x1 + _dot(ox_ref[...], wo_ref[...])

    sq = _full((1024, 1024))
    return _call(
        body, "f_mid", (s_len // tm,),
        [_rows(tm, 1024), _rows(tm, 512), _rows(tm, 512), _full((1, 512)), _full((1, 512)), sq,
         _full((1, 1024)), sq, _full((m_len, 1024)), _full((m_len, 1024)), sq],
        [_rows(tm, 1024)] * 6,
        [_sds((s_len, 1024), BF16), _sds((s_len, 1024), F32), _sds((s_len, 1024), BF16),
         _sds((s_len, 1024), BF16), _sds((s_len, 1024), BF16), _sds((s_len, 1024), F32)],
        [], (x, att, rec, g_oa, g_ol, w_out, g_cross, wq, kx, vx, wo), "arbitrary", comm)


def _load_weights_once(pairs):
    @pl.when(pl.program_id(0) == 0)
    def _():
        for hbm, vmem in pairs:
            pltpu.sync_copy(hbm, vmem)


FF_CHUNKS = [(0, 1280), (1280, D_FF)]


def _f_ffn(x2, tgt, g_ffn, g_final, wg, wu, wd, tm):
    s_len = x2.shape[0]

    def body(x2_ref, t_ref, gf_ref, gfin_ref, wg_hbm, wu_hbm, wd_hbm,
             hf_ref, g_ref, u_ref, a_ref, dx3_ref, loss_ref, dgfin_ref, wg_ref, wu_ref, wd_ref):
        _load_weights_once([(wg_hbm, wg_ref), (wu_hbm, wu_ref), (wd_hbm, wd_ref)])

        @pl.when(pl.program_id(0) == 0)
        def _():
            loss_ref[...] = jnp.zeros_like(loss_ref)
            dgfin_ref[...] = jnp.zeros_like(dgfin_ref)

        x2v = x2_ref[...]
        hf = (x2v * _rinv(x2v) * gf_ref[...]).astype(BF16)
        hf_ref[...] = hf
        x3 = x2v
        for c0, c1 in FF_CHUNKS:
            gv = _dot_nt(hf, wg_ref[c0:c1, :])
            uv = _dot_nt(hf, wu_ref[c0:c1, :])
            av = (gv * jax.nn.sigmoid(gv) * uv).astype(BF16)
            g_ref[:, c0:c1] = gv.astype(BF16)
            u_ref[:, c0:c1] = uv.astype(BF16)
            a_ref[:, c0:c1] = av
            x3 = x3 + _dot(av, wd_ref[c0:c1, :])
        r3 = _rinv(x3)
        yh = x3 * r3
        gfin = gfin_ref[...]
        err = yh * gfin - t_ref[...]
        loss_ref[...] += jnp.full((1, 128), 0.5 / D_MODEL, F32) * jnp.sum(err * err)
        dy = err * (1.0 / D_MODEL)
        dgfin_ref[...] += jnp.sum(dy * yh, axis=0, keepdims=True)
        dyh = dy * gfin
        dx3_ref[...] = r3 * (dyh - yh * jnp.mean(dyh * yh, axis=-1, keepdims=True))

    vec = _full((1, 1024))
    return pl.pallas_call(
        body, name="f_ffn", grid=(s_len // tm,),
        in_specs=[_rows(tm, 1024), _rows(tm, 1024), vec, vec, _any(), _any(), _any()],
        out_specs=[_rows(tm, 1024), _rows(tm, D_FF), _rows(tm, D_FF), _rows(tm, D_FF),
                   _rows(tm, 1024), _full((1, 128)), vec],
        out_shape=[_sds((s_len, 1024), BF16)] + [_sds((s_len, D_FF), BF16)] * 3
                  + [_sds((s_len, 1024), F32), _sds((1, 128), F32), _sds((1, 1024), F32)],
        scratch_shapes=[pltpu.VMEM((D_FF, 1024), BF16)] * 3,
        compiler_params=_cp("arbitrary"))(x2, tgt, g_ffn, g_final, wg, wu, wd)


def _b_ffn(dx3, x2, gact, uact, g_ffn, wg, wu, wd, tm):
    s_len = x2.shape[0]

    def body(dx3_ref, x2_ref, g_ref, u_ref, gf_ref, wg_hbm, wu_hbm, wd_hbm,
             dg_ref, du_ref, dx2_ref, dgf_ref, wg_ref, wu_ref, wd_ref):
        _load_weights_once([(wg_hbm, wg_ref), (wu_hbm, wu_ref), (wd_hbm, wd_ref)])

        @pl.when(pl.program_id(0) == 0)
        def _():
            dgf_ref[...] = jnp.zeros_like(dgf_ref)

        dx3v = dx3_ref[...]
        dx3b = dx3v.astype(BF16)
        dhf = jnp.zeros(dx3v.shape, F32)
        for c0, c1 in FF_CHUNKS:
            da = _dot_nt(dx3b, wd_ref[c0:c1, :])
            gv = g_ref[:, c0:c1].astype(F32)
            uv = u_ref[:, c0:c1].astype(F32)
            sg = jax.nn.sigmoid(gv)
            dub = (da * gv * sg).astype(BF16)
            dgb = (da * uv * (sg * (1.0 + gv * (1.0 - sg)))).astype(BF16)
            du_ref[:, c0:c1] = dub
            dg_ref[:, c0:c1] = dgb
            dhf = dhf + _dot(dgb, wg_ref[c0:c1, :]) + _dot(dub, wu_ref[c0:c1, :])
        dx, dgf = _rms_bwd(dhf, x2_ref[...], gf_ref[...])
        dx2_ref[...] = dx3v + dx
        dgf_ref[...] += dgf

    vec = _full((1, 1024))
    return pl.pallas_call(
        body, name="b_ffn", grid=(s_len // tm,),
        in_specs=[_rows(tm, 1024), _rows(tm, 1024), _rows(tm, D_FF), _rows(tm, D_FF), vec,
                  _any(), _any(), _any()],
        out_specs=[_rows(tm, D_FF), _rows(tm, D_FF), _rows(tm, 1024), vec],
        out_shape=[_sds((s_len, D_FF), BF16)] * 2 + [_sds((s_len, 1024), F32), _sds((1, 1024), F32)],
        scratch_shapes=[pltpu.VMEM((D_FF, 1024), BF16)] * 3,
        compiler_params=_cp("arbitrary"))(dx3, x2, gact, uact, g_ffn, wg, wu, wd)


def _b_mid(dx2, qx, x1, att, rec, kx, vx, wo, wq, w_out, g_cross, g_oa, g_ol, tm, comm=None):
    s_len = x1.shape[0]
    m_len = kx.shape[0]

    def body(dx2_ref, qx_ref, x1_ref, att_ref, rec_ref, kx_ref, vx_ref, wo_ref, wq_ref, wout_ref,
             gc_ref, goa_ref, gol_ref,
             dqx_ref, dx1_ref, datt_ref, drec_ref, dkx_ref, dvx_ref, dgc_ref, dgoa_ref, dgol_ref):
        @pl.when(pl.program_id(0) == 0)
        def _():
            for r in (dkx_ref, dvx_ref, dgc_ref, dgoa_ref, dgol_ref):
                r[...] = jnp.zeros_like(r)

        dx2v = dx2_ref[...]
        dox = _dot_nt(dx2v.astype(BF16), wo_ref[...])
        for h in range(X_HEADS):
            sl = slice(h * X_HEAD_DIM, (h + 1) * X_HEAD_DIM)
            q = qx_ref[:, sl]
            p, l = _xattn_probs(q, kx_ref[:, sl])
            pn = p / l
            dob = dox[:, sl].astype(BF16)
            dp = _dot_nt(dob, vx_ref[:, sl])
            dvx_ref[:, sl] += _dot_tn(pn.astype(BF16), dob)
            ds = pn * (dp - jnp.sum(dp * pn, axis=-1, keepdims=True))
            dsb = (ds * X_SCALE).astype(BF16)
            dqx_ref[:, sl] = _dot(dsb, kx_ref[:, sl]).astype(BF16)
            dkx_ref[:, sl] += _dot_tn(dsb, q)
        dhc = _dot_nt(dqx_ref[...], wq_ref[...])
        dx, dgc = _rms_bwd(dhc, x1_ref[...], gc_ref[...])
        dx1 = dx2v + dx
        dx1_ref[...] = dx1
        dgc_ref[...] += dgc
        dmg = _dot_nt(dx1.astype(BF16), wout_ref[...])
        da, dgoa = _rms_bwd(dmg[:, 0:D_ATT], att_ref[...], goa_ref[...])
        datt_ref[...] = da
        dgoa_ref[...] += dgoa
        dr, dgol = _rms_bwd(dmg[:, D_ATT:1024], rec_ref[...], gol_ref[...])
        drec_ref[...] = dr
        dgol_ref[...] += dgol

    sq = _full((1024, 1024))
    mk = _full((m_len, 1024))
    return _call(
        body, "b_mid", (s_len // tm,),
        [_rows(tm, 1024), _rows(tm, 1024), _rows(tm, 1024), _rows(tm, 512), _rows(tm, 512), mk, mk,
         sq, sq, sq, _full((1, 1024)), _full((1, 512)), _full((1, 512))],
        [_rows(tm, 1024), _rows(tm, 1024), _rows(tm, 512), _rows(tm, 512), mk, mk,
         _full((1, 1024)), _full((1, 512)), _full((1, 512))],
        [_sds((s_len, 1024), BF16), _sds((s_len, 1024), F32), _sds((s_len, 512), F32),
         _sds((s_len, 512), F32), _sds((m_len, 1024), F32), _sds((m_len, 1024), F32),
         _sds((1, 1024), F32), _sds((1, 512), F32), _sds((1, 512), F32)],
        [], (dx2, qx, x1, att, rec, kx, vx, wo, wq, w_out, g_cross, g_oa, g_ol), "arbitrary", comm)


def _b_mem(dkx, dvx, mem, mn, g_mem, wk, wv):
    def body(dkx_ref, dvx_ref, mem_ref, mn_ref, g_ref, wk_ref, wv_ref, dwk_ref, dwv_ref, dgm_ref,
             dwkb_ref, dwvb_ref):
        dkb = dkx_ref[...].astype(BF16)
        dvb = dvx_ref[...].astype(BF16)
        dwk = _dot_tn(mn_ref[...], dkb)
        dwv = _dot_tn(mn_ref[...], dvb)
        dwk_ref[...] = dwk
        dwv_ref[...] = dwv
        dwkb_ref[...] = dwk.astype(BF16)
        dwvb_ref[...] = dwv.astype(BF16)
        dmn = _dot_nt(dkb, wk_ref[...]) + _dot_nt(dvb, wv_ref[...])
        mv = mem_ref[...]
        dgm_ref[...] = jnp.sum(dmn * (mv * _rinv(mv)), axis=0, keepdims=True)

    return pl.pallas_call(
        body, name="b_mem",
        out_shape=[_sds((1024, 1024), F32), _sds((1024, 1024), F32), _sds((1, 1024), F32),
                   _sds((1024, 1024), BF16), _sds((1024, 1024), BF16)],
        compiler_params=_cp())(dkx, dvx, mem, mn, g_mem, wk, wv)


def _b_lru(drec, hs, u, xg, conv_w, wrg, brg, wig, big, lam, tl, comm=None):
    s_len = xg.shape[0]
    nt = s_len // tl

    def body(drec_ref, hs_ref, hsp_ref, u_ref, xg_ref, cw_ref, wrg_ref, brg_ref, wig_ref, big_ref, l_ref,
             dxg_ref, dwrg_ref, dwig_ref, dbrg_ref, dbig_ref, dlam_ref, dcw_ref, dcb_ref,
             hbuf, abuf, dubuf, c_sc, d_sc, lam_sc, lcar, wacc_r, wacc_i):
        i = pl.program_id(0)
        tt = nt - 1 - i

        @pl.when(i == 0)
        def _():
            for r in (wacc_r, wacc_i, dbrg_ref, dbig_ref, dlam_ref, dcw_ref, dcb_ref):
                r[...] = jnp.zeros_like(r)
            abuf[tl:tl + 8, :] = jnp.zeros((8, D_LRU), F32)
            dubuf[tl:tl + 8, :] = jnp.zeros((8, D_LRU), F32)
            lcar[...] = jnp.zeros((8, D_LRU), F32)

        xu0 = xg_ref[:, 0:D_LRU]
        hsv = hs_ref[...]
        uv = u_ref[...]
        hbuf[8:8 + tl, :] = hsv
        hbuf[0:8, :] = jnp.where(tt > 0, hsp_ref[...], 0.0)
        hshift = hbuf[pl.ds(7, tl), :]
        wrg_v = wrg_ref[...]
        wig_v = wig_ref[...]
        lamv = l_ref[...]
        ub, r, ig, sp, a, mult = _lru_gates(uv, wrg_v, brg_ref[...], wig_v, big_ref[...], lamv)
        abuf[0:tl, :] = a
        c_sc[...] = abuf[pl.ds(1, tl), :]
        gel, dgel = _gelu_and_grad(xg_ref[:, D_LRU:2 * D_LRU])
        drv = drec_ref[...]
        d_sc[...] = drv * gel
        dxg_ref[:, D_LRU:2 * D_LRU] = (drv * hsv * dgel).astype(BF16)

        def grp(k, lnext):
            off = pl.multiple_of((tl // 8 - 1 - k) * 8, 8)
            l8 = _rscan8(c_sc[pl.ds(off, 8), :], d_sc[pl.ds(off, 8), :], lnext)
            lam_sc[pl.ds(off, 8), :] = l8
            return l8[0:1, :]

        lcar[0:1, :] = lax.fori_loop(0, tl // 8, grp, lcar[0:1, :])
        abuf[tl:tl + 8, :] = a[0:8, :]
        db = lam_sc[...]
        da = db * hshift
        dmult = db * (ig * uv)
        dig = db * mult * uv
        du = db * mult * ig
        dla = da * a - dmult * (a * a) / mult
        dlam_ref[...] += jnp.sum(dla * (-LRU_C) * r, axis=0, keepdims=True)
        dzr = dla * (-LRU_C * sp) * r * (1.0 - r)
        dzi = dig * ig * (1.0 - ig)
        dzrb = dzr.astype(BF16)
        dzib = dzi.astype(BF16)
        du = du + _dot_nt(dzrb, wrg_v) + _dot_nt(dzib, wig_v)
        wacc_r[...] += _dot_tn(ub, dzrb)
        wacc_i[...] += _dot_tn(ub, dzib)
        dbrg_ref[...] += jnp.sum(dzr, axis=0, keepdims=True)
        dbig_ref[...] += jnp.sum(dzi, axis=0, keepdims=True)
        dcb_ref[...] += jnp.sum(du, axis=0, keepdims=True)
        dubuf[0:tl, :] = du
        dxu0 = jnp.zeros((tl, D_LRU), F32)
        for j in range(4):
            dsh = dubuf[pl.ds(3 - j, tl), :]
            dxu0 = dxu0 + cw_ref[j:j + 1, :] * dsh
            dcw_ref[j:j + 1, :] += jnp.sum(xu0 * dsh, axis=0, keepdims=True)
        dubuf[tl:tl + 8, :] = du[0:8, :]
        dxg_ref[:, 0:D_LRU] = dxu0.astype(BF16)

        @pl.when(i == nt - 1)
        def _():
            dlam_ref[...] = dlam_ref[...] * (-jax.nn.sigmoid(-lamv))
            for n in range(LRU_BLOCKS):
                blk = slice(n * LRU_BLOCK, (n + 1) * LRU_BLOCK)
                dwrg_ref[n] = wacc_r[blk, blk]
                dwig_ref[n] = wacc_i[blk, blk]

    def rev(n):
        return pl.BlockSpec((tl, n), lambda i: (nt - 1 - i, 0))

    prev8 = pl.BlockSpec((8, D_LRU), lambda i: (jnp.maximum((nt - 1 - i) * (tl // 8) - 1, 0), 0))
    vec = _full((1, D_LRU))
    sq = _full((D_LRU, D_LRU))
    blocks_shape = (LRU_BLOCKS, LRU_BLOCK, LRU_BLOCK)
    blocks = _full(blocks_shape)
    return _call(
        body, "b_lru", (nt,),
        [rev(D_LRU), rev(D_LRU), prev8, rev(D_LRU), rev(1024), _full((4, D_LRU)), sq, vec, sq, vec, vec],
        [rev(1024), blocks, blocks, vec, vec, vec, _full((4, D_LRU)), vec],
        [_sds((s_len, 1024), BF16), _sds(blocks_shape, F32), _sds(blocks_shape, F32),
         _sds((1, D_LRU), F32), _sds((1, D_LRU), F32), _sds((1, D_LRU), F32),
         _sds((4, D_LRU), F32), _sds((1, D_LRU), F32)],
        [pltpu.VMEM((tl + 8, D_LRU), F32)] * 3 + [pltpu.VMEM((tl, D_LRU), F32)] * 3
        + [pltpu.VMEM((8, D_LRU), F32)] + [pltpu.VMEM((D_LRU, D_LRU), F32)] * 2,
        (drec, hs, hs, u, xg, conv_w, wrg, brg, wig, big, lam), "arbitrary", comm)


def _b_attn(qkv_pad, att, datt, frow, comm=None):
    s_len = datt.shape[0]
    nb = s_len // QB
    n_pair = ATT_HEADS // 2
    pair_w = 2 * HEAD_DIM

    def body(q_ref, k0, k1, k2, v0, v1, v2, o_ref, do_ref, frow_ref, dq_ref, dkv_ref, dfrow_ref,
             bias_sc, dt_sc, acc_sc):
        t = pl.program_id(0)

        @pl.when(t == 0)
        def _():
            _bias_table(frow_ref, bias_sc)
            dt_sc[...] = jnp.zeros_like(dt_sc)
            acc_sc[...] = jnp.zeros_like(acc_sc)

        @pl.when(t < nb)
        def _():
            var = jnp.minimum(t, N_BIAS - 1)
            even = _even_lanes()
            for hp in range(n_pair):
                cs = slice(hp * pair_w, (hp + 1) * pair_w)
                qt = q_ref[:, cs]
                kts = [k0[:, cs], k1[:, cs], k2[:, cs]]
                vts = [v0[:, cs], v1[:, cs], v2[:, cs]]
                dot = do_ref[:, cs]
                dd = dot * o_ref[:, cs]
                dos_pair, dsbs, pbs, dqs = None, [], [], []
                for e in range(2):
                    keep = even if e == 0 else jnp.logical_not(even)
                    qm = jnp.where(keep, qt, 0)
                    p = _att_probs(qm, kts, bias_sc[var, 2 * hp + e])
                    inv = 1.0 / jnp.sum(p, axis=-1, keepdims=True)
                    dos = jnp.where(keep, dot * inv, 0.0)
                    delta = jnp.sum(jnp.where(keep, dd, 0.0), axis=-1, keepdims=True) * inv
                    dp = jnp.concatenate([_dot_nt(dos.astype(BF16), v) for v in vts], axis=1)
                    ds = p * (dp - delta)
                    dt_sc[2 * hp + e] += ds
                    dsb = ds.astype(BF16)
                    dq = _dot(dsb[:, 0:QB], kts[0])
                    for j in (1, 2):
                        dq = dq + _dot(dsb[:, j * QB:(j + 1) * QB], kts[j])
                    dqs.append(dq)
                    dsbs.append(dsb)
                    pbs.append(p.astype(BF16))
                    dos_pair = dos if e == 0 else dos_pair + dos
                dq_ref[:, cs] = (jnp.where(even, dqs[0], dqs[1]) * ATT_SCALE).astype(BF16)
                qtt = qt.astype(F32).T.astype(BF16)
                dost = dos_pair.T.astype(BF16)
                for j in range(3):
                    slot = (t + 1 + j) % 3
                    js = slice(j * QB, (j + 1) * QB)
                    for e in range(2):
                        hr = slice(e * HEAD_DIM, (e + 1) * HEAD_DIM)
                        acc_sc[slot, hp, hr, :] += _dot(qtt[hr], dsbs[e][:, js])
                        acc_sc[slot, n_pair + hp, hr, :] += _dot(dost[hr], pbs[e][:, js])

        done = (t + 1) % 3

        @pl.when(t >= 2)
        def _():
            for i in range(2 * n_pair):
                dkv_ref[:, i * pair_w:(i + 1) * pair_w] = acc_sc[done, i].T.astype(BF16)

        acc_sc[done] = jnp.zeros((2 * n_pair, pair_w, QB), F32)

        @pl.when(t == nb + 1)
        def _():
            row = lax.broadcasted_iota(jnp.int32, (8, ROLL_W), 0)
            pad = jnp.zeros((8, ROLL_W - KB), F32)
            for h in range(ATT_HEADS):
                acc8 = jnp.concatenate([dt_sc[h, 0:8, :], pad], axis=1)
                for a1 in range(1, QB // 8):
                    blk = jnp.concatenate([dt_sc[h, 8 * a1:8 * a1 + 8, :], pad], axis=1)
                    acc8 = acc8 + pltpu.roll(blk, ROLL_W - 8 * a1, 1)
                for k in range(3):
                    acc8 = jnp.where(((row >> k) & 1) == 1, pltpu.roll(acc8, ROLL_W - (1 << k), 1), acc8)
                dfrow_ref[h:h + 1, :] = jnp.sum(acc8, axis=0, keepdims=True)

    clamp = lambda t: jnp.minimum(t, nb - 1)
    qrows = pl.BlockSpec((QB, D_ATT), lambda t: (clamp(t), 0))
    return _call(
        body, "b_attn", (nb + 2,),
        _att_in_specs(clamp) + [qrows, qrows, _full((ATT_HEADS, ROLL_W))],
        [qrows, pl.BlockSpec((QB, 2 * D_ATT), lambda t: (jnp.maximum(t - 2, 0), 0)),
         _full((ATT_HEADS, ROLL_W))],
        [_sds((s_len, D_ATT), BF16), _sds((s_len, 2 * D_ATT), BF16), _sds((ATT_HEADS, ROLL_W), F32)],
        [pltpu.VMEM((N_BIAS, ATT_HEADS, QB, KB), F32), pltpu.VMEM((ATT_HEADS, QB, KB), F32),
         pltpu.VMEM((3, 2 * n_pair, pair_w, QB), F32)],
        (*([qkv_pad] * 7), att, datt, frow), "arbitrary", comm)


def _b_win(dq, dkv, dxg, h, ts):
    s_len = h.shape[0]
    steps = s_len // ts

    def body(dq_ref, dkv_ref, dxg_ref, h_ref, dw_ref, dwb_ref):
        @pl.when(pl.program_id(0) == 0)
        def _():
            dw_ref[...] = jnp.zeros_like(dw_ref)

        dproj = jnp.concatenate([dq_ref[...], dkv_ref[...], dxg_ref[...]], axis=1)
        hv = h_ref[...]
        for s in range(N_SHARD):
            dw_ref[s] += _dot_tn(hv, dproj[:, s * IN_SH:(s + 1) * IN_SH])

        @pl.when(pl.program_id(0) == steps - 1)
        def _():
            dwb_ref[...] = dw_ref[...].astype(BF16)

    wspec = _full((N_SHARD, 1024, IN_SH))
    return pl.pallas_call(
        body, name="b_win", grid=(steps,),
        in_specs=[_rows(ts, 512), _rows(ts, 1024), _rows(ts, 1024), _rows(ts, 1024)],
        out_specs=[wspec, wspec],
        out_shape=[_sds((N_SHARD, 1024, IN_SH), F32), _sds((N_SHARD, 1024, IN_SH), BF16)],
        compiler_params=_cp("arbitrary"))(dq, dkv, dxg, h)


def _b_inproj(dq, dkv, dxg, x, dx1, g_mix, w_in_g, tm, comm=None):
    s_len = x.shape[0]

    def body(dq_ref, dkv_ref, dxg_ref, x_ref, dx1_ref, g_ref, w_hbm, gx_ref, dgm_ref, w_ref):
        _load_w_in_once(w_hbm, w_ref)

        @pl.when(pl.program_id(0) == 0)
        def _():
            dgm_ref[...] = jnp.zeros_like(dgm_ref)

        dproj = jnp.concatenate([dq_ref[...], dkv_ref[...], dxg_ref[...]], axis=1)
        dh = _dot_nt(dproj, w_ref[...])
        dx, dgm = _rms_bwd(dh, x_ref[...], g_ref[...])
        gx_ref[...] = dx1_ref[...] + dx
        dgm_ref[...] += dgm

    return _call(
        body, "b_inproj", (s_len // tm,),
        [_rows(tm, 512), _rows(tm, 1024), _rows(tm, 1024), _rows(tm, 1024), _rows(tm, 1024),
         _full((1, 1024)), _any()],
        [_rows(tm, 1024), _full((1, 1024))],
        [_sds((s_len, 1024), F32), _sds((1, 1024), F32)],
        [pltpu.VMEM((1024, D_IN), BF16)], (dq, dkv, dxg, x, dx1, g_mix, w_in_g), "arbitrary", comm)


def _mm_tn(xa, ya, name, ts):
    s_len, k = xa.shape
    n = ya.shape[1]

    steps = s_len // ts

    def body(x_ref, y_ref, o_ref, ob_ref):
        @pl.when(pl.program_id(0) == 0)
        def _():
            o_ref[...] = jnp.zeros_like(o_ref)
        o_ref[...] += _dot_tn(x_ref[...].astype(BF16), y_ref[...].astype(BF16))

        @pl.when(pl.program_id(0) == steps - 1)
        def _():
            ob_ref[...] = o_ref[...].astype(BF16)

    return pl.pallas_call(
        body, name=name, grid=(steps,), in_specs=[_rows(ts, k), _rows(ts, n)],
        out_specs=[_full((k, n))] * 2, out_shape=[_sds((k, n), F32), _sds((k, n), BF16)],
        compiler_params=_cp("arbitrary"))(xa, ya)


def _frow_from_rel_bias(rb):
    hi = jnp.broadcast_to(rb[:, 256:257], (ATT_HEADS, 385))
    mid = rb[:, 1:256][:, ::-1]
    lo = jnp.broadcast_to(rb[:, 0:1], (ATT_HEADS, 128))
    wrap = jnp.broadcast_to(rb[:, 256:257], (ATT_HEADS, ROLL_W - KB))
    return jnp.concatenate([hi, mid, lo, wrap], axis=1)


def _rel_bias_grad_from_dfrow(df):
    g256 = jnp.sum(df[:, 0:385], axis=1, keepdims=True) + jnp.sum(df[:, KB:ROLL_W], axis=1, keepdims=True)
    mid = df[:, 385:640][:, ::-1]
    g0 = jnp.sum(df[:, 640:KB], axis=1, keepdims=True)
    return jnp.concatenate([g0, mid, g256], axis=1)


def _block_diag(w):
    eye = jnp.eye(8, dtype=w.dtype)
    return (w[:, :, None, :] * eye[:, None, :, None]).reshape(D_LRU, D_LRU)


MID = ['w_out', 'wq_c', 'wk_c', 'wv_c', 'wo_c']
TRANSPOSED = ['w_gate', 'w_up']
AG_IN_INPROJ = ['w_out', 'wq_c', 'wk_c']
AG_IN_ATTN = ['wv_c', 'wo_c', 'w_gate']
AG_IN_LRU = ['w_up']
AG_IN_MID = ['w_down']
RS_IN_MID = ['w_gate', 'w_up']
RS_IN_LRU = ['w_down']
RS_IN_ATTN = MID


def _local_step(x, mem, tgt, p, gw, shards=None, chip=None):
    s_len = x.shape[0]
    tm = min(256, s_len)
    tmb = min(512, s_len)
    tl = min(512, s_len)
    frow = _frow_from_rel_bias(p['rel_bias'])
    wrg = _block_diag(p['w_rg']).astype(BF16)
    wig = _block_diag(p['w_ig']).astype(BF16)
    gw = dict(gw)

    big, bigb, recv, part, sib = {}, {}, {}, {}, {}

    def ag(names):
        return [] if shards is None else [("ag", [shards[n] for n in names])]

    def rs(names):
        return [] if shards is None else [("rs", [bigb[n] for n in names])]

    def swap(names):
        return [] if shards is None else [("swap", [part[n] for n in names])]

    def reduce_own(names):
        if shards is not None:
            for n in names:
                part[n] = _sum_parts(big[n], recv[n], chip, "sum_" + n)

    h, qkv_pad, xg, *got = _f_inproj(x, p['g_mix'], gw['w_in'], tmb, ag(AG_IN_INPROJ))
    gw.update(zip(AG_IN_INPROJ, got))
    att, *got = _f_attn(qkv_pad, frow, ag(AG_IN_ATTN))
    gw.update(zip(AG_IN_ATTN, got))
    rec, u, hs, *got = _f_lru(xg, p['conv_w'], p['conv_b'], wrg, p['b_rg'], wig, p['b_ig'], p['lru_L'], tl,
                              ag(AG_IN_LRU))
    gw.update(zip(AG_IN_LRU, got))
    w_out = gw['w_out'].reshape(1024, 1024)
    wq = gw['wq_c'].reshape(1024, 1024)
    wk = gw['wk_c'].reshape(1024, 1024)
    wv = gw['wv_c'].reshape(1024, 1024)
    wo = gw['wo_c'].reshape(1024, 1024)
    mn, kx, vx = _f_mem(mem, p['g_mem'], wk, wv)
    mg, x1, hc, qx, ox, x2, *got = _f_mid(x, att, rec, p['g_out_attn'], p['g_out_lru'], w_out, p['g_cross'],
                                          wq, kx, vx, wo, tmb, ag(AG_IN_MID))
    gw.update(zip(AG_IN_MID, got))
    ffn_w = [gw[n].reshape(D_FF, 1024) for n in ('w_gate', 'w_up', 'w_down')]
    hf, gact, uact, aact, dx3, loss, dg_final = _f_ffn(x2, tgt, p['g_ffn'], p['g_final'], *ffn_w, tmb)

    ts = min(512, s_len)
    dgact, duact, dx2, dg_ffn = _b_ffn(dx3, x2, gact, uact, p['g_ffn'], *ffn_w, tm)
    big['w_gate'], bigb['w_gate'] = _mm_tn(dgact, hf, "dw_gate", ts)
    big['w_up'], bigb['w_up'] = _mm_tn(duact, hf, "dw_up", ts)
    big['w_down'], bigb['w_down'] = _mm_tn(aact, dx3, "dw_down", ts)
    for n in ('w_gate', 'w_up', 'w_down'):
        big[n] = big[n].reshape(N_SHARD, FF_SH, 1024)
        bigb[n] = bigb[n].reshape(N_SHARD, FF_SH, 1024)

    dqx, dx1, datt, drec, dkx, dvx, dg_cross, dg_oa, dg_ol, *got = _b_mid(
        dx2, qx, x1, att, rec, kx, vx, wo, wq, w_out, p['g_cross'], p['g_out_attn'], p['g_out_lru'], tmb,
        rs(RS_IN_MID))
    recv.update(zip(RS_IN_MID, got))
    reduce_own(RS_IN_MID)
    dwk, dwv, dg_mem, dwkb, dwvb = _b_mem(dkx, dvx, mem, mn, p['g_mem'], wk, wv)
    big['wk_c'], bigb['wk_c'] = dwk, dwkb
    big['wv_c'], bigb['wv_c'] = dwv, dwvb
    tsb = min(1024, s_len)
    big['w_out'], bigb['w_out'] = _mm_tn(mg, dx1, "dw_out", tsb)
    big['wq_c'], bigb['wq_c'] = _mm_tn(hc, dqx, "dw_q", tsb)
    big['wo_c'], bigb['wo_c'] = _mm_tn(ox, dx2, "dw_o", tsb)
    for n in MID:
        big[n] = big[n].reshape(N_SHARD, 256, 1024)
        bigb[n] = bigb[n].reshape(N_SHARD, 256, 1024)

    dxg, dwrg, dwig, dbrg, dbig, dlam, dcw, dcb, *got = _b_lru(
        drec, hs, u, xg, p['conv_w'], wrg, p['b_rg'], wig, p['b_ig'], p['lru_L'], tl,
        rs(RS_IN_LRU) + swap(RS_IN_MID))
    recv.update(zip(RS_IN_LRU, got))
    sib.update(zip(RS_IN_MID, got[len(RS_IN_LRU):]))
    reduce_own(RS_IN_LRU)
    small = {
        'conv_w': dcw, 'conv_b': dcb, 'w_rg': dwrg, 'b_rg': dbrg, 'w_ig': dwig, 'b_ig': dbig, 'lru_L': dlam,
        'g_out_attn': dg_oa, 'g_out_lru': dg_ol, 'g_cross': dg_cross, 'g_mem': dg_mem, 'g_ffn': dg_ffn,
        'g_final': dg_final,
    }
    names = [n for n in SMALL if n in small]
    gather = [] if shards is None else [("ag8", [_pack_small(names, [small[n] for n in names], loss)])]
    dq, dkv, dfrow, *got = _b_attn(qkv_pad, att, datt, frow, rs(RS_IN_ATTN) + swap(RS_IN_LRU) + gather)
    recv.update(zip(RS_IN_ATTN, got))
    sib.update(zip(RS_IN_LRU, got[len(RS_IN_ATTN):]))
    packs = got[-1] if gather else None
    reduce_own(RS_IN_ATTN)
    small['rel_bias'] = _rel_bias_grad_from_dfrow(dfrow)
    big['w_in'], bigb['w_in'] = _b_win(dq, dkv, dxg, h, ts)
    grad_x, small['g_mix'], *got = _b_inproj(dq, dkv, dxg, x, dx1, p['g_mix'], gw['w_in'], tmb,
                                             rs(['w_in']) + swap(RS_IN_ATTN))
    recv.update(zip(['w_in'], got))
    sib.update(zip(RS_IN_ATTN, got[1:]))
    reduce_own(['w_in'])
    return loss, grad_x, small, big, part, sib, packs


def _cast_shards(ws):
    def body(*refs):
        n = len(refs) // 2
        for src, dst in zip(refs[:n], refs[n:]):
            dst[...] = src[...].astype(BF16)

    return pl.pallas_call(body, name="cast_shards", out_shape=[_sds(w.shape, BF16) for w in ws],
                          compiler_params=_cp())(*ws)


def _sum_parts(own4, recv3, chip, name):
    _, r, c = own4.shape
    steps = _ew_steps(r)
    tr = r // steps

    def body(chip_ref, own_ref, rc_ref, o_ref):
        o_ref[...] = ((own_ref[0] + rc_ref[0].astype(F32)) + rc_ref[1].astype(F32)) + rc_ref[2].astype(F32)

    grid_spec = pltpu.PrefetchScalarGridSpec(
        num_scalar_prefetch=1, grid=(steps,),
        in_specs=[pl.BlockSpec((1, tr, c), lambda i, ch: (ch[0], i, 0)),
                  pl.BlockSpec((3, tr, c), lambda i, ch: (0, i, 0))],
        out_specs=pl.BlockSpec((tr, c), lambda i, ch: (i, 0)))
    return pl.pallas_call(body, name=name, grid_spec=grid_spec, out_shape=_sds((r, c), F32),
                          compiler_params=_cp("parallel"))(chip, own4, recv3)


def _adamw_math(w, g, m, v):
    m = ADAM_B1 * m + (1.0 - ADAM_B1) * g
    v = ADAM_B2 * v + (1.0 - ADAM_B2) * (g * g)
    m_hat = m / (1.0 - ADAM_B1 ** ADAM_STEP)
    v_hat = v / (1.0 - ADAM_B2 ** ADAM_STEP)
    delta = -ADAM_LR * (m_hat / (jnp.sqrt(v_hat) + ADAM_EPS) + ADAM_WD * w)
    return delta, m, v


def _final_adamw(pa, pb, w, m, v, name):
    r, c = w.shape
    steps = _ew_steps(r)
    tr = r // steps

    def body(pa_ref, pb_ref, w_ref, m_ref, v_ref, g_ref, d_ref, nm_ref, nv_ref):
        g = pa_ref[...] + pb_ref[...]
        g_ref[...] = g
        d_ref[...], nm_ref[...], nv_ref[...] = _adamw_math(w_ref[...], g, m_ref[...], v_ref[...])

    return pl.pallas_call(
        body, name=name, grid=(steps,), in_specs=[_rows(tr, c)] * 5, out_specs=[_rows(tr, c)] * 4,
        out_shape=[_sds((r, c), F32)] * 4, compiler_params=_cp("parallel"))(pa, pb, w, m, v)


def _pack_put(ref, name, val_ref):
    r = _pack_rows()[name]
    shape = val_ref.shape
    if len(shape) == 3:
        for b in range(shape[0]):
            ref[r:r + shape[1], b * shape[2]:(b + 1) * shape[2]] = val_ref[b]
    elif shape[1] == 2 * PACK_W:
        ref[r:r + 1, :] = val_ref[:, 0:PACK_W]
        ref[r + 1:r + 2, :] = val_ref[:, PACK_W:2 * PACK_W]
    else:
        ref[r:r + shape[0], 0:shape[1]] = val_ref[...]


def _pack_get(ref, name, shape):
    r = _pack_rows()[name]
    if len(shape) == 3:
        return jnp.stack([ref[r:r + shape[1], b * shape[2]:(b + 1) * shape[2]] for b in range(shape[0])])
    if shape[1] == 2 * PACK_W:
        return jnp.concatenate([ref[r:r + 1, :], ref[r + 1:r + 2, :]], axis=1)
    return ref[r:r + shape[0], 0:shape[1]]


def _pack_small(names, g, loss):
    n = len(g)

    def body(*refs):
        pack = refs[n + 1]
        pack[...] = jnp.zeros_like(pack)
        for a, name in enumerate(names):
            _pack_put(pack, name, refs[a])
        _pack_put(pack, 'loss', refs[n])

    return pl.pallas_call(body, name="pack_small", out_shape=_sds((PACK_ROWS, PACK_W), F32),
                          compiler_params=_cp())(*g, loss)


def _all_peers():
    x, y, c = _mesh_pos()
    peers = []
    for k in range(1, 8):
        px = 1 - x if k & 4 else x
        py = 1 - y if k & 2 else y
        pc = 1 - c if k & 1 else c
        peers.append(((px, py, pc), 4 * px + 2 * py + pc))
    return peers, 4 * x + 2 * y + c


def _ag8_copies(ins, outs, sems):
    send_sems, recv_sems, loc_sems = sems
    n = len(ins)
    peers, me = _all_peers()

    def remote(k, j, slot):
        return pltpu.make_async_remote_copy(
            src_ref=ins[k], dst_ref=outs[k].at[slot], send_sem=send_sems.at[k, j], recv_sem=recv_sems.at[k, j],
            device_id=peers[j][0], device_id_type=MESH_ID)

    def local(k):
        return pltpu.make_async_copy(ins[k], outs[k].at[me], loc_sems.at[k])

    def start():
        for k in range(n):
            local(k).start()
            for j in range(7):
                remote(k, j, me).start()

    def wait():
        for k in range(n):
            for j in range(7):
                remote(k, j, peers[j][1]).wait_recv()
        for k in range(n):
            for j in range(7):
                remote(k, j, me).wait_send()
            local(k).wait()

    return start, _no_forward, wait


def _ar_late(names, g):
    n = len(g)

    def body(*refs):
        tot_ref, pack, buf, send_sems, recv_sems = refs[n:]
        peers, me = _all_peers()

        def remote(j, slot):
            return pltpu.make_async_remote_copy(
                src_ref=pack, dst_ref=buf.at[slot], send_sem=send_sems.at[j], recv_sem=recv_sems.at[j],
                device_id=peers[j][0], device_id_type=MESH_ID)

        pack[...] = jnp.zeros_like(pack)
        for a, name in enumerate(names):
            _pack_put(pack, name, refs[a])
        for j in range(7):
            remote(j, me).start()
        buf[me] = pack[...]
        for j in range(7):
            remote(j, peers[j][1]).wait_recv()
        for j in range(7):
            remote(j, me).wait_send()
        tot = buf[0]
        for d in range(1, 8):
            tot = tot + buf[d]
        tot_ref[...] = tot

    return pl.pallas_call(
        body, name="ar_late", out_shape=_sds((LATE_ROWS, PACK_W), F32),
        scratch_shapes=[pltpu.VMEM((LATE_ROWS, PACK_W), F32), pltpu.VMEM((8, LATE_ROWS, PACK_W), F32),
                        pltpu.SemaphoreType.DMA((7,)), pltpu.SemaphoreType.DMA((7,))],
        compiler_params=_cp())(*g)


def _adamw_small(packs, late_tot, g_shapes, loss_shape, w, m, v):
    n = len(w)

    def body(*refs):
        packs_ref, late_ref = refs[0], refs[1]
        w_refs, m_refs, v_refs = (refs[2 + i * n:2 + (i + 1) * n] for i in range(3))
        o0 = 3 * n + 2
        go, do, mo, vo = (refs[o0 + i * n:o0 + (i + 1) * n] for i in range(4))
        loss_out, tot_ref = refs[o0 + 4 * n], refs[o0 + 4 * n + 1]
        x, y, _ = _mesh_pos()
        tot = packs_ref[0]
        for d in range(1, 8):
            tot = tot + packs_ref[d]
        tot_ref[...] = tot
        tot_ref[0:LATE_ROWS, :] += late_ref[...]
        loss_out[...] = _pack_get(tot_ref, 'loss', loss_shape)
        for a, name in enumerate(SMALL):
            if name == 'conv_w':
                r = _pack_rows()[name]
                ga = tot_ref[r:r + g_shapes[a][0], pl.ds(pl.multiple_of((2 * x + y) * 128, 128), 128)]
            else:
                ga = _pack_get(tot_ref, name, g_shapes[a])
            go[a][...] = ga
            do[a][...], mo[a][...], vo[a][...] = _adamw_math(w_refs[a][...], ga, m_refs[a][...], v_refs[a][...])

    out_shape = [_sds(a.shape, F32) for a in w] * 4 + [_sds(loss_shape, F32)]
    return pl.pallas_call(body, name="adamw_small", out_shape=out_shape,
                          scratch_shapes=[pltpu.VMEM((PACK_ROWS, PACK_W), F32)],
                          compiler_params=_cp())(packs, late_tot, *w, *m, *v)


PACK_W = 512
PACK_ROWS = 160
LATE = ['g_mix', 'rel_bias']
LATE_ROWS = 32


def _pack_rows():
    rows, r = {}, 0
    for name in ['g_mix', 'g_cross', 'g_mem', 'g_ffn', 'g_final']:
        rows[name] = r
        r += 2
    for name in ['conv_b', 'b_rg', 'b_ig', 'lru_L', 'g_out_attn', 'g_out_lru']:
        rows[name] = r
        r += 1
    rows['conv_w'] = r
    rows['loss'] = r + 4
    rows['rel_bias'] = 24
    rows['w_rg'] = 32
    rows['w_ig'] = 32 + LRU_BLOCK
    assert r + 5 <= 24 and rows['w_ig'] + LRU_BLOCK == PACK_ROWS
    assert rows['g_mix'] + 2 <= LATE_ROWS and rows['rel_bias'] + 8 <= LATE_ROWS
    return rows


INPUT_NAMES = (['x', 'mem'] + WEIGHTS + ['loss_target'] + ['m_' + n for n in WEIGHTS] + ['v_' + n for n in WEIGHTS])


def kernel(x, mem, g_mix, w_in, rel_bias, conv_w, conv_b, w_rg, b_rg, w_ig, b_ig, lru_L, g_out_attn, g_out_lru, w_out, g_cross, g_mem, wq_c, wk_c, wv_c, wo_c, g_ffn, w_gate, w_up, w_down, g_final, loss_target, m_g_mix, m_w_in, m_rel_bias, m_conv_w, m_conv_b, m_w_rg, m_b_rg, m_w_ig, m_b_ig, m_lru_L, m_g_out_attn, m_g_out_lru, m_w_out, m_g_cross, m_g_mem, m_wq_c, m_wk_c, m_wv_c, m_wo_c, m_g_ffn, m_w_gate, m_w_up, m_w_down, m_g_final, v_g_mix, v_w_in, v_rel_bias, v_conv_w, v_conv_b, v_w_rg, v_b_rg, v_w_ig, v_b_ig, v_lru_L, v_g_out_attn, v_g_out_lru, v_w_out, v_g_cross, v_g_mem, v_wq_c, v_wk_c, v_wv_c, v_wo_c, v_g_ffn, v_w_gate, v_w_up, v_w_down, v_g_final):
    a = dict(zip(INPUT_NAMES, (x, mem, g_mix, w_in, rel_bias, conv_w, conv_b, w_rg, b_rg, w_ig, b_ig, lru_L, g_out_attn, g_out_lru, w_out, g_cross, g_mem, wq_c, wk_c, wv_c, wo_c, g_ffn, w_gate, w_up, w_down, g_final, loss_target, m_g_mix, m_w_in, m_rel_bias, m_conv_w, m_conv_b, m_w_rg, m_b_rg, m_w_ig, m_b_ig, m_lru_L, m_g_out_attn, m_g_out_lru, m_w_out, m_g_cross, m_g_mem, m_wq_c, m_wk_c, m_wv_c, m_wo_c, m_g_ffn, m_w_gate, m_w_up, m_w_down, m_g_final, v_g_mix, v_w_in, v_rel_bias, v_conv_w, v_conv_b, v_w_rg, v_b_rg, v_w_ig, v_b_ig, v_lru_L, v_g_out_attn, v_g_out_lru, v_w_out, v_g_cross, v_g_mem, v_wq_c, v_wk_c, v_wv_c, v_wo_c, v_g_ffn, v_w_gate, v_w_up, v_w_down, v_g_final)))
    chip = 2 * lax.axis_index("x") + lax.axis_index("y")

    def shard(name):
        arr = a[name][0]
        return jnp.swapaxes(arr, 0, 1) if name[2:] in TRANSPOSED or name in TRANSPOSED else arr

    shards = dict(zip(BIG, _cast_shards([shard(n) for n in BIG])))
    w_in_g, conv_w_g = _comm_only("ag_w_in", [("ag", [shards['w_in']]), ("agf", [a['conv_w'][0]])])
    conv_w_full = conv_w_g.transpose(1, 0, 2).reshape(4, D_LRU)

    p = {n: a[n] for n in SMALL}
    p['rel_bias'] = a['rel_bias'][0]
    p['w_rg'] = a['w_rg'][0]
    p['w_ig'] = a['w_ig'][0]
    p['conv_w'] = conv_w_full
    p['g_final'] = a['g_final'][None, :]
    chip_arr = jnp.reshape(chip, (1,)).astype(jnp.int32)
    loss_part, grad_x, small, _, part, sib, packs = _local_step(
        a['x'][0], a['mem'][0], a['loss_target'][0], p, {'w_in': w_in_g}, shards, chip_arr)

    sib['w_in'], = _comm_only("swap_w_in", [("swap", [part['w_in']])])
    out = {}
    for n in BIG:
        res = _final_adamw(part[n], sib[n], shard(n), shard('m_' + n), shard('v_' + n), "adamw_" + n)
        out[n] = [jnp.swapaxes(r, 0, 1) for r in res] if n in TRANSPOSED else res

    def natural(arr):
        return arr[0] if arr.ndim >= 3 else (arr[None, :] if arr.ndim == 1 else arr)

    small_out = _adamw_small(packs, _ar_late(LATE, [small[n] for n in LATE]), [small[n].shape for n in SMALL],
                             loss_part.shape, *[[natural(a[pre + n]) for n in SMALL] for pre in ('', 'm_', 'v_')])
    ns = len(SMALL)
    loss = small_out[4 * ns][0, 0]

    def leaf(i, n):
        if n in BIG:
            return out[n][i][None]
        return small_out[i * ns + SMALL.index(n)].reshape(a[n].shape)

    return (loss, grad_x[None], *[leaf(i, n) for i in range(4) for n in WEIGHTS])
```

```python
import math

import jax
import jax.numpy as jnp
from jax import lax
from jax.experimental import pallas as pl
from jax.experimental.pallas import tpu as pltpu

F32 = jnp.float32
BF16 = jnp.bfloat16

D_MODEL = 1024
D_ATT = 512
D_LRU = 512
HEAD_DIM = 64
ATT_HEADS = 8
CHUNK = 64
LEFT_CHUNKS = 8
X_HEADS = 4
X_HEAD_DIM = 256
N_SHARD = 4
IN_SH = 640
D_IN = N_SHARD * IN_SH
FF_SH = 704
D_FF = N_SHARD * FF_SH
EPS = 1e-6
LRU_C = 8.0
LRU_BLOCKS = 8
LRU_BLOCK = 64
QB = 256
KB = 768
ROLL_W = 1024
NEG = -1e30
ATT_SCALE = HEAD_DIM ** -0.5
X_SCALE = X_HEAD_DIM ** -0.5

ADAM_LR = 0.001
ADAM_B1 = 0.9
ADAM_B2 = 0.999
ADAM_EPS = 1e-08
ADAM_WD = 0.01
ADAM_STEP = 10

VMEM_LIMIT_V7X = 56 * 1024 * 1024
BF16_ROWS = 16


def _ew_steps(rows):
    return max(s for s in (2, 1) if rows % (s * BF16_ROWS) == 0)
MESH_ID = pl.DeviceIdType.MESH

WEIGHTS = ['g_mix', 'w_in', 'rel_bias', 'conv_w', 'conv_b', 'w_rg', 'b_rg', 'w_ig', 'b_ig', 'lru_L',
           'g_out_attn', 'g_out_lru', 'w_out', 'g_cross', 'g_mem', 'wq_c', 'wk_c', 'wv_c', 'wo_c',
           'g_ffn', 'w_gate', 'w_up', 'w_down', 'g_final']
BIG = ['w_in', 'w_out', 'wq_c', 'wk_c', 'wv_c', 'wo_c', 'w_gate', 'w_up', 'w_down']
SMALL = [n for n in WEIGHTS if n not in BIG]


def _sds(shape, dtype):
    return jax.ShapeDtypeStruct(shape, dtype)


def _cp(*sem):
    return pltpu.CompilerParams(dimension_semantics=sem or None, vmem_limit_bytes=VMEM_LIMIT_V7X)


def _rows(tm, n):
    return pl.BlockSpec((tm, n), lambda i: (i, 0))


def _full(shape):
    nd = len(shape)
    return pl.BlockSpec(shape, lambda i: (0,) * nd)


def _dot(a, b):
    return jnp.dot(a, b, preferred_element_type=F32)


def _dot_nt(a, b):
    return lax.dot_general(a, b, (((1,), (1,)), ((), ())), preferred_element_type=F32)


def _dot_tn(a, b):
    return lax.dot_general(a, b, (((0,), (0,)), ((), ())), preferred_element_type=F32)


def _recip(x):
    return pl.reciprocal(x, approx=True)


def _sigmoid(x):
    return _recip(1.0 + jnp.exp(-x))


def _rinv(x):
    return lax.rsqrt(jnp.mean(x * x, axis=-1, keepdims=True) + EPS)


def _rms_bwd(dy, x, g):
    r = _rinv(x)
    yh = x * r
    dyh = dy * g
    dx = r * (dyh - yh * jnp.mean(dyh * yh, axis=-1, keepdims=True))
    return dx, jnp.sum(dy * yh, axis=0, keepdims=True)


def _gelu(x):
    c = math.sqrt(2.0 / math.pi)
    t = jnp.tanh(c * (x + 0.044715 * x * x * x))
    return 0.5 * x * (1.0 + t)


def _gelu_and_grad(x):
    c = math.sqrt(2.0 / math.pi)
    t = jnp.tanh(c * (x + 0.044715 * x * x * x))
    g = 0.5 * x * (1.0 + t)
    dg = 0.5 * (1.0 + t) + 0.5 * x * (1.0 - t * t) * c * (1.0 + 3.0 * 0.044715 * x * x)
    return g, dg


def _neg_expm1(z):
    series = -z * (1.0 + z * (0.5 + z * ((1.0 / 6.0) + z * (1.0 / 24.0))))
    return jnp.where(z > -0.03, series, 1.0 - jnp.exp(z))


def _lru_gates(u, wrg, brg, wig, big, lam):
    ub = u.astype(BF16)
    r = _sigmoid(_dot(ub, wrg) + brg)
    ig = _sigmoid(_dot(ub, wig) + big)
    sp = jnp.maximum(-lam, 0.0) + jnp.log1p(jnp.exp(-jnp.abs(lam)))
    la = -LRU_C * r * sp
    a = jnp.exp(la)
    mult = jnp.sqrt(jnp.maximum(_neg_expm1(2.0 * la), 0.0))
    return ub, r, ig, sp, a, mult


def _scan8(a8, b8, hprev):
    row = lax.broadcasted_iota(jnp.int32, a8.shape, 0)
    aa, bb = a8, b8
    for d in (1, 2, 4):
        a_s = pltpu.roll(aa, d, 0)
        b_s = pltpu.roll(bb, d, 0)
        m = row >= d
        bb = jnp.where(m, aa * b_s + bb, bb)
        aa = jnp.where(m, aa * a_s, aa)
    return aa * hprev + bb


def _mesh_pos():
    return lax.axis_index("x"), lax.axis_index("y"), lax.axis_index("c")


def _other_chips(x, y):
    return [(1 - x, y), (x, 1 - y), (1 - x, 1 - y)]


def _no_forward():
    pass


def _ag_full_copies(ins, outs, sems):
    send_sems, recv_sems, loc_sems = sems
    n = len(ins)
    x, y, c = _mesh_pos()
    mine = 2 * x + y
    chips = _other_chips(x, y)

    def remote(k, j, slot):
        px, py = chips[j]
        return pltpu.make_async_remote_copy(
            src_ref=ins[k], dst_ref=outs[k].at[slot], send_sem=send_sems.at[k, j], recv_sem=recv_sems.at[k, j],
            device_id=(px, py, c), device_id_type=MESH_ID)

    def local(k):
        return pltpu.make_async_copy(ins[k], outs[k].at[mine], loc_sems.at[k])

    def start():
        for k in range(n):
            local(k).start()
            for j in range(3):
                remote(k, j, mine).start()

    def wait():
        for k in range(n):
            for j, (px, py) in enumerate(chips):
                remote(k, j, 2 * px + py).wait_recv()
        for k in range(n):
            for j in range(3):
                remote(k, j, mine).wait_send()
            local(k).wait()

    return start, _no_forward, wait


def _ag_copies(ins, outs, sems):
    send_sems, recv_sems, fsend_sems, frecv_sems, loc_sems = sems
    n = len(ins)
    x, y, c = _mesh_pos()
    mine = 2 * x + y
    chips = _other_chips(x, y)

    def half(ref, hc):
        r = ref.shape[0] // 2
        return ref.at[pl.ds(pl.multiple_of(hc * r, 16), r)]

    def ici(k, j, slot):
        px, py = chips[j]
        return pltpu.make_async_remote_copy(
            src_ref=half(ins[k], c), dst_ref=half(outs[k].at[slot], c),
            send_sem=send_sems.at[k, j], recv_sem=recv_sems.at[k, j],
            device_id=(px, py, c), device_id_type=MESH_ID)

    def d2d(k, j, hc):
        px, py = chips[j]
        part = half(outs[k].at[2 * px + py], hc)
        return pltpu.make_async_remote_copy(
            src_ref=part, dst_ref=part, send_sem=fsend_sems.at[k, j], recv_sem=frecv_sems.at[k, j],
            device_id=(x, y, 1 - c), device_id_type=MESH_ID)

    def local(k):
        return pltpu.make_async_copy(ins[k], outs[k].at[mine], loc_sems.at[k])

    def start():
        for k in range(n):
            local(k).start()
            for j in range(3):
                ici(k, j, mine).start()

    def forward():
        for k in range(n):
            for j, (px, py) in enumerate(chips):
                ici(k, j, 2 * px + py).wait_recv()
                d2d(k, j, c).start()

    def wait():
        for k in range(n):
            for j in range(3):
                d2d(k, j, 1 - c).wait_recv()
        for k in range(n):
            for j in range(3):
                d2d(k, j, c).wait_send()
                ici(k, j, mine).wait_send()
            local(k).wait()

    return start, forward, wait


def _rs_copies(ins, outs, sems):
    send_sems, recv_sems = sems
    n = len(ins)
    x, y, c = _mesh_pos()
    chips = _other_chips(x, y)

    def remote(k, j):
        px, py = chips[j]
        return pltpu.make_async_remote_copy(
            src_ref=ins[k].at[2 * px + py], dst_ref=outs[k].at[j],
            send_sem=send_sems.at[k, j], recv_sem=recv_sems.at[k, j],
            device_id=(px, py, c), device_id_type=MESH_ID)

    def start():
        for k in range(n):
            for j in range(3):
                remote(k, j).start()

    def wait():
        for k in range(n):
            for j in range(3):
                remote(k, j).wait_recv()
        for k in range(n):
            for j in range(3):
                remote(k, j).wait_send()

    return start, _no_forward, wait


def _swap_copies(ins, outs, sems):
    send_sems, recv_sems = sems
    x, y, c = _mesh_pos()
    copies = [pltpu.make_async_remote_copy(
        src_ref=ins[k], dst_ref=outs[k], send_sem=send_sems.at[k], recv_sem=recv_sems.at[k],
        device_id=(x, y, 1 - c), device_id_type=MESH_ID) for k in range(len(ins))]

    def start():
        for cp in copies:
            cp.start()

    def wait():
        for cp in copies:
            cp.wait()

    return start, _no_forward, wait


def _comm_plan(groups):
    plan, arrs, shapes, sems = [], [], [], []
    for kind, group in groups:
        k = len(group)
        arrs += group
        per_peer = pltpu.SemaphoreType.DMA((k, 3))
        if kind == "ag":
            shapes += [_sds((N_SHARD,) + w.shape, w.dtype) for w in group]
            gsems = [per_peer] * 4 + [pltpu.SemaphoreType.DMA((k,))]
            maker = _ag_copies
        elif kind == "agf":
            shapes += [_sds((N_SHARD,) + w.shape, w.dtype) for w in group]
            gsems = [per_peer] * 2 + [pltpu.SemaphoreType.DMA((k,))]
            maker = _ag_full_copies
        elif kind == "ag8":
            shapes += [_sds((8,) + g.shape, g.dtype) for g in group]
            gsems = [pltpu.SemaphoreType.DMA((k, 7))] * 2 + [pltpu.SemaphoreType.DMA((k,))]
            maker = _ag8_copies
        elif kind == "rs":
            shapes += [_sds((3,) + g.shape[1:], g.dtype) for g in group]
            gsems = [pltpu.SemaphoreType.DMA((k, 3)), pltpu.SemaphoreType.DMA((k, 3))]
            maker = _rs_copies
        else:
            shapes += [_sds(g.shape, g.dtype) for g in group]
            gsems = [pltpu.SemaphoreType.DMA((k,)), pltpu.SemaphoreType.DMA((k,))]
            maker = _swap_copies
        plan.append((maker, k, len(gsems)))
        sems += gsems
    return plan, arrs, shapes, sems


def _comm_fns(plan, cins, couts, sems):
    fns, a, s = [], 0, 0
    for maker, k, ns in plan:
        fns.append(maker(cins[a:a + k], couts[a:a + k], sems[s:s + ns]))
        a += k
        s += ns

    def start():
        for st, _, _ in fns:
            st()

    def forward():
        for _, fw, _ in fns:
            fw()

    def wait():
        for _, _, wt in fns:
            wt()

    return start, forward, wait


def _call(body, name, grid, in_specs, out_specs, out_shape, scratch, args, sem, comm=None):
    if not comm:
        return pl.pallas_call(body, name=name, grid=grid, in_specs=in_specs, out_specs=out_specs,
                              out_shape=out_shape, scratch_shapes=scratch, compiler_params=_cp(sem))(*args)
    plan, c_arrs, c_shapes, c_sems = _comm_plan(comm)
    k = len(c_arrs)
    n_in, n_out, n_scr = len(in_specs), len(out_specs), len(scratch)
    last = grid[0] - 1
    fwd_step = max(1, (2 * last) // 3)

    def wrapped(*refs):
        ins, cins = refs[:n_in], refs[n_in:n_in + k]
        o0 = n_in + k
        outs, couts = refs[o0:o0 + n_out], refs[o0 + n_out:o0 + n_out + k]
        s0 = o0 + n_out + k
        start, forward, wait = _comm_fns(plan, cins, couts, refs[s0 + n_scr:])
        pl.when(pl.program_id(0) == 0)(start)
        pl.when(pl.program_id(0) == fwd_step)(forward)
        body(*ins, *outs, *refs[s0:s0 + n_scr])
        pl.when(pl.program_id(0) == last)(wait)

    return pl.pallas_call(
        wrapped, name=name, grid=grid, in_specs=list(in_specs) + [_any()] * k,
        out_specs=list(out_specs) + [_any()] * k, out_shape=list(out_shape) + c_shapes,
        scratch_shapes=list(scratch) + c_sems, compiler_params=_cp(sem))(*args, *c_arrs)


def _comm_only(name, comm):
    plan, c_arrs, c_shapes, c_sems = _comm_plan(comm)
    k = len(c_arrs)

    def body(*refs):
        start, forward, wait = _comm_fns(plan, refs[:k], refs[k:2 * k], refs[2 * k:])
        start()
        forward()
        wait()

    return pl.pallas_call(body, name=name, in_specs=[_any()] * k, out_specs=[_any()] * k, out_shape=c_shapes,
                          scratch_shapes=c_sems, compiler_params=_cp())(*c_arrs)


def _any():
    return pl.BlockSpec(memory_space=pl.ANY)


def _rscan8(c8, d8, lnext):
    row = lax.broadcasted_iota(jnp.int32, c8.shape, 0)
    cc, dd = c8, d8
    for d in (1, 2, 4):
        c_s = pltpu.roll(cc, 8 - d, 0)
        d_s = pltpu.roll(dd, 8 - d, 0)
        m = row < 8 - d
        dd = jnp.where(m, cc * d_s + dd, dd)
        cc = jnp.where(m, cc * c_s, cc)
    return cc * lnext + dd


def _load_w_in_once(w_hbm, w_ref):
    @pl.when(pl.program_id(0) == 0)
    def _():
        for s in range(N_SHARD):
            pltpu.sync_copy(w_hbm.at[s], w_ref.at[:, pl.ds(s * IN_SH, IN_SH)])


def _f_inproj(x, g_mix, w_in_g, tm, comm=None):
    s_len = x.shape[0]
    pad_rows = LEFT_CHUNKS * CHUNK
    npad = pad_rows // tm

    def body(x_ref, g_ref, w_hbm, h_ref, qkv_ref, xg_ref, w_ref):
        i = pl.program_id(0)
        _load_w_in_once(w_hbm, w_ref)

        @pl.when(i < npad)
        def _():
            qkv_ref[...] = jnp.zeros_like(qkv_ref)

        @pl.when(i >= npad)
        def _():
            xv = x_ref[...]
            h = (xv * _rinv(xv) * g_ref[...]).astype(BF16)
            h_ref[...] = h
            proj = _dot(h, w_ref[...])
            qkv_ref[:, 0:D_ATT] = (proj[:, 0:D_ATT] * ATT_SCALE).astype(BF16)
            qkv_ref[:, D_ATT:3 * D_ATT] = proj[:, D_ATT:3 * D_ATT].astype(BF16)
            xg_ref[...] = proj[:, 3 * D_ATT:D_IN]

    def tok(n):
        return pl.BlockSpec((tm, n), lambda i: (jnp.maximum(i - npad, 0), 0))

    return _call(
        body, "f_inproj", (s_len // tm + npad,),
        [tok(1024), _full((1, 1024)), _any()],
        [tok(1024), _rows(tm, 1536), tok(1024)],
        [_sds((s_len, 1024), BF16), _sds((s_len + pad_rows, 1536), BF16), _sds((s_len, 1024), F32)],
        [pltpu.VMEM((1024, D_IN), BF16)], (x, g_mix, w_in_g), "arbitrary", comm)


N_BIAS = 3


def _bias_table(frow_ref, bias_sc):
    qa = lax.broadcasted_iota(jnp.int32, (QB, KB), 0) // CHUNK
    kcol = lax.broadcasted_iota(jnp.int32, (QB, KB), 1)
    kb = kcol // CHUNK
    band = jnp.where((kb >= qa) & (kb - qa <= LEFT_CHUNKS), 0.0, NEG).astype(F32)
    for h in range(ATT_HEADS):
        row = jnp.broadcast_to(frow_ref[h:h + 1, :], (QB, ROLL_W))
        toep = pltpu.roll(row, 0, 1, stride=1, stride_axis=0)
        gen = toep[:, 0:KB] + band
        bias_sc[N_BIAS - 1, h] = gen
        for v in range(N_BIAS - 1):
            pad_keys = LEFT_CHUNKS * CHUNK - v * QB
            bias_sc[v, h] = gen + jnp.where(kcol < pad_keys, NEG, 0.0).astype(F32)


def _even_lanes():
    return lax.broadcasted_iota(jnp.int32, (1, 2 * HEAD_DIM), 1) < HEAD_DIM


def _att_probs(qm, kts, bias):
    s = jnp.concatenate([_dot_nt(qm, k) for k in kts], axis=1) + bias
    return jnp.exp(s - jnp.max(s, axis=-1, keepdims=True))


def _att_in_specs(clamp):
    def spec(j, col):
        return pl.BlockSpec((QB, D_ATT), lambda i: (clamp(i) + j, col))
    return [spec(2, 0), spec(0, 1), spec(1, 1), spec(2, 1), spec(0, 2), spec(1, 2), spec(2, 2)]


def _f_attn(qkv_pad, frow, comm=None):
    s_len = qkv_pad.shape[0] - LEFT_CHUNKS * CHUNK
    nb = s_len // QB

    def body(q_ref, k0, k1, k2, v0, v1, v2, frow_ref, o_ref, bias_sc):
        i = pl.program_id(0)

        @pl.when(i == 0)
        def _():
            _bias_table(frow_ref, bias_sc)

        var = jnp.minimum(i, N_BIAS - 1)
        even = _even_lanes()
        for hp in range(ATT_HEADS // 2):
            cs = slice(hp * 2 * HEAD_DIM, (hp + 1) * 2 * HEAD_DIM)
            qt = q_ref[:, cs]
            kts = [k0[:, cs], k1[:, cs], k2[:, cs]]
            vts = [v0[:, cs], v1[:, cs], v2[:, cs]]
            res = []
            for e in range(2):
                keep = even if e == 0 else jnp.logical_not(even)
                pb = _att_probs(jnp.where(keep, qt, 0), kts, bias_sc[var, 2 * hp + e]).astype(BF16)
                r = _dot(pb[:, 0:QB], jnp.where(keep, vts[0], 1))
                for j in (1, 2):
                    r = r + _dot(pb[:, j * QB:(j + 1) * QB], jnp.where(keep, vts[j], 1))
                res.append(r * _recip(pltpu.roll(r, HEAD_DIM, 1)))
            o_ref[:, cs] = jnp.where(even, res[0], res[1])

    return _call(
        body, "f_attn", (nb,),
        _att_in_specs(lambda i: i) + [_full((ATT_HEADS, ROLL_W))],
        [_rows(QB, D_ATT)], [_sds((s_len, D_ATT), F32)],
        [pltpu.VMEM((N_BIAS, ATT_HEADS, QB, KB), F32)], (*([qkv_pad] * 7), frow), "arbitrary", comm)


def _f_lru(xg, conv_w, conv_b, wrg, brg, wig, big, lam, tl, comm=None):
    s_len = xg.shape[0]

    def body(xg_ref, cw_ref, cb_ref, wrg_ref, brg_ref, wig_ref, big_ref, l_ref,
             rec_ref, u_ref, hs_ref, xbuf, a_sc, b_sc, hcar):
        i = pl.program_id(0)

        @pl.when(i == 0)
        def _():
            xbuf[0:8, :] = jnp.zeros((8, D_LRU), F32)
            hcar[...] = jnp.zeros((8, D_LRU), F32)

        xu0 = xg_ref[:, 0:D_LRU]
        xbuf[8:8 + tl, :] = xu0
        u = cb_ref[...] + cw_ref[0:1, :] * xbuf[pl.ds(5, tl), :]
        for j in range(1, 4):
            u = u + cw_ref[j:j + 1, :] * xbuf[pl.ds(5 + j, tl), :]
        xbuf[0:8, :] = xu0[tl - 8:tl, :]
        u_ref[...] = u
        _, _, ig, _, a, mult = _lru_gates(u, wrg_ref[...], brg_ref[...], wig_ref[...], big_ref[...], l_ref[...])
        a_sc[...] = a
        b_sc[...] = mult * (ig * u)

        def grp(g, hprev):
            off = pl.multiple_of(g * 8, 8)
            h8 = _scan8(a_sc[pl.ds(off, 8), :], b_sc[pl.ds(off, 8), :], hprev)
            hs_ref[pl.ds(off, 8), :] = h8
            return h8[7:8, :]

        hcar[0:1, :] = lax.fori_loop(0, tl // 8, grp, hcar[0:1, :])
        rec_ref[...] = hs_ref[...] * _gelu(xg_ref[:, D_LRU:2 * D_LRU])

    vec = _full((1, D_LRU))
    return _call(
        body, "f_lru", (s_len // tl,),
        [_rows(tl, 1024), _full((4, D_LRU)), vec, _full((D_LRU, D_LRU)), vec, _full((D_LRU, D_LRU)), vec, vec],
        [_rows(tl, D_LRU)] * 3, [_sds((s_len, D_LRU), F32)] * 3,
        [pltpu.VMEM((tl + 8, D_LRU), F32), pltpu.VMEM((tl, D_LRU), F32),
         pltpu.VMEM((tl, D_LRU), F32), pltpu.VMEM((8, D_LRU), F32)],
        (xg, conv_w, conv_b, wrg, brg, wig, big, lam), "arbitrary", comm)


def _f_mem(mem, g_mem, wk, wv):
    def body(mem_ref, g_ref, wk_ref, wv_ref, mn_ref, kx_ref, vx_ref):
        mv = mem_ref[...]
        mn = (mv * _rinv(mv) * g_ref[...]).astype(BF16)
        mn_ref[...] = mn
        kx_ref[...] = _dot(mn, wk_ref[...]).astype(BF16)
        vx_ref[...] = _dot(mn, wv_ref[...]).astype(BF16)

    m = mem.shape[0]
    return pl.pallas_call(
        body, name="f_mem", out_shape=[_sds((m, 1024), BF16)] * 3,
        compiler_params=_cp())(mem, g_mem, wk, wv)


def _xattn_probs(q, k):
    s = _dot_nt(q, k) * X_SCALE
    m = jnp.max(s, axis=-1, keepdims=True)
    p = jnp.exp(s - m)
    return p, jnp.sum(p, axis=-1, keepdims=True)


def _f_mid(x, att, rec, g_oa, g_ol, w_out, g_cross, wq, kx, vx, wo, tm, comm=None):
    s_len = x.shape[0]
    m_len = kx.shape[0]

    def body(x_ref, att_ref, rec_ref, goa_ref, gol_ref, wout_ref, gc_ref, wq_ref, kx_ref, vx_ref, wo_ref,
             mg_ref, x1_ref, hc_ref, qx_ref, ox_ref, x2_ref):
        av = att_ref[...]
        rv = rec_ref[...]
        mg_ref[:, 0:D_ATT] = (av * _rinv(av) * goa_ref[...]).astype(BF16)
        mg_ref[:, D_ATT:1024] = (rv * _rinv(rv) * gol_ref[...]).astype(BF16)
        x1 = x_ref[...] + _dot(mg_ref[...], wout_ref[...])
        x1_ref[...] = x1
        hc = (x1 * _rinv(x1) * gc_ref[...]).astype(BF16)
        hc_ref[...] = hc
        qx_ref[...] = _dot(hc, wq_ref[...]).astype(BF16)
        for h in range(X_HEADS):
            sl = slice(h * X_HEAD_DIM, (h + 1) * X_HEAD_DIM)
            p, l = _xattn_probs(qx_ref[:, sl], kx_ref[:, sl])
            ox_ref[:, sl] = (_dot(p.astype(BF16), vx_ref[:, sl]) * _recip(l)).astype(BF16)
        x2_ref[...] = x1 + _dot(ox_ref[...], wo_ref[...])

    sq = _full((1024, 1024))
    return _call(
        body, "f_mid", (s_len // tm,),
        [_rows(tm, 1024), _rows(tm, 512), _rows(tm, 512), _full((1, 512)), _full((1, 512)), sq,
         _full((1, 1024)), sq, _full((m_len, 1024)), _full((m_len, 1024)), sq],
        [_rows(tm, 1024)] * 6,
        [_sds((s_len, 1024), BF16), _sds((s_len, 1024), F32), _sds((s_len, 1024), BF16),
         _sds((s_len, 1024), BF16), _sds((s_len, 1024), BF16), _sds((s_len, 1024), F32)],
        [], (x, att, rec, g_oa, g_ol, w_out, g_cross, wq, kx, vx, wo), "arbitrary", comm)


def _load_weights_once(pairs):
    @pl.when(pl.program_id(0) == 0)
    def _():
        for hbm, vmem in pairs:
            pltpu.sync_copy(hbm, vmem)


FF_CHUNKS = [(0, 1280), (1280, D_FF)]


def _f_ffn(x2, tgt, g_ffn, g_final, wg, wu, wd, tm):
    s_len = x2.shape[0]

    def body(x2_ref, t_ref, gf_ref, gfin_ref, wg_hbm, wu_hbm, wd_hbm,
             hf_ref, g_ref, u_ref, a_ref, dx3_ref, loss_ref, dgfin_ref, wg_ref, wu_ref, wd_ref):
        _load_weights_once([(wg_hbm, wg_ref), (wu_hbm, wu_ref), (wd_hbm, wd_ref)])

        @pl.when(pl.program_id(0) == 0)
        def _():
            loss_ref[...] = jnp.zeros_like(loss_ref)
            dgfin_ref[...] = jnp.zeros_like(dgfin_ref)

        x2v = x2_ref[...]
        hf = (x2v * _rinv(x2v) * gf_ref[...]).astype(BF16)
        hf_ref[...] = hf
        x3 = x2v
        for c0, c1 in FF_CHUNKS:
            gv = _dot_nt(hf, wg_ref[c0:c1, :])
            uv = _dot_nt(hf, wu_ref[c0:c1, :])
            av = (gv * _sigmoid(gv) * uv).astype(BF16)
            g_ref[:, c0:c1] = gv.astype(BF16)
            u_ref[:, c0:c1] = uv.astype(BF16)
            a_ref[:, c0:c1] = av
            x3 = x3 + _dot(av, wd_ref[c0:c1, :])
        r3 = _rinv(x3)
        yh = x3 * r3
        gfin = gfin_ref[...]
        err = yh * gfin - t_ref[...]
        loss_ref[...] += jnp.full((1, 128), 0.5 / D_MODEL, F32) * jnp.sum(err * err)
        dy = err * (1.0 / D_MODEL)
        dgfin_ref[...] += jnp.sum(dy * yh, axis=0, keepdims=True)
        dyh = dy * gfin
        dx3_ref[...] = r3 * (dyh - yh * jnp.mean(dyh * yh, axis=-1, keepdims=True))

    vec = _full((1, 1024))
    return pl.pallas_call(
        body, name="f_ffn", grid=(s_len // tm,),
        in_specs=[_rows(tm, 1024), _rows(tm, 1024), vec, vec, _any(), _any(), _any()],
        out_specs=[_rows(tm, 1024), _rows(tm, D_FF), _rows(tm, D_FF), _rows(tm, D_FF),
                   _rows(tm, 1024), _full((1, 128)), vec],
        out_shape=[_sds((s_len, 1024), BF16)] + [_sds((s_len, D_FF), BF16)] * 3
                  + [_sds((s_len, 1024), F32), _sds((1, 128), F32), _sds((1, 1024), F32)],
        scratch_shapes=[pltpu.VMEM((D_FF, 1024), BF16)] * 3,
        compiler_params=_cp("arbitrary"))(x2, tgt, g_ffn, g_final, wg, wu, wd)


def _b_ffn(dx3, x2, gact, uact, g_ffn, wg, wu, wd, tm):
    s_len = x2.shape[0]

    def body(dx3_ref, x2_ref, g_ref, u_ref, gf_ref, wg_hbm, wu_hbm, wd_hbm,
             dg_ref, du_ref, dx2_ref, dgf_ref, wg_ref, wu_ref, wd_ref):
        _load_weights_once([(wg_hbm, wg_ref), (wu_hbm, wu_ref), (wd_hbm, wd_ref)])

        @pl.when(pl.program_id(0) == 0)
        def _():
            dgf_ref[...] = jnp.zeros_like(dgf_ref)

        dx3v = dx3_ref[...]
        dx3b = dx3v.astype(BF16)
        dhf = jnp.zeros(dx3v.shape, F32)
        for c0, c1 in FF_CHUNKS:
            da = _dot_nt(dx3b, wd_ref[c0:c1, :])
            gv = g_ref[:, c0:c1].astype(F32)
            uv = u_ref[:, c0:c1].astype(F32)
            sg = _sigmoid(gv)
            dub = (da * gv * sg).astype(BF16)
            dgb = (da * uv * (sg * (1.0 + gv * (1.0 - sg)))).astype(BF16)
            du_ref[:, c0:c1] = dub
            dg_ref[:, c0:c1] = dgb
            dhf = dhf + _dot(dgb, wg_ref[c0:c1, :]) + _dot(dub, wu_ref[c0:c1, :])
        dx, dgf = _rms_bwd(dhf, x2_ref[...], gf_ref[...])
        dx2_ref[...] = dx3v + dx
        dgf_ref[...] += dgf

    vec = _full((1, 1024))
    return pl.pallas_call(
        body, name="b_ffn", grid=(s_len // tm,),
        in_specs=[_rows(tm, 1024), _rows(tm, 1024), _rows(tm, D_FF), _rows(tm, D_FF), vec,
                  _any(), _any(), _any()],
        out_specs=[_rows(tm, D_FF), _rows(tm, D_FF), _rows(tm, 1024), vec],
        out_shape=[_sds((s_len, D_FF), BF16)] * 2 + [_sds((s_len, 1024), F32), _sds((1, 1024), F32)],
        scratch_shapes=[pltpu.VMEM((D_FF, 1024), BF16)] * 3,
        compiler_params=_cp("arbitrary"))(dx3, x2, gact, uact, g_ffn, wg, wu, wd)


def _b_mid(dx2, qx, x1, att, rec, kx, vx, wo, wq, w_out, g_cross, g_oa, g_ol, tm, comm=None):
    s_len = x1.shape[0]
    m_len = kx.shape[0]

    def body(dx2_ref, qx_ref, x1_ref, att_ref, rec_ref, kx_ref, vx_ref, wo_ref, wq_ref, wout_ref,
             gc_ref, goa_ref, gol_ref,
             dqx_ref, dx1_ref, datt_ref, drec_ref, dkx_ref, dvx_ref, dgc_ref, dgoa_ref, dgol_ref):
        @pl.when(pl.program_id(0) == 0)
        def _():
            for r in (dkx_ref, dvx_ref, dgc_ref, dgoa_ref, dgol_ref):
                r[...] = jnp.zeros_like(r)

        dx2v = dx2_ref[...]
        dox = _dot_nt(dx2v.astype(BF16), wo_ref[...])
        for h in range(X_HEADS):
            sl = slice(h * X_HEAD_DIM, (h + 1) * X_HEAD_DIM)
            q = qx_ref[:, sl]
            p, l = _xattn_probs(q, kx_ref[:, sl])
            pn = p * _recip(l)
            dob = dox[:, sl].astype(BF16)
            dp = _dot_nt(dob, vx_ref[:, sl])
            dvx_ref[:, sl] += _dot_tn(pn.astype(BF16), dob)
            ds = pn * (dp - jnp.sum(dp * pn, axis=-1, keepdims=True))
            dsb = (ds * X_SCALE).astype(BF16)
            dqx_ref[:, sl] = _dot(dsb, kx_ref[:, sl]).astype(BF16)
            dkx_ref[:, sl] += _dot_tn(dsb, q)
        dhc = _dot_nt(dqx_ref[...], wq_ref[...])
        dx, dgc = _rms_bwd(dhc, x1_ref[...], gc_ref[...])
        dx1 = dx2v + dx
        dx1_ref[...] = dx1
        dgc_ref[...] += dgc
        dmg = _dot_nt(dx1.astype(BF16), wout_ref[...])
        da, dgoa = _rms_bwd(dmg[:, 0:D_ATT], att_ref[...], goa_ref[...])
        datt_ref[...] = da
        dgoa_ref[...] += dgoa
        dr, dgol = _rms_bwd(dmg[:, D_ATT:1024], rec_ref[...], gol_ref[...])
        drec_ref[...] = dr
        dgol_ref[...] += dgol

    sq = _full((1024, 1024))
    mk = _full((m_len, 1024))
    return _call(
        body, "b_mid", (s_len // tm,),
        [_rows(tm, 1024), _rows(tm, 1024), _rows(tm, 1024), _rows(tm, 512), _rows(tm, 512), mk, mk,
         sq, sq, sq, _full((1, 1024)), _full((1, 512)), _full((1, 512))],
        [_rows(tm, 1024), _rows(tm, 1024), _rows(tm, 512), _rows(tm, 512), mk, mk,
         _full((1, 1024)), _full((1, 512)), _full((1, 512))],
        [_sds((s_len, 1024), BF16), _sds((s_len, 1024), F32), _sds((s_len, 512), F32),
         _sds((s_len, 512), F32), _sds((m_len, 1024), F32), _sds((m_len, 1024), F32),
         _sds((1, 1024), F32), _sds((1, 512), F32), _sds((1, 512), F32)],
        [], (dx2, qx, x1, att, rec, kx, vx, wo, wq, w_out, g_cross, g_oa, g_ol), "arbitrary", comm)


def _b_mem(dkx, dvx, mem, mn, g_mem, wk, wv):
    def body(dkx_ref, dvx_ref, mem_ref, mn_ref, g_ref, wk_ref, wv_ref, dwk_ref, dwv_ref, dgm_ref,
             dwkb_ref, dwvb_ref):
        dkb = dkx_ref[...].astype(BF16)
        dvb = dvx_ref[...].astype(BF16)
        dwk = _dot_tn(mn_ref[...], dkb)
        dwv = _dot_tn(mn_ref[...], dvb)
        dwk_ref[...] = dwk
        dwv_ref[...] = dwv
        dwkb_ref[...] = dwk.astype(BF16)
        dwvb_ref[...] = dwv.astype(BF16)
        dmn = _dot_nt(dkb, wk_ref[...]) + _dot_nt(dvb, wv_ref[...])
        mv = mem_ref[...]
        dgm_ref[...] = jnp.sum(dmn * (mv * _rinv(mv)), axis=0, keepdims=True)

    return pl.pallas_call(
        body, name="b_mem",
        out_shape=[_sds((1024, 1024), F32), _sds((1024, 1024), F32), _sds((1, 1024), F32),
                   _sds((1024, 1024), BF16), _sds((1024, 1024), BF16)],
        compiler_params=_cp())(dkx, dvx, mem, mn, g_mem, wk, wv)


def _b_lru(drec, hs, u, xg, conv_w, wrg, brg, wig, big, lam, tl, comm=None):
    s_len = xg.shape[0]
    nt = s_len // tl

    def body(drec_ref, hs_ref, hsp_ref, u_ref, xg_ref, cw_ref, wrg_ref, brg_ref, wig_ref, big_ref, l_ref,
             dxg_ref, dwrg_ref, dwig_ref, dbrg_ref, dbig_ref, dlam_ref, dcw_ref, dcb_ref,
             hbuf, abuf, dubuf, c_sc, d_sc, lam_sc, lcar, wacc_r, wacc_i):
        i = pl.program_id(0)
        tt = nt - 1 - i

        @pl.when(i == 0)
        def _():
            for r in (wacc_r, wacc_i, dbrg_ref, dbig_ref, dlam_ref, dcw_ref, dcb_ref):
                r[...] = jnp.zeros_like(r)
            abuf[tl:tl + 8, :] = jnp.zeros((8, D_LRU), F32)
            dubuf[tl:tl + 8, :] = jnp.zeros((8, D_LRU), F32)
            lcar[...] = jnp.zeros((8, D_LRU), F32)

        xu0 = xg_ref[:, 0:D_LRU]
        hsv = hs_ref[...]
        uv = u_ref[...]
        hbuf[8:8 + tl, :] = hsv
        hbuf[0:8, :] = jnp.where(tt > 0, hsp_ref[...], 0.0)
        hshift = hbuf[pl.ds(7, tl), :]
        wrg_v = wrg_ref[...]
        wig_v = wig_ref[...]
        lamv = l_ref[...]
        ub, r, ig, sp, a, mult = _lru_gates(uv, wrg_v, brg_ref[...], wig_v, big_ref[...], lamv)
        abuf[0:tl, :] = a
        c_sc[...] = abuf[pl.ds(1, tl), :]
        gel, dgel = _gelu_and_grad(xg_ref[:, D_LRU:2 * D_LRU])
        drv = drec_ref[...]
        d_sc[...] = drv * gel
        dxg_ref[:, D_LRU:2 * D_LRU] = (drv * hsv * dgel).astype(BF16)

        def grp(k, lnext):
            off = pl.multiple_of((tl // 8 - 1 - k) * 8, 8)
            l8 = _rscan8(c_sc[pl.ds(off, 8), :], d_sc[pl.ds(off, 8), :], lnext)
            lam_sc[pl.ds(off, 8), :] = l8
            return l8[0:1, :]

        lcar[0:1, :] = lax.fori_loop(0, tl // 8, grp, lcar[0:1, :])
        abuf[tl:tl + 8, :] = a[0:8, :]
        db = lam_sc[...]
        da = db * hshift
        dmult = db * (ig * uv)
        dig = db * mult * uv
        du = db * mult * ig
        dla = da * a - dmult * (a * a) * _recip(mult)
        dlam_ref[...] += jnp.sum(dla * (-LRU_C) * r, axis=0, keepdims=True)
        dzr = dla * (-LRU_C * sp) * r * (1.0 - r)
        dzi = dig * ig * (1.0 - ig)
        dzrb = dzr.astype(BF16)
        dzib = dzi.astype(BF16)
        du = du + _dot_nt(dzrb, wrg_v) + _dot_nt(dzib, wig_v)
        wacc_r[...] += _dot_tn(ub, dzrb)
        wacc_i[...] += _dot_tn(ub, dzib)
        dbrg_ref[...] += jnp.sum(dzr, axis=0, keepdims=True)
        dbig_ref[...] += jnp.sum(dzi, axis=0, keepdims=True)
        dcb_ref[...] += jnp.sum(du, axis=0, keepdims=True)
        dubuf[0:tl, :] = du
        dxu0 = jnp.zeros((tl, D_LRU), F32)
        for j in range(4):
            dsh = dubuf[pl.ds(3 - j, tl), :]
            dxu0 = dxu0 + cw_ref[j:j + 1, :] * dsh
            dcw_ref[j:j + 1, :] += jnp.sum(xu0 * dsh, axis=0, keepdims=True)
        dubuf[tl:tl + 8, :] = du[0:8, :]
        dxg_ref[:, 0:D_LRU] = dxu0.astype(BF16)

        @pl.when(i == nt - 1)
        def _():
            dlam_ref[...] = dlam_ref[...] * (-_sigmoid(-lamv))
            for n in range(LRU_BLOCKS):
                blk = slice(n * LRU_BLOCK, (n + 1) * LRU_BLOCK)
                dwrg_ref[n] = wacc_r[blk, blk]
                dwig_ref[n] = wacc_i[blk, blk]

    def rev(n):
        return pl.BlockSpec((tl, n), lambda i: (nt - 1 - i, 0))

    prev8 = pl.BlockSpec((8, D_LRU), lambda i: (jnp.maximum((nt - 1 - i) * (tl // 8) - 1, 0), 0))
    vec = _full((1, D_LRU))
    sq = _full((D_LRU, D_LRU))
    blocks_shape = (LRU_BLOCKS, LRU_BLOCK, LRU_BLOCK)
    blocks = _full(blocks_shape)
    return _call(
        body, "b_lru", (nt,),
        [rev(D_LRU), rev(D_LRU), prev8, rev(D_LRU), rev(1024), _full((4, D_LRU)), sq, vec, sq, vec, vec],
        [rev(1024), blocks, blocks, vec, vec, vec, _full((4, D_LRU)), vec],
        [_sds((s_len, 1024), BF16), _sds(blocks_shape, F32), _sds(blocks_shape, F32),
         _sds((1, D_LRU), F32), _sds((1, D_LRU), F32), _sds((1, D_LRU), F32),
         _sds((4, D_LRU), F32), _sds((1, D_LRU), F32)],
        [pltpu.VMEM((tl + 8, D_LRU), F32)] * 3 + [pltpu.VMEM((tl, D_LRU), F32)] * 3
        + [pltpu.VMEM((8, D_LRU), F32)] + [pltpu.VMEM((D_LRU, D_LRU), F32)] * 2,
        (drec, hs, hs, u, xg, conv_w, wrg, brg, wig, big, lam), "arbitrary", comm)


def _b_attn(qkv_pad, att, datt, frow, comm=None):
    s_len = datt.shape[0]
    nb = s_len // QB
    n_pair = ATT_HEADS // 2
    pair_w = 2 * HEAD_DIM

    def body(q_ref, k0, k1, k2, v0, v1, v2, o_ref, do_ref, frow_ref, dq_ref, dkv_ref, dfrow_ref,
             bias_sc, dt_sc, acc_sc):
        t = pl.program_id(0)

        @pl.when(t == 0)
        def _():
            _bias_table(frow_ref, bias_sc)
            dt_sc[...] = jnp.zeros_like(dt_sc)
            acc_sc[...] = jnp.zeros_like(acc_sc)

        @pl.when(t < nb)
        def _():
            var = jnp.minimum(t, N_BIAS - 1)
            even = _even_lanes()
            for hp in range(n_pair):
                cs = slice(hp * pair_w, (hp + 1) * pair_w)
                qt = q_ref[:, cs]
                kts = [k0[:, cs], k1[:, cs], k2[:, cs]]
                vts = [v0[:, cs], v1[:, cs], v2[:, cs]]
                dot = do_ref[:, cs]
                dd = dot * o_ref[:, cs]
                dos_pair, dsbs, pbs, dqs = None, [], [], []
                for e in range(2):
                    keep = even if e == 0 else jnp.logical_not(even)
                    qm = jnp.where(keep, qt, 0)
                    p = _att_probs(qm, kts, bias_sc[var, 2 * hp + e])
                    inv = _recip(jnp.sum(p, axis=-1, keepdims=True))
                    dos = jnp.where(keep, dot * inv, 0.0)
                    delta = jnp.sum(jnp.where(keep, dd, 0.0), axis=-1, keepdims=True) * inv
                    dp = jnp.concatenate([_dot_nt(dos.astype(BF16), v) for v in vts], axis=1)
                    ds = p * (dp - delta)
                    dt_sc[2 * hp + e] += ds
                    dsb = ds.astype(BF16)
                    dq = _dot(dsb[:, 0:QB], kts[0])
                    for j in (1, 2):
                        dq = dq + _dot(dsb[:, j * QB:(j + 1) * QB], kts[j])
                    dqs.append(dq)
                    dsbs.append(dsb)
                    pbs.append(p.astype(BF16))
                    dos_pair = dos if e == 0 else dos_pair + dos
                dq_ref[:, cs] = (jnp.where(even, dqs[0], dqs[1]) * ATT_SCALE).astype(BF16)
                qtt = qt.astype(F32).T.astype(BF16)
                dost = dos_pair.T.astype(BF16)
                for j in range(3):
                    slot = (t + 1 + j) % 3
                    js = slice(j * QB, (j + 1) * QB)
                    for e in range(2):
                        hr = slice(e * HEAD_DIM, (e + 1) * HEAD_DIM)
                        acc_sc[slot, hp, hr, :] += _dot(qtt[hr], dsbs[e][:, js])
                        acc_sc[slot, n_pair + hp, hr, :] += _dot(dost[hr], pbs[e][:, js])

        done = (t + 1) % 3

        @pl.when(t >= 2)
        def _():
            for i in range(2 * n_pair):
                dkv_ref[:, i * pair_w:(i + 1) * pair_w] = acc_sc[done, i].T.astype(BF16)

        acc_sc[done] = jnp.zeros((2 * n_pair, pair_w, QB), F32)

        @pl.when(t == nb + 1)
        def _():
            row = lax.broadcasted_iota(jnp.int32, (8, ROLL_W), 0)
            pad = jnp.zeros((8, ROLL_W - KB), F32)
            for h in range(ATT_HEADS):
                acc8 = jnp.concatenate([dt_sc[h, 0:8, :], pad], axis=1)
                for a1 in range(1, QB // 8):
                    blk = jnp.concatenate([dt_sc[h, 8 * a1:8 * a1 + 8, :], pad], axis=1)
                    acc8 = acc8 + pltpu.roll(blk, ROLL_W - 8 * a1, 1)
                for k in range(3):
                    acc8 = jnp.where(((row >> k) & 1) == 1, pltpu.roll(acc8, ROLL_W - (1 << k), 1), acc8)
                dfrow_ref[h:h + 1, :] = jnp.sum(acc8, axis=0, keepdims=True)

    clamp = lambda t: jnp.minimum(t, nb - 1)
    qrows = pl.BlockSpec((QB, D_ATT), lambda t: (clamp(t), 0))
    return _call(
        body, "b_attn", (nb + 2,),
        _att_in_specs(clamp) + [qrows, qrows, _full((ATT_HEADS, ROLL_W))],
        [qrows, pl.BlockSpec((QB, 2 * D_ATT), lambda t: (jnp.maximum(t - 2, 0), 0)),
         _full((ATT_HEADS, ROLL_W))],
        [_sds((s_len, D_ATT), BF16), _sds((s_len, 2 * D_ATT), BF16), _sds((ATT_HEADS, ROLL_W), F32)],
        [pltpu.VMEM((N_BIAS, ATT_HEADS, QB, KB), F32), pltpu.VMEM((ATT_HEADS, QB, KB), F32),
         pltpu.VMEM((3, 2 * n_pair, pair_w, QB), F32)],
        (*([qkv_pad] * 7), att, datt, frow), "arbitrary", comm)


def _flush_grad(steps, acc, accb, out_hbm, outb_hbm):
    @pl.when(pl.program_id(0) == steps - 1)
    def _():
        accb[...] = acc[...].astype(BF16)
        pltpu.sync_copy(acc, out_hbm)
        pltpu.sync_copy(accb, outb_hbm)


def _b_win(dq, dkv, dxg, h, ts):
    s_len = h.shape[0]
    steps = s_len // ts

    def body(dq_ref, dkv_ref, dxg_ref, h_ref, dw_hbm, dwb_hbm, acc, accb):
        @pl.when(pl.program_id(0) == 0)
        def _():
            acc[...] = jnp.zeros_like(acc)

        dproj = jnp.concatenate([dq_ref[...], dkv_ref[...], dxg_ref[...]], axis=1)
        hv = h_ref[...]
        for s in range(N_SHARD):
            acc[s] += _dot_tn(hv, dproj[:, s * IN_SH:(s + 1) * IN_SH])
        _flush_grad(steps, acc, accb, dw_hbm, dwb_hbm)

    shape = (N_SHARD, 1024, IN_SH)
    return pl.pallas_call(
        body, name="b_win", grid=(steps,),
        in_specs=[_rows(ts, 512), _rows(ts, 1024), _rows(ts, 1024), _rows(ts, 1024)],
        out_specs=[_any()] * 2, out_shape=[_sds(shape, F32), _sds(shape, BF16)],
        scratch_shapes=[pltpu.VMEM(shape, F32), pltpu.VMEM(shape, BF16)],
        compiler_params=_cp("arbitrary"))(dq, dkv, dxg, h)


def _b_inproj(dq, dkv, dxg, x, dx1, g_mix, w_in_g, tm, comm=None):
    s_len = x.shape[0]

    def body(dq_ref, dkv_ref, dxg_ref, x_ref, dx1_ref, g_ref, w_hbm, gx_ref, dgm_ref, w_ref):
        _load_w_in_once(w_hbm, w_ref)

        @pl.when(pl.program_id(0) == 0)
        def _():
            dgm_ref[...] = jnp.zeros_like(dgm_ref)

        dproj = jnp.concatenate([dq_ref[...], dkv_ref[...], dxg_ref[...]], axis=1)
        dh = _dot_nt(dproj, w_ref[...])
        dx, dgm = _rms_bwd(dh, x_ref[...], g_ref[...])
        gx_ref[...] = dx1_ref[...] + dx
        dgm_ref[...] += dgm

    return _call(
        body, "b_inproj", (s_len // tm,),
        [_rows(tm, 512), _rows(tm, 1024), _rows(tm, 1024), _rows(tm, 1024), _rows(tm, 1024),
         _full((1, 1024)), _any()],
        [_rows(tm, 1024), _full((1, 1024))],
        [_sds((s_len, 1024), F32), _sds((1, 1024), F32)],
        [pltpu.VMEM((1024, D_IN), BF16)], (dq, dkv, dxg, x, dx1, g_mix, w_in_g), "arbitrary", comm)


def _mm_tn(xa, ya, name, ts):
    s_len, k = xa.shape
    n = ya.shape[1]

    steps = s_len // ts

    def body(x_ref, y_ref, o_hbm, ob_hbm, acc, accb):
        @pl.when(pl.program_id(0) == 0)
        def _():
            acc[...] = jnp.zeros_like(acc)
        acc[...] += _dot_tn(x_ref[...].astype(BF16), y_ref[...].astype(BF16))
        _flush_grad(steps, acc, accb, o_hbm, ob_hbm)

    return pl.pallas_call(
        body, name=name, grid=(steps,), in_specs=[_rows(ts, k), _rows(ts, n)],
        out_specs=[_any()] * 2, out_shape=[_sds((k, n), F32), _sds((k, n), BF16)],
        scratch_shapes=[pltpu.VMEM((k, n), F32), pltpu.VMEM((k, n), BF16)],
        compiler_params=_cp("arbitrary"))(xa, ya)


def _frow_from_rel_bias(rb):
    hi = jnp.broadcast_to(rb[:, 256:257], (ATT_HEADS, 385))
    mid = rb[:, 1:256][:, ::-1]
    lo = jnp.broadcast_to(rb[:, 0:1], (ATT_HEADS, 128))
    wrap = jnp.broadcast_to(rb[:, 256:257], (ATT_HEADS, ROLL_W - KB))
    return jnp.concatenate([hi, mid, lo, wrap], axis=1)


def _rel_bias_grad_from_dfrow(df):
    g256 = jnp.sum(df[:, 0:385], axis=1, keepdims=True) + jnp.sum(df[:, KB:ROLL_W], axis=1, keepdims=True)
    mid = df[:, 385:640][:, ::-1]
    g0 = jnp.sum(df[:, 640:KB], axis=1, keepdims=True)
    return jnp.concatenate([g0, mid, g256], axis=1)


def _block_diag(w):
    eye = jnp.eye(8, dtype=w.dtype)
    return (w[:, :, None, :] * eye[:, None, :, None]).reshape(D_LRU, D_LRU)


MID = ['w_out', 'wq_c', 'wk_c', 'wv_c', 'wo_c']
TRANSPOSED = ['w_gate', 'w_up']
AG_IN_INPROJ = ['w_out', 'wq_c', 'wk_c']
AG_IN_ATTN = ['wv_c', 'wo_c', 'w_gate']
AG_IN_LRU = ['w_up']
AG_IN_MID = ['w_down']
RS_IN_MID = ['w_gate', 'w_up']
RS_IN_LRU = ['w_down']
RS_IN_ATTN = MID


def _local_step(x, mem, tgt, p, gw, shards=None, chip=None):
    s_len = x.shape[0]
    tm = min(256, s_len)
    tmb = min(512, s_len)
    tl = min(512, s_len)
    frow = _frow_from_rel_bias(p['rel_bias'])
    wrg = _block_diag(p['w_rg']).astype(BF16)
    wig = _block_diag(p['w_ig']).astype(BF16)
    gw = dict(gw)

    big, bigb, recv, part, sib = {}, {}, {}, {}, {}

    def ag(names):
        return [] if shards is None else [("ag", [shards[n] for n in names])]

    def rs(names):
        return [] if shards is None else [("rs", [bigb[n] for n in names])]

    def swap(names):
        return [] if shards is None else [("swap", [part[n] for n in names])]

    def reduce_own(names):
        if shards is not None:
            for n in names:
                part[n] = _sum_parts(big[n], recv[n], chip, "sum_" + n)

    h, qkv_pad, xg, *got = _f_inproj(x, p['g_mix'], gw['w_in'], tmb, ag(AG_IN_INPROJ))
    gw.update(zip(AG_IN_INPROJ, got))
    att, *got = _f_attn(qkv_pad, frow, ag(AG_IN_ATTN))
    gw.update(zip(AG_IN_ATTN, got))
    rec, u, hs, *got = _f_lru(xg, p['conv_w'], p['conv_b'], wrg, p['b_rg'], wig, p['b_ig'], p['lru_L'], tl,
                              ag(AG_IN_LRU))
    gw.update(zip(AG_IN_LRU, got))
    w_out = gw['w_out'].reshape(1024, 1024)
    wq = gw['wq_c'].reshape(1024, 1024)
    wk = gw['wk_c'].reshape(1024, 1024)
    wv = gw['wv_c'].reshape(1024, 1024)
    wo = gw['wo_c'].reshape(1024, 1024)
    mn, kx, vx = _f_mem(mem, p['g_mem'], wk, wv)
    mg, x1, hc, qx, ox, x2, *got = _f_mid(x, att, rec, p['g_out_attn'], p['g_out_lru'], w_out, p['g_cross'],
                                          wq, kx, vx, wo, tmb, ag(AG_IN_MID))
    gw.update(zip(AG_IN_MID, got))
    ffn_w = [gw[n].reshape(D_FF, 1024) for n in ('w_gate', 'w_up', 'w_down')]
    hf, gact, uact, aact, dx3, loss, dg_final = _f_ffn(x2, tgt, p['g_ffn'], p['g_final'], *ffn_w, tmb)

    ts = min(1024, s_len)
    dgact, duact, dx2, dg_ffn = _b_ffn(dx3, x2, gact, uact, p['g_ffn'], *ffn_w, tm)
    big['w_gate'], bigb['w_gate'] = _mm_tn(dgact, hf, "dw_gate", ts)
    big['w_up'], bigb['w_up'] = _mm_tn(duact, hf, "dw_up", ts)
    big['w_down'], bigb['w_down'] = _mm_tn(aact, dx3, "dw_down", ts)
    for n in ('w_gate', 'w_up', 'w_down'):
        big[n] = big[n].reshape(N_SHARD, FF_SH, 1024)
        bigb[n] = bigb[n].reshape(N_SHARD, FF_SH, 1024)

    dqx, dx1, datt, drec, dkx, dvx, dg_cross, dg_oa, dg_ol, *got = _b_mid(
        dx2, qx, x1, att, rec, kx, vx, wo, wq, w_out, p['g_cross'], p['g_out_attn'], p['g_out_lru'], tmb,
        rs(RS_IN_MID))
    recv.update(zip(RS_IN_MID, got))
    reduce_own(RS_IN_MID)
    dwk, dwv, dg_mem, dwkb, dwvb = _b_mem(dkx, dvx, mem, mn, p['g_mem'], wk, wv)
    big['wk_c'], bigb['wk_c'] = dwk, dwkb
    big['wv_c'], bigb['wv_c'] = dwv, dwvb
    big['w_out'], bigb['w_out'] = _mm_tn(mg, dx1, "dw_out", ts)
    big['wq_c'], bigb['wq_c'] = _mm_tn(hc, dqx, "dw_q", ts)
    big['wo_c'], bigb['wo_c'] = _mm_tn(ox, dx2, "dw_o", ts)
    for n in MID:
        big[n] = big[n].reshape(N_SHARD, 256, 1024)
        bigb[n] = bigb[n].reshape(N_SHARD, 256, 1024)

    dxg, dwrg, dwig, dbrg, dbig, dlam, dcw, dcb, *got = _b_lru(
        drec, hs, u, xg, p['conv_w'], wrg, p['b_rg'], wig, p['b_ig'], p['lru_L'], tl,
        rs(RS_IN_LRU) + swap(RS_IN_MID))
    recv.update(zip(RS_IN_LRU, got))
    sib.update(zip(RS_IN_MID, got[len(RS_IN_LRU):]))
    reduce_own(RS_IN_LRU)
    small = {
        'conv_w': dcw, 'conv_b': dcb, 'w_rg': dwrg, 'b_rg': dbrg, 'w_ig': dwig, 'b_ig': dbig, 'lru_L': dlam,
        'g_out_attn': dg_oa, 'g_out_lru': dg_ol, 'g_cross': dg_cross, 'g_mem': dg_mem, 'g_ffn': dg_ffn,
        'g_final': dg_final,
    }
    names = [n for n in SMALL if n in small]
    gather = [] if shards is None else [("ag8", [_pack_small(names, [small[n] for n in names], loss)])]
    dq, dkv, dfrow, *got = _b_attn(qkv_pad, att, datt, frow, rs(RS_IN_ATTN) + swap(RS_IN_LRU) + gather)
    recv.update(zip(RS_IN_ATTN, got))
    sib.update(zip(RS_IN_LRU, got[len(RS_IN_ATTN):]))
    packs = got[-1] if gather else None
    reduce_own(RS_IN_ATTN)
    small['rel_bias'] = _rel_bias_grad_from_dfrow(dfrow)
    big['w_in'], bigb['w_in'] = _b_win(dq, dkv, dxg, h, ts)
    grad_x, small['g_mix'], *got = _b_inproj(dq, dkv, dxg, x, dx1, p['g_mix'], gw['w_in'], tmb,
                                             rs(['w_in']) + swap(RS_IN_ATTN))
    recv.update(zip(['w_in'], got))
    sib.update(zip(RS_IN_ATTN, got[1:]))
    reduce_own(['w_in'])
    return loss, grad_x, small, big, part, sib, packs


def _cast_shards(ws):
    def body(*refs):
        n = len(refs) // 2
        for src, dst in zip(refs[:n], refs[n:]):
            dst[...] = src[...].astype(BF16)

    return pl.pallas_call(body, name="cast_shards", out_shape=[_sds(w.shape, BF16) for w in ws],
                          compiler_params=_cp())(*ws)


def _sum_parts(own4, recv3, chip, name):
    _, r, c = own4.shape
    steps = _ew_steps(r)
    tr = r // steps

    def body(chip_ref, own_ref, rc_ref, o_ref):
        o_ref[...] = ((own_ref[0] + rc_ref[0].astype(F32)) + rc_ref[1].astype(F32)) + rc_ref[2].astype(F32)

    grid_spec = pltpu.PrefetchScalarGridSpec(
        num_scalar_prefetch=1, grid=(steps,),
        in_specs=[pl.BlockSpec((1, tr, c), lambda i, ch: (ch[0], i, 0)),
                  pl.BlockSpec((3, tr, c), lambda i, ch: (0, i, 0))],
        out_specs=pl.BlockSpec((tr, c), lambda i, ch: (i, 0)))
    return pl.pallas_call(body, name=name, grid_spec=grid_spec, out_shape=_sds((r, c), F32),
                          compiler_params=_cp("parallel"))(chip, own4, recv3)


def _adamw_math(w, g, m, v):
    m = ADAM_B1 * m + (1.0 - ADAM_B1) * g
    v = ADAM_B2 * v + (1.0 - ADAM_B2) * (g * g)
    m_hat = m / (1.0 - ADAM_B1 ** ADAM_STEP)
    v_hat = v / (1.0 - ADAM_B2 ** ADAM_STEP)
    delta = -ADAM_LR * (m_hat / (jnp.sqrt(v_hat) + ADAM_EPS) + ADAM_WD * w)
    return delta, m, v


def _final_adamw(pa, pb, w, m, v, name):
    r, c = w.shape
    steps = _ew_steps(r)
    tr = r // steps

    def body(pa_ref, pb_ref, w_ref, m_ref, v_ref, g_ref, d_ref, nm_ref, nv_ref):
        g = pa_ref[...] + pb_ref[...]
        g_ref[...] = g
        d_ref[...], nm_ref[...], nv_ref[...] = _adamw_math(w_ref[...], g, m_ref[...], v_ref[...])

    return pl.pallas_call(
        body, name=name, grid=(steps,), in_specs=[_rows(tr, c)] * 5, out_specs=[_rows(tr, c)] * 4,
        out_shape=[_sds((r, c), F32)] * 4, compiler_params=_cp("parallel"))(pa, pb, w, m, v)


def _pack_put(ref, name, val_ref):
    r = _pack_rows()[name]
    shape = val_ref.shape
    if len(shape) == 3:
        for b in range(shape[0]):
            ref[r:r + shape[1], b * shape[2]:(b + 1) * shape[2]] = val_ref[b]
    elif shape[1] == 2 * PACK_W:
        ref[r:r + 1, :] = val_ref[:, 0:PACK_W]
        ref[r + 1:r + 2, :] = val_ref[:, PACK_W:2 * PACK_W]
    else:
        ref[r:r + shape[0], 0:shape[1]] = val_ref[...]


def _pack_get(ref, name, shape):
    r = _pack_rows()[name]
    if len(shape) == 3:
        return jnp.stack([ref[r:r + shape[1], b * shape[2]:(b + 1) * shape[2]] for b in range(shape[0])])
    if shape[1] == 2 * PACK_W:
        return jnp.concatenate([ref[r:r + 1, :], ref[r + 1:r + 2, :]], axis=1)
    return ref[r:r + shape[0], 0:shape[1]]


def _pack_small(names, g, loss):
    n = len(g)

    def body(*refs):
        pack = refs[n + 1]
        pack[...] = jnp.zeros_like(pack)
        for a, name in enumerate(names):
            _pack_put(pack, name, refs[a])
        _pack_put(pack, 'loss', refs[n])

    return pl.pallas_call(body, name="pack_small", out_shape=_sds((PACK_ROWS, PACK_W), F32),
                          compiler_params=_cp())(*g, loss)


def _all_peers():
    x, y, c = _mesh_pos()
    peers = []
    for k in range(1, 8):
        px = 1 - x if k & 4 else x
        py = 1 - y if k & 2 else y
        pc = 1 - c if k & 1 else c
        peers.append(((px, py, pc), 4 * px + 2 * py + pc))
    return peers, 4 * x + 2 * y + c


def _ag8_copies(ins, outs, sems):
    send_sems, recv_sems, loc_sems = sems
    n = len(ins)
    peers, me = _all_peers()

    def remote(k, j, slot):
        return pltpu.make_async_remote_copy(
            src_ref=ins[k], dst_ref=outs[k].at[slot], send_sem=send_sems.at[k, j], recv_sem=recv_sems.at[k, j],
            device_id=peers[j][0], device_id_type=MESH_ID)

    def local(k):
        return pltpu.make_async_copy(ins[k], outs[k].at[me], loc_sems.at[k])

    def start():
        for k in range(n):
            local(k).start()
            for j in range(7):
                remote(k, j, me).start()

    def wait():
        for k in range(n):
            for j in range(7):
                remote(k, j, peers[j][1]).wait_recv()
        for k in range(n):
            for j in range(7):
                remote(k, j, me).wait_send()
            local(k).wait()

    return start, _no_forward, wait


def _ar_late(names, g):
    n = len(g)

    def body(*refs):
        tot_ref, pack, buf, send_sems, recv_sems = refs[n:]
        peers, me = _all_peers()

        def remote(j, slot):
            return pltpu.make_async_remote_copy(
                src_ref=pack, dst_ref=buf.at[slot], send_sem=send_sems.at[j], recv_sem=recv_sems.at[j],
                device_id=peers[j][0], device_id_type=MESH_ID)

        pack[...] = jnp.zeros_like(pack)
        for a, name in enumerate(names):
            _pack_put(pack, name, refs[a])
        for j in range(7):
            remote(j, me).start()
        buf[me] = pack[...]
        for j in range(7):
            remote(j, peers[j][1]).wait_recv()
        for j in range(7):
            remote(j, me).wait_send()
        tot = buf[0]
        for d in range(1, 8):
            tot = tot + buf[d]
        tot_ref[...] = tot

    return pl.pallas_call(
        body, name="ar_late", out_shape=_sds((LATE_ROWS, PACK_W), F32),
        scratch_shapes=[pltpu.VMEM((LATE_ROWS, PACK_W), F32), pltpu.VMEM((8, LATE_ROWS, PACK_W), F32),
                        pltpu.SemaphoreType.DMA((7,)), pltpu.SemaphoreType.DMA((7,))],
        compiler_params=_cp())(*g)


def _adamw_small(packs, late_tot, g_shapes, loss_shape, w, m, v):
    n = len(w)

    def body(*refs):
        packs_ref, late_ref = refs[0], refs[1]
        w_refs, m_refs, v_refs = (refs[2 + i * n:2 + (i + 1) * n] for i in range(3))
        o0 = 3 * n + 2
        go, do, mo, vo = (refs[o0 + i * n:o0 + (i + 1) * n] for i in range(4))
        loss_out, tot_ref = refs[o0 + 4 * n], refs[o0 + 4 * n + 1]
        x, y, _ = _mesh_pos()
        tot = packs_ref[0]
        for d in range(1, 8):
            tot = tot + packs_ref[d]
        tot_ref[...] = tot
        tot_ref[0:LATE_ROWS, :] += late_ref[...]
        loss_out[...] = _pack_get(tot_ref, 'loss', loss_shape)
        for a, name in enumerate(SMALL):
            if name == 'conv_w':
                r = _pack_rows()[name]
                ga = tot_ref[r:r + g_shapes[a][0], pl.ds(pl.multiple_of((2 * x + y) * 128, 128), 128)]
            else:
                ga = _pack_get(tot_ref, name, g_shapes[a])
            go[a][...] = ga
            do[a][...], mo[a][...], vo[a][...] = _adamw_math(w_refs[a][...], ga, m_refs[a][...], v_refs[a][...])

    out_shape = [_sds(a.shape, F32) for a in w] * 4 + [_sds(loss_shape, F32)]
    return pl.pallas_call(body, name="adamw_small", out_shape=out_shape,
                          scratch_shapes=[pltpu.VMEM((PACK_ROWS, PACK_W), F32)],
                          compiler_params=_cp())(packs, late_tot, *w, *m, *v)


PACK_W = 512
PACK_ROWS = 160
LATE = ['g_mix', 'rel_bias']
LATE_ROWS = 32


def _pack_rows():
    rows, r = {}, 0
    for name in ['g_mix', 'g_cross', 'g_mem', 'g_ffn', 'g_final']:
        rows[name] = r
        r += 2
    for name in ['conv_b', 'b_rg', 'b_ig', 'lru_L', 'g_out_attn', 'g_out_lru']:
        rows[name] = r
        r += 1
    rows['conv_w'] = r
    rows['loss'] = r + 4
    rows['rel_bias'] = 24
    rows['w_rg'] = 32
    rows['w_ig'] = 32 + LRU_BLOCK
    assert r + 5 <= 24 and rows['w_ig'] + LRU_BLOCK == PACK_ROWS
    assert rows['g_mix'] + 2 <= LATE_ROWS and rows['rel_bias'] + 8 <= LATE_ROWS
    return rows


INPUT_NAMES = (['x', 'mem'] + WEIGHTS + ['loss_target'] + ['m_' + n for n in WEIGHTS] + ['v_' + n for n in WEIGHTS])


def kernel(x, mem, g_mix, w_in, rel_bias, conv_w, conv_b, w_rg, b_rg, w_ig, b_ig, lru_L, g_out_attn, g_out_lru, w_out, g_cross, g_mem, wq_c, wk_c, wv_c, wo_c, g_ffn, w_gate, w_up, w_down, g_final, loss_target, m_g_mix, m_w_in, m_rel_bias, m_conv_w, m_conv_b, m_w_rg, m_b_rg, m_w_ig, m_b_ig, m_lru_L, m_g_out_attn, m_g_out_lru, m_w_out, m_g_cross, m_g_mem, m_wq_c, m_wk_c, m_wv_c, m_wo_c, m_g_ffn, m_w_gate, m_w_up, m_w_down, m_g_final, v_g_mix, v_w_in, v_rel_bias, v_conv_w, v_conv_b, v_w_rg, v_b_rg, v_w_ig, v_b_ig, v_lru_L, v_g_out_attn, v_g_out_lru, v_w_out, v_g_cross, v_g_mem, v_wq_c, v_wk_c, v_wv_c, v_wo_c, v_g_ffn, v_w_gate, v_w_up, v_w_down, v_g_final):
    a = dict(zip(INPUT_NAMES, (x, mem, g_mix, w_in, rel_bias, conv_w, conv_b, w_rg, b_rg, w_ig, b_ig, lru_L, g_out_attn, g_out_lru, w_out, g_cross, g_mem, wq_c, wk_c, wv_c, wo_c, g_ffn, w_gate, w_up, w_down, g_final, loss_target, m_g_mix, m_w_in, m_rel_bias, m_conv_w, m_conv_b, m_w_rg, m_b_rg, m_w_ig, m_b_ig, m_lru_L, m_g_out_attn, m_g_out_lru, m_w_out, m_g_cross, m_g_mem, m_wq_c, m_wk_c, m_wv_c, m_wo_c, m_g_ffn, m_w_gate, m_w_up, m_w_down, m_g_final, v_g_mix, v_w_in, v_rel_bias, v_conv_w, v_conv_b, v_w_rg, v_b_rg, v_w_ig, v_b_ig, v_lru_L, v_g_out_attn, v_g_out_lru, v_w_out, v_g_cross, v_g_mem, v_wq_c, v_wk_c, v_wv_c, v_wo_c, v_g_ffn, v_w_gate, v_w_up, v_w_down, v_g_final)))
    chip = 2 * lax.axis_index("x") + lax.axis_index("y")

    def shard(name):
        arr = a[name][0]
        return jnp.swapaxes(arr, 0, 1) if name[2:] in TRANSPOSED or name in TRANSPOSED else arr

    shards = dict(zip(BIG, _cast_shards([shard(n) for n in BIG])))
    w_in_g, conv_w_g = _comm_only("ag_w_in", [("ag", [shards['w_in']]), ("agf", [a['conv_w'][0]])])
    conv_w_full = conv_w_g.transpose(1, 0, 2).reshape(4, D_LRU)

    p = {n: a[n] for n in SMALL}
    p['rel_bias'] = a['rel_bias'][0]
    p['w_rg'] = a['w_rg'][0]
    p['w_ig'] = a['w_ig'][0]
    p['conv_w'] = conv_w_full
    p['g_final'] = a['g_final'][None, :]
    chip_arr = jnp.reshape(chip, (1,)).astype(jnp.int32)
    loss_part, grad_x, small, _, part, sib, packs = _local_step(
        a['x'][0], a['mem'][0], a['loss_target'][0], p, {'w_in': w_in_g}, shards, chip_arr)

    sib['w_in'], = _comm_only("swap_w_in", [("swap", [part['w_in']])])
    out = {}
    for n in BIG:
        res = _final_adamw(part[n], sib[n], shard(n), shard('m_' + n), shard('v_' + n), "adamw_" + n)
        out[n] = [jnp.swapaxes(r, 0, 1) for r in res] if n in TRANSPOSED else res

    def natural(arr):
        return arr[0] if arr.ndim >= 3 else (arr[None, :] if arr.ndim == 1 else arr)

    small_out = _adamw_small(packs, _ar_late(LATE, [small[n] for n in LATE]), [small[n].shape for n in SMALL],
                             loss_part.shape, *[[natural(a[pre + n]) for n in SMALL] for pre in ('', 'm_', 'v_')])
    ns = len(SMALL)
    loss = small_out[4 * ns][0, 0]

    def leaf(i, n):
        if n in BIG:
            return out[n][i][None]
        return small_out[i * ns + SMALL.index(n)].reshape(a[n].shape)

    return (loss, grad_x[None], *[leaf(i, n) for i in range(4) for n in WEIGHTS])
```

```python
import math

import jax
import jax.numpy as jnp
from jax import lax
from jax.experimental import pallas as pl
from jax.experimental.pallas import tpu as pltpu

F32 = jnp.float32
BF16 = jnp.bfloat16

D_MODEL = 1024
D_ATT = 512
D_LRU = 512
HEAD_DIM = 64
ATT_HEADS = 8
CHUNK = 64
LEFT_CHUNKS = 8
X_HEADS = 4
X_HEAD_DIM = 256
N_SHARD = 4
IN_SH = 640
D_IN = N_SHARD * IN_SH
FF_SH = 704
D_FF = N_SHARD * FF_SH
EPS = 1e-6
LRU_C = 8.0
LRU_BLOCKS = 8
LRU_BLOCK = 64
QB = 256
KB = 768
ROLL_W = 1024
NEG = -1e30
ATT_SCALE = HEAD_DIM ** -0.5
X_SCALE = X_HEAD_DIM ** -0.5

ADAM_LR = 0.001
ADAM_B1 = 0.9
ADAM_B2 = 0.999
ADAM_EPS = 1e-08
ADAM_WD = 0.01
ADAM_STEP = 10

VMEM_LIMIT_V7X = 56 * 1024 * 1024
BF16_ROWS = 16


def _ew_steps(rows):
    return max(s for s in (2, 1) if rows % (s * BF16_ROWS) == 0)
MESH_ID = pl.DeviceIdType.MESH

WEIGHTS = ['g_mix', 'w_in', 'rel_bias', 'conv_w', 'conv_b', 'w_rg', 'b_rg', 'w_ig', 'b_ig', 'lru_L',
           'g_out_attn', 'g_out_lru', 'w_out', 'g_cross', 'g_mem', 'wq_c', 'wk_c', 'wv_c', 'wo_c',
           'g_ffn', 'w_gate', 'w_up', 'w_down', 'g_final']
BIG = ['w_in', 'w_out', 'wq_c', 'wk_c', 'wv_c', 'wo_c', 'w_gate', 'w_up', 'w_down']
SMALL = [n for n in WEIGHTS if n not in BIG]


def _sds(shape, dtype):
    return jax.ShapeDtypeStruct(shape, dtype)


def _cp(*sem):
    return pltpu.CompilerParams(dimension_semantics=sem or None, vmem_limit_bytes=VMEM_LIMIT_V7X)


def _rows(tm, n):
    return pl.BlockSpec((tm, n), lambda i: (i, 0))


def _full(shape):
    nd = len(shape)
    return pl.BlockSpec(shape, lambda i: (0,) * nd)


def _dot(a, b):
    return jnp.dot(a, b, preferred_element_type=F32)


def _dot_nt(a, b):
    return lax.dot_general(a, b, (((1,), (1,)), ((), ())), preferred_element_type=F32)


def _dot_tn(a, b):
    return lax.dot_general(a, b, (((0,), (0,)), ((), ())), preferred_element_type=F32)


def _rinv(x):
    return lax.rsqrt(jnp.mean(x * x, axis=-1, keepdims=True) + EPS)


def _rms_bwd(dy, x, g):
    r = _rinv(x)
    yh = x * r
    dyh = dy * g
    dx = r * (dyh - yh * jnp.mean(dyh * yh, axis=-1, keepdims=True))
    return dx, jnp.sum(dy * yh, axis=0, keepdims=True)


def _gelu(x):
    c = math.sqrt(2.0 / math.pi)
    t = jnp.tanh(c * (x + 0.044715 * x * x * x))
    return 0.5 * x * (1.0 + t)


def _gelu_and_grad(x):
    c = math.sqrt(2.0 / math.pi)
    t = jnp.tanh(c * (x + 0.044715 * x * x * x))
    g = 0.5 * x * (1.0 + t)
    dg = 0.5 * (1.0 + t) + 0.5 * x * (1.0 - t * t) * c * (1.0 + 3.0 * 0.044715 * x * x)
    return g, dg


def _neg_expm1(z):
    series = -z * (1.0 + z * (0.5 + z * ((1.0 / 6.0) + z * (1.0 / 24.0))))
    return jnp.where(z > -0.03, series, 1.0 - jnp.exp(z))


def _lru_gates(u, wrg, brg, wig, big, lam):
    ub = u.astype(BF16)
    r = jax.nn.sigmoid(_dot(ub, wrg) + brg)
    ig = jax.nn.sigmoid(_dot(ub, wig) + big)
    sp = jnp.maximum(-lam, 0.0) + jnp.log1p(jnp.exp(-jnp.abs(lam)))
    la = -LRU_C * r * sp
    a = jnp.exp(la)
    mult = jnp.sqrt(jnp.maximum(_neg_expm1(2.0 * la), 0.0))
    return ub, r, ig, sp, a, mult


SCAN_SEGS = 8
LANES = 128


def _scan_scratch(t_len):
    return pltpu.VMEM((D_LRU // LANES, t_len, LANES), F32)


def _to_lane_blocks(ref, val):
    for cb in range(ref.shape[0]):
        ref[cb] = val[:, cb * LANES:(cb + 1) * LANES]


def _seg_scan(coef_sc, val_sc, out_ref, carry, reverse):
    nblk, t_len, lanes = coef_sc.shape
    n = t_len // SCAN_SEGS

    def step(j, state):
        rows = pl.ds((n - 1 - j) if reverse else j, SCAN_SEGS, stride=n)
        new = []
        for cb in range(nblk):
            y, prod = state[2 * cb], state[2 * cb + 1]
            cf = coef_sc[cb, rows, :]
            y = cf * y + val_sc[cb, rows, :]
            prod = cf * prod
            val_sc[cb, rows, :] = y
            coef_sc[cb, rows, :] = prod
            new += [y, prod]
        return tuple(new)

    init = (jnp.zeros((SCAN_SEGS, lanes), F32), jnp.ones((SCAN_SEGS, lanes), F32)) * nblk
    end = lax.fori_loop(0, n, step, init)
    order = range(SCAN_SEGS - 1, -1, -1) if reverse else range(SCAN_SEGS)
    carries = [carry[:, cb * lanes:(cb + 1) * lanes] for cb in range(nblk)]
    for s in order:
        seg = slice(s * n, (s + 1) * n)
        for cb in range(nblk):
            cols = slice(cb * lanes, (cb + 1) * lanes)
            out_ref[seg, cols] = val_sc[cb, seg, :] + coef_sc[cb, seg, :] * carries[cb]
            carries[cb] = end[2 * cb][s:s + 1, :] + end[2 * cb + 1][s:s + 1, :] * carries[cb]
    return jnp.concatenate(carries, axis=1)


def _mesh_pos():
    return lax.axis_index("x"), lax.axis_index("y"), lax.axis_index("c")


def _other_chips(x, y):
    return [(1 - x, y), (x, 1 - y), (1 - x, 1 - y)]


def _no_forward():
    pass


def _ag_full_copies(ins, outs, sems):
    send_sems, recv_sems, loc_sems = sems
    n = len(ins)
    x, y, c = _mesh_pos()
    mine = 2 * x + y
    chips = _other_chips(x, y)

    def remote(k, j, slot):
        px, py = chips[j]
        return pltpu.make_async_remote_copy(
            src_ref=ins[k], dst_ref=outs[k].at[slot], send_sem=send_sems.at[k, j], recv_sem=recv_sems.at[k, j],
            device_id=(px, py, c), device_id_type=MESH_ID)

    def local(k):
        return pltpu.make_async_copy(ins[k], outs[k].at[mine], loc_sems.at[k])

    def start():
        for k in range(n):
            local(k).start()
            for j in range(3):
                remote(k, j, mine).start()

    def wait():
        for k in range(n):
            for j, (px, py) in enumerate(chips):
                remote(k, j, 2 * px + py).wait_recv()
        for k in range(n):
            for j in range(3):
                remote(k, j, mine).wait_send()
            local(k).wait()

    return start, _no_forward, wait


def _ag_copies(ins, outs, sems):
    send_sems, recv_sems, fsend_sems, frecv_sems, loc_sems = sems
    n = len(ins)
    x, y, c = _mesh_pos()
    mine = 2 * x + y
    chips = _other_chips(x, y)

    def half(ref, hc):
        r = ref.shape[0] // 2
        return ref.at[pl.ds(pl.multiple_of(hc * r, 16), r)]

    def ici(k, j, slot):
        px, py = chips[j]
        return pltpu.make_async_remote_copy(
            src_ref=half(ins[k], c), dst_ref=half(outs[k].at[slot], c),
            send_sem=send_sems.at[k, j], recv_sem=recv_sems.at[k, j],
            device_id=(px, py, c), device_id_type=MESH_ID)

    def d2d(k, j, hc):
        px, py = chips[j]
        part = half(outs[k].at[2 * px + py], hc)
        return pltpu.make_async_remote_copy(
            src_ref=part, dst_ref=part, send_sem=fsend_sems.at[k, j], recv_sem=frecv_sems.at[k, j],
            device_id=(x, y, 1 - c), device_id_type=MESH_ID)

    def local(k):
        return pltpu.make_async_copy(ins[k], outs[k].at[mine], loc_sems.at[k])

    def start():
        for k in range(n):
            local(k).start()
            for j in range(3):
                ici(k, j, mine).start()

    def forward():
        for k in range(n):
            for j, (px, py) in enumerate(chips):
                ici(k, j, 2 * px + py).wait_recv()
                d2d(k, j, c).start()

    def wait():
        for k in range(n):
            for j in range(3):
                d2d(k, j, 1 - c).wait_recv()
        for k in range(n):
            for j in range(3):
                d2d(k, j, c).wait_send()
                ici(k, j, mine).wait_send()
            local(k).wait()

    return start, forward, wait


def _rs_copies(ins, outs, sems):
    send_sems, recv_sems = sems
    n = len(ins)
    x, y, c = _mesh_pos()
    chips = _other_chips(x, y)

    def remote(k, j):
        px, py = chips[j]
        return pltpu.make_async_remote_copy(
            src_ref=ins[k].at[2 * px + py], dst_ref=outs[k].at[j],
            send_sem=send_sems.at[k, j], recv_sem=recv_sems.at[k, j],
            device_id=(px, py, c), device_id_type=MESH_ID)

    def start():
        for k in range(n):
            for j in range(3):
                remote(k, j).start()

    def wait():
        for k in range(n):
            for j in range(3):
                remote(k, j).wait_recv()
        for k in range(n):
            for j in range(3):
                remote(k, j).wait_send()

    return start, _no_forward, wait


def _swap_copies(ins, outs, sems):
    send_sems, recv_sems = sems
    x, y, c = _mesh_pos()
    copies = [pltpu.make_async_remote_copy(
        src_ref=ins[k], dst_ref=outs[k], send_sem=send_sems.at[k], recv_sem=recv_sems.at[k],
        device_id=(x, y, 1 - c), device_id_type=MESH_ID) for k in range(len(ins))]

    def start():
        for cp in copies:
            cp.start()

    def wait():
        for cp in copies:
            cp.wait()

    return start, _no_forward, wait


def _comm_plan(groups):
    plan, arrs, shapes, sems = [], [], [], []
    for kind, group in groups:
        k = len(group)
        arrs += group
        per_peer = pltpu.SemaphoreType.DMA((k, 3))
        if kind == "ag":
            shapes += [_sds((N_SHARD,) + w.shape, w.dtype) for w in group]
            gsems = [per_peer] * 4 + [pltpu.SemaphoreType.DMA((k,))]
            maker = _ag_copies
        elif kind == "agf":
            shapes += [_sds((N_SHARD,) + w.shape, w.dtype) for w in group]
            gsems = [per_peer] * 2 + [pltpu.SemaphoreType.DMA((k,))]
            maker = _ag_full_copies
        elif kind == "ag8":
            shapes += [_sds((8,) + g.shape, g.dtype) for g in group]
            gsems = [pltpu.SemaphoreType.DMA((k, 7))] * 2 + [pltpu.SemaphoreType.DMA((k,))]
            maker = _ag8_copies
        elif kind == "rs":
            shapes += [_sds((3,) + g.shape[1:], g.dtype) for g in group]
            gsems = [pltpu.SemaphoreType.DMA((k, 3)), pltpu.SemaphoreType.DMA((k, 3))]
            maker = _rs_copies
        else:
            shapes += [_sds(g.shape, g.dtype) for g in group]
            gsems = [pltpu.SemaphoreType.DMA((k,)), pltpu.SemaphoreType.DMA((k,))]
            maker = _swap_copies
        plan.append((maker, k, len(gsems)))
        sems += gsems
    return plan, arrs, shapes, sems


def _comm_fns(plan, cins, couts, sems):
    fns, a, s = [], 0, 0
    for maker, k, ns in plan:
        fns.append(maker(cins[a:a + k], couts[a:a + k], sems[s:s + ns]))
        a += k
        s += ns

    def start():
        for st, _, _ in fns:
            st()

    def forward():
        for _, fw, _ in fns:
            fw()

    def wait():
        for _, _, wt in fns:
            wt()

    return start, forward, wait


def _call(body, name, grid, in_specs, out_specs, out_shape, scratch, args, sem, comm=None):
    if not comm:
        return pl.pallas_call(body, name=name, grid=grid, in_specs=in_specs, out_specs=out_specs,
                              out_shape=out_shape, scratch_shapes=scratch, compiler_params=_cp(sem))(*args)
    plan, c_arrs, c_shapes, c_sems = _comm_plan(comm)
    k = len(c_arrs)
    n_in, n_out, n_scr = len(in_specs), len(out_specs), len(scratch)
    last = grid[0] - 1
    fwd_step = max(1, (2 * last) // 3)

    def wrapped(*refs):
        ins, cins = refs[:n_in], refs[n_in:n_in + k]
        o0 = n_in + k
        outs, couts = refs[o0:o0 + n_out], refs[o0 + n_out:o0 + n_out + k]
        s0 = o0 + n_out + k
        start, forward, wait = _comm_fns(plan, cins, couts, refs[s0 + n_scr:])
        pl.when(pl.program_id(0) == 0)(start)
        pl.when(pl.program_id(0) == fwd_step)(forward)
        body(*ins, *outs, *refs[s0:s0 + n_scr])
        pl.when(pl.program_id(0) == last)(wait)

    return pl.pallas_call(
        wrapped, name=name, grid=grid, in_specs=list(in_specs) + [_any()] * k,
        out_specs=list(out_specs) + [_any()] * k, out_shape=list(out_shape) + c_shapes,
        scratch_shapes=list(scratch) + c_sems, compiler_params=_cp(sem))(*args, *c_arrs)


def _comm_only(name, comm):
    plan, c_arrs, c_shapes, c_sems = _comm_plan(comm)
    k = len(c_arrs)

    def body(*refs):
        start, forward, wait = _comm_fns(plan, refs[:k], refs[k:2 * k], refs[2 * k:])
        start()
        forward()
        wait()

    return pl.pallas_call(body, name=name, in_specs=[_any()] * k, out_specs=[_any()] * k, out_shape=c_shapes,
                          scratch_shapes=c_sems, compiler_params=_cp())(*c_arrs)


def _any():
    return pl.BlockSpec(memory_space=pl.ANY)


def _load_w_in_once(w_hbm, w_ref):
    @pl.when(pl.program_id(0) == 0)
    def _():
        for s in range(N_SHARD):
            pltpu.sync_copy(w_hbm.at[s], w_ref.at[:, pl.ds(s * IN_SH, IN_SH)])


def _f_inproj(x, g_mix, w_in_g, tm, comm=None):
    s_len = x.shape[0]
    pad_rows = LEFT_CHUNKS * CHUNK
    npad = pad_rows // tm

    def body(x_ref, g_ref, w_hbm, h_ref, qkv_ref, xg_ref, w_ref):
        i = pl.program_id(0)
        _load_w_in_once(w_hbm, w_ref)

        @pl.when(i < npad)
        def _():
            qkv_ref[...] = jnp.zeros_like(qkv_ref)

        @pl.when(i >= npad)
        def _():
            xv = x_ref[...]
            h = (xv * _rinv(xv) * g_ref[...]).astype(BF16)
            h_ref[...] = h
            proj = _dot(h, w_ref[...])
            qkv_ref[:, 0:D_ATT] = (proj[:, 0:D_ATT] * ATT_SCALE).astype(BF16)
            qkv_ref[:, D_ATT:3 * D_ATT] = proj[:, D_ATT:3 * D_ATT].astype(BF16)
            xg_ref[...] = proj[:, 3 * D_ATT:D_IN]

    def tok(n):
        return pl.BlockSpec((tm, n), lambda i: (jnp.maximum(i - npad, 0), 0))

    return _call(
        body, "f_inproj", (s_len // tm + npad,),
        [tok(1024), _full((1, 1024)), _any()],
        [tok(1024), _rows(tm, 1536), tok(1024)],
        [_sds((s_len, 1024), BF16), _sds((s_len + pad_rows, 1536), BF16), _sds((s_len, 1024), F32)],
        [pltpu.VMEM((1024, D_IN), BF16)], (x, g_mix, w_in_g), "arbitrary", comm)


N_BIAS = 3


def _bias_table(frow_ref, bias_sc):
    qa = lax.broadcasted_iota(jnp.int32, (QB, KB), 0) // CHUNK
    kcol = lax.broadcasted_iota(jnp.int32, (QB, KB), 1)
    kb = kcol // CHUNK
    band = jnp.where((kb >= qa) & (kb - qa <= LEFT_CHUNKS), 0.0, NEG).astype(F32)
    for h in range(ATT_HEADS):
        row = jnp.broadcast_to(frow_ref[h:h + 1, :], (QB, ROLL_W))
        toep = pltpu.roll(row, 0, 1, stride=1, stride_axis=0)
        gen = toep[:, 0:KB] + band
        bias_sc[N_BIAS - 1, h] = gen
        for v in range(N_BIAS - 1):
            pad_keys = LEFT_CHUNKS * CHUNK - v * QB
            bias_sc[v, h] = gen + jnp.where(kcol < pad_keys, NEG, 0.0).astype(F32)


def _even_lanes():
    return lax.broadcasted_iota(jnp.int32, (1, 2 * HEAD_DIM), 1) < HEAD_DIM


def _att_probs(qm, kts, bias):
    s = jnp.concatenate([_dot_nt(qm, k) for k in kts], axis=1) + bias
    return jnp.exp(s - jnp.max(s, axis=-1, keepdims=True))


def _att_in_specs(clamp):
    def spec(j, col):
        return pl.BlockSpec((QB, D_ATT), lambda i: (clamp(i) + j, col))
    return [spec(2, 0), spec(0, 1), spec(1, 1), spec(2, 1), spec(0, 2), spec(1, 2), spec(2, 2)]


def _f_attn(qkv_pad, frow, comm=None):
    s_len = qkv_pad.shape[0] - LEFT_CHUNKS * CHUNK
    nb = s_len // QB

    def body(q_ref, k0, k1, k2, v0, v1, v2, frow_ref, o_ref, bias_sc):
        i = pl.program_id(0)

        @pl.when(i == 0)
        def _():
            _bias_table(frow_ref, bias_sc)

        var = jnp.minimum(i, N_BIAS - 1)
        even = _even_lanes()
        for hp in range(ATT_HEADS // 2):
            cs = slice(hp * 2 * HEAD_DIM, (hp + 1) * 2 * HEAD_DIM)
            qt = q_ref[:, cs]
            kts = [k0[:, cs], k1[:, cs], k2[:, cs]]
            vts = [v0[:, cs], v1[:, cs], v2[:, cs]]
            res = []
            for e in range(2):
                keep = even if e == 0 else jnp.logical_not(even)
                pb = _att_probs(jnp.where(keep, qt, 0), kts, bias_sc[var, 2 * hp + e]).astype(BF16)
                r = _dot(pb[:, 0:QB], jnp.where(keep, vts[0], 1))
                for j in (1, 2):
                    r = r + _dot(pb[:, j * QB:(j + 1) * QB], jnp.where(keep, vts[j], 1))
                res.append(r / pltpu.roll(r, HEAD_DIM, 1))
            o_ref[:, cs] = jnp.where(even, res[0], res[1])

    return _call(
        body, "f_attn", (nb,),
        _att_in_specs(lambda i: i) + [_full((ATT_HEADS, ROLL_W))],
        [_rows(QB, D_ATT)], [_sds((s_len, D_ATT), F32)],
        [pltpu.VMEM((N_BIAS, ATT_HEADS, QB, KB), F32)], (*([qkv_pad] * 7), frow), "arbitrary", comm)


def _f_lru(xg, conv_w, conv_b, wrg, brg, wig, big, lam, tl, comm=None):
    s_len = xg.shape[0]

    def body(xg_ref, cw_ref, cb_ref, wrg_ref, brg_ref, wig_ref, big_ref, l_ref,
             rec_ref, u_ref, hs_ref, xbuf, a_sc, b_sc, hcar):
        i = pl.program_id(0)

        @pl.when(i == 0)
        def _():
            xbuf[0:8, :] = jnp.zeros((8, D_LRU), F32)
            hcar[...] = jnp.zeros((8, D_LRU), F32)

        xu0 = xg_ref[:, 0:D_LRU]
        xbuf[8:8 + tl, :] = xu0
        u = cb_ref[...] + cw_ref[0:1, :] * xbuf[pl.ds(5, tl), :]
        for j in range(1, 4):
            u = u + cw_ref[j:j + 1, :] * xbuf[pl.ds(5 + j, tl), :]
        xbuf[0:8, :] = xu0[tl - 8:tl, :]
        u_ref[...] = u
        _, _, ig, _, a, mult = _lru_gates(u, wrg_ref[...], brg_ref[...], wig_ref[...], big_ref[...], l_ref[...])
        _to_lane_blocks(a_sc, a)
        _to_lane_blocks(b_sc, mult * (ig * u))

        hcar[0:1, :] = _seg_scan(a_sc, b_sc, hs_ref, hcar[0:1, :], reverse=False)
        rec_ref[...] = hs_ref[...] * _gelu(xg_ref[:, D_LRU:2 * D_LRU])

    vec = _full((1, D_LRU))
    return _call(
        body, "f_lru", (s_len // tl,),
        [_rows(tl, 1024), _full((4, D_LRU)), vec, _full((D_LRU, D_LRU)), vec, _full((D_LRU, D_LRU)), vec, vec],
        [_rows(tl, D_LRU)] * 3, [_sds((s_len, D_LRU), F32)] * 3,
        [pltpu.VMEM((tl + 8, D_LRU), F32), _scan_scratch(tl), _scan_scratch(tl), pltpu.VMEM((8, D_LRU), F32)],
        (xg, conv_w, conv_b, wrg, brg, wig, big, lam), "arbitrary", comm)


def _f_mem(mem, g_mem, wk, wv):
    def body(mem_ref, g_ref, wk_ref, wv_ref, mn_ref, kx_ref, vx_ref):
        mv = mem_ref[...]
        mn = (mv * _rinv(mv) * g_ref[...]).astype(BF16)
        mn_ref[...] = mn
        kx_ref[...] = _dot(mn, wk_ref[...]).astype(BF16)
        vx_ref[...] = _dot(mn, wv_ref[...]).astype(BF16)

    m = mem.shape[0]
    return pl.pallas_call(
        body, name="f_mem", out_shape=[_sds((m, 1024), BF16)] * 3,
        compiler_params=_cp())(mem, g_mem, wk, wv)


def _xattn_probs(q, k):
    s = _dot_nt(q, k) * X_SCALE
    m = jnp.max(s, axis=-1, keepdims=True)
    p = jnp.exp(s - m)
    return p, jnp.sum(p, axis=-1, keepdims=True)


def _f_mid(x, att, rec, g_oa, g_ol, w_out, g_cross, wq, kx, vx, wo, tm, comm=None):
    s_len = x.shape[0]
    m_len = kx.shape[0]

    def body(x_ref, att_ref, rec_ref, goa_ref, gol_ref, wout_ref, gc_ref, wq_ref, kx_ref, vx_ref, wo_ref,
             mg_ref, x1_ref, hc_ref, qx_ref, ox_ref, x2_ref):
        av = att_ref[...]
        rv = rec_ref[...]
        mg_ref[:, 0:D_ATT] = (av * _rinv(av) * goa_ref[...]).astype(BF16)
        mg_ref[:, D_ATT:1024] = (rv * _rinv(rv) * gol_ref[...]).astype(BF16)
        x1 = x_ref[...] + _dot(mg_ref[...], wout_ref[...])
        x1_ref[...] = x1
        hc = (x1 * _rinv(x1) * gc_ref[...]).astype(BF16)
        hc_ref[...] = hc
        qx_ref[...] = _dot(hc, wq_ref[...]).astype(BF16)
        for h in range(X_HEADS):
            sl = slice(h * X_HEAD_DIM, (h + 1) * X_HEAD_DIM)
            p, l = _xattn_probs(qx_ref[:, sl], kx_ref[:, sl])
            ox_ref[:, sl] = (_dot(p.astype(BF16), vx_ref[:, sl]) / l).astype(BF16)
        x2_ref[...] = x1 + _dot(ox_ref[...], wo_ref[...])

    sq = _full((1024, 1024))
    return _call(
        body, "f_mid", (s_len // tm,),
        [_rows(tm, 1024), _rows(tm, 512), _rows(tm, 512), _full((1, 512)), _full((1, 512)), sq,
         _full((1, 1024)), sq, _full((m_len, 1024)), _full((m_len, 1024)), sq],
        [_rows(tm, 1024)] * 6,
        [_sds((s_len, 1024), BF16), _sds((s_len, 1024), F32), _sds((s_len, 1024), BF16),
         _sds((s_len, 1024), BF16), _sds((s_len, 1024), BF16), _sds((s_len, 1024), F32)],
        [], (x, att, rec, g_oa, g_ol, w_out, g_cross, wq, kx, vx, wo), "arbitrary", comm)


def _load_weights_once(pairs):
    @pl.when(pl.program_id(0) == 0)
    def _():
        for hbm, vmem in pairs:
            pltpu.sync_copy(hbm, vmem)


FF_CHUNKS = [(0, 1280), (1280, D_FF)]


def _f_ffn(x2, tgt, g_ffn, g_final, wg, wu, wd, tm):
    s_len = x2.shape[0]

    def body(x2_ref, t_ref, gf_ref, gfin_ref, wg_hbm, wu_hbm, wd_hbm,
             hf_ref, g_ref, u_ref, a_ref, dx3_ref, loss_ref, dgfin_ref, wg_ref, wu_ref, wd_ref):
        _load_weights_once([(wg_hbm, wg_ref), (wu_hbm, wu_ref), (wd_hbm, wd_ref)])

        @pl.when(pl.program_id(0) == 0)
        def _():
            loss_ref[...] = jnp.zeros_like(loss_ref)
            dgfin_ref[...] = jnp.zeros_like(dgfin_ref)

        x2v = x2_ref[...]
        hf = (x2v * _rinv(x2v) * gf_ref[...]).astype(BF16)
        hf_ref[...] = hf
        x3 = x2v
        for c0, c1 in FF_CHUNKS:
            gv = _dot_nt(hf, wg_ref[c0:c1, :])
            uv = _dot_nt(hf, wu_ref[c0:c1, :])
            av = (gv * jax.nn.sigmoid(gv) * uv).astype(BF16)
            g_ref[:, c0:c1] = gv.astype(BF16)
            u_ref[:, c0:c1] = uv.astype(BF16)
            a_ref[:, c0:c1] = av
            x3 = x3 + _dot(av, wd_ref[c0:c1, :])
        r3 = _rinv(x3)
        yh = x3 * r3
        gfin = gfin_ref[...]
        err = yh * gfin - t_ref[...]
        loss_ref[...] += jnp.full((1, 128), 0.5 / D_MODEL, F32) * jnp.sum(err * err)
        dy = err * (1.0 / D_MODEL)
        dgfin_ref[...] += jnp.sum(dy * yh, axis=0, keepdims=True)
        dyh = dy * gfin
        dx3_ref[...] = r3 * (dyh - yh * jnp.mean(dyh * yh, axis=-1, keepdims=True))

    vec = _full((1, 1024))
    return pl.pallas_call(
        body, name="f_ffn", grid=(s_len // tm,),
        in_specs=[_rows(tm, 1024), _rows(tm, 1024), vec, vec, _any(), _any(), _any()],
        out_specs=[_rows(tm, 1024), _rows(tm, D_FF), _rows(tm, D_FF), _rows(tm, D_FF),
                   _rows(tm, 1024), _full((1, 128)), vec],
        out_shape=[_sds((s_len, 1024), BF16)] + [_sds((s_len, D_FF), BF16)] * 3
                  + [_sds((s_len, 1024), F32), _sds((1, 128), F32), _sds((1, 1024), F32)],
        scratch_shapes=[pltpu.VMEM((D_FF, 1024), BF16)] * 3,
        compiler_params=_cp("arbitrary"))(x2, tgt, g_ffn, g_final, wg, wu, wd)


def _b_ffn(dx3, x2, gact, uact, g_ffn, wg, wu, wd, tm):
    s_len = x2.shape[0]

    def body(dx3_ref, x2_ref, g_ref, u_ref, gf_ref, wg_hbm, wu_hbm, wd_hbm,
             dg_ref, du_ref, dx2_ref, dgf_ref, wg_ref, wu_ref, wd_ref):
        _load_weights_once([(wg_hbm, wg_ref), (wu_hbm, wu_ref), (wd_hbm, wd_ref)])

        @pl.when(pl.program_id(0) == 0)
        def _():
            dgf_ref[...] = jnp.zeros_like(dgf_ref)

        dx3v = dx3_ref[...]
        dx3b = dx3v.astype(BF16)
        dhf = jnp.zeros(dx3v.shape, F32)
        for c0, c1 in FF_CHUNKS:
            da = _dot_nt(dx3b, wd_ref[c0:c1, :])
            gv = g_ref[:, c0:c1].astype(F32)
            uv = u_ref[:, c0:c1].astype(F32)
            sg = jax.nn.sigmoid(gv)
            dub = (da * gv * sg).astype(BF16)
            dgb = (da * uv * (sg * (1.0 + gv * (1.0 - sg)))).astype(BF16)
            du_ref[:, c0:c1] = dub
            dg_ref[:, c0:c1] = dgb
            dhf = dhf + _dot(dgb, wg_ref[c0:c1, :]) + _dot(dub, wu_ref[c0:c1, :])
        dx, dgf = _rms_bwd(dhf, x2_ref[...], gf_ref[...])
        dx2_ref[...] = dx3v + dx
        dgf_ref[...] += dgf

    vec = _full((1, 1024))
    return pl.pallas_call(
        body, name="b_ffn", grid=(s_len // tm,),
        in_specs=[_rows(tm, 1024), _rows(tm, 1024), _rows(tm, D_FF), _rows(tm, D_FF), vec,
                  _any(), _any(), _any()],
        out_specs=[_rows(tm, D_FF), _rows(tm, D_FF), _rows(tm, 1024), vec],
        out_shape=[_sds((s_len, D_FF), BF16)] * 2 + [_sds((s_len, 1024), F32), _sds((1, 1024), F32)],
        scratch_shapes=[pltpu.VMEM((D_FF, 1024), BF16)] * 3,
        compiler_params=_cp("arbitrary"))(dx3, x2, gact, uact, g_ffn, wg, wu, wd)


def _b_mid(dx2, qx, x1, att, rec, kx, vx, wo, wq, w_out, g_cross, g_oa, g_ol, tm, comm=None):
    s_len = x1.shape[0]
    m_len = kx.shape[0]

    def body(dx2_ref, qx_ref, x1_ref, att_ref, rec_ref, kx_ref, vx_ref, wo_ref, wq_ref, wout_ref,
             gc_ref, goa_ref, gol_ref,
             dqx_ref, dx1_ref, datt_ref, drec_ref, dkx_ref, dvx_ref, dgc_ref, dgoa_ref, dgol_ref):
        @pl.when(pl.program_id(0) == 0)
        def _():
            for r in (dkx_ref, dvx_ref, dgc_ref, dgoa_ref, dgol_ref):
                r[...] = jnp.zeros_like(r)

        dx2v = dx2_ref[...]
        dox = _dot_nt(dx2v.astype(BF16), wo_ref[...])
        for h in range(X_HEADS):
            sl = slice(h * X_HEAD_DIM, (h + 1) * X_HEAD_DIM)
            q = qx_ref[:, sl]
            p, l = _xattn_probs(q, kx_ref[:, sl])
            pn = p * (1.0 / l)
            dob = dox[:, sl].astype(BF16)
            dp = _dot_nt(dob, vx_ref[:, sl])
            dvx_ref[:, sl] += _dot_tn(pn.astype(BF16), dob)
            ds = pn * (dp - jnp.sum(dp * pn, axis=-1, keepdims=True))
            dsb = (ds * X_SCALE).astype(BF16)
            dqx_ref[:, sl] = _dot(dsb, kx_ref[:, sl]).astype(BF16)
            dkx_ref[:, sl] += _dot_tn(dsb, q)
        dhc = _dot_nt(dqx_ref[...], wq_ref[...])
        dx, dgc = _rms_bwd(dhc, x1_ref[...], gc_ref[...])
        dx1 = dx2v + dx
        dx1_ref[...] = dx1
        dgc_ref[...] += dgc
        dmg = _dot_nt(dx1.astype(BF16), wout_ref[...])
        da, dgoa = _rms_bwd(dmg[:, 0:D_ATT], att_ref[...], goa_ref[...])
        datt_ref[...] = da
        dgoa_ref[...] += dgoa
        dr, dgol = _rms_bwd(dmg[:, D_ATT:1024], rec_ref[...], gol_ref[...])
        drec_ref[...] = dr
        dgol_ref[...] += dgol

    sq = _full((1024, 1024))
    mk = _full((m_len, 1024))
    return _call(
        body, "b_mid", (s_len // tm,),
        [_rows(tm, 1024), _rows(tm, 1024), _rows(tm, 1024), _rows(tm, 512), _rows(tm, 512), mk, mk,
         sq, sq, sq, _full((1, 1024)), _full((1, 512)), _full((1, 512))],
        [_rows(tm, 1024), _rows(tm, 1024), _rows(tm, 512), _rows(tm, 512), mk, mk,
         _full((1, 1024)), _full((1, 512)), _full((1, 512))],
        [_sds((s_len, 1024), BF16), _sds((s_len, 1024), F32), _sds((s_len, 512), F32),
         _sds((s_len, 512), F32), _sds((m_len, 1024), F32), _sds((m_len, 1024), F32),
         _sds((1, 1024), F32), _sds((1, 512), F32), _sds((1, 512), F32)],
        [], (dx2, qx, x1, att, rec, kx, vx, wo, wq, w_out, g_cross, g_oa, g_ol), "arbitrary", comm)


def _b_mem(dkx, dvx, mem, mn, g_mem, wk, wv):
    def body(dkx_ref, dvx_ref, mem_ref, mn_ref, g_ref, wk_ref, wv_ref, dwk_ref, dwv_ref, dgm_ref,
             dwkb_ref, dwvb_ref):
        dkb = dkx_ref[...].astype(BF16)
        dvb = dvx_ref[...].astype(BF16)
        dwk = _dot_tn(mn_ref[...], dkb)
        dwv = _dot_tn(mn_ref[...], dvb)
        dwk_ref[...] = dwk
        dwv_ref[...] = dwv
        dwkb_ref[...] = dwk.astype(BF16)
        dwvb_ref[...] = dwv.astype(BF16)
        dmn = _dot_nt(dkb, wk_ref[...]) + _dot_nt(dvb, wv_ref[...])
        mv = mem_ref[...]
        dgm_ref[...] = jnp.sum(dmn * (mv * _rinv(mv)), axis=0, keepdims=True)

    return pl.pallas_call(
        body, name="b_mem",
        out_shape=[_sds((1024, 1024), F32), _sds((1024, 1024), F32), _sds((1, 1024), F32),
                   _sds((1024, 1024), BF16), _sds((1024, 1024), BF16)],
        compiler_params=_cp())(dkx, dvx, mem, mn, g_mem, wk, wv)


def _b_lru(drec, hs, u, xg, conv_w, wrg, brg, wig, big, lam, tl, comm=None):
    s_len = xg.shape[0]
    nt = s_len // tl

    def body(drec_ref, hs_ref, hsp_ref, u_ref, xg_ref, cw_ref, wrg_ref, brg_ref, wig_ref, big_ref, l_ref,
             dxg_ref, dwrg_ref, dwig_ref, dbrg_ref, dbig_ref, dlam_ref, dcw_ref, dcb_ref,
             hbuf, abuf, dubuf, c_sc, d_sc, lam_sc, lcar, wacc_r, wacc_i):
        i = pl.program_id(0)
        tt = nt - 1 - i

        @pl.when(i == 0)
        def _():
            for r in (wacc_r, wacc_i, dbrg_ref, dbig_ref, dlam_ref, dcw_ref, dcb_ref):
                r[...] = jnp.zeros_like(r)
            abuf[tl:tl + 8, :] = jnp.zeros((8, D_LRU), F32)
            dubuf[tl:tl + 8, :] = jnp.zeros((8, D_LRU), F32)
            lcar[...] = jnp.zeros((8, D_LRU), F32)

        xu0 = xg_ref[:, 0:D_LRU]
        hsv = hs_ref[...]
        uv = u_ref[...]
        hbuf[8:8 + tl, :] = hsv
        hbuf[0:8, :] = jnp.where(tt > 0, hsp_ref[...], 0.0)
        hshift = hbuf[pl.ds(7, tl), :]
        wrg_v = wrg_ref[...]
        wig_v = wig_ref[...]
        lamv = l_ref[...]
        ub, r, ig, sp, a, mult = _lru_gates(uv, wrg_v, brg_ref[...], wig_v, big_ref[...], lamv)
        abuf[0:tl, :] = a
        _to_lane_blocks(c_sc, abuf[pl.ds(1, tl), :])
        gel, dgel = _gelu_and_grad(xg_ref[:, D_LRU:2 * D_LRU])
        drv = drec_ref[...]
        _to_lane_blocks(d_sc, drv * gel)
        dxg_ref[:, D_LRU:2 * D_LRU] = (drv * hsv * dgel).astype(BF16)

        lcar[0:1, :] = _seg_scan(c_sc, d_sc, lam_sc, lcar[0:1, :], reverse=True)
        abuf[tl:tl + 8, :] = a[0:8, :]
        db = lam_sc[...]
        da = db * hshift
        dmult = db * (ig * uv)
        dig = db * mult * uv
        du = db * mult * ig
        dla = da * a - dmult * (a * a) / mult
        dlam_ref[...] += jnp.sum(dla * (-LRU_C) * r, axis=0, keepdims=True)
        dzr = dla * (-LRU_C * sp) * r * (1.0 - r)
        dzi = dig * ig * (1.0 - ig)
        dzrb = dzr.astype(BF16)
        dzib = dzi.astype(BF16)
        du = du + _dot_nt(dzrb, wrg_v) + _dot_nt(dzib, wig_v)
        wacc_r[...] += _dot_tn(ub, dzrb)
        wacc_i[...] += _dot_tn(ub, dzib)
        dbrg_ref[...] += jnp.sum(dzr, axis=0, keepdims=True)
        dbig_ref[...] += jnp.sum(dzi, axis=0, keepdims=True)
        dcb_ref[...] += jnp.sum(du, axis=0, keepdims=True)
        dubuf[0:tl, :] = du
        dxu0 = jnp.zeros((tl, D_LRU), F32)
        for j in range(4):
            dsh = dubuf[pl.ds(3 - j, tl), :]
            dxu0 = dxu0 + cw_ref[j:j + 1, :] * dsh
            dcw_ref[j:j + 1, :] += jnp.sum(xu0 * dsh, axis=0, keepdims=True)
        dubuf[tl:tl + 8, :] = du[0:8, :]
        dxg_ref[:, 0:D_LRU] = dxu0.astype(BF16)

        @pl.when(i == nt - 1)
        def _():
            dlam_ref[...] = dlam_ref[...] * (-jax.nn.sigmoid(-lamv))
            for n in range(LRU_BLOCKS):
                blk = slice(n * LRU_BLOCK, (n + 1) * LRU_BLOCK)
                dwrg_ref[n] = wacc_r[blk, blk]
                dwig_ref[n] = wacc_i[blk, blk]

    def rev(n):
        return pl.BlockSpec((tl, n), lambda i: (nt - 1 - i, 0))

    prev8 = pl.BlockSpec((8, D_LRU), lambda i: (jnp.maximum((nt - 1 - i) * (tl // 8) - 1, 0), 0))
    vec = _full((1, D_LRU))
    sq = _full((D_LRU, D_LRU))
    blocks_shape = (LRU_BLOCKS, LRU_BLOCK, LRU_BLOCK)
    blocks = _full(blocks_shape)
    return _call(
        body, "b_lru", (nt,),
        [rev(D_LRU), rev(D_LRU), prev8, rev(D_LRU), rev(1024), _full((4, D_LRU)), sq, vec, sq, vec, vec],
        [rev(1024), blocks, blocks, vec, vec, vec, _full((4, D_LRU)), vec],
        [_sds((s_len, 1024), BF16), _sds(blocks_shape, F32), _sds(blocks_shape, F32),
         _sds((1, D_LRU), F32), _sds((1, D_LRU), F32), _sds((1, D_LRU), F32),
         _sds((4, D_LRU), F32), _sds((1, D_LRU), F32)],
        [pltpu.VMEM((tl + 8, D_LRU), F32)] * 3 + [_scan_scratch(tl)] * 2 + [pltpu.VMEM((tl, D_LRU), F32)]
        + [pltpu.VMEM((8, D_LRU), F32)] + [pltpu.VMEM((D_LRU, D_LRU), F32)] * 2,
        (drec, hs, hs, u, xg, conv_w, wrg, brg, wig, big, lam), "arbitrary", comm)


def _b_attn(qkv_pad, att, datt, frow, comm=None):
    s_len = datt.shape[0]
    nb = s_len // QB
    n_pair = ATT_HEADS // 2
    pair_w = 2 * HEAD_DIM

    def body(q_ref, k0, k1, k2, v0, v1, v2, o_ref, do_ref, frow_ref, dq_ref, dkv_ref, dfrow_ref,
             bias_sc, dt_sc, acc_sc):
        t = pl.program_id(0)

        @pl.when(t == 0)
        def _():
            _bias_table(frow_ref, bias_sc)
            dt_sc[...] = jnp.zeros_like(dt_sc)
            acc_sc[...] = jnp.zeros_like(acc_sc)

        @pl.when(t < nb)
        def _():
            var = jnp.minimum(t, N_BIAS - 1)
            even = _even_lanes()
            for hp in range(n_pair):
                cs = slice(hp * pair_w, (hp + 1) * pair_w)
                qt = q_ref[:, cs]
                kts = [k0[:, cs], k1[:, cs], k2[:, cs]]
                vts = [v0[:, cs], v1[:, cs], v2[:, cs]]
                dot = do_ref[:, cs]
                dd = dot * o_ref[:, cs]
                dos_pair, dsbs, pbs, dqs = None, [], [], []
                for e in range(2):
                    keep = even if e == 0 else jnp.logical_not(even)
                    qm = jnp.where(keep, qt, 0)
                    p = _att_probs(qm, kts, bias_sc[var, 2 * hp + e])
                    inv = 1.0 / jnp.sum(p, axis=-1, keepdims=True)
                    dos = jnp.where(keep, dot * inv, 0.0)
                    delta = jnp.sum(jnp.where(keep, dd, 0.0), axis=-1, keepdims=True) * inv
                    dp = jnp.concatenate([_dot_nt(dos.astype(BF16), v) for v in vts], axis=1)
                    ds = p * (dp - delta)
                    dt_sc[2 * hp + e] += ds
                    dsb = ds.astype(BF16)
                    dq = _dot(dsb[:, 0:QB], kts[0])
                    for j in (1, 2):
                        dq = dq + _dot(dsb[:, j * QB:(j + 1) * QB], kts[j])
                    dqs.append(dq)
                    dsbs.append(dsb)
                    pbs.append(p.astype(BF16))
                    dos_pair = dos if e == 0 else dos_pair + dos
                dq_ref[:, cs] = (jnp.where(even, dqs[0], dqs[1]) * ATT_SCALE).astype(BF16)
                qtt = qt.astype(F32).T.astype(BF16)
                dost = dos_pair.T.astype(BF16)
                for j in range(3):
                    slot = (t + 1 + j) % 3
                    js = slice(j * QB, (j + 1) * QB)
                    for e in range(2):
                        hr = slice(e * HEAD_DIM, (e + 1) * HEAD_DIM)
                        acc_sc[slot, hp, hr, :] += _dot(qtt[hr], dsbs[e][:, js])
                        acc_sc[slot, n_pair + hp, hr, :] += _dot(dost[hr], pbs[e][:, js])

        done = (t + 1) % 3

        @pl.when(t >= 2)
        def _():
            for i in range(2 * n_pair):
                dkv_ref[:, i * pair_w:(i + 1) * pair_w] = acc_sc[done, i].T.astype(BF16)

        acc_sc[done] = jnp.zeros((2 * n_pair, pair_w, QB), F32)

        @pl.when(t == nb + 1)
        def _():
            row = lax.broadcasted_iota(jnp.int32, (8, ROLL_W), 0)
            pad = jnp.zeros((8, ROLL_W - KB), F32)
            for h in range(ATT_HEADS):
                acc8 = jnp.concatenate([dt_sc[h, 0:8, :], pad], axis=1)
                for a1 in range(1, QB // 8):
                    blk = jnp.concatenate([dt_sc[h, 8 * a1:8 * a1 + 8, :], pad], axis=1)
                    acc8 = acc8 + pltpu.roll(blk, ROLL_W - 8 * a1, 1)
                for k in range(3):
                    acc8 = jnp.where(((row >> k) & 1) == 1, pltpu.roll(acc8, ROLL_W - (1 << k), 1), acc8)
                dfrow_ref[h:h + 1, :] = jnp.sum(acc8, axis=0, keepdims=True)

    clamp = lambda t: jnp.minimum(t, nb - 1)
    qrows = pl.BlockSpec((QB, D_ATT), lambda t: (clamp(t), 0))
    return _call(
        body, "b_attn", (nb + 2,),
        _att_in_specs(clamp) + [qrows, qrows, _full((ATT_HEADS, ROLL_W))],
        [qrows, pl.BlockSpec((QB, 2 * D_ATT), lambda t: (jnp.maximum(t - 2, 0), 0)),
         _full((ATT_HEADS, ROLL_W))],
        [_sds((s_len, D_ATT), BF16), _sds((s_len, 2 * D_ATT), BF16), _sds((ATT_HEADS, ROLL_W), F32)],
        [pltpu.VMEM((N_BIAS, ATT_HEADS, QB, KB), F32), pltpu.VMEM((ATT_HEADS, QB, KB), F32),
         pltpu.VMEM((3, 2 * n_pair, pair_w, QB), F32)],
        (*([qkv_pad] * 7), att, datt, frow), "arbitrary", comm)


def _flush_grad(steps, acc, accb, out_hbm, outb_hbm):
    @pl.when(pl.program_id(0) == steps - 1)
    def _():
        accb[...] = acc[...].astype(BF16)
        pltpu.sync_copy(acc, out_hbm)
        pltpu.sync_copy(accb, outb_hbm)


def _b_win(dq, dkv, dxg, h, ts):
    s_len = h.shape[0]
    steps = s_len // ts

    def body(dq_ref, dkv_ref, dxg_ref, h_ref, dw_hbm, dwb_hbm, acc, accb):
        @pl.when(pl.program_id(0) == 0)
        def _():
            acc[...] = jnp.zeros_like(acc)

        dproj = jnp.concatenate([dq_ref[...], dkv_ref[...], dxg_ref[...]], axis=1)
        hv = h_ref[...]
        for s in range(N_SHARD):
            acc[s] += _dot_tn(hv, dproj[:, s * IN_SH:(s + 1) * IN_SH])
        _flush_grad(steps, acc, accb, dw_hbm, dwb_hbm)

    shape = (N_SHARD, 1024, IN_SH)
    return pl.pallas_call(
        body, name="b_win", grid=(steps,),
        in_specs=[_rows(ts, 512), _rows(ts, 1024), _rows(ts, 1024), _rows(ts, 1024)],
        out_specs=[_any()] * 2, out_shape=[_sds(shape, F32), _sds(shape, BF16)],
        scratch_shapes=[pltpu.VMEM(shape, F32), pltpu.VMEM(shape, BF16)],
        compiler_params=_cp("arbitrary"))(dq, dkv, dxg, h)


def _b_inproj(dq, dkv, dxg, x, dx1, g_mix, w_in_g, tm, comm=None):
    s_len = x.shape[0]

    def body(dq_ref, dkv_ref, dxg_ref, x_ref, dx1_ref, g_ref, w_hbm, gx_ref, dgm_ref, w_ref):
        _load_w_in_once(w_hbm, w_ref)

        @pl.when(pl.program_id(0) == 0)
        def _():
            dgm_ref[...] = jnp.zeros_like(dgm_ref)

        dproj = jnp.concatenate([dq_ref[...], dkv_ref[...], dxg_ref[...]], axis=1)
        dh = _dot_nt(dproj, w_ref[...])
        dx, dgm = _rms_bwd(dh, x_ref[...], g_ref[...])
        gx_ref[...] = dx1_ref[...] + dx
        dgm_ref[...] += dgm

    return _call(
        body, "b_inproj", (s_len // tm,),
        [_rows(tm, 512), _rows(tm, 1024), _rows(tm, 1024), _rows(tm, 1024), _rows(tm, 1024),
         _full((1, 1024)), _any()],
        [_rows(tm, 1024), _full((1, 1024))],
        [_sds((s_len, 1024), F32), _sds((1, 1024), F32)],
        [pltpu.VMEM((1024, D_IN), BF16)], (dq, dkv, dxg, x, dx1, g_mix, w_in_g), "arbitrary", comm)


def _mm_tn(xa, ya, name, ts):
    s_len, k = xa.shape
    n = ya.shape[1]

    steps = s_len // ts

    def body(x_ref, y_ref, o_hbm, ob_hbm, acc, accb):
        @pl.when(pl.program_id(0) == 0)
        def _():
            acc[...] = jnp.zeros_like(acc)
        acc[...] += _dot_tn(x_ref[...].astype(BF16), y_ref[...].astype(BF16))
        _flush_grad(steps, acc, accb, o_hbm, ob_hbm)

    return pl.pallas_call(
        body, name=name, grid=(steps,), in_specs=[_rows(ts, k), _rows(ts, n)],
        out_specs=[_any()] * 2, out_shape=[_sds((k, n), F32), _sds((k, n), BF16)],
        scratch_shapes=[pltpu.VMEM((k, n), F32), pltpu.VMEM((k, n), BF16)],
        compiler_params=_cp("arbitrary"))(xa, ya)


def _frow_from_rel_bias(rb):
    hi = jnp.broadcast_to(rb[:, 256:257], (ATT_HEADS, 385))
    mid = rb[:, 1:256][:, ::-1]
    lo = jnp.broadcast_to(rb[:, 0:1], (ATT_HEADS, 128))
    wrap = jnp.broadcast_to(rb[:, 256:257], (ATT_HEADS, ROLL_W - KB))
    return jnp.concatenate([hi, mid, lo, wrap], axis=1)


def _rel_bias_grad_from_dfrow(df):
    g256 = jnp.sum(df[:, 0:385], axis=1, keepdims=True) + jnp.sum(df[:, KB:ROLL_W], axis=1, keepdims=True)
    mid = df[:, 385:640][:, ::-1]
    g0 = jnp.sum(df[:, 640:KB], axis=1, keepdims=True)
    return jnp.concatenate([g0, mid, g256], axis=1)


def _block_diag(w):
    eye = jnp.eye(8, dtype=w.dtype)
    return (w[:, :, None, :] * eye[:, None, :, None]).reshape(D_LRU, D_LRU)


MID = ['w_out', 'wq_c', 'wk_c', 'wv_c', 'wo_c']
TRANSPOSED = ['w_gate', 'w_up']
AG_IN_INPROJ = ['w_out', 'wq_c', 'wk_c']
AG_IN_ATTN = ['wv_c', 'wo_c', 'w_gate']
AG_IN_LRU = ['w_up']
AG_IN_MID = ['w_down']
RS_IN_MID = ['w_gate', 'w_up']
RS_IN_LRU = ['w_down']
RS_IN_ATTN = MID


def _local_step(x, mem, tgt, p, gw, shards=None, chip=None):
    s_len = x.shape[0]
    tm = min(256, s_len)
    tmb = min(512, s_len)
    tl = min(512, s_len)
    frow = _frow_from_rel_bias(p['rel_bias'])
    wrg = _block_diag(p['w_rg']).astype(BF16)
    wig = _block_diag(p['w_ig']).astype(BF16)
    gw = dict(gw)

    big, bigb, recv, part, sib = {}, {}, {}, {}, {}

    def ag(names):
        return [] if shards is None else [("ag", [shards[n] for n in names])]

    def rs(names):
        return [] if shards is None else [("rs", [bigb[n] for n in names])]

    def swap(names):
        return [] if shards is None else [("swap", [part[n] for n in names])]

    def reduce_own(names):
        if shards is not None:
            for n in names:
                part[n] = _sum_parts(big[n], recv[n], chip, "sum_" + n)

    h, qkv_pad, xg, *got = _f_inproj(x, p['g_mix'], gw['w_in'], tmb, ag(AG_IN_INPROJ))
    gw.update(zip(AG_IN_INPROJ, got))
    att, *got = _f_attn(qkv_pad, frow, ag(AG_IN_ATTN))
    gw.update(zip(AG_IN_ATTN, got))
    rec, u, hs, *got = _f_lru(xg, p['conv_w'], p['conv_b'], wrg, p['b_rg'], wig, p['b_ig'], p['lru_L'], tl,
                              ag(AG_IN_LRU))
    gw.update(zip(AG_IN_LRU, got))
    w_out = gw['w_out'].reshape(1024, 1024)
    wq = gw['wq_c'].reshape(1024, 1024)
    wk = gw['wk_c'].reshape(1024, 1024)
    wv = gw['wv_c'].reshape(1024, 1024)
    wo = gw['wo_c'].reshape(1024, 1024)
    mn, kx, vx = _f_mem(mem, p['g_mem'], wk, wv)
    mg, x1, hc, qx, ox, x2, *got = _f_mid(x, att, rec, p['g_out_attn'], p['g_out_lru'], w_out, p['g_cross'],
                                          wq, kx, vx, wo, tmb, ag(AG_IN_MID))
    gw.update(zip(AG_IN_MID, got))
    ffn_w = [gw[n].reshape(D_FF, 1024) for n in ('w_gate', 'w_up', 'w_down')]
    hf, gact, uact, aact, dx3, loss, dg_final = _f_ffn(x2, tgt, p['g_ffn'], p['g_final'], *ffn_w, tmb)

    ts = min(1024, s_len)
    dgact, duact, dx2, dg_ffn = _b_ffn(dx3, x2, gact, uact, p['g_ffn'], *ffn_w, tm)
    big['w_gate'], bigb['w_gate'] = _mm_tn(dgact, hf, "dw_gate", ts)
    big['w_up'], bigb['w_up'] = _mm_tn(duact, hf, "dw_up", ts)
    big['w_down'], bigb['w_down'] = _mm_tn(aact, dx3, "dw_down", ts)
    for n in ('w_gate', 'w_up', 'w_down'):
        big[n] = big[n].reshape(N_SHARD, FF_SH, 1024)
        bigb[n] = bigb[n].reshape(N_SHARD, FF_SH, 1024)

    dqx, dx1, datt, drec, dkx, dvx, dg_cross, dg_oa, dg_ol, *got = _b_mid(
        dx2, qx, x1, att, rec, kx, vx, wo, wq, w_out, p['g_cross'], p['g_out_attn'], p['g_out_lru'], tmb,
        rs(RS_IN_MID))
    recv.update(zip(RS_IN_MID, got))
    reduce_own(RS_IN_MID)
    dwk, dwv, dg_mem, dwkb, dwvb = _b_mem(dkx, dvx, mem, mn, p['g_mem'], wk, wv)
    big['wk_c'], bigb['wk_c'] = dwk, dwkb
    big['wv_c'], bigb['wv_c'] = dwv, dwvb
    big['w_out'], bigb['w_out'] = _mm_tn(mg, dx1, "dw_out", ts)
    big['wq_c'], bigb['wq_c'] = _mm_tn(hc, dqx, "dw_q", ts)
    big['wo_c'], bigb['wo_c'] = _mm_tn(ox, dx2, "dw_o", ts)
    for n in MID:
        big[n] = big[n].reshape(N_SHARD, 256, 1024)
        bigb[n] = bigb[n].reshape(N_SHARD, 256, 1024)

    dxg, dwrg, dwig, dbrg, dbig, dlam, dcw, dcb, *got = _b_lru(
        drec, hs, u, xg, p['conv_w'], wrg, p['b_rg'], wig, p['b_ig'], p['lru_L'], tl,
        rs(RS_IN_LRU) + swap(RS_IN_MID))
    recv.update(zip(RS_IN_LRU, got))
    sib.update(zip(RS_IN_MID, got[len(RS_IN_LRU):]))
    reduce_own(RS_IN_LRU)
    small = {
        'conv_w': dcw, 'conv_b': dcb, 'w_rg': dwrg, 'b_rg': dbrg, 'w_ig': dwig, 'b_ig': dbig, 'lru_L': dlam,
        'g_out_attn': dg_oa, 'g_out_lru': dg_ol, 'g_cross': dg_cross, 'g_mem': dg_mem, 'g_ffn': dg_ffn,
        'g_final': dg_final,
    }
    names = [n for n in SMALL if n in small]
    gather = [] if shards is None else [("ag8", [_pack_small(names, [small[n] for n in names], loss)])]
    dq, dkv, dfrow, *got = _b_attn(qkv_pad, att, datt, frow, rs(RS_IN_ATTN) + swap(RS_IN_LRU) + gather)
    recv.update(zip(RS_IN_ATTN, got))
    sib.update(zip(RS_IN_LRU, got[len(RS_IN_ATTN):]))
    packs = got[-1] if gather else None
    reduce_own(RS_IN_ATTN)
    small['rel_bias'] = _rel_bias_grad_from_dfrow(dfrow)
    big['w_in'], bigb['w_in'] = _b_win(dq, dkv, dxg, h, ts)
    grad_x, small['g_mix'], *got = _b_inproj(dq, dkv, dxg, x, dx1, p['g_mix'], gw['w_in'], tmb,
                                             rs(['w_in']) + swap(RS_IN_ATTN))
    recv.update(zip(['w_in'], got))
    sib.update(zip(RS_IN_ATTN, got[1:]))
    reduce_own(['w_in'])
    return loss, grad_x, small, big, part, sib, packs


def _cast_shards(ws):
    def body(*refs):
        n = len(refs) // 2
        for src, dst in zip(refs[:n], refs[n:]):
            dst[...] = src[...].astype(BF16)

    return pl.pallas_call(body, name="cast_shards", out_shape=[_sds(w.shape, BF16) for w in ws],
                          compiler_params=_cp())(*ws)


def _sum_parts(own4, recv3, chip, name):
    _, r, c = own4.shape
    steps = _ew_steps(r)
    tr = r // steps

    def body(chip_ref, own_ref, rc_ref, o_ref):
        o_ref[...] = ((own_ref[0] + rc_ref[0].astype(F32)) + rc_ref[1].astype(F32)) + rc_ref[2].astype(F32)

    grid_spec = pltpu.PrefetchScalarGridSpec(
        num_scalar_prefetch=1, grid=(steps,),
        in_specs=[pl.BlockSpec((1, tr, c), lambda i, ch: (ch[0], i, 0)),
                  pl.BlockSpec((3, tr, c), lambda i, ch: (0, i, 0))],
        out_specs=pl.BlockSpec((tr, c), lambda i, ch: (i, 0)))
    return pl.pallas_call(body, name=name, grid_spec=grid_spec, out_shape=_sds((r, c), F32),
                          compiler_params=_cp("parallel"))(chip, own4, recv3)


def _adamw_math(w, g, m, v):
    m = ADAM_B1 * m + (1.0 - ADAM_B1) * g
    v = ADAM_B2 * v + (1.0 - ADAM_B2) * (g * g)
    m_hat = m / (1.0 - ADAM_B1 ** ADAM_STEP)
    v_hat = v / (1.0 - ADAM_B2 ** ADAM_STEP)
    delta = -ADAM_LR * (m_hat / (jnp.sqrt(v_hat) + ADAM_EPS) + ADAM_WD * w)
    return delta, m, v


def _final_adamw(pa, pb, w, m, v, name):
    r, c = w.shape
    steps = _ew_steps(r)
    tr = r // steps

    def body(pa_ref, pb_ref, w_ref, m_ref, v_ref, g_ref, d_ref, nm_ref, nv_ref):
        g = pa_ref[...] + pb_ref[...]
        g_ref[...] = g
        d_ref[...], nm_ref[...], nv_ref[...] = _adamw_math(w_ref[...], g, m_ref[...], v_ref[...])

    return pl.pallas_call(
        body, name=name, grid=(steps,), in_specs=[_rows(tr, c)] * 5, out_specs=[_rows(tr, c)] * 4,
        out_shape=[_sds((r, c), F32)] * 4, compiler_params=_cp("parallel"))(pa, pb, w, m, v)


def _pack_put(ref, name, val_ref):
    r = _pack_rows()[name]
    shape = val_ref.shape
    if len(shape) == 3:
        for b in range(shape[0]):
            ref[r:r + shape[1], b * shape[2]:(b + 1) * shape[2]] = val_ref[b]
    elif shape[1] == 2 * PACK_W:
        ref[r:r + 1, :] = val_ref[:, 0:PACK_W]
        ref[r + 1:r + 2, :] = val_ref[:, PACK_W:2 * PACK_W]
    else:
        ref[r:r + shape[0], 0:shape[1]] = val_ref[...]


def _pack_get(ref, name, shape):
    r = _pack_rows()[name]
    if len(shape) == 3:
        return jnp.stack([ref[r:r + shape[1], b * shape[2]:(b + 1) * shape[2]] for b in range(shape[0])])
    if shape[1] == 2 * PACK_W:
        return jnp.concatenate([ref[r:r + 1, :], ref[r + 1:r + 2, :]], axis=1)
    return ref[r:r + shape[0], 0:shape[1]]


def _pack_small(names, g, loss):
    n = len(g)

    def body(*refs):
        pack = refs[n + 1]
        pack[...] = jnp.zeros_like(pack)
        for a, name in enumerate(names):
            _pack_put(pack, name, refs[a])
        _pack_put(pack, 'loss', refs[n])

    return pl.pallas_call(body, name="pack_small", out_shape=_sds((PACK_ROWS, PACK_W), F32),
                          compiler_params=_cp())(*g, loss)


def _all_peers():
    x, y, c = _mesh_pos()
    peers = []
    for k in range(1, 8):
        px = 1 - x if k & 4 else x
        py = 1 - y if k & 2 else y
        pc = 1 - c if k & 1 else c
        peers.append(((px, py, pc), 4 * px + 2 * py + pc))
    return peers, 4 * x + 2 * y + c


def _ag8_copies(ins, outs, sems):
    send_sems, recv_sems, loc_sems = sems
    n = len(ins)
    peers, me = _all_peers()

    def remote(k, j, slot):
        return pltpu.make_async_remote_copy(
            src_ref=ins[k], dst_ref=outs[k].at[slot], send_sem=send_sems.at[k, j], recv_sem=recv_sems.at[k, j],
            device_id=peers[j][0], device_id_type=MESH_ID)

    def local(k):
        return pltpu.make_async_copy(ins[k], outs[k].at[me], loc_sems.at[k])

    def start():
        for k in range(n):
            local(k).start()
            for j in range(7):
                remote(k, j, me).start()

    def wait():
        for k in range(n):
            for j in range(7):
                remote(k, j, peers[j][1]).wait_recv()
        for k in range(n):
            for j in range(7):
                remote(k, j, me).wait_send()
            local(k).wait()

    return start, _no_forward, wait


def _ar_late(names, g):
    n = len(g)

    def body(*refs):
        tot_ref, pack, buf, send_sems, recv_sems = refs[n:]
        peers, me = _all_peers()

        def remote(j, slot):
            return pltpu.make_async_remote_copy(
                src_ref=pack, dst_ref=buf.at[slot], send_sem=send_sems.at[j], recv_sem=recv_sems.at[j],
                device_id=peers[j][0], device_id_type=MESH_ID)

        pack[...] = jnp.zeros_like(pack)
        for a, name in enumerate(names):
            _pack_put(pack, name, refs[a])
        for j in range(7):
            remote(j, me).start()
        buf[me] = pack[...]
        for j in range(7):
            remote(j, peers[j][1]).wait_recv()
        for j in range(7):
            remote(j, me).wait_send()
        tot = buf[0]
        for d in range(1, 8):
            tot = tot + buf[d]
        tot_ref[...] = tot

    return pl.pallas_call(
        body, name="ar_late", out_shape=_sds((LATE_ROWS, PACK_W), F32),
        scratch_shapes=[pltpu.VMEM((LATE_ROWS, PACK_W), F32), pltpu.VMEM((8, LATE_ROWS, PACK_W), F32),
                        pltpu.SemaphoreType.DMA((7,)), pltpu.SemaphoreType.DMA((7,))],
        compiler_params=_cp())(*g)


def _adamw_small(packs, late_tot, g_shapes, loss_shape, w, m, v):
    n = len(w)

    def body(*refs):
        packs_ref, late_ref = refs[0], refs[1]
        w_refs, m_refs, v_refs = (refs[2 + i * n:2 + (i + 1) * n] for i in range(3))
        o0 = 3 * n + 2
        go, do, mo, vo = (refs[o0 + i * n:o0 + (i + 1) * n] for i in range(4))
        loss_out, tot_ref = refs[o0 + 4 * n], refs[o0 + 4 * n + 1]
        x, y, _ = _mesh_pos()
        tot = packs_ref[0]
        for d in range(1, 8):
            tot = tot + packs_ref[d]
        tot_ref[...] = tot
        tot_ref[0:LATE_ROWS, :] += late_ref[...]
        loss_out[...] = _pack_get(tot_ref, 'loss', loss_shape)
        for a, name in enumerate(SMALL):
            if name == 'conv_w':
                r = _pack_rows()[name]
                ga = tot_ref[r:r + g_shapes[a][0], pl.ds(pl.multiple_of((2 * x + y) * 128, 128), 128)]
            else:
                ga = _pack_get(tot_ref, name, g_shapes[a])
            go[a][...] = ga
            do[a][...], mo[a][...], vo[a][...] = _adamw_math(w_refs[a][...], ga, m_refs[a][...], v_refs[a][...])

    out_shape = [_sds(a.shape, F32) for a in w] * 4 + [_sds(loss_shape, F32)]
    return pl.pallas_call(body, name="adamw_small", out_shape=out_shape,
                          scratch_shapes=[pltpu.VMEM((PACK_ROWS, PACK_W), F32)],
                          compiler_params=_cp())(packs, late_tot, *w, *m, *v)


PACK_W = 512
PACK_ROWS = 160
LATE = ['g_mix', 'rel_bias']
LATE_ROWS = 32


def _pack_rows():
    rows, r = {}, 0
    for name in ['g_mix', 'g_cross', 'g_mem', 'g_ffn', 'g_final']:
        rows[name] = r
        r += 2
    for name in ['conv_b', 'b_rg', 'b_ig', 'lru_L', 'g_out_attn', 'g_out_lru']:
        rows[name] = r
        r += 1
    rows['conv_w'] = r
    rows['loss'] = r + 4
    rows['rel_bias'] = 24
    rows['w_rg'] = 32
    rows['w_ig'] = 32 + LRU_BLOCK
    assert r + 5 <= 24 and rows['w_ig'] + LRU_BLOCK == PACK_ROWS
    assert rows['g_mix'] + 2 <= LATE_ROWS and rows['rel_bias'] + 8 <= LATE_ROWS
    return rows


INPUT_NAMES = (['x', 'mem'] + WEIGHTS + ['loss_target'] + ['m_' + n for n in WEIGHTS] + ['v_' + n for n in WEIGHTS])


def kernel(x, mem, g_mix, w_in, rel_bias, conv_w, conv_b, w_rg, b_rg, w_ig, b_ig, lru_L, g_out_attn, g_out_lru, w_out, g_cross, g_mem, wq_c, wk_c, wv_c, wo_c, g_ffn, w_gate, w_up, w_down, g_final, loss_target, m_g_mix, m_w_in, m_rel_bias, m_conv_w, m_conv_b, m_w_rg, m_b_rg, m_w_ig, m_b_ig, m_lru_L, m_g_out_attn, m_g_out_lru, m_w_out, m_g_cross, m_g_mem, m_wq_c, m_wk_c, m_wv_c, m_wo_c, m_g_ffn, m_w_gate, m_w_up, m_w_down, m_g_final, v_g_mix, v_w_in, v_rel_bias, v_conv_w, v_conv_b, v_w_rg, v_b_rg, v_w_ig, v_b_ig, v_lru_L, v_g_out_attn, v_g_out_lru, v_w_out, v_g_cross, v_g_mem, v_wq_c, v_wk_c, v_wv_c, v_wo_c, v_g_ffn, v_w_gate, v_w_up, v_w_down, v_g_final):
    a = dict(zip(INPUT_NAMES, (x, mem, g_mix, w_in, rel_bias, conv_w, conv_b, w_rg, b_rg, w_ig, b_ig, lru_L, g_out_attn, g_out_lru, w_out, g_cross, g_mem, wq_c, wk_c, wv_c, wo_c, g_ffn, w_gate, w_up, w_down, g_final, loss_target, m_g_mix, m_w_in, m_rel_bias, m_conv_w, m_conv_b, m_w_rg, m_b_rg, m_w_ig, m_b_ig, m_lru_L, m_g_out_attn, m_g_out_lru, m_w_out, m_g_cross, m_g_mem, m_wq_c, m_wk_c, m_wv_c, m_wo_c, m_g_ffn, m_w_gate, m_w_up, m_w_down, m_g_final, v_g_mix, v_w_in, v_rel_bias, v_conv_w, v_conv_b, v_w_rg, v_b_rg, v_w_ig, v_b_ig, v_lru_L, v_g_out_attn, v_g_out_lru, v_w_out, v_g_cross, v_g_mem, v_wq_c, v_wk_c, v_wv_c, v_wo_c, v_g_ffn, v_w_gate, v_w_up, v_w_down, v_g_final)))
    chip = 2 * lax.axis_index("x") + lax.axis_index("y")

    def shard(name):
        arr = a[name][0]
        return jnp.swapaxes(arr, 0, 1) if name[2:] in TRANSPOSED or name in TRANSPOSED else arr

    shards = dict(zip(BIG, _cast_shards([shard(n) for n in BIG])))
    w_in_g, conv_w_g = _comm_only("ag_w_in", [("ag", [shards['w_in']]), ("agf", [a['conv_w'][0]])])
    conv_w_full = conv_w_g.transpose(1, 0, 2).reshape(4, D_LRU)

    p = {n: a[n] for n in SMALL}
    p['rel_bias'] = a['rel_bias'][0]
    p['w_rg'] = a['w_rg'][0]
    p['w_ig'] = a['w_ig'][0]
    p['conv_w'] = conv_w_full
    p['g_final'] = a['g_final'][None, :]
    chip_arr = jnp.reshape(chip, (1,)).astype(jnp.int32)
    loss_part, grad_x, small, _, part, sib, packs = _local_step(
        a['x'][0], a['mem'][0], a['loss_target'][0], p, {'w_in': w_in_g}, shards, chip_arr)

    sib['w_in'], = _comm_only("swap_w_in", [("swap", [part['w_in']])])
    out = {}
    for n in BIG:
        res = _final_adamw(part[n], sib[n], shard(n), shard('m_' + n), shard('v_' + n), "adamw_" + n)
        out[n] = [jnp.swapaxes(r, 0, 1) for r in res] if n in TRANSPOSED else res

    def natural(arr):
        return arr[0] if arr.ndim >= 3 else (arr[None, :] if arr.ndim == 1 else arr)

    small_out = _adamw_small(packs, _ar_late(LATE, [small[n] for n in LATE]), [small[n].shape for n in SMALL],
                             loss_part.shape, *[[natural(a[pre + n]) for n in SMALL] for pre in ('', 'm_', 'v_')])
    ns = len(SMALL)
    loss = small_out[4 * ns][0, 0]

    def leaf(i, n):
        if n in BIG:
            return out[n][i][None]
        return small_out[i * ns + SMALL.index(n)].reshape(a[n].shape)

    return (loss, grad_x[None], *[leaf(i, n) for i in range(4) for n in WEIGHTS])
```

```python
import math

import jax
import jax.numpy as jnp
from jax import lax
from jax.experimental import pallas as pl
from jax.experimental.pallas import tpu as pltpu

F32 = jnp.float32
BF16 = jnp.bfloat16

D_MODEL = 1024
D_ATT = 512
D_LRU = 512
HEAD_DIM = 64
ATT_HEADS = 8
CHUNK = 64
LEFT_CHUNKS = 8
X_HEADS = 4
X_HEAD_DIM = 256
N_SHARD = 4
IN_SH = 640
D_IN = N_SHARD * IN_SH
FF_SH = 704
D_FF = N_SHARD * FF_SH
EPS = 1e-6
LRU_C = 8.0
LRU_BLOCKS = 8
LRU_BLOCK = 64
QB = 256
KB = 768
ROLL_W = 1024
NEG = -1e30
ATT_SCALE = HEAD_DIM ** -0.5
X_SCALE = X_HEAD_DIM ** -0.5

ADAM_LR = 0.001
ADAM_B1 = 0.9
ADAM_B2 = 0.999
ADAM_EPS = 1e-08
ADAM_WD = 0.01
ADAM_STEP = 10

VMEM_LIMIT_V7X = 56 * 1024 * 1024
BF16_ROWS = 16


def _ew_steps(rows):
    return max(s for s in (2, 1) if rows % (s * BF16_ROWS) == 0)
MESH_ID = pl.DeviceIdType.MESH

WEIGHTS = ['g_mix', 'w_in', 'rel_bias', 'conv_w', 'conv_b', 'w_rg', 'b_rg', 'w_ig', 'b_ig', 'lru_L',
           'g_out_attn', 'g_out_lru', 'w_out', 'g_cross', 'g_mem', 'wq_c', 'wk_c', 'wv_c', 'wo_c',
           'g_ffn', 'w_gate', 'w_up', 'w_down', 'g_final']
BIG = ['w_in', 'w_out', 'wq_c', 'wk_c', 'wv_c', 'wo_c', 'w_gate', 'w_up', 'w_down']
SMALL = [n for n in WEIGHTS if n not in BIG]


def _sds(shape, dtype):
    return jax.ShapeDtypeStruct(shape, dtype)


def _cp(*sem):
    return pltpu.CompilerParams(dimension_semantics=sem or None, vmem_limit_bytes=VMEM_LIMIT_V7X)


def _rows(tm, n):
    return pl.BlockSpec((tm, n), lambda i: (i, 0))


def _full(shape):
    nd = len(shape)
    return pl.BlockSpec(shape, lambda i: (0,) * nd)


def _dot(a, b):
    return jnp.dot(a, b, preferred_element_type=F32)


def _dot_nt(a, b):
    return lax.dot_general(a, b, (((1,), (1,)), ((), ())), preferred_element_type=F32)


def _dot_tn(a, b):
    return lax.dot_general(a, b, (((0,), (0,)), ((), ())), preferred_element_type=F32)


def _rinv(x):
    return lax.rsqrt(jnp.mean(x * x, axis=-1, keepdims=True) + EPS)


def _rms_bwd(dy, x, g):
    r = _rinv(x)
    yh = x * r
    dyh = dy * g
    dx = r * (dyh - yh * jnp.mean(dyh * yh, axis=-1, keepdims=True))
    return dx, jnp.sum(dy * yh, axis=0, keepdims=True)


def _gelu(x):
    c = math.sqrt(2.0 / math.pi)
    t = jnp.tanh(c * (x + 0.044715 * x * x * x))
    return 0.5 * x * (1.0 + t)


def _gelu_and_grad(x):
    c = math.sqrt(2.0 / math.pi)
    t = jnp.tanh(c * (x + 0.044715 * x * x * x))
    g = 0.5 * x * (1.0 + t)
    dg = 0.5 * (1.0 + t) + 0.5 * x * (1.0 - t * t) * c * (1.0 + 3.0 * 0.044715 * x * x)
    return g, dg


def _neg_expm1(z):
    series = -z * (1.0 + z * (0.5 + z * ((1.0 / 6.0) + z * (1.0 / 24.0))))
    return jnp.where(z > -0.03, series, 1.0 - jnp.exp(z))


def _lru_gates(u, wrg, brg, wig, big, lam):
    ub = u.astype(BF16)
    r = jax.nn.sigmoid(_dot(ub, wrg) + brg)
    ig = jax.nn.sigmoid(_dot(ub, wig) + big)
    sp = jnp.maximum(-lam, 0.0) + jnp.log1p(jnp.exp(-jnp.abs(lam)))
    la = -LRU_C * r * sp
    a = jnp.exp(la)
    mult = jnp.sqrt(jnp.maximum(_neg_expm1(2.0 * la), 0.0))
    return ub, r, ig, sp, a, mult


def _scan8(a8, b8, hprev):
    row = lax.broadcasted_iota(jnp.int32, a8.shape, 0)
    aa, bb = a8, b8
    for d in (1, 2, 4):
        a_s = pltpu.roll(aa, d, 0)
        b_s = pltpu.roll(bb, d, 0)
        m = row >= d
        bb = jnp.where(m, aa * b_s + bb, bb)
        aa = jnp.where(m, aa * a_s, aa)
    return aa * hprev + bb


def _rscan8(c8, d8, lnext):
    row = lax.broadcasted_iota(jnp.int32, c8.shape, 0)
    cc, dd = c8, d8
    for d in (1, 2, 4):
        c_s = pltpu.roll(cc, 8 - d, 0)
        d_s = pltpu.roll(dd, 8 - d, 0)
        m = row < 8 - d
        dd = jnp.where(m, cc * d_s + dd, dd)
        cc = jnp.where(m, cc * c_s, cc)
    return cc * lnext + dd


def _mesh_pos():
    return lax.axis_index("x"), lax.axis_index("y"), lax.axis_index("c")


def _other_chips(x, y):
    return [(1 - x, y), (x, 1 - y), (1 - x, 1 - y)]


def _no_forward():
    pass


def _ag_full_copies(ins, outs, sems):
    send_sems, recv_sems, loc_sems = sems
    n = len(ins)
    x, y, c = _mesh_pos()
    mine = 2 * x + y
    chips = _other_chips(x, y)

    def remote(k, j, slot):
        px, py = chips[j]
        return pltpu.make_async_remote_copy(
            src_ref=ins[k], dst_ref=outs[k].at[slot], send_sem=send_sems.at[k, j], recv_sem=recv_sems.at[k, j],
            device_id=(px, py, c), device_id_type=MESH_ID)

    def local(k):
        return pltpu.make_async_copy(ins[k], outs[k].at[mine], loc_sems.at[k])

    def start():
        for k in range(n):
            local(k).start()
            for j in range(3):
                remote(k, j, mine).start()

    def wait():
        for k in range(n):
            for j, (px, py) in enumerate(chips):
                remote(k, j, 2 * px + py).wait_recv()
        for k in range(n):
            for j in range(3):
                remote(k, j, mine).wait_send()
            local(k).wait()

    return start, _no_forward, wait


def _ag_copies(ins, outs, sems):
    send_sems, recv_sems, fsend_sems, frecv_sems, loc_sems = sems
    n = len(ins)
    x, y, c = _mesh_pos()
    mine = 2 * x + y
    chips = _other_chips(x, y)

    def half(ref, hc):
        r = ref.shape[0] // 2
        return ref.at[pl.ds(pl.multiple_of(hc * r, 16), r)]

    def ici(k, j, slot):
        px, py = chips[j]
        return pltpu.make_async_remote_copy(
            src_ref=half(ins[k], c), dst_ref=half(outs[k].at[slot], c),
            send_sem=send_sems.at[k, j], recv_sem=recv_sems.at[k, j],
            device_id=(px, py, c), device_id_type=MESH_ID)

    def d2d(k, j, hc):
        px, py = chips[j]
        part = half(outs[k].at[2 * px + py], hc)
        return pltpu.make_async_remote_copy(
            src_ref=part, dst_ref=part, send_sem=fsend_sems.at[k, j], recv_sem=frecv_sems.at[k, j],
            device_id=(x, y, 1 - c), device_id_type=MESH_ID)

    def local(k):
        return pltpu.make_async_copy(ins[k], outs[k].at[mine], loc_sems.at[k])

    def start():
        for k in range(n):
            local(k).start()
            for j in range(3):
                ici(k, j, mine).start()

    def forward():
        for k in range(n):
            for j, (px, py) in enumerate(chips):
                ici(k, j, 2 * px + py).wait_recv()
                d2d(k, j, c).start()

    def wait():
        for k in range(n):
            for j in range(3):
                d2d(k, j, 1 - c).wait_recv()
        for k in range(n):
            for j in range(3):
                d2d(k, j, c).wait_send()
                ici(k, j, mine).wait_send()
            local(k).wait()

    return start, forward, wait


def _rs_copies(ins, outs, sems):
    send_sems, recv_sems = sems
    n = len(ins)
    x, y, c = _mesh_pos()
    chips = _other_chips(x, y)

    def remote(k, j):
        px, py = chips[j]
        return pltpu.make_async_remote_copy(
            src_ref=ins[k].at[2 * px + py], dst_ref=outs[k].at[j],
            send_sem=send_sems.at[k, j], recv_sem=recv_sems.at[k, j],
            device_id=(px, py, c), device_id_type=MESH_ID)

    def start():
        for k in range(n):
            for j in range(3):
                remote(k, j).start()

    def wait():
        for k in range(n):
            for j in range(3):
                remote(k, j).wait_recv()
        for k in range(n):
            for j in range(3):
                remote(k, j).wait_send()

    return start, _no_forward, wait


def _swap_copies(ins, outs, sems):
    send_sems, recv_sems = sems
    x, y, c = _mesh_pos()
    copies = [pltpu.make_async_remote_copy(
        src_ref=ins[k], dst_ref=outs[k], send_sem=send_sems.at[k], recv_sem=recv_sems.at[k],
        device_id=(x, y, 1 - c), device_id_type=MESH_ID) for k in range(len(ins))]

    def start():
        for cp in copies:
            cp.start()

    def wait():
        for cp in copies:
            cp.wait()

    return start, _no_forward, wait


def _comm_plan(groups):
    plan, arrs, shapes, sems = [], [], [], []
    for kind, group in groups:
        k = len(group)
        arrs += group
        per_peer = pltpu.SemaphoreType.DMA((k, 3))
        if kind == "ag":
            shapes += [_sds((N_SHARD,) + w.shape, w.dtype) for w in group]
            gsems = [per_peer] * 4 + [pltpu.SemaphoreType.DMA((k,))]
            maker = _ag_copies
        elif kind == "agf":
            shapes += [_sds((N_SHARD,) + w.shape, w.dtype) for w in group]
            gsems = [per_peer] * 2 + [pltpu.SemaphoreType.DMA((k,))]
            maker = _ag_full_copies
        elif kind == "ag8":
            shapes += [_sds((8,) + g.shape, g.dtype) for g in group]
            gsems = [pltpu.SemaphoreType.DMA((k, 7))] * 2 + [pltpu.SemaphoreType.DMA((k,))]
            maker = _ag8_copies
        elif kind == "rs":
            shapes += [_sds((3,) + g.shape[1:], g.dtype) for g in group]
            gsems = [pltpu.SemaphoreType.DMA((k, 3)), pltpu.SemaphoreType.DMA((k, 3))]
            maker = _rs_copies
        else:
            shapes += [_sds(g.shape, g.dtype) for g in group]
            gsems = [pltpu.SemaphoreType.DMA((k,)), pltpu.SemaphoreType.DMA((k,))]
            maker = _swap_copies
        plan.append((maker, k, len(gsems)))
        sems += gsems
    return plan, arrs, shapes, sems


def _comm_fns(plan, cins, couts, sems):
    fns, a, s = [], 0, 0
    for maker, k, ns in plan:
        fns.append(maker(cins[a:a + k], couts[a:a + k], sems[s:s + ns]))
        a += k
        s += ns

    def start():
        for st, _, _ in fns:
            st()

    def forward():
        for _, fw, _ in fns:
            fw()

    def wait():
        for _, _, wt in fns:
            wt()

    return start, forward, wait


def _call(body, name, grid, in_specs, out_specs, out_shape, scratch, args, sem, comm=None):
    if not comm:
        return pl.pallas_call(body, name=name, grid=grid, in_specs=in_specs, out_specs=out_specs,
                              out_shape=out_shape, scratch_shapes=scratch, compiler_params=_cp(sem))(*args)
    plan, c_arrs, c_shapes, c_sems = _comm_plan(comm)
    k = len(c_arrs)
    n_in, n_out, n_scr = len(in_specs), len(out_specs), len(scratch)
    last = grid[0] - 1
    fwd_step = max(1, (2 * last) // 3)

    def wrapped(*refs):
        ins, cins = refs[:n_in], refs[n_in:n_in + k]
        o0 = n_in + k
        outs, couts = refs[o0:o0 + n_out], refs[o0 + n_out:o0 + n_out + k]
        s0 = o0 + n_out + k
        start, forward, wait = _comm_fns(plan, cins, couts, refs[s0 + n_scr:])
        pl.when(pl.program_id(0) == 0)(start)
        pl.when(pl.program_id(0) == fwd_step)(forward)
        body(*ins, *outs, *refs[s0:s0 + n_scr])
        pl.when(pl.program_id(0) == last)(wait)

    return pl.pallas_call(
        wrapped, name=name, grid=grid, in_specs=list(in_specs) + [_any()] * k,
        out_specs=list(out_specs) + [_any()] * k, out_shape=list(out_shape) + c_shapes,
        scratch_shapes=list(scratch) + c_sems, compiler_params=_cp(sem))(*args, *c_arrs)


def _comm_only(name, comm):
    plan, c_arrs, c_shapes, c_sems = _comm_plan(comm)
    k = len(c_arrs)

    def body(*refs):
        start, forward, wait = _comm_fns(plan, refs[:k], refs[k:2 * k], refs[2 * k:])
        start()
        forward()
        wait()

    return pl.pallas_call(body, name=name, in_specs=[_any()] * k, out_specs=[_any()] * k, out_shape=c_shapes,
                          scratch_shapes=c_sems, compiler_params=_cp())(*c_arrs)


def _any():
    return pl.BlockSpec(memory_space=pl.ANY)


def _load_w_in_once(w_hbm, w_ref):
    @pl.when(pl.program_id(0) == 0)
    def _():
        for s in range(N_SHARD):
            pltpu.sync_copy(w_hbm.at[s], w_ref.at[:, pl.ds(s * IN_SH, IN_SH)])


def _f_inproj(x, g_mix, w_in_g, tm, comm=None):
    s_len = x.shape[0]
    pad_rows = LEFT_CHUNKS * CHUNK
    npad = pad_rows // tm

    def body(x_ref, g_ref, w_hbm, h_ref, qkv_ref, xg_ref, w_ref):
        i = pl.program_id(0)
        _load_w_in_once(w_hbm, w_ref)

        @pl.when(i < npad)
        def _():
            qkv_ref[...] = jnp.zeros_like(qkv_ref)

        @pl.when(i >= npad)
        def _():
            xv = x_ref[...]
            h = (xv * _rinv(xv) * g_ref[...]).astype(BF16)
            h_ref[...] = h
            proj = _dot(h, w_ref[...])
            qkv_ref[:, 0:D_ATT] = (proj[:, 0:D_ATT] * ATT_SCALE).astype(BF16)
            qkv_ref[:, D_ATT:3 * D_ATT] = proj[:, D_ATT:3 * D_ATT].astype(BF16)
            xg_ref[...] = proj[:, 3 * D_ATT:D_IN]

    def tok(n):
        return pl.BlockSpec((tm, n), lambda i: (jnp.maximum(i - npad, 0), 0))

    return _call(
        body, "f_inproj", (s_len // tm + npad,),
        [tok(1024), _full((1, 1024)), _any()],
        [tok(1024), _rows(tm, 1536), tok(1024)],
        [_sds((s_len, 1024), BF16), _sds((s_len + pad_rows, 1536), BF16), _sds((s_len, 1024), F32)],
        [pltpu.VMEM((1024, D_IN), BF16)], (x, g_mix, w_in_g), "arbitrary", comm)


N_BIAS = 3


def _bias_table(frow_ref, bias_sc):
    qa = lax.broadcasted_iota(jnp.int32, (QB, KB), 0) // CHUNK
    kcol = lax.broadcasted_iota(jnp.int32, (QB, KB), 1)
    kb = kcol // CHUNK
    band = jnp.where((kb >= qa) & (kb - qa <= LEFT_CHUNKS), 0.0, NEG).astype(F32)
    for h in range(ATT_HEADS):
        row = jnp.broadcast_to(frow_ref[h:h + 1, :], (QB, ROLL_W))
        toep = pltpu.roll(row, 0, 1, stride=1, stride_axis=0)
        gen = toep[:, 0:KB] + band
        bias_sc[N_BIAS - 1, h] = gen
        for v in range(N_BIAS - 1):
            pad_keys = LEFT_CHUNKS * CHUNK - v * QB
            bias_sc[v, h] = gen + jnp.where(kcol < pad_keys, NEG, 0.0).astype(F32)


def _even_lanes():
    return lax.broadcasted_iota(jnp.int32, (1, 2 * HEAD_DIM), 1) < HEAD_DIM


def _att_probs(qm, kts, bias):
    s = jnp.concatenate([_dot_nt(qm, k) for k in kts], axis=1) + bias
    return jnp.exp(s - jnp.max(s, axis=-1, keepdims=True))


def _att_in_specs(clamp):
    def spec(j, col):
        return pl.BlockSpec((QB, D_ATT), lambda i: (clamp(i) + j, col))
    return [spec(2, 0), spec(0, 1), spec(1, 1), spec(2, 1), spec(0, 2), spec(1, 2), spec(2, 2)]


def _f_attn(qkv_pad, frow, comm=None):
    s_len = qkv_pad.shape[0] - LEFT_CHUNKS * CHUNK
    nb = s_len // QB

    def body(q_ref, k0, k1, k2, v0, v1, v2, frow_ref, o_ref, bias_sc):
        i = pl.program_id(0)

        @pl.when(i == 0)
        def _():
            _bias_table(frow_ref, bias_sc)

        var = jnp.minimum(i, N_BIAS - 1)
        even = _even_lanes()
        for hp in range(ATT_HEADS // 2):
            cs = slice(hp * 2 * HEAD_DIM, (hp + 1) * 2 * HEAD_DIM)
            qt = q_ref[:, cs]
            kts = [k0[:, cs], k1[:, cs], k2[:, cs]]
            vts = [v0[:, cs], v1[:, cs], v2[:, cs]]
            res = []
            for e in range(2):
                keep = even if e == 0 else jnp.logical_not(even)
                pb = _att_probs(jnp.where(keep, qt, 0), kts, bias_sc[var, 2 * hp + e]).astype(BF16)
                r = _dot(pb[:, 0:QB], jnp.where(keep, vts[0], 1))
                for j in (1, 2):
                    r = r + _dot(pb[:, j * QB:(j + 1) * QB], jnp.where(keep, vts[j], 1))
                res.append(r / pltpu.roll(r, HEAD_DIM, 1))
            o_ref[:, cs] = jnp.where(even, res[0], res[1])

    return _call(
        body, "f_attn", (nb,),
        _att_in_specs(lambda i: i) + [_full((ATT_HEADS, ROLL_W))],
        [_rows(QB, D_ATT)], [_sds((s_len, D_ATT), F32)],
        [pltpu.VMEM((N_BIAS, ATT_HEADS, QB, KB), F32)], (*([qkv_pad] * 7), frow), "arbitrary", comm)


def _f_lru(xg, conv_w, conv_b, wrg, brg, wig, big, lam, tl, comm=None):
    s_len = xg.shape[0]

    def body(xg_ref, cw_ref, cb_ref, wrg_ref, brg_ref, wig_ref, big_ref, l_ref,
             rec_ref, u_ref, hs_ref, xbuf, a_sc, b_sc, hcar):
        i = pl.program_id(0)

        @pl.when(i == 0)
        def _():
            xbuf[0:8, :] = jnp.zeros((8, D_LRU), F32)
            hcar[...] = jnp.zeros((8, D_LRU), F32)

        xu0 = xg_ref[:, 0:D_LRU]
        xbuf[8:8 + tl, :] = xu0
        u = cb_ref[...] + cw_ref[0:1, :] * xbuf[pl.ds(5, tl), :]
        for j in range(1, 4):
            u = u + cw_ref[j:j + 1, :] * xbuf[pl.ds(5 + j, tl), :]
        xbuf[0:8, :] = xu0[tl - 8:tl, :]
        u_ref[...] = u
        _, _, ig, _, a, mult = _lru_gates(u, wrg_ref[...], brg_ref[...], wig_ref[...], big_ref[...], l_ref[...])
        a_sc[...] = a
        b_sc[...] = mult * (ig * u)

        def grp(g, hprev):
            off = pl.multiple_of(g * 8, 8)
            h8 = _scan8(a_sc[pl.ds(off, 8), :], b_sc[pl.ds(off, 8), :], hprev)
            hs_ref[pl.ds(off, 8), :] = h8
            return h8[7:8, :]

        hcar[0:1, :] = lax.fori_loop(0, tl // 8, grp, hcar[0:1, :])
        rec_ref[...] = hs_ref[...] * _gelu(xg_ref[:, D_LRU:2 * D_LRU])

    vec = _full((1, D_LRU))
    return _call(
        body, "f_lru", (s_len // tl,),
        [_rows(tl, 1024), _full((4, D_LRU)), vec, _full((D_LRU, D_LRU)), vec, _full((D_LRU, D_LRU)), vec, vec],
        [_rows(tl, D_LRU)] * 3, [_sds((s_len, D_LRU), F32)] * 3,
        [pltpu.VMEM((tl + 8, D_LRU), F32), pltpu.VMEM((tl, D_LRU), F32),
         pltpu.VMEM((tl, D_LRU), F32), pltpu.VMEM((8, D_LRU), F32)],
        (xg, conv_w, conv_b, wrg, brg, wig, big, lam), "arbitrary", comm)


def _f_mem(mem, g_mem, wk, wv):
    def body(mem_ref, g_ref, wk_ref, wv_ref, mn_ref, kx_ref, vx_ref):
        mv = mem_ref[...]
        mn = (mv * _rinv(mv) * g_ref[...]).astype(BF16)
        mn_ref[...] = mn
        kx_ref[...] = _dot(mn, wk_ref[...]).astype(BF16)
        vx_ref[...] = _dot(mn, wv_ref[...]).astype(BF16)

    m = mem.shape[0]
    return pl.pallas_call(
        body, name="f_mem", out_shape=[_sds((m, 1024), BF16)] * 3,
        compiler_params=_cp())(mem, g_mem, wk, wv)


def _xattn_probs(q, k):
    s = _dot_nt(q, k) * X_SCALE
    m = jnp.max(s, axis=-1, keepdims=True)
    p = jnp.exp(s - m)
    return p, jnp.sum(p, axis=-1, keepdims=True)


def _f_mid(x, att, rec, g_oa, g_ol, w_out, g_cross, wq, kx, vx, wo, tm, comm=None):
    s_len = x.shape[0]
    m_len = kx.shape[0]

    def body(x_ref, att_ref, rec_ref, goa_ref, gol_ref, wout_ref, gc_ref, wq_ref, kx_ref, vx_ref, wo_ref,
             mg_ref, x1_ref, hc_ref, qx_ref, ox_ref, x2_ref):
        av = att_ref[...]
        rv = rec_ref[...]
        mg_ref[:, 0:D_ATT] = (av * _rinv(av) * goa_ref[...]).astype(BF16)
        mg_ref[:, D_ATT:1024] = (rv * _rinv(rv) * gol_ref[...]).astype(BF16)
        x1 = x_ref[...] + _dot(mg_ref[...], wout_ref[...])
        x1_ref[...] = x1
        hc = (x1 * _rinv(x1) * gc_ref[...]).astype(BF16)
        hc_ref[...] = hc
        qx_ref[...] = _dot(hc, wq_ref[...]).astype(BF16)
        for h in range(X_HEADS):
            sl = slice(h * X_HEAD_DIM, (h + 1) * X_HEAD_DIM)
            p, l = _xattn_probs(qx_ref[:, sl], kx_ref[:, sl])
            ox_ref[:, sl] = (_dot(p.astype(BF16), vx_ref[:, sl]) / l).astype(BF16)
        x2_ref[...] = x1 + _dot(ox_ref[...], wo_ref[...])

    sq = _full((1024, 1024))
    return _call(
        body, "f_mid", (s_len // tm,),
        [_rows(tm, 1024), _rows(tm, 512), _rows(tm, 512), _full((1, 512)), _full((1, 512)), sq,
         _full((1, 1024)), sq, _full((m_len, 1024)), _full((m_len, 1024)), sq],
        [_rows(tm, 1024)] * 6,
        [_sds((s_len, 1024), BF16), _sds((s_len, 1024), F32), _sds((s_len, 1024), BF16),
         _sds((s_len, 1024), BF16), _sds((s_len, 1024), BF16), _sds((s_len, 1024), F32)],
        [], (x, att, rec, g_oa, g_ol, w_out, g_cross, wq, kx, vx, wo), "arbitrary", comm)


def _load_weights_once(pairs):
    @pl.when(pl.program_id(0) == 0)
    def _():
        for hbm, vmem in pairs:
            pltpu.sync_copy(hbm, vmem)


FF_CHUNKS = [(0, 1280), (1280, D_FF)]


def _f_ffn(x2, tgt, g_ffn, g_final, wg, wu, wd, tm):
    s_len = x2.shape[0]

    def body(x2_ref, t_ref, gf_ref, gfin_ref, wg_hbm, wu_hbm, wd_hbm,
             hf_ref, g_ref, u_ref, a_ref, dx3_ref, loss_ref, dgfin_ref, wg_ref, wu_ref, wd_ref):
        _load_weights_once([(wg_hbm, wg_ref), (wu_hbm, wu_ref), (wd_hbm, wd_ref)])

        @pl.when(pl.program_id(0) == 0)
        def _():
            loss_ref[...] = jnp.zeros_like(loss_ref)
            dgfin_ref[...] = jnp.zeros_like(dgfin_ref)

        x2v = x2_ref[...]
        hf = (x2v * _rinv(x2v) * gf_ref[...]).astype(BF16)
        hf_ref[...] = hf
        x3 = x2v
        for c0, c1 in FF_CHUNKS:
            gv = _dot_nt(hf, wg_ref[c0:c1, :])
            uv = _dot_nt(hf, wu_ref[c0:c1, :])
            av = (gv * jax.nn.sigmoid(gv) * uv).astype(BF16)
            g_ref[:, c0:c1] = gv.astype(BF16)
            u_ref[:, c0:c1] = uv.astype(BF16)
            a_ref[:, c0:c1] = av
            x3 = x3 + _dot(av, wd_ref[c0:c1, :])
        r3 = _rinv(x3)
        yh = x3 * r3
        gfin = gfin_ref[...]
        err = yh * gfin - t_ref[...]
        loss_ref[...] += jnp.full((1, 128), 0.5 / D_MODEL, F32) * jnp.sum(err * err)
        dy = err * (1.0 / D_MODEL)
        dgfin_ref[...] += jnp.sum(dy * yh, axis=0, keepdims=True)
        dyh = dy * gfin
        dx3_ref[...] = r3 * (dyh - yh * jnp.mean(dyh * yh, axis=-1, keepdims=True))

    vec = _full((1, 1024))
    return pl.pallas_call(
        body, name="f_ffn", grid=(s_len // tm,),
        in_specs=[_rows(tm, 1024), _rows(tm, 1024), vec, vec, _any(), _any(), _any()],
        out_specs=[_rows(tm, 1024), _rows(tm, D_FF), _rows(tm, D_FF), _rows(tm, D_FF),
                   _rows(tm, 1024), _full((1, 128)), vec],
        out_shape=[_sds((s_len, 1024), BF16)] + [_sds((s_len, D_FF), BF16)] * 3
                  + [_sds((s_len, 1024), F32), _sds((1, 128), F32), _sds((1, 1024), F32)],
        scratch_shapes=[pltpu.VMEM((D_FF, 1024), BF16)] * 3,
        compiler_params=_cp("arbitrary"))(x2, tgt, g_ffn, g_final, wg, wu, wd)


def _b_ffn(dx3, x2, gact, uact, g_ffn, wg, wu, wd, tm):
    s_len = x2.shape[0]

    def body(dx3_ref, x2_ref, g_ref, u_ref, gf_ref, wg_hbm, wu_hbm, wd_hbm,
             dg_ref, du_ref, dx2_ref, dgf_ref, wg_ref, wu_ref, wd_ref):
        _load_weights_once([(wg_hbm, wg_ref), (wu_hbm, wu_ref), (wd_hbm, wd_ref)])

        @pl.when(pl.program_id(0) == 0)
        def _():
            dgf_ref[...] = jnp.zeros_like(dgf_ref)

        dx3v = dx3_ref[...]
        dx3b = dx3v.astype(BF16)
        dhf = jnp.zeros(dx3v.shape, F32)
        for c0, c1 in FF_CHUNKS:
            da = _dot_nt(dx3b, wd_ref[c0:c1, :])
            gv = g_ref[:, c0:c1].astype(F32)
            uv = u_ref[:, c0:c1].astype(F32)
            sg = jax.nn.sigmoid(gv)
            dub = (da * gv * sg).astype(BF16)
            dgb = (da * uv * (sg * (1.0 + gv * (1.0 - sg)))).astype(BF16)
            du_ref[:, c0:c1] = dub
            dg_ref[:, c0:c1] = dgb
            dhf = dhf + _dot(dgb, wg_ref[c0:c1, :]) + _dot(dub, wu_ref[c0:c1, :])
        dx, dgf = _rms_bwd(dhf, x2_ref[...], gf_ref[...])
        dx2_ref[...] = dx3v + dx
        dgf_ref[...] += dgf

    vec = _full((1, 1024))
    return pl.pallas_call(
        body, name="b_ffn", grid=(s_len // tm,),
        in_specs=[_rows(tm, 1024), _rows(tm, 1024), _rows(tm, D_FF), _rows(tm, D_FF), vec,
                  _any(), _any(), _any()],
        out_specs=[_rows(tm, D_FF), _rows(tm, D_FF), _rows(tm, 1024), vec],
        out_shape=[_sds((s_len, D_FF), BF16)] * 2 + [_sds((s_len, 1024), F32), _sds((1, 1024), F32)],
        scratch_shapes=[pltpu.VMEM((D_FF, 1024), BF16)] * 3,
        compiler_params=_cp("arbitrary"))(dx3, x2, gact, uact, g_ffn, wg, wu, wd)


def _b_mid(dx2, qx, x1, att, rec, kx, vx, wo, wq, w_out, g_cross, g_oa, g_ol, tm, comm=None):
    s_len = x1.shape[0]
    m_len = kx.shape[0]

    def body(dx2_ref, qx_ref, x1_ref, att_ref, rec_ref, kx_ref, vx_ref, wo_ref, wq_ref, wout_ref,
             gc_ref, goa_ref, gol_ref,
             dqx_ref, dx1_ref, datt_ref, drec_ref, dkx_ref, dvx_ref, dgc_ref, dgoa_ref, dgol_ref):
        @pl.when(pl.program_id(0) == 0)
        def _():
            for r in (dkx_ref, dvx_ref, dgc_ref, dgoa_ref, dgol_ref):
                r[...] = jnp.zeros_like(r)

        dx2v = dx2_ref[...]
        dox = _dot_nt(dx2v.astype(BF16), wo_ref[...])
        for h in range(X_HEADS):
            sl = slice(h * X_HEAD_DIM, (h + 1) * X_HEAD_DIM)
            q = qx_ref[:, sl]
            p, l = _xattn_probs(q, kx_ref[:, sl])
            pn = p * (1.0 / l)
            dob = dox[:, sl].astype(BF16)
            dp = _dot_nt(dob, vx_ref[:, sl])
            dvx_ref[:, sl] += _dot_tn(pn.astype(BF16), dob)
            ds = pn * (dp - jnp.sum(dp * pn, axis=-1, keepdims=True))
            dsb = (ds * X_SCALE).astype(BF16)
            dqx_ref[:, sl] = _dot(dsb, kx_ref[:, sl]).astype(BF16)
            dkx_ref[:, sl] += _dot_tn(dsb, q)
        dhc = _dot_nt(dqx_ref[...], wq_ref[...])
        dx, dgc = _rms_bwd(dhc, x1_ref[...], gc_ref[...])
        dx1 = dx2v + dx
        dx1_ref[...] = dx1
        dgc_ref[...] += dgc
        dmg = _dot_nt(dx1.astype(BF16), wout_ref[...])
        da, dgoa = _rms_bwd(dmg[:, 0:D_ATT], att_ref[...], goa_ref[...])
        datt_ref[...] = da
        dgoa_ref[...] += dgoa
        dr, dgol = _rms_bwd(dmg[:, D_ATT:1024], rec_ref[...], gol_ref[...])
        drec_ref[...] = dr
        dgol_ref[...] += dgol

    sq = _full((1024, 1024))
    mk = _full((m_len, 1024))
    return _call(
        body, "b_mid", (s_len // tm,),
        [_rows(tm, 1024), _rows(tm, 1024), _rows(tm, 1024), _rows(tm, 512), _rows(tm, 512), mk, mk,
         sq, sq, sq, _full((1, 1024)), _full((1, 512)), _full((1, 512))],
        [_rows(tm, 1024), _rows(tm, 1024), _rows(tm, 512), _rows(tm, 512), mk, mk,
         _full((1, 1024)), _full((1, 512)), _full((1, 512))],
        [_sds((s_len, 1024), BF16), _sds((s_len, 1024), F32), _sds((s_len, 512), F32),
         _sds((s_len, 512), F32), _sds((m_len, 1024), F32), _sds((m_len, 1024), F32),
         _sds((1, 1024), F32), _sds((1, 512), F32), _sds((1, 512), F32)],
        [], (dx2, qx, x1, att, rec, kx, vx, wo, wq, w_out, g_cross, g_oa, g_ol), "arbitrary", comm)


def _b_mem(dkx, dvx, mem, mn, g_mem, wk, wv):
    def body(dkx_ref, dvx_ref, mem_ref, mn_ref, g_ref, wk_ref, wv_ref, dwk_ref, dwv_ref, dgm_ref,
             dwkb_ref, dwvb_ref):
        dkb = dkx_ref[...].astype(BF16)
        dvb = dvx_ref[...].astype(BF16)
        dwk = _dot_tn(mn_ref[...], dkb)
        dwv = _dot_tn(mn_ref[...], dvb)
        dwk_ref[...] = dwk
        dwv_ref[...] = dwv
        dwkb_ref[...] = dwk.astype(BF16)
        dwvb_ref[...] = dwv.astype(BF16)
        dmn = _dot_nt(dkb, wk_ref[...]) + _dot_nt(dvb, wv_ref[...])
        mv = mem_ref[...]
        dgm_ref[...] = jnp.sum(dmn * (mv * _rinv(mv)), axis=0, keepdims=True)

    return pl.pallas_call(
        body, name="b_mem",
        out_shape=[_sds((1024, 1024), F32), _sds((1024, 1024), F32), _sds((1, 1024), F32),
                   _sds((1024, 1024), BF16), _sds((1024, 1024), BF16)],
        compiler_params=_cp())(dkx, dvx, mem, mn, g_mem, wk, wv)


def _b_lru(drec, hs, u, xg, conv_w, wrg, brg, wig, big, lam, tl, comm=None):
    s_len = xg.shape[0]
    nt = s_len // tl

    def body(drec_ref, hs_ref, hsp_ref, u_ref, xg_ref, cw_ref, wrg_ref, brg_ref, wig_ref, big_ref, l_ref,
             dxg_ref, dwrg_ref, dwig_ref, dbrg_ref, dbig_ref, dlam_ref, dcw_ref, dcb_ref,
             hbuf, abuf, dubuf, c_sc, d_sc, lam_sc, lcar, wacc_r, wacc_i):
        i = pl.program_id(0)
        tt = nt - 1 - i

        @pl.when(i == 0)
        def _():
            for r in (wacc_r, wacc_i, dbrg_ref, dbig_ref, dlam_ref, dcw_ref, dcb_ref):
                r[...] = jnp.zeros_like(r)
            abuf[tl:tl + 8, :] = jnp.zeros((8, D_LRU), F32)
            dubuf[tl:tl + 8, :] = jnp.zeros((8, D_LRU), F32)
            lcar[...] = jnp.zeros((8, D_LRU), F32)

        xu0 = xg_ref[:, 0:D_LRU]
        hsv = hs_ref[...]
        uv = u_ref[...]
        hbuf[8:8 + tl, :] = hsv
        hbuf[0:8, :] = jnp.where(tt > 0, hsp_ref[...], 0.0)
        hshift = hbuf[pl.ds(7, tl), :]
        wrg_v = wrg_ref[...]
        wig_v = wig_ref[...]
        lamv = l_ref[...]
        ub, r, ig, sp, a, mult = _lru_gates(uv, wrg_v, brg_ref[...], wig_v, big_ref[...], lamv)
        abuf[0:tl, :] = a
        c_sc[...] = abuf[pl.ds(1, tl), :]
        gel, dgel = _gelu_and_grad(xg_ref[:, D_LRU:2 * D_LRU])
        drv = drec_ref[...]
        d_sc[...] = drv * gel
        dxg_ref[:, D_LRU:2 * D_LRU] = (drv * hsv * dgel).astype(BF16)

        def grp(k, lnext):
            off = pl.multiple_of((tl // 8 - 1 - k) * 8, 8)
            l8 = _rscan8(c_sc[pl.ds(off, 8), :], d_sc[pl.ds(off, 8), :], lnext)
            lam_sc[pl.ds(off, 8), :] = l8
            return l8[0:1, :]

        lcar[0:1, :] = lax.fori_loop(0, tl // 8, grp, lcar[0:1, :])
        abuf[tl:tl + 8, :] = a[0:8, :]
        db = lam_sc[...]
        da = db * hshift
        dmult = db * (ig * uv)
        dig = db * mult * uv
        du = db * mult * ig
        dla = da * a - dmult * (a * a) / mult
        dlam_ref[...] += jnp.sum(dla * (-LRU_C) * r, axis=0, keepdims=True)
        dzr = dla * (-LRU_C * sp) * r * (1.0 - r)
        dzi = dig * ig * (1.0 - ig)
        dzrb = dzr.astype(BF16)
        dzib = dzi.astype(BF16)
        du = du + _dot_nt(dzrb, wrg_v) + _dot_nt(dzib, wig_v)
        wacc_r[...] += _dot_tn(ub, dzrb)
        wacc_i[...] += _dot_tn(ub, dzib)
        dbrg_ref[...] += jnp.sum(dzr, axis=0, keepdims=True)
        dbig_ref[...] += jnp.sum(dzi, axis=0, keepdims=True)
        dcb_ref[...] += jnp.sum(du, axis=0, keepdims=True)
        dubuf[0:tl, :] = du
        dxu0 = jnp.zeros((tl, D_LRU), F32)
        for j in range(4):
            dsh = dubuf[pl.ds(3 - j, tl), :]
            dxu0 = dxu0 + cw_ref[j:j + 1, :] * dsh
            dcw_ref[j:j + 1, :] += jnp.sum(xu0 * dsh, axis=0, keepdims=True)
        dubuf[tl:tl + 8, :] = du[0:8, :]
        dxg_ref[:, 0:D_LRU] = dxu0.astype(BF16)

        @pl.when(i == nt - 1)
        def _():
            dlam_ref[...] = dlam_ref[...] * (-jax.nn.sigmoid(-lamv))
            for n in range(LRU_BLOCKS):
                blk = slice(n * LRU_BLOCK, (n + 1) * LRU_BLOCK)
                dwrg_ref[n] = wacc_r[blk, blk]
                dwig_ref[n] = wacc_i[blk, blk]

    def rev(n):
        return pl.BlockSpec((tl, n), lambda i: (nt - 1 - i, 0))

    prev8 = pl.BlockSpec((8, D_LRU), lambda i: (jnp.maximum((nt - 1 - i) * (tl // 8) - 1, 0), 0))
    vec = _full((1, D_LRU))
    sq = _full((D_LRU, D_LRU))
    blocks_shape = (LRU_BLOCKS, LRU_BLOCK, LRU_BLOCK)
    blocks = _full(blocks_shape)
    return _call(
        body, "b_lru", (nt,),
        [rev(D_LRU), rev(D_LRU), prev8, rev(D_LRU), rev(1024), _full((4, D_LRU)), sq, vec, sq, vec, vec],
        [rev(1024), blocks, blocks, vec, vec, vec, _full((4, D_LRU)), vec],
        [_sds((s_len, 1024), BF16), _sds(blocks_shape, F32), _sds(blocks_shape, F32),
         _sds((1, D_LRU), F32), _sds((1, D_LRU), F32), _sds((1, D_LRU), F32),
         _sds((4, D_LRU), F32), _sds((1, D_LRU), F32)],
        [pltpu.VMEM((tl + 8, D_LRU), F32)] * 3 + [pltpu.VMEM((tl, D_LRU), F32)] * 3
        + [pltpu.VMEM((8, D_LRU), F32)] + [pltpu.VMEM((D_LRU, D_LRU), F32)] * 2,
        (drec, hs, hs, u, xg, conv_w, wrg, brg, wig, big, lam), "arbitrary", comm)


def _b_attn(qkv_pad, att, datt, frow, comm=None):
    s_len = datt.shape[0]
    nb = s_len // QB
    n_pair = ATT_HEADS // 2
    pair_w = 2 * HEAD_DIM

    def body(q_ref, k0, k1, k2, v0, v1, v2, o_ref, do_ref, frow_ref, dq_ref, dkv_ref, dfrow_ref,
             bias_sc, dt_sc, acc_sc):
        t = pl.program_id(0)

        @pl.when(t == 0)
        def _():
            _bias_table(frow_ref, bias_sc)
            dt_sc[...] = jnp.zeros_like(dt_sc)
            acc_sc[...] = jnp.zeros_like(acc_sc)

        @pl.when(t < nb)
        def _():
            var = jnp.minimum(t, N_BIAS - 1)
            even = _even_lanes()
            for hp in range(n_pair):
                cs = slice(hp * pair_w, (hp + 1) * pair_w)
                qt = q_ref[:, cs]
                kts = [k0[:, cs], k1[:, cs], k2[:, cs]]
                vts = [v0[:, cs], v1[:, cs], v2[:, cs]]
                dot = do_ref[:, cs]
                dd = dot * o_ref[:, cs]
                dos_pair, dsbs, pbs, dqs = None, [], [], []
                for e in range(2):
                    keep = even if e == 0 else jnp.logical_not(even)
                    qm = jnp.where(keep, qt, 0)
                    p = _att_probs(qm, kts, bias_sc[var, 2 * hp + e])
                    inv = 1.0 / jnp.sum(p, axis=-1, keepdims=True)
                    dos = jnp.where(keep, dot * inv, 0.0)
                    delta = jnp.sum(jnp.where(keep, dd, 0.0), axis=-1, keepdims=True) * inv
                    dp = jnp.concatenate([_dot_nt(dos.astype(BF16), v) for v in vts], axis=1)
                    ds = p * (dp - delta)
                    dt_sc[2 * hp + e] += ds
                    dsb = ds.astype(BF16)
                    dq = _dot(dsb[:, 0:QB], kts[0])
                    for j in (1, 2):
                        dq = dq + _dot(dsb[:, j * QB:(j + 1) * QB], kts[j])
                    dqs.append(dq)
                    dsbs.append(dsb)
                    pbs.append(p.astype(BF16))
                    dos_pair = dos if e == 0 else dos_pair + dos
                dq_ref[:, cs] = (jnp.where(even, dqs[0], dqs[1]) * ATT_SCALE).astype(BF16)
                qtt = qt.astype(F32).T.astype(BF16)
                dost = dos_pair.T.astype(BF16)
                for j in range(3):
                    slot = (t + 1 + j) % 3
                    js = slice(j * QB, (j + 1) * QB)
                    for e in range(2):
                        hr = slice(e * HEAD_DIM, (e + 1) * HEAD_DIM)
                        acc_sc[slot, hp, hr, :] += _dot(qtt[hr], dsbs[e][:, js])
                        acc_sc[slot, n_pair + hp, hr, :] += _dot(dost[hr], pbs[e][:, js])

        done = (t + 1) % 3

        @pl.when(t >= 2)
        def _():
            for i in range(2 * n_pair):
                dkv_ref[:, i * pair_w:(i + 1) * pair_w] = acc_sc[done, i].T.astype(BF16)

        acc_sc[done] = jnp.zeros((2 * n_pair, pair_w, QB), F32)

        @pl.when(t == nb + 1)
        def _():
            row = lax.broadcasted_iota(jnp.int32, (8, ROLL_W), 0)
            pad = jnp.zeros((8, ROLL_W - KB), F32)
            for h in range(ATT_HEADS):
                acc8 = jnp.concatenate([dt_sc[h, 0:8, :], pad], axis=1)
                for a1 in range(1, QB // 8):
                    blk = jnp.concatenate([dt_sc[h, 8 * a1:8 * a1 + 8, :], pad], axis=1)
                    acc8 = acc8 + pltpu.roll(blk, ROLL_W - 8 * a1, 1)
                for k in range(3):
                    acc8 = jnp.where(((row >> k) & 1) == 1, pltpu.roll(acc8, ROLL_W - (1 << k), 1), acc8)
                dfrow_ref[h:h + 1, :] = jnp.sum(acc8, axis=0, keepdims=True)

    clamp = lambda t: jnp.minimum(t, nb - 1)
    qrows = pl.BlockSpec((QB, D_ATT), lambda t: (clamp(t), 0))
    return _call(
        body, "b_attn", (nb + 2,),
        _att_in_specs(clamp) + [qrows, qrows, _full((ATT_HEADS, ROLL_W))],
        [qrows, pl.BlockSpec((QB, 2 * D_ATT), lambda t: (jnp.maximum(t - 2, 0), 0)),
         _full((ATT_HEADS, ROLL_W))],
        [_sds((s_len, D_ATT), BF16), _sds((s_len, 2 * D_ATT), BF16), _sds((ATT_HEADS, ROLL_W), F32)],
        [pltpu.VMEM((N_BIAS, ATT_HEADS, QB, KB), F32), pltpu.VMEM((ATT_HEADS, QB, KB), F32),
         pltpu.VMEM((3, 2 * n_pair, pair_w, QB), F32)],
        (*([qkv_pad] * 7), att, datt, frow), "arbitrary", comm)


def _flush_grad(steps, acc, accb, out_hbm, outb_hbm):
    @pl.when(pl.program_id(0) == steps - 1)
    def _():
        accb[...] = acc[...].astype(BF16)
        pltpu.sync_copy(acc, out_hbm)
        pltpu.sync_copy(accb, outb_hbm)


def _b_win(dq, dkv, dxg, h, ts):
    s_len = h.shape[0]
    steps = s_len // ts

    def body(dq_ref, dkv_ref, dxg_ref, h_ref, dw_hbm, dwb_hbm, acc, accb):
        @pl.when(pl.program_id(0) == 0)
        def _():
            acc[...] = jnp.zeros_like(acc)

        dproj = jnp.concatenate([dq_ref[...], dkv_ref[...], dxg_ref[...]], axis=1)
        hv = h_ref[...]
        for s in range(N_SHARD):
            acc[s] += _dot_tn(hv, dproj[:, s * IN_SH:(s + 1) * IN_SH])
        _flush_grad(steps, acc, accb, dw_hbm, dwb_hbm)

    shape = (N_SHARD, 1024, IN_SH)
    return pl.pallas_call(
        body, name="b_win", grid=(steps,),
        in_specs=[_rows(ts, 512), _rows(ts, 1024), _rows(ts, 1024), _rows(ts, 1024)],
        out_specs=[_any()] * 2, out_shape=[_sds(shape, F32), _sds(shape, BF16)],
        scratch_shapes=[pltpu.VMEM(shape, F32), pltpu.VMEM(shape, BF16)],
        compiler_params=_cp("arbitrary"))(dq, dkv, dxg, h)


def _b_inproj(dq, dkv, dxg, x, dx1, g_mix, w_in_g, tm, comm=None):
    s_len = x.shape[0]

    def body(dq_ref, dkv_ref, dxg_ref, x_ref, dx1_ref, g_ref, w_hbm, gx_ref, dgm_ref, w_ref):
        _load_w_in_once(w_hbm, w_ref)

        @pl.when(pl.program_id(0) == 0)
        def _():
            dgm_ref[...] = jnp.zeros_like(dgm_ref)

        dproj = jnp.concatenate([dq_ref[...], dkv_ref[...], dxg_ref[...]], axis=1)
        dh = _dot_nt(dproj, w_ref[...])
        dx, dgm = _rms_bwd(dh, x_ref[...], g_ref[...])
        gx_ref[...] = dx1_ref[...] + dx
        dgm_ref[...] += dgm

    return _call(
        body, "b_inproj", (s_len // tm,),
        [_rows(tm, 512), _rows(tm, 1024), _rows(tm, 1024), _rows(tm, 1024), _rows(tm, 1024),
         _full((1, 1024)), _any()],
        [_rows(tm, 1024), _full((1, 1024))],
        [_sds((s_len, 1024), F32), _sds((1, 1024), F32)],
        [pltpu.VMEM((1024, D_IN), BF16)], (dq, dkv, dxg, x, dx1, g_mix, w_in_g), "arbitrary", comm)


def _mm_tn(xa, ya, name, ts):
    s_len, k = xa.shape
    n = ya.shape[1]

    steps = s_len // ts

    def body(x_ref, y_ref, o_hbm, ob_hbm, acc, accb):
        @pl.when(pl.program_id(0) == 0)
        def _():
            acc[...] = jnp.zeros_like(acc)
        acc[...] += _dot_tn(x_ref[...].astype(BF16), y_ref[...].astype(BF16))
        _flush_grad(steps, acc, accb, o_hbm, ob_hbm)

    return pl.pallas_call(
        body, name=name, grid=(steps,), in_specs=[_rows(ts, k), _rows(ts, n)],
        out_specs=[_any()] * 2, out_shape=[_sds((k, n), F32), _sds((k, n), BF16)],
        scratch_shapes=[pltpu.VMEM((k, n), F32), pltpu.VMEM((k, n), BF16)],
        compiler_params=_cp("arbitrary"))(xa, ya)


def _frow_from_rel_bias(rb):
    hi = jnp.broadcast_to(rb[:, 256:257], (ATT_HEADS, 385))
    mid = rb[:, 1:256][:, ::-1]
    lo = jnp.broadcast_to(rb[:, 0:1], (ATT_HEADS, 128))
    wrap = jnp.broadcast_to(rb[:, 256:257], (ATT_HEADS, ROLL_W - KB))
    return jnp.concatenate([hi, mid, lo, wrap], axis=1)


def _rel_bias_grad_from_dfrow(df):
    g256 = jnp.sum(df[:, 0:385], axis=1, keepdims=True) + jnp.sum(df[:, KB:ROLL_W], axis=1, keepdims=True)
    mid = df[:, 385:640][:, ::-1]
    g0 = jnp.sum(df[:, 640:KB], axis=1, keepdims=True)
    return jnp.concatenate([g0, mid, g256], axis=1)


def _block_diag(w):
    eye = jnp.eye(8, dtype=w.dtype)
    return (w[:, :, None, :] * eye[:, None, :, None]).reshape(D_LRU, D_LRU)


MID = ['w_out', 'wq_c', 'wk_c', 'wv_c', 'wo_c']
TRANSPOSED = ['w_gate', 'w_up']
AG_IN_INPROJ = ['w_out', 'wq_c', 'wk_c']
AG_IN_ATTN = ['wv_c', 'wo_c', 'w_gate']
AG_IN_LRU = ['w_up']
AG_IN_MID = ['w_down']
RS_IN_MID = ['w_gate', 'w_up']
RS_IN_LRU = ['w_down']
RS_IN_ATTN = MID


def _local_step(x, mem, tgt, p, gw, shards=None, chip=None):
    s_len = x.shape[0]
    tm = min(256, s_len)
    tmb = min(512, s_len)
    tl = min(512, s_len)
    frow = _frow_from_rel_bias(p['rel_bias'])
    wrg = _block_diag(p['w_rg']).astype(BF16)
    wig = _block_diag(p['w_ig']).astype(BF16)
    gw = dict(gw)

    big, bigb, recv, part, sib = {}, {}, {}, {}, {}

    def ag(names):
        return [] if shards is None else [("ag", [shards[n] for n in names])]

    def rs(names):
        return [] if shards is None else [("rs", [bigb[n] for n in names])]

    def swap(names):
        return [] if shards is None else [("swap", [part[n] for n in names])]

    def reduce_own(names):
        if shards is not None:
            for n in names:
                part[n] = _sum_parts(big[n], recv[n], chip, "sum_" + n)

    h, qkv_pad, xg, *got = _f_inproj(x, p['g_mix'], gw['w_in'], tmb, ag(AG_IN_INPROJ))
    gw.update(zip(AG_IN_INPROJ, got))
    att, *got = _f_attn(qkv_pad, frow, ag(AG_IN_ATTN))
    gw.update(zip(AG_IN_ATTN, got))
    rec, u, hs, *got = _f_lru(xg, p['conv_w'], p['conv_b'], wrg, p['b_rg'], wig, p['b_ig'], p['lru_L'], tl,
                              ag(AG_IN_LRU))
    gw.update(zip(AG_IN_LRU, got))
    w_out = gw['w_out'].reshape(1024, 1024)
    wq = gw['wq_c'].reshape(1024, 1024)
    wk = gw['wk_c'].reshape(1024, 1024)
    wv = gw['wv_c'].reshape(1024, 1024)
    wo = gw['wo_c'].reshape(1024, 1024)
    mn, kx, vx = _f_mem(mem, p['g_mem'], wk, wv)
    mg, x1, hc, qx, ox, x2, *got = _f_mid(x, att, rec, p['g_out_attn'], p['g_out_lru'], w_out, p['g_cross'],
                                          wq, kx, vx, wo, tmb, ag(AG_IN_MID))
    gw.update(zip(AG_IN_MID, got))
    ffn_w = [gw[n].reshape(D_FF, 1024) for n in ('w_gate', 'w_up', 'w_down')]
    hf, gact, uact, aact, dx3, loss, dg_final = _f_ffn(x2, tgt, p['g_ffn'], p['g_final'], *ffn_w, tmb)

    ts = min(1024, s_len)
    dgact, duact, dx2, dg_ffn = _b_ffn(dx3, x2, gact, uact, p['g_ffn'], *ffn_w, tm)
    big['w_gate'], bigb['w_gate'] = _mm_tn(dgact, hf, "dw_gate", ts)
    big['w_up'], bigb['w_up'] = _mm_tn(duact, hf, "dw_up", ts)
    big['w_down'], bigb['w_down'] = _mm_tn(aact, dx3, "dw_down", ts)
    for n in ('w_gate', 'w_up', 'w_down'):
        big[n] = big[n].reshape(N_SHARD, FF_SH, 1024)
        bigb[n] = bigb[n].reshape(N_SHARD, FF_SH, 1024)

    dqx, dx1, datt, drec, dkx, dvx, dg_cross, dg_oa, dg_ol, *got = _b_mid(
        dx2, qx, x1, att, rec, kx, vx, wo, wq, w_out, p['g_cross'], p['g_out_attn'], p['g_out_lru'], tmb,
        rs(RS_IN_MID))
    recv.update(zip(RS_IN_MID, got))
    reduce_own(RS_IN_MID)
    dwk, dwv, dg_mem, dwkb, dwvb = _b_mem(dkx, dvx, mem, mn, p['g_mem'], wk, wv)
    big['wk_c'], bigb['wk_c'] = dwk, dwkb
    big['wv_c'], bigb['wv_c'] = dwv, dwvb
    big['w_out'], bigb['w_out'] = _mm_tn(mg, dx1, "dw_out", ts)
    big['wq_c'], bigb['wq_c'] = _mm_tn(hc, dqx, "dw_q", ts)
    big['wo_c'], bigb['wo_c'] = _mm_tn(ox, dx2, "dw_o", ts)
    for n in MID:
        big[n] = big[n].reshape(N_SHARD, 256, 1024)
        bigb[n] = bigb[n].reshape(N_SHARD, 256, 1024)

    dxg, dwrg, dwig, dbrg, dbig, dlam, dcw, dcb, *got = _b_lru(
        drec, hs, u, xg, p['conv_w'], wrg, p['b_rg'], wig, p['b_ig'], p['lru_L'], tl,
        rs(RS_IN_LRU) + swap(RS_IN_MID))
    recv.update(zip(RS_IN_LRU, got))
    sib.update(zip(RS_IN_MID, got[len(RS_IN_LRU):]))
    reduce_own(RS_IN_LRU)
    small = {
        'conv_w': dcw, 'conv_b': dcb, 'w_rg': dwrg, 'b_rg': dbrg, 'w_ig': dwig, 'b_ig': dbig, 'lru_L': dlam,
        'g_out_attn': dg_oa, 'g_out_lru': dg_ol, 'g_cross': dg_cross, 'g_mem': dg_mem, 'g_ffn': dg_ffn,
        'g_final': dg_final,
    }
    names = [n for n in SMALL if n in small]
    gather = [] if shards is None else [("ag8", [_pack_small(names, [small[n] for n in names], loss)])]
    dq, dkv, dfrow, *got = _b_attn(qkv_pad, att, datt, frow, rs(RS_IN_ATTN) + swap(RS_IN_LRU) + gather)
    recv.update(zip(RS_IN_ATTN, got))
    sib.update(zip(RS_IN_LRU, got[len(RS_IN_ATTN):]))
    packs = got[-1] if gather else None
    reduce_own(RS_IN_ATTN)
    small['rel_bias'] = _rel_bias_grad_from_dfrow(dfrow)
    big['w_in'], bigb['w_in'] = _b_win(dq, dkv, dxg, h, ts)
    grad_x, small['g_mix'], *got = _b_inproj(dq, dkv, dxg, x, dx1, p['g_mix'], gw['w_in'], tmb,
                                             rs(['w_in']) + swap(RS_IN_ATTN))
    recv.update(zip(['w_in'], got))
    sib.update(zip(RS_IN_ATTN, got[1:]))
    reduce_own(['w_in'])
    return loss, grad_x, small, big, part, sib, packs


def _cast_shards(ws):
    def body(*refs):
        n = len(refs) // 2
        for src, dst in zip(refs[:n], refs[n:]):
            dst[...] = src[...].astype(BF16)

    return pl.pallas_call(body, name="cast_shards", out_shape=[_sds(w.shape, BF16) for w in ws],
                          compiler_params=_cp())(*ws)


def _sum_parts(own4, recv3, chip, name):
    _, r, c = own4.shape
    steps = _ew_steps(r)
    tr = r // steps

    def body(chip_ref, own_ref, rc_ref, o_ref):
        o_ref[...] = ((own_ref[0] + rc_ref[0].astype(F32)) + rc_ref[1].astype(F32)) + rc_ref[2].astype(F32)

    grid_spec = pltpu.PrefetchScalarGridSpec(
        num_scalar_prefetch=1, grid=(steps,),
        in_specs=[pl.BlockSpec((1, tr, c), lambda i, ch: (ch[0], i, 0)),
                  pl.BlockSpec((3, tr, c), lambda i, ch: (0, i, 0))],
        out_specs=pl.BlockSpec((tr, c), lambda i, ch: (i, 0)))
    return pl.pallas_call(body, name=name, grid_spec=grid_spec, out_shape=_sds((r, c), F32),
                          compiler_params=_cp("parallel"))(chip, own4, recv3)


def _adamw_math(w, g, m, v):
    m = ADAM_B1 * m + (1.0 - ADAM_B1) * g
    v = ADAM_B2 * v + (1.0 - ADAM_B2) * (g * g)
    m_hat = m / (1.0 - ADAM_B1 ** ADAM_STEP)
    v_hat = v / (1.0 - ADAM_B2 ** ADAM_STEP)
    delta = -ADAM_LR * (m_hat / (jnp.sqrt(v_hat) + ADAM_EPS) + ADAM_WD * w)
    return delta, m, v


def _final_adamw(pa, pb, w, m, v, name):
    r, c = w.shape
    steps = _ew_steps(r)
    tr = r // steps

    def body(pa_ref, pb_ref, w_ref, m_ref, v_ref, g_ref, d_ref, nm_ref, nv_ref):
        g = pa_ref[...] + pb_ref[...]
        g_ref[...] = g
        d_ref[...], nm_ref[...], nv_ref[...] = _adamw_math(w_ref[...], g, m_ref[...], v_ref[...])

    return pl.pallas_call(
        body, name=name, grid=(steps,), in_specs=[_rows(tr, c)] * 5, out_specs=[_rows(tr, c)] * 4,
        out_shape=[_sds((r, c), F32)] * 4, compiler_params=_cp("parallel"))(pa, pb, w, m, v)


def _pack_put(ref, name, val_ref):
    r = _pack_rows()[name]
    shape = val_ref.shape
    if len(shape) == 3:
        for b in range(shape[0]):
            ref[r:r + shape[1], b * shape[2]:(b + 1) * shape[2]] = val_ref[b]
    elif shape[1] == 2 * PACK_W:
        ref[r:r + 1, :] = val_ref[:, 0:PACK_W]
        ref[r + 1:r + 2, :] = val_ref[:, PACK_W:2 * PACK_W]
    else:
        ref[r:r + shape[0], 0:shape[1]] = val_ref[...]


def _pack_get(ref, name, shape):
    r = _pack_rows()[name]
    if len(shape) == 3:
        return jnp.stack([ref[r:r + shape[1], b * shape[2]:(b + 1) * shape[2]] for b in range(shape[0])])
    if shape[1] == 2 * PACK_W:
        return jnp.concatenate([ref[r:r + 1, :], ref[r + 1:r + 2, :]], axis=1)
    return ref[r:r + shape[0], 0:shape[1]]


def _pack_small(names, g, loss):
    n = len(g)

    def body(*refs):
        pack = refs[n + 1]
        pack[...] = jnp.zeros_like(pack)
        for a, name in enumerate(names):
            _pack_put(pack, name, refs[a])
        _pack_put(pack, 'loss', refs[n])

    return pl.pallas_call(body, name="pack_small", out_shape=_sds((PACK_ROWS, PACK_W), F32),
                          compiler_params=_cp())(*g, loss)


def _all_peers():
    x, y, c = _mesh_pos()
    peers = []
    for k in range(1, 8):
        px = 1 - x if k & 4 else x
        py = 1 - y if k & 2 else y
        pc = 1 - c if k & 1 else c
        peers.append(((px, py, pc), 4 * px + 2 * py + pc))
    return peers, 4 * x + 2 * y + c


def _ag8_copies(ins, outs, sems):
    send_sems, recv_sems, loc_sems = sems
    n = len(ins)
    peers, me = _all_peers()

    def remote(k, j, slot):
        return pltpu.make_async_remote_copy(
            src_ref=ins[k], dst_ref=outs[k].at[slot], send_sem=send_sems.at[k, j], recv_sem=recv_sems.at[k, j],
            device_id=peers[j][0], device_id_type=MESH_ID)

    def local(k):
        return pltpu.make_async_copy(ins[k], outs[k].at[me], loc_sems.at[k])

    def start():
        for k in range(n):
            local(k).start()
            for j in range(7):
                remote(k, j, me).start()

    def wait():
        for k in range(n):
            for j in range(7):
                remote(k, j, peers[j][1]).wait_recv()
        for k in range(n):
            for j in range(7):
                remote(k, j, me).wait_send()
            local(k).wait()

    return start, _no_forward, wait


def _ar_late(names, g):
    n = len(g)

    def body(*refs):
        tot_ref, pack, buf, send_sems, recv_sems = refs[n:]
        peers, me = _all_peers()

        def remote(j, slot):
            return pltpu.make_async_remote_copy(
                src_ref=pack, dst_ref=buf.at[slot], send_sem=send_sems.at[j], recv_sem=recv_sems.at[j],
                device_id=peers[j][0], device_id_type=MESH_ID)

        pack[...] = jnp.zeros_like(pack)
        for a, name in enumerate(names):
            _pack_put(pack, name, refs[a])
        for j in range(7):
            remote(j, me).start()
        buf[me] = pack[...]
        for j in range(7):
            remote(j, peers[j][1]).wait_recv()
        for j in range(7):
            remote(j, me).wait_send()
        tot = buf[0]
        for d in range(1, 8):
            tot = tot + buf[d]
        tot_ref[...] = tot

    return pl.pallas_call(
        body, name="ar_late", out_shape=_sds((LATE_ROWS, PACK_W), F32),
        scratch_shapes=[pltpu.VMEM((LATE_ROWS, PACK_W), F32), pltpu.VMEM((8, LATE_ROWS, PACK_W), F32),
                        pltpu.SemaphoreType.DMA((7,)), pltpu.SemaphoreType.DMA((7,))],
        compiler_params=_cp())(*g)


def _adamw_small(packs, late_tot, g_shapes, loss_shape, w, m, v):
    n = len(w)

    def body(*refs):
        packs_ref, late_ref = refs[0], refs[1]
        w_refs, m_refs, v_refs = (refs[2 + i * n:2 + (i + 1) * n] for i in range(3))
        o0 = 3 * n + 2
        go, do, mo, vo = (refs[o0 + i * n:o0 + (i + 1) * n] for i in range(4))
        loss_out, tot_ref = refs[o0 + 4 * n], refs[o0 + 4 * n + 1]
        x, y, _ = _mesh_pos()
        tot = packs_ref[0]
        for d in range(1, 8):
            tot = tot + packs_ref[d]
        tot_ref[...] = tot
        tot_ref[0:LATE_ROWS, :] += late_ref[...]
        loss_out[...] = _pack_get(tot_ref, 'loss', loss_shape)
        for a, name in enumerate(SMALL):
            if name == 'conv_w':
                r = _pack_rows()[name]
                ga = tot_ref[r:r + g_shapes[a][0], pl.ds(pl.multiple_of((2 * x + y) * 128, 128), 128)]
            else:
                ga = _pack_get(tot_ref, name, g_shapes[a])
            go[a][...] = ga
            do[a][...], mo[a][...], vo[a][...] = _adamw_math(w_refs[a][...], ga, m_refs[a][...], v_refs[a][...])

    out_shape = [_sds(a.shape, F32) for a in w] * 4 + [_sds(loss_shape, F32)]
    return pl.pallas_call(body, name="adamw_small", out_shape=out_shape,
                          scratch_shapes=[pltpu.VMEM((PACK_ROWS, PACK_W), F32)],
                          compiler_params=_cp())(packs, late_tot, *w, *m, *v)


PACK_W = 512
PACK_ROWS = 160
LATE = ['g_mix', 'rel_bias']
LATE_ROWS = 32


def _pack_rows():
    rows, r = {}, 0
    for name in ['g_mix', 'g_cross', 'g_mem', 'g_ffn', 'g_final']:
        rows[name] = r
        r += 2
    for name in ['conv_b', 'b_rg', 'b_ig', 'lru_L', 'g_out_attn', 'g_out_lru']:
        rows[name] = r
        r += 1
    rows['conv_w'] = r
    rows['loss'] = r + 4
    rows['rel_bias'] = 24
    rows['w_rg'] = 32
    rows['w_ig'] = 32 + LRU_BLOCK
    assert r + 5 <= 24 and rows['w_ig'] + LRU_BLOCK == PACK_ROWS
    assert rows['g_mix'] + 2 <= LATE_ROWS and rows['rel_bias'] + 8 <= LATE_ROWS
    return rows


INPUT_NAMES = (['x', 'mem'] + WEIGHTS + ['loss_target'] + ['m_' + n for n in WEIGHTS] + ['v_' + n for n in WEIGHTS])


def kernel(x, mem, g_mix, w_in, rel_bias, conv_w, conv_b, w_rg, b_rg, w_ig, b_ig, lru_L, g_out_attn, g_out_lru, w_out, g_cross, g_mem, wq_c, wk_c, wv_c, wo_c, g_ffn, w_gate, w_up, w_down, g_final, loss_target, m_g_mix, m_w_in, m_rel_bias, m_conv_w, m_conv_b, m_w_rg, m_b_rg, m_w_ig, m_b_ig, m_lru_L, m_g_out_attn, m_g_out_lru, m_w_out, m_g_cross, m_g_mem, m_wq_c, m_wk_c, m_wv_c, m_wo_c, m_g_ffn, m_w_gate, m_w_up, m_w_down, m_g_final, v_g_mix, v_w_in, v_rel_bias, v_conv_w, v_conv_b, v_w_rg, v_b_rg, v_w_ig, v_b_ig, v_lru_L, v_g_out_attn, v_g_out_lru, v_w_out, v_g_cross, v_g_mem, v_wq_c, v_wk_c, v_wv_c, v_wo_c, v_g_ffn, v_w_gate, v_w_up, v_w_down, v_g_final):
    a = dict(zip(INPUT_NAMES, (x, mem, g_mix, w_in, rel_bias, conv_w, conv_b, w_rg, b_rg, w_ig, b_ig, lru_L, g_out_attn, g_out_lru, w_out, g_cross, g_mem, wq_c, wk_c, wv_c, wo_c, g_ffn, w_gate, w_up, w_down, g_final, loss_target, m_g_mix, m_w_in, m_rel_bias, m_conv_w, m_conv_b, m_w_rg, m_b_rg, m_w_ig, m_b_ig, m_lru_L, m_g_out_attn, m_g_out_lru, m_w_out, m_g_cross, m_g_mem, m_wq_c, m_wk_c, m_wv_c, m_wo_c, m_g_ffn, m_w_gate, m_w_up, m_w_down, m_g_final, v_g_mix, v_w_in, v_rel_bias, v_conv_w, v_conv_b, v_w_rg, v_b_rg, v_w_ig, v_b_ig, v_lru_L, v_g_out_attn, v_g_out_lru, v_w_out, v_g_cross, v_g_mem, v_wq_c, v_wk_c, v_wv_c, v_wo_c, v_g_ffn, v_w_gate, v_w_up, v_w_down, v_g_final)))
    chip = 2 * lax.axis_index("x") + lax.axis_index("y")

    def shard(name):
        arr = a[name][0]
        return jnp.swapaxes(arr, 0, 1) if name[2:] in TRANSPOSED or name in TRANSPOSED else arr

    shards = dict(zip(BIG, _cast_shards([shard(n) for n in BIG])))
    w_in_g, conv_w_g = _comm_only("ag_w_in", [("ag", [shards['w_in']]), ("agf", [a['conv_w'][0]])])
    conv_w_full = conv_w_g.transpose(1, 0, 2).reshape(4, D_LRU)

    p = {n: a[n] for n in SMALL}
    p['rel_bias'] = a['rel_bias'][0]
    p['w_rg'] = a['w_rg'][0]
    p['w_ig'] = a['w_ig'][0]
    p['conv_w'] = conv_w_full
    p['g_final'] = a['g_final'][None, :]
    chip_arr = jnp.reshape(chip, (1,)).astype(jnp.int32)
    loss_part, grad_x, small, _, part, sib, packs = _local_step(
        a['x'][0], a['mem'][0], a['loss_target'][0], p, {'w_in': w_in_g}, shards, chip_arr)

    sib['w_in'], = _comm_only("swap_w_in", [("swap", [part['w_in']])])
    out = {}
    for n in BIG:
        res = _final_adamw(part[n], sib[n], shard(n), shard('m_' + n), shard('v_' + n), "adamw_" + n)
        out[n] = [jnp.swapaxes(r, 0, 1) for r in res] if n in TRANSPOSED else res

    def natural(arr):
        return arr[0] if arr.ndim >= 3 else (arr[None, :] if arr.ndim == 1 else arr)

    small_out = _adamw_small(packs, _ar_late(LATE, [small[n] for n in LATE]), [small[n].shape for n in SMALL],
                             loss_part.shape, *[[natural(a[pre + n]) for n in SMALL] for pre in ('', 'm_', 'v_')])
    ns = len(SMALL)
    loss = small_out[4 * ns][0, 0]

    def leaf(i, n):
        if n in BIG:
            return out[n][i][None]
        return small_out[i * ns + SMALL.index(n)].reshape(a[n].shape)

    return (loss, grad_x[None], *[leaf(i, n) for i in range(4) for n in WEIGHTS])
```

```python
import math

import jax
import jax.numpy as jnp
from jax import lax
from jax.experimental import pallas as pl
from jax.experimental.pallas import tpu as pltpu

F32 = jnp.float32
BF16 = jnp.bfloat16

D_MODEL = 1024
D_ATT = 512
D_LRU = 512
HEAD_DIM = 64
ATT_HEADS = 8
CHUNK = 64
LEFT_CHUNKS = 8
MAX_REL = 128
X_HEADS = 4
X_HEAD_DIM = 256
N_SHARD = 4
IN_SH = 640
D_IN = N_SHARD * IN_SH
FF_SH = 704
D_FF = N_SHARD * FF_SH
EPS = 1e-6
LRU_C = 8.0
LRU_BLOCKS = 8
LRU_BLOCK = 64
QB = 256
KB = 768
ROLL_W = 1024
NEG = -1e30
ATT_SCALE = HEAD_DIM ** -0.5
X_SCALE = X_HEAD_DIM ** -0.5

ADAM_LR = 0.001
ADAM_B1 = 0.9
ADAM_B2 = 0.999
ADAM_EPS = 1e-08
ADAM_WD = 0.01
ADAM_STEP = 10

VMEM_LIMIT_V7X = 56 * 1024 * 1024
BF16_ROWS = 16


def _ew_steps(rows):
    return max(s for s in (2, 1) if rows % (s * BF16_ROWS) == 0)
MESH_ID = pl.DeviceIdType.MESH

WEIGHTS = ['g_mix', 'w_in', 'rel_bias', 'conv_w', 'conv_b', 'w_rg', 'b_rg', 'w_ig', 'b_ig', 'lru_L',
           'g_out_attn', 'g_out_lru', 'w_out', 'g_cross', 'g_mem', 'wq_c', 'wk_c', 'wv_c', 'wo_c',
           'g_ffn', 'w_gate', 'w_up', 'w_down', 'g_final']
BIG = ['w_in', 'w_out', 'wq_c', 'wk_c', 'wv_c', 'wo_c', 'w_gate', 'w_up', 'w_down']
SMALL = [n for n in WEIGHTS if n not in BIG]


def _sds(shape, dtype):
    return jax.ShapeDtypeStruct(shape, dtype)


def _cp(*sem):
    return pltpu.CompilerParams(dimension_semantics=sem or None, vmem_limit_bytes=VMEM_LIMIT_V7X)


def _rows(tm, n):
    return pl.BlockSpec((tm, n), lambda i: (i, 0))


def _full(shape):
    nd = len(shape)
    return pl.BlockSpec(shape, lambda i: (0,) * nd)


def _dot(a, b):
    return jnp.dot(a, b, preferred_element_type=F32)


def _dot_nt(a, b):
    return lax.dot_general(a, b, (((1,), (1,)), ((), ())), preferred_element_type=F32)


def _dot_tn(a, b):
    return lax.dot_general(a, b, (((0,), (0,)), ((), ())), preferred_element_type=F32)


def _rinv(x):
    return lax.rsqrt(jnp.mean(x * x, axis=-1, keepdims=True) + EPS)


def _rms_bwd(dy, x, g):
    r = _rinv(x)
    yh = x * r
    dyh = dy * g
    dx = r * (dyh - yh * jnp.mean(dyh * yh, axis=-1, keepdims=True))
    return dx, jnp.sum(dy * yh, axis=0, keepdims=True)


def _gelu(x):
    c = math.sqrt(2.0 / math.pi)
    t = jnp.tanh(c * (x + 0.044715 * x * x * x))
    return 0.5 * x * (1.0 + t)


def _gelu_and_grad(x):
    c = math.sqrt(2.0 / math.pi)
    t = jnp.tanh(c * (x + 0.044715 * x * x * x))
    g = 0.5 * x * (1.0 + t)
    dg = 0.5 * (1.0 + t) + 0.5 * x * (1.0 - t * t) * c * (1.0 + 3.0 * 0.044715 * x * x)
    return g, dg


def _neg_expm1(z):
    series = -z * (1.0 + z * (0.5 + z * ((1.0 / 6.0) + z * (1.0 / 24.0))))
    return jnp.where(z > -0.03, series, 1.0 - jnp.exp(z))


def _lru_gates(u, wrg, brg, wig, big, lam):
    ub = u.astype(BF16)
    r = jax.nn.sigmoid(_dot(ub, wrg) + brg)
    ig = jax.nn.sigmoid(_dot(ub, wig) + big)
    sp = jnp.maximum(-lam, 0.0) + jnp.log1p(jnp.exp(-jnp.abs(lam)))
    la = -LRU_C * r * sp
    a = jnp.exp(la)
    mult = jnp.sqrt(jnp.maximum(_neg_expm1(2.0 * la), 0.0))
    return ub, r, ig, sp, a, mult


def _scan8(a8, b8, hprev):
    row = lax.broadcasted_iota(jnp.int32, a8.shape, 0)
    aa, bb = a8, b8
    for d in (1, 2, 4):
        a_s = pltpu.roll(aa, d, 0)
        b_s = pltpu.roll(bb, d, 0)
        m = row >= d
        bb = jnp.where(m, aa * b_s + bb, bb)
        aa = jnp.where(m, aa * a_s, aa)
    return aa * hprev + bb


def _rscan8(c8, d8, lnext):
    row = lax.broadcasted_iota(jnp.int32, c8.shape, 0)
    cc, dd = c8, d8
    for d in (1, 2, 4):
        c_s = pltpu.roll(cc, 8 - d, 0)
        d_s = pltpu.roll(dd, 8 - d, 0)
        m = row < 8 - d
        dd = jnp.where(m, cc * d_s + dd, dd)
        cc = jnp.where(m, cc * c_s, cc)
    return cc * lnext + dd


def _mesh_pos():
    return lax.axis_index("x"), lax.axis_index("y"), lax.axis_index("c")


def _other_chips(x, y):
    return [(1 - x, y), (x, 1 - y), (1 - x, 1 - y)]


def _no_forward():
    pass


def _ag_full_copies(ins, outs, sems):
    send_sems, recv_sems, loc_sems = sems
    n = len(ins)
    x, y, c = _mesh_pos()
    mine = 2 * x + y
    chips = _other_chips(x, y)

    def remote(k, j, slot):
        px, py = chips[j]
        return pltpu.make_async_remote_copy(
            src_ref=ins[k], dst_ref=outs[k].at[slot], send_sem=send_sems.at[k, j], recv_sem=recv_sems.at[k, j],
            device_id=(px, py, c), device_id_type=MESH_ID)

    def local(k):
        return pltpu.make_async_copy(ins[k], outs[k].at[mine], loc_sems.at[k])

    def start():
        for k in range(n):
            local(k).start()
            for j in range(3):
                remote(k, j, mine).start()

    def wait():
        for k in range(n):
            for j, (px, py) in enumerate(chips):
                remote(k, j, 2 * px + py).wait_recv()
        for k in range(n):
            for j in range(3):
                remote(k, j, mine).wait_send()
            local(k).wait()

    return start, _no_forward, wait


def _ag_copies(ins, outs, sems):
    send_sems, recv_sems, fsend_sems, frecv_sems, loc_sems = sems
    n = len(ins)
    x, y, c = _mesh_pos()
    mine = 2 * x + y
    chips = _other_chips(x, y)

    def half(ref, hc):
        r = ref.shape[0] // 2
        return ref.at[pl.ds(pl.multiple_of(hc * r, 16), r)]

    def ici(k, j, slot):
        px, py = chips[j]
        return pltpu.make_async_remote_copy(
            src_ref=half(ins[k], c), dst_ref=half(outs[k].at[slot], c),
            send_sem=send_sems.at[k, j], recv_sem=recv_sems.at[k, j],
            device_id=(px, py, c), device_id_type=MESH_ID)

    def d2d(k, j, hc):
        px, py = chips[j]
        part = half(outs[k].at[2 * px + py], hc)
        return pltpu.make_async_remote_copy(
            src_ref=part, dst_ref=part, send_sem=fsend_sems.at[k, j], recv_sem=frecv_sems.at[k, j],
            device_id=(x, y, 1 - c), device_id_type=MESH_ID)

    def local(k):
        return pltpu.make_async_copy(ins[k], outs[k].at[mine], loc_sems.at[k])

    def start():
        for k in range(n):
            local(k).start()
            for j in range(3):
                ici(k, j, mine).start()

    def forward():
        for k in range(n):
            for j, (px, py) in enumerate(chips):
                ici(k, j, 2 * px + py).wait_recv()
                d2d(k, j, c).start()

    def wait():
        for k in range(n):
            for j in range(3):
                d2d(k, j, 1 - c).wait_recv()
        for k in range(n):
            for j in range(3):
                d2d(k, j, c).wait_send()
                ici(k, j, mine).wait_send()
            local(k).wait()

    return start, forward, wait


def _rs_copies(ins, outs, sems):
    send_sems, recv_sems = sems
    n = len(ins)
    x, y, c = _mesh_pos()
    chips = _other_chips(x, y)

    def remote(k, j):
        px, py = chips[j]
        return pltpu.make_async_remote_copy(
            src_ref=ins[k].at[2 * px + py], dst_ref=outs[k].at[j],
            send_sem=send_sems.at[k, j], recv_sem=recv_sems.at[k, j],
            device_id=(px, py, c), device_id_type=MESH_ID)

    def start():
        for k in range(n):
            for j in range(3):
                remote(k, j).start()

    def wait():
        for k in range(n):
            for j in range(3):
                remote(k, j).wait_recv()
        for k in range(n):
            for j in range(3):
                remote(k, j).wait_send()

    return start, _no_forward, wait


def _swap_copies(ins, outs, sems):
    send_sems, recv_sems = sems
    x, y, c = _mesh_pos()
    copies = [pltpu.make_async_remote_copy(
        src_ref=ins[k], dst_ref=outs[k], send_sem=send_sems.at[k], recv_sem=recv_sems.at[k],
        device_id=(x, y, 1 - c), device_id_type=MESH_ID) for k in range(len(ins))]

    def start():
        for cp in copies:
            cp.start()

    def wait():
        for cp in copies:
            cp.wait()

    return start, _no_forward, wait


def _comm_plan(groups):
    plan, arrs, shapes, sems = [], [], [], []
    for kind, group in groups:
        k = len(group)
        arrs += group
        per_peer = pltpu.SemaphoreType.DMA((k, 3))
        if kind == "ag":
            shapes += [_sds((N_SHARD,) + w.shape, w.dtype) for w in group]
            gsems = [per_peer] * 4 + [pltpu.SemaphoreType.DMA((k,))]
            maker = _ag_copies
        elif kind == "agf":
            shapes += [_sds((N_SHARD,) + w.shape, w.dtype) for w in group]
            gsems = [per_peer] * 2 + [pltpu.SemaphoreType.DMA((k,))]
            maker = _ag_full_copies
        elif kind == "ag8":
            shapes += [_sds((8,) + g.shape, g.dtype) for g in group]
            gsems = [pltpu.SemaphoreType.DMA((k, 7))] * 2 + [pltpu.SemaphoreType.DMA((k,))]
            maker = _ag8_copies
        elif kind == "rs":
            shapes += [_sds((3,) + g.shape[1:], g.dtype) for g in group]
            gsems = [pltpu.SemaphoreType.DMA((k, 3)), pltpu.SemaphoreType.DMA((k, 3))]
            maker = _rs_copies
        else:
            shapes += [_sds(g.shape, g.dtype) for g in group]
            gsems = [pltpu.SemaphoreType.DMA((k,)), pltpu.SemaphoreType.DMA((k,))]
            maker = _swap_copies
        plan.append((maker, k, len(gsems)))
        sems += gsems
    return plan, arrs, shapes, sems


def _comm_fns(plan, cins, couts, sems):
    fns, a, s = [], 0, 0
    for maker, k, ns in plan:
        fns.append(maker(cins[a:a + k], couts[a:a + k], sems[s:s + ns]))
        a += k
        s += ns

    def start():
        for st, _, _ in fns:
            st()

    def forward():
        for _, fw, _ in fns:
            fw()

    def wait():
        for _, _, wt in fns:
            wt()

    return start, forward, wait


def _call(body, name, grid, in_specs, out_specs, out_shape, scratch, args, sem, comm=None):
    if not comm:
        return pl.pallas_call(body, name=name, grid=grid, in_specs=in_specs, out_specs=out_specs,
                              out_shape=out_shape, scratch_shapes=scratch, compiler_params=_cp(sem))(*args)
    plan, c_arrs, c_shapes, c_sems = _comm_plan(comm)
    k = len(c_arrs)
    n_in, n_out, n_scr = len(in_specs), len(out_specs), len(scratch)
    last = grid[0] - 1
    fwd_step = max(1, (2 * last) // 3)

    def wrapped(*refs):
        ins, cins = refs[:n_in], refs[n_in:n_in + k]
        o0 = n_in + k
        outs, couts = refs[o0:o0 + n_out], refs[o0 + n_out:o0 + n_out + k]
        s0 = o0 + n_out + k
        start, forward, wait = _comm_fns(plan, cins, couts, refs[s0 + n_scr:])
        pl.when(pl.program_id(0) == 0)(start)
        pl.when(pl.program_id(0) == fwd_step)(forward)
        body(*ins, *outs, *refs[s0:s0 + n_scr])
        pl.when(pl.program_id(0) == last)(wait)

    return pl.pallas_call(
        wrapped, name=name, grid=grid, in_specs=list(in_specs) + [_any()] * k,
        out_specs=list(out_specs) + [_any()] * k, out_shape=list(out_shape) + c_shapes,
        scratch_shapes=list(scratch) + c_sems, compiler_params=_cp(sem))(*args, *c_arrs)


def _comm_only(name, comm):
    plan, c_arrs, c_shapes, c_sems = _comm_plan(comm)
    k = len(c_arrs)

    def body(*refs):
        start, forward, wait = _comm_fns(plan, refs[:k], refs[k:2 * k], refs[2 * k:])
        start()
        forward()
        wait()

    return pl.pallas_call(body, name=name, in_specs=[_any()] * k, out_specs=[_any()] * k, out_shape=c_shapes,
                          scratch_shapes=c_sems, compiler_params=_cp())(*c_arrs)


def _any():
    return pl.BlockSpec(memory_space=pl.ANY)


def _load_w_in_once(w_hbm, w_ref):
    @pl.when(pl.program_id(0) == 0)
    def _():
        for s in range(N_SHARD):
            pltpu.sync_copy(w_hbm.at[s], w_ref.at[:, pl.ds(s * IN_SH, IN_SH)])


def _f_inproj(x, g_mix, w_in_g, tm, comm=None):
    s_len = x.shape[0]
    pad_rows = LEFT_CHUNKS * CHUNK
    npad = pad_rows // tm

    def body(x_ref, g_ref, w_hbm, h_ref, qkv_ref, xg_ref, w_ref):
        i = pl.program_id(0)
        _load_w_in_once(w_hbm, w_ref)

        @pl.when(i < npad)
        def _():
            qkv_ref[...] = jnp.zeros_like(qkv_ref)

        @pl.when(i >= npad)
        def _():
            xv = x_ref[...]
            h = (xv * _rinv(xv) * g_ref[...]).astype(BF16)
            h_ref[...] = h
            proj = _dot(h, w_ref[...])
            qkv_ref[:, 0:D_ATT] = (proj[:, 0:D_ATT] * ATT_SCALE).astype(BF16)
            qkv_ref[:, D_ATT:3 * D_ATT] = proj[:, D_ATT:3 * D_ATT].astype(BF16)
            xg_ref[...] = proj[:, 3 * D_ATT:D_IN]

    def tok(n):
        return pl.BlockSpec((tm, n), lambda i: (jnp.maximum(i - npad, 0), 0))

    return _call(
        body, "f_inproj", (s_len // tm + npad,),
        [tok(1024), _full((1, 1024)), _any()],
        [tok(1024), _rows(tm, 1536), tok(1024)],
        [_sds((s_len, 1024), BF16), _sds((s_len + pad_rows, 1536), BF16), _sds((s_len, 1024), F32)],
        [pltpu.VMEM((1024, D_IN), BF16)], (x, g_mix, w_in_g), "arbitrary", comm)


N_BIAS = 3


def _bias_table(frow_ref, bias_sc):
    qa = lax.broadcasted_iota(jnp.int32, (QB, KB), 0) // CHUNK
    kcol = lax.broadcasted_iota(jnp.int32, (QB, KB), 1)
    kb = kcol // CHUNK
    band = jnp.where((kb >= qa) & (kb - qa <= LEFT_CHUNKS), 0.0, NEG).astype(F32)
    for h in range(ATT_HEADS):
        row = jnp.broadcast_to(frow_ref[h:h + 1, :], (QB, ROLL_W))
        toep = pltpu.roll(row, 0, 1, stride=1, stride_axis=0)
        gen = toep[:, 0:KB] + band
        bias_sc[N_BIAS - 1, h] = gen
        for v in range(N_BIAS - 1):
            pad_keys = LEFT_CHUNKS * CHUNK - v * QB
            bias_sc[v, h] = gen + jnp.where(kcol < pad_keys, NEG, 0.0).astype(F32)


def _even_lanes():
    return lax.broadcasted_iota(jnp.int32, (1, 2 * HEAD_DIM), 1) < HEAD_DIM


def _att_probs(qm, kts, bias):
    s = jnp.concatenate([_dot_nt(qm, k) for k in kts], axis=1) + bias
    return jnp.exp(s - jnp.max(s, axis=-1, keepdims=True))


def _att_in_specs(clamp):
    def spec(j, col):
        return pl.BlockSpec((QB, D_ATT), lambda i: (clamp(i) + j, col))
    return [spec(2, 0), spec(0, 1), spec(1, 1), spec(2, 1), spec(0, 2), spec(1, 2), spec(2, 2)]


def _f_attn(qkv_pad, frow, comm=None):
    s_len = qkv_pad.shape[0] - LEFT_CHUNKS * CHUNK
    nb = s_len // QB

    def body(q_ref, k0, k1, k2, v0, v1, v2, frow_ref, o_ref, bias_sc):
        i = pl.program_id(0)

        @pl.when(i == 0)
        def _():
            _bias_table(frow_ref, bias_sc)

        var = jnp.minimum(i, N_BIAS - 1)
        even = _even_lanes()
        for hp in range(ATT_HEADS // 2):
            cs = slice(hp * 2 * HEAD_DIM, (hp + 1) * 2 * HEAD_DIM)
            qt = q_ref[:, cs]
            kts = [k0[:, cs], k1[:, cs], k2[:, cs]]
            vts = [v0[:, cs], v1[:, cs], v2[:, cs]]
            res = []
            for e in range(2):
                keep = even if e == 0 else jnp.logical_not(even)
                pb = _att_probs(jnp.where(keep, qt, 0), kts, bias_sc[var, 2 * hp + e]).astype(BF16)
                r = _dot(pb, jnp.concatenate([jnp.where(keep, v, 1) for v in vts], axis=0))
                res.append(r / pltpu.roll(r, HEAD_DIM, 1))
            o_ref[:, cs] = jnp.where(even, res[0], res[1])

    return _call(
        body, "f_attn", (nb,),
        _att_in_specs(lambda i: i) + [_full((ATT_HEADS, ROLL_W))],
        [_rows(QB, D_ATT)], [_sds((s_len, D_ATT), F32)],
        [pltpu.VMEM((N_BIAS, ATT_HEADS, QB, KB), F32)], (*([qkv_pad] * 7), frow), "arbitrary", comm)


def _f_lru(xg, conv_w, conv_b, wrg, brg, wig, big, lam, tl, comm=None):
    s_len = xg.shape[0]

    def body(xg_ref, cw_ref, cb_ref, wrg_ref, brg_ref, wig_ref, big_ref, l_ref,
             rec_ref, u_ref, hs_ref, xbuf, a_sc, b_sc, hcar):
        i = pl.program_id(0)

        @pl.when(i == 0)
        def _():
            xbuf[0:8, :] = jnp.zeros((8, D_LRU), F32)
            hcar[...] = jnp.zeros((8, D_LRU), F32)

        xu0 = xg_ref[:, 0:D_LRU]
        xbuf[8:8 + tl, :] = xu0
        u = cb_ref[...] + cw_ref[0:1, :] * xbuf[pl.ds(5, tl), :]
        for j in range(1, 4):
            u = u + cw_ref[j:j + 1, :] * xbuf[pl.ds(5 + j, tl), :]
        xbuf[0:8, :] = xu0[tl - 8:tl, :]
        u_ref[...] = u
        _, _, ig, _, a, mult = _lru_gates(u, wrg_ref[...], brg_ref[...], wig_ref[...], big_ref[...], l_ref[...])
        a_sc[...] = a
        b_sc[...] = mult * (ig * u)

        def grp(g, hprev):
            off = pl.multiple_of(g * 8, 8)
            h8 = _scan8(a_sc[pl.ds(off, 8), :], b_sc[pl.ds(off, 8), :], hprev)
            hs_ref[pl.ds(off, 8), :] = h8
            return h8[7:8, :]

        hcar[0:1, :] = lax.fori_loop(0, tl // 8, grp, hcar[0:1, :])
        rec_ref[...] = hs_ref[...] * _gelu(xg_ref[:, D_LRU:2 * D_LRU])

    vec = _full((1, D_LRU))
    return _call(
        body, "f_lru", (s_len // tl,),
        [_rows(tl, 1024), _full((4, D_LRU)), vec, _full((D_LRU, D_LRU)), vec, _full((D_LRU, D_LRU)), vec, vec],
        [_rows(tl, D_LRU)] * 3, [_sds((s_len, D_LRU), F32)] * 3,
        [pltpu.VMEM((tl + 8, D_LRU), F32), pltpu.VMEM((tl, D_LRU), F32),
         pltpu.VMEM((tl, D_LRU), F32), pltpu.VMEM((8, D_LRU), F32)],
        (xg, conv_w, conv_b, wrg, brg, wig, big, lam), "arbitrary", comm)


def _f_mem(mem, g_mem, wk, wv):
    def body(mem_ref, g_ref, wk_ref, wv_ref, mn_ref, kx_ref, vx_ref):
        mv = mem_ref[...]
        mn = (mv * _rinv(mv) * g_ref[...]).astype(BF16)
        mn_ref[...] = mn
        kx_ref[...] = _dot(mn, wk_ref[...]).astype(BF16)
        vx_ref[...] = _dot(mn, wv_ref[...]).astype(BF16)

    m = mem.shape[0]
    return pl.pallas_call(
        body, name="f_mem", out_shape=[_sds((m, 1024), BF16)] * 3,
        compiler_params=_cp())(mem, g_mem, wk, wv)


def _xattn_probs(q, k):
    s = _dot_nt(q, k) * X_SCALE
    m = jnp.max(s, axis=-1, keepdims=True)
    p = jnp.exp(s - m)
    return p, jnp.sum(p, axis=-1, keepdims=True)


def _f_mid(x, att, rec, g_oa, g_ol, w_out, g_cross, wq, kx, vx, wo, tm, comm=None):
    s_len = x.shape[0]
    m_len = kx.shape[0]

    def body(x_ref, att_ref, rec_ref, goa_ref, gol_ref, wout_ref, gc_ref, wq_ref, kx_ref, vx_ref, wo_ref,
             mg_ref, x1_ref, hc_ref, qx_ref, ox_ref, x2_ref):
        av = att_ref[...]
        rv = rec_ref[...]
        mg_ref[:, 0:D_ATT] = (av * _rinv(av) * goa_ref[...]).astype(BF16)
        mg_ref[:, D_ATT:1024] = (rv * _rinv(rv) * gol_ref[...]).astype(BF16)
        x1 = x_ref[...] + _dot(mg_ref[...], wout_ref[...])
        x1_ref[...] = x1
        hc = (x1 * _rinv(x1) * gc_ref[...]).astype(BF16)
        hc_ref[...] = hc
        qx_ref[...] = _dot(hc, wq_ref[...]).astype(BF16)
        for h in range(X_HEADS):
            sl = slice(h * X_HEAD_DIM, (h + 1) * X_HEAD_DIM)
            p, l = _xattn_probs(qx_ref[:, sl], kx_ref[:, sl])
            ox_ref[:, sl] = (_dot(p.astype(BF16), vx_ref[:, sl]) / l).astype(BF16)
        x2_ref[...] = x1 + _dot(ox_ref[...], wo_ref[...])

    sq = _full((1024, 1024))
    return _call(
        body, "f_mid", (s_len // tm,),
        [_rows(tm, 1024), _rows(tm, 512), _rows(tm, 512), _full((1, 512)), _full((1, 512)), sq,
         _full((1, 1024)), sq, _full((m_len, 1024)), _full((m_len, 1024)), sq],
        [_rows(tm, 1024)] * 6,
        [_sds((s_len, 1024), BF16), _sds((s_len, 1024), F32), _sds((s_len, 1024), BF16),
         _sds((s_len, 1024), BF16), _sds((s_len, 1024), BF16), _sds((s_len, 1024), F32)],
        [], (x, att, rec, g_oa, g_ol, w_out, g_cross, wq, kx, vx, wo), "arbitrary", comm)


def _load_weights_once(pairs):
    @pl.when(pl.program_id(0) == 0)
    def _():
        for hbm, vmem in pairs:
            pltpu.sync_copy(hbm, vmem)


FF_CHUNKS = [(0, 1280), (1280, D_FF)]


def _f_ffn(x2, tgt, g_ffn, g_final, wg, wu, wd, tm):
    s_len = x2.shape[0]

    def body(x2_ref, t_ref, gf_ref, gfin_ref, wg_hbm, wu_hbm, wd_hbm,
             hf_ref, g_ref, u_ref, a_ref, dx3_ref, loss_ref, dgfin_ref, wg_ref, wu_ref, wd_ref):
        _load_weights_once([(wg_hbm, wg_ref), (wu_hbm, wu_ref), (wd_hbm, wd_ref)])

        @pl.when(pl.program_id(0) == 0)
        def _():
            loss_ref[...] = jnp.zeros_like(loss_ref)
            dgfin_ref[...] = jnp.zeros_like(dgfin_ref)

        x2v = x2_ref[...]
        hf = (x2v * _rinv(x2v) * gf_ref[...]).astype(BF16)
        hf_ref[...] = hf
        x3 = x2v
        for c0, c1 in FF_CHUNKS:
            gv = _dot_nt(hf, wg_ref[c0:c1, :])
            uv = _dot_nt(hf, wu_ref[c0:c1, :])
            av = (gv * jax.nn.sigmoid(gv) * uv).astype(BF16)
            g_ref[:, c0:c1] = gv.astype(BF16)
            u_ref[:, c0:c1] = uv.astype(BF16)
            a_ref[:, c0:c1] = av
            x3 = x3 + _dot(av, wd_ref[c0:c1, :])
        r3 = _rinv(x3)
        yh = x3 * r3
        gfin = gfin_ref[...]
        err = yh * gfin - t_ref[...]
        loss_ref[...] += jnp.full((1, 128), 0.5 / D_MODEL, F32) * jnp.sum(err * err)
        dy = err * (1.0 / D_MODEL)
        dgfin_ref[...] += jnp.sum(dy * yh, axis=0, keepdims=True)
        dyh = dy * gfin
        dx3_ref[...] = r3 * (dyh - yh * jnp.mean(dyh * yh, axis=-1, keepdims=True))

    vec = _full((1, 1024))
    return pl.pallas_call(
        body, name="f_ffn", grid=(s_len // tm,),
        in_specs=[_rows(tm, 1024), _rows(tm, 1024), vec, vec, _any(), _any(), _any()],
        out_specs=[_rows(tm, 1024), _rows(tm, D_FF), _rows(tm, D_FF), _rows(tm, D_FF),
                   _rows(tm, 1024), _full((1, 128)), vec],
        out_shape=[_sds((s_len, 1024), BF16)] + [_sds((s_len, D_FF), BF16)] * 3
                  + [_sds((s_len, 1024), F32), _sds((1, 128), F32), _sds((1, 1024), F32)],
        scratch_shapes=[pltpu.VMEM((D_FF, 1024), BF16)] * 3,
        compiler_params=_cp("arbitrary"))(x2, tgt, g_ffn, g_final, wg, wu, wd)


def _b_ffn(dx3, x2, gact, uact, g_ffn, wg, wu, wd, tm):
    s_len = x2.shape[0]

    def body(dx3_ref, x2_ref, g_ref, u_ref, gf_ref, wg_hbm, wu_hbm, wd_hbm,
             dg_ref, du_ref, dx2_ref, dgf_ref, wg_ref, wu_ref, wd_ref):
        _load_weights_once([(wg_hbm, wg_ref), (wu_hbm, wu_ref), (wd_hbm, wd_ref)])

        @pl.when(pl.program_id(0) == 0)
        def _():
            dgf_ref[...] = jnp.zeros_like(dgf_ref)

        dx3v = dx3_ref[...]
        dx3b = dx3v.astype(BF16)
        dhf = jnp.zeros(dx3v.shape, F32)
        for c0, c1 in FF_CHUNKS:
            da = _dot_nt(dx3b, wd_ref[c0:c1, :])
            gv = g_ref[:, c0:c1].astype(F32)
            uv = u_ref[:, c0:c1].astype(F32)
            sg = jax.nn.sigmoid(gv)
            dub = (da * gv * sg).astype(BF16)
            dgb = (da * uv * (sg * (1.0 + gv * (1.0 - sg)))).astype(BF16)
            du_ref[:, c0:c1] = dub
            dg_ref[:, c0:c1] = dgb
            dhf = dhf + _dot(dgb, wg_ref[c0:c1, :]) + _dot(dub, wu_ref[c0:c1, :])
        dx, dgf = _rms_bwd(dhf, x2_ref[...], gf_ref[...])
        dx2_ref[...] = dx3v + dx
        dgf_ref[...] += dgf

    vec = _full((1, 1024))
    return pl.pallas_call(
        body, name="b_ffn", grid=(s_len // tm,),
        in_specs=[_rows(tm, 1024), _rows(tm, 1024), _rows(tm, D_FF), _rows(tm, D_FF), vec,
                  _any(), _any(), _any()],
        out_specs=[_rows(tm, D_FF), _rows(tm, D_FF), _rows(tm, 1024), vec],
        out_shape=[_sds((s_len, D_FF), BF16)] * 2 + [_sds((s_len, 1024), F32), _sds((1, 1024), F32)],
        scratch_shapes=[pltpu.VMEM((D_FF, 1024), BF16)] * 3,
        compiler_params=_cp("arbitrary"))(dx3, x2, gact, uact, g_ffn, wg, wu, wd)


def _b_mid(dx2, qx, x1, att, rec, kx, vx, wo, wq, w_out, g_cross, g_oa, g_ol, tm, comm=None):
    s_len = x1.shape[0]
    m_len = kx.shape[0]

    def body(dx2_ref, qx_ref, x1_ref, att_ref, rec_ref, kx_ref, vx_ref, wo_ref, wq_ref, wout_ref,
             gc_ref, goa_ref, gol_ref,
             dqx_ref, dx1_ref, datt_ref, drec_ref, dkx_ref, dvx_ref, dgc_ref, dgoa_ref, dgol_ref):
        @pl.when(pl.program_id(0) == 0)
        def _():
            for r in (dkx_ref, dvx_ref, dgc_ref, dgoa_ref, dgol_ref):
                r[...] = jnp.zeros_like(r)

        dx2v = dx2_ref[...]
        dox = _dot_nt(dx2v.astype(BF16), wo_ref[...])
        for h in range(X_HEADS):
            sl = slice(h * X_HEAD_DIM, (h + 1) * X_HEAD_DIM)
            q = qx_ref[:, sl]
            p, l = _xattn_probs(q, kx_ref[:, sl])
            pn = p * (1.0 / l)
            dob = dox[:, sl].astype(BF16)
            dp = _dot_nt(dob, vx_ref[:, sl])
            dvx_ref[:, sl] += _dot_tn(pn.astype(BF16), dob)
            ds = pn * (dp - jnp.sum(dp * pn, axis=-1, keepdims=True))
            dsb = (ds * X_SCALE).astype(BF16)
            dqx_ref[:, sl] = _dot(dsb, kx_ref[:, sl]).astype(BF16)
            dkx_ref[:, sl] += _dot_tn(dsb, q)
        dhc = _dot_nt(dqx_ref[...], wq_ref[...])
        dx, dgc = _rms_bwd(dhc, x1_ref[...], gc_ref[...])
        dx1 = dx2v + dx
        dx1_ref[...] = dx1
        dgc_ref[...] += dgc
        dmg = _dot_nt(dx1.astype(BF16), wout_ref[...])
        da, dgoa = _rms_bwd(dmg[:, 0:D_ATT], att_ref[...], goa_ref[...])
        datt_ref[...] = da
        dgoa_ref[...] += dgoa
        dr, dgol = _rms_bwd(dmg[:, D_ATT:1024], rec_ref[...], gol_ref[...])
        drec_ref[...] = dr
        dgol_ref[...] += dgol

    sq = _full((1024, 1024))
    mk = _full((m_len, 1024))
    return _call(
        body, "b_mid", (s_len // tm,),
        [_rows(tm, 1024), _rows(tm, 1024), _rows(tm, 1024), _rows(tm, 512), _rows(tm, 512), mk, mk,
         sq, sq, sq, _full((1, 1024)), _full((1, 512)), _full((1, 512))],
        [_rows(tm, 1024), _rows(tm, 1024), _rows(tm, 512), _rows(tm, 512), mk, mk,
         _full((1, 1024)), _full((1, 512)), _full((1, 512))],
        [_sds((s_len, 1024), BF16), _sds((s_len, 1024), F32), _sds((s_len, 512), F32),
         _sds((s_len, 512), F32), _sds((m_len, 1024), F32), _sds((m_len, 1024), F32),
         _sds((1, 1024), F32), _sds((1, 512), F32), _sds((1, 512), F32)],
        [], (dx2, qx, x1, att, rec, kx, vx, wo, wq, w_out, g_cross, g_oa, g_ol), "arbitrary", comm)


def _b_mem(dkx, dvx, mem, mn, g_mem, wk, wv):
    def body(dkx_ref, dvx_ref, mem_ref, mn_ref, g_ref, wk_ref, wv_ref, dwk_ref, dwv_ref, dgm_ref,
             dwkb_ref, dwvb_ref):
        dkb = dkx_ref[...].astype(BF16)
        dvb = dvx_ref[...].astype(BF16)
        dwk = _dot_tn(mn_ref[...], dkb)
        dwv = _dot_tn(mn_ref[...], dvb)
        dwk_ref[...] = dwk
        dwv_ref[...] = dwv
        dwkb_ref[...] = dwk.astype(BF16)
        dwvb_ref[...] = dwv.astype(BF16)
        dmn = _dot_nt(dkb, wk_ref[...]) + _dot_nt(dvb, wv_ref[...])
        mv = mem_ref[...]
        dgm_ref[...] = jnp.sum(dmn * (mv * _rinv(mv)), axis=0, keepdims=True)

    return pl.pallas_call(
        body, name="b_mem",
        out_shape=[_sds((1024, 1024), F32), _sds((1024, 1024), F32), _sds((1, 1024), F32),
                   _sds((1024, 1024), BF16), _sds((1024, 1024), BF16)],
        compiler_params=_cp())(dkx, dvx, mem, mn, g_mem, wk, wv)


def _b_lru(drec, hs, u, xg, conv_w, wrg, brg, wig, big, lam, tl, comm=None):
    s_len = xg.shape[0]
    nt = s_len // tl

    def body(drec_ref, hs_ref, hsp_ref, u_ref, xg_ref, cw_ref, wrg_ref, brg_ref, wig_ref, big_ref, l_ref,
             dxg_ref, dwrg_ref, dwig_ref, dbrg_ref, dbig_ref, dlam_ref, dcw_ref, dcb_ref,
             hbuf, abuf, dubuf, c_sc, d_sc, lam_sc, lcar, wacc_r, wacc_i):
        i = pl.program_id(0)
        tt = nt - 1 - i

        @pl.when(i == 0)
        def _():
            for r in (wacc_r, wacc_i, dbrg_ref, dbig_ref, dlam_ref, dcw_ref, dcb_ref):
                r[...] = jnp.zeros_like(r)
            abuf[tl:tl + 8, :] = jnp.zeros((8, D_LRU), F32)
            dubuf[tl:tl + 8, :] = jnp.zeros((8, D_LRU), F32)
            lcar[...] = jnp.zeros((8, D_LRU), F32)

        xu0 = xg_ref[:, 0:D_LRU]
        hsv = hs_ref[...]
        uv = u_ref[...]
        hbuf[8:8 + tl, :] = hsv
        hbuf[0:8, :] = jnp.where(tt > 0, hsp_ref[...], 0.0)
        hshift = hbuf[pl.ds(7, tl), :]
        wrg_v = wrg_ref[...]
        wig_v = wig_ref[...]
        lamv = l_ref[...]
        ub, r, ig, sp, a, mult = _lru_gates(uv, wrg_v, brg_ref[...], wig_v, big_ref[...], lamv)
        abuf[0:tl, :] = a
        c_sc[...] = abuf[pl.ds(1, tl), :]
        gel, dgel = _gelu_and_grad(xg_ref[:, D_LRU:2 * D_LRU])
        drv = drec_ref[...]
        d_sc[...] = drv * gel
        dxg_ref[:, D_LRU:2 * D_LRU] = (drv * hsv * dgel).astype(BF16)

        def grp(k, lnext):
            off = pl.multiple_of((tl // 8 - 1 - k) * 8, 8)
            l8 = _rscan8(c_sc[pl.ds(off, 8), :], d_sc[pl.ds(off, 8), :], lnext)
            lam_sc[pl.ds(off, 8), :] = l8
            return l8[0:1, :]

        lcar[0:1, :] = lax.fori_loop(0, tl // 8, grp, lcar[0:1, :])
        abuf[tl:tl + 8, :] = a[0:8, :]
        db = lam_sc[...]
        da = db * hshift
        dmult = db * (ig * uv)
        dig = db * mult * uv
        du = db * mult * ig
        dla = da * a - dmult * (a * a) / mult
        dlam_ref[...] += jnp.sum(dla * (-LRU_C) * r, axis=0, keepdims=True)
        dzr = dla * (-LRU_C * sp) * r * (1.0 - r)
        dzi = dig * ig * (1.0 - ig)
        dzrb = dzr.astype(BF16)
        dzib = dzi.astype(BF16)
        du = du + _dot_nt(dzrb, wrg_v) + _dot_nt(dzib, wig_v)
        wacc_r[...] += _dot_tn(ub, dzrb)
        wacc_i[...] += _dot_tn(ub, dzib)
        dbrg_ref[...] += jnp.sum(dzr, axis=0, keepdims=True)
        dbig_ref[...] += jnp.sum(dzi, axis=0, keepdims=True)
        dcb_ref[...] += jnp.sum(du, axis=0, keepdims=True)
        dubuf[0:tl, :] = du
        dxu0 = jnp.zeros((tl, D_LRU), F32)
        for j in range(4):
            dsh = dubuf[pl.ds(3 - j, tl), :]
            dxu0 = dxu0 + cw_ref[j:j + 1, :] * dsh
            dcw_ref[j:j + 1, :] += jnp.sum(xu0 * dsh, axis=0, keepdims=True)
        dubuf[tl:tl + 8, :] = du[0:8, :]
        dxg_ref[:, 0:D_LRU] = dxu0.astype(BF16)

        @pl.when(i == nt - 1)
        def _():
            dlam_ref[...] = dlam_ref[...] * (-jax.nn.sigmoid(-lamv))
            for n in range(LRU_BLOCKS):
                blk = slice(n * LRU_BLOCK, (n + 1) * LRU_BLOCK)
                dwrg_ref[n] = wacc_r[blk, blk]
                dwig_ref[n] = wacc_i[blk, blk]

    def rev(n):
        return pl.BlockSpec((tl, n), lambda i: (nt - 1 - i, 0))

    prev8 = pl.BlockSpec((8, D_LRU), lambda i: (jnp.maximum((nt - 1 - i) * (tl // 8) - 1, 0), 0))
    vec = _full((1, D_LRU))
    sq = _full((D_LRU, D_LRU))
    blocks_shape = (LRU_BLOCKS, LRU_BLOCK, LRU_BLOCK)
    blocks = _full(blocks_shape)
    return _call(
        body, "b_lru", (nt,),
        [rev(D_LRU), rev(D_LRU), prev8, rev(D_LRU), rev(1024), _full((4, D_LRU)), sq, vec, sq, vec, vec],
        [rev(1024), blocks, blocks, vec, vec, vec, _full((4, D_LRU)), vec],
        [_sds((s_len, 1024), BF16), _sds(blocks_shape, F32), _sds(blocks_shape, F32),
         _sds((1, D_LRU), F32), _sds((1, D_LRU), F32), _sds((1, D_LRU), F32),
         _sds((4, D_LRU), F32), _sds((1, D_LRU), F32)],
        [pltpu.VMEM((tl + 8, D_LRU), F32)] * 3 + [pltpu.VMEM((tl, D_LRU), F32)] * 3
        + [pltpu.VMEM((8, D_LRU), F32)] + [pltpu.VMEM((D_LRU, D_LRU), F32)] * 2,
        (drec, hs, hs, u, xg, conv_w, wrg, brg, wig, big, lam), "arbitrary", comm)


def _b_attn(qkv_pad, att, datt, frow, comm=None):
    s_len = datt.shape[0]
    nb = s_len // QB
    n_pair = ATT_HEADS // 2
    pair_w = 2 * HEAD_DIM

    def body(q_ref, k0, k1, k2, v0, v1, v2, o_ref, do_ref, frow_ref, dq_ref, dkv_ref, dfrow_ref,
             bias_sc, dt_sc, acc_sc):
        t = pl.program_id(0)

        @pl.when(t == 0)
        def _():
            _bias_table(frow_ref, bias_sc)
            dt_sc[...] = jnp.zeros_like(dt_sc)
            acc_sc[...] = jnp.zeros_like(acc_sc)

        @pl.when(t < nb)
        def _():
            var = jnp.minimum(t, N_BIAS - 1)
            even = _even_lanes()
            for hp in range(n_pair):
                cs = slice(hp * pair_w, (hp + 1) * pair_w)
                qt = q_ref[:, cs]
                kts = [k0[:, cs], k1[:, cs], k2[:, cs]]
                vts = [v0[:, cs], v1[:, cs], v2[:, cs]]
                kcat = jnp.concatenate(kts, axis=0)
                dot = do_ref[:, cs]
                dd = dot * o_ref[:, cs]
                dos_pair, dsbs, pbs, dqs = None, [], [], []
                for e in range(2):
                    keep = even if e == 0 else jnp.logical_not(even)
                    qm = jnp.where(keep, qt, 0)
                    p = _att_probs(qm, kts, bias_sc[var, 2 * hp + e])
                    inv = 1.0 / jnp.sum(p, axis=-1, keepdims=True)
                    dos = jnp.where(keep, dot * inv, 0.0)
                    delta = jnp.sum(jnp.where(keep, dd, 0.0), axis=-1, keepdims=True) * inv
                    dp = jnp.concatenate([_dot_nt(dos.astype(BF16), v) for v in vts], axis=1)
                    ds = p * (dp - delta)
                    dt_sc[2 * hp + e] += ds
                    dsb = ds.astype(BF16)
                    dq = _dot(dsb, kcat)
                    dqs.append(dq)
                    dsbs.append(dsb)
                    pbs.append(p.astype(BF16))
                    dos_pair = dos if e == 0 else dos_pair + dos
                dq_ref[:, cs] = (jnp.where(even, dqs[0], dqs[1]) * ATT_SCALE).astype(BF16)
                qtt = qt.astype(F32).T.astype(BF16)
                dost = dos_pair.T.astype(BF16)
                for j in range(3):
                    slot = (t + 1 + j) % 3
                    js = slice(j * QB, (j + 1) * QB)
                    for e in range(2):
                        hr = slice(e * HEAD_DIM, (e + 1) * HEAD_DIM)
                        acc_sc[slot, hp, hr, :] += _dot(qtt[hr], dsbs[e][:, js])
                        acc_sc[slot, n_pair + hp, hr, :] += _dot(dost[hr], pbs[e][:, js])

        done = (t + 1) % 3

        @pl.when(t >= 2)
        def _():
            for i in range(2 * n_pair):
                dkv_ref[:, i * pair_w:(i + 1) * pair_w] = acc_sc[done, i].T.astype(BF16)

        acc_sc[done] = jnp.zeros((2 * n_pair, pair_w, QB), F32)

        @pl.when(t == nb + 1)
        def _():
            row = lax.broadcasted_iota(jnp.int32, (8, ROLL_W), 0)
            pad = jnp.zeros((8, ROLL_W - KB), F32)
            for h in range(ATT_HEADS):
                acc8 = jnp.concatenate([dt_sc[h, 0:8, :], pad], axis=1)
                for a1 in range(1, QB // 8):
                    blk = jnp.concatenate([dt_sc[h, 8 * a1:8 * a1 + 8, :], pad], axis=1)
                    acc8 = acc8 + pltpu.roll(blk, ROLL_W - 8 * a1, 1)
                for k in range(3):
                    acc8 = jnp.where(((row >> k) & 1) == 1, pltpu.roll(acc8, ROLL_W - (1 << k), 1), acc8)
                dfrow_ref[h:h + 1, :] = jnp.sum(acc8, axis=0, keepdims=True)

    clamp = lambda t: jnp.minimum(t, nb - 1)
    qrows = pl.BlockSpec((QB, D_ATT), lambda t: (clamp(t), 0))
    return _call(
        body, "b_attn", (nb + 2,),
        _att_in_specs(clamp) + [qrows, qrows, _full((ATT_HEADS, ROLL_W))],
        [qrows, pl.BlockSpec((QB, 2 * D_ATT), lambda t: (jnp.maximum(t - 2, 0), 0)),
         _full((ATT_HEADS, ROLL_W))],
        [_sds((s_len, D_ATT), BF16), _sds((s_len, 2 * D_ATT), BF16), _sds((ATT_HEADS, ROLL_W), F32)],
        [pltpu.VMEM((N_BIAS, ATT_HEADS, QB, KB), F32), pltpu.VMEM((ATT_HEADS, QB, KB), F32),
         pltpu.VMEM((3, 2 * n_pair, pair_w, QB), F32)],
        (*([qkv_pad] * 7), att, datt, frow), "arbitrary", comm)


def _flush_grad(steps, acc, accb, out_hbm, outb_hbm):
    @pl.when(pl.program_id(0) == steps - 1)
    def _():
        accb[...] = acc[...].astype(BF16)
        pltpu.sync_copy(acc, out_hbm)
        pltpu.sync_copy(accb, outb_hbm)


def _b_win(dq, dkv, dxg, h, ts):
    s_len = h.shape[0]
    steps = s_len // ts

    def body(dq_ref, dkv_ref, dxg_ref, h_ref, dw_hbm, dwb_hbm, acc, accb):
        @pl.when(pl.program_id(0) == 0)
        def _():
            acc[...] = jnp.zeros_like(acc)

        dproj = jnp.concatenate([dq_ref[...], dkv_ref[...], dxg_ref[...]], axis=1)
        hv = h_ref[...]
        for s in range(N_SHARD):
            acc[s] += _dot_tn(hv, dproj[:, s * IN_SH:(s + 1) * IN_SH])
        _flush_grad(steps, acc, accb, dw_hbm, dwb_hbm)

    shape = (N_SHARD, 1024, IN_SH)
    return pl.pallas_call(
        body, name="b_win", grid=(steps,),
        in_specs=[_rows(ts, 512), _rows(ts, 1024), _rows(ts, 1024), _rows(ts, 1024)],
        out_specs=[_any()] * 2, out_shape=[_sds(shape, F32), _sds(shape, BF16)],
        scratch_shapes=[pltpu.VMEM(shape, F32), pltpu.VMEM(shape, BF16)],
        compiler_params=_cp("arbitrary"))(dq, dkv, dxg, h)


def _b_inproj(dq, dkv, dxg, x, dx1, g_mix, w_in_g, tm, comm=None):
    s_len = x.shape[0]

    def body(dq_ref, dkv_ref, dxg_ref, x_ref, dx1_ref, g_ref, w_hbm, gx_ref, dgm_ref, w_ref):
        _load_w_in_once(w_hbm, w_ref)

        @pl.when(pl.program_id(0) == 0)
        def _():
            dgm_ref[...] = jnp.zeros_like(dgm_ref)

        dproj = jnp.concatenate([dq_ref[...], dkv_ref[...], dxg_ref[...]], axis=1)
        dh = _dot_nt(dproj, w_ref[...])
        dx, dgm = _rms_bwd(dh, x_ref[...], g_ref[...])
        gx_ref[...] = dx1_ref[...] + dx
        dgm_ref[...] += dgm

    return _call(
        body, "b_inproj", (s_len // tm,),
        [_rows(tm, 512), _rows(tm, 1024), _rows(tm, 1024), _rows(tm, 1024), _rows(tm, 1024),
         _full((1, 1024)), _any()],
        [_rows(tm, 1024), _full((1, 1024))],
        [_sds((s_len, 1024), F32), _sds((1, 1024), F32)],
        [pltpu.VMEM((1024, D_IN), BF16)], (dq, dkv, dxg, x, dx1, g_mix, w_in_g), "arbitrary", comm)


def _mm_tn(xa, ya, name, ts):
    s_len, k = xa.shape
    n = ya.shape[1]

    steps = s_len // ts

    def body(x_ref, y_ref, o_hbm, ob_hbm, acc, accb):
        @pl.when(pl.program_id(0) == 0)
        def _():
            acc[...] = jnp.zeros_like(acc)
        acc[...] += _dot_tn(x_ref[...].astype(BF16), y_ref[...].astype(BF16))
        _flush_grad(steps, acc, accb, o_hbm, ob_hbm)

    return pl.pallas_call(
        body, name=name, grid=(steps,), in_specs=[_rows(ts, k), _rows(ts, n)],
        out_specs=[_any()] * 2, out_shape=[_sds((k, n), F32), _sds((k, n), BF16)],
        scratch_shapes=[pltpu.VMEM((k, n), F32), pltpu.VMEM((k, n), BF16)],
        compiler_params=_cp("arbitrary"))(xa, ya)


PAD_KEYS = LEFT_CHUNKS * CHUNK
F_HI = PAD_KEYS - MAX_REL + 1
F_LO = PAD_KEYS + MAX_REL


def _frow_from_rel_bias(rb):
    last = rb[:, 2 * MAX_REL:2 * MAX_REL + 1]
    hi = jnp.broadcast_to(last, (ATT_HEADS, F_HI))
    mid = rb[:, 1:2 * MAX_REL][:, ::-1]
    lo = jnp.broadcast_to(rb[:, 0:1], (ATT_HEADS, KB - F_LO))
    wrap = jnp.broadcast_to(last, (ATT_HEADS, ROLL_W - KB))
    return jnp.concatenate([hi, mid, lo, wrap], axis=1)


def _rel_bias_grad_from_dfrow(df):
    g_last = jnp.sum(df[:, 0:F_HI], axis=1, keepdims=True) + jnp.sum(df[:, KB:ROLL_W], axis=1, keepdims=True)
    mid = df[:, F_HI:F_LO][:, ::-1]
    g_first = jnp.sum(df[:, F_LO:KB], axis=1, keepdims=True)
    return jnp.concatenate([g_first, mid, g_last], axis=1)


def _block_diag(w):
    eye = jnp.eye(8, dtype=w.dtype)
    return (w[:, :, None, :] * eye[:, None, :, None]).reshape(D_LRU, D_LRU)


MID = ['w_out', 'wq_c', 'wk_c', 'wv_c', 'wo_c']
TRANSPOSED = ['w_gate', 'w_up']
AG_IN_INPROJ = ['w_out', 'wq_c', 'wk_c']
AG_IN_ATTN = ['wv_c', 'wo_c', 'w_gate']
AG_IN_LRU = ['w_up']
AG_IN_MID = ['w_down']
RS_IN_MID = ['w_gate', 'w_up']
RS_IN_LRU = ['w_down']
RS_IN_ATTN = MID


def _local_step(x, mem, tgt, p, gw, shards=None, chip=None):
    s_len = x.shape[0]
    tm = min(256, s_len)
    tmb = min(512, s_len)
    tl = min(512, s_len)
    frow = _frow_from_rel_bias(p['rel_bias'])
    wrg = _block_diag(p['w_rg']).astype(BF16)
    wig = _block_diag(p['w_ig']).astype(BF16)
    gw = dict(gw)

    big, bigb, recv, part, sib = {}, {}, {}, {}, {}

    def ag(names):
        return [] if shards is None else [("ag", [shards[n] for n in names])]

    def rs(names):
        return [] if shards is None else [("rs", [bigb[n] for n in names])]

    def swap(names):
        return [] if shards is None else [("swap", [part[n] for n in names])]

    def reduce_own(names):
        if shards is not None:
            for n in names:
                part[n] = _sum_parts(big[n], recv[n], chip, "sum_" + n)

    h, qkv_pad, xg, *got = _f_inproj(x, p['g_mix'], gw['w_in'], tmb, ag(AG_IN_INPROJ))
    gw.update(zip(AG_IN_INPROJ, got))
    att, *got = _f_attn(qkv_pad, frow, ag(AG_IN_ATTN))
    gw.update(zip(AG_IN_ATTN, got))
    rec, u, hs, *got = _f_lru(xg, p['conv_w'], p['conv_b'], wrg, p['b_rg'], wig, p['b_ig'], p['lru_L'], tl,
                              ag(AG_IN_LRU))
    gw.update(zip(AG_IN_LRU, got))
    w_out = gw['w_out'].reshape(1024, 1024)
    wq = gw['wq_c'].reshape(1024, 1024)
    wk = gw['wk_c'].reshape(1024, 1024)
    wv = gw['wv_c'].reshape(1024, 1024)
    wo = gw['wo_c'].reshape(1024, 1024)
    mn, kx, vx = _f_mem(mem, p['g_mem'], wk, wv)
    mg, x1, hc, qx, ox, x2, *got = _f_mid(x, att, rec, p['g_out_attn'], p['g_out_lru'], w_out, p['g_cross'],
                                          wq, kx, vx, wo, tmb, ag(AG_IN_MID))
    gw.update(zip(AG_IN_MID, got))
    ffn_w = [gw[n].reshape(D_FF, 1024) for n in ('w_gate', 'w_up', 'w_down')]
    hf, gact, uact, aact, dx3, loss, dg_final = _f_ffn(x2, tgt, p['g_ffn'], p['g_final'], *ffn_w, tmb)

    ts = min(1024, s_len)
    dgact, duact, dx2, dg_ffn = _b_ffn(dx3, x2, gact, uact, p['g_ffn'], *ffn_w, tm)
    big['w_gate'], bigb['w_gate'] = _mm_tn(dgact, hf, "dw_gate", ts)
    big['w_up'], bigb['w_up'] = _mm_tn(duact, hf, "dw_up", ts)
    big['w_down'], bigb['w_down'] = _mm_tn(aact, dx3, "dw_down", ts)
    for n in ('w_gate', 'w_up', 'w_down'):
        big[n] = big[n].reshape(N_SHARD, FF_SH, 1024)
        bigb[n] = bigb[n].reshape(N_SHARD, FF_SH, 1024)

    dqx, dx1, datt, drec, dkx, dvx, dg_cross, dg_oa, dg_ol, *got = _b_mid(
        dx2, qx, x1, att, rec, kx, vx, wo, wq, w_out, p['g_cross'], p['g_out_attn'], p['g_out_lru'], tmb,
        rs(RS_IN_MID))
    recv.update(zip(RS_IN_MID, got))
    reduce_own(RS_IN_MID)
    dwk, dwv, dg_mem, dwkb, dwvb = _b_mem(dkx, dvx, mem, mn, p['g_mem'], wk, wv)
    big['wk_c'], bigb['wk_c'] = dwk, dwkb
    big['wv_c'], bigb['wv_c'] = dwv, dwvb
    big['w_out'], bigb['w_out'] = _mm_tn(mg, dx1, "dw_out", ts)
    big['wq_c'], bigb['wq_c'] = _mm_tn(hc, dqx, "dw_q", ts)
    big['wo_c'], bigb['wo_c'] = _mm_tn(ox, dx2, "dw_o", ts)
    for n in MID:
        big[n] = big[n].reshape(N_SHARD, 256, 1024)
        bigb[n] = bigb[n].reshape(N_SHARD, 256, 1024)

    dxg, dwrg, dwig, dbrg, dbig, dlam, dcw, dcb, *got = _b_lru(
        drec, hs, u, xg, p['conv_w'], wrg, p['b_rg'], wig, p['b_ig'], p['lru_L'], tl,
        rs(RS_IN_LRU) + swap(RS_IN_MID))
    recv.update(zip(RS_IN_LRU, got))
    sib.update(zip(RS_IN_MID, got[len(RS_IN_LRU):]))
    reduce_own(RS_IN_LRU)
    small = {
        'conv_w': dcw, 'conv_b': dcb, 'w_rg': dwrg, 'b_rg': dbrg, 'w_ig': dwig, 'b_ig': dbig, 'lru_L': dlam,
        'g_out_attn': dg_oa, 'g_out_lru': dg_ol, 'g_cross': dg_cross, 'g_mem': dg_mem, 'g_ffn': dg_ffn,
        'g_final': dg_final,
    }
    names = [n for n in SMALL if n in small]
    gather = [] if shards is None else [("ag8", [_pack_small(names, [small[n] for n in names], loss)])]
    dq, dkv, dfrow, *got = _b_attn(qkv_pad, att, datt, frow, rs(RS_IN_ATTN) + swap(RS_IN_LRU) + gather)
    recv.update(zip(RS_IN_ATTN, got))
    sib.update(zip(RS_IN_LRU, got[len(RS_IN_ATTN):]))
    packs = got[-1] if gather else None
    reduce_own(RS_IN_ATTN)
    small['rel_bias'] = _rel_bias_grad_from_dfrow(dfrow)
    big['w_in'], bigb['w_in'] = _b_win(dq, dkv, dxg, h, ts)
    grad_x, small['g_mix'], *got = _b_inproj(dq, dkv, dxg, x, dx1, p['g_mix'], gw['w_in'], tmb,
                                             rs(['w_in']) + swap(RS_IN_ATTN))
    recv.update(zip(['w_in'], got))
    sib.update(zip(RS_IN_ATTN, got[1:]))
    reduce_own(['w_in'])
    return loss, grad_x, small, big, part, sib, packs


def _cast_shards(ws):
    def body(*refs):
        n = len(refs) // 2
        for src, dst in zip(refs[:n], refs[n:]):
            dst[...] = src[...].astype(BF16)

    return pl.pallas_call(body, name="cast_shards", out_shape=[_sds(w.shape, BF16) for w in ws],
                          compiler_params=_cp())(*ws)


def _sum_parts(own4, recv3, chip, name):
    _, r, c = own4.shape
    steps = _ew_steps(r)
    tr = r // steps

    def body(chip_ref, own_ref, rc_ref, o_ref):
        o_ref[...] = ((own_ref[0] + rc_ref[0].astype(F32)) + rc_ref[1].astype(F32)) + rc_ref[2].astype(F32)

    grid_spec = pltpu.PrefetchScalarGridSpec(
        num_scalar_prefetch=1, grid=(steps,),
        in_specs=[pl.BlockSpec((1, tr, c), lambda i, ch: (ch[0], i, 0)),
                  pl.BlockSpec((3, tr, c), lambda i, ch: (0, i, 0))],
        out_specs=pl.BlockSpec((tr, c), lambda i, ch: (i, 0)))
    return pl.pallas_call(body, name=name, grid_spec=grid_spec, out_shape=_sds((r, c), F32),
                          compiler_params=_cp("parallel"))(chip, own4, recv3)


def _adamw_math(w, g, m, v):
    m = ADAM_B1 * m + (1.0 - ADAM_B1) * g
    v = ADAM_B2 * v + (1.0 - ADAM_B2) * (g * g)
    m_hat = m / (1.0 - ADAM_B1 ** ADAM_STEP)
    v_hat = v / (1.0 - ADAM_B2 ** ADAM_STEP)
    delta = -ADAM_LR * (m_hat / (jnp.sqrt(v_hat) + ADAM_EPS) + ADAM_WD * w)
    return delta, m, v


def _final_adamw(pa, pb, w, m, v, name):
    r, c = w.shape
    steps = _ew_steps(r)
    tr = r // steps

    def body(pa_ref, pb_ref, w_ref, m_ref, v_ref, g_ref, d_ref, nm_ref, nv_ref):
        g = pa_ref[...] + pb_ref[...]
        g_ref[...] = g
        d_ref[...], nm_ref[...], nv_ref[...] = _adamw_math(w_ref[...], g, m_ref[...], v_ref[...])

    return pl.pallas_call(
        body, name=name, grid=(steps,), in_specs=[_rows(tr, c)] * 5, out_specs=[_rows(tr, c)] * 4,
        out_shape=[_sds((r, c), F32)] * 4, compiler_params=_cp("parallel"))(pa, pb, w, m, v)


def _pack_put(ref, name, val_ref):
    r = _pack_rows()[name]
    shape = val_ref.shape
    if len(shape) == 3:
        for b in range(shape[0]):
            ref[r:r + shape[1], b * shape[2]:(b + 1) * shape[2]] = val_ref[b]
    elif shape[1] == 2 * PACK_W:
        ref[r:r + 1, :] = val_ref[:, 0:PACK_W]
        ref[r + 1:r + 2, :] = val_ref[:, PACK_W:2 * PACK_W]
    else:
        ref[r:r + shape[0], 0:shape[1]] = val_ref[...]


def _pack_get(ref, name, shape):
    r = _pack_rows()[name]
    if len(shape) == 3:
        return jnp.stack([ref[r:r + shape[1], b * shape[2]:(b + 1) * shape[2]] for b in range(shape[0])])
    if shape[1] == 2 * PACK_W:
        return jnp.concatenate([ref[r:r + 1, :], ref[r + 1:r + 2, :]], axis=1)
    return ref[r:r + shape[0], 0:shape[1]]


def _pack_small(names, g, loss):
    n = len(g)

    def body(*refs):
        pack = refs[n + 1]
        pack[...] = jnp.zeros_like(pack)
        for a, name in enumerate(names):
            _pack_put(pack, name, refs[a])
        _pack_put(pack, 'loss', refs[n])

    return pl.pallas_call(body, name="pack_small", out_shape=_sds((PACK_ROWS, PACK_W), F32),
                          compiler_params=_cp())(*g, loss)


def _all_peers():
    x, y, c = _mesh_pos()
    peers = []
    for k in range(1, 8):
        px = 1 - x if k & 4 else x
        py = 1 - y if k & 2 else y
        pc = 1 - c if k & 1 else c
        peers.append(((px, py, pc), 4 * px + 2 * py + pc))
    return peers, 4 * x + 2 * y + c


def _ag8_copies(ins, outs, sems):
    send_sems, recv_sems, loc_sems = sems
    n = len(ins)
    peers, me = _all_peers()

    def remote(k, j, slot):
        return pltpu.make_async_remote_copy(
            src_ref=ins[k], dst_ref=outs[k].at[slot], send_sem=send_sems.at[k, j], recv_sem=recv_sems.at[k, j],
            device_id=peers[j][0], device_id_type=MESH_ID)

    def local(k):
        return pltpu.make_async_copy(ins[k], outs[k].at[me], loc_sems.at[k])

    def start():
        for k in range(n):
            local(k).start()
            for j in range(7):
                remote(k, j, me).start()

    def wait():
        for k in range(n):
            for j in range(7):
                remote(k, j, peers[j][1]).wait_recv()
        for k in range(n):
            for j in range(7):
                remote(k, j, me).wait_send()
            local(k).wait()

    return start, _no_forward, wait


def _ar_late(names, g):
    n = len(g)

    def body(*refs):
        tot_ref, pack, buf, send_sems, recv_sems = refs[n:]
        peers, me = _all_peers()

        def remote(j, slot):
            return pltpu.make_async_remote_copy(
                src_ref=pack, dst_ref=buf.at[slot], send_sem=send_sems.at[j], recv_sem=recv_sems.at[j],
                device_id=peers[j][0], device_id_type=MESH_ID)

        pack[...] = jnp.zeros_like(pack)
        for a, name in enumerate(names):
            _pack_put(pack, name, refs[a])
        for j in range(7):
            remote(j, me).start()
        buf[me] = pack[...]
        for j in range(7):
            remote(j, peers[j][1]).wait_recv()
        for j in range(7):
            remote(j, me).wait_send()
        tot = buf[0]
        for d in range(1, 8):
            tot = tot + buf[d]
        tot_ref[...] = tot

    return pl.pallas_call(
        body, name="ar_late", out_shape=_sds((LATE_ROWS, PACK_W), F32),
        scratch_shapes=[pltpu.VMEM((LATE_ROWS, PACK_W), F32), pltpu.VMEM((8, LATE_ROWS, PACK_W), F32),
                        pltpu.SemaphoreType.DMA((7,)), pltpu.SemaphoreType.DMA((7,))],
        compiler_params=_cp())(*g)


def _adamw_small(packs, late_tot, g_shapes, loss_shape, w, m, v):
    n = len(w)

    def body(*refs):
        packs_ref, late_ref = refs[0], refs[1]
        w_refs, m_refs, v_refs = (refs[2 + i * n:2 + (i + 1) * n] for i in range(3))
        o0 = 3 * n + 2
        go, do, mo, vo = (refs[o0 + i * n:o0 + (i + 1) * n] for i in range(4))
        loss_out, tot_ref = refs[o0 + 4 * n], refs[o0 + 4 * n + 1]
        x, y, _ = _mesh_pos()
        tot = packs_ref[0]
        for d in range(1, 8):
            tot = tot + packs_ref[d]
        tot_ref[...] = tot
        tot_ref[0:LATE_ROWS, :] += late_ref[...]
        loss_out[...] = _pack_get(tot_ref, 'loss', loss_shape)
        for a, name in enumerate(SMALL):
            if name == 'conv_w':
                r = _pack_rows()[name]
                ga = tot_ref[r:r + g_shapes[a][0], pl.ds(pl.multiple_of((2 * x + y) * 128, 128), 128)]
            else:
                ga = _pack_get(tot_ref, name, g_shapes[a])
            go[a][...] = ga
            do[a][...], mo[a][...], vo[a][...] = _adamw_math(w_refs[a][...], ga, m_refs[a][...], v_refs[a][...])

    out_shape = [_sds(a.shape, F32) for a in w] * 4 + [_sds(loss_shape, F32)]
    return pl.pallas_call(body, name="adamw_small", out_shape=out_shape,
                          scratch_shapes=[pltpu.VMEM((PACK_ROWS, PACK_W), F32)],
                          compiler_params=_cp())(packs, late_tot, *w, *m, *v)


PACK_W = 512
PACK_ROWS = 160
LATE = ['g_mix', 'rel_bias']
LATE_ROWS = 32


def _pack_rows():
    rows, r = {}, 0
    for name in ['g_mix', 'g_cross', 'g_mem', 'g_ffn', 'g_final']:
        rows[name] = r
        r += 2
    for name in ['conv_b', 'b_rg', 'b_ig', 'lru_L', 'g_out_attn', 'g_out_lru']:
        rows[name] = r
        r += 1
    rows['conv_w'] = r
    rows['loss'] = r + 4
    rows['rel_bias'] = 24
    rows['w_rg'] = 32
    rows['w_ig'] = 32 + LRU_BLOCK
    assert r + 5 <= 24 and rows['w_ig'] + LRU_BLOCK == PACK_ROWS
    assert rows['g_mix'] + 2 <= LATE_ROWS and rows['rel_bias'] + 8 <= LATE_ROWS
    return rows


INPUT_NAMES = (['x', 'mem'] + WEIGHTS + ['loss_target'] + ['m_' + n for n in WEIGHTS] + ['v_' + n for n in WEIGHTS])


def kernel(x, mem, g_mix, w_in, rel_bias, conv_w, conv_b, w_rg, b_rg, w_ig, b_ig, lru_L, g_out_attn, g_out_lru, w_out, g_cross, g_mem, wq_c, wk_c, wv_c, wo_c, g_ffn, w_gate, w_up, w_down, g_final, loss_target, m_g_mix, m_w_in, m_rel_bias, m_conv_w, m_conv_b, m_w_rg, m_b_rg, m_w_ig, m_b_ig, m_lru_L, m_g_out_attn, m_g_out_lru, m_w_out, m_g_cross, m_g_mem, m_wq_c, m_wk_c, m_wv_c, m_wo_c, m_g_ffn, m_w_gate, m_w_up, m_w_down, m_g_final, v_g_mix, v_w_in, v_rel_bias, v_conv_w, v_conv_b, v_w_rg, v_b_rg, v_w_ig, v_b_ig, v_lru_L, v_g_out_attn, v_g_out_lru, v_w_out, v_g_cross, v_g_mem, v_wq_c, v_wk_c, v_wv_c, v_wo_c, v_g_ffn, v_w_gate, v_w_up, v_w_down, v_g_final):
    a = dict(zip(INPUT_NAMES, (x, mem, g_mix, w_in, rel_bias, conv_w, conv_b, w_rg, b_rg, w_ig, b_ig, lru_L, g_out_attn, g_out_lru, w_out, g_cross, g_mem, wq_c, wk_c, wv_c, wo_c, g_ffn, w_gate, w_up, w_down, g_final, loss_target, m_g_mix, m_w_in, m_rel_bias, m_conv_w, m_conv_b, m_w_rg, m_b_rg, m_w_ig, m_b_ig, m_lru_L, m_g_out_attn, m_g_out_lru, m_w_out, m_g_cross, m_g_mem, m_wq_c, m_wk_c, m_wv_c, m_wo_c, m_g_ffn, m_w_gate, m_w_up, m_w_down, m_g_final, v_g_mix, v_w_in, v_rel_bias, v_conv_w, v_conv_b, v_w_rg, v_b_rg, v_w_ig, v_b_ig, v_lru_L, v_g_out_attn, v_g_out_lru, v_w_out, v_g_cross, v_g_mem, v_wq_c, v_wk_c, v_wv_c, v_wo_c, v_g_ffn, v_w_gate, v_w_up, v_w_down, v_g_final)))
    chip = 2 * lax.axis_index("x") + lax.axis_index("y")

    def shard(name):
        arr = a[name][0]
        base = name[2:] if name[:2] in ('m_', 'v_') else name
        return jnp.swapaxes(arr, 0, 1) if base in TRANSPOSED else arr

    shards = dict(zip(BIG, _cast_shards([shard(n) for n in BIG])))
    w_in_g, conv_w_g = _comm_only("ag_w_in", [("ag", [shards['w_in']]), ("agf", [a['conv_w'][0]])])
    conv_w_full = conv_w_g.transpose(1, 0, 2).reshape(4, D_LRU)

    p = {n: a[n] for n in SMALL}
    p['rel_bias'] = a['rel_bias'][0]
    p['w_rg'] = a['w_rg'][0]
    p['w_ig'] = a['w_ig'][0]
    p['conv_w'] = conv_w_full
    p['g_final'] = a['g_final'][None, :]
    chip_arr = jnp.reshape(chip, (1,)).astype(jnp.int32)
    loss_part, grad_x, small, _, part, sib, packs = _local_step(
        a['x'][0], a['mem'][0], a['loss_target'][0], p, {'w_in': w_in_g}, shards, chip_arr)

    sib['w_in'], = _comm_only("swap_w_in", [("swap", [part['w_in']])])
    out = {}
    for n in BIG:
        res = _final_adamw(part[n], sib[n], shard(n), shard('m_' + n), shard('v_' + n), "adamw_" + n)
        out[n] = [jnp.swapaxes(r, 0, 1) for r in res] if n in TRANSPOSED else res

    def natural(arr):
        return arr[0] if arr.ndim >= 3 else (arr[None, :] if arr.ndim == 1 else arr)

    small_out = _adamw_small(packs, _ar_late(LATE, [small[n] for n in LATE]), [small[n].shape for n in SMALL],
                             loss_part.shape, *[[natural(a[pre + n]) for n in SMALL] for pre in ('', 'm_', 'v_')])
    ns = len(SMALL)
    loss = small_out[4 * ns][0, 0]

    def leaf(i, n):
        if n in BIG:
            return out[n][i][None]
        return small_out[i * ns + SMALL.index(n)].reshape(a[n].shape)

    return (loss, grad_x[None], *[leaf(i, n) for i in range(4) for n in WEIGHTS])
```

```python
import math

import jax
import jax.numpy as jnp
from jax import lax
from jax.experimental import pallas as pl
from jax.experimental.pallas import tpu as pltpu

F32 = jnp.float32
BF16 = jnp.bfloat16

D_MODEL = 1024
D_ATT = 512
D_LRU = 512
HEAD_DIM = 64
ATT_HEADS = 8
CHUNK = 64
LEFT_CHUNKS = 8
MAX_REL = 128
X_HEADS = 4
X_HEAD_DIM = 256
N_SHARD = 4
IN_SH = 640
D_IN = N_SHARD * IN_SH
FF_SH = 704
D_FF = N_SHARD * FF_SH
EPS = 1e-6
LRU_C = 8.0
LRU_BLOCKS = 8
LRU_BLOCK = 64
QB = 256
KB = 768
ROLL_W = 1024
NEG = -1e30
ATT_SCALE = HEAD_DIM ** -0.5
X_SCALE = X_HEAD_DIM ** -0.5

ADAM_LR = 0.001
ADAM_B1 = 0.9
ADAM_B2 = 0.999
ADAM_EPS = 1e-08
ADAM_WD = 0.01
ADAM_STEP = 10

VMEM_LIMIT_V7X = 56 * 1024 * 1024
BF16_ROWS = 16


EW_VMEM_BUDGET = 40 * 1024 * 1024


def _ew_steps(rows, bytes_per_row):
    return min(s for s in (2, 4, 8, 16) if rows % (s * BF16_ROWS) == 0
               and 2 * (rows // s) * bytes_per_row <= EW_VMEM_BUDGET)
MESH_ID = pl.DeviceIdType.MESH

WEIGHTS = ['g_mix', 'w_in', 'rel_bias', 'conv_w', 'conv_b', 'w_rg', 'b_rg', 'w_ig', 'b_ig', 'lru_L',
           'g_out_attn', 'g_out_lru', 'w_out', 'g_cross', 'g_mem', 'wq_c', 'wk_c', 'wv_c', 'wo_c',
           'g_ffn', 'w_gate', 'w_up', 'w_down', 'g_final']
BIG = ['w_in', 'w_out', 'wq_c', 'wk_c', 'wv_c', 'wo_c', 'w_gate', 'w_up', 'w_down']
SMALL = [n for n in WEIGHTS if n not in BIG]


def _sds(shape, dtype):
    return jax.ShapeDtypeStruct(shape, dtype)


def _cp(*sem):
    return pltpu.CompilerParams(dimension_semantics=sem or None, vmem_limit_bytes=VMEM_LIMIT_V7X)


def _rows(tm, n):
    return pl.BlockSpec((tm, n), lambda i: (i, 0))


def _full(shape):
    nd = len(shape)
    return pl.BlockSpec(shape, lambda i: (0,) * nd)


def _dot(a, b):
    return jnp.dot(a, b, preferred_element_type=F32)


def _dot_nt(a, b):
    return lax.dot_general(a, b, (((1,), (1,)), ((), ())), preferred_element_type=F32)


def _dot_tn(a, b):
    return lax.dot_general(a, b, (((0,), (0,)), ((), ())), preferred_element_type=F32)


def _rinv(x):
    return lax.rsqrt(jnp.mean(x * x, axis=-1, keepdims=True) + EPS)


def _rms_bwd(dy, x, g):
    r = _rinv(x)
    yh = x * r
    dyh = dy * g
    dx = r * (dyh - yh * jnp.mean(dyh * yh, axis=-1, keepdims=True))
    return dx, jnp.sum(dy * yh, axis=0, keepdims=True)


def _gelu(x):
    c = math.sqrt(2.0 / math.pi)
    t = jnp.tanh(c * (x + 0.044715 * x * x * x))
    return 0.5 * x * (1.0 + t)


def _gelu_and_grad(x):
    c = math.sqrt(2.0 / math.pi)
    t = jnp.tanh(c * (x + 0.044715 * x * x * x))
    g = 0.5 * x * (1.0 + t)
    dg = 0.5 * (1.0 + t) + 0.5 * x * (1.0 - t * t) * c * (1.0 + 3.0 * 0.044715 * x * x)
    return g, dg


def _neg_expm1(z):
    series = -z * (1.0 + z * (0.5 + z * ((1.0 / 6.0) + z * (1.0 / 24.0))))
    return jnp.where(z > -0.03, series, 1.0 - jnp.exp(z))


def _lru_gates(u, wrg, brg, wig, big, lam):
    ub = u.astype(BF16)
    r = jax.nn.sigmoid(_dot(ub, wrg) + brg)
    ig = jax.nn.sigmoid(_dot(ub, wig) + big)
    sp = jnp.maximum(-lam, 0.0) + jnp.log1p(jnp.exp(-jnp.abs(lam)))
    la = -LRU_C * r * sp
    a = jnp.exp(la)
    mult = jnp.sqrt(jnp.maximum(_neg_expm1(2.0 * la), 0.0))
    return ub, r, ig, sp, a, mult


def _scan8(a8, b8, hprev):
    row = lax.broadcasted_iota(jnp.int32, a8.shape, 0)
    aa, bb = a8, b8
    for d in (1, 2, 4):
        a_s = pltpu.roll(aa, d, 0)
        b_s = pltpu.roll(bb, d, 0)
        m = row >= d
        bb = jnp.where(m, aa * b_s + bb, bb)
        aa = jnp.where(m, aa * a_s, aa)
    return aa * hprev + bb


def _rscan8(c8, d8, lnext):
    row = lax.broadcasted_iota(jnp.int32, c8.shape, 0)
    cc, dd = c8, d8
    for d in (1, 2, 4):
        c_s = pltpu.roll(cc, 8 - d, 0)
        d_s = pltpu.roll(dd, 8 - d, 0)
        m = row < 8 - d
        dd = jnp.where(m, cc * d_s + dd, dd)
        cc = jnp.where(m, cc * c_s, cc)
    return cc * lnext + dd


def _mesh_pos():
    return lax.axis_index("x"), lax.axis_index("y"), lax.axis_index("c")


def _other_chips(x, y):
    return [(1 - x, y), (x, 1 - y), (1 - x, 1 - y)]


def _no_forward():
    pass


def _ag_full_copies(ins, outs, sems):
    send_sems, recv_sems, loc_sems = sems
    n = len(ins)
    x, y, c = _mesh_pos()
    mine = 2 * x + y
    chips = _other_chips(x, y)

    def remote(k, j, slot):
        px, py = chips[j]
        return pltpu.make_async_remote_copy(
            src_ref=ins[k], dst_ref=outs[k].at[slot], send_sem=send_sems.at[k, j], recv_sem=recv_sems.at[k, j],
            device_id=(px, py, c), device_id_type=MESH_ID)

    def local(k):
        return pltpu.make_async_copy(ins[k], outs[k].at[mine], loc_sems.at[k])

    def start():
        for k in range(n):
            local(k).start()
            for j in range(3):
                remote(k, j, mine).start()

    def wait():
        for k in range(n):
            for j, (px, py) in enumerate(chips):
                remote(k, j, 2 * px + py).wait_recv()
        for k in range(n):
            for j in range(3):
                remote(k, j, mine).wait_send()
            local(k).wait()

    return start, _no_forward, wait


def _ag_copies(ins, outs, sems):
    send_sems, recv_sems, fsend_sems, frecv_sems, loc_sems = sems
    n = len(ins)
    x, y, c = _mesh_pos()
    mine = 2 * x + y
    chips = _other_chips(x, y)

    def half(ref, hc):
        r = ref.shape[0] // 2
        return ref.at[pl.ds(pl.multiple_of(hc * r, 16), r)]

    def ici(k, j, slot):
        px, py = chips[j]
        return pltpu.make_async_remote_copy(
            src_ref=half(ins[k], c), dst_ref=half(outs[k].at[slot], c),
            send_sem=send_sems.at[k, j], recv_sem=recv_sems.at[k, j],
            device_id=(px, py, c), device_id_type=MESH_ID)

    def d2d(k, j, hc):
        px, py = chips[j]
        part = half(outs[k].at[2 * px + py], hc)
        return pltpu.make_async_remote_copy(
            src_ref=part, dst_ref=part, send_sem=fsend_sems.at[k, j], recv_sem=frecv_sems.at[k, j],
            device_id=(x, y, 1 - c), device_id_type=MESH_ID)

    def local(k):
        return pltpu.make_async_copy(ins[k], outs[k].at[mine], loc_sems.at[k])

    def start():
        for k in range(n):
            local(k).start()
            for j in range(3):
                ici(k, j, mine).start()

    def forward():
        for k in range(n):
            for j, (px, py) in enumerate(chips):
                ici(k, j, 2 * px + py).wait_recv()
                d2d(k, j, c).start()

    def wait():
        for k in range(n):
            for j in range(3):
                d2d(k, j, 1 - c).wait_recv()
        for k in range(n):
            for j in range(3):
                d2d(k, j, c).wait_send()
                ici(k, j, mine).wait_send()
            local(k).wait()

    return start, forward, wait


def _rs_copies(ins, outs, sems):
    send_sems, recv_sems = sems
    n = len(ins)
    x, y, c = _mesh_pos()
    chips = _other_chips(x, y)

    def remote(k, j):
        px, py = chips[j]
        return pltpu.make_async_remote_copy(
            src_ref=ins[k].at[2 * px + py], dst_ref=outs[k].at[j],
            send_sem=send_sems.at[k, j], recv_sem=recv_sems.at[k, j],
            device_id=(px, py, c), device_id_type=MESH_ID)

    def start():
        for k in range(n):
            for j in range(3):
                remote(k, j).start()

    def wait():
        for k in range(n):
            for j in range(3):
                remote(k, j).wait_recv()
        for k in range(n):
            for j in range(3):
                remote(k, j).wait_send()

    return start, _no_forward, wait


def _swap_copies(ins, outs, sems):
    send_sems, recv_sems = sems
    x, y, c = _mesh_pos()
    copies = [pltpu.make_async_remote_copy(
        src_ref=ins[k], dst_ref=outs[k], send_sem=send_sems.at[k], recv_sem=recv_sems.at[k],
        device_id=(x, y, 1 - c), device_id_type=MESH_ID) for k in range(len(ins))]

    def start():
        for cp in copies:
            cp.start()

    def wait():
        for cp in copies:
            cp.wait()

    return start, _no_forward, wait


def _comm_plan(groups):
    plan, arrs, shapes, sems = [], [], [], []
    for kind, group in groups:
        k = len(group)
        arrs += group
        per_peer = pltpu.SemaphoreType.DMA((k, 3))
        if kind == "ag":
            shapes += [_sds((N_SHARD,) + w.shape, w.dtype) for w in group]
            gsems = [per_peer] * 4 + [pltpu.SemaphoreType.DMA((k,))]
            maker = _ag_copies
        elif kind == "agf":
            shapes += [_sds((N_SHARD,) + w.shape, w.dtype) for w in group]
            gsems = [per_peer] * 2 + [pltpu.SemaphoreType.DMA((k,))]
            maker = _ag_full_copies
        elif kind == "ag8":
            shapes += [_sds((8,) + g.shape, g.dtype) for g in group]
            gsems = [pltpu.SemaphoreType.DMA((k, 7))] * 2 + [pltpu.SemaphoreType.DMA((k,))]
            maker = _ag8_copies
        elif kind == "rs":
            shapes += [_sds((3,) + g.shape[1:], g.dtype) for g in group]
            gsems = [pltpu.SemaphoreType.DMA((k, 3)), pltpu.SemaphoreType.DMA((k, 3))]
            maker = _rs_copies
        else:
            shapes += [_sds(g.shape, g.dtype) for g in group]
            gsems = [pltpu.SemaphoreType.DMA((k,)), pltpu.SemaphoreType.DMA((k,))]
            maker = _swap_copies
        plan.append((maker, k, len(gsems)))
        sems += gsems
    return plan, arrs, shapes, sems


def _comm_fns(plan, cins, couts, sems):
    fns, a, s = [], 0, 0
    for maker, k, ns in plan:
        fns.append(maker(cins[a:a + k], couts[a:a + k], sems[s:s + ns]))
        a += k
        s += ns

    def start():
        for st, _, _ in fns:
            st()

    def forward():
        for _, fw, _ in fns:
            fw()

    def wait():
        for _, _, wt in fns:
            wt()

    return start, forward, wait


def _call(body, name, grid, in_specs, out_specs, out_shape, scratch, args, sem, comm=None):
    if not comm:
        return pl.pallas_call(body, name=name, grid=grid, in_specs=in_specs, out_specs=out_specs,
                              out_shape=out_shape, scratch_shapes=scratch, compiler_params=_cp(sem))(*args)
    plan, c_arrs, c_shapes, c_sems = _comm_plan(comm)
    k = len(c_arrs)
    n_in, n_out, n_scr = len(in_specs), len(out_specs), len(scratch)
    last = grid[0] - 1
    fwd_step = max(1, (2 * last) // 3)

    def wrapped(*refs):
        ins, cins = refs[:n_in], refs[n_in:n_in + k]
        o0 = n_in + k
        outs, couts = refs[o0:o0 + n_out], refs[o0 + n_out:o0 + n_out + k]
        s0 = o0 + n_out + k
        start, forward, wait = _comm_fns(plan, cins, couts, refs[s0 + n_scr:])
        pl.when(pl.program_id(0) == 0)(start)
        pl.when(pl.program_id(0) == fwd_step)(forward)
        body(*ins, *outs, *refs[s0:s0 + n_scr])
        pl.when(pl.program_id(0) == last)(wait)

    return pl.pallas_call(
        wrapped, name=name, grid=grid, in_specs=list(in_specs) + [_any()] * k,
        out_specs=list(out_specs) + [_any()] * k, out_shape=list(out_shape) + c_shapes,
        scratch_shapes=list(scratch) + c_sems, compiler_params=_cp(sem))(*args, *c_arrs)


def _comm_only(name, comm):
    plan, c_arrs, c_shapes, c_sems = _comm_plan(comm)
    k = len(c_arrs)

    def body(*refs):
        start, forward, wait = _comm_fns(plan, refs[:k], refs[k:2 * k], refs[2 * k:])
        start()
        forward()
        wait()

    return pl.pallas_call(body, name=name, in_specs=[_any()] * k, out_specs=[_any()] * k, out_shape=c_shapes,
                          scratch_shapes=c_sems, compiler_params=_cp())(*c_arrs)


def _any():
    return pl.BlockSpec(memory_space=pl.ANY)


def _load_w_in_once(w_hbm, w_ref):
    @pl.when(pl.program_id(0) == 0)
    def _():
        for s in range(N_SHARD):
            pltpu.sync_copy(w_hbm.at[s], w_ref.at[:, pl.ds(s * IN_SH, IN_SH)])


def _f_inproj(x, g_mix, w_in_g, tm, comm=None):
    s_len = x.shape[0]
    pad_rows = LEFT_CHUNKS * CHUNK
    npad = pad_rows // tm

    def body(x_ref, g_ref, w_hbm, h_ref, qkv_ref, xg_ref, w_ref):
        i = pl.program_id(0)
        _load_w_in_once(w_hbm, w_ref)

        @pl.when(i < npad)
        def _():
            qkv_ref[...] = jnp.zeros_like(qkv_ref)

        @pl.when(i >= npad)
        def _():
            xv = x_ref[...]
            h = (xv * _rinv(xv) * g_ref[...]).astype(BF16)
            h_ref[...] = h
            proj = _dot(h, w_ref[...])
            qkv_ref[:, 0:D_ATT] = (proj[:, 0:D_ATT] * ATT_SCALE).astype(BF16)
            qkv_ref[:, D_ATT:3 * D_ATT] = proj[:, D_ATT:3 * D_ATT].astype(BF16)
            xg_ref[...] = proj[:, 3 * D_ATT:D_IN]

    def tok(n):
        return pl.BlockSpec((tm, n), lambda i: (jnp.maximum(i - npad, 0), 0))

    return _call(
        body, "f_inproj", (s_len // tm + npad,),
        [tok(1024), _full((1, 1024)), _any()],
        [tok(1024), _rows(tm, 1536), tok(1024)],
        [_sds((s_len, 1024), BF16), _sds((s_len + pad_rows, 1536), BF16), _sds((s_len, 1024), F32)],
        [pltpu.VMEM((1024, D_IN), BF16)], (x, g_mix, w_in_g), "arbitrary", comm)


N_BIAS = 3


def _bias_table(frow_ref, bias_sc):
    qa = lax.broadcasted_iota(jnp.int32, (QB, KB), 0) // CHUNK
    kcol = lax.broadcasted_iota(jnp.int32, (QB, KB), 1)
    kb = kcol // CHUNK
    band = jnp.where((kb >= qa) & (kb - qa <= LEFT_CHUNKS), 0.0, NEG).astype(F32)
    for h in range(ATT_HEADS):
        row = jnp.broadcast_to(frow_ref[h:h + 1, :], (QB, ROLL_W))
        toep = pltpu.roll(row, 0, 1, stride=1, stride_axis=0)
        gen = toep[:, 0:KB] + band
        bias_sc[N_BIAS - 1, h] = gen
        for v in range(N_BIAS - 1):
            pad_keys = LEFT_CHUNKS * CHUNK - v * QB
            bias_sc[v, h] = gen + jnp.where(kcol < pad_keys, NEG, 0.0).astype(F32)


def _even_lanes():
    return lax.broadcasted_iota(jnp.int32, (1, 2 * HEAD_DIM), 1) < HEAD_DIM


def _att_probs(qm, kts, bias):
    s = jnp.concatenate([_dot_nt(qm, k) for k in kts], axis=1) + bias
    return jnp.exp(s - jnp.max(s, axis=-1, keepdims=True))


def _att_in_specs(clamp):
    def spec(j, col):
        return pl.BlockSpec((QB, D_ATT), lambda i: (clamp(i) + j, col))
    return [spec(2, 0), spec(0, 1), spec(1, 1), spec(2, 1), spec(0, 2), spec(1, 2), spec(2, 2)]


def _f_attn(qkv_pad, frow, comm=None):
    s_len = qkv_pad.shape[0] - LEFT_CHUNKS * CHUNK
    nb = s_len // QB

    def body(q_ref, k0, k1, k2, v0, v1, v2, frow_ref, o_ref, bias_sc):
        i = pl.program_id(0)

        @pl.when(i == 0)
        def _():
            _bias_table(frow_ref, bias_sc)

        var = jnp.minimum(i, N_BIAS - 1)
        even = _even_lanes()
        for hp in range(ATT_HEADS // 2):
            cs = slice(hp * 2 * HEAD_DIM, (hp + 1) * 2 * HEAD_DIM)
            qt = q_ref[:, cs]
            kts = [k0[:, cs], k1[:, cs], k2[:, cs]]
            vts = [v0[:, cs], v1[:, cs], v2[:, cs]]
            res = []
            for e in range(2):
                keep = even if e == 0 else jnp.logical_not(even)
                pb = _att_probs(jnp.where(keep, qt, 0), kts, bias_sc[var, 2 * hp + e]).astype(BF16)
                r = _dot(pb, jnp.concatenate([jnp.where(keep, v, 1) for v in vts], axis=0))
                res.append(r / pltpu.roll(r, HEAD_DIM, 1))
            o_ref[:, cs] = jnp.where(even, res[0], res[1])

    return _call(
        body, "f_attn", (nb,),
        _att_in_specs(lambda i: i) + [_full((ATT_HEADS, ROLL_W))],
        [_rows(QB, D_ATT)], [_sds((s_len, D_ATT), F32)],
        [pltpu.VMEM((N_BIAS, ATT_HEADS, QB, KB), F32)], (*([qkv_pad] * 7), frow), "arbitrary", comm)


def _f_lru(xg, conv_w, conv_b, wrg, brg, wig, big, lam, tl, comm=None):
    s_len = xg.shape[0]

    def body(xg_ref, cw_ref, cb_ref, wrg_ref, brg_ref, wig_ref, big_ref, l_ref,
             rec_ref, u_ref, hs_ref, xbuf, a_sc, b_sc, hcar):
        i = pl.program_id(0)

        @pl.when(i == 0)
        def _():
            xbuf[0:8, :] = jnp.zeros((8, D_LRU), F32)
            hcar[...] = jnp.zeros((8, D_LRU), F32)

        xu0 = xg_ref[:, 0:D_LRU]
        xbuf[8:8 + tl, :] = xu0
        u = cb_ref[...] + cw_ref[0:1, :] * xbuf[pl.ds(5, tl), :]
        for j in range(1, 4):
            u = u + cw_ref[j:j + 1, :] * xbuf[pl.ds(5 + j, tl), :]
        xbuf[0:8, :] = xu0[tl - 8:tl, :]
        u_ref[...] = u
        _, _, ig, _, a, mult = _lru_gates(u, wrg_ref[...], brg_ref[...], wig_ref[...], big_ref[...], l_ref[...])
        a_sc[...] = a
        b_sc[...] = mult * (ig * u)

        def grp(g, hprev):
            off = pl.multiple_of(g * 8, 8)
            h8 = _scan8(a_sc[pl.ds(off, 8), :], b_sc[pl.ds(off, 8), :], hprev)
            hs_ref[pl.ds(off, 8), :] = h8
            return h8[7:8, :]

        hcar[0:1, :] = lax.fori_loop(0, tl // 8, grp, hcar[0:1, :])
        rec_ref[...] = hs_ref[...] * _gelu(xg_ref[:, D_LRU:2 * D_LRU])

    vec = _full((1, D_LRU))
    return _call(
        body, "f_lru", (s_len // tl,),
        [_rows(tl, 1024), _full((4, D_LRU)), vec, _full((D_LRU, D_LRU)), vec, _full((D_LRU, D_LRU)), vec, vec],
        [_rows(tl, D_LRU)] * 3, [_sds((s_len, D_LRU), F32)] * 3,
        [pltpu.VMEM((tl + 8, D_LRU), F32), pltpu.VMEM((tl, D_LRU), F32),
         pltpu.VMEM((tl, D_LRU), F32), pltpu.VMEM((8, D_LRU), F32)],
        (xg, conv_w, conv_b, wrg, brg, wig, big, lam), "arbitrary", comm)


def _f_mem(mem, g_mem, wk, wv):
    def body(mem_ref, g_ref, wk_ref, wv_ref, mn_ref, kx_ref, vx_ref):
        mv = mem_ref[...]
        mn = (mv * _rinv(mv) * g_ref[...]).astype(BF16)
        mn_ref[...] = mn
        kx_ref[...] = _dot(mn, wk_ref[...]).astype(BF16)
        vx_ref[...] = _dot(mn, wv_ref[...]).astype(BF16)

    m = mem.shape[0]
    return pl.pallas_call(
        body, name="f_mem", out_shape=[_sds((m, 1024), BF16)] * 3,
        compiler_params=_cp())(mem, g_mem, wk, wv)


def _xattn_probs(q, k):
    s = _dot_nt(q, k) * X_SCALE
    m = jnp.max(s, axis=-1, keepdims=True)
    p = jnp.exp(s - m)
    return p, jnp.sum(p, axis=-1, keepdims=True)


def _f_mid(x, att, rec, g_oa, g_ol, w_out, g_cross, wq, kx, vx, wo, tm, comm=None):
    s_len = x.shape[0]
    m_len = kx.shape[0]

    def body(x_ref, att_ref, rec_ref, goa_ref, gol_ref, wout_ref, gc_ref, wq_ref, kx_ref, vx_ref, wo_ref,
             mg_ref, x1_ref, hc_ref, qx_ref, ox_ref, x2_ref):
        av = att_ref[...]
        rv = rec_ref[...]
        mg_ref[:, 0:D_ATT] = (av * _rinv(av) * goa_ref[...]).astype(BF16)
        mg_ref[:, D_ATT:1024] = (rv * _rinv(rv) * gol_ref[...]).astype(BF16)
        x1 = x_ref[...] + _dot(mg_ref[...], wout_ref[...])
        x1_ref[...] = x1
        hc = (x1 * _rinv(x1) * gc_ref[...]).astype(BF16)
        hc_ref[...] = hc
        qx_ref[...] = _dot(hc, wq_ref[...]).astype(BF16)
        for h in range(X_HEADS):
            sl = slice(h * X_HEAD_DIM, (h + 1) * X_HEAD_DIM)
            p, l = _xattn_probs(qx_ref[:, sl], kx_ref[:, sl])
            ox_ref[:, sl] = (_dot(p.astype(BF16), vx_ref[:, sl]) / l).astype(BF16)
        x2_ref[...] = x1 + _dot(ox_ref[...], wo_ref[...])

    sq = _full((1024, 1024))
    return _call(
        body, "f_mid", (s_len // tm,),
        [_rows(tm, 1024), _rows(tm, 512), _rows(tm, 512), _full((1, 512)), _full((1, 512)), sq,
         _full((1, 1024)), sq, _full((m_len, 1024)), _full((m_len, 1024)), sq],
        [_rows(tm, 1024)] * 6,
        [_sds((s_len, 1024), BF16), _sds((s_len, 1024), F32), _sds((s_len, 1024), BF16),
         _sds((s_len, 1024), BF16), _sds((s_len, 1024), BF16), _sds((s_len, 1024), F32)],
        [], (x, att, rec, g_oa, g_ol, w_out, g_cross, wq, kx, vx, wo), "arbitrary", comm)


def _load_weights_once(pairs):
    @pl.when(pl.program_id(0) == 0)
    def _():
        for hbm, vmem in pairs:
            pltpu.sync_copy(hbm, vmem)


FF_CHUNKS = [(0, 1280), (1280, D_FF)]


def _f_ffn(x2, tgt, g_ffn, g_final, wg, wu, wd, tm):
    s_len = x2.shape[0]

    def body(x2_ref, t_ref, gf_ref, gfin_ref, wg_hbm, wu_hbm, wd_hbm,
             hf_ref, g_ref, u_ref, a_ref, dx3_ref, loss_ref, dgfin_ref, wg_ref, wu_ref, wd_ref):
        _load_weights_once([(wg_hbm, wg_ref), (wu_hbm, wu_ref), (wd_hbm, wd_ref)])

        @pl.when(pl.program_id(0) == 0)
        def _():
            loss_ref[...] = jnp.zeros_like(loss_ref)
            dgfin_ref[...] = jnp.zeros_like(dgfin_ref)

        x2v = x2_ref[...]
        hf = (x2v * _rinv(x2v) * gf_ref[...]).astype(BF16)
        hf_ref[...] = hf
        x3 = x2v
        for c0, c1 in FF_CHUNKS:
            gv = _dot_nt(hf, wg_ref[c0:c1, :])
            uv = _dot_nt(hf, wu_ref[c0:c1, :])
            av = (gv * jax.nn.sigmoid(gv) * uv).astype(BF16)
            g_ref[:, c0:c1] = gv.astype(BF16)
            u_ref[:, c0:c1] = uv.astype(BF16)
            a_ref[:, c0:c1] = av
            x3 = x3 + _dot(av, wd_ref[c0:c1, :])
        r3 = _rinv(x3)
        yh = x3 * r3
        gfin = gfin_ref[...]
        err = yh * gfin - t_ref[...]
        loss_ref[...] += jnp.full((1, 128), 0.5 / D_MODEL, F32) * jnp.sum(err * err)
        dy = err * (1.0 / D_MODEL)
        dgfin_ref[...] += jnp.sum(dy * yh, axis=0, keepdims=True)
        dyh = dy * gfin
        dx3_ref[...] = r3 * (dyh - yh * jnp.mean(dyh * yh, axis=-1, keepdims=True))

    vec = _full((1, 1024))
    return pl.pallas_call(
        body, name="f_ffn", grid=(s_len // tm,),
        in_specs=[_rows(tm, 1024), _rows(tm, 1024), vec, vec, _any(), _any(), _any()],
        out_specs=[_rows(tm, 1024), _rows(tm, D_FF), _rows(tm, D_FF), _rows(tm, D_FF),
                   _rows(tm, 1024), _full((1, 128)), vec],
        out_shape=[_sds((s_len, 1024), BF16)] + [_sds((s_len, D_FF), BF16)] * 3
                  + [_sds((s_len, 1024), F32), _sds((1, 128), F32), _sds((1, 1024), F32)],
        scratch_shapes=[pltpu.VMEM((D_FF, 1024), BF16)] * 3,
        compiler_params=_cp("arbitrary"))(x2, tgt, g_ffn, g_final, wg, wu, wd)


def _b_ffn(dx3, x2, gact, uact, g_ffn, wg, wu, wd, tm):
    s_len = x2.shape[0]

    def body(dx3_ref, x2_ref, g_ref, u_ref, gf_ref, wg_hbm, wu_hbm, wd_hbm,
             dg_ref, du_ref, dx2_ref, dgf_ref, wg_ref, wu_ref, wd_ref):
        _load_weights_once([(wg_hbm, wg_ref), (wu_hbm, wu_ref), (wd_hbm, wd_ref)])

        @pl.when(pl.program_id(0) == 0)
        def _():
            dgf_ref[...] = jnp.zeros_like(dgf_ref)

        dx3v = dx3_ref[...]
        dx3b = dx3v.astype(BF16)
        dhf = jnp.zeros(dx3v.shape, F32)
        for c0, c1 in FF_CHUNKS:
            da = _dot_nt(dx3b, wd_ref[c0:c1, :])
            gv = g_ref[:, c0:c1].astype(F32)
            uv = u_ref[:, c0:c1].astype(F32)
            sg = jax.nn.sigmoid(gv)
            dub = (da * gv * sg).astype(BF16)
            dgb = (da * uv * (sg * (1.0 + gv * (1.0 - sg)))).astype(BF16)
            du_ref[:, c0:c1] = dub
            dg_ref[:, c0:c1] = dgb
            dhf = dhf + _dot(dgb, wg_ref[c0:c1, :]) + _dot(dub, wu_ref[c0:c1, :])
        dx, dgf = _rms_bwd(dhf, x2_ref[...], gf_ref[...])
        dx2_ref[...] = dx3v + dx
        dgf_ref[...] += dgf

    vec = _full((1, 1024))
    return pl.pallas_call(
        body, name="b_ffn", grid=(s_len // tm,),
        in_specs=[_rows(tm, 1024), _rows(tm, 1024), _rows(tm, D_FF), _rows(tm, D_FF), vec,
                  _any(), _any(), _any()],
        out_specs=[_rows(tm, D_FF), _rows(tm, D_FF), _rows(tm, 1024), vec],
        out_shape=[_sds((s_len, D_FF), BF16)] * 2 + [_sds((s_len, 1024), F32), _sds((1, 1024), F32)],
        scratch_shapes=[pltpu.VMEM((D_FF, 1024), BF16)] * 3,
        compiler_params=_cp("arbitrary"))(dx3, x2, gact, uact, g_ffn, wg, wu, wd)


def _b_mid(dx2, qx, x1, att, rec, kx, vx, wo, wq, w_out, g_cross, g_oa, g_ol, tm, comm=None):
    s_len = x1.shape[0]
    m_len = kx.shape[0]

    def body(dx2_ref, qx_ref, x1_ref, att_ref, rec_ref, kx_ref, vx_ref, wo_ref, wq_ref, wout_ref,
             gc_ref, goa_ref, gol_ref,
             dqx_ref, dx1_ref, datt_ref, drec_ref, dkx_ref, dvx_ref, dgc_ref, dgoa_ref, dgol_ref):
        @pl.when(pl.program_id(0) == 0)
        def _():
            for r in (dkx_ref, dvx_ref, dgc_ref, dgoa_ref, dgol_ref):
                r[...] = jnp.zeros_like(r)

        dx2v = dx2_ref[...]
        dox = _dot_nt(dx2v.astype(BF16), wo_ref[...])
        for h in range(X_HEADS):
            sl = slice(h * X_HEAD_DIM, (h + 1) * X_HEAD_DIM)
            q = qx_ref[:, sl]
            p, l = _xattn_probs(q, kx_ref[:, sl])
            pn = p * (1.0 / l)
            dob = dox[:, sl].astype(BF16)
            dp = _dot_nt(dob, vx_ref[:, sl])
            dvx_ref[:, sl] += _dot_tn(pn.astype(BF16), dob)
            ds = pn * (dp - jnp.sum(dp * pn, axis=-1, keepdims=True))
            dsb = (ds * X_SCALE).astype(BF16)
            dqx_ref[:, sl] = _dot(dsb, kx_ref[:, sl]).astype(BF16)
            dkx_ref[:, sl] += _dot_tn(dsb, q)
        dhc = _dot_nt(dqx_ref[...], wq_ref[...])
        dx, dgc = _rms_bwd(dhc, x1_ref[...], gc_ref[...])
        dx1 = dx2v + dx
        dx1_ref[...] = dx1
        dgc_ref[...] += dgc
        dmg = _dot_nt(dx1.astype(BF16), wout_ref[...])
        da, dgoa = _rms_bwd(dmg[:, 0:D_ATT], att_ref[...], goa_ref[...])
        datt_ref[...] = da
        dgoa_ref[...] += dgoa
        dr, dgol = _rms_bwd(dmg[:, D_ATT:1024], rec_ref[...], gol_ref[...])
        drec_ref[...] = dr
        dgol_ref[...] += dgol

    sq = _full((1024, 1024))
    mk = _full((m_len, 1024))
    return _call(
        body, "b_mid", (s_len // tm,),
        [_rows(tm, 1024), _rows(tm, 1024), _rows(tm, 1024), _rows(tm, 512), _rows(tm, 512), mk, mk,
         sq, sq, sq, _full((1, 1024)), _full((1, 512)), _full((1, 512))],
        [_rows(tm, 1024), _rows(tm, 1024), _rows(tm, 512), _rows(tm, 512), mk, mk,
         _full((1, 1024)), _full((1, 512)), _full((1, 512))],
        [_sds((s_len, 1024), BF16), _sds((s_len, 1024), F32), _sds((s_len, 512), F32),
         _sds((s_len, 512), F32), _sds((m_len, 1024), F32), _sds((m_len, 1024), F32),
         _sds((1, 1024), F32), _sds((1, 512), F32), _sds((1, 512), F32)],
        [], (dx2, qx, x1, att, rec, kx, vx, wo, wq, w_out, g_cross, g_oa, g_ol), "arbitrary", comm)


def _b_mem(dkx, dvx, mem, mn, g_mem, wk, wv):
    def body(dkx_ref, dvx_ref, mem_ref, mn_ref, g_ref, wk_ref, wv_ref, dwk_ref, dwv_ref, dgm_ref,
             dwkb_ref, dwvb_ref):
        dkb = dkx_ref[...].astype(BF16)
        dvb = dvx_ref[...].astype(BF16)
        dwk = _dot_tn(mn_ref[...], dkb)
        dwv = _dot_tn(mn_ref[...], dvb)
        dwk_ref[...] = dwk
        dwv_ref[...] = dwv
        dwkb_ref[...] = dwk.astype(BF16)
        dwvb_ref[...] = dwv.astype(BF16)
        dmn = _dot_nt(dkb, wk_ref[...]) + _dot_nt(dvb, wv_ref[...])
        mv = mem_ref[...]
        dgm_ref[...] = jnp.sum(dmn * (mv * _rinv(mv)), axis=0, keepdims=True)

    return pl.pallas_call(
        body, name="b_mem",
        out_shape=[_sds((1024, 1024), F32), _sds((1024, 1024), F32), _sds((1, 1024), F32),
                   _sds((1024, 1024), BF16), _sds((1024, 1024), BF16)],
        compiler_params=_cp())(dkx, dvx, mem, mn, g_mem, wk, wv)


def _b_lru(drec, hs, u, xg, conv_w, wrg, brg, wig, big, lam, tl, comm=None):
    s_len = xg.shape[0]
    nt = s_len // tl

    def body(drec_ref, hs_ref, hsp_ref, u_ref, xg_ref, cw_ref, wrg_ref, brg_ref, wig_ref, big_ref, l_ref,
             dxg_ref, dwrg_ref, dwig_ref, dbrg_ref, dbig_ref, dlam_ref, dcw_ref, dcb_ref,
             hbuf, abuf, dubuf, c_sc, d_sc, lam_sc, lcar, wacc_r, wacc_i):
        i = pl.program_id(0)
        tt = nt - 1 - i

        @pl.when(i == 0)
        def _():
            for r in (wacc_r, wacc_i, dbrg_ref, dbig_ref, dlam_ref, dcw_ref, dcb_ref):
                r[...] = jnp.zeros_like(r)
            abuf[tl:tl + 8, :] = jnp.zeros((8, D_LRU), F32)
            dubuf[tl:tl + 8, :] = jnp.zeros((8, D_LRU), F32)
            lcar[...] = jnp.zeros((8, D_LRU), F32)

        xu0 = xg_ref[:, 0:D_LRU]
        hsv = hs_ref[...]
        uv = u_ref[...]
        hbuf[8:8 + tl, :] = hsv
        hbuf[0:8, :] = jnp.where(tt > 0, hsp_ref[...], 0.0)
        hshift = hbuf[pl.ds(7, tl), :]
        wrg_v = wrg_ref[...]
        wig_v = wig_ref[...]
        lamv = l_ref[...]
        ub, r, ig, sp, a, mult = _lru_gates(uv, wrg_v, brg_ref[...], wig_v, big_ref[...], lamv)
        abuf[0:tl, :] = a
        c_sc[...] = abuf[pl.ds(1, tl), :]
        gel, dgel = _gelu_and_grad(xg_ref[:, D_LRU:2 * D_LRU])
        drv = drec_ref[...]
        d_sc[...] = drv * gel
        dxg_ref[:, D_LRU:2 * D_LRU] = (drv * hsv * dgel).astype(BF16)

        def grp(k, lnext):
            off = pl.multiple_of((tl // 8 - 1 - k) * 8, 8)
            l8 = _rscan8(c_sc[pl.ds(off, 8), :], d_sc[pl.ds(off, 8), :], lnext)
            lam_sc[pl.ds(off, 8), :] = l8
            return l8[0:1, :]

        lcar[0:1, :] = lax.fori_loop(0, tl // 8, grp, lcar[0:1, :])
        abuf[tl:tl + 8, :] = a[0:8, :]
        db = lam_sc[...]
        da = db * hshift
        dmult = db * (ig * uv)
        dig = db * mult * uv
        du = db * mult * ig
        dla = da * a - dmult * (a * a) / mult
        dlam_ref[...] += jnp.sum(dla * (-LRU_C) * r, axis=0, keepdims=True)
        dzr = dla * (-LRU_C * sp) * r * (1.0 - r)
        dzi = dig * ig * (1.0 - ig)
        dzrb = dzr.astype(BF16)
        dzib = dzi.astype(BF16)
        du = du + _dot_nt(dzrb, wrg_v) + _dot_nt(dzib, wig_v)
        wacc_r[...] += _dot_tn(ub, dzrb)
        wacc_i[...] += _dot_tn(ub, dzib)
        dbrg_ref[...] += jnp.sum(dzr, axis=0, keepdims=True)
        dbig_ref[...] += jnp.sum(dzi, axis=0, keepdims=True)
        dcb_ref[...] += jnp.sum(du, axis=0, keepdims=True)
        dubuf[0:tl, :] = du
        dxu0 = jnp.zeros((tl, D_LRU), F32)
        for j in range(4):
            dsh = dubuf[pl.ds(3 - j, tl), :]
            dxu0 = dxu0 + cw_ref[j:j + 1, :] * dsh
            dcw_ref[j:j + 1, :] += jnp.sum(xu0 * dsh, axis=0, keepdims=True)
        dubuf[tl:tl + 8, :] = du[0:8, :]
        dxg_ref[:, 0:D_LRU] = dxu0.astype(BF16)

        @pl.when(i == nt - 1)
        def _():
            dlam_ref[...] = dlam_ref[...] * (-jax.nn.sigmoid(-lamv))
            for n in range(LRU_BLOCKS):
                blk = slice(n * LRU_BLOCK, (n + 1) * LRU_BLOCK)
                dwrg_ref[n] = wacc_r[blk, blk]
                dwig_ref[n] = wacc_i[blk, blk]

    def rev(n):
        return pl.BlockSpec((tl, n), lambda i: (nt - 1 - i, 0))

    prev8 = pl.BlockSpec((8, D_LRU), lambda i: (jnp.maximum((nt - 1 - i) * (tl // 8) - 1, 0), 0))
    vec = _full((1, D_LRU))
    sq = _full((D_LRU, D_LRU))
    blocks_shape = (LRU_BLOCKS, LRU_BLOCK, LRU_BLOCK)
    blocks = _full(blocks_shape)
    return _call(
        body, "b_lru", (nt,),
        [rev(D_LRU), rev(D_LRU), prev8, rev(D_LRU), rev(1024), _full((4, D_LRU)), sq, vec, sq, vec, vec],
        [rev(1024), blocks, blocks, vec, vec, vec, _full((4, D_LRU)), vec],
        [_sds((s_len, 1024), BF16), _sds(blocks_shape, F32), _sds(blocks_shape, F32),
         _sds((1, D_LRU), F32), _sds((1, D_LRU), F32), _sds((1, D_LRU), F32),
         _sds((4, D_LRU), F32), _sds((1, D_LRU), F32)],
        [pltpu.VMEM((tl + 8, D_LRU), F32)] * 3 + [pltpu.VMEM((tl, D_LRU), F32)] * 3
        + [pltpu.VMEM((8, D_LRU), F32)] + [pltpu.VMEM((D_LRU, D_LRU), F32)] * 2,
        (drec, hs, hs, u, xg, conv_w, wrg, brg, wig, big, lam), "arbitrary", comm)


def _b_attn(qkv_pad, att, datt, frow, comm=None):
    s_len = datt.shape[0]
    nb = s_len // QB
    n_pair = ATT_HEADS // 2
    pair_w = 2 * HEAD_DIM

    def body(q_ref, k0, k1, k2, v0, v1, v2, o_ref, do_ref, frow_ref, dq_ref, dkv_ref, dfrow_ref,
             bias_sc, dt_sc, acc_sc):
        t = pl.program_id(0)

        @pl.when(t == 0)
        def _():
            _bias_table(frow_ref, bias_sc)
            dt_sc[...] = jnp.zeros_like(dt_sc)
            acc_sc[...] = jnp.zeros_like(acc_sc)

        @pl.when(t < nb)
        def _():
            var = jnp.minimum(t, N_BIAS - 1)
            even = _even_lanes()
            for hp in range(n_pair):
                cs = slice(hp * pair_w, (hp + 1) * pair_w)
                qt = q_ref[:, cs]
                kts = [k0[:, cs], k1[:, cs], k2[:, cs]]
                vts = [v0[:, cs], v1[:, cs], v2[:, cs]]
                kcat = jnp.concatenate(kts, axis=0)
                dot = do_ref[:, cs]
                dd = dot * o_ref[:, cs]
                dos_pair, dsbs, pbs, dqs = None, [], [], []
                for e in range(2):
                    keep = even if e == 0 else jnp.logical_not(even)
                    qm = jnp.where(keep, qt, 0)
                    p = _att_probs(qm, kts, bias_sc[var, 2 * hp + e])
                    inv = 1.0 / jnp.sum(p, axis=-1, keepdims=True)
                    dos = jnp.where(keep, dot * inv, 0.0)
                    delta = jnp.sum(jnp.where(keep, dd, 0.0), axis=-1, keepdims=True) * inv
                    dp = jnp.concatenate([_dot_nt(dos.astype(BF16), v) for v in vts], axis=1)
                    ds = p * (dp - delta)
                    dt_sc[2 * hp + e] += ds
                    dsb = ds.astype(BF16)
                    dq = _dot(dsb, kcat)
                    dqs.append(dq)
                    dsbs.append(dsb)
                    pbs.append(p.astype(BF16))
                    dos_pair = dos if e == 0 else dos_pair + dos
                dq_ref[:, cs] = (jnp.where(even, dqs[0], dqs[1]) * ATT_SCALE).astype(BF16)
                qtt = qt.astype(F32).T.astype(BF16)
                dost = dos_pair.T.astype(BF16)
                for j in range(3):
                    slot = (t + 1 + j) % 3
                    js = slice(j * QB, (j + 1) * QB)
                    for e in range(2):
                        hr = slice(e * HEAD_DIM, (e + 1) * HEAD_DIM)
                        acc_sc[slot, hp, hr, :] += _dot(qtt[hr], dsbs[e][:, js])
                        acc_sc[slot, n_pair + hp, hr, :] += _dot(dost[hr], pbs[e][:, js])

        done = (t + 1) % 3

        @pl.when(t >= 2)
        def _():
            for i in range(2 * n_pair):
                dkv_ref[:, i * pair_w:(i + 1) * pair_w] = acc_sc[done, i].T.astype(BF16)

        acc_sc[done] = jnp.zeros((2 * n_pair, pair_w, QB), F32)

        @pl.when(t == nb + 1)
        def _():
            row = lax.broadcasted_iota(jnp.int32, (8, ROLL_W), 0)
            pad = jnp.zeros((8, ROLL_W - KB), F32)
            for h in range(ATT_HEADS):
                acc8 = jnp.concatenate([dt_sc[h, 0:8, :], pad], axis=1)
                for a1 in range(1, QB // 8):
                    blk = jnp.concatenate([dt_sc[h, 8 * a1:8 * a1 + 8, :], pad], axis=1)
                    acc8 = acc8 + pltpu.roll(blk, ROLL_W - 8 * a1, 1)
                for k in range(3):
                    acc8 = jnp.where(((row >> k) & 1) == 1, pltpu.roll(acc8, ROLL_W - (1 << k), 1), acc8)
                dfrow_ref[h:h + 1, :] = jnp.sum(acc8, axis=0, keepdims=True)

    clamp = lambda t: jnp.minimum(t, nb - 1)
    qrows = pl.BlockSpec((QB, D_ATT), lambda t: (clamp(t), 0))
    return _call(
        body, "b_attn", (nb + 2,),
        _att_in_specs(clamp) + [qrows, qrows, _full((ATT_HEADS, ROLL_W))],
        [qrows, pl.BlockSpec((QB, 2 * D_ATT), lambda t: (jnp.maximum(t - 2, 0), 0)),
         _full((ATT_HEADS, ROLL_W))],
        [_sds((s_len, D_ATT), BF16), _sds((s_len, 2 * D_ATT), BF16), _sds((ATT_HEADS, ROLL_W), F32)],
        [pltpu.VMEM((N_BIAS, ATT_HEADS, QB, KB), F32), pltpu.VMEM((ATT_HEADS, QB, KB), F32),
         pltpu.VMEM((3, 2 * n_pair, pair_w, QB), F32)],
        (*([qkv_pad] * 7), att, datt, frow), "arbitrary", comm)


def _flush_grad(steps, acc, accb, out_hbm, outb_hbm):
    @pl.when(pl.program_id(0) == steps - 1)
    def _():
        accb[...] = acc[...].astype(BF16)
        pltpu.sync_copy(acc, out_hbm)
        pltpu.sync_copy(accb, outb_hbm)


def _b_win(dq, dkv, dxg, h, ts):
    s_len = h.shape[0]
    steps = s_len // ts

    def body(dq_ref, dkv_ref, dxg_ref, h_ref, dw_hbm, dwb_hbm, acc, accb):
        @pl.when(pl.program_id(0) == 0)
        def _():
            acc[...] = jnp.zeros_like(acc)

        dproj = jnp.concatenate([dq_ref[...], dkv_ref[...], dxg_ref[...]], axis=1)
        hv = h_ref[...]
        for s in range(N_SHARD):
            acc[s] += _dot_tn(hv, dproj[:, s * IN_SH:(s + 1) * IN_SH])
        _flush_grad(steps, acc, accb, dw_hbm, dwb_hbm)

    shape = (N_SHARD, 1024, IN_SH)
    return pl.pallas_call(
        body, name="b_win", grid=(steps,),
        in_specs=[_rows(ts, 512), _rows(ts, 1024), _rows(ts, 1024), _rows(ts, 1024)],
        out_specs=[_any()] * 2, out_shape=[_sds(shape, F32), _sds(shape, BF16)],
        scratch_shapes=[pltpu.VMEM(shape, F32), pltpu.VMEM(shape, BF16)],
        compiler_params=_cp("arbitrary"))(dq, dkv, dxg, h)


def _b_inproj(dq, dkv, dxg, x, dx1, g_mix, w_in_g, tm, comm=None):
    s_len = x.shape[0]

    def body(dq_ref, dkv_ref, dxg_ref, x_ref, dx1_ref, g_ref, w_hbm, gx_ref, dgm_ref, w_ref):
        _load_w_in_once(w_hbm, w_ref)

        @pl.when(pl.program_id(0) == 0)
        def _():
            dgm_ref[...] = jnp.zeros_like(dgm_ref)

        dproj = jnp.concatenate([dq_ref[...], dkv_ref[...], dxg_ref[...]], axis=1)
        dh = _dot_nt(dproj, w_ref[...])
        dx, dgm = _rms_bwd(dh, x_ref[...], g_ref[...])
        gx_ref[...] = dx1_ref[...] + dx
        dgm_ref[...] += dgm

    return _call(
        body, "b_inproj", (s_len // tm,),
        [_rows(tm, 512), _rows(tm, 1024), _rows(tm, 1024), _rows(tm, 1024), _rows(tm, 1024),
         _full((1, 1024)), _any()],
        [_rows(tm, 1024), _full((1, 1024))],
        [_sds((s_len, 1024), F32), _sds((1, 1024), F32)],
        [pltpu.VMEM((1024, D_IN), BF16)], (dq, dkv, dxg, x, dx1, g_mix, w_in_g), "arbitrary", comm)


def _mm_tn(xa, ya, name, ts):
    s_len, k = xa.shape
    n = ya.shape[1]

    steps = s_len // ts

    def body(x_ref, y_ref, o_hbm, ob_hbm, acc, accb):
        @pl.when(pl.program_id(0) == 0)
        def _():
            acc[...] = jnp.zeros_like(acc)
        acc[...] += _dot_tn(x_ref[...].astype(BF16), y_ref[...].astype(BF16))
        _flush_grad(steps, acc, accb, o_hbm, ob_hbm)

    return pl.pallas_call(
        body, name=name, grid=(steps,), in_specs=[_rows(ts, k), _rows(ts, n)],
        out_specs=[_any()] * 2, out_shape=[_sds((k, n), F32), _sds((k, n), BF16)],
        scratch_shapes=[pltpu.VMEM((k, n), F32), pltpu.VMEM((k, n), BF16)],
        compiler_params=_cp("arbitrary"))(xa, ya)


PAD_KEYS = LEFT_CHUNKS * CHUNK
F_HI = PAD_KEYS - MAX_REL + 1
F_LO = PAD_KEYS + MAX_REL


def _frow_from_rel_bias(rb):
    last = rb[:, 2 * MAX_REL:2 * MAX_REL + 1]
    hi = jnp.broadcast_to(last, (ATT_HEADS, F_HI))
    mid = rb[:, 1:2 * MAX_REL][:, ::-1]
    lo = jnp.broadcast_to(rb[:, 0:1], (ATT_HEADS, KB - F_LO))
    wrap = jnp.broadcast_to(last, (ATT_HEADS, ROLL_W - KB))
    return jnp.concatenate([hi, mid, lo, wrap], axis=1)


def _rel_bias_grad_from_dfrow(df):
    g_last = jnp.sum(df[:, 0:F_HI], axis=1, keepdims=True) + jnp.sum(df[:, KB:ROLL_W], axis=1, keepdims=True)
    mid = df[:, F_HI:F_LO][:, ::-1]
    g_first = jnp.sum(df[:, F_LO:KB], axis=1, keepdims=True)
    return jnp.concatenate([g_first, mid, g_last], axis=1)


def _block_diag(w):
    eye = jnp.eye(8, dtype=w.dtype)
    return (w[:, :, None, :] * eye[:, None, :, None]).reshape(D_LRU, D_LRU)


MID = ['w_out', 'wq_c', 'wk_c', 'wv_c', 'wo_c']
TRANSPOSED = ['w_gate', 'w_up']
AG_IN_INPROJ = ['w_out', 'wq_c', 'wk_c']
AG_IN_ATTN = ['wv_c', 'wo_c', 'w_gate']
AG_IN_LRU = ['w_up']
AG_IN_MID = ['w_down']
RS_IN_MID = ['w_gate', 'w_up']
RS_IN_LRU = ['w_down']
RS_IN_ATTN = MID


def _local_step(x, mem, tgt, p, gw, shards=None, chip=None):
    s_len = x.shape[0]
    tm = min(256, s_len)
    tmb = min(512, s_len)
    tl = min(512, s_len)
    frow = _frow_from_rel_bias(p['rel_bias'])
    wrg = _block_diag(p['w_rg']).astype(BF16)
    wig = _block_diag(p['w_ig']).astype(BF16)
    gw = dict(gw)

    big, bigb, recv, part, sib = {}, {}, {}, {}, {}

    def ag(names):
        return [] if shards is None else [("ag", [shards[n] for n in names])]

    def rs(names):
        return [] if shards is None else [("rs", [bigb[n] for n in names])]

    def swap(names):
        return [] if shards is None else [("swap", [part[n] for n in names])]

    def reduce_own(names):
        if shards is not None:
            sums = _sum_parts([big[n] for n in names], [recv[n] for n in names], chip, "sum_" + names[0])
            part.update(zip(names, sums))

    h, qkv_pad, xg, *got = _f_inproj(x, p['g_mix'], gw['w_in'], tmb, ag(AG_IN_INPROJ))
    gw.update(zip(AG_IN_INPROJ, got))
    att, *got = _f_attn(qkv_pad, frow, ag(AG_IN_ATTN))
    gw.update(zip(AG_IN_ATTN, got))
    rec, u, hs, *got = _f_lru(xg, p['conv_w'], p['conv_b'], wrg, p['b_rg'], wig, p['b_ig'], p['lru_L'], tl,
                              ag(AG_IN_LRU))
    gw.update(zip(AG_IN_LRU, got))
    w_out = gw['w_out'].reshape(1024, 1024)
    wq = gw['wq_c'].reshape(1024, 1024)
    wk = gw['wk_c'].reshape(1024, 1024)
    wv = gw['wv_c'].reshape(1024, 1024)
    wo = gw['wo_c'].reshape(1024, 1024)
    mn, kx, vx = _f_mem(mem, p['g_mem'], wk, wv)
    mg, x1, hc, qx, ox, x2, *got = _f_mid(x, att, rec, p['g_out_attn'], p['g_out_lru'], w_out, p['g_cross'],
                                          wq, kx, vx, wo, tmb, ag(AG_IN_MID))
    gw.update(zip(AG_IN_MID, got))
    ffn_w = [gw[n].reshape(D_FF, 1024) for n in ('w_gate', 'w_up', 'w_down')]
    hf, gact, uact, aact, dx3, loss, dg_final = _f_ffn(x2, tgt, p['g_ffn'], p['g_final'], *ffn_w, tmb)

    ts = min(1024, s_len)
    dgact, duact, dx2, dg_ffn = _b_ffn(dx3, x2, gact, uact, p['g_ffn'], *ffn_w, tm)
    big['w_gate'], bigb['w_gate'] = _mm_tn(dgact, hf, "dw_gate", ts)
    big['w_up'], bigb['w_up'] = _mm_tn(duact, hf, "dw_up", ts)
    big['w_down'], bigb['w_down'] = _mm_tn(aact, dx3, "dw_down", ts)
    for n in ('w_gate', 'w_up', 'w_down'):
        big[n] = big[n].reshape(N_SHARD, FF_SH, 1024)
        bigb[n] = bigb[n].reshape(N_SHARD, FF_SH, 1024)

    dqx, dx1, datt, drec, dkx, dvx, dg_cross, dg_oa, dg_ol, *got = _b_mid(
        dx2, qx, x1, att, rec, kx, vx, wo, wq, w_out, p['g_cross'], p['g_out_attn'], p['g_out_lru'], tmb,
        rs(RS_IN_MID))
    recv.update(zip(RS_IN_MID, got))
    reduce_own(RS_IN_MID)
    dwk, dwv, dg_mem, dwkb, dwvb = _b_mem(dkx, dvx, mem, mn, p['g_mem'], wk, wv)
    big['wk_c'], bigb['wk_c'] = dwk, dwkb
    big['wv_c'], bigb['wv_c'] = dwv, dwvb
    big['w_out'], bigb['w_out'] = _mm_tn(mg, dx1, "dw_out", ts)
    big['wq_c'], bigb['wq_c'] = _mm_tn(hc, dqx, "dw_q", ts)
    big['wo_c'], bigb['wo_c'] = _mm_tn(ox, dx2, "dw_o", ts)
    for n in MID:
        big[n] = big[n].reshape(N_SHARD, 256, 1024)
        bigb[n] = bigb[n].reshape(N_SHARD, 256, 1024)

    dxg, dwrg, dwig, dbrg, dbig, dlam, dcw, dcb, *got = _b_lru(
        drec, hs, u, xg, p['conv_w'], wrg, p['b_rg'], wig, p['b_ig'], p['lru_L'], tl,
        rs(RS_IN_LRU) + swap(RS_IN_MID))
    recv.update(zip(RS_IN_LRU, got))
    sib.update(zip(RS_IN_MID, got[len(RS_IN_LRU):]))
    reduce_own(RS_IN_LRU)
    small = {
        'conv_w': dcw, 'conv_b': dcb, 'w_rg': dwrg, 'b_rg': dbrg, 'w_ig': dwig, 'b_ig': dbig, 'lru_L': dlam,
        'g_out_attn': dg_oa, 'g_out_lru': dg_ol, 'g_cross': dg_cross, 'g_mem': dg_mem, 'g_ffn': dg_ffn,
        'g_final': dg_final,
    }
    names = [n for n in SMALL if n in small]
    gather = [] if shards is None else [("ag8", [_pack_small(names, [small[n] for n in names], loss)])]
    dq, dkv, dfrow, *got = _b_attn(qkv_pad, att, datt, frow, rs(RS_IN_ATTN) + swap(RS_IN_LRU) + gather)
    recv.update(zip(RS_IN_ATTN, got))
    sib.update(zip(RS_IN_LRU, got[len(RS_IN_ATTN):]))
    packs = got[-1] if gather else None
    reduce_own(RS_IN_ATTN)
    small['rel_bias'] = _rel_bias_grad_from_dfrow(dfrow)
    big['w_in'], bigb['w_in'] = _b_win(dq, dkv, dxg, h, ts)
    grad_x, small['g_mix'], *got = _b_inproj(dq, dkv, dxg, x, dx1, p['g_mix'], gw['w_in'], tmb,
                                             rs(['w_in']) + swap(RS_IN_ATTN))
    recv.update(zip(['w_in'], got))
    sib.update(zip(RS_IN_ATTN, got[1:]))
    reduce_own(['w_in'])
    return loss, grad_x, small, big, part, sib, packs


def _cast_shards(ws):
    def body(*refs):
        n = len(refs) // 2
        for src, dst in zip(refs[:n], refs[n:]):
            dst[...] = src[...].astype(BF16)

    return pl.pallas_call(body, name="cast_shards", out_shape=[_sds(w.shape, BF16) for w in ws],
                          compiler_params=_cp())(*ws)


def _sum_parts(own4s, recv3s, chip, name):
    n = len(own4s)
    _, r, c = own4s[0].shape
    steps = _ew_steps(r, n * c * (4 + 3 * 2 + 4))
    tr = r // steps

    def body(chip_ref, *refs):
        for own_ref, rc_ref, o_ref in zip(refs[:n], refs[n:2 * n], refs[2 * n:]):
            o_ref[...] = ((own_ref[0] + rc_ref[0].astype(F32)) + rc_ref[1].astype(F32)) + rc_ref[2].astype(F32)

    grid_spec = pltpu.PrefetchScalarGridSpec(
        num_scalar_prefetch=1, grid=(steps,),
        in_specs=[pl.BlockSpec((1, tr, c), lambda i, ch: (ch[0], i, 0))] * n
                 + [pl.BlockSpec((3, tr, c), lambda i, ch: (0, i, 0))] * n,
        out_specs=[pl.BlockSpec((tr, c), lambda i, ch: (i, 0))] * n)
    return pl.pallas_call(body, name=name, grid_spec=grid_spec, out_shape=[_sds((r, c), F32)] * n,
                          compiler_params=_cp("parallel"))(chip, *own4s, *recv3s)


def _adamw_math(w, g, m, v):
    m = ADAM_B1 * m + (1.0 - ADAM_B1) * g
    v = ADAM_B2 * v + (1.0 - ADAM_B2) * (g * g)
    m_hat = m / (1.0 - ADAM_B1 ** ADAM_STEP)
    v_hat = v / (1.0 - ADAM_B2 ** ADAM_STEP)
    delta = -ADAM_LR * (m_hat / (jnp.sqrt(v_hat) + ADAM_EPS) + ADAM_WD * w)
    return delta, m, v


def _final_adamw(pas, pbs, ws, ms, vs, name):
    n = len(ws)
    r, c = ws[0].shape
    steps = _ew_steps(r, n * c * 9 * 4)
    tr = r // steps

    def body(*refs):
        ins, outs = refs[:5 * n], refs[5 * n:]
        for k in range(n):
            pa_ref, pb_ref, w_ref, m_ref, v_ref = (ins[j * n + k] for j in range(5))
            g = pa_ref[...] + pb_ref[...]
            outs[4 * k][...] = g
            outs[4 * k + 1][...], outs[4 * k + 2][...], outs[4 * k + 3][...] = _adamw_math(
                w_ref[...], g, m_ref[...], v_ref[...])

    res = pl.pallas_call(
        body, name=name, grid=(steps,), in_specs=[_rows(tr, c)] * (5 * n), out_specs=[_rows(tr, c)] * (4 * n),
        out_shape=[_sds((r, c), F32)] * (4 * n), compiler_params=_cp("parallel"))(*pas, *pbs, *ws, *ms, *vs)
    return [res[4 * k:4 * k + 4] for k in range(n)]


def _pack_put(ref, name, val_ref):
    r = _pack_rows()[name]
    shape = val_ref.shape
    if len(shape) == 3:
        for b in range(shape[0]):
            ref[r:r + shape[1], b * shape[2]:(b + 1) * shape[2]] = val_ref[b]
    elif shape[1] == 2 * PACK_W:
        ref[r:r + 1, :] = val_ref[:, 0:PACK_W]
        ref[r + 1:r + 2, :] = val_ref[:, PACK_W:2 * PACK_W]
    else:
        ref[r:r + shape[0], 0:shape[1]] = val_ref[...]


def _pack_get(ref, name, shape):
    r = _pack_rows()[name]
    if len(shape) == 3:
        return jnp.stack([ref[r:r + shape[1], b * shape[2]:(b + 1) * shape[2]] for b in range(shape[0])])
    if shape[1] == 2 * PACK_W:
        return jnp.concatenate([ref[r:r + 1, :], ref[r + 1:r + 2, :]], axis=1)
    return ref[r:r + shape[0], 0:shape[1]]


def _pack_small(names, g, loss):
    n = len(g)

    def body(*refs):
        pack = refs[n + 1]
        pack[...] = jnp.zeros_like(pack)
        for a, name in enumerate(names):
            _pack_put(pack, name, refs[a])
        _pack_put(pack, 'loss', refs[n])

    return pl.pallas_call(body, name="pack_small", out_shape=_sds((PACK_ROWS, PACK_W), F32),
                          compiler_params=_cp())(*g, loss)


def _all_peers():
    x, y, c = _mesh_pos()
    peers = []
    for k in range(1, 8):
        px = 1 - x if k & 4 else x
        py = 1 - y if k & 2 else y
        pc = 1 - c if k & 1 else c
        peers.append(((px, py, pc), 4 * px + 2 * py + pc))
    return peers, 4 * x + 2 * y + c


def _ag8_copies(ins, outs, sems):
    send_sems, recv_sems, loc_sems = sems
    n = len(ins)
    peers, me = _all_peers()

    def remote(k, j, slot):
        return pltpu.make_async_remote_copy(
            src_ref=ins[k], dst_ref=outs[k].at[slot], send_sem=send_sems.at[k, j], recv_sem=recv_sems.at[k, j],
            device_id=peers[j][0], device_id_type=MESH_ID)

    def local(k):
        return pltpu.make_async_copy(ins[k], outs[k].at[me], loc_sems.at[k])

    def start():
        for k in range(n):
            local(k).start()
            for j in range(7):
                remote(k, j, me).start()

    def wait():
        for k in range(n):
            for j in range(7):
                remote(k, j, peers[j][1]).wait_recv()
        for k in range(n):
            for j in range(7):
                remote(k, j, me).wait_send()
            local(k).wait()

    return start, _no_forward, wait


def _ar_late(names, g):
    n = len(g)

    def body(*refs):
        tot_ref, pack, buf, send_sems, recv_sems = refs[n:]
        peers, me = _all_peers()

        def remote(j, slot):
            return pltpu.make_async_remote_copy(
                src_ref=pack, dst_ref=buf.at[slot], send_sem=send_sems.at[j], recv_sem=recv_sems.at[j],
                device_id=peers[j][0], device_id_type=MESH_ID)

        pack[...] = jnp.zeros_like(pack)
        for a, name in enumerate(names):
            _pack_put(pack, name, refs[a])
        for j in range(7):
            remote(j, me).start()
        buf[me] = pack[...]
        for j in range(7):
            remote(j, peers[j][1]).wait_recv()
        for j in range(7):
            remote(j, me).wait_send()
        tot = buf[0]
        for d in range(1, 8):
            tot = tot + buf[d]
        tot_ref[...] = tot

    return pl.pallas_call(
        body, name="ar_late", out_shape=_sds((LATE_ROWS, PACK_W), F32),
        scratch_shapes=[pltpu.VMEM((LATE_ROWS, PACK_W), F32), pltpu.VMEM((8, LATE_ROWS, PACK_W), F32),
                        pltpu.SemaphoreType.DMA((7,)), pltpu.SemaphoreType.DMA((7,))],
        compiler_params=_cp())(*g)


def _adamw_small(packs, late_tot, g_shapes, loss_shape, w, m, v):
    n = len(w)

    def body(*refs):
        packs_ref, late_ref = refs[0], refs[1]
        w_refs, m_refs, v_refs = (refs[2 + i * n:2 + (i + 1) * n] for i in range(3))
        o0 = 3 * n + 2
        go, do, mo, vo = (refs[o0 + i * n:o0 + (i + 1) * n] for i in range(4))
        loss_out, tot_ref = refs[o0 + 4 * n], refs[o0 + 4 * n + 1]
        x, y, _ = _mesh_pos()
        tot = packs_ref[0]
        for d in range(1, 8):
            tot = tot + packs_ref[d]
        tot_ref[...] = tot
        tot_ref[0:LATE_ROWS, :] += late_ref[...]
        loss_out[...] = _pack_get(tot_ref, 'loss', loss_shape)
        for a, name in enumerate(SMALL):
            if name == 'conv_w':
                r = _pack_rows()[name]
                ga = tot_ref[r:r + g_shapes[a][0], pl.ds(pl.multiple_of((2 * x + y) * 128, 128), 128)]
            else:
                ga = _pack_get(tot_ref, name, g_shapes[a])
            go[a][...] = ga
            do[a][...], mo[a][...], vo[a][...] = _adamw_math(w_refs[a][...], ga, m_refs[a][...], v_refs[a][...])

    out_shape = [_sds(a.shape, F32) for a in w] * 4 + [_sds(loss_shape, F32)]
    return pl.pallas_call(body, name="adamw_small", out_shape=out_shape,
                          scratch_shapes=[pltpu.VMEM((PACK_ROWS, PACK_W), F32)],
                          compiler_params=_cp())(packs, late_tot, *w, *m, *v)


PACK_W = 512
PACK_ROWS = 160
LATE = ['g_mix', 'rel_bias']
LATE_ROWS = 32


def _pack_rows():
    rows, r = {}, 0
    for name in ['g_mix', 'g_cross', 'g_mem', 'g_ffn', 'g_final']:
        rows[name] = r
        r += 2
    for name in ['conv_b', 'b_rg', 'b_ig', 'lru_L', 'g_out_attn', 'g_out_lru']:
        rows[name] = r
        r += 1
    rows['conv_w'] = r
    rows['loss'] = r + 4
    rows['rel_bias'] = 24
    rows['w_rg'] = 32
    rows['w_ig'] = 32 + LRU_BLOCK
    assert r + 5 <= 24 and rows['w_ig'] + LRU_BLOCK == PACK_ROWS
    assert rows['g_mix'] + 2 <= LATE_ROWS and rows['rel_bias'] + 8 <= LATE_ROWS
    return rows


INPUT_NAMES = (['x', 'mem'] + WEIGHTS + ['loss_target'] + ['m_' + n for n in WEIGHTS] + ['v_' + n for n in WEIGHTS])


def kernel(x, mem, g_mix, w_in, rel_bias, conv_w, conv_b, w_rg, b_rg, w_ig, b_ig, lru_L, g_out_attn, g_out_lru, w_out, g_cross, g_mem, wq_c, wk_c, wv_c, wo_c, g_ffn, w_gate, w_up, w_down, g_final, loss_target, m_g_mix, m_w_in, m_rel_bias, m_conv_w, m_conv_b, m_w_rg, m_b_rg, m_w_ig, m_b_ig, m_lru_L, m_g_out_attn, m_g_out_lru, m_w_out, m_g_cross, m_g_mem, m_wq_c, m_wk_c, m_wv_c, m_wo_c, m_g_ffn, m_w_gate, m_w_up, m_w_down, m_g_final, v_g_mix, v_w_in, v_rel_bias, v_conv_w, v_conv_b, v_w_rg, v_b_rg, v_w_ig, v_b_ig, v_lru_L, v_g_out_attn, v_g_out_lru, v_w_out, v_g_cross, v_g_mem, v_wq_c, v_wk_c, v_wv_c, v_wo_c, v_g_ffn, v_w_gate, v_w_up, v_w_down, v_g_final):
    a = dict(zip(INPUT_NAMES, (x, mem, g_mix, w_in, rel_bias, conv_w, conv_b, w_rg, b_rg, w_ig, b_ig, lru_L, g_out_attn, g_out_lru, w_out, g_cross, g_mem, wq_c, wk_c, wv_c, wo_c, g_ffn, w_gate, w_up, w_down, g_final, loss_target, m_g_mix, m_w_in, m_rel_bias, m_conv_w, m_conv_b, m_w_rg, m_b_rg, m_w_ig, m_b_ig, m_lru_L, m_g_out_attn, m_g_out_lru, m_w_out, m_g_cross, m_g_mem, m_wq_c, m_wk_c, m_wv_c, m_wo_c, m_g_ffn, m_w_gate, m_w_up, m_w_down, m_g_final, v_g_mix, v_w_in, v_rel_bias, v_conv_w, v_conv_b, v_w_rg, v_b_rg, v_w_ig, v_b_ig, v_lru_L, v_g_out_attn, v_g_out_lru, v_w_out, v_g_cross, v_g_mem, v_wq_c, v_wk_c, v_wv_c, v_wo_c, v_g_ffn, v_w_gate, v_w_up, v_w_down, v_g_final)))
    chip = 2 * lax.axis_index("x") + lax.axis_index("y")

    def shard(name):
        arr = a[name][0]
        base = name[2:] if name[:2] in ('m_', 'v_') else name
        return jnp.swapaxes(arr, 0, 1) if base in TRANSPOSED else arr

    shards = dict(zip(BIG, _cast_shards([shard(n) for n in BIG])))
    w_in_g, conv_w_g = _comm_only("ag_w_in", [("ag", [shards['w_in']]), ("agf", [a['conv_w'][0]])])
    conv_w_full = conv_w_g.transpose(1, 0, 2).reshape(4, D_LRU)

    p = {n: a[n] for n in SMALL}
    p['rel_bias'] = a['rel_bias'][0]
    p['w_rg'] = a['w_rg'][0]
    p['w_ig'] = a['w_ig'][0]
    p['conv_w'] = conv_w_full
    p['g_final'] = a['g_final'][None, :]
    chip_arr = jnp.reshape(chip, (1,)).astype(jnp.int32)
    loss_part, grad_x, small, _, part, sib, packs = _local_step(
        a['x'][0], a['mem'][0], a['loss_target'][0], p, {'w_in': w_in_g}, shards, chip_arr)

    sib['w_in'], = _comm_only("swap_w_in", [("swap", [part['w_in']])])
    out = {}
    for group in (['w_in'], MID, ['w_gate', 'w_up', 'w_down']):
        results = _final_adamw([part[n] for n in group], [sib[n] for n in group], [shard(n) for n in group],
                               [shard('m_' + n) for n in group], [shard('v_' + n) for n in group],
                               "adamw_" + group[0])
        for n, res in zip(group, results):
            out[n] = [jnp.swapaxes(r, 0, 1) for r in res] if n in TRANSPOSED else res

    def natural(arr):
        return arr[0] if arr.ndim >= 3 else (arr[None, :] if arr.ndim == 1 else arr)

    small_out = _adamw_small(packs, _ar_late(LATE, [small[n] for n in LATE]), [small[n].shape for n in SMALL],
                             loss_part.shape, *[[natural(a[pre + n]) for n in SMALL] for pre in ('', 'm_', 'v_')])
    ns = len(SMALL)
    loss = small_out[4 * ns][0, 0]

    def leaf(i, n):
        if n in BIG:
            return out[n][i][None]
        return small_out[i * ns + SMALL.index(n)].reshape(a[n].shape)

    return (loss, grad_x[None], *[leaf(i, n) for i in range(4) for n in WEIGHTS])
```

```python
import math

import jax
import jax.numpy as jnp
from jax import lax
from jax.experimental import pallas as pl
from jax.experimental.pallas import tpu as pltpu

F32 = jnp.float32
BF16 = jnp.bfloat16

D_MODEL = 1024
D_ATT = 512
D_LRU = 512
HEAD_DIM = 64
ATT_HEADS = 8
CHUNK = 64
LEFT_CHUNKS = 8
MAX_REL = 128
X_HEADS = 4
X_HEAD_DIM = 256
N_SHARD = 4
IN_SH = 640
D_IN = N_SHARD * IN_SH
FF_SH = 704
D_FF = N_SHARD * FF_SH
EPS = 1e-6
LRU_C = 8.0
LRU_BLOCKS = 8
LRU_BLOCK = 64
QB = 256
KB = 768
ROLL_W = 1024
NEG = -1e30
ATT_SCALE = HEAD_DIM ** -0.5
X_SCALE = X_HEAD_DIM ** -0.5

ADAM_LR = 0.001
ADAM_B1 = 0.9
ADAM_B2 = 0.999
ADAM_EPS = 1e-08
ADAM_WD = 0.01
ADAM_STEP = 10

VMEM_LIMIT_V7X = 56 * 1024 * 1024
BF16_ROWS = 16


EW_VMEM_BUDGET = 40 * 1024 * 1024


def _ew_steps(rows, bytes_per_row):
    return min(s for s in (2, 4, 8, 16) if rows % (s * BF16_ROWS) == 0
               and 2 * (rows // s) * bytes_per_row <= EW_VMEM_BUDGET)
MESH_ID = pl.DeviceIdType.MESH

WEIGHTS = ['g_mix', 'w_in', 'rel_bias', 'conv_w', 'conv_b', 'w_rg', 'b_rg', 'w_ig', 'b_ig', 'lru_L',
           'g_out_attn', 'g_out_lru', 'w_out', 'g_cross', 'g_mem', 'wq_c', 'wk_c', 'wv_c', 'wo_c',
           'g_ffn', 'w_gate', 'w_up', 'w_down', 'g_final']
BIG = ['w_in', 'w_out', 'wq_c', 'wk_c', 'wv_c', 'wo_c', 'w_gate', 'w_up', 'w_down']
SMALL = [n for n in WEIGHTS if n not in BIG]


def _sds(shape, dtype):
    return jax.ShapeDtypeStruct(shape, dtype)


def _cp(*sem):
    return pltpu.CompilerParams(dimension_semantics=sem or None, vmem_limit_bytes=VMEM_LIMIT_V7X)


def _rows(tm, n):
    return pl.BlockSpec((tm, n), lambda i: (i, 0))


def _full(shape):
    nd = len(shape)
    return pl.BlockSpec(shape, lambda i: (0,) * nd)


def _dot(a, b):
    return jnp.dot(a, b, preferred_element_type=F32)


def _dot_nt(a, b):
    return lax.dot_general(a, b, (((1,), (1,)), ((), ())), preferred_element_type=F32)


def _dot_tn(a, b):
    return lax.dot_general(a, b, (((0,), (0,)), ((), ())), preferred_element_type=F32)


def _rinv(x):
    return lax.rsqrt(jnp.mean(x * x, axis=-1, keepdims=True) + EPS)


def _rms_bwd(dy, x, g):
    r = _rinv(x)
    yh = x * r
    dyh = dy * g
    dx = r * (dyh - yh * jnp.mean(dyh * yh, axis=-1, keepdims=True))
    return dx, jnp.sum(dy * yh, axis=0, keepdims=True)


def _gelu(x):
    c = math.sqrt(2.0 / math.pi)
    t = jnp.tanh(c * (x + 0.044715 * x * x * x))
    return 0.5 * x * (1.0 + t)


def _gelu_and_grad(x):
    c = math.sqrt(2.0 / math.pi)
    t = jnp.tanh(c * (x + 0.044715 * x * x * x))
    g = 0.5 * x * (1.0 + t)
    dg = 0.5 * (1.0 + t) + 0.5 * x * (1.0 - t * t) * c * (1.0 + 3.0 * 0.044715 * x * x)
    return g, dg


def _neg_expm1(z):
    series = -z * (1.0 + z * (0.5 + z * ((1.0 / 6.0) + z * (1.0 / 24.0))))
    return jnp.where(z > -0.03, series, 1.0 - jnp.exp(z))


def _lru_gates(u, wrg, brg, wig, big, lam):
    ub = u.astype(BF16)
    r = jax.nn.sigmoid(_dot(ub, wrg) + brg)
    ig = jax.nn.sigmoid(_dot(ub, wig) + big)
    sp = jnp.maximum(-lam, 0.0) + jnp.log1p(jnp.exp(-jnp.abs(lam)))
    la = -LRU_C * r * sp
    a = jnp.exp(la)
    mult = jnp.sqrt(jnp.maximum(_neg_expm1(2.0 * la), 0.0))
    return ub, r, ig, sp, a, mult


def _scan8(a8, b8, hprev):
    row = lax.broadcasted_iota(jnp.int32, a8.shape, 0)
    aa, bb = a8, b8
    for d in (1, 2, 4):
        a_s = pltpu.roll(aa, d, 0)
        b_s = pltpu.roll(bb, d, 0)
        m = row >= d
        bb = jnp.where(m, aa * b_s + bb, bb)
        aa = jnp.where(m, aa * a_s, aa)
    return aa * hprev + bb


def _rscan8(c8, d8, lnext):
    row = lax.broadcasted_iota(jnp.int32, c8.shape, 0)
    cc, dd = c8, d8
    for d in (1, 2, 4):
        c_s = pltpu.roll(cc, 8 - d, 0)
        d_s = pltpu.roll(dd, 8 - d, 0)
        m = row < 8 - d
        dd = jnp.where(m, cc * d_s + dd, dd)
        cc = jnp.where(m, cc * c_s, cc)
    return cc * lnext + dd


def _mesh_pos():
    return lax.axis_index("x"), lax.axis_index("y"), lax.axis_index("c")


def _other_chips(x, y):
    return [(1 - x, y), (x, 1 - y), (1 - x, 1 - y)]


def _no_forward():
    pass


def _ag_full_copies(ins, outs, sems):
    send_sems, recv_sems, loc_sems = sems
    n = len(ins)
    x, y, c = _mesh_pos()
    mine = 2 * x + y
    chips = _other_chips(x, y)

    def remote(k, j, slot):
        px, py = chips[j]
        return pltpu.make_async_remote_copy(
            src_ref=ins[k], dst_ref=outs[k].at[slot], send_sem=send_sems.at[k, j], recv_sem=recv_sems.at[k, j],
            device_id=(px, py, c), device_id_type=MESH_ID)

    def local(k):
        return pltpu.make_async_copy(ins[k], outs[k].at[mine], loc_sems.at[k])

    def start():
        for k in range(n):
            local(k).start()
            for j in range(3):
                remote(k, j, mine).start()

    def wait():
        for k in range(n):
            for j, (px, py) in enumerate(chips):
                remote(k, j, 2 * px + py).wait_recv()
        for k in range(n):
            for j in range(3):
                remote(k, j, mine).wait_send()
            local(k).wait()

    return start, _no_forward, wait


def _ag_copies(ins, outs, sems):
    send_sems, recv_sems, fsend_sems, frecv_sems, loc_sems = sems
    n = len(ins)
    x, y, c = _mesh_pos()
    mine = 2 * x + y
    chips = _other_chips(x, y)

    def half(ref, hc):
        r = ref.shape[0] // 2
        return ref.at[pl.ds(pl.multiple_of(hc * r, 16), r)]

    def ici(k, j, slot):
        px, py = chips[j]
        return pltpu.make_async_remote_copy(
            src_ref=half(ins[k], c), dst_ref=half(outs[k].at[slot], c),
            send_sem=send_sems.at[k, j], recv_sem=recv_sems.at[k, j],
            device_id=(px, py, c), device_id_type=MESH_ID)

    def d2d(k, j, hc):
        px, py = chips[j]
        part = half(outs[k].at[2 * px + py], hc)
        return pltpu.make_async_remote_copy(
            src_ref=part, dst_ref=part, send_sem=fsend_sems.at[k, j], recv_sem=frecv_sems.at[k, j],
            device_id=(x, y, 1 - c), device_id_type=MESH_ID)

    def local(k):
        return pltpu.make_async_copy(ins[k], outs[k].at[mine], loc_sems.at[k])

    def start():
        for k in range(n):
            local(k).start()
            for j in range(3):
                ici(k, j, mine).start()

    def forward():
        for k in range(n):
            for j, (px, py) in enumerate(chips):
                ici(k, j, 2 * px + py).wait_recv()
                d2d(k, j, c).start()

    def wait():
        for k in range(n):
            for j in range(3):
                d2d(k, j, 1 - c).wait_recv()
        for k in range(n):
            for j in range(3):
                d2d(k, j, c).wait_send()
                ici(k, j, mine).wait_send()
            local(k).wait()

    return start, forward, wait


def _rs_copies(ins, outs, sems):
    send_sems, recv_sems = sems
    n = len(ins)
    x, y, c = _mesh_pos()
    chips = _other_chips(x, y)

    def remote(k, j):
        px, py = chips[j]
        return pltpu.make_async_remote_copy(
            src_ref=ins[k].at[2 * px + py], dst_ref=outs[k].at[j],
            send_sem=send_sems.at[k, j], recv_sem=recv_sems.at[k, j],
            device_id=(px, py, c), device_id_type=MESH_ID)

    def start():
        for k in range(n):
            for j in range(3):
                remote(k, j).start()

    def wait():
        for k in range(n):
            for j in range(3):
                remote(k, j).wait_recv()
        for k in range(n):
            for j in range(3):
                remote(k, j).wait_send()

    return start, _no_forward, wait


def _swap_copies(ins, outs, sems):
    send_sems, recv_sems = sems
    x, y, c = _mesh_pos()
    copies = [pltpu.make_async_remote_copy(
        src_ref=ins[k], dst_ref=outs[k], send_sem=send_sems.at[k], recv_sem=recv_sems.at[k],
        device_id=(x, y, 1 - c), device_id_type=MESH_ID) for k in range(len(ins))]

    def start():
        for cp in copies:
            cp.start()

    def wait():
        for cp in copies:
            cp.wait()

    return start, _no_forward, wait


def _comm_plan(groups):
    plan, arrs, shapes, sems = [], [], [], []
    for kind, group in groups:
        k = len(group)
        arrs += group
        per_peer = pltpu.SemaphoreType.DMA((k, 3))
        if kind == "ag":
            shapes += [_sds((N_SHARD,) + w.shape, w.dtype) for w in group]
            gsems = [per_peer] * 4 + [pltpu.SemaphoreType.DMA((k,))]
            maker = _ag_copies
        elif kind == "agf":
            shapes += [_sds((N_SHARD,) + w.shape, w.dtype) for w in group]
            gsems = [per_peer] * 2 + [pltpu.SemaphoreType.DMA((k,))]
            maker = _ag_full_copies
        elif kind == "ag8":
            shapes += [_sds((8,) + g.shape, g.dtype) for g in group]
            gsems = [pltpu.SemaphoreType.DMA((k, 7))] * 2 + [pltpu.SemaphoreType.DMA((k,))]
            maker = _ag8_copies
        elif kind == "rs":
            shapes += [_sds((3,) + g.shape[1:], g.dtype) for g in group]
            gsems = [pltpu.SemaphoreType.DMA((k, 3)), pltpu.SemaphoreType.DMA((k, 3))]
            maker = _rs_copies
        else:
            shapes += [_sds(g.shape, g.dtype) for g in group]
            gsems = [pltpu.SemaphoreType.DMA((k,)), pltpu.SemaphoreType.DMA((k,))]
            maker = _swap_copies
        plan.append((maker, k, len(gsems)))
        sems += gsems
    return plan, arrs, shapes, sems


def _comm_fns(plan, cins, couts, sems):
    fns, a, s = [], 0, 0
    for maker, k, ns in plan:
        fns.append(maker(cins[a:a + k], couts[a:a + k], sems[s:s + ns]))
        a += k
        s += ns

    def start():
        for st, _, _ in fns:
            st()

    def forward():
        for _, fw, _ in fns:
            fw()

    def wait():
        for _, _, wt in fns:
            wt()

    return start, forward, wait


def _call(body, name, grid, in_specs, out_specs, out_shape, scratch, args, sem, comm=None):
    if not comm:
        return pl.pallas_call(body, name=name, grid=grid, in_specs=in_specs, out_specs=out_specs,
                              out_shape=out_shape, scratch_shapes=scratch, compiler_params=_cp(sem))(*args)
    plan, c_arrs, c_shapes, c_sems = _comm_plan(comm)
    k = len(c_arrs)
    n_in, n_out, n_scr = len(in_specs), len(out_specs), len(scratch)
    last = grid[0] - 1
    fwd_step = max(1, (2 * last) // 3)

    def wrapped(*refs):
        ins, cins = refs[:n_in], refs[n_in:n_in + k]
        o0 = n_in + k
        outs, couts = refs[o0:o0 + n_out], refs[o0 + n_out:o0 + n_out + k]
        s0 = o0 + n_out + k
        start, forward, wait = _comm_fns(plan, cins, couts, refs[s0 + n_scr:])
        pl.when(pl.program_id(0) == 0)(start)
        pl.when(pl.program_id(0) == fwd_step)(forward)
        body(*ins, *outs, *refs[s0:s0 + n_scr])
        pl.when(pl.program_id(0) == last)(wait)

    return pl.pallas_call(
        wrapped, name=name, grid=grid, in_specs=list(in_specs) + [_any()] * k,
        out_specs=list(out_specs) + [_any()] * k, out_shape=list(out_shape) + c_shapes,
        scratch_shapes=list(scratch) + c_sems, compiler_params=_cp(sem))(*args, *c_arrs)


def _comm_only(name, comm):
    plan, c_arrs, c_shapes, c_sems = _comm_plan(comm)
    k = len(c_arrs)

    def body(*refs):
        start, forward, wait = _comm_fns(plan, refs[:k], refs[k:2 * k], refs[2 * k:])
        start()
        forward()
        wait()

    return pl.pallas_call(body, name=name, in_specs=[_any()] * k, out_specs=[_any()] * k, out_shape=c_shapes,
                          scratch_shapes=c_sems, compiler_params=_cp())(*c_arrs)


def _any():
    return pl.BlockSpec(memory_space=pl.ANY)


def _load_w_in_once(w_hbm, w_ref):
    @pl.when(pl.program_id(0) == 0)
    def _():
        for s in range(N_SHARD):
            pltpu.sync_copy(w_hbm.at[s], w_ref.at[:, pl.ds(s * IN_SH, IN_SH)])


def _f_inproj(x, g_mix, w_in_g, tm, comm=None):
    s_len = x.shape[0]
    pad_rows = LEFT_CHUNKS * CHUNK
    npad = pad_rows // tm

    def body(x_ref, g_ref, w_hbm, h_ref, qkv_ref, xg_ref, w_ref):
        i = pl.program_id(0)
        _load_w_in_once(w_hbm, w_ref)

        @pl.when(i < npad)
        def _():
            qkv_ref[...] = jnp.zeros_like(qkv_ref)

        @pl.when(i >= npad)
        def _():
            xv = x_ref[...]
            h = (xv * _rinv(xv) * g_ref[...]).astype(BF16)
            h_ref[...] = h
            proj = _dot(h, w_ref[...])
            qkv_ref[:, 0:D_ATT] = (proj[:, 0:D_ATT] * ATT_SCALE).astype(BF16)
            qkv_ref[:, D_ATT:3 * D_ATT] = proj[:, D_ATT:3 * D_ATT].astype(BF16)
            xg_ref[...] = proj[:, 3 * D_ATT:D_IN]

    def tok(n):
        return pl.BlockSpec((tm, n), lambda i: (jnp.maximum(i - npad, 0), 0))

    return _call(
        body, "f_inproj", (s_len // tm + npad,),
        [tok(1024), _full((1, 1024)), _any()],
        [tok(1024), _rows(tm, 1536), tok(1024)],
        [_sds((s_len, 1024), BF16), _sds((s_len + pad_rows, 1536), BF16), _sds((s_len, 1024), F32)],
        [pltpu.VMEM((1024, D_IN), BF16)], (x, g_mix, w_in_g), "arbitrary", comm)


N_BIAS = 3


def _bias_table(frow_ref, bias_sc):
    qa = lax.broadcasted_iota(jnp.int32, (QB, KB), 0) // CHUNK
    kcol = lax.broadcasted_iota(jnp.int32, (QB, KB), 1)
    kb = kcol // CHUNK
    band = jnp.where((kb >= qa) & (kb - qa <= LEFT_CHUNKS), 0.0, NEG).astype(F32)
    for h in range(ATT_HEADS):
        row = jnp.broadcast_to(frow_ref[h:h + 1, :], (QB, ROLL_W))
        toep = pltpu.roll(row, 0, 1, stride=1, stride_axis=0)
        gen = toep[:, 0:KB] + band
        bias_sc[N_BIAS - 1, h] = gen
        for v in range(N_BIAS - 1):
            pad_keys = LEFT_CHUNKS * CHUNK - v * QB
            bias_sc[v, h] = gen + jnp.where(kcol < pad_keys, NEG, 0.0).astype(F32)


def _even_lanes():
    return lax.broadcasted_iota(jnp.int32, (1, 2 * HEAD_DIM), 1) < HEAD_DIM


def _att_probs(qm, kts, bias):
    s = jnp.concatenate([_dot_nt(qm, k) for k in kts], axis=1) + bias
    return jnp.exp(s - jnp.max(s, axis=-1, keepdims=True))


def _att_in_specs(clamp):
    def spec(j, col):
        return pl.BlockSpec((QB, D_ATT), lambda i: (clamp(i) + j, col))
    return [spec(2, 0), spec(0, 1), spec(1, 1), spec(2, 1), spec(0, 2), spec(1, 2), spec(2, 2)]


def _f_attn(qkv_pad, frow, comm=None):
    s_len = qkv_pad.shape[0] - LEFT_CHUNKS * CHUNK
    nb = s_len // QB

    def body(q_ref, k0, k1, k2, v0, v1, v2, frow_ref, o_ref, bias_sc):
        i = pl.program_id(0)

        @pl.when(i == 0)
        def _():
            _bias_table(frow_ref, bias_sc)

        var = jnp.minimum(i, N_BIAS - 1)
        even = _even_lanes()
        for hp in range(ATT_HEADS // 2):
            cs = slice(hp * 2 * HEAD_DIM, (hp + 1) * 2 * HEAD_DIM)
            qt = q_ref[:, cs]
            kts = [k0[:, cs], k1[:, cs], k2[:, cs]]
            vts = [v0[:, cs], v1[:, cs], v2[:, cs]]
            res = []
            for e in range(2):
                keep = even if e == 0 else jnp.logical_not(even)
                pb = _att_probs(jnp.where(keep, qt, 0), kts, bias_sc[var, 2 * hp + e]).astype(BF16)
                r = _dot(pb, jnp.concatenate([jnp.where(keep, v, 1) for v in vts], axis=0))
                res.append(r / pltpu.roll(r, HEAD_DIM, 1))
            o_ref[:, cs] = jnp.where(even, res[0], res[1])

    return _call(
        body, "f_attn", (nb,),
        _att_in_specs(lambda i: i) + [_full((ATT_HEADS, ROLL_W))],
        [_rows(QB, D_ATT)], [_sds((s_len, D_ATT), F32)],
        [pltpu.VMEM((N_BIAS, ATT_HEADS, QB, KB), F32)], (*([qkv_pad] * 7), frow), "arbitrary", comm)


def _f_lru(xg, conv_w, conv_b, wrg, brg, wig, big, lam, tl, comm=None):
    s_len = xg.shape[0]

    def body(xg_ref, cw_ref, cb_ref, wrg_ref, brg_ref, wig_ref, big_ref, l_ref,
             rec_ref, u_ref, hs_ref, xbuf, a_sc, b_sc, hcar):
        i = pl.program_id(0)

        @pl.when(i == 0)
        def _():
            xbuf[0:8, :] = jnp.zeros((8, D_LRU), F32)
            hcar[...] = jnp.zeros((8, D_LRU), F32)

        xu0 = xg_ref[:, 0:D_LRU]
        xbuf[8:8 + tl, :] = xu0
        u = cb_ref[...] + cw_ref[0:1, :] * xbuf[pl.ds(5, tl), :]
        for j in range(1, 4):
            u = u + cw_ref[j:j + 1, :] * xbuf[pl.ds(5 + j, tl), :]
        xbuf[0:8, :] = xu0[tl - 8:tl, :]
        u_ref[...] = u
        _, _, ig, _, a, mult = _lru_gates(u, wrg_ref[...], brg_ref[...], wig_ref[...], big_ref[...], l_ref[...])
        a_sc[...] = a
        b_sc[...] = mult * (ig * u)

        def grp(g, hprev):
            off = pl.multiple_of(g * 8, 8)
            h8 = _scan8(a_sc[pl.ds(off, 8), :], b_sc[pl.ds(off, 8), :], hprev)
            hs_ref[pl.ds(off, 8), :] = h8
            return h8[7:8, :]

        hcar[0:1, :] = lax.fori_loop(0, tl // 8, grp, hcar[0:1, :])
        rec_ref[...] = hs_ref[...] * _gelu(xg_ref[:, D_LRU:2 * D_LRU])

    vec = _full((1, D_LRU))
    return _call(
        body, "f_lru", (s_len // tl,),
        [_rows(tl, 1024), _full((4, D_LRU)), vec, _full((D_LRU, D_LRU)), vec, _full((D_LRU, D_LRU)), vec, vec],
        [_rows(tl, D_LRU)] * 3, [_sds((s_len, D_LRU), F32)] * 3,
        [pltpu.VMEM((tl + 8, D_LRU), F32), pltpu.VMEM((tl, D_LRU), F32),
         pltpu.VMEM((tl, D_LRU), F32), pltpu.VMEM((8, D_LRU), F32)],
        (xg, conv_w, conv_b, wrg, brg, wig, big, lam), "arbitrary", comm)


def _f_mem(mem, g_mem, wk, wv):
    def body(mem_ref, g_ref, wk_ref, wv_ref, mn_ref, kx_ref, vx_ref):
        mv = mem_ref[...]
        mn = (mv * _rinv(mv) * g_ref[...]).astype(BF16)
        mn_ref[...] = mn
        kx_ref[...] = _dot(mn, wk_ref[...]).astype(BF16)
        vx_ref[...] = _dot(mn, wv_ref[...]).astype(BF16)

    m = mem.shape[0]
    return pl.pallas_call(
        body, name="f_mem", out_shape=[_sds((m, 1024), BF16)] * 3,
        compiler_params=_cp())(mem, g_mem, wk, wv)


def _xattn_probs(q, k):
    s = _dot_nt(q, k) * X_SCALE
    m = jnp.max(s, axis=-1, keepdims=True)
    p = jnp.exp(s - m)
    return p, jnp.sum(p, axis=-1, keepdims=True)


def _f_mid(x, att, rec, g_oa, g_ol, w_out, g_cross, wq, kx, vx, wo, tm, comm=None):
    s_len = x.shape[0]
    m_len = kx.shape[0]

    def body(x_ref, att_ref, rec_ref, goa_ref, gol_ref, wout_ref, gc_ref, wq_ref, kx_ref, vx_ref, wo_ref,
             mg_ref, x1_ref, hc_ref, qx_ref, ox_ref, x2_ref):
        av = att_ref[...]
        rv = rec_ref[...]
        mg_ref[:, 0:D_ATT] = (av * _rinv(av) * goa_ref[...]).astype(BF16)
        mg_ref[:, D_ATT:1024] = (rv * _rinv(rv) * gol_ref[...]).astype(BF16)
        x1 = x_ref[...] + _dot(mg_ref[...], wout_ref[...])
        x1_ref[...] = x1
        hc = (x1 * _rinv(x1) * gc_ref[...]).astype(BF16)
        hc_ref[...] = hc
        qx_ref[...] = _dot(hc, wq_ref[...]).astype(BF16)
        for h in range(X_HEADS):
            sl = slice(h * X_HEAD_DIM, (h + 1) * X_HEAD_DIM)
            p, l = _xattn_probs(qx_ref[:, sl], kx_ref[:, sl])
            ox_ref[:, sl] = (_dot(p.astype(BF16), vx_ref[:, sl]) / l).astype(BF16)
        x2_ref[...] = x1 + _dot(ox_ref[...], wo_ref[...])

    sq = _full((1024, 1024))
    return _call(
        body, "f_mid", (s_len // tm,),
        [_rows(tm, 1024), _rows(tm, 512), _rows(tm, 512), _full((1, 512)), _full((1, 512)), sq,
         _full((1, 1024)), sq, _full((m_len, 1024)), _full((m_len, 1024)), sq],
        [_rows(tm, 1024)] * 6,
        [_sds((s_len, 1024), BF16), _sds((s_len, 1024), F32), _sds((s_len, 1024), BF16),
         _sds((s_len, 1024), BF16), _sds((s_len, 1024), BF16), _sds((s_len, 1024), F32)],
        [], (x, att, rec, g_oa, g_ol, w_out, g_cross, wq, kx, vx, wo), "arbitrary", comm)


def _load_weights_once(pairs):
    @pl.when(pl.program_id(0) == 0)
    def _():
        for hbm, vmem in pairs:
            pltpu.sync_copy(hbm, vmem)


FF_CHUNKS = [(0, 1280), (1280, D_FF)]


def _f_ffn(x2, tgt, g_ffn, g_final, wg, wu, wd, tm):
    s_len = x2.shape[0]

    def body(x2_ref, t_ref, gf_ref, gfin_ref, wg_hbm, wu_hbm, wd_hbm,
             hf_ref, g_ref, u_ref, a_ref, dx3_ref, loss_ref, dgfin_ref, wg_ref, wu_ref, wd_ref):
        _load_weights_once([(wg_hbm, wg_ref), (wu_hbm, wu_ref), (wd_hbm, wd_ref)])

        @pl.when(pl.program_id(0) == 0)
        def _():
            loss_ref[...] = jnp.zeros_like(loss_ref)
            dgfin_ref[...] = jnp.zeros_like(dgfin_ref)

        x2v = x2_ref[...]
        hf = (x2v * _rinv(x2v) * gf_ref[...]).astype(BF16)
        hf_ref[...] = hf
        x3 = x2v
        for c0, c1 in FF_CHUNKS:
            gv = _dot_nt(hf, wg_ref[c0:c1, :])
            uv = _dot_nt(hf, wu_ref[c0:c1, :])
            av = (gv * jax.nn.sigmoid(gv) * uv).astype(BF16)
            g_ref[:, c0:c1] = gv.astype(BF16)
            u_ref[:, c0:c1] = uv.astype(BF16)
            a_ref[:, c0:c1] = av
            x3 = x3 + _dot(av, wd_ref[c0:c1, :])
        r3 = _rinv(x3)
        yh = x3 * r3
        gfin = gfin_ref[...]
        err = yh * gfin - t_ref[...]
        loss_ref[...] += jnp.full((1, 128), 0.5 / D_MODEL, F32) * jnp.sum(err * err)
        dy = err * (1.0 / D_MODEL)
        dgfin_ref[...] += jnp.sum(dy * yh, axis=0, keepdims=True)
        dyh = dy * gfin
        dx3_ref[...] = r3 * (dyh - yh * jnp.mean(dyh * yh, axis=-1, keepdims=True))

    vec = _full((1, 1024))
    return pl.pallas_call(
        body, name="f_ffn", grid=(s_len // tm,),
        in_specs=[_rows(tm, 1024), _rows(tm, 1024), vec, vec, _any(), _any(), _any()],
        out_specs=[_rows(tm, 1024), _rows(tm, D_FF), _rows(tm, D_FF), _rows(tm, D_FF),
                   _rows(tm, 1024), _full((1, 128)), vec],
        out_shape=[_sds((s_len, 1024), BF16)] + [_sds((s_len, D_FF), BF16)] * 3
                  + [_sds((s_len, 1024), F32), _sds((1, 128), F32), _sds((1, 1024), F32)],
        scratch_shapes=[pltpu.VMEM((D_FF, 1024), BF16)] * 3,
        compiler_params=_cp("arbitrary"))(x2, tgt, g_ffn, g_final, wg, wu, wd)


def _b_ffn(dx3, x2, gact, uact, g_ffn, wg, wu, wd, tm):
    s_len = x2.shape[0]

    def body(dx3_ref, x2_ref, g_ref, u_ref, gf_ref, wg_hbm, wu_hbm, wd_hbm,
             dg_ref, du_ref, dx2_ref, dgf_ref, wg_ref, wu_ref, wd_ref):
        _load_weights_once([(wg_hbm, wg_ref), (wu_hbm, wu_ref), (wd_hbm, wd_ref)])

        @pl.when(pl.program_id(0) == 0)
        def _():
            dgf_ref[...] = jnp.zeros_like(dgf_ref)

        dx3v = dx3_ref[...]
        dx3b = dx3v.astype(BF16)
        dhf = jnp.zeros(dx3v.shape, F32)
        for c0, c1 in FF_CHUNKS:
            da = _dot_nt(dx3b, wd_ref[c0:c1, :])
            gv = g_ref[:, c0:c1].astype(F32)
            uv = u_ref[:, c0:c1].astype(F32)
            sg = jax.nn.sigmoid(gv)
            dub = (da * gv * sg).astype(BF16)
            dgb = (da * uv * (sg * (1.0 + gv * (1.0 - sg)))).astype(BF16)
            du_ref[:, c0:c1] = dub
            dg_ref[:, c0:c1] = dgb
            dhf = dhf + _dot(dgb, wg_ref[c0:c1, :]) + _dot(dub, wu_ref[c0:c1, :])
        dx, dgf = _rms_bwd(dhf, x2_ref[...], gf_ref[...])
        dx2_ref[...] = dx3v + dx
        dgf_ref[...] += dgf

    vec = _full((1, 1024))
    return pl.pallas_call(
        body, name="b_ffn", grid=(s_len // tm,),
        in_specs=[_rows(tm, 1024), _rows(tm, 1024), _rows(tm, D_FF), _rows(tm, D_FF), vec,
                  _any(), _any(), _any()],
        out_specs=[_rows(tm, D_FF), _rows(tm, D_FF), _rows(tm, 1024), vec],
        out_shape=[_sds((s_len, D_FF), BF16)] * 2 + [_sds((s_len, 1024), F32), _sds((1, 1024), F32)],
        scratch_shapes=[pltpu.VMEM((D_FF, 1024), BF16)] * 3,
        compiler_params=_cp("arbitrary"))(dx3, x2, gact, uact, g_ffn, wg, wu, wd)


def _b_mid(dx2, qx, x1, att, rec, kx, vx, wo, wq, w_out, g_cross, g_oa, g_ol, tm, comm=None):
    s_len = x1.shape[0]
    m_len = kx.shape[0]

    def body(dx2_ref, qx_ref, x1_ref, att_ref, rec_ref, kx_ref, vx_ref, wo_ref, wq_ref, wout_ref,
             gc_ref, goa_ref, gol_ref,
             dqx_ref, dx1_ref, datt_ref, drec_ref, dkx_ref, dvx_ref, dgc_ref, dgoa_ref, dgol_ref):
        @pl.when(pl.program_id(0) == 0)
        def _():
            for r in (dkx_ref, dvx_ref, dgc_ref, dgoa_ref, dgol_ref):
                r[...] = jnp.zeros_like(r)

        dx2v = dx2_ref[...]
        dox = _dot_nt(dx2v.astype(BF16), wo_ref[...])
        for h in range(X_HEADS):
            sl = slice(h * X_HEAD_DIM, (h + 1) * X_HEAD_DIM)
            q = qx_ref[:, sl]
            p, l = _xattn_probs(q, kx_ref[:, sl])
            pn = p * (1.0 / l)
            dob = dox[:, sl].astype(BF16)
            dp = _dot_nt(dob, vx_ref[:, sl])
            dvx_ref[:, sl] += _dot_tn(pn.astype(BF16), dob)
            ds = pn * (dp - jnp.sum(dp * pn, axis=-1, keepdims=True))
            dsb = (ds * X_SCALE).astype(BF16)
            dqx_ref[:, sl] = _dot(dsb, kx_ref[:, sl]).astype(BF16)
            dkx_ref[:, sl] += _dot_tn(dsb, q)
        dhc = _dot_nt(dqx_ref[...], wq_ref[...])
        dx, dgc = _rms_bwd(dhc, x1_ref[...], gc_ref[...])
        dx1 = dx2v + dx
        dx1_ref[...] = dx1
        dgc_ref[...] += dgc
        dmg = _dot_nt(dx1.astype(BF16), wout_ref[...])
        da, dgoa = _rms_bwd(dmg[:, 0:D_ATT], att_ref[...], goa_ref[...])
        datt_ref[...] = da
        dgoa_ref[...] += dgoa
        dr, dgol = _rms_bwd(dmg[:, D_ATT:1024], rec_ref[...], gol_ref[...])
        drec_ref[...] = dr
        dgol_ref[...] += dgol

    sq = _full((1024, 1024))
    mk = _full((m_len, 1024))
    return _call(
        body, "b_mid", (s_len // tm,),
        [_rows(tm, 1024), _rows(tm, 1024), _rows(tm, 1024), _rows(tm, 512), _rows(tm, 512), mk, mk,
         sq, sq, sq, _full((1, 1024)), _full((1, 512)), _full((1, 512))],
        [_rows(tm, 1024), _rows(tm, 1024), _rows(tm, 512), _rows(tm, 512), mk, mk,
         _full((1, 1024)), _full((1, 512)), _full((1, 512))],
        [_sds((s_len, 1024), BF16), _sds((s_len, 1024), F32), _sds((s_len, 512), F32),
         _sds((s_len, 512), F32), _sds((m_len, 1024), F32), _sds((m_len, 1024), F32),
         _sds((1, 1024), F32), _sds((1, 512), F32), _sds((1, 512), F32)],
        [], (dx2, qx, x1, att, rec, kx, vx, wo, wq, w_out, g_cross, g_oa, g_ol), "arbitrary", comm)


def _b_mem(dkx, dvx, mem, mn, g_mem, wk, wv):
    def body(dkx_ref, dvx_ref, mem_ref, mn_ref, g_ref, wk_ref, wv_ref, dwk_ref, dwv_ref, dgm_ref,
             dwkb_ref, dwvb_ref):
        dkb = dkx_ref[...].astype(BF16)
        dvb = dvx_ref[...].astype(BF16)
        dwk = _dot_tn(mn_ref[...], dkb)
        dwv = _dot_tn(mn_ref[...], dvb)
        dwk_ref[...] = dwk
        dwv_ref[...] = dwv
        dwkb_ref[...] = dwk.astype(BF16)
        dwvb_ref[...] = dwv.astype(BF16)
        dmn = _dot_nt(dkb, wk_ref[...]) + _dot_nt(dvb, wv_ref[...])
        mv = mem_ref[...]
        dgm_ref[...] = jnp.sum(dmn * (mv * _rinv(mv)), axis=0, keepdims=True)

    return pl.pallas_call(
        body, name="b_mem",
        out_shape=[_sds((1024, 1024), F32), _sds((1024, 1024), F32), _sds((1, 1024), F32),
                   _sds((1024, 1024), BF16), _sds((1024, 1024), BF16)],
        compiler_params=_cp())(dkx, dvx, mem, mn, g_mem, wk, wv)


def _b_lru(drec, hs, u, xg, conv_w, wrg, brg, wig, big, lam, tl, comm=None):
    s_len = xg.shape[0]
    nt = s_len // tl

    def body(drec_ref, hs_ref, hsp_ref, u_ref, xg_ref, cw_ref, wrg_ref, brg_ref, wig_ref, big_ref, l_ref,
             dxg_ref, dwrg_ref, dwig_ref, dbrg_ref, dbig_ref, dlam_ref, dcw_ref, dcb_ref,
             hbuf, abuf, dubuf, c_sc, d_sc, lam_sc, lcar, wacc_r, wacc_i):
        i = pl.program_id(0)
        tt = nt - 1 - i

        @pl.when(i == 0)
        def _():
            for r in (wacc_r, wacc_i, dbrg_ref, dbig_ref, dlam_ref, dcw_ref, dcb_ref):
                r[...] = jnp.zeros_like(r)
            abuf[tl:tl + 8, :] = jnp.zeros((8, D_LRU), F32)
            dubuf[tl:tl + 8, :] = jnp.zeros((8, D_LRU), F32)
            lcar[...] = jnp.zeros((8, D_LRU), F32)

        xu0 = xg_ref[:, 0:D_LRU]
        hsv = hs_ref[...]
        uv = u_ref[...]
        hbuf[8:8 + tl, :] = hsv
        hbuf[0:8, :] = jnp.where(tt > 0, hsp_ref[...], 0.0)
        hshift = hbuf[pl.ds(7, tl), :]
        wrg_v = wrg_ref[...]
        wig_v = wig_ref[...]
        lamv = l_ref[...]
        ub, r, ig, sp, a, mult = _lru_gates(uv, wrg_v, brg_ref[...], wig_v, big_ref[...], lamv)
        abuf[0:tl, :] = a
        c_sc[...] = abuf[pl.ds(1, tl), :]
        gel, dgel = _gelu_and_grad(xg_ref[:, D_LRU:2 * D_LRU])
        drv = drec_ref[...]
        d_sc[...] = drv * gel
        dxg_ref[:, D_LRU:2 * D_LRU] = (drv * hsv * dgel).astype(BF16)

        def grp(k, lnext):
            off = pl.multiple_of((tl // 8 - 1 - k) * 8, 8)
            l8 = _rscan8(c_sc[pl.ds(off, 8), :], d_sc[pl.ds(off, 8), :], lnext)
            lam_sc[pl.ds(off, 8), :] = l8
            return l8[0:1, :]

        lcar[0:1, :] = lax.fori_loop(0, tl // 8, grp, lcar[0:1, :])
        abuf[tl:tl + 8, :] = a[0:8, :]
        db = lam_sc[...]
        da = db * hshift
        dmult = db * (ig * uv)
        dig = db * mult * uv
        du = db * mult * ig
        dla = da * a - dmult * (a * a) / mult
        dlam_ref[...] += jnp.sum(dla * (-LRU_C) * r, axis=0, keepdims=True)
        dzr = dla * (-LRU_C * sp) * r * (1.0 - r)
        dzi = dig * ig * (1.0 - ig)
        dzrb = dzr.astype(BF16)
        dzib = dzi.astype(BF16)
        du = du + _dot_nt(dzrb, wrg_v) + _dot_nt(dzib, wig_v)
        wacc_r[...] += _dot_tn(ub, dzrb)
        wacc_i[...] += _dot_tn(ub, dzib)
        dbrg_ref[...] += jnp.sum(dzr, axis=0, keepdims=True)
        dbig_ref[...] += jnp.sum(dzi, axis=0, keepdims=True)
        dcb_ref[...] += jnp.sum(du, axis=0, keepdims=True)
        dubuf[0:tl, :] = du
        dxu0 = jnp.zeros((tl, D_LRU), F32)
        for j in range(4):
            dsh = dubuf[pl.ds(3 - j, tl), :]
            dxu0 = dxu0 + cw_ref[j:j + 1, :] * dsh
            dcw_ref[j:j + 1, :] += jnp.sum(xu0 * dsh, axis=0, keepdims=True)
        dubuf[tl:tl + 8, :] = du[0:8, :]
        dxg_ref[:, 0:D_LRU] = dxu0.astype(BF16)

        @pl.when(i == nt - 1)
        def _():
            dlam_ref[...] = dlam_ref[...] * (-jax.nn.sigmoid(-lamv))
            for n in range(LRU_BLOCKS):
                blk = slice(n * LRU_BLOCK, (n + 1) * LRU_BLOCK)
                dwrg_ref[n] = wacc_r[blk, blk]
                dwig_ref[n] = wacc_i[blk, blk]

    def rev(n):
        return pl.BlockSpec((tl, n), lambda i: (nt - 1 - i, 0))

    prev8 = pl.BlockSpec((8, D_LRU), lambda i: (jnp.maximum((nt - 1 - i) * (tl // 8) - 1, 0), 0))
    vec = _full((1, D_LRU))
    sq = _full((D_LRU, D_LRU))
    blocks_shape = (LRU_BLOCKS, LRU_BLOCK, LRU_BLOCK)
    blocks = _full(blocks_shape)
    return _call(
        body, "b_lru", (nt,),
        [rev(D_LRU), rev(D_LRU), prev8, rev(D_LRU), rev(1024), _full((4, D_LRU)), sq, vec, sq, vec, vec],
        [rev(1024), blocks, blocks, vec, vec, vec, _full((4, D_LRU)), vec],
        [_sds((s_len, 1024), BF16), _sds(blocks_shape, F32), _sds(blocks_shape, F32),
         _sds((1, D_LRU), F32), _sds((1, D_LRU), F32), _sds((1, D_LRU), F32),
         _sds((4, D_LRU), F32), _sds((1, D_LRU), F32)],
        [pltpu.VMEM((tl + 8, D_LRU), F32)] * 3 + [pltpu.VMEM((tl, D_LRU), F32)] * 3
        + [pltpu.VMEM((8, D_LRU), F32)] + [pltpu.VMEM((D_LRU, D_LRU), F32)] * 2,
        (drec, hs, hs, u, xg, conv_w, wrg, brg, wig, big, lam), "arbitrary", comm)


def _b_attn(qkv_pad, att, datt, frow, comm=None):
    s_len = datt.shape[0]
    nb = s_len // QB
    n_pair = ATT_HEADS // 2
    pair_w = 2 * HEAD_DIM

    def body(q_ref, k0, k1, k2, v0, v1, v2, o_ref, do_ref, frow_ref, dq_ref, dkv_ref, dfrow_ref,
             bias_sc, dt_sc, acc_sc):
        t = pl.program_id(0)

        @pl.when(t == 0)
        def _():
            _bias_table(frow_ref, bias_sc)
            dt_sc[...] = jnp.zeros_like(dt_sc)
            acc_sc[...] = jnp.zeros_like(acc_sc)

        @pl.when(t < nb)
        def _():
            var = jnp.minimum(t, N_BIAS - 1)
            even = _even_lanes()
            for hp in range(n_pair):
                cs = slice(hp * pair_w, (hp + 1) * pair_w)
                qt = q_ref[:, cs]
                kts = [k0[:, cs], k1[:, cs], k2[:, cs]]
                vts = [v0[:, cs], v1[:, cs], v2[:, cs]]
                kcat = jnp.concatenate(kts, axis=0)
                dot = do_ref[:, cs]
                dd = dot * o_ref[:, cs]
                dos_pair, dsbs, pbs, dqs = None, [], [], []
                for e in range(2):
                    keep = even if e == 0 else jnp.logical_not(even)
                    qm = jnp.where(keep, qt, 0)
                    p = _att_probs(qm, kts, bias_sc[var, 2 * hp + e])
                    inv = 1.0 / jnp.sum(p, axis=-1, keepdims=True)
                    dos = jnp.where(keep, dot * inv, 0.0)
                    delta = jnp.sum(jnp.where(keep, dd, 0.0), axis=-1, keepdims=True) * inv
                    dp = jnp.concatenate([_dot_nt(dos.astype(BF16), v) for v in vts], axis=1)
                    ds = p * (dp - delta)
                    dt_sc[2 * hp + e] += ds
                    dsb = ds.astype(BF16)
                    dq = _dot(dsb, kcat)
                    dqs.append(dq)
                    dsbs.append(dsb)
                    pbs.append(p.astype(BF16))
                    dos_pair = dos if e == 0 else dos_pair + dos
                dq_ref[:, cs] = (jnp.where(even, dqs[0], dqs[1]) * ATT_SCALE).astype(BF16)
                qtt = qt.astype(F32).T.astype(BF16)
                dost = dos_pair.T.astype(BF16)
                for j in range(3):
                    slot = (t + 1 + j) % 3
                    js = slice(j * QB, (j + 1) * QB)
                    for e in range(2):
                        hr = slice(e * HEAD_DIM, (e + 1) * HEAD_DIM)
                        acc_sc[slot, hp, hr, :] += _dot(qtt[hr], dsbs[e][:, js])
                        acc_sc[slot, n_pair + hp, hr, :] += _dot(dost[hr], pbs[e][:, js])

        done = (t + 1) % 3

        @pl.when(t >= 2)
        def _():
            for i in range(2 * n_pair):
                dkv_ref[:, i * pair_w:(i + 1) * pair_w] = acc_sc[done, i].T.astype(BF16)

        acc_sc[done] = jnp.zeros((2 * n_pair, pair_w, QB), F32)

        @pl.when(t == nb + 1)
        def _():
            row = lax.broadcasted_iota(jnp.int32, (8, ROLL_W), 0)
            pad = jnp.zeros((8, ROLL_W - KB), F32)
            for h in range(ATT_HEADS):
                acc8 = jnp.concatenate([dt_sc[h, 0:8, :], pad], axis=1)
                for a1 in range(1, QB // 8):
                    blk = jnp.concatenate([dt_sc[h, 8 * a1:8 * a1 + 8, :], pad], axis=1)
                    acc8 = acc8 + pltpu.roll(blk, ROLL_W - 8 * a1, 1)
                for k in range(3):
                    acc8 = jnp.where(((row >> k) & 1) == 1, pltpu.roll(acc8, ROLL_W - (1 << k), 1), acc8)
                dfrow_ref[h:h + 1, :] = jnp.sum(acc8, axis=0, keepdims=True)

    clamp = lambda t: jnp.minimum(t, nb - 1)
    qrows = pl.BlockSpec((QB, D_ATT), lambda t: (clamp(t), 0))
    return _call(
        body, "b_attn", (nb + 2,),
        _att_in_specs(clamp) + [qrows, qrows, _full((ATT_HEADS, ROLL_W))],
        [qrows, pl.BlockSpec((QB, 2 * D_ATT), lambda t: (jnp.maximum(t - 2, 0), 0)),
         _full((ATT_HEADS, ROLL_W))],
        [_sds((s_len, D_ATT), BF16), _sds((s_len, 2 * D_ATT), BF16), _sds((ATT_HEADS, ROLL_W), F32)],
        [pltpu.VMEM((N_BIAS, ATT_HEADS, QB, KB), F32), pltpu.VMEM((ATT_HEADS, QB, KB), F32),
         pltpu.VMEM((3, 2 * n_pair, pair_w, QB), F32)],
        (*([qkv_pad] * 7), att, datt, frow), "arbitrary", comm)


def _flush_grad(steps, acc, accb, out_hbm, outb_hbm):
    @pl.when(pl.program_id(0) == steps - 1)
    def _():
        accb[...] = acc[...].astype(BF16)
        pltpu.sync_copy(acc, out_hbm)
        pltpu.sync_copy(accb, outb_hbm)


def _b_win(dq, dkv, dxg, h, ts):
    s_len = h.shape[0]
    steps = s_len // ts

    def body(dq_ref, dkv_ref, dxg_ref, h_ref, dw_hbm, dwb_hbm, acc, accb):
        @pl.when(pl.program_id(0) == 0)
        def _():
            acc[...] = jnp.zeros_like(acc)

        dproj = jnp.concatenate([dq_ref[...], dkv_ref[...], dxg_ref[...]], axis=1)
        hv = h_ref[...]
        for s in range(N_SHARD):
            acc[s] += _dot_tn(hv, dproj[:, s * IN_SH:(s + 1) * IN_SH])
        _flush_grad(steps, acc, accb, dw_hbm, dwb_hbm)

    shape = (N_SHARD, 1024, IN_SH)
    return pl.pallas_call(
        body, name="b_win", grid=(steps,),
        in_specs=[_rows(ts, 512), _rows(ts, 1024), _rows(ts, 1024), _rows(ts, 1024)],
        out_specs=[_any()] * 2, out_shape=[_sds(shape, F32), _sds(shape, BF16)],
        scratch_shapes=[pltpu.VMEM(shape, F32), pltpu.VMEM(shape, BF16)],
        compiler_params=_cp("arbitrary"))(dq, dkv, dxg, h)


def _b_inproj(dq, dkv, dxg, x, dx1, g_mix, w_in_g, tm, comm=None):
    s_len = x.shape[0]

    def body(dq_ref, dkv_ref, dxg_ref, x_ref, dx1_ref, g_ref, w_hbm, gx_ref, dgm_ref, w_ref):
        _load_w_in_once(w_hbm, w_ref)

        @pl.when(pl.program_id(0) == 0)
        def _():
            dgm_ref[...] = jnp.zeros_like(dgm_ref)

        dproj = jnp.concatenate([dq_ref[...], dkv_ref[...], dxg_ref[...]], axis=1)
        dh = _dot_nt(dproj, w_ref[...])
        dx, dgm = _rms_bwd(dh, x_ref[...], g_ref[...])
        gx_ref[...] = dx1_ref[...] + dx
        dgm_ref[...] += dgm

    return _call(
        body, "b_inproj", (s_len // tm,),
        [_rows(tm, 512), _rows(tm, 1024), _rows(tm, 1024), _rows(tm, 1024), _rows(tm, 1024),
         _full((1, 1024)), _any()],
        [_rows(tm, 1024), _full((1, 1024))],
        [_sds((s_len, 1024), F32), _sds((1, 1024), F32)],
        [pltpu.VMEM((1024, D_IN), BF16)], (dq, dkv, dxg, x, dx1, g_mix, w_in_g), "arbitrary", comm)


def _mm_tn(xa, ya, name, ts):
    s_len, k = xa.shape
    n = ya.shape[1]

    steps = s_len // ts

    def body(x_ref, y_ref, o_hbm, ob_hbm, acc, accb):
        @pl.when(pl.program_id(0) == 0)
        def _():
            acc[...] = jnp.zeros_like(acc)
        acc[...] += _dot_tn(x_ref[...].astype(BF16), y_ref[...].astype(BF16))
        _flush_grad(steps, acc, accb, o_hbm, ob_hbm)

    return pl.pallas_call(
        body, name=name, grid=(steps,), in_specs=[_rows(ts, k), _rows(ts, n)],
        out_specs=[_any()] * 2, out_shape=[_sds((k, n), F32), _sds((k, n), BF16)],
        scratch_shapes=[pltpu.VMEM((k, n), F32), pltpu.VMEM((k, n), BF16)],
        compiler_params=_cp("arbitrary"))(xa, ya)


PAD_KEYS = LEFT_CHUNKS * CHUNK
F_HI = PAD_KEYS - MAX_REL + 1
F_LO = PAD_KEYS + MAX_REL


def _frow_from_rel_bias(rb):
    last = rb[:, 2 * MAX_REL:2 * MAX_REL + 1]
    hi = jnp.broadcast_to(last, (ATT_HEADS, F_HI))
    mid = rb[:, 1:2 * MAX_REL][:, ::-1]
    lo = jnp.broadcast_to(rb[:, 0:1], (ATT_HEADS, KB - F_LO))
    wrap = jnp.broadcast_to(last, (ATT_HEADS, ROLL_W - KB))
    return jnp.concatenate([hi, mid, lo, wrap], axis=1)


def _rel_bias_grad_from_dfrow(df):
    g_last = jnp.sum(df[:, 0:F_HI], axis=1, keepdims=True) + jnp.sum(df[:, KB:ROLL_W], axis=1, keepdims=True)
    mid = df[:, F_HI:F_LO][:, ::-1]
    g_first = jnp.sum(df[:, F_LO:KB], axis=1, keepdims=True)
    return jnp.concatenate([g_first, mid, g_last], axis=1)


def _block_diag(w):
    eye = jnp.eye(8, dtype=w.dtype)
    return (w[:, :, None, :] * eye[:, None, :, None]).reshape(D_LRU, D_LRU)


MID = ['w_out', 'wq_c', 'wk_c', 'wv_c', 'wo_c']
TRANSPOSED = ['w_gate', 'w_up']
AG_IN_INPROJ = ['w_out', 'wq_c', 'wk_c']
AG_IN_ATTN = ['wv_c', 'wo_c', 'w_gate']
AG_IN_LRU = ['w_up']
AG_IN_MID = ['w_down']
RS_IN_MID = ['w_gate', 'w_up']
RS_IN_LRU = ['w_down']
RS_IN_ATTN = MID


def _local_step(x, mem, tgt, p, gw, shards=None, chip=None):
    s_len = x.shape[0]
    tm = min(256, s_len)
    tmb = min(512, s_len)
    tl = min(512, s_len)
    frow = _frow_from_rel_bias(p['rel_bias'])
    wrg = _block_diag(p['w_rg']).astype(BF16)
    wig = _block_diag(p['w_ig']).astype(BF16)
    gw = dict(gw)

    big, bigb, recv, part, sib = {}, {}, {}, {}, {}

    def ag(names):
        return [] if shards is None else [("ag", [shards[n] for n in names])]

    def rs(names):
        return [] if shards is None else [("rs", [bigb[n] for n in names])]

    def swap(names):
        return [] if shards is None else [("swap", [part[n] for n in names])]

    def reduce_own(names):
        if shards is not None:
            sums = _sum_parts([big[n] for n in names], [recv[n] for n in names], chip, "sum_" + names[0])
            part.update(zip(names, sums))

    h, qkv_pad, xg, *got = _f_inproj(x, p['g_mix'], gw['w_in'], tmb, ag(AG_IN_INPROJ))
    gw.update(zip(AG_IN_INPROJ, got))
    att, *got = _f_attn(qkv_pad, frow, ag(AG_IN_ATTN))
    gw.update(zip(AG_IN_ATTN, got))
    rec, u, hs, *got = _f_lru(xg, p['conv_w'], p['conv_b'], wrg, p['b_rg'], wig, p['b_ig'], p['lru_L'], tl,
                              ag(AG_IN_LRU))
    gw.update(zip(AG_IN_LRU, got))
    w_out = gw['w_out'].reshape(1024, 1024)
    wq = gw['wq_c'].reshape(1024, 1024)
    wk = gw['wk_c'].reshape(1024, 1024)
    wv = gw['wv_c'].reshape(1024, 1024)
    wo = gw['wo_c'].reshape(1024, 1024)
    mn, kx, vx = _f_mem(mem, p['g_mem'], wk, wv)
    mg, x1, hc, qx, ox, x2, *got = _f_mid(x, att, rec, p['g_out_attn'], p['g_out_lru'], w_out, p['g_cross'],
                                          wq, kx, vx, wo, tmb, ag(AG_IN_MID))
    gw.update(zip(AG_IN_MID, got))
    ffn_w = [gw[n].reshape(D_FF, 1024) for n in ('w_gate', 'w_up', 'w_down')]
    hf, gact, uact, aact, dx3, loss, dg_final = _f_ffn(x2, tgt, p['g_ffn'], p['g_final'], *ffn_w, tmb)

    ts = min(1024, s_len)
    dgact, duact, dx2, dg_ffn = _b_ffn(dx3, x2, gact, uact, p['g_ffn'], *ffn_w, tm)
    big['w_gate'], bigb['w_gate'] = _mm_tn(dgact, hf, "dw_gate", ts)
    big['w_up'], bigb['w_up'] = _mm_tn(duact, hf, "dw_up", ts)
    big['w_down'], bigb['w_down'] = _mm_tn(aact, dx3, "dw_down", ts)
    for n in ('w_gate', 'w_up', 'w_down'):
        big[n] = big[n].reshape(N_SHARD, FF_SH, 1024)
        bigb[n] = bigb[n].reshape(N_SHARD, FF_SH, 1024)

    dqx, dx1, datt, drec, dkx, dvx, dg_cross, dg_oa, dg_ol, *got = _b_mid(
        dx2, qx, x1, att, rec, kx, vx, wo, wq, w_out, p['g_cross'], p['g_out_attn'], p['g_out_lru'], tmb,
        rs(RS_IN_MID))
    recv.update(zip(RS_IN_MID, got))
    reduce_own(RS_IN_MID)
    dwk, dwv, dg_mem, dwkb, dwvb = _b_mem(dkx, dvx, mem, mn, p['g_mem'], wk, wv)
    big['wk_c'], bigb['wk_c'] = dwk, dwkb
    big['wv_c'], bigb['wv_c'] = dwv, dwvb
    big['w_out'], bigb['w_out'] = _mm_tn(mg, dx1, "dw_out", ts)
    big['wq_c'], bigb['wq_c'] = _mm_tn(hc, dqx, "dw_q", ts)
    big['wo_c'], bigb['wo_c'] = _mm_tn(ox, dx2, "dw_o", ts)
    for n in MID:
        big[n] = big[n].reshape(N_SHARD, 256, 1024)
        bigb[n] = bigb[n].reshape(N_SHARD, 256, 1024)

    dxg, dwrg, dwig, dbrg, dbig, dlam, dcw, dcb, *got = _b_lru(
        drec, hs, u, xg, p['conv_w'], wrg, p['b_rg'], wig, p['b_ig'], p['lru_L'], tl,
        rs(RS_IN_LRU) + swap(RS_IN_MID))
    recv.update(zip(RS_IN_LRU, got))
    sib.update(zip(RS_IN_MID, got[len(RS_IN_LRU):]))
    reduce_own(RS_IN_LRU)
    small = {
        'conv_w': dcw, 'conv_b': dcb, 'w_rg': dwrg, 'b_rg': dbrg, 'w_ig': dwig, 'b_ig': dbig, 'lru_L': dlam,
        'g_out_attn': dg_oa, 'g_out_lru': dg_ol, 'g_cross': dg_cross, 'g_mem': dg_mem, 'g_ffn': dg_ffn,
        'g_final': dg_final,
    }
    names = [n for n in SMALL if n in small]
    gather = [] if shards is None else [("ag8", [_pack_small(names, [small[n] for n in names], loss)])]
    dq, dkv, dfrow, *got = _b_attn(qkv_pad, att, datt, frow, rs(RS_IN_ATTN) + swap(RS_IN_LRU) + gather)
    recv.update(zip(RS_IN_ATTN, got))
    sib.update(zip(RS_IN_LRU, got[len(RS_IN_ATTN):]))
    packs = got[-1] if gather else None
    reduce_own(RS_IN_ATTN)
    small['rel_bias'] = _rel_bias_grad_from_dfrow(dfrow)
    big['w_in'], bigb['w_in'] = _b_win(dq, dkv, dxg, h, ts)
    grad_x, small['g_mix'], *got = _b_inproj(dq, dkv, dxg, x, dx1, p['g_mix'], gw['w_in'], tmb,
                                             rs(['w_in']) + swap(RS_IN_ATTN))
    recv.update(zip(['w_in'], got))
    sib.update(zip(RS_IN_ATTN, got[1:]))
    reduce_own(['w_in'])
    return loss, grad_x, small, big, part, sib, packs


CAST_STEPS = 4


def _cast_shards(ws, name, comm=None):
    def body(*refs):
        n = len(refs) // 2
        for src, dst in zip(refs[:n], refs[n:]):
            dst[...] = src[...].astype(BF16)

    specs = [_rows(w.shape[0] // CAST_STEPS, w.shape[1]) for w in ws]
    return _call(body, name, (CAST_STEPS,), specs, specs, [_sds(w.shape, BF16) for w in ws], [], tuple(ws),
                 "arbitrary", comm)


def _sum_parts(own4s, recv3s, chip, name):
    n = len(own4s)
    _, r, c = own4s[0].shape
    steps = _ew_steps(r, n * c * (4 + 3 * 2 + 4))
    tr = r // steps

    def body(chip_ref, *refs):
        for own_ref, rc_ref, o_ref in zip(refs[:n], refs[n:2 * n], refs[2 * n:]):
            o_ref[...] = ((own_ref[0] + rc_ref[0].astype(F32)) + rc_ref[1].astype(F32)) + rc_ref[2].astype(F32)

    grid_spec = pltpu.PrefetchScalarGridSpec(
        num_scalar_prefetch=1, grid=(steps,),
        in_specs=[pl.BlockSpec((1, tr, c), lambda i, ch: (ch[0], i, 0))] * n
                 + [pl.BlockSpec((3, tr, c), lambda i, ch: (0, i, 0))] * n,
        out_specs=[pl.BlockSpec((tr, c), lambda i, ch: (i, 0))] * n)
    return pl.pallas_call(body, name=name, grid_spec=grid_spec, out_shape=[_sds((r, c), F32)] * n,
                          compiler_params=_cp("parallel"))(chip, *own4s, *recv3s)


def _adamw_math(w, g, m, v):
    m = ADAM_B1 * m + (1.0 - ADAM_B1) * g
    v = ADAM_B2 * v + (1.0 - ADAM_B2) * (g * g)
    m_hat = m / (1.0 - ADAM_B1 ** ADAM_STEP)
    v_hat = v / (1.0 - ADAM_B2 ** ADAM_STEP)
    delta = -ADAM_LR * (m_hat / (jnp.sqrt(v_hat) + ADAM_EPS) + ADAM_WD * w)
    return delta, m, v


def _final_adamw(pas, pbs, ws, ms, vs, name):
    n = len(ws)
    r, c = ws[0].shape
    steps = _ew_steps(r, n * c * 9 * 4)
    tr = r // steps

    def body(*refs):
        ins, outs = refs[:5 * n], refs[5 * n:]
        for k in range(n):
            pa_ref, pb_ref, w_ref, m_ref, v_ref = (ins[j * n + k] for j in range(5))
            g = pa_ref[...] + pb_ref[...]
            outs[4 * k][...] = g
            outs[4 * k + 1][...], outs[4 * k + 2][...], outs[4 * k + 3][...] = _adamw_math(
                w_ref[...], g, m_ref[...], v_ref[...])

    res = pl.pallas_call(
        body, name=name, grid=(steps,), in_specs=[_rows(tr, c)] * (5 * n), out_specs=[_rows(tr, c)] * (4 * n),
        out_shape=[_sds((r, c), F32)] * (4 * n), compiler_params=_cp("parallel"))(*pas, *pbs, *ws, *ms, *vs)
    return [res[4 * k:4 * k + 4] for k in range(n)]


def _pack_put(ref, name, val_ref):
    r = _pack_rows()[name]
    shape = val_ref.shape
    if len(shape) == 3:
        for b in range(shape[0]):
            ref[r:r + shape[1], b * shape[2]:(b + 1) * shape[2]] = val_ref[b]
    elif shape[1] == 2 * PACK_W:
        ref[r:r + 1, :] = val_ref[:, 0:PACK_W]
        ref[r + 1:r + 2, :] = val_ref[:, PACK_W:2 * PACK_W]
    else:
        ref[r:r + shape[0], 0:shape[1]] = val_ref[...]


def _pack_get(ref, name, shape):
    r = _pack_rows()[name]
    if len(shape) == 3:
        return jnp.stack([ref[r:r + shape[1], b * shape[2]:(b + 1) * shape[2]] for b in range(shape[0])])
    if shape[1] == 2 * PACK_W:
        return jnp.concatenate([ref[r:r + 1, :], ref[r + 1:r + 2, :]], axis=1)
    return ref[r:r + shape[0], 0:shape[1]]


def _pack_small(names, g, loss):
    n = len(g)

    def body(*refs):
        pack = refs[n + 1]
        pack[...] = jnp.zeros_like(pack)
        for a, name in enumerate(names):
            _pack_put(pack, name, refs[a])
        _pack_put(pack, 'loss', refs[n])

    return pl.pallas_call(body, name="pack_small", out_shape=_sds((PACK_ROWS, PACK_W), F32),
                          compiler_params=_cp())(*g, loss)


def _all_peers():
    x, y, c = _mesh_pos()
    peers = []
    for k in range(1, 8):
        px = 1 - x if k & 4 else x
        py = 1 - y if k & 2 else y
        pc = 1 - c if k & 1 else c
        peers.append(((px, py, pc), 4 * px + 2 * py + pc))
    return peers, 4 * x + 2 * y + c


def _ag8_copies(ins, outs, sems):
    send_sems, recv_sems, loc_sems = sems
    n = len(ins)
    peers, me = _all_peers()

    def remote(k, j, slot):
        return pltpu.make_async_remote_copy(
            src_ref=ins[k], dst_ref=outs[k].at[slot], send_sem=send_sems.at[k, j], recv_sem=recv_sems.at[k, j],
            device_id=peers[j][0], device_id_type=MESH_ID)

    def local(k):
        return pltpu.make_async_copy(ins[k], outs[k].at[me], loc_sems.at[k])

    def start():
        for k in range(n):
            local(k).start()
            for j in range(7):
                remote(k, j, me).start()

    def wait():
        for k in range(n):
            for j in range(7):
                remote(k, j, peers[j][1]).wait_recv()
        for k in range(n):
            for j in range(7):
                remote(k, j, me).wait_send()
            local(k).wait()

    return start, _no_forward, wait


def _ar_late(names, g):
    n = len(g)

    def body(*refs):
        tot_ref, pack, buf, send_sems, recv_sems = refs[n:]
        peers, me = _all_peers()

        def remote(j, slot):
            return pltpu.make_async_remote_copy(
                src_ref=pack, dst_ref=buf.at[slot], send_sem=send_sems.at[j], recv_sem=recv_sems.at[j],
                device_id=peers[j][0], device_id_type=MESH_ID)

        pack[...] = jnp.zeros_like(pack)
        for a, name in enumerate(names):
            _pack_put(pack, name, refs[a])
        for j in range(7):
            remote(j, me).start()
        buf[me] = pack[...]
        for j in range(7):
            remote(j, peers[j][1]).wait_recv()
        for j in range(7):
            remote(j, me).wait_send()
        tot = buf[0]
        for d in range(1, 8):
            tot = tot + buf[d]
        tot_ref[...] = tot

    return pl.pallas_call(
        body, name="ar_late", out_shape=_sds((LATE_ROWS, PACK_W), F32),
        scratch_shapes=[pltpu.VMEM((LATE_ROWS, PACK_W), F32), pltpu.VMEM((8, LATE_ROWS, PACK_W), F32),
                        pltpu.SemaphoreType.DMA((7,)), pltpu.SemaphoreType.DMA((7,))],
        compiler_params=_cp())(*g)


def _adamw_small(packs, late_tot, g_shapes, loss_shape, w, m, v):
    n = len(w)

    def body(*refs):
        packs_ref, late_ref = refs[0], refs[1]
        w_refs, m_refs, v_refs = (refs[2 + i * n:2 + (i + 1) * n] for i in range(3))
        o0 = 3 * n + 2
        go, do, mo, vo = (refs[o0 + i * n:o0 + (i + 1) * n] for i in range(4))
        loss_out, tot_ref = refs[o0 + 4 * n], refs[o0 + 4 * n + 1]
        x, y, _ = _mesh_pos()
        tot = packs_ref[0]
        for d in range(1, 8):
            tot = tot + packs_ref[d]
        tot_ref[...] = tot
        tot_ref[0:LATE_ROWS, :] += late_ref[...]
        loss_out[...] = _pack_get(tot_ref, 'loss', loss_shape)
        for a, name in enumerate(SMALL):
            if name == 'conv_w':
                r = _pack_rows()[name]
                ga = tot_ref[r:r + g_shapes[a][0], pl.ds(pl.multiple_of((2 * x + y) * 128, 128), 128)]
            else:
                ga = _pack_get(tot_ref, name, g_shapes[a])
            go[a][...] = ga
            do[a][...], mo[a][...], vo[a][...] = _adamw_math(w_refs[a][...], ga, m_refs[a][...], v_refs[a][...])

    out_shape = [_sds(a.shape, F32) for a in w] * 4 + [_sds(loss_shape, F32)]
    return pl.pallas_call(body, name="adamw_small", out_shape=out_shape,
                          scratch_shapes=[pltpu.VMEM((PACK_ROWS, PACK_W), F32)],
                          compiler_params=_cp())(packs, late_tot, *w, *m, *v)


PACK_W = 512
PACK_ROWS = 160
LATE = ['g_mix', 'rel_bias']
LATE_ROWS = 32


def _pack_rows():
    rows, r = {}, 0
    for name in ['g_mix', 'g_cross', 'g_mem', 'g_ffn', 'g_final']:
        rows[name] = r
        r += 2
    for name in ['conv_b', 'b_rg', 'b_ig', 'lru_L', 'g_out_attn', 'g_out_lru']:
        rows[name] = r
        r += 1
    rows['conv_w'] = r
    rows['loss'] = r + 4
    rows['rel_bias'] = 24
    rows['w_rg'] = 32
    rows['w_ig'] = 32 + LRU_BLOCK
    assert r + 5 <= 24 and rows['w_ig'] + LRU_BLOCK == PACK_ROWS
    assert rows['g_mix'] + 2 <= LATE_ROWS and rows['rel_bias'] + 8 <= LATE_ROWS
    return rows


INPUT_NAMES = (['x', 'mem'] + WEIGHTS + ['loss_target'] + ['m_' + n for n in WEIGHTS] + ['v_' + n for n in WEIGHTS])


def kernel(x, mem, g_mix, w_in, rel_bias, conv_w, conv_b, w_rg, b_rg, w_ig, b_ig, lru_L, g_out_attn, g_out_lru, w_out, g_cross, g_mem, wq_c, wk_c, wv_c, wo_c, g_ffn, w_gate, w_up, w_down, g_final, loss_target, m_g_mix, m_w_in, m_rel_bias, m_conv_w, m_conv_b, m_w_rg, m_b_rg, m_w_ig, m_b_ig, m_lru_L, m_g_out_attn, m_g_out_lru, m_w_out, m_g_cross, m_g_mem, m_wq_c, m_wk_c, m_wv_c, m_wo_c, m_g_ffn, m_w_gate, m_w_up, m_w_down, m_g_final, v_g_mix, v_w_in, v_rel_bias, v_conv_w, v_conv_b, v_w_rg, v_b_rg, v_w_ig, v_b_ig, v_lru_L, v_g_out_attn, v_g_out_lru, v_w_out, v_g_cross, v_g_mem, v_wq_c, v_wk_c, v_wv_c, v_wo_c, v_g_ffn, v_w_gate, v_w_up, v_w_down, v_g_final):
    a = dict(zip(INPUT_NAMES, (x, mem, g_mix, w_in, rel_bias, conv_w, conv_b, w_rg, b_rg, w_ig, b_ig, lru_L, g_out_attn, g_out_lru, w_out, g_cross, g_mem, wq_c, wk_c, wv_c, wo_c, g_ffn, w_gate, w_up, w_down, g_final, loss_target, m_g_mix, m_w_in, m_rel_bias, m_conv_w, m_conv_b, m_w_rg, m_b_rg, m_w_ig, m_b_ig, m_lru_L, m_g_out_attn, m_g_out_lru, m_w_out, m_g_cross, m_g_mem, m_wq_c, m_wk_c, m_wv_c, m_wo_c, m_g_ffn, m_w_gate, m_w_up, m_w_down, m_g_final, v_g_mix, v_w_in, v_rel_bias, v_conv_w, v_conv_b, v_w_rg, v_b_rg, v_w_ig, v_b_ig, v_lru_L, v_g_out_attn, v_g_out_lru, v_w_out, v_g_cross, v_g_mem, v_wq_c, v_wk_c, v_wv_c, v_wo_c, v_g_ffn, v_w_gate, v_w_up, v_w_down, v_g_final)))
    chip = 2 * lax.axis_index("x") + lax.axis_index("y")

    def shard(name):
        arr = a[name][0]
        base = name[2:] if name[:2] in ('m_', 'v_') else name
        return jnp.swapaxes(arr, 0, 1) if base in TRANSPOSED else arr

    shards = {'w_in': _cast_shards([shard('w_in')], "cast_w_in")[0]}
    rest = [n for n in BIG if n != 'w_in']
    *cast, w_in_g, conv_w_g = _cast_shards([shard(n) for n in rest], "cast_rest",
                                           [("ag", [shards['w_in']]), ("agf", [a['conv_w'][0]])])
    shards.update(zip(rest, cast))
    conv_w_full = conv_w_g.transpose(1, 0, 2).reshape(4, D_LRU)

    p = {n: a[n] for n in SMALL}
    p['rel_bias'] = a['rel_bias'][0]
    p['w_rg'] = a['w_rg'][0]
    p['w_ig'] = a['w_ig'][0]
    p['conv_w'] = conv_w_full
    p['g_final'] = a['g_final'][None, :]
    chip_arr = jnp.reshape(chip, (1,)).astype(jnp.int32)
    loss_part, grad_x, small, _, part, sib, packs = _local_step(
        a['x'][0], a['mem'][0], a['loss_target'][0], p, {'w_in': w_in_g}, shards, chip_arr)

    sib['w_in'], = _comm_only("swap_w_in", [("swap", [part['w_in']])])
    out = {}
    for group in (['w_in'], MID, ['w_gate', 'w_up', 'w_down']):
        results = _final_adamw([part[n] for n in group], [sib[n] for n in group], [shard(n) for n in group],
                               [shard('m_' + n) for n in group], [shard('v_' + n) for n in group],
                               "adamw_" + group[0])
        for n, res in zip(group, results):
            out[n] = [jnp.swapaxes(r, 0, 1) for r in res] if n in TRANSPOSED else res

    def natural(arr):
        return arr[0] if arr.ndim >= 3 else (arr[None, :] if arr.ndim == 1 else arr)

    small_out = _adamw_small(packs, _ar_late(LATE, [small[n] for n in LATE]), [small[n].shape for n in SMALL],
                             loss_part.shape, *[[natural(a[pre + n]) for n in SMALL] for pre in ('', 'm_', 'v_')])
    ns = len(SMALL)
    loss = small_out[4 * ns][0, 0]

    def leaf(i, n):
        if n in BIG:
            return out[n][i][None]
        return small_out[i * ns + SMALL.index(n)].reshape(a[n].shape)

    return (loss, grad_x[None], *[leaf(i, n) for i in range(4) for n in WEIGHTS])
```

```python
import math

import jax
import jax.numpy as jnp
from jax import lax
from jax.experimental import pallas as pl
from jax.experimental.pallas import tpu as pltpu

F32 = jnp.float32
BF16 = jnp.bfloat16

D_MODEL = 1024
D_ATT = 512
D_LRU = 512
HEAD_DIM = 64
ATT_HEADS = 8
CHUNK = 64
LEFT_CHUNKS = 8
MAX_REL = 128
X_HEADS = 4
X_HEAD_DIM = 256
N_SHARD = 4
IN_SH = 640
D_IN = N_SHARD * IN_SH
FF_SH = 704
D_FF = N_SHARD * FF_SH
EPS = 1e-6
LRU_C = 8.0
LRU_BLOCKS = 8
LRU_BLOCK = 64
QB = 256
KB = 768
ROLL_W = 1024
NEG = -1e30
ATT_SCALE = HEAD_DIM ** -0.5
X_SCALE = X_HEAD_DIM ** -0.5

ADAM_LR = 0.001
ADAM_B1 = 0.9
ADAM_B2 = 0.999
ADAM_EPS = 1e-08
ADAM_WD = 0.01
ADAM_STEP = 10

VMEM_LIMIT_V7X = 56 * 1024 * 1024
BF16_ROWS = 16


EW_VMEM_BUDGET = 40 * 1024 * 1024


def _ew_steps(rows, bytes_per_row):
    return min(s for s in (2, 4, 8, 16) if rows % (s * BF16_ROWS) == 0
               and 2 * (rows // s) * bytes_per_row <= EW_VMEM_BUDGET)
MESH_ID = pl.DeviceIdType.MESH

WEIGHTS = ['g_mix', 'w_in', 'rel_bias', 'conv_w', 'conv_b', 'w_rg', 'b_rg', 'w_ig', 'b_ig', 'lru_L',
           'g_out_attn', 'g_out_lru', 'w_out', 'g_cross', 'g_mem', 'wq_c', 'wk_c', 'wv_c', 'wo_c',
           'g_ffn', 'w_gate', 'w_up', 'w_down', 'g_final']
BIG = ['w_in', 'w_out', 'wq_c', 'wk_c', 'wv_c', 'wo_c', 'w_gate', 'w_up', 'w_down']
SMALL = [n for n in WEIGHTS if n not in BIG]


def _sds(shape, dtype):
    return jax.ShapeDtypeStruct(shape, dtype)


def _cp(*sem):
    return pltpu.CompilerParams(dimension_semantics=sem or None, vmem_limit_bytes=VMEM_LIMIT_V7X)


def _rows(tm, n):
    return pl.BlockSpec((tm, n), lambda i: (i, 0))


def _full(shape):
    nd = len(shape)
    return pl.BlockSpec(shape, lambda i: (0,) * nd)


def _dot(a, b):
    return jnp.dot(a, b, preferred_element_type=F32)


def _dot_nt(a, b):
    return lax.dot_general(a, b, (((1,), (1,)), ((), ())), preferred_element_type=F32)


def _dot_tn(a, b):
    return lax.dot_general(a, b, (((0,), (0,)), ((), ())), preferred_element_type=F32)


def _rinv(x):
    return lax.rsqrt(jnp.mean(x * x, axis=-1, keepdims=True) + EPS)


def _rms_bwd(dy, x, g):
    r = _rinv(x)
    yh = x * r
    dyh = dy * g
    dx = r * (dyh - yh * jnp.mean(dyh * yh, axis=-1, keepdims=True))
    return dx, jnp.sum(dy * yh, axis=0, keepdims=True)


def _gelu(x):
    c = math.sqrt(2.0 / math.pi)
    t = jnp.tanh(c * (x + 0.044715 * x * x * x))
    return 0.5 * x * (1.0 + t)


def _gelu_and_grad(x):
    c = math.sqrt(2.0 / math.pi)
    t = jnp.tanh(c * (x + 0.044715 * x * x * x))
    g = 0.5 * x * (1.0 + t)
    dg = 0.5 * (1.0 + t) + 0.5 * x * (1.0 - t * t) * c * (1.0 + 3.0 * 0.044715 * x * x)
    return g, dg


def _neg_expm1(z):
    series = -z * (1.0 + z * (0.5 + z * ((1.0 / 6.0) + z * (1.0 / 24.0))))
    return jnp.where(z > -0.03, series, 1.0 - jnp.exp(z))


def _lru_gates(u, wrg, brg, wig, big, lam):
    ub = u.astype(BF16)
    r = jax.nn.sigmoid(_dot(ub, wrg) + brg)
    ig = jax.nn.sigmoid(_dot(ub, wig) + big)
    sp = jnp.maximum(-lam, 0.0) + jnp.log1p(jnp.exp(-jnp.abs(lam)))
    la = -LRU_C * r * sp
    a = jnp.exp(la)
    mult = jnp.sqrt(jnp.maximum(_neg_expm1(2.0 * la), 0.0))
    return ub, r, ig, sp, a, mult


def _scan8(a8, b8, hprev):
    row = lax.broadcasted_iota(jnp.int32, a8.shape, 0)
    aa, bb = a8, b8
    for d in (1, 2, 4):
        a_s = pltpu.roll(aa, d, 0)
        b_s = pltpu.roll(bb, d, 0)
        m = row >= d
        bb = jnp.where(m, aa * b_s + bb, bb)
        aa = jnp.where(m, aa * a_s, aa)
    return aa * hprev + bb


def _rscan8(c8, d8, lnext):
    row = lax.broadcasted_iota(jnp.int32, c8.shape, 0)
    cc, dd = c8, d8
    for d in (1, 2, 4):
        c_s = pltpu.roll(cc, 8 - d, 0)
        d_s = pltpu.roll(dd, 8 - d, 0)
        m = row < 8 - d
        dd = jnp.where(m, cc * d_s + dd, dd)
        cc = jnp.where(m, cc * c_s, cc)
    return cc * lnext + dd


def _mesh_pos():
    return lax.axis_index("x"), lax.axis_index("y"), lax.axis_index("c")


def _other_chips(x, y):
    return [(1 - x, y), (x, 1 - y), (1 - x, 1 - y)]


def _no_forward():
    pass


def _ag_full_copies(ins, outs, sems):
    send_sems, recv_sems, loc_sems = sems
    n = len(ins)
    x, y, c = _mesh_pos()
    mine = 2 * x + y
    chips = _other_chips(x, y)

    def remote(k, j, slot):
        px, py = chips[j]
        return pltpu.make_async_remote_copy(
            src_ref=ins[k], dst_ref=outs[k].at[slot], send_sem=send_sems.at[k, j], recv_sem=recv_sems.at[k, j],
            device_id=(px, py, c), device_id_type=MESH_ID)

    def local(k):
        return pltpu.make_async_copy(ins[k], outs[k].at[mine], loc_sems.at[k])

    def start():
        for k in range(n):
            local(k).start()
            for j in range(3):
                remote(k, j, mine).start()

    def wait():
        for k in range(n):
            for j, (px, py) in enumerate(chips):
                remote(k, j, 2 * px + py).wait_recv()
        for k in range(n):
            for j in range(3):
                remote(k, j, mine).wait_send()
            local(k).wait()

    return start, _no_forward, wait


def _ag_copies(ins, outs, sems):
    send_sems, recv_sems, fsend_sems, frecv_sems, loc_sems = sems
    n = len(ins)
    x, y, c = _mesh_pos()
    mine = 2 * x + y
    chips = _other_chips(x, y)

    def half(ref, hc):
        r = ref.shape[0] // 2
        return ref.at[pl.ds(pl.multiple_of(hc * r, 16), r)]

    def ici(k, j, slot):
        px, py = chips[j]
        return pltpu.make_async_remote_copy(
            src_ref=half(ins[k], c), dst_ref=half(outs[k].at[slot], c),
            send_sem=send_sems.at[k, j], recv_sem=recv_sems.at[k, j],
            device_id=(px, py, c), device_id_type=MESH_ID)

    def d2d(k, j, hc):
        px, py = chips[j]
        part = half(outs[k].at[2 * px + py], hc)
        return pltpu.make_async_remote_copy(
            src_ref=part, dst_ref=part, send_sem=fsend_sems.at[k, j], recv_sem=frecv_sems.at[k, j],
            device_id=(x, y, 1 - c), device_id_type=MESH_ID)

    def local(k):
        return pltpu.make_async_copy(ins[k], outs[k].at[mine], loc_sems.at[k])

    def start():
        for k in range(n):
            local(k).start()
            for j in range(3):
                ici(k, j, mine).start()

    def forward():
        for k in range(n):
            for j, (px, py) in enumerate(chips):
                ici(k, j, 2 * px + py).wait_recv()
                d2d(k, j, c).start()

    def wait():
        for k in range(n):
            for j in range(3):
                d2d(k, j, 1 - c).wait_recv()
        for k in range(n):
            for j in range(3):
                d2d(k, j, c).wait_send()
                ici(k, j, mine).wait_send()
            local(k).wait()

    return start, forward, wait


def _rs_copies(ins, outs, sems):
    send_sems, recv_sems = sems
    n = len(ins)
    x, y, c = _mesh_pos()
    chips = _other_chips(x, y)

    def remote(k, j):
        px, py = chips[j]
        return pltpu.make_async_remote_copy(
            src_ref=ins[k].at[2 * px + py], dst_ref=outs[k].at[j],
            send_sem=send_sems.at[k, j], recv_sem=recv_sems.at[k, j],
            device_id=(px, py, c), device_id_type=MESH_ID)

    def start():
        for k in range(n):
            for j in range(3):
                remote(k, j).start()

    def wait():
        for k in range(n):
            for j in range(3):
                remote(k, j).wait_recv()
        for k in range(n):
            for j in range(3):
                remote(k, j).wait_send()

    return start, _no_forward, wait


def _swap_copies(ins, outs, sems):
    send_sems, recv_sems = sems
    x, y, c = _mesh_pos()
    copies = [pltpu.make_async_remote_copy(
        src_ref=ins[k], dst_ref=outs[k], send_sem=send_sems.at[k], recv_sem=recv_sems.at[k],
        device_id=(x, y, 1 - c), device_id_type=MESH_ID) for k in range(len(ins))]

    def start():
        for cp in copies:
            cp.start()

    def wait():
        for cp in copies:
            cp.wait()

    return start, _no_forward, wait


def _comm_plan(groups):
    plan, arrs, shapes, sems = [], [], [], []
    for kind, group in groups:
        k = len(group)
        arrs += group
        per_peer = pltpu.SemaphoreType.DMA((k, 3))
        if kind == "ag":
            shapes += [_sds((N_SHARD,) + w.shape, w.dtype) for w in group]
            gsems = [per_peer] * 4 + [pltpu.SemaphoreType.DMA((k,))]
            maker = _ag_copies
        elif kind == "agf":
            shapes += [_sds((N_SHARD,) + w.shape, w.dtype) for w in group]
            gsems = [per_peer] * 2 + [pltpu.SemaphoreType.DMA((k,))]
            maker = _ag_full_copies
        elif kind == "ag8":
            shapes += [_sds((8,) + g.shape, g.dtype) for g in group]
            gsems = [pltpu.SemaphoreType.DMA((k, 7))] * 2 + [pltpu.SemaphoreType.DMA((k,))]
            maker = _ag8_copies
        elif kind == "rs":
            shapes += [_sds((3,) + g.shape[1:], g.dtype) for g in group]
            gsems = [pltpu.SemaphoreType.DMA((k, 3)), pltpu.SemaphoreType.DMA((k, 3))]
            maker = _rs_copies
        else:
            shapes += [_sds(g.shape, g.dtype) for g in group]
            gsems = [pltpu.SemaphoreType.DMA((k,)), pltpu.SemaphoreType.DMA((k,))]
            maker = _swap_copies
        plan.append((maker, k, len(gsems)))
        sems += gsems
    return plan, arrs, shapes, sems


def _comm_fns(plan, cins, couts, sems):
    fns, a, s = [], 0, 0
    for maker, k, ns in plan:
        fns.append(maker(cins[a:a + k], couts[a:a + k], sems[s:s + ns]))
        a += k
        s += ns

    def start():
        for st, _, _ in fns:
            st()

    def forward():
        for _, fw, _ in fns:
            fw()

    def wait():
        for _, _, wt in fns:
            wt()

    return start, forward, wait


def _call(body, name, grid, in_specs, out_specs, out_shape, scratch, args, sem, comm=None):
    if not comm:
        return pl.pallas_call(body, name=name, grid=grid, in_specs=in_specs, out_specs=out_specs,
                              out_shape=out_shape, scratch_shapes=scratch, compiler_params=_cp(sem))(*args)
    plan, c_arrs, c_shapes, c_sems = _comm_plan(comm)
    k = len(c_arrs)
    n_in, n_out, n_scr = len(in_specs), len(out_specs), len(scratch)
    last = grid[0] - 1
    fwd_step = max(1, (2 * last) // 3)

    def wrapped(*refs):
        ins, cins = refs[:n_in], refs[n_in:n_in + k]
        o0 = n_in + k
        outs, couts = refs[o0:o0 + n_out], refs[o0 + n_out:o0 + n_out + k]
        s0 = o0 + n_out + k
        start, forward, wait = _comm_fns(plan, cins, couts, refs[s0 + n_scr:])
        pl.when(pl.program_id(0) == 0)(start)
        pl.when(pl.program_id(0) == fwd_step)(forward)
        body(*ins, *outs, *refs[s0:s0 + n_scr])
        pl.when(pl.program_id(0) == last)(wait)

    return pl.pallas_call(
        wrapped, name=name, grid=grid, in_specs=list(in_specs) + [_any()] * k,
        out_specs=list(out_specs) + [_any()] * k, out_shape=list(out_shape) + c_shapes,
        scratch_shapes=list(scratch) + c_sems, compiler_params=_cp(sem))(*args, *c_arrs)


def _comm_only(name, comm):
    plan, c_arrs, c_shapes, c_sems = _comm_plan(comm)
    k = len(c_arrs)

    def body(*refs):
        start, forward, wait = _comm_fns(plan, refs[:k], refs[k:2 * k], refs[2 * k:])
        start()
        forward()
        wait()

    return pl.pallas_call(body, name=name, in_specs=[_any()] * k, out_specs=[_any()] * k, out_shape=c_shapes,
                          scratch_shapes=c_sems, compiler_params=_cp())(*c_arrs)


def _any():
    return pl.BlockSpec(memory_space=pl.ANY)


def _load_w_in_once(w_hbm, w_ref):
    @pl.when(pl.program_id(0) == 0)
    def _():
        for s in range(N_SHARD):
            pltpu.sync_copy(w_hbm.at[s], w_ref.at[:, pl.ds(s * IN_SH, IN_SH)])


def _f_inproj(x, g_mix, w_in_g, tm, comm=None):
    s_len = x.shape[0]
    pad_rows = LEFT_CHUNKS * CHUNK
    npad = pad_rows // tm

    def body(x_ref, g_ref, w_hbm, h_ref, qkv_ref, xg_ref, w_ref):
        i = pl.program_id(0)
        _load_w_in_once(w_hbm, w_ref)

        @pl.when(i < npad)
        def _():
            qkv_ref[...] = jnp.zeros_like(qkv_ref)

        @pl.when(i >= npad)
        def _():
            xv = x_ref[...]
            h = (xv * _rinv(xv) * g_ref[...]).astype(BF16)
            h_ref[...] = h
            proj = _dot(h, w_ref[...])
            qkv_ref[:, 0:D_ATT] = (proj[:, 0:D_ATT] * ATT_SCALE).astype(BF16)
            qkv_ref[:, D_ATT:3 * D_ATT] = proj[:, D_ATT:3 * D_ATT].astype(BF16)
            xg_ref[...] = proj[:, 3 * D_ATT:D_IN]

    def tok(n):
        return pl.BlockSpec((tm, n), lambda i: (jnp.maximum(i - npad, 0), 0))

    return _call(
        body, "f_inproj", (s_len // tm + npad,),
        [tok(1024), _full((1, 1024)), _any()],
        [tok(1024), _rows(tm, 1536), tok(1024)],
        [_sds((s_len, 1024), BF16), _sds((s_len + pad_rows, 1536), BF16), _sds((s_len, 1024), F32)],
        [pltpu.VMEM((1024, D_IN), BF16)], (x, g_mix, w_in_g), "arbitrary", comm)


N_BIAS = 3


def _bias_table(frow_ref, bias_sc):
    qa = lax.broadcasted_iota(jnp.int32, (QB, KB), 0) // CHUNK
    kcol = lax.broadcasted_iota(jnp.int32, (QB, KB), 1)
    kb = kcol // CHUNK
    band = jnp.where((kb >= qa) & (kb - qa <= LEFT_CHUNKS), 0.0, NEG).astype(F32)
    for h in range(ATT_HEADS):
        row = jnp.broadcast_to(frow_ref[h:h + 1, :], (QB, ROLL_W))
        toep = pltpu.roll(row, 0, 1, stride=1, stride_axis=0)
        gen = toep[:, 0:KB] + band
        bias_sc[N_BIAS - 1, h] = gen
        for v in range(N_BIAS - 1):
            pad_keys = LEFT_CHUNKS * CHUNK - v * QB
            bias_sc[v, h] = gen + jnp.where(kcol < pad_keys, NEG, 0.0).astype(F32)


def _even_lanes():
    return lax.broadcasted_iota(jnp.int32, (1, 2 * HEAD_DIM), 1) < HEAD_DIM


def _att_probs(qm, kts, bias):
    s = jnp.concatenate([_dot_nt(qm, k) for k in kts], axis=1) + bias
    return jnp.exp(s - jnp.max(s, axis=-1, keepdims=True))


def _att_in_specs(clamp):
    def spec(j, col):
        return pl.BlockSpec((QB, D_ATT), lambda i: (clamp(i) + j, col))
    return [spec(2, 0), spec(0, 1), spec(1, 1), spec(2, 1), spec(0, 2), spec(1, 2), spec(2, 2)]


def _f_attn(qkv_pad, frow, comm=None):
    s_len = qkv_pad.shape[0] - LEFT_CHUNKS * CHUNK
    nb = s_len // QB

    def body(q_ref, k0, k1, k2, v0, v1, v2, frow_ref, o_ref, bias_sc):
        i = pl.program_id(0)

        @pl.when(i == 0)
        def _():
            _bias_table(frow_ref, bias_sc)

        var = jnp.minimum(i, N_BIAS - 1)
        even = _even_lanes()
        for hp in range(ATT_HEADS // 2):
            cs = slice(hp * 2 * HEAD_DIM, (hp + 1) * 2 * HEAD_DIM)
            qt = q_ref[:, cs]
            kts = [k0[:, cs], k1[:, cs], k2[:, cs]]
            vts = [v0[:, cs], v1[:, cs], v2[:, cs]]
            res = []
            for e in range(2):
                keep = even if e == 0 else jnp.logical_not(even)
                pb = _att_probs(jnp.where(keep, qt, 0), kts, bias_sc[var, 2 * hp + e]).astype(BF16)
                r = _dot(pb, jnp.concatenate([jnp.where(keep, v, 1) for v in vts], axis=0))
                res.append(r / pltpu.roll(r, HEAD_DIM, 1))
            o_ref[:, cs] = jnp.where(even, res[0], res[1])

    return _call(
        body, "f_attn", (nb,),
        _att_in_specs(lambda i: i) + [_full((ATT_HEADS, ROLL_W))],
        [_rows(QB, D_ATT)], [_sds((s_len, D_ATT), F32)],
        [pltpu.VMEM((N_BIAS, ATT_HEADS, QB, KB), F32)], (*([qkv_pad] * 7), frow), "arbitrary", comm)


def _f_lru(xg, conv_w, conv_b, wrg, brg, wig, big, lam, tl, comm=None):
    s_len = xg.shape[0]

    def body(xg_ref, cw_ref, cb_ref, wrg_ref, brg_ref, wig_ref, big_ref, l_ref,
             rec_ref, u_ref, hs_ref, xbuf, a_sc, b_sc, hcar):
        i = pl.program_id(0)

        @pl.when(i == 0)
        def _():
            xbuf[0:8, :] = jnp.zeros((8, D_LRU), F32)
            hcar[...] = jnp.zeros((8, D_LRU), F32)

        xu0 = xg_ref[:, 0:D_LRU]
        xbuf[8:8 + tl, :] = xu0
        u = cb_ref[...] + cw_ref[0:1, :] * xbuf[pl.ds(5, tl), :]
        for j in range(1, 4):
            u = u + cw_ref[j:j + 1, :] * xbuf[pl.ds(5 + j, tl), :]
        xbuf[0:8, :] = xu0[tl - 8:tl, :]
        u_ref[...] = u
        _, _, ig, _, a, mult = _lru_gates(u, wrg_ref[...], brg_ref[...], wig_ref[...], big_ref[...], l_ref[...])
        a_sc[...] = a
        b_sc[...] = mult * (ig * u)

        def grp(g, hprev):
            off = pl.multiple_of(g * 8, 8)
            h8 = _scan8(a_sc[pl.ds(off, 8), :], b_sc[pl.ds(off, 8), :], hprev)
            hs_ref[pl.ds(off, 8), :] = h8
            return h8[7:8, :]

        hcar[0:1, :] = lax.fori_loop(0, tl // 8, grp, hcar[0:1, :])
        rec_ref[...] = hs_ref[...] * _gelu(xg_ref[:, D_LRU:2 * D_LRU])

    vec = _full((1, D_LRU))
    return _call(
        body, "f_lru", (s_len // tl,),
        [_rows(tl, 1024), _full((4, D_LRU)), vec, _full((D_LRU, D_LRU)), vec, _full((D_LRU, D_LRU)), vec, vec],
        [_rows(tl, D_LRU)] * 3, [_sds((s_len, D_LRU), F32)] * 3,
        [pltpu.VMEM((tl + 8, D_LRU), F32), pltpu.VMEM((tl, D_LRU), F32),
         pltpu.VMEM((tl, D_LRU), F32), pltpu.VMEM((8, D_LRU), F32)],
        (xg, conv_w, conv_b, wrg, brg, wig, big, lam), "arbitrary", comm)


def _f_mem(mem, g_mem, wk, wv):
    def body(mem_ref, g_ref, wk_ref, wv_ref, mn_ref, kx_ref, vx_ref):
        mv = mem_ref[...]
        mn = (mv * _rinv(mv) * g_ref[...]).astype(BF16)
        mn_ref[...] = mn
        kx_ref[...] = _dot(mn, wk_ref[...]).astype(BF16)
        vx_ref[...] = _dot(mn, wv_ref[...]).astype(BF16)

    m = mem.shape[0]
    return pl.pallas_call(
        body, name="f_mem", out_shape=[_sds((m, 1024), BF16)] * 3,
        compiler_params=_cp())(mem, g_mem, wk, wv)


def _xattn_probs(q, k):
    s = _dot_nt(q, k) * X_SCALE
    m = jnp.max(s, axis=-1, keepdims=True)
    p = jnp.exp(s - m)
    return p, jnp.sum(p, axis=-1, keepdims=True)


def _f_mid(x, att, rec, g_oa, g_ol, w_out, g_cross, wq, kx, vx, wo, tm, comm=None):
    s_len = x.shape[0]
    m_len = kx.shape[0]

    def body(x_ref, att_ref, rec_ref, goa_ref, gol_ref, wout_ref, gc_ref, wq_ref, kx_ref, vx_ref, wo_ref,
             mg_ref, x1_ref, hc_ref, qx_ref, ox_ref, x2_ref):
        av = att_ref[...]
        rv = rec_ref[...]
        mg_ref[:, 0:D_ATT] = (av * _rinv(av) * goa_ref[...]).astype(BF16)
        mg_ref[:, D_ATT:1024] = (rv * _rinv(rv) * gol_ref[...]).astype(BF16)
        x1 = x_ref[...] + _dot(mg_ref[...], wout_ref[...])
        x1_ref[...] = x1
        hc = (x1 * _rinv(x1) * gc_ref[...]).astype(BF16)
        hc_ref[...] = hc
        qx_ref[...] = _dot(hc, wq_ref[...]).astype(BF16)
        for h in range(X_HEADS):
            sl = slice(h * X_HEAD_DIM, (h + 1) * X_HEAD_DIM)
            p, l = _xattn_probs(qx_ref[:, sl], kx_ref[:, sl])
            ox_ref[:, sl] = (_dot(p.astype(BF16), vx_ref[:, sl]) / l).astype(BF16)
        x2_ref[...] = x1 + _dot(ox_ref[...], wo_ref[...])

    sq = _full((1024, 1024))
    return _call(
        body, "f_mid", (s_len // tm,),
        [_rows(tm, 1024), _rows(tm, 512), _rows(tm, 512), _full((1, 512)), _full((1, 512)), sq,
         _full((1, 1024)), sq, _full((m_len, 1024)), _full((m_len, 1024)), sq],
        [_rows(tm, 1024)] * 6,
        [_sds((s_len, 1024), BF16), _sds((s_len, 1024), F32), _sds((s_len, 1024), BF16),
         _sds((s_len, 1024), BF16), _sds((s_len, 1024), BF16), _sds((s_len, 1024), F32)],
        [], (x, att, rec, g_oa, g_ol, w_out, g_cross, wq, kx, vx, wo), "arbitrary", comm)


def _load_weights_once(pairs):
    @pl.when(pl.program_id(0) == 0)
    def _():
        for hbm, vmem in pairs:
            pltpu.sync_copy(hbm, vmem)


FF_CHUNKS = [(0, 1280), (1280, D_FF)]


def _f_ffn(x2, tgt, g_ffn, g_final, wg, wu, wd, tm):
    s_len = x2.shape[0]

    def body(x2_ref, t_ref, gf_ref, gfin_ref, wg_hbm, wu_hbm, wd_hbm,
             hf_ref, g_ref, u_ref, a_ref, dx3_ref, loss_ref, dgfin_ref, wg_ref, wu_ref, wd_ref):
        _load_weights_once([(wg_hbm, wg_ref), (wu_hbm, wu_ref), (wd_hbm, wd_ref)])

        @pl.when(pl.program_id(0) == 0)
        def _():
            loss_ref[...] = jnp.zeros_like(loss_ref)
            dgfin_ref[...] = jnp.zeros_like(dgfin_ref)

        x2v = x2_ref[...]
        hf = (x2v * _rinv(x2v) * gf_ref[...]).astype(BF16)
        hf_ref[...] = hf
        x3 = x2v
        for c0, c1 in FF_CHUNKS:
            gv = _dot_nt(hf, wg_ref[c0:c1, :])
            uv = _dot_nt(hf, wu_ref[c0:c1, :])
            av = (gv * jax.nn.sigmoid(gv) * uv).astype(BF16)
            g_ref[:, c0:c1] = gv.astype(BF16)
            u_ref[:, c0:c1] = uv.astype(BF16)
            a_ref[:, c0:c1] = av
            x3 = x3 + _dot(av, wd_ref[c0:c1, :])
        r3 = _rinv(x3)
        yh = x3 * r3
        gfin = gfin_ref[...]
        err = yh * gfin - t_ref[...]
        loss_ref[...] += jnp.full((1, 128), 0.5 / D_MODEL, F32) * jnp.sum(err * err)
        dy = err * (1.0 / D_MODEL)
        dgfin_ref[...] += jnp.sum(dy * yh, axis=0, keepdims=True)
        dyh = dy * gfin
        dx3_ref[...] = r3 * (dyh - yh * jnp.mean(dyh * yh, axis=-1, keepdims=True))

    vec = _full((1, 1024))
    return pl.pallas_call(
        body, name="f_ffn", grid=(s_len // tm,),
        in_specs=[_rows(tm, 1024), _rows(tm, 1024), vec, vec, _any(), _any(), _any()],
        out_specs=[_rows(tm, 1024), _rows(tm, D_FF), _rows(tm, D_FF), _rows(tm, D_FF),
                   _rows(tm, 1024), _full((1, 128)), vec],
        out_shape=[_sds((s_len, 1024), BF16)] + [_sds((s_len, D_FF), BF16)] * 3
                  + [_sds((s_len, 1024), F32), _sds((1, 128), F32), _sds((1, 1024), F32)],
        scratch_shapes=[pltpu.VMEM((D_FF, 1024), BF16)] * 3,
        compiler_params=_cp("arbitrary"))(x2, tgt, g_ffn, g_final, wg, wu, wd)


def _b_ffn(dx3, x2, gact, uact, g_ffn, wg, wu, wd, tm):
    s_len = x2.shape[0]

    def body(dx3_ref, x2_ref, g_ref, u_ref, gf_ref, wg_hbm, wu_hbm, wd_hbm,
             dg_ref, du_ref, dx2_ref, dgf_ref, wg_ref, wu_ref, wd_ref):
        _load_weights_once([(wg_hbm, wg_ref), (wu_hbm, wu_ref), (wd_hbm, wd_ref)])

        @pl.when(pl.program_id(0) == 0)
        def _():
            dgf_ref[...] = jnp.zeros_like(dgf_ref)

        dx3v = dx3_ref[...]
        dx3b = dx3v.astype(BF16)
        dhf = jnp.zeros(dx3v.shape, F32)
        for c0, c1 in FF_CHUNKS:
            da = _dot_nt(dx3b, wd_ref[c0:c1, :])
            gv = g_ref[:, c0:c1].astype(F32)
            uv = u_ref[:, c0:c1].astype(F32)
            sg = jax.nn.sigmoid(gv)
            dub = (da * gv * sg).astype(BF16)
            dgb = (da * uv * (sg * (1.0 + gv * (1.0 - sg)))).astype(BF16)
            du_ref[:, c0:c1] = dub
            dg_ref[:, c0:c1] = dgb
            dhf = dhf + _dot(dgb, wg_ref[c0:c1, :]) + _dot(dub, wu_ref[c0:c1, :])
        dx, dgf = _rms_bwd(dhf, x2_ref[...], gf_ref[...])
        dx2_ref[...] = dx3v + dx
        dgf_ref[...] += dgf

    vec = _full((1, 1024))
    return pl.pallas_call(
        body, name="b_ffn", grid=(s_len // tm,),
        in_specs=[_rows(tm, 1024), _rows(tm, 1024), _rows(tm, D_FF), _rows(tm, D_FF), vec,
                  _any(), _any(), _any()],
        out_specs=[_rows(tm, D_FF), _rows(tm, D_FF), _rows(tm, 1024), vec],
        out_shape=[_sds((s_len, D_FF), BF16)] * 2 + [_sds((s_len, 1024), F32), _sds((1, 1024), F32)],
        scratch_shapes=[pltpu.VMEM((D_FF, 1024), BF16)] * 3,
        compiler_params=_cp("arbitrary"))(dx3, x2, gact, uact, g_ffn, wg, wu, wd)


def _b_mid(dx2, qx, x1, att, rec, kx, vx, wo, wq, w_out, g_cross, g_oa, g_ol, tm, comm=None):
    s_len = x1.shape[0]
    m_len = kx.shape[0]

    def body(dx2_ref, qx_ref, x1_ref, att_ref, rec_ref, kx_ref, vx_ref, wo_ref, wq_ref, wout_ref,
             gc_ref, goa_ref, gol_ref,
             dqx_ref, dx1_ref, datt_ref, drec_ref, dkx_ref, dvx_ref, dgc_ref, dgoa_ref, dgol_ref):
        @pl.when(pl.program_id(0) == 0)
        def _():
            for r in (dkx_ref, dvx_ref, dgc_ref, dgoa_ref, dgol_ref):
                r[...] = jnp.zeros_like(r)

        dx2v = dx2_ref[...]
        dox = _dot_nt(dx2v.astype(BF16), wo_ref[...])
        for h in range(X_HEADS):
            sl = slice(h * X_HEAD_DIM, (h + 1) * X_HEAD_DIM)
            q = qx_ref[:, sl]
            p, l = _xattn_probs(q, kx_ref[:, sl])
            pn = p * (1.0 / l)
            dob = dox[:, sl].astype(BF16)
            dp = _dot_nt(dob, vx_ref[:, sl])
            dvx_ref[:, sl] += _dot_tn(pn.astype(BF16), dob)
            ds = pn * (dp - jnp.sum(dp * pn, axis=-1, keepdims=True))
            dsb = (ds * X_SCALE).astype(BF16)
            dqx_ref[:, sl] = _dot(dsb, kx_ref[:, sl]).astype(BF16)
            dkx_ref[:, sl] += _dot_tn(dsb, q)
        dhc = _dot_nt(dqx_ref[...], wq_ref[...])
        dx, dgc = _rms_bwd(dhc, x1_ref[...], gc_ref[...])
        dx1 = dx2v + dx
        dx1_ref[...] = dx1
        dgc_ref[...] += dgc
        dmg = _dot_nt(dx1.astype(BF16), wout_ref[...])
        da, dgoa = _rms_bwd(dmg[:, 0:D_ATT], att_ref[...], goa_ref[...])
        datt_ref[...] = da
        dgoa_ref[...] += dgoa
        dr, dgol = _rms_bwd(dmg[:, D_ATT:1024], rec_ref[...], gol_ref[...])
        drec_ref[...] = dr
        dgol_ref[...] += dgol

    sq = _full((1024, 1024))
    mk = _full((m_len, 1024))
    return _call(
        body, "b_mid", (s_len // tm,),
        [_rows(tm, 1024), _rows(tm, 1024), _rows(tm, 1024), _rows(tm, 512), _rows(tm, 512), mk, mk,
         sq, sq, sq, _full((1, 1024)), _full((1, 512)), _full((1, 512))],
        [_rows(tm, 1024), _rows(tm, 1024), _rows(tm, 512), _rows(tm, 512), mk, mk,
         _full((1, 1024)), _full((1, 512)), _full((1, 512))],
        [_sds((s_len, 1024), BF16), _sds((s_len, 1024), F32), _sds((s_len, 512), F32),
         _sds((s_len, 512), F32), _sds((m_len, 1024), F32), _sds((m_len, 1024), F32),
         _sds((1, 1024), F32), _sds((1, 512), F32), _sds((1, 512), F32)],
        [], (dx2, qx, x1, att, rec, kx, vx, wo, wq, w_out, g_cross, g_oa, g_ol), "arbitrary", comm)


def _b_mem(dkx, dvx, mem, mn, g_mem, wk, wv):
    def body(dkx_ref, dvx_ref, mem_ref, mn_ref, g_ref, wk_ref, wv_ref, dwk_ref, dwv_ref, dgm_ref,
             dwkb_ref, dwvb_ref):
        dkb = dkx_ref[...].astype(BF16)
        dvb = dvx_ref[...].astype(BF16)
        dwk = _dot_tn(mn_ref[...], dkb)
        dwv = _dot_tn(mn_ref[...], dvb)
        dwk_ref[...] = dwk
        dwv_ref[...] = dwv
        dwkb_ref[...] = dwk.astype(BF16)
        dwvb_ref[...] = dwv.astype(BF16)
        dmn = _dot_nt(dkb, wk_ref[...]) + _dot_nt(dvb, wv_ref[...])
        mv = mem_ref[...]
        dgm_ref[...] = jnp.sum(dmn * (mv * _rinv(mv)), axis=0, keepdims=True)

    return pl.pallas_call(
        body, name="b_mem",
        out_shape=[_sds((1024, 1024), F32), _sds((1024, 1024), F32), _sds((1, 1024), F32),
                   _sds((1024, 1024), BF16), _sds((1024, 1024), BF16)],
        compiler_params=_cp())(dkx, dvx, mem, mn, g_mem, wk, wv)


def _b_lru(drec, hs, u, xg, conv_w, wrg, brg, wig, big, lam, tl, comm=None):
    s_len = xg.shape[0]
    nt = s_len // tl

    def body(drec_ref, hs_ref, hsp_ref, u_ref, xg_ref, cw_ref, wrg_ref, brg_ref, wig_ref, big_ref, l_ref,
             dxg_ref, dwrg_ref, dwig_ref, dbrg_ref, dbig_ref, dlam_ref, dcw_ref, dcb_ref,
             hbuf, abuf, dubuf, c_sc, d_sc, lam_sc, lcar, wacc_r, wacc_i):
        i = pl.program_id(0)
        tt = nt - 1 - i

        @pl.when(i == 0)
        def _():
            for r in (wacc_r, wacc_i, dbrg_ref, dbig_ref, dlam_ref, dcw_ref, dcb_ref):
                r[...] = jnp.zeros_like(r)
            abuf[tl:tl + 8, :] = jnp.zeros((8, D_LRU), F32)
            dubuf[tl:tl + 8, :] = jnp.zeros((8, D_LRU), F32)
            lcar[...] = jnp.zeros((8, D_LRU), F32)

        xu0 = xg_ref[:, 0:D_LRU]
        hsv = hs_ref[...]
        uv = u_ref[...]
        hbuf[8:8 + tl, :] = hsv
        hbuf[0:8, :] = jnp.where(tt > 0, hsp_ref[...], 0.0)
        hshift = hbuf[pl.ds(7, tl), :]
        wrg_v = wrg_ref[...]
        wig_v = wig_ref[...]
        lamv = l_ref[...]
        ub, r, ig, sp, a, mult = _lru_gates(uv, wrg_v, brg_ref[...], wig_v, big_ref[...], lamv)
        abuf[0:tl, :] = a
        c_sc[...] = abuf[pl.ds(1, tl), :]
        gel, dgel = _gelu_and_grad(xg_ref[:, D_LRU:2 * D_LRU])
        drv = drec_ref[...]
        d_sc[...] = drv * gel
        dxg_ref[:, D_LRU:2 * D_LRU] = (drv * hsv * dgel).astype(BF16)

        def grp(k, lnext):
            off = pl.multiple_of((tl // 8 - 1 - k) * 8, 8)
            l8 = _rscan8(c_sc[pl.ds(off, 8), :], d_sc[pl.ds(off, 8), :], lnext)
            lam_sc[pl.ds(off, 8), :] = l8
            return l8[0:1, :]

        lcar[0:1, :] = lax.fori_loop(0, tl // 8, grp, lcar[0:1, :])
        abuf[tl:tl + 8, :] = a[0:8, :]
        db = lam_sc[...]
        da = db * hshift
        dmult = db * (ig * uv)
        dig = db * mult * uv
        du = db * mult * ig
        dla = da * a - dmult * (a * a) / mult
        dlam_ref[...] += jnp.sum(dla * (-LRU_C) * r, axis=0, keepdims=True)
        dzr = dla * (-LRU_C * sp) * r * (1.0 - r)
        dzi = dig * ig * (1.0 - ig)
        dzrb = dzr.astype(BF16)
        dzib = dzi.astype(BF16)
        du = du + _dot_nt(dzrb, wrg_v) + _dot_nt(dzib, wig_v)
        wacc_r[...] += _dot_tn(ub, dzrb)
        wacc_i[...] += _dot_tn(ub, dzib)
        dbrg_ref[...] += jnp.sum(dzr, axis=0, keepdims=True)
        dbig_ref[...] += jnp.sum(dzi, axis=0, keepdims=True)
        dcb_ref[...] += jnp.sum(du, axis=0, keepdims=True)
        dubuf[0:tl, :] = du
        dxu0 = jnp.zeros((tl, D_LRU), F32)
        for j in range(4):
            dsh = dubuf[pl.ds(3 - j, tl), :]
            dxu0 = dxu0 + cw_ref[j:j + 1, :] * dsh
            dcw_ref[j:j + 1, :] += jnp.sum(xu0 * dsh, axis=0, keepdims=True)
        dubuf[tl:tl + 8, :] = du[0:8, :]
        dxg_ref[:, 0:D_LRU] = dxu0.astype(BF16)

        @pl.when(i == nt - 1)
        def _():
            dlam_ref[...] = dlam_ref[...] * (-jax.nn.sigmoid(-lamv))
            for n in range(LRU_BLOCKS):
                blk = slice(n * LRU_BLOCK, (n + 1) * LRU_BLOCK)
                dwrg_ref[n] = wacc_r[blk, blk]
                dwig_ref[n] = wacc_i[blk, blk]

    def rev(n):
        return pl.BlockSpec((tl, n), lambda i: (nt - 1 - i, 0))

    prev8 = pl.BlockSpec((8, D_LRU), lambda i: (jnp.maximum((nt - 1 - i) * (tl // 8) - 1, 0), 0))
    vec = _full((1, D_LRU))
    sq = _full((D_LRU, D_LRU))
    blocks_shape = (LRU_BLOCKS, LRU_BLOCK, LRU_BLOCK)
    blocks = _full(blocks_shape)
    return _call(
        body, "b_lru", (nt,),
        [rev(D_LRU), rev(D_LRU), prev8, rev(D_LRU), rev(1024), _full((4, D_LRU)), sq, vec, sq, vec, vec],
        [rev(1024), blocks, blocks, vec, vec, vec, _full((4, D_LRU)), vec],
        [_sds((s_len, 1024), BF16), _sds(blocks_shape, F32), _sds(blocks_shape, F32),
         _sds((1, D_LRU), F32), _sds((1, D_LRU), F32), _sds((1, D_LRU), F32),
         _sds((4, D_LRU), F32), _sds((1, D_LRU), F32)],
        [pltpu.VMEM((tl + 8, D_LRU), F32)] * 3 + [pltpu.VMEM((tl, D_LRU), F32)] * 3
        + [pltpu.VMEM((8, D_LRU), F32)] + [pltpu.VMEM((D_LRU, D_LRU), F32)] * 2,
        (drec, hs, hs, u, xg, conv_w, wrg, brg, wig, big, lam), "arbitrary", comm)


def _b_attn(qkv_pad, att, datt, frow, comm=None):
    s_len = datt.shape[0]
    nb = s_len // QB
    n_pair = ATT_HEADS // 2
    pair_w = 2 * HEAD_DIM

    def body(q_ref, k0, k1, k2, v0, v1, v2, o_ref, do_ref, frow_ref, dq_ref, dkv_ref, dfrow_ref,
             bias_sc, dt_sc, acc_sc):
        t = pl.program_id(0)

        @pl.when(t == 0)
        def _():
            _bias_table(frow_ref, bias_sc)
            dt_sc[...] = jnp.zeros_like(dt_sc)
            acc_sc[...] = jnp.zeros_like(acc_sc)

        @pl.when(t < nb)
        def _():
            var = jnp.minimum(t, N_BIAS - 1)
            even = _even_lanes()
            for hp in range(n_pair):
                cs = slice(hp * pair_w, (hp + 1) * pair_w)
                qt = q_ref[:, cs]
                kts = [k0[:, cs], k1[:, cs], k2[:, cs]]
                vts = [v0[:, cs], v1[:, cs], v2[:, cs]]
                kcat = jnp.concatenate(kts, axis=0)
                dot = do_ref[:, cs]
                dd = dot * o_ref[:, cs]
                dos_pair, dsbs, pbs, dqs = None, [], [], []
                for e in range(2):
                    keep = even if e == 0 else jnp.logical_not(even)
                    qm = jnp.where(keep, qt, 0)
                    p = _att_probs(qm, kts, bias_sc[var, 2 * hp + e])
                    inv = 1.0 / jnp.sum(p, axis=-1, keepdims=True)
                    dos = jnp.where(keep, dot * inv, 0.0)
                    delta = jnp.sum(jnp.where(keep, dd, 0.0), axis=-1, keepdims=True) * inv
                    dp = jnp.concatenate([_dot_nt(dos.astype(BF16), v) for v in vts], axis=1)
                    ds = p * (dp - delta)
                    dt_sc[2 * hp + e] += ds
                    dsb = ds.astype(BF16)
                    dq = _dot(dsb, kcat)
                    dqs.append(dq)
                    dsbs.append(dsb)
                    pbs.append(p.astype(BF16))
                    dos_pair = dos if e == 0 else dos_pair + dos
                dq_ref[:, cs] = (jnp.where(even, dqs[0], dqs[1]) * ATT_SCALE).astype(BF16)
                qtt = qt.astype(F32).T.astype(BF16)
                dost = dos_pair.T.astype(BF16)
                for j in range(3):
                    slot = (t + 1 + j) % 3
                    js = slice(j * QB, (j + 1) * QB)
                    for e in range(2):
                        hr = slice(e * HEAD_DIM, (e + 1) * HEAD_DIM)
                        acc_sc[slot, hp, hr, :] += _dot(qtt[hr], dsbs[e][:, js])
                        acc_sc[slot, n_pair + hp, hr, :] += _dot(dost[hr], pbs[e][:, js])

        done = (t + 1) % 3

        @pl.when(t >= 2)
        def _():
            for i in range(2 * n_pair):
                dkv_ref[:, i * pair_w:(i + 1) * pair_w] = acc_sc[done, i].T.astype(BF16)

        acc_sc[done] = jnp.zeros((2 * n_pair, pair_w, QB), F32)

        @pl.when(t == nb + 1)
        def _():
            row = lax.broadcasted_iota(jnp.int32, (8, ROLL_W), 0)
            pad = jnp.zeros((8, ROLL_W - KB), F32)
            for h in range(ATT_HEADS):
                acc8 = jnp.concatenate([dt_sc[h, 0:8, :], pad], axis=1)
                for a1 in range(1, QB // 8):
                    blk = jnp.concatenate([dt_sc[h, 8 * a1:8 * a1 + 8, :], pad], axis=1)
                    acc8 = acc8 + pltpu.roll(blk, ROLL_W - 8 * a1, 1)
                for k in range(3):
                    acc8 = jnp.where(((row >> k) & 1) == 1, pltpu.roll(acc8, ROLL_W - (1 << k), 1), acc8)
                dfrow_ref[h:h + 1, :] = jnp.sum(acc8, axis=0, keepdims=True)

    clamp = lambda t: jnp.minimum(t, nb - 1)
    qrows = pl.BlockSpec((QB, D_ATT), lambda t: (clamp(t), 0))
    return _call(
        body, "b_attn", (nb + 2,),
        _att_in_specs(clamp) + [qrows, qrows, _full((ATT_HEADS, ROLL_W))],
        [qrows, pl.BlockSpec((QB, 2 * D_ATT), lambda t: (jnp.maximum(t - 2, 0), 0)),
         _full((ATT_HEADS, ROLL_W))],
        [_sds((s_len, D_ATT), BF16), _sds((s_len, 2 * D_ATT), BF16), _sds((ATT_HEADS, ROLL_W), F32)],
        [pltpu.VMEM((N_BIAS, ATT_HEADS, QB, KB), F32), pltpu.VMEM((ATT_HEADS, QB, KB), F32),
         pltpu.VMEM((3, 2 * n_pair, pair_w, QB), F32)],
        (*([qkv_pad] * 7), att, datt, frow), "arbitrary", comm)


def _flush_grad(steps, acc, accb, out_hbm, outb_hbm):
    @pl.when(pl.program_id(0) == steps - 1)
    def _():
        accb[...] = acc[...].astype(BF16)
        pltpu.sync_copy(acc, out_hbm)
        pltpu.sync_copy(accb, outb_hbm)


def _b_win(dq, dkv, dxg, h, ts):
    s_len = h.shape[0]
    steps = s_len // ts

    def body(dq_ref, dkv_ref, dxg_ref, h_ref, dw_hbm, dwb_hbm, acc, accb):
        @pl.when(pl.program_id(0) == 0)
        def _():
            acc[...] = jnp.zeros_like(acc)

        dproj = jnp.concatenate([dq_ref[...], dkv_ref[...], dxg_ref[...]], axis=1)
        acc[...] += _dot_tn(h_ref[...], dproj)

        @pl.when(pl.program_id(0) == steps - 1)
        def _():
            accb[...] = acc[...].astype(BF16)
            for s in range(N_SHARD):
                cols = pl.ds(s * IN_SH, IN_SH)
                pltpu.sync_copy(acc.at[:, cols], dw_hbm.at[s])
                pltpu.sync_copy(accb.at[:, cols], dwb_hbm.at[s])

    shape = (N_SHARD, 1024, IN_SH)
    return pl.pallas_call(
        body, name="b_win", grid=(steps,),
        in_specs=[_rows(ts, 512), _rows(ts, 1024), _rows(ts, 1024), _rows(ts, 1024)],
        out_specs=[_any()] * 2, out_shape=[_sds(shape, F32), _sds(shape, BF16)],
        scratch_shapes=[pltpu.VMEM((1024, D_IN), F32), pltpu.VMEM((1024, D_IN), BF16)],
        compiler_params=_cp("arbitrary"))(dq, dkv, dxg, h)


def _b_inproj(dq, dkv, dxg, x, dx1, g_mix, w_in_g, tm, comm=None):
    s_len = x.shape[0]

    def body(dq_ref, dkv_ref, dxg_ref, x_ref, dx1_ref, g_ref, w_hbm, gx_ref, dgm_ref, w_ref):
        _load_w_in_once(w_hbm, w_ref)

        @pl.when(pl.program_id(0) == 0)
        def _():
            dgm_ref[...] = jnp.zeros_like(dgm_ref)

        dproj = jnp.concatenate([dq_ref[...], dkv_ref[...], dxg_ref[...]], axis=1)
        dh = _dot_nt(dproj, w_ref[...])
        dx, dgm = _rms_bwd(dh, x_ref[...], g_ref[...])
        gx_ref[...] = dx1_ref[...] + dx
        dgm_ref[...] += dgm

    return _call(
        body, "b_inproj", (s_len // tm,),
        [_rows(tm, 512), _rows(tm, 1024), _rows(tm, 1024), _rows(tm, 1024), _rows(tm, 1024),
         _full((1, 1024)), _any()],
        [_rows(tm, 1024), _full((1, 1024))],
        [_sds((s_len, 1024), F32), _sds((1, 1024), F32)],
        [pltpu.VMEM((1024, D_IN), BF16)], (dq, dkv, dxg, x, dx1, g_mix, w_in_g), "arbitrary", comm)


def _mm_tn(xa, ya, name, ts):
    s_len, k = xa.shape
    n = ya.shape[1]

    steps = s_len // ts

    def body(x_ref, y_ref, o_hbm, ob_hbm, acc, accb):
        @pl.when(pl.program_id(0) == 0)
        def _():
            acc[...] = jnp.zeros_like(acc)
        acc[...] += _dot_tn(x_ref[...].astype(BF16), y_ref[...].astype(BF16))
        _flush_grad(steps, acc, accb, o_hbm, ob_hbm)

    return pl.pallas_call(
        body, name=name, grid=(steps,), in_specs=[_rows(ts, k), _rows(ts, n)],
        out_specs=[_any()] * 2, out_shape=[_sds((k, n), F32), _sds((k, n), BF16)],
        scratch_shapes=[pltpu.VMEM((k, n), F32), pltpu.VMEM((k, n), BF16)],
        compiler_params=_cp("arbitrary"))(xa, ya)


PAD_KEYS = LEFT_CHUNKS * CHUNK
F_HI = PAD_KEYS - MAX_REL + 1
F_LO = PAD_KEYS + MAX_REL


def _frow_from_rel_bias(rb):
    last = rb[:, 2 * MAX_REL:2 * MAX_REL + 1]
    hi = jnp.broadcast_to(last, (ATT_HEADS, F_HI))
    mid = rb[:, 1:2 * MAX_REL][:, ::-1]
    lo = jnp.broadcast_to(rb[:, 0:1], (ATT_HEADS, KB - F_LO))
    wrap = jnp.broadcast_to(last, (ATT_HEADS, ROLL_W - KB))
    return jnp.concatenate([hi, mid, lo, wrap], axis=1)


def _rel_bias_grad_from_dfrow(df):
    g_last = jnp.sum(df[:, 0:F_HI], axis=1, keepdims=True) + jnp.sum(df[:, KB:ROLL_W], axis=1, keepdims=True)
    mid = df[:, F_HI:F_LO][:, ::-1]
    g_first = jnp.sum(df[:, F_LO:KB], axis=1, keepdims=True)
    return jnp.concatenate([g_first, mid, g_last], axis=1)


def _block_diag(w):
    eye = jnp.eye(8, dtype=w.dtype)
    return (w[:, :, None, :] * eye[:, None, :, None]).reshape(D_LRU, D_LRU)


MID = ['w_out', 'wq_c', 'wk_c', 'wv_c', 'wo_c']
TRANSPOSED = ['w_gate', 'w_up']
AG_IN_INPROJ = ['w_out', 'wq_c', 'wk_c']
AG_IN_ATTN = ['wv_c', 'wo_c', 'w_gate']
AG_IN_LRU = ['w_up']
AG_IN_MID = ['w_down']
RS_IN_MID = ['w_gate', 'w_up']
RS_IN_LRU = ['w_down']
RS_IN_ATTN = MID


def _local_step(x, mem, tgt, p, gw, shards=None, chip=None):
    s_len = x.shape[0]
    tm = min(256, s_len)
    tmb = min(512, s_len)
    tl = min(512, s_len)
    frow = _frow_from_rel_bias(p['rel_bias'])
    wrg = _block_diag(p['w_rg']).astype(BF16)
    wig = _block_diag(p['w_ig']).astype(BF16)
    gw = dict(gw)

    big, bigb, recv, part, sib = {}, {}, {}, {}, {}

    def ag(names):
        return [] if shards is None else [("ag", [shards[n] for n in names])]

    def rs(names):
        return [] if shards is None else [("rs", [bigb[n] for n in names])]

    def swap(names):
        return [] if shards is None else [("swap", [part[n] for n in names])]

    def reduce_own(names):
        if shards is not None:
            sums = _sum_parts([big[n] for n in names], [recv[n] for n in names], chip, "sum_" + names[0])
            part.update(zip(names, sums))

    h, qkv_pad, xg, *got = _f_inproj(x, p['g_mix'], gw['w_in'], tmb, ag(AG_IN_INPROJ))
    gw.update(zip(AG_IN_INPROJ, got))
    att, *got = _f_attn(qkv_pad, frow, ag(AG_IN_ATTN))
    gw.update(zip(AG_IN_ATTN, got))
    rec, u, hs, *got = _f_lru(xg, p['conv_w'], p['conv_b'], wrg, p['b_rg'], wig, p['b_ig'], p['lru_L'], tl,
                              ag(AG_IN_LRU))
    gw.update(zip(AG_IN_LRU, got))
    w_out = gw['w_out'].reshape(1024, 1024)
    wq = gw['wq_c'].reshape(1024, 1024)
    wk = gw['wk_c'].reshape(1024, 1024)
    wv = gw['wv_c'].reshape(1024, 1024)
    wo = gw['wo_c'].reshape(1024, 1024)
    mn, kx, vx = _f_mem(mem, p['g_mem'], wk, wv)
    mg, x1, hc, qx, ox, x2, *got = _f_mid(x, att, rec, p['g_out_attn'], p['g_out_lru'], w_out, p['g_cross'],
                                          wq, kx, vx, wo, tmb, ag(AG_IN_MID))
    gw.update(zip(AG_IN_MID, got))
    ffn_w = [gw[n].reshape(D_FF, 1024) for n in ('w_gate', 'w_up', 'w_down')]
    hf, gact, uact, aact, dx3, loss, dg_final = _f_ffn(x2, tgt, p['g_ffn'], p['g_final'], *ffn_w, tmb)

    ts = min(1024, s_len)
    dgact, duact, dx2, dg_ffn = _b_ffn(dx3, x2, gact, uact, p['g_ffn'], *ffn_w, tm)
    big['w_gate'], bigb['w_gate'] = _mm_tn(dgact, hf, "dw_gate", ts)
    big['w_up'], bigb['w_up'] = _mm_tn(duact, hf, "dw_up", ts)
    big['w_down'], bigb['w_down'] = _mm_tn(aact, dx3, "dw_down", ts)
    for n in ('w_gate', 'w_up', 'w_down'):
        big[n] = big[n].reshape(N_SHARD, FF_SH, 1024)
        bigb[n] = bigb[n].reshape(N_SHARD, FF_SH, 1024)

    dqx, dx1, datt, drec, dkx, dvx, dg_cross, dg_oa, dg_ol, *got = _b_mid(
        dx2, qx, x1, att, rec, kx, vx, wo, wq, w_out, p['g_cross'], p['g_out_attn'], p['g_out_lru'], tmb,
        rs(RS_IN_MID))
    recv.update(zip(RS_IN_MID, got))
    reduce_own(RS_IN_MID)
    dwk, dwv, dg_mem, dwkb, dwvb = _b_mem(dkx, dvx, mem, mn, p['g_mem'], wk, wv)
    big['wk_c'], bigb['wk_c'] = dwk, dwkb
    big['wv_c'], bigb['wv_c'] = dwv, dwvb
    big['w_out'], bigb['w_out'] = _mm_tn(mg, dx1, "dw_out", ts)
    big['wq_c'], bigb['wq_c'] = _mm_tn(hc, dqx, "dw_q", ts)
    big['wo_c'], bigb['wo_c'] = _mm_tn(ox, dx2, "dw_o", ts)
    for n in MID:
        big[n] = big[n].reshape(N_SHARD, 256, 1024)
        bigb[n] = bigb[n].reshape(N_SHARD, 256, 1024)

    dxg, dwrg, dwig, dbrg, dbig, dlam, dcw, dcb, *got = _b_lru(
        drec, hs, u, xg, p['conv_w'], wrg, p['b_rg'], wig, p['b_ig'], p['lru_L'], tl,
        rs(RS_IN_LRU) + swap(RS_IN_MID))
    recv.update(zip(RS_IN_LRU, got))
    sib.update(zip(RS_IN_MID, got[len(RS_IN_LRU):]))
    reduce_own(RS_IN_LRU)
    small = {
        'conv_w': dcw, 'conv_b': dcb, 'w_rg': dwrg, 'b_rg': dbrg, 'w_ig': dwig, 'b_ig': dbig, 'lru_L': dlam,
        'g_out_attn': dg_oa, 'g_out_lru': dg_ol, 'g_cross': dg_cross, 'g_mem': dg_mem, 'g_ffn': dg_ffn,
        'g_final': dg_final,
    }
    names = [n for n in SMALL if n in small]
    gather = [] if shards is None else [("ag8", [_pack_small(names, [small[n] for n in names], loss)])]
    dq, dkv, dfrow, *got = _b_attn(qkv_pad, att, datt, frow, rs(RS_IN_ATTN) + swap(RS_IN_LRU) + gather)
    recv.update(zip(RS_IN_ATTN, got))
    sib.update(zip(RS_IN_LRU, got[len(RS_IN_ATTN):]))
    packs = got[-1] if gather else None
    reduce_own(RS_IN_ATTN)
    small['rel_bias'] = _rel_bias_grad_from_dfrow(dfrow)
    big['w_in'], bigb['w_in'] = _b_win(dq, dkv, dxg, h, ts)
    grad_x, small['g_mix'], *got = _b_inproj(dq, dkv, dxg, x, dx1, p['g_mix'], gw['w_in'], tmb,
                                             rs(['w_in']) + swap(RS_IN_ATTN))
    recv.update(zip(['w_in'], got))
    sib.update(zip(RS_IN_ATTN, got[1:]))
    reduce_own(['w_in'])
    return loss, grad_x, small, big, part, sib, packs


CAST_STEPS = 4


def _cast_shards(ws, name, comm=None):
    def body(*refs):
        n = len(refs) // 2
        for src, dst in zip(refs[:n], refs[n:]):
            dst[...] = src[...].astype(BF16)

    specs = [_rows(w.shape[0] // CAST_STEPS, w.shape[1]) for w in ws]
    return _call(body, name, (CAST_STEPS,), specs, specs, [_sds(w.shape, BF16) for w in ws], [], tuple(ws),
                 "arbitrary", comm)


def _sum_parts(own4s, recv3s, chip, name):
    n = len(own4s)
    _, r, c = own4s[0].shape
    steps = _ew_steps(r, n * c * (4 + 3 * 2 + 4))
    tr = r // steps

    def body(chip_ref, *refs):
        for own_ref, rc_ref, o_ref in zip(refs[:n], refs[n:2 * n], refs[2 * n:]):
            o_ref[...] = ((own_ref[0] + rc_ref[0].astype(F32)) + rc_ref[1].astype(F32)) + rc_ref[2].astype(F32)

    grid_spec = pltpu.PrefetchScalarGridSpec(
        num_scalar_prefetch=1, grid=(steps,),
        in_specs=[pl.BlockSpec((1, tr, c), lambda i, ch: (ch[0], i, 0))] * n
                 + [pl.BlockSpec((3, tr, c), lambda i, ch: (0, i, 0))] * n,
        out_specs=[pl.BlockSpec((tr, c), lambda i, ch: (i, 0))] * n)
    return pl.pallas_call(body, name=name, grid_spec=grid_spec, out_shape=[_sds((r, c), F32)] * n,
                          compiler_params=_cp("parallel"))(chip, *own4s, *recv3s)


def _adamw_math(w, g, m, v):
    m = ADAM_B1 * m + (1.0 - ADAM_B1) * g
    v = ADAM_B2 * v + (1.0 - ADAM_B2) * (g * g)
    m_hat = m / (1.0 - ADAM_B1 ** ADAM_STEP)
    v_hat = v / (1.0 - ADAM_B2 ** ADAM_STEP)
    delta = -ADAM_LR * (m_hat / (jnp.sqrt(v_hat) + ADAM_EPS) + ADAM_WD * w)
    return delta, m, v


def _final_adamw(pas, pbs, ws, ms, vs, name):
    n = len(ws)
    r, c = ws[0].shape
    steps = _ew_steps(r, n * c * 9 * 4)
    tr = r // steps

    def body(*refs):
        ins, outs = refs[:5 * n], refs[5 * n:]
        for k in range(n):
            pa_ref, pb_ref, w_ref, m_ref, v_ref = (ins[j * n + k] for j in range(5))
            g = pa_ref[...] + pb_ref[...]
            outs[4 * k][...] = g
            outs[4 * k + 1][...], outs[4 * k + 2][...], outs[4 * k + 3][...] = _adamw_math(
                w_ref[...], g, m_ref[...], v_ref[...])

    res = pl.pallas_call(
        body, name=name, grid=(steps,), in_specs=[_rows(tr, c)] * (5 * n), out_specs=[_rows(tr, c)] * (4 * n),
        out_shape=[_sds((r, c), F32)] * (4 * n), compiler_params=_cp("parallel"))(*pas, *pbs, *ws, *ms, *vs)
    return [res[4 * k:4 * k + 4] for k in range(n)]


def _pack_put(ref, name, val_ref):
    r = _pack_rows()[name]
    shape = val_ref.shape
    if len(shape) == 3:
        for b in range(shape[0]):
            ref[r:r + shape[1], b * shape[2]:(b + 1) * shape[2]] = val_ref[b]
    elif shape[1] == 2 * PACK_W:
        ref[r:r + 1, :] = val_ref[:, 0:PACK_W]
        ref[r + 1:r + 2, :] = val_ref[:, PACK_W:2 * PACK_W]
    else:
        ref[r:r + shape[0], 0:shape[1]] = val_ref[...]


def _pack_get(ref, name, shape):
    r = _pack_rows()[name]
    if len(shape) == 3:
        return jnp.stack([ref[r:r + shape[1], b * shape[2]:(b + 1) * shape[2]] for b in range(shape[0])])
    if shape[1] == 2 * PACK_W:
        return jnp.concatenate([ref[r:r + 1, :], ref[r + 1:r + 2, :]], axis=1)
    return ref[r:r + shape[0], 0:shape[1]]


def _pack_small(names, g, loss):
    n = len(g)

    def body(*refs):
        pack = refs[n + 1]
        pack[...] = jnp.zeros_like(pack)
        for a, name in enumerate(names):
            _pack_put(pack, name, refs[a])
        _pack_put(pack, 'loss', refs[n])

    return pl.pallas_call(body, name="pack_small", out_shape=_sds((PACK_ROWS, PACK_W), F32),
                          compiler_params=_cp())(*g, loss)


def _all_peers():
    x, y, c = _mesh_pos()
    peers = []
    for k in range(1, 8):
        px = 1 - x if k & 4 else x
        py = 1 - y if k & 2 else y
        pc = 1 - c if k & 1 else c
        peers.append(((px, py, pc), 4 * px + 2 * py + pc))
    return peers, 4 * x + 2 * y + c


def _ag8_copies(ins, outs, sems):
    send_sems, recv_sems, loc_sems = sems
    n = len(ins)
    peers, me = _all_peers()

    def remote(k, j, slot):
        return pltpu.make_async_remote_copy(
            src_ref=ins[k], dst_ref=outs[k].at[slot], send_sem=send_sems.at[k, j], recv_sem=recv_sems.at[k, j],
            device_id=peers[j][0], device_id_type=MESH_ID)

    def local(k):
        return pltpu.make_async_copy(ins[k], outs[k].at[me], loc_sems.at[k])

    def start():
        for k in range(n):
            local(k).start()
            for j in range(7):
                remote(k, j, me).start()

    def wait():
        for k in range(n):
            for j in range(7):
                remote(k, j, peers[j][1]).wait_recv()
        for k in range(n):
            for j in range(7):
                remote(k, j, me).wait_send()
            local(k).wait()

    return start, _no_forward, wait


def _ar_late(names, g):
    n = len(g)

    def body(*refs):
        tot_ref, pack, buf, send_sems, recv_sems = refs[n:]
        peers, me = _all_peers()

        def remote(j, slot):
            return pltpu.make_async_remote_copy(
                src_ref=pack, dst_ref=buf.at[slot], send_sem=send_sems.at[j], recv_sem=recv_sems.at[j],
                device_id=peers[j][0], device_id_type=MESH_ID)

        pack[...] = jnp.zeros_like(pack)
        for a, name in enumerate(names):
            _pack_put(pack, name, refs[a])
        for j in range(7):
            remote(j, me).start()
        buf[me] = pack[...]
        for j in range(7):
            remote(j, peers[j][1]).wait_recv()
        for j in range(7):
            remote(j, me).wait_send()
        tot = buf[0]
        for d in range(1, 8):
            tot = tot + buf[d]
        tot_ref[...] = tot

    return pl.pallas_call(
        body, name="ar_late", out_shape=_sds((LATE_ROWS, PACK_W), F32),
        scratch_shapes=[pltpu.VMEM((LATE_ROWS, PACK_W), F32), pltpu.VMEM((8, LATE_ROWS, PACK_W), F32),
                        pltpu.SemaphoreType.DMA((7,)), pltpu.SemaphoreType.DMA((7,))],
        compiler_params=_cp())(*g)


def _adamw_small(packs, late_tot, g_shapes, loss_shape, w, m, v):
    n = len(w)

    def body(*refs):
        packs_ref, late_ref = refs[0], refs[1]
        w_refs, m_refs, v_refs = (refs[2 + i * n:2 + (i + 1) * n] for i in range(3))
        o0 = 3 * n + 2
        go, do, mo, vo = (refs[o0 + i * n:o0 + (i + 1) * n] for i in range(4))
        loss_out, tot_ref = refs[o0 + 4 * n], refs[o0 + 4 * n + 1]
        x, y, _ = _mesh_pos()
        tot = packs_ref[0]
        for d in range(1, 8):
            tot = tot + packs_ref[d]
        tot_ref[...] = tot
        tot_ref[0:LATE_ROWS, :] += late_ref[...]
        loss_out[...] = _pack_get(tot_ref, 'loss', loss_shape)
        for a, name in enumerate(SMALL):
            if name == 'conv_w':
                r = _pack_rows()[name]
                ga = tot_ref[r:r + g_shapes[a][0], pl.ds(pl.multiple_of((2 * x + y) * 128, 128), 128)]
            else:
                ga = _pack_get(tot_ref, name, g_shapes[a])
            go[a][...] = ga
            do[a][...], mo[a][...], vo[a][...] = _adamw_math(w_refs[a][...], ga, m_refs[a][...], v_refs[a][...])

    out_shape = [_sds(a.shape, F32) for a in w] * 4 + [_sds(loss_shape, F32)]
    return pl.pallas_call(body, name="adamw_small", out_shape=out_shape,
                          scratch_shapes=[pltpu.VMEM((PACK_ROWS, PACK_W), F32)],
                          compiler_params=_cp())(packs, late_tot, *w, *m, *v)


PACK_W = 512
PACK_ROWS = 160
LATE = ['g_mix', 'rel_bias']
LATE_ROWS = 32


def _pack_rows():
    rows, r = {}, 0
    for name in ['g_mix', 'g_cross', 'g_mem', 'g_ffn', 'g_final']:
        rows[name] = r
        r += 2
    for name in ['conv_b', 'b_rg', 'b_ig', 'lru_L', 'g_out_attn', 'g_out_lru']:
        rows[name] = r
        r += 1
    rows['conv_w'] = r
    rows['loss'] = r + 4
    rows['rel_bias'] = 24
    rows['w_rg'] = 32
    rows['w_ig'] = 32 + LRU_BLOCK
    assert r + 5 <= 24 and rows['w_ig'] + LRU_BLOCK == PACK_ROWS
    assert rows['g_mix'] + 2 <= LATE_ROWS and rows['rel_bias'] + 8 <= LATE_ROWS
    return rows


INPUT_NAMES = (['x', 'mem'] + WEIGHTS + ['loss_target'] + ['m_' + n for n in WEIGHTS] + ['v_' + n for n in WEIGHTS])


def kernel(x, mem, g_mix, w_in, rel_bias, conv_w, conv_b, w_rg, b_rg, w_ig, b_ig, lru_L, g_out_attn, g_out_lru, w_out, g_cross, g_mem, wq_c, wk_c, wv_c, wo_c, g_ffn, w_gate, w_up, w_down, g_final, loss_target, m_g_mix, m_w_in, m_rel_bias, m_conv_w, m_conv_b, m_w_rg, m_b_rg, m_w_ig, m_b_ig, m_lru_L, m_g_out_attn, m_g_out_lru, m_w_out, m_g_cross, m_g_mem, m_wq_c, m_wk_c, m_wv_c, m_wo_c, m_g_ffn, m_w_gate, m_w_up, m_w_down, m_g_final, v_g_mix, v_w_in, v_rel_bias, v_conv_w, v_conv_b, v_w_rg, v_b_rg, v_w_ig, v_b_ig, v_lru_L, v_g_out_attn, v_g_out_lru, v_w_out, v_g_cross, v_g_mem, v_wq_c, v_wk_c, v_wv_c, v_wo_c, v_g_ffn, v_w_gate, v_w_up, v_w_down, v_g_final):
    a = dict(zip(INPUT_NAMES, (x, mem, g_mix, w_in, rel_bias, conv_w, conv_b, w_rg, b_rg, w_ig, b_ig, lru_L, g_out_attn, g_out_lru, w_out, g_cross, g_mem, wq_c, wk_c, wv_c, wo_c, g_ffn, w_gate, w_up, w_down, g_final, loss_target, m_g_mix, m_w_in, m_rel_bias, m_conv_w, m_conv_b, m_w_rg, m_b_rg, m_w_ig, m_b_ig, m_lru_L, m_g_out_attn, m_g_out_lru, m_w_out, m_g_cross, m_g_mem, m_wq_c, m_wk_c, m_wv_c, m_wo_c, m_g_ffn, m_w_gate, m_w_up, m_w_down, m_g_final, v_g_mix, v_w_in, v_rel_bias, v_conv_w, v_conv_b, v_w_rg, v_b_rg, v_w_ig, v_b_ig, v_lru_L, v_g_out_attn, v_g_out_lru, v_w_out, v_g_cross, v_g_mem, v_wq_c, v_wk_c, v_wv_c, v_wo_c, v_g_ffn, v_w_gate, v_w_up, v_w_down, v_g_final)))
    chip = 2 * lax.axis_index("x") + lax.axis_index("y")

    def shard(name):
        arr = a[name][0]
        base = name[2:] if name[:2] in ('m_', 'v_') else name
        return jnp.swapaxes(arr, 0, 1) if base in TRANSPOSED else arr

    shards = {'w_in': _cast_shards([shard('w_in')], "cast_w_in")[0]}
    rest = [n for n in BIG if n != 'w_in']
    *cast, w_in_g, conv_w_g = _cast_shards([shard(n) for n in rest], "cast_rest",
                                           [("ag", [shards['w_in']]), ("agf", [a['conv_w'][0]])])
    shards.update(zip(rest, cast))
    conv_w_full = conv_w_g.transpose(1, 0, 2).reshape(4, D_LRU)

    p = {n: a[n] for n in SMALL}
    p['rel_bias'] = a['rel_bias'][0]
    p['w_rg'] = a['w_rg'][0]
    p['w_ig'] = a['w_ig'][0]
    p['conv_w'] = conv_w_full
    p['g_final'] = a['g_final'][None, :]
    chip_arr = jnp.reshape(chip, (1,)).astype(jnp.int32)
    loss_part, grad_x, small, _, part, sib, packs = _local_step(
        a['x'][0], a['mem'][0], a['loss_target'][0], p, {'w_in': w_in_g}, shards, chip_arr)

    sib['w_in'], = _comm_only("swap_w_in", [("swap", [part['w_in']])])
    out = {}
    for group in (['w_in'], MID, ['w_gate', 'w_up', 'w_down']):
        results = _final_adamw([part[n] for n in group], [sib[n] for n in group], [shard(n) for n in group],
                               [shard('m_' + n) for n in group], [shard('v_' + n) for n in group],
                               "adamw_" + group[0])
        for n, res in zip(group, results):
            out[n] = [jnp.swapaxes(r, 0, 1) for r in res] if n in TRANSPOSED else res

    def natural(arr):
        return arr[0] if arr.ndim >= 3 else (arr[None, :] if arr.ndim == 1 else arr)

    small_out = _adamw_small(packs, _ar_late(LATE, [small[n] for n in LATE]), [small[n].shape for n in SMALL],
                             loss_part.shape, *[[natural(a[pre + n]) for n in SMALL] for pre in ('', 'm_', 'v_')])
    ns = len(SMALL)
    loss = small_out[4 * ns][0, 0]

    def leaf(i, n):
        if n in BIG:
            return out[n][i][None]
        return small_out[i * ns + SMALL.index(n)].reshape(a[n].shape)

    return (loss, grad_x[None], *[leaf(i, n) for i in range(4) for n in WEIGHTS])
```

```python
import math

import jax
import jax.numpy as jnp
from jax import lax
from jax.experimental import pallas as pl
from jax.experimental.pallas import tpu as pltpu

F32 = jnp.float32
BF16 = jnp.bfloat16

D_MODEL = 1024
D_ATT = 512
D_LRU = 512
HEAD_DIM = 64
ATT_HEADS = 8
CHUNK = 64
LEFT_CHUNKS = 8
MAX_REL = 128
X_HEADS = 4
X_HEAD_DIM = 256
N_SHARD = 4
IN_SH = 640
D_IN = N_SHARD * IN_SH
FF_SH = 704
D_FF = N_SHARD * FF_SH
EPS = 1e-6
LRU_C = 8.0
LRU_BLOCKS = 8
LRU_BLOCK = 64
QB = 256
KB = 768
ROLL_W = 1024
NEG = -1e30
ATT_SCALE = HEAD_DIM ** -0.5
X_SCALE = X_HEAD_DIM ** -0.5

ADAM_LR = 0.001
ADAM_B1 = 0.9
ADAM_B2 = 0.999
ADAM_EPS = 1e-08
ADAM_WD = 0.01
ADAM_STEP = 10

VMEM_LIMIT_V7X = 56 * 1024 * 1024
BF16_ROWS = 16


EW_VMEM_BUDGET = 40 * 1024 * 1024


def _ew_steps(rows, bytes_per_row):
    return min(s for s in (2, 4, 8, 16) if rows % (s * BF16_ROWS) == 0
               and 2 * (rows // s) * bytes_per_row <= EW_VMEM_BUDGET)
MESH_ID = pl.DeviceIdType.MESH

WEIGHTS = ['g_mix', 'w_in', 'rel_bias', 'conv_w', 'conv_b', 'w_rg', 'b_rg', 'w_ig', 'b_ig', 'lru_L',
           'g_out_attn', 'g_out_lru', 'w_out', 'g_cross', 'g_mem', 'wq_c', 'wk_c', 'wv_c', 'wo_c',
           'g_ffn', 'w_gate', 'w_up', 'w_down', 'g_final']
BIG = ['w_in', 'w_out', 'wq_c', 'wk_c', 'wv_c', 'wo_c', 'w_gate', 'w_up', 'w_down']
SMALL = [n for n in WEIGHTS if n not in BIG]


def _sds(shape, dtype):
    return jax.ShapeDtypeStruct(shape, dtype)


def _cp(*sem):
    return pltpu.CompilerParams(dimension_semantics=sem or None, vmem_limit_bytes=VMEM_LIMIT_V7X)


def _rows(tm, n):
    return pl.BlockSpec((tm, n), lambda i: (i, 0))


def _full(shape):
    nd = len(shape)
    return pl.BlockSpec(shape, lambda i: (0,) * nd)


def _dot(a, b):
    return jnp.dot(a, b, preferred_element_type=F32)


def _dot_nt(a, b):
    return lax.dot_general(a, b, (((1,), (1,)), ((), ())), preferred_element_type=F32)


def _dot_tn(a, b):
    return lax.dot_general(a, b, (((0,), (0,)), ((), ())), preferred_element_type=F32)


def _rinv(x):
    return lax.rsqrt(jnp.mean(x * x, axis=-1, keepdims=True) + EPS)


def _rms_bwd(dy, x, g):
    r = _rinv(x)
    yh = x * r
    dyh = dy * g
    dx = r * (dyh - yh * jnp.mean(dyh * yh, axis=-1, keepdims=True))
    return dx, jnp.sum(dy * yh, axis=0, keepdims=True)


def _gelu(x):
    c = math.sqrt(2.0 / math.pi)
    t = jnp.tanh(c * (x + 0.044715 * x * x * x))
    return 0.5 * x * (1.0 + t)


def _gelu_and_grad(x):
    c = math.sqrt(2.0 / math.pi)
    t = jnp.tanh(c * (x + 0.044715 * x * x * x))
    g = 0.5 * x * (1.0 + t)
    dg = 0.5 * (1.0 + t) + 0.5 * x * (1.0 - t * t) * c * (1.0 + 3.0 * 0.044715 * x * x)
    return g, dg


def _neg_expm1(z):
    series = -z * (1.0 + z * (0.5 + z * ((1.0 / 6.0) + z * (1.0 / 24.0))))
    return jnp.where(z > -0.03, series, 1.0 - jnp.exp(z))


def _lru_gates(u, wrg, brg, wig, big, lam):
    ub = u.astype(BF16)
    r = jax.nn.sigmoid(_dot(ub, wrg) + brg)
    ig = jax.nn.sigmoid(_dot(ub, wig) + big)
    sp = jnp.maximum(-lam, 0.0) + jnp.log1p(jnp.exp(-jnp.abs(lam)))
    la = -LRU_C * r * sp
    a = jnp.exp(la)
    mult = jnp.sqrt(jnp.maximum(_neg_expm1(2.0 * la), 0.0))
    return ub, r, ig, sp, a, mult


def _scan8(a8, b8, hprev):
    row = lax.broadcasted_iota(jnp.int32, a8.shape, 0)
    aa, bb = a8, b8
    for d in (1, 2, 4):
        a_s = pltpu.roll(aa, d, 0)
        b_s = pltpu.roll(bb, d, 0)
        m = row >= d
        bb = jnp.where(m, aa * b_s + bb, bb)
        aa = jnp.where(m, aa * a_s, aa)
    return aa * hprev + bb


def _rscan8(c8, d8, lnext):
    row = lax.broadcasted_iota(jnp.int32, c8.shape, 0)
    cc, dd = c8, d8
    for d in (1, 2, 4):
        c_s = pltpu.roll(cc, 8 - d, 0)
        d_s = pltpu.roll(dd, 8 - d, 0)
        m = row < 8 - d
        dd = jnp.where(m, cc * d_s + dd, dd)
        cc = jnp.where(m, cc * c_s, cc)
    return cc * lnext + dd


def _mesh_pos():
    return lax.axis_index("x"), lax.axis_index("y"), lax.axis_index("c")


def _other_chips(x, y):
    return [(1 - x, y), (x, 1 - y), (1 - x, 1 - y)]


def _no_forward():
    pass


def _ag_full_copies(ins, outs, sems):
    send_sems, recv_sems, loc_sems = sems
    n = len(ins)
    x, y, c = _mesh_pos()
    mine = 2 * x + y
    chips = _other_chips(x, y)

    def remote(k, j, slot):
        px, py = chips[j]
        return pltpu.make_async_remote_copy(
            src_ref=ins[k], dst_ref=outs[k].at[slot], send_sem=send_sems.at[k, j], recv_sem=recv_sems.at[k, j],
            device_id=(px, py, c), device_id_type=MESH_ID)

    def local(k):
        return pltpu.make_async_copy(ins[k], outs[k].at[mine], loc_sems.at[k])

    def start():
        for k in range(n):
            local(k).start()
            for j in range(3):
                remote(k, j, mine).start()

    def wait():
        for k in range(n):
            for j, (px, py) in enumerate(chips):
                remote(k, j, 2 * px + py).wait_recv()
        for k in range(n):
            for j in range(3):
                remote(k, j, mine).wait_send()
            local(k).wait()

    return start, _no_forward, wait


def _ag_copies(ins, outs, sems):
    send_sems, recv_sems, fsend_sems, frecv_sems, loc_sems = sems
    n = len(ins)
    x, y, c = _mesh_pos()
    mine = 2 * x + y
    chips = _other_chips(x, y)

    def half(ref, hc):
        r = ref.shape[0] // 2
        return ref.at[pl.ds(pl.multiple_of(hc * r, 16), r)]

    def ici(k, j, slot):
        px, py = chips[j]
        return pltpu.make_async_remote_copy(
            src_ref=half(ins[k], c), dst_ref=half(outs[k].at[slot], c),
            send_sem=send_sems.at[k, j], recv_sem=recv_sems.at[k, j],
            device_id=(px, py, c), device_id_type=MESH_ID)

    def d2d(k, j, hc):
        px, py = chips[j]
        part = half(outs[k].at[2 * px + py], hc)
        return pltpu.make_async_remote_copy(
            src_ref=part, dst_ref=part, send_sem=fsend_sems.at[k, j], recv_sem=frecv_sems.at[k, j],
            device_id=(x, y, 1 - c), device_id_type=MESH_ID)

    def local(k):
        return pltpu.make_async_copy(ins[k], outs[k].at[mine], loc_sems.at[k])

    def start():
        for k in range(n):
            local(k).start()
            for j in range(3):
                ici(k, j, mine).start()

    def forward():
        for k in range(n):
            for j, (px, py) in enumerate(chips):
                ici(k, j, 2 * px + py).wait_recv()
                d2d(k, j, c).start()

    def wait():
        for k in range(n):
            for j in range(3):
                d2d(k, j, 1 - c).wait_recv()
        for k in range(n):
            for j in range(3):
                d2d(k, j, c).wait_send()
                ici(k, j, mine).wait_send()
            local(k).wait()

    return start, forward, wait


def _rs_copies(ins, outs, sems):
    send_sems, recv_sems = sems
    n = len(ins)
    x, y, c = _mesh_pos()
    chips = _other_chips(x, y)

    def remote(k, j):
        px, py = chips[j]
        return pltpu.make_async_remote_copy(
            src_ref=ins[k].at[2 * px + py], dst_ref=outs[k].at[j],
            send_sem=send_sems.at[k, j], recv_sem=recv_sems.at[k, j],
            device_id=(px, py, c), device_id_type=MESH_ID)

    def start():
        for k in range(n):
            for j in range(3):
                remote(k, j).start()

    def wait():
        for k in range(n):
            for j in range(3):
                remote(k, j).wait_recv()
        for k in range(n):
            for j in range(3):
                remote(k, j).wait_send()

    return start, _no_forward, wait


def _swap_copies(ins, outs, sems):
    send_sems, recv_sems = sems
    x, y, c = _mesh_pos()
    copies = [pltpu.make_async_remote_copy(
        src_ref=ins[k], dst_ref=outs[k], send_sem=send_sems.at[k], recv_sem=recv_sems.at[k],
        device_id=(x, y, 1 - c), device_id_type=MESH_ID) for k in range(len(ins))]

    def start():
        for cp in copies:
            cp.start()

    def wait():
        for cp in copies:
            cp.wait()

    return start, _no_forward, wait


def _comm_plan(groups):
    plan, arrs, shapes, sems = [], [], [], []
    for kind, group in groups:
        k = len(group)
        arrs += group
        per_peer = pltpu.SemaphoreType.DMA((k, 3))
        if kind == "ag":
            shapes += [_sds((N_SHARD,) + w.shape, w.dtype) for w in group]
            gsems = [per_peer] * 4 + [pltpu.SemaphoreType.DMA((k,))]
            maker = _ag_copies
        elif kind == "agf":
            shapes += [_sds((N_SHARD,) + w.shape, w.dtype) for w in group]
            gsems = [per_peer] * 2 + [pltpu.SemaphoreType.DMA((k,))]
            maker = _ag_full_copies
        elif kind == "ag8":
            shapes += [_sds((8,) + g.shape, g.dtype) for g in group]
            gsems = [pltpu.SemaphoreType.DMA((k, 7))] * 2 + [pltpu.SemaphoreType.DMA((k,))]
            maker = _ag8_copies
        elif kind == "rs":
            shapes += [_sds((3,) + g.shape[1:], g.dtype) for g in group]
            gsems = [pltpu.SemaphoreType.DMA((k, 3)), pltpu.SemaphoreType.DMA((k, 3))]
            maker = _rs_copies
        else:
            shapes += [_sds(g.shape, g.dtype) for g in group]
            gsems = [pltpu.SemaphoreType.DMA((k,)), pltpu.SemaphoreType.DMA((k,))]
            maker = _swap_copies
        plan.append((maker, k, len(gsems)))
        sems += gsems
    return plan, arrs, shapes, sems


def _comm_fns(plan, cins, couts, sems):
    fns, a, s = [], 0, 0
    for maker, k, ns in plan:
        fns.append(maker(cins[a:a + k], couts[a:a + k], sems[s:s + ns]))
        a += k
        s += ns

    def start():
        for st, _, _ in fns:
            st()

    def forward():
        for _, fw, _ in fns:
            fw()

    def wait():
        for _, _, wt in fns:
            wt()

    return start, forward, wait


def _call(body, name, grid, in_specs, out_specs, out_shape, scratch, args, sem, comm=None):
    if not comm:
        return pl.pallas_call(body, name=name, grid=grid, in_specs=in_specs, out_specs=out_specs,
                              out_shape=out_shape, scratch_shapes=scratch, compiler_params=_cp(sem))(*args)
    plan, c_arrs, c_shapes, c_sems = _comm_plan(comm)
    k = len(c_arrs)
    n_in, n_out, n_scr = len(in_specs), len(out_specs), len(scratch)
    last = grid[0] - 1
    fwd_step = max(1, (2 * last) // 3)

    def wrapped(*refs):
        ins, cins = refs[:n_in], refs[n_in:n_in + k]
        o0 = n_in + k
        outs, couts = refs[o0:o0 + n_out], refs[o0 + n_out:o0 + n_out + k]
        s0 = o0 + n_out + k
        start, forward, wait = _comm_fns(plan, cins, couts, refs[s0 + n_scr:])
        pl.when(pl.program_id(0) == 0)(start)
        pl.when(pl.program_id(0) == fwd_step)(forward)
        body(*ins, *outs, *refs[s0:s0 + n_scr])
        pl.when(pl.program_id(0) == last)(wait)

    return pl.pallas_call(
        wrapped, name=name, grid=grid, in_specs=list(in_specs) + [_any()] * k,
        out_specs=list(out_specs) + [_any()] * k, out_shape=list(out_shape) + c_shapes,
        scratch_shapes=list(scratch) + c_sems, compiler_params=_cp(sem))(*args, *c_arrs)


def _tail_copies(slots_ref, land_ref, part_refs, sib_refs, send_sems, recv_sems):
    x, y, c = _mesh_pos()
    copies = []
    for j, (px, py) in enumerate(_other_chips(x, y)):
        copies.append(pltpu.make_async_remote_copy(
            src_ref=slots_ref.at[2 * px + py], dst_ref=land_ref.at[j], send_sem=send_sems[j], recv_sem=recv_sems[j],
            device_id=(px, py, c), device_id_type=MESH_ID))
    for k, (p_ref, s_ref) in enumerate(zip(part_refs, sib_refs)):
        copies.append(pltpu.make_async_remote_copy(
            src_ref=p_ref, dst_ref=s_ref, send_sem=send_sems[3 + k], recv_sem=recv_sems[3 + k],
            device_id=(x, y, 1 - c), device_id_type=MESH_ID))
    return copies


def _tail_exchange_start(slots, parts):
    n = len(parts)
    ncp = 3 + n
    hbm = pl.BlockSpec(memory_space=pltpu.HBM)
    sem = pl.BlockSpec(memory_space=pltpu.SEMAPHORE)
    land = lax.empty((3,) + slots.shape[1:], slots.dtype)
    sibs = [lax.empty(q.shape, q.dtype) for q in parts]
    bufs = [pltpu.with_memory_space_constraint(b, pltpu.HBM) for b in [slots, land, *parts, *sibs]]
    nb = len(bufs)

    def body(*refs):
        ins = refs[:nb]
        send_sems, recv_sems = refs[nb:nb + ncp], refs[nb + ncp:nb + 2 * ncp]
        token = refs[-1]
        for cp in _tail_copies(ins[0], ins[1], ins[2:2 + n], ins[2 + n:2 + 2 * n], send_sems, recv_sems):
            cp.start()
        token[...] = jnp.zeros_like(token)

    out = pl.pallas_call(
        body, name="tail_exchange_start",
        out_shape=[pltpu.SemaphoreType.DMA(())] * (2 * ncp) + [pltpu.HBM(b.shape, b.dtype) for b in bufs]
                  + [_sds((8, 128), F32)],
        in_specs=[hbm] * nb, out_specs=[sem] * (2 * ncp) + [hbm] * nb + [pl.BlockSpec(memory_space=pltpu.VMEM)],
        input_output_aliases={i: 2 * ncp + i for i in range(nb)},
        compiler_params=pltpu.CompilerParams(has_side_effects=pltpu.SideEffectType.DATAFLOW_SIDE_EFFECTING),
    )(*bufs)
    return out[:2 * ncp], out[2 * ncp:2 * ncp + nb], out[-1]


def _tail_exchange_wait(sems, bufs, n, after):
    ncp = 3 + n
    nb = len(bufs)
    hbm = pl.BlockSpec(memory_space=pltpu.HBM)
    sem = pl.BlockSpec(memory_space=pltpu.SEMAPHORE)

    def body(*refs):
        ins = refs[:nb]
        send_sems, recv_sems = refs[nb:nb + ncp], refs[nb + ncp:nb + 2 * ncp]
        for cp in _tail_copies(ins[0], ins[1], ins[2:2 + n], ins[2 + n:2 + 2 * n], send_sems, recv_sems):
            cp.wait_send()
            cp.wait_recv()

    out = pl.pallas_call(
        body, name="tail_exchange_wait", out_shape=[pltpu.HBM(b.shape, b.dtype) for b in bufs],
        in_specs=[hbm] * nb + [sem] * (2 * ncp) + [_any()], out_specs=[hbm] * nb,
        input_output_aliases={i: i for i in range(nb)},
        compiler_params=pltpu.CompilerParams(has_side_effects=pltpu.SideEffectType.DATAFLOW_SIDE_EFFECTING),
    )(*bufs, *sems, after)
    return out[1], out[2 + n:2 + 2 * n]


def _comm_only(name, comm):
    plan, c_arrs, c_shapes, c_sems = _comm_plan(comm)
    k = len(c_arrs)

    def body(*refs):
        start, forward, wait = _comm_fns(plan, refs[:k], refs[k:2 * k], refs[2 * k:])
        start()
        forward()
        wait()

    return pl.pallas_call(body, name=name, in_specs=[_any()] * k, out_specs=[_any()] * k, out_shape=c_shapes,
                          scratch_shapes=c_sems, compiler_params=_cp())(*c_arrs)


def _any():
    return pl.BlockSpec(memory_space=pl.ANY)


def _load_w_in_once(w_hbm, w_ref):
    @pl.when(pl.program_id(0) == 0)
    def _():
        for s in range(N_SHARD):
            pltpu.sync_copy(w_hbm.at[s], w_ref.at[:, pl.ds(s * IN_SH, IN_SH)])


def _f_inproj(x, g_mix, w_in_g, tm, comm=None):
    s_len = x.shape[0]
    pad_rows = LEFT_CHUNKS * CHUNK
    npad = pad_rows // tm

    def body(x_ref, g_ref, w_hbm, h_ref, qkv_ref, xg_ref, w_ref):
        i = pl.program_id(0)
        _load_w_in_once(w_hbm, w_ref)

        @pl.when(i < npad)
        def _():
            qkv_ref[...] = jnp.zeros_like(qkv_ref)

        @pl.when(i >= npad)
        def _():
            xv = x_ref[...]
            h = (xv * _rinv(xv) * g_ref[...]).astype(BF16)
            h_ref[...] = h
            proj = _dot(h, w_ref[...])
            qkv_ref[:, 0:D_ATT] = (proj[:, 0:D_ATT] * ATT_SCALE).astype(BF16)
            qkv_ref[:, D_ATT:3 * D_ATT] = proj[:, D_ATT:3 * D_ATT].astype(BF16)
            xg_ref[...] = proj[:, 3 * D_ATT:D_IN]

    def tok(n):
        return pl.BlockSpec((tm, n), lambda i: (jnp.maximum(i - npad, 0), 0))

    return _call(
        body, "f_inproj", (s_len // tm + npad,),
        [tok(1024), _full((1, 1024)), _any()],
        [tok(1024), _rows(tm, 1536), tok(1024)],
        [_sds((s_len, 1024), BF16), _sds((s_len + pad_rows, 1536), BF16), _sds((s_len, 1024), F32)],
        [pltpu.VMEM((1024, D_IN), BF16)], (x, g_mix, w_in_g), "arbitrary", comm)


N_BIAS = 3


def _bias_table(frow_ref, bias_sc):
    qa = lax.broadcasted_iota(jnp.int32, (QB, KB), 0) // CHUNK
    kcol = lax.broadcasted_iota(jnp.int32, (QB, KB), 1)
    kb = kcol // CHUNK
    band = jnp.where((kb >= qa) & (kb - qa <= LEFT_CHUNKS), 0.0, NEG).astype(F32)
    for h in range(ATT_HEADS):
        row = jnp.broadcast_to(frow_ref[h:h + 1, :], (QB, ROLL_W))
        toep = pltpu.roll(row, 0, 1, stride=1, stride_axis=0)
        gen = toep[:, 0:KB] + band
        bias_sc[N_BIAS - 1, h] = gen
        for v in range(N_BIAS - 1):
            pad_keys = LEFT_CHUNKS * CHUNK - v * QB
            bias_sc[v, h] = gen + jnp.where(kcol < pad_keys, NEG, 0.0).astype(F32)


def _even_lanes():
    return lax.broadcasted_iota(jnp.int32, (1, 2 * HEAD_DIM), 1) < HEAD_DIM


def _att_probs(qm, kts, bias):
    s = jnp.concatenate([_dot_nt(qm, k) for k in kts], axis=1) + bias
    return jnp.exp(s - jnp.max(s, axis=-1, keepdims=True))


def _att_in_specs(clamp):
    def spec(j, col):
        return pl.BlockSpec((QB, D_ATT), lambda i: (clamp(i) + j, col))
    return [spec(2, 0), spec(0, 1), spec(1, 1), spec(2, 1), spec(0, 2), spec(1, 2), spec(2, 2)]


def _f_attn(qkv_pad, frow, comm=None):
    s_len = qkv_pad.shape[0] - LEFT_CHUNKS * CHUNK
    nb = s_len // QB

    def body(q_ref, k0, k1, k2, v0, v1, v2, frow_ref, o_ref, bias_sc):
        i = pl.program_id(0)

        @pl.when(i == 0)
        def _():
            _bias_table(frow_ref, bias_sc)

        var = jnp.minimum(i, N_BIAS - 1)
        even = _even_lanes()
        for hp in range(ATT_HEADS // 2):
            cs = slice(hp * 2 * HEAD_DIM, (hp + 1) * 2 * HEAD_DIM)
            qt = q_ref[:, cs]
            kts = [k0[:, cs], k1[:, cs], k2[:, cs]]
            vts = [v0[:, cs], v1[:, cs], v2[:, cs]]
            res = []
            for e in range(2):
                keep = even if e == 0 else jnp.logical_not(even)
                pb = _att_probs(jnp.where(keep, qt, 0), kts, bias_sc[var, 2 * hp + e]).astype(BF16)
                r = _dot(pb, jnp.concatenate([jnp.where(keep, v, 1) for v in vts], axis=0))
                res.append(r / pltpu.roll(r, HEAD_DIM, 1))
            o_ref[:, cs] = jnp.where(even, res[0], res[1])

    return _call(
        body, "f_attn", (nb,),
        _att_in_specs(lambda i: i) + [_full((ATT_HEADS, ROLL_W))],
        [_rows(QB, D_ATT)], [_sds((s_len, D_ATT), F32)],
        [pltpu.VMEM((N_BIAS, ATT_HEADS, QB, KB), F32)], (*([qkv_pad] * 7), frow), "arbitrary", comm)


def _f_lru(xg, conv_w, conv_b, wrg, brg, wig, big, lam, tl, comm=None):
    s_len = xg.shape[0]

    def body(xg_ref, cw_ref, cb_ref, wrg_ref, brg_ref, wig_ref, big_ref, l_ref,
             rec_ref, u_ref, hs_ref, xbuf, a_sc, b_sc, hcar):
        i = pl.program_id(0)

        @pl.when(i == 0)
        def _():
            xbuf[0:8, :] = jnp.zeros((8, D_LRU), F32)
            hcar[...] = jnp.zeros((8, D_LRU), F32)

        xu0 = xg_ref[:, 0:D_LRU]
        xbuf[8:8 + tl, :] = xu0
        u = cb_ref[...] + cw_ref[0:1, :] * xbuf[pl.ds(5, tl), :]
        for j in range(1, 4):
            u = u + cw_ref[j:j + 1, :] * xbuf[pl.ds(5 + j, tl), :]
        xbuf[0:8, :] = xu0[tl - 8:tl, :]
        u_ref[...] = u
        _, _, ig, _, a, mult = _lru_gates(u, wrg_ref[...], brg_ref[...], wig_ref[...], big_ref[...], l_ref[...])
        a_sc[...] = a
        b_sc[...] = mult * (ig * u)

        def grp(g, hprev):
            off = pl.multiple_of(g * 8, 8)
            h8 = _scan8(a_sc[pl.ds(off, 8), :], b_sc[pl.ds(off, 8), :], hprev)
            hs_ref[pl.ds(off, 8), :] = h8
            return h8[7:8, :]

        hcar[0:1, :] = lax.fori_loop(0, tl // 8, grp, hcar[0:1, :])
        rec_ref[...] = hs_ref[...] * _gelu(xg_ref[:, D_LRU:2 * D_LRU])

    vec = _full((1, D_LRU))
    return _call(
        body, "f_lru", (s_len // tl,),
        [_rows(tl, 1024), _full((4, D_LRU)), vec, _full((D_LRU, D_LRU)), vec, _full((D_LRU, D_LRU)), vec, vec],
        [_rows(tl, D_LRU)] * 3, [_sds((s_len, D_LRU), F32)] * 3,
        [pltpu.VMEM((tl + 8, D_LRU), F32), pltpu.VMEM((tl, D_LRU), F32),
         pltpu.VMEM((tl, D_LRU), F32), pltpu.VMEM((8, D_LRU), F32)],
        (xg, conv_w, conv_b, wrg, brg, wig, big, lam), "arbitrary", comm)


def _f_mem(mem, g_mem, wk, wv):
    def body(mem_ref, g_ref, wk_ref, wv_ref, mn_ref, kx_ref, vx_ref):
        mv = mem_ref[...]
        mn = (mv * _rinv(mv) * g_ref[...]).astype(BF16)
        mn_ref[...] = mn
        kx_ref[...] = _dot(mn, wk_ref[...]).astype(BF16)
        vx_ref[...] = _dot(mn, wv_ref[...]).astype(BF16)

    m = mem.shape[0]
    return pl.pallas_call(
        body, name="f_mem", out_shape=[_sds((m, 1024), BF16)] * 3,
        compiler_params=_cp())(mem, g_mem, wk, wv)


def _xattn_probs(q, k):
    s = _dot_nt(q, k) * X_SCALE
    m = jnp.max(s, axis=-1, keepdims=True)
    p = jnp.exp(s - m)
    return p, jnp.sum(p, axis=-1, keepdims=True)


def _f_mid(x, att, rec, g_oa, g_ol, w_out, g_cross, wq, kx, vx, wo, tm, comm=None):
    s_len = x.shape[0]
    m_len = kx.shape[0]

    def body(x_ref, att_ref, rec_ref, goa_ref, gol_ref, wout_ref, gc_ref, wq_ref, kx_ref, vx_ref, wo_ref,
             mg_ref, x1_ref, hc_ref, qx_ref, ox_ref, x2_ref):
        av = att_ref[...]
        rv = rec_ref[...]
        mg_ref[:, 0:D_ATT] = (av * _rinv(av) * goa_ref[...]).astype(BF16)
        mg_ref[:, D_ATT:1024] = (rv * _rinv(rv) * gol_ref[...]).astype(BF16)
        x1 = x_ref[...] + _dot(mg_ref[...], wout_ref[...])
        x1_ref[...] = x1
        hc = (x1 * _rinv(x1) * gc_ref[...]).astype(BF16)
        hc_ref[...] = hc
        qx_ref[...] = _dot(hc, wq_ref[...]).astype(BF16)
        for h in range(X_HEADS):
            sl = slice(h * X_HEAD_DIM, (h + 1) * X_HEAD_DIM)
            p, l = _xattn_probs(qx_ref[:, sl], kx_ref[:, sl])
            ox_ref[:, sl] = (_dot(p.astype(BF16), vx_ref[:, sl]) / l).astype(BF16)
        x2_ref[...] = x1 + _dot(ox_ref[...], wo_ref[...])

    sq = _full((1024, 1024))
    return _call(
        body, "f_mid", (s_len // tm,),
        [_rows(tm, 1024), _rows(tm, 512), _rows(tm, 512), _full((1, 512)), _full((1, 512)), sq,
         _full((1, 1024)), sq, _full((m_len, 1024)), _full((m_len, 1024)), sq],
        [_rows(tm, 1024)] * 6,
        [_sds((s_len, 1024), BF16), _sds((s_len, 1024), F32), _sds((s_len, 1024), BF16),
         _sds((s_len, 1024), BF16), _sds((s_len, 1024), BF16), _sds((s_len, 1024), F32)],
        [], (x, att, rec, g_oa, g_ol, w_out, g_cross, wq, kx, vx, wo), "arbitrary", comm)


def _load_weights_once(pairs):
    @pl.when(pl.program_id(0) == 0)
    def _():
        for hbm, vmem in pairs:
            pltpu.sync_copy(hbm, vmem)


FF_CHUNKS = [(0, 1280), (1280, D_FF)]


def _f_ffn(x2, tgt, g_ffn, g_final, wg, wu, wd, tm):
    s_len = x2.shape[0]

    def body(x2_ref, t_ref, gf_ref, gfin_ref, wg_hbm, wu_hbm, wd_hbm,
             hf_ref, g_ref, u_ref, a_ref, dx3_ref, loss_ref, dgfin_ref, wg_ref, wu_ref, wd_ref):
        _load_weights_once([(wg_hbm, wg_ref), (wu_hbm, wu_ref), (wd_hbm, wd_ref)])

        @pl.when(pl.program_id(0) == 0)
        def _():
            loss_ref[...] = jnp.zeros_like(loss_ref)
            dgfin_ref[...] = jnp.zeros_like(dgfin_ref)

        x2v = x2_ref[...]
        hf = (x2v * _rinv(x2v) * gf_ref[...]).astype(BF16)
        hf_ref[...] = hf
        x3 = x2v
        for c0, c1 in FF_CHUNKS:
            gv = _dot_nt(hf, wg_ref[c0:c1, :])
            uv = _dot_nt(hf, wu_ref[c0:c1, :])
            av = (gv * jax.nn.sigmoid(gv) * uv).astype(BF16)
            g_ref[:, c0:c1] = gv.astype(BF16)
            u_ref[:, c0:c1] = uv.astype(BF16)
            a_ref[:, c0:c1] = av
            x3 = x3 + _dot(av, wd_ref[c0:c1, :])
        r3 = _rinv(x3)
        yh = x3 * r3
        gfin = gfin_ref[...]
        err = yh * gfin - t_ref[...]
        loss_ref[...] += jnp.full((1, 128), 0.5 / D_MODEL, F32) * jnp.sum(err * err)
        dy = err * (1.0 / D_MODEL)
        dgfin_ref[...] += jnp.sum(dy * yh, axis=0, keepdims=True)
        dyh = dy * gfin
        dx3_ref[...] = r3 * (dyh - yh * jnp.mean(dyh * yh, axis=-1, keepdims=True))

    vec = _full((1, 1024))
    return pl.pallas_call(
        body, name="f_ffn", grid=(s_len // tm,),
        in_specs=[_rows(tm, 1024), _rows(tm, 1024), vec, vec, _any(), _any(), _any()],
        out_specs=[_rows(tm, 1024), _rows(tm, D_FF), _rows(tm, D_FF), _rows(tm, D_FF),
                   _rows(tm, 1024), _full((1, 128)), vec],
        out_shape=[_sds((s_len, 1024), BF16)] + [_sds((s_len, D_FF), BF16)] * 3
                  + [_sds((s_len, 1024), F32), _sds((1, 128), F32), _sds((1, 1024), F32)],
        scratch_shapes=[pltpu.VMEM((D_FF, 1024), BF16)] * 3,
        compiler_params=_cp("arbitrary"))(x2, tgt, g_ffn, g_final, wg, wu, wd)


def _b_ffn(dx3, x2, gact, uact, g_ffn, wg, wu, wd, tm):
    s_len = x2.shape[0]

    def body(dx3_ref, x2_ref, g_ref, u_ref, gf_ref, wg_hbm, wu_hbm, wd_hbm,
             dg_ref, du_ref, dx2_ref, dgf_ref, wg_ref, wu_ref, wd_ref):
        _load_weights_once([(wg_hbm, wg_ref), (wu_hbm, wu_ref), (wd_hbm, wd_ref)])

        @pl.when(pl.program_id(0) == 0)
        def _():
            dgf_ref[...] = jnp.zeros_like(dgf_ref)

        dx3v = dx3_ref[...]
        dx3b = dx3v.astype(BF16)
        dhf = jnp.zeros(dx3v.shape, F32)
        for c0, c1 in FF_CHUNKS:
            da = _dot_nt(dx3b, wd_ref[c0:c1, :])
            gv = g_ref[:, c0:c1].astype(F32)
            uv = u_ref[:, c0:c1].astype(F32)
            sg = jax.nn.sigmoid(gv)
            dub = (da * gv * sg).astype(BF16)
            dgb = (da * uv * (sg * (1.0 + gv * (1.0 - sg)))).astype(BF16)
            du_ref[:, c0:c1] = dub
            dg_ref[:, c0:c1] = dgb
            dhf = dhf + _dot(dgb, wg_ref[c0:c1, :]) + _dot(dub, wu_ref[c0:c1, :])
        dx, dgf = _rms_bwd(dhf, x2_ref[...], gf_ref[...])
        dx2_ref[...] = dx3v + dx
        dgf_ref[...] += dgf

    vec = _full((1, 1024))
    return pl.pallas_call(
        body, name="b_ffn", grid=(s_len // tm,),
        in_specs=[_rows(tm, 1024), _rows(tm, 1024), _rows(tm, D_FF), _rows(tm, D_FF), vec,
                  _any(), _any(), _any()],
        out_specs=[_rows(tm, D_FF), _rows(tm, D_FF), _rows(tm, 1024), vec],
        out_shape=[_sds((s_len, D_FF), BF16)] * 2 + [_sds((s_len, 1024), F32), _sds((1, 1024), F32)],
        scratch_shapes=[pltpu.VMEM((D_FF, 1024), BF16)] * 3,
        compiler_params=_cp("arbitrary"))(dx3, x2, gact, uact, g_ffn, wg, wu, wd)


def _b_mid(dx2, qx, x1, att, rec, kx, vx, wo, wq, w_out, g_cross, g_oa, g_ol, tm, comm=None):
    s_len = x1.shape[0]
    m_len = kx.shape[0]

    def body(dx2_ref, qx_ref, x1_ref, att_ref, rec_ref, kx_ref, vx_ref, wo_ref, wq_ref, wout_ref,
             gc_ref, goa_ref, gol_ref,
             dqx_ref, dx1_ref, datt_ref, drec_ref, dkx_ref, dvx_ref, dgc_ref, dgoa_ref, dgol_ref):
        @pl.when(pl.program_id(0) == 0)
        def _():
            for r in (dkx_ref, dvx_ref, dgc_ref, dgoa_ref, dgol_ref):
                r[...] = jnp.zeros_like(r)

        dx2v = dx2_ref[...]
        dox = _dot_nt(dx2v.astype(BF16), wo_ref[...])
        for h in range(X_HEADS):
            sl = slice(h * X_HEAD_DIM, (h + 1) * X_HEAD_DIM)
            q = qx_ref[:, sl]
            p, l = _xattn_probs(q, kx_ref[:, sl])
            pn = p * (1.0 / l)
            dob = dox[:, sl].astype(BF16)
            dp = _dot_nt(dob, vx_ref[:, sl])
            dvx_ref[:, sl] += _dot_tn(pn.astype(BF16), dob)
            ds = pn * (dp - jnp.sum(dp * pn, axis=-1, keepdims=True))
            dsb = (ds * X_SCALE).astype(BF16)
            dqx_ref[:, sl] = _dot(dsb, kx_ref[:, sl]).astype(BF16)
            dkx_ref[:, sl] += _dot_tn(dsb, q)
        dhc = _dot_nt(dqx_ref[...], wq_ref[...])
        dx, dgc = _rms_bwd(dhc, x1_ref[...], gc_ref[...])
        dx1 = dx2v + dx
        dx1_ref[...] = dx1
        dgc_ref[...] += dgc
        dmg = _dot_nt(dx1.astype(BF16), wout_ref[...])
        da, dgoa = _rms_bwd(dmg[:, 0:D_ATT], att_ref[...], goa_ref[...])
        datt_ref[...] = da
        dgoa_ref[...] += dgoa
        dr, dgol = _rms_bwd(dmg[:, D_ATT:1024], rec_ref[...], gol_ref[...])
        drec_ref[...] = dr
        dgol_ref[...] += dgol

    sq = _full((1024, 1024))
    mk = _full((m_len, 1024))
    return _call(
        body, "b_mid", (s_len // tm,),
        [_rows(tm, 1024), _rows(tm, 1024), _rows(tm, 1024), _rows(tm, 512), _rows(tm, 512), mk, mk,
         sq, sq, sq, _full((1, 1024)), _full((1, 512)), _full((1, 512))],
        [_rows(tm, 1024), _rows(tm, 1024), _rows(tm, 512), _rows(tm, 512), mk, mk,
         _full((1, 1024)), _full((1, 512)), _full((1, 512))],
        [_sds((s_len, 1024), BF16), _sds((s_len, 1024), F32), _sds((s_len, 512), F32),
         _sds((s_len, 512), F32), _sds((m_len, 1024), F32), _sds((m_len, 1024), F32),
         _sds((1, 1024), F32), _sds((1, 512), F32), _sds((1, 512), F32)],
        [], (dx2, qx, x1, att, rec, kx, vx, wo, wq, w_out, g_cross, g_oa, g_ol), "arbitrary", comm)


def _b_mem(dkx, dvx, mem, mn, g_mem, wk, wv):
    def body(dkx_ref, dvx_ref, mem_ref, mn_ref, g_ref, wk_ref, wv_ref, dwk_ref, dwv_ref, dgm_ref,
             dwkb_ref, dwvb_ref):
        dkb = dkx_ref[...].astype(BF16)
        dvb = dvx_ref[...].astype(BF16)
        dwk = _dot_tn(mn_ref[...], dkb)
        dwv = _dot_tn(mn_ref[...], dvb)
        dwk_ref[...] = dwk
        dwv_ref[...] = dwv
        dwkb_ref[...] = dwk.astype(BF16)
        dwvb_ref[...] = dwv.astype(BF16)
        dmn = _dot_nt(dkb, wk_ref[...]) + _dot_nt(dvb, wv_ref[...])
        mv = mem_ref[...]
        dgm_ref[...] = jnp.sum(dmn * (mv * _rinv(mv)), axis=0, keepdims=True)

    return pl.pallas_call(
        body, name="b_mem",
        out_shape=[_sds((1024, 1024), F32), _sds((1024, 1024), F32), _sds((1, 1024), F32),
                   _sds((1024, 1024), BF16), _sds((1024, 1024), BF16)],
        compiler_params=_cp())(dkx, dvx, mem, mn, g_mem, wk, wv)


def _b_lru(drec, hs, u, xg, conv_w, wrg, brg, wig, big, lam, tl, comm=None):
    s_len = xg.shape[0]
    nt = s_len // tl

    def body(drec_ref, hs_ref, hsp_ref, u_ref, xg_ref, cw_ref, wrg_ref, brg_ref, wig_ref, big_ref, l_ref,
             dxg_ref, dwrg_ref, dwig_ref, dbrg_ref, dbig_ref, dlam_ref, dcw_ref, dcb_ref,
             hbuf, abuf, dubuf, c_sc, d_sc, lam_sc, lcar, wacc_r, wacc_i):
        i = pl.program_id(0)
        tt = nt - 1 - i

        @pl.when(i == 0)
        def _():
            for r in (wacc_r, wacc_i, dbrg_ref, dbig_ref, dlam_ref, dcw_ref, dcb_ref):
                r[...] = jnp.zeros_like(r)
            abuf[tl:tl + 8, :] = jnp.zeros((8, D_LRU), F32)
            dubuf[tl:tl + 8, :] = jnp.zeros((8, D_LRU), F32)
            lcar[...] = jnp.zeros((8, D_LRU), F32)

        xu0 = xg_ref[:, 0:D_LRU]
        hsv = hs_ref[...]
        uv = u_ref[...]
        hbuf[8:8 + tl, :] = hsv
        hbuf[0:8, :] = jnp.where(tt > 0, hsp_ref[...], 0.0)
        hshift = hbuf[pl.ds(7, tl), :]
        wrg_v = wrg_ref[...]
        wig_v = wig_ref[...]
        lamv = l_ref[...]
        ub, r, ig, sp, a, mult = _lru_gates(uv, wrg_v, brg_ref[...], wig_v, big_ref[...], lamv)
        abuf[0:tl, :] = a
        c_sc[...] = abuf[pl.ds(1, tl), :]
        gel, dgel = _gelu_and_grad(xg_ref[:, D_LRU:2 * D_LRU])
        drv = drec_ref[...]
        d_sc[...] = drv * gel
        dxg_ref[:, D_LRU:2 * D_LRU] = (drv * hsv * dgel).astype(BF16)

        def grp(k, lnext):
            off = pl.multiple_of((tl // 8 - 1 - k) * 8, 8)
            l8 = _rscan8(c_sc[pl.ds(off, 8), :], d_sc[pl.ds(off, 8), :], lnext)
            lam_sc[pl.ds(off, 8), :] = l8
            return l8[0:1, :]

        lcar[0:1, :] = lax.fori_loop(0, tl // 8, grp, lcar[0:1, :])
        abuf[tl:tl + 8, :] = a[0:8, :]
        db = lam_sc[...]
        da = db * hshift
        dmult = db * (ig * uv)
        dig = db * mult * uv
        du = db * mult * ig
        dla = da * a - dmult * (a * a) / mult
        dlam_ref[...] += jnp.sum(dla * (-LRU_C) * r, axis=0, keepdims=True)
        dzr = dla * (-LRU_C * sp) * r * (1.0 - r)
        dzi = dig * ig * (1.0 - ig)
        dzrb = dzr.astype(BF16)
        dzib = dzi.astype(BF16)
        du = du + _dot_nt(dzrb, wrg_v) + _dot_nt(dzib, wig_v)
        wacc_r[...] += _dot_tn(ub, dzrb)
        wacc_i[...] += _dot_tn(ub, dzib)
        dbrg_ref[...] += jnp.sum(dzr, axis=0, keepdims=True)
        dbig_ref[...] += jnp.sum(dzi, axis=0, keepdims=True)
        dcb_ref[...] += jnp.sum(du, axis=0, keepdims=True)
        dubuf[0:tl, :] = du
        dxu0 = jnp.zeros((tl, D_LRU), F32)
        for j in range(4):
            dsh = dubuf[pl.ds(3 - j, tl), :]
            dxu0 = dxu0 + cw_ref[j:j + 1, :] * dsh
            dcw_ref[j:j + 1, :] += jnp.sum(xu0 * dsh, axis=0, keepdims=True)
        dubuf[tl:tl + 8, :] = du[0:8, :]
        dxg_ref[:, 0:D_LRU] = dxu0.astype(BF16)

        @pl.when(i == nt - 1)
        def _():
            dlam_ref[...] = dlam_ref[...] * (-jax.nn.sigmoid(-lamv))
            for n in range(LRU_BLOCKS):
                blk = slice(n * LRU_BLOCK, (n + 1) * LRU_BLOCK)
                dwrg_ref[n] = wacc_r[blk, blk]
                dwig_ref[n] = wacc_i[blk, blk]

    def rev(n):
        return pl.BlockSpec((tl, n), lambda i: (nt - 1 - i, 0))

    prev8 = pl.BlockSpec((8, D_LRU), lambda i: (jnp.maximum((nt - 1 - i) * (tl // 8) - 1, 0), 0))
    vec = _full((1, D_LRU))
    sq = _full((D_LRU, D_LRU))
    blocks_shape = (LRU_BLOCKS, LRU_BLOCK, LRU_BLOCK)
    blocks = _full(blocks_shape)
    return _call(
        body, "b_lru", (nt,),
        [rev(D_LRU), rev(D_LRU), prev8, rev(D_LRU), rev(1024), _full((4, D_LRU)), sq, vec, sq, vec, vec],
        [rev(1024), blocks, blocks, vec, vec, vec, _full((4, D_LRU)), vec],
        [_sds((s_len, 1024), BF16), _sds(blocks_shape, F32), _sds(blocks_shape, F32),
         _sds((1, D_LRU), F32), _sds((1, D_LRU), F32), _sds((1, D_LRU), F32),
         _sds((4, D_LRU), F32), _sds((1, D_LRU), F32)],
        [pltpu.VMEM((tl + 8, D_LRU), F32)] * 3 + [pltpu.VMEM((tl, D_LRU), F32)] * 3
        + [pltpu.VMEM((8, D_LRU), F32)] + [pltpu.VMEM((D_LRU, D_LRU), F32)] * 2,
        (drec, hs, hs, u, xg, conv_w, wrg, brg, wig, big, lam), "arbitrary", comm)


def _b_attn(qkv_pad, att, datt, frow, comm=None):
    s_len = datt.shape[0]
    nb = s_len // QB
    n_pair = ATT_HEADS // 2
    pair_w = 2 * HEAD_DIM

    def body(q_ref, k0, k1, k2, v0, v1, v2, o_ref, do_ref, frow_ref, dq_ref, dkv_ref, dfrow_ref,
             bias_sc, dt_sc, acc_sc):
        t = pl.program_id(0)

        @pl.when(t == 0)
        def _():
            _bias_table(frow_ref, bias_sc)
            dt_sc[...] = jnp.zeros_like(dt_sc)
            acc_sc[...] = jnp.zeros_like(acc_sc)

        @pl.when(t < nb)
        def _():
            var = jnp.minimum(t, N_BIAS - 1)
            even = _even_lanes()
            for hp in range(n_pair):
                cs = slice(hp * pair_w, (hp + 1) * pair_w)
                qt = q_ref[:, cs]
                kts = [k0[:, cs], k1[:, cs], k2[:, cs]]
                vts = [v0[:, cs], v1[:, cs], v2[:, cs]]
                kcat = jnp.concatenate(kts, axis=0)
                dot = do_ref[:, cs]
                dd = dot * o_ref[:, cs]
                dos_pair, dsbs, pbs, dqs = None, [], [], []
                for e in range(2):
                    keep = even if e == 0 else jnp.logical_not(even)
                    qm = jnp.where(keep, qt, 0)
                    p = _att_probs(qm, kts, bias_sc[var, 2 * hp + e])
                    inv = 1.0 / jnp.sum(p, axis=-1, keepdims=True)
                    dos = jnp.where(keep, dot * inv, 0.0)
                    delta = jnp.sum(jnp.where(keep, dd, 0.0), axis=-1, keepdims=True) * inv
                    dp = jnp.concatenate([_dot_nt(dos.astype(BF16), v) for v in vts], axis=1)
                    ds = p * (dp - delta)
                    dt_sc[2 * hp + e] += ds
                    dsb = ds.astype(BF16)
                    dq = _dot(dsb, kcat)
                    dqs.append(dq)
                    dsbs.append(dsb)
                    pbs.append(p.astype(BF16))
                    dos_pair = dos if e == 0 else dos_pair + dos
                dq_ref[:, cs] = (jnp.where(even, dqs[0], dqs[1]) * ATT_SCALE).astype(BF16)
                qtt = qt.astype(F32).T.astype(BF16)
                dost = dos_pair.T.astype(BF16)
                for j in range(3):
                    slot = (t + 1 + j) % 3
                    js = slice(j * QB, (j + 1) * QB)
                    for e in range(2):
                        hr = slice(e * HEAD_DIM, (e + 1) * HEAD_DIM)
                        acc_sc[slot, hp, hr, :] += _dot(qtt[hr], dsbs[e][:, js])
                        acc_sc[slot, n_pair + hp, hr, :] += _dot(dost[hr], pbs[e][:, js])

        done = (t + 1) % 3

        @pl.when(t >= 2)
        def _():
            for i in range(2 * n_pair):
                dkv_ref[:, i * pair_w:(i + 1) * pair_w] = acc_sc[done, i].T.astype(BF16)

        acc_sc[done] = jnp.zeros((2 * n_pair, pair_w, QB), F32)

        @pl.when(t == nb + 1)
        def _():
            row = lax.broadcasted_iota(jnp.int32, (8, ROLL_W), 0)
            pad = jnp.zeros((8, ROLL_W - KB), F32)
            for h in range(ATT_HEADS):
                acc8 = jnp.concatenate([dt_sc[h, 0:8, :], pad], axis=1)
                for a1 in range(1, QB // 8):
                    blk = jnp.concatenate([dt_sc[h, 8 * a1:8 * a1 + 8, :], pad], axis=1)
                    acc8 = acc8 + pltpu.roll(blk, ROLL_W - 8 * a1, 1)
                for k in range(3):
                    acc8 = jnp.where(((row >> k) & 1) == 1, pltpu.roll(acc8, ROLL_W - (1 << k), 1), acc8)
                dfrow_ref[h:h + 1, :] = jnp.sum(acc8, axis=0, keepdims=True)

    clamp = lambda t: jnp.minimum(t, nb - 1)
    qrows = pl.BlockSpec((QB, D_ATT), lambda t: (clamp(t), 0))
    return _call(
        body, "b_attn", (nb + 2,),
        _att_in_specs(clamp) + [qrows, qrows, _full((ATT_HEADS, ROLL_W))],
        [qrows, pl.BlockSpec((QB, 2 * D_ATT), lambda t: (jnp.maximum(t - 2, 0), 0)),
         _full((ATT_HEADS, ROLL_W))],
        [_sds((s_len, D_ATT), BF16), _sds((s_len, 2 * D_ATT), BF16), _sds((ATT_HEADS, ROLL_W), F32)],
        [pltpu.VMEM((N_BIAS, ATT_HEADS, QB, KB), F32), pltpu.VMEM((ATT_HEADS, QB, KB), F32),
         pltpu.VMEM((3, 2 * n_pair, pair_w, QB), F32)],
        (*([qkv_pad] * 7), att, datt, frow), "arbitrary", comm)


def _flush_grad(steps, acc, accb, out_hbm, outb_hbm):
    @pl.when(pl.program_id(0) == steps - 1)
    def _():
        accb[...] = acc[...].astype(BF16)
        pltpu.sync_copy(acc, out_hbm)
        pltpu.sync_copy(accb, outb_hbm)


def _b_win(dq, dkv, dxg, h, ts):
    s_len = h.shape[0]
    steps = s_len // ts

    def body(dq_ref, dkv_ref, dxg_ref, h_ref, dw_hbm, dwb_hbm, acc, accb):
        @pl.when(pl.program_id(0) == 0)
        def _():
            acc[...] = jnp.zeros_like(acc)

        dproj = jnp.concatenate([dq_ref[...], dkv_ref[...], dxg_ref[...]], axis=1)
        acc[...] += _dot_tn(h_ref[...], dproj)

        @pl.when(pl.program_id(0) == steps - 1)
        def _():
            accb[...] = acc[...].astype(BF16)
            for s in range(N_SHARD):
                cols = pl.ds(s * IN_SH, IN_SH)
                pltpu.sync_copy(acc.at[:, cols], dw_hbm.at[s])
                pltpu.sync_copy(accb.at[:, cols], dwb_hbm.at[s])

    shape = (N_SHARD, 1024, IN_SH)
    return pl.pallas_call(
        body, name="b_win", grid=(steps,),
        in_specs=[_rows(ts, 512), _rows(ts, 1024), _rows(ts, 1024), _rows(ts, 1024)],
        out_specs=[_any()] * 2, out_shape=[_sds(shape, F32), _sds(shape, BF16)],
        scratch_shapes=[pltpu.VMEM((1024, D_IN), F32), pltpu.VMEM((1024, D_IN), BF16)],
        compiler_params=_cp("arbitrary"))(dq, dkv, dxg, h)


def _b_inproj(dq, dkv, dxg, x, dx1, g_mix, w_in_g, tm, comm=None):
    s_len = x.shape[0]

    def body(dq_ref, dkv_ref, dxg_ref, x_ref, dx1_ref, g_ref, w_hbm, gx_ref, dgm_ref, w_ref):
        _load_w_in_once(w_hbm, w_ref)

        @pl.when(pl.program_id(0) == 0)
        def _():
            dgm_ref[...] = jnp.zeros_like(dgm_ref)

        dproj = jnp.concatenate([dq_ref[...], dkv_ref[...], dxg_ref[...]], axis=1)
        dh = _dot_nt(dproj, w_ref[...])
        dx, dgm = _rms_bwd(dh, x_ref[...], g_ref[...])
        gx_ref[...] = dx1_ref[...] + dx
        dgm_ref[...] += dgm

    return _call(
        body, "b_inproj", (s_len // tm,),
        [_rows(tm, 512), _rows(tm, 1024), _rows(tm, 1024), _rows(tm, 1024), _rows(tm, 1024),
         _full((1, 1024)), _any()],
        [_rows(tm, 1024), _full((1, 1024))],
        [_sds((s_len, 1024), F32), _sds((1, 1024), F32)],
        [pltpu.VMEM((1024, D_IN), BF16)], (dq, dkv, dxg, x, dx1, g_mix, w_in_g), "arbitrary", comm)


def _mm_tn(xa, ya, name, ts):
    s_len, k = xa.shape
    n = ya.shape[1]

    steps = s_len // ts

    def body(x_ref, y_ref, o_hbm, ob_hbm, acc, accb):
        @pl.when(pl.program_id(0) == 0)
        def _():
            acc[...] = jnp.zeros_like(acc)
        acc[...] += _dot_tn(x_ref[...].astype(BF16), y_ref[...].astype(BF16))
        _flush_grad(steps, acc, accb, o_hbm, ob_hbm)

    return pl.pallas_call(
        body, name=name, grid=(steps,), in_specs=[_rows(ts, k), _rows(ts, n)],
        out_specs=[_any()] * 2, out_shape=[_sds((k, n), F32), _sds((k, n), BF16)],
        scratch_shapes=[pltpu.VMEM((k, n), F32), pltpu.VMEM((k, n), BF16)],
        compiler_params=_cp("arbitrary"))(xa, ya)


PAD_KEYS = LEFT_CHUNKS * CHUNK
F_HI = PAD_KEYS - MAX_REL + 1
F_LO = PAD_KEYS + MAX_REL


def _frow_from_rel_bias(rb):
    last = rb[:, 2 * MAX_REL:2 * MAX_REL + 1]
    hi = jnp.broadcast_to(last, (ATT_HEADS, F_HI))
    mid = rb[:, 1:2 * MAX_REL][:, ::-1]
    lo = jnp.broadcast_to(rb[:, 0:1], (ATT_HEADS, KB - F_LO))
    wrap = jnp.broadcast_to(last, (ATT_HEADS, ROLL_W - KB))
    return jnp.concatenate([hi, mid, lo, wrap], axis=1)


def _rel_bias_grad_from_dfrow(df):
    g_last = jnp.sum(df[:, 0:F_HI], axis=1, keepdims=True) + jnp.sum(df[:, KB:ROLL_W], axis=1, keepdims=True)
    mid = df[:, F_HI:F_LO][:, ::-1]
    g_first = jnp.sum(df[:, F_LO:KB], axis=1, keepdims=True)
    return jnp.concatenate([g_first, mid, g_last], axis=1)


def _block_diag(w):
    eye = jnp.eye(8, dtype=w.dtype)
    return (w[:, :, None, :] * eye[:, None, :, None]).reshape(D_LRU, D_LRU)


MID = ['w_out', 'wq_c', 'wk_c', 'wv_c', 'wo_c']
TRANSPOSED = ['w_gate', 'w_up']
AG_IN_INPROJ = ['w_out', 'wq_c', 'wk_c']
AG_IN_ATTN = ['wv_c', 'wo_c', 'w_gate']
AG_IN_LRU = ['w_up']
AG_IN_MID = ['w_down']
RS_IN_MID = ['w_gate', 'w_up']
RS_IN_LRU = ['w_down']
RS_IN_ATTN = MID


def _local_step(x, mem, tgt, p, gw, shards=None, chip=None):
    s_len = x.shape[0]
    tm = min(256, s_len)
    tmb = min(512, s_len)
    tl = min(512, s_len)
    frow = _frow_from_rel_bias(p['rel_bias'])
    wrg = _block_diag(p['w_rg']).astype(BF16)
    wig = _block_diag(p['w_ig']).astype(BF16)
    gw = dict(gw)

    big, bigb, recv, part, sib = {}, {}, {}, {}, {}

    def ag(names):
        return [] if shards is None else [("ag", [shards[n] for n in names])]

    def rs(names):
        return [] if shards is None else [("rs", [bigb[n] for n in names])]

    def swap(names):
        return [] if shards is None else [("swap", [part[n] for n in names])]

    def reduce_own(names):
        if shards is not None:
            sums = _sum_parts([big[n] for n in names], [recv[n] for n in names], chip, "sum_" + names[0])
            part.update(zip(names, sums))

    h, qkv_pad, xg, *got = _f_inproj(x, p['g_mix'], gw['w_in'], tmb, ag(AG_IN_INPROJ))
    gw.update(zip(AG_IN_INPROJ, got))
    att, *got = _f_attn(qkv_pad, frow, ag(AG_IN_ATTN))
    gw.update(zip(AG_IN_ATTN, got))
    rec, u, hs, *got = _f_lru(xg, p['conv_w'], p['conv_b'], wrg, p['b_rg'], wig, p['b_ig'], p['lru_L'], tl,
                              ag(AG_IN_LRU))
    gw.update(zip(AG_IN_LRU, got))
    w_out = gw['w_out'].reshape(1024, 1024)
    wq = gw['wq_c'].reshape(1024, 1024)
    wk = gw['wk_c'].reshape(1024, 1024)
    wv = gw['wv_c'].reshape(1024, 1024)
    wo = gw['wo_c'].reshape(1024, 1024)
    mn, kx, vx = _f_mem(mem, p['g_mem'], wk, wv)
    mg, x1, hc, qx, ox, x2, *got = _f_mid(x, att, rec, p['g_out_attn'], p['g_out_lru'], w_out, p['g_cross'],
                                          wq, kx, vx, wo, tmb, ag(AG_IN_MID))
    gw.update(zip(AG_IN_MID, got))
    ffn_w = [gw[n].reshape(D_FF, 1024) for n in ('w_gate', 'w_up', 'w_down')]
    hf, gact, uact, aact, dx3, loss, dg_final = _f_ffn(x2, tgt, p['g_ffn'], p['g_final'], *ffn_w, tmb)

    ts = min(1024, s_len)
    dgact, duact, dx2, dg_ffn = _b_ffn(dx3, x2, gact, uact, p['g_ffn'], *ffn_w, tm)
    big['w_gate'], bigb['w_gate'] = _mm_tn(dgact, hf, "dw_gate", ts)
    big['w_up'], bigb['w_up'] = _mm_tn(duact, hf, "dw_up", ts)
    big['w_down'], bigb['w_down'] = _mm_tn(aact, dx3, "dw_down", ts)
    for n in ('w_gate', 'w_up', 'w_down'):
        big[n] = big[n].reshape(N_SHARD, FF_SH, 1024)
        bigb[n] = bigb[n].reshape(N_SHARD, FF_SH, 1024)

    dqx, dx1, datt, drec, dkx, dvx, dg_cross, dg_oa, dg_ol, *got = _b_mid(
        dx2, qx, x1, att, rec, kx, vx, wo, wq, w_out, p['g_cross'], p['g_out_attn'], p['g_out_lru'], tmb,
        rs(RS_IN_MID))
    recv.update(zip(RS_IN_MID, got))
    reduce_own(RS_IN_MID)
    dwk, dwv, dg_mem, dwkb, dwvb = _b_mem(dkx, dvx, mem, mn, p['g_mem'], wk, wv)
    big['wk_c'], bigb['wk_c'] = dwk, dwkb
    big['wv_c'], bigb['wv_c'] = dwv, dwvb
    big['w_out'], bigb['w_out'] = _mm_tn(mg, dx1, "dw_out", ts)
    big['wq_c'], bigb['wq_c'] = _mm_tn(hc, dqx, "dw_q", ts)
    big['wo_c'], bigb['wo_c'] = _mm_tn(ox, dx2, "dw_o", ts)
    for n in MID:
        big[n] = big[n].reshape(N_SHARD, 256, 1024)
        bigb[n] = bigb[n].reshape(N_SHARD, 256, 1024)

    dxg, dwrg, dwig, dbrg, dbig, dlam, dcw, dcb, *got = _b_lru(
        drec, hs, u, xg, p['conv_w'], wrg, p['b_rg'], wig, p['b_ig'], p['lru_L'], tl,
        rs(RS_IN_LRU) + swap(RS_IN_MID))
    recv.update(zip(RS_IN_LRU, got))
    sib.update(zip(RS_IN_MID, got[len(RS_IN_LRU):]))
    reduce_own(RS_IN_LRU)
    small = {
        'conv_w': dcw, 'conv_b': dcb, 'w_rg': dwrg, 'b_rg': dbrg, 'w_ig': dwig, 'b_ig': dbig, 'lru_L': dlam,
        'g_out_attn': dg_oa, 'g_out_lru': dg_ol, 'g_cross': dg_cross, 'g_mem': dg_mem, 'g_ffn': dg_ffn,
        'g_final': dg_final,
    }
    names = [n for n in SMALL if n in small]
    gather = [] if shards is None else [("ag8", [_pack_small(names, [small[n] for n in names], loss)])]
    dq, dkv, dfrow, *got = _b_attn(qkv_pad, att, datt, frow, rs(RS_IN_ATTN) + swap(RS_IN_LRU) + gather)
    recv.update(zip(RS_IN_ATTN, got))
    sib.update(zip(RS_IN_LRU, got[len(RS_IN_ATTN):]))
    packs = got[-1] if gather else None
    reduce_own(RS_IN_ATTN)
    small['rel_bias'] = _rel_bias_grad_from_dfrow(dfrow)
    big['w_in'], bigb['w_in'] = _b_win(dq, dkv, dxg, h, ts)
    if shards is None:
        grad_x, small['g_mix'] = _b_inproj(dq, dkv, dxg, x, dx1, p['g_mix'], gw['w_in'], tmb)
    else:
        sems, bufs, token = _tail_exchange_start(bigb['w_in'], [part[n] for n in RS_IN_ATTN])
        grad_x, small['g_mix'] = _b_inproj(dq, dkv, dxg, x, dx1, p['g_mix'] + token[0, 0], gw['w_in'], tmb)
        recv['w_in'], sibs = _tail_exchange_wait(sems, bufs, len(RS_IN_ATTN), small['g_mix'])
        sib.update(zip(RS_IN_ATTN, sibs))
    reduce_own(['w_in'])
    return loss, grad_x, small, big, part, sib, packs


CAST_STEPS = 4


def _cast_shards(ws, name, comm=None):
    def body(*refs):
        n = len(refs) // 2
        for src, dst in zip(refs[:n], refs[n:]):
            dst[...] = src[...].astype(BF16)

    specs = [_rows(w.shape[0] // CAST_STEPS, w.shape[1]) for w in ws]
    return _call(body, name, (CAST_STEPS,), specs, specs, [_sds(w.shape, BF16) for w in ws], [], tuple(ws),
                 "arbitrary", comm)


def _sum_parts(own4s, recv3s, chip, name):
    n = len(own4s)
    _, r, c = own4s[0].shape
    steps = _ew_steps(r, n * c * (4 + 3 * 2 + 4))
    tr = r // steps

    def body(chip_ref, *refs):
        for own_ref, rc_ref, o_ref in zip(refs[:n], refs[n:2 * n], refs[2 * n:]):
            o_ref[...] = ((own_ref[0] + rc_ref[0].astype(F32)) + rc_ref[1].astype(F32)) + rc_ref[2].astype(F32)

    grid_spec = pltpu.PrefetchScalarGridSpec(
        num_scalar_prefetch=1, grid=(steps,),
        in_specs=[pl.BlockSpec((1, tr, c), lambda i, ch: (ch[0], i, 0))] * n
                 + [pl.BlockSpec((3, tr, c), lambda i, ch: (0, i, 0))] * n,
        out_specs=[pl.BlockSpec((tr, c), lambda i, ch: (i, 0))] * n)
    return pl.pallas_call(body, name=name, grid_spec=grid_spec, out_shape=[_sds((r, c), F32)] * n,
                          compiler_params=_cp("parallel"))(chip, *own4s, *recv3s)


def _adamw_math(w, g, m, v):
    m = ADAM_B1 * m + (1.0 - ADAM_B1) * g
    v = ADAM_B2 * v + (1.0 - ADAM_B2) * (g * g)
    m_hat = m / (1.0 - ADAM_B1 ** ADAM_STEP)
    v_hat = v / (1.0 - ADAM_B2 ** ADAM_STEP)
    delta = -ADAM_LR * (m_hat / (jnp.sqrt(v_hat) + ADAM_EPS) + ADAM_WD * w)
    return delta, m, v


def _final_adamw(pas, pbs, ws, ms, vs, name):
    n = len(ws)
    r, c = ws[0].shape
    steps = _ew_steps(r, n * c * 9 * 4)
    tr = r // steps

    def body(*refs):
        ins, outs = refs[:5 * n], refs[5 * n:]
        for k in range(n):
            pa_ref, pb_ref, w_ref, m_ref, v_ref = (ins[j * n + k] for j in range(5))
            g = pa_ref[...] + pb_ref[...]
            outs[4 * k][...] = g
            outs[4 * k + 1][...], outs[4 * k + 2][...], outs[4 * k + 3][...] = _adamw_math(
                w_ref[...], g, m_ref[...], v_ref[...])

    res = pl.pallas_call(
        body, name=name, grid=(steps,), in_specs=[_rows(tr, c)] * (5 * n), out_specs=[_rows(tr, c)] * (4 * n),
        out_shape=[_sds((r, c), F32)] * (4 * n), compiler_params=_cp("parallel"))(*pas, *pbs, *ws, *ms, *vs)
    return [res[4 * k:4 * k + 4] for k in range(n)]


def _pack_put(ref, name, val_ref):
    r = _pack_rows()[name]
    shape = val_ref.shape
    if len(shape) == 3:
        for b in range(shape[0]):
            ref[r:r + shape[1], b * shape[2]:(b + 1) * shape[2]] = val_ref[b]
    elif shape[1] == 2 * PACK_W:
        ref[r:r + 1, :] = val_ref[:, 0:PACK_W]
        ref[r + 1:r + 2, :] = val_ref[:, PACK_W:2 * PACK_W]
    else:
        ref[r:r + shape[0], 0:shape[1]] = val_ref[...]


def _pack_get(ref, name, shape):
    r = _pack_rows()[name]
    if len(shape) == 3:
        return jnp.stack([ref[r:r + shape[1], b * shape[2]:(b + 1) * shape[2]] for b in range(shape[0])])
    if shape[1] == 2 * PACK_W:
        return jnp.concatenate([ref[r:r + 1, :], ref[r + 1:r + 2, :]], axis=1)
    return ref[r:r + shape[0], 0:shape[1]]


def _pack_small(names, g, loss):
    n = len(g)

    def body(*refs):
        pack = refs[n + 1]
        pack[...] = jnp.zeros_like(pack)
        for a, name in enumerate(names):
            _pack_put(pack, name, refs[a])
        _pack_put(pack, 'loss', refs[n])

    return pl.pallas_call(body, name="pack_small", out_shape=_sds((PACK_ROWS, PACK_W), F32),
                          compiler_params=_cp())(*g, loss)


def _all_peers():
    x, y, c = _mesh_pos()
    peers = []
    for k in range(1, 8):
        px = 1 - x if k & 4 else x
        py = 1 - y if k & 2 else y
        pc = 1 - c if k & 1 else c
        peers.append(((px, py, pc), 4 * px + 2 * py + pc))
    return peers, 4 * x + 2 * y + c


def _ag8_copies(ins, outs, sems):
    send_sems, recv_sems, loc_sems = sems
    n = len(ins)
    peers, me = _all_peers()

    def remote(k, j, slot):
        return pltpu.make_async_remote_copy(
            src_ref=ins[k], dst_ref=outs[k].at[slot], send_sem=send_sems.at[k, j], recv_sem=recv_sems.at[k, j],
            device_id=peers[j][0], device_id_type=MESH_ID)

    def local(k):
        return pltpu.make_async_copy(ins[k], outs[k].at[me], loc_sems.at[k])

    def start():
        for k in range(n):
            local(k).start()
            for j in range(7):
                remote(k, j, me).start()

    def wait():
        for k in range(n):
            for j in range(7):
                remote(k, j, peers[j][1]).wait_recv()
        for k in range(n):
            for j in range(7):
                remote(k, j, me).wait_send()
            local(k).wait()

    return start, _no_forward, wait


def _ar_late(names, g):
    n = len(g)

    def body(*refs):
        tot_ref, pack, buf, send_sems, recv_sems = refs[n:]
        peers, me = _all_peers()

        def remote(j, slot):
            return pltpu.make_async_remote_copy(
                src_ref=pack, dst_ref=buf.at[slot], send_sem=send_sems.at[j], recv_sem=recv_sems.at[j],
                device_id=peers[j][0], device_id_type=MESH_ID)

        pack[...] = jnp.zeros_like(pack)
        for a, name in enumerate(names):
            _pack_put(pack, name, refs[a])
        for j in range(7):
            remote(j, me).start()
        buf[me] = pack[...]
        for j in range(7):
            remote(j, peers[j][1]).wait_recv()
        for j in range(7):
            remote(j, me).wait_send()
        tot = buf[0]
        for d in range(1, 8):
            tot = tot + buf[d]
        tot_ref[...] = tot

    return pl.pallas_call(
        body, name="ar_late", out_shape=_sds((LATE_ROWS, PACK_W), F32),
        scratch_shapes=[pltpu.VMEM((LATE_ROWS, PACK_W), F32), pltpu.VMEM((8, LATE_ROWS, PACK_W), F32),
                        pltpu.SemaphoreType.DMA((7,)), pltpu.SemaphoreType.DMA((7,))],
        compiler_params=_cp())(*g)


def _adamw_small(packs, late_tot, g_shapes, loss_shape, w, m, v):
    n = len(w)

    def body(*refs):
        packs_ref, late_ref = refs[0], refs[1]
        w_refs, m_refs, v_refs = (refs[2 + i * n:2 + (i + 1) * n] for i in range(3))
        o0 = 3 * n + 2
        go, do, mo, vo = (refs[o0 + i * n:o0 + (i + 1) * n] for i in range(4))
        loss_out, tot_ref = refs[o0 + 4 * n], refs[o0 + 4 * n + 1]
        x, y, _ = _mesh_pos()
        tot = packs_ref[0]
        for d in range(1, 8):
            tot = tot + packs_ref[d]
        tot_ref[...] = tot
        tot_ref[0:LATE_ROWS, :] += late_ref[...]
        loss_out[...] = _pack_get(tot_ref, 'loss', loss_shape)
        for a, name in enumerate(SMALL):
            if name == 'conv_w':
                r = _pack_rows()[name]
                ga = tot_ref[r:r + g_shapes[a][0], pl.ds(pl.multiple_of((2 * x + y) * 128, 128), 128)]
            else:
                ga = _pack_get(tot_ref, name, g_shapes[a])
            go[a][...] = ga
            do[a][...], mo[a][...], vo[a][...] = _adamw_math(w_refs[a][...], ga, m_refs[a][...], v_refs[a][...])

    out_shape = [_sds(a.shape, F32) for a in w] * 4 + [_sds(loss_shape, F32)]
    return pl.pallas_call(body, name="adamw_small", out_shape=out_shape,
                          scratch_shapes=[pltpu.VMEM((PACK_ROWS, PACK_W), F32)],
                          compiler_params=_cp())(packs, late_tot, *w, *m, *v)


PACK_W = 512
PACK_ROWS = 160
LATE = ['g_mix', 'rel_bias']
LATE_ROWS = 32


def _pack_rows():
    rows, r = {}, 0
    for name in ['g_mix', 'g_cross', 'g_mem', 'g_ffn', 'g_final']:
        rows[name] = r
        r += 2
    for name in ['conv_b', 'b_rg', 'b_ig', 'lru_L', 'g_out_attn', 'g_out_lru']:
        rows[name] = r
        r += 1
    rows['conv_w'] = r
    rows['loss'] = r + 4
    rows['rel_bias'] = 24
    rows['w_rg'] = 32
    rows['w_ig'] = 32 + LRU_BLOCK
    assert r + 5 <= 24 and rows['w_ig'] + LRU_BLOCK == PACK_ROWS
    assert rows['g_mix'] + 2 <= LATE_ROWS and rows['rel_bias'] + 8 <= LATE_ROWS
    return rows


INPUT_NAMES = (['x', 'mem'] + WEIGHTS + ['loss_target'] + ['m_' + n for n in WEIGHTS] + ['v_' + n for n in WEIGHTS])


def kernel(x, mem, g_mix, w_in, rel_bias, conv_w, conv_b, w_rg, b_rg, w_ig, b_ig, lru_L, g_out_attn, g_out_lru, w_out, g_cross, g_mem, wq_c, wk_c, wv_c, wo_c, g_ffn, w_gate, w_up, w_down, g_final, loss_target, m_g_mix, m_w_in, m_rel_bias, m_conv_w, m_conv_b, m_w_rg, m_b_rg, m_w_ig, m_b_ig, m_lru_L, m_g_out_attn, m_g_out_lru, m_w_out, m_g_cross, m_g_mem, m_wq_c, m_wk_c, m_wv_c, m_wo_c, m_g_ffn, m_w_gate, m_w_up, m_w_down, m_g_final, v_g_mix, v_w_in, v_rel_bias, v_conv_w, v_conv_b, v_w_rg, v_b_rg, v_w_ig, v_b_ig, v_lru_L, v_g_out_attn, v_g_out_lru, v_w_out, v_g_cross, v_g_mem, v_wq_c, v_wk_c, v_wv_c, v_wo_c, v_g_ffn, v_w_gate, v_w_up, v_w_down, v_g_final):
    a = dict(zip(INPUT_NAMES, (x, mem, g_mix, w_in, rel_bias, conv_w, conv_b, w_rg, b_rg, w_ig, b_ig, lru_L, g_out_attn, g_out_lru, w_out, g_cross, g_mem, wq_c, wk_c, wv_c, wo_c, g_ffn, w_gate, w_up, w_down, g_final, loss_target, m_g_mix, m_w_in, m_rel_bias, m_conv_w, m_conv_b, m_w_rg, m_b_rg, m_w_ig, m_b_ig, m_lru_L, m_g_out_attn, m_g_out_lru, m_w_out, m_g_cross, m_g_mem, m_wq_c, m_wk_c, m_wv_c, m_wo_c, m_g_ffn, m_w_gate, m_w_up, m_w_down, m_g_final, v_g_mix, v_w_in, v_rel_bias, v_conv_w, v_conv_b, v_w_rg, v_b_rg, v_w_ig, v_b_ig, v_lru_L, v_g_out_attn, v_g_out_lru, v_w_out, v_g_cross, v_g_mem, v_wq_c, v_wk_c, v_wv_c, v_wo_c, v_g_ffn, v_w_gate, v_w_up, v_w_down, v_g_final)))
    chip = 2 * lax.axis_index("x") + lax.axis_index("y")

    def shard(name):
        arr = a[name][0]
        base = name[2:] if name[:2] in ('m_', 'v_') else name
        return jnp.swapaxes(arr, 0, 1) if base in TRANSPOSED else arr

    shards = {'w_in': _cast_shards([shard('w_in')], "cast_w_in")[0]}
    rest = [n for n in BIG if n != 'w_in']
    *cast, w_in_g, conv_w_g = _cast_shards([shard(n) for n in rest], "cast_rest",
                                           [("ag", [shards['w_in']]), ("agf", [a['conv_w'][0]])])
    shards.update(zip(rest, cast))
    conv_w_full = conv_w_g.transpose(1, 0, 2).reshape(4, D_LRU)

    p = {n: a[n] for n in SMALL}
    p['rel_bias'] = a['rel_bias'][0]
    p['w_rg'] = a['w_rg'][0]
    p['w_ig'] = a['w_ig'][0]
    p['conv_w'] = conv_w_full
    p['g_final'] = a['g_final'][None, :]
    chip_arr = jnp.reshape(chip, (1,)).astype(jnp.int32)
    loss_part, grad_x, small, _, part, sib, packs = _local_step(
        a['x'][0], a['mem'][0], a['loss_target'][0], p, {'w_in': w_in_g}, shards, chip_arr)

    sib['w_in'], = _comm_only("swap_w_in", [("swap", [part['w_in']])])
    out = {}
    for group in (['w_in'], MID, ['w_gate', 'w_up', 'w_down']):
        results = _final_adamw([part[n] for n in group], [sib[n] for n in group], [shard(n) for n in group],
                               [shard('m_' + n) for n in group], [shard('v_' + n) for n in group],
                               "adamw_" + group[0])
        for n, res in zip(group, results):
            out[n] = [jnp.swapaxes(r, 0, 1) for r in res] if n in TRANSPOSED else res

    def natural(arr):
        return arr[0] if arr.ndim >= 3 else (arr[None, :] if arr.ndim == 1 else arr)

    small_out = _adamw_small(packs, _ar_late(LATE, [small[n] for n in LATE]), [small[n].shape for n in SMALL],
                             loss_part.shape, *[[natural(a[pre + n]) for n in SMALL] for pre in ('', 'm_', 'v_')])
    ns = len(SMALL)
    loss = small_out[4 * ns][0, 0]

    def leaf(i, n):
        if n in BIG:
            return out[n][i][None]
        return small_out[i * ns + SMALL.index(n)].reshape(a[n].shape)

    return (loss, grad_x[None], *[leaf(i, n) for i in range(4) for n in WEIGHTS])
```

```python
import math

import jax
import jax.numpy as jnp
from jax import lax
from jax.experimental import pallas as pl
from jax.experimental.pallas import tpu as pltpu

F32 = jnp.float32
BF16 = jnp.bfloat16

D_MODEL = 1024
D_ATT = 512
D_LRU = 512
HEAD_DIM = 64
ATT_HEADS = 8
CHUNK = 64
LEFT_CHUNKS = 8
MAX_REL = 128
X_HEADS = 4
X_HEAD_DIM = 256
N_SHARD = 4
IN_SH = 640
D_IN = N_SHARD * IN_SH
FF_SH = 704
D_FF = N_SHARD * FF_SH
EPS = 1e-6
LRU_C = 8.0
LRU_BLOCKS = 8
LRU_BLOCK = 64
QB = 256
KB = 768
ROLL_W = 1024
NEG = -1e30
ATT_SCALE = HEAD_DIM ** -0.5
X_SCALE = X_HEAD_DIM ** -0.5

ADAM_LR = 0.001
ADAM_B1 = 0.9
ADAM_B2 = 0.999
ADAM_EPS = 1e-08
ADAM_WD = 0.01
ADAM_STEP = 10

VMEM_LIMIT_V7X = 56 * 1024 * 1024
BF16_ROWS = 16


EW_VMEM_BUDGET = 40 * 1024 * 1024


def _ew_steps(rows, bytes_per_row):
    return min(s for s in (2, 4, 8, 16) if rows % (s * BF16_ROWS) == 0
               and 2 * (rows // s) * bytes_per_row <= EW_VMEM_BUDGET)
MESH_ID = pl.DeviceIdType.MESH

WEIGHTS = ['g_mix', 'w_in', 'rel_bias', 'conv_w', 'conv_b', 'w_rg', 'b_rg', 'w_ig', 'b_ig', 'lru_L',
           'g_out_attn', 'g_out_lru', 'w_out', 'g_cross', 'g_mem', 'wq_c', 'wk_c', 'wv_c', 'wo_c',
           'g_ffn', 'w_gate', 'w_up', 'w_down', 'g_final']
BIG = ['w_in', 'w_out', 'wq_c', 'wk_c', 'wv_c', 'wo_c', 'w_gate', 'w_up', 'w_down']
SMALL = [n for n in WEIGHTS if n not in BIG]


def _sds(shape, dtype):
    return jax.ShapeDtypeStruct(shape, dtype)


def _cp(*sem):
    return pltpu.CompilerParams(dimension_semantics=sem or None, vmem_limit_bytes=VMEM_LIMIT_V7X)


def _rows(tm, n):
    return pl.BlockSpec((tm, n), lambda i: (i, 0))


def _full(shape):
    nd = len(shape)
    return pl.BlockSpec(shape, lambda i: (0,) * nd)


def _dot(a, b):
    return jnp.dot(a, b, preferred_element_type=F32)


def _dot_nt(a, b):
    return lax.dot_general(a, b, (((1,), (1,)), ((), ())), preferred_element_type=F32)


def _dot_tn(a, b):
    return lax.dot_general(a, b, (((0,), (0,)), ((), ())), preferred_element_type=F32)


def _rinv(x):
    return lax.rsqrt(jnp.mean(x * x, axis=-1, keepdims=True) + EPS)


def _rms_bwd(dy, x, g):
    r = _rinv(x)
    yh = x * r
    dyh = dy * g
    dx = r * (dyh - yh * jnp.mean(dyh * yh, axis=-1, keepdims=True))
    return dx, jnp.sum(dy * yh, axis=0, keepdims=True)


def _gelu(x):
    c = math.sqrt(2.0 / math.pi)
    t = jnp.tanh(c * (x + 0.044715 * x * x * x))
    return 0.5 * x * (1.0 + t)


def _gelu_and_grad(x):
    c = math.sqrt(2.0 / math.pi)
    t = jnp.tanh(c * (x + 0.044715 * x * x * x))
    g = 0.5 * x * (1.0 + t)
    dg = 0.5 * (1.0 + t) + 0.5 * x * (1.0 - t * t) * c * (1.0 + 3.0 * 0.044715 * x * x)
    return g, dg


def _neg_expm1(z):
    series = -z * (1.0 + z * (0.5 + z * ((1.0 / 6.0) + z * (1.0 / 24.0))))
    return jnp.where(z > -0.03, series, 1.0 - jnp.exp(z))


def _lru_gates(u, wrg, brg, wig, big, lam):
    ub = u.astype(BF16)
    r = jax.nn.sigmoid(_dot(ub, wrg) + brg)
    ig = jax.nn.sigmoid(_dot(ub, wig) + big)
    sp = jnp.maximum(-lam, 0.0) + jnp.log1p(jnp.exp(-jnp.abs(lam)))
    la = -LRU_C * r * sp
    a = jnp.exp(la)
    mult = jnp.sqrt(jnp.maximum(_neg_expm1(2.0 * la), 0.0))
    return ub, r, ig, sp, a, mult


def _scan8(a8, b8, hprev):
    row = lax.broadcasted_iota(jnp.int32, a8.shape, 0)
    aa, bb = a8, b8
    for d in (1, 2, 4):
        a_s = pltpu.roll(aa, d, 0)
        b_s = pltpu.roll(bb, d, 0)
        m = row >= d
        bb = jnp.where(m, aa * b_s + bb, bb)
        aa = jnp.where(m, aa * a_s, aa)
    return aa * hprev + bb


def _rscan8(c8, d8, lnext):
    row = lax.broadcasted_iota(jnp.int32, c8.shape, 0)
    cc, dd = c8, d8
    for d in (1, 2, 4):
        c_s = pltpu.roll(cc, 8 - d, 0)
        d_s = pltpu.roll(dd, 8 - d, 0)
        m = row < 8 - d
        dd = jnp.where(m, cc * d_s + dd, dd)
        cc = jnp.where(m, cc * c_s, cc)
    return cc * lnext + dd


def _mesh_pos():
    return lax.axis_index("x"), lax.axis_index("y"), lax.axis_index("c")


def _other_chips(x, y):
    return [(1 - x, y), (x, 1 - y), (1 - x, 1 - y)]


def _no_forward():
    pass


def _ag_full_copies(ins, outs, sems):
    send_sems, recv_sems, loc_sems = sems
    n = len(ins)
    x, y, c = _mesh_pos()
    mine = 2 * x + y
    chips = _other_chips(x, y)

    def remote(k, j, slot):
        px, py = chips[j]
        return pltpu.make_async_remote_copy(
            src_ref=ins[k], dst_ref=outs[k].at[slot], send_sem=send_sems.at[k, j], recv_sem=recv_sems.at[k, j],
            device_id=(px, py, c), device_id_type=MESH_ID)

    def local(k):
        return pltpu.make_async_copy(ins[k], outs[k].at[mine], loc_sems.at[k])

    def start():
        for k in range(n):
            local(k).start()
            for j in range(3):
                remote(k, j, mine).start()

    def wait():
        for k in range(n):
            for j, (px, py) in enumerate(chips):
                remote(k, j, 2 * px + py).wait_recv()
        for k in range(n):
            for j in range(3):
                remote(k, j, mine).wait_send()
            local(k).wait()

    return start, _no_forward, wait


def _ag_copies(ins, outs, sems):
    send_sems, recv_sems, fsend_sems, frecv_sems, loc_sems = sems
    n = len(ins)
    x, y, c = _mesh_pos()
    mine = 2 * x + y
    chips = _other_chips(x, y)

    def half(ref, hc):
        r = ref.shape[0] // 2
        return ref.at[pl.ds(pl.multiple_of(hc * r, 16), r)]

    def ici(k, j, slot):
        px, py = chips[j]
        return pltpu.make_async_remote_copy(
            src_ref=half(ins[k], c), dst_ref=half(outs[k].at[slot], c),
            send_sem=send_sems.at[k, j], recv_sem=recv_sems.at[k, j],
            device_id=(px, py, c), device_id_type=MESH_ID)

    def d2d(k, j, hc):
        px, py = chips[j]
        part = half(outs[k].at[2 * px + py], hc)
        return pltpu.make_async_remote_copy(
            src_ref=part, dst_ref=part, send_sem=fsend_sems.at[k, j], recv_sem=frecv_sems.at[k, j],
            device_id=(x, y, 1 - c), device_id_type=MESH_ID)

    def local(k):
        return pltpu.make_async_copy(ins[k], outs[k].at[mine], loc_sems.at[k])

    def start():
        for k in range(n):
            local(k).start()
            for j in range(3):
                ici(k, j, mine).start()

    def forward():
        for k in range(n):
            for j, (px, py) in enumerate(chips):
                ici(k, j, 2 * px + py).wait_recv()
                d2d(k, j, c).start()

    def wait():
        for k in range(n):
            for j in range(3):
                d2d(k, j, 1 - c).wait_recv()
        for k in range(n):
            for j in range(3):
                d2d(k, j, c).wait_send()
                ici(k, j, mine).wait_send()
            local(k).wait()

    return start, forward, wait


def _rs_copies(ins, outs, sems):
    send_sems, recv_sems = sems
    n = len(ins)
    x, y, c = _mesh_pos()
    chips = _other_chips(x, y)

    def remote(k, j):
        px, py = chips[j]
        return pltpu.make_async_remote_copy(
            src_ref=ins[k].at[2 * px + py], dst_ref=outs[k].at[j],
            send_sem=send_sems.at[k, j], recv_sem=recv_sems.at[k, j],
            device_id=(px, py, c), device_id_type=MESH_ID)

    def start():
        for k in range(n):
            for j in range(3):
                remote(k, j).start()

    def wait():
        for k in range(n):
            for j in range(3):
                remote(k, j).wait_recv()
        for k in range(n):
            for j in range(3):
                remote(k, j).wait_send()

    return start, _no_forward, wait


def _swap_copies(ins, outs, sems):
    send_sems, recv_sems = sems
    x, y, c = _mesh_pos()
    copies = [pltpu.make_async_remote_copy(
        src_ref=ins[k], dst_ref=outs[k], send_sem=send_sems.at[k], recv_sem=recv_sems.at[k],
        device_id=(x, y, 1 - c), device_id_type=MESH_ID) for k in range(len(ins))]

    def start():
        for cp in copies:
            cp.start()

    def wait():
        for cp in copies:
            cp.wait()

    return start, _no_forward, wait


def _comm_plan(groups):
    plan, arrs, shapes, sems = [], [], [], []
    for kind, group in groups:
        k = len(group)
        arrs += group
        per_peer = pltpu.SemaphoreType.DMA((k, 3))
        if kind == "ag":
            shapes += [_sds((N_SHARD,) + w.shape, w.dtype) for w in group]
            gsems = [per_peer] * 4 + [pltpu.SemaphoreType.DMA((k,))]
            maker = _ag_copies
        elif kind == "agf":
            shapes += [_sds((N_SHARD,) + w.shape, w.dtype) for w in group]
            gsems = [per_peer] * 2 + [pltpu.SemaphoreType.DMA((k,))]
            maker = _ag_full_copies
        elif kind == "ag8":
            shapes += [_sds((8,) + g.shape, g.dtype) for g in group]
            gsems = [pltpu.SemaphoreType.DMA((k, 7))] * 2 + [pltpu.SemaphoreType.DMA((k,))]
            maker = _ag8_copies
        elif kind == "rs":
            shapes += [_sds((3,) + g.shape[1:], g.dtype) for g in group]
            gsems = [pltpu.SemaphoreType.DMA((k, 3)), pltpu.SemaphoreType.DMA((k, 3))]
            maker = _rs_copies
        else:
            shapes += [_sds(g.shape, g.dtype) for g in group]
            gsems = [pltpu.SemaphoreType.DMA((k,)), pltpu.SemaphoreType.DMA((k,))]
            maker = _swap_copies
        plan.append((maker, k, len(gsems)))
        sems += gsems
    return plan, arrs, shapes, sems


def _comm_fns(plan, cins, couts, sems):
    fns, a, s = [], 0, 0
    for maker, k, ns in plan:
        fns.append(maker(cins[a:a + k], couts[a:a + k], sems[s:s + ns]))
        a += k
        s += ns

    def start():
        for st, _, _ in fns:
            st()

    def forward():
        for _, fw, _ in fns:
            fw()

    def wait():
        for _, _, wt in fns:
            wt()

    return start, forward, wait


def _call(body, name, grid, in_specs, out_specs, out_shape, scratch, args, sem, comm=None):
    if not comm:
        return pl.pallas_call(body, name=name, grid=grid, in_specs=in_specs, out_specs=out_specs,
                              out_shape=out_shape, scratch_shapes=scratch, compiler_params=_cp(sem))(*args)
    plan, c_arrs, c_shapes, c_sems = _comm_plan(comm)
    k = len(c_arrs)
    n_in, n_out, n_scr = len(in_specs), len(out_specs), len(scratch)
    last = grid[0] - 1
    fwd_step = max(1, (2 * last) // 3)

    def wrapped(*refs):
        ins, cins = refs[:n_in], refs[n_in:n_in + k]
        o0 = n_in + k
        outs, couts = refs[o0:o0 + n_out], refs[o0 + n_out:o0 + n_out + k]
        s0 = o0 + n_out + k
        start, forward, wait = _comm_fns(plan, cins, couts, refs[s0 + n_scr:])
        pl.when(pl.program_id(0) == 0)(start)
        pl.when(pl.program_id(0) == fwd_step)(forward)
        body(*ins, *outs, *refs[s0:s0 + n_scr])
        pl.when(pl.program_id(0) == last)(wait)

    return pl.pallas_call(
        wrapped, name=name, grid=grid, in_specs=list(in_specs) + [_any()] * k,
        out_specs=list(out_specs) + [_any()] * k, out_shape=list(out_shape) + c_shapes,
        scratch_shapes=list(scratch) + c_sems, compiler_params=_cp(sem))(*args, *c_arrs)


def _tail_copies(slots_ref, land_ref, part_refs, sib_refs, send_sems, recv_sems):
    x, y, c = _mesh_pos()
    copies = []
    for j, (px, py) in enumerate(_other_chips(x, y)):
        copies.append(pltpu.make_async_remote_copy(
            src_ref=slots_ref.at[2 * px + py], dst_ref=land_ref.at[j], send_sem=send_sems[j], recv_sem=recv_sems[j],
            device_id=(px, py, c), device_id_type=MESH_ID))
    for k, (p_ref, s_ref) in enumerate(zip(part_refs, sib_refs)):
        copies.append(pltpu.make_async_remote_copy(
            src_ref=p_ref, dst_ref=s_ref, send_sem=send_sems[3 + k], recv_sem=recv_sems[3 + k],
            device_id=(x, y, 1 - c), device_id_type=MESH_ID))
    return copies


def _late_copies(part_ref, sib_ref, pack_ref, packs_ref, send_sems, recv_sems):
    x, y, c = _mesh_pos()
    peers, me = _all_peers()
    copies = [pltpu.make_async_remote_copy(
        src_ref=part_ref, dst_ref=sib_ref, send_sem=send_sems[0], recv_sem=recv_sems[0],
        device_id=(x, y, 1 - c), device_id_type=MESH_ID)]
    for j in range(7):
        copies.append(pltpu.make_async_remote_copy(
            src_ref=pack_ref, dst_ref=packs_ref.at[me], send_sem=send_sems[1 + j], recv_sem=recv_sems[1 + j],
            device_id=peers[j][0], device_id_type=MESH_ID))
    return copies


def _split_start(name, bufs, ncp, make_copies):
    hbm = pl.BlockSpec(memory_space=pltpu.HBM)
    sem = pl.BlockSpec(memory_space=pltpu.SEMAPHORE)
    bufs = [pltpu.with_memory_space_constraint(b, pltpu.HBM) for b in bufs]
    nb = len(bufs)

    def body(*refs):
        for cp in make_copies(refs[:nb], refs[nb:nb + ncp], refs[nb + ncp:nb + 2 * ncp]):
            cp.start()
        refs[-1][...] = jnp.zeros_like(refs[-1])

    out = pl.pallas_call(
        body, name=name,
        out_shape=[pltpu.SemaphoreType.DMA(())] * (2 * ncp) + [pltpu.HBM(b.shape, b.dtype) for b in bufs]
                  + [_sds((8, 128), F32)],
        in_specs=[hbm] * nb, out_specs=[sem] * (2 * ncp) + [hbm] * nb + [pl.BlockSpec(memory_space=pltpu.VMEM)],
        input_output_aliases={i: 2 * ncp + i for i in range(nb)},
        compiler_params=pltpu.CompilerParams(has_side_effects=pltpu.SideEffectType.DATAFLOW_SIDE_EFFECTING),
    )(*bufs)
    return out[:2 * ncp], out[2 * ncp:2 * ncp + nb], out[-1]


def _split_wait(name, sems, bufs, ncp, make_copies, after):
    nb = len(bufs)
    hbm = pl.BlockSpec(memory_space=pltpu.HBM)
    sem = pl.BlockSpec(memory_space=pltpu.SEMAPHORE)

    def body(*refs):
        for cp in make_copies(refs[:nb], refs[nb:nb + ncp], refs[nb + ncp:nb + 2 * ncp]):
            cp.wait_send()
            cp.wait_recv()

    return pl.pallas_call(
        body, name=name, out_shape=[pltpu.HBM(b.shape, b.dtype) for b in bufs],
        in_specs=[hbm] * nb + [sem] * (2 * ncp) + [_any()], out_specs=[hbm] * nb,
        input_output_aliases={i: i for i in range(nb)},
        compiler_params=pltpu.CompilerParams(has_side_effects=pltpu.SideEffectType.DATAFLOW_SIDE_EFFECTING),
    )(*bufs, *sems, after)


def _comm_only(name, comm):
    plan, c_arrs, c_shapes, c_sems = _comm_plan(comm)
    k = len(c_arrs)

    def body(*refs):
        start, forward, wait = _comm_fns(plan, refs[:k], refs[k:2 * k], refs[2 * k:])
        start()
        forward()
        wait()

    return pl.pallas_call(body, name=name, in_specs=[_any()] * k, out_specs=[_any()] * k, out_shape=c_shapes,
                          scratch_shapes=c_sems, compiler_params=_cp())(*c_arrs)


def _any():
    return pl.BlockSpec(memory_space=pl.ANY)


def _load_w_in_once(w_hbm, w_ref):
    @pl.when(pl.program_id(0) == 0)
    def _():
        for s in range(N_SHARD):
            pltpu.sync_copy(w_hbm.at[s], w_ref.at[:, pl.ds(s * IN_SH, IN_SH)])


def _f_inproj(x, g_mix, w_in_g, tm, comm=None):
    s_len = x.shape[0]
    pad_rows = LEFT_CHUNKS * CHUNK
    npad = pad_rows // tm

    def body(x_ref, g_ref, w_hbm, h_ref, qkv_ref, xg_ref, w_ref):
        i = pl.program_id(0)
        _load_w_in_once(w_hbm, w_ref)

        @pl.when(i < npad)
        def _():
            qkv_ref[...] = jnp.zeros_like(qkv_ref)

        @pl.when(i >= npad)
        def _():
            xv = x_ref[...]
            h = (xv * _rinv(xv) * g_ref[...]).astype(BF16)
            h_ref[...] = h
            proj = _dot(h, w_ref[...])
            qkv_ref[:, 0:D_ATT] = (proj[:, 0:D_ATT] * ATT_SCALE).astype(BF16)
            qkv_ref[:, D_ATT:3 * D_ATT] = proj[:, D_ATT:3 * D_ATT].astype(BF16)
            xg_ref[...] = proj[:, 3 * D_ATT:D_IN]

    def tok(n):
        return pl.BlockSpec((tm, n), lambda i: (jnp.maximum(i - npad, 0), 0))

    return _call(
        body, "f_inproj", (s_len // tm + npad,),
        [tok(1024), _full((1, 1024)), _any()],
        [tok(1024), _rows(tm, 1536), tok(1024)],
        [_sds((s_len, 1024), BF16), _sds((s_len + pad_rows, 1536), BF16), _sds((s_len, 1024), F32)],
        [pltpu.VMEM((1024, D_IN), BF16)], (x, g_mix, w_in_g), "arbitrary", comm)


N_BIAS = 3


def _bias_table(frow_ref, bias_sc):
    qa = lax.broadcasted_iota(jnp.int32, (QB, KB), 0) // CHUNK
    kcol = lax.broadcasted_iota(jnp.int32, (QB, KB), 1)
    kb = kcol // CHUNK
    band = jnp.where((kb >= qa) & (kb - qa <= LEFT_CHUNKS), 0.0, NEG).astype(F32)
    for h in range(ATT_HEADS):
        row = jnp.broadcast_to(frow_ref[h:h + 1, :], (QB, ROLL_W))
        toep = pltpu.roll(row, 0, 1, stride=1, stride_axis=0)
        gen = toep[:, 0:KB] + band
        bias_sc[N_BIAS - 1, h] = gen
        for v in range(N_BIAS - 1):
            pad_keys = LEFT_CHUNKS * CHUNK - v * QB
            bias_sc[v, h] = gen + jnp.where(kcol < pad_keys, NEG, 0.0).astype(F32)


def _even_lanes():
    return lax.broadcasted_iota(jnp.int32, (1, 2 * HEAD_DIM), 1) < HEAD_DIM


def _att_probs(qm, kts, bias):
    s = jnp.concatenate([_dot_nt(qm, k) for k in kts], axis=1) + bias
    return jnp.exp(s - jnp.max(s, axis=-1, keepdims=True))


def _att_in_specs(clamp):
    def spec(j, col):
        return pl.BlockSpec((QB, D_ATT), lambda i: (clamp(i) + j, col))
    return [spec(2, 0), spec(0, 1), spec(1, 1), spec(2, 1), spec(0, 2), spec(1, 2), spec(2, 2)]


def _f_attn(qkv_pad, frow, comm=None):
    s_len = qkv_pad.shape[0] - LEFT_CHUNKS * CHUNK
    nb = s_len // QB

    def body(q_ref, k0, k1, k2, v0, v1, v2, frow_ref, o_ref, bias_sc):
        i = pl.program_id(0)

        @pl.when(i == 0)
        def _():
            _bias_table(frow_ref, bias_sc)

        var = jnp.minimum(i, N_BIAS - 1)
        even = _even_lanes()
        for hp in range(ATT_HEADS // 2):
            cs = slice(hp * 2 * HEAD_DIM, (hp + 1) * 2 * HEAD_DIM)
            qt = q_ref[:, cs]
            kts = [k0[:, cs], k1[:, cs], k2[:, cs]]
            vts = [v0[:, cs], v1[:, cs], v2[:, cs]]
            res = []
            for e in range(2):
                keep = even if e == 0 else jnp.logical_not(even)
                pb = _att_probs(jnp.where(keep, qt, 0), kts, bias_sc[var, 2 * hp + e]).astype(BF16)
                r = _dot(pb, jnp.concatenate([jnp.where(keep, v, 1) for v in vts], axis=0))
                res.append(r / pltpu.roll(r, HEAD_DIM, 1))
            o_ref[:, cs] = jnp.where(even, res[0], res[1])

    return _call(
        body, "f_attn", (nb,),
        _att_in_specs(lambda i: i) + [_full((ATT_HEADS, ROLL_W))],
        [_rows(QB, D_ATT)], [_sds((s_len, D_ATT), F32)],
        [pltpu.VMEM((N_BIAS, ATT_HEADS, QB, KB), F32)], (*([qkv_pad] * 7), frow), "arbitrary", comm)


def _f_lru(xg, conv_w, conv_b, wrg, brg, wig, big, lam, tl, comm=None):
    s_len = xg.shape[0]

    def body(xg_ref, cw_ref, cb_ref, wrg_ref, brg_ref, wig_ref, big_ref, l_ref,
             rec_ref, u_ref, hs_ref, xbuf, a_sc, b_sc, hcar):
        i = pl.program_id(0)

        @pl.when(i == 0)
        def _():
            xbuf[0:8, :] = jnp.zeros((8, D_LRU), F32)
            hcar[...] = jnp.zeros((8, D_LRU), F32)

        xu0 = xg_ref[:, 0:D_LRU]
        xbuf[8:8 + tl, :] = xu0
        u = cb_ref[...] + cw_ref[0:1, :] * xbuf[pl.ds(5, tl), :]
        for j in range(1, 4):
            u = u + cw_ref[j:j + 1, :] * xbuf[pl.ds(5 + j, tl), :]
        xbuf[0:8, :] = xu0[tl - 8:tl, :]
        u_ref[...] = u
        _, _, ig, _, a, mult = _lru_gates(u, wrg_ref[...], brg_ref[...], wig_ref[...], big_ref[...], l_ref[...])
        a_sc[...] = a
        b_sc[...] = mult * (ig * u)

        def grp(g, hprev):
            off = pl.multiple_of(g * 8, 8)
            h8 = _scan8(a_sc[pl.ds(off, 8), :], b_sc[pl.ds(off, 8), :], hprev)
            hs_ref[pl.ds(off, 8), :] = h8
            return h8[7:8, :]

        hcar[0:1, :] = lax.fori_loop(0, tl // 8, grp, hcar[0:1, :])
        rec_ref[...] = hs_ref[...] * _gelu(xg_ref[:, D_LRU:2 * D_LRU])

    vec = _full((1, D_LRU))
    return _call(
        body, "f_lru", (s_len // tl,),
        [_rows(tl, 1024), _full((4, D_LRU)), vec, _full((D_LRU, D_LRU)), vec, _full((D_LRU, D_LRU)), vec, vec],
        [_rows(tl, D_LRU)] * 3, [_sds((s_len, D_LRU), F32)] * 3,
        [pltpu.VMEM((tl + 8, D_LRU), F32), pltpu.VMEM((tl, D_LRU), F32),
         pltpu.VMEM((tl, D_LRU), F32), pltpu.VMEM((8, D_LRU), F32)],
        (xg, conv_w, conv_b, wrg, brg, wig, big, lam), "arbitrary", comm)


def _f_mem(mem, g_mem, wk, wv):
    def body(mem_ref, g_ref, wk_ref, wv_ref, mn_ref, kx_ref, vx_ref):
        mv = mem_ref[...]
        mn = (mv * _rinv(mv) * g_ref[...]).astype(BF16)
        mn_ref[...] = mn
        kx_ref[...] = _dot(mn, wk_ref[...]).astype(BF16)
        vx_ref[...] = _dot(mn, wv_ref[...]).astype(BF16)

    m = mem.shape[0]
    return pl.pallas_call(
        body, name="f_mem", out_shape=[_sds((m, 1024), BF16)] * 3,
        compiler_params=_cp())(mem, g_mem, wk, wv)


def _xattn_probs(q, k):
    s = _dot_nt(q, k) * X_SCALE
    m = jnp.max(s, axis=-1, keepdims=True)
    p = jnp.exp(s - m)
    return p, jnp.sum(p, axis=-1, keepdims=True)


def _f_mid(x, att, rec, g_oa, g_ol, w_out, g_cross, wq, kx, vx, wo, tm, comm=None):
    s_len = x.shape[0]
    m_len = kx.shape[0]

    def body(x_ref, att_ref, rec_ref, goa_ref, gol_ref, wout_ref, gc_ref, wq_ref, kx_ref, vx_ref, wo_ref,
             mg_ref, x1_ref, hc_ref, qx_ref, ox_ref, x2_ref):
        av = att_ref[...]
        rv = rec_ref[...]
        mg_ref[:, 0:D_ATT] = (av * _rinv(av) * goa_ref[...]).astype(BF16)
        mg_ref[:, D_ATT:1024] = (rv * _rinv(rv) * gol_ref[...]).astype(BF16)
        x1 = x_ref[...] + _dot(mg_ref[...], wout_ref[...])
        x1_ref[...] = x1
        hc = (x1 * _rinv(x1) * gc_ref[...]).astype(BF16)
        hc_ref[...] = hc
        qx_ref[...] = _dot(hc, wq_ref[...]).astype(BF16)
        for h in range(X_HEADS):
            sl = slice(h * X_HEAD_DIM, (h + 1) * X_HEAD_DIM)
            p, l = _xattn_probs(qx_ref[:, sl], kx_ref[:, sl])
            ox_ref[:, sl] = (_dot(p.astype(BF16), vx_ref[:, sl]) / l).astype(BF16)
        x2_ref[...] = x1 + _dot(ox_ref[...], wo_ref[...])

    sq = _full((1024, 1024))
    return _call(
        body, "f_mid", (s_len // tm,),
        [_rows(tm, 1024), _rows(tm, 512), _rows(tm, 512), _full((1, 512)), _full((1, 512)), sq,
         _full((1, 1024)), sq, _full((m_len, 1024)), _full((m_len, 1024)), sq],
        [_rows(tm, 1024)] * 6,
        [_sds((s_len, 1024), BF16), _sds((s_len, 1024), F32), _sds((s_len, 1024), BF16),
         _sds((s_len, 1024), BF16), _sds((s_len, 1024), BF16), _sds((s_len, 1024), F32)],
        [], (x, att, rec, g_oa, g_ol, w_out, g_cross, wq, kx, vx, wo), "arbitrary", comm)


def _load_weights_once(pairs):
    @pl.when(pl.program_id(0) == 0)
    def _():
        for hbm, vmem in pairs:
            pltpu.sync_copy(hbm, vmem)


FF_CHUNKS = [(0, 1280), (1280, D_FF)]


def _f_ffn(x2, tgt, g_ffn, g_final, wg, wu, wd, tm):
    s_len = x2.shape[0]

    def body(x2_ref, t_ref, gf_ref, gfin_ref, wg_hbm, wu_hbm, wd_hbm,
             hf_ref, g_ref, u_ref, a_ref, dx3_ref, loss_ref, dgfin_ref, wg_ref, wu_ref, wd_ref):
        _load_weights_once([(wg_hbm, wg_ref), (wu_hbm, wu_ref), (wd_hbm, wd_ref)])

        @pl.when(pl.program_id(0) == 0)
        def _():
            loss_ref[...] = jnp.zeros_like(loss_ref)
            dgfin_ref[...] = jnp.zeros_like(dgfin_ref)

        x2v = x2_ref[...]
        hf = (x2v * _rinv(x2v) * gf_ref[...]).astype(BF16)
        hf_ref[...] = hf
        x3 = x2v
        for c0, c1 in FF_CHUNKS:
            gv = _dot_nt(hf, wg_ref[c0:c1, :])
            uv = _dot_nt(hf, wu_ref[c0:c1, :])
            av = (gv * jax.nn.sigmoid(gv) * uv).astype(BF16)
            g_ref[:, c0:c1] = gv.astype(BF16)
            u_ref[:, c0:c1] = uv.astype(BF16)
            a_ref[:, c0:c1] = av
            x3 = x3 + _dot(av, wd_ref[c0:c1, :])
        r3 = _rinv(x3)
        yh = x3 * r3
        gfin = gfin_ref[...]
        err = yh * gfin - t_ref[...]
        loss_ref[...] += jnp.full((1, 128), 0.5 / D_MODEL, F32) * jnp.sum(err * err)
        dy = err * (1.0 / D_MODEL)
        dgfin_ref[...] += jnp.sum(dy * yh, axis=0, keepdims=True)
        dyh = dy * gfin
        dx3_ref[...] = r3 * (dyh - yh * jnp.mean(dyh * yh, axis=-1, keepdims=True))

    vec = _full((1, 1024))
    return pl.pallas_call(
        body, name="f_ffn", grid=(s_len // tm,),
        in_specs=[_rows(tm, 1024), _rows(tm, 1024), vec, vec, _any(), _any(), _any()],
        out_specs=[_rows(tm, 1024), _rows(tm, D_FF), _rows(tm, D_FF), _rows(tm, D_FF),
                   _rows(tm, 1024), _full((1, 128)), vec],
        out_shape=[_sds((s_len, 1024), BF16)] + [_sds((s_len, D_FF), BF16)] * 3
                  + [_sds((s_len, 1024), F32), _sds((1, 128), F32), _sds((1, 1024), F32)],
        scratch_shapes=[pltpu.VMEM((D_FF, 1024), BF16)] * 3,
        compiler_params=_cp("arbitrary"))(x2, tgt, g_ffn, g_final, wg, wu, wd)


def _b_ffn(dx3, x2, gact, uact, g_ffn, wg, wu, wd, tm):
    s_len = x2.shape[0]

    def body(dx3_ref, x2_ref, g_ref, u_ref, gf_ref, wg_hbm, wu_hbm, wd_hbm,
             dg_ref, du_ref, dx2_ref, dgf_ref, wg_ref, wu_ref, wd_ref):
        _load_weights_once([(wg_hbm, wg_ref), (wu_hbm, wu_ref), (wd_hbm, wd_ref)])

        @pl.when(pl.program_id(0) == 0)
        def _():
            dgf_ref[...] = jnp.zeros_like(dgf_ref)

        dx3v = dx3_ref[...]
        dx3b = dx3v.astype(BF16)
        dhf = jnp.zeros(dx3v.shape, F32)
        for c0, c1 in FF_CHUNKS:
            da = _dot_nt(dx3b, wd_ref[c0:c1, :])
            gv = g_ref[:, c0:c1].astype(F32)
            uv = u_ref[:, c0:c1].astype(F32)
            sg = jax.nn.sigmoid(gv)
            dub = (da * gv * sg).astype(BF16)
            dgb = (da * uv * (sg * (1.0 + gv * (1.0 - sg)))).astype(BF16)
            du_ref[:, c0:c1] = dub
            dg_ref[:, c0:c1] = dgb
            dhf = dhf + _dot(dgb, wg_ref[c0:c1, :]) + _dot(dub, wu_ref[c0:c1, :])
        dx, dgf = _rms_bwd(dhf, x2_ref[...], gf_ref[...])
        dx2_ref[...] = dx3v + dx
        dgf_ref[...] += dgf

    vec = _full((1, 1024))
    return pl.pallas_call(
        body, name="b_ffn", grid=(s_len // tm,),
        in_specs=[_rows(tm, 1024), _rows(tm, 1024), _rows(tm, D_FF), _rows(tm, D_FF), vec,
                  _any(), _any(), _any()],
        out_specs=[_rows(tm, D_FF), _rows(tm, D_FF), _rows(tm, 1024), vec],
        out_shape=[_sds((s_len, D_FF), BF16)] * 2 + [_sds((s_len, 1024), F32), _sds((1, 1024), F32)],
        scratch_shapes=[pltpu.VMEM((D_FF, 1024), BF16)] * 3,
        compiler_params=_cp("arbitrary"))(dx3, x2, gact, uact, g_ffn, wg, wu, wd)


def _b_mid(dx2, qx, x1, att, rec, kx, vx, wo, wq, w_out, g_cross, g_oa, g_ol, tm, comm=None):
    s_len = x1.shape[0]
    m_len = kx.shape[0]

    def body(dx2_ref, qx_ref, x1_ref, att_ref, rec_ref, kx_ref, vx_ref, wo_ref, wq_ref, wout_ref,
             gc_ref, goa_ref, gol_ref,
             dqx_ref, dx1_ref, datt_ref, drec_ref, dkx_ref, dvx_ref, dgc_ref, dgoa_ref, dgol_ref):
        @pl.when(pl.program_id(0) == 0)
        def _():
            for r in (dkx_ref, dvx_ref, dgc_ref, dgoa_ref, dgol_ref):
                r[...] = jnp.zeros_like(r)

        dx2v = dx2_ref[...]
        dox = _dot_nt(dx2v.astype(BF16), wo_ref[...])
        for h in range(X_HEADS):
            sl = slice(h * X_HEAD_DIM, (h + 1) * X_HEAD_DIM)
            q = qx_ref[:, sl]
            p, l = _xattn_probs(q, kx_ref[:, sl])
            pn = p * (1.0 / l)
            dob = dox[:, sl].astype(BF16)
            dp = _dot_nt(dob, vx_ref[:, sl])
            dvx_ref[:, sl] += _dot_tn(pn.astype(BF16), dob)
            ds = pn * (dp - jnp.sum(dp * pn, axis=-1, keepdims=True))
            dsb = (ds * X_SCALE).astype(BF16)
            dqx_ref[:, sl] = _dot(dsb, kx_ref[:, sl]).astype(BF16)
            dkx_ref[:, sl] += _dot_tn(dsb, q)
        dhc = _dot_nt(dqx_ref[...], wq_ref[...])
        dx, dgc = _rms_bwd(dhc, x1_ref[...], gc_ref[...])
        dx1 = dx2v + dx
        dx1_ref[...] = dx1
        dgc_ref[...] += dgc
        dmg = _dot_nt(dx1.astype(BF16), wout_ref[...])
        da, dgoa = _rms_bwd(dmg[:, 0:D_ATT], att_ref[...], goa_ref[...])
        datt_ref[...] = da
        dgoa_ref[...] += dgoa
        dr, dgol = _rms_bwd(dmg[:, D_ATT:1024], rec_ref[...], gol_ref[...])
        drec_ref[...] = dr
        dgol_ref[...] += dgol

    sq = _full((1024, 1024))
    mk = _full((m_len, 1024))
    return _call(
        body, "b_mid", (s_len // tm,),
        [_rows(tm, 1024), _rows(tm, 1024), _rows(tm, 1024), _rows(tm, 512), _rows(tm, 512), mk, mk,
         sq, sq, sq, _full((1, 1024)), _full((1, 512)), _full((1, 512))],
        [_rows(tm, 1024), _rows(tm, 1024), _rows(tm, 512), _rows(tm, 512), mk, mk,
         _full((1, 1024)), _full((1, 512)), _full((1, 512))],
        [_sds((s_len, 1024), BF16), _sds((s_len, 1024), F32), _sds((s_len, 512), F32),
         _sds((s_len, 512), F32), _sds((m_len, 1024), F32), _sds((m_len, 1024), F32),
         _sds((1, 1024), F32), _sds((1, 512), F32), _sds((1, 512), F32)],
        [], (dx2, qx, x1, att, rec, kx, vx, wo, wq, w_out, g_cross, g_oa, g_ol), "arbitrary", comm)


def _b_mem(dkx, dvx, mem, mn, g_mem, wk, wv):
    def body(dkx_ref, dvx_ref, mem_ref, mn_ref, g_ref, wk_ref, wv_ref, dwk_ref, dwv_ref, dgm_ref,
             dwkb_ref, dwvb_ref):
        dkb = dkx_ref[...].astype(BF16)
        dvb = dvx_ref[...].astype(BF16)
        dwk = _dot_tn(mn_ref[...], dkb)
        dwv = _dot_tn(mn_ref[...], dvb)
        dwk_ref[...] = dwk
        dwv_ref[...] = dwv
        dwkb_ref[...] = dwk.astype(BF16)
        dwvb_ref[...] = dwv.astype(BF16)
        dmn = _dot_nt(dkb, wk_ref[...]) + _dot_nt(dvb, wv_ref[...])
        mv = mem_ref[...]
        dgm_ref[...] = jnp.sum(dmn * (mv * _rinv(mv)), axis=0, keepdims=True)

    return pl.pallas_call(
        body, name="b_mem",
        out_shape=[_sds((1024, 1024), F32), _sds((1024, 1024), F32), _sds((1, 1024), F32),
                   _sds((1024, 1024), BF16), _sds((1024, 1024), BF16)],
        compiler_params=_cp())(dkx, dvx, mem, mn, g_mem, wk, wv)


def _b_lru(drec, hs, u, xg, conv_w, wrg, brg, wig, big, lam, tl, comm=None):
    s_len = xg.shape[0]
    nt = s_len // tl

    def body(drec_ref, hs_ref, hsp_ref, u_ref, xg_ref, cw_ref, wrg_ref, brg_ref, wig_ref, big_ref, l_ref,
             dxg_ref, dwrg_ref, dwig_ref, dbrg_ref, dbig_ref, dlam_ref, dcw_ref, dcb_ref,
             hbuf, abuf, dubuf, c_sc, d_sc, lam_sc, lcar, wacc_r, wacc_i):
        i = pl.program_id(0)
        tt = nt - 1 - i

        @pl.when(i == 0)
        def _():
            for r in (wacc_r, wacc_i, dbrg_ref, dbig_ref, dlam_ref, dcw_ref, dcb_ref):
                r[...] = jnp.zeros_like(r)
            abuf[tl:tl + 8, :] = jnp.zeros((8, D_LRU), F32)
            dubuf[tl:tl + 8, :] = jnp.zeros((8, D_LRU), F32)
            lcar[...] = jnp.zeros((8, D_LRU), F32)

        xu0 = xg_ref[:, 0:D_LRU]
        hsv = hs_ref[...]
        uv = u_ref[...]
        hbuf[8:8 + tl, :] = hsv
        hbuf[0:8, :] = jnp.where(tt > 0, hsp_ref[...], 0.0)
        hshift = hbuf[pl.ds(7, tl), :]
        wrg_v = wrg_ref[...]
        wig_v = wig_ref[...]
        lamv = l_ref[...]
        ub, r, ig, sp, a, mult = _lru_gates(uv, wrg_v, brg_ref[...], wig_v, big_ref[...], lamv)
        abuf[0:tl, :] = a
        c_sc[...] = abuf[pl.ds(1, tl), :]
        gel, dgel = _gelu_and_grad(xg_ref[:, D_LRU:2 * D_LRU])
        drv = drec_ref[...]
        d_sc[...] = drv * gel
        dxg_ref[:, D_LRU:2 * D_LRU] = (drv * hsv * dgel).astype(BF16)

        def grp(k, lnext):
            off = pl.multiple_of((tl // 8 - 1 - k) * 8, 8)
            l8 = _rscan8(c_sc[pl.ds(off, 8), :], d_sc[pl.ds(off, 8), :], lnext)
            lam_sc[pl.ds(off, 8), :] = l8
            return l8[0:1, :]

        lcar[0:1, :] = lax.fori_loop(0, tl // 8, grp, lcar[0:1, :])
        abuf[tl:tl + 8, :] = a[0:8, :]
        db = lam_sc[...]
        da = db * hshift
        dmult = db * (ig * uv)
        dig = db * mult * uv
        du = db * mult * ig
        dla = da * a - dmult * (a * a) / mult
        dlam_ref[...] += jnp.sum(dla * (-LRU_C) * r, axis=0, keepdims=True)
        dzr = dla * (-LRU_C * sp) * r * (1.0 - r)
        dzi = dig * ig * (1.0 - ig)
        dzrb = dzr.astype(BF16)
        dzib = dzi.astype(BF16)
        du = du + _dot_nt(dzrb, wrg_v) + _dot_nt(dzib, wig_v)
        wacc_r[...] += _dot_tn(ub, dzrb)
        wacc_i[...] += _dot_tn(ub, dzib)
        dbrg_ref[...] += jnp.sum(dzr, axis=0, keepdims=True)
        dbig_ref[...] += jnp.sum(dzi, axis=0, keepdims=True)
        dcb_ref[...] += jnp.sum(du, axis=0, keepdims=True)
        dubuf[0:tl, :] = du
        dxu0 = jnp.zeros((tl, D_LRU), F32)
        for j in range(4):
            dsh = dubuf[pl.ds(3 - j, tl), :]
            dxu0 = dxu0 + cw_ref[j:j + 1, :] * dsh
            dcw_ref[j:j + 1, :] += jnp.sum(xu0 * dsh, axis=0, keepdims=True)
        dubuf[tl:tl + 8, :] = du[0:8, :]
        dxg_ref[:, 0:D_LRU] = dxu0.astype(BF16)

        @pl.when(i == nt - 1)
        def _():
            dlam_ref[...] = dlam_ref[...] * (-jax.nn.sigmoid(-lamv))
            for n in range(LRU_BLOCKS):
                blk = slice(n * LRU_BLOCK, (n + 1) * LRU_BLOCK)
                dwrg_ref[n] = wacc_r[blk, blk]
                dwig_ref[n] = wacc_i[blk, blk]

    def rev(n):
        return pl.BlockSpec((tl, n), lambda i: (nt - 1 - i, 0))

    prev8 = pl.BlockSpec((8, D_LRU), lambda i: (jnp.maximum((nt - 1 - i) * (tl // 8) - 1, 0), 0))
    vec = _full((1, D_LRU))
    sq = _full((D_LRU, D_LRU))
    blocks_shape = (LRU_BLOCKS, LRU_BLOCK, LRU_BLOCK)
    blocks = _full(blocks_shape)
    return _call(
        body, "b_lru", (nt,),
        [rev(D_LRU), rev(D_LRU), prev8, rev(D_LRU), rev(1024), _full((4, D_LRU)), sq, vec, sq, vec, vec],
        [rev(1024), blocks, blocks, vec, vec, vec, _full((4, D_LRU)), vec],
        [_sds((s_len, 1024), BF16), _sds(blocks_shape, F32), _sds(blocks_shape, F32),
         _sds((1, D_LRU), F32), _sds((1, D_LRU), F32), _sds((1, D_LRU), F32),
         _sds((4, D_LRU), F32), _sds((1, D_LRU), F32)],
        [pltpu.VMEM((tl + 8, D_LRU), F32)] * 3 + [pltpu.VMEM((tl, D_LRU), F32)] * 3
        + [pltpu.VMEM((8, D_LRU), F32)] + [pltpu.VMEM((D_LRU, D_LRU), F32)] * 2,
        (drec, hs, hs, u, xg, conv_w, wrg, brg, wig, big, lam), "arbitrary", comm)


def _b_attn(qkv_pad, att, datt, frow, comm=None):
    s_len = datt.shape[0]
    nb = s_len // QB
    n_pair = ATT_HEADS // 2
    pair_w = 2 * HEAD_DIM

    def body(q_ref, k0, k1, k2, v0, v1, v2, o_ref, do_ref, frow_ref, dq_ref, dkv_ref, dfrow_ref,
             bias_sc, dt_sc, acc_sc):
        t = pl.program_id(0)

        @pl.when(t == 0)
        def _():
            _bias_table(frow_ref, bias_sc)
            dt_sc[...] = jnp.zeros_like(dt_sc)
            acc_sc[...] = jnp.zeros_like(acc_sc)

        @pl.when(t < nb)
        def _():
            var = jnp.minimum(t, N_BIAS - 1)
            even = _even_lanes()
            for hp in range(n_pair):
                cs = slice(hp * pair_w, (hp + 1) * pair_w)
                qt = q_ref[:, cs]
                kts = [k0[:, cs], k1[:, cs], k2[:, cs]]
                vts = [v0[:, cs], v1[:, cs], v2[:, cs]]
                kcat = jnp.concatenate(kts, axis=0)
                dot = do_ref[:, cs]
                dd = dot * o_ref[:, cs]
                dos_pair, dsbs, pbs, dqs = None, [], [], []
                for e in range(2):
                    keep = even if e == 0 else jnp.logical_not(even)
                    qm = jnp.where(keep, qt, 0)
                    p = _att_probs(qm, kts, bias_sc[var, 2 * hp + e])
                    inv = 1.0 / jnp.sum(p, axis=-1, keepdims=True)
                    dos = jnp.where(keep, dot * inv, 0.0)
                    delta = jnp.sum(jnp.where(keep, dd, 0.0), axis=-1, keepdims=True) * inv
                    dp = jnp.concatenate([_dot_nt(dos.astype(BF16), v) for v in vts], axis=1)
                    ds = p * (dp - delta)
                    dt_sc[2 * hp + e] += ds
                    dsb = ds.astype(BF16)
                    dq = _dot(dsb, kcat)
                    dqs.append(dq)
                    dsbs.append(dsb)
                    pbs.append(p.astype(BF16))
                    dos_pair = dos if e == 0 else dos_pair + dos
                dq_ref[:, cs] = (jnp.where(even, dqs[0], dqs[1]) * ATT_SCALE).astype(BF16)
                qtt = qt.astype(F32).T.astype(BF16)
                dost = dos_pair.T.astype(BF16)
                for j in range(3):
                    slot = (t + 1 + j) % 3
                    js = slice(j * QB, (j + 1) * QB)
                    for e in range(2):
                        hr = slice(e * HEAD_DIM, (e + 1) * HEAD_DIM)
                        acc_sc[slot, hp, hr, :] += _dot(qtt[hr], dsbs[e][:, js])
                        acc_sc[slot, n_pair + hp, hr, :] += _dot(dost[hr], pbs[e][:, js])

        done = (t + 1) % 3

        @pl.when(t >= 2)
        def _():
            for i in range(2 * n_pair):
                dkv_ref[:, i * pair_w:(i + 1) * pair_w] = acc_sc[done, i].T.astype(BF16)

        acc_sc[done] = jnp.zeros((2 * n_pair, pair_w, QB), F32)

        @pl.when(t == nb + 1)
        def _():
            row = lax.broadcasted_iota(jnp.int32, (8, ROLL_W), 0)
            pad = jnp.zeros((8, ROLL_W - KB), F32)
            for h in range(ATT_HEADS):
                acc8 = jnp.concatenate([dt_sc[h, 0:8, :], pad], axis=1)
                for a1 in range(1, QB // 8):
                    blk = jnp.concatenate([dt_sc[h, 8 * a1:8 * a1 + 8, :], pad], axis=1)
                    acc8 = acc8 + pltpu.roll(blk, ROLL_W - 8 * a1, 1)
                for k in range(3):
                    acc8 = jnp.where(((row >> k) & 1) == 1, pltpu.roll(acc8, ROLL_W - (1 << k), 1), acc8)
                dfrow_ref[h:h + 1, :] = jnp.sum(acc8, axis=0, keepdims=True)

    clamp = lambda t: jnp.minimum(t, nb - 1)
    qrows = pl.BlockSpec((QB, D_ATT), lambda t: (clamp(t), 0))
    return _call(
        body, "b_attn", (nb + 2,),
        _att_in_specs(clamp) + [qrows, qrows, _full((ATT_HEADS, ROLL_W))],
        [qrows, pl.BlockSpec((QB, 2 * D_ATT), lambda t: (jnp.maximum(t - 2, 0), 0)),
         _full((ATT_HEADS, ROLL_W))],
        [_sds((s_len, D_ATT), BF16), _sds((s_len, 2 * D_ATT), BF16), _sds((ATT_HEADS, ROLL_W), F32)],
        [pltpu.VMEM((N_BIAS, ATT_HEADS, QB, KB), F32), pltpu.VMEM((ATT_HEADS, QB, KB), F32),
         pltpu.VMEM((3, 2 * n_pair, pair_w, QB), F32)],
        (*([qkv_pad] * 7), att, datt, frow), "arbitrary", comm)


def _flush_grad(steps, acc, accb, out_hbm, outb_hbm):
    @pl.when(pl.program_id(0) == steps - 1)
    def _():
        accb[...] = acc[...].astype(BF16)
        pltpu.sync_copy(acc, out_hbm)
        pltpu.sync_copy(accb, outb_hbm)


def _b_win(dq, dkv, dxg, h, ts):
    s_len = h.shape[0]
    steps = s_len // ts

    def body(dq_ref, dkv_ref, dxg_ref, h_ref, dw_hbm, dwb_hbm, acc, accb):
        @pl.when(pl.program_id(0) == 0)
        def _():
            acc[...] = jnp.zeros_like(acc)

        dproj = jnp.concatenate([dq_ref[...], dkv_ref[...], dxg_ref[...]], axis=1)
        acc[...] += _dot_tn(h_ref[...], dproj)

        @pl.when(pl.program_id(0) == steps - 1)
        def _():
            accb[...] = acc[...].astype(BF16)
            for s in range(N_SHARD):
                cols = pl.ds(s * IN_SH, IN_SH)
                pltpu.sync_copy(acc.at[:, cols], dw_hbm.at[s])
                pltpu.sync_copy(accb.at[:, cols], dwb_hbm.at[s])

    shape = (N_SHARD, 1024, IN_SH)
    return pl.pallas_call(
        body, name="b_win", grid=(steps,),
        in_specs=[_rows(ts, 512), _rows(ts, 1024), _rows(ts, 1024), _rows(ts, 1024)],
        out_specs=[_any()] * 2, out_shape=[_sds(shape, F32), _sds(shape, BF16)],
        scratch_shapes=[pltpu.VMEM((1024, D_IN), F32), pltpu.VMEM((1024, D_IN), BF16)],
        compiler_params=_cp("arbitrary"))(dq, dkv, dxg, h)


def _b_inproj(dq, dkv, dxg, x, dx1, g_mix, w_in_g, tm, comm=None):
    s_len = x.shape[0]

    def body(dq_ref, dkv_ref, dxg_ref, x_ref, dx1_ref, g_ref, w_hbm, gx_ref, dgm_ref, w_ref):
        _load_w_in_once(w_hbm, w_ref)

        @pl.when(pl.program_id(0) == 0)
        def _():
            dgm_ref[...] = jnp.zeros_like(dgm_ref)

        dproj = jnp.concatenate([dq_ref[...], dkv_ref[...], dxg_ref[...]], axis=1)
        dh = _dot_nt(dproj, w_ref[...])
        dx, dgm = _rms_bwd(dh, x_ref[...], g_ref[...])
        gx_ref[...] = dx1_ref[...] + dx
        dgm_ref[...] += dgm

    return _call(
        body, "b_inproj", (s_len // tm,),
        [_rows(tm, 512), _rows(tm, 1024), _rows(tm, 1024), _rows(tm, 1024), _rows(tm, 1024),
         _full((1, 1024)), _any()],
        [_rows(tm, 1024), _full((1, 1024))],
        [_sds((s_len, 1024), F32), _sds((1, 1024), F32)],
        [pltpu.VMEM((1024, D_IN), BF16)], (dq, dkv, dxg, x, dx1, g_mix, w_in_g), "arbitrary", comm)


def _mm_tn(xa, ya, name, ts):
    s_len, k = xa.shape
    n = ya.shape[1]

    steps = s_len // ts

    def body(x_ref, y_ref, o_hbm, ob_hbm, acc, accb):
        @pl.when(pl.program_id(0) == 0)
        def _():
            acc[...] = jnp.zeros_like(acc)
        acc[...] += _dot_tn(x_ref[...].astype(BF16), y_ref[...].astype(BF16))
        _flush_grad(steps, acc, accb, o_hbm, ob_hbm)

    return pl.pallas_call(
        body, name=name, grid=(steps,), in_specs=[_rows(ts, k), _rows(ts, n)],
        out_specs=[_any()] * 2, out_shape=[_sds((k, n), F32), _sds((k, n), BF16)],
        scratch_shapes=[pltpu.VMEM((k, n), F32), pltpu.VMEM((k, n), BF16)],
        compiler_params=_cp("arbitrary"))(xa, ya)


PAD_KEYS = LEFT_CHUNKS * CHUNK
F_HI = PAD_KEYS - MAX_REL + 1
F_LO = PAD_KEYS + MAX_REL


def _frow_from_rel_bias(rb):
    last = rb[:, 2 * MAX_REL:2 * MAX_REL + 1]
    hi = jnp.broadcast_to(last, (ATT_HEADS, F_HI))
    mid = rb[:, 1:2 * MAX_REL][:, ::-1]
    lo = jnp.broadcast_to(rb[:, 0:1], (ATT_HEADS, KB - F_LO))
    wrap = jnp.broadcast_to(last, (ATT_HEADS, ROLL_W - KB))
    return jnp.concatenate([hi, mid, lo, wrap], axis=1)


def _rel_bias_grad_from_dfrow(df):
    g_last = jnp.sum(df[:, 0:F_HI], axis=1, keepdims=True) + jnp.sum(df[:, KB:ROLL_W], axis=1, keepdims=True)
    mid = df[:, F_HI:F_LO][:, ::-1]
    g_first = jnp.sum(df[:, F_LO:KB], axis=1, keepdims=True)
    return jnp.concatenate([g_first, mid, g_last], axis=1)


def _block_diag(w):
    eye = jnp.eye(8, dtype=w.dtype)
    return (w[:, :, None, :] * eye[:, None, :, None]).reshape(D_LRU, D_LRU)


MID = ['w_out', 'wq_c', 'wk_c', 'wv_c', 'wo_c']
TRANSPOSED = ['w_gate', 'w_up']
AG_IN_INPROJ = ['w_out', 'wq_c', 'wk_c']
AG_IN_ATTN = ['wv_c', 'wo_c', 'w_gate']
AG_IN_LRU = ['w_up']
AG_IN_MID = ['w_down']
RS_IN_MID = ['w_gate', 'w_up']
RS_IN_LRU = ['w_down']
RS_IN_ATTN = MID


def _local_step(x, mem, tgt, p, gw, shards=None, chip=None):
    s_len = x.shape[0]
    tm = min(256, s_len)
    tmb = min(512, s_len)
    tl = min(512, s_len)
    frow = _frow_from_rel_bias(p['rel_bias'])
    wrg = _block_diag(p['w_rg']).astype(BF16)
    wig = _block_diag(p['w_ig']).astype(BF16)
    gw = dict(gw)

    big, bigb, recv, part, sib = {}, {}, {}, {}, {}

    def ag(names):
        return [] if shards is None else [("ag", [shards[n] for n in names])]

    def rs(names):
        return [] if shards is None else [("rs", [bigb[n] for n in names])]

    def swap(names):
        return [] if shards is None else [("swap", [part[n] for n in names])]

    def reduce_own(names):
        if shards is not None:
            sums = _sum_parts([big[n] for n in names], [recv[n] for n in names], chip, "sum_" + names[0])
            part.update(zip(names, sums))

    h, qkv_pad, xg, *got = _f_inproj(x, p['g_mix'], gw['w_in'], tmb, ag(AG_IN_INPROJ))
    gw.update(zip(AG_IN_INPROJ, got))
    att, *got = _f_attn(qkv_pad, frow, ag(AG_IN_ATTN))
    gw.update(zip(AG_IN_ATTN, got))
    rec, u, hs, *got = _f_lru(xg, p['conv_w'], p['conv_b'], wrg, p['b_rg'], wig, p['b_ig'], p['lru_L'], tl,
                              ag(AG_IN_LRU))
    gw.update(zip(AG_IN_LRU, got))
    w_out = gw['w_out'].reshape(1024, 1024)
    wq = gw['wq_c'].reshape(1024, 1024)
    wk = gw['wk_c'].reshape(1024, 1024)
    wv = gw['wv_c'].reshape(1024, 1024)
    wo = gw['wo_c'].reshape(1024, 1024)
    mn, kx, vx = _f_mem(mem, p['g_mem'], wk, wv)
    mg, x1, hc, qx, ox, x2, *got = _f_mid(x, att, rec, p['g_out_attn'], p['g_out_lru'], w_out, p['g_cross'],
                                          wq, kx, vx, wo, tmb, ag(AG_IN_MID))
    gw.update(zip(AG_IN_MID, got))
    ffn_w = [gw[n].reshape(D_FF, 1024) for n in ('w_gate', 'w_up', 'w_down')]
    hf, gact, uact, aact, dx3, loss, dg_final = _f_ffn(x2, tgt, p['g_ffn'], p['g_final'], *ffn_w, tmb)

    ts = min(1024, s_len)
    dgact, duact, dx2, dg_ffn = _b_ffn(dx3, x2, gact, uact, p['g_ffn'], *ffn_w, tm)
    big['w_gate'], bigb['w_gate'] = _mm_tn(dgact, hf, "dw_gate", ts)
    big['w_up'], bigb['w_up'] = _mm_tn(duact, hf, "dw_up", ts)
    big['w_down'], bigb['w_down'] = _mm_tn(aact, dx3, "dw_down", ts)
    for n in ('w_gate', 'w_up', 'w_down'):
        big[n] = big[n].reshape(N_SHARD, FF_SH, 1024)
        bigb[n] = bigb[n].reshape(N_SHARD, FF_SH, 1024)

    dqx, dx1, datt, drec, dkx, dvx, dg_cross, dg_oa, dg_ol, *got = _b_mid(
        dx2, qx, x1, att, rec, kx, vx, wo, wq, w_out, p['g_cross'], p['g_out_attn'], p['g_out_lru'], tmb,
        rs(RS_IN_MID))
    recv.update(zip(RS_IN_MID, got))
    reduce_own(RS_IN_MID)
    dwk, dwv, dg_mem, dwkb, dwvb = _b_mem(dkx, dvx, mem, mn, p['g_mem'], wk, wv)
    big['wk_c'], bigb['wk_c'] = dwk, dwkb
    big['wv_c'], bigb['wv_c'] = dwv, dwvb
    big['w_out'], bigb['w_out'] = _mm_tn(mg, dx1, "dw_out", ts)
    big['wq_c'], bigb['wq_c'] = _mm_tn(hc, dqx, "dw_q", ts)
    big['wo_c'], bigb['wo_c'] = _mm_tn(ox, dx2, "dw_o", ts)
    for n in MID:
        big[n] = big[n].reshape(N_SHARD, 256, 1024)
        bigb[n] = bigb[n].reshape(N_SHARD, 256, 1024)

    dxg, dwrg, dwig, dbrg, dbig, dlam, dcw, dcb, *got = _b_lru(
        drec, hs, u, xg, p['conv_w'], wrg, p['b_rg'], wig, p['b_ig'], p['lru_L'], tl,
        rs(RS_IN_LRU) + swap(RS_IN_MID))
    recv.update(zip(RS_IN_LRU, got))
    sib.update(zip(RS_IN_MID, got[len(RS_IN_LRU):]))
    reduce_own(RS_IN_LRU)
    small = {
        'conv_w': dcw, 'conv_b': dcb, 'w_rg': dwrg, 'b_rg': dbrg, 'w_ig': dwig, 'b_ig': dbig, 'lru_L': dlam,
        'g_out_attn': dg_oa, 'g_out_lru': dg_ol, 'g_cross': dg_cross, 'g_mem': dg_mem, 'g_ffn': dg_ffn,
        'g_final': dg_final,
    }
    names = [n for n in SMALL if n in small]
    gather = [] if shards is None else [
        ("ag8", [_pack_small(names, [small[n] for n in names], loss, PACK_ROWS, "pack_small")])]
    dq, dkv, dfrow, *got = _b_attn(qkv_pad, att, datt, frow, rs(RS_IN_ATTN) + swap(RS_IN_LRU) + gather)
    recv.update(zip(RS_IN_ATTN, got))
    sib.update(zip(RS_IN_LRU, got[len(RS_IN_ATTN):]))
    packs = got[-1] if gather else None
    reduce_own(RS_IN_ATTN)
    small['rel_bias'] = _rel_bias_grad_from_dfrow(dfrow)
    big['w_in'], bigb['w_in'] = _b_win(dq, dkv, dxg, h, ts)
    if shards is None:
        grad_x, small['g_mix'] = _b_inproj(dq, dkv, dxg, x, dx1, p['g_mix'], gw['w_in'], tmb)
    else:
        nsw = len(RS_IN_ATTN)

        def copies(refs, send_sems, recv_sems):
            return _tail_copies(refs[0], refs[1], refs[2:2 + nsw], refs[2 + nsw:2 + 2 * nsw], send_sems, recv_sems)

        slots = bigb['w_in']
        bufs = ([slots, lax.empty((3,) + slots.shape[1:], slots.dtype)] + [part[n] for n in RS_IN_ATTN]
                + [lax.empty(part[n].shape, F32) for n in RS_IN_ATTN])
        sems, bufs, token = _split_start("tail_exchange_start", bufs, 3 + nsw, copies)
        grad_x, small['g_mix'] = _b_inproj(dq, dkv, dxg, x, dx1, p['g_mix'] + token[0, 0], gw['w_in'], tmb)
        bufs = _split_wait("tail_exchange_wait", sems, bufs, 3 + nsw, copies, small['g_mix'])
        recv['w_in'] = bufs[1]
        sib.update(zip(RS_IN_ATTN, bufs[2 + nsw:]))
    reduce_own(['w_in'])
    return loss, grad_x, small, big, part, sib, packs


CAST_STEPS = 4


def _cast_shards(ws, name, comm=None):
    def body(*refs):
        n = len(refs) // 2
        for src, dst in zip(refs[:n], refs[n:]):
            dst[...] = src[...].astype(BF16)

    specs = [_rows(w.shape[0] // CAST_STEPS, w.shape[1]) for w in ws]
    return _call(body, name, (CAST_STEPS,), specs, specs, [_sds(w.shape, BF16) for w in ws], [], tuple(ws),
                 "arbitrary", comm)


def _sum_parts(own4s, recv3s, chip, name):
    n = len(own4s)
    _, r, c = own4s[0].shape
    steps = _ew_steps(r, n * c * (4 + 3 * 2 + 4))
    tr = r // steps

    def body(chip_ref, *refs):
        for own_ref, rc_ref, o_ref in zip(refs[:n], refs[n:2 * n], refs[2 * n:]):
            o_ref[...] = ((own_ref[0] + rc_ref[0].astype(F32)) + rc_ref[1].astype(F32)) + rc_ref[2].astype(F32)

    grid_spec = pltpu.PrefetchScalarGridSpec(
        num_scalar_prefetch=1, grid=(steps,),
        in_specs=[pl.BlockSpec((1, tr, c), lambda i, ch: (ch[0], i, 0))] * n
                 + [pl.BlockSpec((3, tr, c), lambda i, ch: (0, i, 0))] * n,
        out_specs=[pl.BlockSpec((tr, c), lambda i, ch: (i, 0))] * n)
    return pl.pallas_call(body, name=name, grid_spec=grid_spec, out_shape=[_sds((r, c), F32)] * n,
                          compiler_params=_cp("parallel"))(chip, *own4s, *recv3s)


def _adamw_math(w, g, m, v):
    m = ADAM_B1 * m + (1.0 - ADAM_B1) * g
    v = ADAM_B2 * v + (1.0 - ADAM_B2) * (g * g)
    m_hat = m / (1.0 - ADAM_B1 ** ADAM_STEP)
    v_hat = v / (1.0 - ADAM_B2 ** ADAM_STEP)
    delta = -ADAM_LR * (m_hat / (jnp.sqrt(v_hat) + ADAM_EPS) + ADAM_WD * w)
    return delta, m, v


def _final_adamw(pas, pbs, ws, ms, vs, name, after=None):
    n = len(ws)
    r, c = ws[0].shape
    steps = _ew_steps(r, n * c * 9 * 4)
    tr = r // steps

    def body(*refs):
        ins, outs = refs[:5 * n], refs[len(refs) - 4 * n:]
        for k in range(n):
            pa_ref, pb_ref, w_ref, m_ref, v_ref = (ins[j * n + k] for j in range(5))
            g = pa_ref[...] + pb_ref[...]
            outs[4 * k][...] = g
            outs[4 * k + 1][...], outs[4 * k + 2][...], outs[4 * k + 3][...] = _adamw_math(
                w_ref[...], g, m_ref[...], v_ref[...])

    order = [] if after is None else [after]
    res = pl.pallas_call(
        body, name=name, grid=(steps,), in_specs=[_rows(tr, c)] * (5 * n) + [_full(t.shape) for t in order],
        out_specs=[_rows(tr, c)] * (4 * n), out_shape=[_sds((r, c), F32)] * (4 * n),
        compiler_params=_cp("parallel"))(*pas, *pbs, *ws, *ms, *vs, *order)
    return [res[4 * k:4 * k + 4] for k in range(n)]


def _pack_put(ref, name, val_ref):
    r = _pack_rows()[name]
    shape = val_ref.shape
    if len(shape) == 3:
        for b in range(shape[0]):
            ref[r:r + shape[1], b * shape[2]:(b + 1) * shape[2]] = val_ref[b]
    elif shape[1] == 2 * PACK_W:
        ref[r:r + 1, :] = val_ref[:, 0:PACK_W]
        ref[r + 1:r + 2, :] = val_ref[:, PACK_W:2 * PACK_W]
    else:
        ref[r:r + shape[0], 0:shape[1]] = val_ref[...]


def _pack_get(ref, name, shape):
    r = _pack_rows()[name]
    if len(shape) == 3:
        return jnp.stack([ref[r:r + shape[1], b * shape[2]:(b + 1) * shape[2]] for b in range(shape[0])])
    if shape[1] == 2 * PACK_W:
        return jnp.concatenate([ref[r:r + 1, :], ref[r + 1:r + 2, :]], axis=1)
    return ref[r:r + shape[0], 0:shape[1]]


def _pack_small(names, g, loss, rows, name):
    n = len(g)
    extra = [] if loss is None else [loss]

    def body(*refs):
        pack = refs[-1]
        pack[...] = jnp.zeros_like(pack)
        for a, nm in enumerate(names):
            _pack_put(pack, nm, refs[a])
        if extra:
            _pack_put(pack, 'loss', refs[n])

    return pl.pallas_call(body, name=name, out_shape=_sds((rows, PACK_W), F32), compiler_params=_cp())(*g, *extra)


def _all_peers():
    x, y, c = _mesh_pos()
    peers = []
    for k in range(1, 8):
        px = 1 - x if k & 4 else x
        py = 1 - y if k & 2 else y
        pc = 1 - c if k & 1 else c
        peers.append(((px, py, pc), 4 * px + 2 * py + pc))
    return peers, 4 * x + 2 * y + c


def _ag8_copies(ins, outs, sems):
    send_sems, recv_sems, loc_sems = sems
    n = len(ins)
    peers, me = _all_peers()

    def remote(k, j, slot):
        return pltpu.make_async_remote_copy(
            src_ref=ins[k], dst_ref=outs[k].at[slot], send_sem=send_sems.at[k, j], recv_sem=recv_sems.at[k, j],
            device_id=peers[j][0], device_id_type=MESH_ID)

    def local(k):
        return pltpu.make_async_copy(ins[k], outs[k].at[me], loc_sems.at[k])

    def start():
        for k in range(n):
            local(k).start()
            for j in range(7):
                remote(k, j, me).start()

    def wait():
        for k in range(n):
            for j in range(7):
                remote(k, j, peers[j][1]).wait_recv()
        for k in range(n):
            for j in range(7):
                remote(k, j, me).wait_send()
            local(k).wait()

    return start, _no_forward, wait


def _adamw_small(packs, late_own, late_packs, g_shapes, loss_shape, w, m, v):
    n = len(w)

    def body(*refs):
        packs_ref, own_ref, late_ref = refs[0], refs[1], refs[2]
        w_refs, m_refs, v_refs = (refs[3 + i * n:3 + (i + 1) * n] for i in range(3))
        o0 = 3 * n + 3
        go, do, mo, vo = (refs[o0 + i * n:o0 + (i + 1) * n] for i in range(4))
        loss_out, tot_ref = refs[o0 + 4 * n], refs[o0 + 4 * n + 1]
        x, y, c = _mesh_pos()
        me = 4 * x + 2 * y + c
        tot = packs_ref[0]
        late = jnp.where(me == 0, own_ref[...], late_ref[0])
        for d in range(1, 8):
            tot = tot + packs_ref[d]
            late = late + jnp.where(me == d, own_ref[...], late_ref[d])
        tot_ref[...] = tot
        tot_ref[0:LATE_ROWS, :] += late
        loss_out[...] = _pack_get(tot_ref, 'loss', loss_shape)
        for a, name in enumerate(SMALL):
            if name == 'conv_w':
                r = _pack_rows()[name]
                ga = tot_ref[r:r + g_shapes[a][0], pl.ds(pl.multiple_of((2 * x + y) * 128, 128), 128)]
            else:
                ga = _pack_get(tot_ref, name, g_shapes[a])
            go[a][...] = ga
            do[a][...], mo[a][...], vo[a][...] = _adamw_math(w_refs[a][...], ga, m_refs[a][...], v_refs[a][...])

    out_shape = [_sds(a.shape, F32) for a in w] * 4 + [_sds(loss_shape, F32)]
    return pl.pallas_call(body, name="adamw_small", out_shape=out_shape,
                          scratch_shapes=[pltpu.VMEM((PACK_ROWS, PACK_W), F32)],
                          compiler_params=_cp())(packs, late_own, late_packs, *w, *m, *v)


PACK_W = 512
PACK_ROWS = 160
LATE = ['g_mix', 'rel_bias']
LATE_ROWS = 32


def _pack_rows():
    rows, r = {}, 0
    for name in ['g_mix', 'g_cross', 'g_mem', 'g_ffn', 'g_final']:
        rows[name] = r
        r += 2
    for name in ['conv_b', 'b_rg', 'b_ig', 'lru_L', 'g_out_attn', 'g_out_lru']:
        rows[name] = r
        r += 1
    rows['conv_w'] = r
    rows['loss'] = r + 4
    rows['rel_bias'] = 24
    rows['w_rg'] = 32
    rows['w_ig'] = 32 + LRU_BLOCK
    assert r + 5 <= 24 and rows['w_ig'] + LRU_BLOCK == PACK_ROWS
    assert rows['g_mix'] + 2 <= LATE_ROWS and rows['rel_bias'] + 8 <= LATE_ROWS
    return rows


INPUT_NAMES = (['x', 'mem'] + WEIGHTS + ['loss_target'] + ['m_' + n for n in WEIGHTS] + ['v_' + n for n in WEIGHTS])


def kernel(x, mem, g_mix, w_in, rel_bias, conv_w, conv_b, w_rg, b_rg, w_ig, b_ig, lru_L, g_out_attn, g_out_lru, w_out, g_cross, g_mem, wq_c, wk_c, wv_c, wo_c, g_ffn, w_gate, w_up, w_down, g_final, loss_target, m_g_mix, m_w_in, m_rel_bias, m_conv_w, m_conv_b, m_w_rg, m_b_rg, m_w_ig, m_b_ig, m_lru_L, m_g_out_attn, m_g_out_lru, m_w_out, m_g_cross, m_g_mem, m_wq_c, m_wk_c, m_wv_c, m_wo_c, m_g_ffn, m_w_gate, m_w_up, m_w_down, m_g_final, v_g_mix, v_w_in, v_rel_bias, v_conv_w, v_conv_b, v_w_rg, v_b_rg, v_w_ig, v_b_ig, v_lru_L, v_g_out_attn, v_g_out_lru, v_w_out, v_g_cross, v_g_mem, v_wq_c, v_wk_c, v_wv_c, v_wo_c, v_g_ffn, v_w_gate, v_w_up, v_w_down, v_g_final):
    a = dict(zip(INPUT_NAMES, (x, mem, g_mix, w_in, rel_bias, conv_w, conv_b, w_rg, b_rg, w_ig, b_ig, lru_L, g_out_attn, g_out_lru, w_out, g_cross, g_mem, wq_c, wk_c, wv_c, wo_c, g_ffn, w_gate, w_up, w_down, g_final, loss_target, m_g_mix, m_w_in, m_rel_bias, m_conv_w, m_conv_b, m_w_rg, m_b_rg, m_w_ig, m_b_ig, m_lru_L, m_g_out_attn, m_g_out_lru, m_w_out, m_g_cross, m_g_mem, m_wq_c, m_wk_c, m_wv_c, m_wo_c, m_g_ffn, m_w_gate, m_w_up, m_w_down, m_g_final, v_g_mix, v_w_in, v_rel_bias, v_conv_w, v_conv_b, v_w_rg, v_b_rg, v_w_ig, v_b_ig, v_lru_L, v_g_out_attn, v_g_out_lru, v_w_out, v_g_cross, v_g_mem, v_wq_c, v_wk_c, v_wv_c, v_wo_c, v_g_ffn, v_w_gate, v_w_up, v_w_down, v_g_final)))
    chip = 2 * lax.axis_index("x") + lax.axis_index("y")

    def shard(name):
        arr = a[name][0]
        base = name[2:] if name[:2] in ('m_', 'v_') else name
        return jnp.swapaxes(arr, 0, 1) if base in TRANSPOSED else arr

    shards = {'w_in': _cast_shards([shard('w_in')], "cast_w_in")[0]}
    rest = [n for n in BIG if n != 'w_in']
    *cast, w_in_g, conv_w_g = _cast_shards([shard(n) for n in rest], "cast_rest",
                                           [("ag", [shards['w_in']]), ("agf", [a['conv_w'][0]])])
    shards.update(zip(rest, cast))
    conv_w_full = conv_w_g.transpose(1, 0, 2).reshape(4, D_LRU)

    p = {n: a[n] for n in SMALL}
    p['rel_bias'] = a['rel_bias'][0]
    p['w_rg'] = a['w_rg'][0]
    p['w_ig'] = a['w_ig'][0]
    p['conv_w'] = conv_w_full
    p['g_final'] = a['g_final'][None, :]
    chip_arr = jnp.reshape(chip, (1,)).astype(jnp.int32)
    loss_part, grad_x, small, _, part, sib, packs = _local_step(
        a['x'][0], a['mem'][0], a['loss_target'][0], p, {'w_in': w_in_g}, shards, chip_arr)

    def late_copies(refs, send_sems, recv_sems):
        return _late_copies(refs[0], refs[1], refs[2], refs[3], send_sems, recv_sems)

    late_pack = _pack_small(LATE, [small[n] for n in LATE], None, LATE_ROWS, "pack_late")
    bufs = [part['w_in'], lax.empty(part['w_in'].shape, F32), late_pack, jnp.zeros((8, LATE_ROWS, PACK_W), F32)]
    sems, bufs, token = _split_start("late_exchange_start", bufs, 8, late_copies)
    out = {}

    def adamw(group, after=None):
        results = _final_adamw([part[n] for n in group], [sib[n] for n in group], [shard(n) for n in group],
                               [shard('m_' + n) for n in group], [shard('v_' + n) for n in group],
                               "adamw_" + group[0], after)
        for n, res in zip(group, results):
            out[n] = [jnp.swapaxes(r, 0, 1) for r in res] if n in TRANSPOSED else res
        return results[-1][0]

    adamw(MID, token)
    done = adamw(['w_gate', 'w_up', 'w_down'], token)
    _, sib['w_in'], late_pack, late_packs = _split_wait("late_exchange_wait", sems, bufs, 8, late_copies, done)
    adamw(['w_in'])

    def natural(arr):
        return arr[0] if arr.ndim >= 3 else (arr[None, :] if arr.ndim == 1 else arr)

    small_out = _adamw_small(packs, late_pack, late_packs, [small[n].shape for n in SMALL],
                             loss_part.shape, *[[natural(a[pre + n]) for n in SMALL] for pre in ('', 'm_', 'v_')])
    ns = len(SMALL)
    loss = small_out[4 * ns][0, 0]

    def leaf(i, n):
        if n in BIG:
            return out[n][i][None]
        return small_out[i * ns + SMALL.index(n)].reshape(a[n].shape)

    return (loss, grad_x[None], *[leaf(i, n) for i in range(4) for n in WEIGHTS])
```

```python
import math

import jax
import jax.numpy as jnp
from jax import lax
from jax.experimental import pallas as pl
from jax.experimental.pallas import tpu as pltpu

F32 = jnp.float32
BF16 = jnp.bfloat16

D_MODEL = 1024
D_ATT = 512
D_LRU = 512
HEAD_DIM = 64
ATT_HEADS = 8
CHUNK = 64
LEFT_CHUNKS = 8
MAX_REL = 128
X_HEADS = 4
X_HEAD_DIM = 256
N_SHARD = 4
IN_SH = 640
D_IN = N_SHARD * IN_SH
FF_SH = 704
D_FF = N_SHARD * FF_SH
EPS = 1e-6
LRU_C = 8.0
LRU_BLOCKS = 8
LRU_BLOCK = 64
QB = 256
KB = 768
ROLL_W = 1024
NEG = -1e30
ATT_SCALE = HEAD_DIM ** -0.5
X_SCALE = X_HEAD_DIM ** -0.5

ADAM_LR = 0.001
ADAM_B1 = 0.9
ADAM_B2 = 0.999
ADAM_EPS = 1e-08
ADAM_WD = 0.01
ADAM_STEP = 10

VMEM_LIMIT_V7X = 56 * 1024 * 1024
BF16_ROWS = 16


EW_VMEM_BUDGET = 40 * 1024 * 1024


def _ew_steps(rows, bytes_per_row):
    return min(s for s in (2, 4, 8, 16) if rows % (s * BF16_ROWS) == 0
               and 2 * (rows // s) * bytes_per_row <= EW_VMEM_BUDGET)
MESH_ID = pl.DeviceIdType.MESH

WEIGHTS = ['g_mix', 'w_in', 'rel_bias', 'conv_w', 'conv_b', 'w_rg', 'b_rg', 'w_ig', 'b_ig', 'lru_L',
           'g_out_attn', 'g_out_lru', 'w_out', 'g_cross', 'g_mem', 'wq_c', 'wk_c', 'wv_c', 'wo_c',
           'g_ffn', 'w_gate', 'w_up', 'w_down', 'g_final']
BIG = ['w_in', 'w_out', 'wq_c', 'wk_c', 'wv_c', 'wo_c', 'w_gate', 'w_up', 'w_down']
SMALL = [n for n in WEIGHTS if n not in BIG]


def _sds(shape, dtype):
    return jax.ShapeDtypeStruct(shape, dtype)


def _cp(*sem):
    return pltpu.CompilerParams(dimension_semantics=sem or None, vmem_limit_bytes=VMEM_LIMIT_V7X)


def _rows(tm, n):
    return pl.BlockSpec((tm, n), lambda i: (i, 0))


def _full(shape):
    nd = len(shape)
    return pl.BlockSpec(shape, lambda i: (0,) * nd)


def _dot(a, b):
    return jnp.dot(a, b, preferred_element_type=F32)


def _dot_nt(a, b):
    return lax.dot_general(a, b, (((1,), (1,)), ((), ())), preferred_element_type=F32)


def _dot_tn(a, b):
    return lax.dot_general(a, b, (((0,), (0,)), ((), ())), preferred_element_type=F32)


def _rinv(x):
    return lax.rsqrt(jnp.mean(x * x, axis=-1, keepdims=True) + EPS)


def _rms_bwd(dy, x, g):
    r = _rinv(x)
    yh = x * r
    dyh = dy * g
    dx = r * (dyh - yh * jnp.mean(dyh * yh, axis=-1, keepdims=True))
    return dx, jnp.sum(dy * yh, axis=0, keepdims=True)


def _gelu(x):
    c = math.sqrt(2.0 / math.pi)
    t = jnp.tanh(c * (x + 0.044715 * x * x * x))
    return 0.5 * x * (1.0 + t)


def _gelu_and_grad(x):
    c = math.sqrt(2.0 / math.pi)
    t = jnp.tanh(c * (x + 0.044715 * x * x * x))
    g = 0.5 * x * (1.0 + t)
    dg = 0.5 * (1.0 + t) + 0.5 * x * (1.0 - t * t) * c * (1.0 + 3.0 * 0.044715 * x * x)
    return g, dg


def _neg_expm1(z):
    series = -z * (1.0 + z * (0.5 + z * ((1.0 / 6.0) + z * (1.0 / 24.0))))
    return jnp.where(z > -0.03, series, 1.0 - jnp.exp(z))


def _lru_gates(u, wrg, brg, wig, big, lam):
    ub = u.astype(BF16)
    r = jax.nn.sigmoid(_dot(ub, wrg) + brg)
    ig = jax.nn.sigmoid(_dot(ub, wig) + big)
    sp = jnp.maximum(-lam, 0.0) + jnp.log1p(jnp.exp(-jnp.abs(lam)))
    la = -LRU_C * r * sp
    a = jnp.exp(la)
    mult = jnp.sqrt(jnp.maximum(_neg_expm1(2.0 * la), 0.0))
    return ub, r, ig, sp, a, mult


def _scan8(a8, b8, hprev):
    row = lax.broadcasted_iota(jnp.int32, a8.shape, 0)
    aa, bb = a8, b8
    for d in (1, 2, 4):
        a_s = pltpu.roll(aa, d, 0)
        b_s = pltpu.roll(bb, d, 0)
        m = row >= d
        bb = jnp.where(m, aa * b_s + bb, bb)
        aa = jnp.where(m, aa * a_s, aa)
    return aa * hprev + bb


def _rscan8(c8, d8, lnext):
    row = lax.broadcasted_iota(jnp.int32, c8.shape, 0)
    cc, dd = c8, d8
    for d in (1, 2, 4):
        c_s = pltpu.roll(cc, 8 - d, 0)
        d_s = pltpu.roll(dd, 8 - d, 0)
        m = row < 8 - d
        dd = jnp.where(m, cc * d_s + dd, dd)
        cc = jnp.where(m, cc * c_s, cc)
    return cc * lnext + dd


def _mesh_pos():
    return lax.axis_index("x"), lax.axis_index("y"), lax.axis_index("c")


def _other_chips(x, y):
    return [(1 - x, y), (x, 1 - y), (1 - x, 1 - y)]


def _no_forward():
    pass


def _ag_full_copies(ins, outs, sems):
    send_sems, recv_sems, loc_sems = sems
    n = len(ins)
    x, y, c = _mesh_pos()
    mine = 2 * x + y
    chips = _other_chips(x, y)

    def remote(k, j, slot):
        px, py = chips[j]
        return pltpu.make_async_remote_copy(
            src_ref=ins[k], dst_ref=outs[k].at[slot], send_sem=send_sems.at[k, j], recv_sem=recv_sems.at[k, j],
            device_id=(px, py, c), device_id_type=MESH_ID)

    def local(k):
        return pltpu.make_async_copy(ins[k], outs[k].at[mine], loc_sems.at[k])

    def start():
        for k in range(n):
            local(k).start()
            for j in range(3):
                remote(k, j, mine).start()

    def wait():
        for k in range(n):
            for j, (px, py) in enumerate(chips):
                remote(k, j, 2 * px + py).wait_recv()
        for k in range(n):
            for j in range(3):
                remote(k, j, mine).wait_send()
            local(k).wait()

    return start, _no_forward, wait


def _ag_copies(ins, outs, sems):
    send_sems, recv_sems, fsend_sems, frecv_sems, loc_sems = sems
    n = len(ins)
    x, y, c = _mesh_pos()
    mine = 2 * x + y
    chips = _other_chips(x, y)

    def half(ref, hc):
        r = ref.shape[0] // 2
        return ref.at[pl.ds(pl.multiple_of(hc * r, 16), r)]

    def ici(k, j, slot):
        px, py = chips[j]
        return pltpu.make_async_remote_copy(
            src_ref=half(ins[k], c), dst_ref=half(outs[k].at[slot], c),
            send_sem=send_sems.at[k, j], recv_sem=recv_sems.at[k, j],
            device_id=(px, py, c), device_id_type=MESH_ID)

    def d2d(k, j, hc):
        px, py = chips[j]
        part = half(outs[k].at[2 * px + py], hc)
        return pltpu.make_async_remote_copy(
            src_ref=part, dst_ref=part, send_sem=fsend_sems.at[k, j], recv_sem=frecv_sems.at[k, j],
            device_id=(x, y, 1 - c), device_id_type=MESH_ID)

    def local(k):
        return pltpu.make_async_copy(ins[k], outs[k].at[mine], loc_sems.at[k])

    def start():
        for k in range(n):
            local(k).start()
            for j in range(3):
                ici(k, j, mine).start()

    def forward():
        for k in range(n):
            for j, (px, py) in enumerate(chips):
                ici(k, j, 2 * px + py).wait_recv()
                d2d(k, j, c).start()

    def wait():
        for k in range(n):
            for j in range(3):
                d2d(k, j, 1 - c).wait_recv()
        for k in range(n):
            for j in range(3):
                d2d(k, j, c).wait_send()
                ici(k, j, mine).wait_send()
            local(k).wait()

    return start, forward, wait


def _rs_copies(ins, outs, sems):
    send_sems, recv_sems = sems
    n = len(ins)
    x, y, c = _mesh_pos()
    chips = _other_chips(x, y)

    def remote(k, j):
        px, py = chips[j]
        return pltpu.make_async_remote_copy(
            src_ref=ins[k].at[2 * px + py], dst_ref=outs[k].at[j],
            send_sem=send_sems.at[k, j], recv_sem=recv_sems.at[k, j],
            device_id=(px, py, c), device_id_type=MESH_ID)

    def start():
        for k in range(n):
            for j in range(3):
                remote(k, j).start()

    def wait():
        for k in range(n):
            for j in range(3):
                remote(k, j).wait_recv()
        for k in range(n):
            for j in range(3):
                remote(k, j).wait_send()

    return start, _no_forward, wait


def _swap_copies(ins, outs, sems):
    send_sems, recv_sems = sems
    x, y, c = _mesh_pos()
    copies = [pltpu.make_async_remote_copy(
        src_ref=ins[k], dst_ref=outs[k], send_sem=send_sems.at[k], recv_sem=recv_sems.at[k],
        device_id=(x, y, 1 - c), device_id_type=MESH_ID) for k in range(len(ins))]

    def start():
        for cp in copies:
            cp.start()

    def wait():
        for cp in copies:
            cp.wait()

    return start, _no_forward, wait


def _comm_plan(groups):
    plan, arrs, shapes, sems = [], [], [], []
    for kind, group in groups:
        k = len(group)
        arrs += group
        per_peer = pltpu.SemaphoreType.DMA((k, 3))
        if kind == "ag":
            shapes += [_sds((N_SHARD,) + w.shape, w.dtype) for w in group]
            gsems = [per_peer] * 4 + [pltpu.SemaphoreType.DMA((k,))]
            maker = _ag_copies
        elif kind == "agf":
            shapes += [_sds((N_SHARD,) + w.shape, w.dtype) for w in group]
            gsems = [per_peer] * 2 + [pltpu.SemaphoreType.DMA((k,))]
            maker = _ag_full_copies
        elif kind == "ag8":
            shapes += [_sds((8,) + g.shape, g.dtype) for g in group]
            gsems = [pltpu.SemaphoreType.DMA((k, 7))] * 2 + [pltpu.SemaphoreType.DMA((k,))]
            maker = _ag8_copies
        elif kind == "rs":
            shapes += [_sds((3,) + g.shape[1:], g.dtype) for g in group]
            gsems = [pltpu.SemaphoreType.DMA((k, 3)), pltpu.SemaphoreType.DMA((k, 3))]
            maker = _rs_copies
        else:
            shapes += [_sds(g.shape, g.dtype) for g in group]
            gsems = [pltpu.SemaphoreType.DMA((k,)), pltpu.SemaphoreType.DMA((k,))]
            maker = _swap_copies
        plan.append((maker, k, len(gsems)))
        sems += gsems
    return plan, arrs, shapes, sems


def _comm_fns(plan, cins, couts, sems):
    fns, a, s = [], 0, 0
    for maker, k, ns in plan:
        fns.append(maker(cins[a:a + k], couts[a:a + k], sems[s:s + ns]))
        a += k
        s += ns

    def start():
        for st, _, _ in fns:
            st()

    def forward():
        for _, fw, _ in fns:
            fw()

    def wait():
        for _, _, wt in fns:
            wt()

    return start, forward, wait


def _call(body, name, grid, in_specs, out_specs, out_shape, scratch, args, sem, comm=None):
    if not comm:
        return pl.pallas_call(body, name=name, grid=grid, in_specs=in_specs, out_specs=out_specs,
                              out_shape=out_shape, scratch_shapes=scratch, compiler_params=_cp(sem))(*args)
    plan, c_arrs, c_shapes, c_sems = _comm_plan(comm)
    k = len(c_arrs)
    n_in, n_out, n_scr = len(in_specs), len(out_specs), len(scratch)
    last = grid[0] - 1
    fwd_step = max(1, (2 * last) // 3)

    def wrapped(*refs):
        ins, cins = refs[:n_in], refs[n_in:n_in + k]
        o0 = n_in + k
        outs, couts = refs[o0:o0 + n_out], refs[o0 + n_out:o0 + n_out + k]
        s0 = o0 + n_out + k
        start, forward, wait = _comm_fns(plan, cins, couts, refs[s0 + n_scr:])
        pl.when(pl.program_id(0) == 0)(start)
        pl.when(pl.program_id(0) == fwd_step)(forward)
        body(*ins, *outs, *refs[s0:s0 + n_scr])
        pl.when(pl.program_id(0) == last)(wait)

    return pl.pallas_call(
        wrapped, name=name, grid=grid, in_specs=list(in_specs) + [_any()] * k,
        out_specs=list(out_specs) + [_any()] * k, out_shape=list(out_shape) + c_shapes,
        scratch_shapes=list(scratch) + c_sems, compiler_params=_cp(sem))(*args, *c_arrs)


def _tail_copies(slots_ref, land_ref, part_refs, sib_refs, send_sems, recv_sems):
    x, y, c = _mesh_pos()
    copies = []
    for j, (px, py) in enumerate(_other_chips(x, y)):
        copies.append(pltpu.make_async_remote_copy(
            src_ref=slots_ref.at[2 * px + py], dst_ref=land_ref.at[j], send_sem=send_sems[j], recv_sem=recv_sems[j],
            device_id=(px, py, c), device_id_type=MESH_ID))
    for k, (p_ref, s_ref) in enumerate(zip(part_refs, sib_refs)):
        copies.append(pltpu.make_async_remote_copy(
            src_ref=p_ref, dst_ref=s_ref, send_sem=send_sems[3 + k], recv_sem=recv_sems[3 + k],
            device_id=(x, y, 1 - c), device_id_type=MESH_ID))
    return copies


def _late_copies(part_ref, sib_ref, pack_ref, packs_ref, send_sems, recv_sems):
    x, y, c = _mesh_pos()
    peers, me = _all_peers()
    copies = [pltpu.make_async_remote_copy(
        src_ref=part_ref, dst_ref=sib_ref, send_sem=send_sems[0], recv_sem=recv_sems[0],
        device_id=(x, y, 1 - c), device_id_type=MESH_ID)]
    for j in range(7):
        copies.append(pltpu.make_async_remote_copy(
            src_ref=pack_ref, dst_ref=packs_ref.at[me], send_sem=send_sems[1 + j], recv_sem=recv_sems[1 + j],
            device_id=peers[j][0], device_id_type=MESH_ID))
    return copies


def _split_start(name, bufs, ncp, make_copies):
    hbm = pl.BlockSpec(memory_space=pltpu.HBM)
    sem = pl.BlockSpec(memory_space=pltpu.SEMAPHORE)
    bufs = [pltpu.with_memory_space_constraint(b, pltpu.HBM) for b in bufs]
    nb = len(bufs)

    def body(*refs):
        for cp in make_copies(refs[:nb], refs[nb:nb + ncp], refs[nb + ncp:nb + 2 * ncp]):
            cp.start()
        refs[-1][...] = jnp.zeros_like(refs[-1])

    out = pl.pallas_call(
        body, name=name,
        out_shape=[pltpu.SemaphoreType.DMA(())] * (2 * ncp) + [pltpu.HBM(b.shape, b.dtype) for b in bufs]
                  + [_sds((8, 128), F32)],
        in_specs=[hbm] * nb, out_specs=[sem] * (2 * ncp) + [hbm] * nb + [pl.BlockSpec(memory_space=pltpu.VMEM)],
        input_output_aliases={i: 2 * ncp + i for i in range(nb)},
        compiler_params=pltpu.CompilerParams(has_side_effects=pltpu.SideEffectType.DATAFLOW_SIDE_EFFECTING),
    )(*bufs)
    return out[:2 * ncp], out[2 * ncp:2 * ncp + nb], out[-1]


def _split_wait(name, sems, bufs, ncp, make_copies, after):
    nb = len(bufs)
    hbm = pl.BlockSpec(memory_space=pltpu.HBM)
    sem = pl.BlockSpec(memory_space=pltpu.SEMAPHORE)

    def body(*refs):
        for cp in make_copies(refs[:nb], refs[nb:nb + ncp], refs[nb + ncp:nb + 2 * ncp]):
            cp.wait_send()
            cp.wait_recv()

    return pl.pallas_call(
        body, name=name, out_shape=[pltpu.HBM(b.shape, b.dtype) for b in bufs],
        in_specs=[hbm] * nb + [sem] * (2 * ncp) + [_any()], out_specs=[hbm] * nb,
        input_output_aliases={i: i for i in range(nb)},
        compiler_params=pltpu.CompilerParams(has_side_effects=pltpu.SideEffectType.DATAFLOW_SIDE_EFFECTING),
    )(*bufs, *sems, after)


def _any():
    return pl.BlockSpec(memory_space=pl.ANY)


def _load_w_in_once(w_hbm, w_ref):
    @pl.when(pl.program_id(0) == 0)
    def _():
        for s in range(N_SHARD):
            pltpu.sync_copy(w_hbm.at[s], w_ref.at[:, pl.ds(s * IN_SH, IN_SH)])


def _f_inproj(x, g_mix, w_in_g, tm, comm=None):
    s_len = x.shape[0]
    pad_rows = LEFT_CHUNKS * CHUNK
    npad = pad_rows // tm

    def body(x_ref, g_ref, w_hbm, h_ref, qkv_ref, xg_ref, w_ref):
        i = pl.program_id(0)
        _load_w_in_once(w_hbm, w_ref)

        @pl.when(i < npad)
        def _():
            qkv_ref[...] = jnp.zeros_like(qkv_ref)

        @pl.when(i >= npad)
        def _():
            xv = x_ref[...]
            h = (xv * _rinv(xv) * g_ref[...]).astype(BF16)
            h_ref[...] = h
            proj = _dot(h, w_ref[...])
            qkv_ref[:, 0:D_ATT] = (proj[:, 0:D_ATT] * ATT_SCALE).astype(BF16)
            qkv_ref[:, D_ATT:3 * D_ATT] = proj[:, D_ATT:3 * D_ATT].astype(BF16)
            xg_ref[...] = proj[:, 3 * D_ATT:D_IN]

    def tok(n):
        return pl.BlockSpec((tm, n), lambda i: (jnp.maximum(i - npad, 0), 0))

    return _call(
        body, "f_inproj", (s_len // tm + npad,),
        [tok(1024), _full((1, 1024)), _any()],
        [tok(1024), _rows(tm, 1536), tok(1024)],
        [_sds((s_len, 1024), BF16), _sds((s_len + pad_rows, 1536), BF16), _sds((s_len, 1024), F32)],
        [pltpu.VMEM((1024, D_IN), BF16)], (x, g_mix, w_in_g), "arbitrary", comm)


N_BIAS = 3


def _bias_table(frow_ref, bias_sc):
    qa = lax.broadcasted_iota(jnp.int32, (QB, KB), 0) // CHUNK
    kcol = lax.broadcasted_iota(jnp.int32, (QB, KB), 1)
    kb = kcol // CHUNK
    band = jnp.where((kb >= qa) & (kb - qa <= LEFT_CHUNKS), 0.0, NEG).astype(F32)
    for h in range(ATT_HEADS):
        row = jnp.broadcast_to(frow_ref[h:h + 1, :], (QB, ROLL_W))
        toep = pltpu.roll(row, 0, 1, stride=1, stride_axis=0)
        gen = toep[:, 0:KB] + band
        bias_sc[N_BIAS - 1, h] = gen
        for v in range(N_BIAS - 1):
            pad_keys = LEFT_CHUNKS * CHUNK - v * QB
            bias_sc[v, h] = gen + jnp.where(kcol < pad_keys, NEG, 0.0).astype(F32)


def _even_lanes():
    return lax.broadcasted_iota(jnp.int32, (1, 2 * HEAD_DIM), 1) < HEAD_DIM


def _att_probs(qm, kts, bias):
    s = jnp.concatenate([_dot_nt(qm, k) for k in kts], axis=1) + bias
    return jnp.exp(s - jnp.max(s, axis=-1, keepdims=True))


def _att_in_specs(clamp):
    def spec(j, col):
        return pl.BlockSpec((QB, D_ATT), lambda i: (clamp(i) + j, col))
    return [spec(2, 0), spec(0, 1), spec(1, 1), spec(2, 1), spec(0, 2), spec(1, 2), spec(2, 2)]


def _f_attn(qkv_pad, frow, comm=None):
    s_len = qkv_pad.shape[0] - LEFT_CHUNKS * CHUNK
    nb = s_len // QB

    def body(q_ref, k0, k1, k2, v0, v1, v2, frow_ref, o_ref, bias_sc):
        i = pl.program_id(0)

        @pl.when(i == 0)
        def _():
            _bias_table(frow_ref, bias_sc)

        var = jnp.minimum(i, N_BIAS - 1)
        even = _even_lanes()
        for hp in range(ATT_HEADS // 2):
            cs = slice(hp * 2 * HEAD_DIM, (hp + 1) * 2 * HEAD_DIM)
            qt = q_ref[:, cs]
            kts = [k0[:, cs], k1[:, cs], k2[:, cs]]
            vts = [v0[:, cs], v1[:, cs], v2[:, cs]]
            res = []
            for e in range(2):
                keep = even if e == 0 else jnp.logical_not(even)
                pb = _att_probs(jnp.where(keep, qt, 0), kts, bias_sc[var, 2 * hp + e]).astype(BF16)
                r = _dot(pb, jnp.concatenate([jnp.where(keep, v, 1) for v in vts], axis=0))
                res.append(r / pltpu.roll(r, HEAD_DIM, 1))
            o_ref[:, cs] = jnp.where(even, res[0], res[1])

    return _call(
        body, "f_attn", (nb,),
        _att_in_specs(lambda i: i) + [_full((ATT_HEADS, ROLL_W))],
        [_rows(QB, D_ATT)], [_sds((s_len, D_ATT), F32)],
        [pltpu.VMEM((N_BIAS, ATT_HEADS, QB, KB), F32)], (*([qkv_pad] * 7), frow), "arbitrary", comm)


def _f_lru(xg, conv_w, conv_b, wrg, brg, wig, big, lam, tl, comm=None):
    s_len = xg.shape[0]

    def body(xg_ref, cw_ref, cb_ref, wrg_ref, brg_ref, wig_ref, big_ref, l_ref,
             rec_ref, u_ref, hs_ref, xbuf, a_sc, b_sc, hcar):
        i = pl.program_id(0)

        @pl.when(i == 0)
        def _():
            xbuf[0:8, :] = jnp.zeros((8, D_LRU), F32)
            hcar[...] = jnp.zeros((8, D_LRU), F32)

        xu0 = xg_ref[:, 0:D_LRU]
        xbuf[8:8 + tl, :] = xu0
        u = cb_ref[...] + cw_ref[0:1, :] * xbuf[pl.ds(5, tl), :]
        for j in range(1, 4):
            u = u + cw_ref[j:j + 1, :] * xbuf[pl.ds(5 + j, tl), :]
        xbuf[0:8, :] = xu0[tl - 8:tl, :]
        u_ref[...] = u
        _, _, ig, _, a, mult = _lru_gates(u, wrg_ref[...], brg_ref[...], wig_ref[...], big_ref[...], l_ref[...])
        a_sc[...] = a
        b_sc[...] = mult * (ig * u)

        def grp(g, hprev):
            off = pl.multiple_of(g * 8, 8)
            h8 = _scan8(a_sc[pl.ds(off, 8), :], b_sc[pl.ds(off, 8), :], hprev)
            hs_ref[pl.ds(off, 8), :] = h8
            return h8[7:8, :]

        hcar[0:1, :] = lax.fori_loop(0, tl // 8, grp, hcar[0:1, :])
        rec_ref[...] = hs_ref[...] * _gelu(xg_ref[:, D_LRU:2 * D_LRU])

    vec = _full((1, D_LRU))
    return _call(
        body, "f_lru", (s_len // tl,),
        [_rows(tl, 1024), _full((4, D_LRU)), vec, _full((D_LRU, D_LRU)), vec, _full((D_LRU, D_LRU)), vec, vec],
        [_rows(tl, D_LRU)] * 3, [_sds((s_len, D_LRU), F32)] * 3,
        [pltpu.VMEM((tl + 8, D_LRU), F32), pltpu.VMEM((tl, D_LRU), F32),
         pltpu.VMEM((tl, D_LRU), F32), pltpu.VMEM((8, D_LRU), F32)],
        (xg, conv_w, conv_b, wrg, brg, wig, big, lam), "arbitrary", comm)


def _f_mem(mem, g_mem, wk, wv):
    def body(mem_ref, g_ref, wk_ref, wv_ref, mn_ref, kx_ref, vx_ref):
        mv = mem_ref[...]
        mn = (mv * _rinv(mv) * g_ref[...]).astype(BF16)
        mn_ref[...] = mn
        kx_ref[...] = _dot(mn, wk_ref[...]).astype(BF16)
        vx_ref[...] = _dot(mn, wv_ref[...]).astype(BF16)

    m = mem.shape[0]
    return pl.pallas_call(
        body, name="f_mem", out_shape=[_sds((m, 1024), BF16)] * 3,
        compiler_params=_cp())(mem, g_mem, wk, wv)


def _xattn_probs(q, k):
    s = _dot_nt(q, k) * X_SCALE
    m = jnp.max(s, axis=-1, keepdims=True)
    p = jnp.exp(s - m)
    return p, jnp.sum(p, axis=-1, keepdims=True)


def _f_mid(x, att, rec, g_oa, g_ol, w_out, g_cross, wq, kx, vx, wo, tm, comm=None):
    s_len = x.shape[0]
    m_len = kx.shape[0]

    def body(x_ref, att_ref, rec_ref, goa_ref, gol_ref, wout_ref, gc_ref, wq_ref, kx_ref, vx_ref, wo_ref,
             mg_ref, x1_ref, hc_ref, qx_ref, ox_ref, x2_ref):
        av = att_ref[...]
        rv = rec_ref[...]
        mg_ref[:, 0:D_ATT] = (av * _rinv(av) * goa_ref[...]).astype(BF16)
        mg_ref[:, D_ATT:1024] = (rv * _rinv(rv) * gol_ref[...]).astype(BF16)
        x1 = x_ref[...] + _dot(mg_ref[...], wout_ref[...])
        x1_ref[...] = x1
        hc = (x1 * _rinv(x1) * gc_ref[...]).astype(BF16)
        hc_ref[...] = hc
        qx_ref[...] = _dot(hc, wq_ref[...]).astype(BF16)
        for h in range(X_HEADS):
            sl = slice(h * X_HEAD_DIM, (h + 1) * X_HEAD_DIM)
            p, l = _xattn_probs(qx_ref[:, sl], kx_ref[:, sl])
            ox_ref[:, sl] = (_dot(p.astype(BF16), vx_ref[:, sl]) / l).astype(BF16)
        x2_ref[...] = x1 + _dot(ox_ref[...], wo_ref[...])

    sq = _full((1024, 1024))
    return _call(
        body, "f_mid", (s_len // tm,),
        [_rows(tm, 1024), _rows(tm, 512), _rows(tm, 512), _full((1, 512)), _full((1, 512)), sq,
         _full((1, 1024)), sq, _full((m_len, 1024)), _full((m_len, 1024)), sq],
        [_rows(tm, 1024)] * 6,
        [_sds((s_len, 1024), BF16), _sds((s_len, 1024), F32), _sds((s_len, 1024), BF16),
         _sds((s_len, 1024), BF16), _sds((s_len, 1024), BF16), _sds((s_len, 1024), F32)],
        [], (x, att, rec, g_oa, g_ol, w_out, g_cross, wq, kx, vx, wo), "arbitrary", comm)


def _load_weights_once(pairs):
    @pl.when(pl.program_id(0) == 0)
    def _():
        for hbm, vmem in pairs:
            pltpu.sync_copy(hbm, vmem)


FF_CHUNKS = [(0, 1280), (1280, D_FF)]
FF_ROWS = 256


def _f_ffn(x2, tgt, g_ffn, g_final, wg, wu, wd, tm):
    s_len = x2.shape[0]

    def body(x2_ref, t_ref, gf_ref, gfin_ref, wg_hbm, wu_hbm, wd_hbm,
             hf_ref, g_ref, u_ref, a_ref, dx3_ref, loss_ref, dgfin_ref, wg_ref, wu_ref, wd_ref):
        _load_weights_once([(wg_hbm, wg_ref), (wu_hbm, wu_ref), (wd_hbm, wd_ref)])

        @pl.when(pl.program_id(0) == 0)
        def _():
            loss_ref[...] = jnp.zeros_like(loss_ref)
            dgfin_ref[...] = jnp.zeros_like(dgfin_ref)

        gfin = gfin_ref[...]
        for r0 in range(0, tm, FF_ROWS):
            rs = slice(r0, r0 + FF_ROWS)
            x2v = x2_ref[rs, :]
            hf = (x2v * _rinv(x2v) * gf_ref[...]).astype(BF16)
            hf_ref[rs, :] = hf
            x3 = x2v
            for c0, c1 in FF_CHUNKS:
                gv = _dot_nt(hf, wg_ref[c0:c1, :])
                uv = _dot_nt(hf, wu_ref[c0:c1, :])
                av = (gv * jax.nn.sigmoid(gv) * uv).astype(BF16)
                g_ref[rs, c0:c1] = gv.astype(BF16)
                u_ref[rs, c0:c1] = uv.astype(BF16)
                a_ref[rs, c0:c1] = av
                x3 = x3 + _dot(av, wd_ref[c0:c1, :])
            r3 = _rinv(x3)
            yh = x3 * r3
            err = yh * gfin - t_ref[rs, :]
            loss_ref[...] += jnp.full((1, 128), 0.5 / D_MODEL, F32) * jnp.sum(err * err)
            dy = err * (1.0 / D_MODEL)
            dgfin_ref[...] += jnp.sum(dy * yh, axis=0, keepdims=True)
            dyh = dy * gfin
            dx3_ref[rs, :] = r3 * (dyh - yh * jnp.mean(dyh * yh, axis=-1, keepdims=True))

    vec = _full((1, 1024))
    return pl.pallas_call(
        body, name="f_ffn", grid=(s_len // tm,),
        in_specs=[_rows(tm, 1024), _rows(tm, 1024), vec, vec, _any(), _any(), _any()],
        out_specs=[_rows(tm, 1024), _rows(tm, D_FF), _rows(tm, D_FF), _rows(tm, D_FF),
                   _rows(tm, 1024), _full((1, 128)), vec],
        out_shape=[_sds((s_len, 1024), BF16)] + [_sds((s_len, D_FF), BF16)] * 3
                  + [_sds((s_len, 1024), F32), _sds((1, 128), F32), _sds((1, 1024), F32)],
        scratch_shapes=[pltpu.VMEM((D_FF, 1024), BF16)] * 3,
        compiler_params=_cp("arbitrary"))(x2, tgt, g_ffn, g_final, wg, wu, wd)


def _b_ffn(dx3, x2, gact, uact, g_ffn, wg, wu, wd, tm):
    s_len = x2.shape[0]

    def body(dx3_ref, x2_ref, g_ref, u_ref, gf_ref, wg_hbm, wu_hbm, wd_hbm,
             dg_ref, du_ref, dx2_ref, dgf_ref, wg_ref, wu_ref, wd_ref):
        _load_weights_once([(wg_hbm, wg_ref), (wu_hbm, wu_ref), (wd_hbm, wd_ref)])

        @pl.when(pl.program_id(0) == 0)
        def _():
            dgf_ref[...] = jnp.zeros_like(dgf_ref)

        dx3v = dx3_ref[...]
        dx3b = dx3v.astype(BF16)
        dhf = jnp.zeros(dx3v.shape, F32)
        for c0, c1 in FF_CHUNKS:
            da = _dot_nt(dx3b, wd_ref[c0:c1, :])
            gv = g_ref[:, c0:c1].astype(F32)
            uv = u_ref[:, c0:c1].astype(F32)
            sg = jax.nn.sigmoid(gv)
            dub = (da * gv * sg).astype(BF16)
            dgb = (da * uv * (sg * (1.0 + gv * (1.0 - sg)))).astype(BF16)
            du_ref[:, c0:c1] = dub
            dg_ref[:, c0:c1] = dgb
            dhf = dhf + _dot(dgb, wg_ref[c0:c1, :]) + _dot(dub, wu_ref[c0:c1, :])
        dx, dgf = _rms_bwd(dhf, x2_ref[...], gf_ref[...])
        dx2_ref[...] = dx3v + dx
        dgf_ref[...] += dgf

    vec = _full((1, 1024))
    return pl.pallas_call(
        body, name="b_ffn", grid=(s_len // tm,),
        in_specs=[_rows(tm, 1024), _rows(tm, 1024), _rows(tm, D_FF), _rows(tm, D_FF), vec,
                  _any(), _any(), _any()],
        out_specs=[_rows(tm, D_FF), _rows(tm, D_FF), _rows(tm, 1024), vec],
        out_shape=[_sds((s_len, D_FF), BF16)] * 2 + [_sds((s_len, 1024), F32), _sds((1, 1024), F32)],
        scratch_shapes=[pltpu.VMEM((D_FF, 1024), BF16)] * 3,
        compiler_params=_cp("arbitrary"))(dx3, x2, gact, uact, g_ffn, wg, wu, wd)


def _b_mid(dx2, qx, x1, att, rec, kx, vx, wo, wq, w_out, g_cross, g_oa, g_ol, tm, comm=None):
    s_len = x1.shape[0]
    m_len = kx.shape[0]

    def body(dx2_ref, qx_ref, x1_ref, att_ref, rec_ref, kx_ref, vx_ref, wo_ref, wq_ref, wout_ref,
             gc_ref, goa_ref, gol_ref,
             dqx_ref, dx1_ref, datt_ref, drec_ref, dkx_ref, dvx_ref, dgc_ref, dgoa_ref, dgol_ref):
        @pl.when(pl.program_id(0) == 0)
        def _():
            for r in (dkx_ref, dvx_ref, dgc_ref, dgoa_ref, dgol_ref):
                r[...] = jnp.zeros_like(r)

        dx2v = dx2_ref[...]
        dox = _dot_nt(dx2v.astype(BF16), wo_ref[...])
        for h in range(X_HEADS):
            sl = slice(h * X_HEAD_DIM, (h + 1) * X_HEAD_DIM)
            q = qx_ref[:, sl]
            p, l = _xattn_probs(q, kx_ref[:, sl])
            pn = p * (1.0 / l)
            dob = dox[:, sl].astype(BF16)
            dp = _dot_nt(dob, vx_ref[:, sl])
            dvx_ref[:, sl] += _dot_tn(pn.astype(BF16), dob)
            ds = pn * (dp - jnp.sum(dp * pn, axis=-1, keepdims=True))
            dsb = (ds * X_SCALE).astype(BF16)
            dqx_ref[:, sl] = _dot(dsb, kx_ref[:, sl]).astype(BF16)
            dkx_ref[:, sl] += _dot_tn(dsb, q)
        dhc = _dot_nt(dqx_ref[...], wq_ref[...])
        dx, dgc = _rms_bwd(dhc, x1_ref[...], gc_ref[...])
        dx1 = dx2v + dx
        dx1_ref[...] = dx1
        dgc_ref[...] += dgc
        dmg = _dot_nt(dx1.astype(BF16), wout_ref[...])
        da, dgoa = _rms_bwd(dmg[:, 0:D_ATT], att_ref[...], goa_ref[...])
        datt_ref[...] = da
        dgoa_ref[...] += dgoa
        dr, dgol = _rms_bwd(dmg[:, D_ATT:1024], rec_ref[...], gol_ref[...])
        drec_ref[...] = dr
        dgol_ref[...] += dgol

    sq = _full((1024, 1024))
    mk = _full((m_len, 1024))
    return _call(
        body, "b_mid", (s_len // tm,),
        [_rows(tm, 1024), _rows(tm, 1024), _rows(tm, 1024), _rows(tm, 512), _rows(tm, 512), mk, mk,
         sq, sq, sq, _full((1, 1024)), _full((1, 512)), _full((1, 512))],
        [_rows(tm, 1024), _rows(tm, 1024), _rows(tm, 512), _rows(tm, 512), mk, mk,
         _full((1, 1024)), _full((1, 512)), _full((1, 512))],
        [_sds((s_len, 1024), BF16), _sds((s_len, 1024), F32), _sds((s_len, 512), F32),
         _sds((s_len, 512), F32), _sds((m_len, 1024), F32), _sds((m_len, 1024), F32),
         _sds((1, 1024), F32), _sds((1, 512), F32), _sds((1, 512), F32)],
        [], (dx2, qx, x1, att, rec, kx, vx, wo, wq, w_out, g_cross, g_oa, g_ol), "arbitrary", comm)


def _b_mem(dkx, dvx, mem, mn, g_mem, wk, wv):
    def body(dkx_ref, dvx_ref, mem_ref, mn_ref, g_ref, wk_ref, wv_ref, dwk_ref, dwv_ref, dgm_ref,
             dwkb_ref, dwvb_ref):
        dkb = dkx_ref[...].astype(BF16)
        dvb = dvx_ref[...].astype(BF16)
        dwk = _dot_tn(mn_ref[...], dkb)
        dwv = _dot_tn(mn_ref[...], dvb)
        dwk_ref[...] = dwk
        dwv_ref[...] = dwv
        dwkb_ref[...] = dwk.astype(BF16)
        dwvb_ref[...] = dwv.astype(BF16)
        dmn = _dot_nt(dkb, wk_ref[...]) + _dot_nt(dvb, wv_ref[...])
        mv = mem_ref[...]
        dgm_ref[...] = jnp.sum(dmn * (mv * _rinv(mv)), axis=0, keepdims=True)

    return pl.pallas_call(
        body, name="b_mem",
        out_shape=[_sds((1024, 1024), F32), _sds((1024, 1024), F32), _sds((1, 1024), F32),
                   _sds((1024, 1024), BF16), _sds((1024, 1024), BF16)],
        compiler_params=_cp())(dkx, dvx, mem, mn, g_mem, wk, wv)


def _b_lru(drec, hs, u, xg, conv_w, wrg, brg, wig, big, lam, tl, comm=None):
    s_len = xg.shape[0]
    nt = s_len // tl

    def body(drec_ref, hs_ref, hsp_ref, u_ref, xg_ref, cw_ref, wrg_ref, brg_ref, wig_ref, big_ref, l_ref,
             dxg_ref, dwrg_ref, dwig_ref, dbrg_ref, dbig_ref, dlam_ref, dcw_ref, dcb_ref,
             hbuf, abuf, dubuf, c_sc, d_sc, lam_sc, lcar, wacc_r, wacc_i):
        i = pl.program_id(0)
        tt = nt - 1 - i

        @pl.when(i == 0)
        def _():
            for r in (wacc_r, wacc_i, dbrg_ref, dbig_ref, dlam_ref, dcw_ref, dcb_ref):
                r[...] = jnp.zeros_like(r)
            abuf[tl:tl + 8, :] = jnp.zeros((8, D_LRU), F32)
            dubuf[tl:tl + 8, :] = jnp.zeros((8, D_LRU), F32)
            lcar[...] = jnp.zeros((8, D_LRU), F32)

        xu0 = xg_ref[:, 0:D_LRU]
        hsv = hs_ref[...]
        uv = u_ref[...]
        hbuf[8:8 + tl, :] = hsv
        hbuf[0:8, :] = jnp.where(tt > 0, hsp_ref[...], 0.0)
        hshift = hbuf[pl.ds(7, tl), :]
        wrg_v = wrg_ref[...]
        wig_v = wig_ref[...]
        lamv = l_ref[...]
        ub, r, ig, sp, a, mult = _lru_gates(uv, wrg_v, brg_ref[...], wig_v, big_ref[...], lamv)
        abuf[0:tl, :] = a
        c_sc[...] = abuf[pl.ds(1, tl), :]
        gel, dgel = _gelu_and_grad(xg_ref[:, D_LRU:2 * D_LRU])
        drv = drec_ref[...]
        d_sc[...] = drv * gel
        dxg_ref[:, D_LRU:2 * D_LRU] = (drv * hsv * dgel).astype(BF16)

        def grp(k, lnext):
            off = pl.multiple_of((tl // 8 - 1 - k) * 8, 8)
            l8 = _rscan8(c_sc[pl.ds(off, 8), :], d_sc[pl.ds(off, 8), :], lnext)
            lam_sc[pl.ds(off, 8), :] = l8
            return l8[0:1, :]

        lcar[0:1, :] = lax.fori_loop(0, tl // 8, grp, lcar[0:1, :])
        abuf[tl:tl + 8, :] = a[0:8, :]
        db = lam_sc[...]
        da = db * hshift
        dmult = db * (ig * uv)
        dig = db * mult * uv
        du = db * mult * ig
        dla = da * a - dmult * (a * a) / mult
        dlam_ref[...] += jnp.sum(dla * (-LRU_C) * r, axis=0, keepdims=True)
        dzr = dla * (-LRU_C * sp) * r * (1.0 - r)
        dzi = dig * ig * (1.0 - ig)
        dzrb = dzr.astype(BF16)
        dzib = dzi.astype(BF16)
        du = du + _dot_nt(dzrb, wrg_v) + _dot_nt(dzib, wig_v)
        wacc_r[...] += _dot_tn(ub, dzrb)
        wacc_i[...] += _dot_tn(ub, dzib)
        dbrg_ref[...] += jnp.sum(dzr, axis=0, keepdims=True)
        dbig_ref[...] += jnp.sum(dzi, axis=0, keepdims=True)
        dcb_ref[...] += jnp.sum(du, axis=0, keepdims=True)
        dubuf[0:tl, :] = du
        dxu0 = jnp.zeros((tl, D_LRU), F32)
        for j in range(4):
            dsh = dubuf[pl.ds(3 - j, tl), :]
            dxu0 = dxu0 + cw_ref[j:j + 1, :] * dsh
            dcw_ref[j:j + 1, :] += jnp.sum(xu0 * dsh, axis=0, keepdims=True)
        dubuf[tl:tl + 8, :] = du[0:8, :]
        dxg_ref[:, 0:D_LRU] = dxu0.astype(BF16)

        @pl.when(i == nt - 1)
        def _():
            dlam_ref[...] = dlam_ref[...] * (-jax.nn.sigmoid(-lamv))
            for n in range(LRU_BLOCKS):
                blk = slice(n * LRU_BLOCK, (n + 1) * LRU_BLOCK)
                dwrg_ref[n] = wacc_r[blk, blk]
                dwig_ref[n] = wacc_i[blk, blk]

    def rev(n):
        return pl.BlockSpec((tl, n), lambda i: (nt - 1 - i, 0))

    prev8 = pl.BlockSpec((8, D_LRU), lambda i: (jnp.maximum((nt - 1 - i) * (tl // 8) - 1, 0), 0))
    vec = _full((1, D_LRU))
    sq = _full((D_LRU, D_LRU))
    blocks_shape = (LRU_BLOCKS, LRU_BLOCK, LRU_BLOCK)
    blocks = _full(blocks_shape)
    return _call(
        body, "b_lru", (nt,),
        [rev(D_LRU), rev(D_LRU), prev8, rev(D_LRU), rev(1024), _full((4, D_LRU)), sq, vec, sq, vec, vec],
        [rev(1024), blocks, blocks, vec, vec, vec, _full((4, D_LRU)), vec],
        [_sds((s_len, 1024), BF16), _sds(blocks_shape, F32), _sds(blocks_shape, F32),
         _sds((1, D_LRU), F32), _sds((1, D_LRU), F32), _sds((1, D_LRU), F32),
         _sds((4, D_LRU), F32), _sds((1, D_LRU), F32)],
        [pltpu.VMEM((tl + 8, D_LRU), F32)] * 3 + [pltpu.VMEM((tl, D_LRU), F32)] * 3
        + [pltpu.VMEM((8, D_LRU), F32)] + [pltpu.VMEM((D_LRU, D_LRU), F32)] * 2,
        (drec, hs, hs, u, xg, conv_w, wrg, brg, wig, big, lam), "arbitrary", comm)


def _b_attn(qkv_pad, att, datt, frow, comm=None):
    s_len = datt.shape[0]
    nb = s_len // QB
    n_pair = ATT_HEADS // 2
    pair_w = 2 * HEAD_DIM

    def body(q_ref, k0, k1, k2, v0, v1, v2, o_ref, do_ref, frow_ref, dq_ref, dkv_ref, dfrow_ref,
             bias_sc, dt_sc, acc_sc):
        t = pl.program_id(0)

        @pl.when(t == 0)
        def _():
            _bias_table(frow_ref, bias_sc)
            dt_sc[...] = jnp.zeros_like(dt_sc)
            acc_sc[...] = jnp.zeros_like(acc_sc)

        @pl.when(t < nb)
        def _():
            var = jnp.minimum(t, N_BIAS - 1)
            even = _even_lanes()
            for hp in range(n_pair):
                cs = slice(hp * pair_w, (hp + 1) * pair_w)
                qt = q_ref[:, cs]
                kts = [k0[:, cs], k1[:, cs], k2[:, cs]]
                vts = [v0[:, cs], v1[:, cs], v2[:, cs]]
                kcat = jnp.concatenate(kts, axis=0)
                dot = do_ref[:, cs]
                dd = dot * o_ref[:, cs]
                dos_pair, dsbs, pbs, dqs = None, [], [], []
                for e in range(2):
                    keep = even if e == 0 else jnp.logical_not(even)
                    qm = jnp.where(keep, qt, 0)
                    p = _att_probs(qm, kts, bias_sc[var, 2 * hp + e])
                    inv = 1.0 / jnp.sum(p, axis=-1, keepdims=True)
                    dos = jnp.where(keep, dot * inv, 0.0)
                    delta = jnp.sum(jnp.where(keep, dd, 0.0), axis=-1, keepdims=True) * inv
                    dp = jnp.concatenate([_dot_nt(dos.astype(BF16), v) for v in vts], axis=1)
                    ds = p * (dp - delta)
                    dt_sc[2 * hp + e] += ds
                    dsb = ds.astype(BF16)
                    dq = _dot(dsb, kcat)
                    dqs.append(dq)
                    dsbs.append(dsb)
                    pbs.append(p.astype(BF16))
                    dos_pair = dos if e == 0 else dos_pair + dos
                dq_ref[:, cs] = (jnp.where(even, dqs[0], dqs[1]) * ATT_SCALE).astype(BF16)
                qtt = qt.astype(F32).T.astype(BF16)
                dost = dos_pair.T.astype(BF16)
                for j in range(3):
                    slot = (t + 1 + j) % 3
                    js = slice(j * QB, (j + 1) * QB)
                    for e in range(2):
                        hr = slice(e * HEAD_DIM, (e + 1) * HEAD_DIM)
                        acc_sc[slot, hp, hr, :] += _dot(qtt[hr], dsbs[e][:, js])
                        acc_sc[slot, n_pair + hp, hr, :] += _dot(dost[hr], pbs[e][:, js])

        done = (t + 1) % 3

        @pl.when(t >= 2)
        def _():
            for i in range(2 * n_pair):
                dkv_ref[:, i * pair_w:(i + 1) * pair_w] = acc_sc[done, i].T.astype(BF16)

        acc_sc[done] = jnp.zeros((2 * n_pair, pair_w, QB), F32)

        @pl.when(t == nb + 1)
        def _():
            row = lax.broadcasted_iota(jnp.int32, (8, ROLL_W), 0)
            pad = jnp.zeros((8, ROLL_W - KB), F32)
            for h in range(ATT_HEADS):
                acc8 = jnp.concatenate([dt_sc[h, 0:8, :], pad], axis=1)
                for a1 in range(1, QB // 8):
                    blk = jnp.concatenate([dt_sc[h, 8 * a1:8 * a1 + 8, :], pad], axis=1)
                    acc8 = acc8 + pltpu.roll(blk, ROLL_W - 8 * a1, 1)
                for k in range(3):
                    acc8 = jnp.where(((row >> k) & 1) == 1, pltpu.roll(acc8, ROLL_W - (1 << k), 1), acc8)
                dfrow_ref[h:h + 1, :] = jnp.sum(acc8, axis=0, keepdims=True)

    clamp = lambda t: jnp.minimum(t, nb - 1)
    qrows = pl.BlockSpec((QB, D_ATT), lambda t: (clamp(t), 0))
    return _call(
        body, "b_attn", (nb + 2,),
        _att_in_specs(clamp) + [qrows, qrows, _full((ATT_HEADS, ROLL_W))],
        [qrows, pl.BlockSpec((QB, 2 * D_ATT), lambda t: (jnp.maximum(t - 2, 0), 0)),
         _full((ATT_HEADS, ROLL_W))],
        [_sds((s_len, D_ATT), BF16), _sds((s_len, 2 * D_ATT), BF16), _sds((ATT_HEADS, ROLL_W), F32)],
        [pltpu.VMEM((N_BIAS, ATT_HEADS, QB, KB), F32), pltpu.VMEM((ATT_HEADS, QB, KB), F32),
         pltpu.VMEM((3, 2 * n_pair, pair_w, QB), F32)],
        (*([qkv_pad] * 7), att, datt, frow), "arbitrary", comm)


def _flush_grad(steps, acc, accb, out_hbm, outb_hbm):
    @pl.when(pl.program_id(0) == steps - 1)
    def _():
        accb[...] = acc[...].astype(BF16)
        pltpu.sync_copy(acc, out_hbm)
        pltpu.sync_copy(accb, outb_hbm)


def _b_win(dq, dkv, dxg, h, ts):
    s_len = h.shape[0]
    steps = s_len // ts

    def body(dq_ref, dkv_ref, dxg_ref, h_ref, dw_hbm, dwb_hbm, acc, accb):
        @pl.when(pl.program_id(0) == 0)
        def _():
            acc[...] = jnp.zeros_like(acc)

        dproj = jnp.concatenate([dq_ref[...], dkv_ref[...], dxg_ref[...]], axis=1)
        acc[...] += _dot_tn(h_ref[...], dproj)

        @pl.when(pl.program_id(0) == steps - 1)
        def _():
            accb[...] = acc[...].astype(BF16)
            for s in range(N_SHARD):
                cols = pl.ds(s * IN_SH, IN_SH)
                pltpu.sync_copy(acc.at[:, cols], dw_hbm.at[s])
                pltpu.sync_copy(accb.at[:, cols], dwb_hbm.at[s])

    shape = (N_SHARD, 1024, IN_SH)
    return pl.pallas_call(
        body, name="b_win", grid=(steps,),
        in_specs=[_rows(ts, 512), _rows(ts, 1024), _rows(ts, 1024), _rows(ts, 1024)],
        out_specs=[_any()] * 2, out_shape=[_sds(shape, F32), _sds(shape, BF16)],
        scratch_shapes=[pltpu.VMEM((1024, D_IN), F32), pltpu.VMEM((1024, D_IN), BF16)],
        compiler_params=_cp("arbitrary"))(dq, dkv, dxg, h)


def _b_inproj(dq, dkv, dxg, x, dx1, g_mix, w_in_g, tm, comm=None):
    s_len = x.shape[0]

    def body(dq_ref, dkv_ref, dxg_ref, x_ref, dx1_ref, g_ref, w_hbm, gx_ref, dgm_ref, w_ref):
        _load_w_in_once(w_hbm, w_ref)

        @pl.when(pl.program_id(0) == 0)
        def _():
            dgm_ref[...] = jnp.zeros_like(dgm_ref)

        dproj = jnp.concatenate([dq_ref[...], dkv_ref[...], dxg_ref[...]], axis=1)
        dh = _dot_nt(dproj, w_ref[...])
        dx, dgm = _rms_bwd(dh, x_ref[...], g_ref[...])
        gx_ref[...] = dx1_ref[...] + dx
        dgm_ref[...] += dgm

    return _call(
        body, "b_inproj", (s_len // tm,),
        [_rows(tm, 512), _rows(tm, 1024), _rows(tm, 1024), _rows(tm, 1024), _rows(tm, 1024),
         _full((1, 1024)), _any()],
        [_rows(tm, 1024), _full((1, 1024))],
        [_sds((s_len, 1024), F32), _sds((1, 1024), F32)],
        [pltpu.VMEM((1024, D_IN), BF16)], (dq, dkv, dxg, x, dx1, g_mix, w_in_g), "arbitrary", comm)


def _mm_tn(xa, ya, name, ts):
    s_len, k = xa.shape
    n = ya.shape[1]

    steps = s_len // ts

    def body(x_ref, y_ref, o_hbm, ob_hbm, acc, accb):
        @pl.when(pl.program_id(0) == 0)
        def _():
            acc[...] = jnp.zeros_like(acc)
        acc[...] += _dot_tn(x_ref[...].astype(BF16), y_ref[...].astype(BF16))
        _flush_grad(steps, acc, accb, o_hbm, ob_hbm)

    return pl.pallas_call(
        body, name=name, grid=(steps,), in_specs=[_rows(ts, k), _rows(ts, n)],
        out_specs=[_any()] * 2, out_shape=[_sds((k, n), F32), _sds((k, n), BF16)],
        scratch_shapes=[pltpu.VMEM((k, n), F32), pltpu.VMEM((k, n), BF16)],
        compiler_params=_cp("arbitrary"))(xa, ya)


PAD_KEYS = LEFT_CHUNKS * CHUNK
F_HI = PAD_KEYS - MAX_REL + 1
F_LO = PAD_KEYS + MAX_REL


def _frow_from_rel_bias(rb):
    last = rb[:, 2 * MAX_REL:2 * MAX_REL + 1]
    hi = jnp.broadcast_to(last, (ATT_HEADS, F_HI))
    mid = rb[:, 1:2 * MAX_REL][:, ::-1]
    lo = jnp.broadcast_to(rb[:, 0:1], (ATT_HEADS, KB - F_LO))
    wrap = jnp.broadcast_to(last, (ATT_HEADS, ROLL_W - KB))
    return jnp.concatenate([hi, mid, lo, wrap], axis=1)


def _rel_bias_grad_from_dfrow(df):
    g_last = jnp.sum(df[:, 0:F_HI], axis=1, keepdims=True) + jnp.sum(df[:, KB:ROLL_W], axis=1, keepdims=True)
    mid = df[:, F_HI:F_LO][:, ::-1]
    g_first = jnp.sum(df[:, F_LO:KB], axis=1, keepdims=True)
    return jnp.concatenate([g_first, mid, g_last], axis=1)


def _block_diag(w):
    eye = jnp.eye(8, dtype=w.dtype)
    return (w[:, :, None, :] * eye[:, None, :, None]).reshape(D_LRU, D_LRU)


MID = ['w_out', 'wq_c', 'wk_c', 'wv_c', 'wo_c']
TRANSPOSED = ['w_gate', 'w_up']
AG_IN_INPROJ = ['w_out', 'wq_c', 'wk_c']
AG_IN_ATTN = ['wv_c', 'wo_c', 'w_gate']
AG_IN_LRU = ['w_up']
AG_IN_MID = ['w_down']
RS_IN_MID = ['w_gate', 'w_up']
RS_IN_LRU = ['w_down']
RS_IN_ATTN = MID


def _local_step(x, mem, tgt, p, gw, shards=None, chip=None):
    s_len = x.shape[0]
    tm = min(256, s_len)
    tmb = min(512, s_len)
    tl = min(512, s_len)
    frow = _frow_from_rel_bias(p['rel_bias'])
    wrg = _block_diag(p['w_rg']).astype(BF16)
    wig = _block_diag(p['w_ig']).astype(BF16)
    gw = dict(gw)

    big, bigb, recv, part, sib = {}, {}, {}, {}, {}

    def ag(names):
        return [] if shards is None else [("ag", [shards[n] for n in names])]

    def rs(names):
        return [] if shards is None else [("rs", [bigb[n] for n in names])]

    def swap(names):
        return [] if shards is None else [("swap", [part[n] for n in names])]

    def reduce_own(names):
        if shards is not None:
            sums = _sum_parts([big[n] for n in names], [recv[n] for n in names], chip, "sum_" + names[0])
            part.update(zip(names, sums))

    h, qkv_pad, xg, *got = _f_inproj(x, p['g_mix'], gw['w_in'], tmb, ag(AG_IN_INPROJ))
    gw.update(zip(AG_IN_INPROJ, got))
    att, *got = _f_attn(qkv_pad, frow, ag(AG_IN_ATTN))
    gw.update(zip(AG_IN_ATTN, got))
    rec, u, hs, *got = _f_lru(xg, p['conv_w'], p['conv_b'], wrg, p['b_rg'], wig, p['b_ig'], p['lru_L'], tl,
                              ag(AG_IN_LRU))
    gw.update(zip(AG_IN_LRU, got))
    w_out = gw['w_out'].reshape(1024, 1024)
    wq = gw['wq_c'].reshape(1024, 1024)
    wk = gw['wk_c'].reshape(1024, 1024)
    wv = gw['wv_c'].reshape(1024, 1024)
    wo = gw['wo_c'].reshape(1024, 1024)
    mn, kx, vx = _f_mem(mem, p['g_mem'], wk, wv)
    mg, x1, hc, qx, ox, x2, *got = _f_mid(x, att, rec, p['g_out_attn'], p['g_out_lru'], w_out, p['g_cross'],
                                          wq, kx, vx, wo, tmb, ag(AG_IN_MID))
    gw.update(zip(AG_IN_MID, got))
    ffn_w = [gw[n].reshape(D_FF, 1024) for n in ('w_gate', 'w_up', 'w_down')]
    hf, gact, uact, aact, dx3, loss, dg_final = _f_ffn(x2, tgt, p['g_ffn'], p['g_final'], *ffn_w, tmb)

    ts = min(1024, s_len)
    dgact, duact, dx2, dg_ffn = _b_ffn(dx3, x2, gact, uact, p['g_ffn'], *ffn_w, tm)
    big['w_gate'], bigb['w_gate'] = _mm_tn(dgact, hf, "dw_gate", ts)
    big['w_up'], bigb['w_up'] = _mm_tn(duact, hf, "dw_up", ts)
    big['w_down'], bigb['w_down'] = _mm_tn(aact, dx3, "dw_down", ts)
    for n in ('w_gate', 'w_up', 'w_down'):
        big[n] = big[n].reshape(N_SHARD, FF_SH, 1024)
        bigb[n] = bigb[n].reshape(N_SHARD, FF_SH, 1024)

    dqx, dx1, datt, drec, dkx, dvx, dg_cross, dg_oa, dg_ol, *got = _b_mid(
        dx2, qx, x1, att, rec, kx, vx, wo, wq, w_out, p['g_cross'], p['g_out_attn'], p['g_out_lru'], tmb,
        rs(RS_IN_MID))
    recv.update(zip(RS_IN_MID, got))
    reduce_own(RS_IN_MID)
    dwk, dwv, dg_mem, dwkb, dwvb = _b_mem(dkx, dvx, mem, mn, p['g_mem'], wk, wv)
    big['wk_c'], bigb['wk_c'] = dwk, dwkb
    big['wv_c'], bigb['wv_c'] = dwv, dwvb
    ts_sq = min(2048, s_len)
    big['w_out'], bigb['w_out'] = _mm_tn(mg, dx1, "dw_out", ts_sq)
    big['wq_c'], bigb['wq_c'] = _mm_tn(hc, dqx, "dw_q", ts_sq)
    big['wo_c'], bigb['wo_c'] = _mm_tn(ox, dx2, "dw_o", ts_sq)
    for n in MID:
        big[n] = big[n].reshape(N_SHARD, 256, 1024)
        bigb[n] = bigb[n].reshape(N_SHARD, 256, 1024)

    dxg, dwrg, dwig, dbrg, dbig, dlam, dcw, dcb, *got = _b_lru(
        drec, hs, u, xg, p['conv_w'], wrg, p['b_rg'], wig, p['b_ig'], p['lru_L'], tl,
        rs(RS_IN_LRU) + swap(RS_IN_MID))
    recv.update(zip(RS_IN_LRU, got))
    sib.update(zip(RS_IN_MID, got[len(RS_IN_LRU):]))
    reduce_own(RS_IN_LRU)
    small = {
        'conv_w': dcw, 'conv_b': dcb, 'w_rg': dwrg, 'b_rg': dbrg, 'w_ig': dwig, 'b_ig': dbig, 'lru_L': dlam,
        'g_out_attn': dg_oa, 'g_out_lru': dg_ol, 'g_cross': dg_cross, 'g_mem': dg_mem, 'g_ffn': dg_ffn,
        'g_final': dg_final,
    }
    names = [n for n in SMALL if n in small]
    gather = [] if shards is None else [
        ("ag8", [_pack_small(names, [small[n] for n in names], loss, PACK_ROWS, "pack_small")])]
    dq, dkv, dfrow, *got = _b_attn(qkv_pad, att, datt, frow, rs(RS_IN_ATTN) + swap(RS_IN_LRU) + gather)
    recv.update(zip(RS_IN_ATTN, got))
    sib.update(zip(RS_IN_LRU, got[len(RS_IN_ATTN):]))
    packs = got[-1] if gather else None
    reduce_own(RS_IN_ATTN)
    small['rel_bias'] = _rel_bias_grad_from_dfrow(dfrow)
    big['w_in'], bigb['w_in'] = _b_win(dq, dkv, dxg, h, ts)
    if shards is None:
        grad_x, small['g_mix'] = _b_inproj(dq, dkv, dxg, x, dx1, p['g_mix'], gw['w_in'], tmb)
    else:
        nsw = len(RS_IN_ATTN)

        def copies(refs, send_sems, recv_sems):
            return _tail_copies(refs[0], refs[1], refs[2:2 + nsw], refs[2 + nsw:2 + 2 * nsw], send_sems, recv_sems)

        slots = bigb['w_in']
        bufs = ([slots, lax.empty((3,) + slots.shape[1:], slots.dtype)] + [part[n] for n in RS_IN_ATTN]
                + [lax.empty(part[n].shape, F32) for n in RS_IN_ATTN])
        sems, bufs, token = _split_start("tail_exchange_start", bufs, 3 + nsw, copies)
        grad_x, small['g_mix'] = _b_inproj(dq, dkv, dxg, x, dx1, p['g_mix'] + token[0, 0], gw['w_in'], tmb)
        bufs = _split_wait("tail_exchange_wait", sems, bufs, 3 + nsw, copies, small['g_mix'])
        recv['w_in'] = bufs[1]
        sib.update(zip(RS_IN_ATTN, bufs[2 + nsw:]))
    reduce_own(['w_in'])
    return loss, grad_x, small, big, part, sib, packs


CAST_STEPS = 4


def _cast_shards(ws, name, comm=None):
    def body(*refs):
        n = len(refs) // 2
        for src, dst in zip(refs[:n], refs[n:]):
            dst[...] = src[...].astype(BF16)

    specs = [_rows(w.shape[0] // CAST_STEPS, w.shape[1]) for w in ws]
    return _call(body, name, (CAST_STEPS,), specs, specs, [_sds(w.shape, BF16) for w in ws], [], tuple(ws),
                 "arbitrary", comm)


def _sum_parts(own4s, recv3s, chip, name):
    n = len(own4s)
    _, r, c = own4s[0].shape
    steps = _ew_steps(r, n * c * (4 + 3 * 2 + 4))
    tr = r // steps

    def body(chip_ref, *refs):
        for own_ref, rc_ref, o_ref in zip(refs[:n], refs[n:2 * n], refs[2 * n:]):
            o_ref[...] = ((own_ref[0] + rc_ref[0].astype(F32)) + rc_ref[1].astype(F32)) + rc_ref[2].astype(F32)

    grid_spec = pltpu.PrefetchScalarGridSpec(
        num_scalar_prefetch=1, grid=(steps,),
        in_specs=[pl.BlockSpec((1, tr, c), lambda i, ch: (ch[0], i, 0))] * n
                 + [pl.BlockSpec((3, tr, c), lambda i, ch: (0, i, 0))] * n,
        out_specs=[pl.BlockSpec((tr, c), lambda i, ch: (i, 0))] * n)
    return pl.pallas_call(body, name=name, grid_spec=grid_spec, out_shape=[_sds((r, c), F32)] * n,
                          compiler_params=_cp("parallel"))(chip, *own4s, *recv3s)


def _adamw_math(w, g, m, v):
    m = ADAM_B1 * m + (1.0 - ADAM_B1) * g
    v = ADAM_B2 * v + (1.0 - ADAM_B2) * (g * g)
    m_hat = m / (1.0 - ADAM_B1 ** ADAM_STEP)
    v_hat = v / (1.0 - ADAM_B2 ** ADAM_STEP)
    delta = -ADAM_LR * (m_hat / (jnp.sqrt(v_hat) + ADAM_EPS) + ADAM_WD * w)
    return delta, m, v


def _final_adamw(pas, pbs, ws, ms, vs, name, after=None):
    n = len(ws)
    r, c = ws[0].shape
    steps = _ew_steps(r, n * c * 9 * 4)
    tr = r // steps

    def body(*refs):
        ins, outs = refs[:5 * n], refs[len(refs) - 4 * n:]
        for k in range(n):
            pa_ref, pb_ref, w_ref, m_ref, v_ref = (ins[j * n + k] for j in range(5))
            g = pa_ref[...] + pb_ref[...]
            outs[4 * k][...] = g
            outs[4 * k + 1][...], outs[4 * k + 2][...], outs[4 * k + 3][...] = _adamw_math(
                w_ref[...], g, m_ref[...], v_ref[...])

    order = [] if after is None else [after]
    res = pl.pallas_call(
        body, name=name, grid=(steps,), in_specs=[_rows(tr, c)] * (5 * n) + [_full(t.shape) for t in order],
        out_specs=[_rows(tr, c)] * (4 * n), out_shape=[_sds((r, c), F32)] * (4 * n),
        compiler_params=_cp("parallel"))(*pas, *pbs, *ws, *ms, *vs, *order)
    return [res[4 * k:4 * k + 4] for k in range(n)]


def _pack_put(ref, name, val_ref):
    r = _pack_rows()[name]
    shape = val_ref.shape
    if len(shape) == 3:
        for b in range(shape[0]):
            ref[r:r + shape[1], b * shape[2]:(b + 1) * shape[2]] = val_ref[b]
    elif shape[1] == 2 * PACK_W:
        ref[r:r + 1, :] = val_ref[:, 0:PACK_W]
        ref[r + 1:r + 2, :] = val_ref[:, PACK_W:2 * PACK_W]
    else:
        ref[r:r + shape[0], 0:shape[1]] = val_ref[...]


def _pack_get(ref, name, shape):
    r = _pack_rows()[name]
    if len(shape) == 3:
        return jnp.stack([ref[r:r + shape[1], b * shape[2]:(b + 1) * shape[2]] for b in range(shape[0])])
    if shape[1] == 2 * PACK_W:
        return jnp.concatenate([ref[r:r + 1, :], ref[r + 1:r + 2, :]], axis=1)
    return ref[r:r + shape[0], 0:shape[1]]


def _pack_small(names, g, loss, rows, name):
    n = len(g)
    extra = [] if loss is None else [loss]

    def body(*refs):
        pack = refs[-1]
        pack[...] = jnp.zeros_like(pack)
        for a, nm in enumerate(names):
            _pack_put(pack, nm, refs[a])
        if extra:
            _pack_put(pack, 'loss', refs[n])

    return pl.pallas_call(body, name=name, out_shape=_sds((rows, PACK_W), F32), compiler_params=_cp())(*g, *extra)


def _all_peers():
    x, y, c = _mesh_pos()
    peers = []
    for k in range(1, 8):
        px = 1 - x if k & 4 else x
        py = 1 - y if k & 2 else y
        pc = 1 - c if k & 1 else c
        peers.append(((px, py, pc), 4 * px + 2 * py + pc))
    return peers, 4 * x + 2 * y + c


def _ag8_copies(ins, outs, sems):
    send_sems, recv_sems, loc_sems = sems
    n = len(ins)
    peers, me = _all_peers()

    def remote(k, j, slot):
        return pltpu.make_async_remote_copy(
            src_ref=ins[k], dst_ref=outs[k].at[slot], send_sem=send_sems.at[k, j], recv_sem=recv_sems.at[k, j],
            device_id=peers[j][0], device_id_type=MESH_ID)

    def local(k):
        return pltpu.make_async_copy(ins[k], outs[k].at[me], loc_sems.at[k])

    def start():
        for k in range(n):
            local(k).start()
            for j in range(7):
                remote(k, j, me).start()

    def wait():
        for k in range(n):
            for j in range(7):
                remote(k, j, peers[j][1]).wait_recv()
        for k in range(n):
            for j in range(7):
                remote(k, j, me).wait_send()
            local(k).wait()

    return start, _no_forward, wait


def _adamw_small(packs, late_own, late_packs, g_shapes, loss_shape, w, m, v):
    n = len(w)

    def body(*refs):
        packs_ref, own_ref, late_ref = refs[0], refs[1], refs[2]
        w_refs, m_refs, v_refs = (refs[3 + i * n:3 + (i + 1) * n] for i in range(3))
        o0 = 3 * n + 3
        go, do, mo, vo = (refs[o0 + i * n:o0 + (i + 1) * n] for i in range(4))
        loss_out, tot_ref = refs[o0 + 4 * n], refs[o0 + 4 * n + 1]
        x, y, c = _mesh_pos()
        me = 4 * x + 2 * y + c
        tot = packs_ref[0]
        late = jnp.where(me == 0, own_ref[...], late_ref[0])
        for d in range(1, 8):
            tot = tot + packs_ref[d]
            late = late + jnp.where(me == d, own_ref[...], late_ref[d])
        tot_ref[...] = tot
        tot_ref[0:LATE_ROWS, :] += late
        loss_out[...] = _pack_get(tot_ref, 'loss', loss_shape)
        for a, name in enumerate(SMALL):
            if name == 'conv_w':
                r = _pack_rows()[name]
                ga = tot_ref[r:r + g_shapes[a][0], pl.ds(pl.multiple_of((2 * x + y) * 128, 128), 128)]
            else:
                ga = _pack_get(tot_ref, name, g_shapes[a])
            go[a][...] = ga
            do[a][...], mo[a][...], vo[a][...] = _adamw_math(w_refs[a][...], ga, m_refs[a][...], v_refs[a][...])

    out_shape = [_sds(a.shape, F32) for a in w] * 4 + [_sds(loss_shape, F32)]
    return pl.pallas_call(body, name="adamw_small", out_shape=out_shape,
                          scratch_shapes=[pltpu.VMEM((PACK_ROWS, PACK_W), F32)],
                          compiler_params=_cp())(packs, late_own, late_packs, *w, *m, *v)


PACK_W = 512
PACK_ROWS = 160
LATE = ['g_mix', 'rel_bias']
LATE_ROWS = 32


def _pack_rows():
    rows, r = {}, 0
    for name in ['g_mix', 'g_cross', 'g_mem', 'g_ffn', 'g_final']:
        rows[name] = r
        r += 2
    for name in ['conv_b', 'b_rg', 'b_ig', 'lru_L', 'g_out_attn', 'g_out_lru']:
        rows[name] = r
        r += 1
    rows['conv_w'] = r
    rows['loss'] = r + 4
    rows['rel_bias'] = 24
    rows['w_rg'] = 32
    rows['w_ig'] = 32 + LRU_BLOCK
    assert r + 5 <= 24 and rows['w_ig'] + LRU_BLOCK == PACK_ROWS
    assert rows['g_mix'] + 2 <= LATE_ROWS and rows['rel_bias'] + 8 <= LATE_ROWS
    return rows


INPUT_NAMES = (['x', 'mem'] + WEIGHTS + ['loss_target'] + ['m_' + n for n in WEIGHTS] + ['v_' + n for n in WEIGHTS])


def kernel(x, mem, g_mix, w_in, rel_bias, conv_w, conv_b, w_rg, b_rg, w_ig, b_ig, lru_L, g_out_attn, g_out_lru, w_out, g_cross, g_mem, wq_c, wk_c, wv_c, wo_c, g_ffn, w_gate, w_up, w_down, g_final, loss_target, m_g_mix, m_w_in, m_rel_bias, m_conv_w, m_conv_b, m_w_rg, m_b_rg, m_w_ig, m_b_ig, m_lru_L, m_g_out_attn, m_g_out_lru, m_w_out, m_g_cross, m_g_mem, m_wq_c, m_wk_c, m_wv_c, m_wo_c, m_g_ffn, m_w_gate, m_w_up, m_w_down, m_g_final, v_g_mix, v_w_in, v_rel_bias, v_conv_w, v_conv_b, v_w_rg, v_b_rg, v_w_ig, v_b_ig, v_lru_L, v_g_out_attn, v_g_out_lru, v_w_out, v_g_cross, v_g_mem, v_wq_c, v_wk_c, v_wv_c, v_wo_c, v_g_ffn, v_w_gate, v_w_up, v_w_down, v_g_final):
    a = dict(zip(INPUT_NAMES, (x, mem, g_mix, w_in, rel_bias, conv_w, conv_b, w_rg, b_rg, w_ig, b_ig, lru_L, g_out_attn, g_out_lru, w_out, g_cross, g_mem, wq_c, wk_c, wv_c, wo_c, g_ffn, w_gate, w_up, w_down, g_final, loss_target, m_g_mix, m_w_in, m_rel_bias, m_conv_w, m_conv_b, m_w_rg, m_b_rg, m_w_ig, m_b_ig, m_lru_L, m_g_out_attn, m_g_out_lru, m_w_out, m_g_cross, m_g_mem, m_wq_c, m_wk_c, m_wv_c, m_wo_c, m_g_ffn, m_w_gate, m_w_up, m_w_down, m_g_final, v_g_mix, v_w_in, v_rel_bias, v_conv_w, v_conv_b, v_w_rg, v_b_rg, v_w_ig, v_b_ig, v_lru_L, v_g_out_attn, v_g_out_lru, v_w_out, v_g_cross, v_g_mem, v_wq_c, v_wk_c, v_wv_c, v_wo_c, v_g_ffn, v_w_gate, v_w_up, v_w_down, v_g_final)))
    chip = 2 * lax.axis_index("x") + lax.axis_index("y")

    def shard(name):
        arr = a[name][0]
        base = name[2:] if name[:2] in ('m_', 'v_') else name
        return jnp.swapaxes(arr, 0, 1) if base in TRANSPOSED else arr

    shards = {'w_in': _cast_shards([shard('w_in')], "cast_w_in")[0]}
    rest = [n for n in BIG if n != 'w_in']
    *cast, w_in_g, conv_w_g = _cast_shards([shard(n) for n in rest], "cast_rest",
                                           [("ag", [shards['w_in']]), ("agf", [a['conv_w'][0]])])
    shards.update(zip(rest, cast))
    conv_w_full = conv_w_g.transpose(1, 0, 2).reshape(4, D_LRU)

    p = {n: a[n] for n in SMALL}
    p['rel_bias'] = a['rel_bias'][0]
    p['w_rg'] = a['w_rg'][0]
    p['w_ig'] = a['w_ig'][0]
    p['conv_w'] = conv_w_full
    p['g_final'] = a['g_final'][None, :]
    chip_arr = jnp.reshape(chip, (1,)).astype(jnp.int32)
    loss_part, grad_x, small, _, part, sib, packs = _local_step(
        a['x'][0], a['mem'][0], a['loss_target'][0], p, {'w_in': w_in_g}, shards, chip_arr)

    def late_copies(refs, send_sems, recv_sems):
        return _late_copies(refs[0], refs[1], refs[2], refs[3], send_sems, recv_sems)

    late_pack = _pack_small(LATE, [small[n] for n in LATE], None, LATE_ROWS, "pack_late")
    bufs = [part['w_in'], lax.empty(part['w_in'].shape, F32), late_pack, jnp.zeros((8, LATE_ROWS, PACK_W), F32)]
    sems, bufs, token = _split_start("late_exchange_start", bufs, 8, late_copies)
    out = {}

    def adamw(group, after=None):
        results = _final_adamw([part[n] for n in group], [sib[n] for n in group], [shard(n) for n in group],
                               [shard('m_' + n) for n in group], [shard('v_' + n) for n in group],
                               "adamw_" + group[0], after)
        for n, res in zip(group, results):
            out[n] = [jnp.swapaxes(r, 0, 1) for r in res] if n in TRANSPOSED else res
        return results[-1][0]

    adamw(MID, token)
    done = adamw(['w_gate', 'w_up', 'w_down'], token)
    _, sib['w_in'], late_pack, late_packs = _split_wait("late_exchange_wait", sems, bufs, 8, late_copies, done)
    adamw(['w_in'])

    def natural(arr):
        return arr[0] if arr.ndim >= 3 else (arr[None, :] if arr.ndim == 1 else arr)

    small_out = _adamw_small(packs, late_pack, late_packs, [small[n].shape for n in SMALL],
                             loss_part.shape, *[[natural(a[pre + n]) for n in SMALL] for pre in ('', 'm_', 'v_')])
    ns = len(SMALL)
    loss = small_out[4 * ns][0, 0]

    def leaf(i, n):
        if n in BIG:
            return out[n][i][None]
        return small_out[i * ns + SMALL.index(n)].reshape(a[n].shape)

    return (loss, grad_x[None], *[leaf(i, n) for i in range(4) for n in WEIGHTS])
```

```python
import math

import jax
import jax.numpy as jnp
from jax import lax
from jax.experimental import pallas as pl
from jax.experimental.pallas import tpu as pltpu

F32 = jnp.float32
BF16 = jnp.bfloat16

D_MODEL = 1024
D_ATT = 512
D_LRU = 512
HEAD_DIM = 64
ATT_HEADS = 8
CHUNK = 64
LEFT_CHUNKS = 8
MAX_REL = 128
X_HEADS = 4
X_HEAD_DIM = 256
N_SHARD = 4
IN_SH = 640
D_IN = N_SHARD * IN_SH
FF_SH = 704
D_FF = N_SHARD * FF_SH
EPS = 1e-6
LRU_C = 8.0
LRU_BLOCKS = 8
LRU_BLOCK = 64
QB = 256
KB = 768
ROLL_W = 1024
NEG = -1e30
ATT_SCALE = HEAD_DIM ** -0.5
X_SCALE = X_HEAD_DIM ** -0.5

ADAM_LR = 0.001
ADAM_B1 = 0.9
ADAM_B2 = 0.999
ADAM_EPS = 1e-08
ADAM_WD = 0.01
ADAM_STEP = 10

VMEM_LIMIT_V7X = 56 * 1024 * 1024
BF16_ROWS = 16


EW_VMEM_BUDGET = 40 * 1024 * 1024


def _ew_steps(rows, bytes_per_row):
    return min(s for s in (2, 4, 8, 16) if rows % (s * BF16_ROWS) == 0
               and 2 * (rows // s) * bytes_per_row <= EW_VMEM_BUDGET)
MESH_ID = pl.DeviceIdType.MESH

WEIGHTS = ['g_mix', 'w_in', 'rel_bias', 'conv_w', 'conv_b', 'w_rg', 'b_rg', 'w_ig', 'b_ig', 'lru_L',
           'g_out_attn', 'g_out_lru', 'w_out', 'g_cross', 'g_mem', 'wq_c', 'wk_c', 'wv_c', 'wo_c',
           'g_ffn', 'w_gate', 'w_up', 'w_down', 'g_final']
BIG = ['w_in', 'w_out', 'wq_c', 'wk_c', 'wv_c', 'wo_c', 'w_gate', 'w_up', 'w_down']
SMALL = [n for n in WEIGHTS if n not in BIG]


def _sds(shape, dtype):
    return jax.ShapeDtypeStruct(shape, dtype)


def _cp(*sem):
    return pltpu.CompilerParams(dimension_semantics=sem or None, vmem_limit_bytes=VMEM_LIMIT_V7X)


def _rows(tm, n):
    return pl.BlockSpec((tm, n), lambda i: (i, 0))


def _full(shape):
    nd = len(shape)
    return pl.BlockSpec(shape, lambda i: (0,) * nd)


def _dot(a, b):
    return jnp.dot(a, b, preferred_element_type=F32)


def _dot_nt(a, b):
    return lax.dot_general(a, b, (((1,), (1,)), ((), ())), preferred_element_type=F32)


def _dot_tn(a, b):
    return lax.dot_general(a, b, (((0,), (0,)), ((), ())), preferred_element_type=F32)


def _rinv(x):
    return lax.rsqrt(jnp.mean(x * x, axis=-1, keepdims=True) + EPS)


def _rms_bwd(dy, x, g):
    r = _rinv(x)
    yh = x * r
    dyh = dy * g
    dx = r * (dyh - yh * jnp.mean(dyh * yh, axis=-1, keepdims=True))
    return dx, jnp.sum(dy * yh, axis=0, keepdims=True)


def _gelu(x):
    c = math.sqrt(2.0 / math.pi)
    t = jnp.tanh(c * (x + 0.044715 * x * x * x))
    return 0.5 * x * (1.0 + t)


def _gelu_and_grad(x):
    c = math.sqrt(2.0 / math.pi)
    t = jnp.tanh(c * (x + 0.044715 * x * x * x))
    g = 0.5 * x * (1.0 + t)
    dg = 0.5 * (1.0 + t) + 0.5 * x * (1.0 - t * t) * c * (1.0 + 3.0 * 0.044715 * x * x)
    return g, dg


def _neg_expm1(z):
    series = -z * (1.0 + z * (0.5 + z * ((1.0 / 6.0) + z * (1.0 / 24.0))))
    return jnp.where(z > -0.03, series, 1.0 - jnp.exp(z))


def _lru_gates(u, wrg, brg, wig, big, lam):
    ub = u.astype(BF16)
    r = jax.nn.sigmoid(_dot(ub, wrg) + brg)
    ig = jax.nn.sigmoid(_dot(ub, wig) + big)
    sp = jnp.maximum(-lam, 0.0) + jnp.log1p(jnp.exp(-jnp.abs(lam)))
    la = -LRU_C * r * sp
    a = jnp.exp(la)
    mult = jnp.sqrt(jnp.maximum(_neg_expm1(2.0 * la), 0.0))
    return ub, r, ig, sp, a, mult


def _scan8(a8, b8, hprev):
    row = lax.broadcasted_iota(jnp.int32, a8.shape, 0)
    aa, bb = a8, b8
    for d in (1, 2, 4):
        a_s = pltpu.roll(aa, d, 0)
        b_s = pltpu.roll(bb, d, 0)
        m = row >= d
        bb = jnp.where(m, aa * b_s + bb, bb)
        aa = jnp.where(m, aa * a_s, aa)
    return aa * hprev + bb


def _rscan8(c8, d8, lnext):
    row = lax.broadcasted_iota(jnp.int32, c8.shape, 0)
    cc, dd = c8, d8
    for d in (1, 2, 4):
        c_s = pltpu.roll(cc, 8 - d, 0)
        d_s = pltpu.roll(dd, 8 - d, 0)
        m = row < 8 - d
        dd = jnp.where(m, cc * d_s + dd, dd)
        cc = jnp.where(m, cc * c_s, cc)
    return cc * lnext + dd


def _mesh_pos():
    return lax.axis_index("x"), lax.axis_index("y"), lax.axis_index("c")


def _other_chips(x, y):
    return [(1 - x, y), (x, 1 - y), (1 - x, 1 - y)]


def _no_forward():
    pass


def _ag_full_copies(ins, outs, sems):
    send_sems, recv_sems, loc_sems = sems
    n = len(ins)
    x, y, c = _mesh_pos()
    mine = 2 * x + y
    chips = _other_chips(x, y)

    def remote(k, j, slot):
        px, py = chips[j]
        return pltpu.make_async_remote_copy(
            src_ref=ins[k], dst_ref=outs[k].at[slot], send_sem=send_sems.at[k, j], recv_sem=recv_sems.at[k, j],
            device_id=(px, py, c), device_id_type=MESH_ID)

    def local(k):
        return pltpu.make_async_copy(ins[k], outs[k].at[mine], loc_sems.at[k])

    def start():
        for k in range(n):
            local(k).start()
            for j in range(3):
                remote(k, j, mine).start()

    def wait():
        for k in range(n):
            for j, (px, py) in enumerate(chips):
                remote(k, j, 2 * px + py).wait_recv()
        for k in range(n):
            for j in range(3):
                remote(k, j, mine).wait_send()
            local(k).wait()

    return start, _no_forward, wait


def _ag_copies(ins, outs, sems):
    send_sems, recv_sems, fsend_sems, frecv_sems, loc_sems = sems
    n = len(ins)
    x, y, c = _mesh_pos()
    mine = 2 * x + y
    chips = _other_chips(x, y)

    def half(ref, hc):
        r = ref.shape[0] // 2
        return ref.at[pl.ds(pl.multiple_of(hc * r, 16), r)]

    def ici(k, j, slot):
        px, py = chips[j]
        return pltpu.make_async_remote_copy(
            src_ref=half(ins[k], c), dst_ref=half(outs[k].at[slot], c),
            send_sem=send_sems.at[k, j], recv_sem=recv_sems.at[k, j],
            device_id=(px, py, c), device_id_type=MESH_ID)

    def d2d(k, j, hc):
        px, py = chips[j]
        part = half(outs[k].at[2 * px + py], hc)
        return pltpu.make_async_remote_copy(
            src_ref=part, dst_ref=part, send_sem=fsend_sems.at[k, j], recv_sem=frecv_sems.at[k, j],
            device_id=(x, y, 1 - c), device_id_type=MESH_ID)

    def local(k):
        return pltpu.make_async_copy(ins[k], outs[k].at[mine], loc_sems.at[k])

    def start():
        for k in range(n):
            local(k).start()
            for j in range(3):
                ici(k, j, mine).start()

    def forward():
        for k in range(n):
            for j, (px, py) in enumerate(chips):
                ici(k, j, 2 * px + py).wait_recv()
                d2d(k, j, c).start()

    def wait():
        for k in range(n):
            for j in range(3):
                d2d(k, j, 1 - c).wait_recv()
        for k in range(n):
            for j in range(3):
                d2d(k, j, c).wait_send()
                ici(k, j, mine).wait_send()
            local(k).wait()

    return start, forward, wait


def _rs_copies(ins, outs, sems):
    send_sems, recv_sems = sems
    n = len(ins)
    x, y, c = _mesh_pos()
    chips = _other_chips(x, y)

    def remote(k, j):
        px, py = chips[j]
        return pltpu.make_async_remote_copy(
            src_ref=ins[k].at[2 * px + py], dst_ref=outs[k].at[j],
            send_sem=send_sems.at[k, j], recv_sem=recv_sems.at[k, j],
            device_id=(px, py, c), device_id_type=MESH_ID)

    def start():
        for k in range(n):
            for j in range(3):
                remote(k, j).start()

    def wait():
        for k in range(n):
            for j in range(3):
                remote(k, j).wait_recv()
        for k in range(n):
            for j in range(3):
                remote(k, j).wait_send()

    return start, _no_forward, wait


def _swap_copies(ins, outs, sems):
    send_sems, recv_sems = sems
    x, y, c = _mesh_pos()
    copies = [pltpu.make_async_remote_copy(
        src_ref=ins[k], dst_ref=outs[k], send_sem=send_sems.at[k], recv_sem=recv_sems.at[k],
        device_id=(x, y, 1 - c), device_id_type=MESH_ID) for k in range(len(ins))]

    def start():
        for cp in copies:
            cp.start()

    def wait():
        for cp in copies:
            cp.wait()

    return start, _no_forward, wait


def _comm_plan(groups):
    plan, arrs, shapes, sems = [], [], [], []
    for kind, group in groups:
        k = len(group)
        arrs += group
        per_peer = pltpu.SemaphoreType.DMA((k, 3))
        if kind == "ag":
            shapes += [_sds((N_SHARD,) + w.shape, w.dtype) for w in group]
            gsems = [per_peer] * 4 + [pltpu.SemaphoreType.DMA((k,))]
            maker = _ag_copies
        elif kind == "agf":
            shapes += [_sds((N_SHARD,) + w.shape, w.dtype) for w in group]
            gsems = [per_peer] * 2 + [pltpu.SemaphoreType.DMA((k,))]
            maker = _ag_full_copies
        elif kind == "ag8":
            shapes += [_sds((8,) + g.shape, g.dtype) for g in group]
            gsems = [pltpu.SemaphoreType.DMA((k, 7))] * 2 + [pltpu.SemaphoreType.DMA((k,))]
            maker = _ag8_copies
        elif kind == "rs":
            shapes += [_sds((3,) + g.shape[1:], g.dtype) for g in group]
            gsems = [pltpu.SemaphoreType.DMA((k, 3)), pltpu.SemaphoreType.DMA((k, 3))]
            maker = _rs_copies
        else:
            shapes += [_sds(g.shape, g.dtype) for g in group]
            gsems = [pltpu.SemaphoreType.DMA((k,)), pltpu.SemaphoreType.DMA((k,))]
            maker = _swap_copies
        plan.append((maker, k, len(gsems)))
        sems += gsems
    return plan, arrs, shapes, sems


def _comm_fns(plan, cins, couts, sems):
    fns, a, s = [], 0, 0
    for maker, k, ns in plan:
        fns.append(maker(cins[a:a + k], couts[a:a + k], sems[s:s + ns]))
        a += k
        s += ns

    def start():
        for st, _, _ in fns:
            st()

    def forward():
        for _, fw, _ in fns:
            fw()

    def wait():
        for _, _, wt in fns:
            wt()

    return start, forward, wait


def _call(body, name, grid, in_specs, out_specs, out_shape, scratch, args, sem, comm=None):
    if not comm:
        return pl.pallas_call(body, name=name, grid=grid, in_specs=in_specs, out_specs=out_specs,
                              out_shape=out_shape, scratch_shapes=scratch, compiler_params=_cp(sem))(*args)
    plan, c_arrs, c_shapes, c_sems = _comm_plan(comm)
    k = len(c_arrs)
    n_in, n_out, n_scr = len(in_specs), len(out_specs), len(scratch)
    last = grid[0] - 1
    fwd_step = max(1, (2 * last) // 3)

    def wrapped(*refs):
        ins, cins = refs[:n_in], refs[n_in:n_in + k]
        o0 = n_in + k
        outs, couts = refs[o0:o0 + n_out], refs[o0 + n_out:o0 + n_out + k]
        s0 = o0 + n_out + k
        start, forward, wait = _comm_fns(plan, cins, couts, refs[s0 + n_scr:])
        pl.when(pl.program_id(0) == 0)(start)
        pl.when(pl.program_id(0) == fwd_step)(forward)
        body(*ins, *outs, *refs[s0:s0 + n_scr])
        pl.when(pl.program_id(0) == last)(wait)

    return pl.pallas_call(
        wrapped, name=name, grid=grid, in_specs=list(in_specs) + [_any()] * k,
        out_specs=list(out_specs) + [_any()] * k, out_shape=list(out_shape) + c_shapes,
        scratch_shapes=list(scratch) + c_sems, compiler_params=_cp(sem))(*args, *c_arrs)


def _tail_copies(slots_ref, land_ref, part_refs, sib_refs, send_sems, recv_sems):
    x, y, c = _mesh_pos()
    copies = []
    for j, (px, py) in enumerate(_other_chips(x, y)):
        copies.append(pltpu.make_async_remote_copy(
            src_ref=slots_ref.at[2 * px + py], dst_ref=land_ref.at[j], send_sem=send_sems[j], recv_sem=recv_sems[j],
            device_id=(px, py, c), device_id_type=MESH_ID))
    for k, (p_ref, s_ref) in enumerate(zip(part_refs, sib_refs)):
        copies.append(pltpu.make_async_remote_copy(
            src_ref=p_ref, dst_ref=s_ref, send_sem=send_sems[3 + k], recv_sem=recv_sems[3 + k],
            device_id=(x, y, 1 - c), device_id_type=MESH_ID))
    return copies


def _late_copies(part_ref, sib_ref, pack_ref, packs_ref, send_sems, recv_sems):
    x, y, c = _mesh_pos()
    peers, me = _all_peers()
    copies = [pltpu.make_async_remote_copy(
        src_ref=part_ref, dst_ref=sib_ref, send_sem=send_sems[0], recv_sem=recv_sems[0],
        device_id=(x, y, 1 - c), device_id_type=MESH_ID)]
    for j in range(7):
        copies.append(pltpu.make_async_remote_copy(
            src_ref=pack_ref, dst_ref=packs_ref.at[me], send_sem=send_sems[1 + j], recv_sem=recv_sems[1 + j],
            device_id=peers[j][0], device_id_type=MESH_ID))
    return copies


def _split_start(name, bufs, ncp, make_copies):
    hbm = pl.BlockSpec(memory_space=pltpu.HBM)
    sem = pl.BlockSpec(memory_space=pltpu.SEMAPHORE)
    bufs = [pltpu.with_memory_space_constraint(b, pltpu.HBM) for b in bufs]
    nb = len(bufs)

    def body(*refs):
        for cp in make_copies(refs[:nb], refs[nb:nb + ncp], refs[nb + ncp:nb + 2 * ncp]):
            cp.start()
        refs[-1][...] = jnp.zeros_like(refs[-1])

    out = pl.pallas_call(
        body, name=name,
        out_shape=[pltpu.SemaphoreType.DMA(())] * (2 * ncp) + [pltpu.HBM(b.shape, b.dtype) for b in bufs]
                  + [_sds((8, 128), F32)],
        in_specs=[hbm] * nb, out_specs=[sem] * (2 * ncp) + [hbm] * nb + [pl.BlockSpec(memory_space=pltpu.VMEM)],
        input_output_aliases={i: 2 * ncp + i for i in range(nb)},
        compiler_params=pltpu.CompilerParams(has_side_effects=pltpu.SideEffectType.DATAFLOW_SIDE_EFFECTING),
    )(*bufs)
    return out[:2 * ncp], out[2 * ncp:2 * ncp + nb], out[-1]


def _split_wait(name, sems, bufs, ncp, make_copies, after):
    nb = len(bufs)
    hbm = pl.BlockSpec(memory_space=pltpu.HBM)
    sem = pl.BlockSpec(memory_space=pltpu.SEMAPHORE)

    def body(*refs):
        for cp in make_copies(refs[:nb], refs[nb:nb + ncp], refs[nb + ncp:nb + 2 * ncp]):
            cp.wait_send()
            cp.wait_recv()

    return pl.pallas_call(
        body, name=name, out_shape=[pltpu.HBM(b.shape, b.dtype) for b in bufs],
        in_specs=[hbm] * nb + [sem] * (2 * ncp) + [_any()], out_specs=[hbm] * nb,
        input_output_aliases={i: i for i in range(nb)},
        compiler_params=pltpu.CompilerParams(has_side_effects=pltpu.SideEffectType.DATAFLOW_SIDE_EFFECTING),
    )(*bufs, *sems, after)


def _any():
    return pl.BlockSpec(memory_space=pl.ANY)


def _copy_together(pairs, sems):
    copies = [pltpu.make_async_copy(src, dst, sems.at[i]) for i, (src, dst) in enumerate(pairs)]
    for cp in copies:
        cp.start()
    for cp in copies:
        cp.wait()


def _load_w_in_once(w_hbm, w_ref, sems):
    @pl.when(pl.program_id(0) == 0)
    def _():
        _copy_together([(w_hbm.at[s], w_ref.at[:, pl.ds(s * IN_SH, IN_SH)]) for s in range(N_SHARD)], sems)


def _f_inproj(x, g_mix, w_in_g, tm, comm=None):
    s_len = x.shape[0]
    pad_rows = LEFT_CHUNKS * CHUNK
    npad = pad_rows // tm

    def body(x_ref, g_ref, w_hbm, h_ref, qkv_ref, xg_ref, w_ref, w_sems):
        i = pl.program_id(0)
        _load_w_in_once(w_hbm, w_ref, w_sems)

        @pl.when(i < npad)
        def _():
            qkv_ref[...] = jnp.zeros_like(qkv_ref)

        @pl.when(i >= npad)
        def _():
            xv = x_ref[...]
            h = (xv * _rinv(xv) * g_ref[...]).astype(BF16)
            h_ref[...] = h
            proj = _dot(h, w_ref[...])
            qkv_ref[:, 0:D_ATT] = (proj[:, 0:D_ATT] * ATT_SCALE).astype(BF16)
            qkv_ref[:, D_ATT:3 * D_ATT] = proj[:, D_ATT:3 * D_ATT].astype(BF16)
            xg_ref[...] = proj[:, 3 * D_ATT:D_IN]

    def tok(n):
        return pl.BlockSpec((tm, n), lambda i: (jnp.maximum(i - npad, 0), 0))

    return _call(
        body, "f_inproj", (s_len // tm + npad,),
        [tok(1024), _full((1, 1024)), _any()],
        [tok(1024), _rows(tm, 1536), tok(1024)],
        [_sds((s_len, 1024), BF16), _sds((s_len + pad_rows, 1536), BF16), _sds((s_len, 1024), F32)],
        [pltpu.VMEM((1024, D_IN), BF16), pltpu.SemaphoreType.DMA((N_SHARD,))], (x, g_mix, w_in_g), "arbitrary", comm)


N_BIAS = 3


def _bias_table(frow_ref, bias_sc):
    qa = lax.broadcasted_iota(jnp.int32, (QB, KB), 0) // CHUNK
    kcol = lax.broadcasted_iota(jnp.int32, (QB, KB), 1)
    kb = kcol // CHUNK
    band = jnp.where((kb >= qa) & (kb - qa <= LEFT_CHUNKS), 0.0, NEG).astype(F32)
    for h in range(ATT_HEADS):
        row = jnp.broadcast_to(frow_ref[h:h + 1, :], (QB, ROLL_W))
        toep = pltpu.roll(row, 0, 1, stride=1, stride_axis=0)
        gen = toep[:, 0:KB] + band
        bias_sc[N_BIAS - 1, h] = gen
        for v in range(N_BIAS - 1):
            pad_keys = LEFT_CHUNKS * CHUNK - v * QB
            bias_sc[v, h] = gen + jnp.where(kcol < pad_keys, NEG, 0.0).astype(F32)


def _even_lanes():
    return lax.broadcasted_iota(jnp.int32, (1, 2 * HEAD_DIM), 1) < HEAD_DIM


def _att_probs(qm, kts, bias):
    s = jnp.concatenate([_dot_nt(qm, k) for k in kts], axis=1) + bias
    return jnp.exp(s - jnp.max(s, axis=-1, keepdims=True))


def _att_in_specs(clamp):
    def spec(j, col):
        return pl.BlockSpec((QB, D_ATT), lambda i: (clamp(i) + j, col))
    return [spec(2, 0), spec(0, 1), spec(1, 1), spec(2, 1), spec(0, 2), spec(1, 2), spec(2, 2)]


def _f_attn(qkv_pad, frow, comm=None):
    s_len = qkv_pad.shape[0] - LEFT_CHUNKS * CHUNK
    nb = s_len // QB

    def body(q_ref, k0, k1, k2, v0, v1, v2, frow_ref, o_ref, bias_sc):
        i = pl.program_id(0)

        @pl.when(i == 0)
        def _():
            _bias_table(frow_ref, bias_sc)

        var = jnp.minimum(i, N_BIAS - 1)
        even = _even_lanes()
        for hp in range(ATT_HEADS // 2):
            cs = slice(hp * 2 * HEAD_DIM, (hp + 1) * 2 * HEAD_DIM)
            qt = q_ref[:, cs]
            kts = [k0[:, cs], k1[:, cs], k2[:, cs]]
            vts = [v0[:, cs], v1[:, cs], v2[:, cs]]
            res = []
            for e in range(2):
                keep = even if e == 0 else jnp.logical_not(even)
                pb = _att_probs(jnp.where(keep, qt, 0), kts, bias_sc[var, 2 * hp + e]).astype(BF16)
                r = _dot(pb, jnp.concatenate([jnp.where(keep, v, 1) for v in vts], axis=0))
                res.append(r / pltpu.roll(r, HEAD_DIM, 1))
            o_ref[:, cs] = jnp.where(even, res[0], res[1])

    return _call(
        body, "f_attn", (nb,),
        _att_in_specs(lambda i: i) + [_full((ATT_HEADS, ROLL_W))],
        [_rows(QB, D_ATT)], [_sds((s_len, D_ATT), F32)],
        [pltpu.VMEM((N_BIAS, ATT_HEADS, QB, KB), F32)], (*([qkv_pad] * 7), frow), "arbitrary", comm)


def _f_lru(xg, conv_w, conv_b, wrg, brg, wig, big, lam, tl, comm=None):
    s_len = xg.shape[0]

    def body(xg_ref, cw_ref, cb_ref, wrg_ref, brg_ref, wig_ref, big_ref, l_ref,
             rec_ref, u_ref, hs_ref, xbuf, a_sc, b_sc, hcar):
        i = pl.program_id(0)

        @pl.when(i == 0)
        def _():
            xbuf[0:8, :] = jnp.zeros((8, D_LRU), F32)
            hcar[...] = jnp.zeros((8, D_LRU), F32)

        xu0 = xg_ref[:, 0:D_LRU]
        xbuf[8:8 + tl, :] = xu0
        u = cb_ref[...] + cw_ref[0:1, :] * xbuf[pl.ds(5, tl), :]
        for j in range(1, 4):
            u = u + cw_ref[j:j + 1, :] * xbuf[pl.ds(5 + j, tl), :]
        xbuf[0:8, :] = xu0[tl - 8:tl, :]
        u_ref[...] = u
        _, _, ig, _, a, mult = _lru_gates(u, wrg_ref[...], brg_ref[...], wig_ref[...], big_ref[...], l_ref[...])
        a_sc[...] = a
        b_sc[...] = mult * (ig * u)

        def grp(g, hprev):
            off = pl.multiple_of(g * 8, 8)
            h8 = _scan8(a_sc[pl.ds(off, 8), :], b_sc[pl.ds(off, 8), :], hprev)
            hs_ref[pl.ds(off, 8), :] = h8
            return h8[7:8, :]

        hcar[0:1, :] = lax.fori_loop(0, tl // 8, grp, hcar[0:1, :])
        rec_ref[...] = hs_ref[...] * _gelu(xg_ref[:, D_LRU:2 * D_LRU])

    vec = _full((1, D_LRU))
    return _call(
        body, "f_lru", (s_len // tl,),
        [_rows(tl, 1024), _full((4, D_LRU)), vec, _full((D_LRU, D_LRU)), vec, _full((D_LRU, D_LRU)), vec, vec],
        [_rows(tl, D_LRU)] * 3, [_sds((s_len, D_LRU), F32)] * 3,
        [pltpu.VMEM((tl + 8, D_LRU), F32), pltpu.VMEM((tl, D_LRU), F32),
         pltpu.VMEM((tl, D_LRU), F32), pltpu.VMEM((8, D_LRU), F32)],
        (xg, conv_w, conv_b, wrg, brg, wig, big, lam), "arbitrary", comm)


def _f_mem(mem, g_mem, wk, wv):
    def body(mem_ref, g_ref, wk_ref, wv_ref, mn_ref, kx_ref, vx_ref):
        mv = mem_ref[...]
        mn = (mv * _rinv(mv) * g_ref[...]).astype(BF16)
        mn_ref[...] = mn
        kx_ref[...] = _dot(mn, wk_ref[...]).astype(BF16)
        vx_ref[...] = _dot(mn, wv_ref[...]).astype(BF16)

    m = mem.shape[0]
    return pl.pallas_call(
        body, name="f_mem", out_shape=[_sds((m, 1024), BF16)] * 3,
        compiler_params=_cp())(mem, g_mem, wk, wv)


def _xattn_probs(q, k):
    s = _dot_nt(q, k) * X_SCALE
    m = jnp.max(s, axis=-1, keepdims=True)
    p = jnp.exp(s - m)
    return p, jnp.sum(p, axis=-1, keepdims=True)


def _f_mid(x, att, rec, g_oa, g_ol, w_out, g_cross, wq, kx, vx, wo, tm, comm=None):
    s_len = x.shape[0]
    m_len = kx.shape[0]

    def body(x_ref, att_ref, rec_ref, goa_ref, gol_ref, wout_ref, gc_ref, wq_ref, kx_ref, vx_ref, wo_ref,
             mg_ref, x1_ref, hc_ref, qx_ref, ox_ref, x2_ref):
        av = att_ref[...]
        rv = rec_ref[...]
        mg_ref[:, 0:D_ATT] = (av * _rinv(av) * goa_ref[...]).astype(BF16)
        mg_ref[:, D_ATT:1024] = (rv * _rinv(rv) * gol_ref[...]).astype(BF16)
        x1 = x_ref[...] + _dot(mg_ref[...], wout_ref[...])
        x1_ref[...] = x1
        hc = (x1 * _rinv(x1) * gc_ref[...]).astype(BF16)
        hc_ref[...] = hc
        qx_ref[...] = _dot(hc, wq_ref[...]).astype(BF16)
        for h in range(X_HEADS):
            sl = slice(h * X_HEAD_DIM, (h + 1) * X_HEAD_DIM)
            p, l = _xattn_probs(qx_ref[:, sl], kx_ref[:, sl])
            ox_ref[:, sl] = (_dot(p.astype(BF16), vx_ref[:, sl]) / l).astype(BF16)
        x2_ref[...] = x1 + _dot(ox_ref[...], wo_ref[...])

    sq = _full((1024, 1024))
    return _call(
        body, "f_mid", (s_len // tm,),
        [_rows(tm, 1024), _rows(tm, 512), _rows(tm, 512), _full((1, 512)), _full((1, 512)), sq,
         _full((1, 1024)), sq, _full((m_len, 1024)), _full((m_len, 1024)), sq],
        [_rows(tm, 1024)] * 6,
        [_sds((s_len, 1024), BF16), _sds((s_len, 1024), F32), _sds((s_len, 1024), BF16),
         _sds((s_len, 1024), BF16), _sds((s_len, 1024), BF16), _sds((s_len, 1024), F32)],
        [], (x, att, rec, g_oa, g_ol, w_out, g_cross, wq, kx, vx, wo), "arbitrary", comm)


def _load_weights_once(pairs, sems):
    @pl.when(pl.program_id(0) == 0)
    def _():
        _copy_together(pairs, sems)


FF_CHUNKS = [(0, 1280), (1280, D_FF)]


def _f_ffn(x2, tgt, g_ffn, g_final, wg, wu, wd, tm):
    s_len = x2.shape[0]

    def body(x2_ref, t_ref, gf_ref, gfin_ref, wg_hbm, wu_hbm, wd_hbm,
             hf_ref, g_ref, u_ref, a_ref, dx3_ref, loss_ref, dgfin_ref, wg_ref, wu_ref, wd_ref, w_sems):
        _load_weights_once([(wg_hbm, wg_ref), (wu_hbm, wu_ref), (wd_hbm, wd_ref)], w_sems)

        @pl.when(pl.program_id(0) == 0)
        def _():
            loss_ref[...] = jnp.zeros_like(loss_ref)
            dgfin_ref[...] = jnp.zeros_like(dgfin_ref)

        x2v = x2_ref[...]
        hf = (x2v * _rinv(x2v) * gf_ref[...]).astype(BF16)
        hf_ref[...] = hf
        x3 = x2v
        for c0, c1 in FF_CHUNKS:
            gv = _dot_nt(hf, wg_ref[c0:c1, :])
            uv = _dot_nt(hf, wu_ref[c0:c1, :])
            av = (gv * jax.nn.sigmoid(gv) * uv).astype(BF16)
            g_ref[:, c0:c1] = gv.astype(BF16)
            u_ref[:, c0:c1] = uv.astype(BF16)
            a_ref[:, c0:c1] = av
            x3 = x3 + _dot(av, wd_ref[c0:c1, :])
        r3 = _rinv(x3)
        yh = x3 * r3
        gfin = gfin_ref[...]
        err = yh * gfin - t_ref[...]
        loss_ref[...] += jnp.full((1, 128), 0.5 / D_MODEL, F32) * jnp.sum(err * err)
        dy = err * (1.0 / D_MODEL)
        dgfin_ref[...] += jnp.sum(dy * yh, axis=0, keepdims=True)
        dyh = dy * gfin
        dx3_ref[...] = r3 * (dyh - yh * jnp.mean(dyh * yh, axis=-1, keepdims=True))

    vec = _full((1, 1024))
    return pl.pallas_call(
        body, name="f_ffn", grid=(s_len // tm,),
        in_specs=[_rows(tm, 1024), _rows(tm, 1024), vec, vec, _any(), _any(), _any()],
        out_specs=[_rows(tm, 1024), _rows(tm, D_FF), _rows(tm, D_FF), _rows(tm, D_FF),
                   _rows(tm, 1024), _full((1, 128)), vec],
        out_shape=[_sds((s_len, 1024), BF16)] + [_sds((s_len, D_FF), BF16)] * 3
                  + [_sds((s_len, 1024), F32), _sds((1, 128), F32), _sds((1, 1024), F32)],
        scratch_shapes=[pltpu.VMEM((D_FF, 1024), BF16)] * 3 + [pltpu.SemaphoreType.DMA((3,))],
        compiler_params=_cp("arbitrary"))(x2, tgt, g_ffn, g_final, wg, wu, wd)


def _b_ffn(dx3, x2, gact, uact, g_ffn, wg, wu, wd, tm):
    s_len = x2.shape[0]

    def body(dx3_ref, x2_ref, g_ref, u_ref, gf_ref, wg_hbm, wu_hbm, wd_hbm,
             dg_ref, du_ref, dx2_ref, dgf_ref, wg_ref, wu_ref, wd_ref, w_sems):
        _load_weights_once([(wg_hbm, wg_ref), (wu_hbm, wu_ref), (wd_hbm, wd_ref)], w_sems)

        @pl.when(pl.program_id(0) == 0)
        def _():
            dgf_ref[...] = jnp.zeros_like(dgf_ref)

        dx3v = dx3_ref[...]
        dx3b = dx3v.astype(BF16)
        dhf = jnp.zeros(dx3v.shape, F32)
        for c0, c1 in FF_CHUNKS:
            da = _dot_nt(dx3b, wd_ref[c0:c1, :])
            gv = g_ref[:, c0:c1].astype(F32)
            uv = u_ref[:, c0:c1].astype(F32)
            sg = jax.nn.sigmoid(gv)
            dub = (da * gv * sg).astype(BF16)
            dgb = (da * uv * (sg * (1.0 + gv * (1.0 - sg)))).astype(BF16)
            du_ref[:, c0:c1] = dub
            dg_ref[:, c0:c1] = dgb
            dhf = dhf + _dot(dgb, wg_ref[c0:c1, :]) + _dot(dub, wu_ref[c0:c1, :])
        dx, dgf = _rms_bwd(dhf, x2_ref[...], gf_ref[...])
        dx2_ref[...] = dx3v + dx
        dgf_ref[...] += dgf

    vec = _full((1, 1024))
    return pl.pallas_call(
        body, name="b_ffn", grid=(s_len // tm,),
        in_specs=[_rows(tm, 1024), _rows(tm, 1024), _rows(tm, D_FF), _rows(tm, D_FF), vec,
                  _any(), _any(), _any()],
        out_specs=[_rows(tm, D_FF), _rows(tm, D_FF), _rows(tm, 1024), vec],
        out_shape=[_sds((s_len, D_FF), BF16)] * 2 + [_sds((s_len, 1024), F32), _sds((1, 1024), F32)],
        scratch_shapes=[pltpu.VMEM((D_FF, 1024), BF16)] * 3 + [pltpu.SemaphoreType.DMA((3,))],
        compiler_params=_cp("arbitrary"))(dx3, x2, gact, uact, g_ffn, wg, wu, wd)


def _b_mid(dx2, qx, x1, att, rec, kx, vx, wo, wq, w_out, g_cross, g_oa, g_ol, tm, comm=None):
    s_len = x1.shape[0]
    m_len = kx.shape[0]

    def body(dx2_ref, qx_ref, x1_ref, att_ref, rec_ref, kx_ref, vx_ref, wo_ref, wq_ref, wout_ref,
             gc_ref, goa_ref, gol_ref,
             dqx_ref, dx1_ref, datt_ref, drec_ref, dkx_ref, dvx_ref, dgc_ref, dgoa_ref, dgol_ref):
        @pl.when(pl.program_id(0) == 0)
        def _():
            for r in (dkx_ref, dvx_ref, dgc_ref, dgoa_ref, dgol_ref):
                r[...] = jnp.zeros_like(r)

        dx2v = dx2_ref[...]
        dox = _dot_nt(dx2v.astype(BF16), wo_ref[...])
        for h in range(X_HEADS):
            sl = slice(h * X_HEAD_DIM, (h + 1) * X_HEAD_DIM)
            q = qx_ref[:, sl]
            p, l = _xattn_probs(q, kx_ref[:, sl])
            pn = p * (1.0 / l)
            dob = dox[:, sl].astype(BF16)
            dp = _dot_nt(dob, vx_ref[:, sl])
            dvx_ref[:, sl] += _dot_tn(pn.astype(BF16), dob)
            ds = pn * (dp - jnp.sum(dp * pn, axis=-1, keepdims=True))
            dsb = (ds * X_SCALE).astype(BF16)
            dqx_ref[:, sl] = _dot(dsb, kx_ref[:, sl]).astype(BF16)
            dkx_ref[:, sl] += _dot_tn(dsb, q)
        dhc = _dot_nt(dqx_ref[...], wq_ref[...])
        dx, dgc = _rms_bwd(dhc, x1_ref[...], gc_ref[...])
        dx1 = dx2v + dx
        dx1_ref[...] = dx1
        dgc_ref[...] += dgc
        dmg = _dot_nt(dx1.astype(BF16), wout_ref[...])
        da, dgoa = _rms_bwd(dmg[:, 0:D_ATT], att_ref[...], goa_ref[...])
        datt_ref[...] = da
        dgoa_ref[...] += dgoa
        dr, dgol = _rms_bwd(dmg[:, D_ATT:1024], rec_ref[...], gol_ref[...])
        drec_ref[...] = dr
        dgol_ref[...] += dgol

    sq = _full((1024, 1024))
    mk = _full((m_len, 1024))
    return _call(
        body, "b_mid", (s_len // tm,),
        [_rows(tm, 1024), _rows(tm, 1024), _rows(tm, 1024), _rows(tm, 512), _rows(tm, 512), mk, mk,
         sq, sq, sq, _full((1, 1024)), _full((1, 512)), _full((1, 512))],
        [_rows(tm, 1024), _rows(tm, 1024), _rows(tm, 512), _rows(tm, 512), mk, mk,
         _full((1, 1024)), _full((1, 512)), _full((1, 512))],
        [_sds((s_len, 1024), BF16), _sds((s_len, 1024), F32), _sds((s_len, 512), F32),
         _sds((s_len, 512), F32), _sds((m_len, 1024), F32), _sds((m_len, 1024), F32),
         _sds((1, 1024), F32), _sds((1, 512), F32), _sds((1, 512), F32)],
        [], (dx2, qx, x1, att, rec, kx, vx, wo, wq, w_out, g_cross, g_oa, g_ol), "arbitrary", comm)


def _b_mem(dkx, dvx, mem, mn, g_mem, wk, wv):
    def body(dkx_ref, dvx_ref, mem_ref, mn_ref, g_ref, wk_ref, wv_ref, dwk_ref, dwv_ref, dgm_ref,
             dwkb_ref, dwvb_ref):
        dkb = dkx_ref[...].astype(BF16)
        dvb = dvx_ref[...].astype(BF16)
        dwk = _dot_tn(mn_ref[...], dkb)
        dwv = _dot_tn(mn_ref[...], dvb)
        dwk_ref[...] = dwk
        dwv_ref[...] = dwv
        dwkb_ref[...] = dwk.astype(BF16)
        dwvb_ref[...] = dwv.astype(BF16)
        dmn = _dot_nt(dkb, wk_ref[...]) + _dot_nt(dvb, wv_ref[...])
        mv = mem_ref[...]
        dgm_ref[...] = jnp.sum(dmn * (mv * _rinv(mv)), axis=0, keepdims=True)

    return pl.pallas_call(
        body, name="b_mem",
        out_shape=[_sds((1024, 1024), F32), _sds((1024, 1024), F32), _sds((1, 1024), F32),
                   _sds((1024, 1024), BF16), _sds((1024, 1024), BF16)],
        compiler_params=_cp())(dkx, dvx, mem, mn, g_mem, wk, wv)


def _b_lru(drec, hs, u, xg, conv_w, wrg, brg, wig, big, lam, tl, comm=None):
    s_len = xg.shape[0]
    nt = s_len // tl

    def body(drec_ref, hs_ref, hsp_ref, u_ref, xg_ref, cw_ref, wrg_ref, brg_ref, wig_ref, big_ref, l_ref,
             dxg_ref, dwrg_ref, dwig_ref, dbrg_ref, dbig_ref, dlam_ref, dcw_ref, dcb_ref,
             hbuf, abuf, dubuf, c_sc, d_sc, lam_sc, lcar, wacc_r, wacc_i):
        i = pl.program_id(0)
        tt = nt - 1 - i

        @pl.when(i == 0)
        def _():
            for r in (wacc_r, wacc_i, dbrg_ref, dbig_ref, dlam_ref, dcw_ref, dcb_ref):
                r[...] = jnp.zeros_like(r)
            abuf[tl:tl + 8, :] = jnp.zeros((8, D_LRU), F32)
            dubuf[tl:tl + 8, :] = jnp.zeros((8, D_LRU), F32)
            lcar[...] = jnp.zeros((8, D_LRU), F32)

        xu0 = xg_ref[:, 0:D_LRU]
        hsv = hs_ref[...]
        uv = u_ref[...]
        hbuf[8:8 + tl, :] = hsv
        hbuf[0:8, :] = jnp.where(tt > 0, hsp_ref[...], 0.0)
        hshift = hbuf[pl.ds(7, tl), :]
        wrg_v = wrg_ref[...]
        wig_v = wig_ref[...]
        lamv = l_ref[...]
        ub, r, ig, sp, a, mult = _lru_gates(uv, wrg_v, brg_ref[...], wig_v, big_ref[...], lamv)
        abuf[0:tl, :] = a
        c_sc[...] = abuf[pl.ds(1, tl), :]
        gel, dgel = _gelu_and_grad(xg_ref[:, D_LRU:2 * D_LRU])
        drv = drec_ref[...]
        d_sc[...] = drv * gel
        dxg_ref[:, D_LRU:2 * D_LRU] = (drv * hsv * dgel).astype(BF16)

        def grp(k, lnext):
            off = pl.multiple_of((tl // 8 - 1 - k) * 8, 8)
            l8 = _rscan8(c_sc[pl.ds(off, 8), :], d_sc[pl.ds(off, 8), :], lnext)
            lam_sc[pl.ds(off, 8), :] = l8
            return l8[0:1, :]

        lcar[0:1, :] = lax.fori_loop(0, tl // 8, grp, lcar[0:1, :])
        abuf[tl:tl + 8, :] = a[0:8, :]
        db = lam_sc[...]
        da = db * hshift
        dmult = db * (ig * uv)
        dig = db * mult * uv
        du = db * mult * ig
        dla = da * a - dmult * (a * a) / mult
        dlam_ref[...] += jnp.sum(dla * (-LRU_C) * r, axis=0, keepdims=True)
        dzr = dla * (-LRU_C * sp) * r * (1.0 - r)
        dzi = dig * ig * (1.0 - ig)
        dzrb = dzr.astype(BF16)
        dzib = dzi.astype(BF16)
        du = du + _dot_nt(dzrb, wrg_v) + _dot_nt(dzib, wig_v)
        wacc_r[...] += _dot_tn(ub, dzrb)
        wacc_i[...] += _dot_tn(ub, dzib)
        dbrg_ref[...] += jnp.sum(dzr, axis=0, keepdims=True)
        dbig_ref[...] += jnp.sum(dzi, axis=0, keepdims=True)
        dcb_ref[...] += jnp.sum(du, axis=0, keepdims=True)
        dubuf[0:tl, :] = du
        dxu0 = jnp.zeros((tl, D_LRU), F32)
        for j in range(4):
            dsh = dubuf[pl.ds(3 - j, tl), :]
            dxu0 = dxu0 + cw_ref[j:j + 1, :] * dsh
            dcw_ref[j:j + 1, :] += jnp.sum(xu0 * dsh, axis=0, keepdims=True)
        dubuf[tl:tl + 8, :] = du[0:8, :]
        dxg_ref[:, 0:D_LRU] = dxu0.astype(BF16)

        @pl.when(i == nt - 1)
        def _():
            dlam_ref[...] = dlam_ref[...] * (-jax.nn.sigmoid(-lamv))
            for n in range(LRU_BLOCKS):
                blk = slice(n * LRU_BLOCK, (n + 1) * LRU_BLOCK)
                dwrg_ref[n] = wacc_r[blk, blk]
                dwig_ref[n] = wacc_i[blk, blk]

    def rev(n):
        return pl.BlockSpec((tl, n), lambda i: (nt - 1 - i, 0))

    prev8 = pl.BlockSpec((8, D_LRU), lambda i: (jnp.maximum((nt - 1 - i) * (tl // 8) - 1, 0), 0))
    vec = _full((1, D_LRU))
    sq = _full((D_LRU, D_LRU))
    blocks_shape = (LRU_BLOCKS, LRU_BLOCK, LRU_BLOCK)
    blocks = _full(blocks_shape)
    return _call(
        body, "b_lru", (nt,),
        [rev(D_LRU), rev(D_LRU), prev8, rev(D_LRU), rev(1024), _full((4, D_LRU)), sq, vec, sq, vec, vec],
        [rev(1024), blocks, blocks, vec, vec, vec, _full((4, D_LRU)), vec],
        [_sds((s_len, 1024), BF16), _sds(blocks_shape, F32), _sds(blocks_shape, F32),
         _sds((1, D_LRU), F32), _sds((1, D_LRU), F32), _sds((1, D_LRU), F32),
         _sds((4, D_LRU), F32), _sds((1, D_LRU), F32)],
        [pltpu.VMEM((tl + 8, D_LRU), F32)] * 3 + [pltpu.VMEM((tl, D_LRU), F32)] * 3
        + [pltpu.VMEM((8, D_LRU), F32)] + [pltpu.VMEM((D_LRU, D_LRU), F32)] * 2,
        (drec, hs, hs, u, xg, conv_w, wrg, brg, wig, big, lam), "arbitrary", comm)


def _b_attn(qkv_pad, att, datt, frow, comm=None):
    s_len = datt.shape[0]
    nb = s_len // QB
    n_pair = ATT_HEADS // 2
    pair_w = 2 * HEAD_DIM

    def body(q_ref, k0, k1, k2, v0, v1, v2, o_ref, do_ref, frow_ref, dq_ref, dkv_ref, dfrow_ref,
             bias_sc, dt_sc, acc_sc):
        t = pl.program_id(0)

        @pl.when(t == 0)
        def _():
            _bias_table(frow_ref, bias_sc)
            dt_sc[...] = jnp.zeros_like(dt_sc)
            acc_sc[...] = jnp.zeros_like(acc_sc)

        @pl.when(t < nb)
        def _():
            var = jnp.minimum(t, N_BIAS - 1)
            even = _even_lanes()
            for hp in range(n_pair):
                cs = slice(hp * pair_w, (hp + 1) * pair_w)
                qt = q_ref[:, cs]
                kts = [k0[:, cs], k1[:, cs], k2[:, cs]]
                vts = [v0[:, cs], v1[:, cs], v2[:, cs]]
                kcat = jnp.concatenate(kts, axis=0)
                dot = do_ref[:, cs]
                dd = dot * o_ref[:, cs]
                dos_pair, dsbs, pbs, dqs = None, [], [], []
                for e in range(2):
                    keep = even if e == 0 else jnp.logical_not(even)
                    qm = jnp.where(keep, qt, 0)
                    p = _att_probs(qm, kts, bias_sc[var, 2 * hp + e])
                    inv = 1.0 / jnp.sum(p, axis=-1, keepdims=True)
                    dos = jnp.where(keep, dot * inv, 0.0)
                    delta = jnp.sum(jnp.where(keep, dd, 0.0), axis=-1, keepdims=True) * inv
                    dp = jnp.concatenate([_dot_nt(dos.astype(BF16), v) for v in vts], axis=1)
                    ds = p * (dp - delta)
                    dt_sc[2 * hp + e] += ds
                    dsb = ds.astype(BF16)
                    dq = _dot(dsb, kcat)
                    dqs.append(dq)
                    dsbs.append(dsb)
                    pbs.append(p.astype(BF16))
                    dos_pair = dos if e == 0 else dos_pair + dos
                dq_ref[:, cs] = (jnp.where(even, dqs[0], dqs[1]) * ATT_SCALE).astype(BF16)
                qtt = qt.astype(F32).T.astype(BF16)
                dost = dos_pair.T.astype(BF16)
                for j in range(3):
                    slot = (t + 1 + j) % 3
                    js = slice(j * QB, (j + 1) * QB)
                    for e in range(2):
                        hr = slice(e * HEAD_DIM, (e + 1) * HEAD_DIM)
                        acc_sc[slot, hp, hr, :] += _dot(qtt[hr], dsbs[e][:, js])
                        acc_sc[slot, n_pair + hp, hr, :] += _dot(dost[hr], pbs[e][:, js])

        done = (t + 1) % 3

        @pl.when(t >= 2)
        def _():
            for i in range(2 * n_pair):
                dkv_ref[:, i * pair_w:(i + 1) * pair_w] = acc_sc[done, i].T.astype(BF16)

        acc_sc[done] = jnp.zeros((2 * n_pair, pair_w, QB), F32)

        @pl.when(t == nb + 1)
        def _():
            row = lax.broadcasted_iota(jnp.int32, (8, ROLL_W), 0)
            pad = jnp.zeros((8, ROLL_W - KB), F32)
            for h in range(ATT_HEADS):
                acc8 = jnp.concatenate([dt_sc[h, 0:8, :], pad], axis=1)
                for a1 in range(1, QB // 8):
                    blk = jnp.concatenate([dt_sc[h, 8 * a1:8 * a1 + 8, :], pad], axis=1)
                    acc8 = acc8 + pltpu.roll(blk, ROLL_W - 8 * a1, 1)
                for k in range(3):
                    acc8 = jnp.where(((row >> k) & 1) == 1, pltpu.roll(acc8, ROLL_W - (1 << k), 1), acc8)
                dfrow_ref[h:h + 1, :] = jnp.sum(acc8, axis=0, keepdims=True)

    clamp = lambda t: jnp.minimum(t, nb - 1)
    qrows = pl.BlockSpec((QB, D_ATT), lambda t: (clamp(t), 0))
    return _call(
        body, "b_attn", (nb + 2,),
        _att_in_specs(clamp) + [qrows, qrows, _full((ATT_HEADS, ROLL_W))],
        [qrows, pl.BlockSpec((QB, 2 * D_ATT), lambda t: (jnp.maximum(t - 2, 0), 0)),
         _full((ATT_HEADS, ROLL_W))],
        [_sds((s_len, D_ATT), BF16), _sds((s_len, 2 * D_ATT), BF16), _sds((ATT_HEADS, ROLL_W), F32)],
        [pltpu.VMEM((N_BIAS, ATT_HEADS, QB, KB), F32), pltpu.VMEM((ATT_HEADS, QB, KB), F32),
         pltpu.VMEM((3, 2 * n_pair, pair_w, QB), F32)],
        (*([qkv_pad] * 7), att, datt, frow), "arbitrary", comm)


def _flush_grad(steps, acc, accb, out_hbm, outb_hbm, sems):
    @pl.when(pl.program_id(0) == steps - 1)
    def _():
        accb[...] = acc[...].astype(BF16)
        _copy_together([(acc, out_hbm), (accb, outb_hbm)], sems)


def _b_win(dq, dkv, dxg, h, ts):
    s_len = h.shape[0]
    steps = s_len // ts

    def body(dq_ref, dkv_ref, dxg_ref, h_ref, dw_hbm, dwb_hbm, acc, accb, sems):
        @pl.when(pl.program_id(0) == 0)
        def _():
            acc[...] = jnp.zeros_like(acc)

        dproj = jnp.concatenate([dq_ref[...], dkv_ref[...], dxg_ref[...]], axis=1)
        acc[...] += _dot_tn(h_ref[...], dproj)

        @pl.when(pl.program_id(0) == steps - 1)
        def _():
            accb[...] = acc[...].astype(BF16)
            cols = [pl.ds(s * IN_SH, IN_SH) for s in range(N_SHARD)]
            _copy_together([(acc.at[:, cols[s]], dw_hbm.at[s]) for s in range(N_SHARD)]
                           + [(accb.at[:, cols[s]], dwb_hbm.at[s]) for s in range(N_SHARD)], sems)

    shape = (N_SHARD, 1024, IN_SH)
    return pl.pallas_call(
        body, name="b_win", grid=(steps,),
        in_specs=[_rows(ts, 512), _rows(ts, 1024), _rows(ts, 1024), _rows(ts, 1024)],
        out_specs=[_any()] * 2, out_shape=[_sds(shape, F32), _sds(shape, BF16)],
        scratch_shapes=[pltpu.VMEM((1024, D_IN), F32), pltpu.VMEM((1024, D_IN), BF16),
                        pltpu.SemaphoreType.DMA((2 * N_SHARD,))],
        compiler_params=_cp("arbitrary"))(dq, dkv, dxg, h)


def _b_inproj(dq, dkv, dxg, x, dx1, g_mix, w_in_g, tm, comm=None):
    s_len = x.shape[0]

    def body(dq_ref, dkv_ref, dxg_ref, x_ref, dx1_ref, g_ref, w_hbm, gx_ref, dgm_ref, w_ref, w_sems):
        _load_w_in_once(w_hbm, w_ref, w_sems)

        @pl.when(pl.program_id(0) == 0)
        def _():
            dgm_ref[...] = jnp.zeros_like(dgm_ref)

        dproj = jnp.concatenate([dq_ref[...], dkv_ref[...], dxg_ref[...]], axis=1)
        dh = _dot_nt(dproj, w_ref[...])
        dx, dgm = _rms_bwd(dh, x_ref[...], g_ref[...])
        gx_ref[...] = dx1_ref[...] + dx
        dgm_ref[...] += dgm

    return _call(
        body, "b_inproj", (s_len // tm,),
        [_rows(tm, 512), _rows(tm, 1024), _rows(tm, 1024), _rows(tm, 1024), _rows(tm, 1024),
         _full((1, 1024)), _any()],
        [_rows(tm, 1024), _full((1, 1024))],
        [_sds((s_len, 1024), F32), _sds((1, 1024), F32)],
        [pltpu.VMEM((1024, D_IN), BF16), pltpu.SemaphoreType.DMA((N_SHARD,))],
        (dq, dkv, dxg, x, dx1, g_mix, w_in_g), "arbitrary", comm)


def _mm_tn(xa, ya, name, ts):
    s_len, k = xa.shape
    n = ya.shape[1]

    steps = s_len // ts

    def body(x_ref, y_ref, o_hbm, ob_hbm, acc, accb, sems):
        @pl.when(pl.program_id(0) == 0)
        def _():
            acc[...] = jnp.zeros_like(acc)
        acc[...] += _dot_tn(x_ref[...].astype(BF16), y_ref[...].astype(BF16))
        _flush_grad(steps, acc, accb, o_hbm, ob_hbm, sems)

    return pl.pallas_call(
        body, name=name, grid=(steps,), in_specs=[_rows(ts, k), _rows(ts, n)],
        out_specs=[_any()] * 2, out_shape=[_sds((k, n), F32), _sds((k, n), BF16)],
        scratch_shapes=[pltpu.VMEM((k, n), F32), pltpu.VMEM((k, n), BF16), pltpu.SemaphoreType.DMA((2,))],
        compiler_params=_cp("arbitrary"))(xa, ya)


PAD_KEYS = LEFT_CHUNKS * CHUNK
F_HI = PAD_KEYS - MAX_REL + 1
F_LO = PAD_KEYS + MAX_REL


def _frow_from_rel_bias(rb):
    last = rb[:, 2 * MAX_REL:2 * MAX_REL + 1]
    hi = jnp.broadcast_to(last, (ATT_HEADS, F_HI))
    mid = rb[:, 1:2 * MAX_REL][:, ::-1]
    lo = jnp.broadcast_to(rb[:, 0:1], (ATT_HEADS, KB - F_LO))
    wrap = jnp.broadcast_to(last, (ATT_HEADS, ROLL_W - KB))
    return jnp.concatenate([hi, mid, lo, wrap], axis=1)


def _rel_bias_grad_from_dfrow(df):
    g_last = jnp.sum(df[:, 0:F_HI], axis=1, keepdims=True) + jnp.sum(df[:, KB:ROLL_W], axis=1, keepdims=True)
    mid = df[:, F_HI:F_LO][:, ::-1]
    g_first = jnp.sum(df[:, F_LO:KB], axis=1, keepdims=True)
    return jnp.concatenate([g_first, mid, g_last], axis=1)


def _block_diag(w):
    eye = jnp.eye(8, dtype=w.dtype)
    return (w[:, :, None, :] * eye[:, None, :, None]).reshape(D_LRU, D_LRU)


MID = ['w_out', 'wq_c', 'wk_c', 'wv_c', 'wo_c']
TRANSPOSED = ['w_gate', 'w_up']
AG_IN_INPROJ = ['w_out', 'wq_c', 'wk_c']
AG_IN_ATTN = ['wv_c', 'wo_c', 'w_gate']
AG_IN_LRU = ['w_up']
AG_IN_MID = ['w_down']
RS_IN_MID = ['w_gate', 'w_up']
RS_IN_LRU = ['w_down']
RS_IN_ATTN = MID


def _local_step(x, mem, tgt, p, gw, shards=None, chip=None):
    s_len = x.shape[0]
    tm = min(256, s_len)
    tmb = min(512, s_len)
    tl = min(512, s_len)
    frow = _frow_from_rel_bias(p['rel_bias'])
    wrg = _block_diag(p['w_rg']).astype(BF16)
    wig = _block_diag(p['w_ig']).astype(BF16)
    gw = dict(gw)

    big, bigb, recv, part, sib = {}, {}, {}, {}, {}

    def ag(names):
        return [] if shards is None else [("ag", [shards[n] for n in names])]

    def rs(names):
        return [] if shards is None else [("rs", [bigb[n] for n in names])]

    def swap(names):
        return [] if shards is None else [("swap", [part[n] for n in names])]

    def reduce_own(names):
        if shards is not None:
            sums = _sum_parts([big[n] for n in names], [recv[n] for n in names], chip, "sum_" + names[0])
            part.update(zip(names, sums))

    h, qkv_pad, xg, *got = _f_inproj(x, p['g_mix'], gw['w_in'], tmb, ag(AG_IN_INPROJ))
    gw.update(zip(AG_IN_INPROJ, got))
    att, *got = _f_attn(qkv_pad, frow, ag(AG_IN_ATTN))
    gw.update(zip(AG_IN_ATTN, got))
    rec, u, hs, *got = _f_lru(xg, p['conv_w'], p['conv_b'], wrg, p['b_rg'], wig, p['b_ig'], p['lru_L'], tl,
                              ag(AG_IN_LRU))
    gw.update(zip(AG_IN_LRU, got))
    w_out = gw['w_out'].reshape(1024, 1024)
    wq = gw['wq_c'].reshape(1024, 1024)
    wk = gw['wk_c'].reshape(1024, 1024)
    wv = gw['wv_c'].reshape(1024, 1024)
    wo = gw['wo_c'].reshape(1024, 1024)
    mn, kx, vx = _f_mem(mem, p['g_mem'], wk, wv)
    mg, x1, hc, qx, ox, x2, *got = _f_mid(x, att, rec, p['g_out_attn'], p['g_out_lru'], w_out, p['g_cross'],
                                          wq, kx, vx, wo, tmb, ag(AG_IN_MID))
    gw.update(zip(AG_IN_MID, got))
    ffn_w = [gw[n].reshape(D_FF, 1024) for n in ('w_gate', 'w_up', 'w_down')]
    hf, gact, uact, aact, dx3, loss, dg_final = _f_ffn(x2, tgt, p['g_ffn'], p['g_final'], *ffn_w, tmb)

    ts = min(1024, s_len)
    dgact, duact, dx2, dg_ffn = _b_ffn(dx3, x2, gact, uact, p['g_ffn'], *ffn_w, tm)
    big['w_gate'], bigb['w_gate'] = _mm_tn(dgact, hf, "dw_gate", ts)
    big['w_up'], bigb['w_up'] = _mm_tn(duact, hf, "dw_up", ts)
    big['w_down'], bigb['w_down'] = _mm_tn(aact, dx3, "dw_down", ts)
    for n in ('w_gate', 'w_up', 'w_down'):
        big[n] = big[n].reshape(N_SHARD, FF_SH, 1024)
        bigb[n] = bigb[n].reshape(N_SHARD, FF_SH, 1024)

    dqx, dx1, datt, drec, dkx, dvx, dg_cross, dg_oa, dg_ol, *got = _b_mid(
        dx2, qx, x1, att, rec, kx, vx, wo, wq, w_out, p['g_cross'], p['g_out_attn'], p['g_out_lru'], tmb,
        rs(RS_IN_MID))
    recv.update(zip(RS_IN_MID, got))
    reduce_own(RS_IN_MID)
    dwk, dwv, dg_mem, dwkb, dwvb = _b_mem(dkx, dvx, mem, mn, p['g_mem'], wk, wv)
    big['wk_c'], bigb['wk_c'] = dwk, dwkb
    big['wv_c'], bigb['wv_c'] = dwv, dwvb
    big['w_out'], bigb['w_out'] = _mm_tn(mg, dx1, "dw_out", ts)
    big['wq_c'], bigb['wq_c'] = _mm_tn(hc, dqx, "dw_q", ts)
    big['wo_c'], bigb['wo_c'] = _mm_tn(ox, dx2, "dw_o", ts)
    for n in MID:
        big[n] = big[n].reshape(N_SHARD, 256, 1024)
        bigb[n] = bigb[n].reshape(N_SHARD, 256, 1024)

    dxg, dwrg, dwig, dbrg, dbig, dlam, dcw, dcb, *got = _b_lru(
        drec, hs, u, xg, p['conv_w'], wrg, p['b_rg'], wig, p['b_ig'], p['lru_L'], tl,
        rs(RS_IN_LRU) + swap(RS_IN_MID))
    recv.update(zip(RS_IN_LRU, got))
    sib.update(zip(RS_IN_MID, got[len(RS_IN_LRU):]))
    reduce_own(RS_IN_LRU)
    small = {
        'conv_w': dcw, 'conv_b': dcb, 'w_rg': dwrg, 'b_rg': dbrg, 'w_ig': dwig, 'b_ig': dbig, 'lru_L': dlam,
        'g_out_attn': dg_oa, 'g_out_lru': dg_ol, 'g_cross': dg_cross, 'g_mem': dg_mem, 'g_ffn': dg_ffn,
        'g_final': dg_final,
    }
    names = [n for n in SMALL if n in small]
    gather = [] if shards is None else [
        ("ag8", [_pack_small(names, [small[n] for n in names], loss, PACK_ROWS, "pack_small")])]
    dq, dkv, dfrow, *got = _b_attn(qkv_pad, att, datt, frow, rs(RS_IN_ATTN) + swap(RS_IN_LRU) + gather)
    recv.update(zip(RS_IN_ATTN, got))
    sib.update(zip(RS_IN_LRU, got[len(RS_IN_ATTN):]))
    packs = got[-1] if gather else None
    reduce_own(RS_IN_ATTN)
    small['rel_bias'] = _rel_bias_grad_from_dfrow(dfrow)
    big['w_in'], bigb['w_in'] = _b_win(dq, dkv, dxg, h, ts)
    if shards is None:
        grad_x, small['g_mix'] = _b_inproj(dq, dkv, dxg, x, dx1, p['g_mix'], gw['w_in'], tmb)
    else:
        nsw = len(RS_IN_ATTN)

        def copies(refs, send_sems, recv_sems):
            return _tail_copies(refs[0], refs[1], refs[2:2 + nsw], refs[2 + nsw:2 + 2 * nsw], send_sems, recv_sems)

        slots = bigb['w_in']
        bufs = ([slots, lax.empty((3,) + slots.shape[1:], slots.dtype)] + [part[n] for n in RS_IN_ATTN]
                + [lax.empty(part[n].shape, F32) for n in RS_IN_ATTN])
        sems, bufs, token = _split_start("tail_exchange_start", bufs, 3 + nsw, copies)
        grad_x, small['g_mix'] = _b_inproj(dq, dkv, dxg, x, dx1, p['g_mix'] + token[0, 0], gw['w_in'], tmb)
        bufs = _split_wait("tail_exchange_wait", sems, bufs, 3 + nsw, copies, small['g_mix'])
        recv['w_in'] = bufs[1]
        sib.update(zip(RS_IN_ATTN, bufs[2 + nsw:]))
    reduce_own(['w_in'])
    return loss, grad_x, small, big, part, sib, packs


CAST_STEPS = 4


def _cast_shards(ws, name, comm=None):
    def body(*refs):
        n = len(refs) // 2
        for src, dst in zip(refs[:n], refs[n:]):
            dst[...] = src[...].astype(BF16)

    specs = [_rows(w.shape[0] // CAST_STEPS, w.shape[1]) for w in ws]
    return _call(body, name, (CAST_STEPS,), specs, specs, [_sds(w.shape, BF16) for w in ws], [], tuple(ws),
                 "arbitrary", comm)


def _sum_parts(own4s, recv3s, chip, name):
    n = len(own4s)
    _, r, c = own4s[0].shape
    steps = _ew_steps(r, n * c * (4 + 3 * 2 + 4))
    tr = r // steps

    def body(chip_ref, *refs):
        for own_ref, rc_ref, o_ref in zip(refs[:n], refs[n:2 * n], refs[2 * n:]):
            o_ref[...] = ((own_ref[0] + rc_ref[0].astype(F32)) + rc_ref[1].astype(F32)) + rc_ref[2].astype(F32)

    grid_spec = pltpu.PrefetchScalarGridSpec(
        num_scalar_prefetch=1, grid=(steps,),
        in_specs=[pl.BlockSpec((1, tr, c), lambda i, ch: (ch[0], i, 0))] * n
                 + [pl.BlockSpec((3, tr, c), lambda i, ch: (0, i, 0))] * n,
        out_specs=[pl.BlockSpec((tr, c), lambda i, ch: (i, 0))] * n)
    return pl.pallas_call(body, name=name, grid_spec=grid_spec, out_shape=[_sds((r, c), F32)] * n,
                          compiler_params=_cp("parallel"))(chip, *own4s, *recv3s)


def _adamw_math(w, g, m, v):
    m = ADAM_B1 * m + (1.0 - ADAM_B1) * g
    v = ADAM_B2 * v + (1.0 - ADAM_B2) * (g * g)
    m_hat = m / (1.0 - ADAM_B1 ** ADAM_STEP)
    v_hat = v / (1.0 - ADAM_B2 ** ADAM_STEP)
    delta = -ADAM_LR * (m_hat / (jnp.sqrt(v_hat) + ADAM_EPS) + ADAM_WD * w)
    return delta, m, v


def _final_adamw(pas, pbs, ws, ms, vs, name, after=None):
    n = len(ws)
    r, c = ws[0].shape
    steps = _ew_steps(r, n * c * 9 * 4)
    tr = r // steps

    def body(*refs):
        ins, outs = refs[:5 * n], refs[len(refs) - 4 * n:]
        for k in range(n):
            pa_ref, pb_ref, w_ref, m_ref, v_ref = (ins[j * n + k] for j in range(5))
            g = pa_ref[...] + pb_ref[...]
            outs[4 * k][...] = g
            outs[4 * k + 1][...], outs[4 * k + 2][...], outs[4 * k + 3][...] = _adamw_math(
                w_ref[...], g, m_ref[...], v_ref[...])

    order = [] if after is None else [after]
    res = pl.pallas_call(
        body, name=name, grid=(steps,), in_specs=[_rows(tr, c)] * (5 * n) + [_full(t.shape) for t in order],
        out_specs=[_rows(tr, c)] * (4 * n), out_shape=[_sds((r, c), F32)] * (4 * n),
        compiler_params=_cp("parallel"))(*pas, *pbs, *ws, *ms, *vs, *order)
    return [res[4 * k:4 * k + 4] for k in range(n)]


def _pack_put(ref, name, val_ref):
    r = _pack_rows()[name]
    shape = val_ref.shape
    if len(shape) == 3:
        for b in range(shape[0]):
            ref[r:r + shape[1], b * shape[2]:(b + 1) * shape[2]] = val_ref[b]
    elif shape[1] == 2 * PACK_W:
        ref[r:r + 1, :] = val_ref[:, 0:PACK_W]
        ref[r + 1:r + 2, :] = val_ref[:, PACK_W:2 * PACK_W]
    else:
        ref[r:r + shape[0], 0:shape[1]] = val_ref[...]


def _pack_get(ref, name, shape):
    r = _pack_rows()[name]
    if len(shape) == 3:
        return jnp.stack([ref[r:r + shape[1], b * shape[2]:(b + 1) * shape[2]] for b in range(shape[0])])
    if shape[1] == 2 * PACK_W:
        return jnp.concatenate([ref[r:r + 1, :], ref[r + 1:r + 2, :]], axis=1)
    return ref[r:r + shape[0], 0:shape[1]]


def _pack_small(names, g, loss, rows, name):
    n = len(g)
    extra = [] if loss is None else [loss]

    def body(*refs):
        pack = refs[-1]
        pack[...] = jnp.zeros_like(pack)
        for a, nm in enumerate(names):
            _pack_put(pack, nm, refs[a])
        if extra:
            _pack_put(pack, 'loss', refs[n])

    return pl.pallas_call(body, name=name, out_shape=_sds((rows, PACK_W), F32), compiler_params=_cp())(*g, *extra)


def _all_peers():
    x, y, c = _mesh_pos()
    peers = []
    for k in range(1, 8):
        px = 1 - x if k & 4 else x
        py = 1 - y if k & 2 else y
        pc = 1 - c if k & 1 else c
        peers.append(((px, py, pc), 4 * px + 2 * py + pc))
    return peers, 4 * x + 2 * y + c


def _ag8_copies(ins, outs, sems):
    send_sems, recv_sems, loc_sems = sems
    n = len(ins)
    peers, me = _all_peers()

    def remote(k, j, slot):
        return pltpu.make_async_remote_copy(
            src_ref=ins[k], dst_ref=outs[k].at[slot], send_sem=send_sems.at[k, j], recv_sem=recv_sems.at[k, j],
            device_id=peers[j][0], device_id_type=MESH_ID)

    def local(k):
        return pltpu.make_async_copy(ins[k], outs[k].at[me], loc_sems.at[k])

    def start():
        for k in range(n):
            local(k).start()
            for j in range(7):
                remote(k, j, me).start()

    def wait():
        for k in range(n):
            for j in range(7):
                remote(k, j, peers[j][1]).wait_recv()
        for k in range(n):
            for j in range(7):
                remote(k, j, me).wait_send()
            local(k).wait()

    return start, _no_forward, wait


def _adamw_small(packs, late_own, late_packs, g_shapes, loss_shape, w, m, v):
    n = len(w)

    def body(*refs):
        packs_ref, own_ref, late_ref = refs[0], refs[1], refs[2]
        w_refs, m_refs, v_refs = (refs[3 + i * n:3 + (i + 1) * n] for i in range(3))
        o0 = 3 * n + 3
        go, do, mo, vo = (refs[o0 + i * n:o0 + (i + 1) * n] for i in range(4))
        loss_out, tot_ref = refs[o0 + 4 * n], refs[o0 + 4 * n + 1]
        x, y, c = _mesh_pos()
        me = 4 * x + 2 * y + c
        tot = packs_ref[0]
        late = jnp.where(me == 0, own_ref[...], late_ref[0])
        for d in range(1, 8):
            tot = tot + packs_ref[d]
            late = late + jnp.where(me == d, own_ref[...], late_ref[d])
        tot_ref[...] = tot
        tot_ref[0:LATE_ROWS, :] += late
        loss_out[...] = _pack_get(tot_ref, 'loss', loss_shape)
        for a, name in enumerate(SMALL):
            if name == 'conv_w':
                r = _pack_rows()[name]
                ga = tot_ref[r:r + g_shapes[a][0], pl.ds(pl.multiple_of((2 * x + y) * 128, 128), 128)]
            else:
                ga = _pack_get(tot_ref, name, g_shapes[a])
            go[a][...] = ga
            do[a][...], mo[a][...], vo[a][...] = _adamw_math(w_refs[a][...], ga, m_refs[a][...], v_refs[a][...])

    out_shape = [_sds(a.shape, F32) for a in w] * 4 + [_sds(loss_shape, F32)]
    return pl.pallas_call(body, name="adamw_small", out_shape=out_shape,
                          scratch_shapes=[pltpu.VMEM((PACK_ROWS, PACK_W), F32)],
                          compiler_params=_cp())(packs, late_own, late_packs, *w, *m, *v)


PACK_W = 512
PACK_ROWS = 160
LATE = ['g_mix', 'rel_bias']
LATE_ROWS = 32


def _pack_rows():
    rows, r = {}, 0
    for name in ['g_mix', 'g_cross', 'g_mem', 'g_ffn', 'g_final']:
        rows[name] = r
        r += 2
    for name in ['conv_b', 'b_rg', 'b_ig', 'lru_L', 'g_out_attn', 'g_out_lru']:
        rows[name] = r
        r += 1
    rows['conv_w'] = r
    rows['loss'] = r + 4
    rows['rel_bias'] = 24
    rows['w_rg'] = 32
    rows['w_ig'] = 32 + LRU_BLOCK
    assert r + 5 <= 24 and rows['w_ig'] + LRU_BLOCK == PACK_ROWS
    assert rows['g_mix'] + 2 <= LATE_ROWS and rows['rel_bias'] + 8 <= LATE_ROWS
    return rows


INPUT_NAMES = (['x', 'mem'] + WEIGHTS + ['loss_target'] + ['m_' + n for n in WEIGHTS] + ['v_' + n for n in WEIGHTS])


def kernel(x, mem, g_mix, w_in, rel_bias, conv_w, conv_b, w_rg, b_rg, w_ig, b_ig, lru_L, g_out_attn, g_out_lru, w_out, g_cross, g_mem, wq_c, wk_c, wv_c, wo_c, g_ffn, w_gate, w_up, w_down, g_final, loss_target, m_g_mix, m_w_in, m_rel_bias, m_conv_w, m_conv_b, m_w_rg, m_b_rg, m_w_ig, m_b_ig, m_lru_L, m_g_out_attn, m_g_out_lru, m_w_out, m_g_cross, m_g_mem, m_wq_c, m_wk_c, m_wv_c, m_wo_c, m_g_ffn, m_w_gate, m_w_up, m_w_down, m_g_final, v_g_mix, v_w_in, v_rel_bias, v_conv_w, v_conv_b, v_w_rg, v_b_rg, v_w_ig, v_b_ig, v_lru_L, v_g_out_attn, v_g_out_lru, v_w_out, v_g_cross, v_g_mem, v_wq_c, v_wk_c, v_wv_c, v_wo_c, v_g_ffn, v_w_gate, v_w_up, v_w_down, v_g_final):
    a = dict(zip(INPUT_NAMES, (x, mem, g_mix, w_in, rel_bias, conv_w, conv_b, w_rg, b_rg, w_ig, b_ig, lru_L, g_out_attn, g_out_lru, w_out, g_cross, g_mem, wq_c, wk_c, wv_c, wo_c, g_ffn, w_gate, w_up, w_down, g_final, loss_target, m_g_mix, m_w_in, m_rel_bias, m_conv_w, m_conv_b, m_w_rg, m_b_rg, m_w_ig, m_b_ig, m_lru_L, m_g_out_attn, m_g_out_lru, m_w_out, m_g_cross, m_g_mem, m_wq_c, m_wk_c, m_wv_c, m_wo_c, m_g_ffn, m_w_gate, m_w_up, m_w_down, m_g_final, v_g_mix, v_w_in, v_rel_bias, v_conv_w, v_conv_b, v_w_rg, v_b_rg, v_w_ig, v_b_ig, v_lru_L, v_g_out_attn, v_g_out_lru, v_w_out, v_g_cross, v_g_mem, v_wq_c, v_wk_c, v_wv_c, v_wo_c, v_g_ffn, v_w_gate, v_w_up, v_w_down, v_g_final)))
    chip = 2 * lax.axis_index("x") + lax.axis_index("y")

    def shard(name):
        arr = a[name][0]
        base = name[2:] if name[:2] in ('m_', 'v_') else name
        return jnp.swapaxes(arr, 0, 1) if base in TRANSPOSED else arr

    shards = {'w_in': _cast_shards([shard('w_in')], "cast_w_in")[0]}
    rest = [n for n in BIG if n != 'w_in']
    *cast, w_in_g, conv_w_g = _cast_shards([shard(n) for n in rest], "cast_rest",
                                           [("ag", [shards['w_in']]), ("agf", [a['conv_w'][0]])])
    shards.update(zip(rest, cast))
    conv_w_full = conv_w_g.transpose(1, 0, 2).reshape(4, D_LRU)

    p = {n: a[n] for n in SMALL}
    p['rel_bias'] = a['rel_bias'][0]
    p['w_rg'] = a['w_rg'][0]
    p['w_ig'] = a['w_ig'][0]
    p['conv_w'] = conv_w_full
    p['g_final'] = a['g_final'][None, :]
    chip_arr = jnp.reshape(chip, (1,)).astype(jnp.int32)
    loss_part, grad_x, small, _, part, sib, packs = _local_step(
        a['x'][0], a['mem'][0], a['loss_target'][0], p, {'w_in': w_in_g}, shards, chip_arr)

    def late_copies(refs, send_sems, recv_sems):
        return _late_copies(refs[0], refs[1], refs[2], refs[3], send_sems, recv_sems)

    late_pack = _pack_small(LATE, [small[n] for n in LATE], None, LATE_ROWS, "pack_late")
    bufs = [part['w_in'], lax.empty(part['w_in'].shape, F32), late_pack, jnp.zeros((8, LATE_ROWS, PACK_W), F32)]
    sems, bufs, token = _split_start("late_exchange_start", bufs, 8, late_copies)
    out = {}

    def adamw(group, after=None):
        results = _final_adamw([part[n] for n in group], [sib[n] for n in group], [shard(n) for n in group],
                               [shard('m_' + n) for n in group], [shard('v_' + n) for n in group],
                               "adamw_" + group[0], after)
        for n, res in zip(group, results):
            out[n] = [jnp.swapaxes(r, 0, 1) for r in res] if n in TRANSPOSED else res
        return results[-1][0]

    adamw(MID, token)
    done = adamw(['w_gate', 'w_up', 'w_down'], token)
    _, sib['w_in'], late_pack, late_packs = _split_wait("late_exchange_wait", sems, bufs, 8, late_copies, done)
    adamw(['w_in'])

    def natural(arr):
        return arr[0] if arr.ndim >= 3 else (arr[None, :] if arr.ndim == 1 else arr)

    small_out = _adamw_small(packs, late_pack, late_packs, [small[n].shape for n in SMALL],
                             loss_part.shape, *[[natural(a[pre + n]) for n in SMALL] for pre in ('', 'm_', 'v_')])
    ns = len(SMALL)
    loss = small_out[4 * ns][0, 0]

    def leaf(i, n):
        if n in BIG:
            return out[n][i][None]
        return small_out[i * ns + SMALL.index(n)].reshape(a[n].shape)

    return (loss, grad_x[None], *[leaf(i, n) for i in range(4) for n in WEIGHTS])
```

```python
import math

import jax
import jax.numpy as jnp
from jax import lax
from jax.experimental import pallas as pl
from jax.experimental.pallas import tpu as pltpu

F32 = jnp.float32
BF16 = jnp.bfloat16

D_MODEL = 1024
D_ATT = 512
D_LRU = 512
HEAD_DIM = 64
ATT_HEADS = 8
CHUNK = 64
LEFT_CHUNKS = 8
MAX_REL = 128
X_HEADS = 4
X_HEAD_DIM = 256
N_SHARD = 4
IN_SH = 640
D_IN = N_SHARD * IN_SH
FF_SH = 704
D_FF = N_SHARD * FF_SH
EPS = 1e-6
LRU_C = 8.0
LRU_BLOCKS = 8
LRU_BLOCK = 64
QB = 256
KB = 768
ROLL_W = 1024
NEG = -1e30
ATT_SCALE = HEAD_DIM ** -0.5
X_SCALE = X_HEAD_DIM ** -0.5

ADAM_LR = 0.001
ADAM_B1 = 0.9
ADAM_B2 = 0.999
ADAM_EPS = 1e-08
ADAM_WD = 0.01
ADAM_STEP = 10

VMEM_LIMIT_V7X = 56 * 1024 * 1024
BF16_ROWS = 16


EW_VMEM_BUDGET = 40 * 1024 * 1024


def _ew_steps(rows, bytes_per_row):
    return min(s for s in (2, 4, 8, 16) if rows % (s * BF16_ROWS) == 0
               and 2 * (rows // s) * bytes_per_row <= EW_VMEM_BUDGET)
MESH_ID = pl.DeviceIdType.MESH

WEIGHTS = ['g_mix', 'w_in', 'rel_bias', 'conv_w', 'conv_b', 'w_rg', 'b_rg', 'w_ig', 'b_ig', 'lru_L',
           'g_out_attn', 'g_out_lru', 'w_out', 'g_cross', 'g_mem', 'wq_c', 'wk_c', 'wv_c', 'wo_c',
           'g_ffn', 'w_gate', 'w_up', 'w_down', 'g_final']
BIG = ['w_in', 'w_out', 'wq_c', 'wk_c', 'wv_c', 'wo_c', 'w_gate', 'w_up', 'w_down']
SMALL = [n for n in WEIGHTS if n not in BIG]


def _sds(shape, dtype):
    return jax.ShapeDtypeStruct(shape, dtype)


def _cp(*sem):
    return pltpu.CompilerParams(dimension_semantics=sem or None, vmem_limit_bytes=VMEM_LIMIT_V7X)


def _rows(tm, n):
    return pl.BlockSpec((tm, n), lambda i: (i, 0))


def _full(shape):
    nd = len(shape)
    return pl.BlockSpec(shape, lambda i: (0,) * nd)


def _dot(a, b):
    return jnp.dot(a, b, preferred_element_type=F32)


def _dot_nt(a, b):
    return lax.dot_general(a, b, (((1,), (1,)), ((), ())), preferred_element_type=F32)


def _dot_tn(a, b):
    return lax.dot_general(a, b, (((0,), (0,)), ((), ())), preferred_element_type=F32)


def _rinv(x):
    return lax.rsqrt(jnp.mean(x * x, axis=-1, keepdims=True) + EPS)


def _rms_bwd(dy, x, g):
    r = _rinv(x)
    yh = x * r
    dyh = dy * g
    dx = r * (dyh - yh * jnp.mean(dyh * yh, axis=-1, keepdims=True))
    return dx, jnp.sum(dy * yh, axis=0, keepdims=True)


def _gelu(x):
    c = math.sqrt(2.0 / math.pi)
    t = jnp.tanh(c * (x + 0.044715 * x * x * x))
    return 0.5 * x * (1.0 + t)


def _gelu_and_grad(x):
    c = math.sqrt(2.0 / math.pi)
    t = jnp.tanh(c * (x + 0.044715 * x * x * x))
    g = 0.5 * x * (1.0 + t)
    dg = 0.5 * (1.0 + t) + 0.5 * x * (1.0 - t * t) * c * (1.0 + 3.0 * 0.044715 * x * x)
    return g, dg


def _neg_expm1(z):
    series = -z * (1.0 + z * (0.5 + z * ((1.0 / 6.0) + z * (1.0 / 24.0))))
    return jnp.where(z > -0.03, series, 1.0 - jnp.exp(z))


def _lru_gates(u, wrg, brg, wig, big, lam):
    ub = u.astype(BF16)
    r = jax.nn.sigmoid(_dot(ub, wrg) + brg)
    ig = jax.nn.sigmoid(_dot(ub, wig) + big)
    sp = jnp.maximum(-lam, 0.0) + jnp.log1p(jnp.exp(-jnp.abs(lam)))
    la = -LRU_C * r * sp
    a = jnp.exp(la)
    mult = jnp.sqrt(jnp.maximum(_neg_expm1(2.0 * la), 0.0))
    return ub, r, ig, sp, a, mult


def _scan8(a8, b8, hprev):
    row = lax.broadcasted_iota(jnp.int32, a8.shape, 0)
    aa, bb = a8, b8
    for d in (1, 2, 4):
        a_s = pltpu.roll(aa, d, 0)
        b_s = pltpu.roll(bb, d, 0)
        m = row >= d
        bb = jnp.where(m, aa * b_s + bb, bb)
        aa = jnp.where(m, aa * a_s, aa)
    return aa * hprev + bb


def _rscan8(c8, d8, lnext):
    row = lax.broadcasted_iota(jnp.int32, c8.shape, 0)
    cc, dd = c8, d8
    for d in (1, 2, 4):
        c_s = pltpu.roll(cc, 8 - d, 0)
        d_s = pltpu.roll(dd, 8 - d, 0)
        m = row < 8 - d
        dd = jnp.where(m, cc * d_s + dd, dd)
        cc = jnp.where(m, cc * c_s, cc)
    return cc * lnext + dd


def _mesh_pos():
    return lax.axis_index("x"), lax.axis_index("y"), lax.axis_index("c")


def _other_chips(x, y):
    return [(1 - x, y), (x, 1 - y), (1 - x, 1 - y)]


def _no_forward():
    pass


def _ag_full_copies(ins, outs, sems):
    send_sems, recv_sems, loc_sems = sems
    n = len(ins)
    x, y, c = _mesh_pos()
    mine = 2 * x + y
    chips = _other_chips(x, y)

    def remote(k, j, slot):
        px, py = chips[j]
        return pltpu.make_async_remote_copy(
            src_ref=ins[k], dst_ref=outs[k].at[slot], send_sem=send_sems.at[k, j], recv_sem=recv_sems.at[k, j],
            device_id=(px, py, c), device_id_type=MESH_ID)

    def local(k):
        return pltpu.make_async_copy(ins[k], outs[k].at[mine], loc_sems.at[k])

    def start():
        for k in range(n):
            local(k).start()
            for j in range(3):
                remote(k, j, mine).start()

    def wait():
        for k in range(n):
            for j, (px, py) in enumerate(chips):
                remote(k, j, 2 * px + py).wait_recv()
        for k in range(n):
            for j in range(3):
                remote(k, j, mine).wait_send()
            local(k).wait()

    return start, _no_forward, wait


def _ag_copies(ins, outs, sems):
    send_sems, recv_sems, fsend_sems, frecv_sems, loc_sems = sems
    n = len(ins)
    x, y, c = _mesh_pos()
    mine = 2 * x + y
    chips = _other_chips(x, y)

    def half(ref, hc):
        r = ref.shape[0] // 2
        return ref.at[pl.ds(pl.multiple_of(hc * r, 16), r)]

    def ici(k, j, slot):
        px, py = chips[j]
        return pltpu.make_async_remote_copy(
            src_ref=half(ins[k], c), dst_ref=half(outs[k].at[slot], c),
            send_sem=send_sems.at[k, j], recv_sem=recv_sems.at[k, j],
            device_id=(px, py, c), device_id_type=MESH_ID)

    def d2d(k, j, hc):
        px, py = chips[j]
        part = half(outs[k].at[2 * px + py], hc)
        return pltpu.make_async_remote_copy(
            src_ref=part, dst_ref=part, send_sem=fsend_sems.at[k, j], recv_sem=frecv_sems.at[k, j],
            device_id=(x, y, 1 - c), device_id_type=MESH_ID)

    def local(k):
        return pltpu.make_async_copy(ins[k], outs[k].at[mine], loc_sems.at[k])

    def start():
        for k in range(n):
            local(k).start()
            for j in range(3):
                ici(k, j, mine).start()

    def forward():
        for k in range(n):
            for j, (px, py) in enumerate(chips):
                ici(k, j, 2 * px + py).wait_recv()
                d2d(k, j, c).start()

    def wait():
        for k in range(n):
            for j in range(3):
                d2d(k, j, 1 - c).wait_recv()
        for k in range(n):
            for j in range(3):
                d2d(k, j, c).wait_send()
                ici(k, j, mine).wait_send()
            local(k).wait()

    return start, forward, wait


def _rs_copies(ins, outs, sems):
    send_sems, recv_sems = sems
    n = len(ins)
    x, y, c = _mesh_pos()
    chips = _other_chips(x, y)

    def remote(k, j):
        px, py = chips[j]
        return pltpu.make_async_remote_copy(
            src_ref=ins[k].at[2 * px + py], dst_ref=outs[k].at[j],
            send_sem=send_sems.at[k, j], recv_sem=recv_sems.at[k, j],
            device_id=(px, py, c), device_id_type=MESH_ID)

    def start():
        for k in range(n):
            for j in range(3):
                remote(k, j).start()

    def wait():
        for k in range(n):
            for j in range(3):
                remote(k, j).wait_recv()
        for k in range(n):
            for j in range(3):
                remote(k, j).wait_send()

    return start, _no_forward, wait


def _swap_copies(ins, outs, sems):
    send_sems, recv_sems = sems
    x, y, c = _mesh_pos()
    copies = [pltpu.make_async_remote_copy(
        src_ref=ins[k], dst_ref=outs[k], send_sem=send_sems.at[k], recv_sem=recv_sems.at[k],
        device_id=(x, y, 1 - c), device_id_type=MESH_ID) for k in range(len(ins))]

    def start():
        for cp in copies:
            cp.start()

    def wait():
        for cp in copies:
            cp.wait()

    return start, _no_forward, wait


def _comm_plan(groups):
    plan, arrs, shapes, sems = [], [], [], []
    for kind, group in groups:
        k = len(group)
        arrs += group
        per_peer = pltpu.SemaphoreType.DMA((k, 3))
        if kind == "ag":
            shapes += [_sds((N_SHARD,) + w.shape, w.dtype) for w in group]
            gsems = [per_peer] * 4 + [pltpu.SemaphoreType.DMA((k,))]
            maker = _ag_copies
        elif kind == "agf":
            shapes += [_sds((N_SHARD,) + w.shape, w.dtype) for w in group]
            gsems = [per_peer] * 2 + [pltpu.SemaphoreType.DMA((k,))]
            maker = _ag_full_copies
        elif kind == "ag8":
            shapes += [_sds((8,) + g.shape, g.dtype) for g in group]
            gsems = [pltpu.SemaphoreType.DMA((k, 7))] * 2 + [pltpu.SemaphoreType.DMA((k,))]
            maker = _ag8_copies
        elif kind == "rs":
            shapes += [_sds((3,) + g.shape[1:], g.dtype) for g in group]
            gsems = [pltpu.SemaphoreType.DMA((k, 3)), pltpu.SemaphoreType.DMA((k, 3))]
            maker = _rs_copies
        else:
            shapes += [_sds(g.shape, g.dtype) for g in group]
            gsems = [pltpu.SemaphoreType.DMA((k,)), pltpu.SemaphoreType.DMA((k,))]
            maker = _swap_copies
        plan.append((maker, k, len(gsems)))
        sems += gsems
    return plan, arrs, shapes, sems


def _comm_fns(plan, cins, couts, sems):
    fns, a, s = [], 0, 0
    for maker, k, ns in plan:
        fns.append(maker(cins[a:a + k], couts[a:a + k], sems[s:s + ns]))
        a += k
        s += ns

    def start():
        for st, _, _ in fns:
            st()

    def forward():
        for _, fw, _ in fns:
            fw()

    def wait():
        for _, _, wt in fns:
            wt()

    return start, forward, wait


def _call(body, name, grid, in_specs, out_specs, out_shape, scratch, args, sem, comm=None):
    if not comm:
        return pl.pallas_call(body, name=name, grid=grid, in_specs=in_specs, out_specs=out_specs,
                              out_shape=out_shape, scratch_shapes=scratch, compiler_params=_cp(sem))(*args)
    plan, c_arrs, c_shapes, c_sems = _comm_plan(comm)
    k = len(c_arrs)
    n_in, n_out, n_scr = len(in_specs), len(out_specs), len(scratch)
    last = grid[0] - 1
    fwd_step = max(1, (2 * last) // 3)

    def wrapped(*refs):
        ins, cins = refs[:n_in], refs[n_in:n_in + k]
        o0 = n_in + k
        outs, couts = refs[o0:o0 + n_out], refs[o0 + n_out:o0 + n_out + k]
        s0 = o0 + n_out + k
        start, forward, wait = _comm_fns(plan, cins, couts, refs[s0 + n_scr:])
        pl.when(pl.program_id(0) == 0)(start)
        pl.when(pl.program_id(0) == fwd_step)(forward)
        body(*ins, *outs, *refs[s0:s0 + n_scr])
        pl.when(pl.program_id(0) == last)(wait)

    return pl.pallas_call(
        wrapped, name=name, grid=grid, in_specs=list(in_specs) + [_any()] * k,
        out_specs=list(out_specs) + [_any()] * k, out_shape=list(out_shape) + c_shapes,
        scratch_shapes=list(scratch) + c_sems, compiler_params=_cp(sem))(*args, *c_arrs)


def _tail_copies(slots_ref, land_ref, part_refs, sib_refs, send_sems, recv_sems):
    x, y, c = _mesh_pos()
    copies = []
    for j, (px, py) in enumerate(_other_chips(x, y)):
        copies.append(pltpu.make_async_remote_copy(
            src_ref=slots_ref.at[2 * px + py], dst_ref=land_ref.at[j], send_sem=send_sems[j], recv_sem=recv_sems[j],
            device_id=(px, py, c), device_id_type=MESH_ID))
    for k, (p_ref, s_ref) in enumerate(zip(part_refs, sib_refs)):
        copies.append(pltpu.make_async_remote_copy(
            src_ref=p_ref, dst_ref=s_ref, send_sem=send_sems[3 + k], recv_sem=recv_sems[3 + k],
            device_id=(x, y, 1 - c), device_id_type=MESH_ID))
    return copies


def _late_copies(part_ref, sib_ref, pack_ref, packs_ref, send_sems, recv_sems):
    x, y, c = _mesh_pos()
    peers, me = _all_peers()
    copies = [pltpu.make_async_remote_copy(
        src_ref=part_ref, dst_ref=sib_ref, send_sem=send_sems[0], recv_sem=recv_sems[0],
        device_id=(x, y, 1 - c), device_id_type=MESH_ID)]
    for j in range(7):
        copies.append(pltpu.make_async_remote_copy(
            src_ref=pack_ref, dst_ref=packs_ref.at[me], send_sem=send_sems[1 + j], recv_sem=recv_sems[1 + j],
            device_id=peers[j][0], device_id_type=MESH_ID))
    return copies


def _split_start(name, bufs, ncp, make_copies):
    hbm = pl.BlockSpec(memory_space=pltpu.HBM)
    sem = pl.BlockSpec(memory_space=pltpu.SEMAPHORE)
    bufs = [pltpu.with_memory_space_constraint(b, pltpu.HBM) for b in bufs]
    nb = len(bufs)

    def body(*refs):
        for cp in make_copies(refs[:nb], refs[nb:nb + ncp], refs[nb + ncp:nb + 2 * ncp]):
            cp.start()
        refs[-1][...] = jnp.zeros_like(refs[-1])

    out = pl.pallas_call(
        body, name=name,
        out_shape=[pltpu.SemaphoreType.DMA(())] * (2 * ncp) + [pltpu.HBM(b.shape, b.dtype) for b in bufs]
                  + [_sds((8, 128), F32)],
        in_specs=[hbm] * nb, out_specs=[sem] * (2 * ncp) + [hbm] * nb + [pl.BlockSpec(memory_space=pltpu.VMEM)],
        input_output_aliases={i: 2 * ncp + i for i in range(nb)},
        compiler_params=pltpu.CompilerParams(has_side_effects=pltpu.SideEffectType.DATAFLOW_SIDE_EFFECTING),
    )(*bufs)
    return out[:2 * ncp], out[2 * ncp:2 * ncp + nb], out[-1]


def _split_wait(name, sems, bufs, ncp, make_copies, after):
    nb = len(bufs)
    hbm = pl.BlockSpec(memory_space=pltpu.HBM)
    sem = pl.BlockSpec(memory_space=pltpu.SEMAPHORE)

    def body(*refs):
        for cp in make_copies(refs[:nb], refs[nb:nb + ncp], refs[nb + ncp:nb + 2 * ncp]):
            cp.wait_send()
            cp.wait_recv()

    return pl.pallas_call(
        body, name=name, out_shape=[pltpu.HBM(b.shape, b.dtype) for b in bufs],
        in_specs=[hbm] * nb + [sem] * (2 * ncp) + [_any()], out_specs=[hbm] * nb,
        input_output_aliases={i: i for i in range(nb)},
        compiler_params=pltpu.CompilerParams(has_side_effects=pltpu.SideEffectType.DATAFLOW_SIDE_EFFECTING),
    )(*bufs, *sems, after)


def _any():
    return pl.BlockSpec(memory_space=pl.ANY)


def _start_copies(pairs, sems, first=0):
    copies = [pltpu.make_async_copy(src, dst, sems.at[first + i]) for i, (src, dst) in enumerate(pairs)]
    for cp in copies:
        cp.start()
    return copies


def _copy_together(pairs, sems):
    for cp in _start_copies(pairs, sems):
        cp.wait()


def _load_w_in_once(w_hbm, w_ref, sems):
    @pl.when(pl.program_id(0) == 0)
    def _():
        _copy_together([(w_hbm.at[s], w_ref.at[:, pl.ds(s * IN_SH, IN_SH)]) for s in range(N_SHARD)], sems)


def _f_inproj(x, g_mix, w_in_g, tm, comm=None):
    s_len = x.shape[0]
    pad_rows = LEFT_CHUNKS * CHUNK
    npad = pad_rows // tm

    def body(x_ref, g_ref, w_hbm, h_ref, qkv_ref, xg_ref, w_ref, w_sems):
        i = pl.program_id(0)
        _load_w_in_once(w_hbm, w_ref, w_sems)

        @pl.when(i < npad)
        def _():
            qkv_ref[...] = jnp.zeros_like(qkv_ref)

        @pl.when(i >= npad)
        def _():
            xv = x_ref[...]
            h = (xv * _rinv(xv) * g_ref[...]).astype(BF16)
            h_ref[...] = h
            proj = _dot(h, w_ref[...])
            qkv_ref[:, 0:D_ATT] = (proj[:, 0:D_ATT] * ATT_SCALE).astype(BF16)
            qkv_ref[:, D_ATT:3 * D_ATT] = proj[:, D_ATT:3 * D_ATT].astype(BF16)
            xg_ref[...] = proj[:, 3 * D_ATT:D_IN]

    def tok(n):
        return pl.BlockSpec((tm, n), lambda i: (jnp.maximum(i - npad, 0), 0))

    return _call(
        body, "f_inproj", (s_len // tm + npad,),
        [tok(1024), _full((1, 1024)), _any()],
        [tok(1024), _rows(tm, 1536), tok(1024)],
        [_sds((s_len, 1024), BF16), _sds((s_len + pad_rows, 1536), BF16), _sds((s_len, 1024), F32)],
        [pltpu.VMEM((1024, D_IN), BF16), pltpu.SemaphoreType.DMA((N_SHARD,))], (x, g_mix, w_in_g), "arbitrary", comm)


N_BIAS = 3


def _bias_table(frow_ref, bias_sc):
    qa = lax.broadcasted_iota(jnp.int32, (QB, KB), 0) // CHUNK
    kcol = lax.broadcasted_iota(jnp.int32, (QB, KB), 1)
    kb = kcol // CHUNK
    band = jnp.where((kb >= qa) & (kb - qa <= LEFT_CHUNKS), 0.0, NEG).astype(F32)
    for h in range(ATT_HEADS):
        row = jnp.broadcast_to(frow_ref[h:h + 1, :], (QB, ROLL_W))
        toep = pltpu.roll(row, 0, 1, stride=1, stride_axis=0)
        gen = toep[:, 0:KB] + band
        bias_sc[N_BIAS - 1, h] = gen
        for v in range(N_BIAS - 1):
            pad_keys = LEFT_CHUNKS * CHUNK - v * QB
            bias_sc[v, h] = gen + jnp.where(kcol < pad_keys, NEG, 0.0).astype(F32)


def _even_lanes():
    return lax.broadcasted_iota(jnp.int32, (1, 2 * HEAD_DIM), 1) < HEAD_DIM


def _att_probs(qm, kts, bias):
    s = jnp.concatenate([_dot_nt(qm, k) for k in kts], axis=1) + bias
    return jnp.exp(s - jnp.max(s, axis=-1, keepdims=True))


def _att_in_specs(clamp):
    def spec(j, col):
        return pl.BlockSpec((QB, D_ATT), lambda i: (clamp(i) + j, col))
    return [spec(2, 0), spec(0, 1), spec(1, 1), spec(2, 1), spec(0, 2), spec(1, 2), spec(2, 2)]


def _f_attn(qkv_pad, frow, comm=None):
    s_len = qkv_pad.shape[0] - LEFT_CHUNKS * CHUNK
    nb = s_len // QB

    def body(q_ref, k0, k1, k2, v0, v1, v2, frow_ref, o_ref, bias_sc):
        i = pl.program_id(0)

        @pl.when(i == 0)
        def _():
            _bias_table(frow_ref, bias_sc)

        var = jnp.minimum(i, N_BIAS - 1)
        even = _even_lanes()
        for hp in range(ATT_HEADS // 2):
            cs = slice(hp * 2 * HEAD_DIM, (hp + 1) * 2 * HEAD_DIM)
            qt = q_ref[:, cs]
            kts = [k0[:, cs], k1[:, cs], k2[:, cs]]
            vts = [v0[:, cs], v1[:, cs], v2[:, cs]]
            res = []
            for e in range(2):
                keep = even if e == 0 else jnp.logical_not(even)
                pb = _att_probs(jnp.where(keep, qt, 0), kts, bias_sc[var, 2 * hp + e]).astype(BF16)
                r = _dot(pb, jnp.concatenate([jnp.where(keep, v, 1) for v in vts], axis=0))
                res.append(r / pltpu.roll(r, HEAD_DIM, 1))
            o_ref[:, cs] = jnp.where(even, res[0], res[1])

    return _call(
        body, "f_attn", (nb,),
        _att_in_specs(lambda i: i) + [_full((ATT_HEADS, ROLL_W))],
        [_rows(QB, D_ATT)], [_sds((s_len, D_ATT), F32)],
        [pltpu.VMEM((N_BIAS, ATT_HEADS, QB, KB), F32)], (*([qkv_pad] * 7), frow), "arbitrary", comm)


def _f_lru(xg, conv_w, conv_b, wrg, brg, wig, big, lam, tl, comm=None):
    s_len = xg.shape[0]

    def body(xg_ref, cw_ref, cb_ref, wrg_ref, brg_ref, wig_ref, big_ref, l_ref,
             rec_ref, u_ref, hs_ref, xbuf, a_sc, b_sc, hcar):
        i = pl.program_id(0)

        @pl.when(i == 0)
        def _():
            xbuf[0:8, :] = jnp.zeros((8, D_LRU), F32)
            hcar[...] = jnp.zeros((8, D_LRU), F32)

        xu0 = xg_ref[:, 0:D_LRU]
        xbuf[8:8 + tl, :] = xu0
        u = cb_ref[...] + cw_ref[0:1, :] * xbuf[pl.ds(5, tl), :]
        for j in range(1, 4):
            u = u + cw_ref[j:j + 1, :] * xbuf[pl.ds(5 + j, tl), :]
        xbuf[0:8, :] = xu0[tl - 8:tl, :]
        u_ref[...] = u
        _, _, ig, _, a, mult = _lru_gates(u, wrg_ref[...], brg_ref[...], wig_ref[...], big_ref[...], l_ref[...])
        a_sc[...] = a
        b_sc[...] = mult * (ig * u)

        def grp(g, hprev):
            off = pl.multiple_of(g * 8, 8)
            h8 = _scan8(a_sc[pl.ds(off, 8), :], b_sc[pl.ds(off, 8), :], hprev)
            hs_ref[pl.ds(off, 8), :] = h8
            return h8[7:8, :]

        hcar[0:1, :] = lax.fori_loop(0, tl // 8, grp, hcar[0:1, :])
        rec_ref[...] = hs_ref[...] * _gelu(xg_ref[:, D_LRU:2 * D_LRU])

    vec = _full((1, D_LRU))
    return _call(
        body, "f_lru", (s_len // tl,),
        [_rows(tl, 1024), _full((4, D_LRU)), vec, _full((D_LRU, D_LRU)), vec, _full((D_LRU, D_LRU)), vec, vec],
        [_rows(tl, D_LRU)] * 3, [_sds((s_len, D_LRU), F32)] * 3,
        [pltpu.VMEM((tl + 8, D_LRU), F32), pltpu.VMEM((tl, D_LRU), F32),
         pltpu.VMEM((tl, D_LRU), F32), pltpu.VMEM((8, D_LRU), F32)],
        (xg, conv_w, conv_b, wrg, brg, wig, big, lam), "arbitrary", comm)


def _f_mem(mem, g_mem, wk, wv):
    def body(mem_ref, g_ref, wk_ref, wv_ref, mn_ref, kx_ref, vx_ref):
        mv = mem_ref[...]
        mn = (mv * _rinv(mv) * g_ref[...]).astype(BF16)
        mn_ref[...] = mn
        kx_ref[...] = _dot(mn, wk_ref[...]).astype(BF16)
        vx_ref[...] = _dot(mn, wv_ref[...]).astype(BF16)

    m = mem.shape[0]
    return pl.pallas_call(
        body, name="f_mem", out_shape=[_sds((m, 1024), BF16)] * 3,
        compiler_params=_cp())(mem, g_mem, wk, wv)


def _xattn_probs(q, k):
    s = _dot_nt(q, k) * X_SCALE
    m = jnp.max(s, axis=-1, keepdims=True)
    p = jnp.exp(s - m)
    return p, jnp.sum(p, axis=-1, keepdims=True)


def _f_mid(x, att, rec, g_oa, g_ol, w_out, g_cross, wq, kx, vx, wo, tm, comm=None):
    s_len = x.shape[0]
    m_len = kx.shape[0]

    def body(x_ref, att_ref, rec_ref, goa_ref, gol_ref, wout_ref, gc_ref, wq_ref, kx_ref, vx_ref, wo_ref,
             mg_ref, x1_ref, hc_ref, qx_ref, ox_ref, x2_ref):
        av = att_ref[...]
        rv = rec_ref[...]
        mg_ref[:, 0:D_ATT] = (av * _rinv(av) * goa_ref[...]).astype(BF16)
        mg_ref[:, D_ATT:1024] = (rv * _rinv(rv) * gol_ref[...]).astype(BF16)
        x1 = x_ref[...] + _dot(mg_ref[...], wout_ref[...])
        x1_ref[...] = x1
        hc = (x1 * _rinv(x1) * gc_ref[...]).astype(BF16)
        hc_ref[...] = hc
        qx_ref[...] = _dot(hc, wq_ref[...]).astype(BF16)
        for h in range(X_HEADS):
            sl = slice(h * X_HEAD_DIM, (h + 1) * X_HEAD_DIM)
            p, l = _xattn_probs(qx_ref[:, sl], kx_ref[:, sl])
            ox_ref[:, sl] = (_dot(p.astype(BF16), vx_ref[:, sl]) / l).astype(BF16)
        x2_ref[...] = x1 + _dot(ox_ref[...], wo_ref[...])

    sq = _full((1024, 1024))
    return _call(
        body, "f_mid", (s_len // tm,),
        [_rows(tm, 1024), _rows(tm, 512), _rows(tm, 512), _full((1, 512)), _full((1, 512)), sq,
         _full((1, 1024)), sq, _full((m_len, 1024)), _full((m_len, 1024)), sq],
        [_rows(tm, 1024)] * 6,
        [_sds((s_len, 1024), BF16), _sds((s_len, 1024), F32), _sds((s_len, 1024), BF16),
         _sds((s_len, 1024), BF16), _sds((s_len, 1024), BF16), _sds((s_len, 1024), F32)],
        [], (x, att, rec, g_oa, g_ol, w_out, g_cross, wq, kx, vx, wo), "arbitrary", comm)


def _load_weights_once(pairs, sems):
    @pl.when(pl.program_id(0) == 0)
    def _():
        _copy_together(pairs, sems)


FF_CHUNKS = [(0, 1280), (1280, D_FF)]


def _f_ffn(x2, tgt, g_ffn, g_final, wg, wu, wd, tm):
    s_len = x2.shape[0]

    def body(x2_ref, t_ref, gf_ref, gfin_ref, wg_hbm, wu_hbm, wd_hbm,
             hf_ref, g_ref, u_ref, a_ref, dx3_ref, loss_ref, dgfin_ref, wg_ref, wu_ref, wd_ref, w_sems):
        _load_weights_once([(wg_hbm, wg_ref), (wu_hbm, wu_ref), (wd_hbm, wd_ref)], w_sems)

        @pl.when(pl.program_id(0) == 0)
        def _():
            loss_ref[...] = jnp.zeros_like(loss_ref)
            dgfin_ref[...] = jnp.zeros_like(dgfin_ref)

        x2v = x2_ref[...]
        hf = (x2v * _rinv(x2v) * gf_ref[...]).astype(BF16)
        hf_ref[...] = hf
        x3 = x2v
        for c0, c1 in FF_CHUNKS:
            gv = _dot_nt(hf, wg_ref[c0:c1, :])
            uv = _dot_nt(hf, wu_ref[c0:c1, :])
            av = (gv * jax.nn.sigmoid(gv) * uv).astype(BF16)
            g_ref[:, c0:c1] = gv.astype(BF16)
            u_ref[:, c0:c1] = uv.astype(BF16)
            a_ref[:, c0:c1] = av
            x3 = x3 + _dot(av, wd_ref[c0:c1, :])
        r3 = _rinv(x3)
        yh = x3 * r3
        gfin = gfin_ref[...]
        err = yh * gfin - t_ref[...]
        loss_ref[...] += jnp.full((1, 128), 0.5 / D_MODEL, F32) * jnp.sum(err * err)
        dy = err * (1.0 / D_MODEL)
        dgfin_ref[...] += jnp.sum(dy * yh, axis=0, keepdims=True)
        dyh = dy * gfin
        dx3_ref[...] = r3 * (dyh - yh * jnp.mean(dyh * yh, axis=-1, keepdims=True))

    vec = _full((1, 1024))
    return pl.pallas_call(
        body, name="f_ffn", grid=(s_len // tm,),
        in_specs=[_rows(tm, 1024), _rows(tm, 1024), vec, vec, _any(), _any(), _any()],
        out_specs=[_rows(tm, 1024), _rows(tm, D_FF), _rows(tm, D_FF), _rows(tm, D_FF),
                   _rows(tm, 1024), _full((1, 128)), vec],
        out_shape=[_sds((s_len, 1024), BF16)] + [_sds((s_len, D_FF), BF16)] * 3
                  + [_sds((s_len, 1024), F32), _sds((1, 128), F32), _sds((1, 1024), F32)],
        scratch_shapes=[pltpu.VMEM((D_FF, 1024), BF16)] * 3 + [pltpu.SemaphoreType.DMA((3,))],
        compiler_params=_cp("arbitrary"))(x2, tgt, g_ffn, g_final, wg, wu, wd)


def _b_ffn(dx3, x2, gact, uact, g_ffn, wg, wu, wd, tm):
    s_len = x2.shape[0]

    def body(dx3_ref, x2_ref, g_ref, u_ref, gf_ref, wg_hbm, wu_hbm, wd_hbm,
             dg_ref, du_ref, dx2_ref, dgf_ref, wg_ref, wu_ref, wd_ref, w_sems):
        _load_weights_once([(wg_hbm, wg_ref), (wu_hbm, wu_ref), (wd_hbm, wd_ref)], w_sems)

        @pl.when(pl.program_id(0) == 0)
        def _():
            dgf_ref[...] = jnp.zeros_like(dgf_ref)

        dx3v = dx3_ref[...]
        dx3b = dx3v.astype(BF16)
        dhf = jnp.zeros(dx3v.shape, F32)
        for c0, c1 in FF_CHUNKS:
            da = _dot_nt(dx3b, wd_ref[c0:c1, :])
            gv = g_ref[:, c0:c1].astype(F32)
            uv = u_ref[:, c0:c1].astype(F32)
            sg = jax.nn.sigmoid(gv)
            dub = (da * gv * sg).astype(BF16)
            dgb = (da * uv * (sg * (1.0 + gv * (1.0 - sg)))).astype(BF16)
            du_ref[:, c0:c1] = dub
            dg_ref[:, c0:c1] = dgb
            dhf = dhf + _dot(dgb, wg_ref[c0:c1, :]) + _dot(dub, wu_ref[c0:c1, :])
        dx, dgf = _rms_bwd(dhf, x2_ref[...], gf_ref[...])
        dx2_ref[...] = dx3v + dx
        dgf_ref[...] += dgf

    vec = _full((1, 1024))
    return pl.pallas_call(
        body, name="b_ffn", grid=(s_len // tm,),
        in_specs=[_rows(tm, 1024), _rows(tm, 1024), _rows(tm, D_FF), _rows(tm, D_FF), vec,
                  _any(), _any(), _any()],
        out_specs=[_rows(tm, D_FF), _rows(tm, D_FF), _rows(tm, 1024), vec],
        out_shape=[_sds((s_len, D_FF), BF16)] * 2 + [_sds((s_len, 1024), F32), _sds((1, 1024), F32)],
        scratch_shapes=[pltpu.VMEM((D_FF, 1024), BF16)] * 3 + [pltpu.SemaphoreType.DMA((3,))],
        compiler_params=_cp("arbitrary"))(dx3, x2, gact, uact, g_ffn, wg, wu, wd)


def _b_mid(dx2, qx, x1, att, rec, kx, vx, wo, wq, w_out, g_cross, g_oa, g_ol, tm, comm=None):
    s_len = x1.shape[0]
    m_len = kx.shape[0]

    def body(dx2_ref, qx_ref, x1_ref, att_ref, rec_ref, kx_ref, vx_ref, wo_ref, wq_ref, wout_ref,
             gc_ref, goa_ref, gol_ref,
             dqx_ref, dx1_ref, datt_ref, drec_ref, dkx_ref, dvx_ref, dgc_ref, dgoa_ref, dgol_ref):
        @pl.when(pl.program_id(0) == 0)
        def _():
            for r in (dkx_ref, dvx_ref, dgc_ref, dgoa_ref, dgol_ref):
                r[...] = jnp.zeros_like(r)

        dx2v = dx2_ref[...]
        dox = _dot_nt(dx2v.astype(BF16), wo_ref[...])
        for h in range(X_HEADS):
            sl = slice(h * X_HEAD_DIM, (h + 1) * X_HEAD_DIM)
            q = qx_ref[:, sl]
            p, l = _xattn_probs(q, kx_ref[:, sl])
            pn = p * (1.0 / l)
            dob = dox[:, sl].astype(BF16)
            dp = _dot_nt(dob, vx_ref[:, sl])
            dvx_ref[:, sl] += _dot_tn(pn.astype(BF16), dob)
            ds = pn * (dp - jnp.sum(dp * pn, axis=-1, keepdims=True))
            dsb = (ds * X_SCALE).astype(BF16)
            dqx_ref[:, sl] = _dot(dsb, kx_ref[:, sl]).astype(BF16)
            dkx_ref[:, sl] += _dot_tn(dsb, q)
        dhc = _dot_nt(dqx_ref[...], wq_ref[...])
        dx, dgc = _rms_bwd(dhc, x1_ref[...], gc_ref[...])
        dx1 = dx2v + dx
        dx1_ref[...] = dx1
        dgc_ref[...] += dgc
        dmg = _dot_nt(dx1.astype(BF16), wout_ref[...])
        da, dgoa = _rms_bwd(dmg[:, 0:D_ATT], att_ref[...], goa_ref[...])
        datt_ref[...] = da
        dgoa_ref[...] += dgoa
        dr, dgol = _rms_bwd(dmg[:, D_ATT:1024], rec_ref[...], gol_ref[...])
        drec_ref[...] = dr
        dgol_ref[...] += dgol

    sq = _full((1024, 1024))
    mk = _full((m_len, 1024))
    return _call(
        body, "b_mid", (s_len // tm,),
        [_rows(tm, 1024), _rows(tm, 1024), _rows(tm, 1024), _rows(tm, 512), _rows(tm, 512), mk, mk,
         sq, sq, sq, _full((1, 1024)), _full((1, 512)), _full((1, 512))],
        [_rows(tm, 1024), _rows(tm, 1024), _rows(tm, 512), _rows(tm, 512), mk, mk,
         _full((1, 1024)), _full((1, 512)), _full((1, 512))],
        [_sds((s_len, 1024), BF16), _sds((s_len, 1024), F32), _sds((s_len, 512), F32),
         _sds((s_len, 512), F32), _sds((m_len, 1024), F32), _sds((m_len, 1024), F32),
         _sds((1, 1024), F32), _sds((1, 512), F32), _sds((1, 512), F32)],
        [], (dx2, qx, x1, att, rec, kx, vx, wo, wq, w_out, g_cross, g_oa, g_ol), "arbitrary", comm)


def _b_mem(dkx, dvx, mem, mn, g_mem, wk, wv):
    def body(dkx_ref, dvx_ref, mem_ref, mn_ref, g_ref, wk_ref, wv_ref, dwk_ref, dwv_ref, dgm_ref,
             dwkb_ref, dwvb_ref):
        dkb = dkx_ref[...].astype(BF16)
        dvb = dvx_ref[...].astype(BF16)
        dwk = _dot_tn(mn_ref[...], dkb)
        dwv = _dot_tn(mn_ref[...], dvb)
        dwk_ref[...] = dwk
        dwv_ref[...] = dwv
        dwkb_ref[...] = dwk.astype(BF16)
        dwvb_ref[...] = dwv.astype(BF16)
        dmn = _dot_nt(dkb, wk_ref[...]) + _dot_nt(dvb, wv_ref[...])
        mv = mem_ref[...]
        dgm_ref[...] = jnp.sum(dmn * (mv * _rinv(mv)), axis=0, keepdims=True)

    return pl.pallas_call(
        body, name="b_mem",
        out_shape=[_sds((1024, 1024), F32), _sds((1024, 1024), F32), _sds((1, 1024), F32),
                   _sds((1024, 1024), BF16), _sds((1024, 1024), BF16)],
        compiler_params=_cp())(dkx, dvx, mem, mn, g_mem, wk, wv)


def _b_lru(drec, hs, u, xg, conv_w, wrg, brg, wig, big, lam, tl, comm=None):
    s_len = xg.shape[0]
    nt = s_len // tl

    def body(drec_ref, hs_ref, hsp_ref, u_ref, xg_ref, cw_ref, wrg_ref, brg_ref, wig_ref, big_ref, l_ref,
             dxg_ref, dwrg_ref, dwig_ref, dbrg_ref, dbig_ref, dlam_ref, dcw_ref, dcb_ref,
             hbuf, abuf, dubuf, c_sc, d_sc, lam_sc, lcar, wacc_r, wacc_i):
        i = pl.program_id(0)
        tt = nt - 1 - i

        @pl.when(i == 0)
        def _():
            for r in (wacc_r, wacc_i, dbrg_ref, dbig_ref, dlam_ref, dcw_ref, dcb_ref):
                r[...] = jnp.zeros_like(r)
            abuf[tl:tl + 8, :] = jnp.zeros((8, D_LRU), F32)
            dubuf[tl:tl + 8, :] = jnp.zeros((8, D_LRU), F32)
            lcar[...] = jnp.zeros((8, D_LRU), F32)

        xu0 = xg_ref[:, 0:D_LRU]
        hsv = hs_ref[...]
        uv = u_ref[...]
        hbuf[8:8 + tl, :] = hsv
        hbuf[0:8, :] = jnp.where(tt > 0, hsp_ref[...], 0.0)
        hshift = hbuf[pl.ds(7, tl), :]
        wrg_v = wrg_ref[...]
        wig_v = wig_ref[...]
        lamv = l_ref[...]
        ub, r, ig, sp, a, mult = _lru_gates(uv, wrg_v, brg_ref[...], wig_v, big_ref[...], lamv)
        abuf[0:tl, :] = a
        c_sc[...] = abuf[pl.ds(1, tl), :]
        gel, dgel = _gelu_and_grad(xg_ref[:, D_LRU:2 * D_LRU])
        drv = drec_ref[...]
        d_sc[...] = drv * gel
        dxg_ref[:, D_LRU:2 * D_LRU] = (drv * hsv * dgel).astype(BF16)

        def grp(k, lnext):
            off = pl.multiple_of((tl // 8 - 1 - k) * 8, 8)
            l8 = _rscan8(c_sc[pl.ds(off, 8), :], d_sc[pl.ds(off, 8), :], lnext)
            lam_sc[pl.ds(off, 8), :] = l8
            return l8[0:1, :]

        lcar[0:1, :] = lax.fori_loop(0, tl // 8, grp, lcar[0:1, :])
        abuf[tl:tl + 8, :] = a[0:8, :]
        db = lam_sc[...]
        da = db * hshift
        dmult = db * (ig * uv)
        dig = db * mult * uv
        du = db * mult * ig
        dla = da * a - dmult * (a * a) / mult
        dlam_ref[...] += jnp.sum(dla * (-LRU_C) * r, axis=0, keepdims=True)
        dzr = dla * (-LRU_C * sp) * r * (1.0 - r)
        dzi = dig * ig * (1.0 - ig)
        dzrb = dzr.astype(BF16)
        dzib = dzi.astype(BF16)
        du = du + _dot_nt(dzrb, wrg_v) + _dot_nt(dzib, wig_v)
        wacc_r[...] += _dot_tn(ub, dzrb)
        wacc_i[...] += _dot_tn(ub, dzib)
        dbrg_ref[...] += jnp.sum(dzr, axis=0, keepdims=True)
        dbig_ref[...] += jnp.sum(dzi, axis=0, keepdims=True)
        dcb_ref[...] += jnp.sum(du, axis=0, keepdims=True)
        dubuf[0:tl, :] = du
        dxu0 = jnp.zeros((tl, D_LRU), F32)
        for j in range(4):
            dsh = dubuf[pl.ds(3 - j, tl), :]
            dxu0 = dxu0 + cw_ref[j:j + 1, :] * dsh
            dcw_ref[j:j + 1, :] += jnp.sum(xu0 * dsh, axis=0, keepdims=True)
        dubuf[tl:tl + 8, :] = du[0:8, :]
        dxg_ref[:, 0:D_LRU] = dxu0.astype(BF16)

        @pl.when(i == nt - 1)
        def _():
            dlam_ref[...] = dlam_ref[...] * (-jax.nn.sigmoid(-lamv))
            for n in range(LRU_BLOCKS):
                blk = slice(n * LRU_BLOCK, (n + 1) * LRU_BLOCK)
                dwrg_ref[n] = wacc_r[blk, blk]
                dwig_ref[n] = wacc_i[blk, blk]

    def rev(n):
        return pl.BlockSpec((tl, n), lambda i: (nt - 1 - i, 0))

    prev8 = pl.BlockSpec((8, D_LRU), lambda i: (jnp.maximum((nt - 1 - i) * (tl // 8) - 1, 0), 0))
    vec = _full((1, D_LRU))
    sq = _full((D_LRU, D_LRU))
    blocks_shape = (LRU_BLOCKS, LRU_BLOCK, LRU_BLOCK)
    blocks = _full(blocks_shape)
    return _call(
        body, "b_lru", (nt,),
        [rev(D_LRU), rev(D_LRU), prev8, rev(D_LRU), rev(1024), _full((4, D_LRU)), sq, vec, sq, vec, vec],
        [rev(1024), blocks, blocks, vec, vec, vec, _full((4, D_LRU)), vec],
        [_sds((s_len, 1024), BF16), _sds(blocks_shape, F32), _sds(blocks_shape, F32),
         _sds((1, D_LRU), F32), _sds((1, D_LRU), F32), _sds((1, D_LRU), F32),
         _sds((4, D_LRU), F32), _sds((1, D_LRU), F32)],
        [pltpu.VMEM((tl + 8, D_LRU), F32)] * 3 + [pltpu.VMEM((tl, D_LRU), F32)] * 3
        + [pltpu.VMEM((8, D_LRU), F32)] + [pltpu.VMEM((D_LRU, D_LRU), F32)] * 2,
        (drec, hs, hs, u, xg, conv_w, wrg, brg, wig, big, lam), "arbitrary", comm)


def _b_attn(qkv_pad, att, datt, frow, comm=None):
    s_len = datt.shape[0]
    nb = s_len // QB
    n_pair = ATT_HEADS // 2
    pair_w = 2 * HEAD_DIM

    def body(q_ref, k0, k1, k2, v0, v1, v2, o_ref, do_ref, frow_ref, dq_ref, dkv_ref, dfrow_ref,
             bias_sc, dt_sc, acc_sc):
        t = pl.program_id(0)

        @pl.when(t == 0)
        def _():
            _bias_table(frow_ref, bias_sc)
            dt_sc[...] = jnp.zeros_like(dt_sc)
            acc_sc[...] = jnp.zeros_like(acc_sc)

        @pl.when(t < nb)
        def _():
            var = jnp.minimum(t, N_BIAS - 1)
            even = _even_lanes()
            for hp in range(n_pair):
                cs = slice(hp * pair_w, (hp + 1) * pair_w)
                qt = q_ref[:, cs]
                kts = [k0[:, cs], k1[:, cs], k2[:, cs]]
                vts = [v0[:, cs], v1[:, cs], v2[:, cs]]
                kcat = jnp.concatenate(kts, axis=0)
                dot = do_ref[:, cs]
                dd = dot * o_ref[:, cs]
                dos_pair, dsbs, pbs, dqs = None, [], [], []
                for e in range(2):
                    keep = even if e == 0 else jnp.logical_not(even)
                    qm = jnp.where(keep, qt, 0)
                    p = _att_probs(qm, kts, bias_sc[var, 2 * hp + e])
                    inv = 1.0 / jnp.sum(p, axis=-1, keepdims=True)
                    dos = jnp.where(keep, dot * inv, 0.0)
                    delta = jnp.sum(jnp.where(keep, dd, 0.0), axis=-1, keepdims=True) * inv
                    dp = jnp.concatenate([_dot_nt(dos.astype(BF16), v) for v in vts], axis=1)
                    ds = p * (dp - delta)
                    dt_sc[2 * hp + e] += ds
                    dsb = ds.astype(BF16)
                    dq = _dot(dsb, kcat)
                    dqs.append(dq)
                    dsbs.append(dsb)
                    pbs.append(p.astype(BF16))
                    dos_pair = dos if e == 0 else dos_pair + dos
                dq_ref[:, cs] = (jnp.where(even, dqs[0], dqs[1]) * ATT_SCALE).astype(BF16)
                qtt = qt.astype(F32).T.astype(BF16)
                dost = dos_pair.T.astype(BF16)
                for j in range(3):
                    slot = (t + 1 + j) % 3
                    js = slice(j * QB, (j + 1) * QB)
                    for e in range(2):
                        hr = slice(e * HEAD_DIM, (e + 1) * HEAD_DIM)
                        acc_sc[slot, hp, hr, :] += _dot(qtt[hr], dsbs[e][:, js])
                        acc_sc[slot, n_pair + hp, hr, :] += _dot(dost[hr], pbs[e][:, js])

        done = (t + 1) % 3

        @pl.when(t >= 2)
        def _():
            for i in range(2 * n_pair):
                dkv_ref[:, i * pair_w:(i + 1) * pair_w] = acc_sc[done, i].T.astype(BF16)

        acc_sc[done] = jnp.zeros((2 * n_pair, pair_w, QB), F32)

        @pl.when(t == nb + 1)
        def _():
            row = lax.broadcasted_iota(jnp.int32, (8, ROLL_W), 0)
            pad = jnp.zeros((8, ROLL_W - KB), F32)
            for h in range(ATT_HEADS):
                acc8 = jnp.concatenate([dt_sc[h, 0:8, :], pad], axis=1)
                for a1 in range(1, QB // 8):
                    blk = jnp.concatenate([dt_sc[h, 8 * a1:8 * a1 + 8, :], pad], axis=1)
                    acc8 = acc8 + pltpu.roll(blk, ROLL_W - 8 * a1, 1)
                for k in range(3):
                    acc8 = jnp.where(((row >> k) & 1) == 1, pltpu.roll(acc8, ROLL_W - (1 << k), 1), acc8)
                dfrow_ref[h:h + 1, :] = jnp.sum(acc8, axis=0, keepdims=True)

    clamp = lambda t: jnp.minimum(t, nb - 1)
    qrows = pl.BlockSpec((QB, D_ATT), lambda t: (clamp(t), 0))
    return _call(
        body, "b_attn", (nb + 2,),
        _att_in_specs(clamp) + [qrows, qrows, _full((ATT_HEADS, ROLL_W))],
        [qrows, pl.BlockSpec((QB, 2 * D_ATT), lambda t: (jnp.maximum(t - 2, 0), 0)),
         _full((ATT_HEADS, ROLL_W))],
        [_sds((s_len, D_ATT), BF16), _sds((s_len, 2 * D_ATT), BF16), _sds((ATT_HEADS, ROLL_W), F32)],
        [pltpu.VMEM((N_BIAS, ATT_HEADS, QB, KB), F32), pltpu.VMEM((ATT_HEADS, QB, KB), F32),
         pltpu.VMEM((3, 2 * n_pair, pair_w, QB), F32)],
        (*([qkv_pad] * 7), att, datt, frow), "arbitrary", comm)


def _b_win(dq, dkv, dxg, h, ts):
    s_len = h.shape[0]
    steps = s_len // ts

    def body(dq_ref, dkv_ref, dxg_ref, h_ref, dw_hbm, dwb_hbm, acc, accb, sems):
        @pl.when(pl.program_id(0) == 0)
        def _():
            acc[...] = jnp.zeros_like(acc)

        @pl.when(pl.program_id(0) < steps - 1)
        def _():
            dproj = jnp.concatenate([dq_ref[...], dkv_ref[...], dxg_ref[...]], axis=1)
            acc[...] += _dot_tn(h_ref[...], dproj)

        @pl.when(pl.program_id(0) == steps - 1)
        def _():
            dproj = jnp.concatenate([dq_ref[...], dkv_ref[...], dxg_ref[...]], axis=1)
            copies = []
            for s in range(N_SHARD):
                cols = slice(s * IN_SH, (s + 1) * IN_SH)
                total = acc[:, cols] + _dot_tn(h_ref[...], dproj[:, cols])
                acc[:, cols] = total
                accb[:, cols] = total.astype(BF16)
                copies += _start_copies([(acc.at[:, cols], dw_hbm.at[s]), (accb.at[:, cols], dwb_hbm.at[s])],
                                        sems, 2 * s)
            for cp in copies:
                cp.wait()

    shape = (N_SHARD, 1024, IN_SH)
    return pl.pallas_call(
        body, name="b_win", grid=(steps,),
        in_specs=[_rows(ts, 512), _rows(ts, 1024), _rows(ts, 1024), _rows(ts, 1024)],
        out_specs=[_any()] * 2, out_shape=[_sds(shape, F32), _sds(shape, BF16)],
        scratch_shapes=[pltpu.VMEM((1024, D_IN), F32), pltpu.VMEM((1024, D_IN), BF16),
                        pltpu.SemaphoreType.DMA((2 * N_SHARD,))],
        compiler_params=_cp("arbitrary"))(dq, dkv, dxg, h)


def _b_inproj(dq, dkv, dxg, x, dx1, g_mix, w_in_g, tm, comm=None):
    s_len = x.shape[0]

    def body(dq_ref, dkv_ref, dxg_ref, x_ref, dx1_ref, g_ref, w_hbm, gx_ref, dgm_ref, w_ref, w_sems):
        _load_w_in_once(w_hbm, w_ref, w_sems)

        @pl.when(pl.program_id(0) == 0)
        def _():
            dgm_ref[...] = jnp.zeros_like(dgm_ref)

        dproj = jnp.concatenate([dq_ref[...], dkv_ref[...], dxg_ref[...]], axis=1)
        dh = _dot_nt(dproj, w_ref[...])
        dx, dgm = _rms_bwd(dh, x_ref[...], g_ref[...])
        gx_ref[...] = dx1_ref[...] + dx
        dgm_ref[...] += dgm

    return _call(
        body, "b_inproj", (s_len // tm,),
        [_rows(tm, 512), _rows(tm, 1024), _rows(tm, 1024), _rows(tm, 1024), _rows(tm, 1024),
         _full((1, 1024)), _any()],
        [_rows(tm, 1024), _full((1, 1024))],
        [_sds((s_len, 1024), F32), _sds((1, 1024), F32)],
        [pltpu.VMEM((1024, D_IN), BF16), pltpu.SemaphoreType.DMA((N_SHARD,))],
        (dq, dkv, dxg, x, dx1, g_mix, w_in_g), "arbitrary", comm)


MXU_DIM_V7X = 256
FLUSH_GROUPS = 4


def _mm_tn(xa, ya, name, ts):
    s_len, k = xa.shape
    n = ya.shape[1]

    steps = s_len // ts
    tiles = k // MXU_DIM_V7X
    edges = [MXU_DIM_V7X * ((tiles * g) // FLUSH_GROUPS) for g in range(FLUSH_GROUPS + 1)]

    def body(x_ref, y_ref, o_hbm, ob_hbm, acc, accb, sems):
        @pl.when(pl.program_id(0) == 0)
        def _():
            acc[...] = jnp.zeros_like(acc)

        @pl.when(pl.program_id(0) < steps - 1)
        def _():
            acc[...] += _dot_tn(x_ref[...].astype(BF16), y_ref[...].astype(BF16))

        @pl.when(pl.program_id(0) == steps - 1)
        def _():
            yb = y_ref[...].astype(BF16)
            copies = []
            for g in range(FLUSH_GROUPS):
                rows = slice(edges[g], edges[g + 1])
                total = acc[rows, :] + _dot_tn(x_ref[:, rows].astype(BF16), yb)
                acc[rows, :] = total
                accb[rows, :] = total.astype(BF16)
                copies += _start_copies([(acc.at[rows, :], o_hbm.at[rows, :]), (accb.at[rows, :], ob_hbm.at[rows, :])],
                                        sems, 2 * g)
            for cp in copies:
                cp.wait()

    return pl.pallas_call(
        body, name=name, grid=(steps,), in_specs=[_rows(ts, k), _rows(ts, n)],
        out_specs=[_any()] * 2, out_shape=[_sds((k, n), F32), _sds((k, n), BF16)],
        scratch_shapes=[pltpu.VMEM((k, n), F32), pltpu.VMEM((k, n), BF16),
                        pltpu.SemaphoreType.DMA((2 * FLUSH_GROUPS,))],
        compiler_params=_cp("arbitrary"))(xa, ya)


PAD_KEYS = LEFT_CHUNKS * CHUNK
F_HI = PAD_KEYS - MAX_REL + 1
F_LO = PAD_KEYS + MAX_REL


def _frow_from_rel_bias(rb):
    last = rb[:, 2 * MAX_REL:2 * MAX_REL + 1]
    hi = jnp.broadcast_to(last, (ATT_HEADS, F_HI))
    mid = rb[:, 1:2 * MAX_REL][:, ::-1]
    lo = jnp.broadcast_to(rb[:, 0:1], (ATT_HEADS, KB - F_LO))
    wrap = jnp.broadcast_to(last, (ATT_HEADS, ROLL_W - KB))
    return jnp.concatenate([hi, mid, lo, wrap], axis=1)


def _rel_bias_grad_from_dfrow(df):
    g_last = jnp.sum(df[:, 0:F_HI], axis=1, keepdims=True) + jnp.sum(df[:, KB:ROLL_W], axis=1, keepdims=True)
    mid = df[:, F_HI:F_LO][:, ::-1]
    g_first = jnp.sum(df[:, F_LO:KB], axis=1, keepdims=True)
    return jnp.concatenate([g_first, mid, g_last], axis=1)


def _block_diag(w):
    eye = jnp.eye(8, dtype=w.dtype)
    return (w[:, :, None, :] * eye[:, None, :, None]).reshape(D_LRU, D_LRU)


MID = ['w_out', 'wq_c', 'wk_c', 'wv_c', 'wo_c']
TRANSPOSED = ['w_gate', 'w_up']
AG_IN_INPROJ = ['w_out', 'wq_c', 'wk_c']
AG_IN_ATTN = ['wv_c', 'wo_c', 'w_gate']
AG_IN_LRU = ['w_up']
AG_IN_MID = ['w_down']
RS_IN_MID = ['w_gate', 'w_up']
RS_IN_LRU = ['w_down']
RS_IN_ATTN = MID


def _local_step(x, mem, tgt, p, gw, shards=None, chip=None):
    s_len = x.shape[0]
    tm = min(256, s_len)
    tmb = min(512, s_len)
    tl = min(512, s_len)
    frow = _frow_from_rel_bias(p['rel_bias'])
    wrg = _block_diag(p['w_rg']).astype(BF16)
    wig = _block_diag(p['w_ig']).astype(BF16)
    gw = dict(gw)

    big, bigb, recv, part, sib = {}, {}, {}, {}, {}

    def ag(names):
        return [] if shards is None else [("ag", [shards[n] for n in names])]

    def rs(names):
        return [] if shards is None else [("rs", [bigb[n] for n in names])]

    def swap(names):
        return [] if shards is None else [("swap", [part[n] for n in names])]

    def reduce_own(names):
        if shards is not None:
            sums = _sum_parts([big[n] for n in names], [recv[n] for n in names], chip, "sum_" + names[0])
            part.update(zip(names, sums))

    h, qkv_pad, xg, *got = _f_inproj(x, p['g_mix'], gw['w_in'], tmb, ag(AG_IN_INPROJ))
    gw.update(zip(AG_IN_INPROJ, got))
    att, *got = _f_attn(qkv_pad, frow, ag(AG_IN_ATTN))
    gw.update(zip(AG_IN_ATTN, got))
    rec, u, hs, *got = _f_lru(xg, p['conv_w'], p['conv_b'], wrg, p['b_rg'], wig, p['b_ig'], p['lru_L'], tl,
                              ag(AG_IN_LRU))
    gw.update(zip(AG_IN_LRU, got))
    w_out = gw['w_out'].reshape(1024, 1024)
    wq = gw['wq_c'].reshape(1024, 1024)
    wk = gw['wk_c'].reshape(1024, 1024)
    wv = gw['wv_c'].reshape(1024, 1024)
    wo = gw['wo_c'].reshape(1024, 1024)
    mn, kx, vx = _f_mem(mem, p['g_mem'], wk, wv)
    mg, x1, hc, qx, ox, x2, *got = _f_mid(x, att, rec, p['g_out_attn'], p['g_out_lru'], w_out, p['g_cross'],
                                          wq, kx, vx, wo, tmb, ag(AG_IN_MID))
    gw.update(zip(AG_IN_MID, got))
    ffn_w = [gw[n].reshape(D_FF, 1024) for n in ('w_gate', 'w_up', 'w_down')]
    hf, gact, uact, aact, dx3, loss, dg_final = _f_ffn(x2, tgt, p['g_ffn'], p['g_final'], *ffn_w, tmb)

    ts = min(1024, s_len)
    dgact, duact, dx2, dg_ffn = _b_ffn(dx3, x2, gact, uact, p['g_ffn'], *ffn_w, tm)
    big['w_gate'], bigb['w_gate'] = _mm_tn(dgact, hf, "dw_gate", ts)
    big['w_up'], bigb['w_up'] = _mm_tn(duact, hf, "dw_up", ts)
    big['w_down'], bigb['w_down'] = _mm_tn(aact, dx3, "dw_down", ts)
    for n in ('w_gate', 'w_up', 'w_down'):
        big[n] = big[n].reshape(N_SHARD, FF_SH, 1024)
        bigb[n] = bigb[n].reshape(N_SHARD, FF_SH, 1024)

    dqx, dx1, datt, drec, dkx, dvx, dg_cross, dg_oa, dg_ol, *got = _b_mid(
        dx2, qx, x1, att, rec, kx, vx, wo, wq, w_out, p['g_cross'], p['g_out_attn'], p['g_out_lru'], tmb,
        rs(RS_IN_MID))
    recv.update(zip(RS_IN_MID, got))
    reduce_own(RS_IN_MID)
    dwk, dwv, dg_mem, dwkb, dwvb = _b_mem(dkx, dvx, mem, mn, p['g_mem'], wk, wv)
    big['wk_c'], bigb['wk_c'] = dwk, dwkb
    big['wv_c'], bigb['wv_c'] = dwv, dwvb
    big['w_out'], bigb['w_out'] = _mm_tn(mg, dx1, "dw_out", ts)
    big['wq_c'], bigb['wq_c'] = _mm_tn(hc, dqx, "dw_q", ts)
    big['wo_c'], bigb['wo_c'] = _mm_tn(ox, dx2, "dw_o", ts)
    for n in MID:
        big[n] = big[n].reshape(N_SHARD, 256, 1024)
        bigb[n] = bigb[n].reshape(N_SHARD, 256, 1024)

    dxg, dwrg, dwig, dbrg, dbig, dlam, dcw, dcb, *got = _b_lru(
        drec, hs, u, xg, p['conv_w'], wrg, p['b_rg'], wig, p['b_ig'], p['lru_L'], tl,
        rs(RS_IN_LRU) + swap(RS_IN_MID))
    recv.update(zip(RS_IN_LRU, got))
    sib.update(zip(RS_IN_MID, got[len(RS_IN_LRU):]))
    reduce_own(RS_IN_LRU)
    small = {
        'conv_w': dcw, 'conv_b': dcb, 'w_rg': dwrg, 'b_rg': dbrg, 'w_ig': dwig, 'b_ig': dbig, 'lru_L': dlam,
        'g_out_attn': dg_oa, 'g_out_lru': dg_ol, 'g_cross': dg_cross, 'g_mem': dg_mem, 'g_ffn': dg_ffn,
        'g_final': dg_final,
    }
    names = [n for n in SMALL if n in small]
    gather = [] if shards is None else [
        ("ag8", [_pack_small(names, [small[n] for n in names], loss, PACK_ROWS, "pack_small")])]
    dq, dkv, dfrow, *got = _b_attn(qkv_pad, att, datt, frow, rs(RS_IN_ATTN) + swap(RS_IN_LRU) + gather)
    recv.update(zip(RS_IN_ATTN, got))
    sib.update(zip(RS_IN_LRU, got[len(RS_IN_ATTN):]))
    packs = got[-1] if gather else None
    reduce_own(RS_IN_ATTN)
    small['rel_bias'] = _rel_bias_grad_from_dfrow(dfrow)
    big['w_in'], bigb['w_in'] = _b_win(dq, dkv, dxg, h, ts)
    if shards is None:
        grad_x, small['g_mix'] = _b_inproj(dq, dkv, dxg, x, dx1, p['g_mix'], gw['w_in'], tmb)
    else:
        nsw = len(RS_IN_ATTN)

        def copies(refs, send_sems, recv_sems):
            return _tail_copies(refs[0], refs[1], refs[2:2 + nsw], refs[2 + nsw:2 + 2 * nsw], send_sems, recv_sems)

        slots = bigb['w_in']
        bufs = ([slots, lax.empty((3,) + slots.shape[1:], slots.dtype)] + [part[n] for n in RS_IN_ATTN]
                + [lax.empty(part[n].shape, F32) for n in RS_IN_ATTN])
        sems, bufs, token = _split_start("tail_exchange_start", bufs, 3 + nsw, copies)
        grad_x, small['g_mix'] = _b_inproj(dq, dkv, dxg, x, dx1, p['g_mix'] + token[0, 0], gw['w_in'], tmb)
        bufs = _split_wait("tail_exchange_wait", sems, bufs, 3 + nsw, copies, small['g_mix'])
        recv['w_in'] = bufs[1]
        sib.update(zip(RS_IN_ATTN, bufs[2 + nsw:]))
    reduce_own(['w_in'])
    return loss, grad_x, small, big, part, sib, packs


CAST_STEPS = 4


def _cast_shards(ws, name, comm=None):
    def body(*refs):
        n = len(refs) // 2
        for src, dst in zip(refs[:n], refs[n:]):
            dst[...] = src[...].astype(BF16)

    specs = [_rows(w.shape[0] // CAST_STEPS, w.shape[1]) for w in ws]
    return _call(body, name, (CAST_STEPS,), specs, specs, [_sds(w.shape, BF16) for w in ws], [], tuple(ws),
                 "arbitrary", comm)


def _sum_parts(own4s, recv3s, chip, name):
    n = len(own4s)
    _, r, c = own4s[0].shape
    steps = _ew_steps(r, n * c * (4 + 3 * 2 + 4))
    tr = r // steps

    def body(chip_ref, *refs):
        for own_ref, rc_ref, o_ref in zip(refs[:n], refs[n:2 * n], refs[2 * n:]):
            o_ref[...] = ((own_ref[0] + rc_ref[0].astype(F32)) + rc_ref[1].astype(F32)) + rc_ref[2].astype(F32)

    grid_spec = pltpu.PrefetchScalarGridSpec(
        num_scalar_prefetch=1, grid=(steps,),
        in_specs=[pl.BlockSpec((1, tr, c), lambda i, ch: (ch[0], i, 0))] * n
                 + [pl.BlockSpec((3, tr, c), lambda i, ch: (0, i, 0))] * n,
        out_specs=[pl.BlockSpec((tr, c), lambda i, ch: (i, 0))] * n)
    return pl.pallas_call(body, name=name, grid_spec=grid_spec, out_shape=[_sds((r, c), F32)] * n,
                          compiler_params=_cp("parallel"))(chip, *own4s, *recv3s)


def _adamw_math(w, g, m, v):
    m = ADAM_B1 * m + (1.0 - ADAM_B1) * g
    v = ADAM_B2 * v + (1.0 - ADAM_B2) * (g * g)
    m_hat = m / (1.0 - ADAM_B1 ** ADAM_STEP)
    v_hat = v / (1.0 - ADAM_B2 ** ADAM_STEP)
    delta = -ADAM_LR * (m_hat / (jnp.sqrt(v_hat) + ADAM_EPS) + ADAM_WD * w)
    return delta, m, v


def _final_adamw(pas, pbs, ws, ms, vs, name, after=None):
    n = len(ws)
    r, c = ws[0].shape
    steps = _ew_steps(r, n * c * 9 * 4)
    tr = r // steps

    def body(*refs):
        ins, outs = refs[:5 * n], refs[len(refs) - 4 * n:]
        for k in range(n):
            pa_ref, pb_ref, w_ref, m_ref, v_ref = (ins[j * n + k] for j in range(5))
            g = pa_ref[...] + pb_ref[...]
            outs[4 * k][...] = g
            outs[4 * k + 1][...], outs[4 * k + 2][...], outs[4 * k + 3][...] = _adamw_math(
                w_ref[...], g, m_ref[...], v_ref[...])

    order = [] if after is None else [after]
    res = pl.pallas_call(
        body, name=name, grid=(steps,), in_specs=[_rows(tr, c)] * (5 * n) + [_full(t.shape) for t in order],
        out_specs=[_rows(tr, c)] * (4 * n), out_shape=[_sds((r, c), F32)] * (4 * n),
        compiler_params=_cp("parallel"))(*pas, *pbs, *ws, *ms, *vs, *order)
    return [res[4 * k:4 * k + 4] for k in range(n)]


def _pack_put(ref, name, val_ref):
    r = _pack_rows()[name]
    shape = val_ref.shape
    if len(shape) == 3:
        for b in range(shape[0]):
            ref[r:r + shape[1], b * shape[2]:(b + 1) * shape[2]] = val_ref[b]
    elif shape[1] == 2 * PACK_W:
        ref[r:r + 1, :] = val_ref[:, 0:PACK_W]
        ref[r + 1:r + 2, :] = val_ref[:, PACK_W:2 * PACK_W]
    else:
        ref[r:r + shape[0], 0:shape[1]] = val_ref[...]


def _pack_get(ref, name, shape):
    r = _pack_rows()[name]
    if len(shape) == 3:
        return jnp.stack([ref[r:r + shape[1], b * shape[2]:(b + 1) * shape[2]] for b in range(shape[0])])
    if shape[1] == 2 * PACK_W:
        return jnp.concatenate([ref[r:r + 1, :], ref[r + 1:r + 2, :]], axis=1)
    return ref[r:r + shape[0], 0:shape[1]]


def _pack_small(names, g, loss, rows, name):
    n = len(g)
    extra = [] if loss is None else [loss]

    def body(*refs):
        pack = refs[-1]
        pack[...] = jnp.zeros_like(pack)
        for a, nm in enumerate(names):
            _pack_put(pack, nm, refs[a])
        if extra:
            _pack_put(pack, 'loss', refs[n])

    return pl.pallas_call(body, name=name, out_shape=_sds((rows, PACK_W), F32), compiler_params=_cp())(*g, *extra)


def _all_peers():
    x, y, c = _mesh_pos()
    peers = []
    for k in range(1, 8):
        px = 1 - x if k & 4 else x
        py = 1 - y if k & 2 else y
        pc = 1 - c if k & 1 else c
        peers.append(((px, py, pc), 4 * px + 2 * py + pc))
    return peers, 4 * x + 2 * y + c


def _ag8_copies(ins, outs, sems):
    send_sems, recv_sems, loc_sems = sems
    n = len(ins)
    peers, me = _all_peers()

    def remote(k, j, slot):
        return pltpu.make_async_remote_copy(
            src_ref=ins[k], dst_ref=outs[k].at[slot], send_sem=send_sems.at[k, j], recv_sem=recv_sems.at[k, j],
            device_id=peers[j][0], device_id_type=MESH_ID)

    def local(k):
        return pltpu.make_async_copy(ins[k], outs[k].at[me], loc_sems.at[k])

    def start():
        for k in range(n):
            local(k).start()
            for j in range(7):
                remote(k, j, me).start()

    def wait():
        for k in range(n):
            for j in range(7):
                remote(k, j, peers[j][1]).wait_recv()
        for k in range(n):
            for j in range(7):
                remote(k, j, me).wait_send()
            local(k).wait()

    return start, _no_forward, wait


def _adamw_small(packs, late_own, late_packs, g_shapes, loss_shape, w, m, v):
    n = len(w)

    def body(*refs):
        packs_ref, own_ref, late_ref = refs[0], refs[1], refs[2]
        w_refs, m_refs, v_refs = (refs[3 + i * n:3 + (i + 1) * n] for i in range(3))
        o0 = 3 * n + 3
        go, do, mo, vo = (refs[o0 + i * n:o0 + (i + 1) * n] for i in range(4))
        loss_out, tot_ref = refs[o0 + 4 * n], refs[o0 + 4 * n + 1]
        x, y, c = _mesh_pos()
        me = 4 * x + 2 * y + c
        tot = packs_ref[0]
        late = jnp.where(me == 0, own_ref[...], late_ref[0])
        for d in range(1, 8):
            tot = tot + packs_ref[d]
            late = late + jnp.where(me == d, own_ref[...], late_ref[d])
        tot_ref[...] = tot
        tot_ref[0:LATE_ROWS, :] += late
        loss_out[...] = _pack_get(tot_ref, 'loss', loss_shape)
        for a, name in enumerate(SMALL):
            if name == 'conv_w':
                r = _pack_rows()[name]
                ga = tot_ref[r:r + g_shapes[a][0], pl.ds(pl.multiple_of((2 * x + y) * 128, 128), 128)]
            else:
                ga = _pack_get(tot_ref, name, g_shapes[a])
            go[a][...] = ga
            do[a][...], mo[a][...], vo[a][...] = _adamw_math(w_refs[a][...], ga, m_refs[a][...], v_refs[a][...])

    out_shape = [_sds(a.shape, F32) for a in w] * 4 + [_sds(loss_shape, F32)]
    return pl.pallas_call(body, name="adamw_small", out_shape=out_shape,
                          scratch_shapes=[pltpu.VMEM((PACK_ROWS, PACK_W), F32)],
                          compiler_params=_cp())(packs, late_own, late_packs, *w, *m, *v)


PACK_W = 512
PACK_ROWS = 160
LATE = ['g_mix', 'rel_bias']
LATE_ROWS = 32


def _pack_rows():
    rows, r = {}, 0
    for name in ['g_mix', 'g_cross', 'g_mem', 'g_ffn', 'g_final']:
        rows[name] = r
        r += 2
    for name in ['conv_b', 'b_rg', 'b_ig', 'lru_L', 'g_out_attn', 'g_out_lru']:
        rows[name] = r
        r += 1
    rows['conv_w'] = r
    rows['loss'] = r + 4
    rows['rel_bias'] = 24
    rows['w_rg'] = 32
    rows['w_ig'] = 32 + LRU_BLOCK
    assert r + 5 <= 24 and rows['w_ig'] + LRU_BLOCK == PACK_ROWS
    assert rows['g_mix'] + 2 <= LATE_ROWS and rows['rel_bias'] + 8 <= LATE_ROWS
    return rows


INPUT_NAMES = (['x', 'mem'] + WEIGHTS + ['loss_target'] + ['m_' + n for n in WEIGHTS] + ['v_' + n for n in WEIGHTS])


def kernel(x, mem, g_mix, w_in, rel_bias, conv_w, conv_b, w_rg, b_rg, w_ig, b_ig, lru_L, g_out_attn, g_out_lru, w_out, g_cross, g_mem, wq_c, wk_c, wv_c, wo_c, g_ffn, w_gate, w_up, w_down, g_final, loss_target, m_g_mix, m_w_in, m_rel_bias, m_conv_w, m_conv_b, m_w_rg, m_b_rg, m_w_ig, m_b_ig, m_lru_L, m_g_out_attn, m_g_out_lru, m_w_out, m_g_cross, m_g_mem, m_wq_c, m_wk_c, m_wv_c, m_wo_c, m_g_ffn, m_w_gate, m_w_up, m_w_down, m_g_final, v_g_mix, v_w_in, v_rel_bias, v_conv_w, v_conv_b, v_w_rg, v_b_rg, v_w_ig, v_b_ig, v_lru_L, v_g_out_attn, v_g_out_lru, v_w_out, v_g_cross, v_g_mem, v_wq_c, v_wk_c, v_wv_c, v_wo_c, v_g_ffn, v_w_gate, v_w_up, v_w_down, v_g_final):
    a = dict(zip(INPUT_NAMES, (x, mem, g_mix, w_in, rel_bias, conv_w, conv_b, w_rg, b_rg, w_ig, b_ig, lru_L, g_out_attn, g_out_lru, w_out, g_cross, g_mem, wq_c, wk_c, wv_c, wo_c, g_ffn, w_gate, w_up, w_down, g_final, loss_target, m_g_mix, m_w_in, m_rel_bias, m_conv_w, m_conv_b, m_w_rg, m_b_rg, m_w_ig, m_b_ig, m_lru_L, m_g_out_attn, m_g_out_lru, m_w_out, m_g_cross, m_g_mem, m_wq_c, m_wk_c, m_wv_c, m_wo_c, m_g_ffn, m_w_gate, m_w_up, m_w_down, m_g_final, v_g_mix, v_w_in, v_rel_bias, v_conv_w, v_conv_b, v_w_rg, v_b_rg, v_w_ig, v_b_ig, v_lru_L, v_g_out_attn, v_g_out_lru, v_w_out, v_g_cross, v_g_mem, v_wq_c, v_wk_c, v_wv_c, v_wo_c, v_g_ffn, v_w_gate, v_w_up, v_w_down, v_g_final)))
    chip = 2 * lax.axis_index("x") + lax.axis_index("y")

    def shard(name):
        arr = a[name][0]
        base = name[2:] if name[:2] in ('m_', 'v_') else name
        return jnp.swapaxes(arr, 0, 1) if base in TRANSPOSED else arr

    shards = {'w_in': _cast_shards([shard('w_in')], "cast_w_in")[0]}
    rest = [n for n in BIG if n != 'w_in']
    *cast, w_in_g, conv_w_g = _cast_shards([shard(n) for n in rest], "cast_rest",
                                           [("ag", [shards['w_in']]), ("agf", [a['conv_w'][0]])])
    shards.update(zip(rest, cast))
    conv_w_full = conv_w_g.transpose(1, 0, 2).reshape(4, D_LRU)

    p = {n: a[n] for n in SMALL}
    p['rel_bias'] = a['rel_bias'][0]
    p['w_rg'] = a['w_rg'][0]
    p['w_ig'] = a['w_ig'][0]
    p['conv_w'] = conv_w_full
    p['g_final'] = a['g_final'][None, :]
    chip_arr = jnp.reshape(chip, (1,)).astype(jnp.int32)
    loss_part, grad_x, small, _, part, sib, packs = _local_step(
        a['x'][0], a['mem'][0], a['loss_target'][0], p, {'w_in': w_in_g}, shards, chip_arr)

    def late_copies(refs, send_sems, recv_sems):
        return _late_copies(refs[0], refs[1], refs[2], refs[3], send_sems, recv_sems)

    late_pack = _pack_small(LATE, [small[n] for n in LATE], None, LATE_ROWS, "pack_late")
    bufs = [part['w_in'], lax.empty(part['w_in'].shape, F32), late_pack, jnp.zeros((8, LATE_ROWS, PACK_W), F32)]
    sems, bufs, token = _split_start("late_exchange_start", bufs, 8, late_copies)
    out = {}

    def adamw(group, after=None):
        results = _final_adamw([part[n] for n in group], [sib[n] for n in group], [shard(n) for n in group],
                               [shard('m_' + n) for n in group], [shard('v_' + n) for n in group],
                               "adamw_" + group[0], after)
        for n, res in zip(group, results):
            out[n] = [jnp.swapaxes(r, 0, 1) for r in res] if n in TRANSPOSED else res
        return results[-1][0]

    adamw(MID, token)
    done = adamw(['w_gate', 'w_up', 'w_down'], token)
    _, sib['w_in'], late_pack, late_packs = _split_wait("late_exchange_wait", sems, bufs, 8, late_copies, done)
    adamw(['w_in'])

    def natural(arr):
        return arr[0] if arr.ndim >= 3 else (arr[None, :] if arr.ndim == 1 else arr)

    small_out = _adamw_small(packs, late_pack, late_packs, [small[n].shape for n in SMALL],
                             loss_part.shape, *[[natural(a[pre + n]) for n in SMALL] for pre in ('', 'm_', 'v_')])
    ns = len(SMALL)
    loss = small_out[4 * ns][0, 0]

    def leaf(i, n):
        if n in BIG:
            return out[n][i][None]
        return small_out[i * ns + SMALL.index(n)].reshape(a[n].shape)

    return (loss, grad_x[None], *[leaf(i, n) for i in range(4) for n in WEIGHTS])
```

```python
import math

import jax
import jax.numpy as jnp
from jax import lax
from jax.experimental import pallas as pl
from jax.experimental.pallas import tpu as pltpu

F32 = jnp.float32
BF16 = jnp.bfloat16

D_MODEL = 1024
D_ATT = 512
D_LRU = 512
HEAD_DIM = 64
ATT_HEADS = 8
CHUNK = 64
LEFT_CHUNKS = 8
MAX_REL = 128
X_HEADS = 4
X_HEAD_DIM = 256
N_SHARD = 4
IN_SH = 640
D_IN = N_SHARD * IN_SH
FF_SH = 704
D_FF = N_SHARD * FF_SH
EPS = 1e-6
LRU_C = 8.0
LRU_BLOCKS = 8
LRU_BLOCK = 64
QB = 256
KB = 768
ROLL_W = 1024
NEG = -1e30
ATT_SCALE = HEAD_DIM ** -0.5
X_SCALE = X_HEAD_DIM ** -0.5

ADAM_LR = 0.001
ADAM_B1 = 0.9
ADAM_B2 = 0.999
ADAM_EPS = 1e-08
ADAM_WD = 0.01
ADAM_STEP = 10

VMEM_LIMIT_V7X = 56 * 1024 * 1024
BF16_ROWS = 16


EW_VMEM_BUDGET = 40 * 1024 * 1024


def _ew_steps(rows, bytes_per_row):
    return min(s for s in (2, 4, 8, 16) if rows % (s * BF16_ROWS) == 0
               and 2 * (rows // s) * bytes_per_row <= EW_VMEM_BUDGET)
MESH_ID = pl.DeviceIdType.MESH

WEIGHTS = ['g_mix', 'w_in', 'rel_bias', 'conv_w', 'conv_b', 'w_rg', 'b_rg', 'w_ig', 'b_ig', 'lru_L',
           'g_out_attn', 'g_out_lru', 'w_out', 'g_cross', 'g_mem', 'wq_c', 'wk_c', 'wv_c', 'wo_c',
           'g_ffn', 'w_gate', 'w_up', 'w_down', 'g_final']
BIG = ['w_in', 'w_out', 'wq_c', 'wk_c', 'wv_c', 'wo_c', 'w_gate', 'w_up', 'w_down']
SMALL = [n for n in WEIGHTS if n not in BIG]


def _sds(shape, dtype):
    return jax.ShapeDtypeStruct(shape, dtype)


def _cp(*sem):
    return pltpu.CompilerParams(dimension_semantics=sem or None, vmem_limit_bytes=VMEM_LIMIT_V7X)


def _rows(tm, n):
    return pl.BlockSpec((tm, n), lambda i: (i, 0))


def _full(shape):
    nd = len(shape)
    return pl.BlockSpec(shape, lambda i: (0,) * nd)


def _dot(a, b):
    return jnp.dot(a, b, preferred_element_type=F32)


def _dot_nt(a, b):
    return lax.dot_general(a, b, (((1,), (1,)), ((), ())), preferred_element_type=F32)


def _dot_tn(a, b):
    return lax.dot_general(a, b, (((0,), (0,)), ((), ())), preferred_element_type=F32)


def _rinv(x):
    return lax.rsqrt(jnp.mean(x * x, axis=-1, keepdims=True) + EPS)


def _rms_bwd(dy, x, g):
    r = _rinv(x)
    yh = x * r
    dyh = dy * g
    dx = r * (dyh - yh * jnp.mean(dyh * yh, axis=-1, keepdims=True))
    return dx, jnp.sum(dy * yh, axis=0, keepdims=True)


def _gelu(x):
    c = math.sqrt(2.0 / math.pi)
    t = jnp.tanh(c * (x + 0.044715 * x * x * x))
    return 0.5 * x * (1.0 + t)


def _gelu_and_grad(x):
    c = math.sqrt(2.0 / math.pi)
    t = jnp.tanh(c * (x + 0.044715 * x * x * x))
    g = 0.5 * x * (1.0 + t)
    dg = 0.5 * (1.0 + t) + 0.5 * x * (1.0 - t * t) * c * (1.0 + 3.0 * 0.044715 * x * x)
    return g, dg


def _neg_expm1(z):
    series = -z * (1.0 + z * (0.5 + z * ((1.0 / 6.0) + z * (1.0 / 24.0))))
    return jnp.where(z > -0.03, series, 1.0 - jnp.exp(z))


def _lru_gates(u, wrg, brg, wig, big, lam):
    ub = u.astype(BF16)
    r = jax.nn.sigmoid(_dot(ub, wrg) + brg)
    ig = jax.nn.sigmoid(_dot(ub, wig) + big)
    sp = jnp.maximum(-lam, 0.0) + jnp.log1p(jnp.exp(-jnp.abs(lam)))
    la = -LRU_C * r * sp
    a = jnp.exp(la)
    mult = jnp.sqrt(jnp.maximum(_neg_expm1(2.0 * la), 0.0))
    return ub, r, ig, sp, a, mult


def _scan8(a8, b8, hprev):
    row = lax.broadcasted_iota(jnp.int32, a8.shape, 0)
    aa, bb = a8, b8
    for d in (1, 2, 4):
        a_s = pltpu.roll(aa, d, 0)
        b_s = pltpu.roll(bb, d, 0)
        m = row >= d
        bb = jnp.where(m, aa * b_s + bb, bb)
        aa = jnp.where(m, aa * a_s, aa)
    return aa * hprev + bb


def _rscan8(c8, d8, lnext):
    row = lax.broadcasted_iota(jnp.int32, c8.shape, 0)
    cc, dd = c8, d8
    for d in (1, 2, 4):
        c_s = pltpu.roll(cc, 8 - d, 0)
        d_s = pltpu.roll(dd, 8 - d, 0)
        m = row < 8 - d
        dd = jnp.where(m, cc * d_s + dd, dd)
        cc = jnp.where(m, cc * c_s, cc)
    return cc * lnext + dd


def _mesh_pos():
    return lax.axis_index("x"), lax.axis_index("y"), lax.axis_index("c")


def _other_chips(x, y):
    return [(1 - x, y), (x, 1 - y), (1 - x, 1 - y)]


def _no_forward():
    pass


def _ag_full_copies(ins, outs, sems):
    send_sems, recv_sems, loc_sems = sems
    n = len(ins)
    x, y, c = _mesh_pos()
    mine = 2 * x + y
    chips = _other_chips(x, y)

    def remote(k, j, slot):
        px, py = chips[j]
        return pltpu.make_async_remote_copy(
            src_ref=ins[k], dst_ref=outs[k].at[slot], send_sem=send_sems.at[k, j], recv_sem=recv_sems.at[k, j],
            device_id=(px, py, c), device_id_type=MESH_ID)

    def local(k):
        return pltpu.make_async_copy(ins[k], outs[k].at[mine], loc_sems.at[k])

    def start():
        for k in range(n):
            local(k).start()
            for j in range(3):
                remote(k, j, mine).start()

    def wait():
        for k in range(n):
            for j, (px, py) in enumerate(chips):
                remote(k, j, 2 * px + py).wait_recv()
        for k in range(n):
            for j in range(3):
                remote(k, j, mine).wait_send()
            local(k).wait()

    return start, _no_forward, wait


def _ag_copies(ins, outs, sems):
    send_sems, recv_sems, fsend_sems, frecv_sems, loc_sems = sems
    n = len(ins)
    x, y, c = _mesh_pos()
    mine = 2 * x + y
    chips = _other_chips(x, y)

    def half(ref, hc):
        r = ref.shape[0] // 2
        return ref.at[pl.ds(pl.multiple_of(hc * r, 16), r)]

    def ici(k, j, slot):
        px, py = chips[j]
        return pltpu.make_async_remote_copy(
            src_ref=half(ins[k], c), dst_ref=half(outs[k].at[slot], c),
            send_sem=send_sems.at[k, j], recv_sem=recv_sems.at[k, j],
            device_id=(px, py, c), device_id_type=MESH_ID)

    def d2d(k, j, hc):
        px, py = chips[j]
        part = half(outs[k].at[2 * px + py], hc)
        return pltpu.make_async_remote_copy(
            src_ref=part, dst_ref=part, send_sem=fsend_sems.at[k, j], recv_sem=frecv_sems.at[k, j],
            device_id=(x, y, 1 - c), device_id_type=MESH_ID)

    def local(k):
        return pltpu.make_async_copy(ins[k], outs[k].at[mine], loc_sems.at[k])

    def start():
        for k in range(n):
            local(k).start()
            for j in range(3):
                ici(k, j, mine).start()

    def forward():
        for k in range(n):
            for j, (px, py) in enumerate(chips):
                ici(k, j, 2 * px + py).wait_recv()
                d2d(k, j, c).start()

    def wait():
        for k in range(n):
            for j in range(3):
                d2d(k, j, 1 - c).wait_recv()
        for k in range(n):
            for j in range(3):
                d2d(k, j, c).wait_send()
                ici(k, j, mine).wait_send()
            local(k).wait()

    return start, forward, wait


def _rs_copies(ins, outs, sems):
    send_sems, recv_sems = sems
    n = len(ins)
    x, y, c = _mesh_pos()
    chips = _other_chips(x, y)

    def remote(k, j):
        px, py = chips[j]
        return pltpu.make_async_remote_copy(
            src_ref=ins[k].at[2 * px + py], dst_ref=outs[k].at[j],
            send_sem=send_sems.at[k, j], recv_sem=recv_sems.at[k, j],
            device_id=(px, py, c), device_id_type=MESH_ID)

    def start():
        for k in range(n):
            for j in range(3):
                remote(k, j).start()

    def wait():
        for k in range(n):
            for j in range(3):
                remote(k, j).wait_recv()
        for k in range(n):
            for j in range(3):
                remote(k, j).wait_send()

    return start, _no_forward, wait


def _swap_copies(ins, outs, sems):
    send_sems, recv_sems = sems
    x, y, c = _mesh_pos()
    copies = [pltpu.make_async_remote_copy(
        src_ref=ins[k], dst_ref=outs[k], send_sem=send_sems.at[k], recv_sem=recv_sems.at[k],
        device_id=(x, y, 1 - c), device_id_type=MESH_ID) for k in range(len(ins))]

    def start():
        for cp in copies:
            cp.start()

    def wait():
        for cp in copies:
            cp.wait()

    return start, _no_forward, wait


def _comm_plan(groups):
    plan, arrs, shapes, sems = [], [], [], []
    for kind, group in groups:
        k = len(group)
        arrs += group
        per_peer = pltpu.SemaphoreType.DMA((k, 3))
        if kind == "ag":
            shapes += [_sds((N_SHARD,) + w.shape, w.dtype) for w in group]
            gsems = [per_peer] * 4 + [pltpu.SemaphoreType.DMA((k,))]
            maker = _ag_copies
        elif kind == "agf":
            shapes += [_sds((N_SHARD,) + w.shape, w.dtype) for w in group]
            gsems = [per_peer] * 2 + [pltpu.SemaphoreType.DMA((k,))]
            maker = _ag_full_copies
        elif kind == "ag8":
            shapes += [_sds((8,) + g.shape, g.dtype) for g in group]
            gsems = [pltpu.SemaphoreType.DMA((k, 7))] * 2 + [pltpu.SemaphoreType.DMA((k,))]
            maker = _ag8_copies
        elif kind == "rs":
            shapes += [_sds((3,) + g.shape[1:], g.dtype) for g in group]
            gsems = [pltpu.SemaphoreType.DMA((k, 3)), pltpu.SemaphoreType.DMA((k, 3))]
            maker = _rs_copies
        else:
            shapes += [_sds(g.shape, g.dtype) for g in group]
            gsems = [pltpu.SemaphoreType.DMA((k,)), pltpu.SemaphoreType.DMA((k,))]
            maker = _swap_copies
        plan.append((maker, k, len(gsems)))
        sems += gsems
    return plan, arrs, shapes, sems


def _comm_fns(plan, cins, couts, sems):
    fns, a, s = [], 0, 0
    for maker, k, ns in plan:
        fns.append(maker(cins[a:a + k], couts[a:a + k], sems[s:s + ns]))
        a += k
        s += ns

    def start():
        for st, _, _ in fns:
            st()

    def forward():
        for _, fw, _ in fns:
            fw()

    def wait():
        for _, _, wt in fns:
            wt()

    return start, forward, wait


def _call(body, name, grid, in_specs, out_specs, out_shape, scratch, args, sem, comm=None):
    if not comm:
        return pl.pallas_call(body, name=name, grid=grid, in_specs=in_specs, out_specs=out_specs,
                              out_shape=out_shape, scratch_shapes=scratch, compiler_params=_cp(sem))(*args)
    plan, c_arrs, c_shapes, c_sems = _comm_plan(comm)
    k = len(c_arrs)
    n_in, n_out, n_scr = len(in_specs), len(out_specs), len(scratch)
    last = grid[0] - 1
    fwd_step = max(1, (2 * last) // 3)

    def wrapped(*refs):
        ins, cins = refs[:n_in], refs[n_in:n_in + k]
        o0 = n_in + k
        outs, couts = refs[o0:o0 + n_out], refs[o0 + n_out:o0 + n_out + k]
        s0 = o0 + n_out + k
        start, forward, wait = _comm_fns(plan, cins, couts, refs[s0 + n_scr:])
        pl.when(pl.program_id(0) == 0)(start)
        pl.when(pl.program_id(0) == fwd_step)(forward)
        body(*ins, *outs, *refs[s0:s0 + n_scr])
        pl.when(pl.program_id(0) == last)(wait)

    return pl.pallas_call(
        wrapped, name=name, grid=grid, in_specs=list(in_specs) + [_any()] * k,
        out_specs=list(out_specs) + [_any()] * k, out_shape=list(out_shape) + c_shapes,
        scratch_shapes=list(scratch) + c_sems, compiler_params=_cp(sem))(*args, *c_arrs)


def _tail_copies(slots_ref, land_ref, part_refs, sib_refs, send_sems, recv_sems):
    x, y, c = _mesh_pos()
    copies = []
    for j, (px, py) in enumerate(_other_chips(x, y)):
        copies.append(pltpu.make_async_remote_copy(
            src_ref=slots_ref.at[2 * px + py], dst_ref=land_ref.at[j], send_sem=send_sems[j], recv_sem=recv_sems[j],
            device_id=(px, py, c), device_id_type=MESH_ID))
    for k, (p_ref, s_ref) in enumerate(zip(part_refs, sib_refs)):
        copies.append(pltpu.make_async_remote_copy(
            src_ref=p_ref, dst_ref=s_ref, send_sem=send_sems[3 + k], recv_sem=recv_sems[3 + k],
            device_id=(x, y, 1 - c), device_id_type=MESH_ID))
    return copies


def _late_copies(part_ref, sib_ref, pack_ref, packs_ref, send_sems, recv_sems):
    x, y, c = _mesh_pos()
    peers, me = _all_peers()
    copies = [pltpu.make_async_remote_copy(
        src_ref=part_ref, dst_ref=sib_ref, send_sem=send_sems[0], recv_sem=recv_sems[0],
        device_id=(x, y, 1 - c), device_id_type=MESH_ID)]
    for j in range(7):
        copies.append(pltpu.make_async_remote_copy(
            src_ref=pack_ref, dst_ref=packs_ref.at[me], send_sem=send_sems[1 + j], recv_sem=recv_sems[1 + j],
            device_id=peers[j][0], device_id_type=MESH_ID))
    return copies


def _split_start(name, bufs, ncp, make_copies):
    hbm = pl.BlockSpec(memory_space=pltpu.HBM)
    sem = pl.BlockSpec(memory_space=pltpu.SEMAPHORE)
    bufs = [pltpu.with_memory_space_constraint(b, pltpu.HBM) for b in bufs]
    nb = len(bufs)

    def body(*refs):
        for cp in make_copies(refs[:nb], refs[nb:nb + ncp], refs[nb + ncp:nb + 2 * ncp]):
            cp.start()
        refs[-1][...] = jnp.zeros_like(refs[-1])

    out = pl.pallas_call(
        body, name=name,
        out_shape=[pltpu.SemaphoreType.DMA(())] * (2 * ncp) + [pltpu.HBM(b.shape, b.dtype) for b in bufs]
                  + [_sds((8, 128), F32)],
        in_specs=[hbm] * nb, out_specs=[sem] * (2 * ncp) + [hbm] * nb + [pl.BlockSpec(memory_space=pltpu.VMEM)],
        input_output_aliases={i: 2 * ncp + i for i in range(nb)},
        compiler_params=pltpu.CompilerParams(has_side_effects=pltpu.SideEffectType.DATAFLOW_SIDE_EFFECTING),
    )(*bufs)
    return out[:2 * ncp], out[2 * ncp:2 * ncp + nb], out[-1]


def _split_wait(name, sems, bufs, ncp, make_copies, after):
    nb = len(bufs)
    hbm = pl.BlockSpec(memory_space=pltpu.HBM)
    sem = pl.BlockSpec(memory_space=pltpu.SEMAPHORE)

    def body(*refs):
        for cp in make_copies(refs[:nb], refs[nb:nb + ncp], refs[nb + ncp:nb + 2 * ncp]):
            cp.wait_send()
            cp.wait_recv()

    return pl.pallas_call(
        body, name=name, out_shape=[pltpu.HBM(b.shape, b.dtype) for b in bufs],
        in_specs=[hbm] * nb + [sem] * (2 * ncp) + [_any()], out_specs=[hbm] * nb,
        input_output_aliases={i: i for i in range(nb)},
        compiler_params=pltpu.CompilerParams(has_side_effects=pltpu.SideEffectType.DATAFLOW_SIDE_EFFECTING),
    )(*bufs, *sems, after)


def _any():
    return pl.BlockSpec(memory_space=pl.ANY)


def _start_copies(pairs, sems, first=0):
    copies = [pltpu.make_async_copy(src, dst, sems.at[first + i]) for i, (src, dst) in enumerate(pairs)]
    for cp in copies:
        cp.start()
    return copies


def _copy_together(pairs, sems):
    for cp in _start_copies(pairs, sems):
        cp.wait()


def _load_w_in_once(w_hbm, w_ref, sems):
    @pl.when(pl.program_id(0) == 0)
    def _():
        _copy_together([(w_hbm.at[s], w_ref.at[:, pl.ds(s * IN_SH, IN_SH)]) for s in range(N_SHARD)], sems)


def _f_inproj(x, g_mix, w_in_g, tm, comm=None):
    s_len = x.shape[0]
    pad_rows = LEFT_CHUNKS * CHUNK
    npad = pad_rows // tm

    def body(x_ref, g_ref, w_hbm, h_ref, qkv_ref, xg_ref, w_ref, w_sems):
        i = pl.program_id(0)
        _load_w_in_once(w_hbm, w_ref, w_sems)

        @pl.when(i < npad)
        def _():
            qkv_ref[...] = jnp.zeros_like(qkv_ref)

        @pl.when(i >= npad)
        def _():
            xv = x_ref[...]
            h = (xv * _rinv(xv) * g_ref[...]).astype(BF16)
            h_ref[...] = h
            proj = _dot(h, w_ref[...])
            qkv_ref[:, 0:D_ATT] = (proj[:, 0:D_ATT] * ATT_SCALE).astype(BF16)
            qkv_ref[:, D_ATT:3 * D_ATT] = proj[:, D_ATT:3 * D_ATT].astype(BF16)
            xg_ref[...] = proj[:, 3 * D_ATT:D_IN]

    def tok(n):
        return pl.BlockSpec((tm, n), lambda i: (jnp.maximum(i - npad, 0), 0))

    return _call(
        body, "f_inproj", (s_len // tm + npad,),
        [tok(1024), _full((1, 1024)), _any()],
        [tok(1024), _rows(tm, 1536), tok(1024)],
        [_sds((s_len, 1024), BF16), _sds((s_len + pad_rows, 1536), BF16), _sds((s_len, 1024), F32)],
        [pltpu.VMEM((1024, D_IN), BF16), pltpu.SemaphoreType.DMA((N_SHARD,))], (x, g_mix, w_in_g), "arbitrary", comm)


N_BIAS = 3


def _bias_table(frow_ref, bias_sc):
    qa = lax.broadcasted_iota(jnp.int32, (QB, KB), 0) // CHUNK
    kcol = lax.broadcasted_iota(jnp.int32, (QB, KB), 1)
    kb = kcol // CHUNK
    band = jnp.where((kb >= qa) & (kb - qa <= LEFT_CHUNKS), 0.0, NEG).astype(F32)
    for h in range(ATT_HEADS):
        row = jnp.broadcast_to(frow_ref[h:h + 1, :], (QB, ROLL_W))
        toep = pltpu.roll(row, 0, 1, stride=1, stride_axis=0)
        gen = toep[:, 0:KB] + band
        bias_sc[N_BIAS - 1, h] = gen
        for v in range(N_BIAS - 1):
            pad_keys = LEFT_CHUNKS * CHUNK - v * QB
            bias_sc[v, h] = gen + jnp.where(kcol < pad_keys, NEG, 0.0).astype(F32)


def _even_lanes():
    return lax.broadcasted_iota(jnp.int32, (1, 2 * HEAD_DIM), 1) < HEAD_DIM


def _att_probs(qm, kts, bias):
    s = jnp.concatenate([_dot_nt(qm, k) for k in kts], axis=1) + bias
    return jnp.exp(s - jnp.max(s, axis=-1, keepdims=True))


def _att_in_specs(clamp):
    def spec(j, col):
        return pl.BlockSpec((QB, D_ATT), lambda i: (clamp(i) + j, col))
    return [spec(2, 0), spec(0, 1), spec(1, 1), spec(2, 1), spec(0, 2), spec(1, 2), spec(2, 2)]


def _f_attn(qkv_pad, frow, comm=None):
    s_len = qkv_pad.shape[0] - LEFT_CHUNKS * CHUNK
    nb = s_len // QB

    def body(q_ref, k0, k1, k2, v0, v1, v2, frow_ref, o_ref, bias_sc):
        i = pl.program_id(0)

        @pl.when(i == 0)
        def _():
            _bias_table(frow_ref, bias_sc)

        var = jnp.minimum(i, N_BIAS - 1)
        even = _even_lanes()
        for hp in range(ATT_HEADS // 2):
            cs = slice(hp * 2 * HEAD_DIM, (hp + 1) * 2 * HEAD_DIM)
            qt = q_ref[:, cs]
            kts = [k0[:, cs], k1[:, cs], k2[:, cs]]
            vts = [v0[:, cs], v1[:, cs], v2[:, cs]]
            res = []
            for e in range(2):
                keep = even if e == 0 else jnp.logical_not(even)
                pb = _att_probs(jnp.where(keep, qt, 0), kts, bias_sc[var, 2 * hp + e]).astype(BF16)
                r = _dot(pb, jnp.concatenate([jnp.where(keep, v, 1) for v in vts], axis=0))
                res.append(r / pltpu.roll(r, HEAD_DIM, 1))
            o_ref[:, cs] = jnp.where(even, res[0], res[1])

    return _call(
        body, "f_attn", (nb,),
        _att_in_specs(lambda i: i) + [_full((ATT_HEADS, ROLL_W))],
        [_rows(QB, D_ATT)], [_sds((s_len, D_ATT), F32)],
        [pltpu.VMEM((N_BIAS, ATT_HEADS, QB, KB), F32)], (*([qkv_pad] * 7), frow), "arbitrary", comm)


def _f_lru(xg, conv_w, conv_b, wrg, brg, wig, big, lam, tl, comm=None):
    s_len = xg.shape[0]

    def body(xg_ref, cw_ref, cb_ref, wrg_ref, brg_ref, wig_ref, big_ref, l_ref,
             rec_ref, u_ref, hs_ref, xbuf, a_sc, b_sc, hcar):
        i = pl.program_id(0)

        @pl.when(i == 0)
        def _():
            xbuf[0:8, :] = jnp.zeros((8, D_LRU), F32)
            hcar[...] = jnp.zeros((8, D_LRU), F32)

        xu0 = xg_ref[:, 0:D_LRU]
        xbuf[8:8 + tl, :] = xu0
        u = cb_ref[...] + cw_ref[0:1, :] * xbuf[pl.ds(5, tl), :]
        for j in range(1, 4):
            u = u + cw_ref[j:j + 1, :] * xbuf[pl.ds(5 + j, tl), :]
        xbuf[0:8, :] = xu0[tl - 8:tl, :]
        u_ref[...] = u
        _, _, ig, _, a, mult = _lru_gates(u, wrg_ref[...], brg_ref[...], wig_ref[...], big_ref[...], l_ref[...])
        a_sc[...] = a
        b_sc[...] = mult * (ig * u)

        def grp(g, hprev):
            off = pl.multiple_of(g * 8, 8)
            h8 = _scan8(a_sc[pl.ds(off, 8), :], b_sc[pl.ds(off, 8), :], hprev)
            hs_ref[pl.ds(off, 8), :] = h8
            return h8[7:8, :]

        hcar[0:1, :] = lax.fori_loop(0, tl // 8, grp, hcar[0:1, :])
        rec_ref[...] = hs_ref[...] * _gelu(xg_ref[:, D_LRU:2 * D_LRU])

    vec = _full((1, D_LRU))
    return _call(
        body, "f_lru", (s_len // tl,),
        [_rows(tl, 1024), _full((4, D_LRU)), vec, _full((D_LRU, D_LRU)), vec, _full((D_LRU, D_LRU)), vec, vec],
        [_rows(tl, D_LRU)] * 3, [_sds((s_len, D_LRU), F32)] * 3,
        [pltpu.VMEM((tl + 8, D_LRU), F32), pltpu.VMEM((tl, D_LRU), F32),
         pltpu.VMEM((tl, D_LRU), F32), pltpu.VMEM((8, D_LRU), F32)],
        (xg, conv_w, conv_b, wrg, brg, wig, big, lam), "arbitrary", comm)


def _f_mem(mem, g_mem, wk, wv):
    def body(mem_ref, g_ref, wk_ref, wv_ref, mn_ref, kx_ref, vx_ref):
        mv = mem_ref[...]
        mn = (mv * _rinv(mv) * g_ref[...]).astype(BF16)
        mn_ref[...] = mn
        kx_ref[...] = _dot(mn, wk_ref[...]).astype(BF16)
        vx_ref[...] = _dot(mn, wv_ref[...]).astype(BF16)

    m = mem.shape[0]
    return pl.pallas_call(
        body, name="f_mem", out_shape=[_sds((m, 1024), BF16)] * 3,
        compiler_params=_cp())(mem, g_mem, wk, wv)


def _xattn_probs(q, k):
    s = _dot_nt(q, k) * X_SCALE
    m = jnp.max(s, axis=-1, keepdims=True)
    p = jnp.exp(s - m)
    return p, jnp.sum(p, axis=-1, keepdims=True)


def _f_mid(x, att, rec, g_oa, g_ol, w_out, g_cross, wq, kx, vx, wo, tm, comm=None):
    s_len = x.shape[0]
    m_len = kx.shape[0]

    def body(x_ref, att_ref, rec_ref, goa_ref, gol_ref, wout_ref, gc_ref, wq_ref, kx_ref, vx_ref, wo_ref,
             mg_ref, x1_ref, hc_ref, qx_ref, ox_ref, x2_ref):
        av = att_ref[...]
        rv = rec_ref[...]
        mg_ref[:, 0:D_ATT] = (av * _rinv(av) * goa_ref[...]).astype(BF16)
        mg_ref[:, D_ATT:1024] = (rv * _rinv(rv) * gol_ref[...]).astype(BF16)
        x1 = x_ref[...] + _dot(mg_ref[...], wout_ref[...])
        x1_ref[...] = x1
        hc = (x1 * _rinv(x1) * gc_ref[...]).astype(BF16)
        hc_ref[...] = hc
        qx_ref[...] = _dot(hc, wq_ref[...]).astype(BF16)
        for h in range(X_HEADS):
            sl = slice(h * X_HEAD_DIM, (h + 1) * X_HEAD_DIM)
            p, l = _xattn_probs(qx_ref[:, sl], kx_ref[:, sl])
            ox_ref[:, sl] = (_dot(p.astype(BF16), vx_ref[:, sl]) / l).astype(BF16)
        x2_ref[...] = x1 + _dot(ox_ref[...], wo_ref[...])

    sq = _full((1024, 1024))
    return _call(
        body, "f_mid", (s_len // tm,),
        [_rows(tm, 1024), _rows(tm, 512), _rows(tm, 512), _full((1, 512)), _full((1, 512)), sq,
         _full((1, 1024)), sq, _full((m_len, 1024)), _full((m_len, 1024)), sq],
        [_rows(tm, 1024)] * 6,
        [_sds((s_len, 1024), BF16), _sds((s_len, 1024), F32), _sds((s_len, 1024), BF16),
         _sds((s_len, 1024), BF16), _sds((s_len, 1024), BF16), _sds((s_len, 1024), F32)],
        [], (x, att, rec, g_oa, g_ol, w_out, g_cross, wq, kx, vx, wo), "arbitrary", comm)


FF_CHUNKS = [(0, 1280), (1280, D_FF)]


def _first_step_and_rest(step):
    pl.when(pl.program_id(0) == 0)(lambda: step(True))
    pl.when(pl.program_id(0) > 0)(lambda: step(False))


def _ffn_weights(first, pairs, sems):
    if not first:
        return lambda c, j: None
    copies = _start_copies([(hbm.at[c0:c1, :], vmem.at[c0:c1, :]) for c0, c1 in FF_CHUNKS for hbm, vmem in pairs],
                           sems)
    return lambda c, j: copies[c * len(pairs) + j].wait()


def _f_ffn(x2, tgt, g_ffn, g_final, wg, wu, wd, tm):
    s_len = x2.shape[0]

    def body(x2_ref, t_ref, gf_ref, gfin_ref, wg_hbm, wu_hbm, wd_hbm,
             hf_ref, g_ref, u_ref, a_ref, dx3_ref, loss_ref, dgfin_ref, wg_ref, wu_ref, wd_ref, w_sems):
        def step(first):
            if first:
                loss_ref[...] = jnp.zeros_like(loss_ref)
                dgfin_ref[...] = jnp.zeros_like(dgfin_ref)
            ready = _ffn_weights(first, [(wg_hbm, wg_ref), (wu_hbm, wu_ref), (wd_hbm, wd_ref)], w_sems)
            x2v = x2_ref[...]
            hf = (x2v * _rinv(x2v) * gf_ref[...]).astype(BF16)
            hf_ref[...] = hf
            x3 = x2v
            for c, (c0, c1) in enumerate(FF_CHUNKS):
                ready(c, 0)
                gv = _dot_nt(hf, wg_ref[c0:c1, :])
                ready(c, 1)
                uv = _dot_nt(hf, wu_ref[c0:c1, :])
                av = (gv * jax.nn.sigmoid(gv) * uv).astype(BF16)
                g_ref[:, c0:c1] = gv.astype(BF16)
                u_ref[:, c0:c1] = uv.astype(BF16)
                a_ref[:, c0:c1] = av
                ready(c, 2)
                x3 = x3 + _dot(av, wd_ref[c0:c1, :])
            r3 = _rinv(x3)
            yh = x3 * r3
            gfin = gfin_ref[...]
            err = yh * gfin - t_ref[...]
            loss_ref[...] += jnp.full((1, 128), 0.5 / D_MODEL, F32) * jnp.sum(err * err)
            dy = err * (1.0 / D_MODEL)
            dgfin_ref[...] += jnp.sum(dy * yh, axis=0, keepdims=True)
            dyh = dy * gfin
            dx3_ref[...] = r3 * (dyh - yh * jnp.mean(dyh * yh, axis=-1, keepdims=True))

        _first_step_and_rest(step)

    vec = _full((1, 1024))
    return pl.pallas_call(
        body, name="f_ffn", grid=(s_len // tm,),
        in_specs=[_rows(tm, 1024), _rows(tm, 1024), vec, vec, _any(), _any(), _any()],
        out_specs=[_rows(tm, 1024), _rows(tm, D_FF), _rows(tm, D_FF), _rows(tm, D_FF),
                   _rows(tm, 1024), _full((1, 128)), vec],
        out_shape=[_sds((s_len, 1024), BF16)] + [_sds((s_len, D_FF), BF16)] * 3
                  + [_sds((s_len, 1024), F32), _sds((1, 128), F32), _sds((1, 1024), F32)],
        scratch_shapes=[pltpu.VMEM((D_FF, 1024), BF16)] * 3 + [pltpu.SemaphoreType.DMA((3 * len(FF_CHUNKS),))],
        compiler_params=_cp("arbitrary"))(x2, tgt, g_ffn, g_final, wg, wu, wd)


def _b_ffn(dx3, x2, gact, uact, g_ffn, wg, wu, wd, tm):
    s_len = x2.shape[0]

    def body(dx3_ref, x2_ref, g_ref, u_ref, gf_ref, wg_hbm, wu_hbm, wd_hbm,
             dg_ref, du_ref, dx2_ref, dgf_ref, wg_ref, wu_ref, wd_ref, w_sems):
        def step(first):
            if first:
                dgf_ref[...] = jnp.zeros_like(dgf_ref)
            ready = _ffn_weights(first, [(wd_hbm, wd_ref), (wg_hbm, wg_ref), (wu_hbm, wu_ref)], w_sems)
            dx3v = dx3_ref[...]
            dx3b = dx3v.astype(BF16)
            dhf = jnp.zeros(dx3v.shape, F32)
            for c, (c0, c1) in enumerate(FF_CHUNKS):
                ready(c, 0)
                da = _dot_nt(dx3b, wd_ref[c0:c1, :])
                gv = g_ref[:, c0:c1].astype(F32)
                uv = u_ref[:, c0:c1].astype(F32)
                sg = jax.nn.sigmoid(gv)
                dub = (da * gv * sg).astype(BF16)
                dgb = (da * uv * (sg * (1.0 + gv * (1.0 - sg)))).astype(BF16)
                du_ref[:, c0:c1] = dub
                dg_ref[:, c0:c1] = dgb
                ready(c, 1)
                ready(c, 2)
                dhf = dhf + _dot(dgb, wg_ref[c0:c1, :]) + _dot(dub, wu_ref[c0:c1, :])
            dx, dgf = _rms_bwd(dhf, x2_ref[...], gf_ref[...])
            dx2_ref[...] = dx3v + dx
            dgf_ref[...] += dgf

        _first_step_and_rest(step)

    vec = _full((1, 1024))
    return pl.pallas_call(
        body, name="b_ffn", grid=(s_len // tm,),
        in_specs=[_rows(tm, 1024), _rows(tm, 1024), _rows(tm, D_FF), _rows(tm, D_FF), vec,
                  _any(), _any(), _any()],
        out_specs=[_rows(tm, D_FF), _rows(tm, D_FF), _rows(tm, 1024), vec],
        out_shape=[_sds((s_len, D_FF), BF16)] * 2 + [_sds((s_len, 1024), F32), _sds((1, 1024), F32)],
        scratch_shapes=[pltpu.VMEM((D_FF, 1024), BF16)] * 3 + [pltpu.SemaphoreType.DMA((3 * len(FF_CHUNKS),))],
        compiler_params=_cp("arbitrary"))(dx3, x2, gact, uact, g_ffn, wg, wu, wd)


def _b_mid(dx2, qx, x1, att, rec, kx, vx, wo, wq, w_out, g_cross, g_oa, g_ol, tm, comm=None):
    s_len = x1.shape[0]
    m_len = kx.shape[0]

    def body(dx2_ref, qx_ref, x1_ref, att_ref, rec_ref, kx_ref, vx_ref, wo_ref, wq_ref, wout_ref,
             gc_ref, goa_ref, gol_ref,
             dqx_ref, dx1_ref, datt_ref, drec_ref, dkx_ref, dvx_ref, dgc_ref, dgoa_ref, dgol_ref):
        @pl.when(pl.program_id(0) == 0)
        def _():
            for r in (dkx_ref, dvx_ref, dgc_ref, dgoa_ref, dgol_ref):
                r[...] = jnp.zeros_like(r)

        dx2v = dx2_ref[...]
        dox = _dot_nt(dx2v.astype(BF16), wo_ref[...])
        for h in range(X_HEADS):
            sl = slice(h * X_HEAD_DIM, (h + 1) * X_HEAD_DIM)
            q = qx_ref[:, sl]
            p, l = _xattn_probs(q, kx_ref[:, sl])
            pn = p * (1.0 / l)
            dob = dox[:, sl].astype(BF16)
            dp = _dot_nt(dob, vx_ref[:, sl])
            dvx_ref[:, sl] += _dot_tn(pn.astype(BF16), dob)
            ds = pn * (dp - jnp.sum(dp * pn, axis=-1, keepdims=True))
            dsb = (ds * X_SCALE).astype(BF16)
            dqx_ref[:, sl] = _dot(dsb, kx_ref[:, sl]).astype(BF16)
            dkx_ref[:, sl] += _dot_tn(dsb, q)
        dhc = _dot_nt(dqx_ref[...], wq_ref[...])
        dx, dgc = _rms_bwd(dhc, x1_ref[...], gc_ref[...])
        dx1 = dx2v + dx
        dx1_ref[...] = dx1
        dgc_ref[...] += dgc
        dmg = _dot_nt(dx1.astype(BF16), wout_ref[...])
        da, dgoa = _rms_bwd(dmg[:, 0:D_ATT], att_ref[...], goa_ref[...])
        datt_ref[...] = da
        dgoa_ref[...] += dgoa
        dr, dgol = _rms_bwd(dmg[:, D_ATT:1024], rec_ref[...], gol_ref[...])
        drec_ref[...] = dr
        dgol_ref[...] += dgol

    sq = _full((1024, 1024))
    mk = _full((m_len, 1024))
    return _call(
        body, "b_mid", (s_len // tm,),
        [_rows(tm, 1024), _rows(tm, 1024), _rows(tm, 1024), _rows(tm, 512), _rows(tm, 512), mk, mk,
         sq, sq, sq, _full((1, 1024)), _full((1, 512)), _full((1, 512))],
        [_rows(tm, 1024), _rows(tm, 1024), _rows(tm, 512), _rows(tm, 512), mk, mk,
         _full((1, 1024)), _full((1, 512)), _full((1, 512))],
        [_sds((s_len, 1024), BF16), _sds((s_len, 1024), F32), _sds((s_len, 512), F32),
         _sds((s_len, 512), F32), _sds((m_len, 1024), F32), _sds((m_len, 1024), F32),
         _sds((1, 1024), F32), _sds((1, 512), F32), _sds((1, 512), F32)],
        [], (dx2, qx, x1, att, rec, kx, vx, wo, wq, w_out, g_cross, g_oa, g_ol), "arbitrary", comm)


def _b_mem(dkx, dvx, mem, mn, g_mem, wk, wv):
    def body(dkx_ref, dvx_ref, mem_ref, mn_ref, g_ref, wk_ref, wv_ref, dwk_ref, dwv_ref, dgm_ref,
             dwkb_ref, dwvb_ref):
        dkb = dkx_ref[...].astype(BF16)
        dvb = dvx_ref[...].astype(BF16)
        dwk = _dot_tn(mn_ref[...], dkb)
        dwv = _dot_tn(mn_ref[...], dvb)
        dwk_ref[...] = dwk
        dwv_ref[...] = dwv
        dwkb_ref[...] = dwk.astype(BF16)
        dwvb_ref[...] = dwv.astype(BF16)
        dmn = _dot_nt(dkb, wk_ref[...]) + _dot_nt(dvb, wv_ref[...])
        mv = mem_ref[...]
        dgm_ref[...] = jnp.sum(dmn * (mv * _rinv(mv)), axis=0, keepdims=True)

    return pl.pallas_call(
        body, name="b_mem",
        out_shape=[_sds((1024, 1024), F32), _sds((1024, 1024), F32), _sds((1, 1024), F32),
                   _sds((1024, 1024), BF16), _sds((1024, 1024), BF16)],
        compiler_params=_cp())(dkx, dvx, mem, mn, g_mem, wk, wv)


def _b_lru(drec, hs, u, xg, conv_w, wrg, brg, wig, big, lam, tl, comm=None):
    s_len = xg.shape[0]
    nt = s_len // tl

    def body(drec_ref, hs_ref, hsp_ref, u_ref, xg_ref, cw_ref, wrg_ref, brg_ref, wig_ref, big_ref, l_ref,
             dxg_ref, dwrg_ref, dwig_ref, dbrg_ref, dbig_ref, dlam_ref, dcw_ref, dcb_ref,
             hbuf, abuf, dubuf, c_sc, d_sc, lam_sc, lcar, wacc_r, wacc_i):
        i = pl.program_id(0)
        tt = nt - 1 - i

        @pl.when(i == 0)
        def _():
            for r in (wacc_r, wacc_i, dbrg_ref, dbig_ref, dlam_ref, dcw_ref, dcb_ref):
                r[...] = jnp.zeros_like(r)
            abuf[tl:tl + 8, :] = jnp.zeros((8, D_LRU), F32)
            dubuf[tl:tl + 8, :] = jnp.zeros((8, D_LRU), F32)
            lcar[...] = jnp.zeros((8, D_LRU), F32)

        xu0 = xg_ref[:, 0:D_LRU]
        hsv = hs_ref[...]
        uv = u_ref[...]
        hbuf[8:8 + tl, :] = hsv
        hbuf[0:8, :] = jnp.where(tt > 0, hsp_ref[...], 0.0)
        hshift = hbuf[pl.ds(7, tl), :]
        wrg_v = wrg_ref[...]
        wig_v = wig_ref[...]
        lamv = l_ref[...]
        ub, r, ig, sp, a, mult = _lru_gates(uv, wrg_v, brg_ref[...], wig_v, big_ref[...], lamv)
        abuf[0:tl, :] = a
        c_sc[...] = abuf[pl.ds(1, tl), :]
        gel, dgel = _gelu_and_grad(xg_ref[:, D_LRU:2 * D_LRU])
        drv = drec_ref[...]
        d_sc[...] = drv * gel
        dxg_ref[:, D_LRU:2 * D_LRU] = (drv * hsv * dgel).astype(BF16)

        def grp(k, lnext):
            off = pl.multiple_of((tl // 8 - 1 - k) * 8, 8)
            l8 = _rscan8(c_sc[pl.ds(off, 8), :], d_sc[pl.ds(off, 8), :], lnext)
            lam_sc[pl.ds(off, 8), :] = l8
            return l8[0:1, :]

        lcar[0:1, :] = lax.fori_loop(0, tl // 8, grp, lcar[0:1, :])
        abuf[tl:tl + 8, :] = a[0:8, :]
        db = lam_sc[...]
        da = db * hshift
        dmult = db * (ig * uv)
        dig = db * mult * uv
        du = db * mult * ig
        dla = da * a - dmult * (a * a) / mult
        dlam_ref[...] += jnp.sum(dla * (-LRU_C) * r, axis=0, keepdims=True)
        dzr = dla * (-LRU_C * sp) * r * (1.0 - r)
        dzi = dig * ig * (1.0 - ig)
        dzrb = dzr.astype(BF16)
        dzib = dzi.astype(BF16)
        du = du + _dot_nt(dzrb, wrg_v) + _dot_nt(dzib, wig_v)
        wacc_r[...] += _dot_tn(ub, dzrb)
        wacc_i[...] += _dot_tn(ub, dzib)
        dbrg_ref[...] += jnp.sum(dzr, axis=0, keepdims=True)
        dbig_ref[...] += jnp.sum(dzi, axis=0, keepdims=True)
        dcb_ref[...] += jnp.sum(du, axis=0, keepdims=True)
        dubuf[0:tl, :] = du
        dxu0 = jnp.zeros((tl, D_LRU), F32)
        for j in range(4):
            dsh = dubuf[pl.ds(3 - j, tl), :]
            dxu0 = dxu0 + cw_ref[j:j + 1, :] * dsh
            dcw_ref[j:j + 1, :] += jnp.sum(xu0 * dsh, axis=0, keepdims=True)
        dubuf[tl:tl + 8, :] = du[0:8, :]
        dxg_ref[:, 0:D_LRU] = dxu0.astype(BF16)

        @pl.when(i == nt - 1)
        def _():
            dlam_ref[...] = dlam_ref[...] * (-jax.nn.sigmoid(-lamv))
            for n in range(LRU_BLOCKS):
                blk = slice(n * LRU_BLOCK, (n + 1) * LRU_BLOCK)
                dwrg_ref[n] = wacc_r[blk, blk]
                dwig_ref[n] = wacc_i[blk, blk]

    def rev(n):
        return pl.BlockSpec((tl, n), lambda i: (nt - 1 - i, 0))

    prev8 = pl.BlockSpec((8, D_LRU), lambda i: (jnp.maximum((nt - 1 - i) * (tl // 8) - 1, 0), 0))
    vec = _full((1, D_LRU))
    sq = _full((D_LRU, D_LRU))
    blocks_shape = (LRU_BLOCKS, LRU_BLOCK, LRU_BLOCK)
    blocks = _full(blocks_shape)
    return _call(
        body, "b_lru", (nt,),
        [rev(D_LRU), rev(D_LRU), prev8, rev(D_LRU), rev(1024), _full((4, D_LRU)), sq, vec, sq, vec, vec],
        [rev(1024), blocks, blocks, vec, vec, vec, _full((4, D_LRU)), vec],
        [_sds((s_len, 1024), BF16), _sds(blocks_shape, F32), _sds(blocks_shape, F32),
         _sds((1, D_LRU), F32), _sds((1, D_LRU), F32), _sds((1, D_LRU), F32),
         _sds((4, D_LRU), F32), _sds((1, D_LRU), F32)],
        [pltpu.VMEM((tl + 8, D_LRU), F32)] * 3 + [pltpu.VMEM((tl, D_LRU), F32)] * 3
        + [pltpu.VMEM((8, D_LRU), F32)] + [pltpu.VMEM((D_LRU, D_LRU), F32)] * 2,
        (drec, hs, hs, u, xg, conv_w, wrg, brg, wig, big, lam), "arbitrary", comm)


def _b_attn(qkv_pad, att, datt, frow, comm=None):
    s_len = datt.shape[0]
    nb = s_len // QB
    n_pair = ATT_HEADS // 2
    pair_w = 2 * HEAD_DIM

    def body(q_ref, k0, k1, k2, v0, v1, v2, o_ref, do_ref, frow_ref, dq_ref, dkv_ref, dfrow_ref,
             bias_sc, dt_sc, acc_sc):
        t = pl.program_id(0)

        @pl.when(t == 0)
        def _():
            _bias_table(frow_ref, bias_sc)
            dt_sc[...] = jnp.zeros_like(dt_sc)
            acc_sc[...] = jnp.zeros_like(acc_sc)

        @pl.when(t < nb)
        def _():
            var = jnp.minimum(t, N_BIAS - 1)
            even = _even_lanes()
            for hp in range(n_pair):
                cs = slice(hp * pair_w, (hp + 1) * pair_w)
                qt = q_ref[:, cs]
                kts = [k0[:, cs], k1[:, cs], k2[:, cs]]
                vts = [v0[:, cs], v1[:, cs], v2[:, cs]]
                kcat = jnp.concatenate(kts, axis=0)
                dot = do_ref[:, cs]
                dd = dot * o_ref[:, cs]
                dos_pair, dsbs, pbs, dqs = None, [], [], []
                for e in range(2):
                    keep = even if e == 0 else jnp.logical_not(even)
                    qm = jnp.where(keep, qt, 0)
                    p = _att_probs(qm, kts, bias_sc[var, 2 * hp + e])
                    inv = 1.0 / jnp.sum(p, axis=-1, keepdims=True)
                    dos = jnp.where(keep, dot * inv, 0.0)
                    delta = jnp.sum(jnp.where(keep, dd, 0.0), axis=-1, keepdims=True) * inv
                    dp = jnp.concatenate([_dot_nt(dos.astype(BF16), v) for v in vts], axis=1)
                    ds = p * (dp - delta)
                    dt_sc[2 * hp + e] += ds
                    dsb = ds.astype(BF16)
                    dq = _dot(dsb, kcat)
                    dqs.append(dq)
                    dsbs.append(dsb)
                    pbs.append(p.astype(BF16))
                    dos_pair = dos if e == 0 else dos_pair + dos
                dq_ref[:, cs] = (jnp.where(even, dqs[0], dqs[1]) * ATT_SCALE).astype(BF16)
                qtt = qt.astype(F32).T.astype(BF16)
                dost = dos_pair.T.astype(BF16)
                for j in range(3):
                    slot = (t + 1 + j) % 3
                    js = slice(j * QB, (j + 1) * QB)
                    for e in range(2):
                        hr = slice(e * HEAD_DIM, (e + 1) * HEAD_DIM)
                        acc_sc[slot, hp, hr, :] += _dot(qtt[hr], dsbs[e][:, js])
                        acc_sc[slot, n_pair + hp, hr, :] += _dot(dost[hr], pbs[e][:, js])

        done = (t + 1) % 3

        @pl.when(t >= 2)
        def _():
            for i in range(2 * n_pair):
                dkv_ref[:, i * pair_w:(i + 1) * pair_w] = acc_sc[done, i].T.astype(BF16)

        acc_sc[done] = jnp.zeros((2 * n_pair, pair_w, QB), F32)

        @pl.when(t == nb + 1)
        def _():
            row = lax.broadcasted_iota(jnp.int32, (8, ROLL_W), 0)
            pad = jnp.zeros((8, ROLL_W - KB), F32)
            for h in range(ATT_HEADS):
                acc8 = jnp.concatenate([dt_sc[h, 0:8, :], pad], axis=1)
                for a1 in range(1, QB // 8):
                    blk = jnp.concatenate([dt_sc[h, 8 * a1:8 * a1 + 8, :], pad], axis=1)
                    acc8 = acc8 + pltpu.roll(blk, ROLL_W - 8 * a1, 1)
                for k in range(3):
                    acc8 = jnp.where(((row >> k) & 1) == 1, pltpu.roll(acc8, ROLL_W - (1 << k), 1), acc8)
                dfrow_ref[h:h + 1, :] = jnp.sum(acc8, axis=0, keepdims=True)

    clamp = lambda t: jnp.minimum(t, nb - 1)
    qrows = pl.BlockSpec((QB, D_ATT), lambda t: (clamp(t), 0))
    return _call(
        body, "b_attn", (nb + 2,),
        _att_in_specs(clamp) + [qrows, qrows, _full((ATT_HEADS, ROLL_W))],
        [qrows, pl.BlockSpec((QB, 2 * D_ATT), lambda t: (jnp.maximum(t - 2, 0), 0)),
         _full((ATT_HEADS, ROLL_W))],
        [_sds((s_len, D_ATT), BF16), _sds((s_len, 2 * D_ATT), BF16), _sds((ATT_HEADS, ROLL_W), F32)],
        [pltpu.VMEM((N_BIAS, ATT_HEADS, QB, KB), F32), pltpu.VMEM((ATT_HEADS, QB, KB), F32),
         pltpu.VMEM((3, 2 * n_pair, pair_w, QB), F32)],
        (*([qkv_pad] * 7), att, datt, frow), "arbitrary", comm)


def _b_win(dq, dkv, dxg, h, ts):
    s_len = h.shape[0]
    steps = s_len // ts

    def body(dq_ref, dkv_ref, dxg_ref, h_ref, dw_hbm, dwb_hbm, acc, accb, sems):
        @pl.when(pl.program_id(0) == 0)
        def _():
            acc[...] = jnp.zeros_like(acc)

        @pl.when(pl.program_id(0) < steps - 1)
        def _():
            dproj = jnp.concatenate([dq_ref[...], dkv_ref[...], dxg_ref[...]], axis=1)
            acc[...] += _dot_tn(h_ref[...], dproj)

        @pl.when(pl.program_id(0) == steps - 1)
        def _():
            dproj = jnp.concatenate([dq_ref[...], dkv_ref[...], dxg_ref[...]], axis=1)
            copies = []
            for s in range(N_SHARD):
                cols = slice(s * IN_SH, (s + 1) * IN_SH)
                total = acc[:, cols] + _dot_tn(h_ref[...], dproj[:, cols])
                acc[:, cols] = total
                accb[:, cols] = total.astype(BF16)
                copies += _start_copies([(acc.at[:, cols], dw_hbm.at[s]), (accb.at[:, cols], dwb_hbm.at[s])],
                                        sems, 2 * s)
            for cp in copies:
                cp.wait()

    shape = (N_SHARD, 1024, IN_SH)
    return pl.pallas_call(
        body, name="b_win", grid=(steps,),
        in_specs=[_rows(ts, 512), _rows(ts, 1024), _rows(ts, 1024), _rows(ts, 1024)],
        out_specs=[_any()] * 2, out_shape=[_sds(shape, F32), _sds(shape, BF16)],
        scratch_shapes=[pltpu.VMEM((1024, D_IN), F32), pltpu.VMEM((1024, D_IN), BF16),
                        pltpu.SemaphoreType.DMA((2 * N_SHARD,))],
        compiler_params=_cp("arbitrary"))(dq, dkv, dxg, h)


def _b_inproj(dq, dkv, dxg, x, dx1, g_mix, w_in_g, tm, comm=None):
    s_len = x.shape[0]

    def body(dq_ref, dkv_ref, dxg_ref, x_ref, dx1_ref, g_ref, w_hbm, gx_ref, dgm_ref, w_ref, w_sems):
        _load_w_in_once(w_hbm, w_ref, w_sems)

        @pl.when(pl.program_id(0) == 0)
        def _():
            dgm_ref[...] = jnp.zeros_like(dgm_ref)

        dproj = jnp.concatenate([dq_ref[...], dkv_ref[...], dxg_ref[...]], axis=1)
        dh = _dot_nt(dproj, w_ref[...])
        dx, dgm = _rms_bwd(dh, x_ref[...], g_ref[...])
        gx_ref[...] = dx1_ref[...] + dx
        dgm_ref[...] += dgm

    return _call(
        body, "b_inproj", (s_len // tm,),
        [_rows(tm, 512), _rows(tm, 1024), _rows(tm, 1024), _rows(tm, 1024), _rows(tm, 1024),
         _full((1, 1024)), _any()],
        [_rows(tm, 1024), _full((1, 1024))],
        [_sds((s_len, 1024), F32), _sds((1, 1024), F32)],
        [pltpu.VMEM((1024, D_IN), BF16), pltpu.SemaphoreType.DMA((N_SHARD,))],
        (dq, dkv, dxg, x, dx1, g_mix, w_in_g), "arbitrary", comm)


MXU_DIM_V7X = 256
FLUSH_GROUPS = 4


def _mm_tn(xa, ya, name, ts):
    s_len, k = xa.shape
    n = ya.shape[1]

    steps = s_len // ts
    tiles = k // MXU_DIM_V7X
    edges = [MXU_DIM_V7X * ((tiles * g) // FLUSH_GROUPS) for g in range(FLUSH_GROUPS + 1)]

    def body(x_ref, y_ref, o_hbm, ob_hbm, acc, accb, sems):
        @pl.when(pl.program_id(0) == 0)
        def _():
            acc[...] = jnp.zeros_like(acc)

        @pl.when(pl.program_id(0) < steps - 1)
        def _():
            acc[...] += _dot_tn(x_ref[...].astype(BF16), y_ref[...].astype(BF16))

        @pl.when(pl.program_id(0) == steps - 1)
        def _():
            yb = y_ref[...].astype(BF16)
            copies = []
            for g in range(FLUSH_GROUPS):
                rows = slice(edges[g], edges[g + 1])
                total = acc[rows, :] + _dot_tn(x_ref[:, rows].astype(BF16), yb)
                acc[rows, :] = total
                accb[rows, :] = total.astype(BF16)
                copies += _start_copies([(acc.at[rows, :], o_hbm.at[rows, :]), (accb.at[rows, :], ob_hbm.at[rows, :])],
                                        sems, 2 * g)
            for cp in copies:
                cp.wait()

    return pl.pallas_call(
        body, name=name, grid=(steps,), in_specs=[_rows(ts, k), _rows(ts, n)],
        out_specs=[_any()] * 2, out_shape=[_sds((k, n), F32), _sds((k, n), BF16)],
        scratch_shapes=[pltpu.VMEM((k, n), F32), pltpu.VMEM((k, n), BF16),
                        pltpu.SemaphoreType.DMA((2 * FLUSH_GROUPS,))],
        compiler_params=_cp("arbitrary"))(xa, ya)


PAD_KEYS = LEFT_CHUNKS * CHUNK
F_HI = PAD_KEYS - MAX_REL + 1
F_LO = PAD_KEYS + MAX_REL


def _frow_from_rel_bias(rb):
    last = rb[:, 2 * MAX_REL:2 * MAX_REL + 1]
    hi = jnp.broadcast_to(last, (ATT_HEADS, F_HI))
    mid = rb[:, 1:2 * MAX_REL][:, ::-1]
    lo = jnp.broadcast_to(rb[:, 0:1], (ATT_HEADS, KB - F_LO))
    wrap = jnp.broadcast_to(last, (ATT_HEADS, ROLL_W - KB))
    return jnp.concatenate([hi, mid, lo, wrap], axis=1)


def _rel_bias_grad_from_dfrow(df):
    g_last = jnp.sum(df[:, 0:F_HI], axis=1, keepdims=True) + jnp.sum(df[:, KB:ROLL_W], axis=1, keepdims=True)
    mid = df[:, F_HI:F_LO][:, ::-1]
    g_first = jnp.sum(df[:, F_LO:KB], axis=1, keepdims=True)
    return jnp.concatenate([g_first, mid, g_last], axis=1)


def _block_diag(w):
    eye = jnp.eye(8, dtype=w.dtype)
    return (w[:, :, None, :] * eye[:, None, :, None]).reshape(D_LRU, D_LRU)


MID = ['w_out', 'wq_c', 'wk_c', 'wv_c', 'wo_c']
TRANSPOSED = ['w_gate', 'w_up']
AG_IN_INPROJ = ['w_out', 'wq_c', 'wk_c']
AG_IN_ATTN = ['wv_c', 'wo_c', 'w_gate']
AG_IN_LRU = ['w_up']
AG_IN_MID = ['w_down']
RS_IN_MID = ['w_gate', 'w_up']
RS_IN_LRU = ['w_down']
RS_IN_ATTN = MID


def _local_step(x, mem, tgt, p, gw, shards=None, chip=None):
    s_len = x.shape[0]
    tm = min(256, s_len)
    tmb = min(512, s_len)
    tl = min(512, s_len)
    frow = _frow_from_rel_bias(p['rel_bias'])
    wrg = _block_diag(p['w_rg']).astype(BF16)
    wig = _block_diag(p['w_ig']).astype(BF16)
    gw = dict(gw)

    big, bigb, recv, part, sib = {}, {}, {}, {}, {}

    def ag(names):
        return [] if shards is None else [("ag", [shards[n] for n in names])]

    def rs(names):
        return [] if shards is None else [("rs", [bigb[n] for n in names])]

    def swap(names):
        return [] if shards is None else [("swap", [part[n] for n in names])]

    def reduce_own(names):
        if shards is not None:
            sums = _sum_parts([big[n] for n in names], [recv[n] for n in names], chip, "sum_" + names[0])
            part.update(zip(names, sums))

    h, qkv_pad, xg, *got = _f_inproj(x, p['g_mix'], gw['w_in'], tmb, ag(AG_IN_INPROJ))
    gw.update(zip(AG_IN_INPROJ, got))
    att, *got = _f_attn(qkv_pad, frow, ag(AG_IN_ATTN))
    gw.update(zip(AG_IN_ATTN, got))
    rec, u, hs, *got = _f_lru(xg, p['conv_w'], p['conv_b'], wrg, p['b_rg'], wig, p['b_ig'], p['lru_L'], tl,
                              ag(AG_IN_LRU))
    gw.update(zip(AG_IN_LRU, got))
    w_out = gw['w_out'].reshape(1024, 1024)
    wq = gw['wq_c'].reshape(1024, 1024)
    wk = gw['wk_c'].reshape(1024, 1024)
    wv = gw['wv_c'].reshape(1024, 1024)
    wo = gw['wo_c'].reshape(1024, 1024)
    mn, kx, vx = _f_mem(mem, p['g_mem'], wk, wv)
    mg, x1, hc, qx, ox, x2, *got = _f_mid(x, att, rec, p['g_out_attn'], p['g_out_lru'], w_out, p['g_cross'],
                                          wq, kx, vx, wo, tmb, ag(AG_IN_MID))
    gw.update(zip(AG_IN_MID, got))
    ffn_w = [gw[n].reshape(D_FF, 1024) for n in ('w_gate', 'w_up', 'w_down')]
    hf, gact, uact, aact, dx3, loss, dg_final = _f_ffn(x2, tgt, p['g_ffn'], p['g_final'], *ffn_w, tmb)

    ts = min(1024, s_len)
    dgact, duact, dx2, dg_ffn = _b_ffn(dx3, x2, gact, uact, p['g_ffn'], *ffn_w, tm)
    big['w_gate'], bigb['w_gate'] = _mm_tn(dgact, hf, "dw_gate", ts)
    big['w_up'], bigb['w_up'] = _mm_tn(duact, hf, "dw_up", ts)
    big['w_down'], bigb['w_down'] = _mm_tn(aact, dx3, "dw_down", ts)
    for n in ('w_gate', 'w_up', 'w_down'):
        big[n] = big[n].reshape(N_SHARD, FF_SH, 1024)
        bigb[n] = bigb[n].reshape(N_SHARD, FF_SH, 1024)

    dqx, dx1, datt, drec, dkx, dvx, dg_cross, dg_oa, dg_ol, *got = _b_mid(
        dx2, qx, x1, att, rec, kx, vx, wo, wq, w_out, p['g_cross'], p['g_out_attn'], p['g_out_lru'], tmb,
        rs(RS_IN_MID))
    recv.update(zip(RS_IN_MID, got))
    reduce_own(RS_IN_MID)
    dwk, dwv, dg_mem, dwkb, dwvb = _b_mem(dkx, dvx, mem, mn, p['g_mem'], wk, wv)
    big['wk_c'], bigb['wk_c'] = dwk, dwkb
    big['wv_c'], bigb['wv_c'] = dwv, dwvb
    big['w_out'], bigb['w_out'] = _mm_tn(mg, dx1, "dw_out", ts)
    big['wq_c'], bigb['wq_c'] = _mm_tn(hc, dqx, "dw_q", ts)
    big['wo_c'], bigb['wo_c'] = _mm_tn(ox, dx2, "dw_o", ts)
    for n in MID:
        big[n] = big[n].reshape(N_SHARD, 256, 1024)
        bigb[n] = bigb[n].reshape(N_SHARD, 256, 1024)

    dxg, dwrg, dwig, dbrg, dbig, dlam, dcw, dcb, *got = _b_lru(
        drec, hs, u, xg, p['conv_w'], wrg, p['b_rg'], wig, p['b_ig'], p['lru_L'], tl,
        rs(RS_IN_LRU) + swap(RS_IN_MID))
    recv.update(zip(RS_IN_LRU, got))
    sib.update(zip(RS_IN_MID, got[len(RS_IN_LRU):]))
    reduce_own(RS_IN_LRU)
    small = {
        'conv_w': dcw, 'conv_b': dcb, 'w_rg': dwrg, 'b_rg': dbrg, 'w_ig': dwig, 'b_ig': dbig, 'lru_L': dlam,
        'g_out_attn': dg_oa, 'g_out_lru': dg_ol, 'g_cross': dg_cross, 'g_mem': dg_mem, 'g_ffn': dg_ffn,
        'g_final': dg_final,
    }
    names = [n for n in SMALL if n in small]
    gather = [] if shards is None else [
        ("ag8", [_pack_small(names, [small[n] for n in names], loss, PACK_ROWS, "pack_small")])]
    dq, dkv, dfrow, *got = _b_attn(qkv_pad, att, datt, frow, rs(RS_IN_ATTN) + swap(RS_IN_LRU) + gather)
    recv.update(zip(RS_IN_ATTN, got))
    sib.update(zip(RS_IN_LRU, got[len(RS_IN_ATTN):]))
    packs = got[-1] if gather else None
    reduce_own(RS_IN_ATTN)
    small['rel_bias'] = _rel_bias_grad_from_dfrow(dfrow)
    big['w_in'], bigb['w_in'] = _b_win(dq, dkv, dxg, h, ts)
    if shards is None:
        grad_x, small['g_mix'] = _b_inproj(dq, dkv, dxg, x, dx1, p['g_mix'], gw['w_in'], tmb)
    else:
        nsw = len(RS_IN_ATTN)

        def copies(refs, send_sems, recv_sems):
            return _tail_copies(refs[0], refs[1], refs[2:2 + nsw], refs[2 + nsw:2 + 2 * nsw], send_sems, recv_sems)

        slots = bigb['w_in']
        bufs = ([slots, lax.empty((3,) + slots.shape[1:], slots.dtype)] + [part[n] for n in RS_IN_ATTN]
                + [lax.empty(part[n].shape, F32) for n in RS_IN_ATTN])
        sems, bufs, token = _split_start("tail_exchange_start", bufs, 3 + nsw, copies)
        grad_x, small['g_mix'] = _b_inproj(dq, dkv, dxg, x, dx1, p['g_mix'] + token[0, 0], gw['w_in'], tmb)
        bufs = _split_wait("tail_exchange_wait", sems, bufs, 3 + nsw, copies, small['g_mix'])
        recv['w_in'] = bufs[1]
        sib.update(zip(RS_IN_ATTN, bufs[2 + nsw:]))
    reduce_own(['w_in'])
    return loss, grad_x, small, big, part, sib, packs


CAST_STEPS = 4


def _cast_shards(ws, name, comm=None):
    def body(*refs):
        n = len(refs) // 2
        for src, dst in zip(refs[:n], refs[n:]):
            dst[...] = src[...].astype(BF16)

    specs = [_rows(w.shape[0] // CAST_STEPS, w.shape[1]) for w in ws]
    return _call(body, name, (CAST_STEPS,), specs, specs, [_sds(w.shape, BF16) for w in ws], [], tuple(ws),
                 "arbitrary", comm)


def _sum_parts(own4s, recv3s, chip, name):
    n = len(own4s)
    _, r, c = own4s[0].shape
    steps = _ew_steps(r, n * c * (4 + 3 * 2 + 4))
    tr = r // steps

    def body(chip_ref, *refs):
        for own_ref, rc_ref, o_ref in zip(refs[:n], refs[n:2 * n], refs[2 * n:]):
            o_ref[...] = ((own_ref[0] + rc_ref[0].astype(F32)) + rc_ref[1].astype(F32)) + rc_ref[2].astype(F32)

    grid_spec = pltpu.PrefetchScalarGridSpec(
        num_scalar_prefetch=1, grid=(steps,),
        in_specs=[pl.BlockSpec((1, tr, c), lambda i, ch: (ch[0], i, 0))] * n
                 + [pl.BlockSpec((3, tr, c), lambda i, ch: (0, i, 0))] * n,
        out_specs=[pl.BlockSpec((tr, c), lambda i, ch: (i, 0))] * n)
    return pl.pallas_call(body, name=name, grid_spec=grid_spec, out_shape=[_sds((r, c), F32)] * n,
                          compiler_params=_cp("parallel"))(chip, *own4s, *recv3s)


def _adamw_math(w, g, m, v):
    m = ADAM_B1 * m + (1.0 - ADAM_B1) * g
    v = ADAM_B2 * v + (1.0 - ADAM_B2) * (g * g)
    m_hat = m / (1.0 - ADAM_B1 ** ADAM_STEP)
    v_hat = v / (1.0 - ADAM_B2 ** ADAM_STEP)
    delta = -ADAM_LR * (m_hat / (jnp.sqrt(v_hat) + ADAM_EPS) + ADAM_WD * w)
    return delta, m, v


def _final_adamw(pas, pbs, ws, ms, vs, name, after=None):
    n = len(ws)
    r, c = ws[0].shape
    steps = _ew_steps(r, n * c * 9 * 4)
    tr = r // steps

    def body(*refs):
        ins, outs = refs[:5 * n], refs[len(refs) - 4 * n:]
        for k in range(n):
            pa_ref, pb_ref, w_ref, m_ref, v_ref = (ins[j * n + k] for j in range(5))
            g = pa_ref[...] + pb_ref[...]
            outs[4 * k][...] = g
            outs[4 * k + 1][...], outs[4 * k + 2][...], outs[4 * k + 3][...] = _adamw_math(
                w_ref[...], g, m_ref[...], v_ref[...])

    order = [] if after is None else [after]
    res = pl.pallas_call(
        body, name=name, grid=(steps,), in_specs=[_rows(tr, c)] * (5 * n) + [_full(t.shape) for t in order],
        out_specs=[_rows(tr, c)] * (4 * n), out_shape=[_sds((r, c), F32)] * (4 * n),
        compiler_params=_cp("parallel"))(*pas, *pbs, *ws, *ms, *vs, *order)
    return [res[4 * k:4 * k + 4] for k in range(n)]


def _pack_put(ref, name, val_ref):
    r = _pack_rows()[name]
    shape = val_ref.shape
    if len(shape) == 3:
        for b in range(shape[0]):
            ref[r:r + shape[1], b * shape[2]:(b + 1) * shape[2]] = val_ref[b]
    elif shape[1] == 2 * PACK_W:
        ref[r:r + 1, :] = val_ref[:, 0:PACK_W]
        ref[r + 1:r + 2, :] = val_ref[:, PACK_W:2 * PACK_W]
    else:
        ref[r:r + shape[0], 0:shape[1]] = val_ref[...]


def _pack_get(ref, name, shape):
    r = _pack_rows()[name]
    if len(shape) == 3:
        return jnp.stack([ref[r:r + shape[1], b * shape[2]:(b + 1) * shape[2]] for b in range(shape[0])])
    if shape[1] == 2 * PACK_W:
        return jnp.concatenate([ref[r:r + 1, :], ref[r + 1:r + 2, :]], axis=1)
    return ref[r:r + shape[0], 0:shape[1]]


def _pack_small(names, g, loss, rows, name):
    n = len(g)
    extra = [] if loss is None else [loss]

    def body(*refs):
        pack = refs[-1]
        pack[...] = jnp.zeros_like(pack)
        for a, nm in enumerate(names):
            _pack_put(pack, nm, refs[a])
        if extra:
            _pack_put(pack, 'loss', refs[n])

    return pl.pallas_call(body, name=name, out_shape=_sds((rows, PACK_W), F32), compiler_params=_cp())(*g, *extra)


def _all_peers():
    x, y, c = _mesh_pos()
    peers = []
    for k in range(1, 8):
        px = 1 - x if k & 4 else x
        py = 1 - y if k & 2 else y
        pc = 1 - c if k & 1 else c
        peers.append(((px, py, pc), 4 * px + 2 * py + pc))
    return peers, 4 * x + 2 * y + c


def _ag8_copies(ins, outs, sems):
    send_sems, recv_sems, loc_sems = sems
    n = len(ins)
    peers, me = _all_peers()

    def remote(k, j, slot):
        return pltpu.make_async_remote_copy(
            src_ref=ins[k], dst_ref=outs[k].at[slot], send_sem=send_sems.at[k, j], recv_sem=recv_sems.at[k, j],
            device_id=peers[j][0], device_id_type=MESH_ID)

    def local(k):
        return pltpu.make_async_copy(ins[k], outs[k].at[me], loc_sems.at[k])

    def start():
        for k in range(n):
            local(k).start()
            for j in range(7):
                remote(k, j, me).start()

    def wait():
        for k in range(n):
            for j in range(7):
                remote(k, j, peers[j][1]).wait_recv()
        for k in range(n):
            for j in range(7):
                remote(k, j, me).wait_send()
            local(k).wait()

    return start, _no_forward, wait


def _adamw_small(packs, late_own, late_packs, g_shapes, loss_shape, w, m, v):
    n = len(w)

    def body(*refs):
        packs_ref, own_ref, late_ref = refs[0], refs[1], refs[2]
        w_refs, m_refs, v_refs = (refs[3 + i * n:3 + (i + 1) * n] for i in range(3))
        o0 = 3 * n + 3
        go, do, mo, vo = (refs[o0 + i * n:o0 + (i + 1) * n] for i in range(4))
        loss_out, tot_ref = refs[o0 + 4 * n], refs[o0 + 4 * n + 1]
        x, y, c = _mesh_pos()
        me = 4 * x + 2 * y + c
        tot = packs_ref[0]
        late = jnp.where(me == 0, own_ref[...], late_ref[0])
        for d in range(1, 8):
            tot = tot + packs_ref[d]
            late = late + jnp.where(me == d, own_ref[...], late_ref[d])
        tot_ref[...] = tot
        tot_ref[0:LATE_ROWS, :] += late
        loss_out[...] = _pack_get(tot_ref, 'loss', loss_shape)
        for a, name in enumerate(SMALL):
            if name == 'conv_w':
                r = _pack_rows()[name]
                ga = tot_ref[r:r + g_shapes[a][0], pl.ds(pl.multiple_of((2 * x + y) * 128, 128), 128)]
            else:
                ga = _pack_get(tot_ref, name, g_shapes[a])
            go[a][...] = ga
            do[a][...], mo[a][...], vo[a][...] = _adamw_math(w_refs[a][...], ga, m_refs[a][...], v_refs[a][...])

    out_shape = [_sds(a.shape, F32) for a in w] * 4 + [_sds(loss_shape, F32)]
    return pl.pallas_call(body, name="adamw_small", out_shape=out_shape,
                          scratch_shapes=[pltpu.VMEM((PACK_ROWS, PACK_W), F32)],
                          compiler_params=_cp())(packs, late_own, late_packs, *w, *m, *v)


PACK_W = 512
PACK_ROWS = 160
LATE = ['g_mix', 'rel_bias']
LATE_ROWS = 32


def _pack_rows():
    rows, r = {}, 0
    for name in ['g_mix', 'g_cross', 'g_mem', 'g_ffn', 'g_final']:
        rows[name] = r
        r += 2
    for name in ['conv_b', 'b_rg', 'b_ig', 'lru_L', 'g_out_attn', 'g_out_lru']:
        rows[name] = r
        r += 1
    rows['conv_w'] = r
    rows['loss'] = r + 4
    rows['rel_bias'] = 24
    rows['w_rg'] = 32
    rows['w_ig'] = 32 + LRU_BLOCK
    assert r + 5 <= 24 and rows['w_ig'] + LRU_BLOCK == PACK_ROWS
    assert rows['g_mix'] + 2 <= LATE_ROWS and rows['rel_bias'] + 8 <= LATE_ROWS
    return rows


INPUT_NAMES = (['x', 'mem'] + WEIGHTS + ['loss_target'] + ['m_' + n for n in WEIGHTS] + ['v_' + n for n in WEIGHTS])


def kernel(x, mem, g_mix, w_in, rel_bias, conv_w, conv_b, w_rg, b_rg, w_ig, b_ig, lru_L, g_out_attn, g_out_lru, w_out, g_cross, g_mem, wq_c, wk_c, wv_c, wo_c, g_ffn, w_gate, w_up, w_down, g_final, loss_target, m_g_mix, m_w_in, m_rel_bias, m_conv_w, m_conv_b, m_w_rg, m_b_rg, m_w_ig, m_b_ig, m_lru_L, m_g_out_attn, m_g_out_lru, m_w_out, m_g_cross, m_g_mem, m_wq_c, m_wk_c, m_wv_c, m_wo_c, m_g_ffn, m_w_gate, m_w_up, m_w_down, m_g_final, v_g_mix, v_w_in, v_rel_bias, v_conv_w, v_conv_b, v_w_rg, v_b_rg, v_w_ig, v_b_ig, v_lru_L, v_g_out_attn, v_g_out_lru, v_w_out, v_g_cross, v_g_mem, v_wq_c, v_wk_c, v_wv_c, v_wo_c, v_g_ffn, v_w_gate, v_w_up, v_w_down, v_g_final):
    a = dict(zip(INPUT_NAMES, (x, mem, g_mix, w_in, rel_bias, conv_w, conv_b, w_rg, b_rg, w_ig, b_ig, lru_L, g_out_attn, g_out_lru, w_out, g_cross, g_mem, wq_c, wk_c, wv_c, wo_c, g_ffn, w_gate, w_up, w_down, g_final, loss_target, m_g_mix, m_w_in, m_rel_bias, m_conv_w, m_conv_b, m_w_rg, m_b_rg, m_w_ig, m_b_ig, m_lru_L, m_g_out_attn, m_g_out_lru, m_w_out, m_g_cross, m_g_mem, m_wq_c, m_wk_c, m_wv_c, m_wo_c, m_g_ffn, m_w_gate, m_w_up, m_w_down, m_g_final, v_g_mix, v_w_in, v_rel_bias, v_conv_w, v_conv_b, v_w_rg, v_b_rg, v_w_ig, v_b_ig, v_lru_L, v_g_out_attn, v_g_out_lru, v_w_out, v_g_cross, v_g_mem, v_wq_c, v_wk_c, v_wv_c, v_wo_c, v_g_ffn, v_w_gate, v_w_up, v_w_down, v_g_final)))
    chip = 2 * lax.axis_index("x") + lax.axis_index("y")

    def shard(name):
        arr = a[name][0]
        base = name[2:] if name[:2] in ('m_', 'v_') else name
        return jnp.swapaxes(arr, 0, 1) if base in TRANSPOSED else arr

    shards = {'w_in': _cast_shards([shard('w_in')], "cast_w_in")[0]}
    rest = [n for n in BIG if n != 'w_in']
    *cast, w_in_g, conv_w_g = _cast_shards([shard(n) for n in rest], "cast_rest",
                                           [("ag", [shards['w_in']]), ("agf", [a['conv_w'][0]])])
    shards.update(zip(rest, cast))
    conv_w_full = conv_w_g.transpose(1, 0, 2).reshape(4, D_LRU)

    p = {n: a[n] for n in SMALL}
    p['rel_bias'] = a['rel_bias'][0]
    p['w_rg'] = a['w_rg'][0]
    p['w_ig'] = a['w_ig'][0]
    p['conv_w'] = conv_w_full
    p['g_final'] = a['g_final'][None, :]
    chip_arr = jnp.reshape(chip, (1,)).astype(jnp.int32)
    loss_part, grad_x, small, _, part, sib, packs = _local_step(
        a['x'][0], a['mem'][0], a['loss_target'][0], p, {'w_in': w_in_g}, shards, chip_arr)

    def late_copies(refs, send_sems, recv_sems):
        return _late_copies(refs[0], refs[1], refs[2], refs[3], send_sems, recv_sems)

    late_pack = _pack_small(LATE, [small[n] for n in LATE], None, LATE_ROWS, "pack_late")
    bufs = [part['w_in'], lax.empty(part['w_in'].shape, F32), late_pack, jnp.zeros((8, LATE_ROWS, PACK_W), F32)]
    sems, bufs, token = _split_start("late_exchange_start", bufs, 8, late_copies)
    out = {}

    def adamw(group, after=None):
        results = _final_adamw([part[n] for n in group], [sib[n] for n in group], [shard(n) for n in group],
                               [shard('m_' + n) for n in group], [shard('v_' + n) for n in group],
                               "adamw_" + group[0], after)
        for n, res in zip(group, results):
            out[n] = [jnp.swapaxes(r, 0, 1) for r in res] if n in TRANSPOSED else res
        return results[-1][0]

    adamw(MID, token)
    done = adamw(['w_gate', 'w_up', 'w_down'], token)
    _, sib['w_in'], late_pack, late_packs = _split_wait("late_exchange_wait", sems, bufs, 8, late_copies, done)
    adamw(['w_in'])

    def natural(arr):
        return arr[0] if arr.ndim >= 3 else (arr[None, :] if arr.ndim == 1 else arr)

    small_out = _adamw_small(packs, late_pack, late_packs, [small[n].shape for n in SMALL],
                             loss_part.shape, *[[natural(a[pre + n]) for n in SMALL] for pre in ('', 'm_', 'v_')])
    ns = len(SMALL)
    loss = small_out[4 * ns][0, 0]

    def leaf(i, n):
        if n in BIG:
            return out[n][i][None]
        return small_out[i * ns + SMALL.index(n)].reshape(a[n].shape)

    return (loss, grad_x[None], *[leaf(i, n) for i in range(4) for n in WEIGHTS])
```

```python
import math

import jax
import jax.numpy as jnp
from jax import lax
from jax.experimental import pallas as pl
from jax.experimental.pallas import tpu as pltpu

F32 = jnp.float32
BF16 = jnp.bfloat16

D_MODEL = 1024
D_ATT = 512
D_LRU = 512
HEAD_DIM = 64
ATT_HEADS = 8
CHUNK = 64
LEFT_CHUNKS = 8
MAX_REL = 128
X_HEADS = 4
X_HEAD_DIM = 256
N_SHARD = 4
IN_SH = 640
D_IN = N_SHARD * IN_SH
FF_SH = 704
D_FF = N_SHARD * FF_SH
EPS = 1e-6
LRU_C = 8.0
LRU_BLOCKS = 8
LRU_BLOCK = 64
QB = 256
KB = 768
ROLL_W = 1024
NEG = -1e30
ATT_SCALE = HEAD_DIM ** -0.5
X_SCALE = X_HEAD_DIM ** -0.5

ADAM_LR = 0.001
ADAM_B1 = 0.9
ADAM_B2 = 0.999
ADAM_EPS = 1e-08
ADAM_WD = 0.01
ADAM_STEP = 10

VMEM_LIMIT_V7X = 56 * 1024 * 1024
BF16_ROWS = 16


EW_VMEM_BUDGET = 40 * 1024 * 1024


def _ew_steps(rows, bytes_per_row):
    return min(s for s in (2, 4, 8, 16) if rows % (s * BF16_ROWS) == 0
               and 2 * (rows // s) * bytes_per_row <= EW_VMEM_BUDGET)
MESH_ID = pl.DeviceIdType.MESH

WEIGHTS = ['g_mix', 'w_in', 'rel_bias', 'conv_w', 'conv_b', 'w_rg', 'b_rg', 'w_ig', 'b_ig', 'lru_L',
           'g_out_attn', 'g_out_lru', 'w_out', 'g_cross', 'g_mem', 'wq_c', 'wk_c', 'wv_c', 'wo_c',
           'g_ffn', 'w_gate', 'w_up', 'w_down', 'g_final']
BIG = ['w_in', 'w_out', 'wq_c', 'wk_c', 'wv_c', 'wo_c', 'w_gate', 'w_up', 'w_down']
SMALL = [n for n in WEIGHTS if n not in BIG]


def _sds(shape, dtype):
    return jax.ShapeDtypeStruct(shape, dtype)


def _cp(*sem):
    return pltpu.CompilerParams(dimension_semantics=sem or None, vmem_limit_bytes=VMEM_LIMIT_V7X)


def _rows(tm, n):
    return pl.BlockSpec((tm, n), lambda i: (i, 0))


def _full(shape):
    nd = len(shape)
    return pl.BlockSpec(shape, lambda i: (0,) * nd)


def _dot(a, b):
    return jnp.dot(a, b, preferred_element_type=F32)


def _dot_nt(a, b):
    return lax.dot_general(a, b, (((1,), (1,)), ((), ())), preferred_element_type=F32)


def _dot_tn(a, b):
    return lax.dot_general(a, b, (((0,), (0,)), ((), ())), preferred_element_type=F32)


def _rinv(x):
    return lax.rsqrt(jnp.mean(x * x, axis=-1, keepdims=True) + EPS)


def _rms_bwd(dy, x, g):
    r = _rinv(x)
    yh = x * r
    dyh = dy * g
    dx = r * (dyh - yh * jnp.mean(dyh * yh, axis=-1, keepdims=True))
    return dx, jnp.sum(dy * yh, axis=0, keepdims=True)


def _gelu(x):
    c = math.sqrt(2.0 / math.pi)
    t = jnp.tanh(c * (x + 0.044715 * x * x * x))
    return 0.5 * x * (1.0 + t)


def _gelu_and_grad(x):
    c = math.sqrt(2.0 / math.pi)
    t = jnp.tanh(c * (x + 0.044715 * x * x * x))
    g = 0.5 * x * (1.0 + t)
    dg = 0.5 * (1.0 + t) + 0.5 * x * (1.0 - t * t) * c * (1.0 + 3.0 * 0.044715 * x * x)
    return g, dg


def _neg_expm1(z):
    series = -z * (1.0 + z * (0.5 + z * ((1.0 / 6.0) + z * (1.0 / 24.0))))
    return jnp.where(z > -0.03, series, 1.0 - jnp.exp(z))


def _lru_gates(u, wrg, brg, wig, big, lam):
    ub = u.astype(BF16)
    r = jax.nn.sigmoid(_dot(ub, wrg) + brg)
    ig = jax.nn.sigmoid(_dot(ub, wig) + big)
    sp = jnp.maximum(-lam, 0.0) + jnp.log1p(jnp.exp(-jnp.abs(lam)))
    la = -LRU_C * r * sp
    a = jnp.exp(la)
    mult = jnp.sqrt(jnp.maximum(_neg_expm1(2.0 * la), 0.0))
    return ub, r, ig, sp, a, mult


def _scan8(a8, b8, hprev):
    row = lax.broadcasted_iota(jnp.int32, a8.shape, 0)
    aa, bb = a8, b8
    for d in (1, 2, 4):
        a_s = pltpu.roll(aa, d, 0)
        b_s = pltpu.roll(bb, d, 0)
        m = row >= d
        bb = jnp.where(m, aa * b_s + bb, bb)
        aa = jnp.where(m, aa * a_s, aa)
    return aa * hprev + bb


def _rscan8(c8, d8, lnext):
    row = lax.broadcasted_iota(jnp.int32, c8.shape, 0)
    cc, dd = c8, d8
    for d in (1, 2, 4):
        c_s = pltpu.roll(cc, 8 - d, 0)
        d_s = pltpu.roll(dd, 8 - d, 0)
        m = row < 8 - d
        dd = jnp.where(m, cc * d_s + dd, dd)
        cc = jnp.where(m, cc * c_s, cc)
    return cc * lnext + dd


def _mesh_pos():
    return lax.axis_index("x"), lax.axis_index("y"), lax.axis_index("c")


def _other_chips(x, y):
    return [(1 - x, y), (x, 1 - y), (1 - x, 1 - y)]


def _no_forward():
    pass


def _ag_full_copies(ins, outs, sems):
    send_sems, recv_sems, loc_sems = sems
    n = len(ins)
    x, y, c = _mesh_pos()
    mine = 2 * x + y
    chips = _other_chips(x, y)

    def remote(k, j, slot):
        px, py = chips[j]
        return pltpu.make_async_remote_copy(
            src_ref=ins[k], dst_ref=outs[k].at[slot], send_sem=send_sems.at[k, j], recv_sem=recv_sems.at[k, j],
            device_id=(px, py, c), device_id_type=MESH_ID)

    def local(k):
        return pltpu.make_async_copy(ins[k], outs[k].at[mine], loc_sems.at[k])

    def start():
        for k in range(n):
            local(k).start()
            for j in range(3):
                remote(k, j, mine).start()

    def wait():
        for k in range(n):
            for j, (px, py) in enumerate(chips):
                remote(k, j, 2 * px + py).wait_recv()
        for k in range(n):
            for j in range(3):
                remote(k, j, mine).wait_send()
            local(k).wait()

    return start, _no_forward, wait


def _ag_copies(ins, outs, sems):
    send_sems, recv_sems, fsend_sems, frecv_sems, loc_sems = sems
    n = len(ins)
    x, y, c = _mesh_pos()
    mine = 2 * x + y
    chips = _other_chips(x, y)

    def half(ref, hc):
        r = ref.shape[0] // 2
        return ref.at[pl.ds(pl.multiple_of(hc * r, 16), r)]

    def ici(k, j, slot):
        px, py = chips[j]
        return pltpu.make_async_remote_copy(
            src_ref=half(ins[k], c), dst_ref=half(outs[k].at[slot], c),
            send_sem=send_sems.at[k, j], recv_sem=recv_sems.at[k, j],
            device_id=(px, py, c), device_id_type=MESH_ID)

    def d2d(k, j, hc):
        px, py = chips[j]
        part = half(outs[k].at[2 * px + py], hc)
        return pltpu.make_async_remote_copy(
            src_ref=part, dst_ref=part, send_sem=fsend_sems.at[k, j], recv_sem=frecv_sems.at[k, j],
            device_id=(x, y, 1 - c), device_id_type=MESH_ID)

    def local(k):
        return pltpu.make_async_copy(ins[k], outs[k].at[mine], loc_sems.at[k])

    def start():
        for k in range(n):
            local(k).start()
            for j in range(3):
                ici(k, j, mine).start()

    def forward():
        for k in range(n):
            for j, (px, py) in enumerate(chips):
                ici(k, j, 2 * px + py).wait_recv()
                d2d(k, j, c).start()

    def wait():
        for k in range(n):
            for j in range(3):
                d2d(k, j, 1 - c).wait_recv()
        for k in range(n):
            for j in range(3):
                d2d(k, j, c).wait_send()
                ici(k, j, mine).wait_send()
            local(k).wait()

    return start, forward, wait


def _rs_copies(ins, outs, sems):
    send_sems, recv_sems = sems
    n = len(ins)
    x, y, c = _mesh_pos()
    chips = _other_chips(x, y)

    def remote(k, j):
        px, py = chips[j]
        return pltpu.make_async_remote_copy(
            src_ref=ins[k].at[2 * px + py], dst_ref=outs[k].at[j],
            send_sem=send_sems.at[k, j], recv_sem=recv_sems.at[k, j],
            device_id=(px, py, c), device_id_type=MESH_ID)

    def start():
        for k in range(n):
            for j in range(3):
                remote(k, j).start()

    def wait():
        for k in range(n):
            for j in range(3):
                remote(k, j).wait_recv()
        for k in range(n):
            for j in range(3):
                remote(k, j).wait_send()

    return start, _no_forward, wait


def _swap_copies(ins, outs, sems):
    send_sems, recv_sems = sems
    x, y, c = _mesh_pos()
    copies = [pltpu.make_async_remote_copy(
        src_ref=ins[k], dst_ref=outs[k], send_sem=send_sems.at[k], recv_sem=recv_sems.at[k],
        device_id=(x, y, 1 - c), device_id_type=MESH_ID) for k in range(len(ins))]

    def start():
        for cp in copies:
            cp.start()

    def wait():
        for cp in copies:
            cp.wait()

    return start, _no_forward, wait


def _comm_plan(groups):
    plan, arrs, shapes, sems = [], [], [], []
    for kind, group in groups:
        k = len(group)
        arrs += group
        per_peer = pltpu.SemaphoreType.DMA((k, 3))
        if kind == "ag":
            shapes += [_sds((N_SHARD,) + w.shape, w.dtype) for w in group]
            gsems = [per_peer] * 4 + [pltpu.SemaphoreType.DMA((k,))]
            maker = _ag_copies
        elif kind == "agf":
            shapes += [_sds((N_SHARD,) + w.shape, w.dtype) for w in group]
            gsems = [per_peer] * 2 + [pltpu.SemaphoreType.DMA((k,))]
            maker = _ag_full_copies
        elif kind == "ag8":
            shapes += [_sds((8,) + g.shape, g.dtype) for g in group]
            gsems = [pltpu.SemaphoreType.DMA((k, 7))] * 2 + [pltpu.SemaphoreType.DMA((k,))]
            maker = _ag8_copies
        elif kind == "rs":
            shapes += [_sds((3,) + g.shape[1:], g.dtype) for g in group]
            gsems = [pltpu.SemaphoreType.DMA((k, 3)), pltpu.SemaphoreType.DMA((k, 3))]
            maker = _rs_copies
        else:
            shapes += [_sds(g.shape, g.dtype) for g in group]
            gsems = [pltpu.SemaphoreType.DMA((k,)), pltpu.SemaphoreType.DMA((k,))]
            maker = _swap_copies
        plan.append((maker, k, len(gsems)))
        sems += gsems
    return plan, arrs, shapes, sems


def _comm_fns(plan, cins, couts, sems):
    fns, a, s = [], 0, 0
    for maker, k, ns in plan:
        fns.append(maker(cins[a:a + k], couts[a:a + k], sems[s:s + ns]))
        a += k
        s += ns

    def start():
        for st, _, _ in fns:
            st()

    def forward():
        for _, fw, _ in fns:
            fw()

    def wait():
        for _, _, wt in fns:
            wt()

    return start, forward, wait


def _call(body, name, grid, in_specs, out_specs, out_shape, scratch, args, sem, comm=None):
    if not comm:
        return pl.pallas_call(body, name=name, grid=grid, in_specs=in_specs, out_specs=out_specs,
                              out_shape=out_shape, scratch_shapes=scratch, compiler_params=_cp(sem))(*args)
    plan, c_arrs, c_shapes, c_sems = _comm_plan(comm)
    k = len(c_arrs)
    n_in, n_out, n_scr = len(in_specs), len(out_specs), len(scratch)
    last = grid[0] - 1
    fwd_step = max(1, (2 * last) // 3)

    def wrapped(*refs):
        ins, cins = refs[:n_in], refs[n_in:n_in + k]
        o0 = n_in + k
        outs, couts = refs[o0:o0 + n_out], refs[o0 + n_out:o0 + n_out + k]
        s0 = o0 + n_out + k
        start, forward, wait = _comm_fns(plan, cins, couts, refs[s0 + n_scr:])
        pl.when(pl.program_id(0) == 0)(start)
        pl.when(pl.program_id(0) == fwd_step)(forward)
        body(*ins, *outs, *refs[s0:s0 + n_scr])
        pl.when(pl.program_id(0) == last)(wait)

    return pl.pallas_call(
        wrapped, name=name, grid=grid, in_specs=list(in_specs) + [_any()] * k,
        out_specs=list(out_specs) + [_any()] * k, out_shape=list(out_shape) + c_shapes,
        scratch_shapes=list(scratch) + c_sems, compiler_params=_cp(sem))(*args, *c_arrs)


def _tail_copies(slots_ref, land_ref, part_refs, sib_refs, send_sems, recv_sems):
    x, y, c = _mesh_pos()
    copies = []
    for j, (px, py) in enumerate(_other_chips(x, y)):
        copies.append(pltpu.make_async_remote_copy(
            src_ref=slots_ref.at[2 * px + py], dst_ref=land_ref.at[j], send_sem=send_sems[j], recv_sem=recv_sems[j],
            device_id=(px, py, c), device_id_type=MESH_ID))
    for k, (p_ref, s_ref) in enumerate(zip(part_refs, sib_refs)):
        copies.append(pltpu.make_async_remote_copy(
            src_ref=p_ref, dst_ref=s_ref, send_sem=send_sems[3 + k], recv_sem=recv_sems[3 + k],
            device_id=(x, y, 1 - c), device_id_type=MESH_ID))
    return copies


def _late_copies(part_ref, sib_ref, pack_ref, packs_ref, send_sems, recv_sems):
    x, y, c = _mesh_pos()
    peers, me = _all_peers()
    copies = [pltpu.make_async_remote_copy(
        src_ref=part_ref, dst_ref=sib_ref, send_sem=send_sems[0], recv_sem=recv_sems[0],
        device_id=(x, y, 1 - c), device_id_type=MESH_ID)]
    for j in range(7):
        copies.append(pltpu.make_async_remote_copy(
            src_ref=pack_ref, dst_ref=packs_ref.at[me], send_sem=send_sems[1 + j], recv_sem=recv_sems[1 + j],
            device_id=peers[j][0], device_id_type=MESH_ID))
    return copies


def _split_start(name, bufs, ncp, make_copies):
    hbm = pl.BlockSpec(memory_space=pltpu.HBM)
    sem = pl.BlockSpec(memory_space=pltpu.SEMAPHORE)
    bufs = [pltpu.with_memory_space_constraint(b, pltpu.HBM) for b in bufs]
    nb = len(bufs)

    def body(*refs):
        for cp in make_copies(refs[:nb], refs[nb:nb + ncp], refs[nb + ncp:nb + 2 * ncp]):
            cp.start()
        refs[-1][...] = jnp.zeros_like(refs[-1])

    out = pl.pallas_call(
        body, name=name,
        out_shape=[pltpu.SemaphoreType.DMA(())] * (2 * ncp) + [pltpu.HBM(b.shape, b.dtype) for b in bufs]
                  + [_sds((8, 128), F32)],
        in_specs=[hbm] * nb, out_specs=[sem] * (2 * ncp) + [hbm] * nb + [pl.BlockSpec(memory_space=pltpu.VMEM)],
        input_output_aliases={i: 2 * ncp + i for i in range(nb)},
        compiler_params=pltpu.CompilerParams(has_side_effects=pltpu.SideEffectType.DATAFLOW_SIDE_EFFECTING),
    )(*bufs)
    return out[:2 * ncp], out[2 * ncp:2 * ncp + nb], out[-1]


def _split_wait(name, sems, bufs, ncp, make_copies, after):
    nb = len(bufs)
    hbm = pl.BlockSpec(memory_space=pltpu.HBM)
    sem = pl.BlockSpec(memory_space=pltpu.SEMAPHORE)

    def body(*refs):
        for cp in make_copies(refs[:nb], refs[nb:nb + ncp], refs[nb + ncp:nb + 2 * ncp]):
            cp.wait_send()
            cp.wait_recv()

    return pl.pallas_call(
        body, name=name, out_shape=[pltpu.HBM(b.shape, b.dtype) for b in bufs],
        in_specs=[hbm] * nb + [sem] * (2 * ncp) + [_any()], out_specs=[hbm] * nb,
        input_output_aliases={i: i for i in range(nb)},
        compiler_params=pltpu.CompilerParams(has_side_effects=pltpu.SideEffectType.DATAFLOW_SIDE_EFFECTING),
    )(*bufs, *sems, after)


def _any():
    return pl.BlockSpec(memory_space=pl.ANY)


def _start_copies(pairs, sems, first=0):
    copies = [pltpu.make_async_copy(src, dst, sems.at[first + i]) for i, (src, dst) in enumerate(pairs)]
    for cp in copies:
        cp.start()
    return copies


def _copy_together(pairs, sems):
    for cp in _start_copies(pairs, sems):
        cp.wait()


def _load_w_in_once(w_hbm, w_ref, sems):
    @pl.when(pl.program_id(0) == 0)
    def _():
        _copy_together([(w_hbm.at[s], w_ref.at[:, pl.ds(s * IN_SH, IN_SH)]) for s in range(N_SHARD)], sems)


def _f_inproj(x, g_mix, w_in_g, tm, comm=None):
    s_len = x.shape[0]
    pad_rows = LEFT_CHUNKS * CHUNK
    npad = pad_rows // tm

    def body(x_ref, g_ref, w_hbm, h_ref, qkv_ref, xg_ref, w_ref, w_sems):
        i = pl.program_id(0)
        _load_w_in_once(w_hbm, w_ref, w_sems)

        @pl.when(i < npad)
        def _():
            qkv_ref[...] = jnp.zeros_like(qkv_ref)

        @pl.when(i >= npad)
        def _():
            xv = x_ref[...]
            h = (xv * _rinv(xv) * g_ref[...]).astype(BF16)
            h_ref[...] = h
            proj = _dot(h, w_ref[...])
            qkv_ref[:, 0:D_ATT] = (proj[:, 0:D_ATT] * ATT_SCALE).astype(BF16)
            qkv_ref[:, D_ATT:3 * D_ATT] = proj[:, D_ATT:3 * D_ATT].astype(BF16)
            xg_ref[...] = proj[:, 3 * D_ATT:D_IN]

    def tok(n):
        return pl.BlockSpec((tm, n), lambda i: (jnp.maximum(i - npad, 0), 0))

    return _call(
        body, "f_inproj", (s_len // tm + npad,),
        [tok(1024), _full((1, 1024)), _any()],
        [tok(1024), _rows(tm, 1536), tok(1024)],
        [_sds((s_len, 1024), BF16), _sds((s_len + pad_rows, 1536), BF16), _sds((s_len, 1024), F32)],
        [pltpu.VMEM((1024, D_IN), BF16), pltpu.SemaphoreType.DMA((N_SHARD,))], (x, g_mix, w_in_g), "arbitrary", comm)


N_BIAS = 3


def _bias_table(frow_ref, bias_sc):
    qa = lax.broadcasted_iota(jnp.int32, (QB, KB), 0) // CHUNK
    kcol = lax.broadcasted_iota(jnp.int32, (QB, KB), 1)
    kb = kcol // CHUNK
    band = jnp.where((kb >= qa) & (kb - qa <= LEFT_CHUNKS), 0.0, NEG).astype(F32)
    for h in range(ATT_HEADS):
        row = jnp.broadcast_to(frow_ref[h:h + 1, :], (QB, ROLL_W))
        toep = pltpu.roll(row, 0, 1, stride=1, stride_axis=0)
        gen = toep[:, 0:KB] + band
        bias_sc[N_BIAS - 1, h] = gen
        for v in range(N_BIAS - 1):
            pad_keys = LEFT_CHUNKS * CHUNK - v * QB
            bias_sc[v, h] = gen + jnp.where(kcol < pad_keys, NEG, 0.0).astype(F32)


def _even_lanes():
    return lax.broadcasted_iota(jnp.int32, (1, 2 * HEAD_DIM), 1) < HEAD_DIM


def _att_probs(qm, kts, bias):
    s = jnp.concatenate([_dot_nt(qm, k) for k in kts], axis=1) + bias
    return jnp.exp(s - jnp.max(s, axis=-1, keepdims=True))


def _att_in_specs(clamp):
    def spec(j, col):
        return pl.BlockSpec((QB, D_ATT), lambda i: (clamp(i) + j, col))
    return [spec(2, 0), spec(0, 1), spec(1, 1), spec(2, 1), spec(0, 2), spec(1, 2), spec(2, 2)]


def _f_attn(qkv_pad, frow, comm=None):
    s_len = qkv_pad.shape[0] - LEFT_CHUNKS * CHUNK
    nb = s_len // QB

    def body(q_ref, k0, k1, k2, v0, v1, v2, frow_ref, o_ref, bias_sc):
        i = pl.program_id(0)

        @pl.when(i == 0)
        def _():
            _bias_table(frow_ref, bias_sc)

        var = jnp.minimum(i, N_BIAS - 1)
        even = _even_lanes()
        for hp in range(ATT_HEADS // 2):
            cs = slice(hp * 2 * HEAD_DIM, (hp + 1) * 2 * HEAD_DIM)
            qt = q_ref[:, cs]
            kts = [k0[:, cs], k1[:, cs], k2[:, cs]]
            vts = [v0[:, cs], v1[:, cs], v2[:, cs]]
            res = []
            for e in range(2):
                keep = even if e == 0 else jnp.logical_not(even)
                pb = _att_probs(jnp.where(keep, qt, 0), kts, bias_sc[var, 2 * hp + e]).astype(BF16)
                r = _dot(pb, jnp.concatenate([jnp.where(keep, v, 1) for v in vts], axis=0))
                res.append(r / pltpu.roll(r, HEAD_DIM, 1))
            o_ref[:, cs] = jnp.where(even, res[0], res[1])

    return _call(
        body, "f_attn", (nb,),
        _att_in_specs(lambda i: i) + [_full((ATT_HEADS, ROLL_W))],
        [_rows(QB, D_ATT)], [_sds((s_len, D_ATT), F32)],
        [pltpu.VMEM((N_BIAS, ATT_HEADS, QB, KB), F32)], (*([qkv_pad] * 7), frow), "arbitrary", comm)


def _f_lru(xg, conv_w, conv_b, wrg, brg, wig, big, lam, tl, comm=None):
    s_len = xg.shape[0]

    def body(xg_ref, cw_ref, cb_ref, wrg_ref, brg_ref, wig_ref, big_ref, l_ref,
             rec_ref, u_ref, hs_ref, xbuf, a_sc, b_sc, hcar):
        i = pl.program_id(0)

        @pl.when(i == 0)
        def _():
            xbuf[0:8, :] = jnp.zeros((8, D_LRU), F32)
            hcar[...] = jnp.zeros((8, D_LRU), F32)

        xu0 = xg_ref[:, 0:D_LRU]
        xbuf[8:8 + tl, :] = xu0
        u = cb_ref[...] + cw_ref[0:1, :] * xbuf[pl.ds(5, tl), :]
        for j in range(1, 4):
            u = u + cw_ref[j:j + 1, :] * xbuf[pl.ds(5 + j, tl), :]
        xbuf[0:8, :] = xu0[tl - 8:tl, :]
        u_ref[...] = u
        _, _, ig, _, a, mult = _lru_gates(u, wrg_ref[...], brg_ref[...], wig_ref[...], big_ref[...], l_ref[...])
        a_sc[...] = a
        b_sc[...] = mult * (ig * u)

        def grp(g, hprev):
            off = pl.multiple_of(g * 8, 8)
            h8 = _scan8(a_sc[pl.ds(off, 8), :], b_sc[pl.ds(off, 8), :], hprev)
            hs_ref[pl.ds(off, 8), :] = h8
            return h8[7:8, :]

        hcar[0:1, :] = lax.fori_loop(0, tl // 8, grp, hcar[0:1, :])
        rec_ref[...] = hs_ref[...] * _gelu(xg_ref[:, D_LRU:2 * D_LRU])

    vec = _full((1, D_LRU))
    return _call(
        body, "f_lru", (s_len // tl,),
        [_rows(tl, 1024), _full((4, D_LRU)), vec, _full((D_LRU, D_LRU)), vec, _full((D_LRU, D_LRU)), vec, vec],
        [_rows(tl, D_LRU)] * 3, [_sds((s_len, D_LRU), F32)] * 3,
        [pltpu.VMEM((tl + 8, D_LRU), F32), pltpu.VMEM((tl, D_LRU), F32),
         pltpu.VMEM((tl, D_LRU), F32), pltpu.VMEM((8, D_LRU), F32)],
        (xg, conv_w, conv_b, wrg, brg, wig, big, lam), "arbitrary", comm)


def _f_mem(mem, g_mem, wk, wv):
    def body(mem_ref, g_ref, wk_ref, wv_ref, mn_ref, kx_ref, vx_ref):
        mv = mem_ref[...]
        mn = (mv * _rinv(mv) * g_ref[...]).astype(BF16)
        mn_ref[...] = mn
        kx_ref[...] = _dot(mn, wk_ref[...]).astype(BF16)
        vx_ref[...] = _dot(mn, wv_ref[...]).astype(BF16)

    m = mem.shape[0]
    return pl.pallas_call(
        body, name="f_mem", out_shape=[_sds((m, 1024), BF16)] * 3,
        compiler_params=_cp())(mem, g_mem, wk, wv)


def _xattn_probs(q, k):
    s = _dot_nt(q, k) * X_SCALE
    m = jnp.max(s, axis=-1, keepdims=True)
    p = jnp.exp(s - m)
    return p, jnp.sum(p, axis=-1, keepdims=True)


def _f_mid(x, att, rec, g_oa, g_ol, w_out, g_cross, wq, kx, vx, wo, tm, comm=None):
    s_len = x.shape[0]
    m_len = kx.shape[0]

    def body(x_ref, att_ref, rec_ref, goa_ref, gol_ref, wout_ref, gc_ref, wq_ref, kx_ref, vx_ref, wo_ref,
             mg_ref, x1_ref, hc_ref, qx_ref, ox_ref, x2_ref):
        av = att_ref[...]
        rv = rec_ref[...]
        mg_ref[:, 0:D_ATT] = (av * _rinv(av) * goa_ref[...]).astype(BF16)
        mg_ref[:, D_ATT:1024] = (rv * _rinv(rv) * gol_ref[...]).astype(BF16)
        x1 = x_ref[...] + _dot(mg_ref[...], wout_ref[...])
        x1_ref[...] = x1
        hc = (x1 * _rinv(x1) * gc_ref[...]).astype(BF16)
        hc_ref[...] = hc
        qx_ref[...] = _dot(hc, wq_ref[...]).astype(BF16)
        for h in range(X_HEADS):
            sl = slice(h * X_HEAD_DIM, (h + 1) * X_HEAD_DIM)
            p, l = _xattn_probs(qx_ref[:, sl], kx_ref[:, sl])
            ox_ref[:, sl] = (_dot(p.astype(BF16), vx_ref[:, sl]) / l).astype(BF16)
        x2_ref[...] = x1 + _dot(ox_ref[...], wo_ref[...])

    sq = _full((1024, 1024))
    return _call(
        body, "f_mid", (s_len // tm,),
        [_rows(tm, 1024), _rows(tm, 512), _rows(tm, 512), _full((1, 512)), _full((1, 512)), sq,
         _full((1, 1024)), sq, _full((m_len, 1024)), _full((m_len, 1024)), sq],
        [_rows(tm, 1024)] * 6,
        [_sds((s_len, 1024), BF16), _sds((s_len, 1024), F32), _sds((s_len, 1024), BF16),
         _sds((s_len, 1024), BF16), _sds((s_len, 1024), BF16), _sds((s_len, 1024), F32)],
        [], (x, att, rec, g_oa, g_ol, w_out, g_cross, wq, kx, vx, wo), "arbitrary", comm)


FF_CHUNKS = [(0, 1280), (1280, D_FF)]


def _first_step_and_rest(step):
    pl.when(pl.program_id(0) == 0)(lambda: step(True))
    pl.when(pl.program_id(0) > 0)(lambda: step(False))


def _ffn_weights(first, pairs, sems):
    if not first:
        return lambda c, j: None
    copies = _start_copies([(hbm.at[c0:c1, :], vmem.at[c0:c1, :]) for c0, c1 in FF_CHUNKS for hbm, vmem in pairs],
                           sems)
    return lambda c, j: copies[c * len(pairs) + j].wait()


def _f_ffn(x2, tgt, g_ffn, g_final, wg, wu, wd, tm):
    s_len = x2.shape[0]

    def body(x2_ref, t_ref, gf_ref, gfin_ref, wg_hbm, wu_hbm, wd_hbm,
             hf_ref, g_ref, u_ref, a_ref, dx3_ref, loss_ref, dgfin_ref, wg_ref, wu_ref, wd_ref, w_sems):
        @pl.when(pl.program_id(0) == 0)
        def _():
            _copy_together([(wg_hbm, wg_ref), (wu_hbm, wu_ref), (wd_hbm, wd_ref)], w_sems)
            loss_ref[...] = jnp.zeros_like(loss_ref)
            dgfin_ref[...] = jnp.zeros_like(dgfin_ref)

        x2v = x2_ref[...]
        hf = (x2v * _rinv(x2v) * gf_ref[...]).astype(BF16)
        hf_ref[...] = hf
        x3 = x2v
        for c0, c1 in FF_CHUNKS:
            gv = _dot_nt(hf, wg_ref[c0:c1, :])
            uv = _dot_nt(hf, wu_ref[c0:c1, :])
            av = (gv * jax.nn.sigmoid(gv) * uv).astype(BF16)
            g_ref[:, c0:c1] = gv.astype(BF16)
            u_ref[:, c0:c1] = uv.astype(BF16)
            a_ref[:, c0:c1] = av
            x3 = x3 + _dot(av, wd_ref[c0:c1, :])
        r3 = _rinv(x3)
        yh = x3 * r3
        gfin = gfin_ref[...]
        err = yh * gfin - t_ref[...]
        loss_ref[...] += jnp.full((1, 128), 0.5 / D_MODEL, F32) * jnp.sum(err * err)
        dy = err * (1.0 / D_MODEL)
        dgfin_ref[...] += jnp.sum(dy * yh, axis=0, keepdims=True)
        dyh = dy * gfin
        dx3_ref[...] = r3 * (dyh - yh * jnp.mean(dyh * yh, axis=-1, keepdims=True))

    vec = _full((1, 1024))
    return pl.pallas_call(
        body, name="f_ffn", grid=(s_len // tm,),
        in_specs=[_rows(tm, 1024), _rows(tm, 1024), vec, vec, _any(), _any(), _any()],
        out_specs=[_rows(tm, 1024), _rows(tm, D_FF), _rows(tm, D_FF), _rows(tm, D_FF),
                   _rows(tm, 1024), _full((1, 128)), vec],
        out_shape=[_sds((s_len, 1024), BF16)] + [_sds((s_len, D_FF), BF16)] * 3
                  + [_sds((s_len, 1024), F32), _sds((1, 128), F32), _sds((1, 1024), F32)],
        scratch_shapes=[pltpu.VMEM((D_FF, 1024), BF16)] * 3 + [pltpu.SemaphoreType.DMA((3,))],
        compiler_params=_cp("arbitrary"))(x2, tgt, g_ffn, g_final, wg, wu, wd)


def _b_ffn(dx3, x2, gact, uact, g_ffn, wg, wu, wd, tm):
    s_len = x2.shape[0]

    def body(dx3_ref, x2_ref, g_ref, u_ref, gf_ref, wg_hbm, wu_hbm, wd_hbm,
             dg_ref, du_ref, dx2_ref, dgf_ref, wg_ref, wu_ref, wd_ref, w_sems):
        def step(first):
            if first:
                dgf_ref[...] = jnp.zeros_like(dgf_ref)
            ready = _ffn_weights(first, [(wd_hbm, wd_ref), (wg_hbm, wg_ref), (wu_hbm, wu_ref)], w_sems)
            dx3v = dx3_ref[...]
            dx3b = dx3v.astype(BF16)
            dhf = jnp.zeros(dx3v.shape, F32)
            for c, (c0, c1) in enumerate(FF_CHUNKS):
                ready(c, 0)
                da = _dot_nt(dx3b, wd_ref[c0:c1, :])
                gv = g_ref[:, c0:c1].astype(F32)
                uv = u_ref[:, c0:c1].astype(F32)
                sg = jax.nn.sigmoid(gv)
                dub = (da * gv * sg).astype(BF16)
                dgb = (da * uv * (sg * (1.0 + gv * (1.0 - sg)))).astype(BF16)
                du_ref[:, c0:c1] = dub
                dg_ref[:, c0:c1] = dgb
                ready(c, 1)
                ready(c, 2)
                dhf = dhf + _dot(dgb, wg_ref[c0:c1, :]) + _dot(dub, wu_ref[c0:c1, :])
            dx, dgf = _rms_bwd(dhf, x2_ref[...], gf_ref[...])
            dx2_ref[...] = dx3v + dx
            dgf_ref[...] += dgf

        _first_step_and_rest(step)

    vec = _full((1, 1024))
    return pl.pallas_call(
        body, name="b_ffn", grid=(s_len // tm,),
        in_specs=[_rows(tm, 1024), _rows(tm, 1024), _rows(tm, D_FF), _rows(tm, D_FF), vec,
                  _any(), _any(), _any()],
        out_specs=[_rows(tm, D_FF), _rows(tm, D_FF), _rows(tm, 1024), vec],
        out_shape=[_sds((s_len, D_FF), BF16)] * 2 + [_sds((s_len, 1024), F32), _sds((1, 1024), F32)],
        scratch_shapes=[pltpu.VMEM((D_FF, 1024), BF16)] * 3 + [pltpu.SemaphoreType.DMA((3 * len(FF_CHUNKS),))],
        compiler_params=_cp("arbitrary"))(dx3, x2, gact, uact, g_ffn, wg, wu, wd)


def _b_mid(dx2, qx, x1, att, rec, kx, vx, wo, wq, w_out, g_cross, g_oa, g_ol, tm, comm=None):
    s_len = x1.shape[0]
    m_len = kx.shape[0]

    def body(dx2_ref, qx_ref, x1_ref, att_ref, rec_ref, kx_ref, vx_ref, wo_ref, wq_ref, wout_ref,
             gc_ref, goa_ref, gol_ref,
             dqx_ref, dx1_ref, datt_ref, drec_ref, dkx_ref, dvx_ref, dgc_ref, dgoa_ref, dgol_ref):
        @pl.when(pl.program_id(0) == 0)
        def _():
            for r in (dkx_ref, dvx_ref, dgc_ref, dgoa_ref, dgol_ref):
                r[...] = jnp.zeros_like(r)

        dx2v = dx2_ref[...]
        dox = _dot_nt(dx2v.astype(BF16), wo_ref[...])
        for h in range(X_HEADS):
            sl = slice(h * X_HEAD_DIM, (h + 1) * X_HEAD_DIM)
            q = qx_ref[:, sl]
            p, l = _xattn_probs(q, kx_ref[:, sl])
            pn = p * (1.0 / l)
            dob = dox[:, sl].astype(BF16)
            dp = _dot_nt(dob, vx_ref[:, sl])
            dvx_ref[:, sl] += _dot_tn(pn.astype(BF16), dob)
            ds = pn * (dp - jnp.sum(dp * pn, axis=-1, keepdims=True))
            dsb = (ds * X_SCALE).astype(BF16)
            dqx_ref[:, sl] = _dot(dsb, kx_ref[:, sl]).astype(BF16)
            dkx_ref[:, sl] += _dot_tn(dsb, q)
        dhc = _dot_nt(dqx_ref[...], wq_ref[...])
        dx, dgc = _rms_bwd(dhc, x1_ref[...], gc_ref[...])
        dx1 = dx2v + dx
        dx1_ref[...] = dx1
        dgc_ref[...] += dgc
        dmg = _dot_nt(dx1.astype(BF16), wout_ref[...])
        da, dgoa = _rms_bwd(dmg[:, 0:D_ATT], att_ref[...], goa_ref[...])
        datt_ref[...] = da
        dgoa_ref[...] += dgoa
        dr, dgol = _rms_bwd(dmg[:, D_ATT:1024], rec_ref[...], gol_ref[...])
        drec_ref[...] = dr
        dgol_ref[...] += dgol

    sq = _full((1024, 1024))
    mk = _full((m_len, 1024))
    return _call(
        body, "b_mid", (s_len // tm,),
        [_rows(tm, 1024), _rows(tm, 1024), _rows(tm, 1024), _rows(tm, 512), _rows(tm, 512), mk, mk,
         sq, sq, sq, _full((1, 1024)), _full((1, 512)), _full((1, 512))],
        [_rows(tm, 1024), _rows(tm, 1024), _rows(tm, 512), _rows(tm, 512), mk, mk,
         _full((1, 1024)), _full((1, 512)), _full((1, 512))],
        [_sds((s_len, 1024), BF16), _sds((s_len, 1024), F32), _sds((s_len, 512), F32),
         _sds((s_len, 512), F32), _sds((m_len, 1024), F32), _sds((m_len, 1024), F32),
         _sds((1, 1024), F32), _sds((1, 512), F32), _sds((1, 512), F32)],
        [], (dx2, qx, x1, att, rec, kx, vx, wo, wq, w_out, g_cross, g_oa, g_ol), "arbitrary", comm)


def _b_mem(dkx, dvx, mem, mn, g_mem, wk, wv):
    def body(dkx_ref, dvx_ref, mem_ref, mn_ref, g_ref, wk_ref, wv_ref, dwk_ref, dwv_ref, dgm_ref,
             dwkb_ref, dwvb_ref):
        dkb = dkx_ref[...].astype(BF16)
        dvb = dvx_ref[...].astype(BF16)
        dwk = _dot_tn(mn_ref[...], dkb)
        dwv = _dot_tn(mn_ref[...], dvb)
        dwk_ref[...] = dwk
        dwv_ref[...] = dwv
        dwkb_ref[...] = dwk.astype(BF16)
        dwvb_ref[...] = dwv.astype(BF16)
        dmn = _dot_nt(dkb, wk_ref[...]) + _dot_nt(dvb, wv_ref[...])
        mv = mem_ref[...]
        dgm_ref[...] = jnp.sum(dmn * (mv * _rinv(mv)), axis=0, keepdims=True)

    return pl.pallas_call(
        body, name="b_mem",
        out_shape=[_sds((1024, 1024), F32), _sds((1024, 1024), F32), _sds((1, 1024), F32),
                   _sds((1024, 1024), BF16), _sds((1024, 1024), BF16)],
        compiler_params=_cp())(dkx, dvx, mem, mn, g_mem, wk, wv)


def _b_lru(drec, hs, u, xg, conv_w, wrg, brg, wig, big, lam, tl, comm=None):
    s_len = xg.shape[0]
    nt = s_len // tl

    def body(drec_ref, hs_ref, hsp_ref, u_ref, xg_ref, cw_ref, wrg_ref, brg_ref, wig_ref, big_ref, l_ref,
             dxg_ref, dwrg_ref, dwig_ref, dbrg_ref, dbig_ref, dlam_ref, dcw_ref, dcb_ref,
             hbuf, abuf, dubuf, c_sc, d_sc, lam_sc, lcar, wacc_r, wacc_i):
        i = pl.program_id(0)
        tt = nt - 1 - i

        @pl.when(i == 0)
        def _():
            for r in (wacc_r, wacc_i, dbrg_ref, dbig_ref, dlam_ref, dcw_ref, dcb_ref):
                r[...] = jnp.zeros_like(r)
            abuf[tl:tl + 8, :] = jnp.zeros((8, D_LRU), F32)
            dubuf[tl:tl + 8, :] = jnp.zeros((8, D_LRU), F32)
            lcar[...] = jnp.zeros((8, D_LRU), F32)

        xu0 = xg_ref[:, 0:D_LRU]
        hsv = hs_ref[...]
        uv = u_ref[...]
        hbuf[8:8 + tl, :] = hsv
        hbuf[0:8, :] = jnp.where(tt > 0, hsp_ref[...], 0.0)
        hshift = hbuf[pl.ds(7, tl), :]
        wrg_v = wrg_ref[...]
        wig_v = wig_ref[...]
        lamv = l_ref[...]
        ub, r, ig, sp, a, mult = _lru_gates(uv, wrg_v, brg_ref[...], wig_v, big_ref[...], lamv)
        abuf[0:tl, :] = a
        c_sc[...] = abuf[pl.ds(1, tl), :]
        gel, dgel = _gelu_and_grad(xg_ref[:, D_LRU:2 * D_LRU])
        drv = drec_ref[...]
        d_sc[...] = drv * gel
        dxg_ref[:, D_LRU:2 * D_LRU] = (drv * hsv * dgel).astype(BF16)

        def grp(k, lnext):
            off = pl.multiple_of((tl // 8 - 1 - k) * 8, 8)
            l8 = _rscan8(c_sc[pl.ds(off, 8), :], d_sc[pl.ds(off, 8), :], lnext)
            lam_sc[pl.ds(off, 8), :] = l8
            return l8[0:1, :]

        lcar[0:1, :] = lax.fori_loop(0, tl // 8, grp, lcar[0:1, :])
        abuf[tl:tl + 8, :] = a[0:8, :]
        db = lam_sc[...]
        da = db * hshift
        dmult = db * (ig * uv)
        dig = db * mult * uv
        du = db * mult * ig
        dla = da * a - dmult * (a * a) / mult
        dlam_ref[...] += jnp.sum(dla * (-LRU_C) * r, axis=0, keepdims=True)
        dzr = dla * (-LRU_C * sp) * r * (1.0 - r)
        dzi = dig * ig * (1.0 - ig)
        dzrb = dzr.astype(BF16)
        dzib = dzi.astype(BF16)
        du = du + _dot_nt(dzrb, wrg_v) + _dot_nt(dzib, wig_v)
        wacc_r[...] += _dot_tn(ub, dzrb)
        wacc_i[...] += _dot_tn(ub, dzib)
        dbrg_ref[...] += jnp.sum(dzr, axis=0, keepdims=True)
        dbig_ref[...] += jnp.sum(dzi, axis=0, keepdims=True)
        dcb_ref[...] += jnp.sum(du, axis=0, keepdims=True)
        dubuf[0:tl, :] = du
        dxu0 = jnp.zeros((tl, D_LRU), F32)
        for j in range(4):
            dsh = dubuf[pl.ds(3 - j, tl), :]
            dxu0 = dxu0 + cw_ref[j:j + 1, :] * dsh
            dcw_ref[j:j + 1, :] += jnp.sum(xu0 * dsh, axis=0, keepdims=True)
        dubuf[tl:tl + 8, :] = du[0:8, :]
        dxg_ref[:, 0:D_LRU] = dxu0.astype(BF16)

        @pl.when(i == nt - 1)
        def _():
            dlam_ref[...] = dlam_ref[...] * (-jax.nn.sigmoid(-lamv))
            for n in range(LRU_BLOCKS):
                blk = slice(n * LRU_BLOCK, (n + 1) * LRU_BLOCK)
                dwrg_ref[n] = wacc_r[blk, blk]
                dwig_ref[n] = wacc_i[blk, blk]

    def rev(n):
        return pl.BlockSpec((tl, n), lambda i: (nt - 1 - i, 0))

    prev8 = pl.BlockSpec((8, D_LRU), lambda i: (jnp.maximum((nt - 1 - i) * (tl // 8) - 1, 0), 0))
    vec = _full((1, D_LRU))
    sq = _full((D_LRU, D_LRU))
    blocks_shape = (LRU_BLOCKS, LRU_BLOCK, LRU_BLOCK)
    blocks = _full(blocks_shape)
    return _call(
        body, "b_lru", (nt,),
        [rev(D_LRU), rev(D_LRU), prev8, rev(D_LRU), rev(1024), _full((4, D_LRU)), sq, vec, sq, vec, vec],
        [rev(1024), blocks, blocks, vec, vec, vec, _full((4, D_LRU)), vec],
        [_sds((s_len, 1024), BF16), _sds(blocks_shape, F32), _sds(blocks_shape, F32),
         _sds((1, D_LRU), F32), _sds((1, D_LRU), F32), _sds((1, D_LRU), F32),
         _sds((4, D_LRU), F32), _sds((1, D_LRU), F32)],
        [pltpu.VMEM((tl + 8, D_LRU), F32)] * 3 + [pltpu.VMEM((tl, D_LRU), F32)] * 3
        + [pltpu.VMEM((8, D_LRU), F32)] + [pltpu.VMEM((D_LRU, D_LRU), F32)] * 2,
        (drec, hs, hs, u, xg, conv_w, wrg, brg, wig, big, lam), "arbitrary", comm)


def _b_attn(qkv_pad, att, datt, frow, comm=None):
    s_len = datt.shape[0]
    nb = s_len // QB
    n_pair = ATT_HEADS // 2
    pair_w = 2 * HEAD_DIM

    def body(q_ref, k0, k1, k2, v0, v1, v2, o_ref, do_ref, frow_ref, dq_ref, dkv_ref, dfrow_ref,
             bias_sc, dt_sc, acc_sc):
        t = pl.program_id(0)

        @pl.when(t == 0)
        def _():
            _bias_table(frow_ref, bias_sc)
            dt_sc[...] = jnp.zeros_like(dt_sc)
            acc_sc[...] = jnp.zeros_like(acc_sc)

        @pl.when(t < nb)
        def _():
            var = jnp.minimum(t, N_BIAS - 1)
            even = _even_lanes()
            for hp in range(n_pair):
                cs = slice(hp * pair_w, (hp + 1) * pair_w)
                qt = q_ref[:, cs]
                kts = [k0[:, cs], k1[:, cs], k2[:, cs]]
                vts = [v0[:, cs], v1[:, cs], v2[:, cs]]
                kcat = jnp.concatenate(kts, axis=0)
                dot = do_ref[:, cs]
                dd = dot * o_ref[:, cs]
                dos_pair, dsbs, pbs, dqs = None, [], [], []
                for e in range(2):
                    keep = even if e == 0 else jnp.logical_not(even)
                    qm = jnp.where(keep, qt, 0)
                    p = _att_probs(qm, kts, bias_sc[var, 2 * hp + e])
                    inv = 1.0 / jnp.sum(p, axis=-1, keepdims=True)
                    dos = jnp.where(keep, dot * inv, 0.0)
                    delta = jnp.sum(jnp.where(keep, dd, 0.0), axis=-1, keepdims=True) * inv
                    dp = jnp.concatenate([_dot_nt(dos.astype(BF16), v) for v in vts], axis=1)
                    ds = p * (dp - delta)
                    dt_sc[2 * hp + e] += ds
                    dsb = ds.astype(BF16)
                    dq = _dot(dsb, kcat)
                    dqs.append(dq)
                    dsbs.append(dsb)
                    pbs.append(p.astype(BF16))
                    dos_pair = dos if e == 0 else dos_pair + dos
                dq_ref[:, cs] = (jnp.where(even, dqs[0], dqs[1]) * ATT_SCALE).astype(BF16)
                qtt = qt.astype(F32).T.astype(BF16)
                dost = dos_pair.T.astype(BF16)
                for j in range(3):
                    slot = (t + 1 + j) % 3
                    js = slice(j * QB, (j + 1) * QB)
                    for e in range(2):
                        hr = slice(e * HEAD_DIM, (e + 1) * HEAD_DIM)
                        acc_sc[slot, hp, hr, :] += _dot(qtt[hr], dsbs[e][:, js])
                        acc_sc[slot, n_pair + hp, hr, :] += _dot(dost[hr], pbs[e][:, js])

        done = (t + 1) % 3

        @pl.when(t >= 2)
        def _():
            for i in range(2 * n_pair):
                dkv_ref[:, i * pair_w:(i + 1) * pair_w] = acc_sc[done, i].T.astype(BF16)

        acc_sc[done] = jnp.zeros((2 * n_pair, pair_w, QB), F32)

        @pl.when(t == nb + 1)
        def _():
            row = lax.broadcasted_iota(jnp.int32, (8, ROLL_W), 0)
            pad = jnp.zeros((8, ROLL_W - KB), F32)
            for h in range(ATT_HEADS):
                acc8 = jnp.concatenate([dt_sc[h, 0:8, :], pad], axis=1)
                for a1 in range(1, QB // 8):
                    blk = jnp.concatenate([dt_sc[h, 8 * a1:8 * a1 + 8, :], pad], axis=1)
                    acc8 = acc8 + pltpu.roll(blk, ROLL_W - 8 * a1, 1)
                for k in range(3):
                    acc8 = jnp.where(((row >> k) & 1) == 1, pltpu.roll(acc8, ROLL_W - (1 << k), 1), acc8)
                dfrow_ref[h:h + 1, :] = jnp.sum(acc8, axis=0, keepdims=True)

    clamp = lambda t: jnp.minimum(t, nb - 1)
    qrows = pl.BlockSpec((QB, D_ATT), lambda t: (clamp(t), 0))
    return _call(
        body, "b_attn", (nb + 2,),
        _att_in_specs(clamp) + [qrows, qrows, _full((ATT_HEADS, ROLL_W))],
        [qrows, pl.BlockSpec((QB, 2 * D_ATT), lambda t: (jnp.maximum(t - 2, 0), 0)),
         _full((ATT_HEADS, ROLL_W))],
        [_sds((s_len, D_ATT), BF16), _sds((s_len, 2 * D_ATT), BF16), _sds((ATT_HEADS, ROLL_W), F32)],
        [pltpu.VMEM((N_BIAS, ATT_HEADS, QB, KB), F32), pltpu.VMEM((ATT_HEADS, QB, KB), F32),
         pltpu.VMEM((3, 2 * n_pair, pair_w, QB), F32)],
        (*([qkv_pad] * 7), att, datt, frow), "arbitrary", comm)


def _b_win(dq, dkv, dxg, h, ts):
    s_len = h.shape[0]
    steps = s_len // ts

    def body(dq_ref, dkv_ref, dxg_ref, h_ref, dw_hbm, dwb_hbm, acc, accb, sems):
        @pl.when(pl.program_id(0) == 0)
        def _():
            acc[...] = jnp.zeros_like(acc)

        @pl.when(pl.program_id(0) < steps - 1)
        def _():
            dproj = jnp.concatenate([dq_ref[...], dkv_ref[...], dxg_ref[...]], axis=1)
            acc[...] += _dot_tn(h_ref[...], dproj)

        @pl.when(pl.program_id(0) == steps - 1)
        def _():
            dproj = jnp.concatenate([dq_ref[...], dkv_ref[...], dxg_ref[...]], axis=1)
            copies = []
            for s in range(N_SHARD):
                cols = slice(s * IN_SH, (s + 1) * IN_SH)
                total = acc[:, cols] + _dot_tn(h_ref[...], dproj[:, cols])
                acc[:, cols] = total
                accb[:, cols] = total.astype(BF16)
                copies += _start_copies([(acc.at[:, cols], dw_hbm.at[s]), (accb.at[:, cols], dwb_hbm.at[s])],
                                        sems, 2 * s)
            for cp in copies:
                cp.wait()

    shape = (N_SHARD, 1024, IN_SH)
    return pl.pallas_call(
        body, name="b_win", grid=(steps,),
        in_specs=[_rows(ts, 512), _rows(ts, 1024), _rows(ts, 1024), _rows(ts, 1024)],
        out_specs=[_any()] * 2, out_shape=[_sds(shape, F32), _sds(shape, BF16)],
        scratch_shapes=[pltpu.VMEM((1024, D_IN), F32), pltpu.VMEM((1024, D_IN), BF16),
                        pltpu.SemaphoreType.DMA((2 * N_SHARD,))],
        compiler_params=_cp("arbitrary"))(dq, dkv, dxg, h)


def _b_inproj(dq, dkv, dxg, x, dx1, g_mix, w_in_g, tm, comm=None):
    s_len = x.shape[0]

    def body(dq_ref, dkv_ref, dxg_ref, x_ref, dx1_ref, g_ref, w_hbm, gx_ref, dgm_ref, w_ref, w_sems):
        _load_w_in_once(w_hbm, w_ref, w_sems)

        @pl.when(pl.program_id(0) == 0)
        def _():
            dgm_ref[...] = jnp.zeros_like(dgm_ref)

        dproj = jnp.concatenate([dq_ref[...], dkv_ref[...], dxg_ref[...]], axis=1)
        dh = _dot_nt(dproj, w_ref[...])
        dx, dgm = _rms_bwd(dh, x_ref[...], g_ref[...])
        gx_ref[...] = dx1_ref[...] + dx
        dgm_ref[...] += dgm

    return _call(
        body, "b_inproj", (s_len // tm,),
        [_rows(tm, 512), _rows(tm, 1024), _rows(tm, 1024), _rows(tm, 1024), _rows(tm, 1024),
         _full((1, 1024)), _any()],
        [_rows(tm, 1024), _full((1, 1024))],
        [_sds((s_len, 1024), F32), _sds((1, 1024), F32)],
        [pltpu.VMEM((1024, D_IN), BF16), pltpu.SemaphoreType.DMA((N_SHARD,))],
        (dq, dkv, dxg, x, dx1, g_mix, w_in_g), "arbitrary", comm)


MXU_DIM_V7X = 256
FLUSH_GROUPS = 4


def _mm_tn(xa, ya, name, ts):
    s_len, k = xa.shape
    n = ya.shape[1]

    steps = s_len // ts
    tiles = k // MXU_DIM_V7X
    edges = [MXU_DIM_V7X * ((tiles * g) // FLUSH_GROUPS) for g in range(FLUSH_GROUPS + 1)]

    def body(x_ref, y_ref, o_hbm, ob_hbm, acc, accb, sems):
        @pl.when(pl.program_id(0) == 0)
        def _():
            acc[...] = jnp.zeros_like(acc)

        @pl.when(pl.program_id(0) < steps - 1)
        def _():
            acc[...] += _dot_tn(x_ref[...].astype(BF16), y_ref[...].astype(BF16))

        @pl.when(pl.program_id(0) == steps - 1)
        def _():
            yb = y_ref[...].astype(BF16)
            copies = []
            for g in range(FLUSH_GROUPS):
                rows = slice(edges[g], edges[g + 1])
                total = acc[rows, :] + _dot_tn(x_ref[:, rows].astype(BF16), yb)
                acc[rows, :] = total
                accb[rows, :] = total.astype(BF16)
                copies += _start_copies([(acc.at[rows, :], o_hbm.at[rows, :]), (accb.at[rows, :], ob_hbm.at[rows, :])],
                                        sems, 2 * g)
            for cp in copies:
                cp.wait()

    return pl.pallas_call(
        body, name=name, grid=(steps,), in_specs=[_rows(ts, k), _rows(ts, n)],
        out_specs=[_any()] * 2, out_shape=[_sds((k, n), F32), _sds((k, n), BF16)],
        scratch_shapes=[pltpu.VMEM((k, n), F32), pltpu.VMEM((k, n), BF16),
                        pltpu.SemaphoreType.DMA((2 * FLUSH_GROUPS,))],
        compiler_params=_cp("arbitrary"))(xa, ya)


PAD_KEYS = LEFT_CHUNKS * CHUNK
F_HI = PAD_KEYS - MAX_REL + 1
F_LO = PAD_KEYS + MAX_REL


def _frow_from_rel_bias(rb):
    last = rb[:, 2 * MAX_REL:2 * MAX_REL + 1]
    hi = jnp.broadcast_to(last, (ATT_HEADS, F_HI))
    mid = rb[:, 1:2 * MAX_REL][:, ::-1]
    lo = jnp.broadcast_to(rb[:, 0:1], (ATT_HEADS, KB - F_LO))
    wrap = jnp.broadcast_to(last, (ATT_HEADS, ROLL_W - KB))
    return jnp.concatenate([hi, mid, lo, wrap], axis=1)


def _rel_bias_grad_from_dfrow(df):
    g_last = jnp.sum(df[:, 0:F_HI], axis=1, keepdims=True) + jnp.sum(df[:, KB:ROLL_W], axis=1, keepdims=True)
    mid = df[:, F_HI:F_LO][:, ::-1]
    g_first = jnp.sum(df[:, F_LO:KB], axis=1, keepdims=True)
    return jnp.concatenate([g_first, mid, g_last], axis=1)


def _block_diag(w):
    eye = jnp.eye(8, dtype=w.dtype)
    return (w[:, :, None, :] * eye[:, None, :, None]).reshape(D_LRU, D_LRU)


MID = ['w_out', 'wq_c', 'wk_c', 'wv_c', 'wo_c']
TRANSPOSED = ['w_gate', 'w_up']
AG_IN_INPROJ = ['w_out', 'wq_c', 'wk_c']
AG_IN_ATTN = ['wv_c', 'wo_c', 'w_gate']
AG_IN_LRU = ['w_up']
AG_IN_MID = ['w_down']
RS_IN_MID = ['w_gate', 'w_up']
RS_IN_LRU = ['w_down']
RS_IN_ATTN = MID


def _local_step(x, mem, tgt, p, gw, shards=None, chip=None):
    s_len = x.shape[0]
    tm = min(256, s_len)
    tmb = min(512, s_len)
    tl = min(512, s_len)
    frow = _frow_from_rel_bias(p['rel_bias'])
    wrg = _block_diag(p['w_rg']).astype(BF16)
    wig = _block_diag(p['w_ig']).astype(BF16)
    gw = dict(gw)

    big, bigb, recv, part, sib = {}, {}, {}, {}, {}

    def ag(names):
        return [] if shards is None else [("ag", [shards[n] for n in names])]

    def rs(names):
        return [] if shards is None else [("rs", [bigb[n] for n in names])]

    def swap(names):
        return [] if shards is None else [("swap", [part[n] for n in names])]

    def reduce_own(names):
        if shards is not None:
            sums = _sum_parts([big[n] for n in names], [recv[n] for n in names], chip, "sum_" + names[0])
            part.update(zip(names, sums))

    h, qkv_pad, xg, *got = _f_inproj(x, p['g_mix'], gw['w_in'], tmb, ag(AG_IN_INPROJ))
    gw.update(zip(AG_IN_INPROJ, got))
    att, *got = _f_attn(qkv_pad, frow, ag(AG_IN_ATTN))
    gw.update(zip(AG_IN_ATTN, got))
    rec, u, hs, *got = _f_lru(xg, p['conv_w'], p['conv_b'], wrg, p['b_rg'], wig, p['b_ig'], p['lru_L'], tl,
                              ag(AG_IN_LRU))
    gw.update(zip(AG_IN_LRU, got))
    w_out = gw['w_out'].reshape(1024, 1024)
    wq = gw['wq_c'].reshape(1024, 1024)
    wk = gw['wk_c'].reshape(1024, 1024)
    wv = gw['wv_c'].reshape(1024, 1024)
    wo = gw['wo_c'].reshape(1024, 1024)
    mn, kx, vx = _f_mem(mem, p['g_mem'], wk, wv)
    mg, x1, hc, qx, ox, x2, *got = _f_mid(x, att, rec, p['g_out_attn'], p['g_out_lru'], w_out, p['g_cross'],
                                          wq, kx, vx, wo, tmb, ag(AG_IN_MID))
    gw.update(zip(AG_IN_MID, got))
    ffn_w = [gw[n].reshape(D_FF, 1024) for n in ('w_gate', 'w_up', 'w_down')]
    hf, gact, uact, aact, dx3, loss, dg_final = _f_ffn(x2, tgt, p['g_ffn'], p['g_final'], *ffn_w, tmb)

    ts = min(1024, s_len)
    dgact, duact, dx2, dg_ffn = _b_ffn(dx3, x2, gact, uact, p['g_ffn'], *ffn_w, tm)
    big['w_gate'], bigb['w_gate'] = _mm_tn(dgact, hf, "dw_gate", ts)
    big['w_up'], bigb['w_up'] = _mm_tn(duact, hf, "dw_up", ts)
    big['w_down'], bigb['w_down'] = _mm_tn(aact, dx3, "dw_down", ts)
    for n in ('w_gate', 'w_up', 'w_down'):
        big[n] = big[n].reshape(N_SHARD, FF_SH, 1024)
        bigb[n] = bigb[n].reshape(N_SHARD, FF_SH, 1024)

    dqx, dx1, datt, drec, dkx, dvx, dg_cross, dg_oa, dg_ol, *got = _b_mid(
        dx2, qx, x1, att, rec, kx, vx, wo, wq, w_out, p['g_cross'], p['g_out_attn'], p['g_out_lru'], tmb,
        rs(RS_IN_MID))
    recv.update(zip(RS_IN_MID, got))
    reduce_own(RS_IN_MID)
    dwk, dwv, dg_mem, dwkb, dwvb = _b_mem(dkx, dvx, mem, mn, p['g_mem'], wk, wv)
    big['wk_c'], bigb['wk_c'] = dwk, dwkb
    big['wv_c'], bigb['wv_c'] = dwv, dwvb
    big['w_out'], bigb['w_out'] = _mm_tn(mg, dx1, "dw_out", ts)
    big['wq_c'], bigb['wq_c'] = _mm_tn(hc, dqx, "dw_q", ts)
    big['wo_c'], bigb['wo_c'] = _mm_tn(ox, dx2, "dw_o", ts)
    for n in MID:
        big[n] = big[n].reshape(N_SHARD, 256, 1024)
        bigb[n] = bigb[n].reshape(N_SHARD, 256, 1024)

    dxg, dwrg, dwig, dbrg, dbig, dlam, dcw, dcb, *got = _b_lru(
        drec, hs, u, xg, p['conv_w'], wrg, p['b_rg'], wig, p['b_ig'], p['lru_L'], tl,
        rs(RS_IN_LRU) + swap(RS_IN_MID))
    recv.update(zip(RS_IN_LRU, got))
    sib.update(zip(RS_IN_MID, got[len(RS_IN_LRU):]))
    reduce_own(RS_IN_LRU)
    small = {
        'conv_w': dcw, 'conv_b': dcb, 'w_rg': dwrg, 'b_rg': dbrg, 'w_ig': dwig, 'b_ig': dbig, 'lru_L': dlam,
        'g_out_attn': dg_oa, 'g_out_lru': dg_ol, 'g_cross': dg_cross, 'g_mem': dg_mem, 'g_ffn': dg_ffn,
        'g_final': dg_final,
    }
    names = [n for n in SMALL if n in small]
    gather = [] if shards is None else [
        ("ag8", [_pack_small(names, [small[n] for n in names], loss, PACK_ROWS, "pack_small")])]
    dq, dkv, dfrow, *got = _b_attn(qkv_pad, att, datt, frow, rs(RS_IN_ATTN) + swap(RS_IN_LRU) + gather)
    recv.update(zip(RS_IN_ATTN, got))
    sib.update(zip(RS_IN_LRU, got[len(RS_IN_ATTN):]))
    packs = got[-1] if gather else None
    reduce_own(RS_IN_ATTN)
    small['rel_bias'] = _rel_bias_grad_from_dfrow(dfrow)
    big['w_in'], bigb['w_in'] = _b_win(dq, dkv, dxg, h, ts)
    if shards is None:
        grad_x, small['g_mix'] = _b_inproj(dq, dkv, dxg, x, dx1, p['g_mix'], gw['w_in'], tmb)
    else:
        nsw = len(RS_IN_ATTN)

        def copies(refs, send_sems, recv_sems):
            return _tail_copies(refs[0], refs[1], refs[2:2 + nsw], refs[2 + nsw:2 + 2 * nsw], send_sems, recv_sems)

        slots = bigb['w_in']
        bufs = ([slots, lax.empty((3,) + slots.shape[1:], slots.dtype)] + [part[n] for n in RS_IN_ATTN]
                + [lax.empty(part[n].shape, F32) for n in RS_IN_ATTN])
        sems, bufs, token = _split_start("tail_exchange_start", bufs, 3 + nsw, copies)
        grad_x, small['g_mix'] = _b_inproj(dq, dkv, dxg, x, dx1, p['g_mix'] + token[0, 0], gw['w_in'], tmb)
        bufs = _split_wait("tail_exchange_wait", sems, bufs, 3 + nsw, copies, small['g_mix'])
        recv['w_in'] = bufs[1]
        sib.update(zip(RS_IN_ATTN, bufs[2 + nsw:]))
    reduce_own(['w_in'])
    return loss, grad_x, small, big, part, sib, packs


CAST_STEPS = 4


def _cast_shards(ws, name, comm=None):
    def body(*refs):
        n = len(refs) // 2
        for src, dst in zip(refs[:n], refs[n:]):
            dst[...] = src[...].astype(BF16)

    specs = [_rows(w.shape[0] // CAST_STEPS, w.shape[1]) for w in ws]
    return _call(body, name, (CAST_STEPS,), specs, specs, [_sds(w.shape, BF16) for w in ws], [], tuple(ws),
                 "arbitrary", comm)


def _sum_parts(own4s, recv3s, chip, name):
    n = len(own4s)
    _, r, c = own4s[0].shape
    steps = _ew_steps(r, n * c * (4 + 3 * 2 + 4))
    tr = r // steps

    def body(chip_ref, *refs):
        for own_ref, rc_ref, o_ref in zip(refs[:n], refs[n:2 * n], refs[2 * n:]):
            o_ref[...] = ((own_ref[0] + rc_ref[0].astype(F32)) + rc_ref[1].astype(F32)) + rc_ref[2].astype(F32)

    grid_spec = pltpu.PrefetchScalarGridSpec(
        num_scalar_prefetch=1, grid=(steps,),
        in_specs=[pl.BlockSpec((1, tr, c), lambda i, ch: (ch[0], i, 0))] * n
                 + [pl.BlockSpec((3, tr, c), lambda i, ch: (0, i, 0))] * n,
        out_specs=[pl.BlockSpec((tr, c), lambda i, ch: (i, 0))] * n)
    return pl.pallas_call(body, name=name, grid_spec=grid_spec, out_shape=[_sds((r, c), F32)] * n,
                          compiler_params=_cp("parallel"))(chip, *own4s, *recv3s)


def _adamw_math(w, g, m, v):
    m = ADAM_B1 * m + (1.0 - ADAM_B1) * g
    v = ADAM_B2 * v + (1.0 - ADAM_B2) * (g * g)
    m_hat = m / (1.0 - ADAM_B1 ** ADAM_STEP)
    v_hat = v / (1.0 - ADAM_B2 ** ADAM_STEP)
    delta = -ADAM_LR * (m_hat / (jnp.sqrt(v_hat) + ADAM_EPS) + ADAM_WD * w)
    return delta, m, v


def _final_adamw(pas, pbs, ws, ms, vs, name, after=None):
    n = len(ws)
    r, c = ws[0].shape
    steps = _ew_steps(r, n * c * 9 * 4)
    tr = r // steps

    def body(*refs):
        ins, outs = refs[:5 * n], refs[len(refs) - 4 * n:]
        for k in range(n):
            pa_ref, pb_ref, w_ref, m_ref, v_ref = (ins[j * n + k] for j in range(5))
            g = pa_ref[...] + pb_ref[...]
            outs[4 * k][...] = g
            outs[4 * k + 1][...], outs[4 * k + 2][...], outs[4 * k + 3][...] = _adamw_math(
                w_ref[...], g, m_ref[...], v_ref[...])

    order = [] if after is None else [after]
    res = pl.pallas_call(
        body, name=name, grid=(steps,), in_specs=[_rows(tr, c)] * (5 * n) + [_full(t.shape) for t in order],
        out_specs=[_rows(tr, c)] * (4 * n), out_shape=[_sds((r, c), F32)] * (4 * n),
        compiler_params=_cp("parallel"))(*pas, *pbs, *ws, *ms, *vs, *order)
    return [res[4 * k:4 * k + 4] for k in range(n)]


def _pack_put(ref, name, val_ref):
    r = _pack_rows()[name]
    shape = val_ref.shape
    if len(shape) == 3:
        for b in range(shape[0]):
            ref[r:r + shape[1], b * shape[2]:(b + 1) * shape[2]] = val_ref[b]
    elif shape[1] == 2 * PACK_W:
        ref[r:r + 1, :] = val_ref[:, 0:PACK_W]
        ref[r + 1:r + 2, :] = val_ref[:, PACK_W:2 * PACK_W]
    else:
        ref[r:r + shape[0], 0:shape[1]] = val_ref[...]


def _pack_get(ref, name, shape):
    r = _pack_rows()[name]
    if len(shape) == 3:
        return jnp.stack([ref[r:r + shape[1], b * shape[2]:(b + 1) * shape[2]] for b in range(shape[0])])
    if shape[1] == 2 * PACK_W:
        return jnp.concatenate([ref[r:r + 1, :], ref[r + 1:r + 2, :]], axis=1)
    return ref[r:r + shape[0], 0:shape[1]]


def _pack_small(names, g, loss, rows, name):
    n = len(g)
    extra = [] if loss is None else [loss]

    def body(*refs):
        pack = refs[-1]
        pack[...] = jnp.zeros_like(pack)
        for a, nm in enumerate(names):
            _pack_put(pack, nm, refs[a])
        if extra:
            _pack_put(pack, 'loss', refs[n])

    return pl.pallas_call(body, name=name, out_shape=_sds((rows, PACK_W), F32), compiler_params=_cp())(*g, *extra)


def _all_peers():
    x, y, c = _mesh_pos()
    peers = []
    for k in range(1, 8):
        px = 1 - x if k & 4 else x
        py = 1 - y if k & 2 else y
        pc = 1 - c if k & 1 else c
        peers.append(((px, py, pc), 4 * px + 2 * py + pc))
    return peers, 4 * x + 2 * y + c


def _ag8_copies(ins, outs, sems):
    send_sems, recv_sems, loc_sems = sems
    n = len(ins)
    peers, me = _all_peers()

    def remote(k, j, slot):
        return pltpu.make_async_remote_copy(
            src_ref=ins[k], dst_ref=outs[k].at[slot], send_sem=send_sems.at[k, j], recv_sem=recv_sems.at[k, j],
            device_id=peers[j][0], device_id_type=MESH_ID)

    def local(k):
        return pltpu.make_async_copy(ins[k], outs[k].at[me], loc_sems.at[k])

    def start():
        for k in range(n):
            local(k).start()
            for j in range(7):
                remote(k, j, me).start()

    def wait():
        for k in range(n):
            for j in range(7):
                remote(k, j, peers[j][1]).wait_recv()
        for k in range(n):
            for j in range(7):
                remote(k, j, me).wait_send()
            local(k).wait()

    return start, _no_forward, wait


def _adamw_small(packs, late_own, late_packs, g_shapes, loss_shape, w, m, v):
    n = len(w)

    def body(*refs):
        packs_ref, own_ref, late_ref = refs[0], refs[1], refs[2]
        w_refs, m_refs, v_refs = (refs[3 + i * n:3 + (i + 1) * n] for i in range(3))
        o0 = 3 * n + 3
        go, do, mo, vo = (refs[o0 + i * n:o0 + (i + 1) * n] for i in range(4))
        loss_out, tot_ref = refs[o0 + 4 * n], refs[o0 + 4 * n + 1]
        x, y, c = _mesh_pos()
        me = 4 * x + 2 * y + c
        tot = packs_ref[0]
        late = jnp.where(me == 0, own_ref[...], late_ref[0])
        for d in range(1, 8):
            tot = tot + packs_ref[d]
            late = late + jnp.where(me == d, own_ref[...], late_ref[d])
        tot_ref[...] = tot
        tot_ref[0:LATE_ROWS, :] += late
        loss_out[...] = _pack_get(tot_ref, 'loss', loss_shape)
        for a, name in enumerate(SMALL):
            if name == 'conv_w':
                r = _pack_rows()[name]
                ga = tot_ref[r:r + g_shapes[a][0], pl.ds(pl.multiple_of((2 * x + y) * 128, 128), 128)]
            else:
                ga = _pack_get(tot_ref, name, g_shapes[a])
            go[a][...] = ga
            do[a][...], mo[a][...], vo[a][...] = _adamw_math(w_refs[a][...], ga, m_refs[a][...], v_refs[a][...])

    out_shape = [_sds(a.shape, F32) for a in w] * 4 + [_sds(loss_shape, F32)]
    return pl.pallas_call(body, name="adamw_small", out_shape=out_shape,
                          scratch_shapes=[pltpu.VMEM((PACK_ROWS, PACK_W), F32)],
                          compiler_params=_cp())(packs, late_own, late_packs, *w, *m, *v)


PACK_W = 512
PACK_ROWS = 160
LATE = ['g_mix', 'rel_bias']
LATE_ROWS = 32


def _pack_rows():
    rows, r = {}, 0
    for name in ['g_mix', 'g_cross', 'g_mem', 'g_ffn', 'g_final']:
        rows[name] = r
        r += 2
    for name in ['conv_b', 'b_rg', 'b_ig', 'lru_L', 'g_out_attn', 'g_out_lru']:
        rows[name] = r
        r += 1
    rows['conv_w'] = r
    rows['loss'] = r + 4
    rows['rel_bias'] = 24
    rows['w_rg'] = 32
    rows['w_ig'] = 32 + LRU_BLOCK
    assert r + 5 <= 24 and rows['w_ig'] + LRU_BLOCK == PACK_ROWS
    assert rows['g_mix'] + 2 <= LATE_ROWS and rows['rel_bias'] + 8 <= LATE_ROWS
    return rows


INPUT_NAMES = (['x', 'mem'] + WEIGHTS + ['loss_target'] + ['m_' + n for n in WEIGHTS] + ['v_' + n for n in WEIGHTS])


def kernel(x, mem, g_mix, w_in, rel_bias, conv_w, conv_b, w_rg, b_rg, w_ig, b_ig, lru_L, g_out_attn, g_out_lru, w_out, g_cross, g_mem, wq_c, wk_c, wv_c, wo_c, g_ffn, w_gate, w_up, w_down, g_final, loss_target, m_g_mix, m_w_in, m_rel_bias, m_conv_w, m_conv_b, m_w_rg, m_b_rg, m_w_ig, m_b_ig, m_lru_L, m_g_out_attn, m_g_out_lru, m_w_out, m_g_cross, m_g_mem, m_wq_c, m_wk_c, m_wv_c, m_wo_c, m_g_ffn, m_w_gate, m_w_up, m_w_down, m_g_final, v_g_mix, v_w_in, v_rel_bias, v_conv_w, v_conv_b, v_w_rg, v_b_rg, v_w_ig, v_b_ig, v_lru_L, v_g_out_attn, v_g_out_lru, v_w_out, v_g_cross, v_g_mem, v_wq_c, v_wk_c, v_wv_c, v_wo_c, v_g_ffn, v_w_gate, v_w_up, v_w_down, v_g_final):
    a = dict(zip(INPUT_NAMES, (x, mem, g_mix, w_in, rel_bias, conv_w, conv_b, w_rg, b_rg, w_ig, b_ig, lru_L, g_out_attn, g_out_lru, w_out, g_cross, g_mem, wq_c, wk_c, wv_c, wo_c, g_ffn, w_gate, w_up, w_down, g_final, loss_target, m_g_mix, m_w_in, m_rel_bias, m_conv_w, m_conv_b, m_w_rg, m_b_rg, m_w_ig, m_b_ig, m_lru_L, m_g_out_attn, m_g_out_lru, m_w_out, m_g_cross, m_g_mem, m_wq_c, m_wk_c, m_wv_c, m_wo_c, m_g_ffn, m_w_gate, m_w_up, m_w_down, m_g_final, v_g_mix, v_w_in, v_rel_bias, v_conv_w, v_conv_b, v_w_rg, v_b_rg, v_w_ig, v_b_ig, v_lru_L, v_g_out_attn, v_g_out_lru, v_w_out, v_g_cross, v_g_mem, v_wq_c, v_wk_c, v_wv_c, v_wo_c, v_g_ffn, v_w_gate, v_w_up, v_w_down, v_g_final)))
    chip = 2 * lax.axis_index("x") + lax.axis_index("y")

    def shard(name):
        arr = a[name][0]
        base = name[2:] if name[:2] in ('m_', 'v_') else name
        return jnp.swapaxes(arr, 0, 1) if base in TRANSPOSED else arr

    shards = {'w_in': _cast_shards([shard('w_in')], "cast_w_in")[0]}
    rest = [n for n in BIG if n != 'w_in']
    *cast, w_in_g, conv_w_g = _cast_shards([shard(n) for n in rest], "cast_rest",
                                           [("ag", [shards['w_in']]), ("agf", [a['conv_w'][0]])])
    shards.update(zip(rest, cast))
    conv_w_full = conv_w_g.transpose(1, 0, 2).reshape(4, D_LRU)

    p = {n: a[n] for n in SMALL}
    p['rel_bias'] = a['rel_bias'][0]
    p['w_rg'] = a['w_rg'][0]
    p['w_ig'] = a['w_ig'][0]
    p['conv_w'] = conv_w_full
    p['g_final'] = a['g_final'][None, :]
    chip_arr = jnp.reshape(chip, (1,)).astype(jnp.int32)
    loss_part, grad_x, small, _, part, sib, packs = _local_step(
        a['x'][0], a['mem'][0], a['loss_target'][0], p, {'w_in': w_in_g}, shards, chip_arr)

    def late_copies(refs, send_sems, recv_sems):
        return _late_copies(refs[0], refs[1], refs[2], refs[3], send_sems, recv_sems)

    late_pack = _pack_small(LATE, [small[n] for n in LATE], None, LATE_ROWS, "pack_late")
    bufs = [part['w_in'], lax.empty(part['w_in'].shape, F32), late_pack, jnp.zeros((8, LATE_ROWS, PACK_W), F32)]
    sems, bufs, token = _split_start("late_exchange_start", bufs, 8, late_copies)
    out = {}

    def adamw(group, after=None):
        results = _final_adamw([part[n] for n in group], [sib[n] for n in group], [shard(n) for n in group],
                               [shard('m_' + n) for n in group], [shard('v_' + n) for n in group],
                               "adamw_" + group[0], after)
        for n, res in zip(group, results):
            out[n] = [jnp.swapaxes(r, 0, 1) for r in res] if n in TRANSPOSED else res
        return results[-1][0]

    adamw(MID, token)
    done = adamw(['w_gate', 'w_up', 'w_down'], token)
    _, sib['w_in'], late_pack, late_packs = _split_wait("late_exchange_wait", sems, bufs, 8, late_copies, done)
    adamw(['w_in'])

    def natural(arr):
        return arr[0] if arr.ndim >= 3 else (arr[None, :] if arr.ndim == 1 else arr)

    small_out = _adamw_small(packs, late_pack, late_packs, [small[n].shape for n in SMALL],
                             loss_part.shape, *[[natural(a[pre + n]) for n in SMALL] for pre in ('', 'm_', 'v_')])
    ns = len(SMALL)
    loss = small_out[4 * ns][0, 0]

    def leaf(i, n):
        if n in BIG:
            return out[n][i][None]
        return small_out[i * ns + SMALL.index(n)].reshape(a[n].shape)

    return (loss, grad_x[None], *[leaf(i, n) for i in range(4) for n in WEIGHTS])
```

```python
import math

import jax
import jax.numpy as jnp
from jax import lax
from jax.experimental import pallas as pl
from jax.experimental.pallas import tpu as pltpu

F32 = jnp.float32
BF16 = jnp.bfloat16

D_MODEL = 1024
D_ATT = 512
D_LRU = 512
HEAD_DIM = 64
ATT_HEADS = 8
CHUNK = 64
LEFT_CHUNKS = 8
MAX_REL = 128
X_HEADS = 4
X_HEAD_DIM = 256
N_SHARD = 4
IN_SH = 640
D_IN = N_SHARD * IN_SH
FF_SH = 704
D_FF = N_SHARD * FF_SH
EPS = 1e-6
LRU_C = 8.0
LRU_BLOCKS = 8
LRU_BLOCK = 64
QB = 256
KB = 768
ROLL_W = 1024
NEG = -1e30
ATT_SCALE = HEAD_DIM ** -0.5
X_SCALE = X_HEAD_DIM ** -0.5

ADAM_LR = 0.001
ADAM_B1 = 0.9
ADAM_B2 = 0.999
ADAM_EPS = 1e-08
ADAM_WD = 0.01
ADAM_STEP = 10

VMEM_LIMIT_V7X = 56 * 1024 * 1024
BF16_ROWS = 16


EW_VMEM_BUDGET = 40 * 1024 * 1024


def _ew_steps(rows, bytes_per_row):
    return min(s for s in (2, 4, 8, 16) if rows % (s * BF16_ROWS) == 0
               and 2 * (rows // s) * bytes_per_row <= EW_VMEM_BUDGET)
MESH_ID = pl.DeviceIdType.MESH

WEIGHTS = ['g_mix', 'w_in', 'rel_bias', 'conv_w', 'conv_b', 'w_rg', 'b_rg', 'w_ig', 'b_ig', 'lru_L',
           'g_out_attn', 'g_out_lru', 'w_out', 'g_cross', 'g_mem', 'wq_c', 'wk_c', 'wv_c', 'wo_c',
           'g_ffn', 'w_gate', 'w_up', 'w_down', 'g_final']
BIG = ['w_in', 'w_out', 'wq_c', 'wk_c', 'wv_c', 'wo_c', 'w_gate', 'w_up', 'w_down']
SMALL = [n for n in WEIGHTS if n not in BIG]


def _sds(shape, dtype):
    return jax.ShapeDtypeStruct(shape, dtype)


def _cp(*sem):
    return pltpu.CompilerParams(dimension_semantics=sem or None, vmem_limit_bytes=VMEM_LIMIT_V7X)


def _rows(tm, n):
    return pl.BlockSpec((tm, n), lambda i: (i, 0))


def _full(shape):
    nd = len(shape)
    return pl.BlockSpec(shape, lambda i: (0,) * nd)


def _dot(a, b):
    return jnp.dot(a, b, preferred_element_type=F32)


def _dot_nt(a, b):
    return lax.dot_general(a, b, (((1,), (1,)), ((), ())), preferred_element_type=F32)


def _dot_tn(a, b):
    return lax.dot_general(a, b, (((0,), (0,)), ((), ())), preferred_element_type=F32)


def _rinv(x):
    return lax.rsqrt(jnp.mean(x * x, axis=-1, keepdims=True) + EPS)


def _rms_bwd(dy, x, g):
    r = _rinv(x)
    yh = x * r
    dyh = dy * g
    dx = r * (dyh - yh * jnp.mean(dyh * yh, axis=-1, keepdims=True))
    return dx, jnp.sum(dy * yh, axis=0, keepdims=True)


def _gelu(x):
    c = math.sqrt(2.0 / math.pi)
    t = jnp.tanh(c * (x + 0.044715 * x * x * x))
    return 0.5 * x * (1.0 + t)


def _gelu_and_grad(x):
    c = math.sqrt(2.0 / math.pi)
    t = jnp.tanh(c * (x + 0.044715 * x * x * x))
    g = 0.5 * x * (1.0 + t)
    dg = 0.5 * (1.0 + t) + 0.5 * x * (1.0 - t * t) * c * (1.0 + 3.0 * 0.044715 * x * x)
    return g, dg


def _neg_expm1(z):
    series = -z * (1.0 + z * (0.5 + z * ((1.0 / 6.0) + z * (1.0 / 24.0))))
    return jnp.where(z > -0.03, series, 1.0 - jnp.exp(z))


def _lru_gates(u, wrg, brg, wig, big, lam):
    ub = u.astype(BF16)
    r = jax.nn.sigmoid(_dot(ub, wrg) + brg)
    ig = jax.nn.sigmoid(_dot(ub, wig) + big)
    sp = jnp.maximum(-lam, 0.0) + jnp.log1p(jnp.exp(-jnp.abs(lam)))
    la = -LRU_C * r * sp
    a = jnp.exp(la)
    mult = jnp.sqrt(jnp.maximum(_neg_expm1(2.0 * la), 0.0))
    return ub, r, ig, sp, a, mult


def _scan8(a8, b8, hprev):
    row = lax.broadcasted_iota(jnp.int32, a8.shape, 0)
    aa, bb = a8, b8
    for d in (1, 2, 4):
        a_s = pltpu.roll(aa, d, 0)
        b_s = pltpu.roll(bb, d, 0)
        m = row >= d
        bb = jnp.where(m, aa * b_s + bb, bb)
        aa = jnp.where(m, aa * a_s, aa)
    return aa * hprev + bb


def _rscan8(c8, d8, lnext):
    row = lax.broadcasted_iota(jnp.int32, c8.shape, 0)
    cc, dd = c8, d8
    for d in (1, 2, 4):
        c_s = pltpu.roll(cc, 8 - d, 0)
        d_s = pltpu.roll(dd, 8 - d, 0)
        m = row < 8 - d
        dd = jnp.where(m, cc * d_s + dd, dd)
        cc = jnp.where(m, cc * c_s, cc)
    return cc * lnext + dd


def _mesh_pos():
    return lax.axis_index("x"), lax.axis_index("y"), lax.axis_index("c")


def _other_chips(x, y):
    return [(1 - x, y), (x, 1 - y), (1 - x, 1 - y)]


def _no_forward():
    pass


def _ag_full_copies(ins, outs, sems):
    send_sems, recv_sems, loc_sems = sems
    n = len(ins)
    x, y, c = _mesh_pos()
    mine = 2 * x + y
    chips = _other_chips(x, y)

    def remote(k, j, slot):
        px, py = chips[j]
        return pltpu.make_async_remote_copy(
            src_ref=ins[k], dst_ref=outs[k].at[slot], send_sem=send_sems.at[k, j], recv_sem=recv_sems.at[k, j],
            device_id=(px, py, c), device_id_type=MESH_ID)

    def local(k):
        return pltpu.make_async_copy(ins[k], outs[k].at[mine], loc_sems.at[k])

    def start():
        for k in range(n):
            local(k).start()
            for j in range(3):
                remote(k, j, mine).start()

    def wait():
        for k in range(n):
            for j, (px, py) in enumerate(chips):
                remote(k, j, 2 * px + py).wait_recv()
        for k in range(n):
            for j in range(3):
                remote(k, j, mine).wait_send()
            local(k).wait()

    return start, _no_forward, wait


def _ag_copies(ins, outs, sems):
    send_sems, recv_sems, fsend_sems, frecv_sems, loc_sems = sems
    n = len(ins)
    x, y, c = _mesh_pos()
    mine = 2 * x + y
    chips = _other_chips(x, y)

    def half(ref, hc):
        r = ref.shape[0] // 2
        return ref.at[pl.ds(pl.multiple_of(hc * r, 16), r)]

    def ici(k, j, slot):
        px, py = chips[j]
        return pltpu.make_async_remote_copy(
            src_ref=half(ins[k], c), dst_ref=half(outs[k].at[slot], c),
            send_sem=send_sems.at[k, j], recv_sem=recv_sems.at[k, j],
            device_id=(px, py, c), device_id_type=MESH_ID)

    def d2d(k, j, hc):
        px, py = chips[j]
        part = half(outs[k].at[2 * px + py], hc)
        return pltpu.make_async_remote_copy(
            src_ref=part, dst_ref=part, send_sem=fsend_sems.at[k, j], recv_sem=frecv_sems.at[k, j],
            device_id=(x, y, 1 - c), device_id_type=MESH_ID)

    def local(k):
        return pltpu.make_async_copy(ins[k], outs[k].at[mine], loc_sems.at[k])

    def start():
        for k in range(n):
            local(k).start()
            for j in range(3):
                ici(k, j, mine).start()

    def forward():
        for k in range(n):
            for j, (px, py) in enumerate(chips):
                ici(k, j, 2 * px + py).wait_recv()
                d2d(k, j, c).start()

    def wait():
        for k in range(n):
            for j in range(3):
                d2d(k, j, 1 - c).wait_recv()
        for k in range(n):
            for j in range(3):
                d2d(k, j, c).wait_send()
                ici(k, j, mine).wait_send()
            local(k).wait()

    return start, forward, wait


def _rs_copies(ins, outs, sems):
    send_sems, recv_sems = sems
    n = len(ins)
    x, y, c = _mesh_pos()
    chips = _other_chips(x, y)

    def remote(k, j):
        px, py = chips[j]
        return pltpu.make_async_remote_copy(
            src_ref=ins[k].at[2 * px + py], dst_ref=outs[k].at[j],
            send_sem=send_sems.at[k, j], recv_sem=recv_sems.at[k, j],
            device_id=(px, py, c), device_id_type=MESH_ID)

    def start():
        for k in range(n):
            for j in range(3):
                remote(k, j).start()

    def wait():
        for k in range(n):
            for j in range(3):
                remote(k, j).wait_recv()
        for k in range(n):
            for j in range(3):
                remote(k, j).wait_send()

    return start, _no_forward, wait


def _swap_copies(ins, outs, sems):
    send_sems, recv_sems = sems
    x, y, c = _mesh_pos()
    copies = [pltpu.make_async_remote_copy(
        src_ref=ins[k], dst_ref=outs[k], send_sem=send_sems.at[k], recv_sem=recv_sems.at[k],
        device_id=(x, y, 1 - c), device_id_type=MESH_ID) for k in range(len(ins))]

    def start():
        for cp in copies:
            cp.start()

    def wait():
        for cp in copies:
            cp.wait()

    return start, _no_forward, wait


def _comm_plan(groups):
    plan, arrs, shapes, sems = [], [], [], []
    for kind, group in groups:
        k = len(group)
        arrs += group
        per_peer = pltpu.SemaphoreType.DMA((k, 3))
        if kind == "ag":
            shapes += [_sds((N_SHARD,) + w.shape, w.dtype) for w in group]
            gsems = [per_peer] * 4 + [pltpu.SemaphoreType.DMA((k,))]
            maker = _ag_copies
        elif kind == "agf":
            shapes += [_sds((N_SHARD,) + w.shape, w.dtype) for w in group]
            gsems = [per_peer] * 2 + [pltpu.SemaphoreType.DMA((k,))]
            maker = _ag_full_copies
        elif kind == "ag8":
            shapes += [_sds((8,) + g.shape, g.dtype) for g in group]
            gsems = [pltpu.SemaphoreType.DMA((k, 7))] * 2 + [pltpu.SemaphoreType.DMA((k,))]
            maker = _ag8_copies
        elif kind == "rs":
            shapes += [_sds((3,) + g.shape[1:], g.dtype) for g in group]
            gsems = [pltpu.SemaphoreType.DMA((k, 3)), pltpu.SemaphoreType.DMA((k, 3))]
            maker = _rs_copies
        else:
            shapes += [_sds(g.shape, g.dtype) for g in group]
            gsems = [pltpu.SemaphoreType.DMA((k,)), pltpu.SemaphoreType.DMA((k,))]
            maker = _swap_copies
        plan.append((maker, k, len(gsems)))
        sems += gsems
    return plan, arrs, shapes, sems


def _comm_fns(plan, cins, couts, sems):
    fns, a, s = [], 0, 0
    for maker, k, ns in plan:
        fns.append(maker(cins[a:a + k], couts[a:a + k], sems[s:s + ns]))
        a += k
        s += ns

    def start():
        for st, _, _ in fns:
            st()

    def forward():
        for _, fw, _ in fns:
            fw()

    def wait():
        for _, _, wt in fns:
            wt()

    return start, forward, wait


def _call(body, name, grid, in_specs, out_specs, out_shape, scratch, args, sem, comm=None):
    if not comm:
        return pl.pallas_call(body, name=name, grid=grid, in_specs=in_specs, out_specs=out_specs,
                              out_shape=out_shape, scratch_shapes=scratch, compiler_params=_cp(sem))(*args)
    plan, c_arrs, c_shapes, c_sems = _comm_plan(comm)
    k = len(c_arrs)
    n_in, n_out, n_scr = len(in_specs), len(out_specs), len(scratch)
    last = grid[0] - 1
    fwd_step = max(1, (2 * last) // 3)

    def wrapped(*refs):
        ins, cins = refs[:n_in], refs[n_in:n_in + k]
        o0 = n_in + k
        outs, couts = refs[o0:o0 + n_out], refs[o0 + n_out:o0 + n_out + k]
        s0 = o0 + n_out + k
        start, forward, wait = _comm_fns(plan, cins, couts, refs[s0 + n_scr:])
        pl.when(pl.program_id(0) == 0)(start)
        pl.when(pl.program_id(0) == fwd_step)(forward)
        body(*ins, *outs, *refs[s0:s0 + n_scr])
        pl.when(pl.program_id(0) == last)(wait)

    return pl.pallas_call(
        wrapped, name=name, grid=grid, in_specs=list(in_specs) + [_any()] * k,
        out_specs=list(out_specs) + [_any()] * k, out_shape=list(out_shape) + c_shapes,
        scratch_shapes=list(scratch) + c_sems, compiler_params=_cp(sem))(*args, *c_arrs)


def _tail_copies(slots_ref, land_ref, part_refs, sib_refs, send_sems, recv_sems):
    x, y, c = _mesh_pos()
    copies = []
    for j, (px, py) in enumerate(_other_chips(x, y)):
        copies.append(pltpu.make_async_remote_copy(
            src_ref=slots_ref.at[2 * px + py], dst_ref=land_ref.at[j], send_sem=send_sems[j], recv_sem=recv_sems[j],
            device_id=(px, py, c), device_id_type=MESH_ID))
    for k, (p_ref, s_ref) in enumerate(zip(part_refs, sib_refs)):
        copies.append(pltpu.make_async_remote_copy(
            src_ref=p_ref, dst_ref=s_ref, send_sem=send_sems[3 + k], recv_sem=recv_sems[3 + k],
            device_id=(x, y, 1 - c), device_id_type=MESH_ID))
    return copies


def _late_copies(part_ref, sib_ref, pack_ref, packs_ref, send_sems, recv_sems):
    x, y, c = _mesh_pos()
    peers, me = _all_peers()
    copies = [pltpu.make_async_remote_copy(
        src_ref=part_ref, dst_ref=sib_ref, send_sem=send_sems[0], recv_sem=recv_sems[0],
        device_id=(x, y, 1 - c), device_id_type=MESH_ID)]
    for j in range(7):
        copies.append(pltpu.make_async_remote_copy(
            src_ref=pack_ref, dst_ref=packs_ref.at[me], send_sem=send_sems[1 + j], recv_sem=recv_sems[1 + j],
            device_id=peers[j][0], device_id_type=MESH_ID))
    return copies


def _split_start(name, bufs, ncp, make_copies):
    hbm = pl.BlockSpec(memory_space=pltpu.HBM)
    sem = pl.BlockSpec(memory_space=pltpu.SEMAPHORE)
    bufs = [pltpu.with_memory_space_constraint(b, pltpu.HBM) for b in bufs]
    nb = len(bufs)

    def body(*refs):
        for cp in make_copies(refs[:nb], refs[nb:nb + ncp], refs[nb + ncp:nb + 2 * ncp]):
            cp.start()
        refs[-1][...] = jnp.zeros_like(refs[-1])

    out = pl.pallas_call(
        body, name=name,
        out_shape=[pltpu.SemaphoreType.DMA(())] * (2 * ncp) + [pltpu.HBM(b.shape, b.dtype) for b in bufs]
                  + [_sds((8, 128), F32)],
        in_specs=[hbm] * nb, out_specs=[sem] * (2 * ncp) + [hbm] * nb + [pl.BlockSpec(memory_space=pltpu.VMEM)],
        input_output_aliases={i: 2 * ncp + i for i in range(nb)},
        compiler_params=pltpu.CompilerParams(has_side_effects=pltpu.SideEffectType.DATAFLOW_SIDE_EFFECTING),
    )(*bufs)
    return out[:2 * ncp], out[2 * ncp:2 * ncp + nb], out[-1]


def _split_wait(name, sems, bufs, ncp, make_copies, after):
    nb = len(bufs)
    hbm = pl.BlockSpec(memory_space=pltpu.HBM)
    sem = pl.BlockSpec(memory_space=pltpu.SEMAPHORE)

    def body(*refs):
        for cp in make_copies(refs[:nb], refs[nb:nb + ncp], refs[nb + ncp:nb + 2 * ncp]):
            cp.wait_send()
            cp.wait_recv()

    return pl.pallas_call(
        body, name=name, out_shape=[pltpu.HBM(b.shape, b.dtype) for b in bufs],
        in_specs=[hbm] * nb + [sem] * (2 * ncp) + [_any()], out_specs=[hbm] * nb,
        input_output_aliases={i: i for i in range(nb)},
        compiler_params=pltpu.CompilerParams(has_side_effects=pltpu.SideEffectType.DATAFLOW_SIDE_EFFECTING),
    )(*bufs, *sems, after)


def _any():
    return pl.BlockSpec(memory_space=pl.ANY)


def _start_copies(pairs, sems, first=0):
    copies = [pltpu.make_async_copy(src, dst, sems.at[first + i]) for i, (src, dst) in enumerate(pairs)]
    for cp in copies:
        cp.start()
    return copies


def _copy_together(pairs, sems):
    for cp in _start_copies(pairs, sems):
        cp.wait()


def _load_w_in_once(w_hbm, w_ref, sems):
    @pl.when(pl.program_id(0) == 0)
    def _():
        _copy_together([(w_hbm.at[s], w_ref.at[:, pl.ds(s * IN_SH, IN_SH)]) for s in range(N_SHARD)], sems)


def _f_inproj(x, g_mix, w_in_g, tm, comm=None):
    s_len = x.shape[0]
    pad_rows = LEFT_CHUNKS * CHUNK
    npad = pad_rows // tm

    def body(x_ref, g_ref, w_hbm, h_ref, qkv_ref, xg_ref, w_ref, w_sems):
        i = pl.program_id(0)
        _load_w_in_once(w_hbm, w_ref, w_sems)

        @pl.when(i < npad)
        def _():
            qkv_ref[...] = jnp.zeros_like(qkv_ref)

        @pl.when(i >= npad)
        def _():
            xv = x_ref[...]
            h = (xv * _rinv(xv) * g_ref[...]).astype(BF16)
            h_ref[...] = h
            proj = _dot(h, w_ref[...])
            qkv_ref[:, 0:D_ATT] = (proj[:, 0:D_ATT] * ATT_SCALE).astype(BF16)
            qkv_ref[:, D_ATT:3 * D_ATT] = proj[:, D_ATT:3 * D_ATT].astype(BF16)
            xg_ref[...] = proj[:, 3 * D_ATT:D_IN]

    def tok(n):
        return pl.BlockSpec((tm, n), lambda i: (jnp.maximum(i - npad, 0), 0))

    return _call(
        body, "f_inproj", (s_len // tm + npad,),
        [tok(1024), _full((1, 1024)), _any()],
        [tok(1024), _rows(tm, 1536), tok(1024)],
        [_sds((s_len, 1024), BF16), _sds((s_len + pad_rows, 1536), BF16), _sds((s_len, 1024), F32)],
        [pltpu.VMEM((1024, D_IN), BF16), pltpu.SemaphoreType.DMA((N_SHARD,))], (x, g_mix, w_in_g), "arbitrary", comm)


N_BIAS = 3


def _bias_table(frow_ref, bias_sc):
    qa = lax.broadcasted_iota(jnp.int32, (QB, KB), 0) // CHUNK
    kcol = lax.broadcasted_iota(jnp.int32, (QB, KB), 1)
    kb = kcol // CHUNK
    band = jnp.where((kb >= qa) & (kb - qa <= LEFT_CHUNKS), 0.0, NEG).astype(F32)
    for h in range(ATT_HEADS):
        row = jnp.broadcast_to(frow_ref[h:h + 1, :], (QB, ROLL_W))
        toep = pltpu.roll(row, 0, 1, stride=1, stride_axis=0)
        gen = toep[:, 0:KB] + band
        bias_sc[N_BIAS - 1, h] = gen
        for v in range(N_BIAS - 1):
            pad_keys = LEFT_CHUNKS * CHUNK - v * QB
            bias_sc[v, h] = gen + jnp.where(kcol < pad_keys, NEG, 0.0).astype(F32)


def _even_lanes():
    return lax.broadcasted_iota(jnp.int32, (1, 2 * HEAD_DIM), 1) < HEAD_DIM


def _att_probs(qm, kts, bias):
    s = jnp.concatenate([_dot_nt(qm, k) for k in kts], axis=1) + bias
    return jnp.exp(s - jnp.max(s, axis=-1, keepdims=True))


def _att_in_specs(clamp):
    def spec(j, col):
        return pl.BlockSpec((QB, D_ATT), lambda i: (clamp(i) + j, col))
    return [spec(2, 0), spec(0, 1), spec(1, 1), spec(2, 1), spec(0, 2), spec(1, 2), spec(2, 2)]


def _f_attn(qkv_pad, frow, comm=None):
    s_len = qkv_pad.shape[0] - LEFT_CHUNKS * CHUNK
    nb = s_len // QB

    def body(q_ref, k0, k1, k2, v0, v1, v2, frow_ref, o_ref, bias_sc):
        i = pl.program_id(0)

        @pl.when(i == 0)
        def _():
            _bias_table(frow_ref, bias_sc)

        var = jnp.minimum(i, N_BIAS - 1)
        even = _even_lanes()
        for hp in range(ATT_HEADS // 2):
            cs = slice(hp * 2 * HEAD_DIM, (hp + 1) * 2 * HEAD_DIM)
            qt = q_ref[:, cs]
            kts = [k0[:, cs], k1[:, cs], k2[:, cs]]
            vts = [v0[:, cs], v1[:, cs], v2[:, cs]]
            res = []
            for e in range(2):
                keep = even if e == 0 else jnp.logical_not(even)
                pb = _att_probs(jnp.where(keep, qt, 0), kts, bias_sc[var, 2 * hp + e]).astype(BF16)
                r = _dot(pb, jnp.concatenate([jnp.where(keep, v, 1) for v in vts], axis=0))
                res.append(r / pltpu.roll(r, HEAD_DIM, 1))
            o_ref[:, cs] = jnp.where(even, res[0], res[1])

    return _call(
        body, "f_attn", (nb,),
        _att_in_specs(lambda i: i) + [_full((ATT_HEADS, ROLL_W))],
        [_rows(QB, D_ATT)], [_sds((s_len, D_ATT), F32)],
        [pltpu.VMEM((N_BIAS, ATT_HEADS, QB, KB), F32)], (*([qkv_pad] * 7), frow), "arbitrary", comm)


def _f_lru(xg, conv_w, conv_b, wrg, brg, wig, big, lam, tl, comm=None):
    s_len = xg.shape[0]

    def body(xg_ref, cw_ref, cb_ref, wrg_ref, brg_ref, wig_ref, big_ref, l_ref,
             rec_ref, u_ref, hs_ref, xbuf, a_sc, b_sc, hcar):
        i = pl.program_id(0)

        @pl.when(i == 0)
        def _():
            xbuf[0:8, :] = jnp.zeros((8, D_LRU), F32)
            hcar[...] = jnp.zeros((8, D_LRU), F32)

        xu0 = xg_ref[:, 0:D_LRU]
        xbuf[8:8 + tl, :] = xu0
        u = cb_ref[...] + cw_ref[0:1, :] * xbuf[pl.ds(5, tl), :]
        for j in range(1, 4):
            u = u + cw_ref[j:j + 1, :] * xbuf[pl.ds(5 + j, tl), :]
        xbuf[0:8, :] = xu0[tl - 8:tl, :]
        u_ref[...] = u
        _, _, ig, _, a, mult = _lru_gates(u, wrg_ref[...], brg_ref[...], wig_ref[...], big_ref[...], l_ref[...])
        a_sc[...] = a
        b_sc[...] = mult * (ig * u)

        def grp(g, hprev):
            off = pl.multiple_of(g * 8, 8)
            h8 = _scan8(a_sc[pl.ds(off, 8), :], b_sc[pl.ds(off, 8), :], hprev)
            hs_ref[pl.ds(off, 8), :] = h8
            return h8[7:8, :]

        hcar[0:1, :] = lax.fori_loop(0, tl // 8, grp, hcar[0:1, :])
        rec_ref[...] = hs_ref[...] * _gelu(xg_ref[:, D_LRU:2 * D_LRU])

    vec = _full((1, D_LRU))
    return _call(
        body, "f_lru", (s_len // tl,),
        [_rows(tl, 1024), _full((4, D_LRU)), vec, _full((D_LRU, D_LRU)), vec, _full((D_LRU, D_LRU)), vec, vec],
        [_rows(tl, D_LRU)] * 3, [_sds((s_len, D_LRU), F32)] * 3,
        [pltpu.VMEM((tl + 8, D_LRU), F32), pltpu.VMEM((tl, D_LRU), F32),
         pltpu.VMEM((tl, D_LRU), F32), pltpu.VMEM((8, D_LRU), F32)],
        (xg, conv_w, conv_b, wrg, brg, wig, big, lam), "arbitrary", comm)


def _f_mem(mem, g_mem, wk, wv):
    def body(mem_ref, g_ref, wk_ref, wv_ref, mn_ref, kx_ref, vx_ref):
        mv = mem_ref[...]
        mn = (mv * _rinv(mv) * g_ref[...]).astype(BF16)
        mn_ref[...] = mn
        kx_ref[...] = _dot(mn, wk_ref[...]).astype(BF16)
        vx_ref[...] = _dot(mn, wv_ref[...]).astype(BF16)

    m = mem.shape[0]
    return pl.pallas_call(
        body, name="f_mem", out_shape=[_sds((m, 1024), BF16)] * 3,
        compiler_params=_cp())(mem, g_mem, wk, wv)


def _xattn_probs(q, k):
    s = _dot_nt(q, k) * X_SCALE
    m = jnp.max(s, axis=-1, keepdims=True)
    p = jnp.exp(s - m)
    return p, jnp.sum(p, axis=-1, keepdims=True)


def _f_mid(x, att, rec, g_oa, g_ol, w_out, g_cross, wq, kx, vx, wo, tm, comm=None):
    s_len = x.shape[0]
    m_len = kx.shape[0]

    def body(x_ref, att_ref, rec_ref, goa_ref, gol_ref, wout_ref, gc_ref, wq_ref, kx_ref, vx_ref, wo_ref,
             mg_ref, x1_ref, hc_ref, qx_ref, ox_ref, x2_ref):
        av = att_ref[...]
        rv = rec_ref[...]
        mg_ref[:, 0:D_ATT] = (av * _rinv(av) * goa_ref[...]).astype(BF16)
        mg_ref[:, D_ATT:1024] = (rv * _rinv(rv) * gol_ref[...]).astype(BF16)
        x1 = x_ref[...] + _dot(mg_ref[...], wout_ref[...])
        x1_ref[...] = x1
        hc = (x1 * _rinv(x1) * gc_ref[...]).astype(BF16)
        hc_ref[...] = hc
        qx_ref[...] = _dot(hc, wq_ref[...]).astype(BF16)
        for h in range(X_HEADS):
            sl = slice(h * X_HEAD_DIM, (h + 1) * X_HEAD_DIM)
            p, l = _xattn_probs(qx_ref[:, sl], kx_ref[:, sl])
            ox_ref[:, sl] = (_dot(p.astype(BF16), vx_ref[:, sl]) / l).astype(BF16)
        x2_ref[...] = x1 + _dot(ox_ref[...], wo_ref[...])

    sq = _full((1024, 1024))
    return _call(
        body, "f_mid", (s_len // tm,),
        [_rows(tm, 1024), _rows(tm, 512), _rows(tm, 512), _full((1, 512)), _full((1, 512)), sq,
         _full((1, 1024)), sq, _full((m_len, 1024)), _full((m_len, 1024)), sq],
        [_rows(tm, 1024)] * 6,
        [_sds((s_len, 1024), BF16), _sds((s_len, 1024), F32), _sds((s_len, 1024), BF16),
         _sds((s_len, 1024), BF16), _sds((s_len, 1024), BF16), _sds((s_len, 1024), F32)],
        [], (x, att, rec, g_oa, g_ol, w_out, g_cross, wq, kx, vx, wo), "arbitrary", comm)


FF_CHUNKS = [(0, 1280), (1280, D_FF)]


def _first_step_and_rest(step):
    pl.when(pl.program_id(0) == 0)(lambda: step(True))
    pl.when(pl.program_id(0) > 0)(lambda: step(False))


def _ffn_weights(first, pairs, sems):
    if not first:
        return lambda c, j: None
    copies = _start_copies([(hbm.at[c0:c1, :], vmem.at[c0:c1, :]) for c0, c1 in FF_CHUNKS for hbm, vmem in pairs],
                           sems)
    return lambda c, j: copies[c * len(pairs) + j].wait()


def _f_ffn(x2, tgt, g_ffn, g_final, wg, wu, wd, tm):
    s_len = x2.shape[0]

    def body(x2_ref, t_ref, gf_ref, gfin_ref, wg_hbm, wu_hbm, wd_hbm,
             hf_ref, g_ref, u_ref, a_ref, dx3_ref, loss_ref, dgfin_ref, wg_ref, wu_ref, wd_ref, w_sems):
        @pl.when(pl.program_id(0) == 0)
        def _():
            _copy_together([(wg_hbm, wg_ref), (wu_hbm, wu_ref), (wd_hbm, wd_ref)], w_sems)
            loss_ref[...] = jnp.zeros_like(loss_ref)
            dgfin_ref[...] = jnp.zeros_like(dgfin_ref)

        x2v = x2_ref[...]
        hf = (x2v * _rinv(x2v) * gf_ref[...]).astype(BF16)
        hf_ref[...] = hf
        x3 = x2v
        for c0, c1 in FF_CHUNKS:
            gv = _dot_nt(hf, wg_ref[c0:c1, :])
            uv = _dot_nt(hf, wu_ref[c0:c1, :])
            av = (gv * jax.nn.sigmoid(gv) * uv).astype(BF16)
            g_ref[:, c0:c1] = gv.astype(BF16)
            u_ref[:, c0:c1] = uv.astype(BF16)
            a_ref[:, c0:c1] = av
            x3 = x3 + _dot(av, wd_ref[c0:c1, :])
        r3 = _rinv(x3)
        yh = x3 * r3
        gfin = gfin_ref[...]
        err = yh * gfin - t_ref[...]
        loss_ref[...] += jnp.full((1, 128), 0.5 / D_MODEL, F32) * jnp.sum(err * err)
        dy = err * (1.0 / D_MODEL)
        dgfin_ref[...] += jnp.sum(dy * yh, axis=0, keepdims=True)
        dyh = dy * gfin
        dx3_ref[...] = r3 * (dyh - yh * jnp.mean(dyh * yh, axis=-1, keepdims=True))

    vec = _full((1, 1024))
    return pl.pallas_call(
        body, name="f_ffn", grid=(s_len // tm,),
        in_specs=[_rows(tm, 1024), _rows(tm, 1024), vec, vec, _any(), _any(), _any()],
        out_specs=[_rows(tm, 1024), _rows(tm, D_FF), _rows(tm, D_FF), _rows(tm, D_FF),
                   _rows(tm, 1024), _full((1, 128)), vec],
        out_shape=[_sds((s_len, 1024), BF16)] + [_sds((s_len, D_FF), BF16)] * 3
                  + [_sds((s_len, 1024), F32), _sds((1, 128), F32), _sds((1, 1024), F32)],
        scratch_shapes=[pltpu.VMEM((D_FF, 1024), BF16)] * 3 + [pltpu.SemaphoreType.DMA((3,))],
        compiler_params=_cp("arbitrary"))(x2, tgt, g_ffn, g_final, wg, wu, wd)


def _b_ffn(dx3, x2, gact, uact, g_ffn, wg, wu, wd, tm):
    s_len = x2.shape[0]

    def body(dx3_ref, x2_ref, g_ref, u_ref, gf_ref, wg_hbm, wu_hbm, wd_hbm,
             dg_ref, du_ref, dx2_ref, dgf_ref, wg_ref, wu_ref, wd_ref, w_sems):
        def step(first):
            if first:
                dgf_ref[...] = jnp.zeros_like(dgf_ref)
            ready = _ffn_weights(first, [(wd_hbm, wd_ref), (wg_hbm, wg_ref), (wu_hbm, wu_ref)], w_sems)
            dx3v = dx3_ref[...]
            dx3b = dx3v.astype(BF16)
            dhf = jnp.zeros(dx3v.shape, F32)
            for c, (c0, c1) in enumerate(FF_CHUNKS):
                ready(c, 0)
                da = _dot_nt(dx3b, wd_ref[c0:c1, :])
                gv = g_ref[:, c0:c1].astype(F32)
                uv = u_ref[:, c0:c1].astype(F32)
                sg = jax.nn.sigmoid(gv)
                dub = (da * gv * sg).astype(BF16)
                dgb = (da * uv * (sg * (1.0 + gv * (1.0 - sg)))).astype(BF16)
                du_ref[:, c0:c1] = dub
                dg_ref[:, c0:c1] = dgb
                ready(c, 1)
                ready(c, 2)
                dhf = dhf + _dot(dgb, wg_ref[c0:c1, :]) + _dot(dub, wu_ref[c0:c1, :])
            dx, dgf = _rms_bwd(dhf, x2_ref[...], gf_ref[...])
            dx2_ref[...] = dx3v + dx
            dgf_ref[...] += dgf

        _first_step_and_rest(step)

    vec = _full((1, 1024))
    return pl.pallas_call(
        body, name="b_ffn", grid=(s_len // tm,),
        in_specs=[_rows(tm, 1024), _rows(tm, 1024), _rows(tm, D_FF), _rows(tm, D_FF), vec,
                  _any(), _any(), _any()],
        out_specs=[_rows(tm, D_FF), _rows(tm, D_FF), _rows(tm, 1024), vec],
        out_shape=[_sds((s_len, D_FF), BF16)] * 2 + [_sds((s_len, 1024), F32), _sds((1, 1024), F32)],
        scratch_shapes=[pltpu.VMEM((D_FF, 1024), BF16)] * 3 + [pltpu.SemaphoreType.DMA((3 * len(FF_CHUNKS),))],
        compiler_params=_cp("arbitrary"))(dx3, x2, gact, uact, g_ffn, wg, wu, wd)


def _b_mid(dx2, qx, x1, att, rec, kx, vx, wo, wq, w_out, g_cross, g_oa, g_ol, tm, comm=None):
    s_len = x1.shape[0]
    m_len = kx.shape[0]

    def body(dx2_ref, qx_ref, x1_ref, att_ref, rec_ref, kx_ref, vx_ref, wo_ref, wq_ref, wout_ref,
             gc_ref, goa_ref, gol_ref,
             dqx_ref, dx1_ref, datt_ref, drec_ref, dkx_ref, dvx_ref, dgc_ref, dgoa_ref, dgol_ref):
        @pl.when(pl.program_id(0) == 0)
        def _():
            for r in (dkx_ref, dvx_ref, dgc_ref, dgoa_ref, dgol_ref):
                r[...] = jnp.zeros_like(r)

        dx2v = dx2_ref[...]
        dox = _dot_nt(dx2v.astype(BF16), wo_ref[...])
        for h in range(X_HEADS):
            sl = slice(h * X_HEAD_DIM, (h + 1) * X_HEAD_DIM)
            q = qx_ref[:, sl]
            p, l = _xattn_probs(q, kx_ref[:, sl])
            pn = p * (1.0 / l)
            dob = dox[:, sl].astype(BF16)
            dp = _dot_nt(dob, vx_ref[:, sl])
            dvx_ref[:, sl] += _dot_tn(pn.astype(BF16), dob)
            ds = pn * (dp - jnp.sum(dp * pn, axis=-1, keepdims=True))
            dsb = (ds * X_SCALE).astype(BF16)
            dqx_ref[:, sl] = _dot(dsb, kx_ref[:, sl]).astype(BF16)
            dkx_ref[:, sl] += _dot_tn(dsb, q)
        dhc = _dot_nt(dqx_ref[...], wq_ref[...])
        dx, dgc = _rms_bwd(dhc, x1_ref[...], gc_ref[...])
        dx1 = dx2v + dx
        dx1_ref[...] = dx1
        dgc_ref[...] += dgc
        dmg = _dot_nt(dx1.astype(BF16), wout_ref[...])
        da, dgoa = _rms_bwd(dmg[:, 0:D_ATT], att_ref[...], goa_ref[...])
        datt_ref[...] = da
        dgoa_ref[...] += dgoa
        dr, dgol = _rms_bwd(dmg[:, D_ATT:1024], rec_ref[...], gol_ref[...])
        drec_ref[...] = dr
        dgol_ref[...] += dgol

    sq = _full((1024, 1024))
    mk = _full((m_len, 1024))
    return _call(
        body, "b_mid", (s_len // tm,),
        [_rows(tm, 1024), _rows(tm, 1024), _rows(tm, 1024), _rows(tm, 512), _rows(tm, 512), mk, mk,
         sq, sq, sq, _full((1, 1024)), _full((1, 512)), _full((1, 512))],
        [_rows(tm, 1024), _rows(tm, 1024), _rows(tm, 512), _rows(tm, 512), mk, mk,
         _full((1, 1024)), _full((1, 512)), _full((1, 512))],
        [_sds((s_len, 1024), BF16), _sds((s_len, 1024), F32), _sds((s_len, 512), F32),
         _sds((s_len, 512), F32), _sds((m_len, 1024), F32), _sds((m_len, 1024), F32),
         _sds((1, 1024), F32), _sds((1, 512), F32), _sds((1, 512), F32)],
        [], (dx2, qx, x1, att, rec, kx, vx, wo, wq, w_out, g_cross, g_oa, g_ol), "arbitrary", comm)


def _b_mem(dkx, dvx, mem, mn, g_mem, wk, wv):
    def body(dkx_ref, dvx_ref, mem_ref, mn_ref, g_ref, wk_ref, wv_ref, dwk_ref, dwv_ref, dgm_ref,
             dwkb_ref, dwvb_ref):
        dkb = dkx_ref[...].astype(BF16)
        dvb = dvx_ref[...].astype(BF16)
        dwk = _dot_tn(mn_ref[...], dkb)
        dwv = _dot_tn(mn_ref[...], dvb)
        dwk_ref[...] = dwk
        dwv_ref[...] = dwv
        dwkb_ref[...] = dwk.astype(BF16)
        dwvb_ref[...] = dwv.astype(BF16)
        dmn = _dot_nt(dkb, wk_ref[...]) + _dot_nt(dvb, wv_ref[...])
        mv = mem_ref[...]
        dgm_ref[...] = jnp.sum(dmn * (mv * _rinv(mv)), axis=0, keepdims=True)

    return pl.pallas_call(
        body, name="b_mem",
        out_shape=[_sds((1024, 1024), F32), _sds((1024, 1024), F32), _sds((1, 1024), F32),
                   _sds((1024, 1024), BF16), _sds((1024, 1024), BF16)],
        compiler_params=_cp())(dkx, dvx, mem, mn, g_mem, wk, wv)


def _b_lru(drec, hs, u, xg, conv_w, wrg, brg, wig, big, lam, tl, comm=None):
    s_len = xg.shape[0]
    nt = s_len // tl

    def body(drec_ref, hs_ref, hsp_ref, u_ref, xg_ref, cw_ref, wrg_ref, brg_ref, wig_ref, big_ref, l_ref,
             dxg_ref, dwrg_ref, dwig_ref, dbrg_ref, dbig_ref, dlam_ref, dcw_ref, dcb_ref,
             hbuf, abuf, dubuf, c_sc, d_sc, lam_sc, lcar, wacc_r, wacc_i):
        i = pl.program_id(0)
        tt = nt - 1 - i

        @pl.when(i == 0)
        def _():
            for r in (wacc_r, wacc_i, dbrg_ref, dbig_ref, dlam_ref, dcw_ref, dcb_ref):
                r[...] = jnp.zeros_like(r)
            abuf[tl:tl + 8, :] = jnp.zeros((8, D_LRU), F32)
            dubuf[tl:tl + 8, :] = jnp.zeros((8, D_LRU), F32)
            lcar[...] = jnp.zeros((8, D_LRU), F32)

        xu0 = xg_ref[:, 0:D_LRU]
        hsv = hs_ref[...]
        uv = u_ref[...]
        hbuf[8:8 + tl, :] = hsv
        hbuf[0:8, :] = jnp.where(tt > 0, hsp_ref[...], 0.0)
        hshift = hbuf[pl.ds(7, tl), :]
        wrg_v = wrg_ref[...]
        wig_v = wig_ref[...]
        lamv = l_ref[...]
        ub, r, ig, sp, a, mult = _lru_gates(uv, wrg_v, brg_ref[...], wig_v, big_ref[...], lamv)
        abuf[0:tl, :] = a
        c_sc[...] = abuf[pl.ds(1, tl), :]
        gel, dgel = _gelu_and_grad(xg_ref[:, D_LRU:2 * D_LRU])
        drv = drec_ref[...]
        d_sc[...] = drv * gel
        dxg_ref[:, D_LRU:2 * D_LRU] = (drv * hsv * dgel).astype(BF16)

        def grp(k, lnext):
            off = pl.multiple_of((tl // 8 - 1 - k) * 8, 8)
            l8 = _rscan8(c_sc[pl.ds(off, 8), :], d_sc[pl.ds(off, 8), :], lnext)
            lam_sc[pl.ds(off, 8), :] = l8
            return l8[0:1, :]

        lcar[0:1, :] = lax.fori_loop(0, tl // 8, grp, lcar[0:1, :])
        abuf[tl:tl + 8, :] = a[0:8, :]
        db = lam_sc[...]
        da = db * hshift
        dmult = db * (ig * uv)
        dig = db * mult * uv
        du = db * mult * ig
        dla = da * a - dmult * (a * a) / mult
        dlam_ref[...] += jnp.sum(dla * (-LRU_C) * r, axis=0, keepdims=True)
        dzr = dla * (-LRU_C * sp) * r * (1.0 - r)
        dzi = dig * ig * (1.0 - ig)
        dzrb = dzr.astype(BF16)
        dzib = dzi.astype(BF16)
        du = du + _dot_nt(dzrb, wrg_v) + _dot_nt(dzib, wig_v)
        wacc_r[...] += _dot_tn(ub, dzrb)
        wacc_i[...] += _dot_tn(ub, dzib)
        dbrg_ref[...] += jnp.sum(dzr, axis=0, keepdims=True)
        dbig_ref[...] += jnp.sum(dzi, axis=0, keepdims=True)
        dcb_ref[...] += jnp.sum(du, axis=0, keepdims=True)
        dubuf[0:tl, :] = du
        dxu0 = jnp.zeros((tl, D_LRU), F32)
        for j in range(4):
            dsh = dubuf[pl.ds(3 - j, tl), :]
            dxu0 = dxu0 + cw_ref[j:j + 1, :] * dsh
            dcw_ref[j:j + 1, :] += jnp.sum(xu0 * dsh, axis=0, keepdims=True)
        dubuf[tl:tl + 8, :] = du[0:8, :]
        dxg_ref[:, 0:D_LRU] = dxu0.astype(BF16)

        @pl.when(i == nt - 1)
        def _():
            dlam_ref[...] = dlam_ref[...] * (-jax.nn.sigmoid(-lamv))
            for n in range(LRU_BLOCKS):
                blk = slice(n * LRU_BLOCK, (n + 1) * LRU_BLOCK)
                dwrg_ref[n] = wacc_r[blk, blk]
                dwig_ref[n] = wacc_i[blk, blk]

    def rev(n):
        return pl.BlockSpec((tl, n), lambda i: (nt - 1 - i, 0))

    prev8 = pl.BlockSpec((8, D_LRU), lambda i: (jnp.maximum((nt - 1 - i) * (tl // 8) - 1, 0), 0))
    vec = _full((1, D_LRU))
    sq = _full((D_LRU, D_LRU))
    blocks_shape = (LRU_BLOCKS, LRU_BLOCK, LRU_BLOCK)
    blocks = _full(blocks_shape)
    return _call(
        body, "b_lru", (nt,),
        [rev(D_LRU), rev(D_LRU), prev8, rev(D_LRU), rev(1024), _full((4, D_LRU)), sq, vec, sq, vec, vec],
        [rev(1024), blocks, blocks, vec, vec, vec, _full((4, D_LRU)), vec],
        [_sds((s_len, 1024), BF16), _sds(blocks_shape, F32), _sds(blocks_shape, F32),
         _sds((1, D_LRU), F32), _sds((1, D_LRU), F32), _sds((1, D_LRU), F32),
         _sds((4, D_LRU), F32), _sds((1, D_LRU), F32)],
        [pltpu.VMEM((tl + 8, D_LRU), F32)] * 3 + [pltpu.VMEM((tl, D_LRU), F32)] * 3
        + [pltpu.VMEM((8, D_LRU), F32)] + [pltpu.VMEM((D_LRU, D_LRU), F32)] * 2,
        (drec, hs, hs, u, xg, conv_w, wrg, brg, wig, big, lam), "arbitrary", comm)


def _b_attn(qkv_pad, att, datt, frow, comm=None):
    s_len = datt.shape[0]
    nb = s_len // QB
    n_pair = ATT_HEADS // 2
    pair_w = 2 * HEAD_DIM

    def body(q_ref, k0, k1, k2, v0, v1, v2, o_ref, do_ref, frow_ref, dq_ref, dkv_ref, dfrow_ref,
             bias_sc, dt_sc, acc_sc):
        t = pl.program_id(0)

        @pl.when(t == 0)
        def _():
            _bias_table(frow_ref, bias_sc)
            dt_sc[...] = jnp.zeros_like(dt_sc)
            acc_sc[...] = jnp.zeros_like(acc_sc)

        @pl.when(t < nb)
        def _():
            var = jnp.minimum(t, N_BIAS - 1)
            even = _even_lanes()
            for hp in range(n_pair):
                cs = slice(hp * pair_w, (hp + 1) * pair_w)
                qt = q_ref[:, cs]
                kts = [k0[:, cs], k1[:, cs], k2[:, cs]]
                vts = [v0[:, cs], v1[:, cs], v2[:, cs]]
                kcat = jnp.concatenate(kts, axis=0)
                dot = do_ref[:, cs]
                dd = dot * o_ref[:, cs]
                dos_pair, dsbs, pbs, dqs = None, [], [], []
                for e in range(2):
                    keep = even if e == 0 else jnp.logical_not(even)
                    qm = jnp.where(keep, qt, 0)
                    p = _att_probs(qm, kts, bias_sc[var, 2 * hp + e])
                    inv = 1.0 / jnp.sum(p, axis=-1, keepdims=True)
                    dos = jnp.where(keep, dot * inv, 0.0)
                    delta = jnp.sum(jnp.where(keep, dd, 0.0), axis=-1, keepdims=True) * inv
                    dp = jnp.concatenate([_dot_nt(dos.astype(BF16), v) for v in vts], axis=1)
                    ds = p * (dp - delta)
                    dt_sc[2 * hp + e] += ds
                    dsb = ds.astype(BF16)
                    dq = _dot(dsb, kcat)
                    dqs.append(dq)
                    dsbs.append(dsb)
                    pbs.append(p.astype(BF16))
                    dos_pair = dos if e == 0 else dos_pair + dos
                dq_ref[:, cs] = (jnp.where(even, dqs[0], dqs[1]) * ATT_SCALE).astype(BF16)
                qtt = qt.astype(F32).T.astype(BF16)
                dost = dos_pair.T.astype(BF16)
                for j in range(3):
                    slot = (t + 1 + j) % 3
                    js = slice(j * QB, (j + 1) * QB)
                    for e in range(2):
                        hr = slice(e * HEAD_DIM, (e + 1) * HEAD_DIM)
                        acc_sc[slot, hp, hr, :] += _dot(qtt[hr], dsbs[e][:, js])
                        acc_sc[slot, n_pair + hp, hr, :] += _dot(dost[hr], pbs[e][:, js])

        done = (t + 1) % 3

        @pl.when(t >= 2)
        def _():
            for i in range(2 * n_pair):
                dkv_ref[:, i * pair_w:(i + 1) * pair_w] = acc_sc[done, i].T.astype(BF16)

        acc_sc[done] = jnp.zeros((2 * n_pair, pair_w, QB), F32)

        @pl.when(t == nb + 1)
        def _():
            row = lax.broadcasted_iota(jnp.int32, (8, ROLL_W), 0)
            pad = jnp.zeros((8, ROLL_W - KB), F32)
            for h in range(ATT_HEADS):
                acc8 = jnp.concatenate([dt_sc[h, 0:8, :], pad], axis=1)
                for a1 in range(1, QB // 8):
                    blk = jnp.concatenate([dt_sc[h, 8 * a1:8 * a1 + 8, :], pad], axis=1)
                    acc8 = acc8 + pltpu.roll(blk, ROLL_W - 8 * a1, 1)
                for k in range(3):
                    acc8 = jnp.where(((row >> k) & 1) == 1, pltpu.roll(acc8, ROLL_W - (1 << k), 1), acc8)
                dfrow_ref[h:h + 1, :] = jnp.sum(acc8, axis=0, keepdims=True)

    clamp = lambda t: jnp.minimum(t, nb - 1)
    qrows = pl.BlockSpec((QB, D_ATT), lambda t: (clamp(t), 0))
    return _call(
        body, "b_attn", (nb + 2,),
        _att_in_specs(clamp) + [qrows, qrows, _full((ATT_HEADS, ROLL_W))],
        [qrows, pl.BlockSpec((QB, 2 * D_ATT), lambda t: (jnp.maximum(t - 2, 0), 0)),
         _full((ATT_HEADS, ROLL_W))],
        [_sds((s_len, D_ATT), BF16), _sds((s_len, 2 * D_ATT), BF16), _sds((ATT_HEADS, ROLL_W), F32)],
        [pltpu.VMEM((N_BIAS, ATT_HEADS, QB, KB), F32), pltpu.VMEM((ATT_HEADS, QB, KB), F32),
         pltpu.VMEM((3, 2 * n_pair, pair_w, QB), F32)],
        (*([qkv_pad] * 7), att, datt, frow), "arbitrary", comm)


def _b_win(dq, dkv, dxg, h, ts):
    s_len = h.shape[0]
    steps = s_len // ts

    def body(dq_ref, dkv_ref, dxg_ref, h_ref, dw_hbm, dwb_hbm, acc, accb, sems):
        @pl.when(pl.program_id(0) == 0)
        def _():
            acc[...] = jnp.zeros_like(acc)

        @pl.when(pl.program_id(0) < steps - 1)
        def _():
            dproj = jnp.concatenate([dq_ref[...], dkv_ref[...], dxg_ref[...]], axis=1)
            acc[...] += _dot_tn(h_ref[...], dproj)

        @pl.when(pl.program_id(0) == steps - 1)
        def _():
            dproj = jnp.concatenate([dq_ref[...], dkv_ref[...], dxg_ref[...]], axis=1)
            copies = []
            for s in range(N_SHARD):
                cols = slice(s * IN_SH, (s + 1) * IN_SH)
                total = acc[:, cols] + _dot_tn(h_ref[...], dproj[:, cols])
                acc[:, cols] = total
                accb[:, cols] = total.astype(BF16)
                copies += _start_copies([(acc.at[:, cols], dw_hbm.at[s]), (accb.at[:, cols], dwb_hbm.at[s])],
                                        sems, 2 * s)
            for cp in copies:
                cp.wait()

    shape = (N_SHARD, 1024, IN_SH)
    return pl.pallas_call(
        body, name="b_win", grid=(steps,),
        in_specs=[_rows(ts, 512), _rows(ts, 1024), _rows(ts, 1024), _rows(ts, 1024)],
        out_specs=[_any()] * 2, out_shape=[_sds(shape, F32), _sds(shape, BF16)],
        scratch_shapes=[pltpu.VMEM((1024, D_IN), F32), pltpu.VMEM((1024, D_IN), BF16),
                        pltpu.SemaphoreType.DMA((2 * N_SHARD,))],
        compiler_params=_cp("arbitrary"))(dq, dkv, dxg, h)


RING = 3


def _ring_tiles(hbm_refs, bufs, sems, tm, steps):
    i = pl.program_id(0)

    def copies(step):
        slot = step % RING
        return [pltpu.make_async_copy(h.at[pl.ds(step * tm, tm), :], b.at[slot], sems.at[k, slot])
                for k, (h, b) in enumerate(zip(hbm_refs, bufs))]

    @pl.when(i == 0)
    def _():
        for s in range(min(RING - 1, steps)):
            for cp in copies(s):
                cp.start()

    @pl.when(i + RING - 1 < steps)
    def _():
        for cp in copies(i + RING - 1):
            cp.start()

    for cp in copies(i):
        cp.wait()
    return [b.at[i % RING] for b in bufs]


def _b_inproj(dq, dkv, dxg, x, dx1, g_mix, w_in_g, tm, comm=None):
    s_len = x.shape[0]
    steps = s_len // tm

    def body(dq_ref, dkv_ref, dxg_ref, x_hbm, dx1_hbm, g_ref, w_hbm, gx_ref, dgm_ref, w_ref, w_sems,
             xbuf, dx1buf, ring_sems):
        _load_w_in_once(w_hbm, w_ref, w_sems)

        @pl.when(pl.program_id(0) == 0)
        def _():
            dgm_ref[...] = jnp.zeros_like(dgm_ref)

        x_ref, dx1_ref = _ring_tiles([x_hbm, dx1_hbm], [xbuf, dx1buf], ring_sems, tm, steps)
        dproj = jnp.concatenate([dq_ref[...], dkv_ref[...], dxg_ref[...]], axis=1)
        dh = _dot_nt(dproj, w_ref[...])
        dx, dgm = _rms_bwd(dh, x_ref[...], g_ref[...])
        gx_ref[...] = dx1_ref[...] + dx
        dgm_ref[...] += dgm

    return _call(
        body, "b_inproj", (steps,),
        [_rows(tm, 512), _rows(tm, 1024), _rows(tm, 1024), _any(), _any(), _full((1, 1024)), _any()],
        [_rows(tm, 1024), _full((1, 1024))],
        [_sds((s_len, 1024), F32), _sds((1, 1024), F32)],
        [pltpu.VMEM((1024, D_IN), BF16), pltpu.SemaphoreType.DMA((N_SHARD,)),
         pltpu.VMEM((RING, tm, 1024), F32), pltpu.VMEM((RING, tm, 1024), F32), pltpu.SemaphoreType.DMA((2, RING))],
        (dq, dkv, dxg, x, dx1, g_mix, w_in_g), "arbitrary", comm)


MXU_DIM_V7X = 256
FLUSH_GROUPS = 4


def _mm_tn(xa, ya, name, ts):
    s_len, k = xa.shape
    n = ya.shape[1]

    steps = s_len // ts
    tiles = k // MXU_DIM_V7X
    edges = [MXU_DIM_V7X * ((tiles * g) // FLUSH_GROUPS) for g in range(FLUSH_GROUPS + 1)]

    def body(x_ref, y_ref, o_hbm, ob_hbm, acc, accb, sems):
        @pl.when(pl.program_id(0) == 0)
        def _():
            acc[...] = jnp.zeros_like(acc)

        @pl.when(pl.program_id(0) < steps - 1)
        def _():
            acc[...] += _dot_tn(x_ref[...].astype(BF16), y_ref[...].astype(BF16))

        @pl.when(pl.program_id(0) == steps - 1)
        def _():
            yb = y_ref[...].astype(BF16)
            copies = []
            for g in range(FLUSH_GROUPS):
                rows = slice(edges[g], edges[g + 1])
                total = acc[rows, :] + _dot_tn(x_ref[:, rows].astype(BF16), yb)
                acc[rows, :] = total
                accb[rows, :] = total.astype(BF16)
                copies += _start_copies([(acc.at[rows, :], o_hbm.at[rows, :]), (accb.at[rows, :], ob_hbm.at[rows, :])],
                                        sems, 2 * g)
            for cp in copies:
                cp.wait()

    return pl.pallas_call(
        body, name=name, grid=(steps,), in_specs=[_rows(ts, k), _rows(ts, n)],
        out_specs=[_any()] * 2, out_shape=[_sds((k, n), F32), _sds((k, n), BF16)],
        scratch_shapes=[pltpu.VMEM((k, n), F32), pltpu.VMEM((k, n), BF16),
                        pltpu.SemaphoreType.DMA((2 * FLUSH_GROUPS,))],
        compiler_params=_cp("arbitrary"))(xa, ya)


PAD_KEYS = LEFT_CHUNKS * CHUNK
F_HI = PAD_KEYS - MAX_REL + 1
F_LO = PAD_KEYS + MAX_REL


def _frow_from_rel_bias(rb):
    last = rb[:, 2 * MAX_REL:2 * MAX_REL + 1]
    hi = jnp.broadcast_to(last, (ATT_HEADS, F_HI))
    mid = rb[:, 1:2 * MAX_REL][:, ::-1]
    lo = jnp.broadcast_to(rb[:, 0:1], (ATT_HEADS, KB - F_LO))
    wrap = jnp.broadcast_to(last, (ATT_HEADS, ROLL_W - KB))
    return jnp.concatenate([hi, mid, lo, wrap], axis=1)


def _rel_bias_grad_from_dfrow(df):
    g_last = jnp.sum(df[:, 0:F_HI], axis=1, keepdims=True) + jnp.sum(df[:, KB:ROLL_W], axis=1, keepdims=True)
    mid = df[:, F_HI:F_LO][:, ::-1]
    g_first = jnp.sum(df[:, F_LO:KB], axis=1, keepdims=True)
    return jnp.concatenate([g_first, mid, g_last], axis=1)


def _block_diag(w):
    eye = jnp.eye(8, dtype=w.dtype)
    return (w[:, :, None, :] * eye[:, None, :, None]).reshape(D_LRU, D_LRU)


MID = ['w_out', 'wq_c', 'wk_c', 'wv_c', 'wo_c']
TRANSPOSED = ['w_gate', 'w_up']
AG_IN_INPROJ = ['w_out', 'wq_c', 'wk_c']
AG_IN_ATTN = ['wv_c', 'wo_c', 'w_gate']
AG_IN_LRU = ['w_up']
AG_IN_MID = ['w_down']
RS_IN_MID = ['w_gate', 'w_up']
RS_IN_LRU = ['w_down']
RS_IN_ATTN = MID


def _local_step(x, mem, tgt, p, gw, shards=None, chip=None):
    s_len = x.shape[0]
    tm = min(256, s_len)
    tmb = min(512, s_len)
    tl = min(512, s_len)
    frow = _frow_from_rel_bias(p['rel_bias'])
    wrg = _block_diag(p['w_rg']).astype(BF16)
    wig = _block_diag(p['w_ig']).astype(BF16)
    gw = dict(gw)

    big, bigb, recv, part, sib = {}, {}, {}, {}, {}

    def ag(names):
        return [] if shards is None else [("ag", [shards[n] for n in names])]

    def rs(names):
        return [] if shards is None else [("rs", [bigb[n] for n in names])]

    def swap(names):
        return [] if shards is None else [("swap", [part[n] for n in names])]

    def reduce_own(names):
        if shards is not None:
            sums = _sum_parts([big[n] for n in names], [recv[n] for n in names], chip, "sum_" + names[0])
            part.update(zip(names, sums))

    h, qkv_pad, xg, *got = _f_inproj(x, p['g_mix'], gw['w_in'], tmb, ag(AG_IN_INPROJ))
    gw.update(zip(AG_IN_INPROJ, got))
    att, *got = _f_attn(qkv_pad, frow, ag(AG_IN_ATTN))
    gw.update(zip(AG_IN_ATTN, got))
    rec, u, hs, *got = _f_lru(xg, p['conv_w'], p['conv_b'], wrg, p['b_rg'], wig, p['b_ig'], p['lru_L'], tl,
                              ag(AG_IN_LRU))
    gw.update(zip(AG_IN_LRU, got))
    w_out = gw['w_out'].reshape(1024, 1024)
    wq = gw['wq_c'].reshape(1024, 1024)
    wk = gw['wk_c'].reshape(1024, 1024)
    wv = gw['wv_c'].reshape(1024, 1024)
    wo = gw['wo_c'].reshape(1024, 1024)
    mn, kx, vx = _f_mem(mem, p['g_mem'], wk, wv)
    mg, x1, hc, qx, ox, x2, *got = _f_mid(x, att, rec, p['g_out_attn'], p['g_out_lru'], w_out, p['g_cross'],
                                          wq, kx, vx, wo, tmb, ag(AG_IN_MID))
    gw.update(zip(AG_IN_MID, got))
    ffn_w = [gw[n].reshape(D_FF, 1024) for n in ('w_gate', 'w_up', 'w_down')]
    hf, gact, uact, aact, dx3, loss, dg_final = _f_ffn(x2, tgt, p['g_ffn'], p['g_final'], *ffn_w, tmb)

    ts = min(1024, s_len)
    dgact, duact, dx2, dg_ffn = _b_ffn(dx3, x2, gact, uact, p['g_ffn'], *ffn_w, tm)
    big['w_gate'], bigb['w_gate'] = _mm_tn(dgact, hf, "dw_gate", ts)
    big['w_up'], bigb['w_up'] = _mm_tn(duact, hf, "dw_up", ts)
    big['w_down'], bigb['w_down'] = _mm_tn(aact, dx3, "dw_down", ts)
    for n in ('w_gate', 'w_up', 'w_down'):
        big[n] = big[n].reshape(N_SHARD, FF_SH, 1024)
        bigb[n] = bigb[n].reshape(N_SHARD, FF_SH, 1024)

    dqx, dx1, datt, drec, dkx, dvx, dg_cross, dg_oa, dg_ol, *got = _b_mid(
        dx2, qx, x1, att, rec, kx, vx, wo, wq, w_out, p['g_cross'], p['g_out_attn'], p['g_out_lru'], tmb,
        rs(RS_IN_MID))
    recv.update(zip(RS_IN_MID, got))
    reduce_own(RS_IN_MID)
    dwk, dwv, dg_mem, dwkb, dwvb = _b_mem(dkx, dvx, mem, mn, p['g_mem'], wk, wv)
    big['wk_c'], bigb['wk_c'] = dwk, dwkb
    big['wv_c'], bigb['wv_c'] = dwv, dwvb
    big['w_out'], bigb['w_out'] = _mm_tn(mg, dx1, "dw_out", ts)
    big['wq_c'], bigb['wq_c'] = _mm_tn(hc, dqx, "dw_q", ts)
    big['wo_c'], bigb['wo_c'] = _mm_tn(ox, dx2, "dw_o", ts)
    for n in MID:
        big[n] = big[n].reshape(N_SHARD, 256, 1024)
        bigb[n] = bigb[n].reshape(N_SHARD, 256, 1024)

    dxg, dwrg, dwig, dbrg, dbig, dlam, dcw, dcb, *got = _b_lru(
        drec, hs, u, xg, p['conv_w'], wrg, p['b_rg'], wig, p['b_ig'], p['lru_L'], tl,
        rs(RS_IN_LRU) + swap(RS_IN_MID))
    recv.update(zip(RS_IN_LRU, got))
    sib.update(zip(RS_IN_MID, got[len(RS_IN_LRU):]))
    reduce_own(RS_IN_LRU)
    small = {
        'conv_w': dcw, 'conv_b': dcb, 'w_rg': dwrg, 'b_rg': dbrg, 'w_ig': dwig, 'b_ig': dbig, 'lru_L': dlam,
        'g_out_attn': dg_oa, 'g_out_lru': dg_ol, 'g_cross': dg_cross, 'g_mem': dg_mem, 'g_ffn': dg_ffn,
        'g_final': dg_final,
    }
    names = [n for n in SMALL if n in small]
    gather = [] if shards is None else [
        ("ag8", [_pack_small(names, [small[n] for n in names], loss, PACK_ROWS, "pack_small")])]
    dq, dkv, dfrow, *got = _b_attn(qkv_pad, att, datt, frow, rs(RS_IN_ATTN) + swap(RS_IN_LRU) + gather)
    recv.update(zip(RS_IN_ATTN, got))
    sib.update(zip(RS_IN_LRU, got[len(RS_IN_ATTN):]))
    packs = got[-1] if gather else None
    reduce_own(RS_IN_ATTN)
    small['rel_bias'] = _rel_bias_grad_from_dfrow(dfrow)
    big['w_in'], bigb['w_in'] = _b_win(dq, dkv, dxg, h, ts)
    if shards is None:
        grad_x, small['g_mix'] = _b_inproj(dq, dkv, dxg, x, dx1, p['g_mix'], gw['w_in'], tmb)
    else:
        nsw = len(RS_IN_ATTN)

        def copies(refs, send_sems, recv_sems):
            return _tail_copies(refs[0], refs[1], refs[2:2 + nsw], refs[2 + nsw:2 + 2 * nsw], send_sems, recv_sems)

        slots = bigb['w_in']
        bufs = ([slots, lax.empty((3,) + slots.shape[1:], slots.dtype)] + [part[n] for n in RS_IN_ATTN]
                + [lax.empty(part[n].shape, F32) for n in RS_IN_ATTN])
        sems, bufs, token = _split_start("tail_exchange_start", bufs, 3 + nsw, copies)
        grad_x, small['g_mix'] = _b_inproj(dq, dkv, dxg, x, dx1, p['g_mix'] + token[0, 0], gw['w_in'], tmb)
        bufs = _split_wait("tail_exchange_wait", sems, bufs, 3 + nsw, copies, small['g_mix'])
        recv['w_in'] = bufs[1]
        sib.update(zip(RS_IN_ATTN, bufs[2 + nsw:]))
    reduce_own(['w_in'])
    return loss, grad_x, small, big, part, sib, packs


CAST_STEPS = 4


def _cast_shards(ws, name, comm=None):
    def body(*refs):
        n = len(refs) // 2
        for src, dst in zip(refs[:n], refs[n:]):
            dst[...] = src[...].astype(BF16)

    specs = [_rows(w.shape[0] // CAST_STEPS, w.shape[1]) for w in ws]
    return _call(body, name, (CAST_STEPS,), specs, specs, [_sds(w.shape, BF16) for w in ws], [], tuple(ws),
                 "arbitrary", comm)


def _sum_parts(own4s, recv3s, chip, name):
    n = len(own4s)
    _, r, c = own4s[0].shape
    steps = _ew_steps(r, n * c * (4 + 3 * 2 + 4))
    tr = r // steps

    def body(chip_ref, *refs):
        for own_ref, rc_ref, o_ref in zip(refs[:n], refs[n:2 * n], refs[2 * n:]):
            o_ref[...] = ((own_ref[0] + rc_ref[0].astype(F32)) + rc_ref[1].astype(F32)) + rc_ref[2].astype(F32)

    grid_spec = pltpu.PrefetchScalarGridSpec(
        num_scalar_prefetch=1, grid=(steps,),
        in_specs=[pl.BlockSpec((1, tr, c), lambda i, ch: (ch[0], i, 0))] * n
                 + [pl.BlockSpec((3, tr, c), lambda i, ch: (0, i, 0))] * n,
        out_specs=[pl.BlockSpec((tr, c), lambda i, ch: (i, 0))] * n)
    return pl.pallas_call(body, name=name, grid_spec=grid_spec, out_shape=[_sds((r, c), F32)] * n,
                          compiler_params=_cp("parallel"))(chip, *own4s, *recv3s)


def _adamw_math(w, g, m, v):
    m = ADAM_B1 * m + (1.0 - ADAM_B1) * g
    v = ADAM_B2 * v + (1.0 - ADAM_B2) * (g * g)
    m_hat = m / (1.0 - ADAM_B1 ** ADAM_STEP)
    v_hat = v / (1.0 - ADAM_B2 ** ADAM_STEP)
    delta = -ADAM_LR * (m_hat / (jnp.sqrt(v_hat) + ADAM_EPS) + ADAM_WD * w)
    return delta, m, v


def _final_adamw(pas, pbs, ws, ms, vs, name, after=None):
    n = len(ws)
    r, c = ws[0].shape
    steps = _ew_steps(r, n * c * 9 * 4)
    tr = r // steps

    def body(*refs):
        ins, outs = refs[:5 * n], refs[len(refs) - 4 * n:]
        for k in range(n):
            pa_ref, pb_ref, w_ref, m_ref, v_ref = (ins[j * n + k] for j in range(5))
            g = pa_ref[...] + pb_ref[...]
            outs[4 * k][...] = g
            outs[4 * k + 1][...], outs[4 * k + 2][...], outs[4 * k + 3][...] = _adamw_math(
                w_ref[...], g, m_ref[...], v_ref[...])

    order = [] if after is None else [after]
    res = pl.pallas_call(
        body, name=name, grid=(steps,), in_specs=[_rows(tr, c)] * (5 * n) + [_full(t.shape) for t in order],
        out_specs=[_rows(tr, c)] * (4 * n), out_shape=[_sds((r, c), F32)] * (4 * n),
        compiler_params=_cp("parallel"))(*pas, *pbs, *ws, *ms, *vs, *order)
    return [res[4 * k:4 * k + 4] for k in range(n)]


def _pack_put(ref, name, val_ref):
    r = _pack_rows()[name]
    shape = val_ref.shape
    if len(shape) == 3:
        for b in range(shape[0]):
            ref[r:r + shape[1], b * shape[2]:(b + 1) * shape[2]] = val_ref[b]
    elif shape[1] == 2 * PACK_W:
        ref[r:r + 1, :] = val_ref[:, 0:PACK_W]
        ref[r + 1:r + 2, :] = val_ref[:, PACK_W:2 * PACK_W]
    else:
        ref[r:r + shape[0], 0:shape[1]] = val_ref[...]


def _pack_get(ref, name, shape):
    r = _pack_rows()[name]
    if len(shape) == 3:
        return jnp.stack([ref[r:r + shape[1], b * shape[2]:(b + 1) * shape[2]] for b in range(shape[0])])
    if shape[1] == 2 * PACK_W:
        return jnp.concatenate([ref[r:r + 1, :], ref[r + 1:r + 2, :]], axis=1)
    return ref[r:r + shape[0], 0:shape[1]]


def _pack_small(names, g, loss, rows, name):
    n = len(g)
    extra = [] if loss is None else [loss]

    def body(*refs):
        pack = refs[-1]
        pack[...] = jnp.zeros_like(pack)
        for a, nm in enumerate(names):
            _pack_put(pack, nm, refs[a])
        if extra:
            _pack_put(pack, 'loss', refs[n])

    return pl.pallas_call(body, name=name, out_shape=_sds((rows, PACK_W), F32), compiler_params=_cp())(*g, *extra)


def _all_peers():
    x, y, c = _mesh_pos()
    peers = []
    for k in range(1, 8):
        px = 1 - x if k & 4 else x
        py = 1 - y if k & 2 else y
        pc = 1 - c if k & 1 else c
        peers.append(((px, py, pc), 4 * px + 2 * py + pc))
    return peers, 4 * x + 2 * y + c


def _ag8_copies(ins, outs, sems):
    send_sems, recv_sems, loc_sems = sems
    n = len(ins)
    peers, me = _all_peers()

    def remote(k, j, slot):
        return pltpu.make_async_remote_copy(
            src_ref=ins[k], dst_ref=outs[k].at[slot], send_sem=send_sems.at[k, j], recv_sem=recv_sems.at[k, j],
            device_id=peers[j][0], device_id_type=MESH_ID)

    def local(k):
        return pltpu.make_async_copy(ins[k], outs[k].at[me], loc_sems.at[k])

    def start():
        for k in range(n):
            local(k).start()
            for j in range(7):
                remote(k, j, me).start()

    def wait():
        for k in range(n):
            for j in range(7):
                remote(k, j, peers[j][1]).wait_recv()
        for k in range(n):
            for j in range(7):
                remote(k, j, me).wait_send()
            local(k).wait()

    return start, _no_forward, wait


def _adamw_small(packs, late_own, late_packs, g_shapes, loss_shape, w, m, v):
    n = len(w)

    def body(*refs):
        packs_ref, own_ref, late_ref = refs[0], refs[1], refs[2]
        w_refs, m_refs, v_refs = (refs[3 + i * n:3 + (i + 1) * n] for i in range(3))
        o0 = 3 * n + 3
        go, do, mo, vo = (refs[o0 + i * n:o0 + (i + 1) * n] for i in range(4))
        loss_out, tot_ref = refs[o0 + 4 * n], refs[o0 + 4 * n + 1]
        x, y, c = _mesh_pos()
        me = 4 * x + 2 * y + c
        tot = packs_ref[0]
        late = jnp.where(me == 0, own_ref[...], late_ref[0])
        for d in range(1, 8):
            tot = tot + packs_ref[d]
            late = late + jnp.where(me == d, own_ref[...], late_ref[d])
        tot_ref[...] = tot
        tot_ref[0:LATE_ROWS, :] += late
        loss_out[...] = _pack_get(tot_ref, 'loss', loss_shape)
        for a, name in enumerate(SMALL):
            if name == 'conv_w':
                r = _pack_rows()[name]
                ga = tot_ref[r:r + g_shapes[a][0], pl.ds(pl.multiple_of((2 * x + y) * 128, 128), 128)]
            else:
                ga = _pack_get(tot_ref, name, g_shapes[a])
            go[a][...] = ga
            do[a][...], mo[a][...], vo[a][...] = _adamw_math(w_refs[a][...], ga, m_refs[a][...], v_refs[a][...])

    out_shape = [_sds(a.shape, F32) for a in w] * 4 + [_sds(loss_shape, F32)]
    return pl.pallas_call(body, name="adamw_small", out_shape=out_shape,
                          scratch_shapes=[pltpu.VMEM((PACK_ROWS, PACK_W), F32)],
                          compiler_params=_cp())(packs, late_own, late_packs, *w, *m, *v)


PACK_W = 512
PACK_ROWS = 160
LATE = ['g_mix', 'rel_bias']
LATE_ROWS = 32


def _pack_rows():
    rows, r = {}, 0
    for name in ['g_mix', 'g_cross', 'g_mem', 'g_ffn', 'g_final']:
        rows[name] = r
        r += 2
    for name in ['conv_b', 'b_rg', 'b_ig', 'lru_L', 'g_out_attn', 'g_out_lru']:
        rows[name] = r
        r += 1
    rows['conv_w'] = r
    rows['loss'] = r + 4
    rows['rel_bias'] = 24
    rows['w_rg'] = 32
    rows['w_ig'] = 32 + LRU_BLOCK
    assert r + 5 <= 24 and rows['w_ig'] + LRU_BLOCK == PACK_ROWS
    assert rows['g_mix'] + 2 <= LATE_ROWS and rows['rel_bias'] + 8 <= LATE_ROWS
    return rows


INPUT_NAMES = (['x', 'mem'] + WEIGHTS + ['loss_target'] + ['m_' + n for n in WEIGHTS] + ['v_' + n for n in WEIGHTS])


def kernel(x, mem, g_mix, w_in, rel_bias, conv_w, conv_b, w_rg, b_rg, w_ig, b_ig, lru_L, g_out_attn, g_out_lru, w_out, g_cross, g_mem, wq_c, wk_c, wv_c, wo_c, g_ffn, w_gate, w_up, w_down, g_final, loss_target, m_g_mix, m_w_in, m_rel_bias, m_conv_w, m_conv_b, m_w_rg, m_b_rg, m_w_ig, m_b_ig, m_lru_L, m_g_out_attn, m_g_out_lru, m_w_out, m_g_cross, m_g_mem, m_wq_c, m_wk_c, m_wv_c, m_wo_c, m_g_ffn, m_w_gate, m_w_up, m_w_down, m_g_final, v_g_mix, v_w_in, v_rel_bias, v_conv_w, v_conv_b, v_w_rg, v_b_rg, v_w_ig, v_b_ig, v_lru_L, v_g_out_attn, v_g_out_lru, v_w_out, v_g_cross, v_g_mem, v_wq_c, v_wk_c, v_wv_c, v_wo_c, v_g_ffn, v_w_gate, v_w_up, v_w_down, v_g_final):
    a = dict(zip(INPUT_NAMES, (x, mem, g_mix, w_in, rel_bias, conv_w, conv_b, w_rg, b_rg, w_ig, b_ig, lru_L, g_out_attn, g_out_lru, w_out, g_cross, g_mem, wq_c, wk_c, wv_c, wo_c, g_ffn, w_gate, w_up, w_down, g_final, loss_target, m_g_mix, m_w_in, m_rel_bias, m_conv_w, m_conv_b, m_w_rg, m_b_rg, m_w_ig, m_b_ig, m_lru_L, m_g_out_attn, m_g_out_lru, m_w_out, m_g_cross, m_g_mem, m_wq_c, m_wk_c, m_wv_c, m_wo_c, m_g_ffn, m_w_gate, m_w_up, m_w_down, m_g_final, v_g_mix, v_w_in, v_rel_bias, v_conv_w, v_conv_b, v_w_rg, v_b_rg, v_w_ig, v_b_ig, v_lru_L, v_g_out_attn, v_g_out_lru, v_w_out, v_g_cross, v_g_mem, v_wq_c, v_wk_c, v_wv_c, v_wo_c, v_g_ffn, v_w_gate, v_w_up, v_w_down, v_g_final)))
    chip = 2 * lax.axis_index("x") + lax.axis_index("y")

    def shard(name):
        arr = a[name][0]
        base = name[2:] if name[:2] in ('m_', 'v_') else name
        return jnp.swapaxes(arr, 0, 1) if base in TRANSPOSED else arr

    shards = {'w_in': _cast_shards([shard('w_in')], "cast_w_in")[0]}
    rest = [n for n in BIG if n != 'w_in']
    *cast, w_in_g, conv_w_g = _cast_shards([shard(n) for n in rest], "cast_rest",
                                           [("ag", [shards['w_in']]), ("agf", [a['conv_w'][0]])])
    shards.update(zip(rest, cast))
    conv_w_full = conv_w_g.transpose(1, 0, 2).reshape(4, D_LRU)

    p = {n: a[n] for n in SMALL}
    p['rel_bias'] = a['rel_bias'][0]
    p['w_rg'] = a['w_rg'][0]
    p['w_ig'] = a['w_ig'][0]
    p['conv_w'] = conv_w_full
    p['g_final'] = a['g_final'][None, :]
    chip_arr = jnp.reshape(chip, (1,)).astype(jnp.int32)
    loss_part, grad_x, small, _, part, sib, packs = _local_step(
        a['x'][0], a['mem'][0], a['loss_target'][0], p, {'w_in': w_in_g}, shards, chip_arr)

    def late_copies(refs, send_sems, recv_sems):
        return _late_copies(refs[0], refs[1], refs[2], refs[3], send_sems, recv_sems)

    late_pack = _pack_small(LATE, [small[n] for n in LATE], None, LATE_ROWS, "pack_late")
    bufs = [part['w_in'], lax.empty(part['w_in'].shape, F32), late_pack, jnp.zeros((8, LATE_ROWS, PACK_W), F32)]
    sems, bufs, token = _split_start("late_exchange_start", bufs, 8, late_copies)
    out = {}

    def adamw(group, after=None):
        results = _final_adamw([part[n] for n in group], [sib[n] for n in group], [shard(n) for n in group],
                               [shard('m_' + n) for n in group], [shard('v_' + n) for n in group],
                               "adamw_" + group[0], after)
        for n, res in zip(group, results):
            out[n] = [jnp.swapaxes(r, 0, 1) for r in res] if n in TRANSPOSED else res
        return results[-1][0]

    adamw(MID, token)
    done = adamw(['w_gate', 'w_up', 'w_down'], token)
    _, sib['w_in'], late_pack, late_packs = _split_wait("late_exchange_wait", sems, bufs, 8, late_copies, done)
    adamw(['w_in'])

    def natural(arr):
        return arr[0] if arr.ndim >= 3 else (arr[None, :] if arr.ndim == 1 else arr)

    small_out = _adamw_small(packs, late_pack, late_packs, [small[n].shape for n in SMALL],
                             loss_part.shape, *[[natural(a[pre + n]) for n in SMALL] for pre in ('', 'm_', 'v_')])
    ns = len(SMALL)
    loss = small_out[4 * ns][0, 0]

    def leaf(i, n):
        if n in BIG:
            return out[n][i][None]
        return small_out[i * ns + SMALL.index(n)].reshape(a[n].shape)

    return (loss, grad_x[None], *[leaf(i, n) for i in range(4) for n in WEIGHTS])
```

```python
import math

import jax
import jax.numpy as jnp
from jax import lax
from jax.experimental import pallas as pl
from jax.experimental.pallas import tpu as pltpu

F32 = jnp.float32
BF16 = jnp.bfloat16

D_MODEL = 1024
D_ATT = 512
D_LRU = 512
HEAD_DIM = 64
ATT_HEADS = 8
CHUNK = 64
LEFT_CHUNKS = 8
MAX_REL = 128
X_HEADS = 4
X_HEAD_DIM = 256
N_SHARD = 4
IN_SH = 640
D_IN = N_SHARD * IN_SH
FF_SH = 704
D_FF = N_SHARD * FF_SH
EPS = 1e-6
LRU_C = 8.0
LRU_BLOCKS = 8
LRU_BLOCK = 64
QB = 256
KB = 768
ROLL_W = 1024
NEG = -1e30
ATT_SCALE = HEAD_DIM ** -0.5
X_SCALE = X_HEAD_DIM ** -0.5

ADAM_LR = 0.001
ADAM_B1 = 0.9
ADAM_B2 = 0.999
ADAM_EPS = 1e-08
ADAM_WD = 0.01
ADAM_STEP = 10

VMEM_LIMIT_V7X = 56 * 1024 * 1024
BF16_ROWS = 16


EW_VMEM_BUDGET = 40 * 1024 * 1024


def _ew_steps(rows, bytes_per_row):
    return min(s for s in (2, 4, 8, 16) if rows % (s * BF16_ROWS) == 0
               and 2 * (rows // s) * bytes_per_row <= EW_VMEM_BUDGET)
MESH_ID = pl.DeviceIdType.MESH

WEIGHTS = ['g_mix', 'w_in', 'rel_bias', 'conv_w', 'conv_b', 'w_rg', 'b_rg', 'w_ig', 'b_ig', 'lru_L',
           'g_out_attn', 'g_out_lru', 'w_out', 'g_cross', 'g_mem', 'wq_c', 'wk_c', 'wv_c', 'wo_c',
           'g_ffn', 'w_gate', 'w_up', 'w_down', 'g_final']
BIG = ['w_in', 'w_out', 'wq_c', 'wk_c', 'wv_c', 'wo_c', 'w_gate', 'w_up', 'w_down']
SMALL = [n for n in WEIGHTS if n not in BIG]


def _sds(shape, dtype):
    return jax.ShapeDtypeStruct(shape, dtype)


def _cp(*sem):
    return pltpu.CompilerParams(dimension_semantics=sem or None, vmem_limit_bytes=VMEM_LIMIT_V7X)


def _rows(tm, n):
    return pl.BlockSpec((tm, n), lambda i: (i, 0))


def _full(shape):
    nd = len(shape)
    return pl.BlockSpec(shape, lambda i: (0,) * nd)


def _dot(a, b):
    return jnp.dot(a, b, preferred_element_type=F32)


def _dot_nt(a, b):
    return lax.dot_general(a, b, (((1,), (1,)), ((), ())), preferred_element_type=F32)


def _dot_tn(a, b):
    return lax.dot_general(a, b, (((0,), (0,)), ((), ())), preferred_element_type=F32)


def _rinv(x):
    return lax.rsqrt(jnp.mean(x * x, axis=-1, keepdims=True) + EPS)


def _rms_bwd(dy, x, g):
    r = _rinv(x)
    yh = x * r
    dyh = dy * g
    dx = r * (dyh - yh * jnp.mean(dyh * yh, axis=-1, keepdims=True))
    return dx, jnp.sum(dy * yh, axis=0, keepdims=True)


def _gelu(x):
    c = math.sqrt(2.0 / math.pi)
    t = jnp.tanh(c * (x + 0.044715 * x * x * x))
    return 0.5 * x * (1.0 + t)


def _gelu_and_grad(x):
    c = math.sqrt(2.0 / math.pi)
    t = jnp.tanh(c * (x + 0.044715 * x * x * x))
    g = 0.5 * x * (1.0 + t)
    dg = 0.5 * (1.0 + t) + 0.5 * x * (1.0 - t * t) * c * (1.0 + 3.0 * 0.044715 * x * x)
    return g, dg


def _neg_expm1(z):
    series = -z * (1.0 + z * (0.5 + z * ((1.0 / 6.0) + z * (1.0 / 24.0))))
    return jnp.where(z > -0.03, series, 1.0 - jnp.exp(z))


def _lru_gates(u, wrg, brg, wig, big, lam):
    ub = u.astype(BF16)
    r = jax.nn.sigmoid(_dot(ub, wrg) + brg)
    ig = jax.nn.sigmoid(_dot(ub, wig) + big)
    sp = jnp.maximum(-lam, 0.0) + jnp.log1p(jnp.exp(-jnp.abs(lam)))
    la = -LRU_C * r * sp
    a = jnp.exp(la)
    mult = jnp.sqrt(jnp.maximum(_neg_expm1(2.0 * la), 0.0))
    return ub, r, ig, sp, a, mult


def _scan8(a8, b8, hprev):
    row = lax.broadcasted_iota(jnp.int32, a8.shape, 0)
    aa, bb = a8, b8
    for d in (1, 2, 4):
        a_s = pltpu.roll(aa, d, 0)
        b_s = pltpu.roll(bb, d, 0)
        m = row >= d
        bb = jnp.where(m, aa * b_s + bb, bb)
        aa = jnp.where(m, aa * a_s, aa)
    return aa * hprev + bb


def _rscan8(c8, d8, lnext):
    row = lax.broadcasted_iota(jnp.int32, c8.shape, 0)
    cc, dd = c8, d8
    for d in (1, 2, 4):
        c_s = pltpu.roll(cc, 8 - d, 0)
        d_s = pltpu.roll(dd, 8 - d, 0)
        m = row < 8 - d
        dd = jnp.where(m, cc * d_s + dd, dd)
        cc = jnp.where(m, cc * c_s, cc)
    return cc * lnext + dd


def _mesh_pos():
    return lax.axis_index("x"), lax.axis_index("y"), lax.axis_index("c")


def _other_chips(x, y):
    return [(1 - x, y), (x, 1 - y), (1 - x, 1 - y)]


def _no_forward():
    pass


def _ag_full_copies(ins, outs, sems):
    send_sems, recv_sems, loc_sems = sems
    n = len(ins)
    x, y, c = _mesh_pos()
    mine = 2 * x + y
    chips = _other_chips(x, y)

    def remote(k, j, slot):
        px, py = chips[j]
        return pltpu.make_async_remote_copy(
            src_ref=ins[k], dst_ref=outs[k].at[slot], send_sem=send_sems.at[k, j], recv_sem=recv_sems.at[k, j],
            device_id=(px, py, c), device_id_type=MESH_ID)

    def local(k):
        return pltpu.make_async_copy(ins[k], outs[k].at[mine], loc_sems.at[k])

    def start():
        for k in range(n):
            local(k).start()
            for j in range(3):
                remote(k, j, mine).start()

    def wait():
        for k in range(n):
            for j, (px, py) in enumerate(chips):
                remote(k, j, 2 * px + py).wait_recv()
        for k in range(n):
            for j in range(3):
                remote(k, j, mine).wait_send()
            local(k).wait()

    return start, _no_forward, wait


def _ag_copies(ins, outs, sems):
    send_sems, recv_sems, fsend_sems, frecv_sems, loc_sems = sems
    n = len(ins)
    x, y, c = _mesh_pos()
    mine = 2 * x + y
    chips = _other_chips(x, y)

    def half(ref, hc):
        r = ref.shape[0] // 2
        return ref.at[pl.ds(pl.multiple_of(hc * r, 16), r)]

    def ici(k, j, slot):
        px, py = chips[j]
        return pltpu.make_async_remote_copy(
            src_ref=half(ins[k], c), dst_ref=half(outs[k].at[slot], c),
            send_sem=send_sems.at[k, j], recv_sem=recv_sems.at[k, j],
            device_id=(px, py, c), device_id_type=MESH_ID)

    def d2d(k, j, hc):
        px, py = chips[j]
        part = half(outs[k].at[2 * px + py], hc)
        return pltpu.make_async_remote_copy(
            src_ref=part, dst_ref=part, send_sem=fsend_sems.at[k, j], recv_sem=frecv_sems.at[k, j],
            device_id=(x, y, 1 - c), device_id_type=MESH_ID)

    def local(k):
        return pltpu.make_async_copy(ins[k], outs[k].at[mine], loc_sems.at[k])

    def start():
        for k in range(n):
            local(k).start()
            for j in range(3):
                ici(k, j, mine).start()

    def forward():
        for k in range(n):
            for j, (px, py) in enumerate(chips):
                ici(k, j, 2 * px + py).wait_recv()
                d2d(k, j, c).start()

    def wait():
        for k in range(n):
            for j in range(3):
                d2d(k, j, 1 - c).wait_recv()
        for k in range(n):
            for j in range(3):
                d2d(k, j, c).wait_send()
                ici(k, j, mine).wait_send()
            local(k).wait()

    return start, forward, wait


def _rs_copies(ins, outs, sems):
    send_sems, recv_sems = sems
    n = len(ins)
    x, y, c = _mesh_pos()
    chips = _other_chips(x, y)

    def remote(k, j):
        px, py = chips[j]
        return pltpu.make_async_remote_copy(
            src_ref=ins[k].at[2 * px + py], dst_ref=outs[k].at[j],
            send_sem=send_sems.at[k, j], recv_sem=recv_sems.at[k, j],
            device_id=(px, py, c), device_id_type=MESH_ID)

    def start():
        for k in range(n):
            for j in range(3):
                remote(k, j).start()

    def wait():
        for k in range(n):
            for j in range(3):
                remote(k, j).wait_recv()
        for k in range(n):
            for j in range(3):
                remote(k, j).wait_send()

    return start, _no_forward, wait


def _swap_copies(ins, outs, sems):
    send_sems, recv_sems = sems
    x, y, c = _mesh_pos()
    copies = [pltpu.make_async_remote_copy(
        src_ref=ins[k], dst_ref=outs[k], send_sem=send_sems.at[k], recv_sem=recv_sems.at[k],
        device_id=(x, y, 1 - c), device_id_type=MESH_ID) for k in range(len(ins))]

    def start():
        for cp in copies:
            cp.start()

    def wait():
        for cp in copies:
            cp.wait()

    return start, _no_forward, wait


def _comm_plan(groups):
    plan, arrs, shapes, sems = [], [], [], []
    for kind, group in groups:
        k = len(group)
        arrs += group
        per_peer = pltpu.SemaphoreType.DMA((k, 3))
        if kind == "ag":
            shapes += [_sds((N_SHARD,) + w.shape, w.dtype) for w in group]
            gsems = [per_peer] * 4 + [pltpu.SemaphoreType.DMA((k,))]
            maker = _ag_copies
        elif kind == "agf":
            shapes += [_sds((N_SHARD,) + w.shape, w.dtype) for w in group]
            gsems = [per_peer] * 2 + [pltpu.SemaphoreType.DMA((k,))]
            maker = _ag_full_copies
        elif kind == "ag8":
            shapes += [_sds((8,) + g.shape, g.dtype) for g in group]
            gsems = [pltpu.SemaphoreType.DMA((k, 7))] * 2 + [pltpu.SemaphoreType.DMA((k,))]
            maker = _ag8_copies
        elif kind == "rs":
            shapes += [_sds((3,) + g.shape[1:], g.dtype) for g in group]
            gsems = [pltpu.SemaphoreType.DMA((k, 3)), pltpu.SemaphoreType.DMA((k, 3))]
            maker = _rs_copies
        else:
            shapes += [_sds(g.shape, g.dtype) for g in group]
            gsems = [pltpu.SemaphoreType.DMA((k,)), pltpu.SemaphoreType.DMA((k,))]
            maker = _swap_copies
        plan.append((maker, k, len(gsems)))
        sems += gsems
    return plan, arrs, shapes, sems


def _comm_fns(plan, cins, couts, sems):
    fns, a, s = [], 0, 0
    for maker, k, ns in plan:
        fns.append(maker(cins[a:a + k], couts[a:a + k], sems[s:s + ns]))
        a += k
        s += ns

    def start():
        for st, _, _ in fns:
            st()

    def forward():
        for _, fw, _ in fns:
            fw()

    def wait():
        for _, _, wt in fns:
            wt()

    return start, forward, wait


def _call(body, name, grid, in_specs, out_specs, out_shape, scratch, args, sem, comm=None):
    if not comm:
        return pl.pallas_call(body, name=name, grid=grid, in_specs=in_specs, out_specs=out_specs,
                              out_shape=out_shape, scratch_shapes=scratch, compiler_params=_cp(sem))(*args)
    plan, c_arrs, c_shapes, c_sems = _comm_plan(comm)
    k = len(c_arrs)
    n_in, n_out, n_scr = len(in_specs), len(out_specs), len(scratch)
    last = grid[0] - 1
    fwd_step = max(1, (2 * last) // 3)

    def wrapped(*refs):
        ins, cins = refs[:n_in], refs[n_in:n_in + k]
        o0 = n_in + k
        outs, couts = refs[o0:o0 + n_out], refs[o0 + n_out:o0 + n_out + k]
        s0 = o0 + n_out + k
        start, forward, wait = _comm_fns(plan, cins, couts, refs[s0 + n_scr:])
        pl.when(pl.program_id(0) == 0)(start)
        pl.when(pl.program_id(0) == fwd_step)(forward)
        body(*ins, *outs, *refs[s0:s0 + n_scr])
        pl.when(pl.program_id(0) == last)(wait)

    return pl.pallas_call(
        wrapped, name=name, grid=grid, in_specs=list(in_specs) + [_any()] * k,
        out_specs=list(out_specs) + [_any()] * k, out_shape=list(out_shape) + c_shapes,
        scratch_shapes=list(scratch) + c_sems, compiler_params=_cp(sem))(*args, *c_arrs)


def _tail_copies(slots_ref, land_ref, part_refs, sib_refs, send_sems, recv_sems):
    x, y, c = _mesh_pos()
    copies = []
    for j, (px, py) in enumerate(_other_chips(x, y)):
        copies.append(pltpu.make_async_remote_copy(
            src_ref=slots_ref.at[2 * px + py], dst_ref=land_ref.at[j], send_sem=send_sems[j], recv_sem=recv_sems[j],
            device_id=(px, py, c), device_id_type=MESH_ID))
    for k, (p_ref, s_ref) in enumerate(zip(part_refs, sib_refs)):
        copies.append(pltpu.make_async_remote_copy(
            src_ref=p_ref, dst_ref=s_ref, send_sem=send_sems[3 + k], recv_sem=recv_sems[3 + k],
            device_id=(x, y, 1 - c), device_id_type=MESH_ID))
    return copies


def _late_copies(part_ref, sib_ref, pack_ref, packs_ref, send_sems, recv_sems):
    x, y, c = _mesh_pos()
    peers, me = _all_peers()
    copies = [pltpu.make_async_remote_copy(
        src_ref=part_ref, dst_ref=sib_ref, send_sem=send_sems[0], recv_sem=recv_sems[0],
        device_id=(x, y, 1 - c), device_id_type=MESH_ID)]
    for j in range(7):
        copies.append(pltpu.make_async_remote_copy(
            src_ref=pack_ref, dst_ref=packs_ref.at[me], send_sem=send_sems[1 + j], recv_sem=recv_sems[1 + j],
            device_id=peers[j][0], device_id_type=MESH_ID))
    return copies


def _split_start(name, bufs, ncp, make_copies):
    hbm = pl.BlockSpec(memory_space=pltpu.HBM)
    sem = pl.BlockSpec(memory_space=pltpu.SEMAPHORE)
    bufs = [pltpu.with_memory_space_constraint(b, pltpu.HBM) for b in bufs]
    nb = len(bufs)

    def body(*refs):
        for cp in make_copies(refs[:nb], refs[nb:nb + ncp], refs[nb + ncp:nb + 2 * ncp]):
            cp.start()
        refs[-1][...] = jnp.zeros_like(refs[-1])

    out = pl.pallas_call(
        body, name=name,
        out_shape=[pltpu.SemaphoreType.DMA(())] * (2 * ncp) + [pltpu.HBM(b.shape, b.dtype) for b in bufs]
                  + [_sds((8, 128), F32)],
        in_specs=[hbm] * nb, out_specs=[sem] * (2 * ncp) + [hbm] * nb + [pl.BlockSpec(memory_space=pltpu.VMEM)],
        input_output_aliases={i: 2 * ncp + i for i in range(nb)},
        compiler_params=pltpu.CompilerParams(has_side_effects=pltpu.SideEffectType.DATAFLOW_SIDE_EFFECTING),
    )(*bufs)
    return out[:2 * ncp], out[2 * ncp:2 * ncp + nb], out[-1]


def _split_wait(name, sems, bufs, ncp, make_copies, after):
    nb = len(bufs)
    hbm = pl.BlockSpec(memory_space=pltpu.HBM)
    sem = pl.BlockSpec(memory_space=pltpu.SEMAPHORE)

    def body(*refs):
        for cp in make_copies(refs[:nb], refs[nb:nb + ncp], refs[nb + ncp:nb + 2 * ncp]):
            cp.wait_send()
            cp.wait_recv()

    return pl.pallas_call(
        body, name=name, out_shape=[pltpu.HBM(b.shape, b.dtype) for b in bufs],
        in_specs=[hbm] * nb + [sem] * (2 * ncp) + [_any()], out_specs=[hbm] * nb,
        input_output_aliases={i: i for i in range(nb)},
        compiler_params=pltpu.CompilerParams(has_side_effects=pltpu.SideEffectType.DATAFLOW_SIDE_EFFECTING),
    )(*bufs, *sems, after)


def _any():
    return pl.BlockSpec(memory_space=pl.ANY)


def _start_copies(pairs, sems, first=0):
    copies = [pltpu.make_async_copy(src, dst, sems.at[first + i]) for i, (src, dst) in enumerate(pairs)]
    for cp in copies:
        cp.start()
    return copies


def _copy_together(pairs, sems):
    for cp in _start_copies(pairs, sems):
        cp.wait()


def _load_w_in_once(w_hbm, w_ref, sems):
    @pl.when(pl.program_id(0) == 0)
    def _():
        _copy_together([(w_hbm.at[s], w_ref.at[:, pl.ds(s * IN_SH, IN_SH)]) for s in range(N_SHARD)], sems)


def _f_inproj(x, g_mix, w_in_g, tm, comm=None):
    s_len = x.shape[0]
    pad_rows = LEFT_CHUNKS * CHUNK
    npad = pad_rows // tm

    def body(x_ref, g_ref, w_hbm, h_ref, qkv_ref, xg_ref, w_ref, w_sems):
        i = pl.program_id(0)
        _load_w_in_once(w_hbm, w_ref, w_sems)

        @pl.when(i < npad)
        def _():
            qkv_ref[...] = jnp.zeros_like(qkv_ref)

        @pl.when(i >= npad)
        def _():
            xv = x_ref[...]
            h = (xv * _rinv(xv) * g_ref[...]).astype(BF16)
            h_ref[...] = h
            proj = _dot(h, w_ref[...])
            qkv_ref[:, 0:D_ATT] = (proj[:, 0:D_ATT] * ATT_SCALE).astype(BF16)
            qkv_ref[:, D_ATT:3 * D_ATT] = proj[:, D_ATT:3 * D_ATT].astype(BF16)
            xg_ref[...] = proj[:, 3 * D_ATT:D_IN]

    def tok(n):
        return pl.BlockSpec((tm, n), lambda i: (jnp.maximum(i - npad, 0), 0))

    return _call(
        body, "f_inproj", (s_len // tm + npad,),
        [tok(1024), _full((1, 1024)), _any()],
        [tok(1024), _rows(tm, 1536), tok(1024)],
        [_sds((s_len, 1024), BF16), _sds((s_len + pad_rows, 1536), BF16), _sds((s_len, 1024), F32)],
        [pltpu.VMEM((1024, D_IN), BF16), pltpu.SemaphoreType.DMA((N_SHARD,))], (x, g_mix, w_in_g), "arbitrary", comm)


N_BIAS = 3


def _bias_table(frow_ref, bias_sc):
    qa = lax.broadcasted_iota(jnp.int32, (QB, KB), 0) // CHUNK
    kcol = lax.broadcasted_iota(jnp.int32, (QB, KB), 1)
    kb = kcol // CHUNK
    band = jnp.where((kb >= qa) & (kb - qa <= LEFT_CHUNKS), 0.0, NEG).astype(F32)
    for h in range(ATT_HEADS):
        row = jnp.broadcast_to(frow_ref[h:h + 1, :], (QB, ROLL_W))
        toep = pltpu.roll(row, 0, 1, stride=1, stride_axis=0)
        gen = toep[:, 0:KB] + band
        bias_sc[N_BIAS - 1, h] = gen
        for v in range(N_BIAS - 1):
            pad_keys = LEFT_CHUNKS * CHUNK - v * QB
            bias_sc[v, h] = gen + jnp.where(kcol < pad_keys, NEG, 0.0).astype(F32)


def _even_lanes():
    return lax.broadcasted_iota(jnp.int32, (1, 2 * HEAD_DIM), 1) < HEAD_DIM


def _att_probs(qm, kts, bias):
    s = jnp.concatenate([_dot_nt(qm, k) for k in kts], axis=1) + bias
    return jnp.exp(s - jnp.max(s, axis=-1, keepdims=True))


def _att_in_specs(clamp):
    def spec(j, col):
        return pl.BlockSpec((QB, D_ATT), lambda i: (clamp(i) + j, col))
    return [spec(2, 0), spec(0, 1), spec(1, 1), spec(2, 1), spec(0, 2), spec(1, 2), spec(2, 2)]


def _f_attn(qkv_pad, frow, comm=None):
    s_len = qkv_pad.shape[0] - LEFT_CHUNKS * CHUNK
    nb = s_len // QB

    def body(q_ref, k0, k1, k2, v0, v1, v2, frow_ref, o_ref, bias_sc):
        i = pl.program_id(0)

        @pl.when(i == 0)
        def _():
            _bias_table(frow_ref, bias_sc)

        var = jnp.minimum(i, N_BIAS - 1)
        even = _even_lanes()
        for hp in range(ATT_HEADS // 2):
            cs = slice(hp * 2 * HEAD_DIM, (hp + 1) * 2 * HEAD_DIM)
            qt = q_ref[:, cs]
            kts = [k0[:, cs], k1[:, cs], k2[:, cs]]
            vts = [v0[:, cs], v1[:, cs], v2[:, cs]]
            res = []
            for e in range(2):
                keep = even if e == 0 else jnp.logical_not(even)
                pb = _att_probs(jnp.where(keep, qt, 0), kts, bias_sc[var, 2 * hp + e]).astype(BF16)
                r = _dot(pb, jnp.concatenate([jnp.where(keep, v, 1) for v in vts], axis=0))
                res.append(r / pltpu.roll(r, HEAD_DIM, 1))
            o_ref[:, cs] = jnp.where(even, res[0], res[1])

    return _call(
        body, "f_attn", (nb,),
        _att_in_specs(lambda i: i) + [_full((ATT_HEADS, ROLL_W))],
        [_rows(QB, D_ATT)], [_sds((s_len, D_ATT), F32)],
        [pltpu.VMEM((N_BIAS, ATT_HEADS, QB, KB), F32)], (*([qkv_pad] * 7), frow), "arbitrary", comm)


def _f_lru(xg, conv_w, conv_b, wrg, brg, wig, big, lam, tl, comm=None):
    s_len = xg.shape[0]

    def body(xg_ref, cw_ref, cb_ref, wrg_ref, brg_ref, wig_ref, big_ref, l_ref,
             rec_ref, u_ref, hs_ref, xbuf, a_sc, b_sc, hcar):
        i = pl.program_id(0)

        @pl.when(i == 0)
        def _():
            xbuf[0:8, :] = jnp.zeros((8, D_LRU), F32)
            hcar[...] = jnp.zeros((8, D_LRU), F32)

        xu0 = xg_ref[:, 0:D_LRU]
        xbuf[8:8 + tl, :] = xu0
        u = cb_ref[...] + cw_ref[0:1, :] * xbuf[pl.ds(5, tl), :]
        for j in range(1, 4):
            u = u + cw_ref[j:j + 1, :] * xbuf[pl.ds(5 + j, tl), :]
        xbuf[0:8, :] = xu0[tl - 8:tl, :]
        u_ref[...] = u
        _, _, ig, _, a, mult = _lru_gates(u, wrg_ref[...], brg_ref[...], wig_ref[...], big_ref[...], l_ref[...])
        a_sc[...] = a
        b_sc[...] = mult * (ig * u)

        def grp(g, hprev):
            off = pl.multiple_of(g * 8, 8)
            h8 = _scan8(a_sc[pl.ds(off, 8), :], b_sc[pl.ds(off, 8), :], hprev)
            hs_ref[pl.ds(off, 8), :] = h8
            return h8[7:8, :]

        hcar[0:1, :] = lax.fori_loop(0, tl // 8, grp, hcar[0:1, :])
        rec_ref[...] = hs_ref[...] * _gelu(xg_ref[:, D_LRU:2 * D_LRU])

    vec = _full((1, D_LRU))
    return _call(
        body, "f_lru", (s_len // tl,),
        [_rows(tl, 1024), _full((4, D_LRU)), vec, _full((D_LRU, D_LRU)), vec, _full((D_LRU, D_LRU)), vec, vec],
        [_rows(tl, D_LRU)] * 3, [_sds((s_len, D_LRU), F32)] * 3,
        [pltpu.VMEM((tl + 8, D_LRU), F32), pltpu.VMEM((tl, D_LRU), F32),
         pltpu.VMEM((tl, D_LRU), F32), pltpu.VMEM((8, D_LRU), F32)],
        (xg, conv_w, conv_b, wrg, brg, wig, big, lam), "arbitrary", comm)


def _f_mem(mem, g_mem, wk, wv):
    def body(mem_ref, g_ref, wk_ref, wv_ref, mn_ref, kx_ref, vx_ref):
        mv = mem_ref[...]
        mn = (mv * _rinv(mv) * g_ref[...]).astype(BF16)
        mn_ref[...] = mn
        kx_ref[...] = _dot(mn, wk_ref[...]).astype(BF16)
        vx_ref[...] = _dot(mn, wv_ref[...]).astype(BF16)

    m = mem.shape[0]
    return pl.pallas_call(
        body, name="f_mem", out_shape=[_sds((m, 1024), BF16)] * 3,
        compiler_params=_cp())(mem, g_mem, wk, wv)


def _xattn_probs(q, k):
    s = _dot_nt(q, k) * X_SCALE
    m = jnp.max(s, axis=-1, keepdims=True)
    p = jnp.exp(s - m)
    return p, jnp.sum(p, axis=-1, keepdims=True)


def _f_mid(x, att, rec, g_oa, g_ol, w_out, g_cross, wq, kx, vx, wo, tm, comm=None):
    s_len = x.shape[0]
    m_len = kx.shape[0]

    steps = s_len // tm

    def body(x_hbm, att_hbm, rec_hbm, goa_ref, gol_ref, wout_ref, gc_ref, wq_ref, kx_ref, vx_ref, wo_ref,
             mg_ref, x1_ref, hc_ref, qx_ref, ox_ref, x2_ref, xbuf, abuf, rbuf, ring_sems):
        x_ref, att_ref, rec_ref = _ring_tiles([x_hbm, att_hbm, rec_hbm], [xbuf, abuf, rbuf], ring_sems, tm, steps)
        av = att_ref[...]
        rv = rec_ref[...]
        mg_ref[:, 0:D_ATT] = (av * _rinv(av) * goa_ref[...]).astype(BF16)
        mg_ref[:, D_ATT:1024] = (rv * _rinv(rv) * gol_ref[...]).astype(BF16)
        x1 = x_ref[...] + _dot(mg_ref[...], wout_ref[...])
        x1_ref[...] = x1
        hc = (x1 * _rinv(x1) * gc_ref[...]).astype(BF16)
        hc_ref[...] = hc
        qx_ref[...] = _dot(hc, wq_ref[...]).astype(BF16)
        for h in range(X_HEADS):
            sl = slice(h * X_HEAD_DIM, (h + 1) * X_HEAD_DIM)
            p, l = _xattn_probs(qx_ref[:, sl], kx_ref[:, sl])
            ox_ref[:, sl] = (_dot(p.astype(BF16), vx_ref[:, sl]) / l).astype(BF16)
        x2_ref[...] = x1 + _dot(ox_ref[...], wo_ref[...])

    sq = _full((1024, 1024))
    return _call(
        body, "f_mid", (steps,),
        [_any(), _any(), _any(), _full((1, 512)), _full((1, 512)), sq,
         _full((1, 1024)), sq, _full((m_len, 1024)), _full((m_len, 1024)), sq],
        [_rows(tm, 1024)] * 6,
        [_sds((s_len, 1024), BF16), _sds((s_len, 1024), F32), _sds((s_len, 1024), BF16),
         _sds((s_len, 1024), BF16), _sds((s_len, 1024), BF16), _sds((s_len, 1024), F32)],
        [pltpu.VMEM((RING, tm, 1024), x.dtype), pltpu.VMEM((RING, tm, 512), att.dtype),
         pltpu.VMEM((RING, tm, 512), rec.dtype), pltpu.SemaphoreType.DMA((3, RING))],
        (x, att, rec, g_oa, g_ol, w_out, g_cross, wq, kx, vx, wo), "arbitrary", comm)


FF_CHUNKS = [(0, 1280), (1280, D_FF)]


def _first_step_and_rest(step):
    pl.when(pl.program_id(0) == 0)(lambda: step(True))
    pl.when(pl.program_id(0) > 0)(lambda: step(False))


def _ffn_weights(first, pairs, sems):
    if not first:
        return lambda c, j: None
    copies = _start_copies([(hbm.at[c0:c1, :], vmem.at[c0:c1, :]) for c0, c1 in FF_CHUNKS for hbm, vmem in pairs],
                           sems)
    return lambda c, j: copies[c * len(pairs) + j].wait()


def _f_ffn(x2, tgt, g_ffn, g_final, wg, wu, wd, tm):
    s_len = x2.shape[0]

    def body(x2_ref, t_ref, gf_ref, gfin_ref, wg_hbm, wu_hbm, wd_hbm,
             hf_ref, g_ref, u_ref, a_ref, dx3_ref, loss_ref, dgfin_ref, wg_ref, wu_ref, wd_ref, w_sems):
        @pl.when(pl.program_id(0) == 0)
        def _():
            _copy_together([(wg_hbm, wg_ref), (wu_hbm, wu_ref), (wd_hbm, wd_ref)], w_sems)
            loss_ref[...] = jnp.zeros_like(loss_ref)
            dgfin_ref[...] = jnp.zeros_like(dgfin_ref)

        x2v = x2_ref[...]
        hf = (x2v * _rinv(x2v) * gf_ref[...]).astype(BF16)
        hf_ref[...] = hf
        x3 = x2v
        for c0, c1 in FF_CHUNKS:
            gv = _dot_nt(hf, wg_ref[c0:c1, :])
            uv = _dot_nt(hf, wu_ref[c0:c1, :])
            av = (gv * jax.nn.sigmoid(gv) * uv).astype(BF16)
            g_ref[:, c0:c1] = gv.astype(BF16)
            u_ref[:, c0:c1] = uv.astype(BF16)
            a_ref[:, c0:c1] = av
            x3 = x3 + _dot(av, wd_ref[c0:c1, :])
        r3 = _rinv(x3)
        yh = x3 * r3
        gfin = gfin_ref[...]
        err = yh * gfin - t_ref[...]
        loss_ref[...] += jnp.full((1, 128), 0.5 / D_MODEL, F32) * jnp.sum(err * err)
        dy = err * (1.0 / D_MODEL)
        dgfin_ref[...] += jnp.sum(dy * yh, axis=0, keepdims=True)
        dyh = dy * gfin
        dx3_ref[...] = r3 * (dyh - yh * jnp.mean(dyh * yh, axis=-1, keepdims=True))

    vec = _full((1, 1024))
    return pl.pallas_call(
        body, name="f_ffn", grid=(s_len // tm,),
        in_specs=[_rows(tm, 1024), _rows(tm, 1024), vec, vec, _any(), _any(), _any()],
        out_specs=[_rows(tm, 1024), _rows(tm, D_FF), _rows(tm, D_FF), _rows(tm, D_FF),
                   _rows(tm, 1024), _full((1, 128)), vec],
        out_shape=[_sds((s_len, 1024), BF16)] + [_sds((s_len, D_FF), BF16)] * 3
                  + [_sds((s_len, 1024), F32), _sds((1, 128), F32), _sds((1, 1024), F32)],
        scratch_shapes=[pltpu.VMEM((D_FF, 1024), BF16)] * 3 + [pltpu.SemaphoreType.DMA((3,))],
        compiler_params=_cp("arbitrary"))(x2, tgt, g_ffn, g_final, wg, wu, wd)


def _b_ffn(dx3, x2, gact, uact, g_ffn, wg, wu, wd, tm):
    s_len = x2.shape[0]

    def body(dx3_ref, x2_ref, g_ref, u_ref, gf_ref, wg_hbm, wu_hbm, wd_hbm,
             dg_ref, du_ref, dx2_ref, dgf_ref, wg_ref, wu_ref, wd_ref, w_sems):
        def step(first):
            if first:
                dgf_ref[...] = jnp.zeros_like(dgf_ref)
            ready = _ffn_weights(first, [(wd_hbm, wd_ref), (wg_hbm, wg_ref), (wu_hbm, wu_ref)], w_sems)
            dx3v = dx3_ref[...]
            dx3b = dx3v.astype(BF16)
            dhf = jnp.zeros(dx3v.shape, F32)
            for c, (c0, c1) in enumerate(FF_CHUNKS):
                ready(c, 0)
                da = _dot_nt(dx3b, wd_ref[c0:c1, :])
                gv = g_ref[:, c0:c1].astype(F32)
                uv = u_ref[:, c0:c1].astype(F32)
                sg = jax.nn.sigmoid(gv)
                dub = (da * gv * sg).astype(BF16)
                dgb = (da * uv * (sg * (1.0 + gv * (1.0 - sg)))).astype(BF16)
                du_ref[:, c0:c1] = dub
                dg_ref[:, c0:c1] = dgb
                ready(c, 1)
                ready(c, 2)
                dhf = dhf + _dot(dgb, wg_ref[c0:c1, :]) + _dot(dub, wu_ref[c0:c1, :])
            dx, dgf = _rms_bwd(dhf, x2_ref[...], gf_ref[...])
            dx2_ref[...] = dx3v + dx
            dgf_ref[...] += dgf

        _first_step_and_rest(step)

    vec = _full((1, 1024))
    return pl.pallas_call(
        body, name="b_ffn", grid=(s_len // tm,),
        in_specs=[_rows(tm, 1024), _rows(tm, 1024), _rows(tm, D_FF), _rows(tm, D_FF), vec,
                  _any(), _any(), _any()],
        out_specs=[_rows(tm, D_FF), _rows(tm, D_FF), _rows(tm, 1024), vec],
        out_shape=[_sds((s_len, D_FF), BF16)] * 2 + [_sds((s_len, 1024), F32), _sds((1, 1024), F32)],
        scratch_shapes=[pltpu.VMEM((D_FF, 1024), BF16)] * 3 + [pltpu.SemaphoreType.DMA((3 * len(FF_CHUNKS),))],
        compiler_params=_cp("arbitrary"))(dx3, x2, gact, uact, g_ffn, wg, wu, wd)


def _b_mid(dx2, qx, x1, att, rec, kx, vx, wo, wq, w_out, g_cross, g_oa, g_ol, tm, comm=None):
    s_len = x1.shape[0]
    m_len = kx.shape[0]

    def body(dx2_ref, qx_ref, x1_ref, att_ref, rec_ref, kx_ref, vx_ref, wo_ref, wq_ref, wout_ref,
             gc_ref, goa_ref, gol_ref,
             dqx_ref, dx1_ref, datt_ref, drec_ref, dkx_ref, dvx_ref, dgc_ref, dgoa_ref, dgol_ref):
        @pl.when(pl.program_id(0) == 0)
        def _():
            for r in (dkx_ref, dvx_ref, dgc_ref, dgoa_ref, dgol_ref):
                r[...] = jnp.zeros_like(r)

        dx2v = dx2_ref[...]
        dox = _dot_nt(dx2v.astype(BF16), wo_ref[...])
        for h in range(X_HEADS):
            sl = slice(h * X_HEAD_DIM, (h + 1) * X_HEAD_DIM)
            q = qx_ref[:, sl]
            p, l = _xattn_probs(q, kx_ref[:, sl])
            pn = p * (1.0 / l)
            dob = dox[:, sl].astype(BF16)
            dp = _dot_nt(dob, vx_ref[:, sl])
            dvx_ref[:, sl] += _dot_tn(pn.astype(BF16), dob)
            ds = pn * (dp - jnp.sum(dp * pn, axis=-1, keepdims=True))
            dsb = (ds * X_SCALE).astype(BF16)
            dqx_ref[:, sl] = _dot(dsb, kx_ref[:, sl]).astype(BF16)
            dkx_ref[:, sl] += _dot_tn(dsb, q)
        dhc = _dot_nt(dqx_ref[...], wq_ref[...])
        dx, dgc = _rms_bwd(dhc, x1_ref[...], gc_ref[...])
        dx1 = dx2v + dx
        dx1_ref[...] = dx1
        dgc_ref[...] += dgc
        dmg = _dot_nt(dx1.astype(BF16), wout_ref[...])
        da, dgoa = _rms_bwd(dmg[:, 0:D_ATT], att_ref[...], goa_ref[...])
        datt_ref[...] = da
        dgoa_ref[...] += dgoa
        dr, dgol = _rms_bwd(dmg[:, D_ATT:1024], rec_ref[...], gol_ref[...])
        drec_ref[...] = dr
        dgol_ref[...] += dgol

    sq = _full((1024, 1024))
    mk = _full((m_len, 1024))
    return _call(
        body, "b_mid", (s_len // tm,),
        [_rows(tm, 1024), _rows(tm, 1024), _rows(tm, 1024), _rows(tm, 512), _rows(tm, 512), mk, mk,
         sq, sq, sq, _full((1, 1024)), _full((1, 512)), _full((1, 512))],
        [_rows(tm, 1024), _rows(tm, 1024), _rows(tm, 512), _rows(tm, 512), mk, mk,
         _full((1, 1024)), _full((1, 512)), _full((1, 512))],
        [_sds((s_len, 1024), BF16), _sds((s_len, 1024), F32), _sds((s_len, 512), F32),
         _sds((s_len, 512), F32), _sds((m_len, 1024), F32), _sds((m_len, 1024), F32),
         _sds((1, 1024), F32), _sds((1, 512), F32), _sds((1, 512), F32)],
        [], (dx2, qx, x1, att, rec, kx, vx, wo, wq, w_out, g_cross, g_oa, g_ol), "arbitrary", comm)


def _b_mem(dkx, dvx, mem, mn, g_mem, wk, wv):
    def body(dkx_ref, dvx_ref, mem_ref, mn_ref, g_ref, wk_ref, wv_ref, dwk_ref, dwv_ref, dgm_ref,
             dwkb_ref, dwvb_ref):
        dkb = dkx_ref[...].astype(BF16)
        dvb = dvx_ref[...].astype(BF16)
        dwk = _dot_tn(mn_ref[...], dkb)
        dwv = _dot_tn(mn_ref[...], dvb)
        dwk_ref[...] = dwk
        dwv_ref[...] = dwv
        dwkb_ref[...] = dwk.astype(BF16)
        dwvb_ref[...] = dwv.astype(BF16)
        dmn = _dot_nt(dkb, wk_ref[...]) + _dot_nt(dvb, wv_ref[...])
        mv = mem_ref[...]
        dgm_ref[...] = jnp.sum(dmn * (mv * _rinv(mv)), axis=0, keepdims=True)

    return pl.pallas_call(
        body, name="b_mem",
        out_shape=[_sds((1024, 1024), F32), _sds((1024, 1024), F32), _sds((1, 1024), F32),
                   _sds((1024, 1024), BF16), _sds((1024, 1024), BF16)],
        compiler_params=_cp())(dkx, dvx, mem, mn, g_mem, wk, wv)


def _b_lru(drec, hs, u, xg, conv_w, wrg, brg, wig, big, lam, tl, comm=None):
    s_len = xg.shape[0]
    nt = s_len // tl

    def body(drec_ref, hs_ref, hsp_ref, u_ref, xg_ref, cw_ref, wrg_ref, brg_ref, wig_ref, big_ref, l_ref,
             dxg_ref, dwrg_ref, dwig_ref, dbrg_ref, dbig_ref, dlam_ref, dcw_ref, dcb_ref,
             hbuf, abuf, dubuf, c_sc, d_sc, lam_sc, lcar, wacc_r, wacc_i):
        i = pl.program_id(0)
        tt = nt - 1 - i

        @pl.when(i == 0)
        def _():
            for r in (wacc_r, wacc_i, dbrg_ref, dbig_ref, dlam_ref, dcw_ref, dcb_ref):
                r[...] = jnp.zeros_like(r)
            abuf[tl:tl + 8, :] = jnp.zeros((8, D_LRU), F32)
            dubuf[tl:tl + 8, :] = jnp.zeros((8, D_LRU), F32)
            lcar[...] = jnp.zeros((8, D_LRU), F32)

        xu0 = xg_ref[:, 0:D_LRU]
        hsv = hs_ref[...]
        uv = u_ref[...]
        hbuf[8:8 + tl, :] = hsv
        hbuf[0:8, :] = jnp.where(tt > 0, hsp_ref[...], 0.0)
        hshift = hbuf[pl.ds(7, tl), :]
        wrg_v = wrg_ref[...]
        wig_v = wig_ref[...]
        lamv = l_ref[...]
        ub, r, ig, sp, a, mult = _lru_gates(uv, wrg_v, brg_ref[...], wig_v, big_ref[...], lamv)
        abuf[0:tl, :] = a
        c_sc[...] = abuf[pl.ds(1, tl), :]
        gel, dgel = _gelu_and_grad(xg_ref[:, D_LRU:2 * D_LRU])
        drv = drec_ref[...]
        d_sc[...] = drv * gel
        dxg_ref[:, D_LRU:2 * D_LRU] = (drv * hsv * dgel).astype(BF16)

        def grp(k, lnext):
            off = pl.multiple_of((tl // 8 - 1 - k) * 8, 8)
            l8 = _rscan8(c_sc[pl.ds(off, 8), :], d_sc[pl.ds(off, 8), :], lnext)
            lam_sc[pl.ds(off, 8), :] = l8
            return l8[0:1, :]

        lcar[0:1, :] = lax.fori_loop(0, tl // 8, grp, lcar[0:1, :])
        abuf[tl:tl + 8, :] = a[0:8, :]
        db = lam_sc[...]
        da = db * hshift
        dmult = db * (ig * uv)
        dig = db * mult * uv
        du = db * mult * ig
        dla = da * a - dmult * (a * a) / mult
        dlam_ref[...] += jnp.sum(dla * (-LRU_C) * r, axis=0, keepdims=True)
        dzr = dla * (-LRU_C * sp) * r * (1.0 - r)
        dzi = dig * ig * (1.0 - ig)
        dzrb = dzr.astype(BF16)
        dzib = dzi.astype(BF16)
        du = du + _dot_nt(dzrb, wrg_v) + _dot_nt(dzib, wig_v)
        wacc_r[...] += _dot_tn(ub, dzrb)
        wacc_i[...] += _dot_tn(ub, dzib)
        dbrg_ref[...] += jnp.sum(dzr, axis=0, keepdims=True)
        dbig_ref[...] += jnp.sum(dzi, axis=0, keepdims=True)
        dcb_ref[...] += jnp.sum(du, axis=0, keepdims=True)
        dubuf[0:tl, :] = du
        dxu0 = jnp.zeros((tl, D_LRU), F32)
        for j in range(4):
            dsh = dubuf[pl.ds(3 - j, tl), :]
            dxu0 = dxu0 + cw_ref[j:j + 1, :] * dsh
            dcw_ref[j:j + 1, :] += jnp.sum(xu0 * dsh, axis=0, keepdims=True)
        dubuf[tl:tl + 8, :] = du[0:8, :]
        dxg_ref[:, 0:D_LRU] = dxu0.astype(BF16)

        @pl.when(i == nt - 1)
        def _():
            dlam_ref[...] = dlam_ref[...] * (-jax.nn.sigmoid(-lamv))
            for n in range(LRU_BLOCKS):
                blk = slice(n * LRU_BLOCK, (n + 1) * LRU_BLOCK)
                dwrg_ref[n] = wacc_r[blk, blk]
                dwig_ref[n] = wacc_i[blk, blk]

    def rev(n):
        return pl.BlockSpec((tl, n), lambda i: (nt - 1 - i, 0))

    prev8 = pl.BlockSpec((8, D_LRU), lambda i: (jnp.maximum((nt - 1 - i) * (tl // 8) - 1, 0), 0))
    vec = _full((1, D_LRU))
    sq = _full((D_LRU, D_LRU))
    blocks_shape = (LRU_BLOCKS, LRU_BLOCK, LRU_BLOCK)
    blocks = _full(blocks_shape)
    return _call(
        body, "b_lru", (nt,),
        [rev(D_LRU), rev(D_LRU), prev8, rev(D_LRU), rev(1024), _full((4, D_LRU)), sq, vec, sq, vec, vec],
        [rev(1024), blocks, blocks, vec, vec, vec, _full((4, D_LRU)), vec],
        [_sds((s_len, 1024), BF16), _sds(blocks_shape, F32), _sds(blocks_shape, F32),
         _sds((1, D_LRU), F32), _sds((1, D_LRU), F32), _sds((1, D_LRU), F32),
         _sds((4, D_LRU), F32), _sds((1, D_LRU), F32)],
        [pltpu.VMEM((tl + 8, D_LRU), F32)] * 3 + [pltpu.VMEM((tl, D_LRU), F32)] * 3
        + [pltpu.VMEM((8, D_LRU), F32)] + [pltpu.VMEM((D_LRU, D_LRU), F32)] * 2,
        (drec, hs, hs, u, xg, conv_w, wrg, brg, wig, big, lam), "arbitrary", comm)


def _b_attn(qkv_pad, att, datt, frow, comm=None):
    s_len = datt.shape[0]
    nb = s_len // QB
    n_pair = ATT_HEADS // 2
    pair_w = 2 * HEAD_DIM

    def body(q_ref, k0, k1, k2, v0, v1, v2, o_ref, do_ref, frow_ref, dq_ref, dkv_ref, dfrow_ref,
             bias_sc, dt_sc, acc_sc):
        t = pl.program_id(0)

        @pl.when(t == 0)
        def _():
            _bias_table(frow_ref, bias_sc)
            dt_sc[...] = jnp.zeros_like(dt_sc)
            acc_sc[...] = jnp.zeros_like(acc_sc)

        @pl.when(t < nb)
        def _():
            var = jnp.minimum(t, N_BIAS - 1)
            even = _even_lanes()
            for hp in range(n_pair):
                cs = slice(hp * pair_w, (hp + 1) * pair_w)
                qt = q_ref[:, cs]
                kts = [k0[:, cs], k1[:, cs], k2[:, cs]]
                vts = [v0[:, cs], v1[:, cs], v2[:, cs]]
                kcat = jnp.concatenate(kts, axis=0)
                dot = do_ref[:, cs]
                dd = dot * o_ref[:, cs]
                dos_pair, dsbs, pbs, dqs = None, [], [], []
                for e in range(2):
                    keep = even if e == 0 else jnp.logical_not(even)
                    qm = jnp.where(keep, qt, 0)
                    p = _att_probs(qm, kts, bias_sc[var, 2 * hp + e])
                    inv = 1.0 / jnp.sum(p, axis=-1, keepdims=True)
                    dos = jnp.where(keep, dot * inv, 0.0)
                    delta = jnp.sum(jnp.where(keep, dd, 0.0), axis=-1, keepdims=True) * inv
                    dp = jnp.concatenate([_dot_nt(dos.astype(BF16), v) for v in vts], axis=1)
                    ds = p * (dp - delta)
                    dt_sc[2 * hp + e] += ds
                    dsb = ds.astype(BF16)
                    dq = _dot(dsb, kcat)
                    dqs.append(dq)
                    dsbs.append(dsb)
                    pbs.append(p.astype(BF16))
                    dos_pair = dos if e == 0 else dos_pair + dos
                dq_ref[:, cs] = (jnp.where(even, dqs[0], dqs[1]) * ATT_SCALE).astype(BF16)
                qtt = qt.astype(F32).T.astype(BF16)
                dost = dos_pair.T.astype(BF16)
                for j in range(3):
                    slot = (t + 1 + j) % 3
                    js = slice(j * QB, (j + 1) * QB)
                    for e in range(2):
                        hr = slice(e * HEAD_DIM, (e + 1) * HEAD_DIM)
                        acc_sc[slot, hp, hr, :] += _dot(qtt[hr], dsbs[e][:, js])
                        acc_sc[slot, n_pair + hp, hr, :] += _dot(dost[hr], pbs[e][:, js])

        done = (t + 1) % 3

        @pl.when(t >= 2)
        def _():
            for i in range(2 * n_pair):
                dkv_ref[:, i * pair_w:(i + 1) * pair_w] = acc_sc[done, i].T.astype(BF16)

        acc_sc[done] = jnp.zeros((2 * n_pair, pair_w, QB), F32)

        @pl.when(t == nb + 1)
        def _():
            row = lax.broadcasted_iota(jnp.int32, (8, ROLL_W), 0)
            pad = jnp.zeros((8, ROLL_W - KB), F32)
            for h in range(ATT_HEADS):
                acc8 = jnp.concatenate([dt_sc[h, 0:8, :], pad], axis=1)
                for a1 in range(1, QB // 8):
                    blk = jnp.concatenate([dt_sc[h, 8 * a1:8 * a1 + 8, :], pad], axis=1)
                    acc8 = acc8 + pltpu.roll(blk, ROLL_W - 8 * a1, 1)
                for k in range(3):
                    acc8 = jnp.where(((row >> k) & 1) == 1, pltpu.roll(acc8, ROLL_W - (1 << k), 1), acc8)
                dfrow_ref[h:h + 1, :] = jnp.sum(acc8, axis=0, keepdims=True)

    clamp = lambda t: jnp.minimum(t, nb - 1)
    qrows = pl.BlockSpec((QB, D_ATT), lambda t: (clamp(t), 0))
    return _call(
        body, "b_attn", (nb + 2,),
        _att_in_specs(clamp) + [qrows, qrows, _full((ATT_HEADS, ROLL_W))],
        [qrows, pl.BlockSpec((QB, 2 * D_ATT), lambda t: (jnp.maximum(t - 2, 0), 0)),
         _full((ATT_HEADS, ROLL_W))],
        [_sds((s_len, D_ATT), BF16), _sds((s_len, 2 * D_ATT), BF16), _sds((ATT_HEADS, ROLL_W), F32)],
        [pltpu.VMEM((N_BIAS, ATT_HEADS, QB, KB), F32), pltpu.VMEM((ATT_HEADS, QB, KB), F32),
         pltpu.VMEM((3, 2 * n_pair, pair_w, QB), F32)],
        (*([qkv_pad] * 7), att, datt, frow), "arbitrary", comm)


def _b_win(dq, dkv, dxg, h, ts):
    s_len = h.shape[0]
    steps = s_len // ts

    def body(dq_ref, dkv_ref, dxg_ref, h_ref, dw_hbm, dwb_hbm, acc, accb, sems):
        @pl.when(pl.program_id(0) == 0)
        def _():
            acc[...] = jnp.zeros_like(acc)

        @pl.when(pl.program_id(0) < steps - 1)
        def _():
            dproj = jnp.concatenate([dq_ref[...], dkv_ref[...], dxg_ref[...]], axis=1)
            acc[...] += _dot_tn(h_ref[...], dproj)

        @pl.when(pl.program_id(0) == steps - 1)
        def _():
            dproj = jnp.concatenate([dq_ref[...], dkv_ref[...], dxg_ref[...]], axis=1)
            copies = []
            for s in range(N_SHARD):
                cols = slice(s * IN_SH, (s + 1) * IN_SH)
                total = acc[:, cols] + _dot_tn(h_ref[...], dproj[:, cols])
                acc[:, cols] = total
                accb[:, cols] = total.astype(BF16)
                copies += _start_copies([(acc.at[:, cols], dw_hbm.at[s]), (accb.at[:, cols], dwb_hbm.at[s])],
                                        sems, 2 * s)
            for cp in copies:
                cp.wait()

    shape = (N_SHARD, 1024, IN_SH)
    return pl.pallas_call(
        body, name="b_win", grid=(steps,),
        in_specs=[_rows(ts, 512), _rows(ts, 1024), _rows(ts, 1024), _rows(ts, 1024)],
        out_specs=[_any()] * 2, out_shape=[_sds(shape, F32), _sds(shape, BF16)],
        scratch_shapes=[pltpu.VMEM((1024, D_IN), F32), pltpu.VMEM((1024, D_IN), BF16),
                        pltpu.SemaphoreType.DMA((2 * N_SHARD,))],
        compiler_params=_cp("arbitrary"))(dq, dkv, dxg, h)


RING = 3


def _ring_tiles(hbm_refs, bufs, sems, tm, steps):
    i = pl.program_id(0)

    def copies(step):
        slot = step % RING
        return [pltpu.make_async_copy(h.at[pl.ds(step * tm, tm), :], b.at[slot], sems.at[k, slot])
                for k, (h, b) in enumerate(zip(hbm_refs, bufs))]

    @pl.when(i == 0)
    def _():
        for s in range(min(RING - 1, steps)):
            for cp in copies(s):
                cp.start()

    @pl.when(i + RING - 1 < steps)
    def _():
        for cp in copies(i + RING - 1):
            cp.start()

    for cp in copies(i):
        cp.wait()
    return [b.at[i % RING] for b in bufs]


def _b_inproj(dq, dkv, dxg, x, dx1, g_mix, w_in_g, tm, comm=None):
    s_len = x.shape[0]
    steps = s_len // tm

    def body(dq_ref, dkv_ref, dxg_ref, x_hbm, dx1_hbm, g_ref, w_hbm, gx_ref, dgm_ref, w_ref, w_sems,
             xbuf, dx1buf, ring_sems):
        _load_w_in_once(w_hbm, w_ref, w_sems)

        @pl.when(pl.program_id(0) == 0)
        def _():
            dgm_ref[...] = jnp.zeros_like(dgm_ref)

        x_ref, dx1_ref = _ring_tiles([x_hbm, dx1_hbm], [xbuf, dx1buf], ring_sems, tm, steps)
        dproj = jnp.concatenate([dq_ref[...], dkv_ref[...], dxg_ref[...]], axis=1)
        dh = _dot_nt(dproj, w_ref[...])
        dx, dgm = _rms_bwd(dh, x_ref[...], g_ref[...])
        gx_ref[...] = dx1_ref[...] + dx
        dgm_ref[...] += dgm

    return _call(
        body, "b_inproj", (steps,),
        [_rows(tm, 512), _rows(tm, 1024), _rows(tm, 1024), _any(), _any(), _full((1, 1024)), _any()],
        [_rows(tm, 1024), _full((1, 1024))],
        [_sds((s_len, 1024), F32), _sds((1, 1024), F32)],
        [pltpu.VMEM((1024, D_IN), BF16), pltpu.SemaphoreType.DMA((N_SHARD,)),
         pltpu.VMEM((RING, tm, 1024), F32), pltpu.VMEM((RING, tm, 1024), F32), pltpu.SemaphoreType.DMA((2, RING))],
        (dq, dkv, dxg, x, dx1, g_mix, w_in_g), "arbitrary", comm)


MXU_DIM_V7X = 256
FLUSH_GROUPS = 4


def _mm_tn(xa, ya, name, ts):
    s_len, k = xa.shape
    n = ya.shape[1]

    steps = s_len // ts
    tiles = k // MXU_DIM_V7X
    edges = [MXU_DIM_V7X * ((tiles * g) // FLUSH_GROUPS) for g in range(FLUSH_GROUPS + 1)]

    def body(x_ref, y_ref, o_hbm, ob_hbm, acc, accb, sems):
        @pl.when(pl.program_id(0) == 0)
        def _():
            acc[...] = jnp.zeros_like(acc)

        @pl.when(pl.program_id(0) < steps - 1)
        def _():
            acc[...] += _dot_tn(x_ref[...].astype(BF16), y_ref[...].astype(BF16))

        @pl.when(pl.program_id(0) == steps - 1)
        def _():
            yb = y_ref[...].astype(BF16)
            copies = []
            for g in range(FLUSH_GROUPS):
                rows = slice(edges[g], edges[g + 1])
                total = acc[rows, :] + _dot_tn(x_ref[:, rows].astype(BF16), yb)
                acc[rows, :] = total
                accb[rows, :] = total.astype(BF16)
                copies += _start_copies([(acc.at[rows, :], o_hbm.at[rows, :]), (accb.at[rows, :], ob_hbm.at[rows, :])],
                                        sems, 2 * g)
            for cp in copies:
                cp.wait()

    return pl.pallas_call(
        body, name=name, grid=(steps,), in_specs=[_rows(ts, k), _rows(ts, n)],
        out_specs=[_any()] * 2, out_shape=[_sds((k, n), F32), _sds((k, n), BF16)],
        scratch_shapes=[pltpu.VMEM((k, n), F32), pltpu.VMEM((k, n), BF16),
                        pltpu.SemaphoreType.DMA((2 * FLUSH_GROUPS,))],
        compiler_params=_cp("arbitrary"))(xa, ya)


PAD_KEYS = LEFT_CHUNKS * CHUNK
F_HI = PAD_KEYS - MAX_REL + 1
F_LO = PAD_KEYS + MAX_REL


def _frow_from_rel_bias(rb):
    last = rb[:, 2 * MAX_REL:2 * MAX_REL + 1]
    hi = jnp.broadcast_to(last, (ATT_HEADS, F_HI))
    mid = rb[:, 1:2 * MAX_REL][:, ::-1]
    lo = jnp.broadcast_to(rb[:, 0:1], (ATT_HEADS, KB - F_LO))
    wrap = jnp.broadcast_to(last, (ATT_HEADS, ROLL_W - KB))
    return jnp.concatenate([hi, mid, lo, wrap], axis=1)


def _rel_bias_grad_from_dfrow(df):
    g_last = jnp.sum(df[:, 0:F_HI], axis=1, keepdims=True) + jnp.sum(df[:, KB:ROLL_W], axis=1, keepdims=True)
    mid = df[:, F_HI:F_LO][:, ::-1]
    g_first = jnp.sum(df[:, F_LO:KB], axis=1, keepdims=True)
    return jnp.concatenate([g_first, mid, g_last], axis=1)


def _block_diag(w):
    eye = jnp.eye(8, dtype=w.dtype)
    return (w[:, :, None, :] * eye[:, None, :, None]).reshape(D_LRU, D_LRU)


MID = ['w_out', 'wq_c', 'wk_c', 'wv_c', 'wo_c']
TRANSPOSED = ['w_gate', 'w_up']
AG_IN_INPROJ = ['w_out', 'wq_c', 'wk_c']
AG_IN_ATTN = ['wv_c', 'wo_c', 'w_gate']
AG_IN_LRU = ['w_up']
AG_IN_MID = ['w_down']
RS_IN_MID = ['w_gate', 'w_up']
RS_IN_LRU = ['w_down']
RS_IN_ATTN = MID


def _local_step(x, mem, tgt, p, gw, shards=None, chip=None):
    s_len = x.shape[0]
    tm = min(256, s_len)
    tmb = min(512, s_len)
    tl = min(512, s_len)
    frow = _frow_from_rel_bias(p['rel_bias'])
    wrg = _block_diag(p['w_rg']).astype(BF16)
    wig = _block_diag(p['w_ig']).astype(BF16)
    gw = dict(gw)

    big, bigb, recv, part, sib = {}, {}, {}, {}, {}

    def ag(names):
        return [] if shards is None else [("ag", [shards[n] for n in names])]

    def rs(names):
        return [] if shards is None else [("rs", [bigb[n] for n in names])]

    def swap(names):
        return [] if shards is None else [("swap", [part[n] for n in names])]

    def reduce_own(names):
        if shards is not None:
            sums = _sum_parts([big[n] for n in names], [recv[n] for n in names], chip, "sum_" + names[0])
            part.update(zip(names, sums))

    h, qkv_pad, xg, *got = _f_inproj(x, p['g_mix'], gw['w_in'], tmb, ag(AG_IN_INPROJ))
    gw.update(zip(AG_IN_INPROJ, got))
    att, *got = _f_attn(qkv_pad, frow, ag(AG_IN_ATTN))
    gw.update(zip(AG_IN_ATTN, got))
    rec, u, hs, *got = _f_lru(xg, p['conv_w'], p['conv_b'], wrg, p['b_rg'], wig, p['b_ig'], p['lru_L'], tl,
                              ag(AG_IN_LRU))
    gw.update(zip(AG_IN_LRU, got))
    w_out = gw['w_out'].reshape(1024, 1024)
    wq = gw['wq_c'].reshape(1024, 1024)
    wk = gw['wk_c'].reshape(1024, 1024)
    wv = gw['wv_c'].reshape(1024, 1024)
    wo = gw['wo_c'].reshape(1024, 1024)
    mn, kx, vx = _f_mem(mem, p['g_mem'], wk, wv)
    mg, x1, hc, qx, ox, x2, *got = _f_mid(x, att, rec, p['g_out_attn'], p['g_out_lru'], w_out, p['g_cross'],
                                          wq, kx, vx, wo, tmb, ag(AG_IN_MID))
    gw.update(zip(AG_IN_MID, got))
    ffn_w = [gw[n].reshape(D_FF, 1024) for n in ('w_gate', 'w_up', 'w_down')]
    hf, gact, uact, aact, dx3, loss, dg_final = _f_ffn(x2, tgt, p['g_ffn'], p['g_final'], *ffn_w, tmb)

    ts = min(1024, s_len)
    dgact, duact, dx2, dg_ffn = _b_ffn(dx3, x2, gact, uact, p['g_ffn'], *ffn_w, tm)
    big['w_gate'], bigb['w_gate'] = _mm_tn(dgact, hf, "dw_gate", ts)
    big['w_up'], bigb['w_up'] = _mm_tn(duact, hf, "dw_up", ts)
    big['w_down'], bigb['w_down'] = _mm_tn(aact, dx3, "dw_down", ts)
    for n in ('w_gate', 'w_up', 'w_down'):
        big[n] = big[n].reshape(N_SHARD, FF_SH, 1024)
        bigb[n] = bigb[n].reshape(N_SHARD, FF_SH, 1024)

    dqx, dx1, datt, drec, dkx, dvx, dg_cross, dg_oa, dg_ol, *got = _b_mid(
        dx2, qx, x1, att, rec, kx, vx, wo, wq, w_out, p['g_cross'], p['g_out_attn'], p['g_out_lru'], tmb,
        rs(RS_IN_MID))
    recv.update(zip(RS_IN_MID, got))
    reduce_own(RS_IN_MID)
    dwk, dwv, dg_mem, dwkb, dwvb = _b_mem(dkx, dvx, mem, mn, p['g_mem'], wk, wv)
    big['wk_c'], bigb['wk_c'] = dwk, dwkb
    big['wv_c'], bigb['wv_c'] = dwv, dwvb
    big['w_out'], bigb['w_out'] = _mm_tn(mg, dx1, "dw_out", ts)
    big['wq_c'], bigb['wq_c'] = _mm_tn(hc, dqx, "dw_q", ts)
    big['wo_c'], bigb['wo_c'] = _mm_tn(ox, dx2, "dw_o", ts)
    for n in MID:
        big[n] = big[n].reshape(N_SHARD, 256, 1024)
        bigb[n] = bigb[n].reshape(N_SHARD, 256, 1024)

    dxg, dwrg, dwig, dbrg, dbig, dlam, dcw, dcb, *got = _b_lru(
        drec, hs, u, xg, p['conv_w'], wrg, p['b_rg'], wig, p['b_ig'], p['lru_L'], tl,
        rs(RS_IN_LRU) + swap(RS_IN_MID))
    recv.update(zip(RS_IN_LRU, got))
    sib.update(zip(RS_IN_MID, got[len(RS_IN_LRU):]))
    reduce_own(RS_IN_LRU)
    small = {
        'conv_w': dcw, 'conv_b': dcb, 'w_rg': dwrg, 'b_rg': dbrg, 'w_ig': dwig, 'b_ig': dbig, 'lru_L': dlam,
        'g_out_attn': dg_oa, 'g_out_lru': dg_ol, 'g_cross': dg_cross, 'g_mem': dg_mem, 'g_ffn': dg_ffn,
        'g_final': dg_final,
    }
    names = [n for n in SMALL if n in small]
    gather = [] if shards is None else [
        ("ag8", [_pack_small(names, [small[n] for n in names], loss, PACK_ROWS, "pack_small")])]
    dq, dkv, dfrow, *got = _b_attn(qkv_pad, att, datt, frow, rs(RS_IN_ATTN) + swap(RS_IN_LRU) + gather)
    recv.update(zip(RS_IN_ATTN, got))
    sib.update(zip(RS_IN_LRU, got[len(RS_IN_ATTN):]))
    packs = got[-1] if gather else None
    reduce_own(RS_IN_ATTN)
    small['rel_bias'] = _rel_bias_grad_from_dfrow(dfrow)
    big['w_in'], bigb['w_in'] = _b_win(dq, dkv, dxg, h, ts)
    if shards is None:
        grad_x, small['g_mix'] = _b_inproj(dq, dkv, dxg, x, dx1, p['g_mix'], gw['w_in'], tmb)
    else:
        nsw = len(RS_IN_ATTN)

        def copies(refs, send_sems, recv_sems):
            return _tail_copies(refs[0], refs[1], refs[2:2 + nsw], refs[2 + nsw:2 + 2 * nsw], send_sems, recv_sems)

        slots = bigb['w_in']
        bufs = ([slots, lax.empty((3,) + slots.shape[1:], slots.dtype)] + [part[n] for n in RS_IN_ATTN]
                + [lax.empty(part[n].shape, F32) for n in RS_IN_ATTN])
        sems, bufs, token = _split_start("tail_exchange_start", bufs, 3 + nsw, copies)
        grad_x, small['g_mix'] = _b_inproj(dq, dkv, dxg, x, dx1, p['g_mix'] + token[0, 0], gw['w_in'], tmb)
        bufs = _split_wait("tail_exchange_wait", sems, bufs, 3 + nsw, copies, small['g_mix'])
        recv['w_in'] = bufs[1]
        sib.update(zip(RS_IN_ATTN, bufs[2 + nsw:]))
    reduce_own(['w_in'])
    return loss, grad_x, small, big, part, sib, packs


CAST_STEPS = 4


def _cast_shards(ws, name, comm=None):
    def body(*refs):
        n = len(refs) // 2
        for src, dst in zip(refs[:n], refs[n:]):
            dst[...] = src[...].astype(BF16)

    specs = [_rows(w.shape[0] // CAST_STEPS, w.shape[1]) for w in ws]
    return _call(body, name, (CAST_STEPS,), specs, specs, [_sds(w.shape, BF16) for w in ws], [], tuple(ws),
                 "arbitrary", comm)


def _sum_parts(own4s, recv3s, chip, name):
    n = len(own4s)
    _, r, c = own4s[0].shape
    steps = _ew_steps(r, n * c * (4 + 3 * 2 + 4))
    tr = r // steps

    def body(chip_ref, *refs):
        for own_ref, rc_ref, o_ref in zip(refs[:n], refs[n:2 * n], refs[2 * n:]):
            o_ref[...] = ((own_ref[0] + rc_ref[0].astype(F32)) + rc_ref[1].astype(F32)) + rc_ref[2].astype(F32)

    grid_spec = pltpu.PrefetchScalarGridSpec(
        num_scalar_prefetch=1, grid=(steps,),
        in_specs=[pl.BlockSpec((1, tr, c), lambda i, ch: (ch[0], i, 0))] * n
                 + [pl.BlockSpec((3, tr, c), lambda i, ch: (0, i, 0))] * n,
        out_specs=[pl.BlockSpec((tr, c), lambda i, ch: (i, 0))] * n)
    return pl.pallas_call(body, name=name, grid_spec=grid_spec, out_shape=[_sds((r, c), F32)] * n,
                          compiler_params=_cp("parallel"))(chip, *own4s, *recv3s)


def _adamw_math(w, g, m, v):
    m = ADAM_B1 * m + (1.0 - ADAM_B1) * g
    v = ADAM_B2 * v + (1.0 - ADAM_B2) * (g * g)
    m_hat = m / (1.0 - ADAM_B1 ** ADAM_STEP)
    v_hat = v / (1.0 - ADAM_B2 ** ADAM_STEP)
    delta = -ADAM_LR * (m_hat / (jnp.sqrt(v_hat) + ADAM_EPS) + ADAM_WD * w)
    return delta, m, v


def _final_adamw(pas, pbs, ws, ms, vs, name, after=None):
    n = len(ws)
    r, c = ws[0].shape
    steps = _ew_steps(r, n * c * 9 * 4)
    tr = r // steps

    def body(*refs):
        ins, outs = refs[:5 * n], refs[len(refs) - 4 * n:]
        for k in range(n):
            pa_ref, pb_ref, w_ref, m_ref, v_ref = (ins[j * n + k] for j in range(5))
            g = pa_ref[...] + pb_ref[...]
            outs[4 * k][...] = g
            outs[4 * k + 1][...], outs[4 * k + 2][...], outs[4 * k + 3][...] = _adamw_math(
                w_ref[...], g, m_ref[...], v_ref[...])

    order = [] if after is None else [after]
    res = pl.pallas_call(
        body, name=name, grid=(steps,), in_specs=[_rows(tr, c)] * (5 * n) + [_full(t.shape) for t in order],
        out_specs=[_rows(tr, c)] * (4 * n), out_shape=[_sds((r, c), F32)] * (4 * n),
        compiler_params=_cp("parallel"))(*pas, *pbs, *ws, *ms, *vs, *order)
    return [res[4 * k:4 * k + 4] for k in range(n)]


def _pack_put(ref, name, val_ref):
    r = _pack_rows()[name]
    shape = val_ref.shape
    if len(shape) == 3:
        for b in range(shape[0]):
            ref[r:r + shape[1], b * shape[2]:(b + 1) * shape[2]] = val_ref[b]
    elif shape[1] == 2 * PACK_W:
        ref[r:r + 1, :] = val_ref[:, 0:PACK_W]
        ref[r + 1:r + 2, :] = val_ref[:, PACK_W:2 * PACK_W]
    else:
        ref[r:r + shape[0], 0:shape[1]] = val_ref[...]


def _pack_get(ref, name, shape):
    r = _pack_rows()[name]
    if len(shape) == 3:
        return jnp.stack([ref[r:r + shape[1], b * shape[2]:(b + 1) * shape[2]] for b in range(shape[0])])
    if shape[1] == 2 * PACK_W:
        return jnp.concatenate([ref[r:r + 1, :], ref[r + 1:r + 2, :]], axis=1)
    return ref[r:r + shape[0], 0:shape[1]]


def _pack_small(names, g, loss, rows, name):
    n = len(g)
    extra = [] if loss is None else [loss]

    def body(*refs):
        pack = refs[-1]
        pack[...] = jnp.zeros_like(pack)
        for a, nm in enumerate(names):
            _pack_put(pack, nm, refs[a])
        if extra:
            _pack_put(pack, 'loss', refs[n])

    return pl.pallas_call(body, name=name, out_shape=_sds((rows, PACK_W), F32), compiler_params=_cp())(*g, *extra)


def _all_peers():
    x, y, c = _mesh_pos()
    peers = []
    for k in range(1, 8):
        px = 1 - x if k & 4 else x
        py = 1 - y if k & 2 else y
        pc = 1 - c if k & 1 else c
        peers.append(((px, py, pc), 4 * px + 2 * py + pc))
    return peers, 4 * x + 2 * y + c


def _ag8_copies(ins, outs, sems):
    send_sems, recv_sems, loc_sems = sems
    n = len(ins)
    peers, me = _all_peers()

    def remote(k, j, slot):
        return pltpu.make_async_remote_copy(
            src_ref=ins[k], dst_ref=outs[k].at[slot], send_sem=send_sems.at[k, j], recv_sem=recv_sems.at[k, j],
            device_id=peers[j][0], device_id_type=MESH_ID)

    def local(k):
        return pltpu.make_async_copy(ins[k], outs[k].at[me], loc_sems.at[k])

    def start():
        for k in range(n):
            local(k).start()
            for j in range(7):
                remote(k, j, me).start()

    def wait():
        for k in range(n):
            for j in range(7):
                remote(k, j, peers[j][1]).wait_recv()
        for k in range(n):
            for j in range(7):
                remote(k, j, me).wait_send()
            local(k).wait()

    return start, _no_forward, wait


def _adamw_small(packs, late_own, late_packs, g_shapes, loss_shape, w, m, v):
    n = len(w)

    def body(*refs):
        packs_ref, own_ref, late_ref = refs[0], refs[1], refs[2]
        w_refs, m_refs, v_refs = (refs[3 + i * n:3 + (i + 1) * n] for i in range(3))
        o0 = 3 * n + 3
        go, do, mo, vo = (refs[o0 + i * n:o0 + (i + 1) * n] for i in range(4))
        loss_out, tot_ref = refs[o0 + 4 * n], refs[o0 + 4 * n + 1]
        x, y, c = _mesh_pos()
        me = 4 * x + 2 * y + c
        tot = packs_ref[0]
        late = jnp.where(me == 0, own_ref[...], late_ref[0])
        for d in range(1, 8):
            tot = tot + packs_ref[d]
            late = late + jnp.where(me == d, own_ref[...], late_ref[d])
        tot_ref[...] = tot
        tot_ref[0:LATE_ROWS, :] += late
        loss_out[...] = _pack_get(tot_ref, 'loss', loss_shape)
        for a, name in enumerate(SMALL):
            if name == 'conv_w':
                r = _pack_rows()[name]
                ga = tot_ref[r:r + g_shapes[a][0], pl.ds(pl.multiple_of((2 * x + y) * 128, 128), 128)]
            else:
                ga = _pack_get(tot_ref, name, g_shapes[a])
            go[a][...] = ga
            do[a][...], mo[a][...], vo[a][...] = _adamw_math(w_refs[a][...], ga, m_refs[a][...], v_refs[a][...])

    out_shape = [_sds(a.shape, F32) for a in w] * 4 + [_sds(loss_shape, F32)]
    return pl.pallas_call(body, name="adamw_small", out_shape=out_shape,
                          scratch_shapes=[pltpu.VMEM((PACK_ROWS, PACK_W), F32)],
                          compiler_params=_cp())(packs, late_own, late_packs, *w, *m, *v)


PACK_W = 512
PACK_ROWS = 160
LATE = ['g_mix', 'rel_bias']
LATE_ROWS = 32


def _pack_rows():
    rows, r = {}, 0
    for name in ['g_mix', 'g_cross', 'g_mem', 'g_ffn', 'g_final']:
        rows[name] = r
        r += 2
    for name in ['conv_b', 'b_rg', 'b_ig', 'lru_L', 'g_out_attn', 'g_out_lru']:
        rows[name] = r
        r += 1
    rows['conv_w'] = r
    rows['loss'] = r + 4
    rows['rel_bias'] = 24
    rows['w_rg'] = 32
    rows['w_ig'] = 32 + LRU_BLOCK
    assert r + 5 <= 24 and rows['w_ig'] + LRU_BLOCK == PACK_ROWS
    assert rows['g_mix'] + 2 <= LATE_ROWS and rows['rel_bias'] + 8 <= LATE_ROWS
    return rows


INPUT_NAMES = (['x', 'mem'] + WEIGHTS + ['loss_target'] + ['m_' + n for n in WEIGHTS] + ['v_' + n for n in WEIGHTS])


def kernel(x, mem, g_mix, w_in, rel_bias, conv_w, conv_b, w_rg, b_rg, w_ig, b_ig, lru_L, g_out_attn, g_out_lru, w_out, g_cross, g_mem, wq_c, wk_c, wv_c, wo_c, g_ffn, w_gate, w_up, w_down, g_final, loss_target, m_g_mix, m_w_in, m_rel_bias, m_conv_w, m_conv_b, m_w_rg, m_b_rg, m_w_ig, m_b_ig, m_lru_L, m_g_out_attn, m_g_out_lru, m_w_out, m_g_cross, m_g_mem, m_wq_c, m_wk_c, m_wv_c, m_wo_c, m_g_ffn, m_w_gate, m_w_up, m_w_down, m_g_final, v_g_mix, v_w_in, v_rel_bias, v_conv_w, v_conv_b, v_w_rg, v_b_rg, v_w_ig, v_b_ig, v_lru_L, v_g_out_attn, v_g_out_lru, v_w_out, v_g_cross, v_g_mem, v_wq_c, v_wk_c, v_wv_c, v_wo_c, v_g_ffn, v_w_gate, v_w_up, v_w_down, v_g_final):
    a = dict(zip(INPUT_NAMES, (x, mem, g_mix, w_in, rel_bias, conv_w, conv_b, w_rg, b_rg, w_ig, b_ig, lru_L, g_out_attn, g_out_lru, w_out, g_cross, g_mem, wq_c, wk_c, wv_c, wo_c, g_ffn, w_gate, w_up, w_down, g_final, loss_target, m_g_mix, m_w_in, m_rel_bias, m_conv_w, m_conv_b, m_w_rg, m_b_rg, m_w_ig, m_b_ig, m_lru_L, m_g_out_attn, m_g_out_lru, m_w_out, m_g_cross, m_g_mem, m_wq_c, m_wk_c, m_wv_c, m_wo_c, m_g_ffn, m_w_gate, m_w_up, m_w_down, m_g_final, v_g_mix, v_w_in, v_rel_bias, v_conv_w, v_conv_b, v_w_rg, v_b_rg, v_w_ig, v_b_ig, v_lru_L, v_g_out_attn, v_g_out_lru, v_w_out, v_g_cross, v_g_mem, v_wq_c, v_wk_c, v_wv_c, v_wo_c, v_g_ffn, v_w_gate, v_w_up, v_w_down, v_g_final)))
    chip = 2 * lax.axis_index("x") + lax.axis_index("y")

    def shard(name):
        arr = a[name][0]
        base = name[2:] if name[:2] in ('m_', 'v_') else name
        return jnp.swapaxes(arr, 0, 1) if base in TRANSPOSED else arr

    shards = {'w_in': _cast_shards([shard('w_in')], "cast_w_in")[0]}
    rest = [n for n in BIG if n != 'w_in']
    *cast, w_in_g, conv_w_g = _cast_shards([shard(n) for n in rest], "cast_rest",
                                           [("ag", [shards['w_in']]), ("agf", [a['conv_w'][0]])])
    shards.update(zip(rest, cast))
    conv_w_full = conv_w_g.transpose(1, 0, 2).reshape(4, D_LRU)

    p = {n: a[n] for n in SMALL}
    p['rel_bias'] = a['rel_bias'][0]
    p['w_rg'] = a['w_rg'][0]
    p['w_ig'] = a['w_ig'][0]
    p['conv_w'] = conv_w_full
    p['g_final'] = a['g_final'][None, :]
    chip_arr = jnp.reshape(chip, (1,)).astype(jnp.int32)
    loss_part, grad_x, small, _, part, sib, packs = _local_step(
        a['x'][0], a['mem'][0], a['loss_target'][0], p, {'w_in': w_in_g}, shards, chip_arr)

    def late_copies(refs, send_sems, recv_sems):
        return _late_copies(refs[0], refs[1], refs[2], refs[3], send_sems, recv_sems)

    late_pack = _pack_small(LATE, [small[n] for n in LATE], None, LATE_ROWS, "pack_late")
    bufs = [part['w_in'], lax.empty(part['w_in'].shape, F32), late_pack, jnp.zeros((8, LATE_ROWS, PACK_W), F32)]
    sems, bufs, token = _split_start("late_exchange_start", bufs, 8, late_copies)
    out = {}

    def adamw(group, after=None):
        results = _final_adamw([part[n] for n in group], [sib[n] for n in group], [shard(n) for n in group],
                               [shard('m_' + n) for n in group], [shard('v_' + n) for n in group],
                               "adamw_" + group[0], after)
        for n, res in zip(group, results):
            out[n] = [jnp.swapaxes(r, 0, 1) for r in res] if n in TRANSPOSED else res
        return results[-1][0]

    adamw(MID, token)
    done = adamw(['w_gate', 'w_up', 'w_down'], token)
    _, sib['w_in'], late_pack, late_packs = _split_wait("late_exchange_wait", sems, bufs, 8, late_copies, done)
    adamw(['w_in'])

    def natural(arr):
        return arr[0] if arr.ndim >= 3 else (arr[None, :] if arr.ndim == 1 else arr)

    small_out = _adamw_small(packs, late_pack, late_packs, [small[n].shape for n in SMALL],
                             loss_part.shape, *[[natural(a[pre + n]) for n in SMALL] for pre in ('', 'm_', 'v_')])
    ns = len(SMALL)
    loss = small_out[4 * ns][0, 0]

    def leaf(i, n):
        if n in BIG:
            return out[n][i][None]
        return small_out[i * ns + SMALL.index(n)].reshape(a[n].shape)

    return (loss, grad_x[None], *[leaf(i, n) for i in range(4) for n in WEIGHTS])
```

```python
import math

import jax
import jax.numpy as jnp
from jax import lax
from jax.experimental import pallas as pl
from jax.experimental.pallas import tpu as pltpu

F32 = jnp.float32
BF16 = jnp.bfloat16

D_MODEL = 1024
D_ATT = 512
D_LRU = 512
HEAD_DIM = 64
ATT_HEADS = 8
CHUNK = 64
LEFT_CHUNKS = 8
MAX_REL = 128
X_HEADS = 4
X_HEAD_DIM = 256
N_SHARD = 4
IN_SH = 640
D_IN = N_SHARD * IN_SH
FF_SH = 704
D_FF = N_SHARD * FF_SH
EPS = 1e-6
LRU_C = 8.0
LRU_BLOCKS = 8
LRU_BLOCK = 64
QB = 256
KB = 768
ROLL_W = 1024
NEG = -1e30
ATT_SCALE = HEAD_DIM ** -0.5
X_SCALE = X_HEAD_DIM ** -0.5

ADAM_LR = 0.001
ADAM_B1 = 0.9
ADAM_B2 = 0.999
ADAM_EPS = 1e-08
ADAM_WD = 0.01
ADAM_STEP = 10

VMEM_LIMIT_V7X = 56 * 1024 * 1024
BF16_ROWS = 16


EW_VMEM_BUDGET = 40 * 1024 * 1024


def _ew_steps(rows, bytes_per_row):
    return min(s for s in (2, 4, 8, 16) if rows % (s * BF16_ROWS) == 0
               and 2 * (rows // s) * bytes_per_row <= EW_VMEM_BUDGET)
MESH_ID = pl.DeviceIdType.MESH

WEIGHTS = ['g_mix', 'w_in', 'rel_bias', 'conv_w', 'conv_b', 'w_rg', 'b_rg', 'w_ig', 'b_ig', 'lru_L',
           'g_out_attn', 'g_out_lru', 'w_out', 'g_cross', 'g_mem', 'wq_c', 'wk_c', 'wv_c', 'wo_c',
           'g_ffn', 'w_gate', 'w_up', 'w_down', 'g_final']
BIG = ['w_in', 'w_out', 'wq_c', 'wk_c', 'wv_c', 'wo_c', 'w_gate', 'w_up', 'w_down']
SMALL = [n for n in WEIGHTS if n not in BIG]


def _sds(shape, dtype):
    return jax.ShapeDtypeStruct(shape, dtype)


def _cp(*sem):
    return pltpu.CompilerParams(dimension_semantics=sem or None, vmem_limit_bytes=VMEM_LIMIT_V7X)


def _rows(tm, n):
    return pl.BlockSpec((tm, n), lambda i: (i, 0))


def _full(shape):
    nd = len(shape)
    return pl.BlockSpec(shape, lambda i: (0,) * nd)


def _dot(a, b):
    return jnp.dot(a, b, preferred_element_type=F32)


def _dot_nt(a, b):
    return lax.dot_general(a, b, (((1,), (1,)), ((), ())), preferred_element_type=F32)


def _dot_tn(a, b):
    return lax.dot_general(a, b, (((0,), (0,)), ((), ())), preferred_element_type=F32)


def _rinv(x):
    return lax.rsqrt(jnp.mean(x * x, axis=-1, keepdims=True) + EPS)


def _rms_bwd(dy, x, g):
    r = _rinv(x)
    yh = x * r
    dyh = dy * g
    dx = r * (dyh - yh * jnp.mean(dyh * yh, axis=-1, keepdims=True))
    return dx, jnp.sum(dy * yh, axis=0, keepdims=True)


def _gelu(x):
    c = math.sqrt(2.0 / math.pi)
    t = jnp.tanh(c * (x + 0.044715 * x * x * x))
    return 0.5 * x * (1.0 + t)


def _gelu_and_grad(x):
    c = math.sqrt(2.0 / math.pi)
    t = jnp.tanh(c * (x + 0.044715 * x * x * x))
    g = 0.5 * x * (1.0 + t)
    dg = 0.5 * (1.0 + t) + 0.5 * x * (1.0 - t * t) * c * (1.0 + 3.0 * 0.044715 * x * x)
    return g, dg


def _neg_expm1(z):
    series = -z * (1.0 + z * (0.5 + z * ((1.0 / 6.0) + z * (1.0 / 24.0))))
    return jnp.where(z > -0.03, series, 1.0 - jnp.exp(z))


def _lru_gates(u, wrg, brg, wig, big, lam):
    ub = u.astype(BF16)
    r = jax.nn.sigmoid(_dot(ub, wrg) + brg)
    ig = jax.nn.sigmoid(_dot(ub, wig) + big)
    sp = jnp.maximum(-lam, 0.0) + jnp.log1p(jnp.exp(-jnp.abs(lam)))
    la = -LRU_C * r * sp
    a = jnp.exp(la)
    mult = jnp.sqrt(jnp.maximum(_neg_expm1(2.0 * la), 0.0))
    return ub, r, ig, sp, a, mult


def _scan8(a8, b8, hprev):
    row = lax.broadcasted_iota(jnp.int32, a8.shape, 0)
    aa, bb = a8, b8
    for d in (1, 2, 4):
        a_s = pltpu.roll(aa, d, 0)
        b_s = pltpu.roll(bb, d, 0)
        m = row >= d
        bb = jnp.where(m, aa * b_s + bb, bb)
        aa = jnp.where(m, aa * a_s, aa)
    return aa * hprev + bb


def _rscan8(c8, d8, lnext):
    row = lax.broadcasted_iota(jnp.int32, c8.shape, 0)
    cc, dd = c8, d8
    for d in (1, 2, 4):
        c_s = pltpu.roll(cc, 8 - d, 0)
        d_s = pltpu.roll(dd, 8 - d, 0)
        m = row < 8 - d
        dd = jnp.where(m, cc * d_s + dd, dd)
        cc = jnp.where(m, cc * c_s, cc)
    return cc * lnext + dd


def _mesh_pos():
    return lax.axis_index("x"), lax.axis_index("y"), lax.axis_index("c")


def _other_chips(x, y):
    return [(1 - x, y), (x, 1 - y), (1 - x, 1 - y)]


def _no_forward():
    pass


def _ag_full_copies(ins, outs, sems):
    send_sems, recv_sems, loc_sems = sems
    n = len(ins)
    x, y, c = _mesh_pos()
    mine = 2 * x + y
    chips = _other_chips(x, y)

    def remote(k, j, slot):
        px, py = chips[j]
        return pltpu.make_async_remote_copy(
            src_ref=ins[k], dst_ref=outs[k].at[slot], send_sem=send_sems.at[k, j], recv_sem=recv_sems.at[k, j],
            device_id=(px, py, c), device_id_type=MESH_ID)

    def local(k):
        return pltpu.make_async_copy(ins[k], outs[k].at[mine], loc_sems.at[k])

    def start():
        for k in range(n):
            local(k).start()
            for j in range(3):
                remote(k, j, mine).start()

    def wait():
        for k in range(n):
            for j, (px, py) in enumerate(chips):
                remote(k, j, 2 * px + py).wait_recv()
        for k in range(n):
            for j in range(3):
                remote(k, j, mine).wait_send()
            local(k).wait()

    return start, _no_forward, wait


def _ag_copies(ins, outs, sems):
    send_sems, recv_sems, fsend_sems, frecv_sems, loc_sems = sems
    n = len(ins)
    x, y, c = _mesh_pos()
    mine = 2 * x + y
    chips = _other_chips(x, y)

    def half(ref, hc):
        r = ref.shape[0] // 2
        return ref.at[pl.ds(pl.multiple_of(hc * r, 16), r)]

    def ici(k, j, slot):
        px, py = chips[j]
        return pltpu.make_async_remote_copy(
            src_ref=half(ins[k], c), dst_ref=half(outs[k].at[slot], c),
            send_sem=send_sems.at[k, j], recv_sem=recv_sems.at[k, j],
            device_id=(px, py, c), device_id_type=MESH_ID)

    def d2d(k, j, hc):
        px, py = chips[j]
        part = half(outs[k].at[2 * px + py], hc)
        return pltpu.make_async_remote_copy(
            src_ref=part, dst_ref=part, send_sem=fsend_sems.at[k, j], recv_sem=frecv_sems.at[k, j],
            device_id=(x, y, 1 - c), device_id_type=MESH_ID)

    def local(k):
        return pltpu.make_async_copy(ins[k], outs[k].at[mine], loc_sems.at[k])

    def start():
        for k in range(n):
            local(k).start()
            for j in range(3):
                ici(k, j, mine).start()

    def forward():
        for k in range(n):
            for j, (px, py) in enumerate(chips):
                ici(k, j, 2 * px + py).wait_recv()
                d2d(k, j, c).start()

    def wait():
        for k in range(n):
            for j in range(3):
                d2d(k, j, 1 - c).wait_recv()
        for k in range(n):
            for j in range(3):
                d2d(k, j, c).wait_send()
                ici(k, j, mine).wait_send()
            local(k).wait()

    return start, forward, wait


def _rs_copies(ins, outs, sems):
    send_sems, recv_sems = sems
    n = len(ins)
    x, y, c = _mesh_pos()
    chips = _other_chips(x, y)

    def remote(k, j):
        px, py = chips[j]
        return pltpu.make_async_remote_copy(
            src_ref=ins[k].at[2 * px + py], dst_ref=outs[k].at[j],
            send_sem=send_sems.at[k, j], recv_sem=recv_sems.at[k, j],
            device_id=(px, py, c), device_id_type=MESH_ID)

    def start():
        for k in range(n):
            for j in range(3):
                remote(k, j).start()

    def wait():
        for k in range(n):
            for j in range(3):
                remote(k, j).wait_recv()
        for k in range(n):
            for j in range(3):
                remote(k, j).wait_send()

    return start, _no_forward, wait


def _swap_copies(ins, outs, sems):
    send_sems, recv_sems = sems
    x, y, c = _mesh_pos()
    copies = [pltpu.make_async_remote_copy(
        src_ref=ins[k], dst_ref=outs[k], send_sem=send_sems.at[k], recv_sem=recv_sems.at[k],
        device_id=(x, y, 1 - c), device_id_type=MESH_ID) for k in range(len(ins))]

    def start():
        for cp in copies:
            cp.start()

    def wait():
        for cp in copies:
            cp.wait()

    return start, _no_forward, wait


def _comm_plan(groups):
    plan, arrs, shapes, sems = [], [], [], []
    for kind, group in groups:
        k = len(group)
        arrs += group
        per_peer = pltpu.SemaphoreType.DMA((k, 3))
        if kind == "ag":
            shapes += [_sds((N_SHARD,) + w.shape, w.dtype) for w in group]
            gsems = [per_peer] * 4 + [pltpu.SemaphoreType.DMA((k,))]
            maker = _ag_copies
        elif kind == "agf":
            shapes += [_sds((N_SHARD,) + w.shape, w.dtype) for w in group]
            gsems = [per_peer] * 2 + [pltpu.SemaphoreType.DMA((k,))]
            maker = _ag_full_copies
        elif kind == "ag8":
            shapes += [_sds((8,) + g.shape, g.dtype) for g in group]
            gsems = [pltpu.SemaphoreType.DMA((k, 7))] * 2 + [pltpu.SemaphoreType.DMA((k,))]
            maker = _ag8_copies
        elif kind == "rs":
            shapes += [_sds((3,) + g.shape[1:], g.dtype) for g in group]
            gsems = [pltpu.SemaphoreType.DMA((k, 3)), pltpu.SemaphoreType.DMA((k, 3))]
            maker = _rs_copies
        else:
            shapes += [_sds(g.shape, g.dtype) for g in group]
            gsems = [pltpu.SemaphoreType.DMA((k,)), pltpu.SemaphoreType.DMA((k,))]
            maker = _swap_copies
        plan.append((maker, k, len(gsems)))
        sems += gsems
    return plan, arrs, shapes, sems


def _comm_fns(plan, cins, couts, sems):
    fns, a, s = [], 0, 0
    for maker, k, ns in plan:
        fns.append(maker(cins[a:a + k], couts[a:a + k], sems[s:s + ns]))
        a += k
        s += ns

    def start():
        for st, _, _ in fns:
            st()

    def forward():
        for _, fw, _ in fns:
            fw()

    def wait():
        for _, _, wt in fns:
            wt()

    return start, forward, wait


def _call(body, name, grid, in_specs, out_specs, out_shape, scratch, args, sem, comm=None):
    if not comm:
        return pl.pallas_call(body, name=name, grid=grid, in_specs=in_specs, out_specs=out_specs,
                              out_shape=out_shape, scratch_shapes=scratch, compiler_params=_cp(sem))(*args)
    plan, c_arrs, c_shapes, c_sems = _comm_plan(comm)
    k = len(c_arrs)
    n_in, n_out, n_scr = len(in_specs), len(out_specs), len(scratch)
    last = grid[0] - 1
    fwd_step = max(1, (2 * last) // 3)

    def wrapped(*refs):
        ins, cins = refs[:n_in], refs[n_in:n_in + k]
        o0 = n_in + k
        outs, couts = refs[o0:o0 + n_out], refs[o0 + n_out:o0 + n_out + k]
        s0 = o0 + n_out + k
        start, forward, wait = _comm_fns(plan, cins, couts, refs[s0 + n_scr:])
        pl.when(pl.program_id(0) == 0)(start)
        pl.when(pl.program_id(0) == fwd_step)(forward)
        body(*ins, *outs, *refs[s0:s0 + n_scr])
        pl.when(pl.program_id(0) == last)(wait)

    return pl.pallas_call(
        wrapped, name=name, grid=grid, in_specs=list(in_specs) + [_any()] * k,
        out_specs=list(out_specs) + [_any()] * k, out_shape=list(out_shape) + c_shapes,
        scratch_shapes=list(scratch) + c_sems, compiler_params=_cp(sem))(*args, *c_arrs)


def _tail_copies(slots_ref, land_ref, part_refs, sib_refs, send_sems, recv_sems):
    x, y, c = _mesh_pos()
    copies = []
    for j, (px, py) in enumerate(_other_chips(x, y)):
        copies.append(pltpu.make_async_remote_copy(
            src_ref=slots_ref.at[2 * px + py], dst_ref=land_ref.at[j], send_sem=send_sems[j], recv_sem=recv_sems[j],
            device_id=(px, py, c), device_id_type=MESH_ID))
    for k, (p_ref, s_ref) in enumerate(zip(part_refs, sib_refs)):
        copies.append(pltpu.make_async_remote_copy(
            src_ref=p_ref, dst_ref=s_ref, send_sem=send_sems[3 + k], recv_sem=recv_sems[3 + k],
            device_id=(x, y, 1 - c), device_id_type=MESH_ID))
    return copies


def _late_copies(part_ref, sib_ref, pack_ref, packs_ref, send_sems, recv_sems):
    x, y, c = _mesh_pos()
    peers, me = _all_peers()
    copies = [pltpu.make_async_remote_copy(
        src_ref=part_ref, dst_ref=sib_ref, send_sem=send_sems[0], recv_sem=recv_sems[0],
        device_id=(x, y, 1 - c), device_id_type=MESH_ID)]
    for j in range(7):
        copies.append(pltpu.make_async_remote_copy(
            src_ref=pack_ref, dst_ref=packs_ref.at[me], send_sem=send_sems[1 + j], recv_sem=recv_sems[1 + j],
            device_id=peers[j][0], device_id_type=MESH_ID))
    return copies


def _split_start(name, bufs, ncp, make_copies):
    hbm = pl.BlockSpec(memory_space=pltpu.HBM)
    sem = pl.BlockSpec(memory_space=pltpu.SEMAPHORE)
    bufs = [pltpu.with_memory_space_constraint(b, pltpu.HBM) for b in bufs]
    nb = len(bufs)

    def body(*refs):
        for cp in make_copies(refs[:nb], refs[nb:nb + ncp], refs[nb + ncp:nb + 2 * ncp]):
            cp.start()
        refs[-1][...] = jnp.zeros_like(refs[-1])

    out = pl.pallas_call(
        body, name=name,
        out_shape=[pltpu.SemaphoreType.DMA(())] * (2 * ncp) + [pltpu.HBM(b.shape, b.dtype) for b in bufs]
                  + [_sds((8, 128), F32)],
        in_specs=[hbm] * nb, out_specs=[sem] * (2 * ncp) + [hbm] * nb + [pl.BlockSpec(memory_space=pltpu.VMEM)],
        input_output_aliases={i: 2 * ncp + i for i in range(nb)},
        compiler_params=pltpu.CompilerParams(has_side_effects=pltpu.SideEffectType.DATAFLOW_SIDE_EFFECTING),
    )(*bufs)
    return out[:2 * ncp], out[2 * ncp:2 * ncp + nb], out[-1]


def _split_wait(name, sems, bufs, ncp, make_copies, after):
    nb = len(bufs)
    hbm = pl.BlockSpec(memory_space=pltpu.HBM)
    sem = pl.BlockSpec(memory_space=pltpu.SEMAPHORE)

    def body(*refs):
        for cp in make_copies(refs[:nb], refs[nb:nb + ncp], refs[nb + ncp:nb + 2 * ncp]):
            cp.wait_send()
            cp.wait_recv()

    return pl.pallas_call(
        body, name=name, out_shape=[pltpu.HBM(b.shape, b.dtype) for b in bufs],
        in_specs=[hbm] * nb + [sem] * (2 * ncp) + [_any()], out_specs=[hbm] * nb,
        input_output_aliases={i: i for i in range(nb)},
        compiler_params=pltpu.CompilerParams(has_side_effects=pltpu.SideEffectType.DATAFLOW_SIDE_EFFECTING),
    )(*bufs, *sems, after)


def _any():
    return pl.BlockSpec(memory_space=pl.ANY)


def _start_copies(pairs, sems, first=0):
    copies = [pltpu.make_async_copy(src, dst, sems.at[first + i]) for i, (src, dst) in enumerate(pairs)]
    for i, cp in enumerate(copies):
        cp.start(priority=i % 2)
    return copies


def _copy_together(pairs, sems):
    for cp in _start_copies(pairs, sems):
        cp.wait()


def _load_w_in_once(w_hbm, w_ref, sems):
    @pl.when(pl.program_id(0) == 0)
    def _():
        _copy_together([(w_hbm.at[s], w_ref.at[:, pl.ds(s * IN_SH, IN_SH)]) for s in range(N_SHARD)], sems)


def _f_inproj(x, g_mix, w_in_g, tm, comm=None):
    s_len = x.shape[0]
    pad_rows = LEFT_CHUNKS * CHUNK
    npad = pad_rows // tm

    def body(x_ref, g_ref, w_hbm, h_ref, qkv_ref, xg_ref, w_ref, w_sems):
        i = pl.program_id(0)
        _load_w_in_once(w_hbm, w_ref, w_sems)

        @pl.when(i < npad)
        def _():
            qkv_ref[...] = jnp.zeros_like(qkv_ref)

        @pl.when(i >= npad)
        def _():
            xv = x_ref[...]
            h = (xv * _rinv(xv) * g_ref[...]).astype(BF16)
            h_ref[...] = h
            proj = _dot(h, w_ref[...])
            qkv_ref[:, 0:D_ATT] = (proj[:, 0:D_ATT] * ATT_SCALE).astype(BF16)
            qkv_ref[:, D_ATT:3 * D_ATT] = proj[:, D_ATT:3 * D_ATT].astype(BF16)
            xg_ref[...] = proj[:, 3 * D_ATT:D_IN]

    def tok(n):
        return pl.BlockSpec((tm, n), lambda i: (jnp.maximum(i - npad, 0), 0))

    return _call(
        body, "f_inproj", (s_len // tm + npad,),
        [tok(1024), _full((1, 1024)), _any()],
        [tok(1024), _rows(tm, 1536), tok(1024)],
        [_sds((s_len, 1024), BF16), _sds((s_len + pad_rows, 1536), BF16), _sds((s_len, 1024), F32)],
        [pltpu.VMEM((1024, D_IN), BF16), pltpu.SemaphoreType.DMA((N_SHARD,))], (x, g_mix, w_in_g), "arbitrary", comm)


N_BIAS = 3


def _bias_table(frow_ref, bias_sc):
    qa = lax.broadcasted_iota(jnp.int32, (QB, KB), 0) // CHUNK
    kcol = lax.broadcasted_iota(jnp.int32, (QB, KB), 1)
    kb = kcol // CHUNK
    band = jnp.where((kb >= qa) & (kb - qa <= LEFT_CHUNKS), 0.0, NEG).astype(F32)
    for h in range(ATT_HEADS):
        row = jnp.broadcast_to(frow_ref[h:h + 1, :], (QB, ROLL_W))
        toep = pltpu.roll(row, 0, 1, stride=1, stride_axis=0)
        gen = toep[:, 0:KB] + band
        bias_sc[N_BIAS - 1, h] = gen
        for v in range(N_BIAS - 1):
            pad_keys = LEFT_CHUNKS * CHUNK - v * QB
            bias_sc[v, h] = gen + jnp.where(kcol < pad_keys, NEG, 0.0).astype(F32)


def _even_lanes():
    return lax.broadcasted_iota(jnp.int32, (1, 2 * HEAD_DIM), 1) < HEAD_DIM


def _att_probs(qm, kts, bias):
    s = jnp.concatenate([_dot_nt(qm, k) for k in kts], axis=1) + bias
    return jnp.exp(s - jnp.max(s, axis=-1, keepdims=True))


def _att_in_specs(clamp):
    def spec(j, col):
        return pl.BlockSpec((QB, D_ATT), lambda i: (clamp(i) + j, col))
    return [spec(2, 0), spec(0, 1), spec(1, 1), spec(2, 1), spec(0, 2), spec(1, 2), spec(2, 2)]


def _f_attn(qkv_pad, frow, comm=None):
    s_len = qkv_pad.shape[0] - LEFT_CHUNKS * CHUNK
    nb = s_len // QB

    def body(q_ref, k0, k1, k2, v0, v1, v2, frow_ref, o_ref, bias_sc):
        i = pl.program_id(0)

        @pl.when(i == 0)
        def _():
            _bias_table(frow_ref, bias_sc)

        var = jnp.minimum(i, N_BIAS - 1)
        even = _even_lanes()
        for hp in range(ATT_HEADS // 2):
            cs = slice(hp * 2 * HEAD_DIM, (hp + 1) * 2 * HEAD_DIM)
            qt = q_ref[:, cs]
            kts = [k0[:, cs], k1[:, cs], k2[:, cs]]
            vts = [v0[:, cs], v1[:, cs], v2[:, cs]]
            res = []
            for e in range(2):
                keep = even if e == 0 else jnp.logical_not(even)
                pb = _att_probs(jnp.where(keep, qt, 0), kts, bias_sc[var, 2 * hp + e]).astype(BF16)
                r = _dot(pb, jnp.concatenate([jnp.where(keep, v, 1) for v in vts], axis=0))
                res.append(r / pltpu.roll(r, HEAD_DIM, 1))
            o_ref[:, cs] = jnp.where(even, res[0], res[1])

    return _call(
        body, "f_attn", (nb,),
        _att_in_specs(lambda i: i) + [_full((ATT_HEADS, ROLL_W))],
        [_rows(QB, D_ATT)], [_sds((s_len, D_ATT), F32)],
        [pltpu.VMEM((N_BIAS, ATT_HEADS, QB, KB), F32)], (*([qkv_pad] * 7), frow), "arbitrary", comm)


def _f_lru(xg, conv_w, conv_b, wrg, brg, wig, big, lam, tl, comm=None):
    s_len = xg.shape[0]

    def body(xg_ref, cw_ref, cb_ref, wrg_ref, brg_ref, wig_ref, big_ref, l_ref,
             rec_ref, u_ref, hs_ref, xbuf, a_sc, b_sc, hcar):
        i = pl.program_id(0)

        @pl.when(i == 0)
        def _():
            xbuf[0:8, :] = jnp.zeros((8, D_LRU), F32)
            hcar[...] = jnp.zeros((8, D_LRU), F32)

        xu0 = xg_ref[:, 0:D_LRU]
        xbuf[8:8 + tl, :] = xu0
        u = cb_ref[...] + cw_ref[0:1, :] * xbuf[pl.ds(5, tl), :]
        for j in range(1, 4):
            u = u + cw_ref[j:j + 1, :] * xbuf[pl.ds(5 + j, tl), :]
        xbuf[0:8, :] = xu0[tl - 8:tl, :]
        u_ref[...] = u
        _, _, ig, _, a, mult = _lru_gates(u, wrg_ref[...], brg_ref[...], wig_ref[...], big_ref[...], l_ref[...])
        a_sc[...] = a
        b_sc[...] = mult * (ig * u)

        def grp(g, hprev):
            off = pl.multiple_of(g * 8, 8)
            h8 = _scan8(a_sc[pl.ds(off, 8), :], b_sc[pl.ds(off, 8), :], hprev)
            hs_ref[pl.ds(off, 8), :] = h8
            return h8[7:8, :]

        hcar[0:1, :] = lax.fori_loop(0, tl // 8, grp, hcar[0:1, :])
        rec_ref[...] = hs_ref[...] * _gelu(xg_ref[:, D_LRU:2 * D_LRU])

    vec = _full((1, D_LRU))
    return _call(
        body, "f_lru", (s_len // tl,),
        [_rows(tl, 1024), _full((4, D_LRU)), vec, _full((D_LRU, D_LRU)), vec, _full((D_LRU, D_LRU)), vec, vec],
        [_rows(tl, D_LRU)] * 3, [_sds((s_len, D_LRU), F32)] * 3,
        [pltpu.VMEM((tl + 8, D_LRU), F32), pltpu.VMEM((tl, D_LRU), F32),
         pltpu.VMEM((tl, D_LRU), F32), pltpu.VMEM((8, D_LRU), F32)],
        (xg, conv_w, conv_b, wrg, brg, wig, big, lam), "arbitrary", comm)


def _f_mem(mem, g_mem, wk, wv):
    def body(mem_ref, g_ref, wk_ref, wv_ref, mn_ref, kx_ref, vx_ref):
        mv = mem_ref[...]
        mn = (mv * _rinv(mv) * g_ref[...]).astype(BF16)
        mn_ref[...] = mn
        kx_ref[...] = _dot(mn, wk_ref[...]).astype(BF16)
        vx_ref[...] = _dot(mn, wv_ref[...]).astype(BF16)

    m = mem.shape[0]
    return pl.pallas_call(
        body, name="f_mem", out_shape=[_sds((m, 1024), BF16)] * 3,
        compiler_params=_cp())(mem, g_mem, wk, wv)


def _xattn_probs(q, k):
    s = _dot_nt(q, k) * X_SCALE
    m = jnp.max(s, axis=-1, keepdims=True)
    p = jnp.exp(s - m)
    return p, jnp.sum(p, axis=-1, keepdims=True)


def _f_mid(x, att, rec, g_oa, g_ol, w_out, g_cross, wq, kx, vx, wo, tm, comm=None):
    s_len = x.shape[0]
    m_len = kx.shape[0]

    def body(x_ref, att_ref, rec_ref, goa_ref, gol_ref, wout_ref, gc_ref, wq_ref, kx_ref, vx_ref, wo_ref,
             mg_ref, x1_ref, hc_ref, qx_ref, ox_ref, x2_ref):
        av = att_ref[...]
        rv = rec_ref[...]
        mg_ref[:, 0:D_ATT] = (av * _rinv(av) * goa_ref[...]).astype(BF16)
        mg_ref[:, D_ATT:1024] = (rv * _rinv(rv) * gol_ref[...]).astype(BF16)
        x1 = x_ref[...] + _dot(mg_ref[...], wout_ref[...])
        x1_ref[...] = x1
        hc = (x1 * _rinv(x1) * gc_ref[...]).astype(BF16)
        hc_ref[...] = hc
        qx_ref[...] = _dot(hc, wq_ref[...]).astype(BF16)
        for h in range(X_HEADS):
            sl = slice(h * X_HEAD_DIM, (h + 1) * X_HEAD_DIM)
            p, l = _xattn_probs(qx_ref[:, sl], kx_ref[:, sl])
            ox_ref[:, sl] = (_dot(p.astype(BF16), vx_ref[:, sl]) / l).astype(BF16)
        x2_ref[...] = x1 + _dot(ox_ref[...], wo_ref[...])

    sq = _full((1024, 1024))
    return _call(
        body, "f_mid", (s_len // tm,),
        [_rows(tm, 1024), _rows(tm, 512), _rows(tm, 512), _full((1, 512)), _full((1, 512)), sq,
         _full((1, 1024)), sq, _full((m_len, 1024)), _full((m_len, 1024)), sq],
        [_rows(tm, 1024)] * 6,
        [_sds((s_len, 1024), BF16), _sds((s_len, 1024), F32), _sds((s_len, 1024), BF16),
         _sds((s_len, 1024), BF16), _sds((s_len, 1024), BF16), _sds((s_len, 1024), F32)],
        [], (x, att, rec, g_oa, g_ol, w_out, g_cross, wq, kx, vx, wo), "arbitrary", comm)


FF_CHUNKS = [(0, 1280), (1280, D_FF)]


def _first_step_and_rest(step):
    pl.when(pl.program_id(0) == 0)(lambda: step(True))
    pl.when(pl.program_id(0) > 0)(lambda: step(False))


def _ffn_weights(first, pairs, sems):
    if not first:
        return lambda c, j: None
    copies = _start_copies([(hbm.at[c0:c1, :], vmem.at[c0:c1, :]) for c0, c1 in FF_CHUNKS for hbm, vmem in pairs],
                           sems)
    return lambda c, j: copies[c * len(pairs) + j].wait()


def _f_ffn(x2, tgt, g_ffn, g_final, wg, wu, wd, tm):
    s_len = x2.shape[0]

    def body(x2_ref, t_ref, gf_ref, gfin_ref, wg_hbm, wu_hbm, wd_hbm,
             hf_ref, g_ref, u_ref, a_ref, dx3_ref, loss_ref, dgfin_ref, wg_ref, wu_ref, wd_ref, w_sems):
        @pl.when(pl.program_id(0) == 0)
        def _():
            _copy_together([(wg_hbm, wg_ref), (wu_hbm, wu_ref), (wd_hbm, wd_ref)], w_sems)
            loss_ref[...] = jnp.zeros_like(loss_ref)
            dgfin_ref[...] = jnp.zeros_like(dgfin_ref)

        x2v = x2_ref[...]
        hf = (x2v * _rinv(x2v) * gf_ref[...]).astype(BF16)
        hf_ref[...] = hf
        x3 = x2v
        for c0, c1 in FF_CHUNKS:
            gv = _dot_nt(hf, wg_ref[c0:c1, :])
            uv = _dot_nt(hf, wu_ref[c0:c1, :])
            av = (gv * jax.nn.sigmoid(gv) * uv).astype(BF16)
            g_ref[:, c0:c1] = gv.astype(BF16)
            u_ref[:, c0:c1] = uv.astype(BF16)
            a_ref[:, c0:c1] = av
            x3 = x3 + _dot(av, wd_ref[c0:c1, :])
        r3 = _rinv(x3)
        yh = x3 * r3
        gfin = gfin_ref[...]
        err = yh * gfin - t_ref[...]
        loss_ref[...] += jnp.full((1, 128), 0.5 / D_MODEL, F32) * jnp.sum(err * err)
        dy = err * (1.0 / D_MODEL)
        dgfin_ref[...] += jnp.sum(dy * yh, axis=0, keepdims=True)
        dyh = dy * gfin
        dx3_ref[...] = r3 * (dyh - yh * jnp.mean(dyh * yh, axis=-1, keepdims=True))

    vec = _full((1, 1024))
    return pl.pallas_call(
        body, name="f_ffn", grid=(s_len // tm,),
        in_specs=[_rows(tm, 1024), _rows(tm, 1024), vec, vec, _any(), _any(), _any()],
        out_specs=[_rows(tm, 1024), _rows(tm, D_FF), _rows(tm, D_FF), _rows(tm, D_FF),
                   _rows(tm, 1024), _full((1, 128)), vec],
        out_shape=[_sds((s_len, 1024), BF16)] + [_sds((s_len, D_FF), BF16)] * 3
                  + [_sds((s_len, 1024), F32), _sds((1, 128), F32), _sds((1, 1024), F32)],
        scratch_shapes=[pltpu.VMEM((D_FF, 1024), BF16)] * 3 + [pltpu.SemaphoreType.DMA((3,))],
        compiler_params=_cp("arbitrary"))(x2, tgt, g_ffn, g_final, wg, wu, wd)


def _b_ffn(dx3, x2, gact, uact, g_ffn, wg, wu, wd, tm):
    s_len = x2.shape[0]

    def body(dx3_ref, x2_ref, g_ref, u_ref, gf_ref, wg_hbm, wu_hbm, wd_hbm,
             dg_ref, du_ref, dx2_ref, dgf_ref, wg_ref, wu_ref, wd_ref, w_sems):
        def step(first):
            if first:
                dgf_ref[...] = jnp.zeros_like(dgf_ref)
            ready = _ffn_weights(first, [(wd_hbm, wd_ref), (wg_hbm, wg_ref), (wu_hbm, wu_ref)], w_sems)
            dx3v = dx3_ref[...]
            dx3b = dx3v.astype(BF16)
            dhf = jnp.zeros(dx3v.shape, F32)
            for c, (c0, c1) in enumerate(FF_CHUNKS):
                ready(c, 0)
                da = _dot_nt(dx3b, wd_ref[c0:c1, :])
                gv = g_ref[:, c0:c1].astype(F32)
                uv = u_ref[:, c0:c1].astype(F32)
                sg = jax.nn.sigmoid(gv)
                dub = (da * gv * sg).astype(BF16)
                dgb = (da * uv * (sg * (1.0 + gv * (1.0 - sg)))).astype(BF16)
                du_ref[:, c0:c1] = dub
                dg_ref[:, c0:c1] = dgb
                ready(c, 1)
                ready(c, 2)
                dhf = dhf + _dot(dgb, wg_ref[c0:c1, :]) + _dot(dub, wu_ref[c0:c1, :])
            dx, dgf = _rms_bwd(dhf, x2_ref[...], gf_ref[...])
            dx2_ref[...] = dx3v + dx
            dgf_ref[...] += dgf

        _first_step_and_rest(step)

    vec = _full((1, 1024))
    return pl.pallas_call(
        body, name="b_ffn", grid=(s_len // tm,),
        in_specs=[_rows(tm, 1024), _rows(tm, 1024), _rows(tm, D_FF), _rows(tm, D_FF), vec,
                  _any(), _any(), _any()],
        out_specs=[_rows(tm, D_FF), _rows(tm, D_FF), _rows(tm, 1024), vec],
        out_shape=[_sds((s_len, D_FF), BF16)] * 2 + [_sds((s_len, 1024), F32), _sds((1, 1024), F32)],
        scratch_shapes=[pltpu.VMEM((D_FF, 1024), BF16)] * 3 + [pltpu.SemaphoreType.DMA((3 * len(FF_CHUNKS),))],
        compiler_params=_cp("arbitrary"))(dx3, x2, gact, uact, g_ffn, wg, wu, wd)


def _b_mid(dx2, qx, x1, att, rec, kx, vx, wo, wq, w_out, g_cross, g_oa, g_ol, tm, comm=None):
    s_len = x1.shape[0]
    m_len = kx.shape[0]

    def body(dx2_ref, qx_ref, x1_ref, att_ref, rec_ref, kx_ref, vx_ref, wo_ref, wq_ref, wout_ref,
             gc_ref, goa_ref, gol_ref,
             dqx_ref, dx1_ref, datt_ref, drec_ref, dkx_ref, dvx_ref, dgc_ref, dgoa_ref, dgol_ref):
        @pl.when(pl.program_id(0) == 0)
        def _():
            for r in (dkx_ref, dvx_ref, dgc_ref, dgoa_ref, dgol_ref):
                r[...] = jnp.zeros_like(r)

        dx2v = dx2_ref[...]
        dox = _dot_nt(dx2v.astype(BF16), wo_ref[...])
        for h in range(X_HEADS):
            sl = slice(h * X_HEAD_DIM, (h + 1) * X_HEAD_DIM)
            q = qx_ref[:, sl]
            p, l = _xattn_probs(q, kx_ref[:, sl])
            pn = p * (1.0 / l)
            dob = dox[:, sl].astype(BF16)
            dp = _dot_nt(dob, vx_ref[:, sl])
            dvx_ref[:, sl] += _dot_tn(pn.astype(BF16), dob)
            ds = pn * (dp - jnp.sum(dp * pn, axis=-1, keepdims=True))
            dsb = (ds * X_SCALE).astype(BF16)
            dqx_ref[:, sl] = _dot(dsb, kx_ref[:, sl]).astype(BF16)
            dkx_ref[:, sl] += _dot_tn(dsb, q)
        dhc = _dot_nt(dqx_ref[...], wq_ref[...])
        dx, dgc = _rms_bwd(dhc, x1_ref[...], gc_ref[...])
        dx1 = dx2v + dx
        dx1_ref[...] = dx1
        dgc_ref[...] += dgc
        dmg = _dot_nt(dx1.astype(BF16), wout_ref[...])
        da, dgoa = _rms_bwd(dmg[:, 0:D_ATT], att_ref[...], goa_ref[...])
        datt_ref[...] = da
        dgoa_ref[...] += dgoa
        dr, dgol = _rms_bwd(dmg[:, D_ATT:1024], rec_ref[...], gol_ref[...])
        drec_ref[...] = dr
        dgol_ref[...] += dgol

    sq = _full((1024, 1024))
    mk = _full((m_len, 1024))
    return _call(
        body, "b_mid", (s_len // tm,),
        [_rows(tm, 1024), _rows(tm, 1024), _rows(tm, 1024), _rows(tm, 512), _rows(tm, 512), mk, mk,
         sq, sq, sq, _full((1, 1024)), _full((1, 512)), _full((1, 512))],
        [_rows(tm, 1024), _rows(tm, 1024), _rows(tm, 512), _rows(tm, 512), mk, mk,
         _full((1, 1024)), _full((1, 512)), _full((1, 512))],
        [_sds((s_len, 1024), BF16), _sds((s_len, 1024), F32), _sds((s_len, 512), F32),
         _sds((s_len, 512), F32), _sds((m_len, 1024), F32), _sds((m_len, 1024), F32),
         _sds((1, 1024), F32), _sds((1, 512), F32), _sds((1, 512), F32)],
        [], (dx2, qx, x1, att, rec, kx, vx, wo, wq, w_out, g_cross, g_oa, g_ol), "arbitrary", comm)


def _b_mem(dkx, dvx, mem, mn, g_mem, wk, wv):
    def body(dkx_ref, dvx_ref, mem_ref, mn_ref, g_ref, wk_ref, wv_ref, dwk_ref, dwv_ref, dgm_ref,
             dwkb_ref, dwvb_ref):
        dkb = dkx_ref[...].astype(BF16)
        dvb = dvx_ref[...].astype(BF16)
        dwk = _dot_tn(mn_ref[...], dkb)
        dwv = _dot_tn(mn_ref[...], dvb)
        dwk_ref[...] = dwk
        dwv_ref[...] = dwv
        dwkb_ref[...] = dwk.astype(BF16)
        dwvb_ref[...] = dwv.astype(BF16)
        dmn = _dot_nt(dkb, wk_ref[...]) + _dot_nt(dvb, wv_ref[...])
        mv = mem_ref[...]
        dgm_ref[...] = jnp.sum(dmn * (mv * _rinv(mv)), axis=0, keepdims=True)

    return pl.pallas_call(
        body, name="b_mem",
        out_shape=[_sds((1024, 1024), F32), _sds((1024, 1024), F32), _sds((1, 1024), F32),
                   _sds((1024, 1024), BF16), _sds((1024, 1024), BF16)],
        compiler_params=_cp())(dkx, dvx, mem, mn, g_mem, wk, wv)


def _b_lru(drec, hs, u, xg, conv_w, wrg, brg, wig, big, lam, tl, comm=None):
    s_len = xg.shape[0]
    nt = s_len // tl

    def body(drec_ref, hs_ref, hsp_ref, u_ref, xg_ref, cw_ref, wrg_ref, brg_ref, wig_ref, big_ref, l_ref,
             dxg_ref, dwrg_ref, dwig_ref, dbrg_ref, dbig_ref, dlam_ref, dcw_ref, dcb_ref,
             hbuf, abuf, dubuf, c_sc, d_sc, lam_sc, lcar, wacc_r, wacc_i):
        i = pl.program_id(0)
        tt = nt - 1 - i

        @pl.when(i == 0)
        def _():
            for r in (wacc_r, wacc_i, dbrg_ref, dbig_ref, dlam_ref, dcw_ref, dcb_ref):
                r[...] = jnp.zeros_like(r)
            abuf[tl:tl + 8, :] = jnp.zeros((8, D_LRU), F32)
            dubuf[tl:tl + 8, :] = jnp.zeros((8, D_LRU), F32)
            lcar[...] = jnp.zeros((8, D_LRU), F32)

        xu0 = xg_ref[:, 0:D_LRU]
        hsv = hs_ref[...]
        uv = u_ref[...]
        hbuf[8:8 + tl, :] = hsv
        hbuf[0:8, :] = jnp.where(tt > 0, hsp_ref[...], 0.0)
        hshift = hbuf[pl.ds(7, tl), :]
        wrg_v = wrg_ref[...]
        wig_v = wig_ref[...]
        lamv = l_ref[...]
        ub, r, ig, sp, a, mult = _lru_gates(uv, wrg_v, brg_ref[...], wig_v, big_ref[...], lamv)
        abuf[0:tl, :] = a
        c_sc[...] = abuf[pl.ds(1, tl), :]
        gel, dgel = _gelu_and_grad(xg_ref[:, D_LRU:2 * D_LRU])
        drv = drec_ref[...]
        d_sc[...] = drv * gel
        dxg_ref[:, D_LRU:2 * D_LRU] = (drv * hsv * dgel).astype(BF16)

        def grp(k, lnext):
            off = pl.multiple_of((tl // 8 - 1 - k) * 8, 8)
            l8 = _rscan8(c_sc[pl.ds(off, 8), :], d_sc[pl.ds(off, 8), :], lnext)
            lam_sc[pl.ds(off, 8), :] = l8
            return l8[0:1, :]

        lcar[0:1, :] = lax.fori_loop(0, tl // 8, grp, lcar[0:1, :])
        abuf[tl:tl + 8, :] = a[0:8, :]
        db = lam_sc[...]
        da = db * hshift
        dmult = db * (ig * uv)
        dig = db * mult * uv
        du = db * mult * ig
        dla = da * a - dmult * (a * a) / mult
        dlam_ref[...] += jnp.sum(dla * (-LRU_C) * r, axis=0, keepdims=True)
        dzr = dla * (-LRU_C * sp) * r * (1.0 - r)
        dzi = dig * ig * (1.0 - ig)
        dzrb = dzr.astype(BF16)
        dzib = dzi.astype(BF16)
        du = du + _dot_nt(dzrb, wrg_v) + _dot_nt(dzib, wig_v)
        wacc_r[...] += _dot_tn(ub, dzrb)
        wacc_i[...] += _dot_tn(ub, dzib)
        dbrg_ref[...] += jnp.sum(dzr, axis=0, keepdims=True)
        dbig_ref[...] += jnp.sum(dzi, axis=0, keepdims=True)
        dcb_ref[...] += jnp.sum(du, axis=0, keepdims=True)
        dubuf[0:tl, :] = du
        dxu0 = jnp.zeros((tl, D_LRU), F32)
        for j in range(4):
            dsh = dubuf[pl.ds(3 - j, tl), :]
            dxu0 = dxu0 + cw_ref[j:j + 1, :] * dsh
            dcw_ref[j:j + 1, :] += jnp.sum(xu0 * dsh, axis=0, keepdims=True)
        dubuf[tl:tl + 8, :] = du[0:8, :]
        dxg_ref[:, 0:D_LRU] = dxu0.astype(BF16)

        @pl.when(i == nt - 1)
        def _():
            dlam_ref[...] = dlam_ref[...] * (-jax.nn.sigmoid(-lamv))
            for n in range(LRU_BLOCKS):
                blk = slice(n * LRU_BLOCK, (n + 1) * LRU_BLOCK)
                dwrg_ref[n] = wacc_r[blk, blk]
                dwig_ref[n] = wacc_i[blk, blk]

    def rev(n):
        return pl.BlockSpec((tl, n), lambda i: (nt - 1 - i, 0))

    prev8 = pl.BlockSpec((8, D_LRU), lambda i: (jnp.maximum((nt - 1 - i) * (tl // 8) - 1, 0), 0))
    vec = _full((1, D_LRU))
    sq = _full((D_LRU, D_LRU))
    blocks_shape = (LRU_BLOCKS, LRU_BLOCK, LRU_BLOCK)
    blocks = _full(blocks_shape)
    return _call(
        body, "b_lru", (nt,),
        [rev(D_LRU), rev(D_LRU), prev8, rev(D_LRU), rev(1024), _full((4, D_LRU)), sq, vec, sq, vec, vec],
        [rev(1024), blocks, blocks, vec, vec, vec, _full((4, D_LRU)), vec],
        [_sds((s_len, 1024), BF16), _sds(blocks_shape, F32), _sds(blocks_shape, F32),
         _sds((1, D_LRU), F32), _sds((1, D_LRU), F32), _sds((1, D_LRU), F32),
         _sds((4, D_LRU), F32), _sds((1, D_LRU), F32)],
        [pltpu.VMEM((tl + 8, D_LRU), F32)] * 3 + [pltpu.VMEM((tl, D_LRU), F32)] * 3
        + [pltpu.VMEM((8, D_LRU), F32)] + [pltpu.VMEM((D_LRU, D_LRU), F32)] * 2,
        (drec, hs, hs, u, xg, conv_w, wrg, brg, wig, big, lam), "arbitrary", comm)


def _b_attn(qkv_pad, att, datt, frow, comm=None):
    s_len = datt.shape[0]
    nb = s_len // QB
    n_pair = ATT_HEADS // 2
    pair_w = 2 * HEAD_DIM

    def body(q_ref, k0, k1, k2, v0, v1, v2, o_ref, do_ref, frow_ref, dq_ref, dkv_ref, dfrow_ref,
             bias_sc, dt_sc, acc_sc):
        t = pl.program_id(0)

        @pl.when(t == 0)
        def _():
            _bias_table(frow_ref, bias_sc)
            dt_sc[...] = jnp.zeros_like(dt_sc)
            acc_sc[...] = jnp.zeros_like(acc_sc)

        @pl.when(t < nb)
        def _():
            var = jnp.minimum(t, N_BIAS - 1)
            even = _even_lanes()
            for hp in range(n_pair):
                cs = slice(hp * pair_w, (hp + 1) * pair_w)
                qt = q_ref[:, cs]
                kts = [k0[:, cs], k1[:, cs], k2[:, cs]]
                vts = [v0[:, cs], v1[:, cs], v2[:, cs]]
                kcat = jnp.concatenate(kts, axis=0)
                dot = do_ref[:, cs]
                dd = dot * o_ref[:, cs]
                dos_pair, dsbs, pbs, dqs = None, [], [], []
                for e in range(2):
                    keep = even if e == 0 else jnp.logical_not(even)
                    qm = jnp.where(keep, qt, 0)
                    p = _att_probs(qm, kts, bias_sc[var, 2 * hp + e])
                    inv = 1.0 / jnp.sum(p, axis=-1, keepdims=True)
                    dos = jnp.where(keep, dot * inv, 0.0)
                    delta = jnp.sum(jnp.where(keep, dd, 0.0), axis=-1, keepdims=True) * inv
                    dp = jnp.concatenate([_dot_nt(dos.astype(BF16), v) for v in vts], axis=1)
                    ds = p * (dp - delta)
                    dt_sc[2 * hp + e] += ds
                    dsb = ds.astype(BF16)
                    dq = _dot(dsb, kcat)
                    dqs.append(dq)
                    dsbs.append(dsb)
                    pbs.append(p.astype(BF16))
                    dos_pair = dos if e == 0 else dos_pair + dos
                dq_ref[:, cs] = (jnp.where(even, dqs[0], dqs[1]) * ATT_SCALE).astype(BF16)
                qtt = qt.astype(F32).T.astype(BF16)
                dost = dos_pair.T.astype(BF16)
                for j in range(3):
                    slot = (t + 1 + j) % 3
                    js = slice(j * QB, (j + 1) * QB)
                    for e in range(2):
                        hr = slice(e * HEAD_DIM, (e + 1) * HEAD_DIM)
                        acc_sc[slot, hp, hr, :] += _dot(qtt[hr], dsbs[e][:, js])
                        acc_sc[slot, n_pair + hp, hr, :] += _dot(dost[hr], pbs[e][:, js])

        done = (t + 1) % 3

        @pl.when(t >= 2)
        def _():
            for i in range(2 * n_pair):
                dkv_ref[:, i * pair_w:(i + 1) * pair_w] = acc_sc[done, i].T.astype(BF16)

        acc_sc[done] = jnp.zeros((2 * n_pair, pair_w, QB), F32)

        @pl.when(t == nb + 1)
        def _():
            row = lax.broadcasted_iota(jnp.int32, (8, ROLL_W), 0)
            pad = jnp.zeros((8, ROLL_W - KB), F32)
            for h in range(ATT_HEADS):
                acc8 = jnp.concatenate([dt_sc[h, 0:8, :], pad], axis=1)
                for a1 in range(1, QB // 8):
                    blk = jnp.concatenate([dt_sc[h, 8 * a1:8 * a1 + 8, :], pad], axis=1)
                    acc8 = acc8 + pltpu.roll(blk, ROLL_W - 8 * a1, 1)
                for k in range(3):
                    acc8 = jnp.where(((row >> k) & 1) == 1, pltpu.roll(acc8, ROLL_W - (1 << k), 1), acc8)
                dfrow_ref[h:h + 1, :] = jnp.sum(acc8, axis=0, keepdims=True)

    clamp = lambda t: jnp.minimum(t, nb - 1)
    qrows = pl.BlockSpec((QB, D_ATT), lambda t: (clamp(t), 0))
    return _call(
        body, "b_attn", (nb + 2,),
        _att_in_specs(clamp) + [qrows, qrows, _full((ATT_HEADS, ROLL_W))],
        [qrows, pl.BlockSpec((QB, 2 * D_ATT), lambda t: (jnp.maximum(t - 2, 0), 0)),
         _full((ATT_HEADS, ROLL_W))],
        [_sds((s_len, D_ATT), BF16), _sds((s_len, 2 * D_ATT), BF16), _sds((ATT_HEADS, ROLL_W), F32)],
        [pltpu.VMEM((N_BIAS, ATT_HEADS, QB, KB), F32), pltpu.VMEM((ATT_HEADS, QB, KB), F32),
         pltpu.VMEM((3, 2 * n_pair, pair_w, QB), F32)],
        (*([qkv_pad] * 7), att, datt, frow), "arbitrary", comm)


def _b_win(dq, dkv, dxg, h, ts):
    s_len = h.shape[0]
    steps = s_len // ts

    def body(dq_ref, dkv_ref, dxg_ref, h_ref, dw_hbm, dwb_hbm, acc, accb, sems):
        @pl.when(pl.program_id(0) == 0)
        def _():
            acc[...] = jnp.zeros_like(acc)

        @pl.when(pl.program_id(0) < steps - 1)
        def _():
            dproj = jnp.concatenate([dq_ref[...], dkv_ref[...], dxg_ref[...]], axis=1)
            acc[...] += _dot_tn(h_ref[...], dproj)

        @pl.when(pl.program_id(0) == steps - 1)
        def _():
            dproj = jnp.concatenate([dq_ref[...], dkv_ref[...], dxg_ref[...]], axis=1)
            copies = []
            for s in range(N_SHARD):
                cols = slice(s * IN_SH, (s + 1) * IN_SH)
                total = acc[:, cols] + _dot_tn(h_ref[...], dproj[:, cols])
                acc[:, cols] = total
                accb[:, cols] = total.astype(BF16)
                copies += _start_copies([(acc.at[:, cols], dw_hbm.at[s]), (accb.at[:, cols], dwb_hbm.at[s])],
                                        sems, 2 * s)
            for cp in copies:
                cp.wait()

    shape = (N_SHARD, 1024, IN_SH)
    return pl.pallas_call(
        body, name="b_win", grid=(steps,),
        in_specs=[_rows(ts, 512), _rows(ts, 1024), _rows(ts, 1024), _rows(ts, 1024)],
        out_specs=[_any()] * 2, out_shape=[_sds(shape, F32), _sds(shape, BF16)],
        scratch_shapes=[pltpu.VMEM((1024, D_IN), F32), pltpu.VMEM((1024, D_IN), BF16),
                        pltpu.SemaphoreType.DMA((2 * N_SHARD,))],
        compiler_params=_cp("arbitrary"))(dq, dkv, dxg, h)


RING = 3


def _ring_tiles(hbm_refs, bufs, sems, tm, steps):
    i = pl.program_id(0)

    def copies(step):
        slot = step % RING
        return [pltpu.make_async_copy(h.at[pl.ds(step * tm, tm), :], b.at[slot], sems.at[k, slot])
                for k, (h, b) in enumerate(zip(hbm_refs, bufs))]

    @pl.when(i == 0)
    def _():
        for s in range(min(RING - 1, steps)):
            for cp in copies(s):
                cp.start()

    @pl.when(i + RING - 1 < steps)
    def _():
        for cp in copies(i + RING - 1):
            cp.start()

    for cp in copies(i):
        cp.wait()
    return [b.at[i % RING] for b in bufs]


def _b_inproj(dq, dkv, dxg, x, dx1, g_mix, w_in_g, tm, comm=None):
    s_len = x.shape[0]
    steps = s_len // tm

    def body(dq_ref, dkv_ref, dxg_ref, x_hbm, dx1_hbm, g_ref, w_hbm, gx_ref, dgm_ref, w_ref, w_sems,
             xbuf, dx1buf, ring_sems):
        _load_w_in_once(w_hbm, w_ref, w_sems)

        @pl.when(pl.program_id(0) == 0)
        def _():
            dgm_ref[...] = jnp.zeros_like(dgm_ref)

        x_ref, dx1_ref = _ring_tiles([x_hbm, dx1_hbm], [xbuf, dx1buf], ring_sems, tm, steps)
        dproj = jnp.concatenate([dq_ref[...], dkv_ref[...], dxg_ref[...]], axis=1)
        dh = _dot_nt(dproj, w_ref[...])
        dx, dgm = _rms_bwd(dh, x_ref[...], g_ref[...])
        gx_ref[...] = dx1_ref[...] + dx
        dgm_ref[...] += dgm

    return _call(
        body, "b_inproj", (steps,),
        [_rows(tm, 512), _rows(tm, 1024), _rows(tm, 1024), _any(), _any(), _full((1, 1024)), _any()],
        [_rows(tm, 1024), _full((1, 1024))],
        [_sds((s_len, 1024), F32), _sds((1, 1024), F32)],
        [pltpu.VMEM((1024, D_IN), BF16), pltpu.SemaphoreType.DMA((N_SHARD,)),
         pltpu.VMEM((RING, tm, 1024), F32), pltpu.VMEM((RING, tm, 1024), F32), pltpu.SemaphoreType.DMA((2, RING))],
        (dq, dkv, dxg, x, dx1, g_mix, w_in_g), "arbitrary", comm)


MXU_DIM_V7X = 256
FLUSH_GROUPS = 4


def _mm_tn(xa, ya, name, ts):
    s_len, k = xa.shape
    n = ya.shape[1]

    steps = s_len // ts
    tiles = k // MXU_DIM_V7X
    edges = [MXU_DIM_V7X * ((tiles * g) // FLUSH_GROUPS) for g in range(FLUSH_GROUPS + 1)]

    def body(x_ref, y_ref, o_hbm, ob_hbm, acc, accb, sems):
        @pl.when(pl.program_id(0) == 0)
        def _():
            acc[...] = jnp.zeros_like(acc)

        @pl.when(pl.program_id(0) < steps - 1)
        def _():
            acc[...] += _dot_tn(x_ref[...].astype(BF16), y_ref[...].astype(BF16))

        @pl.when(pl.program_id(0) == steps - 1)
        def _():
            yb = y_ref[...].astype(BF16)
            copies = []
            for g in range(FLUSH_GROUPS):
                rows = slice(edges[g], edges[g + 1])
                total = acc[rows, :] + _dot_tn(x_ref[:, rows].astype(BF16), yb)
                acc[rows, :] = total
                accb[rows, :] = total.astype(BF16)
                copies += _start_copies([(acc.at[rows, :], o_hbm.at[rows, :]), (accb.at[rows, :], ob_hbm.at[rows, :])],
                                        sems, 2 * g)
            for cp in copies:
                cp.wait()

    return pl.pallas_call(
        body, name=name, grid=(steps,), in_specs=[_rows(ts, k), _rows(ts, n)],
        out_specs=[_any()] * 2, out_shape=[_sds((k, n), F32), _sds((k, n), BF16)],
        scratch_shapes=[pltpu.VMEM((k, n), F32), pltpu.VMEM((k, n), BF16),
                        pltpu.SemaphoreType.DMA((2 * FLUSH_GROUPS,))],
        compiler_params=_cp("arbitrary"))(xa, ya)


PAD_KEYS = LEFT_CHUNKS * CHUNK
F_HI = PAD_KEYS - MAX_REL + 1
F_LO = PAD_KEYS + MAX_REL


def _frow_from_rel_bias(rb):
    last = rb[:, 2 * MAX_REL:2 * MAX_REL + 1]
    hi = jnp.broadcast_to(last, (ATT_HEADS, F_HI))
    mid = rb[:, 1:2 * MAX_REL][:, ::-1]
    lo = jnp.broadcast_to(rb[:, 0:1], (ATT_HEADS, KB - F_LO))
    wrap = jnp.broadcast_to(last, (ATT_HEADS, ROLL_W - KB))
    return jnp.concatenate([hi, mid, lo, wrap], axis=1)


def _rel_bias_grad_from_dfrow(df):
    g_last = jnp.sum(df[:, 0:F_HI], axis=1, keepdims=True) + jnp.sum(df[:, KB:ROLL_W], axis=1, keepdims=True)
    mid = df[:, F_HI:F_LO][:, ::-1]
    g_first = jnp.sum(df[:, F_LO:KB], axis=1, keepdims=True)
    return jnp.concatenate([g_first, mid, g_last], axis=1)


def _block_diag(w):
    eye = jnp.eye(8, dtype=w.dtype)
    return (w[:, :, None, :] * eye[:, None, :, None]).reshape(D_LRU, D_LRU)


MID = ['w_out', 'wq_c', 'wk_c', 'wv_c', 'wo_c']
TRANSPOSED = ['w_gate', 'w_up']
AG_IN_INPROJ = ['w_out', 'wq_c', 'wk_c']
AG_IN_ATTN = ['wv_c', 'wo_c', 'w_gate']
AG_IN_LRU = ['w_up']
AG_IN_MID = ['w_down']
RS_IN_MID = ['w_gate', 'w_up']
RS_IN_LRU = ['w_down']
RS_IN_ATTN = MID


def _local_step(x, mem, tgt, p, gw, shards=None, chip=None):
    s_len = x.shape[0]
    tm = min(256, s_len)
    tmb = min(512, s_len)
    tl = min(512, s_len)
    frow = _frow_from_rel_bias(p['rel_bias'])
    wrg = _block_diag(p['w_rg']).astype(BF16)
    wig = _block_diag(p['w_ig']).astype(BF16)
    gw = dict(gw)

    big, bigb, recv, part, sib = {}, {}, {}, {}, {}

    def ag(names):
        return [] if shards is None else [("ag", [shards[n] for n in names])]

    def rs(names):
        return [] if shards is None else [("rs", [bigb[n] for n in names])]

    def swap(names):
        return [] if shards is None else [("swap", [part[n] for n in names])]

    def reduce_own(names):
        if shards is not None:
            sums = _sum_parts([big[n] for n in names], [recv[n] for n in names], chip, "sum_" + names[0])
            part.update(zip(names, sums))

    h, qkv_pad, xg, *got = _f_inproj(x, p['g_mix'], gw['w_in'], tmb, ag(AG_IN_INPROJ))
    gw.update(zip(AG_IN_INPROJ, got))
    att, *got = _f_attn(qkv_pad, frow, ag(AG_IN_ATTN))
    gw.update(zip(AG_IN_ATTN, got))
    rec, u, hs, *got = _f_lru(xg, p['conv_w'], p['conv_b'], wrg, p['b_rg'], wig, p['b_ig'], p['lru_L'], tl,
                              ag(AG_IN_LRU))
    gw.update(zip(AG_IN_LRU, got))
    w_out = gw['w_out'].reshape(1024, 1024)
    wq = gw['wq_c'].reshape(1024, 1024)
    wk = gw['wk_c'].reshape(1024, 1024)
    wv = gw['wv_c'].reshape(1024, 1024)
    wo = gw['wo_c'].reshape(1024, 1024)
    mn, kx, vx = _f_mem(mem, p['g_mem'], wk, wv)
    mg, x1, hc, qx, ox, x2, *got = _f_mid(x, att, rec, p['g_out_attn'], p['g_out_lru'], w_out, p['g_cross'],
                                          wq, kx, vx, wo, tmb, ag(AG_IN_MID))
    gw.update(zip(AG_IN_MID, got))
    ffn_w = [gw[n].reshape(D_FF, 1024) for n in ('w_gate', 'w_up', 'w_down')]
    hf, gact, uact, aact, dx3, loss, dg_final = _f_ffn(x2, tgt, p['g_ffn'], p['g_final'], *ffn_w, tmb)

    ts = min(1024, s_len)
    dgact, duact, dx2, dg_ffn = _b_ffn(dx3, x2, gact, uact, p['g_ffn'], *ffn_w, tm)
    big['w_gate'], bigb['w_gate'] = _mm_tn(dgact, hf, "dw_gate", ts)
    big['w_up'], bigb['w_up'] = _mm_tn(duact, hf, "dw_up", ts)
    big['w_down'], bigb['w_down'] = _mm_tn(aact, dx3, "dw_down", ts)
    for n in ('w_gate', 'w_up', 'w_down'):
        big[n] = big[n].reshape(N_SHARD, FF_SH, 1024)
        bigb[n] = bigb[n].reshape(N_SHARD, FF_SH, 1024)

    dqx, dx1, datt, drec, dkx, dvx, dg_cross, dg_oa, dg_ol, *got = _b_mid(
        dx2, qx, x1, att, rec, kx, vx, wo, wq, w_out, p['g_cross'], p['g_out_attn'], p['g_out_lru'], tmb,
        rs(RS_IN_MID))
    recv.update(zip(RS_IN_MID, got))
    reduce_own(RS_IN_MID)
    dwk, dwv, dg_mem, dwkb, dwvb = _b_mem(dkx, dvx, mem, mn, p['g_mem'], wk, wv)
    big['wk_c'], bigb['wk_c'] = dwk, dwkb
    big['wv_c'], bigb['wv_c'] = dwv, dwvb
    big['w_out'], bigb['w_out'] = _mm_tn(mg, dx1, "dw_out", ts)
    big['wq_c'], bigb['wq_c'] = _mm_tn(hc, dqx, "dw_q", ts)
    big['wo_c'], bigb['wo_c'] = _mm_tn(ox, dx2, "dw_o", ts)
    for n in MID:
        big[n] = big[n].reshape(N_SHARD, 256, 1024)
        bigb[n] = bigb[n].reshape(N_SHARD, 256, 1024)

    dxg, dwrg, dwig, dbrg, dbig, dlam, dcw, dcb, *got = _b_lru(
        drec, hs, u, xg, p['conv_w'], wrg, p['b_rg'], wig, p['b_ig'], p['lru_L'], tl,
        rs(RS_IN_LRU) + swap(RS_IN_MID))
    recv.update(zip(RS_IN_LRU, got))
    sib.update(zip(RS_IN_MID, got[len(RS_IN_LRU):]))
    reduce_own(RS_IN_LRU)
    small = {
        'conv_w': dcw, 'conv_b': dcb, 'w_rg': dwrg, 'b_rg': dbrg, 'w_ig': dwig, 'b_ig': dbig, 'lru_L': dlam,
        'g_out_attn': dg_oa, 'g_out_lru': dg_ol, 'g_cross': dg_cross, 'g_mem': dg_mem, 'g_ffn': dg_ffn,
        'g_final': dg_final,
    }
    names = [n for n in SMALL if n in small]
    gather = [] if shards is None else [
        ("ag8", [_pack_small(names, [small[n] for n in names], loss, PACK_ROWS, "pack_small")])]
    dq, dkv, dfrow, *got = _b_attn(qkv_pad, att, datt, frow, rs(RS_IN_ATTN) + swap(RS_IN_LRU) + gather)
    recv.update(zip(RS_IN_ATTN, got))
    sib.update(zip(RS_IN_LRU, got[len(RS_IN_ATTN):]))
    packs = got[-1] if gather else None
    reduce_own(RS_IN_ATTN)
    small['rel_bias'] = _rel_bias_grad_from_dfrow(dfrow)
    big['w_in'], bigb['w_in'] = _b_win(dq, dkv, dxg, h, ts)
    if shards is None:
        grad_x, small['g_mix'] = _b_inproj(dq, dkv, dxg, x, dx1, p['g_mix'], gw['w_in'], tmb)
    else:
        nsw = len(RS_IN_ATTN)

        def copies(refs, send_sems, recv_sems):
            return _tail_copies(refs[0], refs[1], refs[2:2 + nsw], refs[2 + nsw:2 + 2 * nsw], send_sems, recv_sems)

        slots = bigb['w_in']
        bufs = ([slots, lax.empty((3,) + slots.shape[1:], slots.dtype)] + [part[n] for n in RS_IN_ATTN]
                + [lax.empty(part[n].shape, F32) for n in RS_IN_ATTN])
        sems, bufs, token = _split_start("tail_exchange_start", bufs, 3 + nsw, copies)
        grad_x, small['g_mix'] = _b_inproj(dq, dkv, dxg, x, dx1, p['g_mix'] + token[0, 0], gw['w_in'], tmb)
        bufs = _split_wait("tail_exchange_wait", sems, bufs, 3 + nsw, copies, small['g_mix'])
        recv['w_in'] = bufs[1]
        sib.update(zip(RS_IN_ATTN, bufs[2 + nsw:]))
    reduce_own(['w_in'])
    return loss, grad_x, small, big, part, sib, packs


CAST_STEPS = 4


def _cast_shards(ws, name, comm=None):
    def body(*refs):
        n = len(refs) // 2
        for src, dst in zip(refs[:n], refs[n:]):
            dst[...] = src[...].astype(BF16)

    specs = [_rows(w.shape[0] // CAST_STEPS, w.shape[1]) for w in ws]
    return _call(body, name, (CAST_STEPS,), specs, specs, [_sds(w.shape, BF16) for w in ws], [], tuple(ws),
                 "arbitrary", comm)


def _sum_parts(own4s, recv3s, chip, name):
    n = len(own4s)
    _, r, c = own4s[0].shape
    steps = _ew_steps(r, n * c * (4 + 3 * 2 + 4))
    tr = r // steps

    def body(chip_ref, *refs):
        for own_ref, rc_ref, o_ref in zip(refs[:n], refs[n:2 * n], refs[2 * n:]):
            o_ref[...] = ((own_ref[0] + rc_ref[0].astype(F32)) + rc_ref[1].astype(F32)) + rc_ref[2].astype(F32)

    grid_spec = pltpu.PrefetchScalarGridSpec(
        num_scalar_prefetch=1, grid=(steps,),
        in_specs=[pl.BlockSpec((1, tr, c), lambda i, ch: (ch[0], i, 0))] * n
                 + [pl.BlockSpec((3, tr, c), lambda i, ch: (0, i, 0))] * n,
        out_specs=[pl.BlockSpec((tr, c), lambda i, ch: (i, 0))] * n)
    return pl.pallas_call(body, name=name, grid_spec=grid_spec, out_shape=[_sds((r, c), F32)] * n,
                          compiler_params=_cp("parallel"))(chip, *own4s, *recv3s)


def _adamw_math(w, g, m, v):
    m = ADAM_B1 * m + (1.0 - ADAM_B1) * g
    v = ADAM_B2 * v + (1.0 - ADAM_B2) * (g * g)
    m_hat = m / (1.0 - ADAM_B1 ** ADAM_STEP)
    v_hat = v / (1.0 - ADAM_B2 ** ADAM_STEP)
    delta = -ADAM_LR * (m_hat / (jnp.sqrt(v_hat) + ADAM_EPS) + ADAM_WD * w)
    return delta, m, v


def _final_adamw(pas, pbs, ws, ms, vs, name, after=None):
    n = len(ws)
    r, c = ws[0].shape
    steps = _ew_steps(r, n * c * 9 * 4)
    tr = r // steps

    def body(*refs):
        ins, outs = refs[:5 * n], refs[len(refs) - 4 * n:]
        for k in range(n):
            pa_ref, pb_ref, w_ref, m_ref, v_ref = (ins[j * n + k] for j in range(5))
            g = pa_ref[...] + pb_ref[...]
            outs[4 * k][...] = g
            outs[4 * k + 1][...], outs[4 * k + 2][...], outs[4 * k + 3][...] = _adamw_math(
                w_ref[...], g, m_ref[...], v_ref[...])

    order = [] if after is None else [after]
    res = pl.pallas_call(
        body, name=name, grid=(steps,), in_specs=[_rows(tr, c)] * (5 * n) + [_full(t.shape) for t in order],
        out_specs=[_rows(tr, c)] * (4 * n), out_shape=[_sds((r, c), F32)] * (4 * n),
        compiler_params=_cp("parallel"))(*pas, *pbs, *ws, *ms, *vs, *order)
    return [res[4 * k:4 * k + 4] for k in range(n)]


def _pack_put(ref, name, val_ref):
    r = _pack_rows()[name]
    shape = val_ref.shape
    if len(shape) == 3:
        for b in range(shape[0]):
            ref[r:r + shape[1], b * shape[2]:(b + 1) * shape[2]] = val_ref[b]
    elif shape[1] == 2 * PACK_W:
        ref[r:r + 1, :] = val_ref[:, 0:PACK_W]
        ref[r + 1:r + 2, :] = val_ref[:, PACK_W:2 * PACK_W]
    else:
        ref[r:r + shape[0], 0:shape[1]] = val_ref[...]


def _pack_get(ref, name, shape):
    r = _pack_rows()[name]
    if len(shape) == 3:
        return jnp.stack([ref[r:r + shape[1], b * shape[2]:(b + 1) * shape[2]] for b in range(shape[0])])
    if shape[1] == 2 * PACK_W:
        return jnp.concatenate([ref[r:r + 1, :], ref[r + 1:r + 2, :]], axis=1)
    return ref[r:r + shape[0], 0:shape[1]]


def _pack_small(names, g, loss, rows, name):
    n = len(g)
    extra = [] if loss is None else [loss]

    def body(*refs):
        pack = refs[-1]
        pack[...] = jnp.zeros_like(pack)
        for a, nm in enumerate(names):
            _pack_put(pack, nm, refs[a])
        if extra:
            _pack_put(pack, 'loss', refs[n])

    return pl.pallas_call(body, name=name, out_shape=_sds((rows, PACK_W), F32), compiler_params=_cp())(*g, *extra)


def _all_peers():
    x, y, c = _mesh_pos()
    peers = []
    for k in range(1, 8):
        px = 1 - x if k & 4 else x
        py = 1 - y if k & 2 else y
        pc = 1 - c if k & 1 else c
        peers.append(((px, py, pc), 4 * px + 2 * py + pc))
    return peers, 4 * x + 2 * y + c


def _ag8_copies(ins, outs, sems):
    send_sems, recv_sems, loc_sems = sems
    n = len(ins)
    peers, me = _all_peers()

    def remote(k, j, slot):
        return pltpu.make_async_remote_copy(
            src_ref=ins[k], dst_ref=outs[k].at[slot], send_sem=send_sems.at[k, j], recv_sem=recv_sems.at[k, j],
            device_id=peers[j][0], device_id_type=MESH_ID)

    def local(k):
        return pltpu.make_async_copy(ins[k], outs[k].at[me], loc_sems.at[k])

    def start():
        for k in range(n):
            local(k).start()
            for j in range(7):
                remote(k, j, me).start()

    def wait():
        for k in range(n):
            for j in range(7):
                remote(k, j, peers[j][1]).wait_recv()
        for k in range(n):
            for j in range(7):
                remote(k, j, me).wait_send()
            local(k).wait()

    return start, _no_forward, wait


def _adamw_small(packs, late_own, late_packs, g_shapes, loss_shape, w, m, v):
    n = len(w)

    def body(*refs):
        packs_ref, own_ref, late_ref = refs[0], refs[1], refs[2]
        w_refs, m_refs, v_refs = (refs[3 + i * n:3 + (i + 1) * n] for i in range(3))
        o0 = 3 * n + 3
        go, do, mo, vo = (refs[o0 + i * n:o0 + (i + 1) * n] for i in range(4))
        loss_out, tot_ref = refs[o0 + 4 * n], refs[o0 + 4 * n + 1]
        x, y, c = _mesh_pos()
        me = 4 * x + 2 * y + c
        tot = packs_ref[0]
        late = jnp.where(me == 0, own_ref[...], late_ref[0])
        for d in range(1, 8):
            tot = tot + packs_ref[d]
            late = late + jnp.where(me == d, own_ref[...], late_ref[d])
        tot_ref[...] = tot
        tot_ref[0:LATE_ROWS, :] += late
        loss_out[...] = _pack_get(tot_ref, 'loss', loss_shape)
        for a, name in enumerate(SMALL):
            if name == 'conv_w':
                r = _pack_rows()[name]
                ga = tot_ref[r:r + g_shapes[a][0], pl.ds(pl.multiple_of((2 * x + y) * 128, 128), 128)]
            else:
                ga = _pack_get(tot_ref, name, g_shapes[a])
            go[a][...] = ga
            do[a][...], mo[a][...], vo[a][...] = _adamw_math(w_refs[a][...], ga, m_refs[a][...], v_refs[a][...])

    out_shape = [_sds(a.shape, F32) for a in w] * 4 + [_sds(loss_shape, F32)]
    return pl.pallas_call(body, name="adamw_small", out_shape=out_shape,
                          scratch_shapes=[pltpu.VMEM((PACK_ROWS, PACK_W), F32)],
                          compiler_params=_cp())(packs, late_own, late_packs, *w, *m, *v)


PACK_W = 512
PACK_ROWS = 160
LATE = ['g_mix', 'rel_bias']
LATE_ROWS = 32


def _pack_rows():
    rows, r = {}, 0
    for name in ['g_mix', 'g_cross', 'g_mem', 'g_ffn', 'g_final']:
        rows[name] = r
        r += 2
    for name in ['conv_b', 'b_rg', 'b_ig', 'lru_L', 'g_out_attn', 'g_out_lru']:
        rows[name] = r
        r += 1
    rows['conv_w'] = r
    rows['loss'] = r + 4
    rows['rel_bias'] = 24
    rows['w_rg'] = 32
    rows['w_ig'] = 32 + LRU_BLOCK
    assert r + 5 <= 24 and rows['w_ig'] + LRU_BLOCK == PACK_ROWS
    assert rows['g_mix'] + 2 <= LATE_ROWS and rows['rel_bias'] + 8 <= LATE_ROWS
    return rows


INPUT_NAMES = (['x', 'mem'] + WEIGHTS + ['loss_target'] + ['m_' + n for n in WEIGHTS] + ['v_' + n for n in WEIGHTS])


def kernel(x, mem, g_mix, w_in, rel_bias, conv_w, conv_b, w_rg, b_rg, w_ig, b_ig, lru_L, g_out_attn, g_out_lru, w_out, g_cross, g_mem, wq_c, wk_c, wv_c, wo_c, g_ffn, w_gate, w_up, w_down, g_final, loss_target, m_g_mix, m_w_in, m_rel_bias, m_conv_w, m_conv_b, m_w_rg, m_b_rg, m_w_ig, m_b_ig, m_lru_L, m_g_out_attn, m_g_out_lru, m_w_out, m_g_cross, m_g_mem, m_wq_c, m_wk_c, m_wv_c, m_wo_c, m_g_ffn, m_w_gate, m_w_up, m_w_down, m_g_final, v_g_mix, v_w_in, v_rel_bias, v_conv_w, v_conv_b, v_w_rg, v_b_rg, v_w_ig, v_b_ig, v_lru_L, v_g_out_attn, v_g_out_lru, v_w_out, v_g_cross, v_g_mem, v_wq_c, v_wk_c, v_wv_c, v_wo_c, v_g_ffn, v_w_gate, v_w_up, v_w_down, v_g_final):
    a = dict(zip(INPUT_NAMES, (x, mem, g_mix, w_in, rel_bias, conv_w, conv_b, w_rg, b_rg, w_ig, b_ig, lru_L, g_out_attn, g_out_lru, w_out, g_cross, g_mem, wq_c, wk_c, wv_c, wo_c, g_ffn, w_gate, w_up, w_down, g_final, loss_target, m_g_mix, m_w_in, m_rel_bias, m_conv_w, m_conv_b, m_w_rg, m_b_rg, m_w_ig, m_b_ig, m_lru_L, m_g_out_attn, m_g_out_lru, m_w_out, m_g_cross, m_g_mem, m_wq_c, m_wk_c, m_wv_c, m_wo_c, m_g_ffn, m_w_gate, m_w_up, m_w_down, m_g_final, v_g_mix, v_w_in, v_rel_bias, v_conv_w, v_conv_b, v_w_rg, v_b_rg, v_w_ig, v_b_ig, v_lru_L, v_g_out_attn, v_g_out_lru, v_w_out, v_g_cross, v_g_mem, v_wq_c, v_wk_c, v_wv_c, v_wo_c, v_g_ffn, v_w_gate, v_w_up, v_w_down, v_g_final)))
    chip = 2 * lax.axis_index("x") + lax.axis_index("y")

    def shard(name):
        arr = a[name][0]
        base = name[2:] if name[:2] in ('m_', 'v_') else name
        return jnp.swapaxes(arr, 0, 1) if base in TRANSPOSED else arr

    shards = {'w_in': _cast_shards([shard('w_in')], "cast_w_in")[0]}
    rest = [n for n in BIG if n != 'w_in']
    *cast, w_in_g, conv_w_g = _cast_shards([shard(n) for n in rest], "cast_rest",
                                           [("ag", [shards['w_in']]), ("agf", [a['conv_w'][0]])])
    shards.update(zip(rest, cast))
    conv_w_full = conv_w_g.transpose(1, 0, 2).reshape(4, D_LRU)

    p = {n: a[n] for n in SMALL}
    p['rel_bias'] = a['rel_bias'][0]
    p['w_rg'] = a['w_rg'][0]
    p['w_ig'] = a['w_ig'][0]
    p['conv_w'] = conv_w_full
    p['g_final'] = a['g_final'][None, :]
    chip_arr = jnp.reshape(chip, (1,)).astype(jnp.int32)
    loss_part, grad_x, small, _, part, sib, packs = _local_step(
        a['x'][0], a['mem'][0], a['loss_target'][0], p, {'w_in': w_in_g}, shards, chip_arr)

    def late_copies(refs, send_sems, recv_sems):
        return _late_copies(refs[0], refs[1], refs[2], refs[3], send_sems, recv_sems)

    late_pack = _pack_small(LATE, [small[n] for n in LATE], None, LATE_ROWS, "pack_late")
    bufs = [part['w_in'], lax.empty(part['w_in'].shape, F32), late_pack, jnp.zeros((8, LATE_ROWS, PACK_W), F32)]
    sems, bufs, token = _split_start("late_exchange_start", bufs, 8, late_copies)
    out = {}

    def adamw(group, after=None):
        results = _final_adamw([part[n] for n in group], [sib[n] for n in group], [shard(n) for n in group],
                               [shard('m_' + n) for n in group], [shard('v_' + n) for n in group],
                               "adamw_" + group[0], after)
        for n, res in zip(group, results):
            out[n] = [jnp.swapaxes(r, 0, 1) for r in res] if n in TRANSPOSED else res
        return results[-1][0]

    adamw(MID, token)
    done = adamw(['w_gate', 'w_up', 'w_down'], token)
    _, sib['w_in'], late_pack, late_packs = _split_wait("late_exchange_wait", sems, bufs, 8, late_copies, done)
    adamw(['w_in'])

    def natural(arr):
        return arr[0] if arr.ndim >= 3 else (arr[None, :] if arr.ndim == 1 else arr)

    small_out = _adamw_small(packs, late_pack, late_packs, [small[n].shape for n in SMALL],
                             loss_part.shape, *[[natural(a[pre + n]) for n in SMALL] for pre in ('', 'm_', 'v_')])
    ns = len(SMALL)
    loss = small_out[4 * ns][0, 0]

    def leaf(i, n):
        if n in BIG:
            return out[n][i][None]
        return small_out[i * ns + SMALL.index(n)].reshape(a[n].shape)

    return (loss, grad_x[None], *[leaf(i, n) for i in range(4) for n in WEIGHTS])
```

```python
import math

import jax
import jax.numpy as jnp
from jax import lax
from jax.experimental import pallas as pl
from jax.experimental.pallas import tpu as pltpu

F32 = jnp.float32
BF16 = jnp.bfloat16

D_MODEL = 1024
D_ATT = 512
D_LRU = 512
HEAD_DIM = 64
ATT_HEADS = 8
CHUNK = 64
LEFT_CHUNKS = 8
MAX_REL = 128
X_HEADS = 4
X_HEAD_DIM = 256
N_SHARD = 4
IN_SH = 640
D_IN = N_SHARD * IN_SH
FF_SH = 704
D_FF = N_SHARD * FF_SH
EPS = 1e-6
LRU_C = 8.0
LRU_BLOCKS = 8
LRU_BLOCK = 64
QB = 256
KB = 768
ROLL_W = 1024
NEG = -1e30
ATT_SCALE = HEAD_DIM ** -0.5
X_SCALE = X_HEAD_DIM ** -0.5

ADAM_LR = 0.001
ADAM_B1 = 0.9
ADAM_B2 = 0.999
ADAM_EPS = 1e-08
ADAM_WD = 0.01
ADAM_STEP = 10

VMEM_LIMIT_V7X = 56 * 1024 * 1024
BF16_ROWS = 16


EW_VMEM_BUDGET = 40 * 1024 * 1024


def _ew_steps(rows, bytes_per_row):
    return min(s for s in (2, 4, 8, 16) if rows % (s * BF16_ROWS) == 0
               and 2 * (rows // s) * bytes_per_row <= EW_VMEM_BUDGET)
MESH_ID = pl.DeviceIdType.MESH

WEIGHTS = ['g_mix', 'w_in', 'rel_bias', 'conv_w', 'conv_b', 'w_rg', 'b_rg', 'w_ig', 'b_ig', 'lru_L',
           'g_out_attn', 'g_out_lru', 'w_out', 'g_cross', 'g_mem', 'wq_c', 'wk_c', 'wv_c', 'wo_c',
           'g_ffn', 'w_gate', 'w_up', 'w_down', 'g_final']
BIG = ['w_in', 'w_out', 'wq_c', 'wk_c', 'wv_c', 'wo_c', 'w_gate', 'w_up', 'w_down']
SMALL = [n for n in WEIGHTS if n not in BIG]


def _sds(shape, dtype):
    return jax.ShapeDtypeStruct(shape, dtype)


def _cp(*sem):
    return pltpu.CompilerParams(dimension_semantics=sem or None, vmem_limit_bytes=VMEM_LIMIT_V7X)


def _rows(tm, n):
    return pl.BlockSpec((tm, n), lambda i: (i, 0))


def _full(shape):
    nd = len(shape)
    return pl.BlockSpec(shape, lambda i: (0,) * nd)


def _dot(a, b):
    return jnp.dot(a, b, preferred_element_type=F32)


def _dot_nt(a, b):
    return lax.dot_general(a, b, (((1,), (1,)), ((), ())), preferred_element_type=F32)


def _dot_tn(a, b):
    return lax.dot_general(a, b, (((0,), (0,)), ((), ())), preferred_element_type=F32)


def _rinv(x):
    return lax.rsqrt(jnp.mean(x * x, axis=-1, keepdims=True) + EPS)


def _rms_bwd(dy, x, g):
    r = _rinv(x)
    yh = x * r
    dyh = dy * g
    dx = r * (dyh - yh * jnp.mean(dyh * yh, axis=-1, keepdims=True))
    return dx, jnp.sum(dy * yh, axis=0, keepdims=True)


def _gelu(x):
    c = math.sqrt(2.0 / math.pi)
    t = jnp.tanh(c * (x + 0.044715 * x * x * x))
    return 0.5 * x * (1.0 + t)


def _gelu_and_grad(x):
    c = math.sqrt(2.0 / math.pi)
    t = jnp.tanh(c * (x + 0.044715 * x * x * x))
    g = 0.5 * x * (1.0 + t)
    dg = 0.5 * (1.0 + t) + 0.5 * x * (1.0 - t * t) * c * (1.0 + 3.0 * 0.044715 * x * x)
    return g, dg


def _neg_expm1(z):
    series = -z * (1.0 + z * (0.5 + z * ((1.0 / 6.0) + z * (1.0 / 24.0))))
    return jnp.where(z > -0.03, series, 1.0 - jnp.exp(z))


def _lru_gates(u, wrg, brg, wig, big, lam):
    ub = u.astype(BF16)
    r = jax.nn.sigmoid(_dot(ub, wrg) + brg)
    ig = jax.nn.sigmoid(_dot(ub, wig) + big)
    sp = jnp.maximum(-lam, 0.0) + jnp.log1p(jnp.exp(-jnp.abs(lam)))
    la = -LRU_C * r * sp
    a = jnp.exp(la)
    mult = jnp.sqrt(jnp.maximum(_neg_expm1(2.0 * la), 0.0))
    return ub, r, ig, sp, a, mult


def _scan8(a8, b8, hprev):
    row = lax.broadcasted_iota(jnp.int32, a8.shape, 0)
    aa, bb = a8, b8
    for d in (1, 2, 4):
        a_s = pltpu.roll(aa, d, 0)
        b_s = pltpu.roll(bb, d, 0)
        m = row >= d
        bb = jnp.where(m, aa * b_s + bb, bb)
        aa = jnp.where(m, aa * a_s, aa)
    return aa * hprev + bb


def _rscan8(c8, d8, lnext):
    row = lax.broadcasted_iota(jnp.int32, c8.shape, 0)
    cc, dd = c8, d8
    for d in (1, 2, 4):
        c_s = pltpu.roll(cc, 8 - d, 0)
        d_s = pltpu.roll(dd, 8 - d, 0)
        m = row < 8 - d
        dd = jnp.where(m, cc * d_s + dd, dd)
        cc = jnp.where(m, cc * c_s, cc)
    return cc * lnext + dd


def _mesh_pos():
    return lax.axis_index("x"), lax.axis_index("y"), lax.axis_index("c")


def _other_chips(x, y):
    return [(1 - x, y), (x, 1 - y), (1 - x, 1 - y)]


def _no_forward():
    pass


def _ag_full_copies(ins, outs, sems):
    send_sems, recv_sems, loc_sems = sems
    n = len(ins)
    x, y, c = _mesh_pos()
    mine = 2 * x + y
    chips = _other_chips(x, y)

    def remote(k, j, slot):
        px, py = chips[j]
        return pltpu.make_async_remote_copy(
            src_ref=ins[k], dst_ref=outs[k].at[slot], send_sem=send_sems.at[k, j], recv_sem=recv_sems.at[k, j],
            device_id=(px, py, c), device_id_type=MESH_ID)

    def local(k):
        return pltpu.make_async_copy(ins[k], outs[k].at[mine], loc_sems.at[k])

    def start():
        for k in range(n):
            local(k).start()
            for j in range(3):
                remote(k, j, mine).start()

    def wait():
        for k in range(n):
            for j, (px, py) in enumerate(chips):
                remote(k, j, 2 * px + py).wait_recv()
        for k in range(n):
            for j in range(3):
                remote(k, j, mine).wait_send()
            local(k).wait()

    return start, _no_forward, wait


def _ag_copies(ins, outs, sems):
    send_sems, recv_sems, fsend_sems, frecv_sems, loc_sems = sems
    n = len(ins)
    x, y, c = _mesh_pos()
    mine = 2 * x + y
    chips = _other_chips(x, y)

    def half(ref, hc):
        r = ref.shape[0] // 2
        return ref.at[pl.ds(pl.multiple_of(hc * r, 16), r)]

    def ici(k, j, slot):
        px, py = chips[j]
        return pltpu.make_async_remote_copy(
            src_ref=half(ins[k], c), dst_ref=half(outs[k].at[slot], c),
            send_sem=send_sems.at[k, j], recv_sem=recv_sems.at[k, j],
            device_id=(px, py, c), device_id_type=MESH_ID)

    def d2d(k, j, hc):
        px, py = chips[j]
        part = half(outs[k].at[2 * px + py], hc)
        return pltpu.make_async_remote_copy(
            src_ref=part, dst_ref=part, send_sem=fsend_sems.at[k, j], recv_sem=frecv_sems.at[k, j],
            device_id=(x, y, 1 - c), device_id_type=MESH_ID)

    def local(k):
        return pltpu.make_async_copy(ins[k], outs[k].at[mine], loc_sems.at[k])

    def start():
        for k in range(n):
            local(k).start()
            for j in range(3):
                ici(k, j, mine).start()

    def forward():
        for k in range(n):
            for j, (px, py) in enumerate(chips):
                ici(k, j, 2 * px + py).wait_recv()
                d2d(k, j, c).start()

    def wait():
        for k in range(n):
            for j in range(3):
                d2d(k, j, 1 - c).wait_recv()
        for k in range(n):
            for j in range(3):
                d2d(k, j, c).wait_send()
                ici(k, j, mine).wait_send()
            local(k).wait()

    return start, forward, wait


def _rs_copies(ins, outs, sems):
    send_sems, recv_sems = sems
    n = len(ins)
    x, y, c = _mesh_pos()
    chips = _other_chips(x, y)

    def remote(k, j):
        px, py = chips[j]
        return pltpu.make_async_remote_copy(
            src_ref=ins[k].at[2 * px + py], dst_ref=outs[k].at[j],
            send_sem=send_sems.at[k, j], recv_sem=recv_sems.at[k, j],
            device_id=(px, py, c), device_id_type=MESH_ID)

    def start():
        for k in range(n):
            for j in range(3):
                remote(k, j).start()

    def wait():
        for k in range(n):
            for j in range(3):
                remote(k, j).wait_recv()
        for k in range(n):
            for j in range(3):
                remote(k, j).wait_send()

    return start, _no_forward, wait


def _swap_copies(ins, outs, sems):
    send_sems, recv_sems = sems
    x, y, c = _mesh_pos()
    copies = [pltpu.make_async_remote_copy(
        src_ref=ins[k], dst_ref=outs[k], send_sem=send_sems.at[k], recv_sem=recv_sems.at[k],
        device_id=(x, y, 1 - c), device_id_type=MESH_ID) for k in range(len(ins))]

    def start():
        for cp in copies:
            cp.start()

    def wait():
        for cp in copies:
            cp.wait()

    return start, _no_forward, wait


def _comm_plan(groups):
    plan, arrs, shapes, sems = [], [], [], []
    for kind, group in groups:
        k = len(group)
        arrs += group
        per_peer = pltpu.SemaphoreType.DMA((k, 3))
        if kind == "ag":
            shapes += [_sds((N_SHARD,) + w.shape, w.dtype) for w in group]
            gsems = [per_peer] * 4 + [pltpu.SemaphoreType.DMA((k,))]
            maker = _ag_copies
        elif kind == "agf":
            shapes += [_sds((N_SHARD,) + w.shape, w.dtype) for w in group]
            gsems = [per_peer] * 2 + [pltpu.SemaphoreType.DMA((k,))]
            maker = _ag_full_copies
        elif kind == "ag8":
            shapes += [_sds((8,) + g.shape, g.dtype) for g in group]
            gsems = [pltpu.SemaphoreType.DMA((k, 7))] * 2 + [pltpu.SemaphoreType.DMA((k,))]
            maker = _ag8_copies
        elif kind == "rs":
            shapes += [_sds((3,) + g.shape[1:], g.dtype) for g in group]
            gsems = [pltpu.SemaphoreType.DMA((k, 3)), pltpu.SemaphoreType.DMA((k, 3))]
            maker = _rs_copies
        else:
            shapes += [_sds(g.shape, g.dtype) for g in group]
            gsems = [pltpu.SemaphoreType.DMA((k,)), pltpu.SemaphoreType.DMA((k,))]
            maker = _swap_copies
        plan.append((maker, k, len(gsems)))
        sems += gsems
    return plan, arrs, shapes, sems


def _comm_fns(plan, cins, couts, sems):
    fns, a, s = [], 0, 0
    for maker, k, ns in plan:
        fns.append(maker(cins[a:a + k], couts[a:a + k], sems[s:s + ns]))
        a += k
        s += ns

    def start():
        for st, _, _ in fns:
            st()

    def forward():
        for _, fw, _ in fns:
            fw()

    def wait():
        for _, _, wt in fns:
            wt()

    return start, forward, wait


def _call(body, name, grid, in_specs, out_specs, out_shape, scratch, args, sem, comm=None):
    if not comm:
        return pl.pallas_call(body, name=name, grid=grid, in_specs=in_specs, out_specs=out_specs,
                              out_shape=out_shape, scratch_shapes=scratch, compiler_params=_cp(sem))(*args)
    plan, c_arrs, c_shapes, c_sems = _comm_plan(comm)
    k = len(c_arrs)
    n_in, n_out, n_scr = len(in_specs), len(out_specs), len(scratch)
    last = grid[0] - 1
    fwd_step = max(1, (2 * last) // 3)

    def wrapped(*refs):
        ins, cins = refs[:n_in], refs[n_in:n_in + k]
        o0 = n_in + k
        outs, couts = refs[o0:o0 + n_out], refs[o0 + n_out:o0 + n_out + k]
        s0 = o0 + n_out + k
        start, forward, wait = _comm_fns(plan, cins, couts, refs[s0 + n_scr:])
        pl.when(pl.program_id(0) == 0)(start)
        pl.when(pl.program_id(0) == fwd_step)(forward)
        body(*ins, *outs, *refs[s0:s0 + n_scr])
        pl.when(pl.program_id(0) == last)(wait)

    return pl.pallas_call(
        wrapped, name=name, grid=grid, in_specs=list(in_specs) + [_any()] * k,
        out_specs=list(out_specs) + [_any()] * k, out_shape=list(out_shape) + c_shapes,
        scratch_shapes=list(scratch) + c_sems, compiler_params=_cp(sem))(*args, *c_arrs)


def _tail_copies(slots_ref, land_ref, part_refs, sib_refs, send_sems, recv_sems):
    x, y, c = _mesh_pos()
    copies = []
    for j, (px, py) in enumerate(_other_chips(x, y)):
        copies.append(pltpu.make_async_remote_copy(
            src_ref=slots_ref.at[2 * px + py], dst_ref=land_ref.at[j], send_sem=send_sems[j], recv_sem=recv_sems[j],
            device_id=(px, py, c), device_id_type=MESH_ID))
    for k, (p_ref, s_ref) in enumerate(zip(part_refs, sib_refs)):
        copies.append(pltpu.make_async_remote_copy(
            src_ref=p_ref, dst_ref=s_ref, send_sem=send_sems[3 + k], recv_sem=recv_sems[3 + k],
            device_id=(x, y, 1 - c), device_id_type=MESH_ID))
    return copies


def _late_copies(part_ref, sib_ref, pack_ref, packs_ref, send_sems, recv_sems):
    x, y, c = _mesh_pos()
    peers, me = _all_peers()
    copies = [pltpu.make_async_remote_copy(
        src_ref=part_ref, dst_ref=sib_ref, send_sem=send_sems[0], recv_sem=recv_sems[0],
        device_id=(x, y, 1 - c), device_id_type=MESH_ID)]
    for j in range(7):
        copies.append(pltpu.make_async_remote_copy(
            src_ref=pack_ref, dst_ref=packs_ref.at[me], send_sem=send_sems[1 + j], recv_sem=recv_sems[1 + j],
            device_id=peers[j][0], device_id_type=MESH_ID))
    return copies


def _split_start(name, bufs, ncp, make_copies):
    hbm = pl.BlockSpec(memory_space=pltpu.HBM)
    sem = pl.BlockSpec(memory_space=pltpu.SEMAPHORE)
    bufs = [pltpu.with_memory_space_constraint(b, pltpu.HBM) for b in bufs]
    nb = len(bufs)

    def body(*refs):
        for cp in make_copies(refs[:nb], refs[nb:nb + ncp], refs[nb + ncp:nb + 2 * ncp]):
            cp.start()
        refs[-1][...] = jnp.zeros_like(refs[-1])

    out = pl.pallas_call(
        body, name=name,
        out_shape=[pltpu.SemaphoreType.DMA(())] * (2 * ncp) + [pltpu.HBM(b.shape, b.dtype) for b in bufs]
                  + [_sds((8, 128), F32)],
        in_specs=[hbm] * nb, out_specs=[sem] * (2 * ncp) + [hbm] * nb + [pl.BlockSpec(memory_space=pltpu.VMEM)],
        input_output_aliases={i: 2 * ncp + i for i in range(nb)},
        compiler_params=pltpu.CompilerParams(has_side_effects=pltpu.SideEffectType.DATAFLOW_SIDE_EFFECTING),
    )(*bufs)
    return out[:2 * ncp], out[2 * ncp:2 * ncp + nb], out[-1]


def _split_wait(name, sems, bufs, ncp, make_copies, after):
    nb = len(bufs)
    hbm = pl.BlockSpec(memory_space=pltpu.HBM)
    sem = pl.BlockSpec(memory_space=pltpu.SEMAPHORE)

    def body(*refs):
        for cp in make_copies(refs[:nb], refs[nb:nb + ncp], refs[nb + ncp:nb + 2 * ncp]):
            cp.wait_send()
            cp.wait_recv()

    return pl.pallas_call(
        body, name=name, out_shape=[pltpu.HBM(b.shape, b.dtype) for b in bufs],
        in_specs=[hbm] * nb + [sem] * (2 * ncp) + [_any()], out_specs=[hbm] * nb,
        input_output_aliases={i: i for i in range(nb)},
        compiler_params=pltpu.CompilerParams(has_side_effects=pltpu.SideEffectType.DATAFLOW_SIDE_EFFECTING),
    )(*bufs, *sems, after)


def _any():
    return pl.BlockSpec(memory_space=pl.ANY)


def _start_copies(pairs, sems, first=0):
    copies = [pltpu.make_async_copy(src, dst, sems.at[first + i]) for i, (src, dst) in enumerate(pairs)]
    for cp in copies:
        cp.start()
    return copies


def _copy_together(pairs, sems):
    for cp in _start_copies(pairs, sems):
        cp.wait()


def _load_w_in_once(w_hbm, w_ref, sems):
    @pl.when(pl.program_id(0) == 0)
    def _():
        _copy_together([(w_hbm.at[s], w_ref.at[:, pl.ds(s * IN_SH, IN_SH)]) for s in range(N_SHARD)], sems)


def _f_inproj(x, g_mix, w_in_g, tm, comm=None):
    s_len = x.shape[0]
    pad_rows = LEFT_CHUNKS * CHUNK
    npad = pad_rows // tm

    def body(x_ref, g_ref, w_hbm, h_ref, qkv_ref, xg_ref, w_ref, w_sems):
        i = pl.program_id(0)
        _load_w_in_once(w_hbm, w_ref, w_sems)

        @pl.when(i < npad)
        def _():
            qkv_ref[...] = jnp.zeros_like(qkv_ref)

        @pl.when(i >= npad)
        def _():
            xv = x_ref[...]
            h = (xv * _rinv(xv) * g_ref[...]).astype(BF16)
            h_ref[...] = h
            proj = _dot(h, w_ref[...])
            qkv_ref[:, 0:D_ATT] = (proj[:, 0:D_ATT] * ATT_SCALE).astype(BF16)
            qkv_ref[:, D_ATT:3 * D_ATT] = proj[:, D_ATT:3 * D_ATT].astype(BF16)
            xg_ref[...] = proj[:, 3 * D_ATT:D_IN]

    def tok(n):
        return pl.BlockSpec((tm, n), lambda i: (jnp.maximum(i - npad, 0), 0))

    return _call(
        body, "f_inproj", (s_len // tm + npad,),
        [tok(1024), _full((1, 1024)), _any()],
        [tok(1024), _rows(tm, 1536), tok(1024)],
        [_sds((s_len, 1024), BF16), _sds((s_len + pad_rows, 1536), BF16), _sds((s_len, 1024), F32)],
        [pltpu.VMEM((1024, D_IN), BF16), pltpu.SemaphoreType.DMA((N_SHARD,))], (x, g_mix, w_in_g), "arbitrary", comm)


N_BIAS = 3


def _bias_table(frow_ref, bias_sc):
    qa = lax.broadcasted_iota(jnp.int32, (QB, KB), 0) // CHUNK
    kcol = lax.broadcasted_iota(jnp.int32, (QB, KB), 1)
    kb = kcol // CHUNK
    band = jnp.where((kb >= qa) & (kb - qa <= LEFT_CHUNKS), 0.0, NEG).astype(F32)
    for h in range(ATT_HEADS):
        row = jnp.broadcast_to(frow_ref[h:h + 1, :], (QB, ROLL_W))
        toep = pltpu.roll(row, 0, 1, stride=1, stride_axis=0)
        gen = toep[:, 0:KB] + band
        bias_sc[N_BIAS - 1, h] = gen
        for v in range(N_BIAS - 1):
            pad_keys = LEFT_CHUNKS * CHUNK - v * QB
            bias_sc[v, h] = gen + jnp.where(kcol < pad_keys, NEG, 0.0).astype(F32)


def _even_lanes():
    return lax.broadcasted_iota(jnp.int32, (1, 2 * HEAD_DIM), 1) < HEAD_DIM


def _att_probs(qm, kts, bias):
    s = jnp.concatenate([_dot_nt(qm, k) for k in kts], axis=1) + bias
    return jnp.exp(s - jnp.max(s, axis=-1, keepdims=True))


def _att_in_specs(clamp):
    def spec(j, col):
        return pl.BlockSpec((QB, D_ATT), lambda i: (clamp(i) + j, col))
    return [spec(2, 0), spec(0, 1), spec(1, 1), spec(2, 1), spec(0, 2), spec(1, 2), spec(2, 2)]


def _f_attn(qkv_pad, frow, comm=None):
    s_len = qkv_pad.shape[0] - LEFT_CHUNKS * CHUNK
    nb = s_len // QB

    def body(q_ref, k0, k1, k2, v0, v1, v2, frow_ref, o_ref, bias_sc):
        i = pl.program_id(0)

        @pl.when(i == 0)
        def _():
            _bias_table(frow_ref, bias_sc)

        var = jnp.minimum(i, N_BIAS - 1)
        even = _even_lanes()
        for hp in range(ATT_HEADS // 2):
            cs = slice(hp * 2 * HEAD_DIM, (hp + 1) * 2 * HEAD_DIM)
            qt = q_ref[:, cs]
            kts = [k0[:, cs], k1[:, cs], k2[:, cs]]
            vts = [v0[:, cs], v1[:, cs], v2[:, cs]]
            res = []
            for e in range(2):
                keep = even if e == 0 else jnp.logical_not(even)
                pb = _att_probs(jnp.where(keep, qt, 0), kts, bias_sc[var, 2 * hp + e]).astype(BF16)
                r = _dot(pb, jnp.concatenate([jnp.where(keep, v, 1) for v in vts], axis=0))
                res.append(r / pltpu.roll(r, HEAD_DIM, 1))
            o_ref[:, cs] = jnp.where(even, res[0], res[1])

    return _call(
        body, "f_attn", (nb,),
        _att_in_specs(lambda i: i) + [_full((ATT_HEADS, ROLL_W))],
        [_rows(QB, D_ATT)], [_sds((s_len, D_ATT), F32)],
        [pltpu.VMEM((N_BIAS, ATT_HEADS, QB, KB), F32)], (*([qkv_pad] * 7), frow), "arbitrary", comm)


def _f_lru(xg, conv_w, conv_b, wrg, brg, wig, big, lam, tl, comm=None):
    s_len = xg.shape[0]

    def body(xg_ref, cw_ref, cb_ref, wrg_ref, brg_ref, wig_ref, big_ref, l_ref,
             rec_ref, u_ref, hs_ref, xbuf, a_sc, b_sc, hcar):
        i = pl.program_id(0)

        @pl.when(i == 0)
        def _():
            xbuf[0:8, :] = jnp.zeros((8, D_LRU), F32)
            hcar[...] = jnp.zeros((8, D_LRU), F32)

        xu0 = xg_ref[:, 0:D_LRU]
        xbuf[8:8 + tl, :] = xu0
        u = cb_ref[...] + cw_ref[0:1, :] * xbuf[pl.ds(5, tl), :]
        for j in range(1, 4):
            u = u + cw_ref[j:j + 1, :] * xbuf[pl.ds(5 + j, tl), :]
        xbuf[0:8, :] = xu0[tl - 8:tl, :]
        u_ref[...] = u
        _, _, ig, _, a, mult = _lru_gates(u, wrg_ref[...], brg_ref[...], wig_ref[...], big_ref[...], l_ref[...])
        a_sc[...] = a
        b_sc[...] = mult * (ig * u)

        def grp(g, hprev):
            off = pl.multiple_of(g * 8, 8)
            h8 = _scan8(a_sc[pl.ds(off, 8), :], b_sc[pl.ds(off, 8), :], hprev)
            hs_ref[pl.ds(off, 8), :] = h8
            return h8[7:8, :]

        hcar[0:1, :] = lax.fori_loop(0, tl // 8, grp, hcar[0:1, :])
        rec_ref[...] = hs_ref[...] * _gelu(xg_ref[:, D_LRU:2 * D_LRU])

    vec = _full((1, D_LRU))
    return _call(
        body, "f_lru", (s_len // tl,),
        [_rows(tl, 1024), _full((4, D_LRU)), vec, _full((D_LRU, D_LRU)), vec, _full((D_LRU, D_LRU)), vec, vec],
        [_rows(tl, D_LRU)] * 3, [_sds((s_len, D_LRU), F32)] * 3,
        [pltpu.VMEM((tl + 8, D_LRU), F32), pltpu.VMEM((tl, D_LRU), F32),
         pltpu.VMEM((tl, D_LRU), F32), pltpu.VMEM((8, D_LRU), F32)],
        (xg, conv_w, conv_b, wrg, brg, wig, big, lam), "arbitrary", comm)


def _f_mem(mem, g_mem, wk, wv):
    def body(mem_ref, g_ref, wk_ref, wv_ref, mn_ref, kx_ref, vx_ref):
        mv = mem_ref[...]
        mn = (mv * _rinv(mv) * g_ref[...]).astype(BF16)
        mn_ref[...] = mn
        kx_ref[...] = _dot(mn, wk_ref[...]).astype(BF16)
        vx_ref[...] = _dot(mn, wv_ref[...]).astype(BF16)

    m = mem.shape[0]
    return pl.pallas_call(
        body, name="f_mem", out_shape=[_sds((m, 1024), BF16)] * 3,
        compiler_params=_cp())(mem, g_mem, wk, wv)


def _xattn_probs(q, k):
    s = _dot_nt(q, k) * X_SCALE
    m = jnp.max(s, axis=-1, keepdims=True)
    p = jnp.exp(s - m)
    return p, jnp.sum(p, axis=-1, keepdims=True)


def _f_mid(x, att, rec, g_oa, g_ol, w_out, g_cross, wq, kx, vx, wo, tm, comm=None):
    s_len = x.shape[0]
    m_len = kx.shape[0]

    def body(x_ref, att_ref, rec_ref, goa_ref, gol_ref, wout_ref, gc_ref, wq_ref, kx_ref, vx_ref, wo_ref,
             mg_ref, x1_ref, hc_ref, qx_ref, ox_ref, x2_ref):
        av = att_ref[...]
        rv = rec_ref[...]
        mg_ref[:, 0:D_ATT] = (av * _rinv(av) * goa_ref[...]).astype(BF16)
        mg_ref[:, D_ATT:1024] = (rv * _rinv(rv) * gol_ref[...]).astype(BF16)
        x1 = x_ref[...] + _dot(mg_ref[...], wout_ref[...])
        x1_ref[...] = x1
        hc = (x1 * _rinv(x1) * gc_ref[...]).astype(BF16)
        hc_ref[...] = hc
        qx_ref[...] = _dot(hc, wq_ref[...]).astype(BF16)
        for h in range(X_HEADS):
            sl = slice(h * X_HEAD_DIM, (h + 1) * X_HEAD_DIM)
            p, l = _xattn_probs(qx_ref[:, sl], kx_ref[:, sl])
            ox_ref[:, sl] = (_dot(p.astype(BF16), vx_ref[:, sl]) / l).astype(BF16)
        x2_ref[...] = x1 + _dot(ox_ref[...], wo_ref[...])

    sq = _full((1024, 1024))
    return _call(
        body, "f_mid", (s_len // tm,),
        [_rows(tm, 1024), _rows(tm, 512), _rows(tm, 512), _full((1, 512)), _full((1, 512)), sq,
         _full((1, 1024)), sq, _full((m_len, 1024)), _full((m_len, 1024)), sq],
        [_rows(tm, 1024)] * 6,
        [_sds((s_len, 1024), BF16), _sds((s_len, 1024), F32), _sds((s_len, 1024), BF16),
         _sds((s_len, 1024), BF16), _sds((s_len, 1024), BF16), _sds((s_len, 1024), F32)],
        [], (x, att, rec, g_oa, g_ol, w_out, g_cross, wq, kx, vx, wo), "arbitrary", comm)


FF_CHUNKS = [(0, 1280), (1280, D_FF)]


def _first_step_and_rest(step):
    pl.when(pl.program_id(0) == 0)(lambda: step(True))
    pl.when(pl.program_id(0) > 0)(lambda: step(False))


def _ffn_weights(first, pairs, sems):
    if not first:
        return lambda c, j: None
    copies = _start_copies([(hbm.at[c0:c1, :], vmem.at[c0:c1, :]) for c0, c1 in FF_CHUNKS for hbm, vmem in pairs],
                           sems)
    return lambda c, j: copies[c * len(pairs) + j].wait()


def _f_ffn(x2, tgt, g_ffn, g_final, wg, wu, wd, tm):
    s_len = x2.shape[0]

    def body(x2_ref, t_ref, gf_ref, gfin_ref, wg_hbm, wu_hbm, wd_hbm,
             hf_ref, g_ref, u_ref, a_ref, dx3_ref, loss_ref, dgfin_ref, wg_ref, wu_ref, wd_ref, w_sems):
        @pl.when(pl.program_id(0) == 0)
        def _():
            _copy_together([(wg_hbm, wg_ref), (wu_hbm, wu_ref), (wd_hbm, wd_ref)], w_sems)
            loss_ref[...] = jnp.zeros_like(loss_ref)
            dgfin_ref[...] = jnp.zeros_like(dgfin_ref)

        x2v = x2_ref[...]
        hf = (x2v * _rinv(x2v) * gf_ref[...]).astype(BF16)
        hf_ref[...] = hf
        x3 = x2v
        for c0, c1 in FF_CHUNKS:
            gv = _dot_nt(hf, wg_ref[c0:c1, :])
            uv = _dot_nt(hf, wu_ref[c0:c1, :])
            av = (gv * jax.nn.sigmoid(gv) * uv).astype(BF16)
            g_ref[:, c0:c1] = gv.astype(BF16)
            u_ref[:, c0:c1] = uv.astype(BF16)
            a_ref[:, c0:c1] = av
            x3 = x3 + _dot(av, wd_ref[c0:c1, :])
        r3 = _rinv(x3)
        yh = x3 * r3
        gfin = gfin_ref[...]
        err = yh * gfin - t_ref[...]
        loss_ref[...] += jnp.full((1, 128), 0.5 / D_MODEL, F32) * jnp.sum(err * err)
        dy = err * (1.0 / D_MODEL)
        dgfin_ref[...] += jnp.sum(dy * yh, axis=0, keepdims=True)
        dyh = dy * gfin
        dx3_ref[...] = r3 * (dyh - yh * jnp.mean(dyh * yh, axis=-1, keepdims=True))

    vec = _full((1, 1024))
    return pl.pallas_call(
        body, name="f_ffn", grid=(s_len // tm,),
        in_specs=[_rows(tm, 1024), _rows(tm, 1024), vec, vec, _any(), _any(), _any()],
        out_specs=[_rows(tm, 1024), _rows(tm, D_FF), _rows(tm, D_FF), _rows(tm, D_FF),
                   _rows(tm, 1024), _full((1, 128)), vec],
        out_shape=[_sds((s_len, 1024), BF16)] + [_sds((s_len, D_FF), BF16)] * 3
                  + [_sds((s_len, 1024), F32), _sds((1, 128), F32), _sds((1, 1024), F32)],
        scratch_shapes=[pltpu.VMEM((D_FF, 1024), BF16)] * 3 + [pltpu.SemaphoreType.DMA((3,))],
        compiler_params=_cp("arbitrary"))(x2, tgt, g_ffn, g_final, wg, wu, wd)


def _b_ffn(dx3, x2, gact, uact, g_ffn, wg, wu, wd, tm):
    s_len = x2.shape[0]

    def body(dx3_ref, x2_ref, g_ref, u_ref, gf_ref, wg_hbm, wu_hbm, wd_hbm,
             dg_ref, du_ref, dx2_ref, dgf_ref, wg_ref, wu_ref, wd_ref, w_sems):
        def step(first):
            if first:
                dgf_ref[...] = jnp.zeros_like(dgf_ref)
            ready = _ffn_weights(first, [(wd_hbm, wd_ref), (wg_hbm, wg_ref), (wu_hbm, wu_ref)], w_sems)
            dx3v = dx3_ref[...]
            dx3b = dx3v.astype(BF16)
            dhf = jnp.zeros(dx3v.shape, F32)
            for c, (c0, c1) in enumerate(FF_CHUNKS):
                ready(c, 0)
                da = _dot_nt(dx3b, wd_ref[c0:c1, :])
                gv = g_ref[:, c0:c1].astype(F32)
                uv = u_ref[:, c0:c1].astype(F32)
                sg = jax.nn.sigmoid(gv)
                dub = (da * gv * sg).astype(BF16)
                dgb = (da * uv * (sg * (1.0 + gv * (1.0 - sg)))).astype(BF16)
                du_ref[:, c0:c1] = dub
                dg_ref[:, c0:c1] = dgb
                ready(c, 1)
                ready(c, 2)
                dhf = dhf + _dot(dgb, wg_ref[c0:c1, :]) + _dot(dub, wu_ref[c0:c1, :])
            dx, dgf = _rms_bwd(dhf, x2_ref[...], gf_ref[...])
            dx2_ref[...] = dx3v + dx
            dgf_ref[...] += dgf

        _first_step_and_rest(step)

    vec = _full((1, 1024))
    return pl.pallas_call(
        body, name="b_ffn", grid=(s_len // tm,),
        in_specs=[_rows(tm, 1024), _rows(tm, 1024), _rows(tm, D_FF), _rows(tm, D_FF), vec,
                  _any(), _any(), _any()],
        out_specs=[_rows(tm, D_FF), _rows(tm, D_FF), _rows(tm, 1024), vec],
        out_shape=[_sds((s_len, D_FF), BF16)] * 2 + [_sds((s_len, 1024), F32), _sds((1, 1024), F32)],
        scratch_shapes=[pltpu.VMEM((D_FF, 1024), BF16)] * 3 + [pltpu.SemaphoreType.DMA((3 * len(FF_CHUNKS),))],
        compiler_params=_cp("arbitrary"))(dx3, x2, gact, uact, g_ffn, wg, wu, wd)


def _b_mid(dx2, qx, x1, att, rec, kx, vx, wo, wq, w_out, g_cross, g_oa, g_ol, tm, comm=None):
    s_len = x1.shape[0]
    m_len = kx.shape[0]

    def body(dx2_ref, qx_ref, x1_ref, att_ref, rec_ref, kx_ref, vx_ref, wo_ref, wq_ref, wout_ref,
             gc_ref, goa_ref, gol_ref,
             dqx_ref, dx1_ref, datt_ref, drec_ref, dkx_ref, dvx_ref, dgc_ref, dgoa_ref, dgol_ref):
        @pl.when(pl.program_id(0) == 0)
        def _():
            for r in (dkx_ref, dvx_ref, dgc_ref, dgoa_ref, dgol_ref):
                r[...] = jnp.zeros_like(r)

        dx2v = dx2_ref[...]
        dox = _dot_nt(dx2v.astype(BF16), wo_ref[...])
        for h in range(X_HEADS):
            sl = slice(h * X_HEAD_DIM, (h + 1) * X_HEAD_DIM)
            q = qx_ref[:, sl]
            p, l = _xattn_probs(q, kx_ref[:, sl])
            pn = p * (1.0 / l)
            dob = dox[:, sl].astype(BF16)
            dp = _dot_nt(dob, vx_ref[:, sl])
            dvx_ref[:, sl] += _dot_tn(pn.astype(BF16), dob)
            ds = pn * (dp - jnp.sum(dp * pn, axis=-1, keepdims=True))
            dsb = (ds * X_SCALE).astype(BF16)
            dqx_ref[:, sl] = _dot(dsb, kx_ref[:, sl]).astype(BF16)
            dkx_ref[:, sl] += _dot_tn(dsb, q)
        dhc = _dot_nt(dqx_ref[...], wq_ref[...])
        dx, dgc = _rms_bwd(dhc, x1_ref[...], gc_ref[...])
        dx1 = dx2v + dx
        dx1_ref[...] = dx1
        dgc_ref[...] += dgc
        dmg = _dot_nt(dx1.astype(BF16), wout_ref[...])
        da, dgoa = _rms_bwd(dmg[:, 0:D_ATT], att_ref[...], goa_ref[...])
        datt_ref[...] = da
        dgoa_ref[...] += dgoa
        dr, dgol = _rms_bwd(dmg[:, D_ATT:1024], rec_ref[...], gol_ref[...])
        drec_ref[...] = dr
        dgol_ref[...] += dgol

    sq = _full((1024, 1024))
    mk = _full((m_len, 1024))
    return _call(
        body, "b_mid", (s_len // tm,),
        [_rows(tm, 1024), _rows(tm, 1024), _rows(tm, 1024), _rows(tm, 512), _rows(tm, 512), mk, mk,
         sq, sq, sq, _full((1, 1024)), _full((1, 512)), _full((1, 512))],
        [_rows(tm, 1024), _rows(tm, 1024), _rows(tm, 512), _rows(tm, 512), mk, mk,
         _full((1, 1024)), _full((1, 512)), _full((1, 512))],
        [_sds((s_len, 1024), BF16), _sds((s_len, 1024), F32), _sds((s_len, 512), F32),
         _sds((s_len, 512), F32), _sds((m_len, 1024), F32), _sds((m_len, 1024), F32),
         _sds((1, 1024), F32), _sds((1, 512), F32), _sds((1, 512), F32)],
        [], (dx2, qx, x1, att, rec, kx, vx, wo, wq, w_out, g_cross, g_oa, g_ol), "arbitrary", comm)


def _b_mem(dkx, dvx, mem, mn, g_mem, wk, wv):
    def body(dkx_ref, dvx_ref, mem_ref, mn_ref, g_ref, wk_ref, wv_ref, dwk_ref, dwv_ref, dgm_ref,
             dwkb_ref, dwvb_ref):
        dkb = dkx_ref[...].astype(BF16)
        dvb = dvx_ref[...].astype(BF16)
        dwk = _dot_tn(mn_ref[...], dkb)
        dwv = _dot_tn(mn_ref[...], dvb)
        dwk_ref[...] = dwk
        dwv_ref[...] = dwv
        dwkb_ref[...] = dwk.astype(BF16)
        dwvb_ref[...] = dwv.astype(BF16)
        dmn = _dot_nt(dkb, wk_ref[...]) + _dot_nt(dvb, wv_ref[...])
        mv = mem_ref[...]
        dgm_ref[...] = jnp.sum(dmn * (mv * _rinv(mv)), axis=0, keepdims=True)

    return pl.pallas_call(
        body, name="b_mem",
        out_shape=[_sds((1024, 1024), F32), _sds((1024, 1024), F32), _sds((1, 1024), F32),
                   _sds((1024, 1024), BF16), _sds((1024, 1024), BF16)],
        compiler_params=_cp())(dkx, dvx, mem, mn, g_mem, wk, wv)


def _b_lru(drec, hs, u, xg, conv_w, wrg, brg, wig, big, lam, tl, comm=None):
    s_len = xg.shape[0]
    nt = s_len // tl

    def body(drec_ref, hs_ref, hsp_ref, u_ref, xg_ref, cw_ref, wrg_ref, brg_ref, wig_ref, big_ref, l_ref,
             dxg_ref, dwrg_ref, dwig_ref, dbrg_ref, dbig_ref, dlam_ref, dcw_ref, dcb_ref,
             hbuf, abuf, dubuf, c_sc, d_sc, lam_sc, lcar, wacc_r, wacc_i):
        i = pl.program_id(0)
        tt = nt - 1 - i

        @pl.when(i == 0)
        def _():
            for r in (wacc_r, wacc_i, dbrg_ref, dbig_ref, dlam_ref, dcw_ref, dcb_ref):
                r[...] = jnp.zeros_like(r)
            abuf[tl:tl + 8, :] = jnp.zeros((8, D_LRU), F32)
            dubuf[tl:tl + 8, :] = jnp.zeros((8, D_LRU), F32)
            lcar[...] = jnp.zeros((8, D_LRU), F32)

        xu0 = xg_ref[:, 0:D_LRU]
        hsv = hs_ref[...]
        uv = u_ref[...]
        hbuf[8:8 + tl, :] = hsv
        hbuf[0:8, :] = jnp.where(tt > 0, hsp_ref[...], 0.0)
        hshift = hbuf[pl.ds(7, tl), :]
        wrg_v = wrg_ref[...]
        wig_v = wig_ref[...]
        lamv = l_ref[...]
        ub, r, ig, sp, a, mult = _lru_gates(uv, wrg_v, brg_ref[...], wig_v, big_ref[...], lamv)
        abuf[0:tl, :] = a
        c_sc[...] = abuf[pl.ds(1, tl), :]
        gel, dgel = _gelu_and_grad(xg_ref[:, D_LRU:2 * D_LRU])
        drv = drec_ref[...]
        d_sc[...] = drv * gel
        dxg_ref[:, D_LRU:2 * D_LRU] = (drv * hsv * dgel).astype(BF16)

        def grp(k, lnext):
            off = pl.multiple_of((tl // 8 - 1 - k) * 8, 8)
            l8 = _rscan8(c_sc[pl.ds(off, 8), :], d_sc[pl.ds(off, 8), :], lnext)
            lam_sc[pl.ds(off, 8), :] = l8
            return l8[0:1, :]

        lcar[0:1, :] = lax.fori_loop(0, tl // 8, grp, lcar[0:1, :])
        abuf[tl:tl + 8, :] = a[0:8, :]
        db = lam_sc[...]
        da = db * hshift
        dmult = db * (ig * uv)
        dig = db * mult * uv
        du = db * mult * ig
        dla = da * a - dmult * (a * a) / mult
        dlam_ref[...] += jnp.sum(dla * (-LRU_C) * r, axis=0, keepdims=True)
        dzr = dla * (-LRU_C * sp) * r * (1.0 - r)
        dzi = dig * ig * (1.0 - ig)
        dzrb = dzr.astype(BF16)
        dzib = dzi.astype(BF16)
        du = du + _dot_nt(dzrb, wrg_v) + _dot_nt(dzib, wig_v)
        wacc_r[...] += _dot_tn(ub, dzrb)
        wacc_i[...] += _dot_tn(ub, dzib)
        dbrg_ref[...] += jnp.sum(dzr, axis=0, keepdims=True)
        dbig_ref[...] += jnp.sum(dzi, axis=0, keepdims=True)
        dcb_ref[...] += jnp.sum(du, axis=0, keepdims=True)
        dubuf[0:tl, :] = du
        dxu0 = jnp.zeros((tl, D_LRU), F32)
        for j in range(4):
            dsh = dubuf[pl.ds(3 - j, tl), :]
            dxu0 = dxu0 + cw_ref[j:j + 1, :] * dsh
            dcw_ref[j:j + 1, :] += jnp.sum(xu0 * dsh, axis=0, keepdims=True)
        dubuf[tl:tl + 8, :] = du[0:8, :]
        dxg_ref[:, 0:D_LRU] = dxu0.astype(BF16)

        @pl.when(i == nt - 1)
        def _():
            dlam_ref[...] = dlam_ref[...] * (-jax.nn.sigmoid(-lamv))
            for n in range(LRU_BLOCKS):
                blk = slice(n * LRU_BLOCK, (n + 1) * LRU_BLOCK)
                dwrg_ref[n] = wacc_r[blk, blk]
                dwig_ref[n] = wacc_i[blk, blk]

    def rev(n):
        return pl.BlockSpec((tl, n), lambda i: (nt - 1 - i, 0))

    prev8 = pl.BlockSpec((8, D_LRU), lambda i: (jnp.maximum((nt - 1 - i) * (tl // 8) - 1, 0), 0))
    vec = _full((1, D_LRU))
    sq = _full((D_LRU, D_LRU))
    blocks_shape = (LRU_BLOCKS, LRU_BLOCK, LRU_BLOCK)
    blocks = _full(blocks_shape)
    return _call(
        body, "b_lru", (nt,),
        [rev(D_LRU), rev(D_LRU), prev8, rev(D_LRU), rev(1024), _full((4, D_LRU)), sq, vec, sq, vec, vec],
        [rev(1024), blocks, blocks, vec, vec, vec, _full((4, D_LRU)), vec],
        [_sds((s_len, 1024), BF16), _sds(blocks_shape, F32), _sds(blocks_shape, F32),
         _sds((1, D_LRU), F32), _sds((1, D_LRU), F32), _sds((1, D_LRU), F32),
         _sds((4, D_LRU), F32), _sds((1, D_LRU), F32)],
        [pltpu.VMEM((tl + 8, D_LRU), F32)] * 3 + [pltpu.VMEM((tl, D_LRU), F32)] * 3
        + [pltpu.VMEM((8, D_LRU), F32)] + [pltpu.VMEM((D_LRU, D_LRU), F32)] * 2,
        (drec, hs, hs, u, xg, conv_w, wrg, brg, wig, big, lam), "arbitrary", comm)


def _b_attn(qkv_pad, att, datt, frow, comm=None):
    s_len = datt.shape[0]
    nb = s_len // QB
    n_pair = ATT_HEADS // 2
    pair_w = 2 * HEAD_DIM

    def body(q_ref, k0, k1, k2, v0, v1, v2, o_ref, do_ref, frow_ref, dq_ref, dkv_ref, dfrow_ref,
             bias_sc, dt_sc, acc_sc):
        t = pl.program_id(0)

        @pl.when(t == 0)
        def _():
            _bias_table(frow_ref, bias_sc)
            dt_sc[...] = jnp.zeros_like(dt_sc)
            acc_sc[...] = jnp.zeros_like(acc_sc)

        @pl.when(t < nb)
        def _():
            var = jnp.minimum(t, N_BIAS - 1)
            even = _even_lanes()
            for hp in range(n_pair):
                cs = slice(hp * pair_w, (hp + 1) * pair_w)
                qt = q_ref[:, cs]
                kts = [k0[:, cs], k1[:, cs], k2[:, cs]]
                vts = [v0[:, cs], v1[:, cs], v2[:, cs]]
                kcat = jnp.concatenate(kts, axis=0)
                dot = do_ref[:, cs]
                dd = dot * o_ref[:, cs]
                dos_pair, dsbs, pbs, dqs = None, [], [], []
                for e in range(2):
                    keep = even if e == 0 else jnp.logical_not(even)
                    qm = jnp.where(keep, qt, 0)
                    p = _att_probs(qm, kts, bias_sc[var, 2 * hp + e])
                    inv = 1.0 / jnp.sum(p, axis=-1, keepdims=True)
                    dos = jnp.where(keep, dot * inv, 0.0)
                    delta = jnp.sum(jnp.where(keep, dd, 0.0), axis=-1, keepdims=True) * inv
                    dp = jnp.concatenate([_dot_nt(dos.astype(BF16), v) for v in vts], axis=1)
                    ds = p * (dp - delta)
                    dt_sc[2 * hp + e] += ds
                    dsb = ds.astype(BF16)
                    dq = _dot(dsb, kcat)
                    dqs.append(dq)
                    dsbs.append(dsb)
                    pbs.append(p.astype(BF16))
                    dos_pair = dos if e == 0 else dos_pair + dos
                dq_ref[:, cs] = (jnp.where(even, dqs[0], dqs[1]) * ATT_SCALE).astype(BF16)
                qtt = qt.astype(F32).T.astype(BF16)
                dost = dos_pair.T.astype(BF16)
                for j in range(3):
                    slot = (t + 1 + j) % 3
                    js = slice(j * QB, (j + 1) * QB)
                    for e in range(2):
                        hr = slice(e * HEAD_DIM, (e + 1) * HEAD_DIM)
                        acc_sc[slot, hp, hr, :] += _dot(qtt[hr], dsbs[e][:, js])
                        acc_sc[slot, n_pair + hp, hr, :] += _dot(dost[hr], pbs[e][:, js])

        done = (t + 1) % 3

        @pl.when(t >= 2)
        def _():
            for i in range(2 * n_pair):
                dkv_ref[:, i * pair_w:(i + 1) * pair_w] = acc_sc[done, i].T.astype(BF16)

        acc_sc[done] = jnp.zeros((2 * n_pair, pair_w, QB), F32)

        @pl.when(t == nb + 1)
        def _():
            row = lax.broadcasted_iota(jnp.int32, (8, ROLL_W), 0)
            pad = jnp.zeros((8, ROLL_W - KB), F32)
            for h in range(ATT_HEADS):
                acc8 = jnp.concatenate([dt_sc[h, 0:8, :], pad], axis=1)
                for a1 in range(1, QB // 8):
                    blk = jnp.concatenate([dt_sc[h, 8 * a1:8 * a1 + 8, :], pad], axis=1)
                    acc8 = acc8 + pltpu.roll(blk, ROLL_W - 8 * a1, 1)
                for k in range(3):
                    acc8 = jnp.where(((row >> k) & 1) == 1, pltpu.roll(acc8, ROLL_W - (1 << k), 1), acc8)
                dfrow_ref[h:h + 1, :] = jnp.sum(acc8, axis=0, keepdims=True)

    clamp = lambda t: jnp.minimum(t, nb - 1)
    qrows = pl.BlockSpec((QB, D_ATT), lambda t: (clamp(t), 0))
    return _call(
        body, "b_attn", (nb + 2,),
        _att_in_specs(clamp) + [qrows, qrows, _full((ATT_HEADS, ROLL_W))],
        [qrows, pl.BlockSpec((QB, 2 * D_ATT), lambda t: (jnp.maximum(t - 2, 0), 0)),
         _full((ATT_HEADS, ROLL_W))],
        [_sds((s_len, D_ATT), BF16), _sds((s_len, 2 * D_ATT), BF16), _sds((ATT_HEADS, ROLL_W), F32)],
        [pltpu.VMEM((N_BIAS, ATT_HEADS, QB, KB), F32), pltpu.VMEM((ATT_HEADS, QB, KB), F32),
         pltpu.VMEM((3, 2 * n_pair, pair_w, QB), F32)],
        (*([qkv_pad] * 7), att, datt, frow), "arbitrary", comm)


def _b_win(dq, dkv, dxg, h, ts):
    s_len = h.shape[0]
    steps = s_len // ts

    def body(dq_ref, dkv_ref, dxg_ref, h_ref, dw_hbm, dwb_hbm, acc, accb, sems):
        @pl.when(pl.program_id(0) == 0)
        def _():
            acc[...] = jnp.zeros_like(acc)

        @pl.when(pl.program_id(0) < steps - 1)
        def _():
            dproj = jnp.concatenate([dq_ref[...], dkv_ref[...], dxg_ref[...]], axis=1)
            acc[...] += _dot_tn(h_ref[...], dproj)

        @pl.when(pl.program_id(0) == steps - 1)
        def _():
            dproj = jnp.concatenate([dq_ref[...], dkv_ref[...], dxg_ref[...]], axis=1)
            copies = []
            for s in range(N_SHARD):
                cols = slice(s * IN_SH, (s + 1) * IN_SH)
                total = acc[:, cols] + _dot_tn(h_ref[...], dproj[:, cols])
                acc[:, cols] = total
                accb[:, cols] = total.astype(BF16)
                copies += _start_copies([(acc.at[:, cols], dw_hbm.at[s]), (accb.at[:, cols], dwb_hbm.at[s])],
                                        sems, 2 * s)
            for cp in copies:
                cp.wait()

    shape = (N_SHARD, 1024, IN_SH)
    return pl.pallas_call(
        body, name="b_win", grid=(steps,),
        in_specs=[_rows(ts, 512), _rows(ts, 1024), _rows(ts, 1024), _rows(ts, 1024)],
        out_specs=[_any()] * 2, out_shape=[_sds(shape, F32), _sds(shape, BF16)],
        scratch_shapes=[pltpu.VMEM((1024, D_IN), F32), pltpu.VMEM((1024, D_IN), BF16),
                        pltpu.SemaphoreType.DMA((2 * N_SHARD,))],
        compiler_params=_cp("arbitrary"))(dq, dkv, dxg, h)


RING = 3


def _ring_tiles(hbm_refs, bufs, sems, tm, steps):
    i = pl.program_id(0)

    def copies(step):
        slot = step % RING
        return [pltpu.make_async_copy(h.at[pl.ds(step * tm, tm), :], b.at[slot], sems.at[k, slot])
                for k, (h, b) in enumerate(zip(hbm_refs, bufs))]

    @pl.when(i == 0)
    def _():
        for s in range(min(RING - 1, steps)):
            for cp in copies(s):
                cp.start()

    @pl.when(i + RING - 1 < steps)
    def _():
        for cp in copies(i + RING - 1):
            cp.start()

    for cp in copies(i):
        cp.wait()
    return [b.at[i % RING] for b in bufs]


def _b_inproj(dq, dkv, dxg, x, dx1, g_mix, w_in_g, tm, comm=None):
    s_len = x.shape[0]
    steps = s_len // tm

    def body(dq_ref, dkv_ref, dxg_ref, x_hbm, dx1_hbm, g_ref, w_hbm, gx_ref, dgm_ref, w_ref, w_sems,
             xbuf, dx1buf, ring_sems):
        _load_w_in_once(w_hbm, w_ref, w_sems)

        @pl.when(pl.program_id(0) == 0)
        def _():
            dgm_ref[...] = jnp.zeros_like(dgm_ref)

        x_ref, dx1_ref = _ring_tiles([x_hbm, dx1_hbm], [xbuf, dx1buf], ring_sems, tm, steps)
        dproj = jnp.concatenate([dq_ref[...], dkv_ref[...], dxg_ref[...]], axis=1)
        dh = _dot_nt(dproj, w_ref[...])
        dx, dgm = _rms_bwd(dh, x_ref[...], g_ref[...])
        gx_ref[...] = dx1_ref[...] + dx
        dgm_ref[...] += dgm

    return _call(
        body, "b_inproj", (steps,),
        [_rows(tm, 512), _rows(tm, 1024), _rows(tm, 1024), _any(), _any(), _full((1, 1024)), _any()],
        [_rows(tm, 1024), _full((1, 1024))],
        [_sds((s_len, 1024), F32), _sds((1, 1024), F32)],
        [pltpu.VMEM((1024, D_IN), BF16), pltpu.SemaphoreType.DMA((N_SHARD,)),
         pltpu.VMEM((RING, tm, 1024), F32), pltpu.VMEM((RING, tm, 1024), F32), pltpu.SemaphoreType.DMA((2, RING))],
        (dq, dkv, dxg, x, dx1, g_mix, w_in_g), "arbitrary", comm)


MXU_DIM_V7X = 256
FLUSH_GROUPS = 4


def _mm_tn(xa, ya, name, ts):
    s_len, k = xa.shape
    n = ya.shape[1]

    steps = s_len // ts
    tiles = k // MXU_DIM_V7X
    edges = [MXU_DIM_V7X * ((tiles * g) // FLUSH_GROUPS) for g in range(FLUSH_GROUPS + 1)]

    def body(x_ref, y_ref, o_hbm, ob_hbm, acc, accb, sems):
        @pl.when(pl.program_id(0) == 0)
        def _():
            if steps > 1:
                acc[...] = _dot_tn(x_ref[...].astype(BF16), y_ref[...].astype(BF16))
            else:
                acc[...] = jnp.zeros_like(acc)

        @pl.when((pl.program_id(0) > 0) & (pl.program_id(0) < steps - 1))
        def _():
            acc[...] += _dot_tn(x_ref[...].astype(BF16), y_ref[...].astype(BF16))

        @pl.when(pl.program_id(0) == steps - 1)
        def _():
            yb = y_ref[...].astype(BF16)
            copies = []
            for g in range(FLUSH_GROUPS):
                rows = slice(edges[g], edges[g + 1])
                total = acc[rows, :] + _dot_tn(x_ref[:, rows].astype(BF16), yb)
                acc[rows, :] = total
                accb[rows, :] = total.astype(BF16)
                copies += _start_copies([(acc.at[rows, :], o_hbm.at[rows, :]), (accb.at[rows, :], ob_hbm.at[rows, :])],
                                        sems, 2 * g)
            for cp in copies:
                cp.wait()

    return pl.pallas_call(
        body, name=name, grid=(steps,), in_specs=[_rows(ts, k), _rows(ts, n)],
        out_specs=[_any()] * 2, out_shape=[_sds((k, n), F32), _sds((k, n), BF16)],
        scratch_shapes=[pltpu.VMEM((k, n), F32), pltpu.VMEM((k, n), BF16),
                        pltpu.SemaphoreType.DMA((2 * FLUSH_GROUPS,))],
        compiler_params=_cp("arbitrary"))(xa, ya)


PAD_KEYS = LEFT_CHUNKS * CHUNK
F_HI = PAD_KEYS - MAX_REL + 1
F_LO = PAD_KEYS + MAX_REL


def _frow_from_rel_bias(rb):
    last = rb[:, 2 * MAX_REL:2 * MAX_REL + 1]
    hi = jnp.broadcast_to(last, (ATT_HEADS, F_HI))
    mid = rb[:, 1:2 * MAX_REL][:, ::-1]
    lo = jnp.broadcast_to(rb[:, 0:1], (ATT_HEADS, KB - F_LO))
    wrap = jnp.broadcast_to(last, (ATT_HEADS, ROLL_W - KB))
    return jnp.concatenate([hi, mid, lo, wrap], axis=1)


def _rel_bias_grad_from_dfrow(df):
    g_last = jnp.sum(df[:, 0:F_HI], axis=1, keepdims=True) + jnp.sum(df[:, KB:ROLL_W], axis=1, keepdims=True)
    mid = df[:, F_HI:F_LO][:, ::-1]
    g_first = jnp.sum(df[:, F_LO:KB], axis=1, keepdims=True)
    return jnp.concatenate([g_first, mid, g_last], axis=1)


def _block_diag(w):
    eye = jnp.eye(8, dtype=w.dtype)
    return (w[:, :, None, :] * eye[:, None, :, None]).reshape(D_LRU, D_LRU)


MID = ['w_out', 'wq_c', 'wk_c', 'wv_c', 'wo_c']
TRANSPOSED = ['w_gate', 'w_up']
AG_IN_INPROJ = ['w_out', 'wq_c', 'wk_c']
AG_IN_ATTN = ['wv_c', 'wo_c', 'w_gate']
AG_IN_LRU = ['w_up']
AG_IN_MID = ['w_down']
RS_IN_MID = ['w_gate', 'w_up']
RS_IN_LRU = ['w_down']
RS_IN_ATTN = MID


def _local_step(x, mem, tgt, p, gw, shards=None, chip=None):
    s_len = x.shape[0]
    tm = min(256, s_len)
    tmb = min(512, s_len)
    tl = min(512, s_len)
    frow = _frow_from_rel_bias(p['rel_bias'])
    wrg = _block_diag(p['w_rg']).astype(BF16)
    wig = _block_diag(p['w_ig']).astype(BF16)
    gw = dict(gw)

    big, bigb, recv, part, sib = {}, {}, {}, {}, {}

    def ag(names):
        return [] if shards is None else [("ag", [shards[n] for n in names])]

    def rs(names):
        return [] if shards is None else [("rs", [bigb[n] for n in names])]

    def swap(names):
        return [] if shards is None else [("swap", [part[n] for n in names])]

    def reduce_own(names):
        if shards is not None:
            sums = _sum_parts([big[n] for n in names], [recv[n] for n in names], chip, "sum_" + names[0])
            part.update(zip(names, sums))

    h, qkv_pad, xg, *got = _f_inproj(x, p['g_mix'], gw['w_in'], tmb, ag(AG_IN_INPROJ))
    gw.update(zip(AG_IN_INPROJ, got))
    att, *got = _f_attn(qkv_pad, frow, ag(AG_IN_ATTN))
    gw.update(zip(AG_IN_ATTN, got))
    rec, u, hs, *got = _f_lru(xg, p['conv_w'], p['conv_b'], wrg, p['b_rg'], wig, p['b_ig'], p['lru_L'], tl,
                              ag(AG_IN_LRU))
    gw.update(zip(AG_IN_LRU, got))
    w_out = gw['w_out'].reshape(1024, 1024)
    wq = gw['wq_c'].reshape(1024, 1024)
    wk = gw['wk_c'].reshape(1024, 1024)
    wv = gw['wv_c'].reshape(1024, 1024)
    wo = gw['wo_c'].reshape(1024, 1024)
    mn, kx, vx = _f_mem(mem, p['g_mem'], wk, wv)
    mg, x1, hc, qx, ox, x2, *got = _f_mid(x, att, rec, p['g_out_attn'], p['g_out_lru'], w_out, p['g_cross'],
                                          wq, kx, vx, wo, tmb, ag(AG_IN_MID))
    gw.update(zip(AG_IN_MID, got))
    ffn_w = [gw[n].reshape(D_FF, 1024) for n in ('w_gate', 'w_up', 'w_down')]
    hf, gact, uact, aact, dx3, loss, dg_final = _f_ffn(x2, tgt, p['g_ffn'], p['g_final'], *ffn_w, tmb)

    ts = min(1024, s_len)
    dgact, duact, dx2, dg_ffn = _b_ffn(dx3, x2, gact, uact, p['g_ffn'], *ffn_w, tm)
    big['w_gate'], bigb['w_gate'] = _mm_tn(dgact, hf, "dw_gate", ts)
    big['w_up'], bigb['w_up'] = _mm_tn(duact, hf, "dw_up", ts)
    big['w_down'], bigb['w_down'] = _mm_tn(aact, dx3, "dw_down", ts)
    for n in ('w_gate', 'w_up', 'w_down'):
        big[n] = big[n].reshape(N_SHARD, FF_SH, 1024)
        bigb[n] = bigb[n].reshape(N_SHARD, FF_SH, 1024)

    dqx, dx1, datt, drec, dkx, dvx, dg_cross, dg_oa, dg_ol, *got = _b_mid(
        dx2, qx, x1, att, rec, kx, vx, wo, wq, w_out, p['g_cross'], p['g_out_attn'], p['g_out_lru'], tmb,
        rs(RS_IN_MID))
    recv.update(zip(RS_IN_MID, got))
    reduce_own(RS_IN_MID)
    dwk, dwv, dg_mem, dwkb, dwvb = _b_mem(dkx, dvx, mem, mn, p['g_mem'], wk, wv)
    big['wk_c'], bigb['wk_c'] = dwk, dwkb
    big['wv_c'], bigb['wv_c'] = dwv, dwvb
    big['w_out'], bigb['w_out'] = _mm_tn(mg, dx1, "dw_out", ts)
    big['wq_c'], bigb['wq_c'] = _mm_tn(hc, dqx, "dw_q", ts)
    big['wo_c'], bigb['wo_c'] = _mm_tn(ox, dx2, "dw_o", ts)
    for n in MID:
        big[n] = big[n].reshape(N_SHARD, 256, 1024)
        bigb[n] = bigb[n].reshape(N_SHARD, 256, 1024)

    dxg, dwrg, dwig, dbrg, dbig, dlam, dcw, dcb, *got = _b_lru(
        drec, hs, u, xg, p['conv_w'], wrg, p['b_rg'], wig, p['b_ig'], p['lru_L'], tl,
        rs(RS_IN_LRU) + swap(RS_IN_MID))
    recv.update(zip(RS_IN_LRU, got))
    sib.update(zip(RS_IN_MID, got[len(RS_IN_LRU):]))
    reduce_own(RS_IN_LRU)
    small = {
        'conv_w': dcw, 'conv_b': dcb, 'w_rg': dwrg, 'b_rg': dbrg, 'w_ig': dwig, 'b_ig': dbig, 'lru_L': dlam,
        'g_out_attn': dg_oa, 'g_out_lru': dg_ol, 'g_cross': dg_cross, 'g_mem': dg_mem, 'g_ffn': dg_ffn,
        'g_final': dg_final,
    }
    names = [n for n in SMALL if n in small]
    gather = [] if shards is None else [
        ("ag8", [_pack_small(names, [small[n] for n in names], loss, PACK_ROWS, "pack_small")])]
    dq, dkv, dfrow, *got = _b_attn(qkv_pad, att, datt, frow, rs(RS_IN_ATTN) + swap(RS_IN_LRU) + gather)
    recv.update(zip(RS_IN_ATTN, got))
    sib.update(zip(RS_IN_LRU, got[len(RS_IN_ATTN):]))
    packs = got[-1] if gather else None
    reduce_own(RS_IN_ATTN)
    small['rel_bias'] = _rel_bias_grad_from_dfrow(dfrow)
    big['w_in'], bigb['w_in'] = _b_win(dq, dkv, dxg, h, ts)
    if shards is None:
        grad_x, small['g_mix'] = _b_inproj(dq, dkv, dxg, x, dx1, p['g_mix'], gw['w_in'], tmb)
    else:
        nsw = len(RS_IN_ATTN)

        def copies(refs, send_sems, recv_sems):
            return _tail_copies(refs[0], refs[1], refs[2:2 + nsw], refs[2 + nsw:2 + 2 * nsw], send_sems, recv_sems)

        slots = bigb['w_in']
        bufs = ([slots, lax.empty((3,) + slots.shape[1:], slots.dtype)] + [part[n] for n in RS_IN_ATTN]
                + [lax.empty(part[n].shape, F32) for n in RS_IN_ATTN])
        sems, bufs, token = _split_start("tail_exchange_start", bufs, 3 + nsw, copies)
        grad_x, small['g_mix'] = _b_inproj(dq, dkv, dxg, x, dx1, p['g_mix'] + token[0, 0], gw['w_in'], tmb)
        bufs = _split_wait("tail_exchange_wait", sems, bufs, 3 + nsw, copies, small['g_mix'])
        recv['w_in'] = bufs[1]
        sib.update(zip(RS_IN_ATTN, bufs[2 + nsw:]))
    reduce_own(['w_in'])
    return loss, grad_x, small, big, part, sib, packs


CAST_STEPS = 4


def _cast_shards(ws, name, comm=None):
    def body(*refs):
        n = len(refs) // 2
        for src, dst in zip(refs[:n], refs[n:]):
            dst[...] = src[...].astype(BF16)

    specs = [_rows(w.shape[0] // CAST_STEPS, w.shape[1]) for w in ws]
    return _call(body, name, (CAST_STEPS,), specs, specs, [_sds(w.shape, BF16) for w in ws], [], tuple(ws),
                 "arbitrary", comm)


def _sum_parts(own4s, recv3s, chip, name):
    n = len(own4s)
    _, r, c = own4s[0].shape
    steps = _ew_steps(r, n * c * (4 + 3 * 2 + 4))
    tr = r // steps

    def body(chip_ref, *refs):
        for own_ref, rc_ref, o_ref in zip(refs[:n], refs[n:2 * n], refs[2 * n:]):
            o_ref[...] = ((own_ref[0] + rc_ref[0].astype(F32)) + rc_ref[1].astype(F32)) + rc_ref[2].astype(F32)

    grid_spec = pltpu.PrefetchScalarGridSpec(
        num_scalar_prefetch=1, grid=(steps,),
        in_specs=[pl.BlockSpec((1, tr, c), lambda i, ch: (ch[0], i, 0))] * n
                 + [pl.BlockSpec((3, tr, c), lambda i, ch: (0, i, 0))] * n,
        out_specs=[pl.BlockSpec((tr, c), lambda i, ch: (i, 0))] * n)
    return pl.pallas_call(body, name=name, grid_spec=grid_spec, out_shape=[_sds((r, c), F32)] * n,
                          compiler_params=_cp("parallel"))(chip, *own4s, *recv3s)


def _adamw_math(w, g, m, v):
    m = ADAM_B1 * m + (1.0 - ADAM_B1) * g
    v = ADAM_B2 * v + (1.0 - ADAM_B2) * (g * g)
    m_hat = m / (1.0 - ADAM_B1 ** ADAM_STEP)
    v_hat = v / (1.0 - ADAM_B2 ** ADAM_STEP)
    delta = -ADAM_LR * (m_hat / (jnp.sqrt(v_hat) + ADAM_EPS) + ADAM_WD * w)
    return delta, m, v


def _final_adamw(pas, pbs, ws, ms, vs, name, after=None):
    n = len(ws)
    r, c = ws[0].shape
    steps = _ew_steps(r, n * c * 9 * 4)
    tr = r // steps

    def body(*refs):
        ins, outs = refs[:5 * n], refs[len(refs) - 4 * n:]
        for k in range(n):
            pa_ref, pb_ref, w_ref, m_ref, v_ref = (ins[j * n + k] for j in range(5))
            g = pa_ref[...] + pb_ref[...]
            outs[4 * k][...] = g
            outs[4 * k + 1][...], outs[4 * k + 2][...], outs[4 * k + 3][...] = _adamw_math(
                w_ref[...], g, m_ref[...], v_ref[...])

    order = [] if after is None else [after]
    res = pl.pallas_call(
        body, name=name, grid=(steps,), in_specs=[_rows(tr, c)] * (5 * n) + [_full(t.shape) for t in order],
        out_specs=[_rows(tr, c)] * (4 * n), out_shape=[_sds((r, c), F32)] * (4 * n),
        compiler_params=_cp("parallel"))(*pas, *pbs, *ws, *ms, *vs, *order)
    return [res[4 * k:4 * k + 4] for k in range(n)]


def _pack_put(ref, name, val_ref):
    r = _pack_rows()[name]
    shape = val_ref.shape
    if len(shape) == 3:
        for b in range(shape[0]):
            ref[r:r + shape[1], b * shape[2]:(b + 1) * shape[2]] = val_ref[b]
    elif shape[1] == 2 * PACK_W:
        ref[r:r + 1, :] = val_ref[:, 0:PACK_W]
        ref[r + 1:r + 2, :] = val_ref[:, PACK_W:2 * PACK_W]
    else:
        ref[r:r + shape[0], 0:shape[1]] = val_ref[...]


def _pack_get(ref, name, shape):
    r = _pack_rows()[name]
    if len(shape) == 3:
        return jnp.stack([ref[r:r + shape[1], b * shape[2]:(b + 1) * shape[2]] for b in range(shape[0])])
    if shape[1] == 2 * PACK_W:
        return jnp.concatenate([ref[r:r + 1, :], ref[r + 1:r + 2, :]], axis=1)
    return ref[r:r + shape[0], 0:shape[1]]


def _pack_small(names, g, loss, rows, name):
    n = len(g)
    extra = [] if loss is None else [loss]

    def body(*refs):
        pack = refs[-1]
        pack[...] = jnp.zeros_like(pack)
        for a, nm in enumerate(names):
            _pack_put(pack, nm, refs[a])
        if extra:
            _pack_put(pack, 'loss', refs[n])

    return pl.pallas_call(body, name=name, out_shape=_sds((rows, PACK_W), F32), compiler_params=_cp())(*g, *extra)


def _all_peers():
    x, y, c = _mesh_pos()
    peers = []
    for k in range(1, 8):
        px = 1 - x if k & 4 else x
        py = 1 - y if k & 2 else y
        pc = 1 - c if k & 1 else c
        peers.append(((px, py, pc), 4 * px + 2 * py + pc))
    return peers, 4 * x + 2 * y + c


def _ag8_copies(ins, outs, sems):
    send_sems, recv_sems, loc_sems = sems
    n = len(ins)
    peers, me = _all_peers()

    def remote(k, j, slot):
        return pltpu.make_async_remote_copy(
            src_ref=ins[k], dst_ref=outs[k].at[slot], send_sem=send_sems.at[k, j], recv_sem=recv_sems.at[k, j],
            device_id=peers[j][0], device_id_type=MESH_ID)

    def local(k):
        return pltpu.make_async_copy(ins[k], outs[k].at[me], loc_sems.at[k])

    def start():
        for k in range(n):
            local(k).start()
            for j in range(7):
                remote(k, j, me).start()

    def wait():
        for k in range(n):
            for j in range(7):
                remote(k, j, peers[j][1]).wait_recv()
        for k in range(n):
            for j in range(7):
                remote(k, j, me).wait_send()
            local(k).wait()

    return start, _no_forward, wait


def _adamw_small(packs, late_own, late_packs, g_shapes, loss_shape, w, m, v):
    n = len(w)

    def body(*refs):
        packs_ref, own_ref, late_ref = refs[0], refs[1], refs[2]
        w_refs, m_refs, v_refs = (refs[3 + i * n:3 + (i + 1) * n] for i in range(3))
        o0 = 3 * n + 3
        go, do, mo, vo = (refs[o0 + i * n:o0 + (i + 1) * n] for i in range(4))
        loss_out, tot_ref = refs[o0 + 4 * n], refs[o0 + 4 * n + 1]
        x, y, c = _mesh_pos()
        me = 4 * x + 2 * y + c
        tot = packs_ref[0]
        late = jnp.where(me == 0, own_ref[...], late_ref[0])
        for d in range(1, 8):
            tot = tot + packs_ref[d]
            late = late + jnp.where(me == d, own_ref[...], late_ref[d])
        tot_ref[...] = tot
        tot_ref[0:LATE_ROWS, :] += late
        loss_out[...] = _pack_get(tot_ref, 'loss', loss_shape)
        for a, name in enumerate(SMALL):
            if name == 'conv_w':
                r = _pack_rows()[name]
                ga = tot_ref[r:r + g_shapes[a][0], pl.ds(pl.multiple_of((2 * x + y) * 128, 128), 128)]
            else:
                ga = _pack_get(tot_ref, name, g_shapes[a])
            go[a][...] = ga
            do[a][...], mo[a][...], vo[a][...] = _adamw_math(w_refs[a][...], ga, m_refs[a][...], v_refs[a][...])

    out_shape = [_sds(a.shape, F32) for a in w] * 4 + [_sds(loss_shape, F32)]
    return pl.pallas_call(body, name="adamw_small", out_shape=out_shape,
                          scratch_shapes=[pltpu.VMEM((PACK_ROWS, PACK_W), F32)],
                          compiler_params=_cp())(packs, late_own, late_packs, *w, *m, *v)


PACK_W = 512
PACK_ROWS = 160
LATE = ['g_mix', 'rel_bias']
LATE_ROWS = 32


def _pack_rows():
    rows, r = {}, 0
    for name in ['g_mix', 'g_cross', 'g_mem', 'g_ffn', 'g_final']:
        rows[name] = r
        r += 2
    for name in ['conv_b', 'b_rg', 'b_ig', 'lru_L', 'g_out_attn', 'g_out_lru']:
        rows[name] = r
        r += 1
    rows['conv_w'] = r
    rows['loss'] = r + 4
    rows['rel_bias'] = 24
    rows['w_rg'] = 32
    rows['w_ig'] = 32 + LRU_BLOCK
    assert r + 5 <= 24 and rows['w_ig'] + LRU_BLOCK == PACK_ROWS
    assert rows['g_mix'] + 2 <= LATE_ROWS and rows['rel_bias'] + 8 <= LATE_ROWS
    return rows


INPUT_NAMES = (['x', 'mem'] + WEIGHTS + ['loss_target'] + ['m_' + n for n in WEIGHTS] + ['v_' + n for n in WEIGHTS])


def kernel(x, mem, g_mix, w_in, rel_bias, conv_w, conv_b, w_rg, b_rg, w_ig, b_ig, lru_L, g_out_attn, g_out_lru, w_out, g_cross, g_mem, wq_c, wk_c, wv_c, wo_c, g_ffn, w_gate, w_up, w_down, g_final, loss_target, m_g_mix, m_w_in, m_rel_bias, m_conv_w, m_conv_b, m_w_rg, m_b_rg, m_w_ig, m_b_ig, m_lru_L, m_g_out_attn, m_g_out_lru, m_w_out, m_g_cross, m_g_mem, m_wq_c, m_wk_c, m_wv_c, m_wo_c, m_g_ffn, m_w_gate, m_w_up, m_w_down, m_g_final, v_g_mix, v_w_in, v_rel_bias, v_conv_w, v_conv_b, v_w_rg, v_b_rg, v_w_ig, v_b_ig, v_lru_L, v_g_out_attn, v_g_out_lru, v_w_out, v_g_cross, v_g_mem, v_wq_c, v_wk_c, v_wv_c, v_wo_c, v_g_ffn, v_w_gate, v_w_up, v_w_down, v_g_final):
    a = dict(zip(INPUT_NAMES, (x, mem, g_mix, w_in, rel_bias, conv_w, conv_b, w_rg, b_rg, w_ig, b_ig, lru_L, g_out_attn, g_out_lru, w_out, g_cross, g_mem, wq_c, wk_c, wv_c, wo_c, g_ffn, w_gate, w_up, w_down, g_final, loss_target, m_g_mix, m_w_in, m_rel_bias, m_conv_w, m_conv_b, m_w_rg, m_b_rg, m_w_ig, m_b_ig, m_lru_L, m_g_out_attn, m_g_out_lru, m_w_out, m_g_cross, m_g_mem, m_wq_c, m_wk_c, m_wv_c, m_wo_c, m_g_ffn, m_w_gate, m_w_up, m_w_down, m_g_final, v_g_mix, v_w_in, v_rel_bias, v_conv_w, v_conv_b, v_w_rg, v_b_rg, v_w_ig, v_b_ig, v_lru_L, v_g_out_attn, v_g_out_lru, v_w_out, v_g_cross, v_g_mem, v_wq_c, v_wk_c, v_wv_c, v_wo_c, v_g_ffn, v_w_gate, v_w_up, v_w_down, v_g_final)))
    chip = 2 * lax.axis_index("x") + lax.axis_index("y")

    def shard(name):
        arr = a[name][0]
        base = name[2:] if name[:2] in ('m_', 'v_') else name
        return jnp.swapaxes(arr, 0, 1) if base in TRANSPOSED else arr

    shards = {'w_in': _cast_shards([shard('w_in')], "cast_w_in")[0]}
    rest = [n for n in BIG if n != 'w_in']
    *cast, w_in_g, conv_w_g = _cast_shards([shard(n) for n in rest], "cast_rest",
                                           [("ag", [shards['w_in']]), ("agf", [a['conv_w'][0]])])
    shards.update(zip(rest, cast))
    conv_w_full = conv_w_g.transpose(1, 0, 2).reshape(4, D_LRU)

    p = {n: a[n] for n in SMALL}
    p['rel_bias'] = a['rel_bias'][0]
    p['w_rg'] = a['w_rg'][0]
    p['w_ig'] = a['w_ig'][0]
    p['conv_w'] = conv_w_full
    p['g_final'] = a['g_final'][None, :]
    chip_arr = jnp.reshape(chip, (1,)).astype(jnp.int32)
    loss_part, grad_x, small, _, part, sib, packs = _local_step(
        a['x'][0], a['mem'][0], a['loss_target'][0], p, {'w_in': w_in_g}, shards, chip_arr)

    def late_copies(refs, send_sems, recv_sems):
        return _late_copies(refs[0], refs[1], refs[2], refs[3], send_sems, recv_sems)

    late_pack = _pack_small(LATE, [small[n] for n in LATE], None, LATE_ROWS, "pack_late")
    bufs = [part['w_in'], lax.empty(part['w_in'].shape, F32), late_pack, jnp.zeros((8, LATE_ROWS, PACK_W), F32)]
    sems, bufs, token = _split_start("late_exchange_start", bufs, 8, late_copies)
    out = {}

    def adamw(group, after=None):
        results = _final_adamw([part[n] for n in group], [sib[n] for n in group], [shard(n) for n in group],
                               [shard('m_' + n) for n in group], [shard('v_' + n) for n in group],
                               "adamw_" + group[0], after)
        for n, res in zip(group, results):
            out[n] = [jnp.swapaxes(r, 0, 1) for r in res] if n in TRANSPOSED else res
        return results[-1][0]

    adamw(MID, token)
    done = adamw(['w_gate', 'w_up', 'w_down'], token)
    _, sib['w_in'], late_pack, late_packs = _split_wait("late_exchange_wait", sems, bufs, 8, late_copies, done)
    adamw(['w_in'])

    def natural(arr):
        return arr[0] if arr.ndim >= 3 else (arr[None, :] if arr.ndim == 1 else arr)

    small_out = _adamw_small(packs, late_pack, late_packs, [small[n].shape for n in SMALL],
                             loss_part.shape, *[[natural(a[pre + n]) for n in SMALL] for pre in ('', 'm_', 'v_')])
    ns = len(SMALL)
    loss = small_out[4 * ns][0, 0]

    def leaf(i, n):
        if n in BIG:
            return out[n][i][None]
        return small_out[i * ns + SMALL.index(n)].reshape(a[n].shape)

    return (loss, grad_x[None], *[leaf(i, n) for i in range(4) for n in WEIGHTS])
```
